```python
import jax, jax.numpy as jnp
from jax import lax
import numpy as np

D_MODEL = 1024
BATCH = 8
SEQ = 4096
DEPTH = 4

HEAD_DIM = 64
ATTN_WIDTH = D_MODEL // 2
CONV_WIDTH = D_MODEL - ATTN_WIDTH
N_ATTN_HEADS = ATTN_WIDTH // HEAD_DIM
N_CONV_GROUPS = CONV_WIDTH // HEAD_DIM
IN_PROJ_WIDTH = 3 * ATTN_WIDTH + 3 * CONV_WIDTH
CONV_K = 3
ROPE_DIM = HEAD_DIM // 4
ROPE_THETA = 500000.0
DILATED_BRANCHES = ((128, 1), (512, 4), (2048, 16))
FFN_HIDDEN = ((8 * D_MODEL // 3 + 255) // 256) * 256
RMS_EPS = 1e-6
NEG_INF = -1e30

kernel_name = "hybrid_dilated_attn_shortconv_encoder"


def rms_norm(x, g):
    xf = x.astype(jnp.float32)
    y = xf * lax.rsqrt(jnp.mean(xf * xf, axis=-1, keepdims=True) + RMS_EPS)
    return (y * g.astype(jnp.float32)).astype(x.dtype)


def rotary_tables(positions):
    inv_freq = ROPE_THETA ** (-jnp.arange(0, ROPE_DIM, 2, dtype=jnp.float32) / ROPE_DIM)
    ang = positions.astype(jnp.float32)[..., None] * inv_freq
    return jnp.cos(ang)[:, None], jnp.sin(ang)[:, None]


def apply_partial_rotary(t, cos, sin):
    tf = t.astype(jnp.float32)
    half = ROPE_DIM // 2
    t1 = tf[..., :half]
    t2 = tf[..., half:ROPE_DIM]
    out = jnp.concatenate([t1 * cos - t2 * sin, t2 * cos + t1 * sin, tf[..., ROPE_DIM:]], axis=-1)
    return out.astype(t.dtype)


def banded_attention(q, k, v, half):
    L, dh = q.shape[-2], q.shape[-1]
    lead = q.shape[:-2]
    nb = -(-L // half)
    lp = nb * half
    pad_q = [(0, 0)] * len(lead) + [(0, lp - L), (0, 0)]
    pad_kv = [(0, 0)] * len(lead) + [(half, lp - L + half), (0, 0)]
    qb = jnp.pad(q, pad_q).reshape(*lead, nb, half, dh).astype(jnp.float32)

    def windows(t):
        tb = jnp.pad(t, pad_kv).reshape(*lead, nb + 2, half, dh).astype(jnp.float32)
        return jnp.concatenate([tb[..., :-2, :, :], tb[..., 1:-1, :, :], tb[..., 2:, :, :]], axis=-2)

    kw = windows(k)
    vw = windows(v)
    s = jnp.einsum('...nqd,...nkd->...nqk', qb, kw) * (dh ** -0.5)
    qi = jnp.arange(nb)[:, None] * half + jnp.arange(half)[None, :]
    ki = jnp.arange(nb)[:, None] * half + jnp.arange(3 * half)[None, :] - half
    valid = (jnp.abs(qi[:, :, None] - ki[:, None, :]) <= half) & ((ki >= 0) & (ki < L))[:, None, :]
    s = jnp.where(valid, s, NEG_INF)
    lse = jax.nn.logsumexp(s, axis=-1)
    p = jnp.exp(s - lse[..., None])
    o = jnp.einsum('...nqk,...nkd->...nqd', p, vw)
    o = o.reshape(*lead, lp, dh)[..., :L, :]
    lse = lse.reshape(*lead, lp)[..., :L]
    return o, lse


def dilated_mixture_attention(q, k, v):
    b, h, s, dh = q.shape
    outs, lses = [], []
    for window, dil in DILATED_BRANCHES:
        L = s // dil
        half = window // (2 * dil)

        def by_residue(t):
            return t.reshape(b, h, L, dil, dh).swapaxes(2, 3)

        o, lse = banded_attention(by_residue(q), by_residue(k), by_residue(v), half)
        outs.append(o.swapaxes(2, 3).reshape(b, h, s, dh))
        lses.append(lse.swapaxes(2, 3).reshape(b, h, s))
    w = jax.nn.softmax(jnp.stack(lses, axis=0), axis=0)
    o = jnp.sum(w[..., None] * jnp.stack(outs, axis=0), axis=0)
    return o.astype(q.dtype)


def short_conv(u, w):
    c = u.shape[-1]
    return lax.conv_general_dilated(
        u, w.reshape(CONV_K, 1, c).astype(u.dtype), window_strides=(1,),
        padding=[(CONV_K // 2, CONV_K // 2)], dimension_numbers=('NWC', 'WIO', 'NWC'),
        feature_group_count=c)


def _fwd_setup_inputs(seed: int = 0) -> dict:
    key = jax.random.key(seed)
    ks = jax.random.split(key, 16)
    f32 = jnp.float32

    def gain(k, shape):
        return 1.0 + 0.02 * jax.random.normal(k, shape, f32)

    x = jax.random.normal(ks[0], (BATCH, SEQ, D_MODEL), f32)
    offsets = jax.random.randint(ks[1], (BATCH, 1), 0, 1024, dtype=jnp.int32)
    positions = (jnp.arange(SEQ, dtype=jnp.int32)[None, :] + offsets).astype(jnp.int32)
    return {
        "x": x,
        "positions": positions,
        "pre_mix_norm": gain(ks[2], (DEPTH, D_MODEL)),
        "w_in": jax.random.normal(ks[3], (DEPTH, D_MODEL, IN_PROJ_WIDTH), f32) * D_MODEL ** -0.5,
        "conv_w": jax.random.normal(ks[4], (DEPTH, CONV_K, CONV_WIDTH), f32) * CONV_K ** -0.5,
        "attn_out_norm": gain(ks[5], (DEPTH, ATTN_WIDTH)),
        "conv_out_norm": gain(ks[6], (DEPTH, CONV_WIDTH)),
        "w_out": jax.random.normal(ks[7], (DEPTH, D_MODEL, D_MODEL), f32) * D_MODEL ** -0.5,
        "post_mix_norm": gain(ks[8], (DEPTH, D_MODEL)),
        "pre_ffn_norm": gain(ks[9], (DEPTH, D_MODEL)),
        "w_gate_up": jax.random.normal(ks[10], (DEPTH, D_MODEL, 2 * FFN_HIDDEN), f32) * D_MODEL ** -0.5,
        "w_down": jax.random.normal(ks[11], (DEPTH, FFN_HIDDEN, D_MODEL), f32) * FFN_HIDDEN ** -0.5,
        "post_ffn_norm": gain(ks[12], (DEPTH, D_MODEL)),
    }


def _fwd_reference(x, positions, pre_mix_norm, w_in, conv_w, attn_out_norm, conv_out_norm,
              w_out, post_mix_norm, pre_ffn_norm, w_gate_up, w_down, post_ffn_norm):
    b, s, _ = x.shape
    cos, sin = rotary_tables(positions)
    split_points = np.cumsum([ATTN_WIDTH] * 3 + [CONV_WIDTH] * 2).tolist()

    def heads(t):
        return t.reshape(b, s, N_ATTN_HEADS, HEAD_DIM).transpose(0, 2, 1, 3)

    for l in range(DEPTH):
        h = rms_norm(x, pre_mix_norm[l])
        proj = jnp.einsum('bsd,de->bse', h, w_in[l])
        q, k, v, conv_u, gate_b, gate_c = jnp.split(proj, split_points, axis=-1)
        q = apply_partial_rotary(heads(q), cos, sin)
        k = apply_partial_rotary(heads(k), cos, sin)
        attn = dilated_mixture_attention(q, k, heads(v))
        attn = attn.transpose(0, 2, 1, 3).reshape(b, s, ATTN_WIDTH)
        conv_y = gate_b * short_conv(gate_c * conv_u, conv_w[l])
        merged = jnp.concatenate([rms_norm(attn, attn_out_norm[l]),
                                  rms_norm(conv_y, conv_out_norm[l])], axis=-1)
        mix = jnp.einsum('bse,ed->bsd', merged, w_out[l])
        x = x + rms_norm(mix, post_mix_norm[l])
        h = rms_norm(x, pre_ffn_norm[l])
        g, u = jnp.split(jnp.einsum('bsd,df->bsf', h, w_gate_up[l]), 2, axis=-1)
        f = jnp.einsum('bsf,fd->bsd', jax.nn.silu(g) * u, w_down[l])
        x = x + rms_norm(f, post_ffn_norm[l])
    return x


import jax as _jax
import jax.numpy as _jnp

TWIN_FORMAT = 'train_step'
FWD_PARAMS = ['x', 'positions', 'pre_mix_norm', 'w_in', 'conv_w', 'attn_out_norm', 'conv_out_norm', 'w_out', 'post_mix_norm', 'pre_ffn_norm', 'w_gate_up', 'w_down', 'post_ffn_norm']
TWIN_WEIGHTS = ['pre_mix_norm', 'w_in', 'conv_w', 'attn_out_norm', 'conv_out_norm', 'w_out', 'post_mix_norm', 'pre_ffn_norm', 'w_gate_up', 'w_down', 'post_ffn_norm']
TWIN_DIFF_INPUT = 'x'
TWIN_INPUTS = ['x', 'positions', 'pre_mix_norm', 'w_in', 'conv_w', 'attn_out_norm', 'conv_out_norm', 'w_out', 'post_mix_norm', 'pre_ffn_norm', 'w_gate_up', 'w_down', 'post_ffn_norm', 'loss_target', 'm_pre_mix_norm', 'm_w_in', 'm_conv_w', 'm_attn_out_norm', 'm_conv_out_norm', 'm_w_out', 'm_post_mix_norm', 'm_pre_ffn_norm', 'm_w_gate_up', 'm_w_down', 'm_post_ffn_norm', 'v_pre_mix_norm', 'v_w_in', 'v_conv_w', 'v_attn_out_norm', 'v_conv_out_norm', 'v_w_out', 'v_post_mix_norm', 'v_pre_ffn_norm', 'v_w_gate_up', 'v_w_down', 'v_post_ffn_norm']
TWIN_OUTPUTS = ['loss', 'grad_x', 'grad_pre_mix_norm', 'grad_w_in', 'grad_conv_w', 'grad_attn_out_norm', 'grad_conv_out_norm', 'grad_w_out', 'grad_post_mix_norm', 'grad_pre_ffn_norm', 'grad_w_gate_up', 'grad_w_down', 'grad_post_ffn_norm', 'delta_pre_mix_norm', 'delta_w_in', 'delta_conv_w', 'delta_attn_out_norm', 'delta_conv_out_norm', 'delta_w_out', 'delta_post_mix_norm', 'delta_pre_ffn_norm', 'delta_w_gate_up', 'delta_w_down', 'delta_post_ffn_norm', 'new_m_pre_mix_norm', 'new_m_w_in', 'new_m_conv_w', 'new_m_attn_out_norm', 'new_m_conv_out_norm', 'new_m_w_out', 'new_m_post_mix_norm', 'new_m_pre_ffn_norm', 'new_m_w_gate_up', 'new_m_w_down', 'new_m_post_ffn_norm', 'new_v_pre_mix_norm', 'new_v_w_in', 'new_v_conv_w', 'new_v_attn_out_norm', 'new_v_conv_out_norm', 'new_v_w_out', 'new_v_post_mix_norm', 'new_v_pre_ffn_norm', 'new_v_w_gate_up', 'new_v_w_down', 'new_v_post_ffn_norm']
TWIN_LEAF_KINDS = {'loss': 'loss', 'grad_x': 'grad_x', 'grad_pre_mix_norm': 'grad_w', 'grad_w_in': 'grad_w', 'grad_conv_w': 'grad_w', 'grad_attn_out_norm': 'grad_w', 'grad_conv_out_norm': 'grad_w', 'grad_w_out': 'grad_w', 'grad_post_mix_norm': 'grad_w', 'grad_pre_ffn_norm': 'grad_w', 'grad_w_gate_up': 'grad_w', 'grad_w_down': 'grad_w', 'grad_post_ffn_norm': 'grad_w', 'delta_pre_mix_norm': 'delta_w', 'delta_w_in': 'delta_w', 'delta_conv_w': 'delta_w', 'delta_attn_out_norm': 'delta_w', 'delta_conv_out_norm': 'delta_w', 'delta_w_out': 'delta_w', 'delta_post_mix_norm': 'delta_w', 'delta_pre_ffn_norm': 'delta_w', 'delta_w_gate_up': 'delta_w', 'delta_w_down': 'delta_w', 'delta_post_ffn_norm': 'delta_w', 'new_m_pre_mix_norm': 'new_m', 'new_m_w_in': 'new_m', 'new_m_conv_w': 'new_m', 'new_m_attn_out_norm': 'new_m', 'new_m_conv_out_norm': 'new_m', 'new_m_w_out': 'new_m', 'new_m_post_mix_norm': 'new_m', 'new_m_pre_ffn_norm': 'new_m', 'new_m_w_gate_up': 'new_m', 'new_m_w_down': 'new_m', 'new_m_post_ffn_norm': 'new_m', 'new_v_pre_mix_norm': 'new_v', 'new_v_w_in': 'new_v', 'new_v_conv_w': 'new_v', 'new_v_attn_out_norm': 'new_v', 'new_v_conv_out_norm': 'new_v', 'new_v_w_out': 'new_v', 'new_v_post_mix_norm': 'new_v', 'new_v_pre_ffn_norm': 'new_v', 'new_v_w_gate_up': 'new_v', 'new_v_w_down': 'new_v', 'new_v_post_ffn_norm': 'new_v'}


def _forward(args):
    return _fwd_reference(*[args[k] for k in FWD_PARAMS])


def _output_shape():
    def fwd():
        inp = _fwd_setup_inputs(0)
        return _fwd_reference(*[inp[k] for k in FWD_PARAMS])
    out = _jax.eval_shape(fwd)
    return out.shape, out.dtype

N_MICROBATCH = 1
ADAM_LR = 0.001
ADAM_B1 = 0.9
ADAM_B2 = 0.999
ADAM_EPS = 1e-08
ADAM_WD = 0.01
ADAM_STEP = 10
PER_EXAMPLE_BATCH_AXIS = {'x': 0, 'positions': 0, 'loss_target': 0}
SHARED_INPUTS = []
_WEIGHT_DTYPES = {'pre_mix_norm': _jnp.float32, 'w_in': _jnp.float32, 'conv_w': _jnp.float32, 'attn_out_norm': _jnp.float32, 'conv_out_norm': _jnp.float32, 'w_out': _jnp.float32, 'post_mix_norm': _jnp.float32, 'pre_ffn_norm': _jnp.float32, 'w_gate_up': _jnp.float32, 'w_down': _jnp.float32, 'post_ffn_norm': _jnp.float32}
MOMENT_SCALE = {'pre_mix_norm': 5.657726e+00, 'w_in': 3.262094e+00, 'conv_w': 1.482445e+00, 'attn_out_norm': 7.787352e+00, 'conv_out_norm': 1.575025e+00, 'w_out': 5.483551e+00, 'post_mix_norm': 3.167957e+01, 'pre_ffn_norm': 2.515969e+00, 'w_gate_up': 1.003646e+00, 'w_down': 1.898885e+00, 'post_ffn_norm': 3.192492e+01}


def _to_microbatches(a, axis):
    t = _jnp.moveaxis(a, axis, 0)
    t = t.reshape((N_MICROBATCH, t.shape[0] // N_MICROBATCH) + t.shape[1:])
    return _jnp.moveaxis(t, 1, axis + 1)


def setup_inputs(seed: int = 0) -> dict:
    inp = _fwd_setup_inputs(seed)
    key = _jax.random.fold_in(_jax.random.key(seed), 7919)
    shape, _ = _output_shape()
    out = dict(inp)
    out["loss_target"] = _jax.random.normal(_jax.random.fold_in(key, 0), shape, _jnp.float32)
    for i, name in enumerate(TWIN_WEIGHTS):
        w = inp[name].astype(_jnp.float32)
        if MOMENT_SCALE is None:
            s = _jnp.sqrt(_jnp.mean(_jnp.square(w)) + 1e-30)
        else:
            s = MOMENT_SCALE[name]
        km, kv = _jax.random.split(_jax.random.fold_in(key, i + 1))
        out[name] = w
        out["m_" + name] = s * _jax.random.normal(km, w.shape, _jnp.float32)
        out["v_" + name] = (s * s) * _jax.random.uniform(kv, w.shape, _jnp.float32, 0.5, 1.5)
    if N_MICROBATCH > 1:
        for name, axis in PER_EXAMPLE_BATCH_AXIS.items():
            out[name] = _to_microbatches(out[name], axis)
    return {'x': out['x'], 'positions': out['positions'], 'pre_mix_norm': out['pre_mix_norm'], 'w_in': out['w_in'], 'conv_w': out['conv_w'], 'attn_out_norm': out['attn_out_norm'], 'conv_out_norm': out['conv_out_norm'], 'w_out': out['w_out'], 'post_mix_norm': out['post_mix_norm'], 'pre_ffn_norm': out['pre_ffn_norm'], 'w_gate_up': out['w_gate_up'], 'w_down': out['w_down'], 'post_ffn_norm': out['post_ffn_norm'], 'loss_target': out['loss_target'], 'm_pre_mix_norm': out['m_pre_mix_norm'], 'm_w_in': out['m_w_in'], 'm_conv_w': out['m_conv_w'], 'm_attn_out_norm': out['m_attn_out_norm'], 'm_conv_out_norm': out['m_conv_out_norm'], 'm_w_out': out['m_w_out'], 'm_post_mix_norm': out['m_post_mix_norm'], 'm_pre_ffn_norm': out['m_pre_ffn_norm'], 'm_w_gate_up': out['m_w_gate_up'], 'm_w_down': out['m_w_down'], 'm_post_ffn_norm': out['m_post_ffn_norm'], 'v_pre_mix_norm': out['v_pre_mix_norm'], 'v_w_in': out['v_w_in'], 'v_conv_w': out['v_conv_w'], 'v_attn_out_norm': out['v_attn_out_norm'], 'v_conv_out_norm': out['v_conv_out_norm'], 'v_w_out': out['v_w_out'], 'v_post_mix_norm': out['v_post_mix_norm'], 'v_pre_ffn_norm': out['v_pre_ffn_norm'], 'v_w_gate_up': out['v_w_gate_up'], 'v_w_down': out['v_w_down'], 'v_post_ffn_norm': out['v_post_ffn_norm']}


def _loss(weights, diff, rest, loss_target):
    with _jax.named_scope("forward"):
        args = {**rest, TWIN_DIFF_INPUT: diff, **{k: w.astype(_WEIGHT_DTYPES[k]) for k, w in weights.items()}}
        y = _forward(args)
    with _jax.named_scope("loss_head"):
        err = _jnp.square(y.astype(_jnp.float32) - loss_target)
        return 0.5 * _jnp.sum(_jnp.mean(err, axis=-1)) if err.ndim else 0.5 * err


def _adamw(w, g, m, v):
    m = ADAM_B1 * m + (1.0 - ADAM_B1) * g
    v = ADAM_B2 * v + (1.0 - ADAM_B2) * _jnp.square(g)
    m_hat = m / (1.0 - ADAM_B1 ** ADAM_STEP)
    v_hat = v / (1.0 - ADAM_B2 ** ADAM_STEP)
    delta = -ADAM_LR * (m_hat / (_jnp.sqrt(v_hat) + ADAM_EPS) + ADAM_WD * w)
    return delta, m, v


def reference(x, positions, pre_mix_norm, w_in, conv_w, attn_out_norm, conv_out_norm, w_out, post_mix_norm, pre_ffn_norm, w_gate_up, w_down, post_ffn_norm, loss_target, m_pre_mix_norm, m_w_in, m_conv_w, m_attn_out_norm, m_conv_out_norm, m_w_out, m_post_mix_norm, m_pre_ffn_norm, m_w_gate_up, m_w_down, m_post_ffn_norm, v_pre_mix_norm, v_w_in, v_conv_w, v_attn_out_norm, v_conv_out_norm, v_w_out, v_post_mix_norm, v_pre_ffn_norm, v_w_gate_up, v_w_down, v_post_ffn_norm):
    given = dict(x=x, positions=positions, pre_mix_norm=pre_mix_norm, w_in=w_in, conv_w=conv_w, attn_out_norm=attn_out_norm, conv_out_norm=conv_out_norm, w_out=w_out, post_mix_norm=post_mix_norm, pre_ffn_norm=pre_ffn_norm, w_gate_up=w_gate_up, w_down=w_down, post_ffn_norm=post_ffn_norm, loss_target=loss_target, m_pre_mix_norm=m_pre_mix_norm, m_w_in=m_w_in, m_conv_w=m_conv_w, m_attn_out_norm=m_attn_out_norm, m_conv_out_norm=m_conv_out_norm, m_w_out=m_w_out, m_post_mix_norm=m_post_mix_norm, m_pre_ffn_norm=m_pre_ffn_norm, m_w_gate_up=m_w_gate_up, m_w_down=m_w_down, m_post_ffn_norm=m_post_ffn_norm, v_pre_mix_norm=v_pre_mix_norm, v_w_in=v_w_in, v_conv_w=v_conv_w, v_attn_out_norm=v_attn_out_norm, v_conv_out_norm=v_conv_out_norm, v_w_out=v_w_out, v_post_mix_norm=v_post_mix_norm, v_pre_ffn_norm=v_pre_ffn_norm, v_w_gate_up=v_w_gate_up, v_w_down=v_w_down, v_post_ffn_norm=v_post_ffn_norm)
    weights = {n: given[n] for n in TWIN_WEIGHTS}
    shared = {n: given[n] for n in SHARED_INPUTS}
    per_example = {n: given[n] for n in ['x', 'positions']}
    grad_fn = _jax.value_and_grad(_loss, argnums=(0, 1))

    def one_microbatch(ex, loss_target):
        ex = dict(ex)
        diff = ex.pop(TWIN_DIFF_INPUT)
        return grad_fn(weights, diff, {**shared, **ex}, loss_target)

    if N_MICROBATCH == 1:
        loss, (grad_w, grad_x) = one_microbatch(per_example, given["loss_target"])
    else:
        def body(carry, xs):
            loss_sum, grad_sum = carry
            l_k, (gw_k, gx_k) = one_microbatch(xs[0], xs[1])
            with _jax.named_scope("update"):
                return (loss_sum + l_k, _jax.tree.map(_jnp.add, grad_sum, gw_k)), gx_k

        init = (_jnp.zeros((), _jnp.float32), _jax.tree.map(_jnp.zeros_like, weights))
        (loss, grad_w), grad_x = _jax.lax.scan(body, init, (per_example, given["loss_target"]))
    with _jax.named_scope("update"):
        delta_w, new_m, new_v = {}, {}, {}
        for n in TWIN_WEIGHTS:
            delta_w[n], new_m[n], new_v[n] = _adamw(weights[n], grad_w[n], given["m_" + n], given["v_" + n])
    return (loss, grad_x, *[grad_w[n] for n in TWIN_WEIGHTS], *[delta_w[n] for n in TWIN_WEIGHTS],
            *[new_m[n] for n in TWIN_WEIGHTS], *[new_v[n] for n in TWIN_WEIGHTS])
```

```python
import functools

import jax
import jax.numpy as jnp
from jax import lax
from jax.experimental import pallas as pl
from jax.experimental.pallas import tpu as pltpu

F32 = jnp.float32
BF16 = jnp.bfloat16
MESH = pl.DeviceIdType.MESH

SEQ = 4096
D_MODEL = 1024
DEPTH = 4
N_DEV = 8
ATTN_WIDTH = 512
IN_PROJ_WIDTH = 3072
FFN_HIDDEN = 2816
FFN_BLOCK = 2 * FFN_HIDDEN // N_DEV
W_IN_BLOCK = IN_PROJ_WIDTH // N_DEV
W_OUT_BLOCK = D_MODEL // N_DEV
W_DOWN_BLOCK = FFN_HIDDEN // N_DEV
HEAD_DIM = 64
ROPE_DIM = 16
ROPE_THETA = 500000.0
DILATIONS = (1, 4, 16)
HALF_WINDOW = 64
RMS_EPS = 1e-6
NEG_INF = -1e30
LANES = 128
Q_BLOCK = 128
K_WINDOW = Q_BLOCK + 2 * HALF_WINDOW
PERM_CHUNK = 256
ROW_TILE = 512
FFN_TILE = 256
WGRAD_TILE = 1024
ADAM_LR, ADAM_B1, ADAM_B2, ADAM_EPS, ADAM_WD, ADAM_STEP = 0.001, 0.9, 0.999, 1e-08, 0.01, 10
MIB = 1024 * 1024


def _params(vmem_mib):
    return pltpu.CompilerParams(vmem_limit_bytes=vmem_mib * MIB)


def _sds(shape, dtype):
    return jax.ShapeDtypeStruct(shape, dtype)


def _rows(width, tile=ROW_TILE):
    return pl.BlockSpec((tile, width), lambda i: (i, 0))


def _const(shape):
    return pl.BlockSpec(shape, lambda i: (0,) * len(shape))


def _const1(shape):
    return pl.BlockSpec(shape, lambda i: (0,) * len(shape), pipeline_mode=pl.Buffered(1))


def _halo_prev(width, tile=ROW_TILE):
    return pl.BlockSpec((8, width), lambda i: (jnp.maximum(i * (tile // 8) - 1, 0), 0))


def _halo_next(width, tile=ROW_TILE):
    return pl.BlockSpec((8, width), lambda i: (jnp.minimum((i + 1) * (tile // 8), SEQ // 8 - 1), 0))


def _rms(x):
    r = lax.rsqrt(jnp.mean(x * x, axis=-1, keepdims=True) + RMS_EPS)
    return x * r, r


def _rms_bwd(dn, y, r):
    return r * (dn - y * jnp.mean(dn * y, axis=-1, keepdims=True))


def _dot(a, b):
    return jnp.dot(a, b, preferred_element_type=F32)


def _dot_nt(a, b):
    return lax.dot_general(a, b, (((1,), (1,)), ((), ())), preferred_element_type=F32)


def _dot_tn(a, b):
    return lax.dot_general(a, b, (((0,), (0,)), ((), ())), preferred_element_type=F32)


def _rope_tables(pos_col, freq_row):
    def body(p_ref, f_ref, c_ref, sa_ref, sb_ref):
        ang = p_ref[...].astype(F32) * f_ref[...]
        lane = lax.broadcasted_iota(jnp.int32, ang.shape, 1) % HEAD_DIM
        cos, sin = jnp.cos(ang), jnp.sin(ang)
        c_ref[...] = jnp.where(lane < ROPE_DIM, cos, 1.0)
        sa_ref[...] = jnp.where(lane < ROPE_DIM // 2, -sin, 0.0)
        sb_ref[...] = jnp.where((lane >= ROPE_DIM // 2) & (lane < ROPE_DIM), sin, 0.0)

    return pl.pallas_call(
        body, grid=(SEQ // ROW_TILE,), in_specs=[_rows(1), _const((1, LANES))], out_specs=[_rows(LANES)] * 3,
        out_shape=[_sds((SEQ, LANES), F32)] * 3, name="rope_tables")(pos_col, freq_row)


def _rotate(t, c, sa, sb):
    parts = []
    for g in range(ATTN_WIDTH // LANES):
        tg = t[:, g * LANES:(g + 1) * LANES]
        parts.append(tg * c + pltpu.roll(tg, LANES - 8, axis=1) * sa + pltpu.roll(tg, 8, axis=1) * sb)
    return jnp.concatenate(parts, axis=1)


def _rotate_transposed(dt, c, sa, sb):
    parts = []
    for g in range(ATTN_WIDTH // LANES):
        dg = dt[:, g * LANES:(g + 1) * LANES]
        parts.append(dg * c + pltpu.roll(dg * sa, 8, axis=1) + pltpu.roll(dg * sb, LANES - 8, axis=1))
    return jnp.concatenate(parts, axis=1)


def _inproj_fwd(x, g_pre, w_in, tc, tsa, tsb):
    def body(x_ref, g_ref, w_ref, c_ref, sa_ref, sb_ref, q_ref, k_ref, v_ref, u_ref, b_ref, cc_ref):
        y, _ = _rms(x_ref[...])
        h = (y * g_ref[...]).astype(BF16)

        def proj(n):
            return _dot(h, w_ref[:, n * ATTN_WIDTH:(n + 1) * ATTN_WIDTH])

        c, sa, sb = c_ref[...], sa_ref[...], sb_ref[...]
        q_ref[...] = (_rotate(proj(0), c, sa, sb) * (HEAD_DIM ** -0.5)).astype(BF16)
        k_ref[...] = _rotate(proj(1), c, sa, sb).astype(BF16)
        v_ref[...] = proj(2).astype(BF16)
        u_ref[...] = proj(3)
        b_ref[...] = proj(4)
        cc_ref[...] = proj(5)

    a = ATTN_WIDTH
    return pl.pallas_call(
        body, grid=(SEQ // ROW_TILE,),
        in_specs=[_rows(D_MODEL), _const((1, D_MODEL)), _const1((D_MODEL, IN_PROJ_WIDTH)), _rows(LANES), _rows(LANES), _rows(LANES)],
        out_specs=[_rows(a)] * 6,
        out_shape=[_sds((SEQ, a), BF16)] * 3 + [_sds((SEQ, a), F32)] * 3,
        compiler_params=_params(40), name="inproj_fwd")(x, g_pre, w_in, tc, tsa, tsb)


def _head_masks():
    lane = lax.broadcasted_iota(jnp.int32, (1, LANES), 1)
    first = lane < HEAD_DIM
    return first, first.astype(F32), 1.0 - first.astype(F32)


def _perm_chunks(dil):
    length = SEQ // dil
    out = []
    for r in range(dil):
        for c0 in range(0, length, PERM_CHUNK):
            chunk = (r * length + c0) // PERM_CHUNK
            rows = pl.ds(c0, PERM_CHUNK) if dil == 1 else pl.ds(r + dil * c0, PERM_CHUNK, stride=dil)
            out.append((chunk, rows))
    return out


def _chunk(c, offset=0):
    return pl.ds(offset + c * PERM_CHUNK, PERM_CHUNK)


def _band_mask(m0, length):
    rr = lax.broadcasted_iota(jnp.int32, (Q_BLOCK, K_WINDOW), 0)
    cc = lax.broadcasted_iota(jnp.int32, (Q_BLOCK, K_WINDOW), 1)
    lo = (m0 // length) * length
    key = m0 - HALF_WINDOW + cc
    return (cc >= rr) & (cc - rr <= 2 * HALF_WINDOW) & (key >= lo) & (key < lo + length)


def _attn_fwd(q, k, v):
    def body(q_ref, k_ref, v_ref, o_ref, lse_ref, st32, qp, ka, kb, va, vb, accp, mp, lp, z_ref):
        first, mask_a, mask_b = _head_masks()
        pad = jnp.zeros((HALF_WINDOW, LANES), BF16)
        for buf in (ka, kb, va, vb):
            buf[pl.ds(0, HALF_WINDOW), :] = pad
            buf[pl.ds(SEQ + HALF_WINDOW, HALF_WINDOW), :] = pad

        for branch, dil in enumerate(DILATIONS):
            length = SEQ // dil
            chunks = _perm_chunks(dil)
            st32[...] = q_ref[...].astype(F32)
            for c, rows in chunks:
                qp[_chunk(c), :] = st32[rows, :].astype(BF16)
            for src, da, db in ((k_ref, ka, kb), (v_ref, va, vb)):
                st32[...] = src[...].astype(F32)
                for c, rows in chunks:
                    val = st32[rows, :]
                    da[_chunk(c, HALF_WINDOW), :] = (val * mask_a).astype(BF16)
                    db[_chunk(c, HALF_WINDOW), :] = (val * mask_b).astype(BF16)

            def block(i, carry):
                m0 = pl.multiple_of(i * Q_BLOCK, Q_BLOCK)
                valid = _band_mask(m0, length)
                qb = qp[pl.ds(m0, Q_BLOCK), :]
                acc, ms, ls = [], [], []
                for kx, vx in ((ka, va), (kb, vb)):
                    s = _dot_nt(qb, kx[pl.ds(m0, K_WINDOW), :])
                    s = jnp.where(valid, s, NEG_INF)
                    m = jnp.max(s, axis=1, keepdims=True)
                    p = jnp.exp(s - m)
                    ls.append(jnp.sum(p, axis=1, keepdims=True))
                    ms.append(m)
                    acc.append(_dot(p.astype(BF16), vx[pl.ds(m0, K_WINDOW), :]))
                accp[pl.ds(m0, Q_BLOCK), :] = acc[0] + acc[1]
                mp[pl.ds(m0, Q_BLOCK), :] = jnp.where(first, ms[0], ms[1])
                lp[pl.ds(m0, Q_BLOCK), :] = jnp.where(first, ls[0], ls[1])
                return carry

            lax.fori_loop(0, SEQ // Q_BLOCK, block, 0)

            for c, rows in chunks:
                a, m, l = accp[_chunk(c), :], mp[_chunk(c), :], lp[_chunk(c), :]
                if branch == 0:
                    o_ref[rows, :] = a
                    lse_ref[rows, :] = m
                    z_ref[rows, :] = l
                else:
                    m_old = lse_ref[rows, :]
                    m_new = jnp.maximum(m_old, m)
                    e_old, e_new = jnp.exp(m_old - m_new), jnp.exp(m - m_new)
                    o_ref[rows, :] = o_ref[rows, :] * e_old + a * e_new
                    z_ref[rows, :] = z_ref[rows, :] * e_old + l * e_new
                    lse_ref[rows, :] = m_new

        for c in range(SEQ // PERM_CHUNK):
            z = z_ref[_chunk(c), :]
            o_ref[_chunk(c), :] = o_ref[_chunk(c), :] / z
            lse_ref[_chunk(c), :] = lse_ref[_chunk(c), :] + jnp.log(z)

    col = pl.BlockSpec((SEQ, LANES), lambda h: (0, h))
    padded = SEQ + 2 * HALF_WINDOW
    return pl.pallas_call(
        body, grid=(ATTN_WIDTH // LANES,), in_specs=[col] * 3, out_specs=[col] * 2,
        out_shape=[_sds((SEQ, ATTN_WIDTH), F32)] * 2,
        scratch_shapes=[pltpu.VMEM((SEQ, LANES), F32), pltpu.VMEM((SEQ, LANES), BF16)]
        + [pltpu.VMEM((padded, LANES), BF16)] * 4 + [pltpu.VMEM((SEQ, LANES), F32)] * 4,
        compiler_params=_params(44), name="attn_fwd")(q, k, v)


def _shifted(t, before, after, i):
    tile = t.shape[0]
    row = lax.broadcasted_iota(jnp.int32, (tile, 1), 0)
    before = jnp.where(i > 0, before, 0.0)
    after = jnp.where(i < SEQ // tile - 1, after, 0.0)
    return (jnp.where(row == 0, before, pltpu.roll(t, 1, axis=0)),
            jnp.where(row == tile - 1, after, pltpu.roll(t, tile - 1, axis=0)))


def _conv_parts(u, c, u_prev, c_prev, u_next, c_next, cw, i):
    t = c * u
    t_prev, t_next = _shifted(t, c_prev[7:8, :] * u_prev[7:8, :], c_next[0:1, :] * u_next[0:1, :], i)
    s = cw[0:1, :] * t_prev + cw[1:2, :] * t + cw[2:3, :] * t_next
    return t, t_prev, t_next, s


def _mix_fwd(attn, u, b, c, conv_w, g_attn, g_conv, w_out, x, g_post):
    def body(a_ref, u_ref, b_ref, c_ref, up_ref, cp_ref, un_ref, cn_ref, cw_ref, ga_ref, gc_ref, w_ref, x_ref, gp_ref,
             x1_ref, mg_ref, mix_ref):
        i = pl.program_id(0)
        _, _, _, s = _conv_parts(u_ref[...], c_ref[...], up_ref[...], cp_ref[...], un_ref[...], cn_ref[...], cw_ref[...], i)
        ya, _ = _rms(a_ref[...])
        yc, _ = _rms(b_ref[...] * s)
        merged = jnp.concatenate([ya * ga_ref[...], yc * gc_ref[...]], axis=1).astype(BF16)
        mix = _dot(merged, w_ref[...])
        ym, _ = _rms(mix)
        mg_ref[...] = merged
        mix_ref[...] = mix
        x1_ref[...] = x_ref[...] + ym * gp_ref[...]

    a = ATTN_WIDTH
    return pl.pallas_call(
        body, grid=(SEQ // ROW_TILE,),
        in_specs=[_rows(a)] * 4 + [_halo_prev(a)] * 2 + [_halo_next(a)] * 2
        + [_const((3, a)), _const((1, a)), _const((1, a)), _const1((D_MODEL, D_MODEL)), _rows(D_MODEL), _const((1, D_MODEL))],
        out_specs=[_rows(D_MODEL)] * 3,
        out_shape=[_sds((SEQ, D_MODEL), F32), _sds((SEQ, D_MODEL), BF16), _sds((SEQ, D_MODEL), F32)],
        compiler_params=_params(40), name="mix_fwd")(attn, u, b, c, u, c, u, c, conv_w, g_attn, g_conv, w_out, x, g_post)


def _gu_spec():
    return pl.BlockSpec((N_DEV, FFN_TILE, FFN_BLOCK), lambda i: (0, i, 0))


def _frows(width):
    return _rows(width, FFN_TILE)


def _ffn_fwd(x1, g_pre, w_gu, w_dn, g_post):
    def body(x_ref, g_ref, wgu_ref, wdn_ref, gp_ref, x2_ref, gu_ref, f_ref):
        x1 = x_ref[...]
        y, _ = _rms(x1)
        h = (y * g_ref[...]).astype(BF16)
        f = jnp.zeros((FFN_TILE, D_MODEL), F32)
        for j in range(N_DEV // 2):
            gate = _dot(h, wgu_ref[j])
            up = _dot(h, wgu_ref[j + N_DEV // 2])
            gu_ref[j] = gate.astype(BF16)
            gu_ref[j + N_DEV // 2] = up.astype(BF16)
            act = (gate * jax.nn.sigmoid(gate) * up).astype(BF16)
            f = f + _dot(act, wdn_ref[pl.ds(j * FFN_BLOCK, FFN_BLOCK), :])
        yf, _ = _rms(f)
        f_ref[...] = f
        x2_ref[...] = x1 + yf * gp_ref[...]

    return pl.pallas_call(
        body, grid=(SEQ // FFN_TILE,),
        in_specs=[_frows(D_MODEL), _const((1, D_MODEL)), _const1((N_DEV, D_MODEL, FFN_BLOCK)), _const1((FFN_HIDDEN, D_MODEL)),
                  _const((1, D_MODEL))],
        out_specs=[_frows(D_MODEL), _gu_spec(), _frows(D_MODEL)],
        out_shape=[_sds((SEQ, D_MODEL), F32), _sds((N_DEV, SEQ, FFN_BLOCK), BF16), _sds((SEQ, D_MODEL), F32)],
        compiler_params=_params(48), name="ffn_fwd")(x1, g_pre, w_gu, w_dn, g_post)


def _loss_and_grad(y, target):
    n_tiles = SEQ // ROW_TILE

    def body(y_ref, t_ref, loss_ref, dy_ref, acc):
        i = pl.program_id(0)
        err = y_ref[...] - t_ref[...]
        dy_ref[...] = err * (1.0 / D_MODEL)

        @pl.when(i == 0)
        def _():
            acc[...] = jnp.zeros_like(acc)

        acc[...] += jnp.sum(err * err, axis=0, keepdims=True)

        @pl.when(i == n_tiles - 1)
        def _():
            loss_ref[...] = jnp.sum(acc[...], axis=1, keepdims=True) * (0.5 / D_MODEL)

    return pl.pallas_call(
        body, grid=(n_tiles,), in_specs=[_rows(D_MODEL)] * 2, out_specs=[_const((1, 1)), _rows(D_MODEL)],
        out_shape=[_sds((1, 1), F32), _sds((SEQ, D_MODEL), F32)], scratch_shapes=[pltpu.VMEM((1, D_MODEL), F32)],
        name="loss_and_grad")(y, target)


def _accumulate(ref, value, i):
    @pl.when(i == 0)
    def _():
        ref[...] = value

    @pl.when(i > 0)
    def _():
        ref[...] += value


def _colsum(v):
    return jnp.sum(v, axis=0, keepdims=True)


def _ffn_bwd(dx2, f, x1, gu, w_gu, w_dn, g_post, g_pre):
    half = N_DEV // 2

    def body(dx2_ref, f_ref, x1_ref, gu_ref, wgu_ref, wdn_ref, gpost_ref, gpre_ref,
             dx1_ref, df_ref, act_ref, dgu_ref, h_ref, dgpost_ref, dgpre_ref):
        i = pl.program_id(0)
        dx2 = dx2_ref[...]
        yf, rf = _rms(f_ref[...])
        _accumulate(dgpost_ref, _colsum(dx2 * yf), i)
        df = _rms_bwd(dx2 * gpost_ref[...], yf, rf).astype(BF16)
        df_ref[...] = df
        dh = jnp.zeros((FFN_TILE, D_MODEL), F32)
        for j in range(half):
            dact = _dot_nt(df, wdn_ref[pl.ds(j * FFN_BLOCK, FFN_BLOCK), :])
            gate = gu_ref[j].astype(F32)
            up = gu_ref[j + half].astype(F32)
            sig = jax.nn.sigmoid(gate)
            silu = gate * sig
            act_ref[j] = (silu * up).astype(BF16)
            dgate = (dact * up * (sig * (1.0 + gate * (1.0 - sig)))).astype(BF16)
            dup = (dact * silu).astype(BF16)
            dgu_ref[j] = dgate
            dgu_ref[j + half] = dup
            dh = dh + _dot_nt(dgate, wgu_ref[j]) + _dot_nt(dup, wgu_ref[j + half])
        y1, r1 = _rms(x1_ref[...])
        h_ref[...] = (y1 * gpre_ref[...]).astype(BF16)
        _accumulate(dgpre_ref, _colsum(dh * y1), i)
        dx1_ref[...] = dx2 + _rms_bwd(dh * gpre_ref[...], y1, r1)

    act_spec = pl.BlockSpec((half, FFN_TILE, FFN_BLOCK), lambda i: (0, i, 0))
    return pl.pallas_call(
        body, grid=(SEQ // FFN_TILE,),
        in_specs=[_frows(D_MODEL)] * 3 + [_gu_spec(), _const1((N_DEV, D_MODEL, FFN_BLOCK)), _const1((FFN_HIDDEN, D_MODEL)),
                                          _const((1, D_MODEL)), _const((1, D_MODEL))],
        out_specs=[_frows(D_MODEL), _frows(D_MODEL), act_spec, _gu_spec(), _frows(D_MODEL), _const((1, D_MODEL)), _const((1, D_MODEL))],
        out_shape=[_sds((SEQ, D_MODEL), F32), _sds((SEQ, D_MODEL), BF16), _sds((half, SEQ, FFN_BLOCK), BF16),
                   _sds((N_DEV, SEQ, FFN_BLOCK), BF16), _sds((SEQ, D_MODEL), BF16), _sds((1, D_MODEL), F32), _sds((1, D_MODEL), F32)],
        compiler_params=_params(52), name="ffn_bwd")(dx2, f, x1, gu, w_gu, w_dn, g_post, g_pre)


def _wgrad(a, g, a_spec, g_spec, out_shape, out_spec, acc_shape, n_blocks, name):
    n_tiles = SEQ // WGRAD_TILE

    def body(a_ref, g_ref, o_ref, acc):
        s = pl.program_id(1)
        av = a_ref[...].reshape(WGRAD_TILE, a_ref.shape[-1])
        gv = g_ref[...].reshape(WGRAD_TILE, g_ref.shape[-1])
        part = _dot_tn(av, gv)

        @pl.when(s == 0)
        def _():
            acc[...] = part

        @pl.when(s > 0)
        def _():
            acc[...] += part

        @pl.when(s == n_tiles - 1)
        def _():
            o_ref[...] = acc[...].astype(BF16).reshape(o_ref.shape)

    return pl.pallas_call(
        body, grid=(n_blocks, n_tiles), in_specs=[a_spec, g_spec], out_specs=out_spec, out_shape=_sds(out_shape, BF16),
        scratch_shapes=[pltpu.VMEM(acc_shape, F32)], compiler_params=_params(40), name=name)(a, g)


def _wgrad_gate_up(h, dgu):
    t = WGRAD_TILE
    return _wgrad(h, dgu, pl.BlockSpec((t, D_MODEL), lambda j, s: (s, 0)), pl.BlockSpec((1, t, FFN_BLOCK), lambda j, s: (j, s, 0)),
                  (N_DEV, D_MODEL, FFN_BLOCK), pl.BlockSpec((1, D_MODEL, FFN_BLOCK), lambda j, s: (j, 0, 0)),
                  (D_MODEL, FFN_BLOCK), N_DEV, "wgrad_gate_up")


def _wgrad_down(act, df):
    t = WGRAD_TILE
    return _wgrad(act, df, pl.BlockSpec((1, t, FFN_BLOCK), lambda j, s: (j, s, 0)), pl.BlockSpec((t, D_MODEL), lambda j, s: (s, 0)),
                  (FFN_HIDDEN, D_MODEL), pl.BlockSpec((FFN_BLOCK, D_MODEL), lambda j, s: (j, 0)),
                  (FFN_BLOCK, D_MODEL), N_DEV // 2, "wgrad_down")


def _wgrad_out(merged, dmix):
    t = WGRAD_TILE
    return _wgrad(merged, dmix, pl.BlockSpec((t, D_MODEL), lambda j, s: (s, 0)), pl.BlockSpec((t, 512), lambda j, s: (s, j)),
                  (D_MODEL, D_MODEL), pl.BlockSpec((D_MODEL, 512), lambda j, s: (0, j)), (D_MODEL, 512), 2, "wgrad_out")


def _wgrad_in(h, dproj):
    t = WGRAD_TILE
    return _wgrad(h, dproj, pl.BlockSpec((t, D_MODEL), lambda j, s: (s, 0)), pl.BlockSpec((t, 512), lambda j, s: (s, j)),
                  (D_MODEL, IN_PROJ_WIDTH), pl.BlockSpec((D_MODEL, 512), lambda j, s: (0, j)), (D_MODEL, 512),
                  IN_PROJ_WIDTH // 512, "wgrad_in")


def _mix_bwd(dx1, mix, attn, u, b, c, conv_w, g_attn, g_conv, g_post, w_out):
    def body(dx1_ref, mix_ref, a_ref, u_ref, b_ref, c_ref, up_ref, cp_ref, un_ref, cn_ref, cw_ref, ga_ref, gc_ref, gp_ref, w_ref,
             dmix_ref, da_ref, ds_ref, db_ref, dgp_ref, dga_ref, dgc_ref):
        i = pl.program_id(0)
        dx1 = dx1_ref[...]
        ym, rm = _rms(mix_ref[...])
        _accumulate(dgp_ref, _colsum(dx1 * ym), i)
        dmix = _rms_bwd(dx1 * gp_ref[...], ym, rm).astype(BF16)
        dmix_ref[...] = dmix
        dmerged = _dot_nt(dmix, w_ref[...])
        dna, dnc = dmerged[:, :ATTN_WIDTH], dmerged[:, ATTN_WIDTH:]
        ya, ra = _rms(a_ref[...])
        _accumulate(dga_ref, _colsum(dna * ya), i)
        da_ref[...] = _rms_bwd(dna * ga_ref[...], ya, ra)
        _, _, _, s = _conv_parts(u_ref[...], c_ref[...], up_ref[...], cp_ref[...], un_ref[...], cn_ref[...], cw_ref[...], i)
        gate_b = b_ref[...]
        yc, rc = _rms(gate_b * s)
        _accumulate(dgc_ref, _colsum(dnc * yc), i)
        dy = _rms_bwd(dnc * gc_ref[...], yc, rc)
        db_ref[...] = dy * s
        ds_ref[...] = dy * gate_b

    a = ATTN_WIDTH
    return pl.pallas_call(
        body, grid=(SEQ // ROW_TILE,),
        in_specs=[_rows(D_MODEL)] * 2 + [_rows(a)] * 4 + [_halo_prev(a)] * 2 + [_halo_next(a)] * 2
        + [_const((3, a)), _const((1, a)), _const((1, a)), _const((1, D_MODEL)), _const1((D_MODEL, D_MODEL))],
        out_specs=[_rows(D_MODEL)] + [_rows(a)] * 3 + [_const((1, D_MODEL)), _const((1, a)), _const((1, a))],
        out_shape=[_sds((SEQ, D_MODEL), BF16)] + [_sds((SEQ, a), F32)] * 3 + [_sds((1, D_MODEL), F32), _sds((1, a), F32), _sds((1, a), F32)],
        compiler_params=_params(40), name="mix_bwd")(dx1, mix, attn, u, b, c, u, c, u, c, conv_w, g_attn, g_conv, g_post, w_out)


def _attn_bwd(q, k, v, do, o, lse):
    def body(q_ref, k_ref, v_ref, do_ref, o_ref, lse_ref, dq_ref, dk_ref, dv_ref,
             st32, qa, qb, doa, dob, ka, kb, va, vb, lsep, dlp, dnat, dqp, dkp, dvp):
        first, mask_a, mask_b = _head_masks()
        pad = jnp.zeros((HALF_WINDOW, LANES), BF16)
        for buf in (ka, kb, va, vb):
            buf[pl.ds(0, HALF_WINDOW), :] = pad
            buf[pl.ds(SEQ + HALF_WINDOW, HALF_WINDOW), :] = pad
        for c in range(SEQ // PERM_CHUNK):
            prod = do_ref[_chunk(c), :] * o_ref[_chunk(c), :]
            d_a = jnp.sum(prod * mask_a, axis=1, keepdims=True)
            d_b = jnp.sum(prod * mask_b, axis=1, keepdims=True)
            dnat[_chunk(c), :] = jnp.where(first, d_a, d_b)

        for branch, dil in enumerate(DILATIONS):
            length = SEQ // dil
            chunks = _perm_chunks(dil)
            for src, da, db, off in ((q_ref, qa, qb, 0), (k_ref, ka, kb, HALF_WINDOW), (v_ref, va, vb, HALF_WINDOW)):
                st32[...] = src[...].astype(F32)
                for c, rows in chunks:
                    val = st32[rows, :]
                    da[_chunk(c, off), :] = (val * mask_a).astype(BF16)
                    db[_chunk(c, off), :] = (val * mask_b).astype(BF16)
            for c, rows in chunks:
                val = do_ref[rows, :]
                doa[_chunk(c), :] = (val * mask_a).astype(BF16)
                dob[_chunk(c), :] = (val * mask_b).astype(BF16)
                lsep[_chunk(c), :] = lse_ref[rows, :]
                dlp[_chunk(c), :] = dnat[rows, :]
            zero = jnp.zeros((PERM_CHUNK, LANES), F32)
            for c in range(SEQ // PERM_CHUNK):
                dkp[_chunk(c), :] = zero
                dvp[_chunk(c), :] = zero
            dkp[pl.ds(SEQ, 2 * HALF_WINDOW), :] = zero[:2 * HALF_WINDOW]
            dvp[pl.ds(SEQ, 2 * HALF_WINDOW), :] = zero[:2 * HALF_WINDOW]

            def block(i, carry):
                m0 = pl.multiple_of(i * Q_BLOCK, Q_BLOCK)
                valid = _band_mask(m0, length)
                qrows, krows = pl.ds(m0, Q_BLOCK), pl.ds(m0, K_WINDOW)
                lse_b, d_b = lsep[qrows, :], dlp[qrows, :]
                dq = jnp.zeros((Q_BLOCK, LANES), F32)
                dk = jnp.zeros((K_WINDOW, LANES), F32)
                dv = jnp.zeros((K_WINDOW, LANES), F32)
                for qx, dox, kx, vx, col in ((qa, doa, ka, va, 0), (qb, dob, kb, vb, HEAD_DIM)):
                    qv, dov, kw, vw = qx[qrows, :], dox[qrows, :], kx[krows, :], vx[krows, :]
                    s = jnp.where(valid, _dot_nt(qv, kw), NEG_INF)
                    p = jnp.exp(s - lse_b[:, col:col + 1])
                    ds = (p * (_dot_nt(dov, vw) - d_b[:, col:col + 1])).astype(BF16)
                    dq = dq + _dot(ds, kw)
                    dk = dk + _dot_tn(ds, qv)
                    dv = dv + _dot_tn(p.astype(BF16), dov)
                dqp[qrows, :] = dq * (HEAD_DIM ** -0.5)
                dkp[krows, :] += dk
                dvp[krows, :] += dv
                return carry

            lax.fori_loop(0, SEQ // Q_BLOCK, block, 0)

            for c, rows in chunks:
                g_q, g_k, g_v = dqp[_chunk(c), :], dkp[_chunk(c, HALF_WINDOW), :], dvp[_chunk(c, HALF_WINDOW), :]
                if branch == 0:
                    dq_ref[rows, :] = g_q
                    dk_ref[rows, :] = g_k
                    dv_ref[rows, :] = g_v
                else:
                    dq_ref[rows, :] = dq_ref[rows, :] + g_q
                    dk_ref[rows, :] = dk_ref[rows, :] + g_k
                    dv_ref[rows, :] = dv_ref[rows, :] + g_v

    col = pl.BlockSpec((SEQ, LANES), lambda h: (0, h))
    col1 = pl.BlockSpec((SEQ, LANES), lambda h: (0, h), pipeline_mode=pl.Buffered(1))
    padded = SEQ + 2 * HALF_WINDOW
    return pl.pallas_call(
        body, grid=(ATTN_WIDTH // LANES,), in_specs=[col1] * 6, out_specs=[col] * 3,
        out_shape=[_sds((SEQ, ATTN_WIDTH), F32)] * 3,
        scratch_shapes=[pltpu.VMEM((SEQ, LANES), F32)] + [pltpu.VMEM((SEQ, LANES), BF16)] * 4
        + [pltpu.VMEM((padded, LANES), BF16)] * 4 + [pltpu.VMEM((SEQ, LANES), F32)] * 4 + [pltpu.VMEM((padded, LANES), F32)] * 2,
        compiler_params=_params(56), name="attn_bwd")(q, k, v, do, o, lse)


def _inproj_bwd(dq, dk, dv, ds, db, u, c, conv_w, tc, tsa, tsb, w_in, x, g_pre, dx1):
    def body(dq_ref, dk_ref, dv_ref, ds_ref, db_ref, u_ref, c_ref, dsp_ref, up_ref, cp_ref, dsn_ref, un_ref, cn_ref, cw_ref,
             tc_ref, tsa_ref, tsb_ref, w_ref, x_ref, g_ref, dx1_ref, dx_ref, dproj_ref, h_ref, dg_ref, dcw_ref):
        i = pl.program_id(0)
        cw = cw_ref[...]
        u, c = u_ref[...], c_ref[...]
        t, t_prev, t_next, _ = _conv_parts(u, c, up_ref[...], cp_ref[...], un_ref[...], cn_ref[...], cw, i)
        ds = ds_ref[...]
        ds_prev, ds_next = _shifted(ds, dsp_ref[7:8, :], dsn_ref[0:1, :], i)
        dt = cw[0:1, :] * ds_next + cw[1:2, :] * ds + cw[2:3, :] * ds_prev
        _accumulate(dcw_ref, jnp.concatenate([_colsum(ds * t_prev), _colsum(ds * t), _colsum(ds * t_next)], axis=0), i)
        tc_, tsa_, tsb_ = tc_ref[...], tsa_ref[...], tsb_ref[...]
        dproj = jnp.concatenate(
            [_rotate_transposed(dq_ref[...], tc_, tsa_, tsb_), _rotate_transposed(dk_ref[...], tc_, tsa_, tsb_), dv_ref[...],
             dt * c, db_ref[...], dt * u], axis=1).astype(BF16)
        dproj_ref[...] = dproj
        dh = _dot_nt(dproj, w_ref[...])
        y, r = _rms(x_ref[...])
        h_ref[...] = (y * g_ref[...]).astype(BF16)
        _accumulate(dg_ref, _colsum(dh * y), i)
        dx_ref[...] = dx1_ref[...] + _rms_bwd(dh * g_ref[...], y, r)

    a = ATTN_WIDTH
    tile = FFN_TILE
    rows = functools.partial(_rows, tile=tile)
    return pl.pallas_call(
        body, grid=(SEQ // tile,),
        in_specs=[rows(a)] * 7 + [_halo_prev(a, tile)] * 3 + [_halo_next(a, tile)] * 3
        + [_const((3, a)), rows(LANES), rows(LANES), rows(LANES), _const1((D_MODEL, IN_PROJ_WIDTH)), rows(D_MODEL),
           _const((1, D_MODEL)), rows(D_MODEL)],
        out_specs=[rows(D_MODEL), rows(IN_PROJ_WIDTH), rows(D_MODEL), _const((1, D_MODEL)), _const((3, a))],
        out_shape=[_sds((SEQ, D_MODEL), F32), _sds((SEQ, IN_PROJ_WIDTH), BF16), _sds((SEQ, D_MODEL), BF16),
                   _sds((1, D_MODEL), F32), _sds((3, a), F32)],
        compiler_params=_params(48), name="inproj_bwd")(dq, dk, dv, ds, db, u, c, ds, u, c, ds, u, c, conv_w, tc, tsa, tsb, w_in, x, g_pre, dx1)


def _place():
    return lax.axis_index("x"), lax.axis_index("y"), lax.axis_index("c")


def _block_of(px, py, pc):
    return 4 * px + 2 * py + pc


def _weight_block(ref, kind, blk):
    if kind == "in":
        return ref.at[:, pl.ds(blk * W_IN_BLOCK, W_IN_BLOCK)]
    if kind == "out":
        return ref.at[pl.ds(blk * W_OUT_BLOCK, W_OUT_BLOCK), :]
    if kind == "gu":
        return ref.at[blk]
    return ref.at[pl.ds(blk * W_DOWN_BLOCK, W_DOWN_BLOCK), :]


WEIGHT_KINDS = ("in", "out", "gu", "dn")
FULL_SHAPES = {"in": (D_MODEL, IN_PROJ_WIDTH), "out": (D_MODEL, D_MODEL), "gu": (N_DEV, D_MODEL, FFN_BLOCK), "dn": (FFN_HIDDEN, D_MODEL)}
SHARD_SHAPES = {"in": (D_MODEL, W_IN_BLOCK), "out": (W_OUT_BLOCK, D_MODEL), "gu": (D_MODEL, FFN_BLOCK), "dn": (W_DOWN_BLOCK, D_MODEL)}
ANY = pl.BlockSpec(memory_space=pl.ANY)


def _allgather_weights(shards):
    n = len(WEIGHT_KINDS)

    def body(*refs):
        srcs, outs = refs[:n], refs[n:2 * n]
        send, recv, local = refs[2 * n:]
        x, y, c = _place()
        me, sibling = (x, y, c), (x, y, 1 - c)
        chips = [(1 - x, y), (x, 1 - y), (1 - x, 1 - y)]

        def copy(w, k, block, to, src=None):
            dst = _weight_block(outs[w], WEIGHT_KINDS[w], _block_of(*block))
            return pltpu.make_async_remote_copy(
                src_ref=dst if src is None else src, dst_ref=dst, send_sem=send.at[w, k], recv_sem=recv.at[w, k],
                device_id=to, device_id_type=MESH)

        own, first, passed = [], [], []
        for w in range(n):
            mine = pltpu.make_async_copy(srcs[w], _weight_block(outs[w], WEIGHT_KINDS[w], _block_of(*me)), local.at[w])
            mine.start()
            own.append(mine)
            first.append(copy(w, 0, me, sibling, src=srcs[w]))
            first += [copy(w, 1 + j, me, (*chip, c), src=srcs[w]) for j, chip in enumerate(chips)]
        for cp in first:
            cp.start()
        for j, chip in enumerate(chips):
            for w in range(n):
                copy(w, 1 + j, (*chip, c), me).wait_recv()
                fwd = copy(w, 4 + j, (*chip, c), sibling)
                fwd.start()
                passed.append(fwd)
        for w in range(n):
            copy(w, 0, sibling, me).wait_recv()
            for j, chip in enumerate(chips):
                copy(w, 4 + j, (*chip, 1 - c), me).wait_recv()
        for cp in first + passed:
            cp.wait_send()
        for mine in own:
            mine.wait()

    return pl.pallas_call(
        body, in_specs=[ANY] * n, out_specs=[ANY] * n, out_shape=[_sds(FULL_SHAPES[k], BF16) for k in WEIGHT_KINDS],
        scratch_shapes=[pltpu.SemaphoreType.DMA((n, 7)), pltpu.SemaphoreType.DMA((n, 7)), pltpu.SemaphoreType.DMA((n,))],
        name="allgather_weights")(*shards)


def _peers(x, y, c):
    return [(x ^ a, y ^ b, c ^ e) for a in (0, 1) for b in (0, 1) for e in (0, 1) if (a, b, e) != (0, 0, 0)]


def _reduce_scatter_send(grads):
    n = len(WEIGHT_KINDS)

    def body(*refs):
        srcs, outs = refs[:n], refs[n:2 * n]
        send, recv, local = refs[2 * n:]
        x, y, c = _place()
        me = _block_of(x, y, c)
        copies = []
        for w, kind in enumerate(WEIGHT_KINDS):
            mine = pltpu.make_async_copy(_weight_block(srcs[w], kind, me), outs[w].at[me], local.at[w])
            mine.start()
            copies.append(mine)
            for k, peer in enumerate(_peers(x, y, c)):
                cp = pltpu.make_async_remote_copy(
                    src_ref=_weight_block(srcs[w], kind, _block_of(*peer)), dst_ref=outs[w].at[me],
                    send_sem=send.at[w, k], recv_sem=recv.at[w, k], device_id=peer, device_id_type=MESH)
                cp.start()
                copies.append(cp)
        for cp in copies:
            cp.wait()

    return pl.pallas_call(
        body, in_specs=[ANY] * n, out_specs=[ANY] * n, out_shape=[_sds((N_DEV,) + SHARD_SHAPES[k], BF16) for k in WEIGHT_KINDS],
        scratch_shapes=[pltpu.SemaphoreType.DMA((n, 7)), pltpu.SemaphoreType.DMA((n, 7)), pltpu.SemaphoreType.DMA((n,))],
        name="reduce_scatter_send")(*grads)


def _allgather_small(v, name):
    def body(v_ref, o_ref, send, recv, local):
        x, y, c = _place()
        me = _block_of(x, y, c)
        mine = pltpu.make_async_copy(v_ref, o_ref.at[me], local)
        mine.start()
        copies = [mine]
        for k, peer in enumerate(_peers(x, y, c)):
            cp = pltpu.make_async_remote_copy(src_ref=v_ref, dst_ref=o_ref.at[me], send_sem=send.at[k], recv_sem=recv.at[k],
                                              device_id=peer, device_id_type=MESH)
            cp.start()
            copies.append(cp)
        for cp in copies:
            cp.wait()

    vm = pl.BlockSpec(memory_space=pltpu.VMEM)
    return pl.pallas_call(
        body, in_specs=[vm], out_specs=vm, out_shape=_sds((N_DEV,) + v.shape, F32),
        scratch_shapes=[pltpu.SemaphoreType.DMA((7,)), pltpu.SemaphoreType.DMA((7,)), pltpu.SemaphoreType.DMA],
        name=name)(v)


def _adamw_math(w, g, m, v):
    m = ADAM_B1 * m + (1.0 - ADAM_B1) * g
    v = ADAM_B2 * v + (1.0 - ADAM_B2) * (g * g)
    m_hat = m / (1.0 - ADAM_B1 ** ADAM_STEP)
    v_hat = v / (1.0 - ADAM_B2 ** ADAM_STEP)
    delta = -ADAM_LR * (m_hat / (jnp.sqrt(v_hat) + ADAM_EPS) + ADAM_WD * w)
    return delta, m, v


def _adamw_sum8(parts, w, m, v, row_tile, name):
    rows, cols = w.shape

    def body(p_ref, w_ref, m_ref, v_ref, g_ref, d_ref, nm_ref, nv_ref):
        g = p_ref[0].astype(F32)
        for k in range(1, N_DEV):
            g = g + p_ref[k].astype(F32)
        g_ref[...] = g
        d_ref[...], nm_ref[...], nv_ref[...] = _adamw_math(w_ref[...], g, m_ref[...], v_ref[...])

    tile = pl.BlockSpec((row_tile, cols), lambda i: (i, 0))
    return pl.pallas_call(
        body, grid=(rows // row_tile,), in_specs=[pl.BlockSpec((N_DEV, row_tile, cols), lambda i: (0, i, 0))] + [tile] * 3,
        out_specs=[tile] * 4, out_shape=[_sds((rows, cols), F32)] * 4, name=name)(parts, w, m, v)


def _adamw_plain(g, w, m, v, name):
    def body(g_ref, w_ref, m_ref, v_ref, d_ref, nm_ref, nv_ref):
        d_ref[...], nm_ref[...], nv_ref[...] = _adamw_math(w_ref[...], g_ref[...], m_ref[...], v_ref[...])

    return pl.pallas_call(body, out_shape=[_sds(w.shape, F32)] * 3, name=name)(g, w, m, v)


SMALL_ROWS = 8


def _pack_small(pre_mix, post_mix, pre_ffn, post_ffn, attn_out, conv_out, taps):
    zeros = jnp.zeros((1, ATTN_WIDTH), F32)
    return jnp.concatenate([
        pre_mix, post_mix, pre_ffn, post_ffn, jnp.concatenate([attn_out, conv_out], axis=1),
        jnp.concatenate([taps[0:1], taps[1:2]], axis=1), jnp.concatenate([taps[2:3], zeros], axis=1),
        jnp.zeros((1, D_MODEL), F32)], axis=0)


def _local_step(x, target, positions, gains, conv_w_full, weights):
    inv_freq = ROPE_THETA ** (-jnp.arange(0, ROPE_DIM, 2, dtype=F32) / ROPE_DIM)
    lane = jnp.arange(LANES) % HEAD_DIM
    freq_row = jnp.where(lane < ROPE_DIM, inv_freq[lane % (ROPE_DIM // 2)], 0.0).reshape(1, LANES).astype(F32)
    tc, tsa, tsb = _rope_tables(positions.reshape(SEQ, 1), freq_row)

    def gain(name, l):
        return gains[name][l:l + 1]

    saved = []
    h = x
    for l in range(DEPTH):
        w_in, w_out, w_gu, w_dn = weights[l]
        q, k, v, u, b, c = _inproj_fwd(h, gain("pre_mix_norm", l), w_in, tc, tsa, tsb)
        attn, lse = _attn_fwd(q, k, v)
        x1, merged, mix = _mix_fwd(attn, u, b, c, conv_w_full[l], gain("attn_out_norm", l), gain("conv_out_norm", l), w_out, h,
                                   gain("post_mix_norm", l))
        x2, gu, f = _ffn_fwd(x1, gain("pre_ffn_norm", l), w_gu, w_dn, gain("post_ffn_norm", l))
        saved.append((h, q, k, v, u, b, c, attn, lse, merged, mix, x1, gu, f))
        h = x2

    loss, dx = _loss_and_grad(h, target)

    weight_grads, small_grads = [None] * DEPTH, [None] * DEPTH
    for l in reversed(range(DEPTH)):
        w_in, w_out, w_gu, w_dn = weights[l]
        x0, q, k, v, u, b, c, attn, lse, merged, mix, x1, gu, f = saved[l]
        dx1, df, act, dgu, h2, dg_post_ffn, dg_pre_ffn = _ffn_bwd(dx, f, x1, gu, w_gu, w_dn, gain("post_ffn_norm", l),
                                                                  gain("pre_ffn_norm", l))
        g_gu = _wgrad_gate_up(h2, dgu)
        g_dn = _wgrad_down(act, df)
        dmix, dattn, ds, db, dg_post_mix, dg_attn, dg_conv = _mix_bwd(
            dx1, mix, attn, u, b, c, conv_w_full[l], gain("attn_out_norm", l), gain("conv_out_norm", l), gain("post_mix_norm", l), w_out)
        g_out = _wgrad_out(merged, dmix)
        dq, dk, dv = _attn_bwd(q, k, v, dattn, attn, lse)
        dx, dproj, h1, dg_pre_mix, dtaps = _inproj_bwd(dq, dk, dv, ds, db, u, c, conv_w_full[l], tc, tsa, tsb, w_in, x0,
                                                       gain("pre_mix_norm", l), dx1)
        g_in = _wgrad_in(h1, dproj)
        weight_grads[l] = (g_in, g_out, g_gu, g_dn)
        small_grads[l] = _pack_small(dg_pre_mix, dg_post_mix, dg_pre_ffn, dg_post_ffn, dg_attn, dg_conv, dtaps)
    return loss, dx, weight_grads, small_grads


def kernel(x, positions, pre_mix_norm, w_in, conv_w, attn_out_norm, conv_out_norm, w_out, post_mix_norm, pre_ffn_norm, w_gate_up, w_down, post_ffn_norm, loss_target, m_pre_mix_norm, m_w_in, m_conv_w, m_attn_out_norm, m_conv_out_norm, m_w_out, m_post_mix_norm, m_pre_ffn_norm, m_w_gate_up, m_w_down, m_post_ffn_norm, v_pre_mix_norm, v_w_in, v_conv_w, v_attn_out_norm, v_conv_out_norm, v_w_out, v_post_mix_norm, v_pre_ffn_norm, v_w_gate_up, v_w_down, v_post_ffn_norm):
    mx, my, mc = _place()
    me = _block_of(mx, my, mc)
    conv_channels = conv_w.shape[-1]

    taps_flat = jnp.pad(conv_w.reshape(-1), (0, 8 * LANES - conv_w.size)).reshape(8, LANES)
    taps_all = _allgather_small(taps_flat, "allgather_taps").reshape(N_DEV, 8 * LANES)[:, :conv_w.size]
    conv_w_full = taps_all.reshape(N_DEV, DEPTH, 3, conv_channels).transpose(1, 2, 0, 3).reshape(DEPTH, 3, ATTN_WIDTH)

    weights = [_allgather_weights((w_in[l].astype(BF16), w_out[l].astype(BF16), w_gate_up[l].astype(BF16), w_down[l].astype(BF16)))
               for l in range(DEPTH)]
    gains = dict(pre_mix_norm=pre_mix_norm, attn_out_norm=attn_out_norm, conv_out_norm=conv_out_norm, post_mix_norm=post_mix_norm,
                 pre_ffn_norm=pre_ffn_norm, post_ffn_norm=post_ffn_norm)
    loss, dx, weight_grads, small_grads = _local_step(x[0], loss_target[0], positions, gains, conv_w_full, weights)
    loss = lax.psum(loss[0, 0], ("x", "y", "c"))

    big_w = dict(zip(WEIGHT_KINDS, (w_in, w_out, w_gate_up, w_down)))
    big_m = dict(zip(WEIGHT_KINDS, (m_w_in, m_w_out, m_w_gate_up, m_w_down)))
    big_v = dict(zip(WEIGHT_KINDS, (v_w_in, v_w_out, v_w_gate_up, v_w_down)))
    tiles = {"in": 256, "out": 128, "gu": 256, "dn": 176}
    big_out = {kind: [] for kind in WEIGHT_KINDS}
    for l in range(DEPTH):
        parts = _reduce_scatter_send(weight_grads[l])
        for kind, part in zip(WEIGHT_KINDS, parts):
            big_out[kind].append(_adamw_sum8(part, big_w[kind][l], big_m[kind][l], big_v[kind][l], tiles[kind], "adamw_" + kind))
    big = {kind: [jnp.stack([big_out[kind][l][o] for l in range(DEPTH)]) for o in range(4)] for kind in WEIGHT_KINDS}

    packed = jnp.concatenate(small_grads, axis=0)
    gathered = _allgather_small(packed, "allgather_small_grads")

    def pack_state(pre_mix, post_mix, pre_ffn, post_ffn, attn_out, conv_out):
        rows = [_pack_small(pre_mix[l:l + 1], post_mix[l:l + 1], pre_ffn[l:l + 1], post_ffn[l:l + 1], attn_out[l:l + 1], conv_out[l:l + 1],
                            jnp.zeros((3, ATTN_WIDTH), F32)) for l in range(DEPTH)]
        return jnp.concatenate(rows, axis=0)

    sw = pack_state(pre_mix_norm, post_mix_norm, pre_ffn_norm, post_ffn_norm, attn_out_norm, conv_out_norm)
    sm = pack_state(m_pre_mix_norm, m_post_mix_norm, m_pre_ffn_norm, m_post_ffn_norm, m_attn_out_norm, m_conv_out_norm)
    sv = pack_state(v_pre_mix_norm, v_post_mix_norm, v_pre_ffn_norm, v_post_ffn_norm, v_attn_out_norm, v_conv_out_norm)
    sg, sd, snm, snv = _adamw_sum8(gathered, sw, sm, sv, DEPTH * SMALL_ROWS, "adamw_small")

    def unpack(p):
        p = p.reshape(DEPTH, SMALL_ROWS, D_MODEL)
        return dict(pre_mix_norm=p[:, 0], post_mix_norm=p[:, 1], pre_ffn_norm=p[:, 2], post_ffn_norm=p[:, 3],
                    attn_out_norm=p[:, 4, :ATTN_WIDTH], conv_out_norm=p[:, 4, ATTN_WIDTH:])

    small = [unpack(p) for p in (sg, sd, snm, snv)]
    sg3 = sg.reshape(DEPTH, SMALL_ROWS, D_MODEL)
    taps_grad_full = jnp.stack([sg3[:, 5, :ATTN_WIDTH], sg3[:, 5, ATTN_WIDTH:], sg3[:, 6, :ATTN_WIDTH]], axis=1)
    taps_grad = lax.dynamic_slice_in_dim(taps_grad_full, me * conv_channels, conv_channels, axis=2)

    def flat(a):
        return a.reshape(DEPTH * 3, conv_channels)

    td, tnm, tnv = _adamw_plain(flat(taps_grad), flat(conv_w), flat(m_conv_w), flat(v_conv_w), "adamw_taps")
    taps = [taps_grad] + [a.reshape(conv_w.shape) for a in (td, tnm, tnv)]

    def leaves(o):
        s, b = small[o], big
        return (s["pre_mix_norm"], b["in"][o], taps[o], s["attn_out_norm"], s["conv_out_norm"], b["out"][o], s["post_mix_norm"],
                s["pre_ffn_norm"], b["gu"][o], b["dn"][o], s["post_ffn_norm"])

    return (loss, dx[None], *leaves(0), *leaves(1), *leaves(2), *leaves(3))
```

```python
import math

import jax
import jax.numpy as jnp
from jax import lax
from jax.experimental import pallas as pl
from jax.experimental.pallas import tpu as pltpu

F32 = jnp.float32
BF16 = jnp.bfloat16
MESH = pl.DeviceIdType.MESH

SEQ = 4096
D_MODEL = 1024
DEPTH = 4
N_DEV = 8
ATTN_WIDTH = 512
IN_PROJ_WIDTH = 3072
FFN_HIDDEN = 2816
FFN_BLOCK = 2 * FFN_HIDDEN // N_DEV
W_IN_BLOCK = IN_PROJ_WIDTH // N_DEV
W_OUT_BLOCK = D_MODEL // N_DEV
W_DOWN_BLOCK = FFN_HIDDEN // N_DEV
HEAD_DIM = 64
ROPE_DIM = 16
ROPE_THETA = 500000.0
DILATIONS = (1, 4, 16)
HALF_WINDOW = 64
RMS_EPS = 1e-6
NEG_INF = -1e30
LANES = 128
Q_BLOCK = 128
K_WINDOW = Q_BLOCK + 2 * HALF_WINDOW
PERM_CHUNK = 256
ROW_TILE = 512
FFN_TILE = 256
WGRAD_TILE = 1024
ADAM_LR, ADAM_B1, ADAM_B2, ADAM_EPS, ADAM_WD, ADAM_STEP = 0.001, 0.9, 0.999, 1e-08, 0.01, 10
MIB = 1024 * 1024

WEIGHT_KINDS = ("in", "out", "gu", "dn")
FULL_SHAPES = {"in": (D_MODEL, IN_PROJ_WIDTH), "out": (D_MODEL, D_MODEL), "gu": (N_DEV, D_MODEL, FFN_BLOCK), "dn": (FFN_HIDDEN, D_MODEL)}
SHARD_SHAPES = {"in": (D_MODEL, W_IN_BLOCK), "out": (W_OUT_BLOCK, D_MODEL), "gu": (D_MODEL, FFN_BLOCK), "dn": (W_DOWN_BLOCK, D_MODEL)}
ANY = pl.BlockSpec(memory_space=pl.ANY)


def _sds(shape, dtype):
    return jax.ShapeDtypeStruct(shape, dtype)


def _rows(width, tile=ROW_TILE):
    return pl.BlockSpec((tile, width), lambda i: (i, 0))


def _frows(width):
    return _rows(width, FFN_TILE)


def _const(shape):
    return pl.BlockSpec(shape, lambda i: (0,) * len(shape))


def _const1(shape):
    return pl.BlockSpec(shape, lambda i: (0,) * len(shape), pipeline_mode=pl.Buffered(1))


def _halo_prev(width, tile=ROW_TILE):
    return pl.BlockSpec((8, width), lambda i: (jnp.maximum(i * (tile // 8) - 1, 0), 0))


def _halo_next(width, tile=ROW_TILE):
    return pl.BlockSpec((8, width), lambda i: (jnp.minimum((i + 1) * (tile // 8), SEQ // 8 - 1), 0))


def _rms(x):
    r = lax.rsqrt(jnp.mean(x * x, axis=-1, keepdims=True) + RMS_EPS)
    return x * r, r


def _rms_bwd(dn, y, r):
    return r * (dn - y * jnp.mean(dn * y, axis=-1, keepdims=True))


def _dot(a, b):
    return jnp.dot(a, b, preferred_element_type=F32)


def _dot_nt(a, b):
    return lax.dot_general(a, b, (((1,), (1,)), ((), ())), preferred_element_type=F32)


def _dot_tn(a, b):
    return lax.dot_general(a, b, (((0,), (0,)), ((), ())), preferred_element_type=F32)


def _place():
    return lax.axis_index("x"), lax.axis_index("y"), lax.axis_index("c")


def _block_of(px, py, pc):
    return 4 * px + 2 * py + pc


def _weight_block(ref, kind, blk):
    if kind == "in":
        return ref.at[:, pl.ds(blk * W_IN_BLOCK, W_IN_BLOCK)]
    if kind == "out":
        return ref.at[pl.ds(blk * W_OUT_BLOCK, W_OUT_BLOCK), :]
    if kind == "gu":
        return ref.at[blk]
    return ref.at[pl.ds(blk * W_DOWN_BLOCK, W_DOWN_BLOCK), :]


def _dma_semaphores(n):
    return [pltpu.SemaphoreType.DMA((n, 7)), pltpu.SemaphoreType.DMA((n, 7)), pltpu.SemaphoreType.DMA((n,))]


class _Gather:
    def __init__(self, kinds, shards):
        self.kinds, self.operands = tuple(kinds), list(shards)
        self.tag = "gather_" + "_".join(kinds)
        self.out_shape = [_sds(FULL_SHAPES[k], BF16) for k in kinds]
        self.scratch = _dma_semaphores(len(kinds))

    def _parties(self):
        x, y, c = _place()
        return (x, y, c), (x, y, 1 - c), [(1 - x, y), (x, 1 - y), (1 - x, 1 - y)], c

    def _copy(self, outs, sems, w, k, block, to, src=None):
        dst = _weight_block(outs[w], self.kinds[w], _block_of(*block))
        return pltpu.make_async_remote_copy(src_ref=dst if src is None else src, dst_ref=dst, send_sem=sems[0].at[w, k],
                                            recv_sem=sems[1].at[w, k], device_id=to, device_id_type=MESH)

    def _own(self, srcs, outs, sems, w, me):
        return pltpu.make_async_copy(srcs[w], _weight_block(outs[w], self.kinds[w], _block_of(*me)), sems[2].at[w])

    def _first(self, srcs, outs, sems, w):
        me, sibling, chips, c = self._parties()
        return [self._copy(outs, sems, w, 0, me, sibling, src=srcs[w])] + [
            self._copy(outs, sems, w, 1 + j, me, (*chip, c), src=srcs[w]) for j, chip in enumerate(chips)]

    def start(self, srcs, outs, sems):
        me = self._parties()[0]
        for w in range(len(self.kinds)):
            self._own(srcs, outs, sems, w, me).start()
            for cp in self._first(srcs, outs, sems, w):
                cp.start()

    def forward(self, srcs, outs, sems):
        me, sibling, chips, c = self._parties()
        for j, chip in enumerate(chips):
            for w in range(len(self.kinds)):
                self._copy(outs, sems, w, 1 + j, (*chip, c), me).wait_recv()
                self._copy(outs, sems, w, 4 + j, (*chip, c), sibling).start()

    def finish(self, srcs, outs, sems):
        me, sibling, chips, c = self._parties()
        for w in range(len(self.kinds)):
            self._copy(outs, sems, w, 0, sibling, me).wait_recv()
            for j, chip in enumerate(chips):
                self._copy(outs, sems, w, 4 + j, (*chip, 1 - c), me).wait_recv()
        for w in range(len(self.kinds)):
            for cp in self._first(srcs, outs, sems, w):
                cp.wait_send()
            for j, chip in enumerate(chips):
                self._copy(outs, sems, w, 4 + j, (*chip, c), sibling).wait_send()
            self._own(srcs, outs, sems, w, me).wait()


def _peers(x, y, c):
    return [(x ^ a, y ^ b, c ^ e) for a in (0, 1) for b in (0, 1) for e in (0, 1) if (a, b, e) != (0, 0, 0)]


class _Scatter:
    def __init__(self, kinds, grads):
        self.kinds, self.operands = tuple(kinds), list(grads)
        self.tag = "scatter_" + "_".join(kinds)
        self.out_shape = [_sds((N_DEV,) + SHARD_SHAPES[k], BF16) for k in kinds]
        self.scratch = _dma_semaphores(len(kinds))

    def _copies(self, srcs, outs, sems):
        x, y, c = _place()
        me = _block_of(x, y, c)
        copies = []
        for w, kind in enumerate(self.kinds):
            copies.append(pltpu.make_async_copy(_weight_block(srcs[w], kind, me), outs[w].at[me], sems[2].at[w]))
            for k, peer in enumerate(_peers(x, y, c)):
                copies.append(pltpu.make_async_remote_copy(
                    src_ref=_weight_block(srcs[w], kind, _block_of(*peer)), dst_ref=outs[w].at[me],
                    send_sem=sems[0].at[w, k], recv_sem=sems[1].at[w, k], device_id=peer, device_id_type=MESH))
        return copies

    def start(self, srcs, outs, sems):
        for cp in self._copies(srcs, outs, sems):
            cp.start()

    def forward(self, srcs, outs, sems):
        pass

    def finish(self, srcs, outs, sems):
        for cp in self._copies(srcs, outs, sems):
            cp.wait()


def _call(body, args, *, grid, in_specs, out_specs, out_shape, scratch_shapes=(), vmem_mib=None, name, comm=None):
    kwargs = {} if vmem_mib is None else dict(compiler_params=pltpu.CompilerParams(vmem_limit_bytes=vmem_mib * MIB))
    in_specs, out_specs, out_shape, scratch_shapes = list(in_specs), list(out_specs), list(out_shape), list(scratch_shapes)
    if comm is None:
        res = pl.pallas_call(body, grid=grid, in_specs=in_specs, out_specs=out_specs, out_shape=out_shape,
                             scratch_shapes=scratch_shapes, name=name, **kwargs)(*args)
        return list(res), None
    n_in, n_out, n_scr = len(in_specs), len(out_specs), len(scratch_shapes)
    c_in, c_out = len(comm.operands), len(comm.out_shape)
    last = math.prod(grid) - 1

    def carried(*refs):
        cuts = [n_in, c_in, n_out, c_out, n_scr]
        parts, at = [], 0
        for n in cuts:
            parts.append(refs[at:at + n])
            at += n
        ins, c_ins, outs, c_outs, scr = parts
        sems = refs[at:]
        step = pl.program_id(0)
        for axis in range(1, len(grid)):
            step = step * grid[axis] + pl.program_id(axis)

        @pl.when(step == 0)
        def _():
            comm.start(c_ins, c_outs, sems)

        @pl.when(step == last)
        def _():
            comm.forward(c_ins, c_outs, sems)

        body(*ins, *outs, *scr)

        @pl.when(step == last)
        def _():
            comm.finish(c_ins, c_outs, sems)

    res = pl.pallas_call(carried, grid=grid, in_specs=in_specs + [ANY] * c_in, out_specs=out_specs + [ANY] * c_out,
                         out_shape=out_shape + comm.out_shape, scratch_shapes=scratch_shapes + comm.scratch,
                         name=name + "_" + comm.tag, **kwargs)(*args, *comm.operands)
    return list(res[:n_out]), list(res[n_out:])


def _comm_only(comm, name):
    def body(*refs):
        n_in, n_out = len(comm.operands), len(comm.out_shape)
        srcs, outs, sems = refs[:n_in], refs[n_in:n_in + n_out], refs[n_in + n_out:]
        comm.start(srcs, outs, sems)
        comm.forward(srcs, outs, sems)
        comm.finish(srcs, outs, sems)

    return pl.pallas_call(body, in_specs=[ANY] * len(comm.operands), out_specs=[ANY] * len(comm.out_shape), out_shape=comm.out_shape,
                          scratch_shapes=comm.scratch, name=name)(*comm.operands)


def _allgather_small(v, name):
    def body(v_ref, o_ref, send, recv, local):
        x, y, c = _place()
        me = _block_of(x, y, c)
        mine = pltpu.make_async_copy(v_ref, o_ref.at[me], local)
        mine.start()
        copies = [mine]
        for k, peer in enumerate(_peers(x, y, c)):
            cp = pltpu.make_async_remote_copy(src_ref=v_ref, dst_ref=o_ref.at[me], send_sem=send.at[k], recv_sem=recv.at[k],
                                              device_id=peer, device_id_type=MESH)
            cp.start()
            copies.append(cp)
        for cp in copies:
            cp.wait()

    vm = pl.BlockSpec(memory_space=pltpu.VMEM)
    return pl.pallas_call(
        body, in_specs=[vm], out_specs=vm, out_shape=_sds((N_DEV,) + v.shape, F32),
        scratch_shapes=[pltpu.SemaphoreType.DMA((7,)), pltpu.SemaphoreType.DMA((7,)), pltpu.SemaphoreType.DMA],
        name=name)(v)


def _rope_tables(pos_col, freq_row):
    def body(p_ref, f_ref, c_ref, sa_ref, sb_ref):
        ang = p_ref[...].astype(F32) * f_ref[...]
        lane = lax.broadcasted_iota(jnp.int32, ang.shape, 1) % HEAD_DIM
        cos, sin = jnp.cos(ang), jnp.sin(ang)
        c_ref[...] = jnp.where(lane < ROPE_DIM, cos, 1.0)
        sa_ref[...] = jnp.where(lane < ROPE_DIM // 2, -sin, 0.0)
        sb_ref[...] = jnp.where((lane >= ROPE_DIM // 2) & (lane < ROPE_DIM), sin, 0.0)

    return pl.pallas_call(
        body, grid=(SEQ // ROW_TILE,), in_specs=[_rows(1), _const((1, LANES))], out_specs=[_rows(LANES)] * 3,
        out_shape=[_sds((SEQ, LANES), F32)] * 3, name="rope_tables")(pos_col, freq_row)


def _rotate(t, c, sa, sb):
    parts = []
    for g in range(ATTN_WIDTH // LANES):
        tg = t[:, g * LANES:(g + 1) * LANES]
        parts.append(tg * c + pltpu.roll(tg, LANES - 8, axis=1) * sa + pltpu.roll(tg, 8, axis=1) * sb)
    return jnp.concatenate(parts, axis=1)


def _rotate_transposed(dt, c, sa, sb):
    parts = []
    for g in range(ATTN_WIDTH // LANES):
        dg = dt[:, g * LANES:(g + 1) * LANES]
        parts.append(dg * c + pltpu.roll(dg * sa, 8, axis=1) + pltpu.roll(dg * sb, LANES - 8, axis=1))
    return jnp.concatenate(parts, axis=1)


def _inproj_fwd(x, g_pre, w_in, tc, tsa, tsb):
    def body(x_ref, g_ref, w_ref, c_ref, sa_ref, sb_ref, q_ref, k_ref, v_ref, u_ref, b_ref, cc_ref):
        y, _ = _rms(x_ref[...])
        h = (y * g_ref[...]).astype(BF16)

        def proj(n):
            return _dot(h, w_ref[:, n * ATTN_WIDTH:(n + 1) * ATTN_WIDTH])

        c, sa, sb = c_ref[...], sa_ref[...], sb_ref[...]
        q_ref[...] = (_rotate(proj(0), c, sa, sb) * (HEAD_DIM ** -0.5)).astype(BF16)
        k_ref[...] = _rotate(proj(1), c, sa, sb).astype(BF16)
        v_ref[...] = proj(2).astype(BF16)
        u_ref[...] = proj(3)
        b_ref[...] = proj(4)
        cc_ref[...] = proj(5)

    a = ATTN_WIDTH
    return _call(
        body, (x, g_pre, w_in, tc, tsa, tsb), grid=(SEQ // ROW_TILE,),
        in_specs=[_rows(D_MODEL), _const((1, D_MODEL)), _const1((D_MODEL, IN_PROJ_WIDTH)), _rows(LANES), _rows(LANES), _rows(LANES)],
        out_specs=[_rows(a)] * 6, out_shape=[_sds((SEQ, a), BF16)] * 3 + [_sds((SEQ, a), F32)] * 3,
        vmem_mib=40, name="inproj_fwd")[0]


def _head_masks():
    lane = lax.broadcasted_iota(jnp.int32, (1, LANES), 1)
    first = lane < HEAD_DIM
    return first, first.astype(F32), 1.0 - first.astype(F32)


def _perm_chunks(dil):
    length = SEQ // dil
    out = []
    for r in range(dil):
        for c0 in range(0, length, PERM_CHUNK):
            chunk = (r * length + c0) // PERM_CHUNK
            rows = pl.ds(c0, PERM_CHUNK) if dil == 1 else pl.ds(r + dil * c0, PERM_CHUNK, stride=dil)
            out.append((chunk, rows))
    return out


def _chunk(c, offset=0):
    return pl.ds(offset + c * PERM_CHUNK, PERM_CHUNK)


def _write_band_bias(bias_ref):
    rr = lax.broadcasted_iota(jnp.int32, (Q_BLOCK, K_WINDOW), 0)
    cc = lax.broadcasted_iota(jnp.int32, (Q_BLOCK, K_WINDOW), 1)
    band = (cc >= rr) & (cc - rr <= 2 * HALF_WINDOW)
    bias_ref[0] = jnp.where(band, 0.0, NEG_INF)
    bias_ref[1] = jnp.where(band & (cc >= HALF_WINDOW), 0.0, NEG_INF)
    bias_ref[2] = jnp.where(band & (cc < Q_BLOCK + HALF_WINDOW), 0.0, NEG_INF)


def _band_bias_index(m0, length):
    return jnp.where(m0 % length == 0, 1, 0) + jnp.where((m0 + Q_BLOCK) % length == 0, 2, 0)


def _zero_key_padding(bufs):
    pad = jnp.zeros((HALF_WINDOW, LANES), BF16)
    for buf in bufs:
        buf[pl.ds(0, HALF_WINDOW), :] = pad
        buf[pl.ds(SEQ + HALF_WINDOW, HALF_WINDOW), :] = pad


def _attn_fwd(q, k, v, comm=None):
    group = 4

    def body(q_ref, k_ref, v_ref, o_ref, lse_ref, q32, k32, v32, qp, ka, kb, va, vb, accp, mp, lp, z_ref, bias_ref):
        first, mask_a, mask_b = _head_masks()
        _zero_key_padding((ka, kb, va, vb))
        _write_band_bias(bias_ref)
        q32[...] = q_ref[...].astype(F32)
        k32[...] = k_ref[...].astype(F32)
        v32[...] = v_ref[...].astype(F32)

        for branch, dil in enumerate(DILATIONS):
            length = SEQ // dil
            assert length >= 2 * Q_BLOCK
            chunks = _perm_chunks(dil)
            for c, rows in chunks:
                qp[_chunk(c), :] = q32[rows, :].astype(BF16)
                for src, da, db in ((k32, ka, kb), (v32, va, vb)):
                    val = src[rows, :]
                    da[_chunk(c, HALF_WINDOW), :] = (val * mask_a).astype(BF16)
                    db[_chunk(c, HALF_WINDOW), :] = (val * mask_b).astype(BF16)

            def blocks(i, carry):
                base = pl.multiple_of(i * (group * Q_BLOCK), group * Q_BLOCK)
                starts = [base + g * Q_BLOCK for g in range(group)]
                scores = [[_dot_nt(qp[pl.ds(m0, Q_BLOCK), :], kx[pl.ds(m0, K_WINDOW), :]) for kx in (ka, kb)] for m0 in starts]
                probs = []
                for m0, pair in zip(starts, scores):
                    bias = bias_ref[_band_bias_index(m0, length)]
                    ms, ls, ps = [], [], []
                    for s in pair:
                        s = s + bias
                        m = jnp.max(s, axis=1, keepdims=True)
                        p = jnp.exp(s - m)
                        ls.append(jnp.sum(p, axis=1, keepdims=True))
                        ms.append(m)
                        ps.append(p.astype(BF16))
                    mp[pl.ds(m0, Q_BLOCK), :] = jnp.where(first, ms[0], ms[1])
                    lp[pl.ds(m0, Q_BLOCK), :] = jnp.where(first, ls[0], ls[1])
                    probs.append(ps)
                for m0, ps in zip(starts, probs):
                    accp[pl.ds(m0, Q_BLOCK), :] = _dot(ps[0], va[pl.ds(m0, K_WINDOW), :]) + _dot(ps[1], vb[pl.ds(m0, K_WINDOW), :])
                return carry

            lax.fori_loop(0, SEQ // (group * Q_BLOCK), blocks, 0)

            for c, rows in chunks:
                a, m, l = accp[_chunk(c), :], mp[_chunk(c), :], lp[_chunk(c), :]
                if branch == 0:
                    o_ref[rows, :] = a
                    lse_ref[rows, :] = m
                    z_ref[rows, :] = l
                else:
                    m_old = lse_ref[rows, :]
                    m_new = jnp.maximum(m_old, m)
                    e_old, e_new = jnp.exp(m_old - m_new), jnp.exp(m - m_new)
                    o_ref[rows, :] = o_ref[rows, :] * e_old + a * e_new
                    z_ref[rows, :] = z_ref[rows, :] * e_old + l * e_new
                    lse_ref[rows, :] = m_new

        for c in range(SEQ // PERM_CHUNK):
            z = z_ref[_chunk(c), :]
            o_ref[_chunk(c), :] = o_ref[_chunk(c), :] / z
            lse_ref[_chunk(c), :] = lse_ref[_chunk(c), :] + jnp.log(z)

    col = pl.BlockSpec((SEQ, LANES), lambda h: (0, h))
    padded = SEQ + 2 * HALF_WINDOW
    return _call(
        body, (q, k, v), grid=(ATTN_WIDTH // LANES,), in_specs=[col] * 3, out_specs=[col] * 2,
        out_shape=[_sds((SEQ, ATTN_WIDTH), F32)] * 2,
        scratch_shapes=[pltpu.VMEM((SEQ, LANES), F32)] * 3 + [pltpu.VMEM((SEQ, LANES), BF16)]
        + [pltpu.VMEM((padded, LANES), BF16)] * 4 + [pltpu.VMEM((SEQ, LANES), F32)] * 4
        + [pltpu.VMEM((3, Q_BLOCK, K_WINDOW), F32)],
        vmem_mib=48, name="attn_fwd", comm=comm)


def _shifted(t, before, after, i):
    tile = t.shape[0]
    row = lax.broadcasted_iota(jnp.int32, (tile, 1), 0)
    before = jnp.where(i > 0, before, 0.0)
    after = jnp.where(i < SEQ // tile - 1, after, 0.0)
    return (jnp.where(row == 0, before, pltpu.roll(t, 1, axis=0)),
            jnp.where(row == tile - 1, after, pltpu.roll(t, tile - 1, axis=0)))


def _conv_parts(u, c, u_prev, c_prev, u_next, c_next, cw, i):
    t = c * u
    t_prev, t_next = _shifted(t, c_prev[7:8, :] * u_prev[7:8, :], c_next[0:1, :] * u_next[0:1, :], i)
    s = cw[0:1, :] * t_prev + cw[1:2, :] * t + cw[2:3, :] * t_next
    return t, t_prev, t_next, s


def _mix_fwd(attn, u, b, c, conv_w, g_attn, g_conv, w_out, x, g_post):
    def body(a_ref, u_ref, b_ref, c_ref, up_ref, cp_ref, un_ref, cn_ref, cw_ref, ga_ref, gc_ref, w_ref, x_ref, gp_ref,
             x1_ref, mg_ref, mix_ref):
        i = pl.program_id(0)
        _, _, _, s = _conv_parts(u_ref[...], c_ref[...], up_ref[...], cp_ref[...], un_ref[...], cn_ref[...], cw_ref[...], i)
        ya, _ = _rms(a_ref[...])
        yc, _ = _rms(b_ref[...] * s)
        merged = jnp.concatenate([ya * ga_ref[...], yc * gc_ref[...]], axis=1).astype(BF16)
        mix = _dot(merged, w_ref[...])
        ym, _ = _rms(mix)
        mg_ref[...] = merged
        mix_ref[...] = mix
        x1_ref[...] = x_ref[...] + ym * gp_ref[...]

    a = ATTN_WIDTH
    return _call(
        body, (attn, u, b, c, u, c, u, c, conv_w, g_attn, g_conv, w_out, x, g_post), grid=(SEQ // ROW_TILE,),
        in_specs=[_rows(a)] * 4 + [_halo_prev(a)] * 2 + [_halo_next(a)] * 2
        + [_const((3, a)), _const((1, a)), _const((1, a)), _const1((D_MODEL, D_MODEL)), _rows(D_MODEL), _const((1, D_MODEL))],
        out_specs=[_rows(D_MODEL)] * 3,
        out_shape=[_sds((SEQ, D_MODEL), F32), _sds((SEQ, D_MODEL), BF16), _sds((SEQ, D_MODEL), F32)],
        vmem_mib=40, name="mix_fwd")[0]


def _gu_spec():
    return pl.BlockSpec((N_DEV, FFN_TILE, FFN_BLOCK), lambda i: (0, i, 0))


def _ffn_fwd(x1, g_pre, w_gu, w_dn, g_post, comm=None):
    def body(x_ref, g_ref, wgu_ref, wdn_ref, gp_ref, x2_ref, gu_ref, f_ref):
        x1 = x_ref[...]
        y, _ = _rms(x1)
        h = (y * g_ref[...]).astype(BF16)
        f = jnp.zeros((FFN_TILE, D_MODEL), F32)
        for j in range(N_DEV // 2):
            gate = _dot(h, wgu_ref[j])
            up = _dot(h, wgu_ref[j + N_DEV // 2])
            gu_ref[j] = gate.astype(BF16)
            gu_ref[j + N_DEV // 2] = up.astype(BF16)
            act = (gate * jax.nn.sigmoid(gate) * up).astype(BF16)
            f = f + _dot(act, wdn_ref[pl.ds(j * FFN_BLOCK, FFN_BLOCK), :])
        yf, _ = _rms(f)
        f_ref[...] = f
        x2_ref[...] = x1 + yf * gp_ref[...]

    return _call(
        body, (x1, g_pre, w_gu, w_dn, g_post), grid=(SEQ // FFN_TILE,),
        in_specs=[_frows(D_MODEL), _const((1, D_MODEL)), _const1((N_DEV, D_MODEL, FFN_BLOCK)), _const1((FFN_HIDDEN, D_MODEL)),
                  _const((1, D_MODEL))],
        out_specs=[_frows(D_MODEL), _gu_spec(), _frows(D_MODEL)],
        out_shape=[_sds((SEQ, D_MODEL), F32), _sds((N_DEV, SEQ, FFN_BLOCK), BF16), _sds((SEQ, D_MODEL), F32)],
        vmem_mib=48, name="ffn_fwd", comm=comm)


def _loss_and_grad(y, target):
    n_tiles = SEQ // ROW_TILE

    def body(y_ref, t_ref, loss_ref, dy_ref, acc):
        i = pl.program_id(0)
        err = y_ref[...] - t_ref[...]
        dy_ref[...] = err * (1.0 / D_MODEL)

        @pl.when(i == 0)
        def _():
            acc[...] = jnp.zeros_like(acc)

        acc[...] += jnp.sum(err * err, axis=0, keepdims=True)

        @pl.when(i == n_tiles - 1)
        def _():
            loss_ref[...] = jnp.sum(acc[...], axis=1, keepdims=True) * (0.5 / D_MODEL)

    return _call(
        body, (y, target), grid=(n_tiles,), in_specs=[_rows(D_MODEL)] * 2, out_specs=[_const((1, 1)), _rows(D_MODEL)],
        out_shape=[_sds((1, 1), F32), _sds((SEQ, D_MODEL), F32)], scratch_shapes=[pltpu.VMEM((1, D_MODEL), F32)],
        name="loss_and_grad")[0]


def _accumulate(ref, value, i):
    @pl.when(i == 0)
    def _():
        ref[...] = value

    @pl.when(i > 0)
    def _():
        ref[...] += value


def _colsum(v):
    return jnp.sum(v, axis=0, keepdims=True)


def _ffn_bwd(dx2, f, x1, gu, w_gu, w_dn, g_post, g_pre, comm=None):
    half = N_DEV // 2

    def body(dx2_ref, f_ref, x1_ref, gu_ref, wgu_ref, wdn_ref, gpost_ref, gpre_ref,
             dx1_ref, df_ref, act_ref, dgu_ref, h_ref, dgpost_ref, dgpre_ref):
        i = pl.program_id(0)
        dx2 = dx2_ref[...]
        yf, rf = _rms(f_ref[...])
        _accumulate(dgpost_ref, _colsum(dx2 * yf), i)
        df = _rms_bwd(dx2 * gpost_ref[...], yf, rf).astype(BF16)
        df_ref[...] = df
        dh = jnp.zeros((FFN_TILE, D_MODEL), F32)
        for j in range(half):
            dact = _dot_nt(df, wdn_ref[pl.ds(j * FFN_BLOCK, FFN_BLOCK), :])
            gate = gu_ref[j].astype(F32)
            up = gu_ref[j + half].astype(F32)
            sig = jax.nn.sigmoid(gate)
            silu = gate * sig
            act_ref[j] = (silu * up).astype(BF16)
            dgate = (dact * up * (sig * (1.0 + gate * (1.0 - sig)))).astype(BF16)
            dup = (dact * silu).astype(BF16)
            dgu_ref[j] = dgate
            dgu_ref[j + half] = dup
            dh = dh + _dot_nt(dgate, wgu_ref[j]) + _dot_nt(dup, wgu_ref[j + half])
        y1, r1 = _rms(x1_ref[...])
        h_ref[...] = (y1 * gpre_ref[...]).astype(BF16)
        _accumulate(dgpre_ref, _colsum(dh * y1), i)
        dx1_ref[...] = dx2 + _rms_bwd(dh * gpre_ref[...], y1, r1)

    act_spec = pl.BlockSpec((half, FFN_TILE, FFN_BLOCK), lambda i: (0, i, 0))
    return _call(
        body, (dx2, f, x1, gu, w_gu, w_dn, g_post, g_pre), grid=(SEQ // FFN_TILE,),
        in_specs=[_frows(D_MODEL)] * 3 + [_gu_spec(), _const1((N_DEV, D_MODEL, FFN_BLOCK)), _const1((FFN_HIDDEN, D_MODEL)),
                                          _const((1, D_MODEL)), _const((1, D_MODEL))],
        out_specs=[_frows(D_MODEL), _frows(D_MODEL), act_spec, _gu_spec(), _frows(D_MODEL), _const((1, D_MODEL)), _const((1, D_MODEL))],
        out_shape=[_sds((SEQ, D_MODEL), F32), _sds((SEQ, D_MODEL), BF16), _sds((half, SEQ, FFN_BLOCK), BF16),
                   _sds((N_DEV, SEQ, FFN_BLOCK), BF16), _sds((SEQ, D_MODEL), BF16), _sds((1, D_MODEL), F32), _sds((1, D_MODEL), F32)],
        vmem_mib=52, name="ffn_bwd", comm=comm)


def _wgrad(a, g, a_spec, g_spec, out_shape, out_spec, acc_shape, n_blocks, name):
    n_tiles = SEQ // WGRAD_TILE

    def body(a_ref, g_ref, o_ref, acc):
        s = pl.program_id(1)
        av = a_ref[...].reshape(WGRAD_TILE, a_ref.shape[-1])
        gv = g_ref[...].reshape(WGRAD_TILE, g_ref.shape[-1])
        part = _dot_tn(av, gv)

        @pl.when(s == 0)
        def _():
            acc[...] = part

        @pl.when(s > 0)
        def _():
            acc[...] += part

        @pl.when(s == n_tiles - 1)
        def _():
            o_ref[...] = acc[...].astype(BF16).reshape(o_ref.shape)

    return _call(body, (a, g), grid=(n_blocks, n_tiles), in_specs=[a_spec, g_spec], out_specs=[out_spec],
                 out_shape=[_sds(out_shape, BF16)], scratch_shapes=[pltpu.VMEM(acc_shape, F32)], vmem_mib=40, name=name)[0][0]


def _wgrad_gate_up(h, dgu):
    t = WGRAD_TILE
    return _wgrad(h, dgu, pl.BlockSpec((t, D_MODEL), lambda j, s: (s, 0)), pl.BlockSpec((1, t, FFN_BLOCK), lambda j, s: (j, s, 0)),
                  (N_DEV, D_MODEL, FFN_BLOCK), pl.BlockSpec((1, D_MODEL, FFN_BLOCK), lambda j, s: (j, 0, 0)),
                  (D_MODEL, FFN_BLOCK), N_DEV, "wgrad_gate_up")


def _wgrad_down(act, df):
    t = WGRAD_TILE
    return _wgrad(act, df, pl.BlockSpec((1, t, FFN_BLOCK), lambda j, s: (j, s, 0)), pl.BlockSpec((t, D_MODEL), lambda j, s: (s, 0)),
                  (FFN_HIDDEN, D_MODEL), pl.BlockSpec((FFN_BLOCK, D_MODEL), lambda j, s: (j, 0)),
                  (FFN_BLOCK, D_MODEL), N_DEV // 2, "wgrad_down")


def _wgrad_out(merged, dmix):
    t = WGRAD_TILE
    return _wgrad(merged, dmix, pl.BlockSpec((t, D_MODEL), lambda j, s: (s, 0)), pl.BlockSpec((t, 512), lambda j, s: (s, j)),
                  (D_MODEL, D_MODEL), pl.BlockSpec((D_MODEL, 512), lambda j, s: (0, j)), (D_MODEL, 512), 2, "wgrad_out")


def _wgrad_in(h, dproj):
    t = WGRAD_TILE
    return _wgrad(h, dproj, pl.BlockSpec((t, D_MODEL), lambda j, s: (s, 0)), pl.BlockSpec((t, 512), lambda j, s: (s, j)),
                  (D_MODEL, IN_PROJ_WIDTH), pl.BlockSpec((D_MODEL, 512), lambda j, s: (0, j)), (D_MODEL, 512),
                  IN_PROJ_WIDTH // 512, "wgrad_in")


def _mix_bwd(dx1, mix, attn, u, b, c, conv_w, g_attn, g_conv, g_post, w_out):
    def body(dx1_ref, mix_ref, a_ref, u_ref, b_ref, c_ref, up_ref, cp_ref, un_ref, cn_ref, cw_ref, ga_ref, gc_ref, gp_ref, w_ref,
             dmix_ref, da_ref, ds_ref, db_ref, dgp_ref, dga_ref, dgc_ref):
        i = pl.program_id(0)
        dx1 = dx1_ref[...]
        ym, rm = _rms(mix_ref[...])
        _accumulate(dgp_ref, _colsum(dx1 * ym), i)
        dmix = _rms_bwd(dx1 * gp_ref[...], ym, rm).astype(BF16)
        dmix_ref[...] = dmix
        dmerged = _dot_nt(dmix, w_ref[...])
        dna, dnc = dmerged[:, :ATTN_WIDTH], dmerged[:, ATTN_WIDTH:]
        ya, ra = _rms(a_ref[...])
        _accumulate(dga_ref, _colsum(dna * ya), i)
        da_ref[...] = _rms_bwd(dna * ga_ref[...], ya, ra)
        _, _, _, s = _conv_parts(u_ref[...], c_ref[...], up_ref[...], cp_ref[...], un_ref[...], cn_ref[...], cw_ref[...], i)
        gate_b = b_ref[...]
        yc, rc = _rms(gate_b * s)
        _accumulate(dgc_ref, _colsum(dnc * yc), i)
        dy = _rms_bwd(dnc * gc_ref[...], yc, rc)
        db_ref[...] = dy * s
        ds_ref[...] = dy * gate_b

    a = ATTN_WIDTH
    return _call(
        body, (dx1, mix, attn, u, b, c, u, c, u, c, conv_w, g_attn, g_conv, g_post, w_out), grid=(SEQ // ROW_TILE,),
        in_specs=[_rows(D_MODEL)] * 2 + [_rows(a)] * 4 + [_halo_prev(a)] * 2 + [_halo_next(a)] * 2
        + [_const((3, a)), _const((1, a)), _const((1, a)), _const((1, D_MODEL)), _const1((D_MODEL, D_MODEL))],
        out_specs=[_rows(D_MODEL)] + [_rows(a)] * 3 + [_const((1, D_MODEL)), _const((1, a)), _const((1, a))],
        out_shape=[_sds((SEQ, D_MODEL), BF16)] + [_sds((SEQ, a), F32)] * 3 + [_sds((1, D_MODEL), F32), _sds((1, a), F32), _sds((1, a), F32)],
        vmem_mib=40, name="mix_bwd")[0]


def _attn_bwd(q, k, v, do, o, lse, comm=None):
    group = 2

    def body(q_ref, k_ref, v_ref, do_ref, o_ref, lse_ref, dq_ref, dk_ref, dv_ref,
             st32, qa, qb, doa, dob, ka, kb, va, vb, lsep, dlp, dnat, dqp, dkp, dvp, bias_ref):
        first, mask_a, mask_b = _head_masks()
        _zero_key_padding((ka, kb, va, vb))
        _write_band_bias(bias_ref)
        for c in range(SEQ // PERM_CHUNK):
            prod = do_ref[_chunk(c), :] * o_ref[_chunk(c), :]
            d_a = jnp.sum(prod * mask_a, axis=1, keepdims=True)
            d_b = jnp.sum(prod * mask_b, axis=1, keepdims=True)
            dnat[_chunk(c), :] = jnp.where(first, d_a, d_b)

        for branch, dil in enumerate(DILATIONS):
            length = SEQ // dil
            assert length >= 2 * Q_BLOCK
            chunks = _perm_chunks(dil)
            for src, da, db, off in ((q_ref, qa, qb, 0), (k_ref, ka, kb, HALF_WINDOW), (v_ref, va, vb, HALF_WINDOW)):
                st32[...] = src[...].astype(F32)
                for c, rows in chunks:
                    val = st32[rows, :]
                    da[_chunk(c, off), :] = (val * mask_a).astype(BF16)
                    db[_chunk(c, off), :] = (val * mask_b).astype(BF16)
            for c, rows in chunks:
                val = do_ref[rows, :]
                doa[_chunk(c), :] = (val * mask_a).astype(BF16)
                dob[_chunk(c), :] = (val * mask_b).astype(BF16)
                lsep[_chunk(c), :] = lse_ref[rows, :]
                dlp[_chunk(c), :] = dnat[rows, :]
            zero = jnp.zeros((PERM_CHUNK, LANES), F32)
            for c in range(SEQ // PERM_CHUNK):
                dkp[_chunk(c), :] = zero
                dvp[_chunk(c), :] = zero
            dkp[pl.ds(SEQ, 2 * HALF_WINDOW), :] = zero[:2 * HALF_WINDOW]
            dvp[pl.ds(SEQ, 2 * HALF_WINDOW), :] = zero[:2 * HALF_WINDOW]

            heads = ((qa, doa, ka, va, 0), (qb, dob, kb, vb, HEAD_DIM))

            def blocks(i, carry):
                base = pl.multiple_of(i * (group * Q_BLOCK), group * Q_BLOCK)
                starts = [base + g * Q_BLOCK for g in range(group)]
                raw = [[(_dot_nt(qx[pl.ds(m0, Q_BLOCK), :], kx[pl.ds(m0, K_WINDOW), :]),
                         _dot_nt(dox[pl.ds(m0, Q_BLOCK), :], vx[pl.ds(m0, K_WINDOW), :])) for qx, dox, kx, vx, _ in heads]
                       for m0 in starts]
                grads = []
                for m0, pair in zip(starts, raw):
                    bias = bias_ref[_band_bias_index(m0, length)]
                    lse_b, d_b = lsep[pl.ds(m0, Q_BLOCK), :], dlp[pl.ds(m0, Q_BLOCK), :]
                    out = []
                    for (s, dp), (_, _, _, _, col) in zip(pair, heads):
                        p = jnp.exp(s + bias - lse_b[:, col:col + 1])
                        out.append(((p * (dp - d_b[:, col:col + 1])).astype(BF16), p.astype(BF16)))
                    grads.append(out)
                for m0, out in zip(starts, grads):
                    qrows, krows = pl.ds(m0, Q_BLOCK), pl.ds(m0, K_WINDOW)
                    dq = jnp.zeros((Q_BLOCK, LANES), F32)
                    dk = jnp.zeros((K_WINDOW, LANES), F32)
                    dv = jnp.zeros((K_WINDOW, LANES), F32)
                    for (ds, p), (qx, dox, kx, _, _) in zip(out, heads):
                        dq = dq + _dot(ds, kx[krows, :])
                        dk = dk + _dot_tn(ds, qx[qrows, :])
                        dv = dv + _dot_tn(p, dox[qrows, :])
                    dqp[qrows, :] = dq * (HEAD_DIM ** -0.5)
                    dkp[krows, :] += dk
                    dvp[krows, :] += dv
                return carry

            lax.fori_loop(0, SEQ // (group * Q_BLOCK), blocks, 0)

            for c, rows in chunks:
                g_q, g_k, g_v = dqp[_chunk(c), :], dkp[_chunk(c, HALF_WINDOW), :], dvp[_chunk(c, HALF_WINDOW), :]
                if branch == 0:
                    dq_ref[rows, :] = g_q
                    dk_ref[rows, :] = g_k
                    dv_ref[rows, :] = g_v
                else:
                    dq_ref[rows, :] = dq_ref[rows, :] + g_q
                    dk_ref[rows, :] = dk_ref[rows, :] + g_k
                    dv_ref[rows, :] = dv_ref[rows, :] + g_v

    col = pl.BlockSpec((SEQ, LANES), lambda h: (0, h))
    col1 = pl.BlockSpec((SEQ, LANES), lambda h: (0, h), pipeline_mode=pl.Buffered(1))
    padded = SEQ + 2 * HALF_WINDOW
    return _call(
        body, (q, k, v, do, o, lse), grid=(ATTN_WIDTH // LANES,), in_specs=[col1] * 6, out_specs=[col] * 3,
        out_shape=[_sds((SEQ, ATTN_WIDTH), F32)] * 3,
        scratch_shapes=[pltpu.VMEM((SEQ, LANES), F32)] + [pltpu.VMEM((SEQ, LANES), BF16)] * 4
        + [pltpu.VMEM((padded, LANES), BF16)] * 4 + [pltpu.VMEM((SEQ, LANES), F32)] * 4 + [pltpu.VMEM((padded, LANES), F32)] * 2
        + [pltpu.VMEM((3, Q_BLOCK, K_WINDOW), F32)],
        vmem_mib=56, name="attn_bwd", comm=comm)


def _inproj_bwd(dq, dk, dv, ds, db, u, c, conv_w, tc, tsa, tsb, w_in, x, g_pre, dx1):
    def body(dq_ref, dk_ref, dv_ref, ds_ref, db_ref, u_ref, c_ref, dsp_ref, up_ref, cp_ref, dsn_ref, un_ref, cn_ref, cw_ref,
             tc_ref, tsa_ref, tsb_ref, w_ref, x_ref, g_ref, dx1_ref, dx_ref, dproj_ref, h_ref, dg_ref, dcw_ref):
        i = pl.program_id(0)
        cw = cw_ref[...]
        u, c = u_ref[...], c_ref[...]
        t, t_prev, t_next, _ = _conv_parts(u, c, up_ref[...], cp_ref[...], un_ref[...], cn_ref[...], cw, i)
        ds = ds_ref[...]
        ds_prev, ds_next = _shifted(ds, dsp_ref[7:8, :], dsn_ref[0:1, :], i)
        dt = cw[0:1, :] * ds_next + cw[1:2, :] * ds + cw[2:3, :] * ds_prev
        _accumulate(dcw_ref, jnp.concatenate([_colsum(ds * t_prev), _colsum(ds * t), _colsum(ds * t_next)], axis=0), i)
        tc_, tsa_, tsb_ = tc_ref[...], tsa_ref[...], tsb_ref[...]
        dproj = jnp.concatenate(
            [_rotate_transposed(dq_ref[...], tc_, tsa_, tsb_), _rotate_transposed(dk_ref[...], tc_, tsa_, tsb_), dv_ref[...],
             dt * c, db_ref[...], dt * u], axis=1).astype(BF16)
        dproj_ref[...] = dproj
        dh = _dot_nt(dproj, w_ref[...])
        y, r = _rms(x_ref[...])
        h_ref[...] = (y * g_ref[...]).astype(BF16)
        _accumulate(dg_ref, _colsum(dh * y), i)
        dx_ref[...] = dx1_ref[...] + _rms_bwd(dh * g_ref[...], y, r)

    a = ATTN_WIDTH
    tile = FFN_TILE

    def rows(width):
        return _rows(width, tile)

    return _call(
        body, (dq, dk, dv, ds, db, u, c, ds, u, c, ds, u, c, conv_w, tc, tsa, tsb, w_in, x, g_pre, dx1), grid=(SEQ // tile,),
        in_specs=[rows(a)] * 7 + [_halo_prev(a, tile)] * 3 + [_halo_next(a, tile)] * 3
        + [_const((3, a)), rows(LANES), rows(LANES), rows(LANES), _const1((D_MODEL, IN_PROJ_WIDTH)), rows(D_MODEL),
           _const((1, D_MODEL)), rows(D_MODEL)],
        out_specs=[rows(D_MODEL), rows(IN_PROJ_WIDTH), rows(D_MODEL), _const((1, D_MODEL)), _const((3, a))],
        out_shape=[_sds((SEQ, D_MODEL), F32), _sds((SEQ, IN_PROJ_WIDTH), BF16), _sds((SEQ, D_MODEL), BF16),
                   _sds((1, D_MODEL), F32), _sds((3, a), F32)],
        vmem_mib=48, name="inproj_bwd")[0]


def _adamw_math(w, g, m, v):
    m = ADAM_B1 * m + (1.0 - ADAM_B1) * g
    v = ADAM_B2 * v + (1.0 - ADAM_B2) * (g * g)
    m_hat = m / (1.0 - ADAM_B1 ** ADAM_STEP)
    v_hat = v / (1.0 - ADAM_B2 ** ADAM_STEP)
    delta = -ADAM_LR * (m_hat / (jnp.sqrt(v_hat) + ADAM_EPS) + ADAM_WD * w)
    return delta, m, v


def _sum_parts(p_ref):
    g = p_ref[0].astype(F32)
    for k in range(1, N_DEV):
        g = g + p_ref[k].astype(F32)
    return g


def _adamw_layers(parts, w, m, v, row_tile, name, comm=None):
    _, rows, cols = w.shape
    n_tiles = rows // row_tile

    def body(*refs):
        p_refs = refs[:DEPTH]
        w_ref, m_ref, v_ref, g_ref, d_ref, nm_ref, nv_ref = refs[DEPTH:]
        layer = pl.program_id(0)
        for l, p_ref in enumerate(p_refs):
            @pl.when(layer == l)
            def _(p_ref=p_ref):
                g = _sum_parts(p_ref)
                g_ref[0] = g
                d_ref[0], nm_ref[0], nv_ref[0] = _adamw_math(w_ref[0], g, m_ref[0], v_ref[0])

    def part_spec(l):
        return pl.BlockSpec((N_DEV, row_tile, cols),
                            lambda layer, i: (0, jnp.where(layer == l, i, jnp.where(layer < l, 0, n_tiles - 1)), 0))

    tile = pl.BlockSpec((1, row_tile, cols), lambda layer, i: (layer, i, 0))
    return _call(body, (*parts, w, m, v), grid=(DEPTH, n_tiles), in_specs=[part_spec(l) for l in range(DEPTH)] + [tile] * 3,
                 out_specs=[tile] * 4, out_shape=[_sds(w.shape, F32)] * 4, name=name, comm=comm)


def _adamw_sum8(parts, w, m, v, name):
    def body(p_ref, w_ref, m_ref, v_ref, g_ref, d_ref, nm_ref, nv_ref):
        g = _sum_parts(p_ref)
        g_ref[...] = g
        d_ref[...], nm_ref[...], nv_ref[...] = _adamw_math(w_ref[...], g, m_ref[...], v_ref[...])

    return pl.pallas_call(body, out_shape=[_sds(w.shape, F32)] * 4, name=name)(parts, w, m, v)


def _adamw_plain(g, w, m, v, name):
    def body(g_ref, w_ref, m_ref, v_ref, d_ref, nm_ref, nv_ref):
        d_ref[...], nm_ref[...], nv_ref[...] = _adamw_math(w_ref[...], g_ref[...], m_ref[...], v_ref[...])

    return pl.pallas_call(body, out_shape=[_sds(w.shape, F32)] * 3, name=name)(g, w, m, v)


SMALL_ROWS = 8
GAIN_NAMES = ("pre_mix_norm", "post_mix_norm", "pre_ffn_norm", "post_ffn_norm", "attn_out_norm", "conv_out_norm")


def _pack_small(pre_mix, post_mix, pre_ffn, post_ffn, attn_out, conv_out, taps):
    zeros = jnp.zeros((1, ATTN_WIDTH), F32)
    return jnp.concatenate([
        pre_mix, post_mix, pre_ffn, post_ffn, jnp.concatenate([attn_out, conv_out], axis=1),
        jnp.concatenate([taps[0:1], taps[1:2]], axis=1), jnp.concatenate([taps[2:3], zeros], axis=1),
        jnp.zeros((1, D_MODEL), F32)], axis=0)


def _rope(positions):
    inv_freq = ROPE_THETA ** (-jnp.arange(0, ROPE_DIM, 2, dtype=F32) / ROPE_DIM)
    lane = jnp.arange(LANES) % HEAD_DIM
    freq_row = jnp.where(lane < ROPE_DIM, inv_freq[lane % (ROPE_DIM // 2)], 0.0).reshape(1, LANES).astype(F32)
    return _rope_tables(positions.reshape(SEQ, 1), freq_row)


def _layer_forward(h, gains, taps, tables, w, attn_comm=None, ffn_comm=None):
    q, k, v, u, b, c = _inproj_fwd(h, gains["pre_mix_norm"], w["in"], *tables)
    (attn, lse), landed = _attn_fwd(q, k, v, comm=attn_comm)
    if attn_comm is not None:
        w = {**w, **dict(zip(attn_comm.kinds, landed))}
    x1, merged, mix = _mix_fwd(attn, u, b, c, taps, gains["attn_out_norm"], gains["conv_out_norm"], w["out"], h, gains["post_mix_norm"])
    (x2, gu, f), landed_next = _ffn_fwd(x1, gains["pre_ffn_norm"], w["gu"], w["dn"], gains["post_ffn_norm"], comm=ffn_comm)
    return x2, (h, q, k, v, u, b, c, attn, lse, merged, mix, x1, gu, f), w, landed_next


def _layer_backward(dx, saved, gains, taps, tables, w, ffn_comm=None, scatter=False):
    x0, q, k, v, u, b, c, attn, lse, merged, mix, x1, gu, f = saved
    (dx1, df, act, dgu, h2, dg_post_ffn, dg_pre_ffn), landed_prev = _ffn_bwd(
        dx, f, x1, gu, w["gu"], w["dn"], gains["post_ffn_norm"], gains["pre_ffn_norm"], comm=ffn_comm)
    g_gu = _wgrad_gate_up(h2, dgu)
    g_dn = _wgrad_down(act, df)
    dmix, dattn, ds, db, dg_post_mix, dg_attn, dg_conv = _mix_bwd(
        dx1, mix, attn, u, b, c, taps, gains["attn_out_norm"], gains["conv_out_norm"], gains["post_mix_norm"], w["out"])
    g_out = _wgrad_out(merged, dmix)
    comm = _Scatter(("gu", "dn", "out"), (g_gu, g_dn, g_out)) if scatter else None
    (dq, dk, dv), landed = _attn_bwd(q, k, v, dattn, attn, lse, comm=comm)
    dx0, dproj, h1, dg_pre_mix, dtaps = _inproj_bwd(dq, dk, dv, ds, db, u, c, taps, *tables, w["in"], x0, gains["pre_mix_norm"], dx1)
    g_in = _wgrad_in(h1, dproj)
    small = _pack_small(dg_pre_mix, dg_post_mix, dg_pre_ffn, dg_post_ffn, dg_attn, dg_conv, dtaps)
    rest = dict(zip(comm.kinds, landed)) if scatter else dict(gu=g_gu, dn=g_dn, out=g_out)
    return dx0, g_in, small, landed_prev, rest


def kernel(x, positions, pre_mix_norm, w_in, conv_w, attn_out_norm, conv_out_norm, w_out, post_mix_norm, pre_ffn_norm, w_gate_up, w_down, post_ffn_norm, loss_target, m_pre_mix_norm, m_w_in, m_conv_w, m_attn_out_norm, m_conv_out_norm, m_w_out, m_post_mix_norm, m_pre_ffn_norm, m_w_gate_up, m_w_down, m_post_ffn_norm, v_pre_mix_norm, v_w_in, v_conv_w, v_attn_out_norm, v_conv_out_norm, v_w_out, v_post_mix_norm, v_pre_ffn_norm, v_w_gate_up, v_w_down, v_post_ffn_norm):
    mx, my, mc = _place()
    me = _block_of(mx, my, mc)
    conv_channels = conv_w.shape[-1]

    taps_flat = jnp.pad(conv_w.reshape(-1), (0, 8 * LANES - conv_w.size)).reshape(8, LANES)
    taps_all = _allgather_small(taps_flat, "allgather_taps").reshape(N_DEV, 8 * LANES)[:, :conv_w.size]
    conv_w_full = taps_all.reshape(N_DEV, DEPTH, 3, conv_channels).transpose(1, 2, 0, 3).reshape(DEPTH, 3, ATTN_WIDTH)

    big_w = dict(zip(WEIGHT_KINDS, (w_in, w_out, w_gate_up, w_down)))
    big_m = dict(zip(WEIGHT_KINDS, (m_w_in, m_w_out, m_w_gate_up, m_w_down)))
    big_v = dict(zip(WEIGHT_KINDS, (v_w_in, v_w_out, v_w_gate_up, v_w_down)))
    shards = {kind: big_w[kind].astype(BF16) for kind in WEIGHT_KINDS}
    all_gains = dict(pre_mix_norm=pre_mix_norm, attn_out_norm=attn_out_norm, conv_out_norm=conv_out_norm, post_mix_norm=post_mix_norm,
                     pre_ffn_norm=pre_ffn_norm, post_ffn_norm=post_ffn_norm)

    def gather(kinds, l):
        return _Gather(kinds, [shards[kind][l] for kind in kinds])

    def gains(l):
        return {name: g[l:l + 1] for name, g in all_gains.items()}

    tables = _rope(positions)

    first = _comm_only(gather(("in", "out"), 0), "allgather_first")
    weights = [dict(zip(("in", "out"), first))] + [None] * (DEPTH - 1)
    saved = [None] * DEPTH
    h = x[0]
    for l in range(DEPTH):
        ffn_comm = gather(("in", "out"), l + 1) if l + 1 < DEPTH else None
        h, saved[l], weights[l], landed = _layer_forward(h, gains(l), conv_w_full[l], tables, weights[l], gather(("gu", "dn"), l), ffn_comm)
        if ffn_comm is not None:
            weights[l + 1] = dict(zip(ffn_comm.kinds, landed))

    loss, dx = _loss_and_grad(h, loss_target[0])
    loss = lax.psum(loss[0, 0], ("x", "y", "c"))

    parts = {kind: [None] * DEPTH for kind in WEIGHT_KINDS}
    small_grads = [None] * DEPTH
    g_in_above = None
    for l in reversed(range(DEPTH)):
        ffn_comm = _Scatter(("in",), (g_in_above,)) if g_in_above is not None else None
        dx, g_in_above, small_grads[l], landed, rest = _layer_backward(dx, saved[l], gains(l), conv_w_full[l], tables, weights[l], ffn_comm, True)
        if ffn_comm is not None:
            parts["in"][l + 1] = landed[0]
        for kind, part in rest.items():
            parts[kind][l] = part

    tiles = {"in": 256, "out": 128, "gu": 256, "dn": 176}
    last = _Scatter(("in",), (g_in_above,))
    big = {}
    big["gu"], landed = _adamw_layers(parts["gu"], big_w["gu"], big_m["gu"], big_v["gu"], tiles["gu"], "adamw_gu", comm=last)
    parts["in"][0] = landed[0]
    for kind in ("dn", "out", "in"):
        big[kind] = _adamw_layers(parts[kind], big_w[kind], big_m[kind], big_v[kind], tiles[kind], "adamw_" + kind)[0]

    packed = jnp.concatenate(small_grads, axis=0)
    gathered = _allgather_small(packed, "allgather_small_grads")

    def pack_state(state):
        rows = [_pack_small(*[state[name][l:l + 1] for name in GAIN_NAMES], jnp.zeros((3, ATTN_WIDTH), F32)) for l in range(DEPTH)]
        return jnp.concatenate(rows, axis=0)

    sw = pack_state(all_gains)
    sm = pack_state(dict(zip(GAIN_NAMES, (m_pre_mix_norm, m_post_mix_norm, m_pre_ffn_norm, m_post_ffn_norm, m_attn_out_norm, m_conv_out_norm))))
    sv = pack_state(dict(zip(GAIN_NAMES, (v_pre_mix_norm, v_post_mix_norm, v_pre_ffn_norm, v_post_ffn_norm, v_attn_out_norm, v_conv_out_norm))))
    sg, sd, snm, snv = _adamw_sum8(gathered, sw, sm, sv, "adamw_small")

    def unpack(p):
        p = p.reshape(DEPTH, SMALL_ROWS, D_MODEL)
        return dict(pre_mix_norm=p[:, 0], post_mix_norm=p[:, 1], pre_ffn_norm=p[:, 2], post_ffn_norm=p[:, 3],
                    attn_out_norm=p[:, 4, :ATTN_WIDTH], conv_out_norm=p[:, 4, ATTN_WIDTH:])

    small = [unpack(p) for p in (sg, sd, snm, snv)]
    sg3 = sg.reshape(DEPTH, SMALL_ROWS, D_MODEL)
    taps_grad_full = jnp.stack([sg3[:, 5, :ATTN_WIDTH], sg3[:, 5, ATTN_WIDTH:], sg3[:, 6, :ATTN_WIDTH]], axis=1)
    taps_grad = lax.dynamic_slice_in_dim(taps_grad_full, me * conv_channels, conv_channels, axis=2)

    def flat(a):
        return a.reshape(DEPTH * 3, conv_channels)

    td, tnm, tnv = _adamw_plain(flat(taps_grad), flat(conv_w), flat(m_conv_w), flat(v_conv_w), "adamw_taps")
    taps = [taps_grad] + [a.reshape(conv_w.shape) for a in (td, tnm, tnv)]

    def leaves(o):
        s = small[o]
        return (s["pre_mix_norm"], big["in"][o], taps[o], s["attn_out_norm"], s["conv_out_norm"], big["out"][o], s["post_mix_norm"],
                s["pre_ffn_norm"], big["gu"][o], big["dn"][o], s["post_ffn_norm"])

    return (loss, dx[None], *leaves(0), *leaves(1), *leaves(2), *leaves(3))
```

```python
import math

import jax
import jax.numpy as jnp
from jax import lax
from jax.experimental import pallas as pl
from jax.experimental.pallas import tpu as pltpu

F32 = jnp.float32
BF16 = jnp.bfloat16
MESH = pl.DeviceIdType.MESH

SEQ = 4096
D_MODEL = 1024
DEPTH = 4
N_DEV = 8
ATTN_WIDTH = 512
IN_PROJ_WIDTH = 3072
FFN_HIDDEN = 2816
FFN_BLOCK = 2 * FFN_HIDDEN // N_DEV
W_IN_BLOCK = IN_PROJ_WIDTH // N_DEV
W_OUT_BLOCK = D_MODEL // N_DEV
W_DOWN_BLOCK = FFN_HIDDEN // N_DEV
HEAD_DIM = 64
ROPE_DIM = 16
ROPE_THETA = 500000.0
DILATIONS = (1, 4, 16)
HALF_WINDOW = 64
RMS_EPS = 1e-6
NEG_INF = -1e30
LANES = 128
Q_BLOCK = 128
K_WINDOW = Q_BLOCK + 2 * HALF_WINDOW
PERM_CHUNK = 256
ROW_TILE = 512
FFN_TILE = 256
WGRAD_TILE = 1024
ADAM_LR, ADAM_B1, ADAM_B2, ADAM_EPS, ADAM_WD, ADAM_STEP = 0.001, 0.9, 0.999, 1e-08, 0.01, 10
MIB = 1024 * 1024

WEIGHT_KINDS = ("in", "out", "gu", "dn")
FULL_SHAPES = {"in": (D_MODEL, IN_PROJ_WIDTH), "out": (D_MODEL, D_MODEL), "gu": (N_DEV, FFN_BLOCK, D_MODEL), "dn": (FFN_HIDDEN, D_MODEL)}
SHARD_SHAPES = {"in": (D_MODEL, W_IN_BLOCK), "out": (W_OUT_BLOCK, D_MODEL), "gu": (FFN_BLOCK, D_MODEL), "dn": (W_DOWN_BLOCK, D_MODEL)}
ANY = pl.BlockSpec(memory_space=pl.ANY)


def _sds(shape, dtype):
    return jax.ShapeDtypeStruct(shape, dtype)


def _rows(width, tile=ROW_TILE):
    return pl.BlockSpec((tile, width), lambda i: (i, 0))


def _frows(width):
    return _rows(width, FFN_TILE)


def _const(shape):
    return pl.BlockSpec(shape, lambda i: (0,) * len(shape))


def _const1(shape):
    return pl.BlockSpec(shape, lambda i: (0,) * len(shape), pipeline_mode=pl.Buffered(1))


def _halo_prev(width, tile=ROW_TILE):
    return pl.BlockSpec((8, width), lambda i: (jnp.maximum(i * (tile // 8) - 1, 0), 0))


def _halo_next(width, tile=ROW_TILE):
    return pl.BlockSpec((8, width), lambda i: (jnp.minimum((i + 1) * (tile // 8), SEQ // 8 - 1), 0))


def _rms(x):
    r = lax.rsqrt(jnp.mean(x * x, axis=-1, keepdims=True) + RMS_EPS)
    return x * r, r


def _rms_bwd(dn, y, r):
    return r * (dn - y * jnp.mean(dn * y, axis=-1, keepdims=True))


def _dot(a, b):
    return jnp.dot(a, b, preferred_element_type=F32)


def _dot_nt(a, b):
    return lax.dot_general(a, b, (((1,), (1,)), ((), ())), preferred_element_type=F32)


def _dot_tn(a, b):
    return lax.dot_general(a, b, (((0,), (0,)), ((), ())), preferred_element_type=F32)


def _place():
    return lax.axis_index("x"), lax.axis_index("y"), lax.axis_index("c")


def _block_of(px, py, pc):
    return 4 * px + 2 * py + pc


def _weight_block(ref, kind, blk):
    if kind == "in":
        return ref.at[:, pl.ds(blk * W_IN_BLOCK, W_IN_BLOCK)]
    if kind == "out":
        return ref.at[pl.ds(blk * W_OUT_BLOCK, W_OUT_BLOCK), :]
    if kind == "gu":
        return ref.at[blk]
    return ref.at[pl.ds(blk * W_DOWN_BLOCK, W_DOWN_BLOCK), :]


def _dma_semaphores(n):
    return [pltpu.SemaphoreType.DMA((n, 7)), pltpu.SemaphoreType.DMA((n, 7)), pltpu.SemaphoreType.DMA((n,))]


class _Gather:
    def __init__(self, kinds, shards):
        self.kinds, self.operands = tuple(kinds), list(shards)
        self.tag = "gather_" + "_".join(kinds)
        self.out_shape = [_sds(FULL_SHAPES[k], BF16) for k in kinds]
        self.scratch = _dma_semaphores(len(kinds))

    def _parties(self):
        x, y, c = _place()
        return (x, y, c), (x, y, 1 - c), [(1 - x, y), (x, 1 - y), (1 - x, 1 - y)], c

    def _copy(self, outs, sems, w, k, block, to, src=None):
        dst = _weight_block(outs[w], self.kinds[w], _block_of(*block))
        return pltpu.make_async_remote_copy(src_ref=dst if src is None else src, dst_ref=dst, send_sem=sems[0].at[w, k],
                                            recv_sem=sems[1].at[w, k], device_id=to, device_id_type=MESH)

    def _own(self, srcs, outs, sems, w, me):
        return pltpu.make_async_copy(srcs[w], _weight_block(outs[w], self.kinds[w], _block_of(*me)), sems[2].at[w])

    def _first(self, srcs, outs, sems, w):
        me, sibling, chips, c = self._parties()
        return [self._copy(outs, sems, w, 0, me, sibling, src=srcs[w])] + [
            self._copy(outs, sems, w, 1 + j, me, (*chip, c), src=srcs[w]) for j, chip in enumerate(chips)]

    def start(self, srcs, outs, sems):
        me = self._parties()[0]
        for w in range(len(self.kinds)):
            self._own(srcs, outs, sems, w, me).start()
            for cp in self._first(srcs, outs, sems, w):
                cp.start()

    def forward(self, srcs, outs, sems):
        me, sibling, chips, c = self._parties()
        for j, chip in enumerate(chips):
            for w in range(len(self.kinds)):
                self._copy(outs, sems, w, 1 + j, (*chip, c), me).wait_recv()
                self._copy(outs, sems, w, 4 + j, (*chip, c), sibling).start()

    def finish(self, srcs, outs, sems):
        me, sibling, chips, c = self._parties()
        for w in range(len(self.kinds)):
            self._copy(outs, sems, w, 0, sibling, me).wait_recv()
            for j, chip in enumerate(chips):
                self._copy(outs, sems, w, 4 + j, (*chip, 1 - c), me).wait_recv()
        for w in range(len(self.kinds)):
            for cp in self._first(srcs, outs, sems, w):
                cp.wait_send()
            for j, chip in enumerate(chips):
                self._copy(outs, sems, w, 4 + j, (*chip, c), sibling).wait_send()
            self._own(srcs, outs, sems, w, me).wait()


def _peers(x, y, c):
    return [(x ^ a, y ^ b, c ^ e) for a in (0, 1) for b in (0, 1) for e in (0, 1) if (a, b, e) != (0, 0, 0)]


class _Scatter:
    def __init__(self, kinds, grads):
        self.kinds, self.operands = tuple(kinds), list(grads)
        self.tag = "scatter_" + "_".join(kinds)
        self.out_shape = [_sds((N_DEV,) + SHARD_SHAPES[k], BF16) for k in kinds]
        self.scratch = _dma_semaphores(len(kinds))

    def _copies(self, srcs, outs, sems):
        x, y, c = _place()
        me = _block_of(x, y, c)
        copies = []
        for w, kind in enumerate(self.kinds):
            copies.append(pltpu.make_async_copy(_weight_block(srcs[w], kind, me), outs[w].at[me], sems[2].at[w]))
            for k, peer in enumerate(_peers(x, y, c)):
                copies.append(pltpu.make_async_remote_copy(
                    src_ref=_weight_block(srcs[w], kind, _block_of(*peer)), dst_ref=outs[w].at[me],
                    send_sem=sems[0].at[w, k], recv_sem=sems[1].at[w, k], device_id=peer, device_id_type=MESH))
        return copies

    def start(self, srcs, outs, sems):
        for cp in self._copies(srcs, outs, sems):
            cp.start()

    def forward(self, srcs, outs, sems):
        pass

    def finish(self, srcs, outs, sems):
        for cp in self._copies(srcs, outs, sems):
            cp.wait()


def _call(body, args, *, grid, in_specs, out_specs, out_shape, scratch_shapes=(), vmem_mib=None, name, comm=None):
    kwargs = {} if vmem_mib is None else dict(compiler_params=pltpu.CompilerParams(vmem_limit_bytes=vmem_mib * MIB))
    in_specs, out_specs, out_shape, scratch_shapes = list(in_specs), list(out_specs), list(out_shape), list(scratch_shapes)
    if comm is None:
        res = pl.pallas_call(body, grid=grid, in_specs=in_specs, out_specs=out_specs, out_shape=out_shape,
                             scratch_shapes=scratch_shapes, name=name, **kwargs)(*args)
        return list(res), None
    n_in, n_out, n_scr = len(in_specs), len(out_specs), len(scratch_shapes)
    c_in, c_out = len(comm.operands), len(comm.out_shape)
    last = math.prod(grid) - 1

    def carried(*refs):
        cuts = [n_in, c_in, n_out, c_out, n_scr]
        parts, at = [], 0
        for n in cuts:
            parts.append(refs[at:at + n])
            at += n
        ins, c_ins, outs, c_outs, scr = parts
        sems = refs[at:]
        step = pl.program_id(0)
        for axis in range(1, len(grid)):
            step = step * grid[axis] + pl.program_id(axis)

        @pl.when(step == 0)
        def _():
            comm.start(c_ins, c_outs, sems)

        @pl.when(step == last)
        def _():
            comm.forward(c_ins, c_outs, sems)

        body(*ins, *outs, *scr)

        @pl.when(step == last)
        def _():
            comm.finish(c_ins, c_outs, sems)

    res = pl.pallas_call(carried, grid=grid, in_specs=in_specs + [ANY] * c_in, out_specs=out_specs + [ANY] * c_out,
                         out_shape=out_shape + comm.out_shape, scratch_shapes=scratch_shapes + comm.scratch,
                         name=name + "_" + comm.tag, **kwargs)(*args, *comm.operands)
    return list(res[:n_out]), list(res[n_out:])


def _comm_only(comm, name):
    def body(*refs):
        n_in, n_out = len(comm.operands), len(comm.out_shape)
        srcs, outs, sems = refs[:n_in], refs[n_in:n_in + n_out], refs[n_in + n_out:]
        comm.start(srcs, outs, sems)
        comm.forward(srcs, outs, sems)
        comm.finish(srcs, outs, sems)

    return pl.pallas_call(body, in_specs=[ANY] * len(comm.operands), out_specs=[ANY] * len(comm.out_shape), out_shape=comm.out_shape,
                          scratch_shapes=comm.scratch, name=name)(*comm.operands)


def _allgather_small(v, name):
    def body(v_ref, o_ref, send, recv, local):
        x, y, c = _place()
        me = _block_of(x, y, c)
        mine = pltpu.make_async_copy(v_ref, o_ref.at[me], local)
        mine.start()
        copies = [mine]
        for k, peer in enumerate(_peers(x, y, c)):
            cp = pltpu.make_async_remote_copy(src_ref=v_ref, dst_ref=o_ref.at[me], send_sem=send.at[k], recv_sem=recv.at[k],
                                              device_id=peer, device_id_type=MESH)
            cp.start()
            copies.append(cp)
        for cp in copies:
            cp.wait()

    vm = pl.BlockSpec(memory_space=pltpu.VMEM)
    return pl.pallas_call(
        body, in_specs=[vm], out_specs=vm, out_shape=_sds((N_DEV,) + v.shape, F32),
        scratch_shapes=[pltpu.SemaphoreType.DMA((7,)), pltpu.SemaphoreType.DMA((7,)), pltpu.SemaphoreType.DMA],
        name=name)(v)


def _rope_tables(pos_col, freq_row):
    def body(p_ref, f_ref, c_ref, sa_ref, sb_ref):
        ang = p_ref[...].astype(F32) * f_ref[...]
        lane = lax.broadcasted_iota(jnp.int32, ang.shape, 1) % HEAD_DIM
        cos, sin = jnp.cos(ang), jnp.sin(ang)
        c_ref[...] = jnp.where(lane < ROPE_DIM, cos, 1.0)
        sa_ref[...] = jnp.where(lane < ROPE_DIM // 2, -sin, 0.0)
        sb_ref[...] = jnp.where((lane >= ROPE_DIM // 2) & (lane < ROPE_DIM), sin, 0.0)

    return pl.pallas_call(
        body, grid=(SEQ // ROW_TILE,), in_specs=[_rows(1), _const((1, LANES))], out_specs=[_rows(LANES)] * 3,
        out_shape=[_sds((SEQ, LANES), F32)] * 3, name="rope_tables")(pos_col, freq_row)


def _rotate(t, c, sa, sb):
    parts = []
    for g in range(ATTN_WIDTH // LANES):
        tg = t[:, g * LANES:(g + 1) * LANES]
        parts.append(tg * c + pltpu.roll(tg, LANES - 8, axis=1) * sa + pltpu.roll(tg, 8, axis=1) * sb)
    return jnp.concatenate(parts, axis=1)


def _rotate_transposed(dt, c, sa, sb):
    parts = []
    for g in range(ATTN_WIDTH // LANES):
        dg = dt[:, g * LANES:(g + 1) * LANES]
        parts.append(dg * c + pltpu.roll(dg * sa, 8, axis=1) + pltpu.roll(dg * sb, LANES - 8, axis=1))
    return jnp.concatenate(parts, axis=1)


def _inproj_fwd(x, g_pre, w_in, tc, tsa, tsb):
    def body(x_ref, g_ref, w_ref, c_ref, sa_ref, sb_ref, q_ref, k_ref, v_ref, u_ref, b_ref, cc_ref):
        y, _ = _rms(x_ref[...])
        h = (y * g_ref[...]).astype(BF16)

        def proj(n):
            return _dot(h, w_ref[:, n * ATTN_WIDTH:(n + 1) * ATTN_WIDTH])

        c, sa, sb = c_ref[...], sa_ref[...], sb_ref[...]
        q_ref[...] = (_rotate(proj(0), c, sa, sb) * (HEAD_DIM ** -0.5)).astype(BF16)
        k_ref[...] = _rotate(proj(1), c, sa, sb).astype(BF16)
        v_ref[...] = proj(2).astype(BF16)
        u_ref[...] = proj(3)
        b_ref[...] = proj(4)
        cc_ref[...] = proj(5)

    a = ATTN_WIDTH
    return _call(
        body, (x, g_pre, w_in, tc, tsa, tsb), grid=(SEQ // ROW_TILE,),
        in_specs=[_rows(D_MODEL), _const((1, D_MODEL)), _const1((D_MODEL, IN_PROJ_WIDTH)), _rows(LANES), _rows(LANES), _rows(LANES)],
        out_specs=[_rows(a)] * 6, out_shape=[_sds((SEQ, a), BF16)] * 3 + [_sds((SEQ, a), F32)] * 3,
        vmem_mib=40, name="inproj_fwd")[0]


def _head_masks():
    lane = lax.broadcasted_iota(jnp.int32, (1, LANES), 1)
    first = lane < HEAD_DIM
    return first, first.astype(F32), 1.0 - first.astype(F32)


def _perm_chunks(dil):
    length = SEQ // dil
    out = []
    for r in range(dil):
        for c0 in range(0, length, PERM_CHUNK):
            chunk = (r * length + c0) // PERM_CHUNK
            rows = pl.ds(c0, PERM_CHUNK) if dil == 1 else pl.ds(r + dil * c0, PERM_CHUNK, stride=dil)
            out.append((chunk, rows))
    return out


def _chunk(c, offset=0):
    return pl.ds(offset + c * PERM_CHUNK, PERM_CHUNK)


def _write_band_bias(bias_ref):
    rr = lax.broadcasted_iota(jnp.int32, (Q_BLOCK, K_WINDOW), 0)
    cc = lax.broadcasted_iota(jnp.int32, (Q_BLOCK, K_WINDOW), 1)
    band = (cc >= rr) & (cc - rr <= 2 * HALF_WINDOW)
    bias_ref[0] = jnp.where(band, 0.0, NEG_INF)
    bias_ref[1] = jnp.where(band & (cc >= HALF_WINDOW), 0.0, NEG_INF)
    bias_ref[2] = jnp.where(band & (cc < Q_BLOCK + HALF_WINDOW), 0.0, NEG_INF)


def _band_bias_index(m0, length):
    return jnp.where(m0 % length == 0, 1, 0) + jnp.where((m0 + Q_BLOCK) % length == 0, 2, 0)


def _zero_key_padding(bufs):
    pad = jnp.zeros((HALF_WINDOW, LANES), BF16)
    for buf in bufs:
        buf[pl.ds(0, HALF_WINDOW), :] = pad
        buf[pl.ds(SEQ + HALF_WINDOW, HALF_WINDOW), :] = pad


def _attn_fwd(q, k, v, comm=None):
    group = 4

    def body(q_ref, k_ref, v_ref, o_ref, lse_ref, q32, k32, v32, qa, qb, kp, vp, accp, mlp,
             acc0, acc1, acc2, ml0, ml1, ml2, bias_ref):
        first, mask_a, mask_b = _head_masks()
        low = lax.broadcasted_iota(jnp.int32, (1, LANES), 1) % HEAD_DIM < HEAD_DIM // 2
        _zero_key_padding((kp, vp))
        _write_band_bias(bias_ref)
        q32[...] = q_ref[...].astype(F32)
        k32[...] = k_ref[...].astype(F32)
        v32[...] = v_ref[...].astype(F32)
        natural = ((acc0, ml0), (acc1, ml1), (acc2, ml2))

        for branch, dil in enumerate(DILATIONS):
            length = SEQ // dil
            assert length >= 2 * Q_BLOCK
            chunks = _perm_chunks(dil)
            for c, rows in chunks:
                val = q32[rows, :]
                qa[_chunk(c), :] = (val * mask_a).astype(BF16)
                qb[_chunk(c), :] = (val * mask_b).astype(BF16)
                kp[_chunk(c, HALF_WINDOW), :] = k32[rows, :].astype(BF16)
                vp[_chunk(c, HALF_WINDOW), :] = v32[rows, :].astype(BF16)
            acc_dst, ml_dst = natural[branch] if dil == 1 else (accp, mlp)

            def blocks(i, carry, length=length, acc_dst=acc_dst, ml_dst=ml_dst):
                base = pl.multiple_of(i * (group * Q_BLOCK), group * Q_BLOCK)
                starts = [base + g * Q_BLOCK for g in range(group)]
                scores = [[_dot_nt(qx[pl.ds(m0, Q_BLOCK), :], kp[pl.ds(m0, K_WINDOW), :]) for qx in (qa, qb)] for m0 in starts]
                probs = []
                for m0, pair in zip(starts, scores):
                    bias = bias_ref[_band_bias_index(m0, length)]
                    stats, ps = [], []
                    for s in pair:
                        s = s + bias
                        m = jnp.max(s, axis=1, keepdims=True)
                        p = jnp.exp(s - m)
                        stats.append(jnp.where(low, m, jnp.sum(p, axis=1, keepdims=True)))
                        ps.append(p.astype(BF16))
                    ml_dst[pl.ds(m0, Q_BLOCK), :] = jnp.where(first, stats[0], stats[1])
                    probs.append(ps)
                for m0, ps in zip(starts, probs):
                    vw = vp[pl.ds(m0, K_WINDOW), :]
                    acc_dst[pl.ds(m0, Q_BLOCK), :] = jnp.where(first, _dot(ps[0], vw), _dot(ps[1], vw))
                return carry

            lax.fori_loop(0, SEQ // (group * Q_BLOCK), blocks, 0)

            if dil > 1:
                for c, rows in chunks:
                    natural[branch][0][rows, :] = accp[_chunk(c), :]
                    natural[branch][1][rows, :] = mlp[_chunk(c), :]

        for c in range(SEQ // PERM_CHUNK):
            packed = [ml[_chunk(c), :] for _, ml in natural]
            ms = [jnp.where(low, ml, pltpu.roll(ml, HEAD_DIM // 2, axis=1)) for ml in packed]
            ls = [jnp.where(low, pltpu.roll(ml, LANES - HEAD_DIM // 2, axis=1), ml) for ml in packed]
            m_all = jnp.maximum(jnp.maximum(ms[0], ms[1]), ms[2])
            es = [jnp.exp(m - m_all) for m in ms]
            z = ls[0] * es[0] + ls[1] * es[1] + ls[2] * es[2]
            num = natural[0][0][_chunk(c), :] * es[0] + natural[1][0][_chunk(c), :] * es[1] + natural[2][0][_chunk(c), :] * es[2]
            o_ref[_chunk(c), :] = num / z
            lse_ref[_chunk(c), :] = m_all + jnp.log(z)

    col = pl.BlockSpec((SEQ, LANES), lambda h: (0, h))
    padded = SEQ + 2 * HALF_WINDOW
    return _call(
        body, (q, k, v), grid=(ATTN_WIDTH // LANES,), in_specs=[col] * 3, out_specs=[col] * 2,
        out_shape=[_sds((SEQ, ATTN_WIDTH), F32)] * 2,
        scratch_shapes=[pltpu.VMEM((SEQ, LANES), F32)] * 3 + [pltpu.VMEM((SEQ, LANES), BF16)] * 2
        + [pltpu.VMEM((padded, LANES), BF16)] * 2 + [pltpu.VMEM((SEQ, LANES), F32)] * 8
        + [pltpu.VMEM((3, Q_BLOCK, K_WINDOW), F32)],
        vmem_mib=52, name="attn_fwd", comm=comm)


def _shifted(t, before, after, i):
    tile = t.shape[0]
    row = lax.broadcasted_iota(jnp.int32, (tile, 1), 0)
    before = jnp.where(i > 0, before, 0.0)
    after = jnp.where(i < SEQ // tile - 1, after, 0.0)
    return (jnp.where(row == 0, before, pltpu.roll(t, 1, axis=0)),
            jnp.where(row == tile - 1, after, pltpu.roll(t, tile - 1, axis=0)))


def _conv_parts(u, c, u_prev, c_prev, u_next, c_next, cw, i):
    t = c * u
    t_prev, t_next = _shifted(t, c_prev[7:8, :] * u_prev[7:8, :], c_next[0:1, :] * u_next[0:1, :], i)
    s = cw[0:1, :] * t_prev + cw[1:2, :] * t + cw[2:3, :] * t_next
    return t, t_prev, t_next, s


def _mix_fwd(attn, u, b, c, conv_w, g_attn, g_conv, w_out, x, g_post):
    def body(a_ref, u_ref, b_ref, c_ref, up_ref, cp_ref, un_ref, cn_ref, cw_ref, ga_ref, gc_ref, w_ref, x_ref, gp_ref,
             x1_ref, mg_ref, mix_ref):
        i = pl.program_id(0)
        _, _, _, s = _conv_parts(u_ref[...], c_ref[...], up_ref[...], cp_ref[...], un_ref[...], cn_ref[...], cw_ref[...], i)
        ya, _ = _rms(a_ref[...])
        yc, _ = _rms(b_ref[...] * s)
        merged = jnp.concatenate([ya * ga_ref[...], yc * gc_ref[...]], axis=1).astype(BF16)
        mix = _dot(merged, w_ref[...])
        ym, _ = _rms(mix)
        mg_ref[...] = merged
        mix_ref[...] = mix
        x1_ref[...] = x_ref[...] + ym * gp_ref[...]

    a = ATTN_WIDTH
    return _call(
        body, (attn, u, b, c, u, c, u, c, conv_w, g_attn, g_conv, w_out, x, g_post), grid=(SEQ // ROW_TILE,),
        in_specs=[_rows(a)] * 4 + [_halo_prev(a)] * 2 + [_halo_next(a)] * 2
        + [_const((3, a)), _const((1, a)), _const((1, a)), _const1((D_MODEL, D_MODEL)), _rows(D_MODEL), _const((1, D_MODEL))],
        out_specs=[_rows(D_MODEL)] * 3,
        out_shape=[_sds((SEQ, D_MODEL), F32), _sds((SEQ, D_MODEL), BF16), _sds((SEQ, D_MODEL), F32)],
        vmem_mib=40, name="mix_fwd")[0]


def _gu_spec():
    return pl.BlockSpec((N_DEV, FFN_TILE, FFN_BLOCK), lambda i: (0, i, 0))


def _ffn_fwd(x1, g_pre, w_gu, w_dn, g_post, comm=None):
    def body(x_ref, g_ref, wgu_ref, wdn_ref, gp_ref, x2_ref, gu_ref, f_ref):
        x1 = x_ref[...]
        y, _ = _rms(x1)
        h = (y * g_ref[...]).astype(BF16)
        f = jnp.zeros((FFN_TILE, D_MODEL), F32)
        for j in range(N_DEV // 2):
            gate = _dot_nt(h, wgu_ref[j])
            up = _dot_nt(h, wgu_ref[j + N_DEV // 2])
            gu_ref[j] = gate.astype(BF16)
            gu_ref[j + N_DEV // 2] = up.astype(BF16)
            act = (gate * jax.nn.sigmoid(gate) * up).astype(BF16)
            f = f + _dot(act, wdn_ref[pl.ds(j * FFN_BLOCK, FFN_BLOCK), :])
        yf, _ = _rms(f)
        f_ref[...] = f
        x2_ref[...] = x1 + yf * gp_ref[...]

    return _call(
        body, (x1, g_pre, w_gu, w_dn, g_post), grid=(SEQ // FFN_TILE,),
        in_specs=[_frows(D_MODEL), _const((1, D_MODEL)), _const1((N_DEV, FFN_BLOCK, D_MODEL)), _const1((FFN_HIDDEN, D_MODEL)),
                  _const((1, D_MODEL))],
        out_specs=[_frows(D_MODEL), _gu_spec(), _frows(D_MODEL)],
        out_shape=[_sds((SEQ, D_MODEL), F32), _sds((N_DEV, SEQ, FFN_BLOCK), BF16), _sds((SEQ, D_MODEL), F32)],
        vmem_mib=48, name="ffn_fwd", comm=comm)


def _loss_and_grad(y, target):
    n_tiles = SEQ // ROW_TILE

    def body(y_ref, t_ref, loss_ref, dy_ref, acc):
        i = pl.program_id(0)
        err = y_ref[...] - t_ref[...]
        dy_ref[...] = err * (1.0 / D_MODEL)

        @pl.when(i == 0)
        def _():
            acc[...] = jnp.zeros_like(acc)

        acc[...] += jnp.sum(err * err, axis=0, keepdims=True)

        @pl.when(i == n_tiles - 1)
        def _():
            loss_ref[...] = jnp.sum(acc[...], axis=1, keepdims=True) * (0.5 / D_MODEL)

    return _call(
        body, (y, target), grid=(n_tiles,), in_specs=[_rows(D_MODEL)] * 2, out_specs=[_const((1, 1)), _rows(D_MODEL)],
        out_shape=[_sds((1, 1), F32), _sds((SEQ, D_MODEL), F32)], scratch_shapes=[pltpu.VMEM((1, D_MODEL), F32)],
        name="loss_and_grad")[0]


def _accumulate(ref, value, i):
    @pl.when(i == 0)
    def _():
        ref[...] = value

    @pl.when(i > 0)
    def _():
        ref[...] += value


def _colsum(v):
    return jnp.sum(v, axis=0, keepdims=True)


def _ffn_bwd(dx2, f, x1, gu, w_gu, w_dn, g_post, g_pre, comm=None):
    half = N_DEV // 2

    def body(dx2_ref, f_ref, x1_ref, gu_ref, wgu_ref, wdn_ref, gpost_ref, gpre_ref,
             dx1_ref, df_ref, act_ref, dgu_ref, h_ref, dgpost_ref, dgpre_ref):
        i = pl.program_id(0)
        dx2 = dx2_ref[...]
        yf, rf = _rms(f_ref[...])
        _accumulate(dgpost_ref, _colsum(dx2 * yf), i)
        df = _rms_bwd(dx2 * gpost_ref[...], yf, rf).astype(BF16)
        df_ref[...] = df
        dh = jnp.zeros((FFN_TILE, D_MODEL), F32)
        for j in range(half):
            dact = _dot_nt(df, wdn_ref[pl.ds(j * FFN_BLOCK, FFN_BLOCK), :])
            gate = gu_ref[j].astype(F32)
            up = gu_ref[j + half].astype(F32)
            sig = jax.nn.sigmoid(gate)
            silu = gate * sig
            act_ref[j] = (silu * up).astype(BF16)
            dgate = (dact * up * (sig * (1.0 + gate * (1.0 - sig)))).astype(BF16)
            dup = (dact * silu).astype(BF16)
            dgu_ref[j] = dgate
            dgu_ref[j + half] = dup
            dh = dh + _dot(dgate, wgu_ref[j]) + _dot(dup, wgu_ref[j + half])
        y1, r1 = _rms(x1_ref[...])
        h_ref[...] = (y1 * gpre_ref[...]).astype(BF16)
        _accumulate(dgpre_ref, _colsum(dh * y1), i)
        dx1_ref[...] = dx2 + _rms_bwd(dh * gpre_ref[...], y1, r1)

    act_spec = pl.BlockSpec((half, FFN_TILE, FFN_BLOCK), lambda i: (0, i, 0))
    return _call(
        body, (dx2, f, x1, gu, w_gu, w_dn, g_post, g_pre), grid=(SEQ // FFN_TILE,),
        in_specs=[_frows(D_MODEL)] * 3 + [_gu_spec(), _const1((N_DEV, FFN_BLOCK, D_MODEL)), _const1((FFN_HIDDEN, D_MODEL)),
                                          _const((1, D_MODEL)), _const((1, D_MODEL))],
        out_specs=[_frows(D_MODEL), _frows(D_MODEL), act_spec, _gu_spec(), _frows(D_MODEL), _const((1, D_MODEL)), _const((1, D_MODEL))],
        out_shape=[_sds((SEQ, D_MODEL), F32), _sds((SEQ, D_MODEL), BF16), _sds((half, SEQ, FFN_BLOCK), BF16),
                   _sds((N_DEV, SEQ, FFN_BLOCK), BF16), _sds((SEQ, D_MODEL), BF16), _sds((1, D_MODEL), F32), _sds((1, D_MODEL), F32)],
        vmem_mib=52, name="ffn_bwd", comm=comm)


def _wgrad(a, g, a_spec, g_spec, out_shape, out_spec, acc_shape, n_blocks, name):
    n_tiles = SEQ // WGRAD_TILE

    def body(a_ref, g_ref, o_ref, acc):
        s = pl.program_id(1)
        av = a_ref[...].reshape(WGRAD_TILE, a_ref.shape[-1])
        gv = g_ref[...].reshape(WGRAD_TILE, g_ref.shape[-1])
        part = _dot_tn(av, gv)

        @pl.when(s == 0)
        def _():
            acc[...] = part

        @pl.when(s > 0)
        def _():
            acc[...] += part

        @pl.when(s == n_tiles - 1)
        def _():
            o_ref[...] = acc[...].astype(BF16).reshape(o_ref.shape)

    return _call(body, (a, g), grid=(n_blocks, n_tiles), in_specs=[a_spec, g_spec], out_specs=[out_spec],
                 out_shape=[_sds(out_shape, BF16)], scratch_shapes=[pltpu.VMEM(acc_shape, F32)], vmem_mib=40, name=name)[0][0]


def _wgrad_gate_up(h, dgu):
    t = WGRAD_TILE
    return _wgrad(dgu, h, pl.BlockSpec((1, t, FFN_BLOCK), lambda j, s: (j, s, 0)), pl.BlockSpec((t, D_MODEL), lambda j, s: (s, 0)),
                  (N_DEV, FFN_BLOCK, D_MODEL), pl.BlockSpec((1, FFN_BLOCK, D_MODEL), lambda j, s: (j, 0, 0)),
                  (FFN_BLOCK, D_MODEL), N_DEV, "wgrad_gate_up")


def _wgrad_down(act, df):
    t = WGRAD_TILE
    return _wgrad(act, df, pl.BlockSpec((1, t, FFN_BLOCK), lambda j, s: (j, s, 0)), pl.BlockSpec((t, D_MODEL), lambda j, s: (s, 0)),
                  (FFN_HIDDEN, D_MODEL), pl.BlockSpec((FFN_BLOCK, D_MODEL), lambda j, s: (j, 0)),
                  (FFN_BLOCK, D_MODEL), N_DEV // 2, "wgrad_down")


def _wgrad_out(merged, dmix):
    t = WGRAD_TILE
    return _wgrad(merged, dmix, pl.BlockSpec((t, D_MODEL), lambda j, s: (s, 0)), pl.BlockSpec((t, 512), lambda j, s: (s, j)),
                  (D_MODEL, D_MODEL), pl.BlockSpec((D_MODEL, 512), lambda j, s: (0, j)), (D_MODEL, 512), 2, "wgrad_out")


def _wgrad_in(h, dproj):
    t = WGRAD_TILE
    return _wgrad(h, dproj, pl.BlockSpec((t, D_MODEL), lambda j, s: (s, 0)), pl.BlockSpec((t, 512), lambda j, s: (s, j)),
                  (D_MODEL, IN_PROJ_WIDTH), pl.BlockSpec((D_MODEL, 512), lambda j, s: (0, j)), (D_MODEL, 512),
                  IN_PROJ_WIDTH // 512, "wgrad_in")


def _mix_bwd(dx1, mix, attn, u, b, c, conv_w, g_attn, g_conv, g_post, w_out):
    def body(dx1_ref, mix_ref, a_ref, u_ref, b_ref, c_ref, up_ref, cp_ref, un_ref, cn_ref, cw_ref, ga_ref, gc_ref, gp_ref, w_ref,
             dmix_ref, da_ref, ds_ref, db_ref, dgp_ref, dga_ref, dgc_ref):
        i = pl.program_id(0)
        dx1 = dx1_ref[...]
        ym, rm = _rms(mix_ref[...])
        _accumulate(dgp_ref, _colsum(dx1 * ym), i)
        dmix = _rms_bwd(dx1 * gp_ref[...], ym, rm).astype(BF16)
        dmix_ref[...] = dmix
        dmerged = _dot_nt(dmix, w_ref[...])
        dna, dnc = dmerged[:, :ATTN_WIDTH], dmerged[:, ATTN_WIDTH:]
        ya, ra = _rms(a_ref[...])
        _accumulate(dga_ref, _colsum(dna * ya), i)
        da_ref[...] = _rms_bwd(dna * ga_ref[...], ya, ra)
        _, _, _, s = _conv_parts(u_ref[...], c_ref[...], up_ref[...], cp_ref[...], un_ref[...], cn_ref[...], cw_ref[...], i)
        gate_b = b_ref[...]
        yc, rc = _rms(gate_b * s)
        _accumulate(dgc_ref, _colsum(dnc * yc), i)
        dy = _rms_bwd(dnc * gc_ref[...], yc, rc)
        db_ref[...] = dy * s
        ds_ref[...] = dy * gate_b

    a = ATTN_WIDTH
    return _call(
        body, (dx1, mix, attn, u, b, c, u, c, u, c, conv_w, g_attn, g_conv, g_post, w_out), grid=(SEQ // ROW_TILE,),
        in_specs=[_rows(D_MODEL)] * 2 + [_rows(a)] * 4 + [_halo_prev(a)] * 2 + [_halo_next(a)] * 2
        + [_const((3, a)), _const((1, a)), _const((1, a)), _const((1, D_MODEL)), _const1((D_MODEL, D_MODEL))],
        out_specs=[_rows(D_MODEL)] + [_rows(a)] * 3 + [_const((1, D_MODEL)), _const((1, a)), _const((1, a))],
        out_shape=[_sds((SEQ, D_MODEL), BF16)] + [_sds((SEQ, a), F32)] * 3 + [_sds((1, D_MODEL), F32), _sds((1, a), F32), _sds((1, a), F32)],
        vmem_mib=40, name="mix_bwd")[0]


def _attn_bwd(q, k, v, do, o, lse, comm=None):
    group = 2

    def body(q_ref, k_ref, v_ref, do_ref, o_ref, lse_ref, dq_ref, dk_ref, dv_ref,
             q32, k32, v32, qa, qb, doa, dob, kp, vp, lsep, dlp, dnat, dqp, dkp, dvp, bias_ref):
        first, mask_a, mask_b = _head_masks()
        _zero_key_padding((kp, vp))
        _write_band_bias(bias_ref)
        q32[...] = q_ref[...].astype(F32)
        k32[...] = k_ref[...].astype(F32)
        v32[...] = v_ref[...].astype(F32)
        for c in range(SEQ // PERM_CHUNK):
            prod = do_ref[_chunk(c), :] * o_ref[_chunk(c), :]
            d_a = jnp.sum(prod * mask_a, axis=1, keepdims=True)
            d_b = jnp.sum(prod * mask_b, axis=1, keepdims=True)
            dnat[_chunk(c), :] = jnp.where(first, d_a, d_b)

        for step, dil in enumerate(DILATIONS[1:] + DILATIONS[:1]):
            length = SEQ // dil
            assert length >= 2 * Q_BLOCK
            chunks = _perm_chunks(dil)
            for c, rows in chunks:
                val = q32[rows, :]
                qa[_chunk(c), :] = (val * mask_a).astype(BF16)
                qb[_chunk(c), :] = (val * mask_b).astype(BF16)
                val = do_ref[rows, :]
                doa[_chunk(c), :] = (val * mask_a).astype(BF16)
                dob[_chunk(c), :] = (val * mask_b).astype(BF16)
                kp[_chunk(c, HALF_WINDOW), :] = k32[rows, :].astype(BF16)
                vp[_chunk(c, HALF_WINDOW), :] = v32[rows, :].astype(BF16)
                lsep[_chunk(c), :] = lse_ref[rows, :]
                dlp[_chunk(c), :] = dnat[rows, :]
            zero = jnp.zeros((PERM_CHUNK, LANES), F32)
            for c in range(SEQ // PERM_CHUNK):
                dkp[_chunk(c), :] = zero
                dvp[_chunk(c), :] = zero
            dkp[pl.ds(SEQ, 2 * HALF_WINDOW), :] = zero[:2 * HALF_WINDOW]
            dvp[pl.ds(SEQ, 2 * HALF_WINDOW), :] = zero[:2 * HALF_WINDOW]

            heads = ((qa, doa, 0), (qb, dob, HEAD_DIM))

            def blocks(i, carry, length=length):
                base = pl.multiple_of(i * (group * Q_BLOCK), group * Q_BLOCK)
                starts = [base + g * Q_BLOCK for g in range(group)]
                raw = [[(_dot_nt(qx[pl.ds(m0, Q_BLOCK), :], kp[pl.ds(m0, K_WINDOW), :]),
                         _dot_nt(dox[pl.ds(m0, Q_BLOCK), :], vp[pl.ds(m0, K_WINDOW), :])) for qx, dox, _ in heads]
                       for m0 in starts]
                grads = []
                for m0, pair in zip(starts, raw):
                    bias = bias_ref[_band_bias_index(m0, length)]
                    lse_b, d_b = lsep[pl.ds(m0, Q_BLOCK), :], dlp[pl.ds(m0, Q_BLOCK), :]
                    out = []
                    for (s, dp), (_, _, col) in zip(pair, heads):
                        p = jnp.exp(s + bias - lse_b[:, col:col + 1])
                        out.append(((p * (dp - d_b[:, col:col + 1])).astype(BF16), p.astype(BF16)))
                    grads.append(out)
                for m0, out in zip(starts, grads):
                    qrows, krows = pl.ds(m0, Q_BLOCK), pl.ds(m0, K_WINDOW)
                    kw = kp[krows, :]
                    dk = jnp.zeros((K_WINDOW, LANES), F32)
                    dv = jnp.zeros((K_WINDOW, LANES), F32)
                    for (ds, p), (qx, dox, _) in zip(out, heads):
                        dk = dk + _dot_tn(ds, qx[qrows, :])
                        dv = dv + _dot_tn(p, dox[qrows, :])
                    dqp[qrows, :] = jnp.where(first, _dot(out[0][0], kw), _dot(out[1][0], kw)) * (HEAD_DIM ** -0.5)
                    dkp[krows, :] += dk
                    dvp[krows, :] += dv
                return carry

            lax.fori_loop(0, SEQ // (group * Q_BLOCK), blocks, 0)

            for c, rows in chunks:
                g_q, g_k, g_v = dqp[_chunk(c), :], dkp[_chunk(c, HALF_WINDOW), :], dvp[_chunk(c, HALF_WINDOW), :]
                if step == 0:
                    dq_ref[rows, :] = g_q
                    dk_ref[rows, :] = g_k
                    dv_ref[rows, :] = g_v
                else:
                    dq_ref[rows, :] = dq_ref[rows, :] + g_q
                    dk_ref[rows, :] = dk_ref[rows, :] + g_k
                    dv_ref[rows, :] = dv_ref[rows, :] + g_v

    col = pl.BlockSpec((SEQ, LANES), lambda h: (0, h))
    col1 = pl.BlockSpec((SEQ, LANES), lambda h: (0, h), pipeline_mode=pl.Buffered(1))
    padded = SEQ + 2 * HALF_WINDOW
    return _call(
        body, (q, k, v, do, o, lse), grid=(ATTN_WIDTH // LANES,), in_specs=[col1] * 6, out_specs=[col] * 3,
        out_shape=[_sds((SEQ, ATTN_WIDTH), F32)] * 3,
        scratch_shapes=[pltpu.VMEM((SEQ, LANES), F32)] * 3 + [pltpu.VMEM((SEQ, LANES), BF16)] * 4
        + [pltpu.VMEM((padded, LANES), BF16)] * 2 + [pltpu.VMEM((SEQ, LANES), F32)] * 4 + [pltpu.VMEM((padded, LANES), F32)] * 2
        + [pltpu.VMEM((3, Q_BLOCK, K_WINDOW), F32)],
        vmem_mib=56, name="attn_bwd", comm=comm)


def _inproj_bwd(dq, dk, dv, ds, db, u, c, conv_w, tc, tsa, tsb, w_in, x, g_pre, dx1, comm=None):
    def body(dq_ref, dk_ref, dv_ref, ds_ref, db_ref, u_ref, c_ref, dsp_ref, up_ref, cp_ref, dsn_ref, un_ref, cn_ref, cw_ref,
             tc_ref, tsa_ref, tsb_ref, w_ref, x_ref, g_ref, dx1_ref, dx_ref, dproj_ref, h_ref, dg_ref, dcw_ref):
        i = pl.program_id(0)
        cw = cw_ref[...]
        u, c = u_ref[...], c_ref[...]
        t, t_prev, t_next, _ = _conv_parts(u, c, up_ref[...], cp_ref[...], un_ref[...], cn_ref[...], cw, i)
        ds = ds_ref[...]
        ds_prev, ds_next = _shifted(ds, dsp_ref[7:8, :], dsn_ref[0:1, :], i)
        dt = cw[0:1, :] * ds_next + cw[1:2, :] * ds + cw[2:3, :] * ds_prev
        _accumulate(dcw_ref, jnp.concatenate([_colsum(ds * t_prev), _colsum(ds * t), _colsum(ds * t_next)], axis=0), i)
        tc_, tsa_, tsb_ = tc_ref[...], tsa_ref[...], tsb_ref[...]
        dproj = jnp.concatenate(
            [_rotate_transposed(dq_ref[...], tc_, tsa_, tsb_), _rotate_transposed(dk_ref[...], tc_, tsa_, tsb_), dv_ref[...],
             dt * c, db_ref[...], dt * u], axis=1).astype(BF16)
        dproj_ref[...] = dproj
        dh = _dot_nt(dproj, w_ref[...])
        y, r = _rms(x_ref[...])
        h_ref[...] = (y * g_ref[...]).astype(BF16)
        _accumulate(dg_ref, _colsum(dh * y), i)
        dx_ref[...] = dx1_ref[...] + _rms_bwd(dh * g_ref[...], y, r)

    a = ATTN_WIDTH
    tile = FFN_TILE

    def rows(width):
        return _rows(width, tile)

    return _call(
        body, (dq, dk, dv, ds, db, u, c, ds, u, c, ds, u, c, conv_w, tc, tsa, tsb, w_in, x, g_pre, dx1), grid=(SEQ // tile,),
        in_specs=[rows(a)] * 7 + [_halo_prev(a, tile)] * 3 + [_halo_next(a, tile)] * 3
        + [_const((3, a)), rows(LANES), rows(LANES), rows(LANES), _const1((D_MODEL, IN_PROJ_WIDTH)), rows(D_MODEL),
           _const((1, D_MODEL)), rows(D_MODEL)],
        out_specs=[rows(D_MODEL), rows(IN_PROJ_WIDTH), rows(D_MODEL), _const((1, D_MODEL)), _const((3, a))],
        out_shape=[_sds((SEQ, D_MODEL), F32), _sds((SEQ, IN_PROJ_WIDTH), BF16), _sds((SEQ, D_MODEL), BF16),
                   _sds((1, D_MODEL), F32), _sds((3, a), F32)],
        vmem_mib=48, name="inproj_bwd", comm=comm)


def _adamw_math(w, g, m, v):
    m = ADAM_B1 * m + (1.0 - ADAM_B1) * g
    v = ADAM_B2 * v + (1.0 - ADAM_B2) * (g * g)
    m_hat = m / (1.0 - ADAM_B1 ** ADAM_STEP)
    v_hat = v / (1.0 - ADAM_B2 ** ADAM_STEP)
    delta = -ADAM_LR * (m_hat / (jnp.sqrt(v_hat) + ADAM_EPS) + ADAM_WD * w)
    return delta, m, v


def _sum_parts(p_ref):
    g = p_ref[0].astype(F32)
    for k in range(1, N_DEV):
        g = g + p_ref[k].astype(F32)
    return g


def _adamw_layers(parts, w, m, v, row_tile, name, comm=None):
    _, rows, cols = w.shape
    n_tiles = rows // row_tile

    def body(*refs):
        p_refs = refs[:DEPTH]
        w_ref, m_ref, v_ref, g_ref, d_ref, nm_ref, nv_ref = refs[DEPTH:]
        layer = pl.program_id(0)
        for l, p_ref in enumerate(p_refs):
            @pl.when(layer == l)
            def _(p_ref=p_ref):
                g = _sum_parts(p_ref)
                g_ref[0] = g
                d_ref[0], nm_ref[0], nv_ref[0] = _adamw_math(w_ref[0], g, m_ref[0], v_ref[0])

    def part_spec(l):
        return pl.BlockSpec((N_DEV, row_tile, cols),
                            lambda layer, i: (0, jnp.where(layer == l, i, jnp.where(layer < l, 0, n_tiles - 1)), 0))

    tile = pl.BlockSpec((1, row_tile, cols), lambda layer, i: (layer, i, 0))
    return _call(body, (*parts, w, m, v), grid=(DEPTH, n_tiles), in_specs=[part_spec(l) for l in range(DEPTH)] + [tile] * 3,
                 out_specs=[tile] * 4, out_shape=[_sds(w.shape, F32)] * 4, name=name, comm=comm)


def _adamw_sum8(parts, w, m, v, name):
    def body(p_ref, w_ref, m_ref, v_ref, g_ref, d_ref, nm_ref, nv_ref):
        g = _sum_parts(p_ref)
        g_ref[...] = g
        d_ref[...], nm_ref[...], nv_ref[...] = _adamw_math(w_ref[...], g, m_ref[...], v_ref[...])

    return pl.pallas_call(body, out_shape=[_sds(w.shape, F32)] * 4, name=name)(parts, w, m, v)


def _adamw_plain(g, w, m, v, name):
    def body(g_ref, w_ref, m_ref, v_ref, d_ref, nm_ref, nv_ref):
        d_ref[...], nm_ref[...], nv_ref[...] = _adamw_math(w_ref[...], g_ref[...], m_ref[...], v_ref[...])

    return pl.pallas_call(body, out_shape=[_sds(w.shape, F32)] * 3, name=name)(g, w, m, v)


SMALL_ROWS = 8
GAIN_NAMES = ("pre_mix_norm", "post_mix_norm", "pre_ffn_norm", "post_ffn_norm", "attn_out_norm", "conv_out_norm")


def _pack_small(pre_mix, post_mix, pre_ffn, post_ffn, attn_out, conv_out, taps):
    zeros = jnp.zeros((1, ATTN_WIDTH), F32)
    return jnp.concatenate([
        pre_mix, post_mix, pre_ffn, post_ffn, jnp.concatenate([attn_out, conv_out], axis=1),
        jnp.concatenate([taps[0:1], taps[1:2]], axis=1), jnp.concatenate([taps[2:3], zeros], axis=1),
        jnp.zeros((1, D_MODEL), F32)], axis=0)


def _rope(positions):
    inv_freq = ROPE_THETA ** (-jnp.arange(0, ROPE_DIM, 2, dtype=F32) / ROPE_DIM)
    lane = jnp.arange(LANES) % HEAD_DIM
    freq_row = jnp.where(lane < ROPE_DIM, inv_freq[lane % (ROPE_DIM // 2)], 0.0).reshape(1, LANES).astype(F32)
    return _rope_tables(positions.reshape(SEQ, 1), freq_row)


def _layer_forward(h, gains, taps, tables, w, attn_comm=None, ffn_comm=None):
    q, k, v, u, b, c = _inproj_fwd(h, gains["pre_mix_norm"], w["in"], *tables)
    (attn, lse), landed = _attn_fwd(q, k, v, comm=attn_comm)
    if attn_comm is not None:
        w = {**w, **dict(zip(attn_comm.kinds, landed))}
    x1, merged, mix = _mix_fwd(attn, u, b, c, taps, gains["attn_out_norm"], gains["conv_out_norm"], w["out"], h, gains["post_mix_norm"])
    (x2, gu, f), landed_next = _ffn_fwd(x1, gains["pre_ffn_norm"], w["gu"], w["dn"], gains["post_ffn_norm"], comm=ffn_comm)
    return x2, (h, q, k, v, u, b, c, attn, lse, merged, mix, x1, gu, f), w, landed_next


def _layer_backward(dx, saved, gains, taps, tables, w, ffn_comm=None, scatter=False):
    x0, q, k, v, u, b, c, attn, lse, merged, mix, x1, gu, f = saved
    (dx1, df, act, dgu, h2, dg_post_ffn, dg_pre_ffn), landed_prev = _ffn_bwd(
        dx, f, x1, gu, w["gu"], w["dn"], gains["post_ffn_norm"], gains["pre_ffn_norm"], comm=ffn_comm)
    g_gu = _wgrad_gate_up(h2, dgu)
    g_dn = _wgrad_down(act, df)
    dmix, dattn, ds, db, dg_post_mix, dg_attn, dg_conv = _mix_bwd(
        dx1, mix, attn, u, b, c, taps, gains["attn_out_norm"], gains["conv_out_norm"], gains["post_mix_norm"], w["out"])
    g_out = _wgrad_out(merged, dmix)
    attn_comm = _Scatter(("gu", "dn"), (g_gu, g_dn)) if scatter else None
    inproj_comm = _Scatter(("out",), (g_out,)) if scatter else None
    (dq, dk, dv), landed_ffn = _attn_bwd(q, k, v, dattn, attn, lse, comm=attn_comm)
    (dx0, dproj, h1, dg_pre_mix, dtaps), landed_out = _inproj_bwd(
        dq, dk, dv, ds, db, u, c, taps, *tables, w["in"], x0, gains["pre_mix_norm"], dx1, comm=inproj_comm)
    g_in = _wgrad_in(h1, dproj)
    small = _pack_small(dg_pre_mix, dg_post_mix, dg_pre_ffn, dg_post_ffn, dg_attn, dg_conv, dtaps)
    rest = dict(gu=landed_ffn[0], dn=landed_ffn[1], out=landed_out[0]) if scatter else dict(gu=g_gu, dn=g_dn, out=g_out)
    return dx0, g_in, small, landed_prev, rest


def kernel(x, positions, pre_mix_norm, w_in, conv_w, attn_out_norm, conv_out_norm, w_out, post_mix_norm, pre_ffn_norm, w_gate_up, w_down, post_ffn_norm, loss_target, m_pre_mix_norm, m_w_in, m_conv_w, m_attn_out_norm, m_conv_out_norm, m_w_out, m_post_mix_norm, m_pre_ffn_norm, m_w_gate_up, m_w_down, m_post_ffn_norm, v_pre_mix_norm, v_w_in, v_conv_w, v_attn_out_norm, v_conv_out_norm, v_w_out, v_post_mix_norm, v_pre_ffn_norm, v_w_gate_up, v_w_down, v_post_ffn_norm):
    mx, my, mc = _place()
    me = _block_of(mx, my, mc)
    conv_channels = conv_w.shape[-1]

    taps_flat = jnp.pad(conv_w.reshape(-1), (0, 8 * LANES - conv_w.size)).reshape(8, LANES)
    taps_all = _allgather_small(taps_flat, "allgather_taps").reshape(N_DEV, 8 * LANES)[:, :conv_w.size]
    conv_w_full = taps_all.reshape(N_DEV, DEPTH, 3, conv_channels).transpose(1, 2, 0, 3).reshape(DEPTH, 3, ATTN_WIDTH)

    def hidden_major(a):
        return jnp.swapaxes(a, 1, 2)

    big_w = dict(zip(WEIGHT_KINDS, (w_in, w_out, hidden_major(w_gate_up), w_down)))
    big_m = dict(zip(WEIGHT_KINDS, (m_w_in, m_w_out, hidden_major(m_w_gate_up), m_w_down)))
    big_v = dict(zip(WEIGHT_KINDS, (v_w_in, v_w_out, hidden_major(v_w_gate_up), v_w_down)))
    shards = {kind: big_w[kind].astype(BF16) for kind in WEIGHT_KINDS}
    all_gains = dict(pre_mix_norm=pre_mix_norm, attn_out_norm=attn_out_norm, conv_out_norm=conv_out_norm, post_mix_norm=post_mix_norm,
                     pre_ffn_norm=pre_ffn_norm, post_ffn_norm=post_ffn_norm)

    def gather(kinds, l):
        return _Gather(kinds, [shards[kind][l] for kind in kinds])

    def gains(l):
        return {name: g[l:l + 1] for name, g in all_gains.items()}

    tables = _rope(positions)

    early = ("in", "out", "dn")
    first = _comm_only(gather(early, 0), "allgather_first")
    weights = [dict(zip(early, first))] + [None] * (DEPTH - 1)
    saved = [None] * DEPTH
    h = x[0]
    for l in range(DEPTH):
        ffn_comm = gather(early, l + 1) if l + 1 < DEPTH else None
        h, saved[l], weights[l], landed = _layer_forward(h, gains(l), conv_w_full[l], tables, weights[l], gather(("gu",), l), ffn_comm)
        if ffn_comm is not None:
            weights[l + 1] = dict(zip(ffn_comm.kinds, landed))

    loss, dx = _loss_and_grad(h, loss_target[0])
    loss = lax.psum(loss[0, 0], ("x", "y", "c"))

    parts = {kind: [None] * DEPTH for kind in WEIGHT_KINDS}
    small_grads = [None] * DEPTH
    g_in_above = None
    for l in reversed(range(DEPTH)):
        ffn_comm = _Scatter(("in",), (g_in_above,)) if g_in_above is not None else None
        dx, g_in_above, small_grads[l], landed, rest = _layer_backward(dx, saved[l], gains(l), conv_w_full[l], tables, weights[l], ffn_comm, True)
        if ffn_comm is not None:
            parts["in"][l + 1] = landed[0]
        for kind, part in rest.items():
            parts[kind][l] = part

    tiles = {"in": 256, "out": 128, "gu": 176, "dn": 176}
    last = _Scatter(("in",), (g_in_above,))
    big = {}
    big["gu"], landed = _adamw_layers(parts["gu"], big_w["gu"], big_m["gu"], big_v["gu"], tiles["gu"], "adamw_gu", comm=last)
    parts["in"][0] = landed[0]
    for kind in ("dn", "out", "in"):
        big[kind] = _adamw_layers(parts[kind], big_w[kind], big_m[kind], big_v[kind], tiles[kind], "adamw_" + kind)[0]

    packed = jnp.concatenate(small_grads, axis=0)
    gathered = _allgather_small(packed, "allgather_small_grads")

    def pack_state(state):
        rows = [_pack_small(*[state[name][l:l + 1] for name in GAIN_NAMES], jnp.zeros((3, ATTN_WIDTH), F32)) for l in range(DEPTH)]
        return jnp.concatenate(rows, axis=0)

    sw = pack_state(all_gains)
    sm = pack_state(dict(zip(GAIN_NAMES, (m_pre_mix_norm, m_post_mix_norm, m_pre_ffn_norm, m_post_ffn_norm, m_attn_out_norm, m_conv_out_norm))))
    sv = pack_state(dict(zip(GAIN_NAMES, (v_pre_mix_norm, v_post_mix_norm, v_pre_ffn_norm, v_post_ffn_norm, v_attn_out_norm, v_conv_out_norm))))
    sg, sd, snm, snv = _adamw_sum8(gathered, sw, sm, sv, "adamw_small")

    def unpack(p):
        p = p.reshape(DEPTH, SMALL_ROWS, D_MODEL)
        return dict(pre_mix_norm=p[:, 0], post_mix_norm=p[:, 1], pre_ffn_norm=p[:, 2], post_ffn_norm=p[:, 3],
                    attn_out_norm=p[:, 4, :ATTN_WIDTH], conv_out_norm=p[:, 4, ATTN_WIDTH:])

    small = [unpack(p) for p in (sg, sd, snm, snv)]
    sg3 = sg.reshape(DEPTH, SMALL_ROWS, D_MODEL)
    taps_grad_full = jnp.stack([sg3[:, 5, :ATTN_WIDTH], sg3[:, 5, ATTN_WIDTH:], sg3[:, 6, :ATTN_WIDTH]], axis=1)
    taps_grad = lax.dynamic_slice_in_dim(taps_grad_full, me * conv_channels, conv_channels, axis=2)

    def flat(a):
        return a.reshape(DEPTH * 3, conv_channels)

    td, tnm, tnv = _adamw_plain(flat(taps_grad), flat(conv_w), flat(m_conv_w), flat(v_conv_w), "adamw_taps")
    taps = [taps_grad] + [a.reshape(conv_w.shape) for a in (td, tnm, tnv)]

    def leaves(o):
        s = small[o]
        return (s["pre_mix_norm"], big["in"][o], taps[o], s["attn_out_norm"], s["conv_out_norm"], big["out"][o], s["post_mix_norm"],
                s["pre_ffn_norm"], hidden_major(big["gu"][o]), big["dn"][o], s["post_ffn_norm"])

    return (loss, dx[None], *leaves(0), *leaves(1), *leaves(2), *leaves(3))
```

```python
import math

import jax
import jax.numpy as jnp
from jax import lax
from jax.experimental import pallas as pl
from jax.experimental.pallas import tpu as pltpu

F32 = jnp.float32
BF16 = jnp.bfloat16
MESH = pl.DeviceIdType.MESH

SEQ = 4096
D_MODEL = 1024
DEPTH = 4
N_DEV = 8
ATTN_WIDTH = 512
IN_PROJ_WIDTH = 3072
FFN_HIDDEN = 2816
FFN_BLOCK = 2 * FFN_HIDDEN // N_DEV
W_IN_BLOCK = IN_PROJ_WIDTH // N_DEV
W_OUT_BLOCK = D_MODEL // N_DEV
W_DOWN_BLOCK = FFN_HIDDEN // N_DEV
HEAD_DIM = 64
ROPE_DIM = 16
ROPE_THETA = 500000.0
DILATIONS = (1, 4, 16)
HALF_WINDOW = 64
RMS_EPS = 1e-6
NEG_INF = -1e30
LANES = 128
Q_BLOCK = 128
K_WINDOW = Q_BLOCK + 2 * HALF_WINDOW
PERM_CHUNK = 256
ROW_TILE = 512
FFN_TILE = 256
WGRAD_TILE = 1024
ADAM_LR, ADAM_B1, ADAM_B2, ADAM_EPS, ADAM_WD, ADAM_STEP = 0.001, 0.9, 0.999, 1e-08, 0.01, 10
MIB = 1024 * 1024

WEIGHT_KINDS = ("in", "out", "gu", "dn")
FULL_SHAPES = {"in": (D_MODEL, IN_PROJ_WIDTH), "out": (D_MODEL, D_MODEL), "gu": (N_DEV, FFN_BLOCK, D_MODEL), "dn": (FFN_HIDDEN, D_MODEL)}
SHARD_SHAPES = {"in": (D_MODEL, W_IN_BLOCK), "out": (W_OUT_BLOCK, D_MODEL), "gu": (FFN_BLOCK, D_MODEL), "dn": (W_DOWN_BLOCK, D_MODEL)}
ANY = pl.BlockSpec(memory_space=pl.ANY)


def _sds(shape, dtype):
    return jax.ShapeDtypeStruct(shape, dtype)


def _rows(width, tile=ROW_TILE):
    return pl.BlockSpec((tile, width), lambda i: (i, 0))


def _frows(width):
    return _rows(width, FFN_TILE)


def _const(shape):
    return pl.BlockSpec(shape, lambda i: (0,) * len(shape))


def _const1(shape):
    return pl.BlockSpec(shape, lambda i: (0,) * len(shape), pipeline_mode=pl.Buffered(1))


def _halo_prev(width, tile=ROW_TILE):
    return pl.BlockSpec((8, width), lambda i: (jnp.maximum(i * (tile // 8) - 1, 0), 0))


def _halo_next(width, tile=ROW_TILE):
    return pl.BlockSpec((8, width), lambda i: (jnp.minimum((i + 1) * (tile // 8), SEQ // 8 - 1), 0))


def _rms(x):
    r = lax.rsqrt(jnp.mean(x * x, axis=-1, keepdims=True) + RMS_EPS)
    return x * r, r


def _rms_bwd(dn, y, r):
    return r * (dn - y * jnp.mean(dn * y, axis=-1, keepdims=True))


def _dot(a, b):
    return jnp.dot(a, b, preferred_element_type=F32)


def _dot_nt(a, b):
    return lax.dot_general(a, b, (((1,), (1,)), ((), ())), preferred_element_type=F32)


def _dot_tn(a, b):
    return lax.dot_general(a, b, (((0,), (0,)), ((), ())), preferred_element_type=F32)


def _place():
    return lax.axis_index("x"), lax.axis_index("y"), lax.axis_index("c")


def _block_of(px, py, pc):
    return 4 * px + 2 * py + pc


def _weight_block(ref, kind, blk):
    if kind == "in":
        return ref.at[:, pl.ds(blk * W_IN_BLOCK, W_IN_BLOCK)]
    if kind == "out":
        return ref.at[pl.ds(blk * W_OUT_BLOCK, W_OUT_BLOCK), :]
    if kind == "gu":
        return ref.at[blk]
    return ref.at[pl.ds(blk * W_DOWN_BLOCK, W_DOWN_BLOCK), :]


def _dma_semaphores(n):
    return [pltpu.SemaphoreType.DMA((n, 7)), pltpu.SemaphoreType.DMA((n, 7)), pltpu.SemaphoreType.DMA((n,))]


class _Gather:
    def __init__(self, kinds, shards):
        self.kinds, self.operands = tuple(kinds), list(shards)
        self.tag = "gather_" + "_".join(kinds)
        self.out_shape = [_sds(FULL_SHAPES[k], BF16) for k in kinds]
        self.scratch = _dma_semaphores(len(kinds))

    def _parties(self):
        x, y, c = _place()
        return (x, y, c), (x, y, 1 - c), [(1 - x, y), (x, 1 - y), (1 - x, 1 - y)], c

    def _copy(self, outs, sems, w, k, block, to, src=None):
        dst = _weight_block(outs[w], self.kinds[w], _block_of(*block))
        return pltpu.make_async_remote_copy(src_ref=dst if src is None else src, dst_ref=dst, send_sem=sems[0].at[w, k],
                                            recv_sem=sems[1].at[w, k], device_id=to, device_id_type=MESH)

    def _own(self, srcs, outs, sems, w, me):
        return pltpu.make_async_copy(srcs[w], _weight_block(outs[w], self.kinds[w], _block_of(*me)), sems[2].at[w])

    def _first(self, srcs, outs, sems, w):
        me, sibling, chips, c = self._parties()
        return [self._copy(outs, sems, w, 0, me, sibling, src=srcs[w])] + [
            self._copy(outs, sems, w, 1 + j, me, (*chip, c), src=srcs[w]) for j, chip in enumerate(chips)]

    def start(self, srcs, outs, sems):
        me = self._parties()[0]
        for w in range(len(self.kinds)):
            self._own(srcs, outs, sems, w, me).start()
            for cp in self._first(srcs, outs, sems, w):
                cp.start()

    def forward(self, srcs, outs, sems):
        me, sibling, chips, c = self._parties()
        for j, chip in enumerate(chips):
            for w in range(len(self.kinds)):
                self._copy(outs, sems, w, 1 + j, (*chip, c), me).wait_recv()
                self._copy(outs, sems, w, 4 + j, (*chip, c), sibling).start()

    def finish(self, srcs, outs, sems):
        me, sibling, chips, c = self._parties()
        for w in range(len(self.kinds)):
            self._copy(outs, sems, w, 0, sibling, me).wait_recv()
            for j, chip in enumerate(chips):
                self._copy(outs, sems, w, 4 + j, (*chip, 1 - c), me).wait_recv()
        for w in range(len(self.kinds)):
            for cp in self._first(srcs, outs, sems, w):
                cp.wait_send()
            for j, chip in enumerate(chips):
                self._copy(outs, sems, w, 4 + j, (*chip, c), sibling).wait_send()
            self._own(srcs, outs, sems, w, me).wait()


def _peers(x, y, c):
    return [(x ^ a, y ^ b, c ^ e) for a in (0, 1) for b in (0, 1) for e in (0, 1) if (a, b, e) != (0, 0, 0)]


class _Scatter:
    def __init__(self, kinds, grads):
        self.kinds, self.operands = tuple(kinds), list(grads)
        self.tag = "scatter_" + "_".join(kinds)
        self.out_shape = [_sds((N_DEV,) + SHARD_SHAPES[k], BF16) for k in kinds]
        self.scratch = _dma_semaphores(len(kinds))

    def _copies(self, srcs, outs, sems):
        x, y, c = _place()
        me = _block_of(x, y, c)
        copies = []
        for w, kind in enumerate(self.kinds):
            copies.append(pltpu.make_async_copy(_weight_block(srcs[w], kind, me), outs[w].at[me], sems[2].at[w]))
            for k, peer in enumerate(_peers(x, y, c)):
                copies.append(pltpu.make_async_remote_copy(
                    src_ref=_weight_block(srcs[w], kind, _block_of(*peer)), dst_ref=outs[w].at[me],
                    send_sem=sems[0].at[w, k], recv_sem=sems[1].at[w, k], device_id=peer, device_id_type=MESH))
        return copies

    def start(self, srcs, outs, sems):
        for cp in self._copies(srcs, outs, sems):
            cp.start()

    def forward(self, srcs, outs, sems):
        pass

    def finish(self, srcs, outs, sems):
        for cp in self._copies(srcs, outs, sems):
            cp.wait()


def _in_hbm(a):
    return pltpu.with_memory_space_constraint(a, pltpu.HBM) if a.size * a.dtype.itemsize >= MIB else a


def _call(body, args, *, grid, in_specs, out_specs, out_shape, scratch_shapes=(), vmem_mib=None, name, comm=None):
    kwargs = {} if vmem_mib is None else dict(compiler_params=pltpu.CompilerParams(vmem_limit_bytes=vmem_mib * MIB))
    in_specs, out_specs, out_shape, scratch_shapes = list(in_specs), list(out_specs), list(out_shape), list(scratch_shapes)
    args = [_in_hbm(a) for a in args]
    if comm is None:
        res = pl.pallas_call(body, grid=grid, in_specs=in_specs, out_specs=out_specs, out_shape=out_shape,
                             scratch_shapes=scratch_shapes, name=name, **kwargs)(*args)
        return list(res), None
    n_in, n_out, n_scr = len(in_specs), len(out_specs), len(scratch_shapes)
    c_in, c_out = len(comm.operands), len(comm.out_shape)
    last = math.prod(grid) - 1

    def carried(*refs):
        cuts = [n_in, c_in, n_out, c_out, n_scr]
        parts, at = [], 0
        for n in cuts:
            parts.append(refs[at:at + n])
            at += n
        ins, c_ins, outs, c_outs, scr = parts
        sems = refs[at:]
        step = pl.program_id(0)
        for axis in range(1, len(grid)):
            step = step * grid[axis] + pl.program_id(axis)

        @pl.when(step == 0)
        def _():
            comm.start(c_ins, c_outs, sems)

        @pl.when(step == last)
        def _():
            comm.forward(c_ins, c_outs, sems)

        body(*ins, *outs, *scr)

        @pl.when(step == last)
        def _():
            comm.finish(c_ins, c_outs, sems)

    res = pl.pallas_call(carried, grid=grid, in_specs=in_specs + [ANY] * c_in, out_specs=out_specs + [ANY] * c_out,
                         out_shape=out_shape + comm.out_shape, scratch_shapes=scratch_shapes + comm.scratch,
                         name=name + "_" + comm.tag, **kwargs)(*args, *comm.operands)
    return list(res[:n_out]), list(res[n_out:])


def _comm_only(comm, name):
    def body(*refs):
        n_in, n_out = len(comm.operands), len(comm.out_shape)
        srcs, outs, sems = refs[:n_in], refs[n_in:n_in + n_out], refs[n_in + n_out:]
        comm.start(srcs, outs, sems)
        comm.forward(srcs, outs, sems)
        comm.finish(srcs, outs, sems)

    return pl.pallas_call(body, in_specs=[ANY] * len(comm.operands), out_specs=[ANY] * len(comm.out_shape), out_shape=comm.out_shape,
                          scratch_shapes=comm.scratch, name=name)(*comm.operands)


def _allgather_small(v, name):
    def body(v_ref, o_ref, send, recv, local):
        x, y, c = _place()
        me = _block_of(x, y, c)
        mine = pltpu.make_async_copy(v_ref, o_ref.at[me], local)
        mine.start()
        copies = [mine]
        for k, peer in enumerate(_peers(x, y, c)):
            cp = pltpu.make_async_remote_copy(src_ref=v_ref, dst_ref=o_ref.at[me], send_sem=send.at[k], recv_sem=recv.at[k],
                                              device_id=peer, device_id_type=MESH)
            cp.start()
            copies.append(cp)
        for cp in copies:
            cp.wait()

    vm = pl.BlockSpec(memory_space=pltpu.VMEM)
    return pl.pallas_call(
        body, in_specs=[vm], out_specs=vm, out_shape=_sds((N_DEV,) + v.shape, F32),
        scratch_shapes=[pltpu.SemaphoreType.DMA((7,)), pltpu.SemaphoreType.DMA((7,)), pltpu.SemaphoreType.DMA],
        name=name)(v)


def _rope_tables(pos_col, freq_row):
    def body(p_ref, f_ref, c_ref, sa_ref, sb_ref):
        ang = p_ref[...].astype(F32) * f_ref[...]
        lane = lax.broadcasted_iota(jnp.int32, ang.shape, 1) % HEAD_DIM
        cos, sin = jnp.cos(ang), jnp.sin(ang)
        c_ref[...] = jnp.where(lane < ROPE_DIM, cos, 1.0)
        sa_ref[...] = jnp.where(lane < ROPE_DIM // 2, -sin, 0.0)
        sb_ref[...] = jnp.where((lane >= ROPE_DIM // 2) & (lane < ROPE_DIM), sin, 0.0)

    return pl.pallas_call(
        body, grid=(SEQ // ROW_TILE,), in_specs=[_rows(1), _const((1, LANES))], out_specs=[_rows(LANES)] * 3,
        out_shape=[_sds((SEQ, LANES), F32)] * 3, name="rope_tables")(pos_col, freq_row)


def _rotate(t, c, sa, sb):
    parts = []
    for g in range(ATTN_WIDTH // LANES):
        tg = t[:, g * LANES:(g + 1) * LANES]
        parts.append(tg * c + pltpu.roll(tg, LANES - 8, axis=1) * sa + pltpu.roll(tg, 8, axis=1) * sb)
    return jnp.concatenate(parts, axis=1)


def _rotate_transposed(dt, c, sa, sb):
    parts = []
    for g in range(ATTN_WIDTH // LANES):
        dg = dt[:, g * LANES:(g + 1) * LANES]
        parts.append(dg * c + pltpu.roll(dg * sa, 8, axis=1) + pltpu.roll(dg * sb, LANES - 8, axis=1))
    return jnp.concatenate(parts, axis=1)


def _inproj_fwd(x, g_pre, w_in, tc, tsa, tsb):
    def body(x_ref, g_ref, w_ref, c_ref, sa_ref, sb_ref, q_ref, k_ref, v_ref, u_ref, b_ref, cc_ref):
        y, _ = _rms(x_ref[...])
        h = (y * g_ref[...]).astype(BF16)

        def proj(n):
            return _dot(h, w_ref[:, n * ATTN_WIDTH:(n + 1) * ATTN_WIDTH])

        c, sa, sb = c_ref[...], sa_ref[...], sb_ref[...]
        q_ref[...] = (_rotate(proj(0), c, sa, sb) * (HEAD_DIM ** -0.5)).astype(BF16)
        k_ref[...] = _rotate(proj(1), c, sa, sb).astype(BF16)
        v_ref[...] = proj(2).astype(BF16)
        u_ref[...] = proj(3)
        b_ref[...] = proj(4)
        cc_ref[...] = proj(5)

    a = ATTN_WIDTH
    return _call(
        body, (x, g_pre, w_in, tc, tsa, tsb), grid=(SEQ // ROW_TILE,),
        in_specs=[_rows(D_MODEL), _const((1, D_MODEL)), _const1((D_MODEL, IN_PROJ_WIDTH)), _rows(LANES), _rows(LANES), _rows(LANES)],
        out_specs=[_rows(a)] * 6, out_shape=[_sds((SEQ, a), BF16)] * 3 + [_sds((SEQ, a), F32)] * 3,
        vmem_mib=40, name="inproj_fwd")[0]


def _head_masks():
    lane = lax.broadcasted_iota(jnp.int32, (1, LANES), 1)
    first = lane < HEAD_DIM
    return first, first.astype(F32), 1.0 - first.astype(F32)


def _perm_chunks(dil):
    length = SEQ // dil
    out = []
    for r in range(dil):
        for c0 in range(0, length, PERM_CHUNK):
            chunk = (r * length + c0) // PERM_CHUNK
            rows = pl.ds(c0, PERM_CHUNK) if dil == 1 else pl.ds(r + dil * c0, PERM_CHUNK, stride=dil)
            out.append((chunk, rows))
    return out


def _chunk(c, offset=0):
    return pl.ds(offset + c * PERM_CHUNK, PERM_CHUNK)


def _write_band_bias(bias_ref):
    rr = lax.broadcasted_iota(jnp.int32, (Q_BLOCK, K_WINDOW), 0)
    cc = lax.broadcasted_iota(jnp.int32, (Q_BLOCK, K_WINDOW), 1)
    band = (cc >= rr) & (cc - rr <= 2 * HALF_WINDOW)
    bias_ref[0] = jnp.where(band, 0.0, NEG_INF)
    bias_ref[1] = jnp.where(band & (cc >= HALF_WINDOW), 0.0, NEG_INF)
    bias_ref[2] = jnp.where(band & (cc < Q_BLOCK + HALF_WINDOW), 0.0, NEG_INF)


def _band_bias_index(m0, length):
    return jnp.where(m0 % length == 0, 1, 0) + jnp.where((m0 + Q_BLOCK) % length == 0, 2, 0)


def _zero_key_padding(bufs):
    pad = jnp.zeros((HALF_WINDOW, LANES), BF16)
    for buf in bufs:
        buf[pl.ds(0, HALF_WINDOW), :] = pad
        buf[pl.ds(SEQ + HALF_WINDOW, HALF_WINDOW), :] = pad


def _attn_fwd(q, k, v, comm=None):
    group = 8

    def body(q_ref, k_ref, v_ref, o_ref, lse_ref, q32, k32, v32, qa, qb, kp, vp, accp, mlp,
             acc0, acc1, acc2, ml0, ml1, ml2, bias_ref):
        first, mask_a, mask_b = _head_masks()
        low = lax.broadcasted_iota(jnp.int32, (1, LANES), 1) % HEAD_DIM < HEAD_DIM // 2
        _zero_key_padding((kp, vp))
        _write_band_bias(bias_ref)
        q32[...] = q_ref[...].astype(F32)
        k32[...] = k_ref[...].astype(F32)
        v32[...] = v_ref[...].astype(F32)
        natural = ((acc0, ml0), (acc1, ml1), (acc2, ml2))

        for branch, dil in enumerate(DILATIONS):
            length = SEQ // dil
            assert length >= 2 * Q_BLOCK
            chunks = _perm_chunks(dil)
            for c, rows in chunks:
                val = q32[rows, :]
                qa[_chunk(c), :] = (val * mask_a).astype(BF16)
                qb[_chunk(c), :] = (val * mask_b).astype(BF16)
                kp[_chunk(c, HALF_WINDOW), :] = k32[rows, :].astype(BF16)
                vp[_chunk(c, HALF_WINDOW), :] = v32[rows, :].astype(BF16)
            acc_dst, ml_dst = natural[branch] if dil == 1 else (accp, mlp)

            def blocks(i, carry, length=length, acc_dst=acc_dst, ml_dst=ml_dst):
                base = pl.multiple_of(i * (group * Q_BLOCK), group * Q_BLOCK)
                starts = [base + g * Q_BLOCK for g in range(group)]
                scores = [[_dot_nt(qx[pl.ds(m0, Q_BLOCK), :], kp[pl.ds(m0, K_WINDOW), :]) for qx in (qa, qb)] for m0 in starts]
                probs = []
                for m0, pair in zip(starts, scores):
                    bias = bias_ref[_band_bias_index(m0, length)]
                    stats, ps = [], []
                    for s in pair:
                        s = s + bias
                        m = jnp.max(s, axis=1, keepdims=True)
                        p = jnp.exp(s - m)
                        stats.append(jnp.where(low, m, jnp.sum(p, axis=1, keepdims=True)))
                        ps.append(p.astype(BF16))
                    ml_dst[pl.ds(m0, Q_BLOCK), :] = jnp.where(first, stats[0], stats[1])
                    probs.append(ps)
                for m0, ps in zip(starts, probs):
                    vw = vp[pl.ds(m0, K_WINDOW), :]
                    acc_dst[pl.ds(m0, Q_BLOCK), :] = jnp.where(first, _dot(ps[0], vw), _dot(ps[1], vw))
                return carry

            lax.fori_loop(0, SEQ // (group * Q_BLOCK), blocks, 0)

            if dil > 1:
                for c, rows in chunks:
                    natural[branch][0][rows, :] = accp[_chunk(c), :]
                    natural[branch][1][rows, :] = mlp[_chunk(c), :]

        for c in range(SEQ // PERM_CHUNK):
            packed = [ml[_chunk(c), :] for _, ml in natural]
            ms = [jnp.where(low, ml, pltpu.roll(ml, HEAD_DIM // 2, axis=1)) for ml in packed]
            ls = [jnp.where(low, pltpu.roll(ml, LANES - HEAD_DIM // 2, axis=1), ml) for ml in packed]
            m_all = jnp.maximum(jnp.maximum(ms[0], ms[1]), ms[2])
            es = [jnp.exp(m - m_all) for m in ms]
            z = ls[0] * es[0] + ls[1] * es[1] + ls[2] * es[2]
            num = natural[0][0][_chunk(c), :] * es[0] + natural[1][0][_chunk(c), :] * es[1] + natural[2][0][_chunk(c), :] * es[2]
            o_ref[_chunk(c), :] = num / z
            lse_ref[_chunk(c), :] = m_all + jnp.log(z)

    col = pl.BlockSpec((SEQ, LANES), lambda h: (0, h))
    padded = SEQ + 2 * HALF_WINDOW
    return _call(
        body, (q, k, v), grid=(ATTN_WIDTH // LANES,), in_specs=[col] * 3, out_specs=[col] * 2,
        out_shape=[_sds((SEQ, ATTN_WIDTH), F32)] * 2,
        scratch_shapes=[pltpu.VMEM((SEQ, LANES), F32)] * 3 + [pltpu.VMEM((SEQ, LANES), BF16)] * 2
        + [pltpu.VMEM((padded, LANES), BF16)] * 2 + [pltpu.VMEM((SEQ, LANES), F32)] * 8
        + [pltpu.VMEM((3, Q_BLOCK, K_WINDOW), F32)],
        vmem_mib=52, name="attn_fwd", comm=comm)


def _shifted(t, before, after, i):
    tile = t.shape[0]
    row = lax.broadcasted_iota(jnp.int32, (tile, 1), 0)
    before = jnp.where(i > 0, before, 0.0)
    after = jnp.where(i < SEQ // tile - 1, after, 0.0)
    return (jnp.where(row == 0, before, pltpu.roll(t, 1, axis=0)),
            jnp.where(row == tile - 1, after, pltpu.roll(t, tile - 1, axis=0)))


def _conv_parts(u, c, u_prev, c_prev, u_next, c_next, cw, i):
    t = c * u
    t_prev, t_next = _shifted(t, c_prev[7:8, :] * u_prev[7:8, :], c_next[0:1, :] * u_next[0:1, :], i)
    s = cw[0:1, :] * t_prev + cw[1:2, :] * t + cw[2:3, :] * t_next
    return t, t_prev, t_next, s


def _mix_fwd(attn, u, b, c, conv_w, g_attn, g_conv, w_out, x, g_post):
    def body(a_ref, u_ref, b_ref, c_ref, up_ref, cp_ref, un_ref, cn_ref, cw_ref, ga_ref, gc_ref, w_ref, x_ref, gp_ref,
             x1_ref, mg_ref, mix_ref):
        i = pl.program_id(0)
        _, _, _, s = _conv_parts(u_ref[...], c_ref[...], up_ref[...], cp_ref[...], un_ref[...], cn_ref[...], cw_ref[...], i)
        ya, _ = _rms(a_ref[...])
        yc, _ = _rms(b_ref[...] * s)
        merged = jnp.concatenate([ya * ga_ref[...], yc * gc_ref[...]], axis=1).astype(BF16)
        mix = _dot(merged, w_ref[...])
        ym, _ = _rms(mix)
        mg_ref[...] = merged
        mix_ref[...] = mix
        x1_ref[...] = x_ref[...] + ym * gp_ref[...]

    a = ATTN_WIDTH
    return _call(
        body, (attn, u, b, c, u, c, u, c, conv_w, g_attn, g_conv, w_out, x, g_post), grid=(SEQ // ROW_TILE,),
        in_specs=[_rows(a)] * 4 + [_halo_prev(a)] * 2 + [_halo_next(a)] * 2
        + [_const((3, a)), _const((1, a)), _const((1, a)), _const1((D_MODEL, D_MODEL)), _rows(D_MODEL), _const((1, D_MODEL))],
        out_specs=[_rows(D_MODEL)] * 3,
        out_shape=[_sds((SEQ, D_MODEL), F32), _sds((SEQ, D_MODEL), BF16), _sds((SEQ, D_MODEL), F32)],
        vmem_mib=40, name="mix_fwd")[0]


def _gu_spec():
    return pl.BlockSpec((N_DEV, FFN_TILE, FFN_BLOCK), lambda i: (0, i, 0))


def _ffn_fwd(x1, g_pre, w_gu, w_dn, g_post, comm=None):
    def body(x_ref, g_ref, wgu_ref, wdn_ref, gp_ref, x2_ref, gu_ref, f_ref):
        x1 = x_ref[...]
        y, _ = _rms(x1)
        h = (y * g_ref[...]).astype(BF16)
        f = jnp.zeros((FFN_TILE, D_MODEL), F32)
        for j in range(N_DEV // 2):
            gate = _dot_nt(h, wgu_ref[j])
            up = _dot_nt(h, wgu_ref[j + N_DEV // 2])
            gu_ref[j] = gate.astype(BF16)
            gu_ref[j + N_DEV // 2] = up.astype(BF16)
            act = (gate * jax.nn.sigmoid(gate) * up).astype(BF16)
            f = f + _dot(act, wdn_ref[pl.ds(j * FFN_BLOCK, FFN_BLOCK), :])
        yf, _ = _rms(f)
        f_ref[...] = f
        x2_ref[...] = x1 + yf * gp_ref[...]

    return _call(
        body, (x1, g_pre, w_gu, w_dn, g_post), grid=(SEQ // FFN_TILE,),
        in_specs=[_frows(D_MODEL), _const((1, D_MODEL)), _const1((N_DEV, FFN_BLOCK, D_MODEL)), _const1((FFN_HIDDEN, D_MODEL)),
                  _const((1, D_MODEL))],
        out_specs=[_frows(D_MODEL), _gu_spec(), _frows(D_MODEL)],
        out_shape=[_sds((SEQ, D_MODEL), F32), _sds((N_DEV, SEQ, FFN_BLOCK), BF16), _sds((SEQ, D_MODEL), F32)],
        vmem_mib=48, name="ffn_fwd", comm=comm)


def _loss_and_grad(y, target):
    n_tiles = SEQ // ROW_TILE

    def body(y_ref, t_ref, loss_ref, dy_ref, acc):
        i = pl.program_id(0)
        err = y_ref[...] - t_ref[...]
        dy_ref[...] = err * (1.0 / D_MODEL)

        @pl.when(i == 0)
        def _():
            acc[...] = jnp.zeros_like(acc)

        acc[...] += jnp.sum(err * err, axis=0, keepdims=True)

        @pl.when(i == n_tiles - 1)
        def _():
            loss_ref[...] = jnp.sum(acc[...], axis=1, keepdims=True) * (0.5 / D_MODEL)

    return _call(
        body, (y, target), grid=(n_tiles,), in_specs=[_rows(D_MODEL)] * 2, out_specs=[_const((1, 1)), _rows(D_MODEL)],
        out_shape=[_sds((1, 1), F32), _sds((SEQ, D_MODEL), F32)], scratch_shapes=[pltpu.VMEM((1, D_MODEL), F32)],
        name="loss_and_grad")[0]


def _accumulate(ref, value, i):
    @pl.when(i == 0)
    def _():
        ref[...] = value

    @pl.when(i > 0)
    def _():
        ref[...] += value


def _colsum(v):
    return jnp.sum(v, axis=0, keepdims=True)


def _ffn_bwd(dx2, f, x1, gu, w_gu, w_dn, g_post, g_pre, comm=None):
    half = N_DEV // 2

    def body(dx2_ref, f_ref, x1_ref, gu_ref, wgu_ref, wdn_ref, gpost_ref, gpre_ref,
             dx1_ref, df_ref, act_ref, dgu_ref, h_ref, dgpost_ref, dgpre_ref):
        i = pl.program_id(0)
        dx2 = dx2_ref[...]
        yf, rf = _rms(f_ref[...])
        _accumulate(dgpost_ref, _colsum(dx2 * yf), i)
        df = _rms_bwd(dx2 * gpost_ref[...], yf, rf).astype(BF16)
        df_ref[...] = df
        dh = jnp.zeros((FFN_TILE, D_MODEL), F32)
        for j in range(half):
            dact = _dot_nt(df, wdn_ref[pl.ds(j * FFN_BLOCK, FFN_BLOCK), :])
            gate = gu_ref[j].astype(F32)
            up = gu_ref[j + half].astype(F32)
            sig = jax.nn.sigmoid(gate)
            silu = gate * sig
            act_ref[j] = (silu * up).astype(BF16)
            dgate = (dact * up * (sig * (1.0 + gate * (1.0 - sig)))).astype(BF16)
            dup = (dact * silu).astype(BF16)
            dgu_ref[j] = dgate
            dgu_ref[j + half] = dup
            dh = dh + _dot(dgate, wgu_ref[j]) + _dot(dup, wgu_ref[j + half])
        y1, r1 = _rms(x1_ref[...])
        h_ref[...] = (y1 * gpre_ref[...]).astype(BF16)
        _accumulate(dgpre_ref, _colsum(dh * y1), i)
        dx1_ref[...] = dx2 + _rms_bwd(dh * gpre_ref[...], y1, r1)

    act_spec = pl.BlockSpec((half, FFN_TILE, FFN_BLOCK), lambda i: (0, i, 0))
    return _call(
        body, (dx2, f, x1, gu, w_gu, w_dn, g_post, g_pre), grid=(SEQ // FFN_TILE,),
        in_specs=[_frows(D_MODEL)] * 3 + [_gu_spec(), _const1((N_DEV, FFN_BLOCK, D_MODEL)), _const1((FFN_HIDDEN, D_MODEL)),
                                          _const((1, D_MODEL)), _const((1, D_MODEL))],
        out_specs=[_frows(D_MODEL), _frows(D_MODEL), act_spec, _gu_spec(), _frows(D_MODEL), _const((1, D_MODEL)), _const((1, D_MODEL))],
        out_shape=[_sds((SEQ, D_MODEL), F32), _sds((SEQ, D_MODEL), BF16), _sds((half, SEQ, FFN_BLOCK), BF16),
                   _sds((N_DEV, SEQ, FFN_BLOCK), BF16), _sds((SEQ, D_MODEL), BF16), _sds((1, D_MODEL), F32), _sds((1, D_MODEL), F32)],
        vmem_mib=52, name="ffn_bwd", comm=comm)


def _wgrad(a, g, a_spec, g_spec, out_shape, out_spec, acc_shape, n_blocks, name, comm=None):
    n_tiles = SEQ // WGRAD_TILE

    def body(a_ref, g_ref, o_ref, acc):
        s = pl.program_id(1)
        av = a_ref[...].reshape(WGRAD_TILE, a_ref.shape[-1])
        gv = g_ref[...].reshape(WGRAD_TILE, g_ref.shape[-1])
        part = _dot_tn(av, gv)

        @pl.when(s == 0)
        def _():
            acc[...] = part

        @pl.when(s > 0)
        def _():
            acc[...] += part

        @pl.when(s == n_tiles - 1)
        def _():
            o_ref[...] = acc[...].astype(BF16).reshape(o_ref.shape)

    res, landed = _call(body, (a, g), grid=(n_blocks, n_tiles), in_specs=[a_spec, g_spec], out_specs=[out_spec],
                        out_shape=[_sds(out_shape, BF16)], scratch_shapes=[pltpu.VMEM(acc_shape, F32)], vmem_mib=40, name=name, comm=comm)
    return res[0], landed


def _wgrad_gate_up(h, dgu):
    t = WGRAD_TILE
    return _wgrad(dgu, h, pl.BlockSpec((1, t, FFN_BLOCK), lambda j, s: (j, s, 0)), pl.BlockSpec((t, D_MODEL), lambda j, s: (s, 0)),
                  (N_DEV, FFN_BLOCK, D_MODEL), pl.BlockSpec((1, FFN_BLOCK, D_MODEL), lambda j, s: (j, 0, 0)),
                  (FFN_BLOCK, D_MODEL), N_DEV, "wgrad_gate_up")[0]


def _wgrad_down(act, df):
    t = WGRAD_TILE
    return _wgrad(act, df, pl.BlockSpec((1, t, FFN_BLOCK), lambda j, s: (j, s, 0)), pl.BlockSpec((t, D_MODEL), lambda j, s: (s, 0)),
                  (FFN_HIDDEN, D_MODEL), pl.BlockSpec((FFN_BLOCK, D_MODEL), lambda j, s: (j, 0)),
                  (FFN_BLOCK, D_MODEL), N_DEV // 2, "wgrad_down")[0]


def _wgrad_out(merged, dmix):
    t = WGRAD_TILE
    return _wgrad(merged, dmix, pl.BlockSpec((t, D_MODEL), lambda j, s: (s, 0)), pl.BlockSpec((t, 512), lambda j, s: (s, j)),
                  (D_MODEL, D_MODEL), pl.BlockSpec((D_MODEL, 512), lambda j, s: (0, j)), (D_MODEL, 512), 2, "wgrad_out")[0]


def _wgrad_in(h, dproj, comm=None):
    t = WGRAD_TILE
    return _wgrad(h, dproj, pl.BlockSpec((t, D_MODEL), lambda j, s: (s, 0)), pl.BlockSpec((t, 512), lambda j, s: (s, j)),
                  (D_MODEL, IN_PROJ_WIDTH), pl.BlockSpec((D_MODEL, 512), lambda j, s: (0, j)), (D_MODEL, 512),
                  IN_PROJ_WIDTH // 512, "wgrad_in", comm=comm)


def _mix_bwd(dx1, mix, attn, u, b, c, conv_w, g_attn, g_conv, g_post, w_out):
    def body(dx1_ref, mix_ref, a_ref, u_ref, b_ref, c_ref, up_ref, cp_ref, un_ref, cn_ref, cw_ref, ga_ref, gc_ref, gp_ref, w_ref,
             dmix_ref, da_ref, ds_ref, db_ref, dgp_ref, dga_ref, dgc_ref):
        i = pl.program_id(0)
        dx1 = dx1_ref[...]
        ym, rm = _rms(mix_ref[...])
        _accumulate(dgp_ref, _colsum(dx1 * ym), i)
        dmix = _rms_bwd(dx1 * gp_ref[...], ym, rm).astype(BF16)
        dmix_ref[...] = dmix
        dmerged = _dot_nt(dmix, w_ref[...])
        dna, dnc = dmerged[:, :ATTN_WIDTH], dmerged[:, ATTN_WIDTH:]
        ya, ra = _rms(a_ref[...])
        _accumulate(dga_ref, _colsum(dna * ya), i)
        da_ref[...] = _rms_bwd(dna * ga_ref[...], ya, ra)
        _, _, _, s = _conv_parts(u_ref[...], c_ref[...], up_ref[...], cp_ref[...], un_ref[...], cn_ref[...], cw_ref[...], i)
        gate_b = b_ref[...]
        yc, rc = _rms(gate_b * s)
        _accumulate(dgc_ref, _colsum(dnc * yc), i)
        dy = _rms_bwd(dnc * gc_ref[...], yc, rc)
        db_ref[...] = dy * s
        ds_ref[...] = dy * gate_b

    a = ATTN_WIDTH
    return _call(
        body, (dx1, mix, attn, u, b, c, u, c, u, c, conv_w, g_attn, g_conv, g_post, w_out), grid=(SEQ // ROW_TILE,),
        in_specs=[_rows(D_MODEL)] * 2 + [_rows(a)] * 4 + [_halo_prev(a)] * 2 + [_halo_next(a)] * 2
        + [_const((3, a)), _const((1, a)), _const((1, a)), _const((1, D_MODEL)), _const1((D_MODEL, D_MODEL))],
        out_specs=[_rows(D_MODEL)] + [_rows(a)] * 3 + [_const((1, D_MODEL)), _const((1, a)), _const((1, a))],
        out_shape=[_sds((SEQ, D_MODEL), BF16)] + [_sds((SEQ, a), F32)] * 3 + [_sds((1, D_MODEL), F32), _sds((1, a), F32), _sds((1, a), F32)],
        vmem_mib=40, name="mix_bwd")[0]


def _attn_bwd(q, k, v, do, o, lse, comm=None):
    group = 8

    def body(q_ref, k_ref, v_ref, do_ref, o_ref, lse_ref, dq_ref, dk_ref, dv_ref,
             q32, k32, v32, qa, qb, doa, dob, kp, vp, lsep, dlp, dnat, dqp, dkp, dvp, bias_ref):
        first, mask_a, mask_b = _head_masks()
        _zero_key_padding((kp, vp))
        _write_band_bias(bias_ref)
        q32[...] = q_ref[...].astype(F32)
        k32[...] = k_ref[...].astype(F32)
        v32[...] = v_ref[...].astype(F32)
        for c in range(SEQ // PERM_CHUNK):
            prod = do_ref[_chunk(c), :] * o_ref[_chunk(c), :]
            d_a = jnp.sum(prod * mask_a, axis=1, keepdims=True)
            d_b = jnp.sum(prod * mask_b, axis=1, keepdims=True)
            dnat[_chunk(c), :] = jnp.where(first, d_a, d_b)

        for step, dil in enumerate(DILATIONS[1:] + DILATIONS[:1]):
            length = SEQ // dil
            assert length >= 2 * Q_BLOCK
            chunks = _perm_chunks(dil)
            for c, rows in chunks:
                val = q32[rows, :]
                qa[_chunk(c), :] = (val * mask_a).astype(BF16)
                qb[_chunk(c), :] = (val * mask_b).astype(BF16)
                val = do_ref[rows, :]
                doa[_chunk(c), :] = (val * mask_a).astype(BF16)
                dob[_chunk(c), :] = (val * mask_b).astype(BF16)
                kp[_chunk(c, HALF_WINDOW), :] = k32[rows, :].astype(BF16)
                vp[_chunk(c, HALF_WINDOW), :] = v32[rows, :].astype(BF16)
                lsep[_chunk(c), :] = lse_ref[rows, :]
                dlp[_chunk(c), :] = dnat[rows, :]
            zero = jnp.zeros((PERM_CHUNK, LANES), F32)
            for c in range(SEQ // PERM_CHUNK):
                dkp[_chunk(c), :] = zero
                dvp[_chunk(c), :] = zero
            dkp[pl.ds(SEQ, 2 * HALF_WINDOW), :] = zero[:2 * HALF_WINDOW]
            dvp[pl.ds(SEQ, 2 * HALF_WINDOW), :] = zero[:2 * HALF_WINDOW]

            heads = ((qa, doa, 0), (qb, dob, HEAD_DIM))

            def blocks(i, carry, length=length):
                base = pl.multiple_of(i * (group * Q_BLOCK), group * Q_BLOCK)
                starts = [base + g * Q_BLOCK for g in range(group)]
                raw = [[(_dot_nt(qx[pl.ds(m0, Q_BLOCK), :], kp[pl.ds(m0, K_WINDOW), :]),
                         _dot_nt(dox[pl.ds(m0, Q_BLOCK), :], vp[pl.ds(m0, K_WINDOW), :])) for qx, dox, _ in heads]
                       for m0 in starts]
                grads = []
                for m0, pair in zip(starts, raw):
                    bias = bias_ref[_band_bias_index(m0, length)]
                    lse_b, d_b = lsep[pl.ds(m0, Q_BLOCK), :], dlp[pl.ds(m0, Q_BLOCK), :]
                    out = []
                    for (s, dp), (_, _, col) in zip(pair, heads):
                        p = jnp.exp(s + bias - lse_b[:, col:col + 1])
                        out.append(((p * (dp - d_b[:, col:col + 1])).astype(BF16), p.astype(BF16)))
                    grads.append(out)
                for m0, out in zip(starts, grads):
                    qrows, krows = pl.ds(m0, Q_BLOCK), pl.ds(m0, K_WINDOW)
                    kw = kp[krows, :]
                    dk = jnp.zeros((K_WINDOW, LANES), F32)
                    dv = jnp.zeros((K_WINDOW, LANES), F32)
                    for (ds, p), (qx, dox, _) in zip(out, heads):
                        dk = dk + _dot_tn(ds, qx[qrows, :])
                        dv = dv + _dot_tn(p, dox[qrows, :])
                    dqp[qrows, :] = jnp.where(first, _dot(out[0][0], kw), _dot(out[1][0], kw)) * (HEAD_DIM ** -0.5)
                    dkp[krows, :] += dk
                    dvp[krows, :] += dv
                return carry

            lax.fori_loop(0, SEQ // (group * Q_BLOCK), blocks, 0)

            for c, rows in chunks:
                g_q, g_k, g_v = dqp[_chunk(c), :], dkp[_chunk(c, HALF_WINDOW), :], dvp[_chunk(c, HALF_WINDOW), :]
                if step == 0:
                    dq_ref[rows, :] = g_q
                    dk_ref[rows, :] = g_k
                    dv_ref[rows, :] = g_v
                else:
                    dq_ref[rows, :] = dq_ref[rows, :] + g_q
                    dk_ref[rows, :] = dk_ref[rows, :] + g_k
                    dv_ref[rows, :] = dv_ref[rows, :] + g_v

    col = pl.BlockSpec((SEQ, LANES), lambda h: (0, h))
    col1 = pl.BlockSpec((SEQ, LANES), lambda h: (0, h), pipeline_mode=pl.Buffered(1))
    padded = SEQ + 2 * HALF_WINDOW
    return _call(
        body, (q, k, v, do, o, lse), grid=(ATTN_WIDTH // LANES,), in_specs=[col1] * 6, out_specs=[col] * 3,
        out_shape=[_sds((SEQ, ATTN_WIDTH), F32)] * 3,
        scratch_shapes=[pltpu.VMEM((SEQ, LANES), F32)] * 3 + [pltpu.VMEM((SEQ, LANES), BF16)] * 4
        + [pltpu.VMEM((padded, LANES), BF16)] * 2 + [pltpu.VMEM((SEQ, LANES), F32)] * 4 + [pltpu.VMEM((padded, LANES), F32)] * 2
        + [pltpu.VMEM((3, Q_BLOCK, K_WINDOW), F32)],
        vmem_mib=56, name="attn_bwd", comm=comm)


def _inproj_bwd(dq, dk, dv, ds, db, u, c, conv_w, tc, tsa, tsb, w_in, x, g_pre, dx1, comm=None):
    def body(dq_ref, dk_ref, dv_ref, ds_ref, db_ref, u_ref, c_ref, dsp_ref, up_ref, cp_ref, dsn_ref, un_ref, cn_ref, cw_ref,
             tc_ref, tsa_ref, tsb_ref, w_ref, x_ref, g_ref, dx1_ref, dx_ref, dproj_ref, h_ref, dg_ref, dcw_ref):
        i = pl.program_id(0)
        cw = cw_ref[...]
        u, c = u_ref[...], c_ref[...]
        t, t_prev, t_next, _ = _conv_parts(u, c, up_ref[...], cp_ref[...], un_ref[...], cn_ref[...], cw, i)
        ds = ds_ref[...]
        ds_prev, ds_next = _shifted(ds, dsp_ref[7:8, :], dsn_ref[0:1, :], i)
        dt = cw[0:1, :] * ds_next + cw[1:2, :] * ds + cw[2:3, :] * ds_prev
        _accumulate(dcw_ref, jnp.concatenate([_colsum(ds * t_prev), _colsum(ds * t), _colsum(ds * t_next)], axis=0), i)
        tc_, tsa_, tsb_ = tc_ref[...], tsa_ref[...], tsb_ref[...]
        dproj = jnp.concatenate(
            [_rotate_transposed(dq_ref[...], tc_, tsa_, tsb_), _rotate_transposed(dk_ref[...], tc_, tsa_, tsb_), dv_ref[...],
             dt * c, db_ref[...], dt * u], axis=1).astype(BF16)
        dproj_ref[...] = dproj
        dh = _dot_nt(dproj, w_ref[...])
        y, r = _rms(x_ref[...])
        h_ref[...] = (y * g_ref[...]).astype(BF16)
        _accumulate(dg_ref, _colsum(dh * y), i)
        dx_ref[...] = dx1_ref[...] + _rms_bwd(dh * g_ref[...], y, r)

    a = ATTN_WIDTH
    tile = FFN_TILE

    def rows(width):
        return _rows(width, tile)

    return _call(
        body, (dq, dk, dv, ds, db, u, c, ds, u, c, ds, u, c, conv_w, tc, tsa, tsb, w_in, x, g_pre, dx1), grid=(SEQ // tile,),
        in_specs=[rows(a)] * 7 + [_halo_prev(a, tile)] * 3 + [_halo_next(a, tile)] * 3
        + [_const((3, a)), rows(LANES), rows(LANES), rows(LANES), _const1((D_MODEL, IN_PROJ_WIDTH)), rows(D_MODEL),
           _const((1, D_MODEL)), rows(D_MODEL)],
        out_specs=[rows(D_MODEL), rows(IN_PROJ_WIDTH), rows(D_MODEL), _const((1, D_MODEL)), _const((3, a))],
        out_shape=[_sds((SEQ, D_MODEL), F32), _sds((SEQ, IN_PROJ_WIDTH), BF16), _sds((SEQ, D_MODEL), BF16),
                   _sds((1, D_MODEL), F32), _sds((3, a), F32)],
        vmem_mib=48, name="inproj_bwd", comm=comm)


def _adamw_math(w, g, m, v):
    m = ADAM_B1 * m + (1.0 - ADAM_B1) * g
    v = ADAM_B2 * v + (1.0 - ADAM_B2) * (g * g)
    m_hat = m / (1.0 - ADAM_B1 ** ADAM_STEP)
    v_hat = v / (1.0 - ADAM_B2 ** ADAM_STEP)
    delta = -ADAM_LR * (m_hat / (jnp.sqrt(v_hat) + ADAM_EPS) + ADAM_WD * w)
    return delta, m, v


def _sum_parts(p_ref):
    g = p_ref[0].astype(F32)
    for k in range(1, N_DEV):
        g = g + p_ref[k].astype(F32)
    return g


def _adamw_layers(parts, w, m, v, row_tile, name, comm=None):
    _, rows, cols = w.shape
    n_tiles = rows // row_tile

    def body(*refs):
        p_refs = refs[:DEPTH]
        w_ref, m_ref, v_ref, g_ref, d_ref, nm_ref, nv_ref = refs[DEPTH:]
        layer = pl.program_id(0)
        for l, p_ref in enumerate(p_refs):
            @pl.when(layer == l)
            def _(p_ref=p_ref):
                g = _sum_parts(p_ref)
                g_ref[0] = g
                d_ref[0], nm_ref[0], nv_ref[0] = _adamw_math(w_ref[0], g, m_ref[0], v_ref[0])

    def part_spec(l):
        return pl.BlockSpec((N_DEV, row_tile, cols),
                            lambda layer, i: (0, jnp.where(layer == l, i, jnp.where(layer < l, 0, n_tiles - 1)), 0))

    tile = pl.BlockSpec((1, row_tile, cols), lambda layer, i: (layer, i, 0))
    return _call(body, (*parts, w, m, v), grid=(DEPTH, n_tiles), in_specs=[part_spec(l) for l in range(DEPTH)] + [tile] * 3,
                 out_specs=[tile] * 4, out_shape=[_sds(w.shape, F32)] * 4, name=name, comm=comm)


def _adamw_sum8(parts, w, m, v, name):
    def body(p_ref, w_ref, m_ref, v_ref, g_ref, d_ref, nm_ref, nv_ref):
        g = _sum_parts(p_ref)
        g_ref[...] = g
        d_ref[...], nm_ref[...], nv_ref[...] = _adamw_math(w_ref[...], g, m_ref[...], v_ref[...])

    return pl.pallas_call(body, out_shape=[_sds(w.shape, F32)] * 4, name=name)(parts, w, m, v)


def _adamw_plain(g, w, m, v, name):
    def body(g_ref, w_ref, m_ref, v_ref, d_ref, nm_ref, nv_ref):
        d_ref[...], nm_ref[...], nv_ref[...] = _adamw_math(w_ref[...], g_ref[...], m_ref[...], v_ref[...])

    return pl.pallas_call(body, out_shape=[_sds(w.shape, F32)] * 3, name=name)(g, w, m, v)


SMALL_ROWS = 8
GAIN_NAMES = ("pre_mix_norm", "post_mix_norm", "pre_ffn_norm", "post_ffn_norm", "attn_out_norm", "conv_out_norm")


def _pack_small(pre_mix, post_mix, pre_ffn, post_ffn, attn_out, conv_out, taps):
    zeros = jnp.zeros((1, ATTN_WIDTH), F32)
    return jnp.concatenate([
        pre_mix, post_mix, pre_ffn, post_ffn, jnp.concatenate([attn_out, conv_out], axis=1),
        jnp.concatenate([taps[0:1], taps[1:2]], axis=1), jnp.concatenate([taps[2:3], zeros], axis=1),
        jnp.zeros((1, D_MODEL), F32)], axis=0)


def _rope(positions):
    inv_freq = ROPE_THETA ** (-jnp.arange(0, ROPE_DIM, 2, dtype=F32) / ROPE_DIM)
    lane = jnp.arange(LANES) % HEAD_DIM
    freq_row = jnp.where(lane < ROPE_DIM, inv_freq[lane % (ROPE_DIM // 2)], 0.0).reshape(1, LANES).astype(F32)
    return _rope_tables(positions.reshape(SEQ, 1), freq_row)


def _layer_forward(h, gains, taps, tables, w, attn_comm=None, ffn_comm=None):
    q, k, v, u, b, c = _inproj_fwd(h, gains["pre_mix_norm"], w["in"], *tables)
    (attn, lse), landed = _attn_fwd(q, k, v, comm=attn_comm)
    if attn_comm is not None:
        w = {**w, **dict(zip(attn_comm.kinds, landed))}
    x1, merged, mix = _mix_fwd(attn, u, b, c, taps, gains["attn_out_norm"], gains["conv_out_norm"], w["out"], h, gains["post_mix_norm"])
    (x2, gu, f), landed_next = _ffn_fwd(x1, gains["pre_ffn_norm"], w["gu"], w["dn"], gains["post_ffn_norm"], comm=ffn_comm)
    return x2, (h, q, k, v, u, b, c, attn, lse, merged, mix, x1, gu, f), w, landed_next


def _layer_backward(dx, saved, gains, taps, tables, w, ffn_comm=None, scatter=False):
    x0, q, k, v, u, b, c, attn, lse, merged, mix, x1, gu, f = saved
    (dx1, df, act, dgu, h2, dg_post_ffn, dg_pre_ffn), landed_prev = _ffn_bwd(
        dx, f, x1, gu, w["gu"], w["dn"], gains["post_ffn_norm"], gains["pre_ffn_norm"], comm=ffn_comm)
    g_gu = _wgrad_gate_up(h2, dgu)
    g_dn = _wgrad_down(act, df)
    dmix, dattn, ds, db, dg_post_mix, dg_attn, dg_conv = _mix_bwd(
        dx1, mix, attn, u, b, c, taps, gains["attn_out_norm"], gains["conv_out_norm"], gains["post_mix_norm"], w["out"])
    g_out = _wgrad_out(merged, dmix)
    attn_comm = _Scatter(("gu",), (g_gu,)) if scatter else None
    inproj_comm = _Scatter(("dn",), (g_dn,)) if scatter else None
    wgrad_comm = _Scatter(("out",), (g_out,)) if scatter else None
    (dq, dk, dv), landed_gu = _attn_bwd(q, k, v, dattn, attn, lse, comm=attn_comm)
    (dx0, dproj, h1, dg_pre_mix, dtaps), landed_dn = _inproj_bwd(
        dq, dk, dv, ds, db, u, c, taps, *tables, w["in"], x0, gains["pre_mix_norm"], dx1, comm=inproj_comm)
    g_in, landed_out = _wgrad_in(h1, dproj, comm=wgrad_comm)
    small = _pack_small(dg_pre_mix, dg_post_mix, dg_pre_ffn, dg_post_ffn, dg_attn, dg_conv, dtaps)
    rest = dict(gu=landed_gu[0], dn=landed_dn[0], out=landed_out[0]) if scatter else dict(gu=g_gu, dn=g_dn, out=g_out)
    return dx0, g_in, small, landed_prev, rest


def kernel(x, positions, pre_mix_norm, w_in, conv_w, attn_out_norm, conv_out_norm, w_out, post_mix_norm, pre_ffn_norm, w_gate_up, w_down, post_ffn_norm, loss_target, m_pre_mix_norm, m_w_in, m_conv_w, m_attn_out_norm, m_conv_out_norm, m_w_out, m_post_mix_norm, m_pre_ffn_norm, m_w_gate_up, m_w_down, m_post_ffn_norm, v_pre_mix_norm, v_w_in, v_conv_w, v_attn_out_norm, v_conv_out_norm, v_w_out, v_post_mix_norm, v_pre_ffn_norm, v_w_gate_up, v_w_down, v_post_ffn_norm):
    mx, my, mc = _place()
    me = _block_of(mx, my, mc)
    conv_channels = conv_w.shape[-1]

    taps_flat = jnp.pad(conv_w.reshape(-1), (0, 8 * LANES - conv_w.size)).reshape(8, LANES)
    taps_all = _allgather_small(taps_flat, "allgather_taps").reshape(N_DEV, 8 * LANES)[:, :conv_w.size]
    conv_w_full = taps_all.reshape(N_DEV, DEPTH, 3, conv_channels).transpose(1, 2, 0, 3).reshape(DEPTH, 3, ATTN_WIDTH)

    def hidden_major(a):
        return jnp.swapaxes(a, 1, 2)

    big_w = dict(zip(WEIGHT_KINDS, (w_in, w_out, hidden_major(w_gate_up), w_down)))
    big_m = dict(zip(WEIGHT_KINDS, (m_w_in, m_w_out, hidden_major(m_w_gate_up), m_w_down)))
    big_v = dict(zip(WEIGHT_KINDS, (v_w_in, v_w_out, hidden_major(v_w_gate_up), v_w_down)))
    shards = {kind: big_w[kind].astype(BF16) for kind in WEIGHT_KINDS}
    all_gains = dict(pre_mix_norm=pre_mix_norm, attn_out_norm=attn_out_norm, conv_out_norm=conv_out_norm, post_mix_norm=post_mix_norm,
                     pre_ffn_norm=pre_ffn_norm, post_ffn_norm=post_ffn_norm)

    def gather(kinds, l):
        return _Gather(kinds, [shards[kind][l] for kind in kinds])

    def gains(l):
        return {name: g[l:l + 1] for name, g in all_gains.items()}

    tables = _rope(positions)

    early = ("in", "out", "dn")
    first = _comm_only(gather(early, 0), "allgather_first")
    weights = [dict(zip(early, first))] + [None] * (DEPTH - 1)
    saved = [None] * DEPTH
    h = x[0]
    for l in range(DEPTH):
        ffn_comm = gather(early, l + 1) if l + 1 < DEPTH else None
        h, saved[l], weights[l], landed = _layer_forward(h, gains(l), conv_w_full[l], tables, weights[l], gather(("gu",), l), ffn_comm)
        if ffn_comm is not None:
            weights[l + 1] = dict(zip(ffn_comm.kinds, landed))

    loss, dx = _loss_and_grad(h, loss_target[0])
    loss = lax.psum(loss[0, 0], ("x", "y", "c"))

    parts = {kind: [None] * DEPTH for kind in WEIGHT_KINDS}
    small_grads = [None] * DEPTH
    g_in_above = None
    for l in reversed(range(DEPTH)):
        ffn_comm = _Scatter(("in",), (g_in_above,)) if g_in_above is not None else None
        dx, g_in_above, small_grads[l], landed, rest = _layer_backward(dx, saved[l], gains(l), conv_w_full[l], tables, weights[l], ffn_comm, True)
        if ffn_comm is not None:
            parts["in"][l + 1] = landed[0]
        for kind, part in rest.items():
            parts[kind][l] = part

    parts["in"][0] = _comm_only(_Scatter(("in",), (g_in_above,)), "scatter_last")[0]
    tiles = {"in": 256, "out": 128, "gu": 176, "dn": 176}
    big = {kind: _adamw_layers(parts[kind], big_w[kind], big_m[kind], big_v[kind], tiles[kind], "adamw_" + kind)[0]
           for kind in WEIGHT_KINDS}

    packed = jnp.concatenate(small_grads, axis=0)
    gathered = _allgather_small(packed, "allgather_small_grads")

    def pack_state(state):
        rows = [_pack_small(*[state[name][l:l + 1] for name in GAIN_NAMES], jnp.zeros((3, ATTN_WIDTH), F32)) for l in range(DEPTH)]
        return jnp.concatenate(rows, axis=0)

    sw = pack_state(all_gains)
    sm = pack_state(dict(zip(GAIN_NAMES, (m_pre_mix_norm, m_post_mix_norm, m_pre_ffn_norm, m_post_ffn_norm, m_attn_out_norm, m_conv_out_norm))))
    sv = pack_state(dict(zip(GAIN_NAMES, (v_pre_mix_norm, v_post_mix_norm, v_pre_ffn_norm, v_post_ffn_norm, v_attn_out_norm, v_conv_out_norm))))
    sg, sd, snm, snv = _adamw_sum8(gathered, sw, sm, sv, "adamw_small")

    def unpack(p):
        p = p.reshape(DEPTH, SMALL_ROWS, D_MODEL)
        return dict(pre_mix_norm=p[:, 0], post_mix_norm=p[:, 1], pre_ffn_norm=p[:, 2], post_ffn_norm=p[:, 3],
                    attn_out_norm=p[:, 4, :ATTN_WIDTH], conv_out_norm=p[:, 4, ATTN_WIDTH:])

    small = [unpack(p) for p in (sg, sd, snm, snv)]
    sg3 = sg.reshape(DEPTH, SMALL_ROWS, D_MODEL)
    taps_grad_full = jnp.stack([sg3[:, 5, :ATTN_WIDTH], sg3[:, 5, ATTN_WIDTH:], sg3[:, 6, :ATTN_WIDTH]], axis=1)
    taps_grad = lax.dynamic_slice_in_dim(taps_grad_full, me * conv_channels, conv_channels, axis=2)

    def flat(a):
        return a.reshape(DEPTH * 3, conv_channels)

    td, tnm, tnv = _adamw_plain(flat(taps_grad), flat(conv_w), flat(m_conv_w), flat(v_conv_w), "adamw_taps")
    taps = [taps_grad] + [a.reshape(conv_w.shape) for a in (td, tnm, tnv)]

    def leaves(o):
        s = small[o]
        return (s["pre_mix_norm"], big["in"][o], taps[o], s["attn_out_norm"], s["conv_out_norm"], big["out"][o], s["post_mix_norm"],
                s["pre_ffn_norm"], hidden_major(big["gu"][o]), big["dn"][o], s["post_ffn_norm"])

    return (loss, dx[None], *leaves(0), *leaves(1), *leaves(2), *leaves(3))
```

```python
import math

import jax
import jax.numpy as jnp
from jax import lax
from jax.experimental import pallas as pl
from jax.experimental.pallas import tpu as pltpu

F32 = jnp.float32
BF16 = jnp.bfloat16
MESH = pl.DeviceIdType.MESH

SEQ = 4096
D_MODEL = 1024
DEPTH = 4
N_DEV = 8
ATTN_WIDTH = 512
IN_PROJ_WIDTH = 3072
FFN_HIDDEN = 2816
FFN_BLOCK = 2 * FFN_HIDDEN // N_DEV
W_IN_BLOCK = IN_PROJ_WIDTH // N_DEV
W_OUT_BLOCK = D_MODEL // N_DEV
W_DOWN_BLOCK = FFN_HIDDEN // N_DEV
HEAD_DIM = 64
ROPE_DIM = 16
ROPE_THETA = 500000.0
DILATIONS = (1, 4, 16)
HALF_WINDOW = 64
RMS_EPS = 1e-6
NEG_INF = -1e30
LANES = 128
Q_BLOCK = 128
K_WINDOW = Q_BLOCK + 2 * HALF_WINDOW
PERM_CHUNK = 256
ROW_TILE = 512
FFN_TILE = 256
WGRAD_TILE = SEQ
ADAM_LR, ADAM_B1, ADAM_B2, ADAM_EPS, ADAM_WD, ADAM_STEP = 0.001, 0.9, 0.999, 1e-08, 0.01, 10
MIB = 1024 * 1024
PINNED_BYTES = 256 * 1024

WEIGHT_KINDS = ("in", "out", "gu", "dn")
FULL_SHAPES = {"in": (D_MODEL, IN_PROJ_WIDTH), "out": (D_MODEL, D_MODEL), "gu": (N_DEV, FFN_BLOCK, D_MODEL), "dn": (FFN_HIDDEN, D_MODEL)}
SHARD_SHAPES = {"in": (D_MODEL, W_IN_BLOCK), "out": (W_OUT_BLOCK, D_MODEL), "gu": (FFN_BLOCK, D_MODEL), "dn": (W_DOWN_BLOCK, D_MODEL)}
ANY = pl.BlockSpec(memory_space=pl.ANY)


def _sds(shape, dtype):
    return jax.ShapeDtypeStruct(shape, dtype)


def _rows(width, tile=ROW_TILE):
    return pl.BlockSpec((tile, width), lambda i: (i, 0))


def _frows(width):
    return _rows(width, FFN_TILE)


def _cols(height, tile):
    return pl.BlockSpec((height, tile), lambda i: (0, i))


def _const(shape):
    return pl.BlockSpec(shape, lambda i: (0,) * len(shape))


def _const1(shape):
    return pl.BlockSpec(shape, lambda i: (0,) * len(shape), pipeline_mode=pl.Buffered(1))


def _halo_prev(width, tile=ROW_TILE):
    return pl.BlockSpec((8, width), lambda i: (jnp.maximum(i * (tile // 8) - 1, 0), 0))


def _halo_next(width, tile=ROW_TILE):
    return pl.BlockSpec((8, width), lambda i: (jnp.minimum((i + 1) * (tile // 8), SEQ // 8 - 1), 0))


def _rms(x):
    r = lax.rsqrt(jnp.mean(x * x, axis=-1, keepdims=True) + RMS_EPS)
    return x * r, r


def _rms_bwd(dn, y, r):
    return r * (dn - y * jnp.mean(dn * y, axis=-1, keepdims=True))


def _dot(a, b):
    return jnp.dot(a, b, preferred_element_type=F32)


def _dot_nt(a, b):
    return lax.dot_general(a, b, (((1,), (1,)), ((), ())), preferred_element_type=F32)


def _dot_tn(a, b):
    return lax.dot_general(a, b, (((0,), (0,)), ((), ())), preferred_element_type=F32)


def _place():
    return lax.axis_index("x"), lax.axis_index("y"), lax.axis_index("c")


def _block_of(px, py, pc):
    return 4 * px + 2 * py + pc


def _weight_block(ref, kind, blk):
    if kind == "in":
        return ref.at[:, pl.ds(blk * W_IN_BLOCK, W_IN_BLOCK)]
    if kind == "out":
        return ref.at[pl.ds(blk * W_OUT_BLOCK, W_OUT_BLOCK), :]
    if kind == "gu":
        return ref.at[blk]
    return ref.at[pl.ds(blk * W_DOWN_BLOCK, W_DOWN_BLOCK), :]


def _dma_semaphores(n):
    return [pltpu.SemaphoreType.DMA((n, 7)), pltpu.SemaphoreType.DMA((n, 7)), pltpu.SemaphoreType.DMA((n,))]


class _Gather:
    def __init__(self, kinds, shards):
        self.kinds, self.operands = tuple(kinds), list(shards)
        self.tag = "gather_" + "_".join(kinds)
        self.out_shape = [_sds(FULL_SHAPES[k], BF16) for k in kinds]
        self.scratch = _dma_semaphores(len(kinds))

    def _parties(self):
        x, y, c = _place()
        return (x, y, c), (x, y, 1 - c), [(1 - x, y), (x, 1 - y), (1 - x, 1 - y)], c

    def _copy(self, outs, sems, w, k, block, to, src=None):
        dst = _weight_block(outs[w], self.kinds[w], _block_of(*block))
        return pltpu.make_async_remote_copy(src_ref=dst if src is None else src, dst_ref=dst, send_sem=sems[0].at[w, k],
                                            recv_sem=sems[1].at[w, k], device_id=to, device_id_type=MESH)

    def _own(self, srcs, outs, sems, w, me):
        return pltpu.make_async_copy(srcs[w], _weight_block(outs[w], self.kinds[w], _block_of(*me)), sems[2].at[w])

    def _first(self, srcs, outs, sems, w):
        me, sibling, chips, c = self._parties()
        return [self._copy(outs, sems, w, 0, me, sibling, src=srcs[w])] + [
            self._copy(outs, sems, w, 1 + j, me, (*chip, c), src=srcs[w]) for j, chip in enumerate(chips)]

    def start(self, srcs, outs, sems):
        me = self._parties()[0]
        for w in range(len(self.kinds)):
            self._own(srcs, outs, sems, w, me).start()
            for cp in self._first(srcs, outs, sems, w):
                cp.start()

    def forward(self, srcs, outs, sems):
        me, sibling, chips, c = self._parties()
        for j, chip in enumerate(chips):
            for w in range(len(self.kinds)):
                self._copy(outs, sems, w, 1 + j, (*chip, c), me).wait_recv()
                self._copy(outs, sems, w, 4 + j, (*chip, c), sibling).start()

    def finish(self, srcs, outs, sems):
        me, sibling, chips, c = self._parties()
        for w in range(len(self.kinds)):
            self._copy(outs, sems, w, 0, sibling, me).wait_recv()
            for j, chip in enumerate(chips):
                self._copy(outs, sems, w, 4 + j, (*chip, 1 - c), me).wait_recv()
        for w in range(len(self.kinds)):
            for cp in self._first(srcs, outs, sems, w):
                cp.wait_send()
            for j, chip in enumerate(chips):
                self._copy(outs, sems, w, 4 + j, (*chip, c), sibling).wait_send()
            self._own(srcs, outs, sems, w, me).wait()


def _peers(x, y, c):
    return [(x ^ a, y ^ b, c ^ e) for a in (0, 1) for b in (0, 1) for e in (0, 1) if (a, b, e) != (0, 0, 0)]


class _Scatter:
    def __init__(self, kinds, grads):
        self.kinds, self.operands = tuple(kinds), list(grads)
        self.tag = "scatter_" + "_".join(kinds)
        self.out_shape = [_sds((N_DEV,) + SHARD_SHAPES[k], BF16) for k in kinds]
        self.scratch = _dma_semaphores(len(kinds))

    def _copies(self, srcs, outs, sems):
        x, y, c = _place()
        me = _block_of(x, y, c)
        copies = []
        for w, kind in enumerate(self.kinds):
            copies.append(pltpu.make_async_copy(_weight_block(srcs[w], kind, me), outs[w].at[me], sems[2].at[w]))
            for k, peer in enumerate(_peers(x, y, c)):
                copies.append(pltpu.make_async_remote_copy(
                    src_ref=_weight_block(srcs[w], kind, _block_of(*peer)), dst_ref=outs[w].at[me],
                    send_sem=sems[0].at[w, k], recv_sem=sems[1].at[w, k], device_id=peer, device_id_type=MESH))
        return copies

    def start(self, srcs, outs, sems):
        for cp in self._copies(srcs, outs, sems):
            cp.start()

    def forward(self, srcs, outs, sems):
        pass

    def finish(self, srcs, outs, sems):
        for cp in self._copies(srcs, outs, sems):
            cp.wait()


def _in_hbm(a):
    return pltpu.with_memory_space_constraint(a, pltpu.HBM) if a.size * a.dtype.itemsize >= PINNED_BYTES else a


def _out_hbm(s):
    return pltpu.HBM(s.shape, s.dtype) if math.prod(s.shape) * jnp.dtype(s.dtype).itemsize >= PINNED_BYTES else s


def _call(body, args, *, grid, in_specs, out_specs, out_shape, scratch_shapes=(), vmem_mib=None, name, comm=None):
    kwargs = {} if vmem_mib is None else dict(compiler_params=pltpu.CompilerParams(vmem_limit_bytes=vmem_mib * MIB))
    in_specs, out_specs, out_shape, scratch_shapes = list(in_specs), list(out_specs), list(out_shape), list(scratch_shapes)
    args = [_in_hbm(a) for a in args]
    out_shape = [_out_hbm(s) for s in out_shape]
    if comm is None:
        res = pl.pallas_call(body, grid=grid, in_specs=in_specs, out_specs=out_specs, out_shape=out_shape,
                             scratch_shapes=scratch_shapes, name=name, **kwargs)(*args)
        return list(res), None
    n_in, n_out, n_scr = len(in_specs), len(out_specs), len(scratch_shapes)
    c_in, c_out = len(comm.operands), len(comm.out_shape)
    last = math.prod(grid) - 1

    def carried(*refs):
        cuts = [n_in, c_in, n_out, c_out, n_scr]
        parts, at = [], 0
        for n in cuts:
            parts.append(refs[at:at + n])
            at += n
        ins, c_ins, outs, c_outs, scr = parts
        sems = refs[at:]
        step = pl.program_id(0)
        for axis in range(1, len(grid)):
            step = step * grid[axis] + pl.program_id(axis)

        @pl.when(step == 0)
        def _():
            comm.start(c_ins, c_outs, sems)

        @pl.when(step == last)
        def _():
            comm.forward(c_ins, c_outs, sems)

        body(*ins, *outs, *scr)

        @pl.when(step == last)
        def _():
            comm.finish(c_ins, c_outs, sems)

    res = pl.pallas_call(carried, grid=grid, in_specs=in_specs + [ANY] * c_in, out_specs=out_specs + [ANY] * c_out,
                         out_shape=out_shape + [_out_hbm(s) for s in comm.out_shape], scratch_shapes=scratch_shapes + comm.scratch,
                         name=name + "_" + comm.tag, **kwargs)(*args, *[_in_hbm(a) for a in comm.operands])
    return list(res[:n_out]), list(res[n_out:])


def _comm_only(comm, name):
    def body(*refs):
        n_in, n_out = len(comm.operands), len(comm.out_shape)
        srcs, outs, sems = refs[:n_in], refs[n_in:n_in + n_out], refs[n_in + n_out:]
        comm.start(srcs, outs, sems)
        comm.forward(srcs, outs, sems)
        comm.finish(srcs, outs, sems)

    return pl.pallas_call(body, in_specs=[ANY] * len(comm.operands), out_specs=[ANY] * len(comm.out_shape),
                          out_shape=[_out_hbm(s) for s in comm.out_shape], scratch_shapes=comm.scratch,
                          name=name)(*[_in_hbm(a) for a in comm.operands])


def _allgather_small(v, name):
    def body(v_ref, o_ref, send, recv, local):
        x, y, c = _place()
        me = _block_of(x, y, c)
        mine = pltpu.make_async_copy(v_ref, o_ref.at[me], local)
        mine.start()
        copies = [mine]
        for k, peer in enumerate(_peers(x, y, c)):
            cp = pltpu.make_async_remote_copy(src_ref=v_ref, dst_ref=o_ref.at[me], send_sem=send.at[k], recv_sem=recv.at[k],
                                              device_id=peer, device_id_type=MESH)
            cp.start()
            copies.append(cp)
        for cp in copies:
            cp.wait()

    vm = pl.BlockSpec(memory_space=pltpu.VMEM)
    return pl.pallas_call(
        body, in_specs=[vm], out_specs=vm, out_shape=_sds((N_DEV,) + v.shape, F32),
        scratch_shapes=[pltpu.SemaphoreType.DMA((7,)), pltpu.SemaphoreType.DMA((7,)), pltpu.SemaphoreType.DMA],
        name=name)(v)


def _rope_tables(pos_col, freq_row):
    def body(p_ref, f_ref, c_ref, sa_ref, sb_ref):
        ang = p_ref[...].astype(F32) * f_ref[...]
        lane = lax.broadcasted_iota(jnp.int32, ang.shape, 1) % HEAD_DIM
        cos, sin = jnp.cos(ang), jnp.sin(ang)
        c_ref[...] = jnp.where(lane < ROPE_DIM, cos, 1.0)
        sa_ref[...] = jnp.where(lane < ROPE_DIM // 2, -sin, 0.0)
        sb_ref[...] = jnp.where((lane >= ROPE_DIM // 2) & (lane < ROPE_DIM), sin, 0.0)

    return pl.pallas_call(
        body, grid=(SEQ // ROW_TILE,), in_specs=[_rows(1), _const((1, LANES))], out_specs=[_rows(LANES)] * 3,
        out_shape=[_sds((SEQ, LANES), F32)] * 3, name="rope_tables")(pos_col, freq_row)


def _rotate(t, c, sa, sb):
    parts = []
    for g in range(ATTN_WIDTH // LANES):
        tg = t[:, g * LANES:(g + 1) * LANES]
        parts.append(tg * c + pltpu.roll(tg, LANES - 8, axis=1) * sa + pltpu.roll(tg, 8, axis=1) * sb)
    return jnp.concatenate(parts, axis=1)


def _rotate_transposed(dt, c, sa, sb):
    parts = []
    for g in range(ATTN_WIDTH // LANES):
        dg = dt[:, g * LANES:(g + 1) * LANES]
        parts.append(dg * c + pltpu.roll(dg * sa, 8, axis=1) + pltpu.roll(dg * sb, LANES - 8, axis=1))
    return jnp.concatenate(parts, axis=1)


def _inproj_fwd(x, g_pre, w_in, tc, tsa, tsb, comm=None):
    def body(x_ref, g_ref, w_ref, c_ref, sa_ref, sb_ref, q_ref, k_ref, v_ref, u_ref, b_ref, cc_ref):
        y, _ = _rms(x_ref[...])
        h = (y * g_ref[...]).astype(BF16)

        def proj(n):
            return _dot(h, w_ref[:, n * ATTN_WIDTH:(n + 1) * ATTN_WIDTH])

        c, sa, sb = c_ref[...], sa_ref[...], sb_ref[...]
        q_ref[...] = (_rotate(proj(0), c, sa, sb) * (HEAD_DIM ** -0.5)).astype(BF16)
        k_ref[...] = _rotate(proj(1), c, sa, sb).astype(BF16)
        v_ref[...] = proj(2).astype(BF16)
        u_ref[...] = proj(3)
        b_ref[...] = proj(4)
        cc_ref[...] = proj(5)

    a = ATTN_WIDTH
    return _call(
        body, (x, g_pre, w_in, tc, tsa, tsb), grid=(SEQ // ROW_TILE,),
        in_specs=[_rows(D_MODEL), _const((1, D_MODEL)), _const1((D_MODEL, IN_PROJ_WIDTH)), _rows(LANES), _rows(LANES), _rows(LANES)],
        out_specs=[_rows(a)] * 6, out_shape=[_sds((SEQ, a), BF16)] * 3 + [_sds((SEQ, a), F32)] * 3,
        vmem_mib=40, name="inproj_fwd", comm=comm)


def _head_masks():
    lane = lax.broadcasted_iota(jnp.int32, (1, LANES), 1)
    first = lane < HEAD_DIM
    return first, first.astype(F32), 1.0 - first.astype(F32)


def _perm_chunks(dil):
    length = SEQ // dil
    out = []
    for r in range(dil):
        for c0 in range(0, length, PERM_CHUNK):
            chunk = (r * length + c0) // PERM_CHUNK
            rows = pl.ds(c0, PERM_CHUNK) if dil == 1 else pl.ds(r + dil * c0, PERM_CHUNK, stride=dil)
            out.append((chunk, rows))
    return out


def _chunk(c, offset=0):
    return pl.ds(offset + c * PERM_CHUNK, PERM_CHUNK)


def _write_band_bias(bias_ref):
    rr = lax.broadcasted_iota(jnp.int32, (Q_BLOCK, K_WINDOW), 0)
    cc = lax.broadcasted_iota(jnp.int32, (Q_BLOCK, K_WINDOW), 1)
    band = (cc >= rr) & (cc - rr <= 2 * HALF_WINDOW)
    bias_ref[0] = jnp.where(band, 0.0, NEG_INF)
    bias_ref[1] = jnp.where(band & (cc >= HALF_WINDOW), 0.0, NEG_INF)
    bias_ref[2] = jnp.where(band & (cc < Q_BLOCK + HALF_WINDOW), 0.0, NEG_INF)


def _band_bias_index(m0, length):
    return jnp.where(m0 % length == 0, 1, 0) + jnp.where((m0 + Q_BLOCK) % length == 0, 2, 0)


def _zero_key_padding(bufs):
    pad = jnp.zeros((HALF_WINDOW, LANES), BF16)
    for buf in bufs:
        buf[pl.ds(0, HALF_WINDOW), :] = pad
        buf[pl.ds(SEQ + HALF_WINDOW, HALF_WINDOW), :] = pad


def _attn_fwd(q, k, v, comm=None):
    group = 8

    def body(q_ref, k_ref, v_ref, o_ref, lse_ref, q32, k32, v32, qa, qb, kp, vp, accp, mlp,
             acc0, acc1, acc2, ml0, ml1, ml2, bias_ref):
        first, mask_a, mask_b = _head_masks()
        low = lax.broadcasted_iota(jnp.int32, (1, LANES), 1) % HEAD_DIM < HEAD_DIM // 2
        _zero_key_padding((kp, vp))
        _write_band_bias(bias_ref)
        q32[...] = q_ref[...].astype(F32)
        k32[...] = k_ref[...].astype(F32)
        v32[...] = v_ref[...].astype(F32)
        natural = ((acc0, ml0), (acc1, ml1), (acc2, ml2))

        for branch, dil in enumerate(DILATIONS):
            length = SEQ // dil
            assert length >= 2 * Q_BLOCK
            chunks = _perm_chunks(dil)
            for c, rows in chunks:
                val = q32[rows, :]
                qa[_chunk(c), :] = (val * mask_a).astype(BF16)
                qb[_chunk(c), :] = (val * mask_b).astype(BF16)
                kp[_chunk(c, HALF_WINDOW), :] = k32[rows, :].astype(BF16)
                vp[_chunk(c, HALF_WINDOW), :] = v32[rows, :].astype(BF16)
            acc_dst, ml_dst = natural[branch] if dil == 1 else (accp, mlp)

            def blocks(i, carry, length=length, acc_dst=acc_dst, ml_dst=ml_dst):
                base = pl.multiple_of(i * (group * Q_BLOCK), group * Q_BLOCK)
                starts = [base + g * Q_BLOCK for g in range(group)]
                scores = [[_dot_nt(qx[pl.ds(m0, Q_BLOCK), :], kp[pl.ds(m0, K_WINDOW), :]) for qx in (qa, qb)] for m0 in starts]
                probs = []
                for m0, pair in zip(starts, scores):
                    bias = bias_ref[_band_bias_index(m0, length)]
                    stats, ps = [], []
                    for s in pair:
                        s = s + bias
                        m = jnp.max(s, axis=1, keepdims=True)
                        p = jnp.exp(s - m)
                        stats.append(jnp.where(low, m, jnp.sum(p, axis=1, keepdims=True)))
                        ps.append(p.astype(BF16))
                    ml_dst[pl.ds(m0, Q_BLOCK), :] = jnp.where(first, stats[0], stats[1])
                    probs.append(ps)
                for m0, ps in zip(starts, probs):
                    vw = vp[pl.ds(m0, K_WINDOW), :]
                    acc_dst[pl.ds(m0, Q_BLOCK), :] = jnp.where(first, _dot(ps[0], vw), _dot(ps[1], vw))
                return carry

            lax.fori_loop(0, SEQ // (group * Q_BLOCK), blocks, 0)

            if dil > 1:
                for c, rows in chunks:
                    natural[branch][0][rows, :] = accp[_chunk(c), :]
                    natural[branch][1][rows, :] = mlp[_chunk(c), :]

        for c in range(SEQ // PERM_CHUNK):
            packed = [ml[_chunk(c), :] for _, ml in natural]
            ms = [jnp.where(low, ml, pltpu.roll(ml, HEAD_DIM // 2, axis=1)) for ml in packed]
            ls = [jnp.where(low, pltpu.roll(ml, LANES - HEAD_DIM // 2, axis=1), ml) for ml in packed]
            m_all = jnp.maximum(jnp.maximum(ms[0], ms[1]), ms[2])
            es = [jnp.exp(m - m_all) for m in ms]
            z = ls[0] * es[0] + ls[1] * es[1] + ls[2] * es[2]
            num = natural[0][0][_chunk(c), :] * es[0] + natural[1][0][_chunk(c), :] * es[1] + natural[2][0][_chunk(c), :] * es[2]
            o_ref[_chunk(c), :] = num / z
            lse_ref[_chunk(c), :] = m_all + jnp.log(z)

    col = pl.BlockSpec((SEQ, LANES), lambda h: (0, h))
    padded = SEQ + 2 * HALF_WINDOW
    return _call(
        body, (q, k, v), grid=(ATTN_WIDTH // LANES,), in_specs=[col] * 3, out_specs=[col] * 2,
        out_shape=[_sds((SEQ, ATTN_WIDTH), F32)] * 2,
        scratch_shapes=[pltpu.VMEM((SEQ, LANES), F32)] * 3 + [pltpu.VMEM((SEQ, LANES), BF16)] * 2
        + [pltpu.VMEM((padded, LANES), BF16)] * 2 + [pltpu.VMEM((SEQ, LANES), F32)] * 8
        + [pltpu.VMEM((3, Q_BLOCK, K_WINDOW), F32)],
        vmem_mib=52, name="attn_fwd", comm=comm)


def _shifted(t, before, after, i):
    tile = t.shape[0]
    row = lax.broadcasted_iota(jnp.int32, (tile, 1), 0)
    before = jnp.where(i > 0, before, 0.0)
    after = jnp.where(i < SEQ // tile - 1, after, 0.0)
    return (jnp.where(row == 0, before, pltpu.roll(t, 1, axis=0)),
            jnp.where(row == tile - 1, after, pltpu.roll(t, tile - 1, axis=0)))


def _conv_parts(u, c, u_prev, c_prev, u_next, c_next, cw, i):
    t = c * u
    t_prev, t_next = _shifted(t, c_prev[7:8, :] * u_prev[7:8, :], c_next[0:1, :] * u_next[0:1, :], i)
    s = cw[0:1, :] * t_prev + cw[1:2, :] * t + cw[2:3, :] * t_next
    return t, t_prev, t_next, s


def _mix_fwd(attn, u, b, c, conv_w, g_attn, g_conv, w_out, x, g_post):
    def body(a_ref, u_ref, b_ref, c_ref, up_ref, cp_ref, un_ref, cn_ref, cw_ref, ga_ref, gc_ref, w_ref, x_ref, gp_ref,
             x1_ref, mg_ref, mix_ref):
        i = pl.program_id(0)
        _, _, _, s = _conv_parts(u_ref[...], c_ref[...], up_ref[...], cp_ref[...], un_ref[...], cn_ref[...], cw_ref[...], i)
        ya, _ = _rms(a_ref[...])
        yc, _ = _rms(b_ref[...] * s)
        merged = jnp.concatenate([ya * ga_ref[...], yc * gc_ref[...]], axis=1).astype(BF16)
        mix = _dot(merged, w_ref[...])
        ym, _ = _rms(mix)
        mg_ref[...] = merged.T
        mix_ref[...] = mix
        x1_ref[...] = x_ref[...] + ym * gp_ref[...]

    a = ATTN_WIDTH
    return _call(
        body, (attn, u, b, c, u, c, u, c, conv_w, g_attn, g_conv, w_out, x, g_post), grid=(SEQ // ROW_TILE,),
        in_specs=[_rows(a)] * 4 + [_halo_prev(a)] * 2 + [_halo_next(a)] * 2
        + [_const((3, a)), _const((1, a)), _const((1, a)), _const1((D_MODEL, D_MODEL)), _rows(D_MODEL), _const((1, D_MODEL))],
        out_specs=[_rows(D_MODEL), _cols(D_MODEL, ROW_TILE), _rows(D_MODEL)],
        out_shape=[_sds((SEQ, D_MODEL), F32), _sds((D_MODEL, SEQ), BF16), _sds((SEQ, D_MODEL), F32)],
        vmem_mib=40, name="mix_fwd")[0]


def _gu_spec():
    return pl.BlockSpec((N_DEV, FFN_TILE, FFN_BLOCK), lambda i: (0, i, 0))


def _ffn_fwd(x1, g_pre, w_gu, w_dn, g_post, comm=None):
    def body(x_ref, g_ref, wgu_ref, wdn_ref, gp_ref, x2_ref, gu_ref, f_ref):
        x1 = x_ref[...]
        y, _ = _rms(x1)
        h = (y * g_ref[...]).astype(BF16)
        f = jnp.zeros((FFN_TILE, D_MODEL), F32)
        for j in range(N_DEV // 2):
            gate = _dot_nt(h, wgu_ref[j])
            up = _dot_nt(h, wgu_ref[j + N_DEV // 2])
            gu_ref[j] = gate.astype(BF16)
            gu_ref[j + N_DEV // 2] = up.astype(BF16)
            act = (gate * jax.nn.sigmoid(gate) * up).astype(BF16)
            f = f + _dot(act, wdn_ref[pl.ds(j * FFN_BLOCK, FFN_BLOCK), :])
        yf, _ = _rms(f)
        f_ref[...] = f
        x2_ref[...] = x1 + yf * gp_ref[...]

    return _call(
        body, (x1, g_pre, w_gu, w_dn, g_post), grid=(SEQ // FFN_TILE,),
        in_specs=[_frows(D_MODEL), _const((1, D_MODEL)), _const1((N_DEV, FFN_BLOCK, D_MODEL)), _const1((FFN_HIDDEN, D_MODEL)),
                  _const((1, D_MODEL))],
        out_specs=[_frows(D_MODEL), _gu_spec(), _frows(D_MODEL)],
        out_shape=[_sds((SEQ, D_MODEL), F32), _sds((N_DEV, SEQ, FFN_BLOCK), BF16), _sds((SEQ, D_MODEL), F32)],
        vmem_mib=48, name="ffn_fwd", comm=comm)


def _loss_and_grad(y, target):
    n_tiles = SEQ // ROW_TILE

    def body(y_ref, t_ref, loss_ref, dy_ref, acc):
        i = pl.program_id(0)
        err = y_ref[...] - t_ref[...]
        dy_ref[...] = err * (1.0 / D_MODEL)

        @pl.when(i == 0)
        def _():
            acc[...] = jnp.zeros_like(acc)

        acc[...] += jnp.sum(err * err, axis=0, keepdims=True)

        @pl.when(i == n_tiles - 1)
        def _():
            loss_ref[...] = jnp.sum(acc[...], axis=1, keepdims=True) * (0.5 / D_MODEL)

    return _call(
        body, (y, target), grid=(n_tiles,), in_specs=[_rows(D_MODEL)] * 2, out_specs=[_const((1, 1)), _rows(D_MODEL)],
        out_shape=[_sds((1, 1), F32), _sds((SEQ, D_MODEL), F32)], scratch_shapes=[pltpu.VMEM((1, D_MODEL), F32)],
        name="loss_and_grad")[0]


def _accumulate(ref, value, i):
    @pl.when(i == 0)
    def _():
        ref[...] = value

    @pl.when(i > 0)
    def _():
        ref[...] += value


def _colsum(v):
    return jnp.sum(v, axis=0, keepdims=True)


def _ffn_bwd(dx2, f, x1, gu, w_gu, w_dn, g_post, g_pre, comm=None):
    half = N_DEV // 2

    def body(dx2_ref, f_ref, x1_ref, gu_ref, wgu_ref, wdn_ref, gpost_ref, gpre_ref,
             dx1_ref, df_ref, act_ref, dgu_ref, h_ref, dgpost_ref, dgpre_ref):
        i = pl.program_id(0)
        dx2 = dx2_ref[...]
        yf, rf = _rms(f_ref[...])
        _accumulate(dgpost_ref, _colsum(dx2 * yf), i)
        df = _rms_bwd(dx2 * gpost_ref[...], yf, rf).astype(BF16)
        df_ref[...] = df
        dh = jnp.zeros((FFN_TILE, D_MODEL), F32)
        for j in range(half):
            dact = _dot_nt(df, wdn_ref[pl.ds(j * FFN_BLOCK, FFN_BLOCK), :])
            gate = gu_ref[j].astype(F32)
            up = gu_ref[j + half].astype(F32)
            sig = jax.nn.sigmoid(gate)
            silu = gate * sig
            act_ref[j] = (silu * up).astype(BF16)
            dgate = (dact * up * (sig * (1.0 + gate * (1.0 - sig)))).astype(BF16)
            dup = (dact * silu).astype(BF16)
            dgu_ref[j] = dgate
            dgu_ref[j + half] = dup
            dh = dh + _dot(dgate, wgu_ref[j]) + _dot(dup, wgu_ref[j + half])
        y1, r1 = _rms(x1_ref[...])
        h_ref[...] = (y1 * gpre_ref[...]).astype(BF16)
        _accumulate(dgpre_ref, _colsum(dh * y1), i)
        dx1_ref[...] = dx2 + _rms_bwd(dh * gpre_ref[...], y1, r1)

    act_spec = pl.BlockSpec((half, FFN_TILE, FFN_BLOCK), lambda i: (0, i, 0))
    return _call(
        body, (dx2, f, x1, gu, w_gu, w_dn, g_post, g_pre), grid=(SEQ // FFN_TILE,),
        in_specs=[_frows(D_MODEL)] * 3 + [_gu_spec(), _const1((N_DEV, FFN_BLOCK, D_MODEL)), _const1((FFN_HIDDEN, D_MODEL)),
                                          _const((1, D_MODEL)), _const((1, D_MODEL))],
        out_specs=[_frows(D_MODEL), _frows(D_MODEL), act_spec, _gu_spec(), _frows(D_MODEL), _const((1, D_MODEL)), _const((1, D_MODEL))],
        out_shape=[_sds((SEQ, D_MODEL), F32), _sds((SEQ, D_MODEL), BF16), _sds((half, SEQ, FFN_BLOCK), BF16),
                   _sds((N_DEV, SEQ, FFN_BLOCK), BF16), _sds((SEQ, D_MODEL), BF16), _sds((1, D_MODEL), F32), _sds((1, D_MODEL), F32)],
        vmem_mib=52, name="ffn_bwd", comm=comm)


def _wgrad(a, g, a_spec, g_spec, out_shape, out_spec, n_blocks, name, a_is_transposed, comm=None):
    def body(a_ref, g_ref, o_ref):
        gv = g_ref[...].reshape(SEQ, g_ref.shape[-1])
        if a_is_transposed:
            part = _dot(a_ref[...].reshape(a_ref.shape[-2], SEQ), gv)
        else:
            part = _dot_tn(a_ref[...].reshape(SEQ, a_ref.shape[-1]), gv)
        o_ref[...] = part.astype(BF16).reshape(o_ref.shape)

    res, landed = _call(body, (a, g), grid=(n_blocks,), in_specs=[a_spec, g_spec], out_specs=[out_spec],
                        out_shape=[_sds(out_shape, BF16)], vmem_mib=48, name=name, comm=comm)
    return res[0], landed


def _wgrad_gate_up(h, dgu):
    return _wgrad(dgu, h, pl.BlockSpec((1, SEQ, FFN_BLOCK), lambda j: (j, 0, 0)), _const1((SEQ, D_MODEL)),
                  (N_DEV, FFN_BLOCK, D_MODEL), pl.BlockSpec((1, FFN_BLOCK, D_MODEL), lambda j: (j, 0, 0)), N_DEV, "wgrad_gate_up", False)[0]


def _wgrad_down(act, df):
    return _wgrad(act, df, pl.BlockSpec((1, SEQ, FFN_BLOCK), lambda j: (j, 0, 0)), _const1((SEQ, D_MODEL)),
                  (FFN_HIDDEN, D_MODEL), pl.BlockSpec((FFN_BLOCK, D_MODEL), lambda j: (j, 0)), N_DEV // 2, "wgrad_down", False)[0]


def _wgrad_out(merged_t, dmix):
    return _wgrad(merged_t, dmix, _const1((D_MODEL, SEQ)), pl.BlockSpec((SEQ, 512), lambda j: (0, j)),
                  (D_MODEL, D_MODEL), pl.BlockSpec((D_MODEL, 512), lambda j: (0, j)), 2, "wgrad_out", True)[0]


def _wgrad_in(h_t, dproj, comm=None):
    return _wgrad(h_t, dproj, _const1((D_MODEL, SEQ)), pl.BlockSpec((SEQ, 512), lambda j: (0, j)),
                  (D_MODEL, IN_PROJ_WIDTH), pl.BlockSpec((D_MODEL, 512), lambda j: (0, j)), IN_PROJ_WIDTH // 512, "wgrad_in", True,
                  comm=comm)


def _mix_bwd(dx1, mix, attn, u, b, c, conv_w, g_attn, g_conv, g_post, w_out):
    def body(dx1_ref, mix_ref, a_ref, u_ref, b_ref, c_ref, up_ref, cp_ref, un_ref, cn_ref, cw_ref, ga_ref, gc_ref, gp_ref, w_ref,
             dmix_ref, da_ref, ds_ref, db_ref, dgp_ref, dga_ref, dgc_ref):
        i = pl.program_id(0)
        dx1 = dx1_ref[...]
        ym, rm = _rms(mix_ref[...])
        _accumulate(dgp_ref, _colsum(dx1 * ym), i)
        dmix = _rms_bwd(dx1 * gp_ref[...], ym, rm).astype(BF16)
        dmix_ref[...] = dmix
        dmerged = _dot_nt(dmix, w_ref[...])
        dna, dnc = dmerged[:, :ATTN_WIDTH], dmerged[:, ATTN_WIDTH:]
        ya, ra = _rms(a_ref[...])
        _accumulate(dga_ref, _colsum(dna * ya), i)
        da_ref[...] = _rms_bwd(dna * ga_ref[...], ya, ra)
        _, _, _, s = _conv_parts(u_ref[...], c_ref[...], up_ref[...], cp_ref[...], un_ref[...], cn_ref[...], cw_ref[...], i)
        gate_b = b_ref[...]
        yc, rc = _rms(gate_b * s)
        _accumulate(dgc_ref, _colsum(dnc * yc), i)
        dy = _rms_bwd(dnc * gc_ref[...], yc, rc)
        db_ref[...] = dy * s
        ds_ref[...] = dy * gate_b

    a = ATTN_WIDTH
    return _call(
        body, (dx1, mix, attn, u, b, c, u, c, u, c, conv_w, g_attn, g_conv, g_post, w_out), grid=(SEQ // ROW_TILE,),
        in_specs=[_rows(D_MODEL)] * 2 + [_rows(a)] * 4 + [_halo_prev(a)] * 2 + [_halo_next(a)] * 2
        + [_const((3, a)), _const((1, a)), _const((1, a)), _const((1, D_MODEL)), _const1((D_MODEL, D_MODEL))],
        out_specs=[_rows(D_MODEL)] + [_rows(a)] * 3 + [_const((1, D_MODEL)), _const((1, a)), _const((1, a))],
        out_shape=[_sds((SEQ, D_MODEL), BF16)] + [_sds((SEQ, a), F32)] * 3 + [_sds((1, D_MODEL), F32), _sds((1, a), F32), _sds((1, a), F32)],
        vmem_mib=40, name="mix_bwd")[0]


def _attn_bwd(q, k, v, do, o, lse, comm=None):
    group = 8

    def body(q_ref, k_ref, v_ref, do_ref, o_ref, lse_ref, dq_ref, dk_ref, dv_ref,
             q32, k32, v32, qa, qb, doa, dob, kp, vp, lsep, dlp, dnat, dqp, dkp, dvp, bias_ref):
        first, mask_a, mask_b = _head_masks()
        _zero_key_padding((kp, vp))
        _write_band_bias(bias_ref)
        q32[...] = q_ref[...].astype(F32)
        k32[...] = k_ref[...].astype(F32)
        v32[...] = v_ref[...].astype(F32)
        for c in range(SEQ // PERM_CHUNK):
            prod = do_ref[_chunk(c), :] * o_ref[_chunk(c), :]
            d_a = jnp.sum(prod * mask_a, axis=1, keepdims=True)
            d_b = jnp.sum(prod * mask_b, axis=1, keepdims=True)
            dnat[_chunk(c), :] = jnp.where(first, d_a, d_b)

        for step, dil in enumerate(DILATIONS[1:] + DILATIONS[:1]):
            length = SEQ // dil
            assert length >= 2 * Q_BLOCK
            chunks = _perm_chunks(dil)
            for c, rows in chunks:
                val = q32[rows, :]
                qa[_chunk(c), :] = (val * mask_a).astype(BF16)
                qb[_chunk(c), :] = (val * mask_b).astype(BF16)
                val = do_ref[rows, :]
                doa[_chunk(c), :] = (val * mask_a).astype(BF16)
                dob[_chunk(c), :] = (val * mask_b).astype(BF16)
                kp[_chunk(c, HALF_WINDOW), :] = k32[rows, :].astype(BF16)
                vp[_chunk(c, HALF_WINDOW), :] = v32[rows, :].astype(BF16)
                lsep[_chunk(c), :] = lse_ref[rows, :]
                dlp[_chunk(c), :] = dnat[rows, :]
            zero = jnp.zeros((PERM_CHUNK, LANES), F32)
            for c in range(SEQ // PERM_CHUNK):
                dkp[_chunk(c), :] = zero
                dvp[_chunk(c), :] = zero
            dkp[pl.ds(SEQ, 2 * HALF_WINDOW), :] = zero[:2 * HALF_WINDOW]
            dvp[pl.ds(SEQ, 2 * HALF_WINDOW), :] = zero[:2 * HALF_WINDOW]

            heads = ((qa, doa, 0), (qb, dob, HEAD_DIM))

            def blocks(i, carry, length=length):
                base = pl.multiple_of(i * (group * Q_BLOCK), group * Q_BLOCK)
                starts = [base + g * Q_BLOCK for g in range(group)]
                raw = [[(_dot_nt(qx[pl.ds(m0, Q_BLOCK), :], kp[pl.ds(m0, K_WINDOW), :]),
                         _dot_nt(dox[pl.ds(m0, Q_BLOCK), :], vp[pl.ds(m0, K_WINDOW), :])) for qx, dox, _ in heads]
                       for m0 in starts]
                grads = []
                for m0, pair in zip(starts, raw):
                    bias = bias_ref[_band_bias_index(m0, length)]
                    lse_b, d_b = lsep[pl.ds(m0, Q_BLOCK), :], dlp[pl.ds(m0, Q_BLOCK), :]
                    out = []
                    for (s, dp), (_, _, col) in zip(pair, heads):
                        p = jnp.exp(s + bias - lse_b[:, col:col + 1])
                        out.append(((p * (dp - d_b[:, col:col + 1])).astype(BF16), p.astype(BF16)))
                    grads.append(out)
                for m0, out in zip(starts, grads):
                    qrows, krows = pl.ds(m0, Q_BLOCK), pl.ds(m0, K_WINDOW)
                    kw = kp[krows, :]
                    dk = jnp.zeros((K_WINDOW, LANES), F32)
                    dv = jnp.zeros((K_WINDOW, LANES), F32)
                    for (ds, p), (qx, dox, _) in zip(out, heads):
                        dk = dk + _dot_tn(ds, qx[qrows, :])
                        dv = dv + _dot_tn(p, dox[qrows, :])
                    dqp[qrows, :] = jnp.where(first, _dot(out[0][0], kw), _dot(out[1][0], kw)) * (HEAD_DIM ** -0.5)
                    dkp[krows, :] += dk
                    dvp[krows, :] += dv
                return carry

            lax.fori_loop(0, SEQ // (group * Q_BLOCK), blocks, 0)

            for c, rows in chunks:
                g_q, g_k, g_v = dqp[_chunk(c), :], dkp[_chunk(c, HALF_WINDOW), :], dvp[_chunk(c, HALF_WINDOW), :]
                if step == 0:
                    dq_ref[rows, :] = g_q
                    dk_ref[rows, :] = g_k
                    dv_ref[rows, :] = g_v
                else:
                    dq_ref[rows, :] = dq_ref[rows, :] + g_q
                    dk_ref[rows, :] = dk_ref[rows, :] + g_k
                    dv_ref[rows, :] = dv_ref[rows, :] + g_v

    col = pl.BlockSpec((SEQ, LANES), lambda h: (0, h))
    col1 = pl.BlockSpec((SEQ, LANES), lambda h: (0, h), pipeline_mode=pl.Buffered(1))
    padded = SEQ + 2 * HALF_WINDOW
    return _call(
        body, (q, k, v, do, o, lse), grid=(ATTN_WIDTH // LANES,), in_specs=[col1] * 6, out_specs=[col] * 3,
        out_shape=[_sds((SEQ, ATTN_WIDTH), F32)] * 3,
        scratch_shapes=[pltpu.VMEM((SEQ, LANES), F32)] * 3 + [pltpu.VMEM((SEQ, LANES), BF16)] * 4
        + [pltpu.VMEM((padded, LANES), BF16)] * 2 + [pltpu.VMEM((SEQ, LANES), F32)] * 4 + [pltpu.VMEM((padded, LANES), F32)] * 2
        + [pltpu.VMEM((3, Q_BLOCK, K_WINDOW), F32)],
        vmem_mib=56, name="attn_bwd", comm=comm)


def _inproj_bwd(dq, dk, dv, ds, db, u, c, conv_w, tc, tsa, tsb, w_in, x, g_pre, dx1, comm=None):
    def body(dq_ref, dk_ref, dv_ref, ds_ref, db_ref, u_ref, c_ref, dsp_ref, up_ref, cp_ref, dsn_ref, un_ref, cn_ref, cw_ref,
             tc_ref, tsa_ref, tsb_ref, w_ref, x_ref, g_ref, dx1_ref, dx_ref, dproj_ref, h_ref, dg_ref, dcw_ref):
        i = pl.program_id(0)
        cw = cw_ref[...]
        u, c = u_ref[...], c_ref[...]
        t, t_prev, t_next, _ = _conv_parts(u, c, up_ref[...], cp_ref[...], un_ref[...], cn_ref[...], cw, i)
        ds = ds_ref[...]
        ds_prev, ds_next = _shifted(ds, dsp_ref[7:8, :], dsn_ref[0:1, :], i)
        dt = cw[0:1, :] * ds_next + cw[1:2, :] * ds + cw[2:3, :] * ds_prev
        _accumulate(dcw_ref, jnp.concatenate([_colsum(ds * t_prev), _colsum(ds * t), _colsum(ds * t_next)], axis=0), i)
        tc_, tsa_, tsb_ = tc_ref[...], tsa_ref[...], tsb_ref[...]
        dproj = jnp.concatenate(
            [_rotate_transposed(dq_ref[...], tc_, tsa_, tsb_), _rotate_transposed(dk_ref[...], tc_, tsa_, tsb_), dv_ref[...],
             dt * c, db_ref[...], dt * u], axis=1).astype(BF16)
        dproj_ref[...] = dproj
        dh = _dot_nt(dproj, w_ref[...])
        y, r = _rms(x_ref[...])
        h_ref[...] = (y * g_ref[...]).astype(BF16).T
        _accumulate(dg_ref, _colsum(dh * y), i)
        dx_ref[...] = dx1_ref[...] + _rms_bwd(dh * g_ref[...], y, r)

    a = ATTN_WIDTH
    tile = FFN_TILE

    def rows(width):
        return _rows(width, tile)

    return _call(
        body, (dq, dk, dv, ds, db, u, c, ds, u, c, ds, u, c, conv_w, tc, tsa, tsb, w_in, x, g_pre, dx1), grid=(SEQ // tile,),
        in_specs=[rows(a)] * 7 + [_halo_prev(a, tile)] * 3 + [_halo_next(a, tile)] * 3
        + [_const((3, a)), rows(LANES), rows(LANES), rows(LANES), _const1((D_MODEL, IN_PROJ_WIDTH)), rows(D_MODEL),
           _const((1, D_MODEL)), rows(D_MODEL)],
        out_specs=[rows(D_MODEL), rows(IN_PROJ_WIDTH), _cols(D_MODEL, tile), _const((1, D_MODEL)), _const((3, a))],
        out_shape=[_sds((SEQ, D_MODEL), F32), _sds((SEQ, IN_PROJ_WIDTH), BF16), _sds((D_MODEL, SEQ), BF16),
                   _sds((1, D_MODEL), F32), _sds((3, a), F32)],
        vmem_mib=48, name="inproj_bwd", comm=comm)


def _adamw_math(w, g, m, v):
    m = ADAM_B1 * m + (1.0 - ADAM_B1) * g
    v = ADAM_B2 * v + (1.0 - ADAM_B2) * (g * g)
    m_hat = m / (1.0 - ADAM_B1 ** ADAM_STEP)
    v_hat = v / (1.0 - ADAM_B2 ** ADAM_STEP)
    delta = -ADAM_LR * (m_hat / (jnp.sqrt(v_hat) + ADAM_EPS) + ADAM_WD * w)
    return delta, m, v


def _sum_parts(p_ref):
    g = p_ref[0].astype(F32)
    for k in range(1, N_DEV):
        g = g + p_ref[k].astype(F32)
    return g


def _adamw_layers(parts, w, m, v, row_tile, name, comm=None):
    _, rows, cols = w.shape
    n_tiles = rows // row_tile

    def body(*refs):
        p_refs = refs[:DEPTH]
        w_ref, m_ref, v_ref, g_ref, d_ref, nm_ref, nv_ref = refs[DEPTH:]
        layer = pl.program_id(0)
        for l, p_ref in enumerate(p_refs):
            @pl.when(layer == l)
            def _(p_ref=p_ref):
                g = _sum_parts(p_ref)
                g_ref[0] = g
                d_ref[0], nm_ref[0], nv_ref[0] = _adamw_math(w_ref[0], g, m_ref[0], v_ref[0])

    def part_spec(l):
        return pl.BlockSpec((N_DEV, row_tile, cols),
                            lambda layer, i: (0, jnp.where(layer == l, i, jnp.where(layer < l, 0, n_tiles - 1)), 0))

    tile = pl.BlockSpec((1, row_tile, cols), lambda layer, i: (layer, i, 0))
    return _call(body, (*parts, w, m, v), grid=(DEPTH, n_tiles), in_specs=[part_spec(l) for l in range(DEPTH)] + [tile] * 3,
                 out_specs=[tile] * 4, out_shape=[_sds(w.shape, F32)] * 4, name=name, comm=comm)


def _adamw_sum8(parts, w, m, v, name):
    def body(p_ref, w_ref, m_ref, v_ref, g_ref, d_ref, nm_ref, nv_ref):
        g = _sum_parts(p_ref)
        g_ref[...] = g
        d_ref[...], nm_ref[...], nv_ref[...] = _adamw_math(w_ref[...], g, m_ref[...], v_ref[...])

    return pl.pallas_call(body, out_shape=[_sds(w.shape, F32)] * 4, name=name)(parts, w, m, v)


def _adamw_plain(g, w, m, v, name):
    def body(g_ref, w_ref, m_ref, v_ref, d_ref, nm_ref, nv_ref):
        d_ref[...], nm_ref[...], nv_ref[...] = _adamw_math(w_ref[...], g_ref[...], m_ref[...], v_ref[...])

    return pl.pallas_call(body, out_shape=[_sds(w.shape, F32)] * 3, name=name)(g, w, m, v)


SMALL_ROWS = 8
GAIN_NAMES = ("pre_mix_norm", "post_mix_norm", "pre_ffn_norm", "post_ffn_norm", "attn_out_norm", "conv_out_norm")


def _pack_small(pre_mix, post_mix, pre_ffn, post_ffn, attn_out, conv_out, taps):
    zeros = jnp.zeros((1, ATTN_WIDTH), F32)
    return jnp.concatenate([
        pre_mix, post_mix, pre_ffn, post_ffn, jnp.concatenate([attn_out, conv_out], axis=1),
        jnp.concatenate([taps[0:1], taps[1:2]], axis=1), jnp.concatenate([taps[2:3], zeros], axis=1),
        jnp.zeros((1, D_MODEL), F32)], axis=0)


def _rope(positions):
    inv_freq = ROPE_THETA ** (-jnp.arange(0, ROPE_DIM, 2, dtype=F32) / ROPE_DIM)
    lane = jnp.arange(LANES) % HEAD_DIM
    freq_row = jnp.where(lane < ROPE_DIM, inv_freq[lane % (ROPE_DIM // 2)], 0.0).reshape(1, LANES).astype(F32)
    return _rope_tables(positions.reshape(SEQ, 1), freq_row)


def _layer_forward(h, gains, taps, tables, w, inproj_comm=None, attn_comm=None, ffn_comm=None):
    (q, k, v, u, b, c), landed = _inproj_fwd(h, gains["pre_mix_norm"], w["in"], *tables, comm=inproj_comm)
    if inproj_comm is not None:
        w = {**w, **dict(zip(inproj_comm.kinds, landed))}
    (attn, lse), landed = _attn_fwd(q, k, v, comm=attn_comm)
    if attn_comm is not None:
        w = {**w, **dict(zip(attn_comm.kinds, landed))}
    x1, merged, mix = _mix_fwd(attn, u, b, c, taps, gains["attn_out_norm"], gains["conv_out_norm"], w["out"], h, gains["post_mix_norm"])
    (x2, gu, f), landed_next = _ffn_fwd(x1, gains["pre_ffn_norm"], w["gu"], w["dn"], gains["post_ffn_norm"], comm=ffn_comm)
    return x2, (h, q, k, v, u, b, c, attn, lse, merged, mix, x1, gu, f), w, landed_next


def _layer_backward(dx, saved, gains, taps, tables, w, ffn_comm=None, scatter=False):
    x0, q, k, v, u, b, c, attn, lse, merged, mix, x1, gu, f = saved
    (dx1, df, act, dgu, h2, dg_post_ffn, dg_pre_ffn), landed_prev = _ffn_bwd(
        dx, f, x1, gu, w["gu"], w["dn"], gains["post_ffn_norm"], gains["pre_ffn_norm"], comm=ffn_comm)
    g_gu = _wgrad_gate_up(h2, dgu)
    g_dn = _wgrad_down(act, df)
    dmix, dattn, ds, db, dg_post_mix, dg_attn, dg_conv = _mix_bwd(
        dx1, mix, attn, u, b, c, taps, gains["attn_out_norm"], gains["conv_out_norm"], gains["post_mix_norm"], w["out"])
    g_out = _wgrad_out(merged, dmix)
    attn_comm = _Scatter(("gu",), (g_gu,)) if scatter else None
    inproj_comm = _Scatter(("dn",), (g_dn,)) if scatter else None
    wgrad_comm = _Scatter(("out",), (g_out,)) if scatter else None
    (dq, dk, dv), landed_gu = _attn_bwd(q, k, v, dattn, attn, lse, comm=attn_comm)
    (dx0, dproj, h1, dg_pre_mix, dtaps), landed_dn = _inproj_bwd(
        dq, dk, dv, ds, db, u, c, taps, *tables, w["in"], x0, gains["pre_mix_norm"], dx1, comm=inproj_comm)
    g_in, landed_out = _wgrad_in(h1, dproj, comm=wgrad_comm)
    small = _pack_small(dg_pre_mix, dg_post_mix, dg_pre_ffn, dg_post_ffn, dg_attn, dg_conv, dtaps)
    rest = dict(gu=landed_gu[0], dn=landed_dn[0], out=landed_out[0]) if scatter else dict(gu=g_gu, dn=g_dn, out=g_out)
    return dx0, g_in, small, landed_prev, rest


def kernel(x, positions, pre_mix_norm, w_in, conv_w, attn_out_norm, conv_out_norm, w_out, post_mix_norm, pre_ffn_norm, w_gate_up, w_down, post_ffn_norm, loss_target, m_pre_mix_norm, m_w_in, m_conv_w, m_attn_out_norm, m_conv_out_norm, m_w_out, m_post_mix_norm, m_pre_ffn_norm, m_w_gate_up, m_w_down, m_post_ffn_norm, v_pre_mix_norm, v_w_in, v_conv_w, v_attn_out_norm, v_conv_out_norm, v_w_out, v_post_mix_norm, v_pre_ffn_norm, v_w_gate_up, v_w_down, v_post_ffn_norm):
    mx, my, mc = _place()
    me = _block_of(mx, my, mc)
    conv_channels = conv_w.shape[-1]

    taps_flat = jnp.pad(conv_w.reshape(-1), (0, 8 * LANES - conv_w.size)).reshape(8, LANES)
    taps_all = _allgather_small(taps_flat, "allgather_taps").reshape(N_DEV, 8 * LANES)[:, :conv_w.size]
    conv_w_full = taps_all.reshape(N_DEV, DEPTH, 3, conv_channels).transpose(1, 2, 0, 3).reshape(DEPTH, 3, ATTN_WIDTH)

    def hidden_major(a):
        return jnp.swapaxes(a, 1, 2)

    big_w = dict(zip(WEIGHT_KINDS, (w_in, w_out, hidden_major(w_gate_up), w_down)))
    big_m = dict(zip(WEIGHT_KINDS, (m_w_in, m_w_out, hidden_major(m_w_gate_up), m_w_down)))
    big_v = dict(zip(WEIGHT_KINDS, (v_w_in, v_w_out, hidden_major(v_w_gate_up), v_w_down)))
    shards = {kind: big_w[kind].astype(BF16) for kind in WEIGHT_KINDS}
    all_gains = dict(pre_mix_norm=pre_mix_norm, attn_out_norm=attn_out_norm, conv_out_norm=conv_out_norm, post_mix_norm=post_mix_norm,
                     pre_ffn_norm=pre_ffn_norm, post_ffn_norm=post_ffn_norm)

    def gather(kinds, l):
        return _Gather(kinds, [shards[kind][l] for kind in kinds])

    def gains(l):
        return {name: g[l:l + 1] for name, g in all_gains.items()}

    tables = _rope(positions)

    early = ("in", "out", "dn")
    weights = [dict(zip(early[:1], _comm_only(gather(early[:1], 0), "allgather_first")))] + [None] * (DEPTH - 1)
    saved = [None] * DEPTH
    h = x[0]
    for l in range(DEPTH):
        inproj_comm = gather(early[1:], 0) if l == 0 else None
        ffn_comm = gather(early, l + 1) if l + 1 < DEPTH else None
        h, saved[l], weights[l], landed = _layer_forward(h, gains(l), conv_w_full[l], tables, weights[l], inproj_comm,
                                                         gather(("gu",), l), ffn_comm)
        if ffn_comm is not None:
            weights[l + 1] = dict(zip(ffn_comm.kinds, landed))

    loss, dx = _loss_and_grad(h, loss_target[0])
    loss = lax.psum(loss[0, 0], ("x", "y", "c"))

    parts = {kind: [None] * DEPTH for kind in WEIGHT_KINDS}
    small_grads = [None] * DEPTH
    g_in_above = None
    for l in reversed(range(DEPTH)):
        ffn_comm = _Scatter(("in",), (g_in_above,)) if g_in_above is not None else None
        dx, g_in_above, small_grads[l], landed, rest = _layer_backward(dx, saved[l], gains(l), conv_w_full[l], tables, weights[l], ffn_comm, True)
        if ffn_comm is not None:
            parts["in"][l + 1] = landed[0]
        for kind, part in rest.items():
            parts[kind][l] = part

    parts["in"][0] = _comm_only(_Scatter(("in",), (g_in_above,)), "scatter_last")[0]
    tiles = {"in": 256, "out": 128, "gu": 176, "dn": 176}
    big = {kind: _adamw_layers(parts[kind], big_w[kind], big_m[kind], big_v[kind], tiles[kind], "adamw_" + kind)[0]
           for kind in WEIGHT_KINDS}

    packed = jnp.concatenate(small_grads, axis=0)
    gathered = _allgather_small(packed, "allgather_small_grads")

    def pack_state(state):
        rows = [_pack_small(*[state[name][l:l + 1] for name in GAIN_NAMES], jnp.zeros((3, ATTN_WIDTH), F32)) for l in range(DEPTH)]
        return jnp.concatenate(rows, axis=0)

    sw = pack_state(all_gains)
    sm = pack_state(dict(zip(GAIN_NAMES, (m_pre_mix_norm, m_post_mix_norm, m_pre_ffn_norm, m_post_ffn_norm, m_attn_out_norm, m_conv_out_norm))))
    sv = pack_state(dict(zip(GAIN_NAMES, (v_pre_mix_norm, v_post_mix_norm, v_pre_ffn_norm, v_post_ffn_norm, v_attn_out_norm, v_conv_out_norm))))
    sg, sd, snm, snv = _adamw_sum8(gathered, sw, sm, sv, "adamw_small")

    def unpack(p):
        p = p.reshape(DEPTH, SMALL_ROWS, D_MODEL)
        return dict(pre_mix_norm=p[:, 0], post_mix_norm=p[:, 1], pre_ffn_norm=p[:, 2], post_ffn_norm=p[:, 3],
                    attn_out_norm=p[:, 4, :ATTN_WIDTH], conv_out_norm=p[:, 4, ATTN_WIDTH:])

    small = [unpack(p) for p in (sg, sd, snm, snv)]
    sg3 = sg.reshape(DEPTH, SMALL_ROWS, D_MODEL)
    taps_grad_full = jnp.stack([sg3[:, 5, :ATTN_WIDTH], sg3[:, 5, ATTN_WIDTH:], sg3[:, 6, :ATTN_WIDTH]], axis=1)
    taps_grad = lax.dynamic_slice_in_dim(taps_grad_full, me * conv_channels, conv_channels, axis=2)

    def flat(a):
        return a.reshape(DEPTH * 3, conv_channels)

    td, tnm, tnv = _adamw_plain(flat(taps_grad), flat(conv_w), flat(m_conv_w), flat(v_conv_w), "adamw_taps")
    taps = [taps_grad] + [a.reshape(conv_w.shape) for a in (td, tnm, tnv)]

    def leaves(o):
        s = small[o]
        return (s["pre_mix_norm"], big["in"][o], taps[o], s["attn_out_norm"], s["conv_out_norm"], big["out"][o], s["post_mix_norm"],
                s["pre_ffn_norm"], hidden_major(big["gu"][o]), big["dn"][o], s["post_ffn_norm"])

    return (loss, dx[None], *leaves(0), *leaves(1), *leaves(2), *leaves(3))
```

```python
import math

import jax
import jax.numpy as jnp
from jax import lax
from jax.experimental import pallas as pl
from jax.experimental.pallas import tpu as pltpu

F32 = jnp.float32
BF16 = jnp.bfloat16
MESH = pl.DeviceIdType.MESH

SEQ = 4096
D_MODEL = 1024
DEPTH = 4
N_DEV = 8
ATTN_WIDTH = 512
IN_PROJ_WIDTH = 3072
FFN_HIDDEN = 2816
FFN_BLOCK = 2 * FFN_HIDDEN // N_DEV
W_IN_BLOCK = IN_PROJ_WIDTH // N_DEV
W_OUT_BLOCK = D_MODEL // N_DEV
W_DOWN_BLOCK = FFN_HIDDEN // N_DEV
HEAD_DIM = 64
ROPE_DIM = 16
ROPE_THETA = 500000.0
DILATIONS = (1, 4, 16)
HALF_WINDOW = 64
RMS_EPS = 1e-6
NEG_INF = -1e30
LANES = 128
Q_BLOCK = 128
K_WINDOW = Q_BLOCK + 2 * HALF_WINDOW
PERM_CHUNK = 256
ROW_TILE = 512
FFN_TILE = 256
WGRAD_TILE = SEQ
ADAM_LR, ADAM_B1, ADAM_B2, ADAM_EPS, ADAM_WD, ADAM_STEP = 0.001, 0.9, 0.999, 1e-08, 0.01, 10
MIB = 1024 * 1024
PINNED_BYTES = 256 * 1024

WEIGHT_KINDS = ("in", "out", "gu", "dn")
FULL_SHAPES = {"in": (D_MODEL, IN_PROJ_WIDTH), "out": (D_MODEL, D_MODEL), "gu": (N_DEV, FFN_BLOCK, D_MODEL), "dn": (FFN_HIDDEN, D_MODEL)}
SHARD_SHAPES = {"in": (D_MODEL, W_IN_BLOCK), "out": (W_OUT_BLOCK, D_MODEL), "gu": (FFN_BLOCK, D_MODEL), "dn": (W_DOWN_BLOCK, D_MODEL)}
ANY = pl.BlockSpec(memory_space=pl.ANY)


def _sds(shape, dtype):
    return jax.ShapeDtypeStruct(shape, dtype)


def _rows(width, tile=ROW_TILE):
    return pl.BlockSpec((tile, width), lambda i: (i, 0))


def _frows(width):
    return _rows(width, FFN_TILE)


def _cols(height, tile):
    return pl.BlockSpec((height, tile), lambda i: (0, i))


def _const(shape):
    return pl.BlockSpec(shape, lambda i: (0,) * len(shape))


def _const1(shape):
    return pl.BlockSpec(shape, lambda i: (0,) * len(shape), pipeline_mode=pl.Buffered(1))


def _halo_prev(width, tile=ROW_TILE):
    return pl.BlockSpec((8, width), lambda i: (jnp.maximum(i * (tile // 8) - 1, 0), 0))


def _halo_next(width, tile=ROW_TILE):
    return pl.BlockSpec((8, width), lambda i: (jnp.minimum((i + 1) * (tile // 8), SEQ // 8 - 1), 0))


def _rms(x):
    r = lax.rsqrt(jnp.mean(x * x, axis=-1, keepdims=True) + RMS_EPS)
    return x * r, r


def _rms_bwd(dn, y, r):
    return r * (dn - y * jnp.mean(dn * y, axis=-1, keepdims=True))


def _dot(a, b):
    return jnp.dot(a, b, preferred_element_type=F32)


def _dot_nt(a, b):
    return lax.dot_general(a, b, (((1,), (1,)), ((), ())), preferred_element_type=F32)


def _dot_tn(a, b):
    return lax.dot_general(a, b, (((0,), (0,)), ((), ())), preferred_element_type=F32)


def _place():
    return lax.axis_index("x"), lax.axis_index("y"), lax.axis_index("c")


def _block_of(px, py, pc):
    return 4 * px + 2 * py + pc


def _weight_block(ref, kind, blk):
    if kind == "in":
        return ref.at[:, pl.ds(blk * W_IN_BLOCK, W_IN_BLOCK)]
    if kind == "out":
        return ref.at[pl.ds(blk * W_OUT_BLOCK, W_OUT_BLOCK), :]
    if kind == "gu":
        return ref.at[blk]
    return ref.at[pl.ds(blk * W_DOWN_BLOCK, W_DOWN_BLOCK), :]


def _dma_semaphores(n):
    return [pltpu.SemaphoreType.DMA((n, 7)), pltpu.SemaphoreType.DMA((n, 7)), pltpu.SemaphoreType.DMA((n,))]


class _Gather:
    def __init__(self, kinds, shards):
        self.kinds, self.operands = tuple(kinds), list(shards)
        self.tag = "gather_" + "_".join(kinds)
        self.out_shape = [_sds(FULL_SHAPES[k], BF16) for k in kinds]
        self.scratch = _dma_semaphores(len(kinds))

    def _parties(self):
        x, y, c = _place()
        return (x, y, c), (x, y, 1 - c), [(1 - x, y), (x, 1 - y), (1 - x, 1 - y)], c

    def _copy(self, outs, sems, w, k, block, to, src=None):
        dst = _weight_block(outs[w], self.kinds[w], _block_of(*block))
        return pltpu.make_async_remote_copy(src_ref=dst if src is None else src, dst_ref=dst, send_sem=sems[0].at[w, k],
                                            recv_sem=sems[1].at[w, k], device_id=to, device_id_type=MESH)

    def _own(self, srcs, outs, sems, w, me):
        return pltpu.make_async_copy(srcs[w], _weight_block(outs[w], self.kinds[w], _block_of(*me)), sems[2].at[w])

    def _first(self, srcs, outs, sems, w):
        me, sibling, chips, c = self._parties()
        return [self._copy(outs, sems, w, 0, me, sibling, src=srcs[w])] + [
            self._copy(outs, sems, w, 1 + j, me, (*chip, c), src=srcs[w]) for j, chip in enumerate(chips)]

    def start(self, srcs, outs, sems):
        me = self._parties()[0]
        for w in range(len(self.kinds)):
            self._own(srcs, outs, sems, w, me).start()
            for cp in self._first(srcs, outs, sems, w):
                cp.start()

    def forward(self, srcs, outs, sems):
        me, sibling, chips, c = self._parties()
        for j, chip in enumerate(chips):
            for w in range(len(self.kinds)):
                self._copy(outs, sems, w, 1 + j, (*chip, c), me).wait_recv()
                self._copy(outs, sems, w, 4 + j, (*chip, c), sibling).start()

    def finish(self, srcs, outs, sems):
        me, sibling, chips, c = self._parties()
        for w in range(len(self.kinds)):
            self._copy(outs, sems, w, 0, sibling, me).wait_recv()
            for j, chip in enumerate(chips):
                self._copy(outs, sems, w, 4 + j, (*chip, 1 - c), me).wait_recv()
        for w in range(len(self.kinds)):
            for cp in self._first(srcs, outs, sems, w):
                cp.wait_send()
            for j, chip in enumerate(chips):
                self._copy(outs, sems, w, 4 + j, (*chip, c), sibling).wait_send()
            self._own(srcs, outs, sems, w, me).wait()


def _peers(x, y, c):
    return [(x ^ a, y ^ b, c ^ e) for a in (0, 1) for b in (0, 1) for e in (0, 1) if (a, b, e) != (0, 0, 0)]


class _Scatter:
    def __init__(self, kinds, grads):
        self.kinds, self.operands = tuple(kinds), list(grads)
        self.tag = "scatter_" + "_".join(kinds)
        self.out_shape = [_sds((N_DEV,) + SHARD_SHAPES[k], BF16) for k in kinds]
        self.scratch = _dma_semaphores(len(kinds))

    def _copies(self, srcs, outs, sems):
        x, y, c = _place()
        me = _block_of(x, y, c)
        copies = []
        for w, kind in enumerate(self.kinds):
            copies.append(pltpu.make_async_copy(_weight_block(srcs[w], kind, me), outs[w].at[me], sems[2].at[w]))
            for k, peer in enumerate(_peers(x, y, c)):
                copies.append(pltpu.make_async_remote_copy(
                    src_ref=_weight_block(srcs[w], kind, _block_of(*peer)), dst_ref=outs[w].at[me],
                    send_sem=sems[0].at[w, k], recv_sem=sems[1].at[w, k], device_id=peer, device_id_type=MESH))
        return copies

    def start(self, srcs, outs, sems):
        for cp in self._copies(srcs, outs, sems):
            cp.start()

    def forward(self, srcs, outs, sems):
        pass

    def finish(self, srcs, outs, sems):
        for cp in self._copies(srcs, outs, sems):
            cp.wait()


def _in_hbm(a):
    return pltpu.with_memory_space_constraint(a, pltpu.HBM) if a.size * a.dtype.itemsize >= PINNED_BYTES else a


def _out_hbm(s):
    return pltpu.HBM(s.shape, s.dtype) if math.prod(s.shape) * jnp.dtype(s.dtype).itemsize >= PINNED_BYTES else s


def _call(body, args, *, grid, in_specs, out_specs, out_shape, scratch_shapes=(), vmem_mib=None, name, comm=None):
    kwargs = {} if vmem_mib is None else dict(compiler_params=pltpu.CompilerParams(vmem_limit_bytes=vmem_mib * MIB))
    in_specs, out_specs, out_shape, scratch_shapes = list(in_specs), list(out_specs), list(out_shape), list(scratch_shapes)
    args = [_in_hbm(a) for a in args]
    out_shape = [_out_hbm(s) for s in out_shape]
    if comm is None:
        res = pl.pallas_call(body, grid=grid, in_specs=in_specs, out_specs=out_specs, out_shape=out_shape,
                             scratch_shapes=scratch_shapes, name=name, **kwargs)(*args)
        return list(res), None
    n_in, n_out, n_scr = len(in_specs), len(out_specs), len(scratch_shapes)
    c_in, c_out = len(comm.operands), len(comm.out_shape)
    last = math.prod(grid) - 1

    def carried(*refs):
        cuts = [n_in, c_in, n_out, c_out, n_scr]
        parts, at = [], 0
        for n in cuts:
            parts.append(refs[at:at + n])
            at += n
        ins, c_ins, outs, c_outs, scr = parts
        sems = refs[at:]
        step = pl.program_id(0)
        for axis in range(1, len(grid)):
            step = step * grid[axis] + pl.program_id(axis)

        @pl.when(step == 0)
        def _():
            comm.start(c_ins, c_outs, sems)

        @pl.when(step == last)
        def _():
            comm.forward(c_ins, c_outs, sems)

        body(*ins, *outs, *scr)

        @pl.when(step == last)
        def _():
            comm.finish(c_ins, c_outs, sems)

    res = pl.pallas_call(carried, grid=grid, in_specs=in_specs + [ANY] * c_in, out_specs=out_specs + [ANY] * c_out,
                         out_shape=out_shape + [_out_hbm(s) for s in comm.out_shape], scratch_shapes=scratch_shapes + comm.scratch,
                         name=name + "_" + comm.tag, **kwargs)(*args, *[_in_hbm(a) for a in comm.operands])
    return list(res[:n_out]), list(res[n_out:])


def _comm_only(comm, name):
    def body(*refs):
        n_in, n_out = len(comm.operands), len(comm.out_shape)
        srcs, outs, sems = refs[:n_in], refs[n_in:n_in + n_out], refs[n_in + n_out:]
        comm.start(srcs, outs, sems)
        comm.forward(srcs, outs, sems)
        comm.finish(srcs, outs, sems)

    return pl.pallas_call(body, in_specs=[ANY] * len(comm.operands), out_specs=[ANY] * len(comm.out_shape),
                          out_shape=[_out_hbm(s) for s in comm.out_shape], scratch_shapes=comm.scratch,
                          name=name)(*[_in_hbm(a) for a in comm.operands])


def _allgather_small(v, name):
    def body(v_ref, o_ref, send, recv, local):
        x, y, c = _place()
        me = _block_of(x, y, c)
        mine = pltpu.make_async_copy(v_ref, o_ref.at[me], local)
        mine.start()
        copies = [mine]
        for k, peer in enumerate(_peers(x, y, c)):
            cp = pltpu.make_async_remote_copy(src_ref=v_ref, dst_ref=o_ref.at[me], send_sem=send.at[k], recv_sem=recv.at[k],
                                              device_id=peer, device_id_type=MESH)
            cp.start()
            copies.append(cp)
        for cp in copies:
            cp.wait()

    vm = pl.BlockSpec(memory_space=pltpu.VMEM)
    return pl.pallas_call(
        body, in_specs=[vm], out_specs=vm, out_shape=_sds((N_DEV,) + v.shape, F32),
        scratch_shapes=[pltpu.SemaphoreType.DMA((7,)), pltpu.SemaphoreType.DMA((7,)), pltpu.SemaphoreType.DMA],
        name=name)(v)


def _rope_tables(pos_col, freq_row):
    def body(p_ref, f_ref, c_ref, sa_ref, sb_ref):
        ang = p_ref[...].astype(F32) * f_ref[...]
        lane = lax.broadcasted_iota(jnp.int32, ang.shape, 1) % HEAD_DIM
        cos, sin = jnp.cos(ang), jnp.sin(ang)
        c_ref[...] = jnp.where(lane < ROPE_DIM, cos, 1.0)
        sa_ref[...] = jnp.where(lane < ROPE_DIM // 2, -sin, 0.0)
        sb_ref[...] = jnp.where((lane >= ROPE_DIM // 2) & (lane < ROPE_DIM), sin, 0.0)

    return pl.pallas_call(
        body, grid=(SEQ // ROW_TILE,), in_specs=[_rows(1), _const((1, LANES))], out_specs=[_rows(LANES)] * 3,
        out_shape=[_sds((SEQ, LANES), F32)] * 3, name="rope_tables")(pos_col, freq_row)


def _rotate(t, c, sa, sb):
    parts = []
    for g in range(ATTN_WIDTH // LANES):
        tg = t[:, g * LANES:(g + 1) * LANES]
        parts.append(tg * c + pltpu.roll(tg, LANES - 8, axis=1) * sa + pltpu.roll(tg, 8, axis=1) * sb)
    return jnp.concatenate(parts, axis=1)


def _rotate_transposed(dt, c, sa, sb):
    parts = []
    for g in range(ATTN_WIDTH // LANES):
        dg = dt[:, g * LANES:(g + 1) * LANES]
        parts.append(dg * c + pltpu.roll(dg * sa, 8, axis=1) + pltpu.roll(dg * sb, LANES - 8, axis=1))
    return jnp.concatenate(parts, axis=1)


def _inproj_fwd(x, g_pre, w_in, tc, tsa, tsb, comm=None):
    def body(x_ref, g_ref, w_ref, c_ref, sa_ref, sb_ref, q_ref, k_ref, v_ref, u_ref, b_ref, cc_ref):
        y, _ = _rms(x_ref[...])
        h = (y * g_ref[...]).astype(BF16)

        def proj(n):
            return _dot(h, w_ref[:, n * ATTN_WIDTH:(n + 1) * ATTN_WIDTH])

        c, sa, sb = c_ref[...], sa_ref[...], sb_ref[...]
        q_ref[...] = (_rotate(proj(0), c, sa, sb) * (HEAD_DIM ** -0.5)).astype(BF16)
        k_ref[...] = _rotate(proj(1), c, sa, sb).astype(BF16)
        v_ref[...] = proj(2).astype(BF16)
        u_ref[...] = proj(3)
        b_ref[...] = proj(4)
        cc_ref[...] = proj(5)

    a = ATTN_WIDTH
    return _call(
        body, (x, g_pre, w_in, tc, tsa, tsb), grid=(SEQ // ROW_TILE,),
        in_specs=[_rows(D_MODEL), _const((1, D_MODEL)), _const1((D_MODEL, IN_PROJ_WIDTH)), _rows(LANES), _rows(LANES), _rows(LANES)],
        out_specs=[_rows(a)] * 6, out_shape=[_sds((SEQ, a), BF16)] * 3 + [_sds((SEQ, a), F32)] * 3,
        vmem_mib=40, name="inproj_fwd", comm=comm)


def _head_masks():
    lane = lax.broadcasted_iota(jnp.int32, (1, LANES), 1)
    first = lane < HEAD_DIM
    return first, first.astype(F32), 1.0 - first.astype(F32)


def _perm_chunks(dil):
    length = SEQ // dil
    out = []
    for r in range(dil):
        for c0 in range(0, length, PERM_CHUNK):
            chunk = (r * length + c0) // PERM_CHUNK
            rows = pl.ds(c0, PERM_CHUNK) if dil == 1 else pl.ds(r + dil * c0, PERM_CHUNK, stride=dil)
            out.append((chunk, rows))
    return out


def _chunk(c, offset=0):
    return pl.ds(offset + c * PERM_CHUNK, PERM_CHUNK)


def _write_band_bias(bias_ref):
    rr = lax.broadcasted_iota(jnp.int32, (Q_BLOCK, K_WINDOW), 0)
    cc = lax.broadcasted_iota(jnp.int32, (Q_BLOCK, K_WINDOW), 1)
    band = (cc >= rr) & (cc - rr <= 2 * HALF_WINDOW)
    bias_ref[0] = jnp.where(band, 0.0, NEG_INF)
    bias_ref[1] = jnp.where(band & (cc >= HALF_WINDOW), 0.0, NEG_INF)
    bias_ref[2] = jnp.where(band & (cc < Q_BLOCK + HALF_WINDOW), 0.0, NEG_INF)


def _band_bias_index(m0, length):
    return jnp.where(m0 % length == 0, 1, 0) + jnp.where((m0 + Q_BLOCK) % length == 0, 2, 0)


def _zero_key_padding(bufs):
    pad = jnp.zeros((HALF_WINDOW, LANES), BF16)
    for buf in bufs:
        buf[pl.ds(0, HALF_WINDOW), :] = pad
        buf[pl.ds(SEQ + HALF_WINDOW, HALF_WINDOW), :] = pad


def _attn_fwd(q, k, v, comm=None):
    group = 8

    def body(q_ref, k_ref, v_ref, o_ref, lse_ref, q32, k32, v32, qa, qb, kp, vp, accp, mlp,
             acc0, acc1, acc2, ml0, ml1, ml2, bias_ref):
        first, mask_a, mask_b = _head_masks()
        low = lax.broadcasted_iota(jnp.int32, (1, LANES), 1) % HEAD_DIM < HEAD_DIM // 2
        _zero_key_padding((kp, vp))
        _write_band_bias(bias_ref)
        q32[...] = q_ref[...].astype(F32)
        k32[...] = k_ref[...].astype(F32)
        v32[...] = v_ref[...].astype(F32)
        natural = ((acc0, ml0), (acc1, ml1), (acc2, ml2))

        for branch, dil in enumerate(DILATIONS):
            length = SEQ // dil
            assert length >= 2 * Q_BLOCK
            chunks = _perm_chunks(dil)
            for c, rows in chunks:
                val = q32[rows, :]
                qa[_chunk(c), :] = (val * mask_a).astype(BF16)
                qb[_chunk(c), :] = (val * mask_b).astype(BF16)
                kp[_chunk(c, HALF_WINDOW), :] = k32[rows, :].astype(BF16)
                vp[_chunk(c, HALF_WINDOW), :] = v32[rows, :].astype(BF16)
            acc_dst, ml_dst = natural[branch] if dil == 1 else (accp, mlp)

            def blocks(i, carry, length=length, acc_dst=acc_dst, ml_dst=ml_dst):
                base = pl.multiple_of(i * (group * Q_BLOCK), group * Q_BLOCK)
                starts = [base + g * Q_BLOCK for g in range(group)]
                scores = [[_dot_nt(qx[pl.ds(m0, Q_BLOCK), :], kp[pl.ds(m0, K_WINDOW), :]) for qx in (qa, qb)] for m0 in starts]
                probs = []
                for m0, pair in zip(starts, scores):
                    bias = bias_ref[_band_bias_index(m0, length)]
                    stats, ps = [], []
                    for s in pair:
                        s = s + bias
                        m = jnp.max(s, axis=1, keepdims=True)
                        p = jnp.exp(s - m)
                        stats.append(jnp.where(low, m, jnp.sum(p, axis=1, keepdims=True)))
                        ps.append(p.astype(BF16))
                    ml_dst[pl.ds(m0, Q_BLOCK), :] = jnp.where(first, stats[0], stats[1])
                    probs.append(ps)
                for m0, ps in zip(starts, probs):
                    vw = vp[pl.ds(m0, K_WINDOW), :]
                    acc_dst[pl.ds(m0, Q_BLOCK), :] = jnp.where(first, _dot(ps[0], vw), _dot(ps[1], vw))
                return carry

            lax.fori_loop(0, SEQ // (group * Q_BLOCK), blocks, 0)

            if dil > 1:
                for c, rows in chunks:
                    natural[branch][0][rows, :] = accp[_chunk(c), :]
                    natural[branch][1][rows, :] = mlp[_chunk(c), :]

        for c in range(SEQ // PERM_CHUNK):
            packed = [ml[_chunk(c), :] for _, ml in natural]
            ms = [jnp.where(low, ml, pltpu.roll(ml, HEAD_DIM // 2, axis=1)) for ml in packed]
            ls = [jnp.where(low, pltpu.roll(ml, LANES - HEAD_DIM // 2, axis=1), ml) for ml in packed]
            m_all = jnp.maximum(jnp.maximum(ms[0], ms[1]), ms[2])
            es = [jnp.exp(m - m_all) for m in ms]
            z = ls[0] * es[0] + ls[1] * es[1] + ls[2] * es[2]
            num = natural[0][0][_chunk(c), :] * es[0] + natural[1][0][_chunk(c), :] * es[1] + natural[2][0][_chunk(c), :] * es[2]
            o_ref[_chunk(c), :] = num / z
            lse_ref[_chunk(c), :] = m_all + jnp.log(z)

    col = pl.BlockSpec((SEQ, LANES), lambda h: (0, h))
    padded = SEQ + 2 * HALF_WINDOW
    return _call(
        body, (q, k, v), grid=(ATTN_WIDTH // LANES,), in_specs=[col] * 3, out_specs=[col] * 2,
        out_shape=[_sds((SEQ, ATTN_WIDTH), F32)] * 2,
        scratch_shapes=[pltpu.VMEM((SEQ, LANES), F32)] * 3 + [pltpu.VMEM((SEQ, LANES), BF16)] * 2
        + [pltpu.VMEM((padded, LANES), BF16)] * 2 + [pltpu.VMEM((SEQ, LANES), F32)] * 8
        + [pltpu.VMEM((3, Q_BLOCK, K_WINDOW), F32)],
        vmem_mib=52, name="attn_fwd", comm=comm)


def _shifted(t, before, after, i):
    tile = t.shape[0]
    row = lax.broadcasted_iota(jnp.int32, (tile, 1), 0)
    before = jnp.where(i > 0, before, 0.0)
    after = jnp.where(i < SEQ // tile - 1, after, 0.0)
    return (jnp.where(row == 0, before, pltpu.roll(t, 1, axis=0)),
            jnp.where(row == tile - 1, after, pltpu.roll(t, tile - 1, axis=0)))


def _conv_parts(u, c, u_prev, c_prev, u_next, c_next, cw, i):
    t = c * u
    t_prev, t_next = _shifted(t, c_prev[7:8, :] * u_prev[7:8, :], c_next[0:1, :] * u_next[0:1, :], i)
    s = cw[0:1, :] * t_prev + cw[1:2, :] * t + cw[2:3, :] * t_next
    return t, t_prev, t_next, s


def _mix_fwd(attn, u, b, c, conv_w, g_attn, g_conv, w_out, x, g_post):
    def body(a_ref, u_ref, b_ref, c_ref, up_ref, cp_ref, un_ref, cn_ref, cw_ref, ga_ref, gc_ref, w_ref, x_ref, gp_ref,
             x1_ref, mg_ref, mix_ref):
        i = pl.program_id(0)
        _, _, _, s = _conv_parts(u_ref[...], c_ref[...], up_ref[...], cp_ref[...], un_ref[...], cn_ref[...], cw_ref[...], i)
        ya, _ = _rms(a_ref[...])
        yc, _ = _rms(b_ref[...] * s)
        merged = jnp.concatenate([ya * ga_ref[...], yc * gc_ref[...]], axis=1).astype(BF16)
        mix = _dot(merged, w_ref[...])
        ym, _ = _rms(mix)
        mg_ref[...] = merged.T
        mix_ref[...] = mix
        x1_ref[...] = x_ref[...] + ym * gp_ref[...]

    a = ATTN_WIDTH
    return _call(
        body, (attn, u, b, c, u, c, u, c, conv_w, g_attn, g_conv, w_out, x, g_post), grid=(SEQ // ROW_TILE,),
        in_specs=[_rows(a)] * 4 + [_halo_prev(a)] * 2 + [_halo_next(a)] * 2
        + [_const((3, a)), _const((1, a)), _const((1, a)), _const1((D_MODEL, D_MODEL)), _rows(D_MODEL), _const((1, D_MODEL))],
        out_specs=[_rows(D_MODEL), _cols(D_MODEL, ROW_TILE), _rows(D_MODEL)],
        out_shape=[_sds((SEQ, D_MODEL), F32), _sds((D_MODEL, SEQ), BF16), _sds((SEQ, D_MODEL), F32)],
        vmem_mib=40, name="mix_fwd")[0]


def _gu_spec():
    return pl.BlockSpec((N_DEV, FFN_TILE, FFN_BLOCK), lambda i: (0, i, 0))


def _ffn_fwd(x1, g_pre, w_gu, w_dn, g_post, comm=None):
    def body(x_ref, g_ref, wgu_ref, wdn_ref, gp_ref, x2_ref, gu_ref, f_ref):
        x1 = x_ref[...]
        y, _ = _rms(x1)
        h = (y * g_ref[...]).astype(BF16)
        f = jnp.zeros((FFN_TILE, D_MODEL), F32)
        for j in range(N_DEV // 2):
            gate = _dot_nt(h, wgu_ref[j])
            up = _dot_nt(h, wgu_ref[j + N_DEV // 2])
            gu_ref[j] = gate.astype(BF16)
            gu_ref[j + N_DEV // 2] = up.astype(BF16)
            act = (gate * jax.nn.sigmoid(gate) * up).astype(BF16)
            f = f + _dot(act, wdn_ref[pl.ds(j * FFN_BLOCK, FFN_BLOCK), :])
        yf, _ = _rms(f)
        f_ref[...] = f
        x2_ref[...] = x1 + yf * gp_ref[...]

    return _call(
        body, (x1, g_pre, w_gu, w_dn, g_post), grid=(SEQ // FFN_TILE,),
        in_specs=[_frows(D_MODEL), _const((1, D_MODEL)), _const1((N_DEV, FFN_BLOCK, D_MODEL)), _const1((FFN_HIDDEN, D_MODEL)),
                  _const((1, D_MODEL))],
        out_specs=[_frows(D_MODEL), _gu_spec(), _frows(D_MODEL)],
        out_shape=[_sds((SEQ, D_MODEL), F32), _sds((N_DEV, SEQ, FFN_BLOCK), BF16), _sds((SEQ, D_MODEL), F32)],
        vmem_mib=48, name="ffn_fwd", comm=comm)


def _loss_and_grad(y, target):
    n_tiles = SEQ // ROW_TILE

    def body(y_ref, t_ref, loss_ref, dy_ref, acc):
        i = pl.program_id(0)
        err = y_ref[...] - t_ref[...]
        dy_ref[...] = err * (1.0 / D_MODEL)

        @pl.when(i == 0)
        def _():
            acc[...] = jnp.zeros_like(acc)

        acc[...] += jnp.sum(err * err, axis=0, keepdims=True)

        @pl.when(i == n_tiles - 1)
        def _():
            loss_ref[...] = jnp.sum(acc[...], axis=1, keepdims=True) * (0.5 / D_MODEL)

    return _call(
        body, (y, target), grid=(n_tiles,), in_specs=[_rows(D_MODEL)] * 2, out_specs=[_const((1, 1)), _rows(D_MODEL)],
        out_shape=[_sds((1, 1), F32), _sds((SEQ, D_MODEL), F32)], scratch_shapes=[pltpu.VMEM((1, D_MODEL), F32)],
        name="loss_and_grad")[0]


def _accumulate(ref, value, i):
    @pl.when(i == 0)
    def _():
        ref[...] = value

    @pl.when(i > 0)
    def _():
        ref[...] += value


def _colsum(v):
    return jnp.sum(v, axis=0, keepdims=True)


def _ffn_bwd(dx2, f, x1, gu, w_gu, w_dn, g_post, g_pre, comm=None):
    half = N_DEV // 2

    def body(dx2_ref, f_ref, x1_ref, gu_ref, wgu_ref, wdn_ref, gpost_ref, gpre_ref,
             dx1_ref, df_ref, act_ref, dgu_ref, h_ref, dgpost_ref, dgpre_ref):
        i = pl.program_id(0)
        dx2 = dx2_ref[...]
        yf, rf = _rms(f_ref[...])
        _accumulate(dgpost_ref, _colsum(dx2 * yf), i)
        df = _rms_bwd(dx2 * gpost_ref[...], yf, rf).astype(BF16)
        df_ref[...] = df
        dh = jnp.zeros((FFN_TILE, D_MODEL), F32)
        for j in range(half):
            dact = _dot_nt(df, wdn_ref[pl.ds(j * FFN_BLOCK, FFN_BLOCK), :])
            gate = gu_ref[j].astype(F32)
            up = gu_ref[j + half].astype(F32)
            sig = jax.nn.sigmoid(gate)
            silu = gate * sig
            act_ref[j] = (silu * up).astype(BF16)
            dgate = (dact * up * (sig * (1.0 + gate * (1.0 - sig)))).astype(BF16)
            dup = (dact * silu).astype(BF16)
            dgu_ref[j] = dgate
            dgu_ref[j + half] = dup
            dh = dh + _dot(dgate, wgu_ref[j]) + _dot(dup, wgu_ref[j + half])
        y1, r1 = _rms(x1_ref[...])
        h_ref[...] = (y1 * gpre_ref[...]).astype(BF16)
        _accumulate(dgpre_ref, _colsum(dh * y1), i)
        dx1_ref[...] = dx2 + _rms_bwd(dh * gpre_ref[...], y1, r1)

    act_spec = pl.BlockSpec((half, FFN_TILE, FFN_BLOCK), lambda i: (0, i, 0))
    return _call(
        body, (dx2, f, x1, gu, w_gu, w_dn, g_post, g_pre), grid=(SEQ // FFN_TILE,),
        in_specs=[_frows(D_MODEL)] * 3 + [_gu_spec(), _const1((N_DEV, FFN_BLOCK, D_MODEL)), _const1((FFN_HIDDEN, D_MODEL)),
                                          _const((1, D_MODEL)), _const((1, D_MODEL))],
        out_specs=[_frows(D_MODEL), _frows(D_MODEL), act_spec, _gu_spec(), _frows(D_MODEL), _const((1, D_MODEL)), _const((1, D_MODEL))],
        out_shape=[_sds((SEQ, D_MODEL), F32), _sds((SEQ, D_MODEL), BF16), _sds((half, SEQ, FFN_BLOCK), BF16),
                   _sds((N_DEV, SEQ, FFN_BLOCK), BF16), _sds((SEQ, D_MODEL), BF16), _sds((1, D_MODEL), F32), _sds((1, D_MODEL), F32)],
        vmem_mib=52, name="ffn_bwd", comm=comm)


def _wgrad(a, g, a_spec, g_spec, out_shape, out_spec, n_blocks, name, a_is_transposed, comm=None):
    def body(a_ref, g_ref, o_ref):
        gv = g_ref[...].reshape(SEQ, g_ref.shape[-1])
        if a_is_transposed:
            part = _dot(a_ref[...].reshape(a_ref.shape[-2], SEQ), gv)
        else:
            part = _dot_tn(a_ref[...].reshape(SEQ, a_ref.shape[-1]), gv)
        o_ref[...] = part.astype(BF16).reshape(o_ref.shape)

    res, landed = _call(body, (a, g), grid=(n_blocks,), in_specs=[a_spec, g_spec], out_specs=[out_spec],
                        out_shape=[_sds(out_shape, BF16)], vmem_mib=48, name=name, comm=comm)
    return res[0], landed


def _wgrad_gate_up(h, dgu, comm=None):
    return _wgrad(dgu, h, pl.BlockSpec((1, SEQ, FFN_BLOCK), lambda j: (j, 0, 0)), _const1((SEQ, D_MODEL)),
                  (N_DEV, FFN_BLOCK, D_MODEL), pl.BlockSpec((1, FFN_BLOCK, D_MODEL), lambda j: (j, 0, 0)), N_DEV, "wgrad_gate_up", False,
                  comm=comm)


def _wgrad_down(act, df):
    return _wgrad(act, df, pl.BlockSpec((1, SEQ, FFN_BLOCK), lambda j: (j, 0, 0)), _const1((SEQ, D_MODEL)),
                  (FFN_HIDDEN, D_MODEL), pl.BlockSpec((FFN_BLOCK, D_MODEL), lambda j: (j, 0)), N_DEV // 2, "wgrad_down", False)[0]


def _wgrad_out(merged_t, dmix):
    return _wgrad(merged_t, dmix, _const1((D_MODEL, SEQ)), pl.BlockSpec((SEQ, 512), lambda j: (0, j)),
                  (D_MODEL, D_MODEL), pl.BlockSpec((D_MODEL, 512), lambda j: (0, j)), 2, "wgrad_out", True)[0]


def _wgrad_in(h_t, dproj, comm=None):
    return _wgrad(h_t, dproj, _const1((D_MODEL, SEQ)), pl.BlockSpec((SEQ, 512), lambda j: (0, j)),
                  (D_MODEL, IN_PROJ_WIDTH), pl.BlockSpec((D_MODEL, 512), lambda j: (0, j)), IN_PROJ_WIDTH // 512, "wgrad_in", True,
                  comm=comm)


def _mix_bwd(dx1, mix, attn, u, b, c, conv_w, g_attn, g_conv, g_post, w_out):
    def body(dx1_ref, mix_ref, a_ref, u_ref, b_ref, c_ref, up_ref, cp_ref, un_ref, cn_ref, cw_ref, ga_ref, gc_ref, gp_ref, w_ref,
             dmix_ref, da_ref, ds_ref, db_ref, dgp_ref, dga_ref, dgc_ref):
        i = pl.program_id(0)
        dx1 = dx1_ref[...]
        ym, rm = _rms(mix_ref[...])
        _accumulate(dgp_ref, _colsum(dx1 * ym), i)
        dmix = _rms_bwd(dx1 * gp_ref[...], ym, rm).astype(BF16)
        dmix_ref[...] = dmix
        dmerged = _dot_nt(dmix, w_ref[...])
        dna, dnc = dmerged[:, :ATTN_WIDTH], dmerged[:, ATTN_WIDTH:]
        ya, ra = _rms(a_ref[...])
        _accumulate(dga_ref, _colsum(dna * ya), i)
        da_ref[...] = _rms_bwd(dna * ga_ref[...], ya, ra)
        _, _, _, s = _conv_parts(u_ref[...], c_ref[...], up_ref[...], cp_ref[...], un_ref[...], cn_ref[...], cw_ref[...], i)
        gate_b = b_ref[...]
        yc, rc = _rms(gate_b * s)
        _accumulate(dgc_ref, _colsum(dnc * yc), i)
        dy = _rms_bwd(dnc * gc_ref[...], yc, rc)
        db_ref[...] = dy * s
        ds_ref[...] = dy * gate_b

    a = ATTN_WIDTH
    return _call(
        body, (dx1, mix, attn, u, b, c, u, c, u, c, conv_w, g_attn, g_conv, g_post, w_out), grid=(SEQ // ROW_TILE,),
        in_specs=[_rows(D_MODEL)] * 2 + [_rows(a)] * 4 + [_halo_prev(a)] * 2 + [_halo_next(a)] * 2
        + [_const((3, a)), _const((1, a)), _const((1, a)), _const((1, D_MODEL)), _const1((D_MODEL, D_MODEL))],
        out_specs=[_rows(D_MODEL)] + [_rows(a)] * 3 + [_const((1, D_MODEL)), _const((1, a)), _const((1, a))],
        out_shape=[_sds((SEQ, D_MODEL), BF16)] + [_sds((SEQ, a), F32)] * 3 + [_sds((1, D_MODEL), F32), _sds((1, a), F32), _sds((1, a), F32)],
        vmem_mib=40, name="mix_bwd")[0]


def _attn_bwd(q, k, v, do, o, lse, comm=None):
    group = 8

    def body(q_ref, k_ref, v_ref, do_ref, o_ref, lse_ref, dq_ref, dk_ref, dv_ref,
             q32, k32, v32, qa, qb, doa, dob, kp, vp, lsep, dlp, dnat, dqp, dkp, dvp, bias_ref):
        first, mask_a, mask_b = _head_masks()
        _zero_key_padding((kp, vp))
        _write_band_bias(bias_ref)
        q32[...] = q_ref[...].astype(F32)
        k32[...] = k_ref[...].astype(F32)
        v32[...] = v_ref[...].astype(F32)
        for c in range(SEQ // PERM_CHUNK):
            prod = do_ref[_chunk(c), :] * o_ref[_chunk(c), :]
            d_a = jnp.sum(prod * mask_a, axis=1, keepdims=True)
            d_b = jnp.sum(prod * mask_b, axis=1, keepdims=True)
            dnat[_chunk(c), :] = jnp.where(first, d_a, d_b)

        for step, dil in enumerate(DILATIONS[1:] + DILATIONS[:1]):
            length = SEQ // dil
            assert length >= 2 * Q_BLOCK
            chunks = _perm_chunks(dil)
            for c, rows in chunks:
                val = q32[rows, :]
                qa[_chunk(c), :] = (val * mask_a).astype(BF16)
                qb[_chunk(c), :] = (val * mask_b).astype(BF16)
                val = do_ref[rows, :]
                doa[_chunk(c), :] = (val * mask_a).astype(BF16)
                dob[_chunk(c), :] = (val * mask_b).astype(BF16)
                kp[_chunk(c, HALF_WINDOW), :] = k32[rows, :].astype(BF16)
                vp[_chunk(c, HALF_WINDOW), :] = v32[rows, :].astype(BF16)
                lsep[_chunk(c), :] = lse_ref[rows, :]
                dlp[_chunk(c), :] = dnat[rows, :]
            zero = jnp.zeros((PERM_CHUNK, LANES), F32)
            for c in range(SEQ // PERM_CHUNK):
                dkp[_chunk(c), :] = zero
                dvp[_chunk(c), :] = zero
            dkp[pl.ds(SEQ, 2 * HALF_WINDOW), :] = zero[:2 * HALF_WINDOW]
            dvp[pl.ds(SEQ, 2 * HALF_WINDOW), :] = zero[:2 * HALF_WINDOW]

            heads = ((qa, doa, 0), (qb, dob, HEAD_DIM))

            def blocks(i, carry, length=length):
                base = pl.multiple_of(i * (group * Q_BLOCK), group * Q_BLOCK)
                starts = [base + g * Q_BLOCK for g in range(group)]
                raw = [[(_dot_nt(qx[pl.ds(m0, Q_BLOCK), :], kp[pl.ds(m0, K_WINDOW), :]),
                         _dot_nt(dox[pl.ds(m0, Q_BLOCK), :], vp[pl.ds(m0, K_WINDOW), :])) for qx, dox, _ in heads]
                       for m0 in starts]
                grads = []
                for m0, pair in zip(starts, raw):
                    bias = bias_ref[_band_bias_index(m0, length)]
                    lse_b, d_b = lsep[pl.ds(m0, Q_BLOCK), :], dlp[pl.ds(m0, Q_BLOCK), :]
                    out = []
                    for (s, dp), (_, _, col) in zip(pair, heads):
                        p = jnp.exp(s + bias - lse_b[:, col:col + 1])
                        out.append(((p * (dp - d_b[:, col:col + 1])).astype(BF16), p.astype(BF16)))
                    grads.append(out)
                for m0, out in zip(starts, grads):
                    qrows, krows = pl.ds(m0, Q_BLOCK), pl.ds(m0, K_WINDOW)
                    kw = kp[krows, :]
                    dk = jnp.zeros((K_WINDOW, LANES), F32)
                    dv = jnp.zeros((K_WINDOW, LANES), F32)
                    for (ds, p), (qx, dox, _) in zip(out, heads):
                        dk = dk + _dot_tn(ds, qx[qrows, :])
                        dv = dv + _dot_tn(p, dox[qrows, :])
                    dqp[qrows, :] = jnp.where(first, _dot(out[0][0], kw), _dot(out[1][0], kw)) * (HEAD_DIM ** -0.5)
                    dkp[krows, :] += dk
                    dvp[krows, :] += dv
                return carry

            lax.fori_loop(0, SEQ // (group * Q_BLOCK), blocks, 0)

            for c, rows in chunks:
                g_q, g_k, g_v = dqp[_chunk(c), :], dkp[_chunk(c, HALF_WINDOW), :], dvp[_chunk(c, HALF_WINDOW), :]
                if step == 0:
                    dq_ref[rows, :] = g_q
                    dk_ref[rows, :] = g_k
                    dv_ref[rows, :] = g_v
                else:
                    dq_ref[rows, :] = dq_ref[rows, :] + g_q
                    dk_ref[rows, :] = dk_ref[rows, :] + g_k
                    dv_ref[rows, :] = dv_ref[rows, :] + g_v

    col = pl.BlockSpec((SEQ, LANES), lambda h: (0, h))
    col1 = pl.BlockSpec((SEQ, LANES), lambda h: (0, h), pipeline_mode=pl.Buffered(1))
    padded = SEQ + 2 * HALF_WINDOW
    return _call(
        body, (q, k, v, do, o, lse), grid=(ATTN_WIDTH // LANES,), in_specs=[col1] * 6, out_specs=[col] * 3,
        out_shape=[_sds((SEQ, ATTN_WIDTH), F32)] * 3,
        scratch_shapes=[pltpu.VMEM((SEQ, LANES), F32)] * 3 + [pltpu.VMEM((SEQ, LANES), BF16)] * 4
        + [pltpu.VMEM((padded, LANES), BF16)] * 2 + [pltpu.VMEM((SEQ, LANES), F32)] * 4 + [pltpu.VMEM((padded, LANES), F32)] * 2
        + [pltpu.VMEM((3, Q_BLOCK, K_WINDOW), F32)],
        vmem_mib=56, name="attn_bwd", comm=comm)


def _inproj_bwd(dq, dk, dv, ds, db, u, c, conv_w, tc, tsa, tsb, w_in, x, g_pre, dx1, comm=None):
    def body(dq_ref, dk_ref, dv_ref, ds_ref, db_ref, u_ref, c_ref, dsp_ref, up_ref, cp_ref, dsn_ref, un_ref, cn_ref, cw_ref,
             tc_ref, tsa_ref, tsb_ref, w_ref, x_ref, g_ref, dx1_ref, dx_ref, dproj_ref, h_ref, dg_ref, dcw_ref):
        i = pl.program_id(0)
        cw = cw_ref[...]
        u, c = u_ref[...], c_ref[...]
        t, t_prev, t_next, _ = _conv_parts(u, c, up_ref[...], cp_ref[...], un_ref[...], cn_ref[...], cw, i)
        ds = ds_ref[...]
        ds_prev, ds_next = _shifted(ds, dsp_ref[7:8, :], dsn_ref[0:1, :], i)
        dt = cw[0:1, :] * ds_next + cw[1:2, :] * ds + cw[2:3, :] * ds_prev
        _accumulate(dcw_ref, jnp.concatenate([_colsum(ds * t_prev), _colsum(ds * t), _colsum(ds * t_next)], axis=0), i)
        tc_, tsa_, tsb_ = tc_ref[...], tsa_ref[...], tsb_ref[...]
        dproj = jnp.concatenate(
            [_rotate_transposed(dq_ref[...], tc_, tsa_, tsb_), _rotate_transposed(dk_ref[...], tc_, tsa_, tsb_), dv_ref[...],
             dt * c, db_ref[...], dt * u], axis=1).astype(BF16)
        dproj_ref[...] = dproj
        dh = _dot_nt(dproj, w_ref[...])
        y, r = _rms(x_ref[...])
        h_ref[...] = (y * g_ref[...]).astype(BF16).T
        _accumulate(dg_ref, _colsum(dh * y), i)
        dx_ref[...] = dx1_ref[...] + _rms_bwd(dh * g_ref[...], y, r)

    a = ATTN_WIDTH
    tile = FFN_TILE

    def rows(width):
        return _rows(width, tile)

    return _call(
        body, (dq, dk, dv, ds, db, u, c, ds, u, c, ds, u, c, conv_w, tc, tsa, tsb, w_in, x, g_pre, dx1), grid=(SEQ // tile,),
        in_specs=[rows(a)] * 7 + [_halo_prev(a, tile)] * 3 + [_halo_next(a, tile)] * 3
        + [_const((3, a)), rows(LANES), rows(LANES), rows(LANES), _const1((D_MODEL, IN_PROJ_WIDTH)), rows(D_MODEL),
           _const((1, D_MODEL)), rows(D_MODEL)],
        out_specs=[rows(D_MODEL), rows(IN_PROJ_WIDTH), _cols(D_MODEL, tile), _const((1, D_MODEL)), _const((3, a))],
        out_shape=[_sds((SEQ, D_MODEL), F32), _sds((SEQ, IN_PROJ_WIDTH), BF16), _sds((D_MODEL, SEQ), BF16),
                   _sds((1, D_MODEL), F32), _sds((3, a), F32)],
        vmem_mib=48, name="inproj_bwd", comm=comm)


def _adamw_math(w, g, m, v):
    m = ADAM_B1 * m + (1.0 - ADAM_B1) * g
    v = ADAM_B2 * v + (1.0 - ADAM_B2) * (g * g)
    m_hat = m / (1.0 - ADAM_B1 ** ADAM_STEP)
    v_hat = v / (1.0 - ADAM_B2 ** ADAM_STEP)
    delta = -ADAM_LR * (m_hat / (jnp.sqrt(v_hat) + ADAM_EPS) + ADAM_WD * w)
    return delta, m, v


def _sum_parts(p_ref):
    g = p_ref[0].astype(F32)
    for k in range(1, N_DEV):
        g = g + p_ref[k].astype(F32)
    return g


def _adamw_layers(parts, w, m, v, row_tile, name, comm=None):
    _, rows, cols = w.shape
    n_tiles = rows // row_tile

    def body(*refs):
        p_refs = refs[:DEPTH]
        w_ref, m_ref, v_ref, g_ref, d_ref, nm_ref, nv_ref = refs[DEPTH:]
        layer = pl.program_id(0)
        for l, p_ref in enumerate(p_refs):
            @pl.when(layer == l)
            def _(p_ref=p_ref):
                g = _sum_parts(p_ref)
                g_ref[0] = g
                d_ref[0], nm_ref[0], nv_ref[0] = _adamw_math(w_ref[0], g, m_ref[0], v_ref[0])

    def part_spec(l):
        return pl.BlockSpec((N_DEV, row_tile, cols),
                            lambda layer, i: (0, jnp.where(layer == l, i, jnp.where(layer < l, 0, n_tiles - 1)), 0))

    tile = pl.BlockSpec((1, row_tile, cols), lambda layer, i: (layer, i, 0))
    return _call(body, (*parts, w, m, v), grid=(DEPTH, n_tiles), in_specs=[part_spec(l) for l in range(DEPTH)] + [tile] * 3,
                 out_specs=[tile] * 4, out_shape=[_sds(w.shape, F32)] * 4, name=name, comm=comm)


def _adamw_sum8(parts, w, m, v, name):
    def body(p_ref, w_ref, m_ref, v_ref, g_ref, d_ref, nm_ref, nv_ref):
        g = _sum_parts(p_ref)
        g_ref[...] = g
        d_ref[...], nm_ref[...], nv_ref[...] = _adamw_math(w_ref[...], g, m_ref[...], v_ref[...])

    return pl.pallas_call(body, out_shape=[_sds(w.shape, F32)] * 4, name=name)(parts, w, m, v)


def _adamw_plain(g, w, m, v, name):
    def body(g_ref, w_ref, m_ref, v_ref, d_ref, nm_ref, nv_ref):
        d_ref[...], nm_ref[...], nv_ref[...] = _adamw_math(w_ref[...], g_ref[...], m_ref[...], v_ref[...])

    return pl.pallas_call(body, out_shape=[_sds(w.shape, F32)] * 3, name=name)(g, w, m, v)


SMALL_ROWS = 8
GAIN_NAMES = ("pre_mix_norm", "post_mix_norm", "pre_ffn_norm", "post_ffn_norm", "attn_out_norm", "conv_out_norm")


def _pack_small(pre_mix, post_mix, pre_ffn, post_ffn, attn_out, conv_out, taps):
    zeros = jnp.zeros((1, ATTN_WIDTH), F32)
    return jnp.concatenate([
        pre_mix, post_mix, pre_ffn, post_ffn, jnp.concatenate([attn_out, conv_out], axis=1),
        jnp.concatenate([taps[0:1], taps[1:2]], axis=1), jnp.concatenate([taps[2:3], zeros], axis=1),
        jnp.zeros((1, D_MODEL), F32)], axis=0)


def _rope(positions):
    inv_freq = ROPE_THETA ** (-jnp.arange(0, ROPE_DIM, 2, dtype=F32) / ROPE_DIM)
    lane = jnp.arange(LANES) % HEAD_DIM
    freq_row = jnp.where(lane < ROPE_DIM, inv_freq[lane % (ROPE_DIM // 2)], 0.0).reshape(1, LANES).astype(F32)
    return _rope_tables(positions.reshape(SEQ, 1), freq_row)


def _layer_forward(h, gains, taps, tables, w, inproj_comm=None, attn_comm=None, ffn_comm=None):
    (q, k, v, u, b, c), landed = _inproj_fwd(h, gains["pre_mix_norm"], w["in"], *tables, comm=inproj_comm)
    if inproj_comm is not None:
        w = {**w, **dict(zip(inproj_comm.kinds, landed))}
    (attn, lse), landed = _attn_fwd(q, k, v, comm=attn_comm)
    if attn_comm is not None:
        w = {**w, **dict(zip(attn_comm.kinds, landed))}
    x1, merged, mix = _mix_fwd(attn, u, b, c, taps, gains["attn_out_norm"], gains["conv_out_norm"], w["out"], h, gains["post_mix_norm"])
    (x2, gu, f), landed_next = _ffn_fwd(x1, gains["pre_ffn_norm"], w["gu"], w["dn"], gains["post_ffn_norm"], comm=ffn_comm)
    return x2, (h, q, k, v, u, b, c, attn, lse, merged, mix, x1, gu, f), w, landed_next


def _layer_backward(dx, saved, gains, taps, tables, w, ffn_comm=None, scatter=False):
    x0, q, k, v, u, b, c, attn, lse, merged, mix, x1, gu, f = saved
    (dx1, df, act, dgu, h2, dg_post_ffn, dg_pre_ffn), landed_prev = _ffn_bwd(
        dx, f, x1, gu, w["gu"], w["dn"], gains["post_ffn_norm"], gains["pre_ffn_norm"], comm=ffn_comm)
    g_dn = _wgrad_down(act, df)
    g_gu, landed_dn = _wgrad_gate_up(h2, dgu, comm=_Scatter(("dn",), (g_dn,)) if scatter else None)
    dmix, dattn, ds, db, dg_post_mix, dg_attn, dg_conv = _mix_bwd(
        dx1, mix, attn, u, b, c, taps, gains["attn_out_norm"], gains["conv_out_norm"], gains["post_mix_norm"], w["out"])
    g_out = _wgrad_out(merged, dmix)
    attn_comm = _Scatter(("gu",), (g_gu,)) if scatter else None
    wgrad_comm = _Scatter(("out",), (g_out,)) if scatter else None
    (dq, dk, dv), landed_gu = _attn_bwd(q, k, v, dattn, attn, lse, comm=attn_comm)
    dx0, dproj, h1, dg_pre_mix, dtaps = _inproj_bwd(
        dq, dk, dv, ds, db, u, c, taps, *tables, w["in"], x0, gains["pre_mix_norm"], dx1)[0]
    g_in, landed_out = _wgrad_in(h1, dproj, comm=wgrad_comm)
    small = _pack_small(dg_pre_mix, dg_post_mix, dg_pre_ffn, dg_post_ffn, dg_attn, dg_conv, dtaps)
    rest = dict(gu=landed_gu[0], dn=landed_dn[0], out=landed_out[0]) if scatter else dict(gu=g_gu, dn=g_dn, out=g_out)
    return dx0, g_in, small, landed_prev, rest


def kernel(x, positions, pre_mix_norm, w_in, conv_w, attn_out_norm, conv_out_norm, w_out, post_mix_norm, pre_ffn_norm, w_gate_up, w_down, post_ffn_norm, loss_target, m_pre_mix_norm, m_w_in, m_conv_w, m_attn_out_norm, m_conv_out_norm, m_w_out, m_post_mix_norm, m_pre_ffn_norm, m_w_gate_up, m_w_down, m_post_ffn_norm, v_pre_mix_norm, v_w_in, v_conv_w, v_attn_out_norm, v_conv_out_norm, v_w_out, v_post_mix_norm, v_pre_ffn_norm, v_w_gate_up, v_w_down, v_post_ffn_norm):
    mx, my, mc = _place()
    me = _block_of(mx, my, mc)
    conv_channels = conv_w.shape[-1]

    taps_flat = jnp.pad(conv_w.reshape(-1), (0, 8 * LANES - conv_w.size)).reshape(8, LANES)
    taps_all = _allgather_small(taps_flat, "allgather_taps").reshape(N_DEV, 8 * LANES)[:, :conv_w.size]
    conv_w_full = taps_all.reshape(N_DEV, DEPTH, 3, conv_channels).transpose(1, 2, 0, 3).reshape(DEPTH, 3, ATTN_WIDTH)

    def hidden_major(a):
        return jnp.swapaxes(a, 1, 2)

    big_w = dict(zip(WEIGHT_KINDS, (w_in, w_out, hidden_major(w_gate_up), w_down)))
    big_m = dict(zip(WEIGHT_KINDS, (m_w_in, m_w_out, hidden_major(m_w_gate_up), m_w_down)))
    big_v = dict(zip(WEIGHT_KINDS, (v_w_in, v_w_out, hidden_major(v_w_gate_up), v_w_down)))
    shards = {kind: big_w[kind].astype(BF16) for kind in WEIGHT_KINDS}
    all_gains = dict(pre_mix_norm=pre_mix_norm, attn_out_norm=attn_out_norm, conv_out_norm=conv_out_norm, post_mix_norm=post_mix_norm,
                     pre_ffn_norm=pre_ffn_norm, post_ffn_norm=post_ffn_norm)

    def gather(kinds, l):
        return _Gather(kinds, [shards[kind][l] for kind in kinds])

    def gains(l):
        return {name: g[l:l + 1] for name, g in all_gains.items()}

    tables = _rope(positions)

    early = ("in", "out", "dn")
    weights = [dict(zip(early[:1], _comm_only(gather(early[:1], 0), "allgather_first")))] + [None] * (DEPTH - 1)
    saved = [None] * DEPTH
    h = x[0]
    for l in range(DEPTH):
        inproj_comm = gather(early[1:], 0) if l == 0 else None
        ffn_comm = gather(early, l + 1) if l + 1 < DEPTH else None
        h, saved[l], weights[l], landed = _layer_forward(h, gains(l), conv_w_full[l], tables, weights[l], inproj_comm,
                                                         gather(("gu",), l), ffn_comm)
        if ffn_comm is not None:
            weights[l + 1] = dict(zip(ffn_comm.kinds, landed))

    loss, dx = _loss_and_grad(h, loss_target[0])
    loss = lax.psum(loss[0, 0], ("x", "y", "c"))

    parts = {kind: [None] * DEPTH for kind in WEIGHT_KINDS}
    small_grads = [None] * DEPTH
    g_in_above = None
    for l in reversed(range(DEPTH)):
        ffn_comm = _Scatter(("in",), (g_in_above,)) if g_in_above is not None else None
        dx, g_in_above, small_grads[l], landed, rest = _layer_backward(dx, saved[l], gains(l), conv_w_full[l], tables, weights[l], ffn_comm, True)
        if ffn_comm is not None:
            parts["in"][l + 1] = landed[0]
        for kind, part in rest.items():
            parts[kind][l] = part

    parts["in"][0] = _comm_only(_Scatter(("in",), (g_in_above,)), "scatter_last")[0]
    tiles = {"in": 256, "out": 128, "gu": 176, "dn": 176}
    big = {kind: _adamw_layers(parts[kind], big_w[kind], big_m[kind], big_v[kind], tiles[kind], "adamw_" + kind)[0]
           for kind in WEIGHT_KINDS}

    packed = jnp.concatenate(small_grads, axis=0)
    gathered = _allgather_small(packed, "allgather_small_grads")

    def pack_state(state):
        rows = [_pack_small(*[state[name][l:l + 1] for name in GAIN_NAMES], jnp.zeros((3, ATTN_WIDTH), F32)) for l in range(DEPTH)]
        return jnp.concatenate(rows, axis=0)

    sw = pack_state(all_gains)
    sm = pack_state(dict(zip(GAIN_NAMES, (m_pre_mix_norm, m_post_mix_norm, m_pre_ffn_norm, m_post_ffn_norm, m_attn_out_norm, m_conv_out_norm))))
    sv = pack_state(dict(zip(GAIN_NAMES, (v_pre_mix_norm, v_post_mix_norm, v_pre_ffn_norm, v_post_ffn_norm, v_attn_out_norm, v_conv_out_norm))))
    sg, sd, snm, snv = _adamw_sum8(gathered, sw, sm, sv, "adamw_small")

    def unpack(p):
        p = p.reshape(DEPTH, SMALL_ROWS, D_MODEL)
        return dict(pre_mix_norm=p[:, 0], post_mix_norm=p[:, 1], pre_ffn_norm=p[:, 2], post_ffn_norm=p[:, 3],
                    attn_out_norm=p[:, 4, :ATTN_WIDTH], conv_out_norm=p[:, 4, ATTN_WIDTH:])

    small = [unpack(p) for p in (sg, sd, snm, snv)]
    sg3 = sg.reshape(DEPTH, SMALL_ROWS, D_MODEL)
    taps_grad_full = jnp.stack([sg3[:, 5, :ATTN_WIDTH], sg3[:, 5, ATTN_WIDTH:], sg3[:, 6, :ATTN_WIDTH]], axis=1)
    taps_grad = lax.dynamic_slice_in_dim(taps_grad_full, me * conv_channels, conv_channels, axis=2)

    def flat(a):
        return a.reshape(DEPTH * 3, conv_channels)

    td, tnm, tnv = _adamw_plain(flat(taps_grad), flat(conv_w), flat(m_conv_w), flat(v_conv_w), "adamw_taps")
    taps = [taps_grad] + [a.reshape(conv_w.shape) for a in (td, tnm, tnv)]

    def leaves(o):
        s = small[o]
        return (s["pre_mix_norm"], big["in"][o], taps[o], s["attn_out_norm"], s["conv_out_norm"], big["out"][o], s["post_mix_norm"],
                s["pre_ffn_norm"], hidden_major(big["gu"][o]), big["dn"][o], s["post_ffn_norm"])

    return (loss, dx[None], *leaves(0), *leaves(1), *leaves(2), *leaves(3))
```

```python
import math

import jax
import jax.numpy as jnp
from jax import lax
from jax.experimental import pallas as pl
from jax.experimental.pallas import tpu as pltpu

F32 = jnp.float32
BF16 = jnp.bfloat16
MESH = pl.DeviceIdType.MESH

SEQ = 4096
D_MODEL = 1024
DEPTH = 4
N_DEV = 8
ATTN_WIDTH = 512
IN_PROJ_WIDTH = 3072
FFN_HIDDEN = 2816
FFN_BLOCK = 2 * FFN_HIDDEN // N_DEV
W_IN_BLOCK = IN_PROJ_WIDTH // N_DEV
W_OUT_BLOCK = D_MODEL // N_DEV
W_DOWN_BLOCK = FFN_HIDDEN // N_DEV
HEAD_DIM = 64
ROPE_DIM = 16
ROPE_THETA = 500000.0
DILATIONS = (1, 4, 16)
HALF_WINDOW = 64
RMS_EPS = 1e-6
NEG_INF = -1e30
LANES = 128
Q_BLOCK = 128
K_WINDOW = Q_BLOCK + 2 * HALF_WINDOW
PERM_CHUNK = 256
ROW_TILE = 512
FFN_TILE = 256
WGRAD_TILE = SEQ
ADAM_LR, ADAM_B1, ADAM_B2, ADAM_EPS, ADAM_WD, ADAM_STEP = 0.001, 0.9, 0.999, 1e-08, 0.01, 10
MIB = 1024 * 1024
PINNED_BYTES = 256 * 1024

WEIGHT_KINDS = ("in", "out", "gu", "dn")
FULL_SHAPES = {"in": (D_MODEL, IN_PROJ_WIDTH), "out": (D_MODEL, D_MODEL), "gu": (N_DEV, FFN_BLOCK, D_MODEL), "dn": (FFN_HIDDEN, D_MODEL)}
SHARD_SHAPES = {"in": (D_MODEL, W_IN_BLOCK), "out": (W_OUT_BLOCK, D_MODEL), "gu": (FFN_BLOCK, D_MODEL), "dn": (W_DOWN_BLOCK, D_MODEL)}
ANY = pl.BlockSpec(memory_space=pl.ANY)


def _sds(shape, dtype):
    return jax.ShapeDtypeStruct(shape, dtype)


def _rows(width, tile=ROW_TILE):
    return pl.BlockSpec((tile, width), lambda i: (i, 0))


def _frows(width):
    return _rows(width, FFN_TILE)


def _cols(height, tile):
    return pl.BlockSpec((height, tile), lambda i: (0, i))


def _const(shape):
    return pl.BlockSpec(shape, lambda i: (0,) * len(shape))


def _const1(shape):
    return pl.BlockSpec(shape, lambda i: (0,) * len(shape), pipeline_mode=pl.Buffered(1))


HALO_ROWS = 16


def _halo_prev(width, tile=ROW_TILE):
    return pl.BlockSpec((HALO_ROWS, width), lambda i: (jnp.maximum(i * (tile // HALO_ROWS) - 1, 0), 0))


def _halo_next(width, tile=ROW_TILE):
    return pl.BlockSpec((HALO_ROWS, width), lambda i: (jnp.minimum((i + 1) * (tile // HALO_ROWS), SEQ // HALO_ROWS - 1), 0))


def _rms(x):
    r = lax.rsqrt(jnp.mean(x * x, axis=-1, keepdims=True) + RMS_EPS)
    return x * r, r


def _rms_bwd(dn, y, r):
    return r * (dn - y * jnp.mean(dn * y, axis=-1, keepdims=True))


def _dot(a, b):
    return jnp.dot(a, b, preferred_element_type=F32)


def _dot_nt(a, b):
    return lax.dot_general(a, b, (((1,), (1,)), ((), ())), preferred_element_type=F32)


def _dot_tn(a, b):
    return lax.dot_general(a, b, (((0,), (0,)), ((), ())), preferred_element_type=F32)


def _place():
    return lax.axis_index("x"), lax.axis_index("y"), lax.axis_index("c")


def _block_of(px, py, pc):
    return 4 * px + 2 * py + pc


def _weight_block(ref, kind, blk):
    if kind == "in":
        return ref.at[:, pl.ds(blk * W_IN_BLOCK, W_IN_BLOCK)]
    if kind == "out":
        return ref.at[pl.ds(blk * W_OUT_BLOCK, W_OUT_BLOCK), :]
    if kind == "gu":
        return ref.at[blk]
    return ref.at[pl.ds(blk * W_DOWN_BLOCK, W_DOWN_BLOCK), :]


def _dma_semaphores(n):
    return [pltpu.SemaphoreType.DMA((n, 7)), pltpu.SemaphoreType.DMA((n, 7)), pltpu.SemaphoreType.DMA((n,))]


class _Gather:
    def __init__(self, kinds, shards):
        self.kinds, self.operands = tuple(kinds), list(shards)
        self.tag = "gather_" + "_".join(kinds)
        self.out_shape = [_sds(FULL_SHAPES[k], BF16) for k in kinds]
        self.scratch = _dma_semaphores(len(kinds))

    def _parties(self):
        x, y, c = _place()
        return (x, y, c), (x, y, 1 - c), [(1 - x, y), (x, 1 - y), (1 - x, 1 - y)], c

    def _copy(self, outs, sems, w, k, block, to, src=None):
        dst = _weight_block(outs[w], self.kinds[w], _block_of(*block))
        return pltpu.make_async_remote_copy(src_ref=dst if src is None else src, dst_ref=dst, send_sem=sems[0].at[w, k],
                                            recv_sem=sems[1].at[w, k], device_id=to, device_id_type=MESH)

    def _own(self, srcs, outs, sems, w, me):
        return pltpu.make_async_copy(srcs[w], _weight_block(outs[w], self.kinds[w], _block_of(*me)), sems[2].at[w])

    def _first(self, srcs, outs, sems, w):
        me, sibling, chips, c = self._parties()
        return [self._copy(outs, sems, w, 0, me, sibling, src=srcs[w])] + [
            self._copy(outs, sems, w, 1 + j, me, (*chip, c), src=srcs[w]) for j, chip in enumerate(chips)]

    def start(self, srcs, outs, sems):
        me = self._parties()[0]
        for w in range(len(self.kinds)):
            self._own(srcs, outs, sems, w, me).start()
            for cp in self._first(srcs, outs, sems, w):
                cp.start()

    def forward(self, srcs, outs, sems):
        me, sibling, chips, c = self._parties()
        for j, chip in enumerate(chips):
            for w in range(len(self.kinds)):
                self._copy(outs, sems, w, 1 + j, (*chip, c), me).wait_recv()
                self._copy(outs, sems, w, 4 + j, (*chip, c), sibling).start()

    def finish(self, srcs, outs, sems):
        me, sibling, chips, c = self._parties()
        for w in range(len(self.kinds)):
            self._copy(outs, sems, w, 0, sibling, me).wait_recv()
            for j, chip in enumerate(chips):
                self._copy(outs, sems, w, 4 + j, (*chip, 1 - c), me).wait_recv()
        for w in range(len(self.kinds)):
            for cp in self._first(srcs, outs, sems, w):
                cp.wait_send()
            for j, chip in enumerate(chips):
                self._copy(outs, sems, w, 4 + j, (*chip, c), sibling).wait_send()
            self._own(srcs, outs, sems, w, me).wait()


def _peers(x, y, c):
    return [(x ^ a, y ^ b, c ^ e) for a in (0, 1) for b in (0, 1) for e in (0, 1) if (a, b, e) != (0, 0, 0)]


class _Scatter:
    def __init__(self, kinds, grads):
        self.kinds, self.operands = tuple(kinds), list(grads)
        self.tag = "scatter_" + "_".join(kinds)
        self.out_shape = [_sds((N_DEV,) + SHARD_SHAPES[k], BF16) for k in kinds]
        self.scratch = _dma_semaphores(len(kinds))

    def _copies(self, srcs, outs, sems):
        x, y, c = _place()
        me = _block_of(x, y, c)
        copies = []
        for w, kind in enumerate(self.kinds):
            copies.append(pltpu.make_async_copy(_weight_block(srcs[w], kind, me), outs[w].at[me], sems[2].at[w]))
            for k, peer in enumerate(_peers(x, y, c)):
                copies.append(pltpu.make_async_remote_copy(
                    src_ref=_weight_block(srcs[w], kind, _block_of(*peer)), dst_ref=outs[w].at[me],
                    send_sem=sems[0].at[w, k], recv_sem=sems[1].at[w, k], device_id=peer, device_id_type=MESH))
        return copies

    def start(self, srcs, outs, sems):
        for cp in self._copies(srcs, outs, sems):
            cp.start()

    def forward(self, srcs, outs, sems):
        pass

    def finish(self, srcs, outs, sems):
        for cp in self._copies(srcs, outs, sems):
            cp.wait()


def _in_hbm(a):
    return pltpu.with_memory_space_constraint(a, pltpu.HBM) if a.size * a.dtype.itemsize >= PINNED_BYTES else a


def _out_hbm(s):
    return pltpu.HBM(s.shape, s.dtype) if math.prod(s.shape) * jnp.dtype(s.dtype).itemsize >= PINNED_BYTES else s


def _call(body, args, *, grid, in_specs, out_specs, out_shape, scratch_shapes=(), vmem_mib=None, name, comm=None):
    kwargs = {} if vmem_mib is None else dict(compiler_params=pltpu.CompilerParams(vmem_limit_bytes=vmem_mib * MIB))
    in_specs, out_specs, out_shape, scratch_shapes = list(in_specs), list(out_specs), list(out_shape), list(scratch_shapes)
    args = [_in_hbm(a) for a in args]
    out_shape = [_out_hbm(s) for s in out_shape]
    if comm is None:
        res = pl.pallas_call(body, grid=grid, in_specs=in_specs, out_specs=out_specs, out_shape=out_shape,
                             scratch_shapes=scratch_shapes, name=name, **kwargs)(*args)
        return list(res), None
    n_in, n_out, n_scr = len(in_specs), len(out_specs), len(scratch_shapes)
    c_in, c_out = len(comm.operands), len(comm.out_shape)
    last = math.prod(grid) - 1

    def carried(*refs):
        cuts = [n_in, c_in, n_out, c_out, n_scr]
        parts, at = [], 0
        for n in cuts:
            parts.append(refs[at:at + n])
            at += n
        ins, c_ins, outs, c_outs, scr = parts
        sems = refs[at:]
        step = pl.program_id(0)
        for axis in range(1, len(grid)):
            step = step * grid[axis] + pl.program_id(axis)

        @pl.when(step == 0)
        def _():
            comm.start(c_ins, c_outs, sems)

        @pl.when(step == last)
        def _():
            comm.forward(c_ins, c_outs, sems)

        body(*ins, *outs, *scr)

        @pl.when(step == last)
        def _():
            comm.finish(c_ins, c_outs, sems)

    res = pl.pallas_call(carried, grid=grid, in_specs=in_specs + [ANY] * c_in, out_specs=out_specs + [ANY] * c_out,
                         out_shape=out_shape + [_out_hbm(s) for s in comm.out_shape], scratch_shapes=scratch_shapes + comm.scratch,
                         name=name + "_" + comm.tag, **kwargs)(*args, *[_in_hbm(a) for a in comm.operands])
    return list(res[:n_out]), list(res[n_out:])


def _comm_only(comm, name):
    def body(*refs):
        n_in, n_out = len(comm.operands), len(comm.out_shape)
        srcs, outs, sems = refs[:n_in], refs[n_in:n_in + n_out], refs[n_in + n_out:]
        comm.start(srcs, outs, sems)
        comm.forward(srcs, outs, sems)
        comm.finish(srcs, outs, sems)

    return pl.pallas_call(body, in_specs=[ANY] * len(comm.operands), out_specs=[ANY] * len(comm.out_shape),
                          out_shape=[_out_hbm(s) for s in comm.out_shape], scratch_shapes=comm.scratch,
                          name=name)(*[_in_hbm(a) for a in comm.operands])


def _allgather_small(v, name):
    def body(v_ref, o_ref, send, recv, local):
        x, y, c = _place()
        me = _block_of(x, y, c)
        mine = pltpu.make_async_copy(v_ref, o_ref.at[me], local)
        mine.start()
        copies = [mine]
        for k, peer in enumerate(_peers(x, y, c)):
            cp = pltpu.make_async_remote_copy(src_ref=v_ref, dst_ref=o_ref.at[me], send_sem=send.at[k], recv_sem=recv.at[k],
                                              device_id=peer, device_id_type=MESH)
            cp.start()
            copies.append(cp)
        for cp in copies:
            cp.wait()

    vm = pl.BlockSpec(memory_space=pltpu.VMEM)
    return pl.pallas_call(
        body, in_specs=[vm], out_specs=vm, out_shape=_sds((N_DEV,) + v.shape, F32),
        scratch_shapes=[pltpu.SemaphoreType.DMA((7,)), pltpu.SemaphoreType.DMA((7,)), pltpu.SemaphoreType.DMA],
        name=name)(v)


def _rope_tables(pos_col, freq_row):
    def body(p_ref, f_ref, c_ref, sa_ref, sb_ref):
        ang = p_ref[...].astype(F32) * f_ref[...]
        lane = lax.broadcasted_iota(jnp.int32, ang.shape, 1) % HEAD_DIM
        cos, sin = jnp.cos(ang), jnp.sin(ang)
        c_ref[...] = jnp.where(lane < ROPE_DIM, cos, 1.0)
        sa_ref[...] = jnp.where(lane < ROPE_DIM // 2, -sin, 0.0)
        sb_ref[...] = jnp.where((lane >= ROPE_DIM // 2) & (lane < ROPE_DIM), sin, 0.0)

    return pl.pallas_call(
        body, grid=(SEQ // ROW_TILE,), in_specs=[_rows(1), _const((1, LANES))], out_specs=[_rows(LANES)] * 3,
        out_shape=[_sds((SEQ, LANES), F32)] * 3, name="rope_tables")(pos_col, freq_row)


def _rotate(t, c, sa, sb):
    parts = []
    for g in range(ATTN_WIDTH // LANES):
        tg = t[:, g * LANES:(g + 1) * LANES]
        parts.append(tg * c + pltpu.roll(tg, LANES - 8, axis=1) * sa + pltpu.roll(tg, 8, axis=1) * sb)
    return jnp.concatenate(parts, axis=1)


def _rotate_transposed(dt, c, sa, sb):
    parts = []
    for g in range(ATTN_WIDTH // LANES):
        dg = dt[:, g * LANES:(g + 1) * LANES]
        parts.append(dg * c + pltpu.roll(dg * sa, 8, axis=1) + pltpu.roll(dg * sb, LANES - 8, axis=1))
    return jnp.concatenate(parts, axis=1)


def _inproj_fwd(x, g_pre, w_in, tc, tsa, tsb, comm=None):
    def body(x_ref, g_ref, w_ref, c_ref, sa_ref, sb_ref, q_ref, k_ref, v_ref, u_ref, b_ref, cc_ref):
        y, _ = _rms(x_ref[...])
        h = (y * g_ref[...]).astype(BF16)

        def proj(n):
            return _dot(h, w_ref[:, n * ATTN_WIDTH:(n + 1) * ATTN_WIDTH])

        c, sa, sb = c_ref[...], sa_ref[...], sb_ref[...]
        q_ref[...] = (_rotate(proj(0), c, sa, sb) * (HEAD_DIM ** -0.5)).astype(BF16)
        k_ref[...] = _rotate(proj(1), c, sa, sb).astype(BF16)
        v_ref[...] = proj(2).astype(BF16)
        u_ref[...] = proj(3).astype(BF16)
        b_ref[...] = proj(4).astype(BF16)
        cc_ref[...] = proj(5).astype(BF16)

    a = ATTN_WIDTH
    return _call(
        body, (x, g_pre, w_in, tc, tsa, tsb), grid=(SEQ // ROW_TILE,),
        in_specs=[_rows(D_MODEL), _const((1, D_MODEL)), _const1((D_MODEL, IN_PROJ_WIDTH)), _rows(LANES), _rows(LANES), _rows(LANES)],
        out_specs=[_rows(a)] * 6, out_shape=[_sds((SEQ, a), BF16)] * 6,
        vmem_mib=40, name="inproj_fwd", comm=comm)


def _head_masks():
    lane = lax.broadcasted_iota(jnp.int32, (1, LANES), 1)
    first = lane < HEAD_DIM
    return first, first.astype(F32), 1.0 - first.astype(F32)


def _perm_chunks(dil):
    length = SEQ // dil
    out = []
    for r in range(dil):
        for c0 in range(0, length, PERM_CHUNK):
            chunk = (r * length + c0) // PERM_CHUNK
            rows = pl.ds(c0, PERM_CHUNK) if dil == 1 else pl.ds(r + dil * c0, PERM_CHUNK, stride=dil)
            out.append((chunk, rows))
    return out


def _chunk(c, offset=0):
    return pl.ds(offset + c * PERM_CHUNK, PERM_CHUNK)


def _write_band_bias(bias_ref):
    rr = lax.broadcasted_iota(jnp.int32, (Q_BLOCK, K_WINDOW), 0)
    cc = lax.broadcasted_iota(jnp.int32, (Q_BLOCK, K_WINDOW), 1)
    band = (cc >= rr) & (cc - rr <= 2 * HALF_WINDOW)
    bias_ref[0] = jnp.where(band, 0.0, NEG_INF)
    bias_ref[1] = jnp.where(band & (cc >= HALF_WINDOW), 0.0, NEG_INF)
    bias_ref[2] = jnp.where(band & (cc < Q_BLOCK + HALF_WINDOW), 0.0, NEG_INF)


def _band_bias_index(m0, length):
    return jnp.where(m0 % length == 0, 1, 0) + jnp.where((m0 + Q_BLOCK) % length == 0, 2, 0)


def _zero_key_padding(bufs):
    pad = jnp.zeros((HALF_WINDOW, LANES), BF16)
    for buf in bufs:
        buf[pl.ds(0, HALF_WINDOW), :] = pad
        buf[pl.ds(SEQ + HALF_WINDOW, HALF_WINDOW), :] = pad


def _attn_fwd(q, k, v, comm=None):
    group = 8

    def body(q_ref, k_ref, v_ref, o_ref, lse_ref, q32, k32, v32, qa, qb, kp, vp, accp, mlp,
             acc0, acc1, acc2, ml0, ml1, ml2, bias_ref):
        first, mask_a, mask_b = _head_masks()
        low = lax.broadcasted_iota(jnp.int32, (1, LANES), 1) % HEAD_DIM < HEAD_DIM // 2
        _zero_key_padding((kp, vp))
        _write_band_bias(bias_ref)
        q32[...] = q_ref[...].astype(F32)
        k32[...] = k_ref[...].astype(F32)
        v32[...] = v_ref[...].astype(F32)
        natural = ((acc0, ml0), (acc1, ml1), (acc2, ml2))

        for branch, dil in enumerate(DILATIONS):
            length = SEQ // dil
            assert length >= 2 * Q_BLOCK
            chunks = _perm_chunks(dil)
            for c, rows in chunks:
                val = q32[rows, :]
                qa[_chunk(c), :] = (val * mask_a).astype(BF16)
                qb[_chunk(c), :] = (val * mask_b).astype(BF16)
                kp[_chunk(c, HALF_WINDOW), :] = k32[rows, :].astype(BF16)
                vp[_chunk(c, HALF_WINDOW), :] = v32[rows, :].astype(BF16)
            acc_dst, ml_dst = natural[branch] if dil == 1 else (accp, mlp)

            def blocks(i, carry, length=length, acc_dst=acc_dst, ml_dst=ml_dst):
                base = pl.multiple_of(i * (group * Q_BLOCK), group * Q_BLOCK)
                starts = [base + g * Q_BLOCK for g in range(group)]
                scores = [[_dot_nt(qx[pl.ds(m0, Q_BLOCK), :], kp[pl.ds(m0, K_WINDOW), :]) for qx in (qa, qb)] for m0 in starts]
                probs = []
                for m0, pair in zip(starts, scores):
                    bias = bias_ref[_band_bias_index(m0, length)]
                    stats, ps = [], []
                    for s in pair:
                        s = s + bias
                        m = jnp.max(s, axis=1, keepdims=True)
                        p = jnp.exp(s - m)
                        stats.append(jnp.where(low, m, jnp.sum(p, axis=1, keepdims=True)))
                        ps.append(p.astype(BF16))
                    ml_dst[pl.ds(m0, Q_BLOCK), :] = jnp.where(first, stats[0], stats[1])
                    probs.append(ps)
                for m0, ps in zip(starts, probs):
                    vw = vp[pl.ds(m0, K_WINDOW), :]
                    acc_dst[pl.ds(m0, Q_BLOCK), :] = jnp.where(first, _dot(ps[0], vw), _dot(ps[1], vw))
                return carry

            lax.fori_loop(0, SEQ // (group * Q_BLOCK), blocks, 0)

            if dil > 1:
                for c, rows in chunks:
                    natural[branch][0][rows, :] = accp[_chunk(c), :]
                    natural[branch][1][rows, :] = mlp[_chunk(c), :]

        for c in range(SEQ // PERM_CHUNK):
            packed = [ml[_chunk(c), :] for _, ml in natural]
            ms = [jnp.where(low, ml, pltpu.roll(ml, HEAD_DIM // 2, axis=1)) for ml in packed]
            ls = [jnp.where(low, pltpu.roll(ml, LANES - HEAD_DIM // 2, axis=1), ml) for ml in packed]
            m_all = jnp.maximum(jnp.maximum(ms[0], ms[1]), ms[2])
            es = [jnp.exp(m - m_all) for m in ms]
            z = ls[0] * es[0] + ls[1] * es[1] + ls[2] * es[2]
            num = natural[0][0][_chunk(c), :] * es[0] + natural[1][0][_chunk(c), :] * es[1] + natural[2][0][_chunk(c), :] * es[2]
            o_ref[_chunk(c), :] = num / z
            lse_ref[_chunk(c), :] = m_all + jnp.log(z)

    col = pl.BlockSpec((SEQ, LANES), lambda h: (0, h))
    padded = SEQ + 2 * HALF_WINDOW
    return _call(
        body, (q, k, v), grid=(ATTN_WIDTH // LANES,), in_specs=[col] * 3, out_specs=[col] * 2,
        out_shape=[_sds((SEQ, ATTN_WIDTH), F32)] * 2,
        scratch_shapes=[pltpu.VMEM((SEQ, LANES), F32)] * 3 + [pltpu.VMEM((SEQ, LANES), BF16)] * 2
        + [pltpu.VMEM((padded, LANES), BF16)] * 2 + [pltpu.VMEM((SEQ, LANES), F32)] * 8
        + [pltpu.VMEM((3, Q_BLOCK, K_WINDOW), F32)],
        vmem_mib=52, name="attn_fwd", comm=comm)


def _shifted(t, before, after, i):
    tile = t.shape[0]
    row = lax.broadcasted_iota(jnp.int32, (tile, 1), 0)
    before = jnp.where(i > 0, before, 0.0)
    after = jnp.where(i < SEQ // tile - 1, after, 0.0)
    return (jnp.where(row == 0, before, pltpu.roll(t, 1, axis=0)),
            jnp.where(row == tile - 1, after, pltpu.roll(t, tile - 1, axis=0)))


def _last_row(ref):
    return ref[HALO_ROWS - 1:HALO_ROWS, :].astype(F32)


def _first_row(ref):
    return ref[0:1, :].astype(F32)


def _conv_parts(u, c, u_prev, c_prev, u_next, c_next, cw, i):
    t = c * u
    t_prev, t_next = _shifted(t, _last_row(c_prev) * _last_row(u_prev), _first_row(c_next) * _first_row(u_next), i)
    s = cw[0:1, :] * t_prev + cw[1:2, :] * t + cw[2:3, :] * t_next
    return t, t_prev, t_next, s


def _mix_fwd(attn, u, b, c, conv_w, g_attn, g_conv, w_out, x, g_post):
    def body(a_ref, u_ref, b_ref, c_ref, up_ref, cp_ref, un_ref, cn_ref, cw_ref, ga_ref, gc_ref, w_ref, x_ref, gp_ref,
             x1_ref, mg_ref, mix_ref):
        i = pl.program_id(0)
        _, _, _, s = _conv_parts(u_ref[...].astype(F32), c_ref[...].astype(F32), up_ref, cp_ref, un_ref, cn_ref, cw_ref[...], i)
        ya, _ = _rms(a_ref[...])
        yc, _ = _rms(b_ref[...].astype(F32) * s)
        merged = jnp.concatenate([ya * ga_ref[...], yc * gc_ref[...]], axis=1).astype(BF16)
        mix = _dot(merged, w_ref[...])
        ym, _ = _rms(mix)
        mg_ref[...] = merged.T
        mix_ref[...] = mix.astype(BF16)
        x1_ref[...] = x_ref[...] + ym * gp_ref[...]

    a = ATTN_WIDTH
    return _call(
        body, (attn, u, b, c, u, c, u, c, conv_w, g_attn, g_conv, w_out, x, g_post), grid=(SEQ // ROW_TILE,),
        in_specs=[_rows(a)] * 4 + [_halo_prev(a)] * 2 + [_halo_next(a)] * 2
        + [_const((3, a)), _const((1, a)), _const((1, a)), _const1((D_MODEL, D_MODEL)), _rows(D_MODEL), _const((1, D_MODEL))],
        out_specs=[_rows(D_MODEL), _cols(D_MODEL, ROW_TILE), _rows(D_MODEL)],
        out_shape=[_sds((SEQ, D_MODEL), F32), _sds((D_MODEL, SEQ), BF16), _sds((SEQ, D_MODEL), BF16)],
        vmem_mib=40, name="mix_fwd")[0]


def _gu_spec():
    return pl.BlockSpec((N_DEV, FFN_TILE, FFN_BLOCK), lambda i: (0, i, 0))


def _ffn_fwd(x1, g_pre, w_gu, w_dn, g_post, comm=None):
    def body(x_ref, g_ref, wgu_ref, wdn_ref, gp_ref, x2_ref, gu_ref, f_ref):
        x1 = x_ref[...]
        y, _ = _rms(x1)
        h = (y * g_ref[...]).astype(BF16)
        f = jnp.zeros((ROW_TILE, D_MODEL), F32)
        for j in range(N_DEV // 2):
            gate = _dot_nt(h, wgu_ref[j])
            up = _dot_nt(h, wgu_ref[j + N_DEV // 2])
            gu_ref[j] = gate.astype(BF16)
            gu_ref[j + N_DEV // 2] = up.astype(BF16)
            act = (gate * jax.nn.sigmoid(gate) * up).astype(BF16)
            f = f + _dot(act, wdn_ref[pl.ds(j * FFN_BLOCK, FFN_BLOCK), :])
        yf, _ = _rms(f)
        f_ref[...] = f
        x2_ref[...] = x1 + yf * gp_ref[...]

    return _call(
        body, (x1, g_pre, w_gu, w_dn, g_post), grid=(SEQ // ROW_TILE,),
        in_specs=[_rows(D_MODEL), _const((1, D_MODEL)), _const1((N_DEV, FFN_BLOCK, D_MODEL)), _const1((FFN_HIDDEN, D_MODEL)),
                  _const((1, D_MODEL))],
        out_specs=[_rows(D_MODEL), pl.BlockSpec((N_DEV, ROW_TILE, FFN_BLOCK), lambda i: (0, i, 0)), _rows(D_MODEL)],
        out_shape=[_sds((SEQ, D_MODEL), F32), _sds((N_DEV, SEQ, FFN_BLOCK), BF16), _sds((SEQ, D_MODEL), F32)],
        vmem_mib=58, name="ffn_fwd", comm=comm)


def _loss_and_grad(y, target):
    n_tiles = SEQ // ROW_TILE

    def body(y_ref, t_ref, loss_ref, dy_ref, acc):
        i = pl.program_id(0)
        err = y_ref[...] - t_ref[...]
        dy_ref[...] = err * (1.0 / D_MODEL)

        @pl.when(i == 0)
        def _():
            acc[...] = jnp.zeros_like(acc)

        acc[...] += jnp.sum(err * err, axis=0, keepdims=True)

        @pl.when(i == n_tiles - 1)
        def _():
            loss_ref[...] = jnp.sum(acc[...], axis=1, keepdims=True) * (0.5 / D_MODEL)

    return _call(
        body, (y, target), grid=(n_tiles,), in_specs=[_rows(D_MODEL)] * 2, out_specs=[_const((1, 1)), _rows(D_MODEL)],
        out_shape=[_sds((1, 1), F32), _sds((SEQ, D_MODEL), F32)], scratch_shapes=[pltpu.VMEM((1, D_MODEL), F32)],
        name="loss_and_grad")[0]


def _accumulate(ref, value, i):
    @pl.when(i == 0)
    def _():
        ref[...] = value

    @pl.when(i > 0)
    def _():
        ref[...] += value


def _colsum(v):
    return jnp.sum(v, axis=0, keepdims=True)


def _ffn_bwd(dx2, f, x1, gu, w_gu, w_dn, g_post, g_pre, comm=None):
    half = N_DEV // 2

    def body(dx2_ref, f_ref, x1_ref, gu_ref, wgu_ref, wdn_ref, gpost_ref, gpre_ref,
             dx1_ref, df_ref, act_ref, dgu_ref, h_ref, dgpost_ref, dgpre_ref):
        i = pl.program_id(0)
        dx2 = dx2_ref[...]
        yf, rf = _rms(f_ref[...])
        _accumulate(dgpost_ref, _colsum(dx2 * yf), i)
        df = _rms_bwd(dx2 * gpost_ref[...], yf, rf).astype(BF16)
        df_ref[...] = df
        dh = jnp.zeros((FFN_TILE, D_MODEL), F32)
        for j in range(half):
            dact = _dot_nt(df, wdn_ref[pl.ds(j * FFN_BLOCK, FFN_BLOCK), :])
            gate = gu_ref[j].astype(F32)
            up = gu_ref[j + half].astype(F32)
            sig = jax.nn.sigmoid(gate)
            silu = gate * sig
            act_ref[j] = (silu * up).astype(BF16)
            dgate = (dact * up * (sig * (1.0 + gate * (1.0 - sig)))).astype(BF16)
            dup = (dact * silu).astype(BF16)
            dgu_ref[j] = dgate
            dgu_ref[j + half] = dup
            dh = dh + _dot(dgate, wgu_ref[j]) + _dot(dup, wgu_ref[j + half])
        y1, r1 = _rms(x1_ref[...])
        h_ref[...] = (y1 * gpre_ref[...]).astype(BF16)
        _accumulate(dgpre_ref, _colsum(dh * y1), i)
        dx1_ref[...] = dx2 + _rms_bwd(dh * gpre_ref[...], y1, r1)

    act_spec = pl.BlockSpec((half, FFN_TILE, FFN_BLOCK), lambda i: (0, i, 0))
    return _call(
        body, (dx2, f, x1, gu, w_gu, w_dn, g_post, g_pre), grid=(SEQ // FFN_TILE,),
        in_specs=[_frows(D_MODEL)] * 3 + [_gu_spec(), _const1((N_DEV, FFN_BLOCK, D_MODEL)), _const1((FFN_HIDDEN, D_MODEL)),
                                          _const((1, D_MODEL)), _const((1, D_MODEL))],
        out_specs=[_frows(D_MODEL), _frows(D_MODEL), act_spec, _gu_spec(), _frows(D_MODEL), _const((1, D_MODEL)), _const((1, D_MODEL))],
        out_shape=[_sds((SEQ, D_MODEL), F32), _sds((SEQ, D_MODEL), BF16), _sds((half, SEQ, FFN_BLOCK), BF16),
                   _sds((N_DEV, SEQ, FFN_BLOCK), BF16), _sds((SEQ, D_MODEL), BF16), _sds((1, D_MODEL), F32), _sds((1, D_MODEL), F32)],
        vmem_mib=52, name="ffn_bwd", comm=comm)


def _wgrad(a, g, a_spec, g_spec, out_shape, out_spec, n_blocks, name, a_is_transposed, comm=None):
    def body(a_ref, g_ref, o_ref):
        gv = g_ref[...].reshape(SEQ, g_ref.shape[-1])
        if a_is_transposed:
            part = _dot(a_ref[...].reshape(a_ref.shape[-2], SEQ), gv)
        else:
            part = _dot_tn(a_ref[...].reshape(SEQ, a_ref.shape[-1]), gv)
        o_ref[...] = part.astype(BF16).reshape(o_ref.shape)

    res, landed = _call(body, (a, g), grid=(n_blocks,), in_specs=[a_spec, g_spec], out_specs=[out_spec],
                        out_shape=[_sds(out_shape, BF16)], vmem_mib=48, name=name, comm=comm)
    return res[0], landed


def _wgrad_gate_up(h, dgu, comm=None):
    return _wgrad(dgu, h, pl.BlockSpec((1, SEQ, FFN_BLOCK), lambda j: (j, 0, 0)), _const1((SEQ, D_MODEL)),
                  (N_DEV, FFN_BLOCK, D_MODEL), pl.BlockSpec((1, FFN_BLOCK, D_MODEL), lambda j: (j, 0, 0)), N_DEV, "wgrad_gate_up", False,
                  comm=comm)


def _wgrad_down(act, df):
    return _wgrad(act, df, pl.BlockSpec((1, SEQ, FFN_BLOCK), lambda j: (j, 0, 0)), _const1((SEQ, D_MODEL)),
                  (FFN_HIDDEN, D_MODEL), pl.BlockSpec((FFN_BLOCK, D_MODEL), lambda j: (j, 0)), N_DEV // 2, "wgrad_down", False)[0]


def _wgrad_out(merged_t, dmix):
    return _wgrad(merged_t, dmix, _const1((D_MODEL, SEQ)), pl.BlockSpec((SEQ, 512), lambda j: (0, j)),
                  (D_MODEL, D_MODEL), pl.BlockSpec((D_MODEL, 512), lambda j: (0, j)), 2, "wgrad_out", True)[0]


def _wgrad_in(h_t, dproj, comm=None):
    return _wgrad(h_t, dproj, _const1((D_MODEL, SEQ)), pl.BlockSpec((SEQ, 512), lambda j: (0, j)),
                  (D_MODEL, IN_PROJ_WIDTH), pl.BlockSpec((D_MODEL, 512), lambda j: (0, j)), IN_PROJ_WIDTH // 512, "wgrad_in", True,
                  comm=comm)


def _mix_bwd(dx1, mix, attn, u, b, c, conv_w, g_attn, g_conv, g_post, w_out):
    def body(dx1_ref, mix_ref, a_ref, u_ref, b_ref, c_ref, up_ref, cp_ref, un_ref, cn_ref, cw_ref, ga_ref, gc_ref, gp_ref, w_ref,
             dmix_ref, da_ref, ds_ref, db_ref, dgp_ref, dga_ref, dgc_ref):
        i = pl.program_id(0)
        dx1 = dx1_ref[...]
        ym, rm = _rms(mix_ref[...].astype(F32))
        _accumulate(dgp_ref, _colsum(dx1 * ym), i)
        dmix = _rms_bwd(dx1 * gp_ref[...], ym, rm).astype(BF16)
        dmix_ref[...] = dmix
        dmerged = _dot_nt(dmix, w_ref[...])
        dna, dnc = dmerged[:, :ATTN_WIDTH], dmerged[:, ATTN_WIDTH:]
        ya, ra = _rms(a_ref[...])
        _accumulate(dga_ref, _colsum(dna * ya), i)
        da_ref[...] = _rms_bwd(dna * ga_ref[...], ya, ra)
        _, _, _, s = _conv_parts(u_ref[...].astype(F32), c_ref[...].astype(F32), up_ref, cp_ref, un_ref, cn_ref, cw_ref[...], i)
        gate_b = b_ref[...].astype(F32)
        yc, rc = _rms(gate_b * s)
        _accumulate(dgc_ref, _colsum(dnc * yc), i)
        dy = _rms_bwd(dnc * gc_ref[...], yc, rc)
        db_ref[...] = (dy * s).astype(BF16)
        ds_ref[...] = (dy * gate_b).astype(BF16)

    a = ATTN_WIDTH
    return _call(
        body, (dx1, mix, attn, u, b, c, u, c, u, c, conv_w, g_attn, g_conv, g_post, w_out), grid=(SEQ // ROW_TILE,),
        in_specs=[_rows(D_MODEL)] * 2 + [_rows(a)] * 4 + [_halo_prev(a)] * 2 + [_halo_next(a)] * 2
        + [_const((3, a)), _const((1, a)), _const((1, a)), _const((1, D_MODEL)), _const1((D_MODEL, D_MODEL))],
        out_specs=[_rows(D_MODEL)] + [_rows(a)] * 3 + [_const((1, D_MODEL)), _const((1, a)), _const((1, a))],
        out_shape=[_sds((SEQ, D_MODEL), BF16), _sds((SEQ, a), F32), _sds((SEQ, a), BF16), _sds((SEQ, a), BF16),
                   _sds((1, D_MODEL), F32), _sds((1, a), F32), _sds((1, a), F32)],
        vmem_mib=40, name="mix_bwd")[0]


def _attn_bwd(q, k, v, do, o, lse, comm=None):
    group = 8

    def body(q_ref, k_ref, v_ref, do_ref, o_ref, lse_ref, dq_ref, dk_ref, dv_ref,
             q32, k32, v32, qa, qb, doa, dob, kp, vp, lsep, dlp, dnat, dqp, dkp, dvp, bias_ref):
        first, mask_a, mask_b = _head_masks()
        _zero_key_padding((kp, vp))
        _write_band_bias(bias_ref)
        q32[...] = q_ref[...].astype(F32)
        k32[...] = k_ref[...].astype(F32)
        v32[...] = v_ref[...].astype(F32)
        for c in range(SEQ // PERM_CHUNK):
            prod = do_ref[_chunk(c), :] * o_ref[_chunk(c), :]
            d_a = jnp.sum(prod * mask_a, axis=1, keepdims=True)
            d_b = jnp.sum(prod * mask_b, axis=1, keepdims=True)
            dnat[_chunk(c), :] = jnp.where(first, d_a, d_b)

        for step, dil in enumerate(DILATIONS[1:] + DILATIONS[:1]):
            length = SEQ // dil
            assert length >= 2 * Q_BLOCK
            chunks = _perm_chunks(dil)
            for c, rows in chunks:
                val = q32[rows, :]
                qa[_chunk(c), :] = (val * mask_a).astype(BF16)
                qb[_chunk(c), :] = (val * mask_b).astype(BF16)
                val = do_ref[rows, :]
                doa[_chunk(c), :] = (val * mask_a).astype(BF16)
                dob[_chunk(c), :] = (val * mask_b).astype(BF16)
                kp[_chunk(c, HALF_WINDOW), :] = k32[rows, :].astype(BF16)
                vp[_chunk(c, HALF_WINDOW), :] = v32[rows, :].astype(BF16)
                lsep[_chunk(c), :] = lse_ref[rows, :]
                dlp[_chunk(c), :] = dnat[rows, :]
            zero = jnp.zeros((PERM_CHUNK, LANES), F32)
            for c in range(SEQ // PERM_CHUNK):
                dkp[_chunk(c), :] = zero
                dvp[_chunk(c), :] = zero
            dkp[pl.ds(SEQ, 2 * HALF_WINDOW), :] = zero[:2 * HALF_WINDOW]
            dvp[pl.ds(SEQ, 2 * HALF_WINDOW), :] = zero[:2 * HALF_WINDOW]

            heads = ((qa, doa, 0), (qb, dob, HEAD_DIM))

            def blocks(i, carry, length=length):
                base = pl.multiple_of(i * (group * Q_BLOCK), group * Q_BLOCK)
                starts = [base + g * Q_BLOCK for g in range(group)]
                raw = [[(_dot_nt(qx[pl.ds(m0, Q_BLOCK), :], kp[pl.ds(m0, K_WINDOW), :]),
                         _dot_nt(dox[pl.ds(m0, Q_BLOCK), :], vp[pl.ds(m0, K_WINDOW), :])) for qx, dox, _ in heads]
                       for m0 in starts]
                grads = []
                for m0, pair in zip(starts, raw):
                    bias = bias_ref[_band_bias_index(m0, length)]
                    lse_b, d_b = lsep[pl.ds(m0, Q_BLOCK), :], dlp[pl.ds(m0, Q_BLOCK), :]
                    out = []
                    for (s, dp), (_, _, col) in zip(pair, heads):
                        p = jnp.exp(s + bias - lse_b[:, col:col + 1])
                        out.append(((p * (dp - d_b[:, col:col + 1])).astype(BF16), p.astype(BF16)))
                    grads.append(out)
                for m0, out in zip(starts, grads):
                    qrows, krows = pl.ds(m0, Q_BLOCK), pl.ds(m0, K_WINDOW)
                    kw = kp[krows, :]
                    dk = jnp.zeros((K_WINDOW, LANES), F32)
                    dv = jnp.zeros((K_WINDOW, LANES), F32)
                    for (ds, p), (qx, dox, _) in zip(out, heads):
                        dk = dk + _dot_tn(ds, qx[qrows, :])
                        dv = dv + _dot_tn(p, dox[qrows, :])
                    dqp[qrows, :] = jnp.where(first, _dot(out[0][0], kw), _dot(out[1][0], kw)) * (HEAD_DIM ** -0.5)
                    dkp[krows, :] += dk
                    dvp[krows, :] += dv
                return carry

            lax.fori_loop(0, SEQ // (group * Q_BLOCK), blocks, 0)

            for c, rows in chunks:
                g_q, g_k, g_v = dqp[_chunk(c), :], dkp[_chunk(c, HALF_WINDOW), :], dvp[_chunk(c, HALF_WINDOW), :]
                if step == 0:
                    dq_ref[rows, :] = g_q
                    dk_ref[rows, :] = g_k
                    dv_ref[rows, :] = g_v
                else:
                    dq_ref[rows, :] = dq_ref[rows, :] + g_q
                    dk_ref[rows, :] = dk_ref[rows, :] + g_k
                    dv_ref[rows, :] = dv_ref[rows, :] + g_v

    col = pl.BlockSpec((SEQ, LANES), lambda h: (0, h))
    col1 = pl.BlockSpec((SEQ, LANES), lambda h: (0, h), pipeline_mode=pl.Buffered(1))
    padded = SEQ + 2 * HALF_WINDOW
    return _call(
        body, (q, k, v, do, o, lse), grid=(ATTN_WIDTH // LANES,), in_specs=[col1] * 6, out_specs=[col] * 3,
        out_shape=[_sds((SEQ, ATTN_WIDTH), F32)] * 3,
        scratch_shapes=[pltpu.VMEM((SEQ, LANES), F32)] * 3 + [pltpu.VMEM((SEQ, LANES), BF16)] * 4
        + [pltpu.VMEM((padded, LANES), BF16)] * 2 + [pltpu.VMEM((SEQ, LANES), F32)] * 4 + [pltpu.VMEM((padded, LANES), F32)] * 2
        + [pltpu.VMEM((3, Q_BLOCK, K_WINDOW), F32)],
        vmem_mib=56, name="attn_bwd", comm=comm)


def _inproj_bwd(dq, dk, dv, ds, db, u, c, conv_w, tc, tsa, tsb, w_in, x, g_pre, dx1):
    def body(dq_ref, dk_ref, dv_ref, ds_ref, db_ref, u_ref, c_ref, dsp_ref, up_ref, cp_ref, dsn_ref, un_ref, cn_ref, cw_ref,
             tc_ref, tsa_ref, tsb_ref, w_ref, x_ref, g_ref, dx1_ref, dx_ref, dproj_ref, h_ref, dg_ref, dcw_ref):
        i = pl.program_id(0)
        cw = cw_ref[...]
        u, c = u_ref[...].astype(F32), c_ref[...].astype(F32)
        t, t_prev, t_next, _ = _conv_parts(u, c, up_ref, cp_ref, un_ref, cn_ref, cw, i)
        ds = ds_ref[...].astype(F32)
        ds_prev, ds_next = _shifted(ds, _last_row(dsp_ref), _first_row(dsn_ref), i)
        dt = cw[0:1, :] * ds_next + cw[1:2, :] * ds + cw[2:3, :] * ds_prev
        _accumulate(dcw_ref, jnp.concatenate([_colsum(ds * t_prev), _colsum(ds * t), _colsum(ds * t_next)], axis=0), i)
        tc_, tsa_, tsb_ = tc_ref[...], tsa_ref[...], tsb_ref[...]
        dproj = jnp.concatenate(
            [_rotate_transposed(dq_ref[...], tc_, tsa_, tsb_).astype(BF16), _rotate_transposed(dk_ref[...], tc_, tsa_, tsb_).astype(BF16),
             dv_ref[...].astype(BF16), (dt * c).astype(BF16), db_ref[...], (dt * u).astype(BF16)], axis=1)
        dproj_ref[...] = dproj
        dh = _dot_nt(dproj, w_ref[...])
        y, r = _rms(x_ref[...])
        h_ref[...] = (y * g_ref[...]).astype(BF16).T
        _accumulate(dg_ref, _colsum(dh * y), i)
        dx_ref[...] = dx1_ref[...] + _rms_bwd(dh * g_ref[...], y, r)

    a = ATTN_WIDTH
    tile = FFN_TILE

    def rows(width):
        return _rows(width, tile)

    return _call(
        body, (dq, dk, dv, ds, db, u, c, ds, u, c, ds, u, c, conv_w, tc, tsa, tsb, w_in, x, g_pre, dx1), grid=(SEQ // tile,),
        in_specs=[rows(a)] * 7 + [_halo_prev(a, tile)] * 3 + [_halo_next(a, tile)] * 3
        + [_const((3, a)), rows(LANES), rows(LANES), rows(LANES), _const1((D_MODEL, IN_PROJ_WIDTH)), rows(D_MODEL),
           _const((1, D_MODEL)), rows(D_MODEL)],
        out_specs=[rows(D_MODEL), rows(IN_PROJ_WIDTH), _cols(D_MODEL, tile), _const((1, D_MODEL)), _const((3, a))],
        out_shape=[_sds((SEQ, D_MODEL), F32), _sds((SEQ, IN_PROJ_WIDTH), BF16), _sds((D_MODEL, SEQ), BF16),
                   _sds((1, D_MODEL), F32), _sds((3, a), F32)],
        vmem_mib=48, name="inproj_bwd")


def _adamw_math(w, g, m, v):
    m = ADAM_B1 * m + (1.0 - ADAM_B1) * g
    v = ADAM_B2 * v + (1.0 - ADAM_B2) * (g * g)
    m_hat = m / (1.0 - ADAM_B1 ** ADAM_STEP)
    v_hat = v / (1.0 - ADAM_B2 ** ADAM_STEP)
    delta = -ADAM_LR * (m_hat / (jnp.sqrt(v_hat) + ADAM_EPS) + ADAM_WD * w)
    return delta, m, v


def _sum_parts(p_ref):
    g = p_ref[0].astype(F32)
    for k in range(1, N_DEV):
        g = g + p_ref[k].astype(F32)
    return g


def _adamw_layers(parts, w, m, v, row_tile, name, comm=None):
    _, rows, cols = w.shape
    n_tiles = rows // row_tile

    def body(*refs):
        p_refs = refs[:DEPTH]
        w_ref, m_ref, v_ref, g_ref, d_ref, nm_ref, nv_ref = refs[DEPTH:]
        layer = pl.program_id(0)
        for l, p_ref in enumerate(p_refs):
            @pl.when(layer == l)
            def _(p_ref=p_ref):
                g = _sum_parts(p_ref)
                g_ref[0] = g
                d_ref[0], nm_ref[0], nv_ref[0] = _adamw_math(w_ref[0], g, m_ref[0], v_ref[0])

    def part_spec(l):
        return pl.BlockSpec((N_DEV, row_tile, cols),
                            lambda layer, i: (0, jnp.where(layer == l, i, jnp.where(layer < l, 0, n_tiles - 1)), 0))

    tile = pl.BlockSpec((1, row_tile, cols), lambda layer, i: (layer, i, 0))
    return _call(body, (*parts, w, m, v), grid=(DEPTH, n_tiles), in_specs=[part_spec(l) for l in range(DEPTH)] + [tile] * 3,
                 out_specs=[tile] * 4, out_shape=[_sds(w.shape, F32)] * 4, name=name, comm=comm)


def _adamw_sum8(parts, w, m, v, name):
    def body(p_ref, w_ref, m_ref, v_ref, g_ref, d_ref, nm_ref, nv_ref):
        g = _sum_parts(p_ref)
        g_ref[...] = g
        d_ref[...], nm_ref[...], nv_ref[...] = _adamw_math(w_ref[...], g, m_ref[...], v_ref[...])

    return pl.pallas_call(body, out_shape=[_sds(w.shape, F32)] * 4, name=name)(parts, w, m, v)


def _adamw_plain(g, w, m, v, name):
    def body(g_ref, w_ref, m_ref, v_ref, d_ref, nm_ref, nv_ref):
        d_ref[...], nm_ref[...], nv_ref[...] = _adamw_math(w_ref[...], g_ref[...], m_ref[...], v_ref[...])

    return pl.pallas_call(body, out_shape=[_sds(w.shape, F32)] * 3, name=name)(g, w, m, v)


SMALL_ROWS = 8
GAIN_NAMES = ("pre_mix_norm", "post_mix_norm", "pre_ffn_norm", "post_ffn_norm", "attn_out_norm", "conv_out_norm")


def _pack_small(pre_mix, post_mix, pre_ffn, post_ffn, attn_out, conv_out, taps):
    zeros = jnp.zeros((1, ATTN_WIDTH), F32)
    return jnp.concatenate([
        pre_mix, post_mix, pre_ffn, post_ffn, jnp.concatenate([attn_out, conv_out], axis=1),
        jnp.concatenate([taps[0:1], taps[1:2]], axis=1), jnp.concatenate([taps[2:3], zeros], axis=1),
        jnp.zeros((1, D_MODEL), F32)], axis=0)


def _rope(positions):
    inv_freq = ROPE_THETA ** (-jnp.arange(0, ROPE_DIM, 2, dtype=F32) / ROPE_DIM)
    lane = jnp.arange(LANES) % HEAD_DIM
    freq_row = jnp.where(lane < ROPE_DIM, inv_freq[lane % (ROPE_DIM // 2)], 0.0).reshape(1, LANES).astype(F32)
    return _rope_tables(positions.reshape(SEQ, 1), freq_row)


def _layer_forward(h, gains, taps, tables, w, inproj_comm=None, attn_comm=None, ffn_comm=None):
    (q, k, v, u, b, c), landed = _inproj_fwd(h, gains["pre_mix_norm"], w["in"], *tables, comm=inproj_comm)
    if inproj_comm is not None:
        w = {**w, **dict(zip(inproj_comm.kinds, landed))}
    (attn, lse), landed = _attn_fwd(q, k, v, comm=attn_comm)
    if attn_comm is not None:
        w = {**w, **dict(zip(attn_comm.kinds, landed))}
    x1, merged, mix = _mix_fwd(attn, u, b, c, taps, gains["attn_out_norm"], gains["conv_out_norm"], w["out"], h, gains["post_mix_norm"])
    (x2, gu, f), landed_next = _ffn_fwd(x1, gains["pre_ffn_norm"], w["gu"], w["dn"], gains["post_ffn_norm"], comm=ffn_comm)
    return x2, (h, q, k, v, u, b, c, attn, lse, merged, mix, x1, gu, f), w, landed_next


def _layer_backward(dx, saved, gains, taps, tables, w, ffn_comm=None, scatter=False):
    x0, q, k, v, u, b, c, attn, lse, merged, mix, x1, gu, f = saved
    (dx1, df, act, dgu, h2, dg_post_ffn, dg_pre_ffn), landed_prev = _ffn_bwd(
        dx, f, x1, gu, w["gu"], w["dn"], gains["post_ffn_norm"], gains["pre_ffn_norm"], comm=ffn_comm)
    g_dn = _wgrad_down(act, df)
    g_gu, landed_dn = _wgrad_gate_up(h2, dgu, comm=_Scatter(("dn",), (g_dn,)) if scatter else None)
    dmix, dattn, ds, db, dg_post_mix, dg_attn, dg_conv = _mix_bwd(
        dx1, mix, attn, u, b, c, taps, gains["attn_out_norm"], gains["conv_out_norm"], gains["post_mix_norm"], w["out"])
    g_out = _wgrad_out(merged, dmix)
    attn_comm = _Scatter(("gu",), (g_gu,)) if scatter else None
    wgrad_comm = _Scatter(("out",), (g_out,)) if scatter else None
    (dq, dk, dv), landed_gu = _attn_bwd(q, k, v, dattn, attn, lse, comm=attn_comm)
    dx0, dproj, h1, dg_pre_mix, dtaps = _inproj_bwd(
        dq, dk, dv, ds, db, u, c, taps, *tables, w["in"], x0, gains["pre_mix_norm"], dx1)[0]
    g_in, landed_out = _wgrad_in(h1, dproj, comm=wgrad_comm)
    small = _pack_small(dg_pre_mix, dg_post_mix, dg_pre_ffn, dg_post_ffn, dg_attn, dg_conv, dtaps)
    rest = dict(gu=landed_gu[0], dn=landed_dn[0], out=landed_out[0]) if scatter else dict(gu=g_gu, dn=g_dn, out=g_out)
    return dx0, g_in, small, landed_prev, rest


def kernel(x, positions, pre_mix_norm, w_in, conv_w, attn_out_norm, conv_out_norm, w_out, post_mix_norm, pre_ffn_norm, w_gate_up, w_down, post_ffn_norm, loss_target, m_pre_mix_norm, m_w_in, m_conv_w, m_attn_out_norm, m_conv_out_norm, m_w_out, m_post_mix_norm, m_pre_ffn_norm, m_w_gate_up, m_w_down, m_post_ffn_norm, v_pre_mix_norm, v_w_in, v_conv_w, v_attn_out_norm, v_conv_out_norm, v_w_out, v_post_mix_norm, v_pre_ffn_norm, v_w_gate_up, v_w_down, v_post_ffn_norm):
    mx, my, mc = _place()
    me = _block_of(mx, my, mc)
    conv_channels = conv_w.shape[-1]

    taps_flat = jnp.pad(conv_w.reshape(-1), (0, 8 * LANES - conv_w.size)).reshape(8, LANES)
    taps_all = _allgather_small(taps_flat, "allgather_taps").reshape(N_DEV, 8 * LANES)[:, :conv_w.size]
    conv_w_full = taps_all.reshape(N_DEV, DEPTH, 3, conv_channels).transpose(1, 2, 0, 3).reshape(DEPTH, 3, ATTN_WIDTH)

    def hidden_major(a):
        return jnp.swapaxes(a, 1, 2)

    big_w = dict(zip(WEIGHT_KINDS, (w_in, w_out, hidden_major(w_gate_up), w_down)))
    big_m = dict(zip(WEIGHT_KINDS, (m_w_in, m_w_out, hidden_major(m_w_gate_up), m_w_down)))
    big_v = dict(zip(WEIGHT_KINDS, (v_w_in, v_w_out, hidden_major(v_w_gate_up), v_w_down)))
    shards = {kind: big_w[kind].astype(BF16) for kind in WEIGHT_KINDS}
    all_gains = dict(pre_mix_norm=pre_mix_norm, attn_out_norm=attn_out_norm, conv_out_norm=conv_out_norm, post_mix_norm=post_mix_norm,
                     pre_ffn_norm=pre_ffn_norm, post_ffn_norm=post_ffn_norm)

    def gather(kinds, l):
        return _Gather(kinds, [shards[kind][l] for kind in kinds])

    def gains(l):
        return {name: g[l:l + 1] for name, g in all_gains.items()}

    tables = _rope(positions)

    early = ("in", "out", "dn")
    weights = [dict(zip(early[:1], _comm_only(gather(early[:1], 0), "allgather_first")))] + [None] * (DEPTH - 1)
    saved = [None] * DEPTH
    h = x[0]
    for l in range(DEPTH):
        inproj_comm = gather(early[1:], 0) if l == 0 else None
        ffn_comm = gather(early, l + 1) if l + 1 < DEPTH else None
        h, saved[l], weights[l], landed = _layer_forward(h, gains(l), conv_w_full[l], tables, weights[l], inproj_comm,
                                                         gather(("gu",), l), ffn_comm)
        if ffn_comm is not None:
            weights[l + 1] = dict(zip(ffn_comm.kinds, landed))

    loss, dx = _loss_and_grad(h, loss_target[0])
    loss = lax.psum(loss[0, 0], ("x", "y", "c"))

    parts = {kind: [None] * DEPTH for kind in WEIGHT_KINDS}
    small_grads = [None] * DEPTH
    g_in_above = None
    for l in reversed(range(DEPTH)):
        ffn_comm = _Scatter(("in",), (g_in_above,)) if g_in_above is not None else None
        dx, g_in_above, small_grads[l], landed, rest = _layer_backward(dx, saved[l], gains(l), conv_w_full[l], tables, weights[l], ffn_comm, True)
        if ffn_comm is not None:
            parts["in"][l + 1] = landed[0]
        for kind, part in rest.items():
            parts[kind][l] = part

    parts["in"][0] = _comm_only(_Scatter(("in",), (g_in_above,)), "scatter_last")[0]
    tiles = {"in": 256, "out": 128, "gu": 176, "dn": 176}
    big = {kind: _adamw_layers(parts[kind], big_w[kind], big_m[kind], big_v[kind], tiles[kind], "adamw_" + kind)[0]
           for kind in WEIGHT_KINDS}

    packed = jnp.concatenate(small_grads, axis=0)
    gathered = _allgather_small(packed, "allgather_small_grads")

    def pack_state(state):
        rows = [_pack_small(*[state[name][l:l + 1] for name in GAIN_NAMES], jnp.zeros((3, ATTN_WIDTH), F32)) for l in range(DEPTH)]
        return jnp.concatenate(rows, axis=0)

    sw = pack_state(all_gains)
    sm = pack_state(dict(zip(GAIN_NAMES, (m_pre_mix_norm, m_post_mix_norm, m_pre_ffn_norm, m_post_ffn_norm, m_attn_out_norm, m_conv_out_norm))))
    sv = pack_state(dict(zip(GAIN_NAMES, (v_pre_mix_norm, v_post_mix_norm, v_pre_ffn_norm, v_post_ffn_norm, v_attn_out_norm, v_conv_out_norm))))
    sg, sd, snm, snv = _adamw_sum8(gathered, sw, sm, sv, "adamw_small")

    def unpack(p):
        p = p.reshape(DEPTH, SMALL_ROWS, D_MODEL)
        return dict(pre_mix_norm=p[:, 0], post_mix_norm=p[:, 1], pre_ffn_norm=p[:, 2], post_ffn_norm=p[:, 3],
                    attn_out_norm=p[:, 4, :ATTN_WIDTH], conv_out_norm=p[:, 4, ATTN_WIDTH:])

    small = [unpack(p) for p in (sg, sd, snm, snv)]
    sg3 = sg.reshape(DEPTH, SMALL_ROWS, D_MODEL)
    taps_grad_full = jnp.stack([sg3[:, 5, :ATTN_WIDTH], sg3[:, 5, ATTN_WIDTH:], sg3[:, 6, :ATTN_WIDTH]], axis=1)
    taps_grad = lax.dynamic_slice_in_dim(taps_grad_full, me * conv_channels, conv_channels, axis=2)

    def flat(a):
        return a.reshape(DEPTH * 3, conv_channels)

    td, tnm, tnv = _adamw_plain(flat(taps_grad), flat(conv_w), flat(m_conv_w), flat(v_conv_w), "adamw_taps")
    taps = [taps_grad] + [a.reshape(conv_w.shape) for a in (td, tnm, tnv)]

    def leaves(o):
        s = small[o]
        return (s["pre_mix_norm"], big["in"][o], taps[o], s["attn_out_norm"], s["conv_out_norm"], big["out"][o], s["post_mix_norm"],
                s["pre_ffn_norm"], hidden_major(big["gu"][o]), big["dn"][o], s["post_ffn_norm"])

    return (loss, dx[None], *leaves(0), *leaves(1), *leaves(2), *leaves(3))
```

```python
import math

import jax
import jax.numpy as jnp
from jax import lax
from jax.experimental import pallas as pl
from jax.experimental.pallas import tpu as pltpu

F32 = jnp.float32
BF16 = jnp.bfloat16
MESH = pl.DeviceIdType.MESH

SEQ = 4096
D_MODEL = 1024
DEPTH = 4
N_DEV = 8
ATTN_WIDTH = 512
IN_PROJ_WIDTH = 3072
FFN_HIDDEN = 2816
FFN_BLOCK = 2 * FFN_HIDDEN // N_DEV
W_IN_BLOCK = IN_PROJ_WIDTH // N_DEV
W_OUT_BLOCK = D_MODEL // N_DEV
W_DOWN_BLOCK = FFN_HIDDEN // N_DEV
HEAD_DIM = 64
ROPE_DIM = 16
ROPE_THETA = 500000.0
DILATIONS = (1, 4, 16)
HALF_WINDOW = 64
RMS_EPS = 1e-6
NEG_INF = -1e30
LANES = 128
Q_BLOCK = 128
K_WINDOW = Q_BLOCK + 2 * HALF_WINDOW
PERM_CHUNK = 256
ROW_TILE = 512
FFN_TILE = 256
WGRAD_TILE = SEQ
ADAM_LR, ADAM_B1, ADAM_B2, ADAM_EPS, ADAM_WD, ADAM_STEP = 0.001, 0.9, 0.999, 1e-08, 0.01, 10
MIB = 1024 * 1024
PINNED_BYTES = 256 * 1024

WEIGHT_KINDS = ("in", "out", "gu", "dn")
FULL_SHAPES = {"in": (D_MODEL, IN_PROJ_WIDTH), "out": (D_MODEL, D_MODEL), "gu": (N_DEV, FFN_BLOCK, D_MODEL), "dn": (FFN_HIDDEN, D_MODEL)}
SHARD_SHAPES = {"in": (D_MODEL, W_IN_BLOCK), "out": (W_OUT_BLOCK, D_MODEL), "gu": (FFN_BLOCK, D_MODEL), "dn": (W_DOWN_BLOCK, D_MODEL)}
ANY = pl.BlockSpec(memory_space=pl.ANY)


def _sds(shape, dtype):
    return jax.ShapeDtypeStruct(shape, dtype)


def _rows(width, tile=ROW_TILE):
    return pl.BlockSpec((tile, width), lambda i: (i, 0))


def _frows(width):
    return _rows(width, FFN_TILE)


def _cols(height, tile):
    return pl.BlockSpec((height, tile), lambda i: (0, i))


def _const(shape):
    return pl.BlockSpec(shape, lambda i: (0,) * len(shape))


def _const1(shape):
    return pl.BlockSpec(shape, lambda i: (0,) * len(shape), pipeline_mode=pl.Buffered(1))


HALO_ROWS = 16


def _halo_prev(width, tile=ROW_TILE):
    return pl.BlockSpec((HALO_ROWS, width), lambda i: (jnp.maximum(i * (tile // HALO_ROWS) - 1, 0), 0))


def _halo_next(width, tile=ROW_TILE):
    return pl.BlockSpec((HALO_ROWS, width), lambda i: (jnp.minimum((i + 1) * (tile // HALO_ROWS), SEQ // HALO_ROWS - 1), 0))


def _rms(x):
    r = lax.rsqrt(jnp.mean(x * x, axis=-1, keepdims=True) + RMS_EPS)
    return x * r, r


def _rms_bwd(dn, y, r):
    return r * (dn - y * jnp.mean(dn * y, axis=-1, keepdims=True))


def _dot(a, b):
    return jnp.dot(a, b, preferred_element_type=F32)


def _dot_nt(a, b):
    return lax.dot_general(a, b, (((1,), (1,)), ((), ())), preferred_element_type=F32)


def _dot_tn(a, b):
    return lax.dot_general(a, b, (((0,), (0,)), ((), ())), preferred_element_type=F32)


def _place():
    return lax.axis_index("x"), lax.axis_index("y"), lax.axis_index("c")


def _block_of(px, py, pc):
    return 4 * px + 2 * py + pc


def _weight_block(ref, kind, blk):
    if kind == "in":
        return ref.at[:, pl.ds(blk * W_IN_BLOCK, W_IN_BLOCK)]
    if kind == "out":
        return ref.at[pl.ds(blk * W_OUT_BLOCK, W_OUT_BLOCK), :]
    if kind == "gu":
        return ref.at[blk]
    return ref.at[pl.ds(blk * W_DOWN_BLOCK, W_DOWN_BLOCK), :]


def _dma_semaphores(n):
    return [pltpu.SemaphoreType.DMA((n, 7)), pltpu.SemaphoreType.DMA((n, 7)), pltpu.SemaphoreType.DMA((n,))]


class _Gather:
    def __init__(self, kinds, shards):
        self.kinds, self.operands = tuple(kinds), list(shards)
        self.tag = "gather_" + "_".join(kinds)
        self.out_shape = [_sds(FULL_SHAPES[k], BF16) for k in kinds]
        self.scratch = _dma_semaphores(len(kinds))

    def _parties(self):
        x, y, c = _place()
        return (x, y, c), (x, y, 1 - c), [(1 - x, y), (x, 1 - y), (1 - x, 1 - y)], c

    def _copy(self, outs, sems, w, k, block, to, src=None):
        dst = _weight_block(outs[w], self.kinds[w], _block_of(*block))
        return pltpu.make_async_remote_copy(src_ref=dst if src is None else src, dst_ref=dst, send_sem=sems[0].at[w, k],
                                            recv_sem=sems[1].at[w, k], device_id=to, device_id_type=MESH)

    def _own(self, srcs, outs, sems, w, me):
        return pltpu.make_async_copy(srcs[w], _weight_block(outs[w], self.kinds[w], _block_of(*me)), sems[2].at[w])

    def _first(self, srcs, outs, sems, w):
        me, sibling, chips, c = self._parties()
        return [self._copy(outs, sems, w, 0, me, sibling, src=srcs[w])] + [
            self._copy(outs, sems, w, 1 + j, me, (*chip, c), src=srcs[w]) for j, chip in enumerate(chips)]

    def start(self, srcs, outs, sems):
        me = self._parties()[0]
        for w in range(len(self.kinds)):
            self._own(srcs, outs, sems, w, me).start()
            for cp in self._first(srcs, outs, sems, w):
                cp.start()

    def forward(self, srcs, outs, sems):
        me, sibling, chips, c = self._parties()
        for j, chip in enumerate(chips):
            for w in range(len(self.kinds)):
                self._copy(outs, sems, w, 1 + j, (*chip, c), me).wait_recv()
                self._copy(outs, sems, w, 4 + j, (*chip, c), sibling).start()

    def finish(self, srcs, outs, sems):
        me, sibling, chips, c = self._parties()
        for w in range(len(self.kinds)):
            self._copy(outs, sems, w, 0, sibling, me).wait_recv()
            for j, chip in enumerate(chips):
                self._copy(outs, sems, w, 4 + j, (*chip, 1 - c), me).wait_recv()
        for w in range(len(self.kinds)):
            for cp in self._first(srcs, outs, sems, w):
                cp.wait_send()
            for j, chip in enumerate(chips):
                self._copy(outs, sems, w, 4 + j, (*chip, c), sibling).wait_send()
            self._own(srcs, outs, sems, w, me).wait()


def _peers(x, y, c):
    return [(x ^ a, y ^ b, c ^ e) for a in (0, 1) for b in (0, 1) for e in (0, 1) if (a, b, e) != (0, 0, 0)]


class _Scatter:
    def __init__(self, kinds, grads):
        self.kinds, self.operands = tuple(kinds), list(grads)
        self.tag = "scatter_" + "_".join(kinds)
        self.out_shape = [_sds((N_DEV,) + SHARD_SHAPES[k], BF16) for k in kinds]
        self.scratch = _dma_semaphores(len(kinds))

    def _copies(self, srcs, outs, sems):
        x, y, c = _place()
        me = _block_of(x, y, c)
        copies = []
        for w, kind in enumerate(self.kinds):
            copies.append(pltpu.make_async_copy(_weight_block(srcs[w], kind, me), outs[w].at[me], sems[2].at[w]))
            for k, peer in enumerate(_peers(x, y, c)):
                copies.append(pltpu.make_async_remote_copy(
                    src_ref=_weight_block(srcs[w], kind, _block_of(*peer)), dst_ref=outs[w].at[me],
                    send_sem=sems[0].at[w, k], recv_sem=sems[1].at[w, k], device_id=peer, device_id_type=MESH))
        return copies

    def start(self, srcs, outs, sems):
        for cp in self._copies(srcs, outs, sems):
            cp.start()

    def forward(self, srcs, outs, sems):
        pass

    def finish(self, srcs, outs, sems):
        for cp in self._copies(srcs, outs, sems):
            cp.wait()


N_CHIPS = N_DEV // 2


class _ScatterChips:
    def __init__(self, kinds, blocks):
        self.kinds, self.operands = tuple(kinds), list(blocks)
        self.tag = "scatter_chips_" + "_".join(kinds)
        self.out_shape = [_sds((N_CHIPS,) + SHARD_SHAPES[k], BF16) for k in kinds]
        self.scratch = _dma_semaphores(len(kinds))

    def _copies(self, srcs, outs, sems):
        x, y, c = _place()
        mine = 2 * x + y
        copies = []
        for w in range(len(self.kinds)):
            copies.append(pltpu.make_async_copy(srcs[w].at[mine], outs[w].at[mine], sems[2].at[w]))
            for j, (px, py) in enumerate([(1 - x, y), (x, 1 - y), (1 - x, 1 - y)]):
                copies.append(pltpu.make_async_remote_copy(
                    src_ref=srcs[w].at[2 * px + py], dst_ref=outs[w].at[mine], send_sem=sems[0].at[w, j], recv_sem=sems[1].at[w, j],
                    device_id=(px, py, c), device_id_type=MESH))
        return copies

    def start(self, srcs, outs, sems):
        for cp in self._copies(srcs, outs, sems):
            cp.start()

    def forward(self, srcs, outs, sems):
        pass

    def finish(self, srcs, outs, sems):
        for cp in self._copies(srcs, outs, sems):
            cp.wait()


def _in_hbm(a):
    return pltpu.with_memory_space_constraint(a, pltpu.HBM) if a.size * a.dtype.itemsize >= PINNED_BYTES else a


def _out_hbm(s):
    return pltpu.HBM(s.shape, s.dtype) if math.prod(s.shape) * jnp.dtype(s.dtype).itemsize >= PINNED_BYTES else s


def _call(body, args, *, grid, in_specs, out_specs, out_shape, scratch_shapes=(), vmem_mib=None, name, comm=None):
    kwargs = {} if vmem_mib is None else dict(compiler_params=pltpu.CompilerParams(vmem_limit_bytes=vmem_mib * MIB))
    in_specs, out_specs, out_shape, scratch_shapes = list(in_specs), list(out_specs), list(out_shape), list(scratch_shapes)
    args = [_in_hbm(a) for a in args]
    out_shape = [_out_hbm(s) for s in out_shape]
    if comm is None:
        res = pl.pallas_call(body, grid=grid, in_specs=in_specs, out_specs=out_specs, out_shape=out_shape,
                             scratch_shapes=scratch_shapes, name=name, **kwargs)(*args)
        return list(res), None
    n_in, n_out, n_scr = len(in_specs), len(out_specs), len(scratch_shapes)
    c_in, c_out = len(comm.operands), len(comm.out_shape)
    last = math.prod(grid) - 1

    def carried(*refs):
        cuts = [n_in, c_in, n_out, c_out, n_scr]
        parts, at = [], 0
        for n in cuts:
            parts.append(refs[at:at + n])
            at += n
        ins, c_ins, outs, c_outs, scr = parts
        sems = refs[at:]
        step = pl.program_id(0)
        for axis in range(1, len(grid)):
            step = step * grid[axis] + pl.program_id(axis)

        @pl.when(step == 0)
        def _():
            comm.start(c_ins, c_outs, sems)

        @pl.when(step == last)
        def _():
            comm.forward(c_ins, c_outs, sems)

        body(*ins, *outs, *scr)

        @pl.when(step == last)
        def _():
            comm.finish(c_ins, c_outs, sems)

    res = pl.pallas_call(carried, grid=grid, in_specs=in_specs + [ANY] * c_in, out_specs=out_specs + [ANY] * c_out,
                         out_shape=out_shape + [_out_hbm(s) for s in comm.out_shape], scratch_shapes=scratch_shapes + comm.scratch,
                         name=name + "_" + comm.tag, **kwargs)(*args, *[_in_hbm(a) for a in comm.operands])
    return list(res[:n_out]), list(res[n_out:])


def _comm_only(comm, name):
    def body(*refs):
        n_in, n_out = len(comm.operands), len(comm.out_shape)
        srcs, outs, sems = refs[:n_in], refs[n_in:n_in + n_out], refs[n_in + n_out:]
        comm.start(srcs, outs, sems)
        comm.forward(srcs, outs, sems)
        comm.finish(srcs, outs, sems)

    return pl.pallas_call(body, in_specs=[ANY] * len(comm.operands), out_specs=[ANY] * len(comm.out_shape),
                          out_shape=[_out_hbm(s) for s in comm.out_shape], scratch_shapes=comm.scratch,
                          name=name)(*[_in_hbm(a) for a in comm.operands])


def _allgather_small(v, name):
    def body(v_ref, o_ref, send, recv, local):
        x, y, c = _place()
        me = _block_of(x, y, c)
        mine = pltpu.make_async_copy(v_ref, o_ref.at[me], local)
        mine.start()
        copies = [mine]
        for k, peer in enumerate(_peers(x, y, c)):
            cp = pltpu.make_async_remote_copy(src_ref=v_ref, dst_ref=o_ref.at[me], send_sem=send.at[k], recv_sem=recv.at[k],
                                              device_id=peer, device_id_type=MESH)
            cp.start()
            copies.append(cp)
        for cp in copies:
            cp.wait()

    vm = pl.BlockSpec(memory_space=pltpu.VMEM)
    return pl.pallas_call(
        body, in_specs=[vm], out_specs=vm, out_shape=_sds((N_DEV,) + v.shape, F32),
        scratch_shapes=[pltpu.SemaphoreType.DMA((7,)), pltpu.SemaphoreType.DMA((7,)), pltpu.SemaphoreType.DMA],
        name=name)(v)


def _rope_tables(pos_col, freq_row):
    def body(p_ref, f_ref, c_ref, sa_ref, sb_ref):
        ang = p_ref[...].astype(F32) * f_ref[...]
        lane = lax.broadcasted_iota(jnp.int32, ang.shape, 1) % HEAD_DIM
        cos, sin = jnp.cos(ang), jnp.sin(ang)
        c_ref[...] = jnp.where(lane < ROPE_DIM, cos, 1.0)
        sa_ref[...] = jnp.where(lane < ROPE_DIM // 2, -sin, 0.0)
        sb_ref[...] = jnp.where((lane >= ROPE_DIM // 2) & (lane < ROPE_DIM), sin, 0.0)

    return pl.pallas_call(
        body, grid=(SEQ // ROW_TILE,), in_specs=[_rows(1), _const((1, LANES))], out_specs=[_rows(LANES)] * 3,
        out_shape=[_sds((SEQ, LANES), F32)] * 3, name="rope_tables")(pos_col, freq_row)


def _rotate(t, c, sa, sb):
    parts = []
    for g in range(ATTN_WIDTH // LANES):
        tg = t[:, g * LANES:(g + 1) * LANES]
        parts.append(tg * c + pltpu.roll(tg, LANES - 8, axis=1) * sa + pltpu.roll(tg, 8, axis=1) * sb)
    return jnp.concatenate(parts, axis=1)


def _rotate_transposed(dt, c, sa, sb):
    parts = []
    for g in range(ATTN_WIDTH // LANES):
        dg = dt[:, g * LANES:(g + 1) * LANES]
        parts.append(dg * c + pltpu.roll(dg * sa, 8, axis=1) + pltpu.roll(dg * sb, LANES - 8, axis=1))
    return jnp.concatenate(parts, axis=1)


def _inproj_fwd(x, g_pre, w_in, tc, tsa, tsb, comm=None):
    def body(x_ref, g_ref, w_ref, c_ref, sa_ref, sb_ref, q_ref, k_ref, v_ref, u_ref, b_ref, cc_ref):
        y, _ = _rms(x_ref[...])
        h = (y * g_ref[...]).astype(BF16)

        def proj(n):
            return _dot(h, w_ref[:, n * ATTN_WIDTH:(n + 1) * ATTN_WIDTH])

        c, sa, sb = c_ref[...], sa_ref[...], sb_ref[...]
        q_ref[...] = (_rotate(proj(0), c, sa, sb) * (HEAD_DIM ** -0.5)).astype(BF16)
        k_ref[...] = _rotate(proj(1), c, sa, sb).astype(BF16)
        v_ref[...] = proj(2).astype(BF16)
        u_ref[...] = proj(3).astype(BF16)
        b_ref[...] = proj(4).astype(BF16)
        cc_ref[...] = proj(5).astype(BF16)

    a = ATTN_WIDTH
    return _call(
        body, (x, g_pre, w_in, tc, tsa, tsb), grid=(SEQ // ROW_TILE,),
        in_specs=[_rows(D_MODEL), _const((1, D_MODEL)), _const1((D_MODEL, IN_PROJ_WIDTH)), _rows(LANES), _rows(LANES), _rows(LANES)],
        out_specs=[_rows(a)] * 6, out_shape=[_sds((SEQ, a), BF16)] * 6,
        vmem_mib=40, name="inproj_fwd", comm=comm)


def _head_masks():
    lane = lax.broadcasted_iota(jnp.int32, (1, LANES), 1)
    first = lane < HEAD_DIM
    return first, first.astype(F32), 1.0 - first.astype(F32)


def _perm_chunks(dil):
    length = SEQ // dil
    out = []
    for r in range(dil):
        for c0 in range(0, length, PERM_CHUNK):
            chunk = (r * length + c0) // PERM_CHUNK
            rows = pl.ds(c0, PERM_CHUNK) if dil == 1 else pl.ds(r + dil * c0, PERM_CHUNK, stride=dil)
            out.append((chunk, rows))
    return out


def _chunk(c, offset=0):
    return pl.ds(offset + c * PERM_CHUNK, PERM_CHUNK)


def _write_band_bias(bias_ref):
    rr = lax.broadcasted_iota(jnp.int32, (Q_BLOCK, K_WINDOW), 0)
    cc = lax.broadcasted_iota(jnp.int32, (Q_BLOCK, K_WINDOW), 1)
    band = (cc >= rr) & (cc - rr <= 2 * HALF_WINDOW)
    bias_ref[0] = jnp.where(band, 0.0, NEG_INF)
    bias_ref[1] = jnp.where(band & (cc >= HALF_WINDOW), 0.0, NEG_INF)
    bias_ref[2] = jnp.where(band & (cc < Q_BLOCK + HALF_WINDOW), 0.0, NEG_INF)


def _band_bias_index(m0, length):
    return jnp.where(m0 % length == 0, 1, 0) + jnp.where((m0 + Q_BLOCK) % length == 0, 2, 0)


def _zero_key_padding(bufs):
    pad = jnp.zeros((HALF_WINDOW, LANES), BF16)
    for buf in bufs:
        buf[pl.ds(0, HALF_WINDOW), :] = pad
        buf[pl.ds(SEQ + HALF_WINDOW, HALF_WINDOW), :] = pad


def _attn_fwd(q, k, v, comm=None):
    group = 8

    def body(q_ref, k_ref, v_ref, o_ref, lse_ref, q32, k32, v32, qa, qb, kp, vp, accp, mlp,
             acc0, acc1, acc2, ml0, ml1, ml2, bias_ref):
        first, mask_a, mask_b = _head_masks()
        low = lax.broadcasted_iota(jnp.int32, (1, LANES), 1) % HEAD_DIM < HEAD_DIM // 2
        _zero_key_padding((kp, vp))
        _write_band_bias(bias_ref)
        q32[...] = q_ref[...].astype(F32)
        k32[...] = k_ref[...].astype(F32)
        v32[...] = v_ref[...].astype(F32)
        natural = ((acc0, ml0), (acc1, ml1), (acc2, ml2))

        for branch, dil in enumerate(DILATIONS):
            length = SEQ // dil
            assert length >= 2 * Q_BLOCK
            chunks = _perm_chunks(dil)
            for c, rows in chunks:
                val = q32[rows, :]
                qa[_chunk(c), :] = (val * mask_a).astype(BF16)
                qb[_chunk(c), :] = (val * mask_b).astype(BF16)
                kp[_chunk(c, HALF_WINDOW), :] = k32[rows, :].astype(BF16)
                vp[_chunk(c, HALF_WINDOW), :] = v32[rows, :].astype(BF16)
            acc_dst, ml_dst = natural[branch] if dil == 1 else (accp, mlp)

            def blocks(i, carry, length=length, acc_dst=acc_dst, ml_dst=ml_dst):
                base = pl.multiple_of(i * (group * Q_BLOCK), group * Q_BLOCK)
                starts = [base + g * Q_BLOCK for g in range(group)]
                scores = [[_dot_nt(qx[pl.ds(m0, Q_BLOCK), :], kp[pl.ds(m0, K_WINDOW), :]) for qx in (qa, qb)] for m0 in starts]
                probs = []
                for m0, pair in zip(starts, scores):
                    bias = bias_ref[_band_bias_index(m0, length)]
                    stats, ps = [], []
                    for s in pair:
                        s = s + bias
                        m = jnp.max(s, axis=1, keepdims=True)
                        p = jnp.exp(s - m)
                        stats.append(jnp.where(low, m, jnp.sum(p, axis=1, keepdims=True)))
                        ps.append(p.astype(BF16))
                    ml_dst[pl.ds(m0, Q_BLOCK), :] = jnp.where(first, stats[0], stats[1])
                    probs.append(ps)
                for m0, ps in zip(starts, probs):
                    vw = vp[pl.ds(m0, K_WINDOW), :]
                    acc_dst[pl.ds(m0, Q_BLOCK), :] = jnp.where(first, _dot(ps[0], vw), _dot(ps[1], vw))
                return carry

            lax.fori_loop(0, SEQ // (group * Q_BLOCK), blocks, 0)

            if dil > 1:
                for c, rows in chunks:
                    natural[branch][0][rows, :] = accp[_chunk(c), :]
                    natural[branch][1][rows, :] = mlp[_chunk(c), :]

        for c in range(SEQ // PERM_CHUNK):
            packed = [ml[_chunk(c), :] for _, ml in natural]
            ms = [jnp.where(low, ml, pltpu.roll(ml, HEAD_DIM // 2, axis=1)) for ml in packed]
            ls = [jnp.where(low, pltpu.roll(ml, LANES - HEAD_DIM // 2, axis=1), ml) for ml in packed]
            m_all = jnp.maximum(jnp.maximum(ms[0], ms[1]), ms[2])
            es = [jnp.exp(m - m_all) for m in ms]
            z = ls[0] * es[0] + ls[1] * es[1] + ls[2] * es[2]
            num = natural[0][0][_chunk(c), :] * es[0] + natural[1][0][_chunk(c), :] * es[1] + natural[2][0][_chunk(c), :] * es[2]
            o_ref[_chunk(c), :] = num / z
            lse_ref[_chunk(c), :] = m_all + jnp.log(z)

    col = pl.BlockSpec((SEQ, LANES), lambda h: (0, h))
    padded = SEQ + 2 * HALF_WINDOW
    return _call(
        body, (q, k, v), grid=(ATTN_WIDTH // LANES,), in_specs=[col] * 3, out_specs=[col] * 2,
        out_shape=[_sds((SEQ, ATTN_WIDTH), F32)] * 2,
        scratch_shapes=[pltpu.VMEM((SEQ, LANES), F32)] * 3 + [pltpu.VMEM((SEQ, LANES), BF16)] * 2
        + [pltpu.VMEM((padded, LANES), BF16)] * 2 + [pltpu.VMEM((SEQ, LANES), F32)] * 8
        + [pltpu.VMEM((3, Q_BLOCK, K_WINDOW), F32)],
        vmem_mib=52, name="attn_fwd", comm=comm)


def _shifted(t, before, after, i):
    tile = t.shape[0]
    row = lax.broadcasted_iota(jnp.int32, (tile, 1), 0)
    before = jnp.where(i > 0, before, 0.0)
    after = jnp.where(i < SEQ // tile - 1, after, 0.0)
    return (jnp.where(row == 0, before, pltpu.roll(t, 1, axis=0)),
            jnp.where(row == tile - 1, after, pltpu.roll(t, tile - 1, axis=0)))


def _last_row(ref):
    return ref[HALO_ROWS - 1:HALO_ROWS, :].astype(F32)


def _first_row(ref):
    return ref[0:1, :].astype(F32)


def _conv_parts(u, c, u_prev, c_prev, u_next, c_next, cw, i):
    t = c * u
    t_prev, t_next = _shifted(t, _last_row(c_prev) * _last_row(u_prev), _first_row(c_next) * _first_row(u_next), i)
    s = cw[0:1, :] * t_prev + cw[1:2, :] * t + cw[2:3, :] * t_next
    return t, t_prev, t_next, s


def _mix_fwd(attn, u, b, c, conv_w, g_attn, g_conv, w_out, x, g_post):
    def body(a_ref, u_ref, b_ref, c_ref, up_ref, cp_ref, un_ref, cn_ref, cw_ref, ga_ref, gc_ref, w_ref, x_ref, gp_ref,
             x1_ref, mg_ref, mix_ref):
        i = pl.program_id(0)
        _, _, _, s = _conv_parts(u_ref[...].astype(F32), c_ref[...].astype(F32), up_ref, cp_ref, un_ref, cn_ref, cw_ref[...], i)
        ya, _ = _rms(a_ref[...])
        yc, _ = _rms(b_ref[...].astype(F32) * s)
        merged = jnp.concatenate([ya * ga_ref[...], yc * gc_ref[...]], axis=1).astype(BF16)
        mix = _dot(merged, w_ref[...])
        ym, _ = _rms(mix)
        mg_ref[...] = merged.T
        mix_ref[...] = mix.astype(BF16)
        x1_ref[...] = x_ref[...] + ym * gp_ref[...]

    a = ATTN_WIDTH
    return _call(
        body, (attn, u, b, c, u, c, u, c, conv_w, g_attn, g_conv, w_out, x, g_post), grid=(SEQ // ROW_TILE,),
        in_specs=[_rows(a)] * 4 + [_halo_prev(a)] * 2 + [_halo_next(a)] * 2
        + [_const((3, a)), _const((1, a)), _const((1, a)), _const1((D_MODEL, D_MODEL)), _rows(D_MODEL), _const((1, D_MODEL))],
        out_specs=[_rows(D_MODEL), _cols(D_MODEL, ROW_TILE), _rows(D_MODEL)],
        out_shape=[_sds((SEQ, D_MODEL), F32), _sds((D_MODEL, SEQ), BF16), _sds((SEQ, D_MODEL), BF16)],
        vmem_mib=40, name="mix_fwd")[0]


def _gu_spec():
    return pl.BlockSpec((N_DEV, FFN_TILE, FFN_BLOCK), lambda i: (0, i, 0))


def _ffn_fwd(x1, g_pre, w_gu, w_dn, g_post, comm=None):
    def body(x_ref, g_ref, wgu_ref, wdn_ref, gp_ref, x2_ref, gu_ref, f_ref):
        x1 = x_ref[...]
        y, _ = _rms(x1)
        h = (y * g_ref[...]).astype(BF16)
        f = jnp.zeros((ROW_TILE, D_MODEL), F32)
        for j in range(N_DEV // 2):
            gate = _dot_nt(h, wgu_ref[j])
            up = _dot_nt(h, wgu_ref[j + N_DEV // 2])
            gu_ref[j] = gate.astype(BF16)
            gu_ref[j + N_DEV // 2] = up.astype(BF16)
            act = (gate * jax.nn.sigmoid(gate) * up).astype(BF16)
            f = f + _dot(act, wdn_ref[pl.ds(j * FFN_BLOCK, FFN_BLOCK), :])
        yf, _ = _rms(f)
        f_ref[...] = f
        x2_ref[...] = x1 + yf * gp_ref[...]

    return _call(
        body, (x1, g_pre, w_gu, w_dn, g_post), grid=(SEQ // ROW_TILE,),
        in_specs=[_rows(D_MODEL), _const((1, D_MODEL)), _const1((N_DEV, FFN_BLOCK, D_MODEL)), _const1((FFN_HIDDEN, D_MODEL)),
                  _const((1, D_MODEL))],
        out_specs=[_rows(D_MODEL), pl.BlockSpec((N_DEV, ROW_TILE, FFN_BLOCK), lambda i: (0, i, 0)), _rows(D_MODEL)],
        out_shape=[_sds((SEQ, D_MODEL), F32), _sds((N_DEV, SEQ, FFN_BLOCK), BF16), _sds((SEQ, D_MODEL), F32)],
        vmem_mib=58, name="ffn_fwd", comm=comm)


def _loss_and_grad(y, target):
    n_tiles = SEQ // ROW_TILE

    def body(y_ref, t_ref, loss_ref, dy_ref, acc):
        i = pl.program_id(0)
        err = y_ref[...] - t_ref[...]
        dy_ref[...] = err * (1.0 / D_MODEL)

        @pl.when(i == 0)
        def _():
            acc[...] = jnp.zeros_like(acc)

        acc[...] += jnp.sum(err * err, axis=0, keepdims=True)

        @pl.when(i == n_tiles - 1)
        def _():
            loss_ref[...] = jnp.sum(acc[...], axis=1, keepdims=True) * (0.5 / D_MODEL)

    return _call(
        body, (y, target), grid=(n_tiles,), in_specs=[_rows(D_MODEL)] * 2, out_specs=[_const((1, 1)), _rows(D_MODEL)],
        out_shape=[_sds((1, 1), F32), _sds((SEQ, D_MODEL), F32)], scratch_shapes=[pltpu.VMEM((1, D_MODEL), F32)],
        name="loss_and_grad")[0]


def _accumulate(ref, value, i):
    @pl.when(i == 0)
    def _():
        ref[...] = value

    @pl.when(i > 0)
    def _():
        ref[...] += value


def _colsum(v):
    return jnp.sum(v, axis=0, keepdims=True)


def _ffn_bwd(dx2, f, x1, gu, w_gu, w_dn, g_post, g_pre, comm=None):
    half = N_DEV // 2

    def body(dx2_ref, f_ref, x1_ref, gu_ref, wgu_ref, wdn_ref, gpost_ref, gpre_ref,
             dx1_ref, df_ref, act_ref, dgu_ref, h_ref, dgpost_ref, dgpre_ref):
        i = pl.program_id(0)
        dx2 = dx2_ref[...]
        yf, rf = _rms(f_ref[...])
        _accumulate(dgpost_ref, _colsum(dx2 * yf), i)
        df = _rms_bwd(dx2 * gpost_ref[...], yf, rf).astype(BF16)
        df_ref[...] = df
        dh = jnp.zeros((FFN_TILE, D_MODEL), F32)
        for j in range(half):
            dact = _dot_nt(df, wdn_ref[pl.ds(j * FFN_BLOCK, FFN_BLOCK), :])
            gate = gu_ref[j].astype(F32)
            up = gu_ref[j + half].astype(F32)
            sig = jax.nn.sigmoid(gate)
            silu = gate * sig
            act_ref[j] = (silu * up).astype(BF16)
            dgate = (dact * up * (sig * (1.0 + gate * (1.0 - sig)))).astype(BF16)
            dup = (dact * silu).astype(BF16)
            dgu_ref[j] = dgate
            dgu_ref[j + half] = dup
            dh = dh + _dot(dgate, wgu_ref[j]) + _dot(dup, wgu_ref[j + half])
        y1, r1 = _rms(x1_ref[...])
        h_ref[...] = (y1 * gpre_ref[...]).astype(BF16)
        _accumulate(dgpre_ref, _colsum(dh * y1), i)
        dx1_ref[...] = dx2 + _rms_bwd(dh * gpre_ref[...], y1, r1)

    act_spec = pl.BlockSpec((half, FFN_TILE, FFN_BLOCK), lambda i: (0, i, 0))
    return _call(
        body, (dx2, f, x1, gu, w_gu, w_dn, g_post, g_pre), grid=(SEQ // FFN_TILE,),
        in_specs=[_frows(D_MODEL)] * 3 + [_gu_spec(), _const1((N_DEV, FFN_BLOCK, D_MODEL)), _const1((FFN_HIDDEN, D_MODEL)),
                                          _const((1, D_MODEL)), _const((1, D_MODEL))],
        out_specs=[_frows(D_MODEL), _frows(D_MODEL), act_spec, _gu_spec(), _frows(D_MODEL), _const((1, D_MODEL)), _const((1, D_MODEL))],
        out_shape=[_sds((SEQ, D_MODEL), F32), _sds((SEQ, D_MODEL), BF16), _sds((half, SEQ, FFN_BLOCK), BF16),
                   _sds((N_DEV, SEQ, FFN_BLOCK), BF16), _sds((SEQ, D_MODEL), BF16), _sds((1, D_MODEL), F32), _sds((1, D_MODEL), F32)],
        vmem_mib=52, name="ffn_bwd", comm=comm)


def _wgrad(a, g, a_spec, g_spec, out_shape, out_spec, n_blocks, name, a_is_transposed, comm=None):
    def body(a_ref, g_ref, o_ref):
        gv = g_ref[...].reshape(SEQ, g_ref.shape[-1])
        if a_is_transposed:
            part = _dot(a_ref[...].reshape(a_ref.shape[-2], SEQ), gv)
        else:
            part = _dot_tn(a_ref[...].reshape(SEQ, a_ref.shape[-1]), gv)
        o_ref[...] = part.astype(BF16).reshape(o_ref.shape)

    res, landed = _call(body, (a, g), grid=(n_blocks,), in_specs=[a_spec, g_spec], out_specs=[out_spec],
                        out_shape=[_sds(out_shape, BF16)], vmem_mib=48, name=name, comm=comm)
    return res[0], landed


def _wgrad_paired(a, g, kind, name):
    shard = SHARD_SHAPES[kind]
    per_chip = {"gu": 2, "dn": 1}[kind]
    n_compute = per_chip * N_CHIPS
    n_steps = n_compute + 1

    def body(a_ref, g_ref, o_ref, sendbuf, recvbuf, keepbuf, send_sem, recv_sem):
        t = pl.program_id(0)
        x, y, c = _place()

        def exchange(q):
            return pltpu.make_async_remote_copy(src_ref=sendbuf.at[q], dst_ref=recvbuf.at[q], send_sem=send_sem.at[q],
                                                recv_sem=recv_sem.at[q], device_id=(x, y, 1 - c), device_id_type=MESH)

        @pl.when((t >= per_chip) & (t % per_chip == 0))
        def _():
            q = t // per_chip - 1
            exchange(q).wait_recv()
            o_ref[0] = (keepbuf[...] + recvbuf[q].astype(F32)).astype(BF16)

        @pl.when(t < n_compute)
        def _():
            r = _dot_tn(a_ref[0], g_ref[...])
            q = t // per_chip
            if kind == "gu":
                @pl.when(t % 2 == c)
                def _():
                    keepbuf[...] = r

                @pl.when(t % 2 != c)
                def _():
                    sendbuf[q] = r.astype(BF16)
                    exchange(q).start()
            else:
                lower, upper = r[:W_DOWN_BLOCK], r[W_DOWN_BLOCK:]
                keepbuf[...] = jnp.where(c == 0, lower, upper)
                sendbuf[q] = jnp.where(c == 0, upper, lower).astype(BF16)
                exchange(q).start()

        @pl.when(t == n_steps - 1)
        def _():
            for q in range(N_CHIPS):
                exchange(q).wait_send()

    res, _ = _call(
        body, (a, g), grid=(n_steps,),
        in_specs=[pl.BlockSpec((1, SEQ, FFN_BLOCK), lambda t: (jnp.minimum(t, n_compute - 1), 0, 0)), _const1((SEQ, D_MODEL))],
        out_specs=[pl.BlockSpec((1,) + shard, lambda t: (jnp.clip(t // per_chip - 1, 0, N_CHIPS - 1), 0, 0))],
        out_shape=[_sds((N_CHIPS,) + shard, BF16)],
        scratch_shapes=[pltpu.VMEM((N_CHIPS,) + shard, BF16), pltpu.VMEM((N_CHIPS,) + shard, BF16), pltpu.VMEM(shard, F32),
                        pltpu.SemaphoreType.DMA((N_CHIPS,)), pltpu.SemaphoreType.DMA((N_CHIPS,))],
        vmem_mib=52, name=name)
    return res[0]


def _wgrad_out(merged_t, dmix):
    return _wgrad(merged_t, dmix, _const1((D_MODEL, SEQ)), pl.BlockSpec((SEQ, 512), lambda j: (0, j)),
                  (D_MODEL, D_MODEL), pl.BlockSpec((D_MODEL, 512), lambda j: (0, j)), 2, "wgrad_out", True)[0]


def _wgrad_in(h_t, dproj, comm=None):
    return _wgrad(h_t, dproj, _const1((D_MODEL, SEQ)), pl.BlockSpec((SEQ, 512), lambda j: (0, j)),
                  (D_MODEL, IN_PROJ_WIDTH), pl.BlockSpec((D_MODEL, 512), lambda j: (0, j)), IN_PROJ_WIDTH // 512, "wgrad_in", True,
                  comm=comm)


def _mix_bwd(dx1, mix, attn, u, b, c, conv_w, g_attn, g_conv, g_post, w_out):
    def body(dx1_ref, mix_ref, a_ref, u_ref, b_ref, c_ref, up_ref, cp_ref, un_ref, cn_ref, cw_ref, ga_ref, gc_ref, gp_ref, w_ref,
             dmix_ref, da_ref, ds_ref, db_ref, dgp_ref, dga_ref, dgc_ref):
        i = pl.program_id(0)
        dx1 = dx1_ref[...]
        ym, rm = _rms(mix_ref[...].astype(F32))
        _accumulate(dgp_ref, _colsum(dx1 * ym), i)
        dmix = _rms_bwd(dx1 * gp_ref[...], ym, rm).astype(BF16)
        dmix_ref[...] = dmix
        dmerged = _dot_nt(dmix, w_ref[...])
        dna, dnc = dmerged[:, :ATTN_WIDTH], dmerged[:, ATTN_WIDTH:]
        ya, ra = _rms(a_ref[...])
        _accumulate(dga_ref, _colsum(dna * ya), i)
        da_ref[...] = _rms_bwd(dna * ga_ref[...], ya, ra)
        _, _, _, s = _conv_parts(u_ref[...].astype(F32), c_ref[...].astype(F32), up_ref, cp_ref, un_ref, cn_ref, cw_ref[...], i)
        gate_b = b_ref[...].astype(F32)
        yc, rc = _rms(gate_b * s)
        _accumulate(dgc_ref, _colsum(dnc * yc), i)
        dy = _rms_bwd(dnc * gc_ref[...], yc, rc)
        db_ref[...] = (dy * s).astype(BF16)
        ds_ref[...] = (dy * gate_b).astype(BF16)

    a = ATTN_WIDTH
    return _call(
        body, (dx1, mix, attn, u, b, c, u, c, u, c, conv_w, g_attn, g_conv, g_post, w_out), grid=(SEQ // ROW_TILE,),
        in_specs=[_rows(D_MODEL)] * 2 + [_rows(a)] * 4 + [_halo_prev(a)] * 2 + [_halo_next(a)] * 2
        + [_const((3, a)), _const((1, a)), _const((1, a)), _const((1, D_MODEL)), _const1((D_MODEL, D_MODEL))],
        out_specs=[_rows(D_MODEL)] + [_rows(a)] * 3 + [_const((1, D_MODEL)), _const((1, a)), _const((1, a))],
        out_shape=[_sds((SEQ, D_MODEL), BF16), _sds((SEQ, a), F32), _sds((SEQ, a), BF16), _sds((SEQ, a), BF16),
                   _sds((1, D_MODEL), F32), _sds((1, a), F32), _sds((1, a), F32)],
        vmem_mib=40, name="mix_bwd")[0]


def _attn_bwd(q, k, v, do, o, lse, comm=None):
    group = 8

    def body(q_ref, k_ref, v_ref, do_ref, o_ref, lse_ref, dq_ref, dk_ref, dv_ref,
             q32, k32, v32, qa, qb, doa, dob, kp, vp, lsep, dlp, dnat, dqp, dkp, dvp, bias_ref):
        first, mask_a, mask_b = _head_masks()
        _zero_key_padding((kp, vp))
        _write_band_bias(bias_ref)
        q32[...] = q_ref[...].astype(F32)
        k32[...] = k_ref[...].astype(F32)
        v32[...] = v_ref[...].astype(F32)
        for c in range(SEQ // PERM_CHUNK):
            prod = do_ref[_chunk(c), :] * o_ref[_chunk(c), :]
            d_a = jnp.sum(prod * mask_a, axis=1, keepdims=True)
            d_b = jnp.sum(prod * mask_b, axis=1, keepdims=True)
            dnat[_chunk(c), :] = jnp.where(first, d_a, d_b)

        for step, dil in enumerate(DILATIONS[1:] + DILATIONS[:1]):
            length = SEQ // dil
            assert length >= 2 * Q_BLOCK
            chunks = _perm_chunks(dil)
            for c, rows in chunks:
                val = q32[rows, :]
                qa[_chunk(c), :] = (val * mask_a).astype(BF16)
                qb[_chunk(c), :] = (val * mask_b).astype(BF16)
                val = do_ref[rows, :]
                doa[_chunk(c), :] = (val * mask_a).astype(BF16)
                dob[_chunk(c), :] = (val * mask_b).astype(BF16)
                kp[_chunk(c, HALF_WINDOW), :] = k32[rows, :].astype(BF16)
                vp[_chunk(c, HALF_WINDOW), :] = v32[rows, :].astype(BF16)
                lsep[_chunk(c), :] = lse_ref[rows, :]
                dlp[_chunk(c), :] = dnat[rows, :]
            zero = jnp.zeros((PERM_CHUNK, LANES), F32)
            for c in range(SEQ // PERM_CHUNK):
                dkp[_chunk(c), :] = zero
                dvp[_chunk(c), :] = zero
            dkp[pl.ds(SEQ, 2 * HALF_WINDOW), :] = zero[:2 * HALF_WINDOW]
            dvp[pl.ds(SEQ, 2 * HALF_WINDOW), :] = zero[:2 * HALF_WINDOW]

            heads = ((qa, doa, 0), (qb, dob, HEAD_DIM))

            def blocks(i, carry, length=length):
                base = pl.multiple_of(i * (group * Q_BLOCK), group * Q_BLOCK)
                starts = [base + g * Q_BLOCK for g in range(group)]
                raw = [[(_dot_nt(qx[pl.ds(m0, Q_BLOCK), :], kp[pl.ds(m0, K_WINDOW), :]),
                         _dot_nt(dox[pl.ds(m0, Q_BLOCK), :], vp[pl.ds(m0, K_WINDOW), :])) for qx, dox, _ in heads]
                       for m0 in starts]
                grads = []
                for m0, pair in zip(starts, raw):
                    bias = bias_ref[_band_bias_index(m0, length)]
                    lse_b, d_b = lsep[pl.ds(m0, Q_BLOCK), :], dlp[pl.ds(m0, Q_BLOCK), :]
                    out = []
                    for (s, dp), (_, _, col) in zip(pair, heads):
                        p = jnp.exp(s + bias - lse_b[:, col:col + 1])
                        out.append(((p * (dp - d_b[:, col:col + 1])).astype(BF16), p.astype(BF16)))
                    grads.append(out)
                for m0, out in zip(starts, grads):
                    qrows, krows = pl.ds(m0, Q_BLOCK), pl.ds(m0, K_WINDOW)
                    kw = kp[krows, :]
                    dk = jnp.zeros((K_WINDOW, LANES), F32)
                    dv = jnp.zeros((K_WINDOW, LANES), F32)
                    for (ds, p), (qx, dox, _) in zip(out, heads):
                        dk = dk + _dot_tn(ds, qx[qrows, :])
                        dv = dv + _dot_tn(p, dox[qrows, :])
                    dqp[qrows, :] = jnp.where(first, _dot(out[0][0], kw), _dot(out[1][0], kw)) * (HEAD_DIM ** -0.5)
                    dkp[krows, :] += dk
                    dvp[krows, :] += dv
                return carry

            lax.fori_loop(0, SEQ // (group * Q_BLOCK), blocks, 0)

            for c, rows in chunks:
                g_q, g_k, g_v = dqp[_chunk(c), :], dkp[_chunk(c, HALF_WINDOW), :], dvp[_chunk(c, HALF_WINDOW), :]
                if step == 0:
                    dq_ref[rows, :] = g_q
                    dk_ref[rows, :] = g_k
                    dv_ref[rows, :] = g_v
                else:
                    dq_ref[rows, :] = dq_ref[rows, :] + g_q
                    dk_ref[rows, :] = dk_ref[rows, :] + g_k
                    dv_ref[rows, :] = dv_ref[rows, :] + g_v

    col = pl.BlockSpec((SEQ, LANES), lambda h: (0, h))
    col1 = pl.BlockSpec((SEQ, LANES), lambda h: (0, h), pipeline_mode=pl.Buffered(1))
    padded = SEQ + 2 * HALF_WINDOW
    return _call(
        body, (q, k, v, do, o, lse), grid=(ATTN_WIDTH // LANES,), in_specs=[col1] * 6, out_specs=[col] * 3,
        out_shape=[_sds((SEQ, ATTN_WIDTH), F32)] * 3,
        scratch_shapes=[pltpu.VMEM((SEQ, LANES), F32)] * 3 + [pltpu.VMEM((SEQ, LANES), BF16)] * 4
        + [pltpu.VMEM((padded, LANES), BF16)] * 2 + [pltpu.VMEM((SEQ, LANES), F32)] * 4 + [pltpu.VMEM((padded, LANES), F32)] * 2
        + [pltpu.VMEM((3, Q_BLOCK, K_WINDOW), F32)],
        vmem_mib=56, name="attn_bwd", comm=comm)


def _inproj_bwd(dq, dk, dv, ds, db, u, c, conv_w, tc, tsa, tsb, w_in, x, g_pre, dx1):
    def body(dq_ref, dk_ref, dv_ref, ds_ref, db_ref, u_ref, c_ref, dsp_ref, up_ref, cp_ref, dsn_ref, un_ref, cn_ref, cw_ref,
             tc_ref, tsa_ref, tsb_ref, w_ref, x_ref, g_ref, dx1_ref, dx_ref, dproj_ref, h_ref, dg_ref, dcw_ref):
        i = pl.program_id(0)
        cw = cw_ref[...]
        u, c = u_ref[...].astype(F32), c_ref[...].astype(F32)
        t, t_prev, t_next, _ = _conv_parts(u, c, up_ref, cp_ref, un_ref, cn_ref, cw, i)
        ds = ds_ref[...].astype(F32)
        ds_prev, ds_next = _shifted(ds, _last_row(dsp_ref), _first_row(dsn_ref), i)
        dt = cw[0:1, :] * ds_next + cw[1:2, :] * ds + cw[2:3, :] * ds_prev
        _accumulate(dcw_ref, jnp.concatenate([_colsum(ds * t_prev), _colsum(ds * t), _colsum(ds * t_next)], axis=0), i)
        tc_, tsa_, tsb_ = tc_ref[...], tsa_ref[...], tsb_ref[...]
        dproj = jnp.concatenate(
            [_rotate_transposed(dq_ref[...], tc_, tsa_, tsb_).astype(BF16), _rotate_transposed(dk_ref[...], tc_, tsa_, tsb_).astype(BF16),
             dv_ref[...].astype(BF16), (dt * c).astype(BF16), db_ref[...], (dt * u).astype(BF16)], axis=1)
        dproj_ref[...] = dproj
        dh = _dot_nt(dproj, w_ref[...])
        y, r = _rms(x_ref[...])
        h_ref[...] = (y * g_ref[...]).astype(BF16).T
        _accumulate(dg_ref, _colsum(dh * y), i)
        dx_ref[...] = dx1_ref[...] + _rms_bwd(dh * g_ref[...], y, r)

    a = ATTN_WIDTH
    tile = FFN_TILE

    def rows(width):
        return _rows(width, tile)

    return _call(
        body, (dq, dk, dv, ds, db, u, c, ds, u, c, ds, u, c, conv_w, tc, tsa, tsb, w_in, x, g_pre, dx1), grid=(SEQ // tile,),
        in_specs=[rows(a)] * 7 + [_halo_prev(a, tile)] * 3 + [_halo_next(a, tile)] * 3
        + [_const((3, a)), rows(LANES), rows(LANES), rows(LANES), _const1((D_MODEL, IN_PROJ_WIDTH)), rows(D_MODEL),
           _const((1, D_MODEL)), rows(D_MODEL)],
        out_specs=[rows(D_MODEL), rows(IN_PROJ_WIDTH), _cols(D_MODEL, tile), _const((1, D_MODEL)), _const((3, a))],
        out_shape=[_sds((SEQ, D_MODEL), F32), _sds((SEQ, IN_PROJ_WIDTH), BF16), _sds((D_MODEL, SEQ), BF16),
                   _sds((1, D_MODEL), F32), _sds((3, a), F32)],
        vmem_mib=48, name="inproj_bwd")


def _adamw_math(w, g, m, v):
    m = ADAM_B1 * m + (1.0 - ADAM_B1) * g
    v = ADAM_B2 * v + (1.0 - ADAM_B2) * (g * g)
    m_hat = m / (1.0 - ADAM_B1 ** ADAM_STEP)
    v_hat = v / (1.0 - ADAM_B2 ** ADAM_STEP)
    delta = -ADAM_LR * (m_hat / (jnp.sqrt(v_hat) + ADAM_EPS) + ADAM_WD * w)
    return delta, m, v


def _sum_parts(p_ref):
    g = p_ref[0].astype(F32)
    for k in range(1, p_ref.shape[0]):
        g = g + p_ref[k].astype(F32)
    return g


def _adamw_layers(parts, w, m, v, row_tile, name, comm=None):
    _, rows, cols = w.shape
    n_tiles = rows // row_tile
    senders = parts[0].shape[0]

    def body(*refs):
        p_refs = refs[:DEPTH]
        w_ref, m_ref, v_ref, g_ref, d_ref, nm_ref, nv_ref = refs[DEPTH:]
        layer = pl.program_id(0)
        for l, p_ref in enumerate(p_refs):
            @pl.when(layer == l)
            def _(p_ref=p_ref):
                g = _sum_parts(p_ref)
                g_ref[0] = g
                d_ref[0], nm_ref[0], nv_ref[0] = _adamw_math(w_ref[0], g, m_ref[0], v_ref[0])

    def part_spec(l):
        return pl.BlockSpec((senders, row_tile, cols),
                            lambda layer, i: (0, jnp.where(layer == l, i, jnp.where(layer < l, 0, n_tiles - 1)), 0))

    tile = pl.BlockSpec((1, row_tile, cols), lambda layer, i: (layer, i, 0))
    return _call(body, (*parts, w, m, v), grid=(DEPTH, n_tiles), in_specs=[part_spec(l) for l in range(DEPTH)] + [tile] * 3,
                 out_specs=[tile] * 4, out_shape=[_sds(w.shape, F32)] * 4, name=name, comm=comm)


def _adamw_sum8(parts, w, m, v, name):
    def body(p_ref, w_ref, m_ref, v_ref, g_ref, d_ref, nm_ref, nv_ref):
        g = _sum_parts(p_ref)
        g_ref[...] = g
        d_ref[...], nm_ref[...], nv_ref[...] = _adamw_math(w_ref[...], g, m_ref[...], v_ref[...])

    return pl.pallas_call(body, out_shape=[_sds(w.shape, F32)] * 4, name=name)(parts, w, m, v)


def _adamw_plain(g, w, m, v, name):
    def body(g_ref, w_ref, m_ref, v_ref, d_ref, nm_ref, nv_ref):
        d_ref[...], nm_ref[...], nv_ref[...] = _adamw_math(w_ref[...], g_ref[...], m_ref[...], v_ref[...])

    return pl.pallas_call(body, out_shape=[_sds(w.shape, F32)] * 3, name=name)(g, w, m, v)


SMALL_ROWS = 8
GAIN_NAMES = ("pre_mix_norm", "post_mix_norm", "pre_ffn_norm", "post_ffn_norm", "attn_out_norm", "conv_out_norm")


def _pack_small(pre_mix, post_mix, pre_ffn, post_ffn, attn_out, conv_out, taps):
    zeros = jnp.zeros((1, ATTN_WIDTH), F32)
    return jnp.concatenate([
        pre_mix, post_mix, pre_ffn, post_ffn, jnp.concatenate([attn_out, conv_out], axis=1),
        jnp.concatenate([taps[0:1], taps[1:2]], axis=1), jnp.concatenate([taps[2:3], zeros], axis=1),
        jnp.zeros((1, D_MODEL), F32)], axis=0)


def _rope(positions):
    inv_freq = ROPE_THETA ** (-jnp.arange(0, ROPE_DIM, 2, dtype=F32) / ROPE_DIM)
    lane = jnp.arange(LANES) % HEAD_DIM
    freq_row = jnp.where(lane < ROPE_DIM, inv_freq[lane % (ROPE_DIM // 2)], 0.0).reshape(1, LANES).astype(F32)
    return _rope_tables(positions.reshape(SEQ, 1), freq_row)


def _layer_forward(h, gains, taps, tables, w, inproj_comm=None, attn_comm=None, ffn_comm=None):
    (q, k, v, u, b, c), landed = _inproj_fwd(h, gains["pre_mix_norm"], w["in"], *tables, comm=inproj_comm)
    if inproj_comm is not None:
        w = {**w, **dict(zip(inproj_comm.kinds, landed))}
    (attn, lse), landed = _attn_fwd(q, k, v, comm=attn_comm)
    if attn_comm is not None:
        w = {**w, **dict(zip(attn_comm.kinds, landed))}
    x1, merged, mix = _mix_fwd(attn, u, b, c, taps, gains["attn_out_norm"], gains["conv_out_norm"], w["out"], h, gains["post_mix_norm"])
    (x2, gu, f), landed_next = _ffn_fwd(x1, gains["pre_ffn_norm"], w["gu"], w["dn"], gains["post_ffn_norm"], comm=ffn_comm)
    return x2, (h, q, k, v, u, b, c, attn, lse, merged, mix, x1, gu, f), w, landed_next


class _GradientExchange:
    def ffn_grads(self, act, df, h2, dgu):
        return dict(dn=_wgrad_paired(act, df, "dn", "wgrad_down_paired"), gu=_wgrad_paired(dgu, h2, "gu", "wgrad_gate_up_paired"))

    def attention_passenger(self, ffn):
        return _ScatterChips(("gu", "dn"), (ffn["gu"], ffn["dn"]))

    def w_in_call_passenger(self, g_out):
        return _Scatter(("out",), (g_out,))


def _layer_backward(dx, saved, gains, taps, tables, w, ffn_comm, exchange):
    x0, q, k, v, u, b, c, attn, lse, merged, mix, x1, gu, f = saved
    (dx1, df, act, dgu, h2, dg_post_ffn, dg_pre_ffn), landed_prev = _ffn_bwd(
        dx, f, x1, gu, w["gu"], w["dn"], gains["post_ffn_norm"], gains["pre_ffn_norm"], comm=ffn_comm)
    ffn = exchange.ffn_grads(act, df, h2, dgu)
    dmix, dattn, ds, db, dg_post_mix, dg_attn, dg_conv = _mix_bwd(
        dx1, mix, attn, u, b, c, taps, gains["attn_out_norm"], gains["conv_out_norm"], gains["post_mix_norm"], w["out"])
    g_out = _wgrad_out(merged, dmix)
    attn_comm = exchange.attention_passenger(ffn)
    (dq, dk, dv), landed_ffn = _attn_bwd(q, k, v, dattn, attn, lse, comm=attn_comm)
    dx0, dproj, h1, dg_pre_mix, dtaps = _inproj_bwd(
        dq, dk, dv, ds, db, u, c, taps, *tables, w["in"], x0, gains["pre_mix_norm"], dx1)[0]
    wgrad_comm = exchange.w_in_call_passenger(g_out)
    g_in, landed_out = _wgrad_in(h1, dproj, comm=wgrad_comm)
    small = _pack_small(dg_pre_mix, dg_post_mix, dg_pre_ffn, dg_post_ffn, dg_attn, dg_conv, dtaps)
    rest = dict(zip(attn_comm.kinds, landed_ffn)) if attn_comm is not None else dict(ffn)
    rest["out"] = landed_out[0] if wgrad_comm is not None else g_out
    return dx0, g_in, small, landed_prev, rest


def kernel(x, positions, pre_mix_norm, w_in, conv_w, attn_out_norm, conv_out_norm, w_out, post_mix_norm, pre_ffn_norm, w_gate_up, w_down, post_ffn_norm, loss_target, m_pre_mix_norm, m_w_in, m_conv_w, m_attn_out_norm, m_conv_out_norm, m_w_out, m_post_mix_norm, m_pre_ffn_norm, m_w_gate_up, m_w_down, m_post_ffn_norm, v_pre_mix_norm, v_w_in, v_conv_w, v_attn_out_norm, v_conv_out_norm, v_w_out, v_post_mix_norm, v_pre_ffn_norm, v_w_gate_up, v_w_down, v_post_ffn_norm):
    mx, my, mc = _place()
    me = _block_of(mx, my, mc)
    conv_channels = conv_w.shape[-1]

    taps_flat = jnp.pad(conv_w.reshape(-1), (0, 8 * LANES - conv_w.size)).reshape(8, LANES)
    taps_all = _allgather_small(taps_flat, "allgather_taps").reshape(N_DEV, 8 * LANES)[:, :conv_w.size]
    conv_w_full = taps_all.reshape(N_DEV, DEPTH, 3, conv_channels).transpose(1, 2, 0, 3).reshape(DEPTH, 3, ATTN_WIDTH)

    def hidden_major(a):
        return jnp.swapaxes(a, 1, 2)

    big_w = dict(zip(WEIGHT_KINDS, (w_in, w_out, hidden_major(w_gate_up), w_down)))
    big_m = dict(zip(WEIGHT_KINDS, (m_w_in, m_w_out, hidden_major(m_w_gate_up), m_w_down)))
    big_v = dict(zip(WEIGHT_KINDS, (v_w_in, v_w_out, hidden_major(v_w_gate_up), v_w_down)))
    shards = {kind: big_w[kind].astype(BF16) for kind in WEIGHT_KINDS}
    all_gains = dict(pre_mix_norm=pre_mix_norm, attn_out_norm=attn_out_norm, conv_out_norm=conv_out_norm, post_mix_norm=post_mix_norm,
                     pre_ffn_norm=pre_ffn_norm, post_ffn_norm=post_ffn_norm)

    def gather(kinds, l):
        return _Gather(kinds, [shards[kind][l] for kind in kinds])

    def gains(l):
        return {name: g[l:l + 1] for name, g in all_gains.items()}

    tables = _rope(positions)

    early = ("in", "out", "dn")
    weights = [dict(zip(early[:1], _comm_only(gather(early[:1], 0), "allgather_first")))] + [None] * (DEPTH - 1)
    saved = [None] * DEPTH
    h = x[0]
    for l in range(DEPTH):
        inproj_comm = gather(early[1:], 0) if l == 0 else None
        ffn_comm = gather(early, l + 1) if l + 1 < DEPTH else None
        h, saved[l], weights[l], landed = _layer_forward(h, gains(l), conv_w_full[l], tables, weights[l], inproj_comm,
                                                         gather(("gu",), l), ffn_comm)
        if ffn_comm is not None:
            weights[l + 1] = dict(zip(ffn_comm.kinds, landed))

    loss, dx = _loss_and_grad(h, loss_target[0])
    loss = lax.psum(loss[0, 0], ("x", "y", "c"))

    parts = {kind: [None] * DEPTH for kind in WEIGHT_KINDS}
    small_grads = [None] * DEPTH
    g_in_above = None
    exchange = _GradientExchange()
    for l in reversed(range(DEPTH)):
        ffn_comm = _Scatter(("in",), (g_in_above,)) if g_in_above is not None else None
        dx, g_in_above, small_grads[l], landed, rest = _layer_backward(dx, saved[l], gains(l), conv_w_full[l], tables, weights[l], ffn_comm,
                                                                      exchange)
        if ffn_comm is not None:
            parts["in"][l + 1] = landed[0]
        for kind, part in rest.items():
            parts[kind][l] = part

    parts["in"][0] = _comm_only(_Scatter(("in",), (g_in_above,)), "scatter_last")[0]
    tiles = {"in": 256, "out": 128, "gu": 176, "dn": 176}
    big = {kind: _adamw_layers(parts[kind], big_w[kind], big_m[kind], big_v[kind], tiles[kind], "adamw_" + kind)[0]
           for kind in WEIGHT_KINDS}

    packed = jnp.concatenate(small_grads, axis=0)
    gathered = _allgather_small(packed, "allgather_small_grads")

    def pack_state(state):
        rows = [_pack_small(*[state[name][l:l + 1] for name in GAIN_NAMES], jnp.zeros((3, ATTN_WIDTH), F32)) for l in range(DEPTH)]
        return jnp.concatenate(rows, axis=0)

    sw = pack_state(all_gains)
    sm = pack_state(dict(zip(GAIN_NAMES, (m_pre_mix_norm, m_post_mix_norm, m_pre_ffn_norm, m_post_ffn_norm, m_attn_out_norm, m_conv_out_norm))))
    sv = pack_state(dict(zip(GAIN_NAMES, (v_pre_mix_norm, v_post_mix_norm, v_pre_ffn_norm, v_post_ffn_norm, v_attn_out_norm, v_conv_out_norm))))
    sg, sd, snm, snv = _adamw_sum8(gathered, sw, sm, sv, "adamw_small")

    def unpack(p):
        p = p.reshape(DEPTH, SMALL_ROWS, D_MODEL)
        return dict(pre_mix_norm=p[:, 0], post_mix_norm=p[:, 1], pre_ffn_norm=p[:, 2], post_ffn_norm=p[:, 3],
                    attn_out_norm=p[:, 4, :ATTN_WIDTH], conv_out_norm=p[:, 4, ATTN_WIDTH:])

    small = [unpack(p) for p in (sg, sd, snm, snv)]
    sg3 = sg.reshape(DEPTH, SMALL_ROWS, D_MODEL)
    taps_grad_full = jnp.stack([sg3[:, 5, :ATTN_WIDTH], sg3[:, 5, ATTN_WIDTH:], sg3[:, 6, :ATTN_WIDTH]], axis=1)
    taps_grad = lax.dynamic_slice_in_dim(taps_grad_full, me * conv_channels, conv_channels, axis=2)

    def flat(a):
        return a.reshape(DEPTH * 3, conv_channels)

    td, tnm, tnv = _adamw_plain(flat(taps_grad), flat(conv_w), flat(m_conv_w), flat(v_conv_w), "adamw_taps")
    taps = [taps_grad] + [a.reshape(conv_w.shape) for a in (td, tnm, tnv)]

    def leaves(o):
        s = small[o]
        return (s["pre_mix_norm"], big["in"][o], taps[o], s["attn_out_norm"], s["conv_out_norm"], big["out"][o], s["post_mix_norm"],
                s["pre_ffn_norm"], hidden_major(big["gu"][o]), big["dn"][o], s["post_ffn_norm"])

    return (loss, dx[None], *leaves(0), *leaves(1), *leaves(2), *leaves(3))
```

```python
import math

import jax
import jax.numpy as jnp
from jax import lax
from jax.experimental import pallas as pl
from jax.experimental.pallas import tpu as pltpu

F32 = jnp.float32
BF16 = jnp.bfloat16
MESH = pl.DeviceIdType.MESH

SEQ = 4096
D_MODEL = 1024
DEPTH = 4
N_DEV = 8
ATTN_WIDTH = 512
IN_PROJ_WIDTH = 3072
FFN_HIDDEN = 2816
FFN_BLOCK = 2 * FFN_HIDDEN // N_DEV
W_IN_BLOCK = IN_PROJ_WIDTH // N_DEV
W_OUT_BLOCK = D_MODEL // N_DEV
W_DOWN_BLOCK = FFN_HIDDEN // N_DEV
HEAD_DIM = 64
ROPE_DIM = 16
ROPE_THETA = 500000.0
DILATIONS = (1, 4, 16)
HALF_WINDOW = 64
RMS_EPS = 1e-6
NEG_INF = -1e30
LANES = 128
Q_BLOCK = 128
K_WINDOW = Q_BLOCK + 2 * HALF_WINDOW
PERM_CHUNK = 256
ROW_TILE = 512
FFN_TILE = 256
WGRAD_TILE = SEQ
ADAM_LR, ADAM_B1, ADAM_B2, ADAM_EPS, ADAM_WD, ADAM_STEP = 0.001, 0.9, 0.999, 1e-08, 0.01, 10
MIB = 1024 * 1024
PINNED_BYTES = 256 * 1024

WEIGHT_KINDS = ("in", "out", "gu", "dn")
FULL_SHAPES = {"in": (D_MODEL, IN_PROJ_WIDTH), "out": (D_MODEL, D_MODEL), "gu": (N_DEV, FFN_BLOCK, D_MODEL), "dn": (FFN_HIDDEN, D_MODEL)}
SHARD_SHAPES = {"in": (D_MODEL, W_IN_BLOCK), "out": (W_OUT_BLOCK, D_MODEL), "gu": (FFN_BLOCK, D_MODEL), "dn": (W_DOWN_BLOCK, D_MODEL)}
ANY = pl.BlockSpec(memory_space=pl.ANY)


def _sds(shape, dtype):
    return jax.ShapeDtypeStruct(shape, dtype)


def _rows(width, tile=ROW_TILE):
    return pl.BlockSpec((tile, width), lambda i: (i, 0))


def _frows(width):
    return _rows(width, FFN_TILE)


def _cols(height, tile):
    return pl.BlockSpec((height, tile), lambda i: (0, i))


def _const(shape):
    return pl.BlockSpec(shape, lambda i: (0,) * len(shape))


def _const1(shape):
    return pl.BlockSpec(shape, lambda i: (0,) * len(shape), pipeline_mode=pl.Buffered(1))


HALO_ROWS = 16


def _halo_prev(width, tile=ROW_TILE):
    return pl.BlockSpec((HALO_ROWS, width), lambda i: (jnp.maximum(i * (tile // HALO_ROWS) - 1, 0), 0))


def _halo_next(width, tile=ROW_TILE):
    return pl.BlockSpec((HALO_ROWS, width), lambda i: (jnp.minimum((i + 1) * (tile // HALO_ROWS), SEQ // HALO_ROWS - 1), 0))


def _rms(x):
    r = lax.rsqrt(jnp.mean(x * x, axis=-1, keepdims=True) + RMS_EPS)
    return x * r, r


def _rms_bwd(dn, y, r):
    return r * (dn - y * jnp.mean(dn * y, axis=-1, keepdims=True))


def _dot(a, b):
    return jnp.dot(a, b, preferred_element_type=F32)


def _dot_nt(a, b):
    return lax.dot_general(a, b, (((1,), (1,)), ((), ())), preferred_element_type=F32)


def _dot_tn(a, b):
    return lax.dot_general(a, b, (((0,), (0,)), ((), ())), preferred_element_type=F32)


def _place():
    return lax.axis_index("x"), lax.axis_index("y"), lax.axis_index("c")


def _block_of(px, py, pc):
    return 4 * px + 2 * py + pc


def _weight_block(ref, kind, blk):
    if kind == "in":
        return ref.at[:, pl.ds(blk * W_IN_BLOCK, W_IN_BLOCK)]
    if kind == "out":
        return ref.at[pl.ds(blk * W_OUT_BLOCK, W_OUT_BLOCK), :]
    if kind == "gu":
        return ref.at[blk]
    return ref.at[pl.ds(blk * W_DOWN_BLOCK, W_DOWN_BLOCK), :]


def _dma_semaphores(n):
    return [pltpu.SemaphoreType.DMA((n, 7)), pltpu.SemaphoreType.DMA((n, 7)), pltpu.SemaphoreType.DMA((n,))]


class _Gather:
    def __init__(self, kinds, shards):
        self.kinds, self.operands = tuple(kinds), list(shards)
        self.tag = "gather_" + "_".join(kinds)
        self.out_shape = [_sds(FULL_SHAPES[k], BF16) for k in kinds]
        self.scratch = _dma_semaphores(len(kinds))

    def _parties(self):
        x, y, c = _place()
        return (x, y, c), (x, y, 1 - c), [(1 - x, y), (x, 1 - y), (1 - x, 1 - y)], c

    def _copy(self, outs, sems, w, k, block, to, src=None):
        dst = _weight_block(outs[w], self.kinds[w], _block_of(*block))
        return pltpu.make_async_remote_copy(src_ref=dst if src is None else src, dst_ref=dst, send_sem=sems[0].at[w, k],
                                            recv_sem=sems[1].at[w, k], device_id=to, device_id_type=MESH)

    def _own(self, srcs, outs, sems, w, me):
        return pltpu.make_async_copy(srcs[w], _weight_block(outs[w], self.kinds[w], _block_of(*me)), sems[2].at[w])

    def _first(self, srcs, outs, sems, w):
        me, sibling, chips, c = self._parties()
        return [self._copy(outs, sems, w, 0, me, sibling, src=srcs[w])] + [
            self._copy(outs, sems, w, 1 + j, me, (*chip, c), src=srcs[w]) for j, chip in enumerate(chips)]

    def start(self, srcs, outs, sems):
        me = self._parties()[0]
        for w in range(len(self.kinds)):
            self._own(srcs, outs, sems, w, me).start()
            for cp in self._first(srcs, outs, sems, w):
                cp.start()

    def forward(self, srcs, outs, sems):
        me, sibling, chips, c = self._parties()
        for j, chip in enumerate(chips):
            for w in range(len(self.kinds)):
                self._copy(outs, sems, w, 1 + j, (*chip, c), me).wait_recv()
                self._copy(outs, sems, w, 4 + j, (*chip, c), sibling).start()

    def finish(self, srcs, outs, sems):
        me, sibling, chips, c = self._parties()
        for w in range(len(self.kinds)):
            self._copy(outs, sems, w, 0, sibling, me).wait_recv()
            for j, chip in enumerate(chips):
                self._copy(outs, sems, w, 4 + j, (*chip, 1 - c), me).wait_recv()
        for w in range(len(self.kinds)):
            for cp in self._first(srcs, outs, sems, w):
                cp.wait_send()
            for j, chip in enumerate(chips):
                self._copy(outs, sems, w, 4 + j, (*chip, c), sibling).wait_send()
            self._own(srcs, outs, sems, w, me).wait()


def _peers(x, y, c):
    return [(x ^ a, y ^ b, c ^ e) for a in (0, 1) for b in (0, 1) for e in (0, 1) if (a, b, e) != (0, 0, 0)]


N_CHIPS = N_DEV // 2


class _ScatterChips:
    def __init__(self, kinds, blocks):
        self.kinds, self.operands = tuple(kinds), list(blocks)
        self.tag = "scatter_chips_" + "_".join(kinds)
        self.out_shape = [_sds((N_CHIPS,) + SHARD_SHAPES[k], BF16) for k in kinds]
        self.scratch = _dma_semaphores(len(kinds))

    def _copies(self, srcs, outs, sems):
        x, y, c = _place()
        mine = 2 * x + y
        copies = []
        for w in range(len(self.kinds)):
            copies.append(pltpu.make_async_copy(srcs[w].at[mine], outs[w].at[mine], sems[2].at[w]))
            for j, (px, py) in enumerate([(1 - x, y), (x, 1 - y), (1 - x, 1 - y)]):
                copies.append(pltpu.make_async_remote_copy(
                    src_ref=srcs[w].at[2 * px + py], dst_ref=outs[w].at[mine], send_sem=sems[0].at[w, j], recv_sem=sems[1].at[w, j],
                    device_id=(px, py, c), device_id_type=MESH))
        return copies

    def start(self, srcs, outs, sems):
        for cp in self._copies(srcs, outs, sems):
            cp.start()

    def forward(self, srcs, outs, sems):
        pass

    def finish(self, srcs, outs, sems):
        for cp in self._copies(srcs, outs, sems):
            cp.wait()


def _in_hbm(a):
    return pltpu.with_memory_space_constraint(a, pltpu.HBM) if a.size * a.dtype.itemsize >= PINNED_BYTES else a


def _out_hbm(s):
    return pltpu.HBM(s.shape, s.dtype) if math.prod(s.shape) * jnp.dtype(s.dtype).itemsize >= PINNED_BYTES else s


def _call(body, args, *, grid, in_specs, out_specs, out_shape, scratch_shapes=(), vmem_mib=None, name, comm=None):
    kwargs = {} if vmem_mib is None else dict(compiler_params=pltpu.CompilerParams(vmem_limit_bytes=vmem_mib * MIB))
    in_specs, out_specs, out_shape, scratch_shapes = list(in_specs), list(out_specs), list(out_shape), list(scratch_shapes)
    args = [_in_hbm(a) for a in args]
    out_shape = [_out_hbm(s) for s in out_shape]
    if comm is None:
        res = pl.pallas_call(body, grid=grid, in_specs=in_specs, out_specs=out_specs, out_shape=out_shape,
                             scratch_shapes=scratch_shapes, name=name, **kwargs)(*args)
        return list(res), None
    n_in, n_out, n_scr = len(in_specs), len(out_specs), len(scratch_shapes)
    c_in, c_out = len(comm.operands), len(comm.out_shape)
    last = math.prod(grid) - 1

    def carried(*refs):
        cuts = [n_in, c_in, n_out, c_out, n_scr]
        parts, at = [], 0
        for n in cuts:
            parts.append(refs[at:at + n])
            at += n
        ins, c_ins, outs, c_outs, scr = parts
        sems = refs[at:]
        step = pl.program_id(0)
        for axis in range(1, len(grid)):
            step = step * grid[axis] + pl.program_id(axis)

        @pl.when(step == 0)
        def _():
            comm.start(c_ins, c_outs, sems)

        @pl.when(step == last)
        def _():
            comm.forward(c_ins, c_outs, sems)

        body(*ins, *outs, *scr)

        @pl.when(step == last)
        def _():
            comm.finish(c_ins, c_outs, sems)

    res = pl.pallas_call(carried, grid=grid, in_specs=in_specs + [ANY] * c_in, out_specs=out_specs + [ANY] * c_out,
                         out_shape=out_shape + [_out_hbm(s) for s in comm.out_shape], scratch_shapes=scratch_shapes + comm.scratch,
                         name=name + "_" + comm.tag, **kwargs)(*args, *[_in_hbm(a) for a in comm.operands])
    return list(res[:n_out]), list(res[n_out:])


def _comm_only(comm, name):
    def body(*refs):
        n_in, n_out = len(comm.operands), len(comm.out_shape)
        srcs, outs, sems = refs[:n_in], refs[n_in:n_in + n_out], refs[n_in + n_out:]
        comm.start(srcs, outs, sems)
        comm.forward(srcs, outs, sems)
        comm.finish(srcs, outs, sems)

    return pl.pallas_call(body, in_specs=[ANY] * len(comm.operands), out_specs=[ANY] * len(comm.out_shape),
                          out_shape=[_out_hbm(s) for s in comm.out_shape], scratch_shapes=comm.scratch,
                          name=name)(*[_in_hbm(a) for a in comm.operands])


def _allgather_small(v, name):
    def body(v_ref, o_ref, send, recv, local):
        x, y, c = _place()
        me = _block_of(x, y, c)
        mine = pltpu.make_async_copy(v_ref, o_ref.at[me], local)
        mine.start()
        copies = [mine]
        for k, peer in enumerate(_peers(x, y, c)):
            cp = pltpu.make_async_remote_copy(src_ref=v_ref, dst_ref=o_ref.at[me], send_sem=send.at[k], recv_sem=recv.at[k],
                                              device_id=peer, device_id_type=MESH)
            cp.start()
            copies.append(cp)
        for cp in copies:
            cp.wait()

    vm = pl.BlockSpec(memory_space=pltpu.VMEM)
    return pl.pallas_call(
        body, in_specs=[vm], out_specs=vm, out_shape=_sds((N_DEV,) + v.shape, F32),
        scratch_shapes=[pltpu.SemaphoreType.DMA((7,)), pltpu.SemaphoreType.DMA((7,)), pltpu.SemaphoreType.DMA],
        name=name)(v)


def _rope_tables(pos_col, freq_row):
    def body(p_ref, f_ref, c_ref, sa_ref, sb_ref):
        ang = p_ref[...].astype(F32) * f_ref[...]
        lane = lax.broadcasted_iota(jnp.int32, ang.shape, 1) % HEAD_DIM
        cos, sin = jnp.cos(ang), jnp.sin(ang)
        c_ref[...] = jnp.where(lane < ROPE_DIM, cos, 1.0)
        sa_ref[...] = jnp.where(lane < ROPE_DIM // 2, -sin, 0.0)
        sb_ref[...] = jnp.where((lane >= ROPE_DIM // 2) & (lane < ROPE_DIM), sin, 0.0)

    return pl.pallas_call(
        body, grid=(SEQ // ROW_TILE,), in_specs=[_rows(1), _const((1, LANES))], out_specs=[_rows(LANES)] * 3,
        out_shape=[_sds((SEQ, LANES), F32)] * 3, name="rope_tables")(pos_col, freq_row)


def _rotate(t, c, sa, sb):
    parts = []
    for g in range(ATTN_WIDTH // LANES):
        tg = t[:, g * LANES:(g + 1) * LANES]
        parts.append(tg * c + pltpu.roll(tg, LANES - 8, axis=1) * sa + pltpu.roll(tg, 8, axis=1) * sb)
    return jnp.concatenate(parts, axis=1)


def _rotate_transposed(dt, c, sa, sb):
    parts = []
    for g in range(ATTN_WIDTH // LANES):
        dg = dt[:, g * LANES:(g + 1) * LANES]
        parts.append(dg * c + pltpu.roll(dg * sa, 8, axis=1) + pltpu.roll(dg * sb, LANES - 8, axis=1))
    return jnp.concatenate(parts, axis=1)


def _inproj_fwd(x, g_pre, w_in, tc, tsa, tsb, comm=None):
    def body(x_ref, g_ref, w_ref, c_ref, sa_ref, sb_ref, q_ref, k_ref, v_ref, u_ref, b_ref, cc_ref):
        y, _ = _rms(x_ref[...])
        h = (y * g_ref[...]).astype(BF16)

        def proj(n):
            return _dot(h, w_ref[:, n * ATTN_WIDTH:(n + 1) * ATTN_WIDTH])

        c, sa, sb = c_ref[...], sa_ref[...], sb_ref[...]
        q_ref[...] = (_rotate(proj(0), c, sa, sb) * (HEAD_DIM ** -0.5)).astype(BF16)
        k_ref[...] = _rotate(proj(1), c, sa, sb).astype(BF16)
        v_ref[...] = proj(2).astype(BF16)
        u_ref[...] = proj(3).astype(BF16)
        b_ref[...] = proj(4).astype(BF16)
        cc_ref[...] = proj(5).astype(BF16)

    a = ATTN_WIDTH
    return _call(
        body, (x, g_pre, w_in, tc, tsa, tsb), grid=(SEQ // ROW_TILE,),
        in_specs=[_rows(D_MODEL), _const((1, D_MODEL)), _const1((D_MODEL, IN_PROJ_WIDTH)), _rows(LANES), _rows(LANES), _rows(LANES)],
        out_specs=[_rows(a)] * 6, out_shape=[_sds((SEQ, a), BF16)] * 6,
        vmem_mib=40, name="inproj_fwd", comm=comm)


def _head_masks():
    lane = lax.broadcasted_iota(jnp.int32, (1, LANES), 1)
    first = lane < HEAD_DIM
    return first, first.astype(F32), 1.0 - first.astype(F32)


def _perm_chunks(dil):
    length = SEQ // dil
    out = []
    for r in range(dil):
        for c0 in range(0, length, PERM_CHUNK):
            chunk = (r * length + c0) // PERM_CHUNK
            rows = pl.ds(c0, PERM_CHUNK) if dil == 1 else pl.ds(r + dil * c0, PERM_CHUNK, stride=dil)
            out.append((chunk, rows))
    return out


def _chunk(c, offset=0):
    return pl.ds(offset + c * PERM_CHUNK, PERM_CHUNK)


def _write_band_bias(bias_ref):
    rr = lax.broadcasted_iota(jnp.int32, (Q_BLOCK, K_WINDOW), 0)
    cc = lax.broadcasted_iota(jnp.int32, (Q_BLOCK, K_WINDOW), 1)
    band = (cc >= rr) & (cc - rr <= 2 * HALF_WINDOW)
    bias_ref[0] = jnp.where(band, 0.0, NEG_INF)
    bias_ref[1] = jnp.where(band & (cc >= HALF_WINDOW), 0.0, NEG_INF)
    bias_ref[2] = jnp.where(band & (cc < Q_BLOCK + HALF_WINDOW), 0.0, NEG_INF)


def _band_bias_index(m0, length):
    return jnp.where(m0 % length == 0, 1, 0) + jnp.where((m0 + Q_BLOCK) % length == 0, 2, 0)


def _zero_key_padding(bufs):
    pad = jnp.zeros((HALF_WINDOW, LANES), BF16)
    for buf in bufs:
        buf[pl.ds(0, HALF_WINDOW), :] = pad
        buf[pl.ds(SEQ + HALF_WINDOW, HALF_WINDOW), :] = pad


def _attn_fwd(q, k, v, comm=None):
    group = 8

    def body(q_ref, k_ref, v_ref, o_ref, lse_ref, q32, k32, v32, qa, qb, kp, vp, accp, mlp,
             acc0, acc1, acc2, ml0, ml1, ml2, bias_ref):
        first, mask_a, mask_b = _head_masks()
        low = lax.broadcasted_iota(jnp.int32, (1, LANES), 1) % HEAD_DIM < HEAD_DIM // 2
        _zero_key_padding((kp, vp))
        _write_band_bias(bias_ref)
        q32[...] = q_ref[...].astype(F32)
        k32[...] = k_ref[...].astype(F32)
        v32[...] = v_ref[...].astype(F32)
        natural = ((acc0, ml0), (acc1, ml1), (acc2, ml2))

        for branch, dil in enumerate(DILATIONS):
            length = SEQ // dil
            assert length >= 2 * Q_BLOCK
            chunks = _perm_chunks(dil)
            for c, rows in chunks:
                val = q32[rows, :]
                qa[_chunk(c), :] = (val * mask_a).astype(BF16)
                qb[_chunk(c), :] = (val * mask_b).astype(BF16)
                kp[_chunk(c, HALF_WINDOW), :] = k32[rows, :].astype(BF16)
                vp[_chunk(c, HALF_WINDOW), :] = v32[rows, :].astype(BF16)
            acc_dst, ml_dst = natural[branch] if dil == 1 else (accp, mlp)

            def blocks(i, carry, length=length, acc_dst=acc_dst, ml_dst=ml_dst):
                base = pl.multiple_of(i * (group * Q_BLOCK), group * Q_BLOCK)
                starts = [base + g * Q_BLOCK for g in range(group)]
                scores = [[_dot_nt(qx[pl.ds(m0, Q_BLOCK), :], kp[pl.ds(m0, K_WINDOW), :]) for qx in (qa, qb)] for m0 in starts]
                probs = []
                for m0, pair in zip(starts, scores):
                    bias = bias_ref[_band_bias_index(m0, length)]
                    stats, ps = [], []
                    for s in pair:
                        s = s + bias
                        m = jnp.max(s, axis=1, keepdims=True)
                        p = jnp.exp(s - m)
                        stats.append(jnp.where(low, m, jnp.sum(p, axis=1, keepdims=True)))
                        ps.append(p.astype(BF16))
                    ml_dst[pl.ds(m0, Q_BLOCK), :] = jnp.where(first, stats[0], stats[1])
                    probs.append(ps)
                for m0, ps in zip(starts, probs):
                    vw = vp[pl.ds(m0, K_WINDOW), :]
                    acc_dst[pl.ds(m0, Q_BLOCK), :] = jnp.where(first, _dot(ps[0], vw), _dot(ps[1], vw))
                return carry

            lax.fori_loop(0, SEQ // (group * Q_BLOCK), blocks, 0)

            if dil > 1:
                for c, rows in chunks:
                    natural[branch][0][rows, :] = accp[_chunk(c), :]
                    natural[branch][1][rows, :] = mlp[_chunk(c), :]

        for c in range(SEQ // PERM_CHUNK):
            packed = [ml[_chunk(c), :] for _, ml in natural]
            ms = [jnp.where(low, ml, pltpu.roll(ml, HEAD_DIM // 2, axis=1)) for ml in packed]
            ls = [jnp.where(low, pltpu.roll(ml, LANES - HEAD_DIM // 2, axis=1), ml) for ml in packed]
            m_all = jnp.maximum(jnp.maximum(ms[0], ms[1]), ms[2])
            es = [jnp.exp(m - m_all) for m in ms]
            z = ls[0] * es[0] + ls[1] * es[1] + ls[2] * es[2]
            num = natural[0][0][_chunk(c), :] * es[0] + natural[1][0][_chunk(c), :] * es[1] + natural[2][0][_chunk(c), :] * es[2]
            o_ref[_chunk(c), :] = num / z
            lse_ref[_chunk(c), :] = m_all + jnp.log(z)

    col = pl.BlockSpec((SEQ, LANES), lambda h: (0, h))
    padded = SEQ + 2 * HALF_WINDOW
    return _call(
        body, (q, k, v), grid=(ATTN_WIDTH // LANES,), in_specs=[col] * 3, out_specs=[col] * 2,
        out_shape=[_sds((SEQ, ATTN_WIDTH), F32)] * 2,
        scratch_shapes=[pltpu.VMEM((SEQ, LANES), F32)] * 3 + [pltpu.VMEM((SEQ, LANES), BF16)] * 2
        + [pltpu.VMEM((padded, LANES), BF16)] * 2 + [pltpu.VMEM((SEQ, LANES), F32)] * 8
        + [pltpu.VMEM((3, Q_BLOCK, K_WINDOW), F32)],
        vmem_mib=52, name="attn_fwd", comm=comm)


def _shifted(t, before, after, i):
    tile = t.shape[0]
    row = lax.broadcasted_iota(jnp.int32, (tile, 1), 0)
    before = jnp.where(i > 0, before, 0.0)
    after = jnp.where(i < SEQ // tile - 1, after, 0.0)
    return (jnp.where(row == 0, before, pltpu.roll(t, 1, axis=0)),
            jnp.where(row == tile - 1, after, pltpu.roll(t, tile - 1, axis=0)))


def _last_row(ref):
    return ref[HALO_ROWS - 1:HALO_ROWS, :].astype(F32)


def _first_row(ref):
    return ref[0:1, :].astype(F32)


def _conv_parts(u, c, u_prev, c_prev, u_next, c_next, cw, i):
    t = c * u
    t_prev, t_next = _shifted(t, _last_row(c_prev) * _last_row(u_prev), _first_row(c_next) * _first_row(u_next), i)
    s = cw[0:1, :] * t_prev + cw[1:2, :] * t + cw[2:3, :] * t_next
    return t, t_prev, t_next, s


def _mix_fwd(attn, u, b, c, conv_w, g_attn, g_conv, w_out, x, g_post):
    def body(a_ref, u_ref, b_ref, c_ref, up_ref, cp_ref, un_ref, cn_ref, cw_ref, ga_ref, gc_ref, w_ref, x_ref, gp_ref,
             x1_ref, mg_ref, mix_ref):
        i = pl.program_id(0)
        _, _, _, s = _conv_parts(u_ref[...].astype(F32), c_ref[...].astype(F32), up_ref, cp_ref, un_ref, cn_ref, cw_ref[...], i)
        ya, _ = _rms(a_ref[...])
        yc, _ = _rms(b_ref[...].astype(F32) * s)
        merged = jnp.concatenate([ya * ga_ref[...], yc * gc_ref[...]], axis=1).astype(BF16)
        mix = _dot(merged, w_ref[...])
        ym, _ = _rms(mix)
        mg_ref[...] = merged.T
        mix_ref[...] = mix.astype(BF16)
        x1_ref[...] = x_ref[...] + ym * gp_ref[...]

    a = ATTN_WIDTH
    return _call(
        body, (attn, u, b, c, u, c, u, c, conv_w, g_attn, g_conv, w_out, x, g_post), grid=(SEQ // ROW_TILE,),
        in_specs=[_rows(a)] * 4 + [_halo_prev(a)] * 2 + [_halo_next(a)] * 2
        + [_const((3, a)), _const((1, a)), _const((1, a)), _const1((D_MODEL, D_MODEL)), _rows(D_MODEL), _const((1, D_MODEL))],
        out_specs=[_rows(D_MODEL), _cols(D_MODEL, ROW_TILE), _rows(D_MODEL)],
        out_shape=[_sds((SEQ, D_MODEL), F32), _sds((D_MODEL, SEQ), BF16), _sds((SEQ, D_MODEL), BF16)],
        vmem_mib=40, name="mix_fwd")[0]


def _gu_spec():
    return pl.BlockSpec((N_DEV, FFN_TILE, FFN_BLOCK), lambda i: (0, i, 0))


def _ffn_fwd(x1, g_pre, w_gu, w_dn, g_post, comm=None):
    def body(x_ref, g_ref, wgu_ref, wdn_ref, gp_ref, x2_ref, gu_ref, f_ref):
        x1 = x_ref[...]
        y, _ = _rms(x1)
        h = (y * g_ref[...]).astype(BF16)
        f = jnp.zeros((ROW_TILE, D_MODEL), F32)
        for j in range(N_DEV // 2):
            gate = _dot_nt(h, wgu_ref[j])
            up = _dot_nt(h, wgu_ref[j + N_DEV // 2])
            gu_ref[j] = gate.astype(BF16)
            gu_ref[j + N_DEV // 2] = up.astype(BF16)
            act = (gate * jax.nn.sigmoid(gate) * up).astype(BF16)
            f = f + _dot(act, wdn_ref[pl.ds(j * FFN_BLOCK, FFN_BLOCK), :])
        yf, _ = _rms(f)
        f_ref[...] = f
        x2_ref[...] = x1 + yf * gp_ref[...]

    return _call(
        body, (x1, g_pre, w_gu, w_dn, g_post), grid=(SEQ // ROW_TILE,),
        in_specs=[_rows(D_MODEL), _const((1, D_MODEL)), _const1((N_DEV, FFN_BLOCK, D_MODEL)), _const1((FFN_HIDDEN, D_MODEL)),
                  _const((1, D_MODEL))],
        out_specs=[_rows(D_MODEL), pl.BlockSpec((N_DEV, ROW_TILE, FFN_BLOCK), lambda i: (0, i, 0)), _rows(D_MODEL)],
        out_shape=[_sds((SEQ, D_MODEL), F32), _sds((N_DEV, SEQ, FFN_BLOCK), BF16), _sds((SEQ, D_MODEL), F32)],
        vmem_mib=58, name="ffn_fwd", comm=comm)


def _loss_and_grad(y, target):
    n_tiles = SEQ // ROW_TILE

    def body(y_ref, t_ref, loss_ref, dy_ref, acc):
        i = pl.program_id(0)
        err = y_ref[...] - t_ref[...]
        dy_ref[...] = err * (1.0 / D_MODEL)

        @pl.when(i == 0)
        def _():
            acc[...] = jnp.zeros_like(acc)

        acc[...] += jnp.sum(err * err, axis=0, keepdims=True)

        @pl.when(i == n_tiles - 1)
        def _():
            loss_ref[...] = jnp.sum(acc[...], axis=1, keepdims=True) * (0.5 / D_MODEL)

    return _call(
        body, (y, target), grid=(n_tiles,), in_specs=[_rows(D_MODEL)] * 2, out_specs=[_const((1, 1)), _rows(D_MODEL)],
        out_shape=[_sds((1, 1), F32), _sds((SEQ, D_MODEL), F32)], scratch_shapes=[pltpu.VMEM((1, D_MODEL), F32)],
        name="loss_and_grad")[0]


def _accumulate(ref, value, i):
    @pl.when(i == 0)
    def _():
        ref[...] = value

    @pl.when(i > 0)
    def _():
        ref[...] += value


def _colsum(v):
    return jnp.sum(v, axis=0, keepdims=True)


def _ffn_bwd(dx2, f, x1, gu, w_gu, w_dn, g_post, g_pre, comm=None):
    half = N_DEV // 2

    def body(dx2_ref, f_ref, x1_ref, gu_ref, wgu_ref, wdn_ref, gpost_ref, gpre_ref,
             dx1_ref, df_ref, act_ref, dgu_ref, h_ref, dgpost_ref, dgpre_ref):
        i = pl.program_id(0)
        dx2 = dx2_ref[...]
        yf, rf = _rms(f_ref[...])
        _accumulate(dgpost_ref, _colsum(dx2 * yf), i)
        df = _rms_bwd(dx2 * gpost_ref[...], yf, rf).astype(BF16)
        df_ref[...] = df
        dh = jnp.zeros((FFN_TILE, D_MODEL), F32)
        for j in range(half):
            dact = _dot_nt(df, wdn_ref[pl.ds(j * FFN_BLOCK, FFN_BLOCK), :])
            gate = gu_ref[j].astype(F32)
            up = gu_ref[j + half].astype(F32)
            sig = jax.nn.sigmoid(gate)
            silu = gate * sig
            act_ref[j] = (silu * up).astype(BF16)
            dgate = (dact * up * (sig * (1.0 + gate * (1.0 - sig)))).astype(BF16)
            dup = (dact * silu).astype(BF16)
            dgu_ref[j] = dgate
            dgu_ref[j + half] = dup
            dh = dh + _dot(dgate, wgu_ref[j]) + _dot(dup, wgu_ref[j + half])
        y1, r1 = _rms(x1_ref[...])
        h_ref[...] = (y1 * gpre_ref[...]).astype(BF16)
        _accumulate(dgpre_ref, _colsum(dh * y1), i)
        dx1_ref[...] = dx2 + _rms_bwd(dh * gpre_ref[...], y1, r1)

    act_spec = pl.BlockSpec((half, FFN_TILE, FFN_BLOCK), lambda i: (0, i, 0))
    return _call(
        body, (dx2, f, x1, gu, w_gu, w_dn, g_post, g_pre), grid=(SEQ // FFN_TILE,),
        in_specs=[_frows(D_MODEL)] * 3 + [_gu_spec(), _const1((N_DEV, FFN_BLOCK, D_MODEL)), _const1((FFN_HIDDEN, D_MODEL)),
                                          _const((1, D_MODEL)), _const((1, D_MODEL))],
        out_specs=[_frows(D_MODEL), _frows(D_MODEL), act_spec, _gu_spec(), _frows(D_MODEL), _const((1, D_MODEL)), _const((1, D_MODEL))],
        out_shape=[_sds((SEQ, D_MODEL), F32), _sds((SEQ, D_MODEL), BF16), _sds((half, SEQ, FFN_BLOCK), BF16),
                   _sds((N_DEV, SEQ, FFN_BLOCK), BF16), _sds((SEQ, D_MODEL), BF16), _sds((1, D_MODEL), F32), _sds((1, D_MODEL), F32)],
        vmem_mib=52, name="ffn_bwd", comm=comm)


def _wgrad_paired(a, g, kind, comm=None):
    shard = SHARD_SHAPES[kind]
    per_chip = 2 if kind == "gu" else 1
    n_compute = per_chip * N_CHIPS
    n_steps = n_compute + 1

    def block(t):
        return jnp.minimum(t, n_compute - 1)

    if kind in ("gu", "dn"):
        specs = [pl.BlockSpec((1, SEQ, FFN_BLOCK), lambda t: (block(t), 0, 0)), _const1((SEQ, D_MODEL))]
    elif kind == "in":
        specs = [_const1((D_MODEL, SEQ)), pl.BlockSpec((SEQ, 2 * W_IN_BLOCK), lambda t: (0, block(t)))]
    else:
        specs = [pl.BlockSpec((2 * W_OUT_BLOCK, SEQ), lambda t: (block(t), 0)), _const1((SEQ, D_MODEL))]

    def halves(a_ref, g_ref):
        if kind == "dn":
            r = _dot_tn(a_ref[0], g_ref[...])
            return r[:W_DOWN_BLOCK], r[W_DOWN_BLOCK:]
        r = _dot(a_ref[...], g_ref[...])
        if kind == "in":
            return r[:, :W_IN_BLOCK], r[:, W_IN_BLOCK:]
        return r[:W_OUT_BLOCK], r[W_OUT_BLOCK:]

    def body(a_ref, g_ref, o_ref, sendbuf, recvbuf, keepbuf, send_sem, recv_sem):
        t = pl.program_id(0)
        x, y, c = _place()

        def exchange(q):
            return pltpu.make_async_remote_copy(src_ref=sendbuf.at[q], dst_ref=recvbuf.at[q], send_sem=send_sem.at[q],
                                                recv_sem=recv_sem.at[q], device_id=(x, y, 1 - c), device_id_type=MESH)

        @pl.when((t >= per_chip) & (t % per_chip == 0))
        def _():
            q = t // per_chip - 1
            exchange(q).wait_recv()
            o_ref[0] = (keepbuf[...] + recvbuf[q].astype(F32)).astype(BF16)

        @pl.when(t < n_compute)
        def _():
            q = t // per_chip
            if kind == "gu":
                r = _dot_tn(a_ref[0], g_ref[...])

                @pl.when(t % 2 == c)
                def _():
                    keepbuf[...] = r

                @pl.when(t % 2 != c)
                def _():
                    sendbuf[q] = r.astype(BF16)
                    exchange(q).start()
            else:
                lower, upper = halves(a_ref, g_ref)
                keepbuf[...] = jnp.where(c == 0, lower, upper)
                sendbuf[q] = jnp.where(c == 0, upper, lower).astype(BF16)
                exchange(q).start()

        @pl.when(t == n_steps - 1)
        def _():
            for q in range(N_CHIPS):
                exchange(q).wait_send()

    res, landed = _call(
        body, (a, g), grid=(n_steps,), in_specs=specs,
        out_specs=[pl.BlockSpec((1,) + shard, lambda t: (jnp.clip(t // per_chip - 1, 0, N_CHIPS - 1), 0, 0))],
        out_shape=[_sds((N_CHIPS,) + shard, BF16)],
        scratch_shapes=[pltpu.VMEM((N_CHIPS,) + shard, BF16), pltpu.VMEM((N_CHIPS,) + shard, BF16), pltpu.VMEM(shard, F32),
                        pltpu.SemaphoreType.DMA((N_CHIPS,)), pltpu.SemaphoreType.DMA((N_CHIPS,))],
        vmem_mib=52, name="wgrad_" + kind, comm=comm)
    return res[0], landed


def _mix_bwd(dx1, mix, attn, u, b, c, conv_w, g_attn, g_conv, g_post, w_out):
    def body(dx1_ref, mix_ref, a_ref, u_ref, b_ref, c_ref, up_ref, cp_ref, un_ref, cn_ref, cw_ref, ga_ref, gc_ref, gp_ref, w_ref,
             dmix_ref, da_ref, ds_ref, db_ref, dgp_ref, dga_ref, dgc_ref):
        i = pl.program_id(0)
        dx1 = dx1_ref[...]
        ym, rm = _rms(mix_ref[...].astype(F32))
        _accumulate(dgp_ref, _colsum(dx1 * ym), i)
        dmix = _rms_bwd(dx1 * gp_ref[...], ym, rm).astype(BF16)
        dmix_ref[...] = dmix
        dmerged = _dot_nt(dmix, w_ref[...])
        dna, dnc = dmerged[:, :ATTN_WIDTH], dmerged[:, ATTN_WIDTH:]
        ya, ra = _rms(a_ref[...])
        _accumulate(dga_ref, _colsum(dna * ya), i)
        da_ref[...] = _rms_bwd(dna * ga_ref[...], ya, ra)
        _, _, _, s = _conv_parts(u_ref[...].astype(F32), c_ref[...].astype(F32), up_ref, cp_ref, un_ref, cn_ref, cw_ref[...], i)
        gate_b = b_ref[...].astype(F32)
        yc, rc = _rms(gate_b * s)
        _accumulate(dgc_ref, _colsum(dnc * yc), i)
        dy = _rms_bwd(dnc * gc_ref[...], yc, rc)
        db_ref[...] = (dy * s).astype(BF16)
        ds_ref[...] = (dy * gate_b).astype(BF16)

    a = ATTN_WIDTH
    return _call(
        body, (dx1, mix, attn, u, b, c, u, c, u, c, conv_w, g_attn, g_conv, g_post, w_out), grid=(SEQ // ROW_TILE,),
        in_specs=[_rows(D_MODEL)] * 2 + [_rows(a)] * 4 + [_halo_prev(a)] * 2 + [_halo_next(a)] * 2
        + [_const((3, a)), _const((1, a)), _const((1, a)), _const((1, D_MODEL)), _const1((D_MODEL, D_MODEL))],
        out_specs=[_rows(D_MODEL)] + [_rows(a)] * 3 + [_const((1, D_MODEL)), _const((1, a)), _const((1, a))],
        out_shape=[_sds((SEQ, D_MODEL), BF16), _sds((SEQ, a), F32), _sds((SEQ, a), BF16), _sds((SEQ, a), BF16),
                   _sds((1, D_MODEL), F32), _sds((1, a), F32), _sds((1, a), F32)],
        vmem_mib=40, name="mix_bwd")[0]


def _attn_bwd(q, k, v, do, o, lse, comm=None):
    group = 8

    def body(q_ref, k_ref, v_ref, do_ref, o_ref, lse_ref, dq_ref, dk_ref, dv_ref,
             q32, k32, v32, qa, qb, doa, dob, kp, vp, lsep, dlp, dnat, dqp, dkp, dvp, bias_ref):
        first, mask_a, mask_b = _head_masks()
        _zero_key_padding((kp, vp))
        _write_band_bias(bias_ref)
        q32[...] = q_ref[...].astype(F32)
        k32[...] = k_ref[...].astype(F32)
        v32[...] = v_ref[...].astype(F32)
        for c in range(SEQ // PERM_CHUNK):
            prod = do_ref[_chunk(c), :] * o_ref[_chunk(c), :]
            d_a = jnp.sum(prod * mask_a, axis=1, keepdims=True)
            d_b = jnp.sum(prod * mask_b, axis=1, keepdims=True)
            dnat[_chunk(c), :] = jnp.where(first, d_a, d_b)

        for step, dil in enumerate(DILATIONS[1:] + DILATIONS[:1]):
            length = SEQ // dil
            assert length >= 2 * Q_BLOCK
            chunks = _perm_chunks(dil)
            for c, rows in chunks:
                val = q32[rows, :]
                qa[_chunk(c), :] = (val * mask_a).astype(BF16)
                qb[_chunk(c), :] = (val * mask_b).astype(BF16)
                val = do_ref[rows, :]
                doa[_chunk(c), :] = (val * mask_a).astype(BF16)
                dob[_chunk(c), :] = (val * mask_b).astype(BF16)
                kp[_chunk(c, HALF_WINDOW), :] = k32[rows, :].astype(BF16)
                vp[_chunk(c, HALF_WINDOW), :] = v32[rows, :].astype(BF16)
                lsep[_chunk(c), :] = lse_ref[rows, :]
                dlp[_chunk(c), :] = dnat[rows, :]
            zero = jnp.zeros((PERM_CHUNK, LANES), F32)
            for c in range(SEQ // PERM_CHUNK):
                dkp[_chunk(c), :] = zero
                dvp[_chunk(c), :] = zero
            dkp[pl.ds(SEQ, 2 * HALF_WINDOW), :] = zero[:2 * HALF_WINDOW]
            dvp[pl.ds(SEQ, 2 * HALF_WINDOW), :] = zero[:2 * HALF_WINDOW]

            heads = ((qa, doa, 0), (qb, dob, HEAD_DIM))

            def blocks(i, carry, length=length):
                base = pl.multiple_of(i * (group * Q_BLOCK), group * Q_BLOCK)
                starts = [base + g * Q_BLOCK for g in range(group)]
                raw = [[(_dot_nt(qx[pl.ds(m0, Q_BLOCK), :], kp[pl.ds(m0, K_WINDOW), :]),
                         _dot_nt(dox[pl.ds(m0, Q_BLOCK), :], vp[pl.ds(m0, K_WINDOW), :])) for qx, dox, _ in heads]
                       for m0 in starts]
                grads = []
                for m0, pair in zip(starts, raw):
                    bias = bias_ref[_band_bias_index(m0, length)]
                    lse_b, d_b = lsep[pl.ds(m0, Q_BLOCK), :], dlp[pl.ds(m0, Q_BLOCK), :]
                    out = []
                    for (s, dp), (_, _, col) in zip(pair, heads):
                        p = jnp.exp(s + bias - lse_b[:, col:col + 1])
                        out.append(((p * (dp - d_b[:, col:col + 1])).astype(BF16), p.astype(BF16)))
                    grads.append(out)
                for m0, out in zip(starts, grads):
                    qrows, krows = pl.ds(m0, Q_BLOCK), pl.ds(m0, K_WINDOW)
                    kw = kp[krows, :]
                    dk = jnp.zeros((K_WINDOW, LANES), F32)
                    dv = jnp.zeros((K_WINDOW, LANES), F32)
                    for (ds, p), (qx, dox, _) in zip(out, heads):
                        dk = dk + _dot_tn(ds, qx[qrows, :])
                        dv = dv + _dot_tn(p, dox[qrows, :])
                    dqp[qrows, :] = jnp.where(first, _dot(out[0][0], kw), _dot(out[1][0], kw)) * (HEAD_DIM ** -0.5)
                    dkp[krows, :] += dk
                    dvp[krows, :] += dv
                return carry

            lax.fori_loop(0, SEQ // (group * Q_BLOCK), blocks, 0)

            for c, rows in chunks:
                g_q, g_k, g_v = dqp[_chunk(c), :], dkp[_chunk(c, HALF_WINDOW), :], dvp[_chunk(c, HALF_WINDOW), :]
                if step == 0:
                    dq_ref[rows, :] = g_q
                    dk_ref[rows, :] = g_k
                    dv_ref[rows, :] = g_v
                else:
                    dq_ref[rows, :] = dq_ref[rows, :] + g_q
                    dk_ref[rows, :] = dk_ref[rows, :] + g_k
                    dv_ref[rows, :] = dv_ref[rows, :] + g_v

    col = pl.BlockSpec((SEQ, LANES), lambda h: (0, h))
    col1 = pl.BlockSpec((SEQ, LANES), lambda h: (0, h), pipeline_mode=pl.Buffered(1))
    padded = SEQ + 2 * HALF_WINDOW
    return _call(
        body, (q, k, v, do, o, lse), grid=(ATTN_WIDTH // LANES,), in_specs=[col1] * 6, out_specs=[col] * 3,
        out_shape=[_sds((SEQ, ATTN_WIDTH), F32)] * 3,
        scratch_shapes=[pltpu.VMEM((SEQ, LANES), F32)] * 3 + [pltpu.VMEM((SEQ, LANES), BF16)] * 4
        + [pltpu.VMEM((padded, LANES), BF16)] * 2 + [pltpu.VMEM((SEQ, LANES), F32)] * 4 + [pltpu.VMEM((padded, LANES), F32)] * 2
        + [pltpu.VMEM((3, Q_BLOCK, K_WINDOW), F32)],
        vmem_mib=56, name="attn_bwd", comm=comm)


def _inproj_bwd(dq, dk, dv, ds, db, u, c, conv_w, tc, tsa, tsb, w_in, x, g_pre, dx1):
    def body(dq_ref, dk_ref, dv_ref, ds_ref, db_ref, u_ref, c_ref, dsp_ref, up_ref, cp_ref, dsn_ref, un_ref, cn_ref, cw_ref,
             tc_ref, tsa_ref, tsb_ref, w_ref, x_ref, g_ref, dx1_ref, dx_ref, dproj_ref, h_ref, dg_ref, dcw_ref):
        i = pl.program_id(0)
        cw = cw_ref[...]
        u, c = u_ref[...].astype(F32), c_ref[...].astype(F32)
        t, t_prev, t_next, _ = _conv_parts(u, c, up_ref, cp_ref, un_ref, cn_ref, cw, i)
        ds = ds_ref[...].astype(F32)
        ds_prev, ds_next = _shifted(ds, _last_row(dsp_ref), _first_row(dsn_ref), i)
        dt = cw[0:1, :] * ds_next + cw[1:2, :] * ds + cw[2:3, :] * ds_prev
        _accumulate(dcw_ref, jnp.concatenate([_colsum(ds * t_prev), _colsum(ds * t), _colsum(ds * t_next)], axis=0), i)
        tc_, tsa_, tsb_ = tc_ref[...], tsa_ref[...], tsb_ref[...]
        dproj = jnp.concatenate(
            [_rotate_transposed(dq_ref[...], tc_, tsa_, tsb_).astype(BF16), _rotate_transposed(dk_ref[...], tc_, tsa_, tsb_).astype(BF16),
             dv_ref[...].astype(BF16), (dt * c).astype(BF16), db_ref[...], (dt * u).astype(BF16)], axis=1)
        dproj_ref[...] = dproj
        dh = _dot_nt(dproj, w_ref[...])
        y, r = _rms(x_ref[...])
        h_ref[...] = (y * g_ref[...]).astype(BF16).T
        _accumulate(dg_ref, _colsum(dh * y), i)
        dx_ref[...] = dx1_ref[...] + _rms_bwd(dh * g_ref[...], y, r)

    a = ATTN_WIDTH
    tile = FFN_TILE

    def rows(width):
        return _rows(width, tile)

    return _call(
        body, (dq, dk, dv, ds, db, u, c, ds, u, c, ds, u, c, conv_w, tc, tsa, tsb, w_in, x, g_pre, dx1), grid=(SEQ // tile,),
        in_specs=[rows(a)] * 7 + [_halo_prev(a, tile)] * 3 + [_halo_next(a, tile)] * 3
        + [_const((3, a)), rows(LANES), rows(LANES), rows(LANES), _const1((D_MODEL, IN_PROJ_WIDTH)), rows(D_MODEL),
           _const((1, D_MODEL)), rows(D_MODEL)],
        out_specs=[rows(D_MODEL), rows(IN_PROJ_WIDTH), _cols(D_MODEL, tile), _const((1, D_MODEL)), _const((3, a))],
        out_shape=[_sds((SEQ, D_MODEL), F32), _sds((SEQ, IN_PROJ_WIDTH), BF16), _sds((D_MODEL, SEQ), BF16),
                   _sds((1, D_MODEL), F32), _sds((3, a), F32)],
        vmem_mib=48, name="inproj_bwd")


def _adamw_math(w, g, m, v):
    m = ADAM_B1 * m + (1.0 - ADAM_B1) * g
    v = ADAM_B2 * v + (1.0 - ADAM_B2) * (g * g)
    m_hat = m / (1.0 - ADAM_B1 ** ADAM_STEP)
    v_hat = v / (1.0 - ADAM_B2 ** ADAM_STEP)
    delta = -ADAM_LR * (m_hat / (jnp.sqrt(v_hat) + ADAM_EPS) + ADAM_WD * w)
    return delta, m, v


def _sum_parts(p_ref):
    g = p_ref[0].astype(F32)
    for k in range(1, p_ref.shape[0]):
        g = g + p_ref[k].astype(F32)
    return g


def _adamw_layers(parts, w, m, v, row_tile, name, comm=None):
    _, rows, cols = w.shape
    n_tiles = rows // row_tile
    senders = parts[0].shape[0]

    def body(*refs):
        p_refs = refs[:DEPTH]
        w_ref, m_ref, v_ref, g_ref, d_ref, nm_ref, nv_ref = refs[DEPTH:]
        layer = pl.program_id(0)
        for l, p_ref in enumerate(p_refs):
            @pl.when(layer == l)
            def _(p_ref=p_ref):
                g = _sum_parts(p_ref)
                g_ref[0] = g
                d_ref[0], nm_ref[0], nv_ref[0] = _adamw_math(w_ref[0], g, m_ref[0], v_ref[0])

    def part_spec(l):
        return pl.BlockSpec((senders, row_tile, cols),
                            lambda layer, i: (0, jnp.where(layer == l, i, jnp.where(layer < l, 0, n_tiles - 1)), 0))

    tile = pl.BlockSpec((1, row_tile, cols), lambda layer, i: (layer, i, 0))
    return _call(body, (*parts, w, m, v), grid=(DEPTH, n_tiles), in_specs=[part_spec(l) for l in range(DEPTH)] + [tile] * 3,
                 out_specs=[tile] * 4, out_shape=[_sds(w.shape, F32)] * 4, name=name, comm=comm)


def _adamw_sum8(parts, w, m, v, name):
    def body(p_ref, w_ref, m_ref, v_ref, g_ref, d_ref, nm_ref, nv_ref):
        g = _sum_parts(p_ref)
        g_ref[...] = g
        d_ref[...], nm_ref[...], nv_ref[...] = _adamw_math(w_ref[...], g, m_ref[...], v_ref[...])

    return pl.pallas_call(body, out_shape=[_sds(w.shape, F32)] * 4, name=name)(parts, w, m, v)


def _adamw_plain(g, w, m, v, name):
    def body(g_ref, w_ref, m_ref, v_ref, d_ref, nm_ref, nv_ref):
        d_ref[...], nm_ref[...], nv_ref[...] = _adamw_math(w_ref[...], g_ref[...], m_ref[...], v_ref[...])

    return pl.pallas_call(body, out_shape=[_sds(w.shape, F32)] * 3, name=name)(g, w, m, v)


SMALL_ROWS = 8
GAIN_NAMES = ("pre_mix_norm", "post_mix_norm", "pre_ffn_norm", "post_ffn_norm", "attn_out_norm", "conv_out_norm")


def _pack_small(pre_mix, post_mix, pre_ffn, post_ffn, attn_out, conv_out, taps):
    zeros = jnp.zeros((1, ATTN_WIDTH), F32)
    return jnp.concatenate([
        pre_mix, post_mix, pre_ffn, post_ffn, jnp.concatenate([attn_out, conv_out], axis=1),
        jnp.concatenate([taps[0:1], taps[1:2]], axis=1), jnp.concatenate([taps[2:3], zeros], axis=1),
        jnp.zeros((1, D_MODEL), F32)], axis=0)


def _rope(positions):
    inv_freq = ROPE_THETA ** (-jnp.arange(0, ROPE_DIM, 2, dtype=F32) / ROPE_DIM)
    lane = jnp.arange(LANES) % HEAD_DIM
    freq_row = jnp.where(lane < ROPE_DIM, inv_freq[lane % (ROPE_DIM // 2)], 0.0).reshape(1, LANES).astype(F32)
    return _rope_tables(positions.reshape(SEQ, 1), freq_row)


def _layer_forward(h, gains, taps, tables, w, inproj_comm=None, attn_comm=None, ffn_comm=None):
    (q, k, v, u, b, c), landed = _inproj_fwd(h, gains["pre_mix_norm"], w["in"], *tables, comm=inproj_comm)
    if inproj_comm is not None:
        w = {**w, **dict(zip(inproj_comm.kinds, landed))}
    (attn, lse), landed = _attn_fwd(q, k, v, comm=attn_comm)
    if attn_comm is not None:
        w = {**w, **dict(zip(attn_comm.kinds, landed))}
    x1, merged, mix = _mix_fwd(attn, u, b, c, taps, gains["attn_out_norm"], gains["conv_out_norm"], w["out"], h, gains["post_mix_norm"])
    (x2, gu, f), landed_next = _ffn_fwd(x1, gains["pre_ffn_norm"], w["gu"], w["dn"], gains["post_ffn_norm"], comm=ffn_comm)
    return x2, (h, q, k, v, u, b, c, attn, lse, merged, mix, x1, gu, f), w, landed_next


class _GradientExchange:
    def ffn_grads(self, act, df, h2, dgu):
        return dict(dn=_wgrad_paired(act, df, "dn")[0], gu=_wgrad_paired(dgu, h2, "gu")[0])

    def attention_passenger(self, ffn):
        return _ScatterChips(("gu", "dn"), (ffn["gu"], ffn["dn"]))

    def w_out_grad(self, merged_t, dmix):
        return _wgrad_paired(merged_t, dmix, "out")[0]

    def w_in_grad(self, h_t, dproj, g_out):
        g_in, landed = _wgrad_paired(h_t, dproj, "in", comm=_ScatterChips(("out",), (g_out,)))
        return g_in, landed[0]


def _layer_backward(dx, saved, gains, taps, tables, w, ffn_comm, exchange):
    x0, q, k, v, u, b, c, attn, lse, merged, mix, x1, gu, f = saved
    (dx1, df, act, dgu, h2, dg_post_ffn, dg_pre_ffn), landed_prev = _ffn_bwd(
        dx, f, x1, gu, w["gu"], w["dn"], gains["post_ffn_norm"], gains["pre_ffn_norm"], comm=ffn_comm)
    ffn = exchange.ffn_grads(act, df, h2, dgu)
    dmix, dattn, ds, db, dg_post_mix, dg_attn, dg_conv = _mix_bwd(
        dx1, mix, attn, u, b, c, taps, gains["attn_out_norm"], gains["conv_out_norm"], gains["post_mix_norm"], w["out"])
    g_out = exchange.w_out_grad(merged, dmix)
    attn_comm = exchange.attention_passenger(ffn)
    (dq, dk, dv), landed_ffn = _attn_bwd(q, k, v, dattn, attn, lse, comm=attn_comm)
    dx0, dproj, h1, dg_pre_mix, dtaps = _inproj_bwd(
        dq, dk, dv, ds, db, u, c, taps, *tables, w["in"], x0, gains["pre_mix_norm"], dx1)[0]
    g_in, out_parts = exchange.w_in_grad(h1, dproj, g_out)
    small = _pack_small(dg_pre_mix, dg_post_mix, dg_pre_ffn, dg_post_ffn, dg_attn, dg_conv, dtaps)
    rest = dict(zip(attn_comm.kinds, landed_ffn)) if attn_comm is not None else dict(ffn)
    rest["out"] = out_parts
    return dx0, g_in, small, landed_prev, rest


def kernel(x, positions, pre_mix_norm, w_in, conv_w, attn_out_norm, conv_out_norm, w_out, post_mix_norm, pre_ffn_norm, w_gate_up, w_down, post_ffn_norm, loss_target, m_pre_mix_norm, m_w_in, m_conv_w, m_attn_out_norm, m_conv_out_norm, m_w_out, m_post_mix_norm, m_pre_ffn_norm, m_w_gate_up, m_w_down, m_post_ffn_norm, v_pre_mix_norm, v_w_in, v_conv_w, v_attn_out_norm, v_conv_out_norm, v_w_out, v_post_mix_norm, v_pre_ffn_norm, v_w_gate_up, v_w_down, v_post_ffn_norm):
    mx, my, mc = _place()
    me = _block_of(mx, my, mc)
    conv_channels = conv_w.shape[-1]

    taps_flat = jnp.pad(conv_w.reshape(-1), (0, 8 * LANES - conv_w.size)).reshape(8, LANES)
    taps_all = _allgather_small(taps_flat, "allgather_taps").reshape(N_DEV, 8 * LANES)[:, :conv_w.size]
    conv_w_full = taps_all.reshape(N_DEV, DEPTH, 3, conv_channels).transpose(1, 2, 0, 3).reshape(DEPTH, 3, ATTN_WIDTH)

    def hidden_major(a):
        return jnp.swapaxes(a, 1, 2)

    big_w = dict(zip(WEIGHT_KINDS, (w_in, w_out, hidden_major(w_gate_up), w_down)))
    big_m = dict(zip(WEIGHT_KINDS, (m_w_in, m_w_out, hidden_major(m_w_gate_up), m_w_down)))
    big_v = dict(zip(WEIGHT_KINDS, (v_w_in, v_w_out, hidden_major(v_w_gate_up), v_w_down)))
    shards = {kind: big_w[kind].astype(BF16) for kind in WEIGHT_KINDS}
    all_gains = dict(pre_mix_norm=pre_mix_norm, attn_out_norm=attn_out_norm, conv_out_norm=conv_out_norm, post_mix_norm=post_mix_norm,
                     pre_ffn_norm=pre_ffn_norm, post_ffn_norm=post_ffn_norm)

    def gather(kinds, l):
        return _Gather(kinds, [shards[kind][l] for kind in kinds])

    def gains(l):
        return {name: g[l:l + 1] for name, g in all_gains.items()}

    tables = _rope(positions)

    early = ("in", "out", "dn")
    weights = [dict(zip(early[:1], _comm_only(gather(early[:1], 0), "allgather_first")))] + [None] * (DEPTH - 1)
    saved = [None] * DEPTH
    h = x[0]
    for l in range(DEPTH):
        inproj_comm = gather(early[1:], 0) if l == 0 else None
        ffn_comm = gather(early, l + 1) if l + 1 < DEPTH else None
        h, saved[l], weights[l], landed = _layer_forward(h, gains(l), conv_w_full[l], tables, weights[l], inproj_comm,
                                                         gather(("gu",), l), ffn_comm)
        if ffn_comm is not None:
            weights[l + 1] = dict(zip(ffn_comm.kinds, landed))

    loss, dx = _loss_and_grad(h, loss_target[0])
    loss = lax.psum(loss[0, 0], ("x", "y", "c"))

    parts = {kind: [None] * DEPTH for kind in WEIGHT_KINDS}
    small_grads = [None] * DEPTH
    g_in_above = None
    exchange = _GradientExchange()
    for l in reversed(range(DEPTH)):
        ffn_comm = _ScatterChips(("in",), (g_in_above,)) if g_in_above is not None else None
        dx, g_in_above, small_grads[l], landed, rest = _layer_backward(dx, saved[l], gains(l), conv_w_full[l], tables, weights[l], ffn_comm,
                                                                      exchange)
        if ffn_comm is not None:
            parts["in"][l + 1] = landed[0]
        for kind, part in rest.items():
            parts[kind][l] = part

    parts["in"][0] = _comm_only(_ScatterChips(("in",), (g_in_above,)), "scatter_last")[0]
    tiles = {"in": 256, "out": 128, "gu": 176, "dn": 176}
    big = {kind: _adamw_layers(parts[kind], big_w[kind], big_m[kind], big_v[kind], tiles[kind], "adamw_" + kind)[0]
           for kind in WEIGHT_KINDS}

    packed = jnp.concatenate(small_grads, axis=0)
    gathered = _allgather_small(packed, "allgather_small_grads")

    def pack_state(state):
        rows = [_pack_small(*[state[name][l:l + 1] for name in GAIN_NAMES], jnp.zeros((3, ATTN_WIDTH), F32)) for l in range(DEPTH)]
        return jnp.concatenate(rows, axis=0)

    sw = pack_state(all_gains)
    sm = pack_state(dict(zip(GAIN_NAMES, (m_pre_mix_norm, m_post_mix_norm, m_pre_ffn_norm, m_post_ffn_norm, m_attn_out_norm, m_conv_out_norm))))
    sv = pack_state(dict(zip(GAIN_NAMES, (v_pre_mix_norm, v_post_mix_norm, v_pre_ffn_norm, v_post_ffn_norm, v_attn_out_norm, v_conv_out_norm))))
    sg, sd, snm, snv = _adamw_sum8(gathered, sw, sm, sv, "adamw_small")

    def unpack(p):
        p = p.reshape(DEPTH, SMALL_ROWS, D_MODEL)
        return dict(pre_mix_norm=p[:, 0], post_mix_norm=p[:, 1], pre_ffn_norm=p[:, 2], post_ffn_norm=p[:, 3],
                    attn_out_norm=p[:, 4, :ATTN_WIDTH], conv_out_norm=p[:, 4, ATTN_WIDTH:])

    small = [unpack(p) for p in (sg, sd, snm, snv)]
    sg3 = sg.reshape(DEPTH, SMALL_ROWS, D_MODEL)
    taps_grad_full = jnp.stack([sg3[:, 5, :ATTN_WIDTH], sg3[:, 5, ATTN_WIDTH:], sg3[:, 6, :ATTN_WIDTH]], axis=1)
    taps_grad = lax.dynamic_slice_in_dim(taps_grad_full, me * conv_channels, conv_channels, axis=2)

    def flat(a):
        return a.reshape(DEPTH * 3, conv_channels)

    td, tnm, tnv = _adamw_plain(flat(taps_grad), flat(conv_w), flat(m_conv_w), flat(v_conv_w), "adamw_taps")
    taps = [taps_grad] + [a.reshape(conv_w.shape) for a in (td, tnm, tnv)]

    def leaves(o):
        s = small[o]
        return (s["pre_mix_norm"], big["in"][o], taps[o], s["attn_out_norm"], s["conv_out_norm"], big["out"][o], s["post_mix_norm"],
                s["pre_ffn_norm"], hidden_major(big["gu"][o]), big["dn"][o], s["post_ffn_norm"])

    return (loss, dx[None], *leaves(0), *leaves(1), *leaves(2), *leaves(3))
```

```python
import math

import jax
import jax.numpy as jnp
from jax import lax
from jax.experimental import pallas as pl
from jax.experimental.pallas import tpu as pltpu

F32 = jnp.float32
BF16 = jnp.bfloat16
MESH = pl.DeviceIdType.MESH

SEQ = 4096
D_MODEL = 1024
DEPTH = 4
N_DEV = 8
ATTN_WIDTH = 512
IN_PROJ_WIDTH = 3072
FFN_HIDDEN = 2816
FFN_BLOCK = 2 * FFN_HIDDEN // N_DEV
W_IN_BLOCK = IN_PROJ_WIDTH // N_DEV
W_OUT_BLOCK = D_MODEL // N_DEV
W_DOWN_BLOCK = FFN_HIDDEN // N_DEV
HEAD_DIM = 64
ROPE_DIM = 16
ROPE_THETA = 500000.0
DILATIONS = (1, 4, 16)
HALF_WINDOW = 64
RMS_EPS = 1e-6
NEG_INF = -1e30
LANES = 128
Q_BLOCK = 128
K_WINDOW = Q_BLOCK + 2 * HALF_WINDOW
PERM_CHUNK = 256
ROW_TILE = 512
FFN_TILE = 256
WGRAD_TILE = SEQ
ADAM_LR, ADAM_B1, ADAM_B2, ADAM_EPS, ADAM_WD, ADAM_STEP = 0.001, 0.9, 0.999, 1e-08, 0.01, 10
MIB = 1024 * 1024
PINNED_BYTES = 256 * 1024

WEIGHT_KINDS = ("in", "out", "gu", "dn")
FULL_SHAPES = {"in": (D_MODEL, IN_PROJ_WIDTH), "out": (D_MODEL, D_MODEL), "gu": (N_DEV, FFN_BLOCK, D_MODEL), "dn": (FFN_HIDDEN, D_MODEL)}
SHARD_SHAPES = {"in": (D_MODEL, W_IN_BLOCK), "out": (W_OUT_BLOCK, D_MODEL), "gu": (FFN_BLOCK, D_MODEL), "dn": (W_DOWN_BLOCK, D_MODEL)}
ANY = pl.BlockSpec(memory_space=pl.ANY)


def _sds(shape, dtype):
    return jax.ShapeDtypeStruct(shape, dtype)


def _rows(width, tile=ROW_TILE):
    return pl.BlockSpec((tile, width), lambda i: (i, 0))


def _frows(width):
    return _rows(width, FFN_TILE)


def _cols(height, tile):
    return pl.BlockSpec((height, tile), lambda i: (0, i))


def _const(shape):
    return pl.BlockSpec(shape, lambda i: (0,) * len(shape))


def _const1(shape):
    return pl.BlockSpec(shape, lambda i: (0,) * len(shape), pipeline_mode=pl.Buffered(1))


HALO_ROWS = 16


def _halo_prev(width, tile=ROW_TILE):
    return pl.BlockSpec((HALO_ROWS, width), lambda i: (jnp.maximum(i * (tile // HALO_ROWS) - 1, 0), 0))


def _halo_next(width, tile=ROW_TILE):
    return pl.BlockSpec((HALO_ROWS, width), lambda i: (jnp.minimum((i + 1) * (tile // HALO_ROWS), SEQ // HALO_ROWS - 1), 0))


def _rms(x):
    r = lax.rsqrt(jnp.mean(x * x, axis=-1, keepdims=True) + RMS_EPS)
    return x * r, r


def _rms_bwd(dn, y, r):
    return r * (dn - y * jnp.mean(dn * y, axis=-1, keepdims=True))


def _dot(a, b):
    return jnp.dot(a, b, preferred_element_type=F32)


def _dot_nt(a, b):
    return lax.dot_general(a, b, (((1,), (1,)), ((), ())), preferred_element_type=F32)


def _dot_tn(a, b):
    return lax.dot_general(a, b, (((0,), (0,)), ((), ())), preferred_element_type=F32)


def _place():
    return lax.axis_index("x"), lax.axis_index("y"), lax.axis_index("c")


def _block_of(px, py, pc):
    return 4 * px + 2 * py + pc


def _weight_block(ref, kind, blk):
    if kind == "in":
        return ref.at[:, pl.ds(blk * W_IN_BLOCK, W_IN_BLOCK)]
    if kind == "out":
        return ref.at[pl.ds(blk * W_OUT_BLOCK, W_OUT_BLOCK), :]
    if kind == "gu":
        return ref.at[blk]
    return ref.at[pl.ds(blk * W_DOWN_BLOCK, W_DOWN_BLOCK), :]


def _dma_semaphores(n):
    return [pltpu.SemaphoreType.DMA((n, 7)), pltpu.SemaphoreType.DMA((n, 7)), pltpu.SemaphoreType.DMA((n,))]


class _Gather:
    def __init__(self, kinds, shards):
        self.kinds, self.operands = tuple(kinds), list(shards)
        self.tag = "gather_" + "_".join(kinds)
        self.out_shape = [_sds(FULL_SHAPES[k], BF16) for k in kinds]
        self.scratch = _dma_semaphores(len(kinds))

    def _parties(self):
        x, y, c = _place()
        return (x, y, c), (x, y, 1 - c), [(1 - x, y), (x, 1 - y), (1 - x, 1 - y)], c

    def _copy(self, outs, sems, w, k, block, to, src=None):
        dst = _weight_block(outs[w], self.kinds[w], _block_of(*block))
        return pltpu.make_async_remote_copy(src_ref=dst if src is None else src, dst_ref=dst, send_sem=sems[0].at[w, k],
                                            recv_sem=sems[1].at[w, k], device_id=to, device_id_type=MESH)

    def _own(self, srcs, outs, sems, w, me):
        return pltpu.make_async_copy(srcs[w], _weight_block(outs[w], self.kinds[w], _block_of(*me)), sems[2].at[w])

    def _first(self, srcs, outs, sems, w):
        me, sibling, chips, c = self._parties()
        return [self._copy(outs, sems, w, 0, me, sibling, src=srcs[w])] + [
            self._copy(outs, sems, w, 1 + j, me, (*chip, c), src=srcs[w]) for j, chip in enumerate(chips)]

    def start(self, srcs, outs, sems):
        me = self._parties()[0]
        for w in range(len(self.kinds)):
            self._own(srcs, outs, sems, w, me).start()
            for cp in self._first(srcs, outs, sems, w):
                cp.start()

    def forward(self, srcs, outs, sems):
        me, sibling, chips, c = self._parties()
        for j, chip in enumerate(chips):
            for w in range(len(self.kinds)):
                self._copy(outs, sems, w, 1 + j, (*chip, c), me).wait_recv()
                self._copy(outs, sems, w, 4 + j, (*chip, c), sibling).start()

    def finish(self, srcs, outs, sems):
        me, sibling, chips, c = self._parties()
        for w in range(len(self.kinds)):
            self._copy(outs, sems, w, 0, sibling, me).wait_recv()
            for j, chip in enumerate(chips):
                self._copy(outs, sems, w, 4 + j, (*chip, 1 - c), me).wait_recv()
        for w in range(len(self.kinds)):
            for cp in self._first(srcs, outs, sems, w):
                cp.wait_send()
            for j, chip in enumerate(chips):
                self._copy(outs, sems, w, 4 + j, (*chip, c), sibling).wait_send()
            self._own(srcs, outs, sems, w, me).wait()


def _peers(x, y, c):
    return [(x ^ a, y ^ b, c ^ e) for a in (0, 1) for b in (0, 1) for e in (0, 1) if (a, b, e) != (0, 0, 0)]


class _Scatter:
    def __init__(self, kinds, grads):
        self.kinds, self.operands = tuple(kinds), list(grads)
        self.tag = "scatter_" + "_".join(kinds)
        self.out_shape = [_sds((N_DEV,) + SHARD_SHAPES[k], BF16) for k in kinds]
        self.scratch = _dma_semaphores(len(kinds))

    def _copies(self, srcs, outs, sems):
        x, y, c = _place()
        me = _block_of(x, y, c)
        copies = []
        for w, kind in enumerate(self.kinds):
            copies.append(pltpu.make_async_copy(_weight_block(srcs[w], kind, me), outs[w].at[me], sems[2].at[w]))
            for k, peer in enumerate(_peers(x, y, c)):
                copies.append(pltpu.make_async_remote_copy(
                    src_ref=_weight_block(srcs[w], kind, _block_of(*peer)), dst_ref=outs[w].at[me],
                    send_sem=sems[0].at[w, k], recv_sem=sems[1].at[w, k], device_id=peer, device_id_type=MESH))
        return copies

    def start(self, srcs, outs, sems):
        for cp in self._copies(srcs, outs, sems):
            cp.start()

    def forward(self, srcs, outs, sems):
        pass

    def finish(self, srcs, outs, sems):
        for cp in self._copies(srcs, outs, sems):
            cp.wait()


N_CHIPS = N_DEV // 2


class _ScatterChips:
    def __init__(self, kinds, blocks):
        self.kinds, self.operands = tuple(kinds), list(blocks)
        self.tag = "scatter_chips_" + "_".join(kinds)
        self.out_shape = [_sds((N_CHIPS,) + SHARD_SHAPES[k], BF16) for k in kinds]
        self.scratch = _dma_semaphores(len(kinds))

    def _copies(self, srcs, outs, sems):
        x, y, c = _place()
        mine = 2 * x + y
        copies = []
        for w in range(len(self.kinds)):
            copies.append(pltpu.make_async_copy(srcs[w].at[mine], outs[w].at[mine], sems[2].at[w]))
            for j, (px, py) in enumerate([(1 - x, y), (x, 1 - y), (1 - x, 1 - y)]):
                copies.append(pltpu.make_async_remote_copy(
                    src_ref=srcs[w].at[2 * px + py], dst_ref=outs[w].at[mine], send_sem=sems[0].at[w, j], recv_sem=sems[1].at[w, j],
                    device_id=(px, py, c), device_id_type=MESH))
        return copies

    def start(self, srcs, outs, sems):
        for cp in self._copies(srcs, outs, sems):
            cp.start()

    def forward(self, srcs, outs, sems):
        pass

    def finish(self, srcs, outs, sems):
        for cp in self._copies(srcs, outs, sems):
            cp.wait()


def _in_hbm(a):
    return pltpu.with_memory_space_constraint(a, pltpu.HBM) if a.size * a.dtype.itemsize >= PINNED_BYTES else a


def _out_hbm(s):
    return pltpu.HBM(s.shape, s.dtype) if math.prod(s.shape) * jnp.dtype(s.dtype).itemsize >= PINNED_BYTES else s


def _call(body, args, *, grid, in_specs, out_specs, out_shape, scratch_shapes=(), vmem_mib=None, name, comm=None):
    kwargs = {} if vmem_mib is None else dict(compiler_params=pltpu.CompilerParams(vmem_limit_bytes=vmem_mib * MIB))
    in_specs, out_specs, out_shape, scratch_shapes = list(in_specs), list(out_specs), list(out_shape), list(scratch_shapes)
    args = [_in_hbm(a) for a in args]
    out_shape = [_out_hbm(s) for s in out_shape]
    if comm is None:
        res = pl.pallas_call(body, grid=grid, in_specs=in_specs, out_specs=out_specs, out_shape=out_shape,
                             scratch_shapes=scratch_shapes, name=name, **kwargs)(*args)
        return list(res), None
    n_in, n_out, n_scr = len(in_specs), len(out_specs), len(scratch_shapes)
    c_in, c_out = len(comm.operands), len(comm.out_shape)
    last = math.prod(grid) - 1

    def carried(*refs):
        cuts = [n_in, c_in, n_out, c_out, n_scr]
        parts, at = [], 0
        for n in cuts:
            parts.append(refs[at:at + n])
            at += n
        ins, c_ins, outs, c_outs, scr = parts
        sems = refs[at:]
        step = pl.program_id(0)
        for axis in range(1, len(grid)):
            step = step * grid[axis] + pl.program_id(axis)

        @pl.when(step == 0)
        def _():
            comm.start(c_ins, c_outs, sems)

        @pl.when(step == last)
        def _():
            comm.forward(c_ins, c_outs, sems)

        body(*ins, *outs, *scr)

        @pl.when(step == last)
        def _():
            comm.finish(c_ins, c_outs, sems)

    res = pl.pallas_call(carried, grid=grid, in_specs=in_specs + [ANY] * c_in, out_specs=out_specs + [ANY] * c_out,
                         out_shape=out_shape + [_out_hbm(s) for s in comm.out_shape], scratch_shapes=scratch_shapes + comm.scratch,
                         name=name + "_" + comm.tag, **kwargs)(*args, *[_in_hbm(a) for a in comm.operands])
    return list(res[:n_out]), list(res[n_out:])


def _comm_only(comm, name):
    def body(*refs):
        n_in, n_out = len(comm.operands), len(comm.out_shape)
        srcs, outs, sems = refs[:n_in], refs[n_in:n_in + n_out], refs[n_in + n_out:]
        comm.start(srcs, outs, sems)
        comm.forward(srcs, outs, sems)
        comm.finish(srcs, outs, sems)

    return pl.pallas_call(body, in_specs=[ANY] * len(comm.operands), out_specs=[ANY] * len(comm.out_shape),
                          out_shape=[_out_hbm(s) for s in comm.out_shape], scratch_shapes=comm.scratch,
                          name=name)(*[_in_hbm(a) for a in comm.operands])


def _allgather_small(v, name):
    def body(v_ref, o_ref, send, recv, local):
        x, y, c = _place()
        me = _block_of(x, y, c)
        mine = pltpu.make_async_copy(v_ref, o_ref.at[me], local)
        mine.start()
        copies = [mine]
        for k, peer in enumerate(_peers(x, y, c)):
            cp = pltpu.make_async_remote_copy(src_ref=v_ref, dst_ref=o_ref.at[me], send_sem=send.at[k], recv_sem=recv.at[k],
                                              device_id=peer, device_id_type=MESH)
            cp.start()
            copies.append(cp)
        for cp in copies:
            cp.wait()

    vm = pl.BlockSpec(memory_space=pltpu.VMEM)
    return pl.pallas_call(
        body, in_specs=[vm], out_specs=vm, out_shape=_sds((N_DEV,) + v.shape, F32),
        scratch_shapes=[pltpu.SemaphoreType.DMA((7,)), pltpu.SemaphoreType.DMA((7,)), pltpu.SemaphoreType.DMA],
        name=name)(v)


def _rope_tables(pos_col, freq_row):
    def body(p_ref, f_ref, c_ref, sa_ref, sb_ref):
        ang = p_ref[...].astype(F32) * f_ref[...]
        lane = lax.broadcasted_iota(jnp.int32, ang.shape, 1) % HEAD_DIM
        cos, sin = jnp.cos(ang), jnp.sin(ang)
        c_ref[...] = jnp.where(lane < ROPE_DIM, cos, 1.0)
        sa_ref[...] = jnp.where(lane < ROPE_DIM // 2, -sin, 0.0)
        sb_ref[...] = jnp.where((lane >= ROPE_DIM // 2) & (lane < ROPE_DIM), sin, 0.0)

    return pl.pallas_call(
        body, grid=(SEQ // ROW_TILE,), in_specs=[_rows(1), _const((1, LANES))], out_specs=[_rows(LANES)] * 3,
        out_shape=[_sds((SEQ, LANES), F32)] * 3, name="rope_tables")(pos_col, freq_row)


def _rotate(t, c, sa, sb):
    parts = []
    for g in range(ATTN_WIDTH // LANES):
        tg = t[:, g * LANES:(g + 1) * LANES]
        parts.append(tg * c + pltpu.roll(tg, LANES - 8, axis=1) * sa + pltpu.roll(tg, 8, axis=1) * sb)
    return jnp.concatenate(parts, axis=1)


def _rotate_transposed(dt, c, sa, sb):
    parts = []
    for g in range(ATTN_WIDTH // LANES):
        dg = dt[:, g * LANES:(g + 1) * LANES]
        parts.append(dg * c + pltpu.roll(dg * sa, 8, axis=1) + pltpu.roll(dg * sb, LANES - 8, axis=1))
    return jnp.concatenate(parts, axis=1)


def _inproj_fwd(x, g_pre, w_in, tc, tsa, tsb, comm=None):
    def body(x_ref, g_ref, w_ref, c_ref, sa_ref, sb_ref, q_ref, k_ref, v_ref, u_ref, b_ref, cc_ref):
        y, _ = _rms(x_ref[...])
        h = (y * g_ref[...]).astype(BF16)

        def proj(n):
            return _dot(h, w_ref[:, n * ATTN_WIDTH:(n + 1) * ATTN_WIDTH])

        c, sa, sb = c_ref[...], sa_ref[...], sb_ref[...]
        q_ref[...] = (_rotate(proj(0), c, sa, sb) * (HEAD_DIM ** -0.5)).astype(BF16)
        k_ref[...] = _rotate(proj(1), c, sa, sb).astype(BF16)
        v_ref[...] = proj(2).astype(BF16)
        u_ref[...] = proj(3).astype(BF16)
        b_ref[...] = proj(4).astype(BF16)
        cc_ref[...] = proj(5).astype(BF16)

    a = ATTN_WIDTH
    return _call(
        body, (x, g_pre, w_in, tc, tsa, tsb), grid=(SEQ // ROW_TILE,),
        in_specs=[_rows(D_MODEL), _const((1, D_MODEL)), _const1((D_MODEL, IN_PROJ_WIDTH)), _rows(LANES), _rows(LANES), _rows(LANES)],
        out_specs=[_rows(a)] * 6, out_shape=[_sds((SEQ, a), BF16)] * 6,
        vmem_mib=40, name="inproj_fwd", comm=comm)


def _head_masks():
    lane = lax.broadcasted_iota(jnp.int32, (1, LANES), 1)
    first = lane < HEAD_DIM
    return first, first.astype(F32), 1.0 - first.astype(F32)


def _perm_chunks(dil):
    length = SEQ // dil
    out = []
    for r in range(dil):
        for c0 in range(0, length, PERM_CHUNK):
            chunk = (r * length + c0) // PERM_CHUNK
            rows = pl.ds(c0, PERM_CHUNK) if dil == 1 else pl.ds(r + dil * c0, PERM_CHUNK, stride=dil)
            out.append((chunk, rows))
    return out


def _chunk(c, offset=0):
    return pl.ds(offset + c * PERM_CHUNK, PERM_CHUNK)


def _write_band_bias(bias_ref):
    rr = lax.broadcasted_iota(jnp.int32, (Q_BLOCK, K_WINDOW), 0)
    cc = lax.broadcasted_iota(jnp.int32, (Q_BLOCK, K_WINDOW), 1)
    band = (cc >= rr) & (cc - rr <= 2 * HALF_WINDOW)
    bias_ref[0] = jnp.where(band, 0.0, NEG_INF)
    bias_ref[1] = jnp.where(band & (cc >= HALF_WINDOW), 0.0, NEG_INF)
    bias_ref[2] = jnp.where(band & (cc < Q_BLOCK + HALF_WINDOW), 0.0, NEG_INF)


def _band_bias_index(m0, length):
    return jnp.where(m0 % length == 0, 1, 0) + jnp.where((m0 + Q_BLOCK) % length == 0, 2, 0)


def _zero_key_padding(bufs):
    pad = jnp.zeros((HALF_WINDOW, LANES), BF16)
    for buf in bufs:
        buf[pl.ds(0, HALF_WINDOW), :] = pad
        buf[pl.ds(SEQ + HALF_WINDOW, HALF_WINDOW), :] = pad


def _attn_fwd(q, k, v, comm=None):
    group = 8

    def body(q_ref, k_ref, v_ref, o_ref, lse_ref, q32, k32, v32, qa, qb, kp, vp, accp, mlp,
             acc0, acc1, acc2, ml0, ml1, ml2, bias_ref):
        first, mask_a, mask_b = _head_masks()
        low = lax.broadcasted_iota(jnp.int32, (1, LANES), 1) % HEAD_DIM < HEAD_DIM // 2
        _zero_key_padding((kp, vp))
        _write_band_bias(bias_ref)
        q32[...] = q_ref[...].astype(F32)
        k32[...] = k_ref[...].astype(F32)
        v32[...] = v_ref[...].astype(F32)
        natural = ((acc0, ml0), (acc1, ml1), (acc2, ml2))

        for branch, dil in enumerate(DILATIONS):
            length = SEQ // dil
            assert length >= 2 * Q_BLOCK
            chunks = _perm_chunks(dil)
            for c, rows in chunks:
                val = q32[rows, :]
                qa[_chunk(c), :] = (val * mask_a).astype(BF16)
                qb[_chunk(c), :] = (val * mask_b).astype(BF16)
                kp[_chunk(c, HALF_WINDOW), :] = k32[rows, :].astype(BF16)
                vp[_chunk(c, HALF_WINDOW), :] = v32[rows, :].astype(BF16)
            acc_dst, ml_dst = natural[branch] if dil == 1 else (accp, mlp)

            def blocks(i, carry, length=length, acc_dst=acc_dst, ml_dst=ml_dst):
                base = pl.multiple_of(i * (group * Q_BLOCK), group * Q_BLOCK)
                starts = [base + g * Q_BLOCK for g in range(group)]
                scores = [[_dot_nt(qx[pl.ds(m0, Q_BLOCK), :], kp[pl.ds(m0, K_WINDOW), :]) for qx in (qa, qb)] for m0 in starts]
                probs = []
                for m0, pair in zip(starts, scores):
                    bias = bias_ref[_band_bias_index(m0, length)]
                    stats, ps = [], []
                    for s in pair:
                        s = s + bias
                        m = jnp.max(s, axis=1, keepdims=True)
                        p = jnp.exp(s - m)
                        stats.append(jnp.where(low, m, jnp.sum(p, axis=1, keepdims=True)))
                        ps.append(p.astype(BF16))
                    ml_dst[pl.ds(m0, Q_BLOCK), :] = jnp.where(first, stats[0], stats[1])
                    probs.append(ps)
                for m0, ps in zip(starts, probs):
                    vw = vp[pl.ds(m0, K_WINDOW), :]
                    acc_dst[pl.ds(m0, Q_BLOCK), :] = jnp.where(first, _dot(ps[0], vw), _dot(ps[1], vw))
                return carry

            lax.fori_loop(0, SEQ // (group * Q_BLOCK), blocks, 0)

            if dil > 1:
                for c, rows in chunks:
                    natural[branch][0][rows, :] = accp[_chunk(c), :]
                    natural[branch][1][rows, :] = mlp[_chunk(c), :]

        for c in range(SEQ // PERM_CHUNK):
            packed = [ml[_chunk(c), :] for _, ml in natural]
            ms = [jnp.where(low, ml, pltpu.roll(ml, HEAD_DIM // 2, axis=1)) for ml in packed]
            ls = [jnp.where(low, pltpu.roll(ml, LANES - HEAD_DIM // 2, axis=1), ml) for ml in packed]
            m_all = jnp.maximum(jnp.maximum(ms[0], ms[1]), ms[2])
            es = [jnp.exp(m - m_all) for m in ms]
            z = ls[0] * es[0] + ls[1] * es[1] + ls[2] * es[2]
            num = natural[0][0][_chunk(c), :] * es[0] + natural[1][0][_chunk(c), :] * es[1] + natural[2][0][_chunk(c), :] * es[2]
            o_ref[_chunk(c), :] = num / z
            lse_ref[_chunk(c), :] = m_all + jnp.log(z)

    col = pl.BlockSpec((SEQ, LANES), lambda h: (0, h))
    padded = SEQ + 2 * HALF_WINDOW
    return _call(
        body, (q, k, v), grid=(ATTN_WIDTH // LANES,), in_specs=[col] * 3, out_specs=[col] * 2,
        out_shape=[_sds((SEQ, ATTN_WIDTH), F32)] * 2,
        scratch_shapes=[pltpu.VMEM((SEQ, LANES), F32)] * 3 + [pltpu.VMEM((SEQ, LANES), BF16)] * 2
        + [pltpu.VMEM((padded, LANES), BF16)] * 2 + [pltpu.VMEM((SEQ, LANES), F32)] * 8
        + [pltpu.VMEM((3, Q_BLOCK, K_WINDOW), F32)],
        vmem_mib=52, name="attn_fwd", comm=comm)


def _shifted(t, before, after, i):
    tile = t.shape[0]
    row = lax.broadcasted_iota(jnp.int32, (tile, 1), 0)
    before = jnp.where(i > 0, before, 0.0)
    after = jnp.where(i < SEQ // tile - 1, after, 0.0)
    return (jnp.where(row == 0, before, pltpu.roll(t, 1, axis=0)),
            jnp.where(row == tile - 1, after, pltpu.roll(t, tile - 1, axis=0)))


def _last_row(ref):
    return ref[HALO_ROWS - 1:HALO_ROWS, :].astype(F32)


def _first_row(ref):
    return ref[0:1, :].astype(F32)


def _conv_parts(u, c, u_prev, c_prev, u_next, c_next, cw, i):
    t = c * u
    t_prev, t_next = _shifted(t, _last_row(c_prev) * _last_row(u_prev), _first_row(c_next) * _first_row(u_next), i)
    s = cw[0:1, :] * t_prev + cw[1:2, :] * t + cw[2:3, :] * t_next
    return t, t_prev, t_next, s


def _mix_fwd(attn, u, b, c, conv_w, g_attn, g_conv, w_out, x, g_post):
    def body(a_ref, u_ref, b_ref, c_ref, up_ref, cp_ref, un_ref, cn_ref, cw_ref, ga_ref, gc_ref, w_ref, x_ref, gp_ref,
             x1_ref, mg_ref, mix_ref):
        i = pl.program_id(0)
        _, _, _, s = _conv_parts(u_ref[...].astype(F32), c_ref[...].astype(F32), up_ref, cp_ref, un_ref, cn_ref, cw_ref[...], i)
        ya, _ = _rms(a_ref[...])
        yc, _ = _rms(b_ref[...].astype(F32) * s)
        merged = jnp.concatenate([ya * ga_ref[...], yc * gc_ref[...]], axis=1).astype(BF16)
        mix = _dot(merged, w_ref[...])
        ym, _ = _rms(mix)
        mg_ref[...] = merged.T
        mix_ref[...] = mix.astype(BF16)
        x1_ref[...] = x_ref[...] + ym * gp_ref[...]

    a = ATTN_WIDTH
    return _call(
        body, (attn, u, b, c, u, c, u, c, conv_w, g_attn, g_conv, w_out, x, g_post), grid=(SEQ // ROW_TILE,),
        in_specs=[_rows(a)] * 4 + [_halo_prev(a)] * 2 + [_halo_next(a)] * 2
        + [_const((3, a)), _const((1, a)), _const((1, a)), _const1((D_MODEL, D_MODEL)), _rows(D_MODEL), _const((1, D_MODEL))],
        out_specs=[_rows(D_MODEL), _cols(D_MODEL, ROW_TILE), _rows(D_MODEL)],
        out_shape=[_sds((SEQ, D_MODEL), F32), _sds((D_MODEL, SEQ), BF16), _sds((SEQ, D_MODEL), BF16)],
        vmem_mib=40, name="mix_fwd")[0]


def _gu_spec():
    return pl.BlockSpec((N_DEV, FFN_TILE, FFN_BLOCK), lambda i: (0, i, 0))


def _ffn_fwd(x1, g_pre, w_gu, w_dn, g_post, comm=None, target=None):
    n_tiles = SEQ // ROW_TILE

    def body(*refs):
        if target is None:
            x_ref, g_ref, wgu_ref, wdn_ref, gp_ref, x2_ref, gu_ref, f_ref = refs
        else:
            x_ref, g_ref, wgu_ref, wdn_ref, gp_ref, t_ref, x2_ref, gu_ref, f_ref, loss_ref, acc = refs
        x1 = x_ref[...]
        y, _ = _rms(x1)
        h = (y * g_ref[...]).astype(BF16)
        f = jnp.zeros((ROW_TILE, D_MODEL), F32)
        for j in range(N_DEV // 2):
            gate = _dot_nt(h, wgu_ref[j])
            up = _dot_nt(h, wgu_ref[j + N_DEV // 2])
            gu_ref[j] = gate.astype(BF16)
            gu_ref[j + N_DEV // 2] = up.astype(BF16)
            act = (gate * jax.nn.sigmoid(gate) * up).astype(BF16)
            f = f + _dot(act, wdn_ref[pl.ds(j * FFN_BLOCK, FFN_BLOCK), :])
        yf, _ = _rms(f)
        f_ref[...] = f
        x2 = x1 + yf * gp_ref[...]
        if target is None:
            x2_ref[...] = x2
        else:
            i = pl.program_id(0)
            err = x2 - t_ref[...]
            x2_ref[...] = err * (1.0 / D_MODEL)

            @pl.when(i == 0)
            def _():
                acc[...] = jnp.zeros_like(acc)

            acc[...] += jnp.sum(err * err, axis=0, keepdims=True)

            @pl.when(i == n_tiles - 1)
            def _():
                loss_ref[...] = jnp.sum(acc[...], axis=1, keepdims=True) * (0.5 / D_MODEL)

    with_loss = target is not None
    return _call(
        body, (x1, g_pre, w_gu, w_dn, g_post) + ((target,) if with_loss else ()), grid=(n_tiles,),
        in_specs=[_rows(D_MODEL), _const((1, D_MODEL)), _const1((N_DEV, FFN_BLOCK, D_MODEL)), _const1((FFN_HIDDEN, D_MODEL)),
                  _const((1, D_MODEL))] + ([_rows(D_MODEL)] if with_loss else []),
        out_specs=[_rows(D_MODEL), pl.BlockSpec((N_DEV, ROW_TILE, FFN_BLOCK), lambda i: (0, i, 0)), _rows(D_MODEL)]
        + ([_const((1, 1))] if with_loss else []),
        out_shape=[_sds((SEQ, D_MODEL), F32), _sds((N_DEV, SEQ, FFN_BLOCK), BF16), _sds((SEQ, D_MODEL), F32)]
        + ([_sds((1, 1), F32)] if with_loss else []),
        scratch_shapes=[pltpu.VMEM((1, D_MODEL), F32)] if with_loss else [],
        vmem_mib=58, name="ffn_fwd_loss" if with_loss else "ffn_fwd", comm=comm)


def _accumulate(ref, value, i):
    @pl.when(i == 0)
    def _():
        ref[...] = value

    @pl.when(i > 0)
    def _():
        ref[...] += value


def _colsum(v):
    return jnp.sum(v, axis=0, keepdims=True)


def _ffn_bwd(dx2, f, x1, gu, w_gu, w_dn, g_post, g_pre, comm=None):
    half = N_DEV // 2

    def body(dx2_ref, f_ref, x1_ref, gu_ref, wgu_ref, wdn_ref, gpost_ref, gpre_ref,
             dx1_ref, df_ref, act_ref, dgu_ref, h_ref, dgpost_ref, dgpre_ref):
        i = pl.program_id(0)
        dx2 = dx2_ref[...]
        yf, rf = _rms(f_ref[...])
        _accumulate(dgpost_ref, _colsum(dx2 * yf), i)
        df = _rms_bwd(dx2 * gpost_ref[...], yf, rf).astype(BF16)
        df_ref[...] = df
        dh = jnp.zeros((FFN_TILE, D_MODEL), F32)
        for j in range(half):
            dact = _dot_nt(df, wdn_ref[pl.ds(j * FFN_BLOCK, FFN_BLOCK), :])
            gate = gu_ref[j].astype(F32)
            up = gu_ref[j + half].astype(F32)
            sig = jax.nn.sigmoid(gate)
            silu = gate * sig
            act_ref[j] = (silu * up).astype(BF16)
            dgate = (dact * up * (sig * (1.0 + gate * (1.0 - sig)))).astype(BF16)
            dup = (dact * silu).astype(BF16)
            dgu_ref[j] = dgate
            dgu_ref[j + half] = dup
            dh = dh + _dot(dgate, wgu_ref[j]) + _dot(dup, wgu_ref[j + half])
        y1, r1 = _rms(x1_ref[...])
        h_ref[...] = (y1 * gpre_ref[...]).astype(BF16)
        _accumulate(dgpre_ref, _colsum(dh * y1), i)
        dx1_ref[...] = dx2 + _rms_bwd(dh * gpre_ref[...], y1, r1)

    act_spec = pl.BlockSpec((half, FFN_TILE, FFN_BLOCK), lambda i: (0, i, 0))
    return _call(
        body, (dx2, f, x1, gu, w_gu, w_dn, g_post, g_pre), grid=(SEQ // FFN_TILE,),
        in_specs=[_frows(D_MODEL)] * 3 + [_gu_spec(), _const1((N_DEV, FFN_BLOCK, D_MODEL)), _const1((FFN_HIDDEN, D_MODEL)),
                                          _const((1, D_MODEL)), _const((1, D_MODEL))],
        out_specs=[_frows(D_MODEL), _frows(D_MODEL), act_spec, _gu_spec(), _frows(D_MODEL), _const((1, D_MODEL)), _const((1, D_MODEL))],
        out_shape=[_sds((SEQ, D_MODEL), F32), _sds((SEQ, D_MODEL), BF16), _sds((half, SEQ, FFN_BLOCK), BF16),
                   _sds((N_DEV, SEQ, FFN_BLOCK), BF16), _sds((SEQ, D_MODEL), BF16), _sds((1, D_MODEL), F32), _sds((1, D_MODEL), F32)],
        vmem_mib=52, name="ffn_bwd", comm=comm)


def _wgrad(a_t, g, n_blocks, out_shape, name, comm=None):
    def body(a_ref, g_ref, o_ref):
        o_ref[...] = _dot(a_ref[...], g_ref[...]).astype(BF16)

    res, landed = _call(body, (a_t, g), grid=(n_blocks,), in_specs=[_const1((D_MODEL, SEQ)), pl.BlockSpec((SEQ, 512), lambda j: (0, j))],
                        out_specs=[pl.BlockSpec((D_MODEL, 512), lambda j: (0, j))], out_shape=[_sds(out_shape, BF16)], vmem_mib=48,
                        name=name, comm=comm)
    return res[0], landed


def _wgrad_paired(a, g, kind):
    shard = SHARD_SHAPES[kind]
    per_chip = 2 if kind == "gu" else 1
    n_compute = per_chip * N_CHIPS
    n_steps = n_compute + 1
    specs = [pl.BlockSpec((1, SEQ, FFN_BLOCK), lambda t: (jnp.minimum(t, n_compute - 1), 0, 0)), _const1((SEQ, D_MODEL))]

    def body(a_ref, g_ref, o_ref, sendbuf, recvbuf, keepbuf, send_sem, recv_sem):
        t = pl.program_id(0)
        x, y, c = _place()

        def exchange(q):
            return pltpu.make_async_remote_copy(src_ref=sendbuf.at[q], dst_ref=recvbuf.at[q], send_sem=send_sem.at[q],
                                                recv_sem=recv_sem.at[q], device_id=(x, y, 1 - c), device_id_type=MESH)

        @pl.when((t >= per_chip) & (t % per_chip == 0))
        def _():
            q = t // per_chip - 1
            exchange(q).wait_recv()
            o_ref[0] = (keepbuf[...] + recvbuf[q].astype(F32)).astype(BF16)

        @pl.when(t < n_compute)
        def _():
            q = t // per_chip
            r = _dot_tn(a_ref[0], g_ref[...])
            if kind == "gu":
                @pl.when(t % 2 == c)
                def _():
                    keepbuf[...] = r

                @pl.when(t % 2 != c)
                def _():
                    sendbuf[q] = r.astype(BF16)
                    exchange(q).start()
            else:
                lower, upper = r[:W_DOWN_BLOCK], r[W_DOWN_BLOCK:]
                keepbuf[...] = jnp.where(c == 0, lower, upper)
                sendbuf[q] = jnp.where(c == 0, upper, lower).astype(BF16)
                exchange(q).start()

        @pl.when(t == n_steps - 1)
        def _():
            for q in range(N_CHIPS):
                exchange(q).wait_send()

    res, _ = _call(
        body, (a, g), grid=(n_steps,), in_specs=specs,
        out_specs=[pl.BlockSpec((1,) + shard, lambda t: (jnp.clip(t // per_chip - 1, 0, N_CHIPS - 1), 0, 0))],
        out_shape=[_sds((N_CHIPS,) + shard, BF16)],
        scratch_shapes=[pltpu.VMEM((N_CHIPS,) + shard, BF16), pltpu.VMEM((N_CHIPS,) + shard, BF16), pltpu.VMEM(shard, F32),
                        pltpu.SemaphoreType.DMA((N_CHIPS,)), pltpu.SemaphoreType.DMA((N_CHIPS,))],
        vmem_mib=52, name="wgrad_" + kind)
    return res[0]


def _mix_bwd(dx1, mix, attn, u, b, c, conv_w, g_attn, g_conv, g_post, w_out):
    def body(dx1_ref, mix_ref, a_ref, u_ref, b_ref, c_ref, up_ref, cp_ref, un_ref, cn_ref, cw_ref, ga_ref, gc_ref, gp_ref, w_ref,
             dmix_ref, da_ref, ds_ref, db_ref, dgp_ref, dga_ref, dgc_ref):
        i = pl.program_id(0)
        dx1 = dx1_ref[...]
        ym, rm = _rms(mix_ref[...].astype(F32))
        _accumulate(dgp_ref, _colsum(dx1 * ym), i)
        dmix = _rms_bwd(dx1 * gp_ref[...], ym, rm).astype(BF16)
        dmix_ref[...] = dmix
        dmerged = _dot_nt(dmix, w_ref[...])
        dna, dnc = dmerged[:, :ATTN_WIDTH], dmerged[:, ATTN_WIDTH:]
        ya, ra = _rms(a_ref[...])
        _accumulate(dga_ref, _colsum(dna * ya), i)
        da_ref[...] = _rms_bwd(dna * ga_ref[...], ya, ra)
        _, _, _, s = _conv_parts(u_ref[...].astype(F32), c_ref[...].astype(F32), up_ref, cp_ref, un_ref, cn_ref, cw_ref[...], i)
        gate_b = b_ref[...].astype(F32)
        yc, rc = _rms(gate_b * s)
        _accumulate(dgc_ref, _colsum(dnc * yc), i)
        dy = _rms_bwd(dnc * gc_ref[...], yc, rc)
        db_ref[...] = (dy * s).astype(BF16)
        ds_ref[...] = (dy * gate_b).astype(BF16)

    a = ATTN_WIDTH
    return _call(
        body, (dx1, mix, attn, u, b, c, u, c, u, c, conv_w, g_attn, g_conv, g_post, w_out), grid=(SEQ // ROW_TILE,),
        in_specs=[_rows(D_MODEL)] * 2 + [_rows(a)] * 4 + [_halo_prev(a)] * 2 + [_halo_next(a)] * 2
        + [_const((3, a)), _const((1, a)), _const((1, a)), _const((1, D_MODEL)), _const1((D_MODEL, D_MODEL))],
        out_specs=[_rows(D_MODEL)] + [_rows(a)] * 3 + [_const((1, D_MODEL)), _const((1, a)), _const((1, a))],
        out_shape=[_sds((SEQ, D_MODEL), BF16), _sds((SEQ, a), F32), _sds((SEQ, a), BF16), _sds((SEQ, a), BF16),
                   _sds((1, D_MODEL), F32), _sds((1, a), F32), _sds((1, a), F32)],
        vmem_mib=40, name="mix_bwd")[0]


def _attn_bwd(q, k, v, do, o, lse, comm=None):
    group = 8

    def body(q_ref, k_ref, v_ref, do_ref, o_ref, lse_ref, dq_ref, dk_ref, dv_ref,
             q32, k32, v32, qa, qb, doa, dob, kp, vp, lsep, dlp, dnat, dqp, dkp, dvp, bias_ref):
        first, mask_a, mask_b = _head_masks()
        _zero_key_padding((kp, vp))
        _write_band_bias(bias_ref)
        q32[...] = q_ref[...].astype(F32)
        k32[...] = k_ref[...].astype(F32)
        v32[...] = v_ref[...].astype(F32)
        for c in range(SEQ // PERM_CHUNK):
            prod = do_ref[_chunk(c), :] * o_ref[_chunk(c), :]
            d_a = jnp.sum(prod * mask_a, axis=1, keepdims=True)
            d_b = jnp.sum(prod * mask_b, axis=1, keepdims=True)
            dnat[_chunk(c), :] = jnp.where(first, d_a, d_b)

        for step, dil in enumerate(DILATIONS[1:] + DILATIONS[:1]):
            length = SEQ // dil
            assert length >= 2 * Q_BLOCK
            chunks = _perm_chunks(dil)
            for c, rows in chunks:
                val = q32[rows, :]
                qa[_chunk(c), :] = (val * mask_a).astype(BF16)
                qb[_chunk(c), :] = (val * mask_b).astype(BF16)
                val = do_ref[rows, :]
                doa[_chunk(c), :] = (val * mask_a).astype(BF16)
                dob[_chunk(c), :] = (val * mask_b).astype(BF16)
                kp[_chunk(c, HALF_WINDOW), :] = k32[rows, :].astype(BF16)
                vp[_chunk(c, HALF_WINDOW), :] = v32[rows, :].astype(BF16)
                lsep[_chunk(c), :] = lse_ref[rows, :]
                dlp[_chunk(c), :] = dnat[rows, :]
            zero = jnp.zeros((PERM_CHUNK, LANES), F32)
            for c in range(SEQ // PERM_CHUNK):
                dkp[_chunk(c), :] = zero
                dvp[_chunk(c), :] = zero
            dkp[pl.ds(SEQ, 2 * HALF_WINDOW), :] = zero[:2 * HALF_WINDOW]
            dvp[pl.ds(SEQ, 2 * HALF_WINDOW), :] = zero[:2 * HALF_WINDOW]

            heads = ((qa, doa, 0), (qb, dob, HEAD_DIM))

            def blocks(i, carry, length=length):
                base = pl.multiple_of(i * (group * Q_BLOCK), group * Q_BLOCK)
                starts = [base + g * Q_BLOCK for g in range(group)]
                raw = [[(_dot_nt(qx[pl.ds(m0, Q_BLOCK), :], kp[pl.ds(m0, K_WINDOW), :]),
                         _dot_nt(dox[pl.ds(m0, Q_BLOCK), :], vp[pl.ds(m0, K_WINDOW), :])) for qx, dox, _ in heads]
                       for m0 in starts]
                grads = []
                for m0, pair in zip(starts, raw):
                    bias = bias_ref[_band_bias_index(m0, length)]
                    lse_b, d_b = lsep[pl.ds(m0, Q_BLOCK), :], dlp[pl.ds(m0, Q_BLOCK), :]
                    out = []
                    for (s, dp), (_, _, col) in zip(pair, heads):
                        p = jnp.exp(s + bias - lse_b[:, col:col + 1])
                        out.append(((p * (dp - d_b[:, col:col + 1])).astype(BF16), p.astype(BF16)))
                    grads.append(out)
                for m0, out in zip(starts, grads):
                    qrows, krows = pl.ds(m0, Q_BLOCK), pl.ds(m0, K_WINDOW)
                    kw = kp[krows, :]
                    dk = jnp.zeros((K_WINDOW, LANES), F32)
                    dv = jnp.zeros((K_WINDOW, LANES), F32)
                    for (ds, p), (qx, dox, _) in zip(out, heads):
                        dk = dk + _dot_tn(ds, qx[qrows, :])
                        dv = dv + _dot_tn(p, dox[qrows, :])
                    dqp[qrows, :] = jnp.where(first, _dot(out[0][0], kw), _dot(out[1][0], kw)) * (HEAD_DIM ** -0.5)
                    dkp[krows, :] += dk
                    dvp[krows, :] += dv
                return carry

            lax.fori_loop(0, SEQ // (group * Q_BLOCK), blocks, 0)

            for c, rows in chunks:
                g_q, g_k, g_v = dqp[_chunk(c), :], dkp[_chunk(c, HALF_WINDOW), :], dvp[_chunk(c, HALF_WINDOW), :]
                if step == 0:
                    dq_ref[rows, :] = g_q
                    dk_ref[rows, :] = g_k
                    dv_ref[rows, :] = g_v
                else:
                    dq_ref[rows, :] = dq_ref[rows, :] + g_q
                    dk_ref[rows, :] = dk_ref[rows, :] + g_k
                    dv_ref[rows, :] = dv_ref[rows, :] + g_v

    col = pl.BlockSpec((SEQ, LANES), lambda h: (0, h))
    col1 = pl.BlockSpec((SEQ, LANES), lambda h: (0, h), pipeline_mode=pl.Buffered(1))
    padded = SEQ + 2 * HALF_WINDOW
    return _call(
        body, (q, k, v, do, o, lse), grid=(ATTN_WIDTH // LANES,), in_specs=[col1] * 6, out_specs=[col] * 3,
        out_shape=[_sds((SEQ, ATTN_WIDTH), F32)] * 3,
        scratch_shapes=[pltpu.VMEM((SEQ, LANES), F32)] * 3 + [pltpu.VMEM((SEQ, LANES), BF16)] * 4
        + [pltpu.VMEM((padded, LANES), BF16)] * 2 + [pltpu.VMEM((SEQ, LANES), F32)] * 4 + [pltpu.VMEM((padded, LANES), F32)] * 2
        + [pltpu.VMEM((3, Q_BLOCK, K_WINDOW), F32)],
        vmem_mib=56, name="attn_bwd", comm=comm)


def _inproj_bwd(dq, dk, dv, ds, db, u, c, conv_w, tc, tsa, tsb, w_in, x, g_pre, dx1):
    def body(dq_ref, dk_ref, dv_ref, ds_ref, db_ref, u_ref, c_ref, dsp_ref, up_ref, cp_ref, dsn_ref, un_ref, cn_ref, cw_ref,
             tc_ref, tsa_ref, tsb_ref, w_ref, x_ref, g_ref, dx1_ref, dx_ref, dproj_ref, h_ref, dg_ref, dcw_ref):
        i = pl.program_id(0)
        cw = cw_ref[...]
        u, c = u_ref[...].astype(F32), c_ref[...].astype(F32)
        t, t_prev, t_next, _ = _conv_parts(u, c, up_ref, cp_ref, un_ref, cn_ref, cw, i)
        ds = ds_ref[...].astype(F32)
        ds_prev, ds_next = _shifted(ds, _last_row(dsp_ref), _first_row(dsn_ref), i)
        dt = cw[0:1, :] * ds_next + cw[1:2, :] * ds + cw[2:3, :] * ds_prev
        _accumulate(dcw_ref, jnp.concatenate([_colsum(ds * t_prev), _colsum(ds * t), _colsum(ds * t_next)], axis=0), i)
        tc_, tsa_, tsb_ = tc_ref[...], tsa_ref[...], tsb_ref[...]
        dproj = jnp.concatenate(
            [_rotate_transposed(dq_ref[...], tc_, tsa_, tsb_).astype(BF16), _rotate_transposed(dk_ref[...], tc_, tsa_, tsb_).astype(BF16),
             dv_ref[...].astype(BF16), (dt * c).astype(BF16), db_ref[...], (dt * u).astype(BF16)], axis=1)
        dproj_ref[...] = dproj
        dh = _dot_nt(dproj, w_ref[...])
        y, r = _rms(x_ref[...])
        h_ref[...] = (y * g_ref[...]).astype(BF16).T
        _accumulate(dg_ref, _colsum(dh * y), i)
        dx_ref[...] = dx1_ref[...] + _rms_bwd(dh * g_ref[...], y, r)

    a = ATTN_WIDTH
    tile = FFN_TILE

    def rows(width):
        return _rows(width, tile)

    return _call(
        body, (dq, dk, dv, ds, db, u, c, ds, u, c, ds, u, c, conv_w, tc, tsa, tsb, w_in, x, g_pre, dx1), grid=(SEQ // tile,),
        in_specs=[rows(a)] * 7 + [_halo_prev(a, tile)] * 3 + [_halo_next(a, tile)] * 3
        + [_const((3, a)), rows(LANES), rows(LANES), rows(LANES), _const1((D_MODEL, IN_PROJ_WIDTH)), rows(D_MODEL),
           _const((1, D_MODEL)), rows(D_MODEL)],
        out_specs=[rows(D_MODEL), rows(IN_PROJ_WIDTH), _cols(D_MODEL, tile), _const((1, D_MODEL)), _const((3, a))],
        out_shape=[_sds((SEQ, D_MODEL), F32), _sds((SEQ, IN_PROJ_WIDTH), BF16), _sds((D_MODEL, SEQ), BF16),
                   _sds((1, D_MODEL), F32), _sds((3, a), F32)],
        vmem_mib=48, name="inproj_bwd")


def _adamw_math(w, g, m, v):
    m = ADAM_B1 * m + (1.0 - ADAM_B1) * g
    v = ADAM_B2 * v + (1.0 - ADAM_B2) * (g * g)
    m_hat = m / (1.0 - ADAM_B1 ** ADAM_STEP)
    v_hat = v / (1.0 - ADAM_B2 ** ADAM_STEP)
    delta = -ADAM_LR * (m_hat / (jnp.sqrt(v_hat) + ADAM_EPS) + ADAM_WD * w)
    return delta, m, v


def _sum_parts(p_ref):
    g = p_ref[0].astype(F32)
    for k in range(1, p_ref.shape[0]):
        g = g + p_ref[k].astype(F32)
    return g


def _adamw_layers(parts, w, m, v, row_tile, name, comm=None):
    _, rows, cols = w.shape
    n_tiles = rows // row_tile
    senders = parts[0].shape[0]

    def body(*refs):
        p_refs = refs[:DEPTH]
        w_ref, m_ref, v_ref, g_ref, d_ref, nm_ref, nv_ref = refs[DEPTH:]
        layer = pl.program_id(0)
        for l, p_ref in enumerate(p_refs):
            @pl.when(layer == l)
            def _(p_ref=p_ref):
                g = _sum_parts(p_ref)
                g_ref[0] = g
                d_ref[0], nm_ref[0], nv_ref[0] = _adamw_math(w_ref[0], g, m_ref[0], v_ref[0])

    def part_spec(l):
        return pl.BlockSpec((senders, row_tile, cols),
                            lambda layer, i: (0, jnp.where(layer == l, i, jnp.where(layer < l, 0, n_tiles - 1)), 0))

    tile = pl.BlockSpec((1, row_tile, cols), lambda layer, i: (layer, i, 0))
    return _call(body, (*parts, w, m, v), grid=(DEPTH, n_tiles), in_specs=[part_spec(l) for l in range(DEPTH)] + [tile] * 3,
                 out_specs=[tile] * 4, out_shape=[_sds(w.shape, F32)] * 4, name=name, comm=comm)


def _adamw_sum8(parts, w, m, v, name):
    def body(p_ref, w_ref, m_ref, v_ref, g_ref, d_ref, nm_ref, nv_ref):
        g = _sum_parts(p_ref)
        g_ref[...] = g
        d_ref[...], nm_ref[...], nv_ref[...] = _adamw_math(w_ref[...], g, m_ref[...], v_ref[...])

    return pl.pallas_call(body, out_shape=[_sds(w.shape, F32)] * 4, name=name)(parts, w, m, v)


def _adamw_plain(g, w, m, v, name):
    def body(g_ref, w_ref, m_ref, v_ref, d_ref, nm_ref, nv_ref):
        d_ref[...], nm_ref[...], nv_ref[...] = _adamw_math(w_ref[...], g_ref[...], m_ref[...], v_ref[...])

    return pl.pallas_call(body, out_shape=[_sds(w.shape, F32)] * 3, name=name)(g, w, m, v)


SMALL_ROWS = 8
GAIN_NAMES = ("pre_mix_norm", "post_mix_norm", "pre_ffn_norm", "post_ffn_norm", "attn_out_norm", "conv_out_norm")


def _pack_small(pre_mix, post_mix, pre_ffn, post_ffn, attn_out, conv_out, taps):
    zeros = jnp.zeros((1, ATTN_WIDTH), F32)
    return jnp.concatenate([
        pre_mix, post_mix, pre_ffn, post_ffn, jnp.concatenate([attn_out, conv_out], axis=1),
        jnp.concatenate([taps[0:1], taps[1:2]], axis=1), jnp.concatenate([taps[2:3], zeros], axis=1),
        jnp.zeros((1, D_MODEL), F32)], axis=0)


def _rope(positions):
    inv_freq = ROPE_THETA ** (-jnp.arange(0, ROPE_DIM, 2, dtype=F32) / ROPE_DIM)
    lane = jnp.arange(LANES) % HEAD_DIM
    freq_row = jnp.where(lane < ROPE_DIM, inv_freq[lane % (ROPE_DIM // 2)], 0.0).reshape(1, LANES).astype(F32)
    return _rope_tables(positions.reshape(SEQ, 1), freq_row)


def _layer_forward(h, gains, taps, tables, w, inproj_comm=None, attn_comm=None, ffn_comm=None, target=None):
    (q, k, v, u, b, c), landed = _inproj_fwd(h, gains["pre_mix_norm"], w["in"], *tables, comm=inproj_comm)
    if inproj_comm is not None:
        w = {**w, **dict(zip(inproj_comm.kinds, landed))}
    (attn, lse), landed = _attn_fwd(q, k, v, comm=attn_comm)
    if attn_comm is not None:
        w = {**w, **dict(zip(attn_comm.kinds, landed))}
    x1, merged, mix = _mix_fwd(attn, u, b, c, taps, gains["attn_out_norm"], gains["conv_out_norm"], w["out"], h, gains["post_mix_norm"])
    (x2, gu, f, *loss), landed_next = _ffn_fwd(x1, gains["pre_ffn_norm"], w["gu"], w["dn"], gains["post_ffn_norm"], comm=ffn_comm,
                                               target=target)
    out = x2 if target is None else (x2, loss[0])
    return out, (h, q, k, v, u, b, c, attn, lse, merged, mix, x1, gu, f), w, landed_next


class _GradientExchange:
    def ffn_grads(self, act, df, h2, dgu):
        return dict(dn=_wgrad_paired(act, df, "dn"), gu=_wgrad_paired(dgu, h2, "gu"))

    def attention_passenger(self, ffn):
        return _ScatterChips(("gu", "dn"), (ffn["gu"], ffn["dn"]))

    def w_out_grad(self, merged_t, dmix):
        return _wgrad(merged_t, dmix, D_MODEL // 512, (D_MODEL, D_MODEL), "wgrad_out")[0]

    def w_in_grad(self, h_t, dproj, g_out):
        g_in, landed = _wgrad(h_t, dproj, IN_PROJ_WIDTH // 512, (D_MODEL, IN_PROJ_WIDTH), "wgrad_in", comm=_Scatter(("out",), (g_out,)))
        return g_in, landed[0]


def _layer_backward(dx, saved, gains, taps, tables, w, ffn_comm, exchange):
    x0, q, k, v, u, b, c, attn, lse, merged, mix, x1, gu, f = saved
    (dx1, df, act, dgu, h2, dg_post_ffn, dg_pre_ffn), landed_prev = _ffn_bwd(
        dx, f, x1, gu, w["gu"], w["dn"], gains["post_ffn_norm"], gains["pre_ffn_norm"], comm=ffn_comm)
    ffn = exchange.ffn_grads(act, df, h2, dgu)
    dmix, dattn, ds, db, dg_post_mix, dg_attn, dg_conv = _mix_bwd(
        dx1, mix, attn, u, b, c, taps, gains["attn_out_norm"], gains["conv_out_norm"], gains["post_mix_norm"], w["out"])
    g_out = exchange.w_out_grad(merged, dmix)
    attn_comm = exchange.attention_passenger(ffn)
    (dq, dk, dv), landed_ffn = _attn_bwd(q, k, v, dattn, attn, lse, comm=attn_comm)
    dx0, dproj, h1, dg_pre_mix, dtaps = _inproj_bwd(
        dq, dk, dv, ds, db, u, c, taps, *tables, w["in"], x0, gains["pre_mix_norm"], dx1)[0]
    g_in, out_parts = exchange.w_in_grad(h1, dproj, g_out)
    small = _pack_small(dg_pre_mix, dg_post_mix, dg_pre_ffn, dg_post_ffn, dg_attn, dg_conv, dtaps)
    rest = dict(zip(attn_comm.kinds, landed_ffn)) if attn_comm is not None else dict(ffn)
    rest["out"] = out_parts
    return dx0, g_in, small, landed_prev, rest


def kernel(x, positions, pre_mix_norm, w_in, conv_w, attn_out_norm, conv_out_norm, w_out, post_mix_norm, pre_ffn_norm, w_gate_up, w_down, post_ffn_norm, loss_target, m_pre_mix_norm, m_w_in, m_conv_w, m_attn_out_norm, m_conv_out_norm, m_w_out, m_post_mix_norm, m_pre_ffn_norm, m_w_gate_up, m_w_down, m_post_ffn_norm, v_pre_mix_norm, v_w_in, v_conv_w, v_attn_out_norm, v_conv_out_norm, v_w_out, v_post_mix_norm, v_pre_ffn_norm, v_w_gate_up, v_w_down, v_post_ffn_norm):
    mx, my, mc = _place()
    me = _block_of(mx, my, mc)
    conv_channels = conv_w.shape[-1]

    taps_flat = jnp.pad(conv_w.reshape(-1), (0, 8 * LANES - conv_w.size)).reshape(8, LANES)
    taps_all = _allgather_small(taps_flat, "allgather_taps").reshape(N_DEV, 8 * LANES)[:, :conv_w.size]
    conv_w_full = taps_all.reshape(N_DEV, DEPTH, 3, conv_channels).transpose(1, 2, 0, 3).reshape(DEPTH, 3, ATTN_WIDTH)

    def hidden_major(a):
        return jnp.swapaxes(a, 1, 2)

    big_w = dict(zip(WEIGHT_KINDS, (w_in, w_out, hidden_major(w_gate_up), w_down)))
    big_m = dict(zip(WEIGHT_KINDS, (m_w_in, m_w_out, hidden_major(m_w_gate_up), m_w_down)))
    big_v = dict(zip(WEIGHT_KINDS, (v_w_in, v_w_out, hidden_major(v_w_gate_up), v_w_down)))
    shards = {kind: big_w[kind].astype(BF16) for kind in WEIGHT_KINDS}
    all_gains = dict(pre_mix_norm=pre_mix_norm, attn_out_norm=attn_out_norm, conv_out_norm=conv_out_norm, post_mix_norm=post_mix_norm,
                     pre_ffn_norm=pre_ffn_norm, post_ffn_norm=post_ffn_norm)

    def gather(kinds, l):
        return _Gather(kinds, [shards[kind][l] for kind in kinds])

    def gains(l):
        return {name: g[l:l + 1] for name, g in all_gains.items()}

    tables = _rope(positions)

    early = ("in", "out", "dn")
    weights = [dict(zip(early[:1], _comm_only(gather(early[:1], 0), "allgather_first")))] + [None] * (DEPTH - 1)
    saved = [None] * DEPTH
    h = x[0]
    for l in range(DEPTH):
        inproj_comm = gather(early[1:], 0) if l == 0 else None
        ffn_comm = gather(early, l + 1) if l + 1 < DEPTH else None
        target = loss_target[0] if l + 1 == DEPTH else None
        h, saved[l], weights[l], landed = _layer_forward(h, gains(l), conv_w_full[l], tables, weights[l], inproj_comm,
                                                         gather(("gu",), l), ffn_comm, target)
        if ffn_comm is not None:
            weights[l + 1] = dict(zip(ffn_comm.kinds, landed))
    dx, local_loss = h

    parts = {kind: [None] * DEPTH for kind in WEIGHT_KINDS}
    small_grads = [None] * DEPTH
    g_in_above = None
    exchange = _GradientExchange()
    for l in reversed(range(DEPTH)):
        ffn_comm = _Scatter(("in",), (g_in_above,)) if g_in_above is not None else None
        dx, g_in_above, small_grads[l], landed, rest = _layer_backward(dx, saved[l], gains(l), conv_w_full[l], tables, weights[l], ffn_comm,
                                                                      exchange)
        if ffn_comm is not None:
            parts["in"][l + 1] = landed[0]
        for kind, part in rest.items():
            parts[kind][l] = part

    parts["in"][0] = _comm_only(_Scatter(("in",), (g_in_above,)), "scatter_last")[0]
    tiles = {"in": 256, "out": 128, "gu": 176, "dn": 176}
    big = {kind: _adamw_layers(parts[kind], big_w[kind], big_m[kind], big_v[kind], tiles[kind], "adamw_" + kind)[0]
           for kind in WEIGHT_KINDS}

    packed = jnp.concatenate(small_grads, axis=0)
    packed = lax.dynamic_update_slice(packed, local_loss, (SMALL_ROWS - 1, 0))
    gathered = _allgather_small(packed, "allgather_small_grads")

    def pack_state(state):
        rows = [_pack_small(*[state[name][l:l + 1] for name in GAIN_NAMES], jnp.zeros((3, ATTN_WIDTH), F32)) for l in range(DEPTH)]
        return jnp.concatenate(rows, axis=0)

    sw = pack_state(all_gains)
    sm = pack_state(dict(zip(GAIN_NAMES, (m_pre_mix_norm, m_post_mix_norm, m_pre_ffn_norm, m_post_ffn_norm, m_attn_out_norm, m_conv_out_norm))))
    sv = pack_state(dict(zip(GAIN_NAMES, (v_pre_mix_norm, v_post_mix_norm, v_pre_ffn_norm, v_post_ffn_norm, v_attn_out_norm, v_conv_out_norm))))
    sg, sd, snm, snv = _adamw_sum8(gathered, sw, sm, sv, "adamw_small")

    def unpack(p):
        p = p.reshape(DEPTH, SMALL_ROWS, D_MODEL)
        return dict(pre_mix_norm=p[:, 0], post_mix_norm=p[:, 1], pre_ffn_norm=p[:, 2], post_ffn_norm=p[:, 3],
                    attn_out_norm=p[:, 4, :ATTN_WIDTH], conv_out_norm=p[:, 4, ATTN_WIDTH:])

    small = [unpack(p) for p in (sg, sd, snm, snv)]
    loss = sg[SMALL_ROWS - 1, 0]
    sg3 = sg.reshape(DEPTH, SMALL_ROWS, D_MODEL)
    taps_grad_full = jnp.stack([sg3[:, 5, :ATTN_WIDTH], sg3[:, 5, ATTN_WIDTH:], sg3[:, 6, :ATTN_WIDTH]], axis=1)
    taps_grad = lax.dynamic_slice_in_dim(taps_grad_full, me * conv_channels, conv_channels, axis=2)

    def flat(a):
        return a.reshape(DEPTH * 3, conv_channels)

    td, tnm, tnv = _adamw_plain(flat(taps_grad), flat(conv_w), flat(m_conv_w), flat(v_conv_w), "adamw_taps")
    taps = [taps_grad] + [a.reshape(conv_w.shape) for a in (td, tnm, tnv)]

    def leaves(o):
        s = small[o]
        return (s["pre_mix_norm"], big["in"][o], taps[o], s["attn_out_norm"], s["conv_out_norm"], big["out"][o], s["post_mix_norm"],
                s["pre_ffn_norm"], hidden_major(big["gu"][o]), big["dn"][o], s["post_ffn_norm"])

    return (loss, dx[None], *leaves(0), *leaves(1), *leaves(2), *leaves(3))
```

```python
import math

import jax
import jax.numpy as jnp
from jax import lax
from jax.experimental import pallas as pl
from jax.experimental.pallas import tpu as pltpu

F32 = jnp.float32
BF16 = jnp.bfloat16
MESH = pl.DeviceIdType.MESH

SEQ = 4096
D_MODEL = 1024
DEPTH = 4
N_DEV = 8
ATTN_WIDTH = 512
IN_PROJ_WIDTH = 3072
FFN_HIDDEN = 2816
FFN_BLOCK = 2 * FFN_HIDDEN // N_DEV
W_IN_BLOCK = IN_PROJ_WIDTH // N_DEV
W_OUT_BLOCK = D_MODEL // N_DEV
W_DOWN_BLOCK = FFN_HIDDEN // N_DEV
HEAD_DIM = 64
ROPE_DIM = 16
ROPE_THETA = 500000.0
DILATIONS = (1, 4, 16)
HALF_WINDOW = 64
RMS_EPS = 1e-6
NEG_INF = -1e30
LANES = 128
Q_BLOCK = 128
K_WINDOW = Q_BLOCK + 2 * HALF_WINDOW
PERM_CHUNK = 256
ROW_TILE = 512
FFN_TILE = 256
WGRAD_TILE = SEQ
ADAM_LR, ADAM_B1, ADAM_B2, ADAM_EPS, ADAM_WD, ADAM_STEP = 0.001, 0.9, 0.999, 1e-08, 0.01, 10
MIB = 1024 * 1024
PINNED_BYTES = 256 * 1024

WEIGHT_KINDS = ("in", "out", "gu", "dn")
FULL_SHAPES = {"in": (D_MODEL, IN_PROJ_WIDTH), "out": (D_MODEL, D_MODEL), "gu": (N_DEV, FFN_BLOCK, D_MODEL), "dn": (FFN_HIDDEN, D_MODEL)}
SHARD_SHAPES = {"in": (D_MODEL, W_IN_BLOCK), "out": (W_OUT_BLOCK, D_MODEL), "gu": (FFN_BLOCK, D_MODEL), "dn": (W_DOWN_BLOCK, D_MODEL)}
ANY = pl.BlockSpec(memory_space=pl.ANY)


def _sds(shape, dtype):
    return jax.ShapeDtypeStruct(shape, dtype)


def _rows(width, tile=ROW_TILE):
    return pl.BlockSpec((tile, width), lambda i: (i, 0))


def _frows(width):
    return _rows(width, FFN_TILE)


def _cols(height, tile):
    return pl.BlockSpec((height, tile), lambda i: (0, i))


def _const(shape):
    return pl.BlockSpec(shape, lambda i: (0,) * len(shape))


def _const1(shape):
    return pl.BlockSpec(shape, lambda i: (0,) * len(shape), pipeline_mode=pl.Buffered(1))


HALO_ROWS = 16


def _halo_prev(width, tile=ROW_TILE):
    return pl.BlockSpec((HALO_ROWS, width), lambda i: (jnp.maximum(i * (tile // HALO_ROWS) - 1, 0), 0))


def _halo_next(width, tile=ROW_TILE):
    return pl.BlockSpec((HALO_ROWS, width), lambda i: (jnp.minimum((i + 1) * (tile // HALO_ROWS), SEQ // HALO_ROWS - 1), 0))


def _rms(x):
    r = lax.rsqrt(jnp.mean(x * x, axis=-1, keepdims=True) + RMS_EPS)
    return x * r, r


def _rms_bwd(dn, y, r):
    return r * (dn - y * jnp.mean(dn * y, axis=-1, keepdims=True))


def _dot(a, b):
    return jnp.dot(a, b, preferred_element_type=F32)


def _dot_nt(a, b):
    return lax.dot_general(a, b, (((1,), (1,)), ((), ())), preferred_element_type=F32)


def _dot_tn(a, b):
    return lax.dot_general(a, b, (((0,), (0,)), ((), ())), preferred_element_type=F32)


def _place():
    return lax.axis_index("x"), lax.axis_index("y"), lax.axis_index("c")


def _block_of(px, py, pc):
    return 4 * px + 2 * py + pc


def _weight_block(ref, kind, blk):
    if kind == "in":
        return ref.at[:, pl.ds(blk * W_IN_BLOCK, W_IN_BLOCK)]
    if kind == "out":
        return ref.at[pl.ds(blk * W_OUT_BLOCK, W_OUT_BLOCK), :]
    if kind == "gu":
        return ref.at[blk]
    return ref.at[pl.ds(blk * W_DOWN_BLOCK, W_DOWN_BLOCK), :]


def _dma_semaphores(n):
    return [pltpu.SemaphoreType.DMA((n, 7)), pltpu.SemaphoreType.DMA((n, 7)), pltpu.SemaphoreType.DMA((n,))]


class _Gather:
    def __init__(self, kinds, shards):
        self.kinds, self.operands = tuple(kinds), list(shards)
        self.tag = "gather_" + "_".join(kinds)
        self.out_shape = [_sds(FULL_SHAPES[k], BF16) for k in kinds]
        self.scratch = _dma_semaphores(len(kinds))

    def _parties(self):
        x, y, c = _place()
        return (x, y, c), (x, y, 1 - c), [(1 - x, y), (x, 1 - y), (1 - x, 1 - y)], c

    def _copy(self, outs, sems, w, k, block, to, src=None):
        dst = _weight_block(outs[w], self.kinds[w], _block_of(*block))
        return pltpu.make_async_remote_copy(src_ref=dst if src is None else src, dst_ref=dst, send_sem=sems[0].at[w, k],
                                            recv_sem=sems[1].at[w, k], device_id=to, device_id_type=MESH)

    def _own(self, srcs, outs, sems, w, me):
        return pltpu.make_async_copy(srcs[w], _weight_block(outs[w], self.kinds[w], _block_of(*me)), sems[2].at[w])

    def _first(self, srcs, outs, sems, w):
        me, sibling, chips, c = self._parties()
        return [self._copy(outs, sems, w, 0, me, sibling, src=srcs[w])] + [
            self._copy(outs, sems, w, 1 + j, me, (*chip, c), src=srcs[w]) for j, chip in enumerate(chips)]

    def start(self, srcs, outs, sems):
        me = self._parties()[0]
        for w in range(len(self.kinds)):
            self._own(srcs, outs, sems, w, me).start()
            for cp in self._first(srcs, outs, sems, w):
                cp.start()

    def forward(self, srcs, outs, sems):
        me, sibling, chips, c = self._parties()
        for j, chip in enumerate(chips):
            for w in range(len(self.kinds)):
                self._copy(outs, sems, w, 1 + j, (*chip, c), me).wait_recv()
                self._copy(outs, sems, w, 4 + j, (*chip, c), sibling).start()

    def finish(self, srcs, outs, sems):
        me, sibling, chips, c = self._parties()
        for w in range(len(self.kinds)):
            self._copy(outs, sems, w, 0, sibling, me).wait_recv()
            for j, chip in enumerate(chips):
                self._copy(outs, sems, w, 4 + j, (*chip, 1 - c), me).wait_recv()
        for w in range(len(self.kinds)):
            for cp in self._first(srcs, outs, sems, w):
                cp.wait_send()
            for j, chip in enumerate(chips):
                self._copy(outs, sems, w, 4 + j, (*chip, c), sibling).wait_send()
            self._own(srcs, outs, sems, w, me).wait()


def _peers(x, y, c):
    return [(x ^ a, y ^ b, c ^ e) for a in (0, 1) for b in (0, 1) for e in (0, 1) if (a, b, e) != (0, 0, 0)]


class _Scatter:
    def __init__(self, kinds, grads):
        self.kinds, self.operands = tuple(kinds), list(grads)
        self.tag = "scatter_" + "_".join(kinds)
        self.out_shape = [_sds((N_DEV,) + SHARD_SHAPES[k], BF16) for k in kinds]
        self.scratch = _dma_semaphores(len(kinds))

    def _copies(self, srcs, outs, sems):
        x, y, c = _place()
        me = _block_of(x, y, c)
        copies = []
        for w, kind in enumerate(self.kinds):
            copies.append(pltpu.make_async_copy(_weight_block(srcs[w], kind, me), outs[w].at[me], sems[2].at[w]))
            for k, peer in enumerate(_peers(x, y, c)):
                copies.append(pltpu.make_async_remote_copy(
                    src_ref=_weight_block(srcs[w], kind, _block_of(*peer)), dst_ref=outs[w].at[me],
                    send_sem=sems[0].at[w, k], recv_sem=sems[1].at[w, k], device_id=peer, device_id_type=MESH))
        return copies

    def start(self, srcs, outs, sems):
        for cp in self._copies(srcs, outs, sems):
            cp.start()

    def forward(self, srcs, outs, sems):
        pass

    def finish(self, srcs, outs, sems):
        for cp in self._copies(srcs, outs, sems):
            cp.wait()


N_CHIPS = N_DEV // 2


class _ScatterChips:
    def __init__(self, kinds, blocks):
        self.kinds, self.operands = tuple(kinds), list(blocks)
        self.tag = "scatter_chips_" + "_".join(kinds)
        self.out_shape = [_sds((N_CHIPS,) + SHARD_SHAPES[k], BF16) for k in kinds]
        self.scratch = _dma_semaphores(len(kinds))

    def _copies(self, srcs, outs, sems):
        x, y, c = _place()
        mine = 2 * x + y
        copies = []
        for w in range(len(self.kinds)):
            copies.append(pltpu.make_async_copy(srcs[w].at[mine], outs[w].at[mine], sems[2].at[w]))
            for j, (px, py) in enumerate([(1 - x, y), (x, 1 - y), (1 - x, 1 - y)]):
                copies.append(pltpu.make_async_remote_copy(
                    src_ref=srcs[w].at[2 * px + py], dst_ref=outs[w].at[mine], send_sem=sems[0].at[w, j], recv_sem=sems[1].at[w, j],
                    device_id=(px, py, c), device_id_type=MESH))
        return copies

    def start(self, srcs, outs, sems):
        for cp in self._copies(srcs, outs, sems):
            cp.start()

    def forward(self, srcs, outs, sems):
        pass

    def finish(self, srcs, outs, sems):
        for cp in self._copies(srcs, outs, sems):
            cp.wait()


def _in_hbm(a):
    return pltpu.with_memory_space_constraint(a, pltpu.HBM) if a.size * a.dtype.itemsize >= PINNED_BYTES else a


def _out_hbm(s):
    return pltpu.HBM(s.shape, s.dtype) if math.prod(s.shape) * jnp.dtype(s.dtype).itemsize >= PINNED_BYTES else s


def _call(body, args, *, grid, in_specs, out_specs, out_shape, scratch_shapes=(), vmem_mib=None, name, comm=None):
    kwargs = {} if vmem_mib is None else dict(compiler_params=pltpu.CompilerParams(vmem_limit_bytes=vmem_mib * MIB))
    in_specs, out_specs, out_shape, scratch_shapes = list(in_specs), list(out_specs), list(out_shape), list(scratch_shapes)
    args = [_in_hbm(a) for a in args]
    out_shape = [_out_hbm(s) for s in out_shape]
    if comm is None:
        res = pl.pallas_call(body, grid=grid, in_specs=in_specs, out_specs=out_specs, out_shape=out_shape,
                             scratch_shapes=scratch_shapes, name=name, **kwargs)(*args)
        return list(res), None
    n_in, n_out, n_scr = len(in_specs), len(out_specs), len(scratch_shapes)
    c_in, c_out = len(comm.operands), len(comm.out_shape)
    last = math.prod(grid) - 1

    def carried(*refs):
        cuts = [n_in, c_in, n_out, c_out, n_scr]
        parts, at = [], 0
        for n in cuts:
            parts.append(refs[at:at + n])
            at += n
        ins, c_ins, outs, c_outs, scr = parts
        sems = refs[at:]
        step = pl.program_id(0)
        for axis in range(1, len(grid)):
            step = step * grid[axis] + pl.program_id(axis)

        @pl.when(step == 0)
        def _():
            comm.start(c_ins, c_outs, sems)

        @pl.when(step == last)
        def _():
            comm.forward(c_ins, c_outs, sems)

        body(*ins, *outs, *scr)

        @pl.when(step == last)
        def _():
            comm.finish(c_ins, c_outs, sems)

    res = pl.pallas_call(carried, grid=grid, in_specs=in_specs + [ANY] * c_in, out_specs=out_specs + [ANY] * c_out,
                         out_shape=out_shape + [_out_hbm(s) for s in comm.out_shape], scratch_shapes=scratch_shapes + comm.scratch,
                         name=name + "_" + comm.tag, **kwargs)(*args, *[_in_hbm(a) for a in comm.operands])
    return list(res[:n_out]), list(res[n_out:])


def _comm_only(comm, name):
    def body(*refs):
        n_in, n_out = len(comm.operands), len(comm.out_shape)
        srcs, outs, sems = refs[:n_in], refs[n_in:n_in + n_out], refs[n_in + n_out:]
        comm.start(srcs, outs, sems)
        comm.forward(srcs, outs, sems)
        comm.finish(srcs, outs, sems)

    return pl.pallas_call(body, in_specs=[ANY] * len(comm.operands), out_specs=[ANY] * len(comm.out_shape),
                          out_shape=[_out_hbm(s) for s in comm.out_shape], scratch_shapes=comm.scratch,
                          name=name)(*[_in_hbm(a) for a in comm.operands])


def _allgather_small(v, name):
    def body(v_ref, o_ref, send, recv, local):
        x, y, c = _place()
        me = _block_of(x, y, c)
        mine = pltpu.make_async_copy(v_ref, o_ref.at[me], local)
        mine.start()
        copies = [mine]
        for k, peer in enumerate(_peers(x, y, c)):
            cp = pltpu.make_async_remote_copy(src_ref=v_ref, dst_ref=o_ref.at[me], send_sem=send.at[k], recv_sem=recv.at[k],
                                              device_id=peer, device_id_type=MESH)
            cp.start()
            copies.append(cp)
        for cp in copies:
            cp.wait()

    vm = pl.BlockSpec(memory_space=pltpu.VMEM)
    return pl.pallas_call(
        body, in_specs=[vm], out_specs=vm, out_shape=_sds((N_DEV,) + v.shape, F32),
        scratch_shapes=[pltpu.SemaphoreType.DMA((7,)), pltpu.SemaphoreType.DMA((7,)), pltpu.SemaphoreType.DMA],
        name=name)(v)


def _rope_tables(pos_col, freq_row):
    def body(p_ref, f_ref, c_ref, sa_ref, sb_ref):
        ang = p_ref[...].astype(F32) * f_ref[...]
        lane = lax.broadcasted_iota(jnp.int32, ang.shape, 1) % HEAD_DIM
        cos, sin = jnp.cos(ang), jnp.sin(ang)
        c_ref[...] = jnp.where(lane < ROPE_DIM, cos, 1.0)
        sa_ref[...] = jnp.where(lane < ROPE_DIM // 2, -sin, 0.0)
        sb_ref[...] = jnp.where((lane >= ROPE_DIM // 2) & (lane < ROPE_DIM), sin, 0.0)

    return pl.pallas_call(
        body, grid=(SEQ // ROW_TILE,), in_specs=[_rows(1), _const((1, LANES))], out_specs=[_rows(LANES)] * 3,
        out_shape=[_sds((SEQ, LANES), F32)] * 3, name="rope_tables")(pos_col, freq_row)


def _rotate(t, c, sa, sb):
    parts = []
    for g in range(ATTN_WIDTH // LANES):
        tg = t[:, g * LANES:(g + 1) * LANES]
        parts.append(tg * c + pltpu.roll(tg, LANES - 8, axis=1) * sa + pltpu.roll(tg, 8, axis=1) * sb)
    return jnp.concatenate(parts, axis=1)


def _rotate_transposed(dt, c, sa, sb):
    parts = []
    for g in range(ATTN_WIDTH // LANES):
        dg = dt[:, g * LANES:(g + 1) * LANES]
        parts.append(dg * c + pltpu.roll(dg * sa, 8, axis=1) + pltpu.roll(dg * sb, LANES - 8, axis=1))
    return jnp.concatenate(parts, axis=1)


def _inproj_fwd(x, g_pre, w_in, tc, tsa, tsb, comm=None):
    def body(x_ref, g_ref, w_ref, c_ref, sa_ref, sb_ref, q_ref, k_ref, v_ref, u_ref, b_ref, cc_ref):
        y, _ = _rms(x_ref[...])
        h = (y * g_ref[...]).astype(BF16)

        def proj(n):
            return _dot(h, w_ref[:, n * ATTN_WIDTH:(n + 1) * ATTN_WIDTH])

        c, sa, sb = c_ref[...], sa_ref[...], sb_ref[...]
        q_ref[...] = (_rotate(proj(0), c, sa, sb) * (HEAD_DIM ** -0.5)).astype(BF16)
        k_ref[...] = _rotate(proj(1), c, sa, sb).astype(BF16)
        v_ref[...] = proj(2).astype(BF16)
        u_ref[...] = proj(3).astype(BF16)
        b_ref[...] = proj(4).astype(BF16)
        cc_ref[...] = proj(5).astype(BF16)

    a = ATTN_WIDTH
    return _call(
        body, (x, g_pre, w_in, tc, tsa, tsb), grid=(SEQ // ROW_TILE,),
        in_specs=[_rows(D_MODEL), _const((1, D_MODEL)), _const1((D_MODEL, IN_PROJ_WIDTH)), _rows(LANES), _rows(LANES), _rows(LANES)],
        out_specs=[_rows(a)] * 6, out_shape=[_sds((SEQ, a), BF16)] * 6,
        vmem_mib=40, name="inproj_fwd", comm=comm)


def _head_masks():
    lane = lax.broadcasted_iota(jnp.int32, (1, LANES), 1)
    first = lane < HEAD_DIM
    return first, first.astype(F32), 1.0 - first.astype(F32)


def _perm_chunks(dil):
    length = SEQ // dil
    out = []
    for r in range(dil):
        for c0 in range(0, length, PERM_CHUNK):
            chunk = (r * length + c0) // PERM_CHUNK
            rows = pl.ds(c0, PERM_CHUNK) if dil == 1 else pl.ds(r + dil * c0, PERM_CHUNK, stride=dil)
            out.append((chunk, rows))
    return out


def _chunk(c, offset=0):
    return pl.ds(offset + c * PERM_CHUNK, PERM_CHUNK)


def _write_band_bias(bias_ref):
    rr = lax.broadcasted_iota(jnp.int32, (Q_BLOCK, K_WINDOW), 0)
    cc = lax.broadcasted_iota(jnp.int32, (Q_BLOCK, K_WINDOW), 1)
    band = (cc >= rr) & (cc - rr <= 2 * HALF_WINDOW)
    bias_ref[0] = jnp.where(band, 0.0, NEG_INF)
    bias_ref[1] = jnp.where(band & (cc >= HALF_WINDOW), 0.0, NEG_INF)
    bias_ref[2] = jnp.where(band & (cc < Q_BLOCK + HALF_WINDOW), 0.0, NEG_INF)


def _band_bias_index(m0, length):
    return jnp.where(m0 % length == 0, 1, 0) + jnp.where((m0 + Q_BLOCK) % length == 0, 2, 0)


def _zero_key_padding(bufs):
    pad = jnp.zeros((HALF_WINDOW, LANES), BF16)
    for buf in bufs:
        buf[pl.ds(0, HALF_WINDOW), :] = pad
        buf[pl.ds(SEQ + HALF_WINDOW, HALF_WINDOW), :] = pad


def _attn_fwd(q, k, v, comm=None):
    group = 8

    def body(q_ref, k_ref, v_ref, o_ref, lse_ref, q32, k32, v32, qa, qb, kp, vp, accp, mlp,
             acc0, acc1, acc2, ml0, ml1, ml2, bias_ref):
        first, mask_a, mask_b = _head_masks()
        low = lax.broadcasted_iota(jnp.int32, (1, LANES), 1) % HEAD_DIM < HEAD_DIM // 2
        _zero_key_padding((kp, vp))
        _write_band_bias(bias_ref)
        q32[...] = q_ref[...].astype(F32)
        k32[...] = k_ref[...].astype(F32)
        v32[...] = v_ref[...].astype(F32)
        natural = ((acc0, ml0), (acc1, ml1), (acc2, ml2))

        for branch, dil in enumerate(DILATIONS):
            length = SEQ // dil
            assert length >= 2 * Q_BLOCK
            chunks = _perm_chunks(dil)
            for c, rows in chunks:
                val = q32[rows, :]
                qa[_chunk(c), :] = (val * mask_a).astype(BF16)
                qb[_chunk(c), :] = (val * mask_b).astype(BF16)
                kp[_chunk(c, HALF_WINDOW), :] = k32[rows, :].astype(BF16)
                vp[_chunk(c, HALF_WINDOW), :] = v32[rows, :].astype(BF16)
            acc_dst, ml_dst = natural[branch] if dil == 1 else (accp, mlp)

            def blocks(i, carry, length=length, acc_dst=acc_dst, ml_dst=ml_dst):
                base = pl.multiple_of(i * (group * Q_BLOCK), group * Q_BLOCK)
                starts = [base + g * Q_BLOCK for g in range(group)]
                scores = [[_dot_nt(qx[pl.ds(m0, Q_BLOCK), :], kp[pl.ds(m0, K_WINDOW), :]) for qx in (qa, qb)] for m0 in starts]
                probs = []
                for m0, pair in zip(starts, scores):
                    bias = bias_ref[_band_bias_index(m0, length)]
                    stats, ps = [], []
                    for s in pair:
                        s = s + bias
                        m = jnp.max(s, axis=1, keepdims=True)
                        p = jnp.exp(s - m)
                        stats.append(jnp.where(low, m, jnp.sum(p, axis=1, keepdims=True)))
                        ps.append(p.astype(BF16))
                    ml_dst[pl.ds(m0, Q_BLOCK), :] = jnp.where(first, stats[0], stats[1])
                    probs.append(ps)
                for m0, ps in zip(starts, probs):
                    vw = vp[pl.ds(m0, K_WINDOW), :]
                    acc_dst[pl.ds(m0, Q_BLOCK), :] = jnp.where(first, _dot(ps[0], vw), _dot(ps[1], vw))
                return carry

            lax.fori_loop(0, SEQ // (group * Q_BLOCK), blocks, 0)

            if dil > 1:
                for c, rows in chunks:
                    natural[branch][0][rows, :] = accp[_chunk(c), :]
                    natural[branch][1][rows, :] = mlp[_chunk(c), :]

        for c in range(SEQ // PERM_CHUNK):
            packed = [ml[_chunk(c), :] for _, ml in natural]
            ms = [jnp.where(low, ml, pltpu.roll(ml, HEAD_DIM // 2, axis=1)) for ml in packed]
            ls = [jnp.where(low, pltpu.roll(ml, LANES - HEAD_DIM // 2, axis=1), ml) for ml in packed]
            m_all = jnp.maximum(jnp.maximum(ms[0], ms[1]), ms[2])
            es = [jnp.exp(m - m_all) for m in ms]
            z = ls[0] * es[0] + ls[1] * es[1] + ls[2] * es[2]
            num = natural[0][0][_chunk(c), :] * es[0] + natural[1][0][_chunk(c), :] * es[1] + natural[2][0][_chunk(c), :] * es[2]
            o_ref[_chunk(c), :] = num / z
            lse_ref[_chunk(c), :] = m_all + jnp.log(z)

    col = pl.BlockSpec((SEQ, LANES), lambda h: (0, h))
    padded = SEQ + 2 * HALF_WINDOW
    return _call(
        body, (q, k, v), grid=(ATTN_WIDTH // LANES,), in_specs=[col] * 3, out_specs=[col] * 2,
        out_shape=[_sds((SEQ, ATTN_WIDTH), F32)] * 2,
        scratch_shapes=[pltpu.VMEM((SEQ, LANES), F32)] * 3 + [pltpu.VMEM((SEQ, LANES), BF16)] * 2
        + [pltpu.VMEM((padded, LANES), BF16)] * 2 + [pltpu.VMEM((SEQ, LANES), F32)] * 8
        + [pltpu.VMEM((3, Q_BLOCK, K_WINDOW), F32)],
        vmem_mib=52, name="attn_fwd", comm=comm)


def _shifted(t, before, after, i):
    tile = t.shape[0]
    row = lax.broadcasted_iota(jnp.int32, (tile, 1), 0)
    before = jnp.where(i > 0, before, 0.0)
    after = jnp.where(i < SEQ // tile - 1, after, 0.0)
    return (jnp.where(row == 0, before, pltpu.roll(t, 1, axis=0)),
            jnp.where(row == tile - 1, after, pltpu.roll(t, tile - 1, axis=0)))


def _last_row(ref):
    return ref[HALO_ROWS - 1:HALO_ROWS, :].astype(F32)


def _first_row(ref):
    return ref[0:1, :].astype(F32)


def _conv_parts(u, c, u_prev, c_prev, u_next, c_next, cw, i):
    t = c * u
    t_prev, t_next = _shifted(t, _last_row(c_prev) * _last_row(u_prev), _first_row(c_next) * _first_row(u_next), i)
    s = cw[0:1, :] * t_prev + cw[1:2, :] * t + cw[2:3, :] * t_next
    return t, t_prev, t_next, s


def _mix_fwd(attn, u, b, c, conv_w, g_attn, g_conv, w_out, x, g_post):
    def body(a_ref, u_ref, b_ref, c_ref, up_ref, cp_ref, un_ref, cn_ref, cw_ref, ga_ref, gc_ref, w_ref, x_ref, gp_ref,
             x1_ref, mg_ref, mix_ref):
        i = pl.program_id(0)
        _, _, _, s = _conv_parts(u_ref[...].astype(F32), c_ref[...].astype(F32), up_ref, cp_ref, un_ref, cn_ref, cw_ref[...], i)
        ya, _ = _rms(a_ref[...])
        yc, _ = _rms(b_ref[...].astype(F32) * s)
        merged = jnp.concatenate([ya * ga_ref[...], yc * gc_ref[...]], axis=1).astype(BF16)
        mix = _dot(merged, w_ref[...])
        ym, _ = _rms(mix)
        mg_ref[...] = merged.T
        mix_ref[...] = mix.astype(BF16)
        x1_ref[...] = x_ref[...] + ym * gp_ref[...]

    a = ATTN_WIDTH
    return _call(
        body, (attn, u, b, c, u, c, u, c, conv_w, g_attn, g_conv, w_out, x, g_post), grid=(SEQ // ROW_TILE,),
        in_specs=[_rows(a)] * 4 + [_halo_prev(a)] * 2 + [_halo_next(a)] * 2
        + [_const((3, a)), _const((1, a)), _const((1, a)), _const1((D_MODEL, D_MODEL)), _rows(D_MODEL), _const((1, D_MODEL))],
        out_specs=[_rows(D_MODEL), _cols(D_MODEL, ROW_TILE), _rows(D_MODEL)],
        out_shape=[_sds((SEQ, D_MODEL), F32), _sds((D_MODEL, SEQ), BF16), _sds((SEQ, D_MODEL), BF16)],
        vmem_mib=40, name="mix_fwd")[0]


def _gu_spec():
    return pl.BlockSpec((N_DEV, FFN_TILE, FFN_BLOCK), lambda i: (0, i, 0))


def _ffn_fwd(x1, g_pre, w_gu, w_dn, g_post, comm=None, target=None):
    n_tiles = SEQ // ROW_TILE

    def body(*refs):
        if target is None:
            x_ref, g_ref, wgu_ref, wdn_ref, gp_ref, x2_ref, gu_ref, f_ref = refs
        else:
            x_ref, g_ref, wgu_ref, wdn_ref, gp_ref, t_ref, x2_ref, gu_ref, f_ref, loss_ref, acc = refs
        x1 = x_ref[...]
        y, _ = _rms(x1)
        h = (y * g_ref[...]).astype(BF16)
        f = jnp.zeros((ROW_TILE, D_MODEL), F32)
        for j in range(N_DEV // 2):
            gate = _dot_nt(h, wgu_ref[j])
            up = _dot_nt(h, wgu_ref[j + N_DEV // 2])
            gu_ref[j] = gate.astype(BF16)
            gu_ref[j + N_DEV // 2] = up.astype(BF16)
            act = (gate * jax.nn.sigmoid(gate) * up).astype(BF16)
            f = f + _dot(act, wdn_ref[pl.ds(j * FFN_BLOCK, FFN_BLOCK), :])
        yf, _ = _rms(f)
        f_ref[...] = f
        x2 = x1 + yf * gp_ref[...]
        if target is None:
            x2_ref[...] = x2
        else:
            i = pl.program_id(0)
            err = x2 - t_ref[...]
            x2_ref[...] = err * (1.0 / D_MODEL)

            @pl.when(i == 0)
            def _():
                acc[...] = jnp.zeros_like(acc)

            acc[...] += jnp.sum(err * err, axis=0, keepdims=True)

            @pl.when(i == n_tiles - 1)
            def _():
                loss_ref[...] = jnp.sum(acc[...], axis=1, keepdims=True) * (0.5 / D_MODEL)

    with_loss = target is not None
    return _call(
        body, (x1, g_pre, w_gu, w_dn, g_post) + ((target,) if with_loss else ()), grid=(n_tiles,),
        in_specs=[_rows(D_MODEL), _const((1, D_MODEL)), _const1((N_DEV, FFN_BLOCK, D_MODEL)), _const1((FFN_HIDDEN, D_MODEL)),
                  _const((1, D_MODEL))] + ([_rows(D_MODEL)] if with_loss else []),
        out_specs=[_rows(D_MODEL), pl.BlockSpec((N_DEV, ROW_TILE, FFN_BLOCK), lambda i: (0, i, 0)), _rows(D_MODEL)]
        + ([_const((1, 1))] if with_loss else []),
        out_shape=[_sds((SEQ, D_MODEL), F32), _sds((N_DEV, SEQ, FFN_BLOCK), BF16), _sds((SEQ, D_MODEL), F32)]
        + ([_sds((1, 1), F32)] if with_loss else []),
        scratch_shapes=[pltpu.VMEM((1, D_MODEL), F32)] if with_loss else [],
        vmem_mib=58, name="ffn_fwd_loss" if with_loss else "ffn_fwd", comm=comm)


def _accumulate(pairs, i):
    @pl.when(i == 0)
    def _():
        for ref, value in pairs:
            ref[...] = value

    @pl.when(i > 0)
    def _():
        for ref, value in pairs:
            ref[...] += value


def _colsum(v):
    return jnp.sum(v, axis=0, keepdims=True)


def _ffn_bwd(dx2, f, x1, gu, w_gu, w_dn, g_post, g_pre, comm=None):
    half = N_DEV // 2

    def body(dx2_ref, f_ref, x1_ref, gu_ref, wgu_ref, wdn_ref, gpost_ref, gpre_ref,
             dx1_ref, df_ref, act_ref, dgu_ref, h_ref, dgpost_ref, dgpre_ref):
        i = pl.program_id(0)
        dx2 = dx2_ref[...]
        yf, rf = _rms(f_ref[...])
        dg_post = _colsum(dx2 * yf)
        df = _rms_bwd(dx2 * gpost_ref[...], yf, rf).astype(BF16)
        df_ref[...] = df
        dh = jnp.zeros((FFN_TILE, D_MODEL), F32)
        for j in range(half):
            dact = _dot_nt(df, wdn_ref[pl.ds(j * FFN_BLOCK, FFN_BLOCK), :])
            gate = gu_ref[j].astype(F32)
            up = gu_ref[j + half].astype(F32)
            sig = jax.nn.sigmoid(gate)
            silu = gate * sig
            act_ref[j] = (silu * up).astype(BF16)
            dgate = (dact * up * (sig * (1.0 + gate * (1.0 - sig)))).astype(BF16)
            dup = (dact * silu).astype(BF16)
            dgu_ref[j] = dgate
            dgu_ref[j + half] = dup
            dh = dh + _dot(dgate, wgu_ref[j]) + _dot(dup, wgu_ref[j + half])
        y1, r1 = _rms(x1_ref[...])
        h_ref[...] = (y1 * gpre_ref[...]).astype(BF16)
        dx1_ref[...] = dx2 + _rms_bwd(dh * gpre_ref[...], y1, r1)
        _accumulate([(dgpost_ref, dg_post), (dgpre_ref, _colsum(dh * y1))], i)

    act_spec = pl.BlockSpec((half, FFN_TILE, FFN_BLOCK), lambda i: (0, i, 0))
    return _call(
        body, (dx2, f, x1, gu, w_gu, w_dn, g_post, g_pre), grid=(SEQ // FFN_TILE,),
        in_specs=[_frows(D_MODEL)] * 3 + [_gu_spec(), _const1((N_DEV, FFN_BLOCK, D_MODEL)), _const1((FFN_HIDDEN, D_MODEL)),
                                          _const((1, D_MODEL)), _const((1, D_MODEL))],
        out_specs=[_frows(D_MODEL), _frows(D_MODEL), act_spec, _gu_spec(), _frows(D_MODEL), _const((1, D_MODEL)), _const((1, D_MODEL))],
        out_shape=[_sds((SEQ, D_MODEL), F32), _sds((SEQ, D_MODEL), BF16), _sds((half, SEQ, FFN_BLOCK), BF16),
                   _sds((N_DEV, SEQ, FFN_BLOCK), BF16), _sds((SEQ, D_MODEL), BF16), _sds((1, D_MODEL), F32), _sds((1, D_MODEL), F32)],
        vmem_mib=52, name="ffn_bwd", comm=comm)


def _wgrad(a_t, g, n_blocks, out_shape, name, comm=None):
    def body(a_ref, g_ref, o_ref):
        o_ref[...] = _dot(a_ref[...], g_ref[...]).astype(BF16)

    res, landed = _call(body, (a_t, g), grid=(n_blocks,), in_specs=[_const1((D_MODEL, SEQ)), pl.BlockSpec((SEQ, 512), lambda j: (0, j))],
                        out_specs=[pl.BlockSpec((D_MODEL, 512), lambda j: (0, j))], out_shape=[_sds(out_shape, BF16)], vmem_mib=48,
                        name=name, comm=comm)
    return res[0], landed


def _wgrad_paired(a, g, kind):
    shard = SHARD_SHAPES[kind]
    per_chip = 2 if kind == "gu" else 1
    n_compute = per_chip * N_CHIPS
    n_steps = n_compute + 1
    specs = [pl.BlockSpec((1, SEQ, FFN_BLOCK), lambda t: (jnp.minimum(t, n_compute - 1), 0, 0)), _const1((SEQ, D_MODEL))]

    def body(a_ref, g_ref, o_ref, sendbuf, recvbuf, keepbuf, send_sem, recv_sem):
        t = pl.program_id(0)
        x, y, c = _place()

        def exchange(q):
            return pltpu.make_async_remote_copy(src_ref=sendbuf.at[q], dst_ref=recvbuf.at[q], send_sem=send_sem.at[q],
                                                recv_sem=recv_sem.at[q], device_id=(x, y, 1 - c), device_id_type=MESH)

        @pl.when((t >= per_chip) & (t % per_chip == 0))
        def _():
            q = t // per_chip - 1
            exchange(q).wait_recv()
            o_ref[0] = (keepbuf[...] + recvbuf[q].astype(F32)).astype(BF16)

        @pl.when(t < n_compute)
        def _():
            q = t // per_chip
            r = _dot_tn(a_ref[0], g_ref[...])
            if kind == "gu":
                @pl.when(t % 2 == c)
                def _():
                    keepbuf[...] = r

                @pl.when(t % 2 != c)
                def _():
                    sendbuf[q] = r.astype(BF16)
                    exchange(q).start()
            else:
                lower, upper = r[:W_DOWN_BLOCK], r[W_DOWN_BLOCK:]
                keepbuf[...] = jnp.where(c == 0, lower, upper)
                sendbuf[q] = jnp.where(c == 0, upper, lower).astype(BF16)
                exchange(q).start()

        @pl.when(t == n_steps - 1)
        def _():
            for q in range(N_CHIPS):
                exchange(q).wait_send()

    res, _ = _call(
        body, (a, g), grid=(n_steps,), in_specs=specs,
        out_specs=[pl.BlockSpec((1,) + shard, lambda t: (jnp.clip(t // per_chip - 1, 0, N_CHIPS - 1), 0, 0))],
        out_shape=[_sds((N_CHIPS,) + shard, BF16)],
        scratch_shapes=[pltpu.VMEM((N_CHIPS,) + shard, BF16), pltpu.VMEM((N_CHIPS,) + shard, BF16), pltpu.VMEM(shard, F32),
                        pltpu.SemaphoreType.DMA((N_CHIPS,)), pltpu.SemaphoreType.DMA((N_CHIPS,))],
        vmem_mib=52, name="wgrad_" + kind)
    return res[0]


def _mix_bwd(dx1, mix, attn, u, b, c, conv_w, g_attn, g_conv, g_post, w_out):
    def body(dx1_ref, mix_ref, a_ref, u_ref, b_ref, c_ref, up_ref, cp_ref, un_ref, cn_ref, cw_ref, ga_ref, gc_ref, gp_ref, w_ref,
             dmix_ref, da_ref, ds_ref, db_ref, dgp_ref, dga_ref, dgc_ref):
        i = pl.program_id(0)
        dx1 = dx1_ref[...]
        ym, rm = _rms(mix_ref[...].astype(F32))
        dg_post = _colsum(dx1 * ym)
        dmix = _rms_bwd(dx1 * gp_ref[...], ym, rm).astype(BF16)
        dmix_ref[...] = dmix
        dmerged = _dot_nt(dmix, w_ref[...])
        dna, dnc = dmerged[:, :ATTN_WIDTH], dmerged[:, ATTN_WIDTH:]
        ya, ra = _rms(a_ref[...])
        da_ref[...] = _rms_bwd(dna * ga_ref[...], ya, ra)
        _, _, _, s = _conv_parts(u_ref[...].astype(F32), c_ref[...].astype(F32), up_ref, cp_ref, un_ref, cn_ref, cw_ref[...], i)
        gate_b = b_ref[...].astype(F32)
        yc, rc = _rms(gate_b * s)
        dy = _rms_bwd(dnc * gc_ref[...], yc, rc)
        db_ref[...] = (dy * s).astype(BF16)
        ds_ref[...] = (dy * gate_b).astype(BF16)
        _accumulate([(dgp_ref, dg_post), (dga_ref, _colsum(dna * ya)), (dgc_ref, _colsum(dnc * yc))], i)

    a = ATTN_WIDTH
    return _call(
        body, (dx1, mix, attn, u, b, c, u, c, u, c, conv_w, g_attn, g_conv, g_post, w_out), grid=(SEQ // ROW_TILE,),
        in_specs=[_rows(D_MODEL)] * 2 + [_rows(a)] * 4 + [_halo_prev(a)] * 2 + [_halo_next(a)] * 2
        + [_const((3, a)), _const((1, a)), _const((1, a)), _const((1, D_MODEL)), _const1((D_MODEL, D_MODEL))],
        out_specs=[_rows(D_MODEL)] + [_rows(a)] * 3 + [_const((1, D_MODEL)), _const((1, a)), _const((1, a))],
        out_shape=[_sds((SEQ, D_MODEL), BF16), _sds((SEQ, a), F32), _sds((SEQ, a), BF16), _sds((SEQ, a), BF16),
                   _sds((1, D_MODEL), F32), _sds((1, a), F32), _sds((1, a), F32)],
        vmem_mib=40, name="mix_bwd")[0]


def _attn_bwd(q, k, v, do, o, lse, comm=None):
    group = 8

    def body(q_ref, k_ref, v_ref, do_ref, o_ref, lse_ref, dq_ref, dk_ref, dv_ref,
             q32, k32, v32, qa, qb, doa, dob, kp, vp, lsep, dlp, dnat, dqp, dkp, dvp, bias_ref):
        first, mask_a, mask_b = _head_masks()
        _zero_key_padding((kp, vp))
        _write_band_bias(bias_ref)
        q32[...] = q_ref[...].astype(F32)
        k32[...] = k_ref[...].astype(F32)
        v32[...] = v_ref[...].astype(F32)
        for c in range(SEQ // PERM_CHUNK):
            prod = do_ref[_chunk(c), :] * o_ref[_chunk(c), :]
            d_a = jnp.sum(prod * mask_a, axis=1, keepdims=True)
            d_b = jnp.sum(prod * mask_b, axis=1, keepdims=True)
            dnat[_chunk(c), :] = jnp.where(first, d_a, d_b)

        for step, dil in enumerate(DILATIONS[1:] + DILATIONS[:1]):
            length = SEQ // dil
            assert length >= 2 * Q_BLOCK
            chunks = _perm_chunks(dil)
            for c, rows in chunks:
                val = q32[rows, :]
                qa[_chunk(c), :] = (val * mask_a).astype(BF16)
                qb[_chunk(c), :] = (val * mask_b).astype(BF16)
                val = do_ref[rows, :]
                doa[_chunk(c), :] = (val * mask_a).astype(BF16)
                dob[_chunk(c), :] = (val * mask_b).astype(BF16)
                kp[_chunk(c, HALF_WINDOW), :] = k32[rows, :].astype(BF16)
                vp[_chunk(c, HALF_WINDOW), :] = v32[rows, :].astype(BF16)
                lsep[_chunk(c), :] = lse_ref[rows, :]
                dlp[_chunk(c), :] = dnat[rows, :]
            zero = jnp.zeros((PERM_CHUNK, LANES), F32)
            for c in range(SEQ // PERM_CHUNK):
                dkp[_chunk(c), :] = zero
                dvp[_chunk(c), :] = zero
            dkp[pl.ds(SEQ, 2 * HALF_WINDOW), :] = zero[:2 * HALF_WINDOW]
            dvp[pl.ds(SEQ, 2 * HALF_WINDOW), :] = zero[:2 * HALF_WINDOW]

            heads = ((qa, doa, 0), (qb, dob, HEAD_DIM))

            def blocks(i, carry, length=length):
                base = pl.multiple_of(i * (group * Q_BLOCK), group * Q_BLOCK)
                starts = [base + g * Q_BLOCK for g in range(group)]
                raw = [[(_dot_nt(qx[pl.ds(m0, Q_BLOCK), :], kp[pl.ds(m0, K_WINDOW), :]),
                         _dot_nt(dox[pl.ds(m0, Q_BLOCK), :], vp[pl.ds(m0, K_WINDOW), :])) for qx, dox, _ in heads]
                       for m0 in starts]
                grads = []
                for m0, pair in zip(starts, raw):
                    bias = bias_ref[_band_bias_index(m0, length)]
                    lse_b, d_b = lsep[pl.ds(m0, Q_BLOCK), :], dlp[pl.ds(m0, Q_BLOCK), :]
                    out = []
                    for (s, dp), (_, _, col) in zip(pair, heads):
                        p = jnp.exp(s + bias - lse_b[:, col:col + 1])
                        out.append(((p * (dp - d_b[:, col:col + 1])).astype(BF16), p.astype(BF16)))
                    grads.append(out)
                for m0, out in zip(starts, grads):
                    qrows, krows = pl.ds(m0, Q_BLOCK), pl.ds(m0, K_WINDOW)
                    kw = kp[krows, :]
                    dk = jnp.zeros((K_WINDOW, LANES), F32)
                    dv = jnp.zeros((K_WINDOW, LANES), F32)
                    for (ds, p), (qx, dox, _) in zip(out, heads):
                        dk = dk + _dot_tn(ds, qx[qrows, :])
                        dv = dv + _dot_tn(p, dox[qrows, :])
                    dqp[qrows, :] = jnp.where(first, _dot(out[0][0], kw), _dot(out[1][0], kw)) * (HEAD_DIM ** -0.5)
                    dkp[krows, :] += dk
                    dvp[krows, :] += dv
                return carry

            lax.fori_loop(0, SEQ // (group * Q_BLOCK), blocks, 0)

            for c, rows in chunks:
                g_q, g_k, g_v = dqp[_chunk(c), :], dkp[_chunk(c, HALF_WINDOW), :], dvp[_chunk(c, HALF_WINDOW), :]
                if step == 0:
                    dq_ref[rows, :] = g_q
                    dk_ref[rows, :] = g_k
                    dv_ref[rows, :] = g_v
                else:
                    dq_ref[rows, :] = dq_ref[rows, :] + g_q
                    dk_ref[rows, :] = dk_ref[rows, :] + g_k
                    dv_ref[rows, :] = dv_ref[rows, :] + g_v

    col = pl.BlockSpec((SEQ, LANES), lambda h: (0, h))
    col1 = pl.BlockSpec((SEQ, LANES), lambda h: (0, h), pipeline_mode=pl.Buffered(1))
    padded = SEQ + 2 * HALF_WINDOW
    return _call(
        body, (q, k, v, do, o, lse), grid=(ATTN_WIDTH // LANES,), in_specs=[col1] * 6, out_specs=[col] * 3,
        out_shape=[_sds((SEQ, ATTN_WIDTH), F32)] * 3,
        scratch_shapes=[pltpu.VMEM((SEQ, LANES), F32)] * 3 + [pltpu.VMEM((SEQ, LANES), BF16)] * 4
        + [pltpu.VMEM((padded, LANES), BF16)] * 2 + [pltpu.VMEM((SEQ, LANES), F32)] * 4 + [pltpu.VMEM((padded, LANES), F32)] * 2
        + [pltpu.VMEM((3, Q_BLOCK, K_WINDOW), F32)],
        vmem_mib=56, name="attn_bwd", comm=comm)


def _inproj_bwd(dq, dk, dv, ds, db, u, c, conv_w, tc, tsa, tsb, w_in, x, g_pre, dx1):
    def body(dq_ref, dk_ref, dv_ref, ds_ref, db_ref, u_ref, c_ref, dsp_ref, up_ref, cp_ref, dsn_ref, un_ref, cn_ref, cw_ref,
             tc_ref, tsa_ref, tsb_ref, w_ref, x_ref, g_ref, dx1_ref, dx_ref, dproj_ref, h_ref, dg_ref, dcw_ref):
        i = pl.program_id(0)
        cw = cw_ref[...]
        u, c = u_ref[...].astype(F32), c_ref[...].astype(F32)
        t, t_prev, t_next, _ = _conv_parts(u, c, up_ref, cp_ref, un_ref, cn_ref, cw, i)
        ds = ds_ref[...].astype(F32)
        ds_prev, ds_next = _shifted(ds, _last_row(dsp_ref), _first_row(dsn_ref), i)
        dt = cw[0:1, :] * ds_next + cw[1:2, :] * ds + cw[2:3, :] * ds_prev
        d_taps = jnp.concatenate([_colsum(ds * t_prev), _colsum(ds * t), _colsum(ds * t_next)], axis=0)
        tc_, tsa_, tsb_ = tc_ref[...], tsa_ref[...], tsb_ref[...]
        groups = ((2, lambda: dv_ref[...].astype(BF16)), (4, lambda: db_ref[...]),
                  (0, lambda: _rotate_transposed(dq_ref[...], tc_, tsa_, tsb_).astype(BF16)),
                  (1, lambda: _rotate_transposed(dk_ref[...], tc_, tsa_, tsb_).astype(BF16)),
                  (3, lambda: (dt * c).astype(BF16)), (5, lambda: (dt * u).astype(BF16)))
        dh = jnp.zeros((x_ref.shape[0], D_MODEL), F32)
        for n, make in groups:
            cols = pl.ds(n * a, a)
            part = make()
            dproj_ref[:, cols] = part
            dh = dh + _dot_nt(part, w_ref[:, cols])
        y, r = _rms(x_ref[...])
        h_ref[...] = (y * g_ref[...]).astype(BF16).T
        dx_ref[...] = dx1_ref[...] + _rms_bwd(dh * g_ref[...], y, r)
        _accumulate([(dg_ref, _colsum(dh * y)), (dcw_ref, d_taps)], i)

    a = ATTN_WIDTH
    tile = FFN_TILE

    def rows(width):
        return _rows(width, tile)

    return _call(
        body, (dq, dk, dv, ds, db, u, c, ds, u, c, ds, u, c, conv_w, tc, tsa, tsb, w_in, x, g_pre, dx1), grid=(SEQ // tile,),
        in_specs=[rows(a)] * 7 + [_halo_prev(a, tile)] * 3 + [_halo_next(a, tile)] * 3
        + [_const((3, a)), rows(LANES), rows(LANES), rows(LANES), _const1((D_MODEL, IN_PROJ_WIDTH)), rows(D_MODEL),
           _const((1, D_MODEL)), rows(D_MODEL)],
        out_specs=[rows(D_MODEL), rows(IN_PROJ_WIDTH), _cols(D_MODEL, tile), _const((1, D_MODEL)), _const((3, a))],
        out_shape=[_sds((SEQ, D_MODEL), F32), _sds((SEQ, IN_PROJ_WIDTH), BF16), _sds((D_MODEL, SEQ), BF16),
                   _sds((1, D_MODEL), F32), _sds((3, a), F32)],
        vmem_mib=48, name="inproj_bwd")


def _adamw_math(w, g, m, v):
    m = ADAM_B1 * m + (1.0 - ADAM_B1) * g
    v = ADAM_B2 * v + (1.0 - ADAM_B2) * (g * g)
    m_hat = m / (1.0 - ADAM_B1 ** ADAM_STEP)
    v_hat = v / (1.0 - ADAM_B2 ** ADAM_STEP)
    delta = -ADAM_LR * (m_hat / (jnp.sqrt(v_hat) + ADAM_EPS) + ADAM_WD * w)
    return delta, m, v


def _sum_parts(p_ref):
    g = p_ref[0].astype(F32)
    for k in range(1, p_ref.shape[0]):
        g = g + p_ref[k].astype(F32)
    return g


def _adamw_layers(parts, w, m, v, row_tile, name, comm=None):
    _, rows, cols = w.shape
    n_tiles = rows // row_tile
    senders = parts[0].shape[0]

    def body(*refs):
        p_refs = refs[:DEPTH]
        w_ref, m_ref, v_ref, g_ref, d_ref, nm_ref, nv_ref = refs[DEPTH:]
        layer = pl.program_id(0)
        for l, p_ref in enumerate(p_refs):
            @pl.when(layer == l)
            def _(p_ref=p_ref):
                g = _sum_parts(p_ref)
                g_ref[0] = g
                d_ref[0], nm_ref[0], nv_ref[0] = _adamw_math(w_ref[0], g, m_ref[0], v_ref[0])

    def part_spec(l):
        return pl.BlockSpec((senders, row_tile, cols),
                            lambda layer, i: (0, jnp.where(layer == l, i, jnp.where(layer < l, 0, n_tiles - 1)), 0))

    tile = pl.BlockSpec((1, row_tile, cols), lambda layer, i: (layer, i, 0))
    return _call(body, (*parts, w, m, v), grid=(DEPTH, n_tiles), in_specs=[part_spec(l) for l in range(DEPTH)] + [tile] * 3,
                 out_specs=[tile] * 4, out_shape=[_sds(w.shape, F32)] * 4, name=name, comm=comm)


def _adamw_sum8(parts, w, m, v, name):
    def body(p_ref, w_ref, m_ref, v_ref, g_ref, d_ref, nm_ref, nv_ref):
        g = _sum_parts(p_ref)
        g_ref[...] = g
        d_ref[...], nm_ref[...], nv_ref[...] = _adamw_math(w_ref[...], g, m_ref[...], v_ref[...])

    return pl.pallas_call(body, out_shape=[_sds(w.shape, F32)] * 4, name=name)(parts, w, m, v)


def _adamw_plain(g, w, m, v, name):
    def body(g_ref, w_ref, m_ref, v_ref, d_ref, nm_ref, nv_ref):
        d_ref[...], nm_ref[...], nv_ref[...] = _adamw_math(w_ref[...], g_ref[...], m_ref[...], v_ref[...])

    return pl.pallas_call(body, out_shape=[_sds(w.shape, F32)] * 3, name=name)(g, w, m, v)


SMALL_ROWS = 8
GAIN_NAMES = ("pre_mix_norm", "post_mix_norm", "pre_ffn_norm", "post_ffn_norm", "attn_out_norm", "conv_out_norm")


def _pack_small(pre_mix, post_mix, pre_ffn, post_ffn, attn_out, conv_out, taps):
    zeros = jnp.zeros((1, ATTN_WIDTH), F32)
    return jnp.concatenate([
        pre_mix, post_mix, pre_ffn, post_ffn, jnp.concatenate([attn_out, conv_out], axis=1),
        jnp.concatenate([taps[0:1], taps[1:2]], axis=1), jnp.concatenate([taps[2:3], zeros], axis=1),
        jnp.zeros((1, D_MODEL), F32)], axis=0)


def _rope(positions):
    inv_freq = ROPE_THETA ** (-jnp.arange(0, ROPE_DIM, 2, dtype=F32) / ROPE_DIM)
    lane = jnp.arange(LANES) % HEAD_DIM
    freq_row = jnp.where(lane < ROPE_DIM, inv_freq[lane % (ROPE_DIM // 2)], 0.0).reshape(1, LANES).astype(F32)
    return _rope_tables(positions.reshape(SEQ, 1), freq_row)


def _layer_forward(h, gains, taps, tables, w, inproj_comm=None, attn_comm=None, ffn_comm=None, target=None):
    (q, k, v, u, b, c), landed = _inproj_fwd(h, gains["pre_mix_norm"], w["in"], *tables, comm=inproj_comm)
    if inproj_comm is not None:
        w = {**w, **dict(zip(inproj_comm.kinds, landed))}
    (attn, lse), landed = _attn_fwd(q, k, v, comm=attn_comm)
    if attn_comm is not None:
        w = {**w, **dict(zip(attn_comm.kinds, landed))}
    x1, merged, mix = _mix_fwd(attn, u, b, c, taps, gains["attn_out_norm"], gains["conv_out_norm"], w["out"], h, gains["post_mix_norm"])
    (x2, gu, f, *loss), landed_next = _ffn_fwd(x1, gains["pre_ffn_norm"], w["gu"], w["dn"], gains["post_ffn_norm"], comm=ffn_comm,
                                               target=target)
    out = x2 if target is None else (x2, loss[0])
    return out, (h, q, k, v, u, b, c, attn, lse, merged, mix, x1, gu, f), w, landed_next


class _GradientExchange:
    def ffn_grads(self, act, df, h2, dgu):
        return dict(dn=_wgrad_paired(act, df, "dn"), gu=_wgrad_paired(dgu, h2, "gu"))

    def attention_passenger(self, ffn):
        return _ScatterChips(("gu", "dn"), (ffn["gu"], ffn["dn"]))

    def w_out_grad(self, merged_t, dmix):
        return _wgrad(merged_t, dmix, D_MODEL // 512, (D_MODEL, D_MODEL), "wgrad_out")[0]

    def w_in_grad(self, h_t, dproj, g_out):
        g_in, landed = _wgrad(h_t, dproj, IN_PROJ_WIDTH // 512, (D_MODEL, IN_PROJ_WIDTH), "wgrad_in", comm=_Scatter(("out",), (g_out,)))
        return g_in, landed[0]


def _layer_backward(dx, saved, gains, taps, tables, w, ffn_comm, exchange):
    x0, q, k, v, u, b, c, attn, lse, merged, mix, x1, gu, f = saved
    (dx1, df, act, dgu, h2, dg_post_ffn, dg_pre_ffn), landed_prev = _ffn_bwd(
        dx, f, x1, gu, w["gu"], w["dn"], gains["post_ffn_norm"], gains["pre_ffn_norm"], comm=ffn_comm)
    ffn = exchange.ffn_grads(act, df, h2, dgu)
    dmix, dattn, ds, db, dg_post_mix, dg_attn, dg_conv = _mix_bwd(
        dx1, mix, attn, u, b, c, taps, gains["attn_out_norm"], gains["conv_out_norm"], gains["post_mix_norm"], w["out"])
    g_out = exchange.w_out_grad(merged, dmix)
    attn_comm = exchange.attention_passenger(ffn)
    (dq, dk, dv), landed_ffn = _attn_bwd(q, k, v, dattn, attn, lse, comm=attn_comm)
    dx0, dproj, h1, dg_pre_mix, dtaps = _inproj_bwd(
        dq, dk, dv, ds, db, u, c, taps, *tables, w["in"], x0, gains["pre_mix_norm"], dx1)[0]
    g_in, out_parts = exchange.w_in_grad(h1, dproj, g_out)
    small = _pack_small(dg_pre_mix, dg_post_mix, dg_pre_ffn, dg_post_ffn, dg_attn, dg_conv, dtaps)
    rest = dict(zip(attn_comm.kinds, landed_ffn)) if attn_comm is not None else dict(ffn)
    rest["out"] = out_parts
    return dx0, g_in, small, landed_prev, rest


def kernel(x, positions, pre_mix_norm, w_in, conv_w, attn_out_norm, conv_out_norm, w_out, post_mix_norm, pre_ffn_norm, w_gate_up, w_down, post_ffn_norm, loss_target, m_pre_mix_norm, m_w_in, m_conv_w, m_attn_out_norm, m_conv_out_norm, m_w_out, m_post_mix_norm, m_pre_ffn_norm, m_w_gate_up, m_w_down, m_post_ffn_norm, v_pre_mix_norm, v_w_in, v_conv_w, v_attn_out_norm, v_conv_out_norm, v_w_out, v_post_mix_norm, v_pre_ffn_norm, v_w_gate_up, v_w_down, v_post_ffn_norm):
    mx, my, mc = _place()
    me = _block_of(mx, my, mc)
    conv_channels = conv_w.shape[-1]

    taps_flat = jnp.pad(conv_w.reshape(-1), (0, 8 * LANES - conv_w.size)).reshape(8, LANES)
    taps_all = _allgather_small(taps_flat, "allgather_taps").reshape(N_DEV, 8 * LANES)[:, :conv_w.size]
    conv_w_full = taps_all.reshape(N_DEV, DEPTH, 3, conv_channels).transpose(1, 2, 0, 3).reshape(DEPTH, 3, ATTN_WIDTH)

    def hidden_major(a):
        return jnp.swapaxes(a, 1, 2)

    big_w = dict(zip(WEIGHT_KINDS, (w_in, w_out, hidden_major(w_gate_up), w_down)))
    big_m = dict(zip(WEIGHT_KINDS, (m_w_in, m_w_out, hidden_major(m_w_gate_up), m_w_down)))
    big_v = dict(zip(WEIGHT_KINDS, (v_w_in, v_w_out, hidden_major(v_w_gate_up), v_w_down)))
    shards = {kind: big_w[kind].astype(BF16) for kind in WEIGHT_KINDS}
    all_gains = dict(pre_mix_norm=pre_mix_norm, attn_out_norm=attn_out_norm, conv_out_norm=conv_out_norm, post_mix_norm=post_mix_norm,
                     pre_ffn_norm=pre_ffn_norm, post_ffn_norm=post_ffn_norm)

    def gather(kinds, l):
        return _Gather(kinds, [shards[kind][l] for kind in kinds])

    def gains(l):
        return {name: g[l:l + 1] for name, g in all_gains.items()}

    tables = _rope(positions)

    early = ("in", "out", "dn")
    weights = [dict(zip(early[:1], _comm_only(gather(early[:1], 0), "allgather_first")))] + [None] * (DEPTH - 1)
    saved = [None] * DEPTH
    h = x[0]
    for l in range(DEPTH):
        inproj_comm = gather(early[1:], 0) if l == 0 else None
        ffn_comm = gather(early, l + 1) if l + 1 < DEPTH else None
        target = loss_target[0] if l + 1 == DEPTH else None
        h, saved[l], weights[l], landed = _layer_forward(h, gains(l), conv_w_full[l], tables, weights[l], inproj_comm,
                                                         gather(("gu",), l), ffn_comm, target)
        if ffn_comm is not None:
            weights[l + 1] = dict(zip(ffn_comm.kinds, landed))
    dx, local_loss = h

    parts = {kind: [None] * DEPTH for kind in WEIGHT_KINDS}
    small_grads = [None] * DEPTH
    g_in_above = None
    exchange = _GradientExchange()
    for l in reversed(range(DEPTH)):
        ffn_comm = _Scatter(("in",), (g_in_above,)) if g_in_above is not None else None
        dx, g_in_above, small_grads[l], landed, rest = _layer_backward(dx, saved[l], gains(l), conv_w_full[l], tables, weights[l], ffn_comm,
                                                                      exchange)
        if ffn_comm is not None:
            parts["in"][l + 1] = landed[0]
        for kind, part in rest.items():
            parts[kind][l] = part

    parts["in"][0] = _comm_only(_Scatter(("in",), (g_in_above,)), "scatter_last")[0]
    tiles = {"in": 256, "out": 128, "gu": 176, "dn": 176}
    big = {kind: _adamw_layers(parts[kind], big_w[kind], big_m[kind], big_v[kind], tiles[kind], "adamw_" + kind)[0]
           for kind in WEIGHT_KINDS}

    packed = jnp.concatenate(small_grads, axis=0)
    packed = lax.dynamic_update_slice(packed, local_loss, (SMALL_ROWS - 1, 0))
    gathered = _allgather_small(packed, "allgather_small_grads")

    def pack_state(state):
        rows = [_pack_small(*[state[name][l:l + 1] for name in GAIN_NAMES], jnp.zeros((3, ATTN_WIDTH), F32)) for l in range(DEPTH)]
        return jnp.concatenate(rows, axis=0)

    sw = pack_state(all_gains)
    sm = pack_state(dict(zip(GAIN_NAMES, (m_pre_mix_norm, m_post_mix_norm, m_pre_ffn_norm, m_post_ffn_norm, m_attn_out_norm, m_conv_out_norm))))
    sv = pack_state(dict(zip(GAIN_NAMES, (v_pre_mix_norm, v_post_mix_norm, v_pre_ffn_norm, v_post_ffn_norm, v_attn_out_norm, v_conv_out_norm))))
    sg, sd, snm, snv = _adamw_sum8(gathered, sw, sm, sv, "adamw_small")

    def unpack(p):
        p = p.reshape(DEPTH, SMALL_ROWS, D_MODEL)
        return dict(pre_mix_norm=p[:, 0], post_mix_norm=p[:, 1], pre_ffn_norm=p[:, 2], post_ffn_norm=p[:, 3],
                    attn_out_norm=p[:, 4, :ATTN_WIDTH], conv_out_norm=p[:, 4, ATTN_WIDTH:])

    small = [unpack(p) for p in (sg, sd, snm, snv)]
    loss = sg[SMALL_ROWS - 1, 0]
    sg3 = sg.reshape(DEPTH, SMALL_ROWS, D_MODEL)
    taps_grad_full = jnp.stack([sg3[:, 5, :ATTN_WIDTH], sg3[:, 5, ATTN_WIDTH:], sg3[:, 6, :ATTN_WIDTH]], axis=1)
    taps_grad = lax.dynamic_slice_in_dim(taps_grad_full, me * conv_channels, conv_channels, axis=2)

    def flat(a):
        return a.reshape(DEPTH * 3, conv_channels)

    td, tnm, tnv = _adamw_plain(flat(taps_grad), flat(conv_w), flat(m_conv_w), flat(v_conv_w), "adamw_taps")
    taps = [taps_grad] + [a.reshape(conv_w.shape) for a in (td, tnm, tnv)]

    def leaves(o):
        s = small[o]
        return (s["pre_mix_norm"], big["in"][o], taps[o], s["attn_out_norm"], s["conv_out_norm"], big["out"][o], s["post_mix_norm"],
                s["pre_ffn_norm"], hidden_major(big["gu"][o]), big["dn"][o], s["post_ffn_norm"])

    return (loss, dx[None], *leaves(0), *leaves(1), *leaves(2), *leaves(3))
```

```python
import math

import jax
import jax.numpy as jnp
from jax import lax
from jax.experimental import pallas as pl
from jax.experimental.pallas import tpu as pltpu

F32 = jnp.float32
BF16 = jnp.bfloat16
MESH = pl.DeviceIdType.MESH

SEQ = 4096
D_MODEL = 1024
DEPTH = 4
N_DEV = 8
ATTN_WIDTH = 512
IN_PROJ_WIDTH = 3072
FFN_HIDDEN = 2816
FFN_BLOCK = 2 * FFN_HIDDEN // N_DEV
W_IN_BLOCK = IN_PROJ_WIDTH // N_DEV
W_OUT_BLOCK = D_MODEL // N_DEV
W_DOWN_BLOCK = FFN_HIDDEN // N_DEV
HEAD_DIM = 64
ROPE_DIM = 16
ROPE_THETA = 500000.0
DILATIONS = (1, 4, 16)
HALF_WINDOW = 64
RMS_EPS = 1e-6
NEG_INF = -1e30
LANES = 128
Q_BLOCK = 128
K_WINDOW = Q_BLOCK + 2 * HALF_WINDOW
PERM_CHUNK = 256
ROW_TILE = 512
FFN_TILE = 256
WGRAD_TILE = SEQ
ADAM_LR, ADAM_B1, ADAM_B2, ADAM_EPS, ADAM_WD, ADAM_STEP = 0.001, 0.9, 0.999, 1e-08, 0.01, 10
MIB = 1024 * 1024
PINNED_BYTES = 256 * 1024

WEIGHT_KINDS = ("in", "out", "gu", "dn")
FULL_SHAPES = {"in": (D_MODEL, IN_PROJ_WIDTH), "out": (D_MODEL, D_MODEL), "gu": (N_DEV, FFN_BLOCK, D_MODEL), "dn": (FFN_HIDDEN, D_MODEL)}
SHARD_SHAPES = {"in": (D_MODEL, W_IN_BLOCK), "out": (W_OUT_BLOCK, D_MODEL), "gu": (FFN_BLOCK, D_MODEL), "dn": (W_DOWN_BLOCK, D_MODEL)}
ANY = pl.BlockSpec(memory_space=pl.ANY)


def _sds(shape, dtype):
    return jax.ShapeDtypeStruct(shape, dtype)


def _rows(width, tile=ROW_TILE):
    return pl.BlockSpec((tile, width), lambda i: (i, 0))


def _frows(width):
    return _rows(width, FFN_TILE)


def _cols(height, tile):
    return pl.BlockSpec((height, tile), lambda i: (0, i))


def _const(shape):
    return pl.BlockSpec(shape, lambda i: (0,) * len(shape))


def _const1(shape):
    return pl.BlockSpec(shape, lambda i: (0,) * len(shape), pipeline_mode=pl.Buffered(1))


HALO_ROWS = 16


def _halo_prev(width, tile=ROW_TILE):
    return pl.BlockSpec((HALO_ROWS, width), lambda i: (jnp.maximum(i * (tile // HALO_ROWS) - 1, 0), 0))


def _halo_next(width, tile=ROW_TILE):
    return pl.BlockSpec((HALO_ROWS, width), lambda i: (jnp.minimum((i + 1) * (tile // HALO_ROWS), SEQ // HALO_ROWS - 1), 0))


def _rms(x):
    r = lax.rsqrt(jnp.mean(x * x, axis=-1, keepdims=True) + RMS_EPS)
    return x * r, r


def _rms_bwd(dn, y, r):
    return r * (dn - y * jnp.mean(dn * y, axis=-1, keepdims=True))


def _dot(a, b):
    return jnp.dot(a, b, preferred_element_type=F32)


def _dot_nt(a, b):
    return lax.dot_general(a, b, (((1,), (1,)), ((), ())), preferred_element_type=F32)


def _dot_tn(a, b):
    return lax.dot_general(a, b, (((0,), (0,)), ((), ())), preferred_element_type=F32)


def _place():
    return lax.axis_index("x"), lax.axis_index("y"), lax.axis_index("c")


def _block_of(px, py, pc):
    return 4 * px + 2 * py + pc


def _weight_block(ref, kind, blk):
    if kind == "in":
        return ref.at[:, pl.ds(blk * W_IN_BLOCK, W_IN_BLOCK)]
    if kind == "out":
        return ref.at[pl.ds(blk * W_OUT_BLOCK, W_OUT_BLOCK), :]
    if kind == "gu":
        return ref.at[blk]
    return ref.at[pl.ds(blk * W_DOWN_BLOCK, W_DOWN_BLOCK), :]


def _dma_semaphores(n):
    return [pltpu.SemaphoreType.DMA((n, 7)), pltpu.SemaphoreType.DMA((n, 7)), pltpu.SemaphoreType.DMA((n,))]


class _Gather:
    def __init__(self, kinds, shards):
        self.kinds, self.operands = tuple(kinds), list(shards)
        self.tag = "gather_" + "_".join(kinds)
        self.out_shape = [_sds(FULL_SHAPES[k], BF16) for k in kinds]
        self.scratch = _dma_semaphores(len(kinds))

    def _parties(self):
        x, y, c = _place()
        return (x, y, c), (x, y, 1 - c), [(1 - x, y), (x, 1 - y), (1 - x, 1 - y)], c

    def _copy(self, outs, sems, w, k, block, to, src=None):
        dst = _weight_block(outs[w], self.kinds[w], _block_of(*block))
        return pltpu.make_async_remote_copy(src_ref=dst if src is None else src, dst_ref=dst, send_sem=sems[0].at[w, k],
                                            recv_sem=sems[1].at[w, k], device_id=to, device_id_type=MESH)

    def _own(self, srcs, outs, sems, w, me):
        return pltpu.make_async_copy(srcs[w], _weight_block(outs[w], self.kinds[w], _block_of(*me)), sems[2].at[w])

    def _first(self, srcs, outs, sems, w):
        me, sibling, chips, c = self._parties()
        return [self._copy(outs, sems, w, 0, me, sibling, src=srcs[w])] + [
            self._copy(outs, sems, w, 1 + j, me, (*chip, c), src=srcs[w]) for j, chip in enumerate(chips)]

    def start(self, srcs, outs, sems):
        me = self._parties()[0]
        for w in range(len(self.kinds)):
            self._own(srcs, outs, sems, w, me).start()
            for cp in self._first(srcs, outs, sems, w):
                cp.start()

    def forward(self, srcs, outs, sems):
        me, sibling, chips, c = self._parties()
        for j, chip in enumerate(chips):
            for w in range(len(self.kinds)):
                self._copy(outs, sems, w, 1 + j, (*chip, c), me).wait_recv()
                self._copy(outs, sems, w, 4 + j, (*chip, c), sibling).start()

    def finish(self, srcs, outs, sems):
        me, sibling, chips, c = self._parties()
        for w in range(len(self.kinds)):
            self._copy(outs, sems, w, 0, sibling, me).wait_recv()
            for j, chip in enumerate(chips):
                self._copy(outs, sems, w, 4 + j, (*chip, 1 - c), me).wait_recv()
        for w in range(len(self.kinds)):
            for cp in self._first(srcs, outs, sems, w):
                cp.wait_send()
            for j, chip in enumerate(chips):
                self._copy(outs, sems, w, 4 + j, (*chip, c), sibling).wait_send()
            self._own(srcs, outs, sems, w, me).wait()


def _peers(x, y, c):
    return [(x ^ a, y ^ b, c ^ e) for a in (0, 1) for b in (0, 1) for e in (0, 1) if (a, b, e) != (0, 0, 0)]


class _Scatter:
    def __init__(self, kinds, grads):
        self.kinds, self.operands = tuple(kinds), list(grads)
        self.tag = "scatter_" + "_".join(kinds)
        self.out_shape = [_sds((N_DEV,) + SHARD_SHAPES[k], BF16) for k in kinds]
        self.scratch = _dma_semaphores(len(kinds))

    def _copies(self, srcs, outs, sems):
        x, y, c = _place()
        me = _block_of(x, y, c)
        copies = []
        for w, kind in enumerate(self.kinds):
            copies.append(pltpu.make_async_copy(_weight_block(srcs[w], kind, me), outs[w].at[me], sems[2].at[w]))
            for k, peer in enumerate(_peers(x, y, c)):
                copies.append(pltpu.make_async_remote_copy(
                    src_ref=_weight_block(srcs[w], kind, _block_of(*peer)), dst_ref=outs[w].at[me],
                    send_sem=sems[0].at[w, k], recv_sem=sems[1].at[w, k], device_id=peer, device_id_type=MESH))
        return copies

    def start(self, srcs, outs, sems):
        for cp in self._copies(srcs, outs, sems):
            cp.start()

    def forward(self, srcs, outs, sems):
        pass

    def finish(self, srcs, outs, sems):
        for cp in self._copies(srcs, outs, sems):
            cp.wait()


N_CHIPS = N_DEV // 2


class _ScatterChips:
    def __init__(self, kinds, blocks):
        self.kinds, self.operands = tuple(kinds), list(blocks)
        self.tag = "scatter_chips_" + "_".join(kinds)
        self.out_shape = [_sds((N_CHIPS,) + SHARD_SHAPES[k], BF16) for k in kinds]
        self.scratch = _dma_semaphores(len(kinds))

    def _copies(self, srcs, outs, sems):
        x, y, c = _place()
        mine = 2 * x + y
        copies = []
        for w in range(len(self.kinds)):
            copies.append(pltpu.make_async_copy(srcs[w].at[mine], outs[w].at[mine], sems[2].at[w]))
            for j, (px, py) in enumerate([(1 - x, y), (x, 1 - y), (1 - x, 1 - y)]):
                copies.append(pltpu.make_async_remote_copy(
                    src_ref=srcs[w].at[2 * px + py], dst_ref=outs[w].at[mine], send_sem=sems[0].at[w, j], recv_sem=sems[1].at[w, j],
                    device_id=(px, py, c), device_id_type=MESH))
        return copies

    def start(self, srcs, outs, sems):
        for cp in self._copies(srcs, outs, sems):
            cp.start()

    def forward(self, srcs, outs, sems):
        pass

    def finish(self, srcs, outs, sems):
        for cp in self._copies(srcs, outs, sems):
            cp.wait()


def _in_hbm(a):
    return pltpu.with_memory_space_constraint(a, pltpu.HBM) if a.size * a.dtype.itemsize >= PINNED_BYTES else a


def _out_hbm(s):
    return pltpu.HBM(s.shape, s.dtype) if math.prod(s.shape) * jnp.dtype(s.dtype).itemsize >= PINNED_BYTES else s


def _call(body, args, *, grid, in_specs, out_specs, out_shape, scratch_shapes=(), vmem_mib=None, name, comm=None):
    kwargs = {} if vmem_mib is None else dict(compiler_params=pltpu.CompilerParams(vmem_limit_bytes=vmem_mib * MIB))
    in_specs, out_specs, out_shape, scratch_shapes = list(in_specs), list(out_specs), list(out_shape), list(scratch_shapes)
    args = [_in_hbm(a) for a in args]
    out_shape = [_out_hbm(s) for s in out_shape]
    if comm is None:
        res = pl.pallas_call(body, grid=grid, in_specs=in_specs, out_specs=out_specs, out_shape=out_shape,
                             scratch_shapes=scratch_shapes, name=name, **kwargs)(*args)
        return list(res), None
    n_in, n_out, n_scr = len(in_specs), len(out_specs), len(scratch_shapes)
    c_in, c_out = len(comm.operands), len(comm.out_shape)
    last = math.prod(grid) - 1

    def carried(*refs):
        cuts = [n_in, c_in, n_out, c_out, n_scr]
        parts, at = [], 0
        for n in cuts:
            parts.append(refs[at:at + n])
            at += n
        ins, c_ins, outs, c_outs, scr = parts
        sems = refs[at:]
        step = pl.program_id(0)
        for axis in range(1, len(grid)):
            step = step * grid[axis] + pl.program_id(axis)

        @pl.when(step == 0)
        def _():
            comm.start(c_ins, c_outs, sems)

        @pl.when(step == last)
        def _():
            comm.forward(c_ins, c_outs, sems)

        body(*ins, *outs, *scr)

        @pl.when(step == last)
        def _():
            comm.finish(c_ins, c_outs, sems)

    res = pl.pallas_call(carried, grid=grid, in_specs=in_specs + [ANY] * c_in, out_specs=out_specs + [ANY] * c_out,
                         out_shape=out_shape + [_out_hbm(s) for s in comm.out_shape], scratch_shapes=scratch_shapes + comm.scratch,
                         name=name + "_" + comm.tag, **kwargs)(*args, *[_in_hbm(a) for a in comm.operands])
    return list(res[:n_out]), list(res[n_out:])


def _comm_only(comm, name):
    def body(*refs):
        n_in, n_out = len(comm.operands), len(comm.out_shape)
        srcs, outs, sems = refs[:n_in], refs[n_in:n_in + n_out], refs[n_in + n_out:]
        comm.start(srcs, outs, sems)
        comm.forward(srcs, outs, sems)
        comm.finish(srcs, outs, sems)

    return pl.pallas_call(body, in_specs=[ANY] * len(comm.operands), out_specs=[ANY] * len(comm.out_shape),
                          out_shape=[_out_hbm(s) for s in comm.out_shape], scratch_shapes=comm.scratch,
                          name=name)(*[_in_hbm(a) for a in comm.operands])


def _allgather_small(v, name):
    def body(v_ref, o_ref, send, recv, local):
        x, y, c = _place()
        me = _block_of(x, y, c)
        mine = pltpu.make_async_copy(v_ref, o_ref.at[me], local)
        mine.start()
        copies = [mine]
        for k, peer in enumerate(_peers(x, y, c)):
            cp = pltpu.make_async_remote_copy(src_ref=v_ref, dst_ref=o_ref.at[me], send_sem=send.at[k], recv_sem=recv.at[k],
                                              device_id=peer, device_id_type=MESH)
            cp.start()
            copies.append(cp)
        for cp in copies:
            cp.wait()

    vm = pl.BlockSpec(memory_space=pltpu.VMEM)
    return pl.pallas_call(
        body, in_specs=[vm], out_specs=vm, out_shape=_sds((N_DEV,) + v.shape, F32),
        scratch_shapes=[pltpu.SemaphoreType.DMA((7,)), pltpu.SemaphoreType.DMA((7,)), pltpu.SemaphoreType.DMA],
        name=name)(v)


def _rope_tables(pos_col, freq_row, comm=None):
    def body(p_ref, f_ref, c_ref, sa_ref, sb_ref):
        ang = p_ref[...].astype(F32) * f_ref[...]
        lane = lax.broadcasted_iota(jnp.int32, ang.shape, 1) % HEAD_DIM
        cos, sin = jnp.cos(ang), jnp.sin(ang)
        c_ref[...] = jnp.where(lane < ROPE_DIM, cos, 1.0)
        sa_ref[...] = jnp.where(lane < ROPE_DIM // 2, -sin, 0.0)
        sb_ref[...] = jnp.where((lane >= ROPE_DIM // 2) & (lane < ROPE_DIM), sin, 0.0)

    return _call(body, (pos_col, freq_row), grid=(SEQ // ROW_TILE,), in_specs=[_rows(1), _const((1, LANES))],
                 out_specs=[_rows(LANES)] * 3, out_shape=[_sds((SEQ, LANES), F32)] * 3, name="rope_tables", comm=comm)


def _rotate(t, c, sa, sb):
    parts = []
    for g in range(ATTN_WIDTH // LANES):
        tg = t[:, g * LANES:(g + 1) * LANES]
        parts.append(tg * c + pltpu.roll(tg, LANES - 8, axis=1) * sa + pltpu.roll(tg, 8, axis=1) * sb)
    return jnp.concatenate(parts, axis=1)


def _rotate_transposed(dt, c, sa, sb):
    parts = []
    for g in range(ATTN_WIDTH // LANES):
        dg = dt[:, g * LANES:(g + 1) * LANES]
        parts.append(dg * c + pltpu.roll(dg * sa, 8, axis=1) + pltpu.roll(dg * sb, LANES - 8, axis=1))
    return jnp.concatenate(parts, axis=1)


def _inproj_fwd(x, g_pre, w_in, tc, tsa, tsb, comm=None):
    def body(x_ref, g_ref, w_ref, c_ref, sa_ref, sb_ref, q_ref, k_ref, v_ref, u_ref, b_ref, cc_ref):
        y, _ = _rms(x_ref[...])
        h = (y * g_ref[...]).astype(BF16)

        def proj(n):
            return _dot(h, w_ref[:, n * ATTN_WIDTH:(n + 1) * ATTN_WIDTH])

        c, sa, sb = c_ref[...], sa_ref[...], sb_ref[...]
        q_ref[...] = (_rotate(proj(0), c, sa, sb) * (HEAD_DIM ** -0.5)).astype(BF16)
        k_ref[...] = _rotate(proj(1), c, sa, sb).astype(BF16)
        v_ref[...] = proj(2).astype(BF16)
        u_ref[...] = proj(3).astype(BF16)
        b_ref[...] = proj(4).astype(BF16)
        cc_ref[...] = proj(5).astype(BF16)

    a = ATTN_WIDTH
    return _call(
        body, (x, g_pre, w_in, tc, tsa, tsb), grid=(SEQ // ROW_TILE,),
        in_specs=[_rows(D_MODEL), _const((1, D_MODEL)), _const1((D_MODEL, IN_PROJ_WIDTH)), _rows(LANES), _rows(LANES), _rows(LANES)],
        out_specs=[_rows(a)] * 6, out_shape=[_sds((SEQ, a), BF16)] * 6,
        vmem_mib=40, name="inproj_fwd", comm=comm)


def _head_masks():
    lane = lax.broadcasted_iota(jnp.int32, (1, LANES), 1)
    first = lane < HEAD_DIM
    return first, first.astype(F32), 1.0 - first.astype(F32)


def _perm_chunks(dil):
    length = SEQ // dil
    out = []
    for r in range(dil):
        for c0 in range(0, length, PERM_CHUNK):
            chunk = (r * length + c0) // PERM_CHUNK
            rows = pl.ds(c0, PERM_CHUNK) if dil == 1 else pl.ds(r + dil * c0, PERM_CHUNK, stride=dil)
            out.append((chunk, rows))
    return out


def _chunk(c, offset=0):
    return pl.ds(offset + c * PERM_CHUNK, PERM_CHUNK)


def _write_band_bias(bias_ref):
    rr = lax.broadcasted_iota(jnp.int32, (Q_BLOCK, K_WINDOW), 0)
    cc = lax.broadcasted_iota(jnp.int32, (Q_BLOCK, K_WINDOW), 1)
    band = (cc >= rr) & (cc - rr <= 2 * HALF_WINDOW)
    bias_ref[0] = jnp.where(band, 0.0, NEG_INF)
    bias_ref[1] = jnp.where(band & (cc >= HALF_WINDOW), 0.0, NEG_INF)
    bias_ref[2] = jnp.where(band & (cc < Q_BLOCK + HALF_WINDOW), 0.0, NEG_INF)


def _band_bias_index(m0, length):
    return jnp.where(m0 % length == 0, 1, 0) + jnp.where((m0 + Q_BLOCK) % length == 0, 2, 0)


def _zero_key_padding(bufs):
    pad = jnp.zeros((HALF_WINDOW, LANES), BF16)
    for buf in bufs:
        buf[pl.ds(0, HALF_WINDOW), :] = pad
        buf[pl.ds(SEQ + HALF_WINDOW, HALF_WINDOW), :] = pad


def _attn_fwd(q, k, v, comm=None):
    group = 8

    def body(q_ref, k_ref, v_ref, o_ref, lse_ref, q32, k32, v32, qa, qb, kp, vp, accp, mlp,
             acc0, acc1, acc2, ml0, ml1, ml2, bias_ref):
        first, mask_a, mask_b = _head_masks()
        low = lax.broadcasted_iota(jnp.int32, (1, LANES), 1) % HEAD_DIM < HEAD_DIM // 2
        _zero_key_padding((kp, vp))
        _write_band_bias(bias_ref)
        q32[...] = q_ref[...].astype(F32)
        k32[...] = k_ref[...].astype(F32)
        v32[...] = v_ref[...].astype(F32)
        natural = ((acc0, ml0), (acc1, ml1), (acc2, ml2))

        for branch, dil in enumerate(DILATIONS):
            length = SEQ // dil
            assert length >= 2 * Q_BLOCK
            chunks = _perm_chunks(dil)
            for c, rows in chunks:
                val = q32[rows, :]
                qa[_chunk(c), :] = (val * mask_a).astype(BF16)
                qb[_chunk(c), :] = (val * mask_b).astype(BF16)
                kp[_chunk(c, HALF_WINDOW), :] = k32[rows, :].astype(BF16)
                vp[_chunk(c, HALF_WINDOW), :] = v32[rows, :].astype(BF16)
            acc_dst, ml_dst = natural[branch] if dil == 1 else (accp, mlp)

            def blocks(i, carry, length=length, acc_dst=acc_dst, ml_dst=ml_dst):
                base = pl.multiple_of(i * (group * Q_BLOCK), group * Q_BLOCK)
                starts = [base + g * Q_BLOCK for g in range(group)]
                scores = [[_dot_nt(qx[pl.ds(m0, Q_BLOCK), :], kp[pl.ds(m0, K_WINDOW), :]) for qx in (qa, qb)] for m0 in starts]
                probs = []
                for m0, pair in zip(starts, scores):
                    bias = bias_ref[_band_bias_index(m0, length)]
                    stats, ps = [], []
                    for s in pair:
                        s = s + bias
                        m = jnp.max(s, axis=1, keepdims=True)
                        p = jnp.exp(s - m)
                        stats.append(jnp.where(low, m, jnp.sum(p, axis=1, keepdims=True)))
                        ps.append(p.astype(BF16))
                    ml_dst[pl.ds(m0, Q_BLOCK), :] = jnp.where(first, stats[0], stats[1])
                    probs.append(ps)
                for m0, ps in zip(starts, probs):
                    vw = vp[pl.ds(m0, K_WINDOW), :]
                    acc_dst[pl.ds(m0, Q_BLOCK), :] = jnp.where(first, _dot(ps[0], vw), _dot(ps[1], vw))
                return carry

            lax.fori_loop(0, SEQ // (group * Q_BLOCK), blocks, 0)

            if dil > 1:
                for c, rows in chunks:
                    natural[branch][0][rows, :] = accp[_chunk(c), :]
                    natural[branch][1][rows, :] = mlp[_chunk(c), :]

        for c in range(SEQ // PERM_CHUNK):
            packed = [ml[_chunk(c), :] for _, ml in natural]
            ms = [jnp.where(low, ml, pltpu.roll(ml, HEAD_DIM // 2, axis=1)) for ml in packed]
            ls = [jnp.where(low, pltpu.roll(ml, LANES - HEAD_DIM // 2, axis=1), ml) for ml in packed]
            m_all = jnp.maximum(jnp.maximum(ms[0], ms[1]), ms[2])
            es = [jnp.exp(m - m_all) for m in ms]
            z = ls[0] * es[0] + ls[1] * es[1] + ls[2] * es[2]
            num = natural[0][0][_chunk(c), :] * es[0] + natural[1][0][_chunk(c), :] * es[1] + natural[2][0][_chunk(c), :] * es[2]
            o_ref[_chunk(c), :] = num / z
            lse_ref[_chunk(c), :] = m_all + jnp.log(z)

    col = pl.BlockSpec((SEQ, LANES), lambda h: (0, h))
    padded = SEQ + 2 * HALF_WINDOW
    return _call(
        body, (q, k, v), grid=(ATTN_WIDTH // LANES,), in_specs=[col] * 3, out_specs=[col] * 2,
        out_shape=[_sds((SEQ, ATTN_WIDTH), F32)] * 2,
        scratch_shapes=[pltpu.VMEM((SEQ, LANES), F32)] * 3 + [pltpu.VMEM((SEQ, LANES), BF16)] * 2
        + [pltpu.VMEM((padded, LANES), BF16)] * 2 + [pltpu.VMEM((SEQ, LANES), F32)] * 8
        + [pltpu.VMEM((3, Q_BLOCK, K_WINDOW), F32)],
        vmem_mib=52, name="attn_fwd", comm=comm)


def _shifted(t, before, after, i):
    tile = t.shape[0]
    row = lax.broadcasted_iota(jnp.int32, (tile, 1), 0)
    before = jnp.where(i > 0, before, 0.0)
    after = jnp.where(i < SEQ // tile - 1, after, 0.0)
    return (jnp.where(row == 0, before, pltpu.roll(t, 1, axis=0)),
            jnp.where(row == tile - 1, after, pltpu.roll(t, tile - 1, axis=0)))


def _last_row(ref):
    return ref[HALO_ROWS - 1:HALO_ROWS, :].astype(F32)


def _first_row(ref):
    return ref[0:1, :].astype(F32)


def _conv_parts(u, c, u_prev, c_prev, u_next, c_next, cw, i):
    t = c * u
    t_prev, t_next = _shifted(t, _last_row(c_prev) * _last_row(u_prev), _first_row(c_next) * _first_row(u_next), i)
    s = cw[0:1, :] * t_prev + cw[1:2, :] * t + cw[2:3, :] * t_next
    return t, t_prev, t_next, s


def _mix_fwd(attn, u, b, c, conv_w, g_attn, g_conv, w_out, x, g_post):
    def body(a_ref, u_ref, b_ref, c_ref, up_ref, cp_ref, un_ref, cn_ref, cw_ref, ga_ref, gc_ref, w_ref, x_ref, gp_ref,
             x1_ref, mg_ref, mix_ref):
        i = pl.program_id(0)
        _, _, _, s = _conv_parts(u_ref[...].astype(F32), c_ref[...].astype(F32), up_ref, cp_ref, un_ref, cn_ref, cw_ref[...], i)
        ya, _ = _rms(a_ref[...])
        yc, _ = _rms(b_ref[...].astype(F32) * s)
        merged = jnp.concatenate([ya * ga_ref[...], yc * gc_ref[...]], axis=1).astype(BF16)
        mix = _dot(merged, w_ref[...])
        ym, _ = _rms(mix)
        mg_ref[...] = merged.T
        mix_ref[...] = mix.astype(BF16)
        x1_ref[...] = x_ref[...] + ym * gp_ref[...]

    a = ATTN_WIDTH
    return _call(
        body, (attn, u, b, c, u, c, u, c, conv_w, g_attn, g_conv, w_out, x, g_post), grid=(SEQ // ROW_TILE,),
        in_specs=[_rows(a)] * 4 + [_halo_prev(a)] * 2 + [_halo_next(a)] * 2
        + [_const((3, a)), _const((1, a)), _const((1, a)), _const1((D_MODEL, D_MODEL)), _rows(D_MODEL), _const((1, D_MODEL))],
        out_specs=[_rows(D_MODEL), _cols(D_MODEL, ROW_TILE), _rows(D_MODEL)],
        out_shape=[_sds((SEQ, D_MODEL), F32), _sds((D_MODEL, SEQ), BF16), _sds((SEQ, D_MODEL), BF16)],
        vmem_mib=40, name="mix_fwd")[0]


def _gu_spec():
    return pl.BlockSpec((N_DEV, FFN_TILE, FFN_BLOCK), lambda i: (0, i, 0))


def _ffn_fwd(x1, g_pre, w_gu, w_dn, g_post, comm=None, target=None):
    n_tiles = SEQ // ROW_TILE

    def body(*refs):
        if target is None:
            x_ref, g_ref, wgu_ref, wdn_ref, gp_ref, x2_ref, gu_ref, f_ref = refs
        else:
            x_ref, g_ref, wgu_ref, wdn_ref, gp_ref, t_ref, x2_ref, gu_ref, f_ref, loss_ref, acc = refs
        x1 = x_ref[...]
        y, _ = _rms(x1)
        h = (y * g_ref[...]).astype(BF16)
        f = jnp.zeros((ROW_TILE, D_MODEL), F32)
        for j in range(N_DEV // 2):
            gate = _dot_nt(h, wgu_ref[j])
            up = _dot_nt(h, wgu_ref[j + N_DEV // 2])
            gu_ref[j] = gate.astype(BF16)
            gu_ref[j + N_DEV // 2] = up.astype(BF16)
            act = (gate * jax.nn.sigmoid(gate) * up).astype(BF16)
            f = f + _dot(act, wdn_ref[pl.ds(j * FFN_BLOCK, FFN_BLOCK), :])
        yf, _ = _rms(f)
        f_ref[...] = f
        x2 = x1 + yf * gp_ref[...]
        if target is None:
            x2_ref[...] = x2
        else:
            i = pl.program_id(0)
            err = x2 - t_ref[...]
            x2_ref[...] = err * (1.0 / D_MODEL)

            @pl.when(i == 0)
            def _():
                acc[...] = jnp.zeros_like(acc)

            acc[...] += jnp.sum(err * err, axis=0, keepdims=True)

            @pl.when(i == n_tiles - 1)
            def _():
                loss_ref[...] = jnp.sum(acc[...], axis=1, keepdims=True) * (0.5 / D_MODEL)

    with_loss = target is not None
    return _call(
        body, (x1, g_pre, w_gu, w_dn, g_post) + ((target,) if with_loss else ()), grid=(n_tiles,),
        in_specs=[_rows(D_MODEL), _const((1, D_MODEL)), _const1((N_DEV, FFN_BLOCK, D_MODEL)), _const1((FFN_HIDDEN, D_MODEL)),
                  _const((1, D_MODEL))] + ([_rows(D_MODEL)] if with_loss else []),
        out_specs=[_rows(D_MODEL), pl.BlockSpec((N_DEV, ROW_TILE, FFN_BLOCK), lambda i: (0, i, 0)), _rows(D_MODEL)]
        + ([_const((1, 1))] if with_loss else []),
        out_shape=[_sds((SEQ, D_MODEL), F32), _sds((N_DEV, SEQ, FFN_BLOCK), BF16), _sds((SEQ, D_MODEL), F32)]
        + ([_sds((1, 1), F32)] if with_loss else []),
        scratch_shapes=[pltpu.VMEM((1, D_MODEL), F32)] if with_loss else [],
        vmem_mib=58, name="ffn_fwd_loss" if with_loss else "ffn_fwd", comm=comm)


def _accumulate(pairs, i):
    @pl.when(i == 0)
    def _():
        for ref, value in pairs:
            ref[...] = value

    @pl.when(i > 0)
    def _():
        for ref, value in pairs:
            ref[...] += value


def _colsum(v):
    return jnp.sum(v, axis=0, keepdims=True)


def _ffn_bwd(dx2, f, x1, gu, w_gu, w_dn, g_post, g_pre, comm=None):
    half = N_DEV // 2

    def body(dx2_ref, f_ref, x1_ref, gu_ref, wgu_ref, wdn_ref, gpost_ref, gpre_ref,
             dx1_ref, df_ref, act_ref, dgu_ref, h_ref, dgpost_ref, dgpre_ref):
        i = pl.program_id(0)
        dx2 = dx2_ref[...]
        yf, rf = _rms(f_ref[...])
        dg_post = _colsum(dx2 * yf)
        df = _rms_bwd(dx2 * gpost_ref[...], yf, rf).astype(BF16)
        df_ref[...] = df
        dh = jnp.zeros((FFN_TILE, D_MODEL), F32)
        for j in range(half):
            dact = _dot_nt(df, wdn_ref[pl.ds(j * FFN_BLOCK, FFN_BLOCK), :])
            gate = gu_ref[j].astype(F32)
            up = gu_ref[j + half].astype(F32)
            sig = jax.nn.sigmoid(gate)
            silu = gate * sig
            act_ref[j] = (silu * up).astype(BF16)
            dgate = (dact * up * (sig * (1.0 + gate * (1.0 - sig)))).astype(BF16)
            dup = (dact * silu).astype(BF16)
            dgu_ref[j] = dgate
            dgu_ref[j + half] = dup
            dh = dh + _dot(dgate, wgu_ref[j]) + _dot(dup, wgu_ref[j + half])
        y1, r1 = _rms(x1_ref[...])
        h_ref[...] = (y1 * gpre_ref[...]).astype(BF16)
        dx1_ref[...] = dx2 + _rms_bwd(dh * gpre_ref[...], y1, r1)
        _accumulate([(dgpost_ref, dg_post), (dgpre_ref, _colsum(dh * y1))], i)

    act_spec = pl.BlockSpec((half, FFN_TILE, FFN_BLOCK), lambda i: (0, i, 0))
    return _call(
        body, (dx2, f, x1, gu, w_gu, w_dn, g_post, g_pre), grid=(SEQ // FFN_TILE,),
        in_specs=[_frows(D_MODEL)] * 3 + [_gu_spec(), _const1((N_DEV, FFN_BLOCK, D_MODEL)), _const1((FFN_HIDDEN, D_MODEL)),
                                          _const((1, D_MODEL)), _const((1, D_MODEL))],
        out_specs=[_frows(D_MODEL), _frows(D_MODEL), act_spec, _gu_spec(), _frows(D_MODEL), _const((1, D_MODEL)), _const((1, D_MODEL))],
        out_shape=[_sds((SEQ, D_MODEL), F32), _sds((SEQ, D_MODEL), BF16), _sds((half, SEQ, FFN_BLOCK), BF16),
                   _sds((N_DEV, SEQ, FFN_BLOCK), BF16), _sds((SEQ, D_MODEL), BF16), _sds((1, D_MODEL), F32), _sds((1, D_MODEL), F32)],
        vmem_mib=52, name="ffn_bwd", comm=comm)


def _wgrad(a_t, g, n_blocks, out_shape, name, comm=None):
    def body(a_ref, g_ref, o_ref):
        o_ref[...] = _dot(a_ref[...], g_ref[...]).astype(BF16)

    res, landed = _call(body, (a_t, g), grid=(n_blocks,), in_specs=[_const1((D_MODEL, SEQ)), pl.BlockSpec((SEQ, 512), lambda j: (0, j))],
                        out_specs=[pl.BlockSpec((D_MODEL, 512), lambda j: (0, j))], out_shape=[_sds(out_shape, BF16)], vmem_mib=48,
                        name=name, comm=comm)
    return res[0], landed


def _wgrad_paired(a, g, kind, comm=None):
    shard = SHARD_SHAPES[kind]
    per_chip = 2 if kind == "gu" else 1
    n_compute = per_chip * N_CHIPS
    n_steps = n_compute + 1

    def block(t):
        return jnp.minimum(t, n_compute - 1)

    if kind == "in":
        specs = [_const1((D_MODEL, SEQ)), pl.BlockSpec((SEQ, 2 * W_IN_BLOCK), lambda t: (0, block(t)))]
    else:
        specs = [pl.BlockSpec((1, SEQ, FFN_BLOCK), lambda t: (block(t), 0, 0)), _const1((SEQ, D_MODEL))]

    def body(a_ref, g_ref, o_ref, sendbuf, recvbuf, keepbuf, send_sem, recv_sem):
        t = pl.program_id(0)
        x, y, c = _place()

        def exchange(q):
            return pltpu.make_async_remote_copy(src_ref=sendbuf.at[q], dst_ref=recvbuf.at[q], send_sem=send_sem.at[q],
                                                recv_sem=recv_sem.at[q], device_id=(x, y, 1 - c), device_id_type=MESH)

        @pl.when((t >= per_chip) & (t % per_chip == 0))
        def _():
            q = t // per_chip - 1
            exchange(q).wait_recv()
            o_ref[0] = (keepbuf[...] + recvbuf[q].astype(F32)).astype(BF16)

        @pl.when(t < n_compute)
        def _():
            q = t // per_chip
            if kind == "gu":
                r = _dot_tn(a_ref[0], g_ref[...])

                @pl.when(t % 2 == c)
                def _():
                    keepbuf[...] = r

                @pl.when(t % 2 != c)
                def _():
                    sendbuf[q] = r.astype(BF16)
                    exchange(q).start()
            else:
                if kind == "dn":
                    r = _dot_tn(a_ref[0], g_ref[...])
                    lower, upper = r[:W_DOWN_BLOCK], r[W_DOWN_BLOCK:]
                else:
                    r = _dot(a_ref[...], g_ref[...])
                    lower, upper = r[:, :W_IN_BLOCK], r[:, W_IN_BLOCK:]
                keepbuf[...] = jnp.where(c == 0, lower, upper)
                sendbuf[q] = jnp.where(c == 0, upper, lower).astype(BF16)
                exchange(q).start()

        @pl.when(t == n_steps - 1)
        def _():
            for q in range(N_CHIPS):
                exchange(q).wait_send()

    res, landed = _call(
        body, (a, g), grid=(n_steps,), in_specs=specs,
        out_specs=[pl.BlockSpec((1,) + shard, lambda t: (jnp.clip(t // per_chip - 1, 0, N_CHIPS - 1), 0, 0))],
        out_shape=[_sds((N_CHIPS,) + shard, BF16)],
        scratch_shapes=[pltpu.VMEM((N_CHIPS,) + shard, BF16), pltpu.VMEM((N_CHIPS,) + shard, BF16), pltpu.VMEM(shard, F32),
                        pltpu.SemaphoreType.DMA((N_CHIPS,)), pltpu.SemaphoreType.DMA((N_CHIPS,))],
        vmem_mib=52, name="wgrad_" + kind + "_paired", comm=comm)
    return res[0], landed


def _mix_bwd(dx1, mix, attn, u, b, c, conv_w, g_attn, g_conv, g_post, w_out):
    def body(dx1_ref, mix_ref, a_ref, u_ref, b_ref, c_ref, up_ref, cp_ref, un_ref, cn_ref, cw_ref, ga_ref, gc_ref, gp_ref, w_ref,
             dmix_ref, da_ref, ds_ref, db_ref, dgp_ref, dga_ref, dgc_ref):
        i = pl.program_id(0)
        dx1 = dx1_ref[...]
        ym, rm = _rms(mix_ref[...].astype(F32))
        dg_post = _colsum(dx1 * ym)
        dmix = _rms_bwd(dx1 * gp_ref[...], ym, rm).astype(BF16)
        dmix_ref[...] = dmix
        dmerged = _dot_nt(dmix, w_ref[...])
        dna, dnc = dmerged[:, :ATTN_WIDTH], dmerged[:, ATTN_WIDTH:]
        ya, ra = _rms(a_ref[...])
        da_ref[...] = _rms_bwd(dna * ga_ref[...], ya, ra)
        _, _, _, s = _conv_parts(u_ref[...].astype(F32), c_ref[...].astype(F32), up_ref, cp_ref, un_ref, cn_ref, cw_ref[...], i)
        gate_b = b_ref[...].astype(F32)
        yc, rc = _rms(gate_b * s)
        dy = _rms_bwd(dnc * gc_ref[...], yc, rc)
        db_ref[...] = (dy * s).astype(BF16)
        ds_ref[...] = (dy * gate_b).astype(BF16)
        _accumulate([(dgp_ref, dg_post), (dga_ref, _colsum(dna * ya)), (dgc_ref, _colsum(dnc * yc))], i)

    a = ATTN_WIDTH
    return _call(
        body, (dx1, mix, attn, u, b, c, u, c, u, c, conv_w, g_attn, g_conv, g_post, w_out), grid=(SEQ // ROW_TILE,),
        in_specs=[_rows(D_MODEL)] * 2 + [_rows(a)] * 4 + [_halo_prev(a)] * 2 + [_halo_next(a)] * 2
        + [_const((3, a)), _const((1, a)), _const((1, a)), _const((1, D_MODEL)), _const1((D_MODEL, D_MODEL))],
        out_specs=[_rows(D_MODEL)] + [_rows(a)] * 3 + [_const((1, D_MODEL)), _const((1, a)), _const((1, a))],
        out_shape=[_sds((SEQ, D_MODEL), BF16), _sds((SEQ, a), F32), _sds((SEQ, a), BF16), _sds((SEQ, a), BF16),
                   _sds((1, D_MODEL), F32), _sds((1, a), F32), _sds((1, a), F32)],
        vmem_mib=40, name="mix_bwd")[0]


def _attn_bwd(q, k, v, do, o, lse, comm=None):
    group = 8

    def body(q_ref, k_ref, v_ref, do_ref, o_ref, lse_ref, dq_ref, dk_ref, dv_ref,
             q32, k32, v32, qa, qb, doa, dob, kp, vp, lsep, dlp, dnat, dqp, dkp, dvp, bias_ref):
        first, mask_a, mask_b = _head_masks()
        _zero_key_padding((kp, vp))
        _write_band_bias(bias_ref)
        q32[...] = q_ref[...].astype(F32)
        k32[...] = k_ref[...].astype(F32)
        v32[...] = v_ref[...].astype(F32)
        for c in range(SEQ // PERM_CHUNK):
            prod = do_ref[_chunk(c), :] * o_ref[_chunk(c), :]
            d_a = jnp.sum(prod * mask_a, axis=1, keepdims=True)
            d_b = jnp.sum(prod * mask_b, axis=1, keepdims=True)
            dnat[_chunk(c), :] = jnp.where(first, d_a, d_b)

        for step, dil in enumerate(DILATIONS[1:] + DILATIONS[:1]):
            length = SEQ // dil
            assert length >= 2 * Q_BLOCK
            chunks = _perm_chunks(dil)
            for c, rows in chunks:
                val = q32[rows, :]
                qa[_chunk(c), :] = (val * mask_a).astype(BF16)
                qb[_chunk(c), :] = (val * mask_b).astype(BF16)
                val = do_ref[rows, :]
                doa[_chunk(c), :] = (val * mask_a).astype(BF16)
                dob[_chunk(c), :] = (val * mask_b).astype(BF16)
                kp[_chunk(c, HALF_WINDOW), :] = k32[rows, :].astype(BF16)
                vp[_chunk(c, HALF_WINDOW), :] = v32[rows, :].astype(BF16)
                lsep[_chunk(c), :] = lse_ref[rows, :]
                dlp[_chunk(c), :] = dnat[rows, :]
            zero = jnp.zeros((PERM_CHUNK, LANES), F32)
            for c in range(SEQ // PERM_CHUNK):
                dkp[_chunk(c), :] = zero
                dvp[_chunk(c), :] = zero
            dkp[pl.ds(SEQ, 2 * HALF_WINDOW), :] = zero[:2 * HALF_WINDOW]
            dvp[pl.ds(SEQ, 2 * HALF_WINDOW), :] = zero[:2 * HALF_WINDOW]

            heads = ((qa, doa, 0), (qb, dob, HEAD_DIM))

            def blocks(i, carry, length=length):
                base = pl.multiple_of(i * (group * Q_BLOCK), group * Q_BLOCK)
                starts = [base + g * Q_BLOCK for g in range(group)]
                raw = [[(_dot_nt(qx[pl.ds(m0, Q_BLOCK), :], kp[pl.ds(m0, K_WINDOW), :]),
                         _dot_nt(dox[pl.ds(m0, Q_BLOCK), :], vp[pl.ds(m0, K_WINDOW), :])) for qx, dox, _ in heads]
                       for m0 in starts]
                grads = []
                for m0, pair in zip(starts, raw):
                    bias = bias_ref[_band_bias_index(m0, length)]
                    lse_b, d_b = lsep[pl.ds(m0, Q_BLOCK), :], dlp[pl.ds(m0, Q_BLOCK), :]
                    out = []
                    for (s, dp), (_, _, col) in zip(pair, heads):
                        p = jnp.exp(s + bias - lse_b[:, col:col + 1])
                        out.append(((p * (dp - d_b[:, col:col + 1])).astype(BF16), p.astype(BF16)))
                    grads.append(out)
                for m0, out in zip(starts, grads):
                    qrows, krows = pl.ds(m0, Q_BLOCK), pl.ds(m0, K_WINDOW)
                    kw = kp[krows, :]
                    dk = jnp.zeros((K_WINDOW, LANES), F32)
                    dv = jnp.zeros((K_WINDOW, LANES), F32)
                    for (ds, p), (qx, dox, _) in zip(out, heads):
                        dk = dk + _dot_tn(ds, qx[qrows, :])
                        dv = dv + _dot_tn(p, dox[qrows, :])
                    dqp[qrows, :] = jnp.where(first, _dot(out[0][0], kw), _dot(out[1][0], kw)) * (HEAD_DIM ** -0.5)
                    dkp[krows, :] += dk
                    dvp[krows, :] += dv
                return carry

            lax.fori_loop(0, SEQ // (group * Q_BLOCK), blocks, 0)

            for c, rows in chunks:
                g_q, g_k, g_v = dqp[_chunk(c), :], dkp[_chunk(c, HALF_WINDOW), :], dvp[_chunk(c, HALF_WINDOW), :]
                if step == 0:
                    dq_ref[rows, :] = g_q
                    dk_ref[rows, :] = g_k
                    dv_ref[rows, :] = g_v
                else:
                    dq_ref[rows, :] = dq_ref[rows, :] + g_q
                    dk_ref[rows, :] = dk_ref[rows, :] + g_k
                    dv_ref[rows, :] = dv_ref[rows, :] + g_v

    col = pl.BlockSpec((SEQ, LANES), lambda h: (0, h))
    col1 = pl.BlockSpec((SEQ, LANES), lambda h: (0, h), pipeline_mode=pl.Buffered(1))
    padded = SEQ + 2 * HALF_WINDOW
    return _call(
        body, (q, k, v, do, o, lse), grid=(ATTN_WIDTH // LANES,), in_specs=[col1] * 6, out_specs=[col] * 3,
        out_shape=[_sds((SEQ, ATTN_WIDTH), F32)] * 3,
        scratch_shapes=[pltpu.VMEM((SEQ, LANES), F32)] * 3 + [pltpu.VMEM((SEQ, LANES), BF16)] * 4
        + [pltpu.VMEM((padded, LANES), BF16)] * 2 + [pltpu.VMEM((SEQ, LANES), F32)] * 4 + [pltpu.VMEM((padded, LANES), F32)] * 2
        + [pltpu.VMEM((3, Q_BLOCK, K_WINDOW), F32)],
        vmem_mib=56, name="attn_bwd", comm=comm)


def _inproj_bwd(dq, dk, dv, ds, db, u, c, conv_w, tc, tsa, tsb, w_in, x, g_pre, dx1):
    def body(dq_ref, dk_ref, dv_ref, ds_ref, db_ref, u_ref, c_ref, dsp_ref, up_ref, cp_ref, dsn_ref, un_ref, cn_ref, cw_ref,
             tc_ref, tsa_ref, tsb_ref, w_ref, x_ref, g_ref, dx1_ref, dx_ref, dproj_ref, h_ref, dg_ref, dcw_ref):
        i = pl.program_id(0)
        cw = cw_ref[...]
        u, c = u_ref[...].astype(F32), c_ref[...].astype(F32)
        t, t_prev, t_next, _ = _conv_parts(u, c, up_ref, cp_ref, un_ref, cn_ref, cw, i)
        ds = ds_ref[...].astype(F32)
        ds_prev, ds_next = _shifted(ds, _last_row(dsp_ref), _first_row(dsn_ref), i)
        dt = cw[0:1, :] * ds_next + cw[1:2, :] * ds + cw[2:3, :] * ds_prev
        d_taps = jnp.concatenate([_colsum(ds * t_prev), _colsum(ds * t), _colsum(ds * t_next)], axis=0)
        tc_, tsa_, tsb_ = tc_ref[...], tsa_ref[...], tsb_ref[...]
        groups = ((2, lambda: dv_ref[...].astype(BF16)), (4, lambda: db_ref[...]),
                  (0, lambda: _rotate_transposed(dq_ref[...], tc_, tsa_, tsb_).astype(BF16)),
                  (1, lambda: _rotate_transposed(dk_ref[...], tc_, tsa_, tsb_).astype(BF16)),
                  (3, lambda: (dt * c).astype(BF16)), (5, lambda: (dt * u).astype(BF16)))
        dh = jnp.zeros((x_ref.shape[0], D_MODEL), F32)
        for n, make in groups:
            cols = pl.ds(n * a, a)
            part = make()
            dproj_ref[:, cols] = part
            dh = dh + _dot_nt(part, w_ref[:, cols])
        y, r = _rms(x_ref[...])
        h_ref[...] = (y * g_ref[...]).astype(BF16).T
        dx_ref[...] = dx1_ref[...] + _rms_bwd(dh * g_ref[...], y, r)
        _accumulate([(dg_ref, _colsum(dh * y)), (dcw_ref, d_taps)], i)

    a = ATTN_WIDTH
    tile = FFN_TILE

    def rows(width):
        return _rows(width, tile)

    return _call(
        body, (dq, dk, dv, ds, db, u, c, ds, u, c, ds, u, c, conv_w, tc, tsa, tsb, w_in, x, g_pre, dx1), grid=(SEQ // tile,),
        in_specs=[rows(a)] * 7 + [_halo_prev(a, tile)] * 3 + [_halo_next(a, tile)] * 3
        + [_const((3, a)), rows(LANES), rows(LANES), rows(LANES), _const1((D_MODEL, IN_PROJ_WIDTH)), rows(D_MODEL),
           _const((1, D_MODEL)), rows(D_MODEL)],
        out_specs=[rows(D_MODEL), rows(IN_PROJ_WIDTH), _cols(D_MODEL, tile), _const((1, D_MODEL)), _const((3, a))],
        out_shape=[_sds((SEQ, D_MODEL), F32), _sds((SEQ, IN_PROJ_WIDTH), BF16), _sds((D_MODEL, SEQ), BF16),
                   _sds((1, D_MODEL), F32), _sds((3, a), F32)],
        vmem_mib=48, name="inproj_bwd")


def _adamw_math(w, g, m, v):
    m = ADAM_B1 * m + (1.0 - ADAM_B1) * g
    v = ADAM_B2 * v + (1.0 - ADAM_B2) * (g * g)
    m_hat = m / (1.0 - ADAM_B1 ** ADAM_STEP)
    v_hat = v / (1.0 - ADAM_B2 ** ADAM_STEP)
    delta = -ADAM_LR * (m_hat / (jnp.sqrt(v_hat) + ADAM_EPS) + ADAM_WD * w)
    return delta, m, v


def _sum_parts(p_ref):
    g = p_ref[0].astype(F32)
    for k in range(1, p_ref.shape[0]):
        g = g + p_ref[k].astype(F32)
    return g


def _adamw_layers(parts, w, m, v, row_tile, name, comm=None):
    _, rows, cols = w.shape
    n_tiles = rows // row_tile

    def body(*refs):
        p_refs = refs[:DEPTH]
        w_ref, m_ref, v_ref, g_ref, d_ref, nm_ref, nv_ref = refs[DEPTH:]
        layer = pl.program_id(0)
        for l, p_ref in enumerate(p_refs):
            @pl.when(layer == l)
            def _(p_ref=p_ref):
                g = _sum_parts(p_ref)
                g_ref[0] = g
                d_ref[0], nm_ref[0], nv_ref[0] = _adamw_math(w_ref[0], g, m_ref[0], v_ref[0])

    def part_spec(l):
        return pl.BlockSpec((parts[l].shape[0], row_tile, cols),
                            lambda layer, i: (0, jnp.where(layer == l, i, jnp.where(layer < l, 0, n_tiles - 1)), 0))

    tile = pl.BlockSpec((1, row_tile, cols), lambda layer, i: (layer, i, 0))
    return _call(body, (*parts, w, m, v), grid=(DEPTH, n_tiles), in_specs=[part_spec(l) for l in range(DEPTH)] + [tile] * 3,
                 out_specs=[tile] * 4, out_shape=[_sds(w.shape, F32)] * 4, name=name, comm=comm)


def _adamw_sum8(parts, w, m, v, name):
    def body(p_ref, w_ref, m_ref, v_ref, g_ref, d_ref, nm_ref, nv_ref):
        g = _sum_parts(p_ref)
        g_ref[...] = g
        d_ref[...], nm_ref[...], nv_ref[...] = _adamw_math(w_ref[...], g, m_ref[...], v_ref[...])

    return pl.pallas_call(body, out_shape=[_sds(w.shape, F32)] * 4, name=name)(parts, w, m, v)


def _adamw_plain(g, w, m, v, name):
    def body(g_ref, w_ref, m_ref, v_ref, d_ref, nm_ref, nv_ref):
        d_ref[...], nm_ref[...], nv_ref[...] = _adamw_math(w_ref[...], g_ref[...], m_ref[...], v_ref[...])

    return pl.pallas_call(body, out_shape=[_sds(w.shape, F32)] * 3, name=name)(g, w, m, v)


SMALL_ROWS = 8
GAIN_NAMES = ("pre_mix_norm", "post_mix_norm", "pre_ffn_norm", "post_ffn_norm", "attn_out_norm", "conv_out_norm")


def _pack_small(pre_mix, post_mix, pre_ffn, post_ffn, attn_out, conv_out, taps):
    zeros = jnp.zeros((1, ATTN_WIDTH), F32)
    return jnp.concatenate([
        pre_mix, post_mix, pre_ffn, post_ffn, jnp.concatenate([attn_out, conv_out], axis=1),
        jnp.concatenate([taps[0:1], taps[1:2]], axis=1), jnp.concatenate([taps[2:3], zeros], axis=1),
        jnp.zeros((1, D_MODEL), F32)], axis=0)


def _rope(positions, comm=None):
    inv_freq = ROPE_THETA ** (-jnp.arange(0, ROPE_DIM, 2, dtype=F32) / ROPE_DIM)
    per_head = jnp.concatenate([inv_freq, inv_freq, jnp.zeros((HEAD_DIM - ROPE_DIM,), F32)])
    freq_row = jnp.tile(per_head, LANES // HEAD_DIM).reshape(1, LANES)
    return _rope_tables(positions.reshape(SEQ, 1), freq_row, comm)


def _layer_forward(h, gains, taps, tables, w, inproj_comm=None, attn_comm=None, ffn_comm=None, target=None):
    (q, k, v, u, b, c), landed = _inproj_fwd(h, gains["pre_mix_norm"], w["in"], *tables, comm=inproj_comm)
    if inproj_comm is not None:
        w = {**w, **dict(zip(inproj_comm.kinds, landed))}
    (attn, lse), landed = _attn_fwd(q, k, v, comm=attn_comm)
    if attn_comm is not None:
        w = {**w, **dict(zip(attn_comm.kinds, landed))}
    x1, merged, mix = _mix_fwd(attn, u, b, c, taps, gains["attn_out_norm"], gains["conv_out_norm"], w["out"], h, gains["post_mix_norm"])
    (x2, gu, f, *loss), landed_next = _ffn_fwd(x1, gains["pre_ffn_norm"], w["gu"], w["dn"], gains["post_ffn_norm"], comm=ffn_comm,
                                               target=target)
    out = x2 if target is None else (x2, loss[0])
    return out, (h, q, k, v, u, b, c, attn, lse, merged, mix, x1, gu, f), w, landed_next


class _GradientExchange:
    def ffn_grads(self, act, df, h2, dgu):
        return dict(dn=_wgrad_paired(act, df, "dn")[0], gu=_wgrad_paired(dgu, h2, "gu")[0])

    def attention_passenger(self, ffn):
        return _ScatterChips(("gu", "dn"), (ffn["gu"], ffn["dn"]))

    def w_out_grad(self, merged_t, dmix):
        return _wgrad(merged_t, dmix, D_MODEL // 512, (D_MODEL, D_MODEL), "wgrad_out")[0]

    def w_in_grad(self, h_t, dproj, g_out, last):
        comm = _Scatter(("out",), (g_out,))
        if last:
            g_in, landed = _wgrad_paired(h_t, dproj, "in", comm=comm)
        else:
            g_in, landed = _wgrad(h_t, dproj, IN_PROJ_WIDTH // 512, (D_MODEL, IN_PROJ_WIDTH), "wgrad_in", comm=comm)
        return g_in, landed[0]


def _layer_backward(dx, saved, gains, taps, tables, w, ffn_comm, exchange, last):
    x0, q, k, v, u, b, c, attn, lse, merged, mix, x1, gu, f = saved
    (dx1, df, act, dgu, h2, dg_post_ffn, dg_pre_ffn), landed_prev = _ffn_bwd(
        dx, f, x1, gu, w["gu"], w["dn"], gains["post_ffn_norm"], gains["pre_ffn_norm"], comm=ffn_comm)
    ffn = exchange.ffn_grads(act, df, h2, dgu)
    dmix, dattn, ds, db, dg_post_mix, dg_attn, dg_conv = _mix_bwd(
        dx1, mix, attn, u, b, c, taps, gains["attn_out_norm"], gains["conv_out_norm"], gains["post_mix_norm"], w["out"])
    g_out = exchange.w_out_grad(merged, dmix)
    attn_comm = exchange.attention_passenger(ffn)
    (dq, dk, dv), landed_ffn = _attn_bwd(q, k, v, dattn, attn, lse, comm=attn_comm)
    dx0, dproj, h1, dg_pre_mix, dtaps = _inproj_bwd(
        dq, dk, dv, ds, db, u, c, taps, *tables, w["in"], x0, gains["pre_mix_norm"], dx1)[0]
    g_in, out_parts = exchange.w_in_grad(h1, dproj, g_out, last)
    small = _pack_small(dg_pre_mix, dg_post_mix, dg_pre_ffn, dg_post_ffn, dg_attn, dg_conv, dtaps)
    rest = dict(zip(attn_comm.kinds, landed_ffn)) if attn_comm is not None else dict(ffn)
    rest["out"] = out_parts
    return dx0, g_in, small, landed_prev, rest


def kernel(x, positions, pre_mix_norm, w_in, conv_w, attn_out_norm, conv_out_norm, w_out, post_mix_norm, pre_ffn_norm, w_gate_up, w_down, post_ffn_norm, loss_target, m_pre_mix_norm, m_w_in, m_conv_w, m_attn_out_norm, m_conv_out_norm, m_w_out, m_post_mix_norm, m_pre_ffn_norm, m_w_gate_up, m_w_down, m_post_ffn_norm, v_pre_mix_norm, v_w_in, v_conv_w, v_attn_out_norm, v_conv_out_norm, v_w_out, v_post_mix_norm, v_pre_ffn_norm, v_w_gate_up, v_w_down, v_post_ffn_norm):
    mx, my, mc = _place()
    me = _block_of(mx, my, mc)
    conv_channels = conv_w.shape[-1]

    taps_flat = jnp.pad(conv_w.reshape(-1), (0, 8 * LANES - conv_w.size)).reshape(8, LANES)
    taps_all = _allgather_small(taps_flat, "allgather_taps").reshape(N_DEV, 8 * LANES)[:, :conv_w.size]
    conv_w_full = taps_all.reshape(N_DEV, DEPTH, 3, conv_channels).transpose(1, 2, 0, 3).reshape(DEPTH, 3, ATTN_WIDTH)

    def hidden_major(a):
        return jnp.swapaxes(a, 1, 2)

    big_w = dict(zip(WEIGHT_KINDS, (w_in, w_out, hidden_major(w_gate_up), w_down)))
    big_m = dict(zip(WEIGHT_KINDS, (m_w_in, m_w_out, hidden_major(m_w_gate_up), m_w_down)))
    big_v = dict(zip(WEIGHT_KINDS, (v_w_in, v_w_out, hidden_major(v_w_gate_up), v_w_down)))
    shards = {kind: big_w[kind].astype(BF16) for kind in WEIGHT_KINDS}
    all_gains = dict(pre_mix_norm=pre_mix_norm, attn_out_norm=attn_out_norm, conv_out_norm=conv_out_norm, post_mix_norm=post_mix_norm,
                     pre_ffn_norm=pre_ffn_norm, post_ffn_norm=post_ffn_norm)

    def gather(kinds, l):
        return _Gather(kinds, [shards[kind][l] for kind in kinds])

    def gains(l):
        return {name: g[l:l + 1] for name, g in all_gains.items()}

    early = ("in", "out", "dn")
    tables, first = _rope(positions, gather(early[:1], 0))
    weights = [dict(zip(early[:1], first))] + [None] * (DEPTH - 1)
    saved = [None] * DEPTH
    h = x[0]
    for l in range(DEPTH):
        inproj_comm = gather(early[1:], 0) if l == 0 else None
        ffn_comm = gather(early, l + 1) if l + 1 < DEPTH else None
        target = loss_target[0] if l + 1 == DEPTH else None
        h, saved[l], weights[l], landed = _layer_forward(h, gains(l), conv_w_full[l], tables, weights[l], inproj_comm,
                                                         gather(("gu",), l), ffn_comm, target)
        if ffn_comm is not None:
            weights[l + 1] = dict(zip(ffn_comm.kinds, landed))
    dx, local_loss = h

    parts = {kind: [None] * DEPTH for kind in WEIGHT_KINDS}
    small_grads = [None] * DEPTH
    g_in_above = None
    exchange = _GradientExchange()
    for l in reversed(range(DEPTH)):
        ffn_comm = _Scatter(("in",), (g_in_above,)) if g_in_above is not None else None
        dx, g_in_above, small_grads[l], landed, rest = _layer_backward(dx, saved[l], gains(l), conv_w_full[l], tables, weights[l], ffn_comm,
                                                                      exchange, l == 0)
        if ffn_comm is not None:
            parts["in"][l + 1] = landed[0]
        for kind, part in rest.items():
            parts[kind][l] = part

    parts["in"][0] = _comm_only(_ScatterChips(("in",), (g_in_above,)), "scatter_last")[0]
    tiles = {"in": 256, "out": 128, "gu": 176, "dn": 176}
    big = {kind: _adamw_layers(parts[kind], big_w[kind], big_m[kind], big_v[kind], tiles[kind], "adamw_" + kind)[0]
           for kind in WEIGHT_KINDS}

    packed = jnp.concatenate(small_grads, axis=0)
    packed = lax.dynamic_update_slice(packed, local_loss, (SMALL_ROWS - 1, 0))
    gathered = _allgather_small(packed, "allgather_small_grads")

    def pack_state(state):
        rows = [_pack_small(*[state[name][l:l + 1] for name in GAIN_NAMES], jnp.zeros((3, ATTN_WIDTH), F32)) for l in range(DEPTH)]
        return jnp.concatenate(rows, axis=0)

    sw = pack_state(all_gains)
    sm = pack_state(dict(zip(GAIN_NAMES, (m_pre_mix_norm, m_post_mix_norm, m_pre_ffn_norm, m_post_ffn_norm, m_attn_out_norm, m_conv_out_norm))))
    sv = pack_state(dict(zip(GAIN_NAMES, (v_pre_mix_norm, v_post_mix_norm, v_pre_ffn_norm, v_post_ffn_norm, v_attn_out_norm, v_conv_out_norm))))
    sg, sd, snm, snv = _adamw_sum8(gathered, sw, sm, sv, "adamw_small")

    def unpack(p):
        p = p.reshape(DEPTH, SMALL_ROWS, D_MODEL)
        return dict(pre_mix_norm=p[:, 0], post_mix_norm=p[:, 1], pre_ffn_norm=p[:, 2], post_ffn_norm=p[:, 3],
                    attn_out_norm=p[:, 4, :ATTN_WIDTH], conv_out_norm=p[:, 4, ATTN_WIDTH:])

    small = [unpack(p) for p in (sg, sd, snm, snv)]
    loss = sg[SMALL_ROWS - 1, 0]
    sg3 = sg.reshape(DEPTH, SMALL_ROWS, D_MODEL)
    taps_grad_full = jnp.stack([sg3[:, 5, :ATTN_WIDTH], sg3[:, 5, ATTN_WIDTH:], sg3[:, 6, :ATTN_WIDTH]], axis=1)
    taps_grad = lax.dynamic_slice_in_dim(taps_grad_full, me * conv_channels, conv_channels, axis=2)

    def flat(a):
        return a.reshape(DEPTH * 3, conv_channels)

    td, tnm, tnv = _adamw_plain(flat(taps_grad), flat(conv_w), flat(m_conv_w), flat(v_conv_w), "adamw_taps")
    taps = [taps_grad] + [a.reshape(conv_w.shape) for a in (td, tnm, tnv)]

    def leaves(o):
        s = small[o]
        return (s["pre_mix_norm"], big["in"][o], taps[o], s["attn_out_norm"], s["conv_out_norm"], big["out"][o], s["post_mix_norm"],
                s["pre_ffn_norm"], hidden_major(big["gu"][o]), big["dn"][o], s["post_ffn_norm"])

    return (loss, dx[None], *leaves(0), *leaves(1), *leaves(2), *leaves(3))
```

```python
import math

import jax
import jax.numpy as jnp
from jax import lax
from jax.experimental import pallas as pl
from jax.experimental.pallas import tpu as pltpu

F32 = jnp.float32
BF16 = jnp.bfloat16
MESH = pl.DeviceIdType.MESH

SEQ = 4096
D_MODEL = 1024
DEPTH = 4
N_DEV = 8
ATTN_WIDTH = 512
IN_PROJ_WIDTH = 3072
FFN_HIDDEN = 2816
FFN_BLOCK = 2 * FFN_HIDDEN // N_DEV
W_IN_BLOCK = IN_PROJ_WIDTH // N_DEV
W_OUT_BLOCK = D_MODEL // N_DEV
W_DOWN_BLOCK = FFN_HIDDEN // N_DEV
HEAD_DIM = 64
ROPE_DIM = 16
ROPE_THETA = 500000.0
DILATIONS = (1, 4, 16)
HALF_WINDOW = 64
RMS_EPS = 1e-6
NEG_INF = -1e30
LANES = 128
Q_BLOCK = 128
K_WINDOW = Q_BLOCK + 2 * HALF_WINDOW
PERM_CHUNK = 256
ROW_TILE = 512
FFN_TILE = 256
WGRAD_COLS = 512
ADAM_LR, ADAM_B1, ADAM_B2, ADAM_EPS, ADAM_WD, ADAM_STEP = 0.001, 0.9, 0.999, 1e-08, 0.01, 10
MIB = 1024 * 1024
PINNED_BYTES = 256 * 1024

WEIGHT_KINDS = ("in", "out", "gu", "dn")
FULL_SHAPES = {"in": (D_MODEL, IN_PROJ_WIDTH), "out": (D_MODEL, D_MODEL), "gu": (N_DEV, FFN_BLOCK, D_MODEL), "dn": (FFN_HIDDEN, D_MODEL)}
SHARD_SHAPES = {"in": (D_MODEL, W_IN_BLOCK), "out": (W_OUT_BLOCK, D_MODEL), "gu": (FFN_BLOCK, D_MODEL), "dn": (W_DOWN_BLOCK, D_MODEL)}
ANY = pl.BlockSpec(memory_space=pl.ANY)


def _sds(shape, dtype):
    return jax.ShapeDtypeStruct(shape, dtype)


def _rows(width, tile=ROW_TILE):
    return pl.BlockSpec((tile, width), lambda i: (i, 0))


def _frows(width):
    return _rows(width, FFN_TILE)


def _cols(height, tile):
    return pl.BlockSpec((height, tile), lambda i: (0, i))


def _const(shape):
    return pl.BlockSpec(shape, lambda i: (0,) * len(shape))


def _const1(shape):
    return pl.BlockSpec(shape, lambda i: (0,) * len(shape), pipeline_mode=pl.Buffered(1))


HALO_ROWS = 16


def _halo_prev(width, tile=ROW_TILE):
    return pl.BlockSpec((HALO_ROWS, width), lambda i: (jnp.maximum(i * (tile // HALO_ROWS) - 1, 0), 0))


def _halo_next(width, tile=ROW_TILE):
    return pl.BlockSpec((HALO_ROWS, width), lambda i: (jnp.minimum((i + 1) * (tile // HALO_ROWS), SEQ // HALO_ROWS - 1), 0))


def _rms(x):
    r = lax.rsqrt(jnp.mean(x * x, axis=-1, keepdims=True) + RMS_EPS)
    return x * r, r


def _rms_bwd(dn, y, r):
    return r * (dn - y * jnp.mean(dn * y, axis=-1, keepdims=True))


def _dot(a, b):
    return jnp.dot(a, b, preferred_element_type=F32)


def _dot_nt(a, b):
    return lax.dot_general(a, b, (((1,), (1,)), ((), ())), preferred_element_type=F32)


def _dot_tn(a, b):
    return lax.dot_general(a, b, (((0,), (0,)), ((), ())), preferred_element_type=F32)


def _place():
    return lax.axis_index("x"), lax.axis_index("y"), lax.axis_index("c")


def _block_of(px, py, pc):
    return 4 * px + 2 * py + pc


def _weight_block(ref, kind, blk):
    if kind == "in":
        return ref.at[:, pl.ds(blk * W_IN_BLOCK, W_IN_BLOCK)]
    if kind == "out":
        return ref.at[pl.ds(blk * W_OUT_BLOCK, W_OUT_BLOCK), :]
    if kind == "gu":
        return ref.at[blk]
    return ref.at[pl.ds(blk * W_DOWN_BLOCK, W_DOWN_BLOCK), :]


def _dma_semaphores(n):
    return [pltpu.SemaphoreType.DMA((n, 7)), pltpu.SemaphoreType.DMA((n, 7)), pltpu.SemaphoreType.DMA((n,))]


class _Gather:
    def __init__(self, kinds, shards):
        self.kinds, self.operands = tuple(kinds), list(shards)
        self.tag = "gather_" + "_".join(kinds)
        self.out_shape = [_sds(FULL_SHAPES[k], BF16) for k in kinds]
        self.scratch = _dma_semaphores(len(kinds))

    def _parties(self):
        x, y, c = _place()
        return (x, y, c), (x, y, 1 - c), [(1 - x, y), (x, 1 - y), (1 - x, 1 - y)], c

    def _copy(self, outs, sems, w, k, block, to, src=None):
        dst = _weight_block(outs[w], self.kinds[w], _block_of(*block))
        return pltpu.make_async_remote_copy(src_ref=dst if src is None else src, dst_ref=dst, send_sem=sems[0].at[w, k],
                                            recv_sem=sems[1].at[w, k], device_id=to, device_id_type=MESH)

    def _own(self, srcs, outs, sems, w, me):
        return pltpu.make_async_copy(srcs[w], _weight_block(outs[w], self.kinds[w], _block_of(*me)), sems[2].at[w])

    def _first(self, srcs, outs, sems, w):
        me, sibling, chips, c = self._parties()
        return [self._copy(outs, sems, w, 0, me, sibling, src=srcs[w])] + [
            self._copy(outs, sems, w, 1 + j, me, (*chip, c), src=srcs[w]) for j, chip in enumerate(chips)]

    def start(self, srcs, outs, sems):
        me = self._parties()[0]
        for w in range(len(self.kinds)):
            self._own(srcs, outs, sems, w, me).start()
            for cp in self._first(srcs, outs, sems, w):
                cp.start()

    def forward(self, srcs, outs, sems):
        me, sibling, chips, c = self._parties()
        for j, chip in enumerate(chips):
            for w in range(len(self.kinds)):
                self._copy(outs, sems, w, 1 + j, (*chip, c), me).wait_recv()
                self._copy(outs, sems, w, 4 + j, (*chip, c), sibling).start()

    def finish(self, srcs, outs, sems):
        me, sibling, chips, c = self._parties()
        for w in range(len(self.kinds)):
            self._copy(outs, sems, w, 0, sibling, me).wait_recv()
            for j, chip in enumerate(chips):
                self._copy(outs, sems, w, 4 + j, (*chip, 1 - c), me).wait_recv()
        for w in range(len(self.kinds)):
            for cp in self._first(srcs, outs, sems, w):
                cp.wait_send()
            for j, chip in enumerate(chips):
                self._copy(outs, sems, w, 4 + j, (*chip, c), sibling).wait_send()
            self._own(srcs, outs, sems, w, me).wait()


def _peers(x, y, c):
    return [(x ^ a, y ^ b, c ^ e) for a in (0, 1) for b in (0, 1) for e in (0, 1) if (a, b, e) != (0, 0, 0)]


class _Scatter:
    def __init__(self, kinds, grads):
        self.kinds, self.operands = tuple(kinds), list(grads)
        self.tag = "scatter_" + "_".join(kinds)
        self.out_shape = [_sds((N_DEV,) + SHARD_SHAPES[k], BF16) for k in kinds]
        self.scratch = _dma_semaphores(len(kinds))

    def _copies(self, srcs, outs, sems):
        x, y, c = _place()
        me = _block_of(x, y, c)
        copies = []
        for w, kind in enumerate(self.kinds):
            copies.append(pltpu.make_async_copy(_weight_block(srcs[w], kind, me), outs[w].at[me], sems[2].at[w]))
            for k, peer in enumerate(_peers(x, y, c)):
                copies.append(pltpu.make_async_remote_copy(
                    src_ref=_weight_block(srcs[w], kind, _block_of(*peer)), dst_ref=outs[w].at[me],
                    send_sem=sems[0].at[w, k], recv_sem=sems[1].at[w, k], device_id=peer, device_id_type=MESH))
        return copies

    def start(self, srcs, outs, sems):
        for cp in self._copies(srcs, outs, sems):
            cp.start()

    def forward(self, srcs, outs, sems):
        pass

    def finish(self, srcs, outs, sems):
        for cp in self._copies(srcs, outs, sems):
            cp.wait()


N_CHIPS = N_DEV // 2


class _ScatterChips:
    def __init__(self, kinds, blocks):
        self.kinds, self.operands = tuple(kinds), list(blocks)
        self.tag = "scatter_chips_" + "_".join(kinds)
        self.out_shape = [_sds((N_CHIPS,) + SHARD_SHAPES[k], BF16) for k in kinds]
        self.scratch = _dma_semaphores(len(kinds))

    def _copies(self, srcs, outs, sems):
        x, y, c = _place()
        mine = 2 * x + y
        copies = []
        for w in range(len(self.kinds)):
            copies.append(pltpu.make_async_copy(srcs[w].at[mine], outs[w].at[mine], sems[2].at[w]))
            for j, (px, py) in enumerate([(1 - x, y), (x, 1 - y), (1 - x, 1 - y)]):
                copies.append(pltpu.make_async_remote_copy(
                    src_ref=srcs[w].at[2 * px + py], dst_ref=outs[w].at[mine], send_sem=sems[0].at[w, j], recv_sem=sems[1].at[w, j],
                    device_id=(px, py, c), device_id_type=MESH))
        return copies

    def start(self, srcs, outs, sems):
        for cp in self._copies(srcs, outs, sems):
            cp.start()

    def forward(self, srcs, outs, sems):
        pass

    def finish(self, srcs, outs, sems):
        for cp in self._copies(srcs, outs, sems):
            cp.wait()


def _in_hbm(a):
    return pltpu.with_memory_space_constraint(a, pltpu.HBM) if a.size * a.dtype.itemsize >= PINNED_BYTES else a


def _out_hbm(s):
    return pltpu.HBM(s.shape, s.dtype) if math.prod(s.shape) * jnp.dtype(s.dtype).itemsize >= PINNED_BYTES else s


def _call(body, args, *, grid, in_specs, out_specs, out_shape, scratch_shapes=(), vmem_mib=None, name, comm=None):
    kwargs = {} if vmem_mib is None else dict(compiler_params=pltpu.CompilerParams(vmem_limit_bytes=vmem_mib * MIB))
    in_specs, out_specs, out_shape, scratch_shapes = list(in_specs), list(out_specs), list(out_shape), list(scratch_shapes)
    args = [_in_hbm(a) for a in args]
    out_shape = [_out_hbm(s) for s in out_shape]
    if comm is None:
        res = pl.pallas_call(body, grid=grid, in_specs=in_specs, out_specs=out_specs, out_shape=out_shape,
                             scratch_shapes=scratch_shapes, name=name, **kwargs)(*args)
        return list(res), None
    n_in, n_out, n_scr = len(in_specs), len(out_specs), len(scratch_shapes)
    c_in, c_out = len(comm.operands), len(comm.out_shape)
    last = math.prod(grid) - 1

    def carried(*refs):
        cuts = [n_in, c_in, n_out, c_out, n_scr]
        parts, at = [], 0
        for n in cuts:
            parts.append(refs[at:at + n])
            at += n
        ins, c_ins, outs, c_outs, scr = parts
        sems = refs[at:]
        step = pl.program_id(0)
        for axis in range(1, len(grid)):
            step = step * grid[axis] + pl.program_id(axis)

        @pl.when(step == 0)
        def _():
            comm.start(c_ins, c_outs, sems)

        @pl.when(step == last)
        def _():
            comm.forward(c_ins, c_outs, sems)

        body(*ins, *outs, *scr)

        @pl.when(step == last)
        def _():
            comm.finish(c_ins, c_outs, sems)

    res = pl.pallas_call(carried, grid=grid, in_specs=in_specs + [ANY] * c_in, out_specs=out_specs + [ANY] * c_out,
                         out_shape=out_shape + [_out_hbm(s) for s in comm.out_shape], scratch_shapes=scratch_shapes + comm.scratch,
                         name=name + "_" + comm.tag, **kwargs)(*args, *[_in_hbm(a) for a in comm.operands])
    return list(res[:n_out]), list(res[n_out:])


def _comm_only(comm, name):
    def body(*refs):
        n_in, n_out = len(comm.operands), len(comm.out_shape)
        srcs, outs, sems = refs[:n_in], refs[n_in:n_in + n_out], refs[n_in + n_out:]
        comm.start(srcs, outs, sems)
        comm.forward(srcs, outs, sems)
        comm.finish(srcs, outs, sems)

    return pl.pallas_call(body, in_specs=[ANY] * len(comm.operands), out_specs=[ANY] * len(comm.out_shape),
                          out_shape=[_out_hbm(s) for s in comm.out_shape], scratch_shapes=comm.scratch,
                          name=name)(*[_in_hbm(a) for a in comm.operands])


class _GatherSmall:
    def __init__(self, v, tag):
        self.operands, self.tag = [v], tag
        self.out_shape = [_sds((N_DEV,) + v.shape, F32)]
        self.scratch = _dma_semaphores(1)

    def _copies(self, srcs, outs, sems):
        x, y, c = _place()
        me = _block_of(x, y, c)
        copies = [pltpu.make_async_copy(srcs[0], outs[0].at[me], sems[2].at[0])]
        for k, peer in enumerate(_peers(x, y, c)):
            copies.append(pltpu.make_async_remote_copy(src_ref=srcs[0], dst_ref=outs[0].at[me], send_sem=sems[0].at[0, k],
                                                       recv_sem=sems[1].at[0, k], device_id=peer, device_id_type=MESH))
        return copies

    def start(self, srcs, outs, sems):
        for cp in self._copies(srcs, outs, sems):
            cp.start()

    def forward(self, srcs, outs, sems):
        pass

    def finish(self, srcs, outs, sems):
        for cp in self._copies(srcs, outs, sems):
            cp.wait()


class _Together:
    def __init__(self, *comms):
        self.comms = comms
        self.operands = [a for cm in comms for a in cm.operands]
        self.out_shape = [s for cm in comms for s in cm.out_shape]
        self.scratch = [s for cm in comms for s in cm.scratch]
        self.tag = "_".join(cm.tag for cm in comms)

    def _each(self, srcs, outs, sems):
        a = o = s = 0
        for cm in self.comms:
            na, no, ns = len(cm.operands), len(cm.out_shape), len(cm.scratch)
            yield cm, srcs[a:a + na], outs[o:o + no], sems[s:s + ns]
            a, o, s = a + na, o + no, s + ns

    def start(self, srcs, outs, sems):
        for cm, *refs in self._each(srcs, outs, sems):
            cm.start(*refs)

    def forward(self, srcs, outs, sems):
        for cm, *refs in self._each(srcs, outs, sems):
            cm.forward(*refs)

    def finish(self, srcs, outs, sems):
        for cm, *refs in self._each(srcs, outs, sems):
            cm.finish(*refs)


def _rope_tables(pos_col, freq_row, comm=None):
    def body(p_ref, f_ref, c_ref, sa_ref, sb_ref):
        ang = p_ref[...].astype(F32) * f_ref[...]
        lane = lax.broadcasted_iota(jnp.int32, ang.shape, 1) % HEAD_DIM
        cos, sin = jnp.cos(ang), jnp.sin(ang)
        c_ref[...] = jnp.where(lane < ROPE_DIM, cos, 1.0)
        sa_ref[...] = jnp.where(lane < ROPE_DIM // 2, -sin, 0.0)
        sb_ref[...] = jnp.where((lane >= ROPE_DIM // 2) & (lane < ROPE_DIM), sin, 0.0)

    return _call(body, (pos_col, freq_row), grid=(SEQ // ROW_TILE,), in_specs=[_rows(1), _const((1, LANES))],
                 out_specs=[_rows(LANES)] * 3, out_shape=[_sds((SEQ, LANES), F32)] * 3, name="rope_tables", comm=comm)


def _rotate(t, c, sa, sb):
    parts = []
    for g in range(ATTN_WIDTH // LANES):
        tg = t[:, g * LANES:(g + 1) * LANES]
        parts.append(tg * c + pltpu.roll(tg, LANES - 8, axis=1) * sa + pltpu.roll(tg, 8, axis=1) * sb)
    return jnp.concatenate(parts, axis=1)


def _rotate_transposed(dt, c, sa, sb):
    parts = []
    for g in range(ATTN_WIDTH // LANES):
        dg = dt[:, g * LANES:(g + 1) * LANES]
        parts.append(dg * c + pltpu.roll(dg * sa, 8, axis=1) + pltpu.roll(dg * sb, LANES - 8, axis=1))
    return jnp.concatenate(parts, axis=1)


def _inproj_fwd(x, g_pre, w_in, tc, tsa, tsb, comm=None):
    def body(x_ref, g_ref, w_ref, c_ref, sa_ref, sb_ref, q_ref, k_ref, v_ref, u_ref, b_ref, cc_ref):
        y, _ = _rms(x_ref[...])
        h = (y * g_ref[...]).astype(BF16)

        def proj(n):
            return _dot(h, w_ref[:, n * ATTN_WIDTH:(n + 1) * ATTN_WIDTH])

        c, sa, sb = c_ref[...], sa_ref[...], sb_ref[...]
        q_ref[...] = (_rotate(proj(0), c, sa, sb) * (HEAD_DIM ** -0.5)).astype(BF16)
        k_ref[...] = _rotate(proj(1), c, sa, sb).astype(BF16)
        v_ref[...] = proj(2).astype(BF16)
        u_ref[...] = proj(3).astype(BF16)
        b_ref[...] = proj(4).astype(BF16)
        cc_ref[...] = proj(5).astype(BF16)

    a = ATTN_WIDTH
    return _call(
        body, (x, g_pre, w_in, tc, tsa, tsb), grid=(SEQ // ROW_TILE,),
        in_specs=[_rows(D_MODEL), _const((1, D_MODEL)), _const1((D_MODEL, IN_PROJ_WIDTH)), _rows(LANES), _rows(LANES), _rows(LANES)],
        out_specs=[_rows(a)] * 6, out_shape=[_sds((SEQ, a), BF16)] * 6,
        vmem_mib=40, name="inproj_fwd", comm=comm)


def _head_masks():
    lane = lax.broadcasted_iota(jnp.int32, (1, LANES), 1)
    first = lane < HEAD_DIM
    return first, first.astype(F32), 1.0 - first.astype(F32)


def _perm_chunks(dil):
    length = SEQ // dil
    out = []
    for r in range(dil):
        for c0 in range(0, length, PERM_CHUNK):
            chunk = (r * length + c0) // PERM_CHUNK
            rows = pl.ds(c0, PERM_CHUNK) if dil == 1 else pl.ds(r + dil * c0, PERM_CHUNK, stride=dil)
            out.append((chunk, rows))
    return out


def _chunk(c, offset=0):
    return pl.ds(offset + c * PERM_CHUNK, PERM_CHUNK)


def _write_band_bias(bias_ref):
    rr = lax.broadcasted_iota(jnp.int32, (Q_BLOCK, K_WINDOW), 0)
    cc = lax.broadcasted_iota(jnp.int32, (Q_BLOCK, K_WINDOW), 1)
    band = (cc >= rr) & (cc - rr <= 2 * HALF_WINDOW)
    bias_ref[0] = jnp.where(band, 0.0, NEG_INF)
    bias_ref[1] = jnp.where(band & (cc >= HALF_WINDOW), 0.0, NEG_INF)
    bias_ref[2] = jnp.where(band & (cc < Q_BLOCK + HALF_WINDOW), 0.0, NEG_INF)


def _band_bias_index(m0, length):
    return jnp.where(m0 % length == 0, 1, 0) + jnp.where((m0 + Q_BLOCK) % length == 0, 2, 0)


def _zero_key_padding(bufs):
    pad = jnp.zeros((HALF_WINDOW, LANES), BF16)
    for buf in bufs:
        buf[pl.ds(0, HALF_WINDOW), :] = pad
        buf[pl.ds(SEQ + HALF_WINDOW, HALF_WINDOW), :] = pad


def _attn_fwd(q, k, v, comm=None):
    group = 8

    def body(q_ref, k_ref, v_ref, o_ref, lse_ref, q32, k32, v32, qa, qb, kp, vp, accp, mlp,
             acc0, acc1, acc2, ml0, ml1, ml2, bias_ref):
        first, mask_a, mask_b = _head_masks()
        low = lax.broadcasted_iota(jnp.int32, (1, LANES), 1) % HEAD_DIM < HEAD_DIM // 2
        _zero_key_padding((kp, vp))
        _write_band_bias(bias_ref)
        q32[...] = q_ref[...].astype(F32)
        k32[...] = k_ref[...].astype(F32)
        v32[...] = v_ref[...].astype(F32)
        natural = ((acc0, ml0), (acc1, ml1), (acc2, ml2))

        for branch, dil in enumerate(DILATIONS):
            length = SEQ // dil
            assert length >= 2 * Q_BLOCK
            chunks = _perm_chunks(dil)
            for c, rows in chunks:
                val = q32[rows, :]
                qa[_chunk(c), :] = (val * mask_a).astype(BF16)
                qb[_chunk(c), :] = (val * mask_b).astype(BF16)
                kp[_chunk(c, HALF_WINDOW), :] = k32[rows, :].astype(BF16)
                vp[_chunk(c, HALF_WINDOW), :] = v32[rows, :].astype(BF16)
            acc_dst, ml_dst = natural[branch] if dil == 1 else (accp, mlp)

            def blocks(i, carry, length=length, acc_dst=acc_dst, ml_dst=ml_dst):
                base = pl.multiple_of(i * (group * Q_BLOCK), group * Q_BLOCK)
                starts = [base + g * Q_BLOCK for g in range(group)]
                scores = [[_dot_nt(qx[pl.ds(m0, Q_BLOCK), :], kp[pl.ds(m0, K_WINDOW), :]) for qx in (qa, qb)] for m0 in starts]
                probs = []
                for m0, pair in zip(starts, scores):
                    bias = bias_ref[_band_bias_index(m0, length)]
                    stats, ps = [], []
                    for s in pair:
                        s = s + bias
                        m = jnp.max(s, axis=1, keepdims=True)
                        p = jnp.exp(s - m)
                        stats.append(jnp.where(low, m, jnp.sum(p, axis=1, keepdims=True)))
                        ps.append(p.astype(BF16))
                    ml_dst[pl.ds(m0, Q_BLOCK), :] = jnp.where(first, stats[0], stats[1])
                    probs.append(ps)
                for m0, ps in zip(starts, probs):
                    vw = vp[pl.ds(m0, K_WINDOW), :]
                    acc_dst[pl.ds(m0, Q_BLOCK), :] = jnp.where(first, _dot(ps[0], vw), _dot(ps[1], vw))
                return carry

            lax.fori_loop(0, SEQ // (group * Q_BLOCK), blocks, 0)

            if dil > 1:
                for c, rows in chunks:
                    natural[branch][0][rows, :] = accp[_chunk(c), :]
                    natural[branch][1][rows, :] = mlp[_chunk(c), :]

        for c in range(SEQ // PERM_CHUNK):
            packed = [ml[_chunk(c), :] for _, ml in natural]
            ms = [jnp.where(low, ml, pltpu.roll(ml, HEAD_DIM // 2, axis=1)) for ml in packed]
            ls = [jnp.where(low, pltpu.roll(ml, LANES - HEAD_DIM // 2, axis=1), ml) for ml in packed]
            m_all = jnp.maximum(jnp.maximum(ms[0], ms[1]), ms[2])
            es = [jnp.exp(m - m_all) for m in ms]
            z = ls[0] * es[0] + ls[1] * es[1] + ls[2] * es[2]
            num = natural[0][0][_chunk(c), :] * es[0] + natural[1][0][_chunk(c), :] * es[1] + natural[2][0][_chunk(c), :] * es[2]
            o_ref[_chunk(c), :] = num / z
            lse_ref[_chunk(c), :] = m_all + jnp.log(z)

    col = pl.BlockSpec((SEQ, LANES), lambda h: (0, h))
    padded = SEQ + 2 * HALF_WINDOW
    return _call(
        body, (q, k, v), grid=(ATTN_WIDTH // LANES,), in_specs=[col] * 3, out_specs=[col] * 2,
        out_shape=[_sds((SEQ, ATTN_WIDTH), F32)] * 2,
        scratch_shapes=[pltpu.VMEM((SEQ, LANES), F32)] * 3 + [pltpu.VMEM((SEQ, LANES), BF16)] * 2
        + [pltpu.VMEM((padded, LANES), BF16)] * 2 + [pltpu.VMEM((SEQ, LANES), F32)] * 8
        + [pltpu.VMEM((3, Q_BLOCK, K_WINDOW), F32)],
        vmem_mib=52, name="attn_fwd", comm=comm)


def _shifted(t, before, after, i):
    tile = t.shape[0]
    row = lax.broadcasted_iota(jnp.int32, (tile, 1), 0)
    before = jnp.where(i > 0, before, 0.0)
    after = jnp.where(i < SEQ // tile - 1, after, 0.0)
    return (jnp.where(row == 0, before, pltpu.roll(t, 1, axis=0)),
            jnp.where(row == tile - 1, after, pltpu.roll(t, tile - 1, axis=0)))


def _last_row(ref):
    return ref[HALO_ROWS - 1:HALO_ROWS, :].astype(F32)


def _first_row(ref):
    return ref[0:1, :].astype(F32)


def _conv_parts(u, c, u_prev, c_prev, u_next, c_next, cw, i):
    t = c * u
    t_prev, t_next = _shifted(t, _last_row(c_prev) * _last_row(u_prev), _first_row(c_next) * _first_row(u_next), i)
    s = cw[0:1, :] * t_prev + cw[1:2, :] * t + cw[2:3, :] * t_next
    return t, t_prev, t_next, s


def _mix_fwd(attn, u, b, c, conv_w, g_attn, g_conv, w_out, x, g_post):
    def body(a_ref, u_ref, b_ref, c_ref, up_ref, cp_ref, un_ref, cn_ref, cw_ref, ga_ref, gc_ref, w_ref, x_ref, gp_ref,
             x1_ref, mg_ref, mix_ref):
        i = pl.program_id(0)
        _, _, _, s = _conv_parts(u_ref[...].astype(F32), c_ref[...].astype(F32), up_ref, cp_ref, un_ref, cn_ref, cw_ref[...], i)
        ya, _ = _rms(a_ref[...])
        yc, _ = _rms(b_ref[...].astype(F32) * s)
        merged = jnp.concatenate([ya * ga_ref[...], yc * gc_ref[...]], axis=1).astype(BF16)
        mix = _dot(merged, w_ref[...])
        ym, _ = _rms(mix)
        mg_ref[...] = merged.T
        mix_ref[...] = mix.astype(BF16)
        x1_ref[...] = x_ref[...] + ym * gp_ref[...]

    a = ATTN_WIDTH
    return _call(
        body, (attn, u, b, c, u, c, u, c, conv_w, g_attn, g_conv, w_out, x, g_post), grid=(SEQ // ROW_TILE,),
        in_specs=[_rows(a)] * 4 + [_halo_prev(a)] * 2 + [_halo_next(a)] * 2
        + [_const((3, a)), _const((1, a)), _const((1, a)), _const1((D_MODEL, D_MODEL)), _rows(D_MODEL), _const((1, D_MODEL))],
        out_specs=[_rows(D_MODEL), _cols(D_MODEL, ROW_TILE), _rows(D_MODEL)],
        out_shape=[_sds((SEQ, D_MODEL), F32), _sds((D_MODEL, SEQ), BF16), _sds((SEQ, D_MODEL), BF16)],
        vmem_mib=40, name="mix_fwd")[0]


def _gu_spec():
    return pl.BlockSpec((N_DEV, FFN_TILE, FFN_BLOCK), lambda i: (0, i, 0))


def _ffn_fwd(x1, g_pre, w_gu, w_dn, g_post, comm=None, target=None):
    n_tiles = SEQ // ROW_TILE

    def body(*refs):
        if target is None:
            x_ref, g_ref, wgu_ref, wdn_ref, gp_ref, x2_ref, gu_ref, f_ref = refs
        else:
            x_ref, g_ref, wgu_ref, wdn_ref, gp_ref, t_ref, x2_ref, gu_ref, f_ref, loss_ref, acc = refs
        x1 = x_ref[...]
        y, _ = _rms(x1)
        h = (y * g_ref[...]).astype(BF16)
        f = jnp.zeros((ROW_TILE, D_MODEL), F32)

        def gate_up(j):
            return _dot_nt(h, wgu_ref[j]), _dot_nt(h, wgu_ref[j + N_DEV // 2])

        ahead = gate_up(0)
        for j in range(N_DEV // 2):
            gate, up = ahead
            if j + 1 < N_DEV // 2:
                ahead = gate_up(j + 1)
            gu_ref[j] = gate.astype(BF16)
            gu_ref[j + N_DEV // 2] = up.astype(BF16)
            act = (gate * jax.nn.sigmoid(gate) * up).astype(BF16)
            f = f + _dot(act, wdn_ref[pl.ds(j * FFN_BLOCK, FFN_BLOCK), :])
        yf, _ = _rms(f)
        f_ref[...] = f
        x2 = x1 + yf * gp_ref[...]
        if target is None:
            x2_ref[...] = x2
        else:
            i = pl.program_id(0)
            err = x2 - t_ref[...]
            x2_ref[...] = err * (1.0 / D_MODEL)

            @pl.when(i == 0)
            def _():
                acc[...] = jnp.zeros_like(acc)

            acc[...] += jnp.sum(err * err, axis=0, keepdims=True)

            @pl.when(i == n_tiles - 1)
            def _():
                loss_ref[...] = jnp.sum(acc[...], axis=1, keepdims=True) * (0.5 / D_MODEL)

    with_loss = target is not None
    return _call(
        body, (x1, g_pre, w_gu, w_dn, g_post) + ((target,) if with_loss else ()), grid=(n_tiles,),
        in_specs=[_rows(D_MODEL), _const((1, D_MODEL)), _const1((N_DEV, FFN_BLOCK, D_MODEL)), _const1((FFN_HIDDEN, D_MODEL)),
                  _const((1, D_MODEL))] + ([_rows(D_MODEL)] if with_loss else []),
        out_specs=[_rows(D_MODEL), pl.BlockSpec((N_DEV, ROW_TILE, FFN_BLOCK), lambda i: (0, i, 0)), _rows(D_MODEL)]
        + ([_const((1, 1))] if with_loss else []),
        out_shape=[_sds((SEQ, D_MODEL), F32), _sds((N_DEV, SEQ, FFN_BLOCK), BF16), _sds((SEQ, D_MODEL), F32)]
        + ([_sds((1, 1), F32)] if with_loss else []),
        scratch_shapes=[pltpu.VMEM((1, D_MODEL), F32)] if with_loss else [],
        vmem_mib=58, name="ffn_fwd_loss" if with_loss else "ffn_fwd", comm=comm)


def _accumulate(pairs, i):
    @pl.when(i == 0)
    def _():
        for ref, value in pairs:
            ref[...] = value

    @pl.when(i > 0)
    def _():
        for ref, value in pairs:
            ref[...] += value


def _colsum(v):
    return jnp.sum(v, axis=0, keepdims=True)


def _ffn_bwd(dx2, f, x1, gu, w_gu, w_dn, g_post, g_pre, comm=None):
    half = N_DEV // 2

    def body(dx2_ref, f_ref, x1_ref, gu_ref, wgu_ref, wdn_ref, gpost_ref, gpre_ref,
             dx1_ref, df_ref, act_ref, dgu_ref, h_ref, dgpost_ref, dgpre_ref):
        i = pl.program_id(0)
        dx2 = dx2_ref[...]
        yf, rf = _rms(f_ref[...])
        dg_post = _colsum(dx2 * yf)
        df = _rms_bwd(dx2 * gpost_ref[...], yf, rf).astype(BF16)
        df_ref[...] = df
        dh = jnp.zeros((FFN_TILE, D_MODEL), F32)

        def d_act(j):
            return _dot_nt(df, wdn_ref[pl.ds(j * FFN_BLOCK, FFN_BLOCK), :])

        ahead = d_act(0)
        for j in range(half):
            dact = ahead
            if j + 1 < half:
                ahead = d_act(j + 1)
            gate = gu_ref[j].astype(F32)
            up = gu_ref[j + half].astype(F32)
            sig = jax.nn.sigmoid(gate)
            silu = gate * sig
            act_ref[j] = (silu * up).astype(BF16)
            dgate = (dact * up * (sig * (1.0 + gate * (1.0 - sig)))).astype(BF16)
            dup = (dact * silu).astype(BF16)
            dgu_ref[j] = dgate
            dgu_ref[j + half] = dup
            dh = dh + _dot(dgate, wgu_ref[j]) + _dot(dup, wgu_ref[j + half])
        y1, r1 = _rms(x1_ref[...])
        h_ref[...] = (y1 * gpre_ref[...]).astype(BF16)
        dx1_ref[...] = dx2 + _rms_bwd(dh * gpre_ref[...], y1, r1)
        _accumulate([(dgpost_ref, dg_post), (dgpre_ref, _colsum(dh * y1))], i)

    act_spec = pl.BlockSpec((half, FFN_TILE, FFN_BLOCK), lambda i: (0, i, 0))
    return _call(
        body, (dx2, f, x1, gu, w_gu, w_dn, g_post, g_pre), grid=(SEQ // FFN_TILE,),
        in_specs=[_frows(D_MODEL)] * 3 + [_gu_spec(), _const1((N_DEV, FFN_BLOCK, D_MODEL)), _const1((FFN_HIDDEN, D_MODEL)),
                                          _const((1, D_MODEL)), _const((1, D_MODEL))],
        out_specs=[_frows(D_MODEL), _frows(D_MODEL), act_spec, _gu_spec(), _frows(D_MODEL), _const((1, D_MODEL)), _const((1, D_MODEL))],
        out_shape=[_sds((SEQ, D_MODEL), F32), _sds((SEQ, D_MODEL), BF16), _sds((half, SEQ, FFN_BLOCK), BF16),
                   _sds((N_DEV, SEQ, FFN_BLOCK), BF16), _sds((SEQ, D_MODEL), BF16), _sds((1, D_MODEL), F32), _sds((1, D_MODEL), F32)],
        vmem_mib=52, name="ffn_bwd", comm=comm)


def _wgrad(a_t, g, name, comm=None):
    width = g.shape[1]

    def body(a_ref, g_ref, o_ref):
        o_ref[...] = _dot(a_ref[...], g_ref[...]).astype(BF16)

    res, landed = _call(body, (a_t, g), grid=(width // WGRAD_COLS,),
                        in_specs=[_const1((D_MODEL, SEQ)), pl.BlockSpec((SEQ, WGRAD_COLS), lambda j: (0, j))],
                        out_specs=[pl.BlockSpec((D_MODEL, WGRAD_COLS), lambda j: (0, j))], out_shape=[_sds((D_MODEL, width), BF16)],
                        vmem_mib=48, name=name, comm=comm)
    return res[0], landed


def _wgrad_paired(a, g, kind, comm=None):
    shard = SHARD_SHAPES[kind]
    per_chip = 2 if kind == "gu" else 1
    n_compute = per_chip * N_CHIPS
    n_steps = n_compute + 1

    def block(t):
        return jnp.minimum(t, n_compute - 1)

    if kind == "in":
        specs = [_const1((D_MODEL, SEQ)), pl.BlockSpec((SEQ, 2 * W_IN_BLOCK), lambda t: (0, block(t)))]
    else:
        specs = [pl.BlockSpec((1, SEQ, FFN_BLOCK), lambda t: (block(t), 0, 0)), _const1((SEQ, D_MODEL))]

    def body(a_ref, g_ref, o_ref, sendbuf, recvbuf, keepbuf, send_sem, recv_sem):
        t = pl.program_id(0)
        x, y, c = _place()

        def exchange(q):
            return pltpu.make_async_remote_copy(src_ref=sendbuf.at[q], dst_ref=recvbuf.at[q], send_sem=send_sem.at[q],
                                                recv_sem=recv_sem.at[q], device_id=(x, y, 1 - c), device_id_type=MESH)

        @pl.when((t >= per_chip) & (t % per_chip == 0))
        def _():
            q = t // per_chip - 1
            exchange(q).wait_recv()
            o_ref[0] = (keepbuf[...] + recvbuf[q].astype(F32)).astype(BF16)

        @pl.when(t < n_compute)
        def _():
            q = t // per_chip
            if kind == "gu":
                r = _dot_tn(a_ref[0], g_ref[...])

                @pl.when(t % 2 == c)
                def _():
                    keepbuf[...] = r

                @pl.when(t % 2 != c)
                def _():
                    sendbuf[q] = r.astype(BF16)
                    exchange(q).start()
            else:
                if kind == "dn":
                    r = _dot_tn(a_ref[0], g_ref[...])
                    lower, upper = r[:W_DOWN_BLOCK], r[W_DOWN_BLOCK:]
                else:
                    r = _dot(a_ref[...], g_ref[...])
                    lower, upper = r[:, :W_IN_BLOCK], r[:, W_IN_BLOCK:]
                keepbuf[...] = jnp.where(c == 0, lower, upper)
                sendbuf[q] = jnp.where(c == 0, upper, lower).astype(BF16)
                exchange(q).start()

        @pl.when(t == n_steps - 1)
        def _():
            for q in range(N_CHIPS):
                exchange(q).wait_send()

    res, landed = _call(
        body, (a, g), grid=(n_steps,), in_specs=specs,
        out_specs=[pl.BlockSpec((1,) + shard, lambda t: (jnp.clip(t // per_chip - 1, 0, N_CHIPS - 1), 0, 0))],
        out_shape=[_sds((N_CHIPS,) + shard, BF16)],
        scratch_shapes=[pltpu.VMEM((N_CHIPS,) + shard, BF16), pltpu.VMEM((N_CHIPS,) + shard, BF16), pltpu.VMEM(shard, F32),
                        pltpu.SemaphoreType.DMA((N_CHIPS,)), pltpu.SemaphoreType.DMA((N_CHIPS,))],
        vmem_mib=52, name="wgrad_" + kind + "_paired", comm=comm)
    return res[0], landed


def _mix_bwd(dx1, mix, attn, u, b, c, conv_w, g_attn, g_conv, g_post, w_out):
    def body(dx1_ref, mix_ref, a_ref, u_ref, b_ref, c_ref, up_ref, cp_ref, un_ref, cn_ref, cw_ref, ga_ref, gc_ref, gp_ref, w_ref,
             dmix_ref, da_ref, ds_ref, db_ref, dgp_ref, dga_ref, dgc_ref):
        i = pl.program_id(0)
        dx1 = dx1_ref[...]
        ym, rm = _rms(mix_ref[...].astype(F32))
        dg_post = _colsum(dx1 * ym)
        dmix = _rms_bwd(dx1 * gp_ref[...], ym, rm).astype(BF16)
        dmix_ref[...] = dmix
        dmerged = _dot_nt(dmix, w_ref[...])
        dna, dnc = dmerged[:, :ATTN_WIDTH], dmerged[:, ATTN_WIDTH:]
        ya, ra = _rms(a_ref[...])
        da_ref[...] = _rms_bwd(dna * ga_ref[...], ya, ra)
        _, _, _, s = _conv_parts(u_ref[...].astype(F32), c_ref[...].astype(F32), up_ref, cp_ref, un_ref, cn_ref, cw_ref[...], i)
        gate_b = b_ref[...].astype(F32)
        yc, rc = _rms(gate_b * s)
        dy = _rms_bwd(dnc * gc_ref[...], yc, rc)
        db_ref[...] = (dy * s).astype(BF16)
        ds_ref[...] = (dy * gate_b).astype(BF16)
        _accumulate([(dgp_ref, dg_post), (dga_ref, _colsum(dna * ya)), (dgc_ref, _colsum(dnc * yc))], i)

    a = ATTN_WIDTH
    return _call(
        body, (dx1, mix, attn, u, b, c, u, c, u, c, conv_w, g_attn, g_conv, g_post, w_out), grid=(SEQ // ROW_TILE,),
        in_specs=[_rows(D_MODEL)] * 2 + [_rows(a)] * 4 + [_halo_prev(a)] * 2 + [_halo_next(a)] * 2
        + [_const((3, a)), _const((1, a)), _const((1, a)), _const((1, D_MODEL)), _const1((D_MODEL, D_MODEL))],
        out_specs=[_rows(D_MODEL)] + [_rows(a)] * 3 + [_const((1, D_MODEL)), _const((1, a)), _const((1, a))],
        out_shape=[_sds((SEQ, D_MODEL), BF16), _sds((SEQ, a), F32), _sds((SEQ, a), BF16), _sds((SEQ, a), BF16),
                   _sds((1, D_MODEL), F32), _sds((1, a), F32), _sds((1, a), F32)],
        vmem_mib=40, name="mix_bwd")[0]


def _attn_bwd(q, k, v, do, o, lse, comm=None):
    group = 8

    def body(q_ref, k_ref, v_ref, do_ref, o_ref, lse_ref, dq_ref, dk_ref, dv_ref,
             q32, k32, v32, qa, qb, doa, dob, kp, vp, lsep, dlp, dnat, dqp, dkp, dvp, bias_ref):
        first, mask_a, mask_b = _head_masks()
        _zero_key_padding((kp, vp))
        _write_band_bias(bias_ref)
        q32[...] = q_ref[...].astype(F32)
        k32[...] = k_ref[...].astype(F32)
        v32[...] = v_ref[...].astype(F32)
        for c in range(SEQ // PERM_CHUNK):
            prod = do_ref[_chunk(c), :] * o_ref[_chunk(c), :]
            d_a = jnp.sum(prod * mask_a, axis=1, keepdims=True)
            d_b = jnp.sum(prod * mask_b, axis=1, keepdims=True)
            dnat[_chunk(c), :] = jnp.where(first, d_a, d_b)

        for step, dil in enumerate(DILATIONS[1:] + DILATIONS[:1]):
            length = SEQ // dil
            assert length >= 2 * Q_BLOCK
            chunks = _perm_chunks(dil)
            for c, rows in chunks:
                val = q32[rows, :]
                qa[_chunk(c), :] = (val * mask_a).astype(BF16)
                qb[_chunk(c), :] = (val * mask_b).astype(BF16)
                val = do_ref[rows, :]
                doa[_chunk(c), :] = (val * mask_a).astype(BF16)
                dob[_chunk(c), :] = (val * mask_b).astype(BF16)
                kp[_chunk(c, HALF_WINDOW), :] = k32[rows, :].astype(BF16)
                vp[_chunk(c, HALF_WINDOW), :] = v32[rows, :].astype(BF16)
                lsep[_chunk(c), :] = lse_ref[rows, :]
                dlp[_chunk(c), :] = dnat[rows, :]
            zero = jnp.zeros((PERM_CHUNK, LANES), F32)
            for c in range(SEQ // PERM_CHUNK):
                dkp[_chunk(c), :] = zero
                dvp[_chunk(c), :] = zero
            dkp[pl.ds(SEQ, 2 * HALF_WINDOW), :] = zero[:2 * HALF_WINDOW]
            dvp[pl.ds(SEQ, 2 * HALF_WINDOW), :] = zero[:2 * HALF_WINDOW]

            heads = ((qa, doa, 0), (qb, dob, HEAD_DIM))

            def blocks(i, carry, length=length):
                base = pl.multiple_of(i * (group * Q_BLOCK), group * Q_BLOCK)
                starts = [base + g * Q_BLOCK for g in range(group)]
                raw = [[(_dot_nt(qx[pl.ds(m0, Q_BLOCK), :], kp[pl.ds(m0, K_WINDOW), :]),
                         _dot_nt(dox[pl.ds(m0, Q_BLOCK), :], vp[pl.ds(m0, K_WINDOW), :])) for qx, dox, _ in heads]
                       for m0 in starts]
                grads = []
                for m0, pair in zip(starts, raw):
                    bias = bias_ref[_band_bias_index(m0, length)]
                    lse_b, d_b = lsep[pl.ds(m0, Q_BLOCK), :], dlp[pl.ds(m0, Q_BLOCK), :]
                    out = []
                    for (s, dp), (_, _, col) in zip(pair, heads):
                        p = jnp.exp(s + bias - lse_b[:, col:col + 1])
                        out.append(((p * (dp - d_b[:, col:col + 1])).astype(BF16), p.astype(BF16)))
                    grads.append(out)
                for m0, out in zip(starts, grads):
                    qrows, krows = pl.ds(m0, Q_BLOCK), pl.ds(m0, K_WINDOW)
                    kw = kp[krows, :]
                    dk = jnp.zeros((K_WINDOW, LANES), F32)
                    dv = jnp.zeros((K_WINDOW, LANES), F32)
                    for (ds, p), (qx, dox, _) in zip(out, heads):
                        dk = dk + _dot_tn(ds, qx[qrows, :])
                        dv = dv + _dot_tn(p, dox[qrows, :])
                    dqp[qrows, :] = jnp.where(first, _dot(out[0][0], kw), _dot(out[1][0], kw)) * (HEAD_DIM ** -0.5)
                    dkp[krows, :] += dk
                    dvp[krows, :] += dv
                return carry

            lax.fori_loop(0, SEQ // (group * Q_BLOCK), blocks, 0)

            for c, rows in chunks:
                g_q, g_k, g_v = dqp[_chunk(c), :], dkp[_chunk(c, HALF_WINDOW), :], dvp[_chunk(c, HALF_WINDOW), :]
                if step == 0:
                    dq_ref[rows, :] = g_q
                    dk_ref[rows, :] = g_k
                    dv_ref[rows, :] = g_v
                else:
                    dq_ref[rows, :] = dq_ref[rows, :] + g_q
                    dk_ref[rows, :] = dk_ref[rows, :] + g_k
                    dv_ref[rows, :] = dv_ref[rows, :] + g_v

    col = pl.BlockSpec((SEQ, LANES), lambda h: (0, h))
    col1 = pl.BlockSpec((SEQ, LANES), lambda h: (0, h), pipeline_mode=pl.Buffered(1))
    padded = SEQ + 2 * HALF_WINDOW
    return _call(
        body, (q, k, v, do, o, lse), grid=(ATTN_WIDTH // LANES,), in_specs=[col1] * 6, out_specs=[col] * 3,
        out_shape=[_sds((SEQ, ATTN_WIDTH), F32)] * 3,
        scratch_shapes=[pltpu.VMEM((SEQ, LANES), F32)] * 3 + [pltpu.VMEM((SEQ, LANES), BF16)] * 4
        + [pltpu.VMEM((padded, LANES), BF16)] * 2 + [pltpu.VMEM((SEQ, LANES), F32)] * 4 + [pltpu.VMEM((padded, LANES), F32)] * 2
        + [pltpu.VMEM((3, Q_BLOCK, K_WINDOW), F32)],
        vmem_mib=56, name="attn_bwd", comm=comm)


def _inproj_bwd(dq, dk, dv, ds, db, u, c, conv_w, tc, tsa, tsb, w_in, x, g_pre, dx1):
    def body(dq_ref, dk_ref, dv_ref, ds_ref, db_ref, u_ref, c_ref, dsp_ref, up_ref, cp_ref, dsn_ref, un_ref, cn_ref, cw_ref,
             tc_ref, tsa_ref, tsb_ref, w_ref, x_ref, g_ref, dx1_ref, dx_ref, dproj_ref, h_ref, dg_ref, dcw_ref):
        i = pl.program_id(0)
        cw = cw_ref[...]
        u, c = u_ref[...].astype(F32), c_ref[...].astype(F32)
        t, t_prev, t_next, _ = _conv_parts(u, c, up_ref, cp_ref, un_ref, cn_ref, cw, i)
        ds = ds_ref[...].astype(F32)
        ds_prev, ds_next = _shifted(ds, _last_row(dsp_ref), _first_row(dsn_ref), i)
        dt = cw[0:1, :] * ds_next + cw[1:2, :] * ds + cw[2:3, :] * ds_prev
        d_taps = jnp.concatenate([_colsum(ds * t_prev), _colsum(ds * t), _colsum(ds * t_next)], axis=0)
        tc_, tsa_, tsb_ = tc_ref[...], tsa_ref[...], tsb_ref[...]
        groups = ((2, lambda: dv_ref[...].astype(BF16)), (4, lambda: db_ref[...]),
                  (0, lambda: _rotate_transposed(dq_ref[...], tc_, tsa_, tsb_).astype(BF16)),
                  (1, lambda: _rotate_transposed(dk_ref[...], tc_, tsa_, tsb_).astype(BF16)),
                  (3, lambda: (dt * c).astype(BF16)), (5, lambda: (dt * u).astype(BF16)))
        dh = jnp.zeros((x_ref.shape[0], D_MODEL), F32)
        for n, make in groups:
            cols = pl.ds(n * a, a)
            part = make()
            dproj_ref[:, cols] = part
            dh = dh + _dot_nt(part, w_ref[:, cols])
        y, r = _rms(x_ref[...])
        h_ref[...] = (y * g_ref[...]).astype(BF16).T
        dx_ref[...] = dx1_ref[...] + _rms_bwd(dh * g_ref[...], y, r)
        _accumulate([(dg_ref, _colsum(dh * y)), (dcw_ref, d_taps)], i)

    a = ATTN_WIDTH
    tile = FFN_TILE

    def rows(width):
        return _rows(width, tile)

    return _call(
        body, (dq, dk, dv, ds, db, u, c, ds, u, c, ds, u, c, conv_w, tc, tsa, tsb, w_in, x, g_pre, dx1), grid=(SEQ // tile,),
        in_specs=[rows(a)] * 7 + [_halo_prev(a, tile)] * 3 + [_halo_next(a, tile)] * 3
        + [_const((3, a)), rows(LANES), rows(LANES), rows(LANES), _const1((D_MODEL, IN_PROJ_WIDTH)), rows(D_MODEL),
           _const((1, D_MODEL)), rows(D_MODEL)],
        out_specs=[rows(D_MODEL), rows(IN_PROJ_WIDTH), _cols(D_MODEL, tile), _const((1, D_MODEL)), _const((3, a))],
        out_shape=[_sds((SEQ, D_MODEL), F32), _sds((SEQ, IN_PROJ_WIDTH), BF16), _sds((D_MODEL, SEQ), BF16),
                   _sds((1, D_MODEL), F32), _sds((3, a), F32)],
        vmem_mib=48, name="inproj_bwd")


def _adamw_math(w, g, m, v):
    m = ADAM_B1 * m + (1.0 - ADAM_B1) * g
    v = ADAM_B2 * v + (1.0 - ADAM_B2) * (g * g)
    m_hat = m / (1.0 - ADAM_B1 ** ADAM_STEP)
    v_hat = v / (1.0 - ADAM_B2 ** ADAM_STEP)
    delta = -ADAM_LR * (m_hat / (jnp.sqrt(v_hat) + ADAM_EPS) + ADAM_WD * w)
    return delta, m, v


def _sum_parts(p_ref):
    g = p_ref[0].astype(F32)
    for k in range(1, p_ref.shape[0]):
        g = g + p_ref[k].astype(F32)
    return g


def _adamw_layers(parts, w, m, v, row_tile, name, comm=None):
    _, rows, cols = w.shape
    n_tiles = rows // row_tile

    def body(*refs):
        p_refs = refs[:DEPTH]
        w_ref, m_ref, v_ref, g_ref, d_ref, nm_ref, nv_ref = refs[DEPTH:]
        layer = pl.program_id(0)
        for l, p_ref in enumerate(p_refs):
            @pl.when(layer == l)
            def _(p_ref=p_ref):
                g = _sum_parts(p_ref)
                g_ref[0] = g
                d_ref[0], nm_ref[0], nv_ref[0] = _adamw_math(w_ref[0], g, m_ref[0], v_ref[0])

    def part_spec(l):
        return pl.BlockSpec((parts[l].shape[0], row_tile, cols),
                            lambda layer, i: (0, jnp.where(layer == l, i, jnp.where(layer < l, 0, n_tiles - 1)), 0))

    tile = pl.BlockSpec((1, row_tile, cols), lambda layer, i: (layer, i, 0))
    return _call(body, (*parts, w, m, v), grid=(DEPTH, n_tiles), in_specs=[part_spec(l) for l in range(DEPTH)] + [tile] * 3,
                 out_specs=[tile] * 4, out_shape=[_sds(w.shape, F32)] * 4, name=name, comm=comm)


def _adamw_sum8(parts, w, m, v, name):
    def body(p_ref, w_ref, m_ref, v_ref, g_ref, d_ref, nm_ref, nv_ref):
        g = _sum_parts(p_ref)
        g_ref[...] = g
        d_ref[...], nm_ref[...], nv_ref[...] = _adamw_math(w_ref[...], g, m_ref[...], v_ref[...])

    return pl.pallas_call(body, out_shape=[_sds(w.shape, F32)] * 4, name=name)(parts, w, m, v)


def _adamw_plain(g, w, m, v, name):
    def body(g_ref, w_ref, m_ref, v_ref, d_ref, nm_ref, nv_ref):
        d_ref[...], nm_ref[...], nv_ref[...] = _adamw_math(w_ref[...], g_ref[...], m_ref[...], v_ref[...])

    return pl.pallas_call(body, out_shape=[_sds(w.shape, F32)] * 3, name=name)(g, w, m, v)


SMALL_ROWS = 8
GAIN_NAMES = ("pre_mix_norm", "post_mix_norm", "pre_ffn_norm", "post_ffn_norm", "attn_out_norm", "conv_out_norm")


def _pack_small(pre_mix, post_mix, pre_ffn, post_ffn, attn_out, conv_out, taps):
    zeros = jnp.zeros((1, ATTN_WIDTH), F32)
    return jnp.concatenate([
        pre_mix, post_mix, pre_ffn, post_ffn, jnp.concatenate([attn_out, conv_out], axis=1),
        jnp.concatenate([taps[0:1], taps[1:2]], axis=1), jnp.concatenate([taps[2:3], zeros], axis=1),
        jnp.zeros((1, D_MODEL), F32)], axis=0)


def _rope(positions, comm=None):
    inv_freq = ROPE_THETA ** (-jnp.arange(0, ROPE_DIM, 2, dtype=F32) / ROPE_DIM)
    per_head = jnp.concatenate([inv_freq, inv_freq, jnp.zeros((HEAD_DIM - ROPE_DIM,), F32)])
    freq_row = jnp.tile(per_head, LANES // HEAD_DIM).reshape(1, LANES)
    return _rope_tables(positions.reshape(SEQ, 1), freq_row, comm)


def _layer_forward(h, gains, taps, tables, w, inproj_comm=None, attn_comm=None, ffn_comm=None, target=None):
    (q, k, v, u, b, c), landed = _inproj_fwd(h, gains["pre_mix_norm"], w["in"], *tables, comm=inproj_comm)
    if inproj_comm is not None:
        w = {**w, **dict(zip(inproj_comm.kinds, landed))}
    (attn, lse), landed = _attn_fwd(q, k, v, comm=attn_comm)
    if attn_comm is not None:
        w = {**w, **dict(zip(attn_comm.kinds, landed))}
    x1, merged, mix = _mix_fwd(attn, u, b, c, taps, gains["attn_out_norm"], gains["conv_out_norm"], w["out"], h, gains["post_mix_norm"])
    (x2, gu, f, *loss), landed_next = _ffn_fwd(x1, gains["pre_ffn_norm"], w["gu"], w["dn"], gains["post_ffn_norm"], comm=ffn_comm,
                                               target=target)
    out = x2 if target is None else (x2, loss[0])
    return out, (h, q, k, v, u, b, c, attn, lse, merged, mix, x1, gu, f), w, landed_next


class _GradientExchange:
    def ffn_grads(self, act, df, h2, dgu):
        return dict(dn=_wgrad_paired(act, df, "dn")[0], gu=_wgrad_paired(dgu, h2, "gu")[0])

    def attention_passenger(self, ffn):
        return _ScatterChips(("gu", "dn"), (ffn["gu"], ffn["dn"]))

    def w_out_grad(self, merged_t, dmix):
        return _wgrad(merged_t, dmix, "wgrad_out")[0]

    def w_in_grad(self, h_t, dproj, g_out, last):
        comm = _Scatter(("out",), (g_out,))
        if last:
            g_in, landed = _wgrad_paired(h_t, dproj, "in", comm=comm)
        else:
            g_in, landed = _wgrad(h_t, dproj, "wgrad_in", comm=comm)
        return g_in, landed[0]


def _layer_backward(dx, saved, gains, taps, tables, w, ffn_comm, exchange, last):
    x0, q, k, v, u, b, c, attn, lse, merged, mix, x1, gu, f = saved
    (dx1, df, act, dgu, h2, dg_post_ffn, dg_pre_ffn), landed_prev = _ffn_bwd(
        dx, f, x1, gu, w["gu"], w["dn"], gains["post_ffn_norm"], gains["pre_ffn_norm"], comm=ffn_comm)
    ffn = exchange.ffn_grads(act, df, h2, dgu)
    dmix, dattn, ds, db, dg_post_mix, dg_attn, dg_conv = _mix_bwd(
        dx1, mix, attn, u, b, c, taps, gains["attn_out_norm"], gains["conv_out_norm"], gains["post_mix_norm"], w["out"])
    g_out = exchange.w_out_grad(merged, dmix)
    attn_comm = exchange.attention_passenger(ffn)
    (dq, dk, dv), landed_ffn = _attn_bwd(q, k, v, dattn, attn, lse, comm=attn_comm)
    dx0, dproj, h1, dg_pre_mix, dtaps = _inproj_bwd(
        dq, dk, dv, ds, db, u, c, taps, *tables, w["in"], x0, gains["pre_mix_norm"], dx1)[0]
    g_in, out_parts = exchange.w_in_grad(h1, dproj, g_out, last)
    small = _pack_small(dg_pre_mix, dg_post_mix, dg_pre_ffn, dg_post_ffn, dg_attn, dg_conv, dtaps)
    rest = dict(zip(attn_comm.kinds, landed_ffn)) if attn_comm is not None else dict(ffn)
    rest["out"] = out_parts
    return dx0, g_in, small, landed_prev, rest


def kernel(x, positions, pre_mix_norm, w_in, conv_w, attn_out_norm, conv_out_norm, w_out, post_mix_norm, pre_ffn_norm, w_gate_up, w_down, post_ffn_norm, loss_target, m_pre_mix_norm, m_w_in, m_conv_w, m_attn_out_norm, m_conv_out_norm, m_w_out, m_post_mix_norm, m_pre_ffn_norm, m_w_gate_up, m_w_down, m_post_ffn_norm, v_pre_mix_norm, v_w_in, v_conv_w, v_attn_out_norm, v_conv_out_norm, v_w_out, v_post_mix_norm, v_pre_ffn_norm, v_w_gate_up, v_w_down, v_post_ffn_norm):
    mx, my, mc = _place()
    me = _block_of(mx, my, mc)
    conv_channels = conv_w.shape[-1]

    def hidden_major(a):
        return jnp.swapaxes(a, 1, 2)

    big_w = dict(zip(WEIGHT_KINDS, (w_in, w_out, hidden_major(w_gate_up), w_down)))
    big_m = dict(zip(WEIGHT_KINDS, (m_w_in, m_w_out, hidden_major(m_w_gate_up), m_w_down)))
    big_v = dict(zip(WEIGHT_KINDS, (v_w_in, v_w_out, hidden_major(v_w_gate_up), v_w_down)))
    shards = {kind: big_w[kind].astype(BF16) for kind in WEIGHT_KINDS}
    all_gains = dict(pre_mix_norm=pre_mix_norm, attn_out_norm=attn_out_norm, conv_out_norm=conv_out_norm, post_mix_norm=post_mix_norm,
                     pre_ffn_norm=pre_ffn_norm, post_ffn_norm=post_ffn_norm)

    def gather(kinds, l):
        return _Gather(kinds, [shards[kind][l] for kind in kinds])

    def gains(l):
        return {name: g[l:l + 1] for name, g in all_gains.items()}

    early = ("in", "out", "dn")
    taps_flat = jnp.pad(conv_w.reshape(-1), (0, 8 * LANES - conv_w.size)).reshape(8, LANES)
    tables, (first, taps_all) = _rope(positions, _Together(gather(early[:1], 0), _GatherSmall(taps_flat, "taps")))
    taps_all = taps_all.reshape(N_DEV, 8 * LANES)[:, :conv_w.size]
    conv_w_full = taps_all.reshape(N_DEV, DEPTH, 3, conv_channels).transpose(1, 2, 0, 3).reshape(DEPTH, 3, ATTN_WIDTH)
    weights = [dict(zip(early[:1], [first]))] + [None] * (DEPTH - 1)
    saved = [None] * DEPTH
    h = x[0]
    for l in range(DEPTH):
        inproj_comm = gather(early[1:], 0) if l == 0 else None
        ffn_comm = gather(early, l + 1) if l + 1 < DEPTH else None
        target = loss_target[0] if l + 1 == DEPTH else None
        h, saved[l], weights[l], landed = _layer_forward(h, gains(l), conv_w_full[l], tables, weights[l], inproj_comm,
                                                         gather(("gu",), l), ffn_comm, target)
        if ffn_comm is not None:
            weights[l + 1] = dict(zip(ffn_comm.kinds, landed))
    dx, local_loss = h

    parts = {kind: [None] * DEPTH for kind in WEIGHT_KINDS}
    small_grads = [None] * DEPTH
    g_in_above = None
    exchange = _GradientExchange()
    for l in reversed(range(DEPTH)):
        ffn_comm = _Scatter(("in",), (g_in_above,)) if g_in_above is not None else None
        dx, g_in_above, small_grads[l], landed, rest = _layer_backward(dx, saved[l], gains(l), conv_w_full[l], tables, weights[l], ffn_comm,
                                                                      exchange, l == 0)
        if ffn_comm is not None:
            parts["in"][l + 1] = landed[0]
        for kind, part in rest.items():
            parts[kind][l] = part

    packed = jnp.concatenate(small_grads, axis=0)
    packed = lax.dynamic_update_slice(packed, local_loss, (SMALL_ROWS - 1, 0))
    parts["in"][0], gathered = _comm_only(_Together(_ScatterChips(("in",), (g_in_above,)), _GatherSmall(packed, "small")), "exchange_last")
    tiles = {"in": 256, "out": 128, "gu": 176, "dn": 176}
    big = {kind: _adamw_layers(parts[kind], big_w[kind], big_m[kind], big_v[kind], tiles[kind], "adamw_" + kind)[0]
           for kind in WEIGHT_KINDS}


    def pack_state(state):
        rows = [_pack_small(*[state[name][l:l + 1] for name in GAIN_NAMES], jnp.zeros((3, ATTN_WIDTH), F32)) for l in range(DEPTH)]
        return jnp.concatenate(rows, axis=0)

    sw = pack_state(all_gains)
    sm = pack_state(dict(zip(GAIN_NAMES, (m_pre_mix_norm, m_post_mix_norm, m_pre_ffn_norm, m_post_ffn_norm, m_attn_out_norm, m_conv_out_norm))))
    sv = pack_state(dict(zip(GAIN_NAMES, (v_pre_mix_norm, v_post_mix_norm, v_pre_ffn_norm, v_post_ffn_norm, v_attn_out_norm, v_conv_out_norm))))
    sg, sd, snm, snv = _adamw_sum8(gathered, sw, sm, sv, "adamw_small")

    def unpack(p):
        p = p.reshape(DEPTH, SMALL_ROWS, D_MODEL)
        return dict(pre_mix_norm=p[:, 0], post_mix_norm=p[:, 1], pre_ffn_norm=p[:, 2], post_ffn_norm=p[:, 3],
                    attn_out_norm=p[:, 4, :ATTN_WIDTH], conv_out_norm=p[:, 4, ATTN_WIDTH:])

    small = [unpack(p) for p in (sg, sd, snm, snv)]
    loss = sg[SMALL_ROWS - 1, 0]
    sg3 = sg.reshape(DEPTH, SMALL_ROWS, D_MODEL)
    taps_grad_full = jnp.stack([sg3[:, 5, :ATTN_WIDTH], sg3[:, 5, ATTN_WIDTH:], sg3[:, 6, :ATTN_WIDTH]], axis=1)
    taps_grad = lax.dynamic_slice_in_dim(taps_grad_full, me * conv_channels, conv_channels, axis=2)

    def flat(a):
        return a.reshape(DEPTH * 3, conv_channels)

    td, tnm, tnv = _adamw_plain(flat(taps_grad), flat(conv_w), flat(m_conv_w), flat(v_conv_w), "adamw_taps")
    taps = [taps_grad] + [a.reshape(conv_w.shape) for a in (td, tnm, tnv)]

    def leaves(o):
        s = small[o]
        return (s["pre_mix_norm"], big["in"][o], taps[o], s["attn_out_norm"], s["conv_out_norm"], big["out"][o], s["post_mix_norm"],
                s["pre_ffn_norm"], hidden_major(big["gu"][o]), big["dn"][o], s["post_ffn_norm"])

    return (loss, dx[None], *leaves(0), *leaves(1), *leaves(2), *leaves(3))
```

```python
import math

import jax
import jax.numpy as jnp
from jax import lax
from jax.experimental import pallas as pl
from jax.experimental.pallas import tpu as pltpu

F32 = jnp.float32
BF16 = jnp.bfloat16
MESH = pl.DeviceIdType.MESH

SEQ = 4096
D_MODEL = 1024
DEPTH = 4
N_DEV = 8
ATTN_WIDTH = 512
IN_PROJ_WIDTH = 3072
FFN_HIDDEN = 2816
FFN_BLOCK = 2 * FFN_HIDDEN // N_DEV
W_IN_BLOCK = IN_PROJ_WIDTH // N_DEV
W_OUT_BLOCK = D_MODEL // N_DEV
W_DOWN_BLOCK = FFN_HIDDEN // N_DEV
HEAD_DIM = 64
ROPE_DIM = 16
ROPE_THETA = 500000.0
DILATIONS = (1, 4, 16)
HALF_WINDOW = 64
RMS_EPS = 1e-6
NEG_INF = -1e30
LANES = 128
Q_BLOCK = 128
K_WINDOW = Q_BLOCK + 2 * HALF_WINDOW
PERM_CHUNK = 256
ROW_TILE = 512
FFN_TILE = 256
WGRAD_COLS = 512
ADAM_LR, ADAM_B1, ADAM_B2, ADAM_EPS, ADAM_WD, ADAM_STEP = 0.001, 0.9, 0.999, 1e-08, 0.01, 10
MIB = 1024 * 1024
PINNED_BYTES = 256 * 1024

WEIGHT_KINDS = ("in", "out", "gu", "dn")
FULL_SHAPES = {"in": (D_MODEL, IN_PROJ_WIDTH), "out": (D_MODEL, D_MODEL), "gu": (N_DEV, FFN_BLOCK, D_MODEL), "dn": (FFN_HIDDEN, D_MODEL)}
SHARD_SHAPES = {"in": (D_MODEL, W_IN_BLOCK), "out": (W_OUT_BLOCK, D_MODEL), "gu": (FFN_BLOCK, D_MODEL), "dn": (W_DOWN_BLOCK, D_MODEL)}
ANY = pl.BlockSpec(memory_space=pl.ANY)


def _sds(shape, dtype):
    return jax.ShapeDtypeStruct(shape, dtype)


def _rows(width, tile=ROW_TILE):
    return pl.BlockSpec((tile, width), lambda i: (i, 0))


def _frows(width):
    return _rows(width, FFN_TILE)


def _cols(height, tile):
    return pl.BlockSpec((height, tile), lambda i: (0, i))


def _const(shape):
    return pl.BlockSpec(shape, lambda i: (0,) * len(shape))


def _const1(shape):
    return pl.BlockSpec(shape, lambda i: (0,) * len(shape), pipeline_mode=pl.Buffered(1))


HALO_ROWS = 16


def _halo_prev(width, tile=ROW_TILE):
    return pl.BlockSpec((HALO_ROWS, width), lambda i: (jnp.maximum(i * (tile // HALO_ROWS) - 1, 0), 0))


def _halo_next(width, tile=ROW_TILE):
    return pl.BlockSpec((HALO_ROWS, width), lambda i: (jnp.minimum((i + 1) * (tile // HALO_ROWS), SEQ // HALO_ROWS - 1), 0))


def _rms(x):
    r = lax.rsqrt(jnp.mean(x * x, axis=-1, keepdims=True) + RMS_EPS)
    return x * r, r


def _rms_bwd(dn, y, r):
    return r * (dn - y * jnp.mean(dn * y, axis=-1, keepdims=True))


def _dot(a, b):
    return jnp.dot(a, b, preferred_element_type=F32)


def _dot_nt(a, b):
    return lax.dot_general(a, b, (((1,), (1,)), ((), ())), preferred_element_type=F32)


def _dot_tn(a, b):
    return lax.dot_general(a, b, (((0,), (0,)), ((), ())), preferred_element_type=F32)


def _place():
    return lax.axis_index("x"), lax.axis_index("y"), lax.axis_index("c")


def _block_of(px, py, pc):
    return 4 * px + 2 * py + pc


def _weight_block(ref, kind, blk):
    if kind == "in":
        return ref.at[:, pl.ds(blk * W_IN_BLOCK, W_IN_BLOCK)]
    if kind == "out":
        return ref.at[pl.ds(blk * W_OUT_BLOCK, W_OUT_BLOCK), :]
    if kind == "gu":
        return ref.at[blk]
    return ref.at[pl.ds(blk * W_DOWN_BLOCK, W_DOWN_BLOCK), :]


def _dma_semaphores(n):
    return [pltpu.SemaphoreType.DMA((n, 7)), pltpu.SemaphoreType.DMA((n, 7)), pltpu.SemaphoreType.DMA((n,))]


class _Gather:
    def __init__(self, kinds, shards):
        self.kinds, self.operands = tuple(kinds), list(shards)
        self.tag = "gather_" + "_".join(kinds)
        self.out_shape = [_sds(FULL_SHAPES[k], BF16) for k in kinds]
        self.scratch = _dma_semaphores(len(kinds))

    def _parties(self):
        x, y, c = _place()
        return (x, y, c), (x, y, 1 - c), [(1 - x, y), (x, 1 - y), (1 - x, 1 - y)], c

    def _copy(self, outs, sems, w, k, block, to, src=None):
        dst = _weight_block(outs[w], self.kinds[w], _block_of(*block))
        return pltpu.make_async_remote_copy(src_ref=dst if src is None else src, dst_ref=dst, send_sem=sems[0].at[w, k],
                                            recv_sem=sems[1].at[w, k], device_id=to, device_id_type=MESH)

    def _own(self, srcs, outs, sems, w, me):
        return pltpu.make_async_copy(srcs[w], _weight_block(outs[w], self.kinds[w], _block_of(*me)), sems[2].at[w])

    def _first(self, srcs, outs, sems, w):
        me, sibling, chips, c = self._parties()
        return [self._copy(outs, sems, w, 0, me, sibling, src=srcs[w])] + [
            self._copy(outs, sems, w, 1 + j, me, (*chip, c), src=srcs[w]) for j, chip in enumerate(chips)]

    def start(self, srcs, outs, sems):
        me = self._parties()[0]
        for w in range(len(self.kinds)):
            self._own(srcs, outs, sems, w, me).start()
            for cp in self._first(srcs, outs, sems, w):
                cp.start()

    def forward(self, srcs, outs, sems):
        me, sibling, chips, c = self._parties()
        for j, chip in enumerate(chips):
            for w in range(len(self.kinds)):
                self._copy(outs, sems, w, 1 + j, (*chip, c), me).wait_recv()
                self._copy(outs, sems, w, 4 + j, (*chip, c), sibling).start()

    def finish(self, srcs, outs, sems):
        me, sibling, chips, c = self._parties()
        for w in range(len(self.kinds)):
            self._copy(outs, sems, w, 0, sibling, me).wait_recv()
            for j, chip in enumerate(chips):
                self._copy(outs, sems, w, 4 + j, (*chip, 1 - c), me).wait_recv()
        for w in range(len(self.kinds)):
            for cp in self._first(srcs, outs, sems, w):
                cp.wait_send()
            for j, chip in enumerate(chips):
                self._copy(outs, sems, w, 4 + j, (*chip, c), sibling).wait_send()
            self._own(srcs, outs, sems, w, me).wait()


def _peers(x, y, c):
    return [(x ^ a, y ^ b, c ^ e) for a in (0, 1) for b in (0, 1) for e in (0, 1) if (a, b, e) != (0, 0, 0)]


class _Scatter:
    def __init__(self, kinds, grads):
        self.kinds, self.operands = tuple(kinds), list(grads)
        self.tag = "scatter_" + "_".join(kinds)
        self.out_shape = [_sds((N_DEV,) + SHARD_SHAPES[k], BF16) for k in kinds]
        self.scratch = _dma_semaphores(len(kinds))

    def _copies(self, srcs, outs, sems):
        x, y, c = _place()
        me = _block_of(x, y, c)
        copies = []
        for w, kind in enumerate(self.kinds):
            copies.append(pltpu.make_async_copy(_weight_block(srcs[w], kind, me), outs[w].at[me], sems[2].at[w]))
            for k, peer in enumerate(_peers(x, y, c)):
                copies.append(pltpu.make_async_remote_copy(
                    src_ref=_weight_block(srcs[w], kind, _block_of(*peer)), dst_ref=outs[w].at[me],
                    send_sem=sems[0].at[w, k], recv_sem=sems[1].at[w, k], device_id=peer, device_id_type=MESH))
        return copies

    def start(self, srcs, outs, sems):
        for cp in self._copies(srcs, outs, sems):
            cp.start()

    def forward(self, srcs, outs, sems):
        pass

    def finish(self, srcs, outs, sems):
        for cp in self._copies(srcs, outs, sems):
            cp.wait()


N_CHIPS = N_DEV // 2


class _ScatterChips:
    def __init__(self, kinds, blocks):
        self.kinds, self.operands = tuple(kinds), list(blocks)
        self.tag = "scatter_chips_" + "_".join(kinds)
        self.out_shape = [_sds((N_CHIPS,) + SHARD_SHAPES[k], BF16) for k in kinds]
        self.scratch = _dma_semaphores(len(kinds))

    def _copies(self, srcs, outs, sems):
        x, y, c = _place()
        mine = 2 * x + y
        copies = []
        for w in range(len(self.kinds)):
            copies.append(pltpu.make_async_copy(srcs[w].at[mine], outs[w].at[mine], sems[2].at[w]))
            for j, (px, py) in enumerate([(1 - x, y), (x, 1 - y), (1 - x, 1 - y)]):
                copies.append(pltpu.make_async_remote_copy(
                    src_ref=srcs[w].at[2 * px + py], dst_ref=outs[w].at[mine], send_sem=sems[0].at[w, j], recv_sem=sems[1].at[w, j],
                    device_id=(px, py, c), device_id_type=MESH))
        return copies

    def start(self, srcs, outs, sems):
        for cp in self._copies(srcs, outs, sems):
            cp.start()

    def forward(self, srcs, outs, sems):
        pass

    def finish(self, srcs, outs, sems):
        for cp in self._copies(srcs, outs, sems):
            cp.wait()


def _in_hbm(a):
    return pltpu.with_memory_space_constraint(a, pltpu.HBM) if a.size * a.dtype.itemsize >= PINNED_BYTES else a


def _out_hbm(s):
    return pltpu.HBM(s.shape, s.dtype) if math.prod(s.shape) * jnp.dtype(s.dtype).itemsize >= PINNED_BYTES else s


def _call(body, args, *, grid, in_specs, out_specs, out_shape, scratch_shapes=(), vmem_mib=None, name, comm=None):
    kwargs = {} if vmem_mib is None else dict(compiler_params=pltpu.CompilerParams(vmem_limit_bytes=vmem_mib * MIB))
    in_specs, out_specs, out_shape, scratch_shapes = list(in_specs), list(out_specs), list(out_shape), list(scratch_shapes)
    args = [_in_hbm(a) for a in args]
    out_shape = [_out_hbm(s) for s in out_shape]
    if comm is None:
        res = pl.pallas_call(body, grid=grid, in_specs=in_specs, out_specs=out_specs, out_shape=out_shape,
                             scratch_shapes=scratch_shapes, name=name, **kwargs)(*args)
        return list(res), None
    n_in, n_out, n_scr = len(in_specs), len(out_specs), len(scratch_shapes)
    c_in, c_out = len(comm.operands), len(comm.out_shape)
    last = math.prod(grid) - 1

    def carried(*refs):
        cuts = [n_in, c_in, n_out, c_out, n_scr]
        parts, at = [], 0
        for n in cuts:
            parts.append(refs[at:at + n])
            at += n
        ins, c_ins, outs, c_outs, scr = parts
        sems = refs[at:]
        step = pl.program_id(0)
        for axis in range(1, len(grid)):
            step = step * grid[axis] + pl.program_id(axis)

        @pl.when(step == 0)
        def _():
            comm.start(c_ins, c_outs, sems)

        @pl.when(step == last)
        def _():
            comm.forward(c_ins, c_outs, sems)

        body(*ins, *outs, *scr)

        @pl.when(step == last)
        def _():
            comm.finish(c_ins, c_outs, sems)

    res = pl.pallas_call(carried, grid=grid, in_specs=in_specs + [ANY] * c_in, out_specs=out_specs + [ANY] * c_out,
                         out_shape=out_shape + [_out_hbm(s) for s in comm.out_shape], scratch_shapes=scratch_shapes + comm.scratch,
                         name=name + "_" + comm.tag, **kwargs)(*args, *[_in_hbm(a) for a in comm.operands])
    return list(res[:n_out]), list(res[n_out:])


def _comm_only(comm, name):
    def body(*refs):
        n_in, n_out = len(comm.operands), len(comm.out_shape)
        srcs, outs, sems = refs[:n_in], refs[n_in:n_in + n_out], refs[n_in + n_out:]
        comm.start(srcs, outs, sems)
        comm.forward(srcs, outs, sems)
        comm.finish(srcs, outs, sems)

    return pl.pallas_call(body, in_specs=[ANY] * len(comm.operands), out_specs=[ANY] * len(comm.out_shape),
                          out_shape=[_out_hbm(s) for s in comm.out_shape], scratch_shapes=comm.scratch,
                          name=name)(*[_in_hbm(a) for a in comm.operands])


class _GatherSmall:
    def __init__(self, v, tag):
        self.operands, self.tag = [v], tag
        self.out_shape = [_sds((N_DEV,) + v.shape, F32)]
        self.scratch = _dma_semaphores(1)

    def _copies(self, srcs, outs, sems):
        x, y, c = _place()
        me = _block_of(x, y, c)
        copies = [pltpu.make_async_copy(srcs[0], outs[0].at[me], sems[2].at[0])]
        for k, peer in enumerate(_peers(x, y, c)):
            copies.append(pltpu.make_async_remote_copy(src_ref=srcs[0], dst_ref=outs[0].at[me], send_sem=sems[0].at[0, k],
                                                       recv_sem=sems[1].at[0, k], device_id=peer, device_id_type=MESH))
        return copies

    def start(self, srcs, outs, sems):
        for cp in self._copies(srcs, outs, sems):
            cp.start()

    def forward(self, srcs, outs, sems):
        pass

    def finish(self, srcs, outs, sems):
        for cp in self._copies(srcs, outs, sems):
            cp.wait()


class _Together:
    def __init__(self, *comms):
        self.comms = comms
        self.operands = [a for cm in comms for a in cm.operands]
        self.out_shape = [s for cm in comms for s in cm.out_shape]
        self.scratch = [s for cm in comms for s in cm.scratch]
        self.tag = "_".join(cm.tag for cm in comms)

    def _each(self, srcs, outs, sems):
        a = o = s = 0
        for cm in self.comms:
            na, no, ns = len(cm.operands), len(cm.out_shape), len(cm.scratch)
            yield cm, srcs[a:a + na], outs[o:o + no], sems[s:s + ns]
            a, o, s = a + na, o + no, s + ns

    def start(self, srcs, outs, sems):
        for cm, *refs in self._each(srcs, outs, sems):
            cm.start(*refs)

    def forward(self, srcs, outs, sems):
        for cm, *refs in self._each(srcs, outs, sems):
            cm.forward(*refs)

    def finish(self, srcs, outs, sems):
        for cm, *refs in self._each(srcs, outs, sems):
            cm.finish(*refs)


def _rope_tables(pos_col, freq_row, comm=None):
    def body(p_ref, f_ref, c_ref, sa_ref, sb_ref):
        ang = p_ref[...].astype(F32) * f_ref[...]
        lane = lax.broadcasted_iota(jnp.int32, ang.shape, 1) % HEAD_DIM
        cos, sin = jnp.cos(ang), jnp.sin(ang)
        c_ref[...] = jnp.where(lane < ROPE_DIM, cos, 1.0)
        sa_ref[...] = jnp.where(lane < ROPE_DIM // 2, -sin, 0.0)
        sb_ref[...] = jnp.where((lane >= ROPE_DIM // 2) & (lane < ROPE_DIM), sin, 0.0)

    return _call(body, (pos_col, freq_row), grid=(SEQ // ROW_TILE,), in_specs=[_rows(1), _const((1, LANES))],
                 out_specs=[_rows(LANES)] * 3, out_shape=[_sds((SEQ, LANES), F32)] * 3, name="rope_tables", comm=comm)


def _rotate(t, c, sa, sb):
    parts = []
    for g in range(ATTN_WIDTH // LANES):
        tg = t[:, g * LANES:(g + 1) * LANES]
        parts.append(tg * c + pltpu.roll(tg, LANES - 8, axis=1) * sa + pltpu.roll(tg, 8, axis=1) * sb)
    return jnp.concatenate(parts, axis=1)


def _rotate_transposed(dt, c, sa, sb):
    parts = []
    for g in range(ATTN_WIDTH // LANES):
        dg = dt[:, g * LANES:(g + 1) * LANES]
        parts.append(dg * c + pltpu.roll(dg * sa, 8, axis=1) + pltpu.roll(dg * sb, LANES - 8, axis=1))
    return jnp.concatenate(parts, axis=1)


def _inproj_fwd(x, g_pre, w_in, tc, tsa, tsb, comm=None):
    def body(x_ref, g_ref, w_ref, c_ref, sa_ref, sb_ref, q_ref, k_ref, v_ref, u_ref, b_ref, cc_ref):
        y, _ = _rms(x_ref[...])
        h = (y * g_ref[...]).astype(BF16)

        def proj(n):
            return _dot(h, w_ref[:, n * ATTN_WIDTH:(n + 1) * ATTN_WIDTH])

        c, sa, sb = c_ref[...], sa_ref[...], sb_ref[...]
        q_ref[...] = (_rotate(proj(0), c, sa, sb) * (HEAD_DIM ** -0.5)).astype(BF16)
        k_ref[...] = _rotate(proj(1), c, sa, sb).astype(BF16)
        v_ref[...] = proj(2).astype(BF16)
        u_ref[...] = proj(3).astype(BF16)
        b_ref[...] = proj(4).astype(BF16)
        cc_ref[...] = proj(5).astype(BF16)

    a = ATTN_WIDTH
    return _call(
        body, (x, g_pre, w_in, tc, tsa, tsb), grid=(SEQ // ROW_TILE,),
        in_specs=[_rows(D_MODEL), _const((1, D_MODEL)), _const1((D_MODEL, IN_PROJ_WIDTH)), _rows(LANES), _rows(LANES), _rows(LANES)],
        out_specs=[_rows(a)] * 6, out_shape=[_sds((SEQ, a), BF16)] * 6,
        vmem_mib=40, name="inproj_fwd", comm=comm)


def _head_masks():
    lane = lax.broadcasted_iota(jnp.int32, (1, LANES), 1)
    first = lane < HEAD_DIM
    return first, first.astype(F32), 1.0 - first.astype(F32)


def _perm_chunks(dil):
    length = SEQ // dil
    out = []
    for r in range(dil):
        for c0 in range(0, length, PERM_CHUNK):
            chunk = (r * length + c0) // PERM_CHUNK
            rows = pl.ds(c0, PERM_CHUNK) if dil == 1 else pl.ds(r + dil * c0, PERM_CHUNK, stride=dil)
            out.append((chunk, rows))
    return out


def _chunk(c, offset=0):
    return pl.ds(offset + c * PERM_CHUNK, PERM_CHUNK)


def _write_band_bias(bias_ref):
    rr = lax.broadcasted_iota(jnp.int32, (Q_BLOCK, K_WINDOW), 0)
    cc = lax.broadcasted_iota(jnp.int32, (Q_BLOCK, K_WINDOW), 1)
    band = (cc >= rr) & (cc - rr <= 2 * HALF_WINDOW)
    bias_ref[0] = jnp.where(band, 0.0, NEG_INF)
    bias_ref[1] = jnp.where(band & (cc >= HALF_WINDOW), 0.0, NEG_INF)
    bias_ref[2] = jnp.where(band & (cc < Q_BLOCK + HALF_WINDOW), 0.0, NEG_INF)


def _band_bias_index(m0, length):
    return jnp.where(m0 % length == 0, 1, 0) + jnp.where((m0 + Q_BLOCK) % length == 0, 2, 0)


def _zero_key_padding(bufs):
    pad = jnp.zeros((HALF_WINDOW, LANES), BF16)
    for buf in bufs:
        buf[pl.ds(0, HALF_WINDOW), :] = pad
        buf[pl.ds(SEQ + HALF_WINDOW, HALF_WINDOW), :] = pad


def _attn_fwd(q, k, v, comm=None):
    group = 8

    def body(q_ref, k_ref, v_ref, o_ref, lse_ref, q32, k32, v32, qa, qb, kp, vp, accp, mlp,
             acc0, acc1, acc2, ml0, ml1, ml2, bias_ref):
        first, mask_a, mask_b = _head_masks()
        low = lax.broadcasted_iota(jnp.int32, (1, LANES), 1) % HEAD_DIM < HEAD_DIM // 2
        _zero_key_padding((kp, vp))
        _write_band_bias(bias_ref)
        q32[...] = q_ref[...].astype(F32)
        k32[...] = k_ref[...].astype(F32)
        v32[...] = v_ref[...].astype(F32)
        natural = ((acc0, ml0), (acc1, ml1), (acc2, ml2))

        for branch, dil in enumerate(DILATIONS):
            length = SEQ // dil
            assert length >= 2 * Q_BLOCK
            chunks = _perm_chunks(dil)
            for c, rows in chunks:
                val = q32[rows, :]
                qa[_chunk(c), :] = (val * mask_a).astype(BF16)
                qb[_chunk(c), :] = (val * mask_b).astype(BF16)
                kp[_chunk(c, HALF_WINDOW), :] = k32[rows, :].astype(BF16)
                vp[_chunk(c, HALF_WINDOW), :] = v32[rows, :].astype(BF16)
            acc_dst, ml_dst = natural[branch] if dil == 1 else (accp, mlp)

            def blocks(i, carry, length=length, acc_dst=acc_dst, ml_dst=ml_dst):
                base = pl.multiple_of(i * (group * Q_BLOCK), group * Q_BLOCK)
                starts = [base + g * Q_BLOCK for g in range(group)]
                scores = [[_dot_nt(qx[pl.ds(m0, Q_BLOCK), :], kp[pl.ds(m0, K_WINDOW), :]) for qx in (qa, qb)] for m0 in starts]
                probs = []
                for m0, pair in zip(starts, scores):
                    bias = bias_ref[_band_bias_index(m0, length)]
                    stats, ps = [], []
                    for s in pair:
                        s = s + bias
                        m = jnp.max(s, axis=1, keepdims=True)
                        p = jnp.exp(s - m)
                        stats.append(jnp.where(low, m, jnp.sum(p, axis=1, keepdims=True)))
                        ps.append(p.astype(BF16))
                    ml_dst[pl.ds(m0, Q_BLOCK), :] = jnp.where(first, stats[0], stats[1])
                    probs.append(ps)
                for m0, ps in zip(starts, probs):
                    vw = vp[pl.ds(m0, K_WINDOW), :]
                    acc_dst[pl.ds(m0, Q_BLOCK), :] = jnp.where(first, _dot(ps[0], vw), _dot(ps[1], vw))
                return carry

            lax.fori_loop(0, SEQ // (group * Q_BLOCK), blocks, 0)

            if dil > 1:
                for c, rows in chunks:
                    natural[branch][0][rows, :] = accp[_chunk(c), :]
                    natural[branch][1][rows, :] = mlp[_chunk(c), :]

        for c in range(SEQ // PERM_CHUNK):
            packed = [ml[_chunk(c), :] for _, ml in natural]
            ms = [jnp.where(low, ml, pltpu.roll(ml, HEAD_DIM // 2, axis=1)) for ml in packed]
            ls = [jnp.where(low, pltpu.roll(ml, LANES - HEAD_DIM // 2, axis=1), ml) for ml in packed]
            m_all = jnp.maximum(jnp.maximum(ms[0], ms[1]), ms[2])
            es = [jnp.exp(m - m_all) for m in ms]
            z = ls[0] * es[0] + ls[1] * es[1] + ls[2] * es[2]
            num = natural[0][0][_chunk(c), :] * es[0] + natural[1][0][_chunk(c), :] * es[1] + natural[2][0][_chunk(c), :] * es[2]
            o_ref[_chunk(c), :] = num / z
            lse_ref[_chunk(c), :] = m_all + jnp.log(z)

    col = pl.BlockSpec((SEQ, LANES), lambda h: (0, h))
    padded = SEQ + 2 * HALF_WINDOW
    return _call(
        body, (q, k, v), grid=(ATTN_WIDTH // LANES,), in_specs=[col] * 3, out_specs=[col] * 2,
        out_shape=[_sds((SEQ, ATTN_WIDTH), F32)] * 2,
        scratch_shapes=[pltpu.VMEM((SEQ, LANES), F32)] * 3 + [pltpu.VMEM((SEQ, LANES), BF16)] * 2
        + [pltpu.VMEM((padded, LANES), BF16)] * 2 + [pltpu.VMEM((SEQ, LANES), F32)] * 8
        + [pltpu.VMEM((3, Q_BLOCK, K_WINDOW), F32)],
        vmem_mib=52, name="attn_fwd", comm=comm)


def _shifted(t, before, after, i):
    tile = t.shape[0]
    row = lax.broadcasted_iota(jnp.int32, (tile, 1), 0)
    before = jnp.where(i > 0, before, 0.0)
    after = jnp.where(i < SEQ // tile - 1, after, 0.0)
    return (jnp.where(row == 0, before, pltpu.roll(t, 1, axis=0)),
            jnp.where(row == tile - 1, after, pltpu.roll(t, tile - 1, axis=0)))


def _last_row(ref):
    return ref[HALO_ROWS - 1:HALO_ROWS, :].astype(F32)


def _first_row(ref):
    return ref[0:1, :].astype(F32)


def _conv_parts(u, c, u_prev, c_prev, u_next, c_next, cw, i):
    t = c * u
    t_prev, t_next = _shifted(t, _last_row(c_prev) * _last_row(u_prev), _first_row(c_next) * _first_row(u_next), i)
    s = cw[0:1, :] * t_prev + cw[1:2, :] * t + cw[2:3, :] * t_next
    return t, t_prev, t_next, s


def _mix_fwd(attn, u, b, c, conv_w, g_attn, g_conv, w_out, x, g_post):
    def body(a_ref, u_ref, b_ref, c_ref, up_ref, cp_ref, un_ref, cn_ref, cw_ref, ga_ref, gc_ref, w_ref, x_ref, gp_ref,
             x1_ref, mg_ref, mix_ref):
        i = pl.program_id(0)
        _, _, _, s = _conv_parts(u_ref[...].astype(F32), c_ref[...].astype(F32), up_ref, cp_ref, un_ref, cn_ref, cw_ref[...], i)
        ya, _ = _rms(a_ref[...])
        yc, _ = _rms(b_ref[...].astype(F32) * s)
        merged = jnp.concatenate([ya * ga_ref[...], yc * gc_ref[...]], axis=1).astype(BF16)
        mix = _dot(merged, w_ref[...])
        ym, _ = _rms(mix)
        mg_ref[...] = merged.T
        mix_ref[...] = mix.astype(BF16)
        x1_ref[...] = x_ref[...] + ym * gp_ref[...]

    a = ATTN_WIDTH
    return _call(
        body, (attn, u, b, c, u, c, u, c, conv_w, g_attn, g_conv, w_out, x, g_post), grid=(SEQ // ROW_TILE,),
        in_specs=[_rows(a)] * 4 + [_halo_prev(a)] * 2 + [_halo_next(a)] * 2
        + [_const((3, a)), _const((1, a)), _const((1, a)), _const1((D_MODEL, D_MODEL)), _rows(D_MODEL), _const((1, D_MODEL))],
        out_specs=[_rows(D_MODEL), _cols(D_MODEL, ROW_TILE), _rows(D_MODEL)],
        out_shape=[_sds((SEQ, D_MODEL), F32), _sds((D_MODEL, SEQ), BF16), _sds((SEQ, D_MODEL), BF16)],
        vmem_mib=40, name="mix_fwd")[0]


def _gu_spec():
    return pl.BlockSpec((N_DEV, FFN_TILE, FFN_BLOCK), lambda i: (0, i, 0))


def _ffn_fwd(x1, g_pre, w_gu, w_dn, g_post, comm=None, target=None):
    n_tiles = SEQ // ROW_TILE

    def body(*refs):
        if target is None:
            x_ref, g_ref, wgu_ref, wdn_ref, gp_ref, x2_ref, gu_ref, f_ref = refs
        else:
            x_ref, g_ref, wgu_ref, wdn_ref, gp_ref, t_ref, x2_ref, gu_ref, f_ref, loss_ref, acc = refs
        x1 = x_ref[...]
        y, _ = _rms(x1)
        h = (y * g_ref[...]).astype(BF16)
        f = jnp.zeros((ROW_TILE, D_MODEL), F32)

        def gate_up(j):
            return _dot_nt(h, wgu_ref[j]), _dot_nt(h, wgu_ref[j + N_DEV // 2])

        ahead = gate_up(0)
        for j in range(N_DEV // 2):
            gate, up = ahead
            if j + 1 < N_DEV // 2:
                ahead = gate_up(j + 1)
            gu_ref[j] = gate.astype(BF16)
            gu_ref[j + N_DEV // 2] = up.astype(BF16)
            act = (gate * jax.nn.sigmoid(gate) * up).astype(BF16)
            f = f + _dot(act, wdn_ref[pl.ds(j * FFN_BLOCK, FFN_BLOCK), :])
        yf, _ = _rms(f)
        f_ref[...] = f
        x2 = x1 + yf * gp_ref[...]
        if target is None:
            x2_ref[...] = x2
        else:
            i = pl.program_id(0)
            err = x2 - t_ref[...]
            x2_ref[...] = err * (1.0 / D_MODEL)

            @pl.when(i == 0)
            def _():
                acc[...] = jnp.zeros_like(acc)

            acc[...] += jnp.sum(err * err, axis=0, keepdims=True)

            @pl.when(i == n_tiles - 1)
            def _():
                loss_ref[...] = jnp.sum(acc[...], axis=1, keepdims=True) * (0.5 / D_MODEL)

    with_loss = target is not None
    return _call(
        body, (x1, g_pre, w_gu, w_dn, g_post) + ((target,) if with_loss else ()), grid=(n_tiles,),
        in_specs=[_rows(D_MODEL), _const((1, D_MODEL)), _const1((N_DEV, FFN_BLOCK, D_MODEL)), _const1((FFN_HIDDEN, D_MODEL)),
                  _const((1, D_MODEL))] + ([_rows(D_MODEL)] if with_loss else []),
        out_specs=[_rows(D_MODEL), pl.BlockSpec((N_DEV, ROW_TILE, FFN_BLOCK), lambda i: (0, i, 0)), _rows(D_MODEL)]
        + ([_const((1, 1))] if with_loss else []),
        out_shape=[_sds((SEQ, D_MODEL), F32), _sds((N_DEV, SEQ, FFN_BLOCK), BF16), _sds((SEQ, D_MODEL), F32)]
        + ([_sds((1, 1), F32)] if with_loss else []),
        scratch_shapes=[pltpu.VMEM((1, D_MODEL), F32)] if with_loss else [],
        vmem_mib=58, name="ffn_fwd_loss" if with_loss else "ffn_fwd", comm=comm)


def _accumulate(pairs, i):
    @pl.when(i == 0)
    def _():
        for ref, value in pairs:
            ref[...] = value

    @pl.when(i > 0)
    def _():
        for ref, value in pairs:
            ref[...] += value


def _colsum(v):
    return jnp.sum(v, axis=0, keepdims=True)


def _ffn_bwd(dx2, f, x1, gu, w_gu, w_dn, g_post, g_pre, comm=None):
    half = N_DEV // 2

    def body(dx2_ref, f_ref, x1_ref, gu_ref, wgu_ref, wdn_ref, gpost_ref, gpre_ref,
             dx1_ref, df_ref, act_ref, dgu_ref, h_ref, dgpost_ref, dgpre_ref):
        i = pl.program_id(0)
        dx2 = dx2_ref[...]
        yf, rf = _rms(f_ref[...])
        dg_post = _colsum(dx2 * yf)
        df = _rms_bwd(dx2 * gpost_ref[...], yf, rf).astype(BF16)
        df_ref[...] = df
        dh = jnp.zeros((FFN_TILE, D_MODEL), F32)

        def d_act(j):
            return _dot_nt(df, wdn_ref[pl.ds(j * FFN_BLOCK, FFN_BLOCK), :])

        ahead = d_act(0)
        for j in range(half):
            dact = ahead
            if j + 1 < half:
                ahead = d_act(j + 1)
            gate = gu_ref[j].astype(F32)
            up = gu_ref[j + half].astype(F32)
            sig = jax.nn.sigmoid(gate)
            silu = gate * sig
            act_ref[j] = (silu * up).astype(BF16)
            dgate = (dact * up * (sig * (1.0 + gate * (1.0 - sig)))).astype(BF16)
            dup = (dact * silu).astype(BF16)
            dgu_ref[j] = dgate
            dgu_ref[j + half] = dup
            dh = dh + _dot(dgate, wgu_ref[j]) + _dot(dup, wgu_ref[j + half])
        y1, r1 = _rms(x1_ref[...])
        h_ref[...] = (y1 * gpre_ref[...]).astype(BF16)
        dx1_ref[...] = dx2 + _rms_bwd(dh * gpre_ref[...], y1, r1)
        _accumulate([(dgpost_ref, dg_post), (dgpre_ref, _colsum(dh * y1))], i)

    act_spec = pl.BlockSpec((half, FFN_TILE, FFN_BLOCK), lambda i: (0, i, 0))
    return _call(
        body, (dx2, f, x1, gu, w_gu, w_dn, g_post, g_pre), grid=(SEQ // FFN_TILE,),
        in_specs=[_frows(D_MODEL)] * 3 + [_gu_spec(), _const1((N_DEV, FFN_BLOCK, D_MODEL)), _const1((FFN_HIDDEN, D_MODEL)),
                                          _const((1, D_MODEL)), _const((1, D_MODEL))],
        out_specs=[_frows(D_MODEL), _frows(D_MODEL), act_spec, _gu_spec(), _frows(D_MODEL), _const((1, D_MODEL)), _const((1, D_MODEL))],
        out_shape=[_sds((SEQ, D_MODEL), F32), _sds((SEQ, D_MODEL), BF16), _sds((half, SEQ, FFN_BLOCK), BF16),
                   _sds((N_DEV, SEQ, FFN_BLOCK), BF16), _sds((SEQ, D_MODEL), BF16), _sds((1, D_MODEL), F32), _sds((1, D_MODEL), F32)],
        vmem_mib=52, name="ffn_bwd", comm=comm)


def _wgrad(a_t, g, name, comm=None):
    width = g.shape[1]

    def body(a_ref, g_ref, o_ref):
        o_ref[...] = _dot(a_ref[...], g_ref[...]).astype(BF16)

    res, landed = _call(body, (a_t, g), grid=(width // WGRAD_COLS,),
                        in_specs=[_const1((D_MODEL, SEQ)), pl.BlockSpec((SEQ, WGRAD_COLS), lambda j: (0, j))],
                        out_specs=[pl.BlockSpec((D_MODEL, WGRAD_COLS), lambda j: (0, j))], out_shape=[_sds((D_MODEL, width), BF16)],
                        vmem_mib=48, name=name, comm=comm)
    return res[0], landed


def _wgrad_paired(a, g, kind, comm=None):
    shard = SHARD_SHAPES[kind]
    per_chip = 2 if kind == "gu" else 1
    n_compute = per_chip * N_CHIPS
    n_steps = n_compute + 1

    def block(t):
        return jnp.minimum(t, n_compute - 1)

    if kind == "in":
        specs = [_const1((D_MODEL, SEQ)), pl.BlockSpec((SEQ, 2 * W_IN_BLOCK), lambda t: (0, block(t)))]
    else:
        specs = [pl.BlockSpec((1, SEQ, FFN_BLOCK), lambda t: (block(t), 0, 0)), _const1((SEQ, D_MODEL))]

    def body(a_ref, g_ref, o_ref, sendbuf, recvbuf, keepbuf, send_sem, recv_sem):
        t = pl.program_id(0)
        x, y, c = _place()

        def exchange(q):
            return pltpu.make_async_remote_copy(src_ref=sendbuf.at[q], dst_ref=recvbuf.at[q], send_sem=send_sem.at[q],
                                                recv_sem=recv_sem.at[q], device_id=(x, y, 1 - c), device_id_type=MESH)

        @pl.when((t >= per_chip) & (t % per_chip == 0))
        def _():
            q = t // per_chip - 1
            exchange(q).wait_recv()
            o_ref[0] = (keepbuf[...] + recvbuf[q].astype(F32)).astype(BF16)

        @pl.when(t < n_compute)
        def _():
            q = t // per_chip
            if kind == "gu":
                r = _dot_tn(a_ref[0], g_ref[...])

                @pl.when(t % 2 == c)
                def _():
                    keepbuf[...] = r

                @pl.when(t % 2 != c)
                def _():
                    sendbuf[q] = r.astype(BF16)
                    exchange(q).start()
            else:
                if kind == "dn":
                    r = _dot_tn(a_ref[0], g_ref[...])
                    lower, upper = r[:W_DOWN_BLOCK], r[W_DOWN_BLOCK:]
                else:
                    r = _dot(a_ref[...], g_ref[...])
                    lower, upper = r[:, :W_IN_BLOCK], r[:, W_IN_BLOCK:]
                keepbuf[...] = jnp.where(c == 0, lower, upper)
                sendbuf[q] = jnp.where(c == 0, upper, lower).astype(BF16)
                exchange(q).start()

        @pl.when(t == n_steps - 1)
        def _():
            for q in range(N_CHIPS):
                exchange(q).wait_send()

    res, landed = _call(
        body, (a, g), grid=(n_steps,), in_specs=specs,
        out_specs=[pl.BlockSpec((1,) + shard, lambda t: (jnp.clip(t // per_chip - 1, 0, N_CHIPS - 1), 0, 0))],
        out_shape=[_sds((N_CHIPS,) + shard, BF16)],
        scratch_shapes=[pltpu.VMEM((N_CHIPS,) + shard, BF16), pltpu.VMEM((N_CHIPS,) + shard, BF16), pltpu.VMEM(shard, F32),
                        pltpu.SemaphoreType.DMA((N_CHIPS,)), pltpu.SemaphoreType.DMA((N_CHIPS,))],
        vmem_mib=52, name="wgrad_" + kind + "_paired", comm=comm)
    return res[0], landed


def _mix_bwd(dx1, mix, attn, u, b, c, conv_w, g_attn, g_conv, g_post, w_out):
    def body(dx1_ref, mix_ref, a_ref, u_ref, b_ref, c_ref, up_ref, cp_ref, un_ref, cn_ref, cw_ref, ga_ref, gc_ref, gp_ref, w_ref,
             dmix_ref, da_ref, ds_ref, db_ref, dgp_ref, dga_ref, dgc_ref):
        i = pl.program_id(0)
        dx1 = dx1_ref[...]
        ym, rm = _rms(mix_ref[...].astype(F32))
        dg_post = _colsum(dx1 * ym)
        dmix = _rms_bwd(dx1 * gp_ref[...], ym, rm).astype(BF16)
        dmix_ref[...] = dmix
        dmerged = _dot_nt(dmix, w_ref[...])
        dna, dnc = dmerged[:, :ATTN_WIDTH], dmerged[:, ATTN_WIDTH:]
        ya, ra = _rms(a_ref[...])
        da_ref[...] = _rms_bwd(dna * ga_ref[...], ya, ra)
        _, _, _, s = _conv_parts(u_ref[...].astype(F32), c_ref[...].astype(F32), up_ref, cp_ref, un_ref, cn_ref, cw_ref[...], i)
        gate_b = b_ref[...].astype(F32)
        yc, rc = _rms(gate_b * s)
        dy = _rms_bwd(dnc * gc_ref[...], yc, rc)
        db_ref[...] = (dy * s).astype(BF16)
        ds_ref[...] = (dy * gate_b).astype(BF16)
        _accumulate([(dgp_ref, dg_post), (dga_ref, _colsum(dna * ya)), (dgc_ref, _colsum(dnc * yc))], i)

    a = ATTN_WIDTH
    return _call(
        body, (dx1, mix, attn, u, b, c, u, c, u, c, conv_w, g_attn, g_conv, g_post, w_out), grid=(SEQ // ROW_TILE,),
        in_specs=[_rows(D_MODEL)] * 2 + [_rows(a)] * 4 + [_halo_prev(a)] * 2 + [_halo_next(a)] * 2
        + [_const((3, a)), _const((1, a)), _const((1, a)), _const((1, D_MODEL)), _const1((D_MODEL, D_MODEL))],
        out_specs=[_rows(D_MODEL)] + [_rows(a)] * 3 + [_const((1, D_MODEL)), _const((1, a)), _const((1, a))],
        out_shape=[_sds((SEQ, D_MODEL), BF16), _sds((SEQ, a), F32), _sds((SEQ, a), BF16), _sds((SEQ, a), BF16),
                   _sds((1, D_MODEL), F32), _sds((1, a), F32), _sds((1, a), F32)],
        vmem_mib=40, name="mix_bwd")[0]


def _attn_bwd(q, k, v, do, o, lse, comm=None):
    group = 8

    n_pairs = ATTN_WIDTH // LANES

    def body(q_hbm, k_hbm, v_hbm, do_hbm, o_hbm, lse_hbm, dq_ref, dk_ref, dv_ref,
             q_ref, k_ref, v_ref, do_ref, o_ref, lse_ref, in_sem,
             q32, k32, v32, qa, qb, doa, dob, kp, vp, lsep, dlp, dnat, dqp, dkp, dvp, bias_ref):
        pair = pl.program_id(0)
        sources = (q_hbm, k_hbm, v_hbm, do_hbm, o_hbm, lse_hbm)
        buffers = (q_ref, k_ref, v_ref, do_ref, o_ref, lse_ref)

        def fetch(i, p):
            return pltpu.make_async_copy(sources[i].at[:, pl.ds(pl.multiple_of(p * LANES, LANES), LANES)], buffers[i], in_sem.at[i])

        def prefetch(*which):
            @pl.when(pair + 1 < n_pairs)
            def _():
                for i in which:
                    fetch(i, pair + 1).start()

        @pl.when(pair == 0)
        def _():
            for i in range(len(sources)):
                fetch(i, 0).start()

        for i in range(len(sources)):
            fetch(i, pair).wait()

        first, mask_a, mask_b = _head_masks()
        _zero_key_padding((kp, vp))
        _write_band_bias(bias_ref)
        q32[...] = q_ref[...].astype(F32)
        k32[...] = k_ref[...].astype(F32)
        v32[...] = v_ref[...].astype(F32)
        prefetch(0, 1, 2)
        for c in range(SEQ // PERM_CHUNK):
            prod = do_ref[_chunk(c), :] * o_ref[_chunk(c), :]
            d_a = jnp.sum(prod * mask_a, axis=1, keepdims=True)
            d_b = jnp.sum(prod * mask_b, axis=1, keepdims=True)
            dnat[_chunk(c), :] = jnp.where(first, d_a, d_b)
        prefetch(4)

        for step, dil in enumerate(DILATIONS[1:] + DILATIONS[:1]):
            length = SEQ // dil
            assert length >= 2 * Q_BLOCK
            chunks = _perm_chunks(dil)
            for c, rows in chunks:
                val = q32[rows, :]
                qa[_chunk(c), :] = (val * mask_a).astype(BF16)
                qb[_chunk(c), :] = (val * mask_b).astype(BF16)
                val = do_ref[rows, :]
                doa[_chunk(c), :] = (val * mask_a).astype(BF16)
                dob[_chunk(c), :] = (val * mask_b).astype(BF16)
                kp[_chunk(c, HALF_WINDOW), :] = k32[rows, :].astype(BF16)
                vp[_chunk(c, HALF_WINDOW), :] = v32[rows, :].astype(BF16)
                lsep[_chunk(c), :] = lse_ref[rows, :]
                dlp[_chunk(c), :] = dnat[rows, :]
            if step == len(DILATIONS) - 1:
                prefetch(3, 5)
            zero = jnp.zeros((PERM_CHUNK, LANES), F32)
            for c in range(SEQ // PERM_CHUNK):
                dkp[_chunk(c), :] = zero
                dvp[_chunk(c), :] = zero
            dkp[pl.ds(SEQ, 2 * HALF_WINDOW), :] = zero[:2 * HALF_WINDOW]
            dvp[pl.ds(SEQ, 2 * HALF_WINDOW), :] = zero[:2 * HALF_WINDOW]

            heads = ((qa, doa, 0), (qb, dob, HEAD_DIM))

            def blocks(i, carry, length=length):
                base = pl.multiple_of(i * (group * Q_BLOCK), group * Q_BLOCK)
                starts = [base + g * Q_BLOCK for g in range(group)]
                raw = [[(_dot_nt(qx[pl.ds(m0, Q_BLOCK), :], kp[pl.ds(m0, K_WINDOW), :]),
                         _dot_nt(dox[pl.ds(m0, Q_BLOCK), :], vp[pl.ds(m0, K_WINDOW), :])) for qx, dox, _ in heads]
                       for m0 in starts]
                grads = []
                for m0, pair in zip(starts, raw):
                    bias = bias_ref[_band_bias_index(m0, length)]
                    lse_b, d_b = lsep[pl.ds(m0, Q_BLOCK), :], dlp[pl.ds(m0, Q_BLOCK), :]
                    out = []
                    for (s, dp), (_, _, col) in zip(pair, heads):
                        p = jnp.exp(s + bias - lse_b[:, col:col + 1])
                        out.append(((p * (dp - d_b[:, col:col + 1])).astype(BF16), p.astype(BF16)))
                    grads.append(out)
                for m0, out in zip(starts, grads):
                    qrows, krows = pl.ds(m0, Q_BLOCK), pl.ds(m0, K_WINDOW)
                    kw = kp[krows, :]
                    dk = jnp.zeros((K_WINDOW, LANES), F32)
                    dv = jnp.zeros((K_WINDOW, LANES), F32)
                    for (ds, p), (qx, dox, _) in zip(out, heads):
                        dk = dk + _dot_tn(ds, qx[qrows, :])
                        dv = dv + _dot_tn(p, dox[qrows, :])
                    dqp[qrows, :] = jnp.where(first, _dot(out[0][0], kw), _dot(out[1][0], kw)) * (HEAD_DIM ** -0.5)
                    dkp[krows, :] += dk
                    dvp[krows, :] += dv
                return carry

            lax.fori_loop(0, SEQ // (group * Q_BLOCK), blocks, 0)

            for c, rows in chunks:
                g_q, g_k, g_v = dqp[_chunk(c), :], dkp[_chunk(c, HALF_WINDOW), :], dvp[_chunk(c, HALF_WINDOW), :]
                if step == 0:
                    dq_ref[rows, :] = g_q
                    dk_ref[rows, :] = g_k
                    dv_ref[rows, :] = g_v
                else:
                    dq_ref[rows, :] = dq_ref[rows, :] + g_q
                    dk_ref[rows, :] = dk_ref[rows, :] + g_k
                    dv_ref[rows, :] = dv_ref[rows, :] + g_v

    col = pl.BlockSpec((SEQ, LANES), lambda h: (0, h))
    padded = SEQ + 2 * HALF_WINDOW
    return _call(
        body, (q, k, v, do, o, lse), grid=(n_pairs,), in_specs=[ANY] * 6, out_specs=[col] * 3,
        out_shape=[_sds((SEQ, ATTN_WIDTH), F32)] * 3,
        scratch_shapes=[pltpu.VMEM((SEQ, LANES), BF16)] * 3 + [pltpu.VMEM((SEQ, LANES), F32)] * 3 + [pltpu.SemaphoreType.DMA((6,))]
        + [pltpu.VMEM((SEQ, LANES), F32)] * 3 + [pltpu.VMEM((SEQ, LANES), BF16)] * 4
        + [pltpu.VMEM((padded, LANES), BF16)] * 2 + [pltpu.VMEM((SEQ, LANES), F32)] * 4 + [pltpu.VMEM((padded, LANES), F32)] * 2
        + [pltpu.VMEM((3, Q_BLOCK, K_WINDOW), F32)],
        vmem_mib=56, name="attn_bwd", comm=comm)


def _inproj_bwd(dq, dk, dv, ds, db, u, c, conv_w, tc, tsa, tsb, w_in, x, g_pre, dx1):
    def body(dq_ref, dk_ref, dv_ref, ds_ref, db_ref, u_ref, c_ref, dsp_ref, up_ref, cp_ref, dsn_ref, un_ref, cn_ref, cw_ref,
             tc_ref, tsa_ref, tsb_ref, w_ref, x_ref, g_ref, dx1_ref, dx_ref, dproj_ref, h_ref, dg_ref, dcw_ref):
        i = pl.program_id(0)
        cw = cw_ref[...]
        u, c = u_ref[...].astype(F32), c_ref[...].astype(F32)
        t, t_prev, t_next, _ = _conv_parts(u, c, up_ref, cp_ref, un_ref, cn_ref, cw, i)
        ds = ds_ref[...].astype(F32)
        ds_prev, ds_next = _shifted(ds, _last_row(dsp_ref), _first_row(dsn_ref), i)
        dt = cw[0:1, :] * ds_next + cw[1:2, :] * ds + cw[2:3, :] * ds_prev
        d_taps = jnp.concatenate([_colsum(ds * t_prev), _colsum(ds * t), _colsum(ds * t_next)], axis=0)
        tc_, tsa_, tsb_ = tc_ref[...], tsa_ref[...], tsb_ref[...]
        groups = ((2, lambda: dv_ref[...].astype(BF16)), (4, lambda: db_ref[...]),
                  (0, lambda: _rotate_transposed(dq_ref[...], tc_, tsa_, tsb_).astype(BF16)),
                  (1, lambda: _rotate_transposed(dk_ref[...], tc_, tsa_, tsb_).astype(BF16)),
                  (3, lambda: (dt * c).astype(BF16)), (5, lambda: (dt * u).astype(BF16)))
        dh = jnp.zeros((x_ref.shape[0], D_MODEL), F32)
        for n, make in groups:
            cols = pl.ds(n * a, a)
            part = make()
            dproj_ref[:, cols] = part
            dh = dh + _dot_nt(part, w_ref[:, cols])
        y, r = _rms(x_ref[...])
        h_ref[...] = (y * g_ref[...]).astype(BF16).T
        dx_ref[...] = dx1_ref[...] + _rms_bwd(dh * g_ref[...], y, r)
        _accumulate([(dg_ref, _colsum(dh * y)), (dcw_ref, d_taps)], i)

    a = ATTN_WIDTH
    tile = FFN_TILE

    def rows(width):
        return _rows(width, tile)

    return _call(
        body, (dq, dk, dv, ds, db, u, c, ds, u, c, ds, u, c, conv_w, tc, tsa, tsb, w_in, x, g_pre, dx1), grid=(SEQ // tile,),
        in_specs=[rows(a)] * 7 + [_halo_prev(a, tile)] * 3 + [_halo_next(a, tile)] * 3
        + [_const((3, a)), rows(LANES), rows(LANES), rows(LANES), _const1((D_MODEL, IN_PROJ_WIDTH)), rows(D_MODEL),
           _const((1, D_MODEL)), rows(D_MODEL)],
        out_specs=[rows(D_MODEL), rows(IN_PROJ_WIDTH), _cols(D_MODEL, tile), _const((1, D_MODEL)), _const((3, a))],
        out_shape=[_sds((SEQ, D_MODEL), F32), _sds((SEQ, IN_PROJ_WIDTH), BF16), _sds((D_MODEL, SEQ), BF16),
                   _sds((1, D_MODEL), F32), _sds((3, a), F32)],
        vmem_mib=48, name="inproj_bwd")


def _adamw_math(w, g, m, v):
    m = ADAM_B1 * m + (1.0 - ADAM_B1) * g
    v = ADAM_B2 * v + (1.0 - ADAM_B2) * (g * g)
    m_hat = m / (1.0 - ADAM_B1 ** ADAM_STEP)
    v_hat = v / (1.0 - ADAM_B2 ** ADAM_STEP)
    delta = -ADAM_LR * (m_hat / (jnp.sqrt(v_hat) + ADAM_EPS) + ADAM_WD * w)
    return delta, m, v


def _sum_parts(p_ref):
    g = p_ref[0].astype(F32)
    for k in range(1, p_ref.shape[0]):
        g = g + p_ref[k].astype(F32)
    return g


def _adamw_layers(parts, w, m, v, row_tile, name, comm=None):
    _, rows, cols = w.shape
    n_tiles = rows // row_tile

    def body(*refs):
        p_refs = refs[:DEPTH]
        w_ref, m_ref, v_ref, g_ref, d_ref, nm_ref, nv_ref = refs[DEPTH:]
        layer = pl.program_id(0)
        for l, p_ref in enumerate(p_refs):
            @pl.when(layer == l)
            def _(p_ref=p_ref):
                g = _sum_parts(p_ref)
                g_ref[0] = g
                d_ref[0], nm_ref[0], nv_ref[0] = _adamw_math(w_ref[0], g, m_ref[0], v_ref[0])

    def part_spec(l):
        return pl.BlockSpec((parts[l].shape[0], row_tile, cols),
                            lambda layer, i: (0, jnp.where(layer == l, i, jnp.where(layer < l, 0, n_tiles - 1)), 0))

    tile = pl.BlockSpec((1, row_tile, cols), lambda layer, i: (layer, i, 0))
    return _call(body, (*parts, w, m, v), grid=(DEPTH, n_tiles), in_specs=[part_spec(l) for l in range(DEPTH)] + [tile] * 3,
                 out_specs=[tile] * 4, out_shape=[_sds(w.shape, F32)] * 4, name=name, comm=comm)


def _adamw_sum8(parts, w, m, v, name):
    def body(p_ref, w_ref, m_ref, v_ref, g_ref, d_ref, nm_ref, nv_ref):
        g = _sum_parts(p_ref)
        g_ref[...] = g
        d_ref[...], nm_ref[...], nv_ref[...] = _adamw_math(w_ref[...], g, m_ref[...], v_ref[...])

    return pl.pallas_call(body, out_shape=[_sds(w.shape, F32)] * 4, name=name)(parts, w, m, v)


def _adamw_plain(g, w, m, v, name):
    def body(g_ref, w_ref, m_ref, v_ref, d_ref, nm_ref, nv_ref):
        d_ref[...], nm_ref[...], nv_ref[...] = _adamw_math(w_ref[...], g_ref[...], m_ref[...], v_ref[...])

    return pl.pallas_call(body, out_shape=[_sds(w.shape, F32)] * 3, name=name)(g, w, m, v)


SMALL_ROWS = 8
GAIN_NAMES = ("pre_mix_norm", "post_mix_norm", "pre_ffn_norm", "post_ffn_norm", "attn_out_norm", "conv_out_norm")


def _pack_small(pre_mix, post_mix, pre_ffn, post_ffn, attn_out, conv_out, taps):
    zeros = jnp.zeros((1, ATTN_WIDTH), F32)
    return jnp.concatenate([
        pre_mix, post_mix, pre_ffn, post_ffn, jnp.concatenate([attn_out, conv_out], axis=1),
        jnp.concatenate([taps[0:1], taps[1:2]], axis=1), jnp.concatenate([taps[2:3], zeros], axis=1),
        jnp.zeros((1, D_MODEL), F32)], axis=0)


def _rope(positions, comm=None):
    inv_freq = ROPE_THETA ** (-jnp.arange(0, ROPE_DIM, 2, dtype=F32) / ROPE_DIM)
    per_head = jnp.concatenate([inv_freq, inv_freq, jnp.zeros((HEAD_DIM - ROPE_DIM,), F32)])
    freq_row = jnp.tile(per_head, LANES // HEAD_DIM).reshape(1, LANES)
    return _rope_tables(positions.reshape(SEQ, 1), freq_row, comm)


def _layer_forward(h, gains, taps, tables, w, inproj_comm=None, attn_comm=None, ffn_comm=None, target=None):
    (q, k, v, u, b, c), landed = _inproj_fwd(h, gains["pre_mix_norm"], w["in"], *tables, comm=inproj_comm)
    if inproj_comm is not None:
        w = {**w, **dict(zip(inproj_comm.kinds, landed))}
    (attn, lse), landed = _attn_fwd(q, k, v, comm=attn_comm)
    if attn_comm is not None:
        w = {**w, **dict(zip(attn_comm.kinds, landed))}
    x1, merged, mix = _mix_fwd(attn, u, b, c, taps, gains["attn_out_norm"], gains["conv_out_norm"], w["out"], h, gains["post_mix_norm"])
    (x2, gu, f, *loss), landed_next = _ffn_fwd(x1, gains["pre_ffn_norm"], w["gu"], w["dn"], gains["post_ffn_norm"], comm=ffn_comm,
                                               target=target)
    out = x2 if target is None else (x2, loss[0])
    return out, (h, q, k, v, u, b, c, attn, lse, merged, mix, x1, gu, f), w, landed_next


class _GradientExchange:
    def ffn_grads(self, act, df, h2, dgu):
        return dict(dn=_wgrad_paired(act, df, "dn")[0], gu=_wgrad_paired(dgu, h2, "gu")[0])

    def attention_passenger(self, ffn):
        return _ScatterChips(("gu", "dn"), (ffn["gu"], ffn["dn"]))

    def w_out_grad(self, merged_t, dmix):
        return _wgrad(merged_t, dmix, "wgrad_out")[0]

    def w_in_grad(self, h_t, dproj, g_out, last):
        comm = _Scatter(("out",), (g_out,))
        if last:
            g_in, landed = _wgrad_paired(h_t, dproj, "in", comm=comm)
        else:
            g_in, landed = _wgrad(h_t, dproj, "wgrad_in", comm=comm)
        return g_in, landed[0]


def _layer_backward(dx, saved, gains, taps, tables, w, ffn_comm, exchange, last):
    x0, q, k, v, u, b, c, attn, lse, merged, mix, x1, gu, f = saved
    (dx1, df, act, dgu, h2, dg_post_ffn, dg_pre_ffn), landed_prev = _ffn_bwd(
        dx, f, x1, gu, w["gu"], w["dn"], gains["post_ffn_norm"], gains["pre_ffn_norm"], comm=ffn_comm)
    ffn = exchange.ffn_grads(act, df, h2, dgu)
    dmix, dattn, ds, db, dg_post_mix, dg_attn, dg_conv = _mix_bwd(
        dx1, mix, attn, u, b, c, taps, gains["attn_out_norm"], gains["conv_out_norm"], gains["post_mix_norm"], w["out"])
    g_out = exchange.w_out_grad(merged, dmix)
    attn_comm = exchange.attention_passenger(ffn)
    (dq, dk, dv), landed_ffn = _attn_bwd(q, k, v, dattn, attn, lse, comm=attn_comm)
    dx0, dproj, h1, dg_pre_mix, dtaps = _inproj_bwd(
        dq, dk, dv, ds, db, u, c, taps, *tables, w["in"], x0, gains["pre_mix_norm"], dx1)[0]
    g_in, out_parts = exchange.w_in_grad(h1, dproj, g_out, last)
    small = _pack_small(dg_pre_mix, dg_post_mix, dg_pre_ffn, dg_post_ffn, dg_attn, dg_conv, dtaps)
    rest = dict(zip(attn_comm.kinds, landed_ffn)) if attn_comm is not None else dict(ffn)
    rest["out"] = out_parts
    return dx0, g_in, small, landed_prev, rest


def kernel(x, positions, pre_mix_norm, w_in, conv_w, attn_out_norm, conv_out_norm, w_out, post_mix_norm, pre_ffn_norm, w_gate_up, w_down, post_ffn_norm, loss_target, m_pre_mix_norm, m_w_in, m_conv_w, m_attn_out_norm, m_conv_out_norm, m_w_out, m_post_mix_norm, m_pre_ffn_norm, m_w_gate_up, m_w_down, m_post_ffn_norm, v_pre_mix_norm, v_w_in, v_conv_w, v_attn_out_norm, v_conv_out_norm, v_w_out, v_post_mix_norm, v_pre_ffn_norm, v_w_gate_up, v_w_down, v_post_ffn_norm):
    mx, my, mc = _place()
    me = _block_of(mx, my, mc)
    conv_channels = conv_w.shape[-1]

    def hidden_major(a):
        return jnp.swapaxes(a, 1, 2)

    big_w = dict(zip(WEIGHT_KINDS, (w_in, w_out, hidden_major(w_gate_up), w_down)))
    big_m = dict(zip(WEIGHT_KINDS, (m_w_in, m_w_out, hidden_major(m_w_gate_up), m_w_down)))
    big_v = dict(zip(WEIGHT_KINDS, (v_w_in, v_w_out, hidden_major(v_w_gate_up), v_w_down)))
    shards = {kind: big_w[kind].astype(BF16) for kind in WEIGHT_KINDS}
    all_gains = dict(pre_mix_norm=pre_mix_norm, attn_out_norm=attn_out_norm, conv_out_norm=conv_out_norm, post_mix_norm=post_mix_norm,
                     pre_ffn_norm=pre_ffn_norm, post_ffn_norm=post_ffn_norm)

    def gather(kinds, l):
        return _Gather(kinds, [shards[kind][l] for kind in kinds])

    def gains(l):
        return {name: g[l:l + 1] for name, g in all_gains.items()}

    early = ("in", "out", "dn")
    taps_flat = jnp.pad(conv_w.reshape(-1), (0, 8 * LANES - conv_w.size)).reshape(8, LANES)
    tables, (first, taps_all) = _rope(positions, _Together(gather(early[:1], 0), _GatherSmall(taps_flat, "taps")))
    taps_all = taps_all.reshape(N_DEV, 8 * LANES)[:, :conv_w.size]
    conv_w_full = taps_all.reshape(N_DEV, DEPTH, 3, conv_channels).transpose(1, 2, 0, 3).reshape(DEPTH, 3, ATTN_WIDTH)
    weights = [dict(zip(early[:1], [first]))] + [None] * (DEPTH - 1)
    saved = [None] * DEPTH
    h = x[0]
    for l in range(DEPTH):
        inproj_comm = gather(early[1:], 0) if l == 0 else None
        ffn_comm = gather(early, l + 1) if l + 1 < DEPTH else None
        target = loss_target[0] if l + 1 == DEPTH else None
        h, saved[l], weights[l], landed = _layer_forward(h, gains(l), conv_w_full[l], tables, weights[l], inproj_comm,
                                                         gather(("gu",), l), ffn_comm, target)
        if ffn_comm is not None:
            weights[l + 1] = dict(zip(ffn_comm.kinds, landed))
    dx, local_loss = h

    parts = {kind: [None] * DEPTH for kind in WEIGHT_KINDS}
    small_grads = [None] * DEPTH
    g_in_above = None
    exchange = _GradientExchange()
    for l in reversed(range(DEPTH)):
        ffn_comm = _Scatter(("in",), (g_in_above,)) if g_in_above is not None else None
        dx, g_in_above, small_grads[l], landed, rest = _layer_backward(dx, saved[l], gains(l), conv_w_full[l], tables, weights[l], ffn_comm,
                                                                      exchange, l == 0)
        if ffn_comm is not None:
            parts["in"][l + 1] = landed[0]
        for kind, part in rest.items():
            parts[kind][l] = part

    packed = jnp.concatenate(small_grads, axis=0)
    packed = lax.dynamic_update_slice(packed, local_loss, (SMALL_ROWS - 1, 0))
    parts["in"][0], gathered = _comm_only(_Together(_ScatterChips(("in",), (g_in_above,)), _GatherSmall(packed, "small")), "exchange_last")
    tiles = {"in": 256, "out": 128, "gu": 176, "dn": 176}
    big = {kind: _adamw_layers(parts[kind], big_w[kind], big_m[kind], big_v[kind], tiles[kind], "adamw_" + kind)[0]
           for kind in WEIGHT_KINDS}


    def pack_state(state):
        rows = [_pack_small(*[state[name][l:l + 1] for name in GAIN_NAMES], jnp.zeros((3, ATTN_WIDTH), F32)) for l in range(DEPTH)]
        return jnp.concatenate(rows, axis=0)

    sw = pack_state(all_gains)
    sm = pack_state(dict(zip(GAIN_NAMES, (m_pre_mix_norm, m_post_mix_norm, m_pre_ffn_norm, m_post_ffn_norm, m_attn_out_norm, m_conv_out_norm))))
    sv = pack_state(dict(zip(GAIN_NAMES, (v_pre_mix_norm, v_post_mix_norm, v_pre_ffn_norm, v_post_ffn_norm, v_attn_out_norm, v_conv_out_norm))))
    sg, sd, snm, snv = _adamw_sum8(gathered, sw, sm, sv, "adamw_small")

    def unpack(p):
        p = p.reshape(DEPTH, SMALL_ROWS, D_MODEL)
        return dict(pre_mix_norm=p[:, 0], post_mix_norm=p[:, 1], pre_ffn_norm=p[:, 2], post_ffn_norm=p[:, 3],
                    attn_out_norm=p[:, 4, :ATTN_WIDTH], conv_out_norm=p[:, 4, ATTN_WIDTH:])

    small = [unpack(p) for p in (sg, sd, snm, snv)]
    loss = sg[SMALL_ROWS - 1, 0]
    sg3 = sg.reshape(DEPTH, SMALL_ROWS, D_MODEL)
    taps_grad_full = jnp.stack([sg3[:, 5, :ATTN_WIDTH], sg3[:, 5, ATTN_WIDTH:], sg3[:, 6, :ATTN_WIDTH]], axis=1)
    taps_grad = lax.dynamic_slice_in_dim(taps_grad_full, me * conv_channels, conv_channels, axis=2)

    def flat(a):
        return a.reshape(DEPTH * 3, conv_channels)

    td, tnm, tnv = _adamw_plain(flat(taps_grad), flat(conv_w), flat(m_conv_w), flat(v_conv_w), "adamw_taps")
    taps = [taps_grad] + [a.reshape(conv_w.shape) for a in (td, tnm, tnv)]

    def leaves(o):
        s = small[o]
        return (s["pre_mix_norm"], big["in"][o], taps[o], s["attn_out_norm"], s["conv_out_norm"], big["out"][o], s["post_mix_norm"],
                s["pre_ffn_norm"], hidden_major(big["gu"][o]), big["dn"][o], s["post_ffn_norm"])

    return (loss, dx[None], *leaves(0), *leaves(1), *leaves(2), *leaves(3))
```

```python
import math

import jax
import jax.numpy as jnp
from jax import lax
from jax.experimental import pallas as pl
from jax.experimental.pallas import tpu as pltpu

F32 = jnp.float32
BF16 = jnp.bfloat16
MESH = pl.DeviceIdType.MESH

SEQ = 4096
D_MODEL = 1024
DEPTH = 4
N_DEV = 8
ATTN_WIDTH = 512
IN_PROJ_WIDTH = 3072
FFN_HIDDEN = 2816
FFN_BLOCK = 2 * FFN_HIDDEN // N_DEV
W_IN_BLOCK = IN_PROJ_WIDTH // N_DEV
W_OUT_BLOCK = D_MODEL // N_DEV
W_DOWN_BLOCK = FFN_HIDDEN // N_DEV
HEAD_DIM = 64
ROPE_DIM = 16
ROPE_THETA = 500000.0
DILATIONS = (1, 4, 16)
HALF_WINDOW = 64
RMS_EPS = 1e-6
NEG_INF = -1e30
LANES = 128
Q_BLOCK = 128
K_WINDOW = Q_BLOCK + 2 * HALF_WINDOW
PERM_CHUNK = 256
ROW_TILE = 512
FFN_TILE = 256
WGRAD_COLS = 512
ADAM_LR, ADAM_B1, ADAM_B2, ADAM_EPS, ADAM_WD, ADAM_STEP = 0.001, 0.9, 0.999, 1e-08, 0.01, 10
MIB = 1024 * 1024
PINNED_BYTES = 256 * 1024

WEIGHT_KINDS = ("in", "out", "gu", "dn")
FULL_SHAPES = {"in": (D_MODEL, IN_PROJ_WIDTH), "out": (D_MODEL, D_MODEL), "gu": (N_DEV, FFN_BLOCK, D_MODEL), "dn": (FFN_HIDDEN, D_MODEL)}
SHARD_SHAPES = {"in": (D_MODEL, W_IN_BLOCK), "out": (W_OUT_BLOCK, D_MODEL), "gu": (FFN_BLOCK, D_MODEL), "dn": (W_DOWN_BLOCK, D_MODEL)}
ANY = pl.BlockSpec(memory_space=pl.ANY)


def _sds(shape, dtype):
    return jax.ShapeDtypeStruct(shape, dtype)


def _rows(width, tile=ROW_TILE):
    return pl.BlockSpec((tile, width), lambda i: (i, 0))


def _frows(width):
    return _rows(width, FFN_TILE)


def _cols(height, tile):
    return pl.BlockSpec((height, tile), lambda i: (0, i))


def _const(shape):
    return pl.BlockSpec(shape, lambda i: (0,) * len(shape))


def _const1(shape):
    return pl.BlockSpec(shape, lambda i: (0,) * len(shape), pipeline_mode=pl.Buffered(1))


HALO_ROWS = 16


def _halo_prev(width, tile=ROW_TILE):
    return pl.BlockSpec((HALO_ROWS, width), lambda i: (jnp.maximum(i * (tile // HALO_ROWS) - 1, 0), 0))


def _halo_next(width, tile=ROW_TILE):
    return pl.BlockSpec((HALO_ROWS, width), lambda i: (jnp.minimum((i + 1) * (tile // HALO_ROWS), SEQ // HALO_ROWS - 1), 0))


def _rms(x):
    r = lax.rsqrt(jnp.mean(x * x, axis=-1, keepdims=True) + RMS_EPS)
    return x * r, r


def _rms_bwd(dn, y, r):
    return r * (dn - y * jnp.mean(dn * y, axis=-1, keepdims=True))


def _dot(a, b):
    return jnp.dot(a, b, preferred_element_type=F32)


def _dot_nt(a, b):
    return lax.dot_general(a, b, (((1,), (1,)), ((), ())), preferred_element_type=F32)


def _dot_tn(a, b):
    return lax.dot_general(a, b, (((0,), (0,)), ((), ())), preferred_element_type=F32)


def _place():
    return lax.axis_index("x"), lax.axis_index("y"), lax.axis_index("c")


def _block_of(px, py, pc):
    return 4 * px + 2 * py + pc


def _weight_block(ref, kind, blk):
    if kind == "in":
        return ref.at[:, pl.ds(blk * W_IN_BLOCK, W_IN_BLOCK)]
    if kind == "out":
        return ref.at[pl.ds(blk * W_OUT_BLOCK, W_OUT_BLOCK), :]
    if kind == "gu":
        return ref.at[blk]
    return ref.at[pl.ds(blk * W_DOWN_BLOCK, W_DOWN_BLOCK), :]


def _dma_semaphores(n):
    return [pltpu.SemaphoreType.DMA((n, 7)), pltpu.SemaphoreType.DMA((n, 7)), pltpu.SemaphoreType.DMA((n,))]


class _Gather:
    def __init__(self, kinds, shards):
        self.kinds, self.operands = tuple(kinds), list(shards)
        self.tag = "gather_" + "_".join(kinds)
        self.out_shape = [_sds(FULL_SHAPES[k], BF16) for k in kinds]
        self.scratch = _dma_semaphores(len(kinds))

    def _parties(self):
        x, y, c = _place()
        return (x, y, c), (x, y, 1 - c), [(1 - x, y), (x, 1 - y), (1 - x, 1 - y)], c

    def _copy(self, outs, sems, w, k, block, to, src=None):
        dst = _weight_block(outs[w], self.kinds[w], _block_of(*block))
        return pltpu.make_async_remote_copy(src_ref=dst if src is None else src, dst_ref=dst, send_sem=sems[0].at[w, k],
                                            recv_sem=sems[1].at[w, k], device_id=to, device_id_type=MESH)

    def _own(self, srcs, outs, sems, w, me):
        return pltpu.make_async_copy(srcs[w], _weight_block(outs[w], self.kinds[w], _block_of(*me)), sems[2].at[w])

    def _first(self, srcs, outs, sems, w):
        me, sibling, chips, c = self._parties()
        return [self._copy(outs, sems, w, 0, me, sibling, src=srcs[w])] + [
            self._copy(outs, sems, w, 1 + j, me, (*chip, c), src=srcs[w]) for j, chip in enumerate(chips)]

    def start(self, srcs, outs, sems):
        me = self._parties()[0]
        for w in range(len(self.kinds)):
            self._own(srcs, outs, sems, w, me).start()
            for cp in self._first(srcs, outs, sems, w):
                cp.start()

    def forward(self, srcs, outs, sems):
        me, sibling, chips, c = self._parties()
        for j, chip in enumerate(chips):
            for w in range(len(self.kinds)):
                self._copy(outs, sems, w, 1 + j, (*chip, c), me).wait_recv()
                self._copy(outs, sems, w, 4 + j, (*chip, c), sibling).start()

    def finish(self, srcs, outs, sems):
        me, sibling, chips, c = self._parties()
        for w in range(len(self.kinds)):
            self._copy(outs, sems, w, 0, sibling, me).wait_recv()
            for j, chip in enumerate(chips):
                self._copy(outs, sems, w, 4 + j, (*chip, 1 - c), me).wait_recv()
        for w in range(len(self.kinds)):
            for cp in self._first(srcs, outs, sems, w):
                cp.wait_send()
            for j, chip in enumerate(chips):
                self._copy(outs, sems, w, 4 + j, (*chip, c), sibling).wait_send()
            self._own(srcs, outs, sems, w, me).wait()


def _peers(x, y, c):
    return [(x ^ a, y ^ b, c ^ e) for a in (0, 1) for b in (0, 1) for e in (0, 1) if (a, b, e) != (0, 0, 0)]


class _Scatter:
    def __init__(self, kinds, grads):
        self.kinds, self.operands = tuple(kinds), list(grads)
        self.tag = "scatter_" + "_".join(kinds)
        self.out_shape = [_sds((N_DEV,) + SHARD_SHAPES[k], BF16) for k in kinds]
        self.scratch = _dma_semaphores(len(kinds))

    def _copies(self, srcs, outs, sems):
        x, y, c = _place()
        me = _block_of(x, y, c)
        copies = []
        for w, kind in enumerate(self.kinds):
            copies.append(pltpu.make_async_copy(_weight_block(srcs[w], kind, me), outs[w].at[me], sems[2].at[w]))
            for k, peer in enumerate(_peers(x, y, c)):
                copies.append(pltpu.make_async_remote_copy(
                    src_ref=_weight_block(srcs[w], kind, _block_of(*peer)), dst_ref=outs[w].at[me],
                    send_sem=sems[0].at[w, k], recv_sem=sems[1].at[w, k], device_id=peer, device_id_type=MESH))
        return copies

    def start(self, srcs, outs, sems):
        for cp in self._copies(srcs, outs, sems):
            cp.start()

    def forward(self, srcs, outs, sems):
        pass

    def finish(self, srcs, outs, sems):
        for cp in self._copies(srcs, outs, sems):
            cp.wait()


N_CHIPS = N_DEV // 2


class _ScatterChips:
    def __init__(self, kinds, blocks):
        self.kinds, self.operands = tuple(kinds), list(blocks)
        self.tag = "scatter_chips_" + "_".join(kinds)
        self.out_shape = [_sds((N_CHIPS,) + SHARD_SHAPES[k], BF16) for k in kinds]
        self.scratch = _dma_semaphores(len(kinds))

    def _copies(self, srcs, outs, sems):
        x, y, c = _place()
        mine = 2 * x + y
        copies = []
        for w in range(len(self.kinds)):
            copies.append(pltpu.make_async_copy(srcs[w].at[mine], outs[w].at[mine], sems[2].at[w]))
            for j, (px, py) in enumerate([(1 - x, y), (x, 1 - y), (1 - x, 1 - y)]):
                copies.append(pltpu.make_async_remote_copy(
                    src_ref=srcs[w].at[2 * px + py], dst_ref=outs[w].at[mine], send_sem=sems[0].at[w, j], recv_sem=sems[1].at[w, j],
                    device_id=(px, py, c), device_id_type=MESH))
        return copies

    def start(self, srcs, outs, sems):
        for cp in self._copies(srcs, outs, sems):
            cp.start()

    def forward(self, srcs, outs, sems):
        pass

    def finish(self, srcs, outs, sems):
        for cp in self._copies(srcs, outs, sems):
            cp.wait()


def _in_hbm(a):
    return pltpu.with_memory_space_constraint(a, pltpu.HBM) if a.size * a.dtype.itemsize >= PINNED_BYTES else a


def _out_hbm(s):
    return pltpu.HBM(s.shape, s.dtype) if math.prod(s.shape) * jnp.dtype(s.dtype).itemsize >= PINNED_BYTES else s


def _call(body, args, *, grid, in_specs, out_specs, out_shape, scratch_shapes=(), vmem_mib=None, name, comm=None):
    kwargs = {} if vmem_mib is None else dict(compiler_params=pltpu.CompilerParams(vmem_limit_bytes=vmem_mib * MIB))
    in_specs, out_specs, out_shape, scratch_shapes = list(in_specs), list(out_specs), list(out_shape), list(scratch_shapes)
    args = [_in_hbm(a) for a in args]
    out_shape = [_out_hbm(s) for s in out_shape]
    if comm is None:
        res = pl.pallas_call(body, grid=grid, in_specs=in_specs, out_specs=out_specs, out_shape=out_shape,
                             scratch_shapes=scratch_shapes, name=name, **kwargs)(*args)
        return list(res), None
    n_in, n_out, n_scr = len(in_specs), len(out_specs), len(scratch_shapes)
    c_in, c_out = len(comm.operands), len(comm.out_shape)
    last = math.prod(grid) - 1

    def carried(*refs):
        cuts = [n_in, c_in, n_out, c_out, n_scr]
        parts, at = [], 0
        for n in cuts:
            parts.append(refs[at:at + n])
            at += n
        ins, c_ins, outs, c_outs, scr = parts
        sems = refs[at:]
        step = pl.program_id(0)
        for axis in range(1, len(grid)):
            step = step * grid[axis] + pl.program_id(axis)

        @pl.when(step == 0)
        def _():
            comm.start(c_ins, c_outs, sems)

        @pl.when(step == last)
        def _():
            comm.forward(c_ins, c_outs, sems)

        body(*ins, *outs, *scr)

        @pl.when(step == last)
        def _():
            comm.finish(c_ins, c_outs, sems)

    res = pl.pallas_call(carried, grid=grid, in_specs=in_specs + [ANY] * c_in, out_specs=out_specs + [ANY] * c_out,
                         out_shape=out_shape + [_out_hbm(s) for s in comm.out_shape], scratch_shapes=scratch_shapes + comm.scratch,
                         name=name + "_" + comm.tag, **kwargs)(*args, *[_in_hbm(a) for a in comm.operands])
    return list(res[:n_out]), list(res[n_out:])


def _comm_only(comm, name):
    def body(*refs):
        n_in, n_out = len(comm.operands), len(comm.out_shape)
        srcs, outs, sems = refs[:n_in], refs[n_in:n_in + n_out], refs[n_in + n_out:]
        comm.start(srcs, outs, sems)
        comm.forward(srcs, outs, sems)
        comm.finish(srcs, outs, sems)

    return pl.pallas_call(body, in_specs=[ANY] * len(comm.operands), out_specs=[ANY] * len(comm.out_shape),
                          out_shape=[_out_hbm(s) for s in comm.out_shape], scratch_shapes=comm.scratch,
                          name=name)(*[_in_hbm(a) for a in comm.operands])


class _GatherSmall:
    def __init__(self, v, tag):
        self.operands, self.tag = [v], tag
        self.out_shape = [_sds((N_DEV,) + v.shape, F32)]
        self.scratch = _dma_semaphores(1)

    def _copies(self, srcs, outs, sems):
        x, y, c = _place()
        me = _block_of(x, y, c)
        copies = [pltpu.make_async_copy(srcs[0], outs[0].at[me], sems[2].at[0])]
        for k, peer in enumerate(_peers(x, y, c)):
            copies.append(pltpu.make_async_remote_copy(src_ref=srcs[0], dst_ref=outs[0].at[me], send_sem=sems[0].at[0, k],
                                                       recv_sem=sems[1].at[0, k], device_id=peer, device_id_type=MESH))
        return copies

    def start(self, srcs, outs, sems):
        for cp in self._copies(srcs, outs, sems):
            cp.start()

    def forward(self, srcs, outs, sems):
        pass

    def finish(self, srcs, outs, sems):
        for cp in self._copies(srcs, outs, sems):
            cp.wait()


class _Together:
    def __init__(self, *comms):
        self.comms = comms
        self.operands = [a for cm in comms for a in cm.operands]
        self.out_shape = [s for cm in comms for s in cm.out_shape]
        self.scratch = [s for cm in comms for s in cm.scratch]
        self.tag = "_".join(cm.tag for cm in comms)

    def _each(self, srcs, outs, sems):
        a = o = s = 0
        for cm in self.comms:
            na, no, ns = len(cm.operands), len(cm.out_shape), len(cm.scratch)
            yield cm, srcs[a:a + na], outs[o:o + no], sems[s:s + ns]
            a, o, s = a + na, o + no, s + ns

    def start(self, srcs, outs, sems):
        for cm, *refs in self._each(srcs, outs, sems):
            cm.start(*refs)

    def forward(self, srcs, outs, sems):
        for cm, *refs in self._each(srcs, outs, sems):
            cm.forward(*refs)

    def finish(self, srcs, outs, sems):
        for cm, *refs in self._each(srcs, outs, sems):
            cm.finish(*refs)


def _rope_tables(pos_col, freq_row, comm=None):
    def body(p_ref, f_ref, c_ref, sa_ref, sb_ref):
        ang = p_ref[...].astype(F32) * f_ref[...]
        lane = lax.broadcasted_iota(jnp.int32, ang.shape, 1) % HEAD_DIM
        cos, sin = jnp.cos(ang), jnp.sin(ang)
        c_ref[...] = jnp.where(lane < ROPE_DIM, cos, 1.0)
        sa_ref[...] = jnp.where(lane < ROPE_DIM // 2, -sin, 0.0)
        sb_ref[...] = jnp.where((lane >= ROPE_DIM // 2) & (lane < ROPE_DIM), sin, 0.0)

    return _call(body, (pos_col, freq_row), grid=(SEQ // ROW_TILE,), in_specs=[_rows(1), _const((1, LANES))],
                 out_specs=[_rows(LANES)] * 3, out_shape=[_sds((SEQ, LANES), F32)] * 3, name="rope_tables", comm=comm)


def _rotate(t, c, sa, sb):
    parts = []
    for g in range(ATTN_WIDTH // LANES):
        tg = t[:, g * LANES:(g + 1) * LANES]
        parts.append(tg * c + pltpu.roll(tg, LANES - 8, axis=1) * sa + pltpu.roll(tg, 8, axis=1) * sb)
    return jnp.concatenate(parts, axis=1)


def _rotate_transposed(dt, c, sa, sb):
    parts = []
    for g in range(ATTN_WIDTH // LANES):
        dg = dt[:, g * LANES:(g + 1) * LANES]
        parts.append(dg * c + pltpu.roll(dg * sa, 8, axis=1) + pltpu.roll(dg * sb, LANES - 8, axis=1))
    return jnp.concatenate(parts, axis=1)


def _inproj_fwd(x, g_pre, w_in, tc, tsa, tsb, comm=None):
    def body(x_ref, g_ref, w_ref, c_ref, sa_ref, sb_ref, q_ref, k_ref, v_ref, u_ref, b_ref, cc_ref):
        y, _ = _rms(x_ref[...])
        h = (y * g_ref[...]).astype(BF16)

        def proj(n):
            return _dot(h, w_ref[:, n * ATTN_WIDTH:(n + 1) * ATTN_WIDTH])

        c, sa, sb = c_ref[...], sa_ref[...], sb_ref[...]
        q_ref[...] = (_rotate(proj(0), c, sa, sb) * (HEAD_DIM ** -0.5)).astype(BF16)
        k_ref[...] = _rotate(proj(1), c, sa, sb).astype(BF16)
        v_ref[...] = proj(2).astype(BF16)
        u_ref[...] = proj(3).astype(BF16)
        b_ref[...] = proj(4).astype(BF16)
        cc_ref[...] = proj(5).astype(BF16)

    a = ATTN_WIDTH
    return _call(
        body, (x, g_pre, w_in, tc, tsa, tsb), grid=(SEQ // ROW_TILE,),
        in_specs=[_rows(D_MODEL), _const((1, D_MODEL)), _const1((D_MODEL, IN_PROJ_WIDTH)), _rows(LANES), _rows(LANES), _rows(LANES)],
        out_specs=[_rows(a)] * 6, out_shape=[_sds((SEQ, a), BF16)] * 6,
        vmem_mib=40, name="inproj_fwd", comm=comm)


def _head_masks():
    lane = lax.broadcasted_iota(jnp.int32, (1, LANES), 1)
    first = lane < HEAD_DIM
    return first, first.astype(F32), 1.0 - first.astype(F32)


def _perm_chunks(dil):
    length = SEQ // dil
    out = []
    for r in range(dil):
        for c0 in range(0, length, PERM_CHUNK):
            chunk = (r * length + c0) // PERM_CHUNK
            rows = pl.ds(c0, PERM_CHUNK) if dil == 1 else pl.ds(r + dil * c0, PERM_CHUNK, stride=dil)
            out.append((chunk, rows))
    return out


def _chunk(c, offset=0):
    return pl.ds(offset + c * PERM_CHUNK, PERM_CHUNK)


def _write_band_bias(bias_ref):
    rr = lax.broadcasted_iota(jnp.int32, (Q_BLOCK, K_WINDOW), 0)
    cc = lax.broadcasted_iota(jnp.int32, (Q_BLOCK, K_WINDOW), 1)
    band = (cc >= rr) & (cc - rr <= 2 * HALF_WINDOW)
    bias_ref[0] = jnp.where(band, 0.0, NEG_INF)
    bias_ref[1] = jnp.where(band & (cc >= HALF_WINDOW), 0.0, NEG_INF)
    bias_ref[2] = jnp.where(band & (cc < Q_BLOCK + HALF_WINDOW), 0.0, NEG_INF)


def _band_bias_index(m0, length):
    return jnp.where(m0 % length == 0, 1, 0) + jnp.where((m0 + Q_BLOCK) % length == 0, 2, 0)


def _zero_key_padding(bufs):
    pad = jnp.zeros((HALF_WINDOW, LANES), BF16)
    for buf in bufs:
        buf[pl.ds(0, HALF_WINDOW), :] = pad
        buf[pl.ds(SEQ + HALF_WINDOW, HALF_WINDOW), :] = pad


def _attn_fwd(q, k, v, comm=None):
    group = 8

    def body(q_ref, k_ref, v_ref, o_ref, lse_ref, q32, k32, v32, qa, qb, kp, vp, accp, mlp,
             acc0, acc1, acc2, ml0, ml1, ml2, bias_ref):
        first, mask_a, mask_b = _head_masks()
        low = lax.broadcasted_iota(jnp.int32, (1, LANES), 1) % HEAD_DIM < HEAD_DIM // 2
        _zero_key_padding((kp, vp))
        _write_band_bias(bias_ref)
        q32[...] = q_ref[...].astype(F32)
        k32[...] = k_ref[...].astype(F32)
        v32[...] = v_ref[...].astype(F32)
        natural = ((acc0, ml0), (acc1, ml1), (acc2, ml2))

        for branch, dil in enumerate(DILATIONS):
            length = SEQ // dil
            assert length >= 2 * Q_BLOCK
            chunks = _perm_chunks(dil)
            for c, rows in chunks:
                val = q32[rows, :]
                qa[_chunk(c), :] = (val * mask_a).astype(BF16)
                qb[_chunk(c), :] = (val * mask_b).astype(BF16)
                kp[_chunk(c, HALF_WINDOW), :] = k32[rows, :].astype(BF16)
                vp[_chunk(c, HALF_WINDOW), :] = v32[rows, :].astype(BF16)
            acc_dst, ml_dst = natural[branch] if dil == 1 else (accp, mlp)

            def blocks(i, carry, length=length, acc_dst=acc_dst, ml_dst=ml_dst):
                base = pl.multiple_of(i * (group * Q_BLOCK), group * Q_BLOCK)
                starts = [base + g * Q_BLOCK for g in range(group)]
                scores = [[_dot_nt(qx[pl.ds(m0, Q_BLOCK), :], kp[pl.ds(m0, K_WINDOW), :]) for qx in (qa, qb)] for m0 in starts]
                probs = []
                for m0, pair in zip(starts, scores):
                    bias = bias_ref[_band_bias_index(m0, length)]
                    stats, ps = [], []
                    for s in pair:
                        s = s + bias
                        m = jnp.max(s, axis=1, keepdims=True)
                        p = jnp.exp(s - m)
                        stats.append(jnp.where(low, m, jnp.sum(p, axis=1, keepdims=True)))
                        ps.append(p.astype(BF16))
                    ml_dst[pl.ds(m0, Q_BLOCK), :] = jnp.where(first, stats[0], stats[1])
                    probs.append(ps)
                for m0, ps in zip(starts, probs):
                    vw = vp[pl.ds(m0, K_WINDOW), :]
                    acc_dst[pl.ds(m0, Q_BLOCK), :] = jnp.where(first, _dot(ps[0], vw), _dot(ps[1], vw))
                return carry

            lax.fori_loop(0, SEQ // (group * Q_BLOCK), blocks, 0)

            if dil > 1:
                for c, rows in chunks:
                    natural[branch][0][rows, :] = accp[_chunk(c), :]
                    natural[branch][1][rows, :] = mlp[_chunk(c), :]

        for c in range(SEQ // PERM_CHUNK):
            packed = [ml[_chunk(c), :] for _, ml in natural]
            ms = [jnp.where(low, ml, pltpu.roll(ml, HEAD_DIM // 2, axis=1)) for ml in packed]
            ls = [jnp.where(low, pltpu.roll(ml, LANES - HEAD_DIM // 2, axis=1), ml) for ml in packed]
            m_all = jnp.maximum(jnp.maximum(ms[0], ms[1]), ms[2])
            es = [jnp.exp(m - m_all) for m in ms]
            z = ls[0] * es[0] + ls[1] * es[1] + ls[2] * es[2]
            num = natural[0][0][_chunk(c), :] * es[0] + natural[1][0][_chunk(c), :] * es[1] + natural[2][0][_chunk(c), :] * es[2]
            o_ref[_chunk(c), :] = num / z
            lse_ref[_chunk(c), :] = m_all + jnp.log(z)

    col = pl.BlockSpec((SEQ, LANES), lambda h: (0, h))
    padded = SEQ + 2 * HALF_WINDOW
    return _call(
        body, (q, k, v), grid=(ATTN_WIDTH // LANES,), in_specs=[col] * 3, out_specs=[col] * 2,
        out_shape=[_sds((SEQ, ATTN_WIDTH), F32)] * 2,
        scratch_shapes=[pltpu.VMEM((SEQ, LANES), F32)] * 3 + [pltpu.VMEM((SEQ, LANES), BF16)] * 2
        + [pltpu.VMEM((padded, LANES), BF16)] * 2 + [pltpu.VMEM((SEQ, LANES), F32)] * 8
        + [pltpu.VMEM((3, Q_BLOCK, K_WINDOW), F32)],
        vmem_mib=52, name="attn_fwd", comm=comm)


def _shifted(t, before, after, i):
    tile = t.shape[0]
    row = lax.broadcasted_iota(jnp.int32, (tile, 1), 0)
    before = jnp.where(i > 0, before, 0.0)
    after = jnp.where(i < SEQ // tile - 1, after, 0.0)
    return (jnp.where(row == 0, before, pltpu.roll(t, 1, axis=0)),
            jnp.where(row == tile - 1, after, pltpu.roll(t, tile - 1, axis=0)))


def _last_row(ref):
    return ref[HALO_ROWS - 1:HALO_ROWS, :].astype(F32)


def _first_row(ref):
    return ref[0:1, :].astype(F32)


def _conv_parts(u, c, u_prev, c_prev, u_next, c_next, cw, i):
    t = c * u
    t_prev, t_next = _shifted(t, _last_row(c_prev) * _last_row(u_prev), _first_row(c_next) * _first_row(u_next), i)
    s = cw[0:1, :] * t_prev + cw[1:2, :] * t + cw[2:3, :] * t_next
    return t, t_prev, t_next, s


def _mix_fwd(attn, u, b, c, conv_w, g_attn, g_conv, w_out, x, g_post):
    def body(a_ref, u_ref, b_ref, c_ref, up_ref, cp_ref, un_ref, cn_ref, cw_ref, ga_ref, gc_ref, w_ref, x_ref, gp_ref,
             x1_ref, mg_ref, mix_ref):
        i = pl.program_id(0)
        _, _, _, s = _conv_parts(u_ref[...].astype(F32), c_ref[...].astype(F32), up_ref, cp_ref, un_ref, cn_ref, cw_ref[...], i)
        ya, _ = _rms(a_ref[...])
        yc, _ = _rms(b_ref[...].astype(F32) * s)
        merged = jnp.concatenate([ya * ga_ref[...], yc * gc_ref[...]], axis=1).astype(BF16)
        mix = _dot(merged, w_ref[...])
        ym, _ = _rms(mix)
        mg_ref[...] = merged.T
        mix_ref[...] = mix.astype(BF16)
        x1_ref[...] = x_ref[...] + ym * gp_ref[...]

    a = ATTN_WIDTH
    return _call(
        body, (attn, u, b, c, u, c, u, c, conv_w, g_attn, g_conv, w_out, x, g_post), grid=(SEQ // ROW_TILE,),
        in_specs=[_rows(a)] * 4 + [_halo_prev(a)] * 2 + [_halo_next(a)] * 2
        + [_const((3, a)), _const((1, a)), _const((1, a)), _const1((D_MODEL, D_MODEL)), _rows(D_MODEL), _const((1, D_MODEL))],
        out_specs=[_rows(D_MODEL), _cols(D_MODEL, ROW_TILE), _rows(D_MODEL)],
        out_shape=[_sds((SEQ, D_MODEL), F32), _sds((D_MODEL, SEQ), BF16), _sds((SEQ, D_MODEL), BF16)],
        vmem_mib=40, name="mix_fwd")[0]


def _gu_spec():
    return pl.BlockSpec((N_DEV, FFN_TILE, FFN_BLOCK), lambda i: (0, i, 0))


def _ffn_fwd(x1, g_pre, w_gu, w_dn, g_post, comm=None, target=None):
    n_tiles = SEQ // ROW_TILE

    def body(*refs):
        if target is None:
            x_ref, g_ref, wgu_ref, wdn_ref, gp_ref, x2_ref, gu_ref, f_ref = refs
        else:
            x_ref, g_ref, wgu_ref, wdn_ref, gp_ref, t_ref, x2_ref, gu_ref, f_ref, loss_ref, acc = refs
        x1 = x_ref[...]
        y, _ = _rms(x1)
        h = (y * g_ref[...]).astype(BF16)
        f = jnp.zeros((ROW_TILE, D_MODEL), F32)

        def gate_up(j):
            return _dot_nt(h, wgu_ref[j]), _dot_nt(h, wgu_ref[j + N_DEV // 2])

        ahead = gate_up(0)
        for j in range(N_DEV // 2):
            gate, up = ahead
            if j + 1 < N_DEV // 2:
                ahead = gate_up(j + 1)
            gu_ref[j] = gate.astype(BF16)
            gu_ref[j + N_DEV // 2] = up.astype(BF16)
            act = (gate * jax.nn.sigmoid(gate) * up).astype(BF16)
            f = f + _dot(act, wdn_ref[pl.ds(j * FFN_BLOCK, FFN_BLOCK), :])
        yf, _ = _rms(f)
        f_ref[...] = f
        x2 = x1 + yf * gp_ref[...]
        if target is None:
            x2_ref[...] = x2
        else:
            i = pl.program_id(0)
            err = x2 - t_ref[...]
            x2_ref[...] = err * (1.0 / D_MODEL)

            @pl.when(i == 0)
            def _():
                acc[...] = jnp.zeros_like(acc)

            acc[...] += jnp.sum(err * err, axis=0, keepdims=True)

            @pl.when(i == n_tiles - 1)
            def _():
                loss_ref[...] = jnp.sum(acc[...], axis=1, keepdims=True) * (0.5 / D_MODEL)

    with_loss = target is not None
    return _call(
        body, (x1, g_pre, w_gu, w_dn, g_post) + ((target,) if with_loss else ()), grid=(n_tiles,),
        in_specs=[_rows(D_MODEL), _const((1, D_MODEL)), _const1((N_DEV, FFN_BLOCK, D_MODEL)), _const1((FFN_HIDDEN, D_MODEL)),
                  _const((1, D_MODEL))] + ([_rows(D_MODEL)] if with_loss else []),
        out_specs=[_rows(D_MODEL), pl.BlockSpec((N_DEV, ROW_TILE, FFN_BLOCK), lambda i: (0, i, 0)), _rows(D_MODEL)]
        + ([_const((1, 1))] if with_loss else []),
        out_shape=[_sds((SEQ, D_MODEL), F32), _sds((N_DEV, SEQ, FFN_BLOCK), BF16), _sds((SEQ, D_MODEL), F32)]
        + ([_sds((1, 1), F32)] if with_loss else []),
        scratch_shapes=[pltpu.VMEM((1, D_MODEL), F32)] if with_loss else [],
        vmem_mib=58, name="ffn_fwd_loss" if with_loss else "ffn_fwd", comm=comm)


def _accumulate(pairs, i):
    @pl.when(i == 0)
    def _():
        for ref, value in pairs:
            ref[...] = value

    @pl.when(i > 0)
    def _():
        for ref, value in pairs:
            ref[...] += value


def _colsum(v):
    return jnp.sum(v, axis=0, keepdims=True)


def _ffn_bwd(dx2, f, x1, gu, w_gu, w_dn, g_post, g_pre, comm=None):
    half = N_DEV // 2

    def body(dx2_ref, f_ref, x1_ref, gu_ref, wgu_ref, wdn_ref, gpost_ref, gpre_ref,
             dx1_ref, df_ref, act_ref, dgu_ref, h_ref, dgpost_ref, dgpre_ref):
        i = pl.program_id(0)
        dx2 = dx2_ref[...]
        yf, rf = _rms(f_ref[...])
        dg_post = _colsum(dx2 * yf)
        df = _rms_bwd(dx2 * gpost_ref[...], yf, rf).astype(BF16)
        df_ref[...] = df
        dh = jnp.zeros((FFN_TILE, D_MODEL), F32)

        def d_act(j):
            return _dot_nt(df, wdn_ref[pl.ds(j * FFN_BLOCK, FFN_BLOCK), :])

        ahead = d_act(0)
        for j in range(half):
            dact = ahead
            if j + 1 < half:
                ahead = d_act(j + 1)
            gate = gu_ref[j].astype(F32)
            up = gu_ref[j + half].astype(F32)
            sig = jax.nn.sigmoid(gate)
            silu = gate * sig
            act_ref[j] = (silu * up).astype(BF16)
            dgate = (dact * up * (sig * (1.0 + gate * (1.0 - sig)))).astype(BF16)
            dup = (dact * silu).astype(BF16)
            dgu_ref[j] = dgate
            dgu_ref[j + half] = dup
            dh = dh + _dot(dgate, wgu_ref[j]) + _dot(dup, wgu_ref[j + half])
        y1, r1 = _rms(x1_ref[...])
        h_ref[...] = (y1 * gpre_ref[...]).astype(BF16)
        dx1_ref[...] = dx2 + _rms_bwd(dh * gpre_ref[...], y1, r1)
        _accumulate([(dgpost_ref, dg_post), (dgpre_ref, _colsum(dh * y1))], i)

    act_spec = pl.BlockSpec((half, FFN_TILE, FFN_BLOCK), lambda i: (0, i, 0))
    return _call(
        body, (dx2, f, x1, gu, w_gu, w_dn, g_post, g_pre), grid=(SEQ // FFN_TILE,),
        in_specs=[_frows(D_MODEL)] * 3 + [_gu_spec(), _const1((N_DEV, FFN_BLOCK, D_MODEL)), _const1((FFN_HIDDEN, D_MODEL)),
                                          _const((1, D_MODEL)), _const((1, D_MODEL))],
        out_specs=[_frows(D_MODEL), _frows(D_MODEL), act_spec, _gu_spec(), _frows(D_MODEL), _const((1, D_MODEL)), _const((1, D_MODEL))],
        out_shape=[_sds((SEQ, D_MODEL), F32), _sds((SEQ, D_MODEL), BF16), _sds((half, SEQ, FFN_BLOCK), BF16),
                   _sds((N_DEV, SEQ, FFN_BLOCK), BF16), _sds((SEQ, D_MODEL), BF16), _sds((1, D_MODEL), F32), _sds((1, D_MODEL), F32)],
        vmem_mib=52, name="ffn_bwd", comm=comm)


def _wgrad(a_t, g, name, comm=None):
    width = g.shape[1]

    def body(a_ref, g_ref, o_ref):
        o_ref[...] = _dot(a_ref[...], g_ref[...]).astype(BF16)

    res, landed = _call(body, (a_t, g), grid=(width // WGRAD_COLS,),
                        in_specs=[_const1((D_MODEL, SEQ)), pl.BlockSpec((SEQ, WGRAD_COLS), lambda j: (0, j))],
                        out_specs=[pl.BlockSpec((D_MODEL, WGRAD_COLS), lambda j: (0, j))], out_shape=[_sds((D_MODEL, width), BF16)],
                        vmem_mib=48, name=name, comm=comm)
    return res[0], landed


def _wgrad_paired(a, g, kind, comm=None, to_owners=False):
    shard = SHARD_SHAPES[kind]
    per_chip = 2 if kind == "gu" else 1
    n_compute = per_chip * N_CHIPS
    n_steps = n_compute + 1

    def block(t):
        return jnp.minimum(t, n_compute - 1)

    if kind == "in":
        specs = [_const1((D_MODEL, SEQ)), pl.BlockSpec((SEQ, 2 * W_IN_BLOCK), lambda t: (0, block(t)))]
    else:
        specs = [pl.BlockSpec((1, SEQ, FFN_BLOCK), lambda t: (block(t), 0, 0)), _const1((SEQ, D_MODEL))]

    def body(a_ref, g_ref, o_ref, *rest):
        if to_owners:
            landing, sendbuf, recvbuf, keepbuf, send_sem, recv_sem, donebuf, owner_send, owner_recv, own_sem = rest
        else:
            sendbuf, recvbuf, keepbuf, send_sem, recv_sem = rest
        t = pl.program_id(0)
        x, y, c = _place()

        def exchange(q):
            return pltpu.make_async_remote_copy(src_ref=sendbuf.at[q], dst_ref=recvbuf.at[q], send_sem=send_sem.at[q],
                                                recv_sem=recv_sem.at[q], device_id=(x, y, 1 - c), device_id_type=MESH)

        my_chip = 2 * x + y

        def to_owner(q):
            return pltpu.make_async_remote_copy(src_ref=donebuf.at[q], dst_ref=landing.at[my_chip], send_sem=owner_send.at[q],
                                                recv_sem=owner_recv.at[my_chip], device_id=(q // 2, q % 2, c), device_id_type=MESH)

        def keep_own():
            return pltpu.make_async_copy(donebuf.at[my_chip], landing.at[my_chip], own_sem)

        @pl.when((t >= per_chip) & (t % per_chip == 0))
        def _():
            q = t // per_chip - 1
            exchange(q).wait_recv()
            done = (keepbuf[...] + recvbuf[q].astype(F32)).astype(BF16)
            o_ref[0] = done
            if to_owners:
                donebuf[q] = done

                @pl.when(q == my_chip)
                def _():
                    keep_own().start()

                @pl.when(q != my_chip)
                def _():
                    to_owner(q).start()

        @pl.when(t < n_compute)
        def _():
            q = t // per_chip
            if kind == "gu":
                r = _dot_tn(a_ref[0], g_ref[...])

                @pl.when(t % 2 == c)
                def _():
                    keepbuf[...] = r

                @pl.when(t % 2 != c)
                def _():
                    sendbuf[q] = r.astype(BF16)
                    exchange(q).start()
            else:
                if kind == "dn":
                    r = _dot_tn(a_ref[0], g_ref[...])
                    lower, upper = r[:W_DOWN_BLOCK], r[W_DOWN_BLOCK:]
                else:
                    r = _dot(a_ref[...], g_ref[...])
                    lower, upper = r[:, :W_IN_BLOCK], r[:, W_IN_BLOCK:]
                keepbuf[...] = jnp.where(c == 0, lower, upper)
                sendbuf[q] = jnp.where(c == 0, upper, lower).astype(BF16)
                exchange(q).start()

        @pl.when(t == n_steps - 1)
        def _():
            for q in range(N_CHIPS):
                exchange(q).wait_send()
            if to_owners:
                keep_own().wait()
                for s in range(N_CHIPS):
                    @pl.when(s != my_chip)
                    def _(s=s):
                        pltpu.make_async_remote_copy(src_ref=donebuf.at[s], dst_ref=landing.at[s], send_sem=owner_send.at[s],
                                                     recv_sem=owner_recv.at[s], device_id=(x, y, c), device_id_type=MESH).wait()

    slots = (N_CHIPS,) + shard
    owners = to_owners
    res, landed = _call(
        body, (a, g), grid=(n_steps,), in_specs=specs,
        out_specs=[pl.BlockSpec((1,) + shard, lambda t: (jnp.clip(t // per_chip - 1, 0, N_CHIPS - 1), 0, 0))] + [ANY] * owners,
        out_shape=[_sds(slots, BF16)] * (1 + owners),
        scratch_shapes=[pltpu.VMEM(slots, BF16), pltpu.VMEM(slots, BF16), pltpu.VMEM(shard, F32),
                        pltpu.SemaphoreType.DMA((N_CHIPS,)), pltpu.SemaphoreType.DMA((N_CHIPS,))]
        + [pltpu.VMEM(slots, BF16), pltpu.SemaphoreType.DMA((N_CHIPS,)), pltpu.SemaphoreType.DMA((N_CHIPS,)), pltpu.SemaphoreType.DMA] * owners,
        vmem_mib=52, name="wgrad_" + kind + "_paired" + "_to_owners" * owners, comm=comm)
    return res[0], landed, (res[1] if to_owners else None)


def _mix_bwd(dx1, mix, attn, u, b, c, conv_w, g_attn, g_conv, g_post, w_out):
    def body(dx1_ref, mix_ref, a_ref, u_ref, b_ref, c_ref, up_ref, cp_ref, un_ref, cn_ref, cw_ref, ga_ref, gc_ref, gp_ref, w_ref,
             dmix_ref, da_ref, ds_ref, db_ref, dgp_ref, dga_ref, dgc_ref):
        i = pl.program_id(0)
        dx1 = dx1_ref[...]
        ym, rm = _rms(mix_ref[...].astype(F32))
        dg_post = _colsum(dx1 * ym)
        dmix = _rms_bwd(dx1 * gp_ref[...], ym, rm).astype(BF16)
        dmix_ref[...] = dmix
        dmerged = _dot_nt(dmix, w_ref[...])
        dna, dnc = dmerged[:, :ATTN_WIDTH], dmerged[:, ATTN_WIDTH:]
        ya, ra = _rms(a_ref[...])
        da_ref[...] = _rms_bwd(dna * ga_ref[...], ya, ra)
        _, _, _, s = _conv_parts(u_ref[...].astype(F32), c_ref[...].astype(F32), up_ref, cp_ref, un_ref, cn_ref, cw_ref[...], i)
        gate_b = b_ref[...].astype(F32)
        yc, rc = _rms(gate_b * s)
        dy = _rms_bwd(dnc * gc_ref[...], yc, rc)
        db_ref[...] = (dy * s).astype(BF16)
        ds_ref[...] = (dy * gate_b).astype(BF16)
        _accumulate([(dgp_ref, dg_post), (dga_ref, _colsum(dna * ya)), (dgc_ref, _colsum(dnc * yc))], i)

    a = ATTN_WIDTH
    return _call(
        body, (dx1, mix, attn, u, b, c, u, c, u, c, conv_w, g_attn, g_conv, g_post, w_out), grid=(SEQ // ROW_TILE,),
        in_specs=[_rows(D_MODEL)] * 2 + [_rows(a)] * 4 + [_halo_prev(a)] * 2 + [_halo_next(a)] * 2
        + [_const((3, a)), _const((1, a)), _const((1, a)), _const((1, D_MODEL)), _const1((D_MODEL, D_MODEL))],
        out_specs=[_rows(D_MODEL)] + [_rows(a)] * 3 + [_const((1, D_MODEL)), _const((1, a)), _const((1, a))],
        out_shape=[_sds((SEQ, D_MODEL), BF16), _sds((SEQ, a), F32), _sds((SEQ, a), BF16), _sds((SEQ, a), BF16),
                   _sds((1, D_MODEL), F32), _sds((1, a), F32), _sds((1, a), F32)],
        vmem_mib=40, name="mix_bwd")[0]


def _attn_bwd(q, k, v, do, o, lse, comm=None):
    group = 8

    n_pairs = ATTN_WIDTH // LANES

    def body(q_hbm, k_hbm, v_hbm, do_hbm, o_hbm, lse_hbm, dq_ref, dk_ref, dv_ref,
             q_ref, k_ref, v_ref, do_ref, o_ref, lse_ref, in_sem,
             q32, k32, v32, qa, qb, doa, dob, kp, vp, lsep, dlp, dnat, dqp, dkp, dvp, bias_ref):
        pair = pl.program_id(0)
        sources = (q_hbm, k_hbm, v_hbm, do_hbm, o_hbm, lse_hbm)
        buffers = (q_ref, k_ref, v_ref, do_ref, o_ref, lse_ref)

        def fetch(i, p):
            return pltpu.make_async_copy(sources[i].at[:, pl.ds(pl.multiple_of(p * LANES, LANES), LANES)], buffers[i], in_sem.at[i])

        def prefetch(*which):
            @pl.when(pair + 1 < n_pairs)
            def _():
                for i in which:
                    fetch(i, pair + 1).start()

        @pl.when(pair == 0)
        def _():
            for i in range(len(sources)):
                fetch(i, 0).start()

        for i in range(len(sources)):
            fetch(i, pair).wait()

        first, mask_a, mask_b = _head_masks()
        _zero_key_padding((kp, vp))
        _write_band_bias(bias_ref)
        q32[...] = q_ref[...].astype(F32)
        k32[...] = k_ref[...].astype(F32)
        v32[...] = v_ref[...].astype(F32)
        prefetch(0, 1, 2)
        for c in range(SEQ // PERM_CHUNK):
            prod = do_ref[_chunk(c), :] * o_ref[_chunk(c), :]
            d_a = jnp.sum(prod * mask_a, axis=1, keepdims=True)
            d_b = jnp.sum(prod * mask_b, axis=1, keepdims=True)
            dnat[_chunk(c), :] = jnp.where(first, d_a, d_b)
        prefetch(4)

        for step, dil in enumerate(DILATIONS[1:] + DILATIONS[:1]):
            length = SEQ // dil
            assert length >= 2 * Q_BLOCK
            chunks = _perm_chunks(dil)
            for c, rows in chunks:
                val = q32[rows, :]
                qa[_chunk(c), :] = (val * mask_a).astype(BF16)
                qb[_chunk(c), :] = (val * mask_b).astype(BF16)
                val = do_ref[rows, :]
                doa[_chunk(c), :] = (val * mask_a).astype(BF16)
                dob[_chunk(c), :] = (val * mask_b).astype(BF16)
                kp[_chunk(c, HALF_WINDOW), :] = k32[rows, :].astype(BF16)
                vp[_chunk(c, HALF_WINDOW), :] = v32[rows, :].astype(BF16)
                lsep[_chunk(c), :] = lse_ref[rows, :]
                dlp[_chunk(c), :] = dnat[rows, :]
            if step == len(DILATIONS) - 1:
                prefetch(3, 5)
            zero = jnp.zeros((PERM_CHUNK, LANES), F32)
            for c in range(SEQ // PERM_CHUNK):
                dkp[_chunk(c), :] = zero
                dvp[_chunk(c), :] = zero
            dkp[pl.ds(SEQ, 2 * HALF_WINDOW), :] = zero[:2 * HALF_WINDOW]
            dvp[pl.ds(SEQ, 2 * HALF_WINDOW), :] = zero[:2 * HALF_WINDOW]

            heads = ((qa, doa, 0), (qb, dob, HEAD_DIM))

            def blocks(i, carry, length=length):
                base = pl.multiple_of(i * (group * Q_BLOCK), group * Q_BLOCK)
                starts = [base + g * Q_BLOCK for g in range(group)]
                raw = [[(_dot_nt(qx[pl.ds(m0, Q_BLOCK), :], kp[pl.ds(m0, K_WINDOW), :]),
                         _dot_nt(dox[pl.ds(m0, Q_BLOCK), :], vp[pl.ds(m0, K_WINDOW), :])) for qx, dox, _ in heads]
                       for m0 in starts]
                grads = []
                for m0, pair in zip(starts, raw):
                    bias = bias_ref[_band_bias_index(m0, length)]
                    lse_b, d_b = lsep[pl.ds(m0, Q_BLOCK), :], dlp[pl.ds(m0, Q_BLOCK), :]
                    out = []
                    for (s, dp), (_, _, col) in zip(pair, heads):
                        p = jnp.exp(s + bias - lse_b[:, col:col + 1])
                        out.append(((p * (dp - d_b[:, col:col + 1])).astype(BF16), p.astype(BF16)))
                    grads.append(out)
                for m0, out in zip(starts, grads):
                    qrows, krows = pl.ds(m0, Q_BLOCK), pl.ds(m0, K_WINDOW)
                    kw = kp[krows, :]
                    dk = jnp.zeros((K_WINDOW, LANES), F32)
                    dv = jnp.zeros((K_WINDOW, LANES), F32)
                    for (ds, p), (qx, dox, _) in zip(out, heads):
                        dk = dk + _dot_tn(ds, qx[qrows, :])
                        dv = dv + _dot_tn(p, dox[qrows, :])
                    dqp[qrows, :] = jnp.where(first, _dot(out[0][0], kw), _dot(out[1][0], kw)) * (HEAD_DIM ** -0.5)
                    dkp[krows, :] += dk
                    dvp[krows, :] += dv
                return carry

            lax.fori_loop(0, SEQ // (group * Q_BLOCK), blocks, 0)

            for c, rows in chunks:
                g_q, g_k, g_v = dqp[_chunk(c), :], dkp[_chunk(c, HALF_WINDOW), :], dvp[_chunk(c, HALF_WINDOW), :]
                if step == 0:
                    dq_ref[rows, :] = g_q
                    dk_ref[rows, :] = g_k
                    dv_ref[rows, :] = g_v
                else:
                    dq_ref[rows, :] = dq_ref[rows, :] + g_q
                    dk_ref[rows, :] = dk_ref[rows, :] + g_k
                    dv_ref[rows, :] = dv_ref[rows, :] + g_v

    col = pl.BlockSpec((SEQ, LANES), lambda h: (0, h))
    padded = SEQ + 2 * HALF_WINDOW
    return _call(
        body, (q, k, v, do, o, lse), grid=(n_pairs,), in_specs=[ANY] * 6, out_specs=[col] * 3,
        out_shape=[_sds((SEQ, ATTN_WIDTH), F32)] * 3,
        scratch_shapes=[pltpu.VMEM((SEQ, LANES), BF16)] * 3 + [pltpu.VMEM((SEQ, LANES), F32)] * 3 + [pltpu.SemaphoreType.DMA((6,))]
        + [pltpu.VMEM((SEQ, LANES), F32)] * 3 + [pltpu.VMEM((SEQ, LANES), BF16)] * 4
        + [pltpu.VMEM((padded, LANES), BF16)] * 2 + [pltpu.VMEM((SEQ, LANES), F32)] * 4 + [pltpu.VMEM((padded, LANES), F32)] * 2
        + [pltpu.VMEM((3, Q_BLOCK, K_WINDOW), F32)],
        vmem_mib=56, name="attn_bwd", comm=comm)


def _inproj_bwd(dq, dk, dv, ds, db, u, c, conv_w, tc, tsa, tsb, w_in, x, g_pre, dx1):
    def body(dq_ref, dk_ref, dv_ref, ds_ref, db_ref, u_ref, c_ref, dsp_ref, up_ref, cp_ref, dsn_ref, un_ref, cn_ref, cw_ref,
             tc_ref, tsa_ref, tsb_ref, w_ref, x_ref, g_ref, dx1_ref, dx_ref, dproj_ref, h_ref, dg_ref, dcw_ref):
        i = pl.program_id(0)
        cw = cw_ref[...]
        u, c = u_ref[...].astype(F32), c_ref[...].astype(F32)
        t, t_prev, t_next, _ = _conv_parts(u, c, up_ref, cp_ref, un_ref, cn_ref, cw, i)
        ds = ds_ref[...].astype(F32)
        ds_prev, ds_next = _shifted(ds, _last_row(dsp_ref), _first_row(dsn_ref), i)
        dt = cw[0:1, :] * ds_next + cw[1:2, :] * ds + cw[2:3, :] * ds_prev
        d_taps = jnp.concatenate([_colsum(ds * t_prev), _colsum(ds * t), _colsum(ds * t_next)], axis=0)
        tc_, tsa_, tsb_ = tc_ref[...], tsa_ref[...], tsb_ref[...]
        groups = ((2, lambda: dv_ref[...].astype(BF16)), (4, lambda: db_ref[...]),
                  (0, lambda: _rotate_transposed(dq_ref[...], tc_, tsa_, tsb_).astype(BF16)),
                  (1, lambda: _rotate_transposed(dk_ref[...], tc_, tsa_, tsb_).astype(BF16)),
                  (3, lambda: (dt * c).astype(BF16)), (5, lambda: (dt * u).astype(BF16)))
        dh = jnp.zeros((x_ref.shape[0], D_MODEL), F32)
        for n, make in groups:
            cols = pl.ds(n * a, a)
            part = make()
            dproj_ref[:, cols] = part
            dh = dh + _dot_nt(part, w_ref[:, cols])
        y, r = _rms(x_ref[...])
        h_ref[...] = (y * g_ref[...]).astype(BF16).T
        dx_ref[...] = dx1_ref[...] + _rms_bwd(dh * g_ref[...], y, r)
        _accumulate([(dg_ref, _colsum(dh * y)), (dcw_ref, d_taps)], i)

    a = ATTN_WIDTH
    tile = FFN_TILE

    def rows(width):
        return _rows(width, tile)

    return _call(
        body, (dq, dk, dv, ds, db, u, c, ds, u, c, ds, u, c, conv_w, tc, tsa, tsb, w_in, x, g_pre, dx1), grid=(SEQ // tile,),
        in_specs=[rows(a)] * 7 + [_halo_prev(a, tile)] * 3 + [_halo_next(a, tile)] * 3
        + [_const((3, a)), rows(LANES), rows(LANES), rows(LANES), _const1((D_MODEL, IN_PROJ_WIDTH)), rows(D_MODEL),
           _const((1, D_MODEL)), rows(D_MODEL)],
        out_specs=[rows(D_MODEL), rows(IN_PROJ_WIDTH), _cols(D_MODEL, tile), _const((1, D_MODEL)), _const((3, a))],
        out_shape=[_sds((SEQ, D_MODEL), F32), _sds((SEQ, IN_PROJ_WIDTH), BF16), _sds((D_MODEL, SEQ), BF16),
                   _sds((1, D_MODEL), F32), _sds((3, a), F32)],
        vmem_mib=48, name="inproj_bwd")


def _adamw_math(w, g, m, v):
    m = ADAM_B1 * m + (1.0 - ADAM_B1) * g
    v = ADAM_B2 * v + (1.0 - ADAM_B2) * (g * g)
    m_hat = m / (1.0 - ADAM_B1 ** ADAM_STEP)
    v_hat = v / (1.0 - ADAM_B2 ** ADAM_STEP)
    delta = -ADAM_LR * (m_hat / (jnp.sqrt(v_hat) + ADAM_EPS) + ADAM_WD * w)
    return delta, m, v


def _sum_parts(p_ref):
    g = p_ref[0].astype(F32)
    for k in range(1, p_ref.shape[0]):
        g = g + p_ref[k].astype(F32)
    return g


def _adamw_layers(parts, w, m, v, row_tile, name, comm=None):
    _, rows, cols = w.shape
    n_tiles = rows // row_tile

    def body(*refs):
        p_refs = refs[:DEPTH]
        w_ref, m_ref, v_ref, g_ref, d_ref, nm_ref, nv_ref = refs[DEPTH:]
        layer = pl.program_id(0)
        for l, p_ref in enumerate(p_refs):
            @pl.when(layer == l)
            def _(p_ref=p_ref):
                g = _sum_parts(p_ref)
                g_ref[0] = g
                d_ref[0], nm_ref[0], nv_ref[0] = _adamw_math(w_ref[0], g, m_ref[0], v_ref[0])

    def part_spec(l):
        return pl.BlockSpec((parts[l].shape[0], row_tile, cols),
                            lambda layer, i: (0, jnp.where(layer == l, i, jnp.where(layer < l, 0, n_tiles - 1)), 0))

    tile = pl.BlockSpec((1, row_tile, cols), lambda layer, i: (layer, i, 0))
    return _call(body, (*parts, w, m, v), grid=(DEPTH, n_tiles), in_specs=[part_spec(l) for l in range(DEPTH)] + [tile] * 3,
                 out_specs=[tile] * 4, out_shape=[_sds(w.shape, F32)] * 4, name=name, comm=comm)


def _adamw_sum8(parts, w, m, v, name):
    def body(p_ref, w_ref, m_ref, v_ref, g_ref, d_ref, nm_ref, nv_ref):
        g = _sum_parts(p_ref)
        g_ref[...] = g
        d_ref[...], nm_ref[...], nv_ref[...] = _adamw_math(w_ref[...], g, m_ref[...], v_ref[...])

    return pl.pallas_call(body, out_shape=[_sds(w.shape, F32)] * 4, name=name)(parts, w, m, v)


def _adamw_plain(g, w, m, v, name):
    def body(g_ref, w_ref, m_ref, v_ref, d_ref, nm_ref, nv_ref):
        d_ref[...], nm_ref[...], nv_ref[...] = _adamw_math(w_ref[...], g_ref[...], m_ref[...], v_ref[...])

    return pl.pallas_call(body, out_shape=[_sds(w.shape, F32)] * 3, name=name)(g, w, m, v)


SMALL_ROWS = 8
GAIN_NAMES = ("pre_mix_norm", "post_mix_norm", "pre_ffn_norm", "post_ffn_norm", "attn_out_norm", "conv_out_norm")


def _pack_small(pre_mix, post_mix, pre_ffn, post_ffn, attn_out, conv_out, taps):
    zeros = jnp.zeros((1, ATTN_WIDTH), F32)
    return jnp.concatenate([
        pre_mix, post_mix, pre_ffn, post_ffn, jnp.concatenate([attn_out, conv_out], axis=1),
        jnp.concatenate([taps[0:1], taps[1:2]], axis=1), jnp.concatenate([taps[2:3], zeros], axis=1),
        jnp.zeros((1, D_MODEL), F32)], axis=0)


def _rope(positions, comm=None):
    inv_freq = ROPE_THETA ** (-jnp.arange(0, ROPE_DIM, 2, dtype=F32) / ROPE_DIM)
    per_head = jnp.concatenate([inv_freq, inv_freq, jnp.zeros((HEAD_DIM - ROPE_DIM,), F32)])
    freq_row = jnp.tile(per_head, LANES // HEAD_DIM).reshape(1, LANES)
    return _rope_tables(positions.reshape(SEQ, 1), freq_row, comm)


def _layer_forward(h, gains, taps, tables, w, inproj_comm=None, attn_comm=None, ffn_comm=None, target=None):
    (q, k, v, u, b, c), landed = _inproj_fwd(h, gains["pre_mix_norm"], w["in"], *tables, comm=inproj_comm)
    if inproj_comm is not None:
        w = {**w, **dict(zip(inproj_comm.kinds, landed))}
    (attn, lse), landed = _attn_fwd(q, k, v, comm=attn_comm)
    if attn_comm is not None:
        w = {**w, **dict(zip(attn_comm.kinds, landed))}
    x1, merged, mix = _mix_fwd(attn, u, b, c, taps, gains["attn_out_norm"], gains["conv_out_norm"], w["out"], h, gains["post_mix_norm"])
    (x2, gu, f, *loss), landed_next = _ffn_fwd(x1, gains["pre_ffn_norm"], w["gu"], w["dn"], gains["post_ffn_norm"], comm=ffn_comm,
                                               target=target)
    out = x2 if target is None else (x2, loss[0])
    return out, (h, q, k, v, u, b, c, attn, lse, merged, mix, x1, gu, f), w, landed_next


class _GradientExchange:
    def ffn_grads(self, act, df, h2, dgu):
        return dict(dn=_wgrad_paired(act, df, "dn")[0], gu=_wgrad_paired(dgu, h2, "gu")[0])

    def attention_passenger(self, ffn):
        return _ScatterChips(("gu", "dn"), (ffn["gu"], ffn["dn"]))

    def w_out_grad(self, merged_t, dmix):
        return _wgrad(merged_t, dmix, "wgrad_out")[0]

    def w_in_grad(self, h_t, dproj, g_out, last):
        comm = _Scatter(("out",), (g_out,))
        if last:
            _, landed, g_in = _wgrad_paired(h_t, dproj, "in", comm=comm, to_owners=True)
        else:
            g_in, landed = _wgrad(h_t, dproj, "wgrad_in", comm=comm)
        return g_in, landed[0]


def _layer_backward(dx, saved, gains, taps, tables, w, ffn_comm, exchange, last):
    x0, q, k, v, u, b, c, attn, lse, merged, mix, x1, gu, f = saved
    (dx1, df, act, dgu, h2, dg_post_ffn, dg_pre_ffn), landed_prev = _ffn_bwd(
        dx, f, x1, gu, w["gu"], w["dn"], gains["post_ffn_norm"], gains["pre_ffn_norm"], comm=ffn_comm)
    ffn = exchange.ffn_grads(act, df, h2, dgu)
    dmix, dattn, ds, db, dg_post_mix, dg_attn, dg_conv = _mix_bwd(
        dx1, mix, attn, u, b, c, taps, gains["attn_out_norm"], gains["conv_out_norm"], gains["post_mix_norm"], w["out"])
    g_out = exchange.w_out_grad(merged, dmix)
    attn_comm = exchange.attention_passenger(ffn)
    (dq, dk, dv), landed_ffn = _attn_bwd(q, k, v, dattn, attn, lse, comm=attn_comm)
    dx0, dproj, h1, dg_pre_mix, dtaps = _inproj_bwd(
        dq, dk, dv, ds, db, u, c, taps, *tables, w["in"], x0, gains["pre_mix_norm"], dx1)[0]
    g_in, out_parts = exchange.w_in_grad(h1, dproj, g_out, last)
    small = _pack_small(dg_pre_mix, dg_post_mix, dg_pre_ffn, dg_post_ffn, dg_attn, dg_conv, dtaps)
    rest = dict(zip(attn_comm.kinds, landed_ffn)) if attn_comm is not None else dict(ffn)
    rest["out"] = out_parts
    return dx0, g_in, small, landed_prev, rest


def kernel(x, positions, pre_mix_norm, w_in, conv_w, attn_out_norm, conv_out_norm, w_out, post_mix_norm, pre_ffn_norm, w_gate_up, w_down, post_ffn_norm, loss_target, m_pre_mix_norm, m_w_in, m_conv_w, m_attn_out_norm, m_conv_out_norm, m_w_out, m_post_mix_norm, m_pre_ffn_norm, m_w_gate_up, m_w_down, m_post_ffn_norm, v_pre_mix_norm, v_w_in, v_conv_w, v_attn_out_norm, v_conv_out_norm, v_w_out, v_post_mix_norm, v_pre_ffn_norm, v_w_gate_up, v_w_down, v_post_ffn_norm):
    mx, my, mc = _place()
    me = _block_of(mx, my, mc)
    conv_channels = conv_w.shape[-1]

    def hidden_major(a):
        return jnp.swapaxes(a, 1, 2)

    big_w = dict(zip(WEIGHT_KINDS, (w_in, w_out, hidden_major(w_gate_up), w_down)))
    big_m = dict(zip(WEIGHT_KINDS, (m_w_in, m_w_out, hidden_major(m_w_gate_up), m_w_down)))
    big_v = dict(zip(WEIGHT_KINDS, (v_w_in, v_w_out, hidden_major(v_w_gate_up), v_w_down)))
    shards = {kind: big_w[kind].astype(BF16) for kind in WEIGHT_KINDS}
    all_gains = dict(pre_mix_norm=pre_mix_norm, attn_out_norm=attn_out_norm, conv_out_norm=conv_out_norm, post_mix_norm=post_mix_norm,
                     pre_ffn_norm=pre_ffn_norm, post_ffn_norm=post_ffn_norm)

    def gather(kinds, l):
        return _Gather(kinds, [shards[kind][l] for kind in kinds])

    def gains(l):
        return {name: g[l:l + 1] for name, g in all_gains.items()}

    early = ("in", "out", "dn")
    taps_flat = jnp.pad(conv_w.reshape(-1), (0, 8 * LANES - conv_w.size)).reshape(8, LANES)
    tables, (first, taps_all) = _rope(positions, _Together(gather(early[:1], 0), _GatherSmall(taps_flat, "taps")))
    taps_all = taps_all.reshape(N_DEV, 8 * LANES)[:, :conv_w.size]
    conv_w_full = taps_all.reshape(N_DEV, DEPTH, 3, conv_channels).transpose(1, 2, 0, 3).reshape(DEPTH, 3, ATTN_WIDTH)
    weights = [dict(zip(early[:1], [first]))] + [None] * (DEPTH - 1)
    saved = [None] * DEPTH
    h = x[0]
    for l in range(DEPTH):
        inproj_comm = gather(early[1:], 0) if l == 0 else None
        ffn_comm = gather(early, l + 1) if l + 1 < DEPTH else None
        target = loss_target[0] if l + 1 == DEPTH else None
        h, saved[l], weights[l], landed = _layer_forward(h, gains(l), conv_w_full[l], tables, weights[l], inproj_comm,
                                                         gather(("gu",), l), ffn_comm, target)
        if ffn_comm is not None:
            weights[l + 1] = dict(zip(ffn_comm.kinds, landed))
    dx, local_loss = h

    parts = {kind: [None] * DEPTH for kind in WEIGHT_KINDS}
    small_grads = [None] * DEPTH
    g_in_above = None
    exchange = _GradientExchange()
    for l in reversed(range(DEPTH)):
        ffn_comm = _Scatter(("in",), (g_in_above,)) if g_in_above is not None else None
        dx, g_in_above, small_grads[l], landed, rest = _layer_backward(dx, saved[l], gains(l), conv_w_full[l], tables, weights[l], ffn_comm,
                                                                      exchange, l == 0)
        if ffn_comm is not None:
            parts["in"][l + 1] = landed[0]
        for kind, part in rest.items():
            parts[kind][l] = part

    parts["in"][0] = g_in_above
    packed = jnp.concatenate(small_grads, axis=0)
    packed = lax.dynamic_update_slice(packed, local_loss, (SMALL_ROWS - 1, 0))
    gathered = _comm_only(_GatherSmall(packed, "small"), "exchange_last")[0]
    tiles = {"in": 256, "out": 128, "gu": 176, "dn": 176}
    big = {kind: _adamw_layers(parts[kind], big_w[kind], big_m[kind], big_v[kind], tiles[kind], "adamw_" + kind)[0]
           for kind in WEIGHT_KINDS}


    def pack_state(state):
        rows = [_pack_small(*[state[name][l:l + 1] for name in GAIN_NAMES], jnp.zeros((3, ATTN_WIDTH), F32)) for l in range(DEPTH)]
        return jnp.concatenate(rows, axis=0)

    sw = pack_state(all_gains)
    sm = pack_state(dict(zip(GAIN_NAMES, (m_pre_mix_norm, m_post_mix_norm, m_pre_ffn_norm, m_post_ffn_norm, m_attn_out_norm, m_conv_out_norm))))
    sv = pack_state(dict(zip(GAIN_NAMES, (v_pre_mix_norm, v_post_mix_norm, v_pre_ffn_norm, v_post_ffn_norm, v_attn_out_norm, v_conv_out_norm))))
    sg, sd, snm, snv = _adamw_sum8(gathered, sw, sm, sv, "adamw_small")

    def unpack(p):
        p = p.reshape(DEPTH, SMALL_ROWS, D_MODEL)
        return dict(pre_mix_norm=p[:, 0], post_mix_norm=p[:, 1], pre_ffn_norm=p[:, 2], post_ffn_norm=p[:, 3],
                    attn_out_norm=p[:, 4, :ATTN_WIDTH], conv_out_norm=p[:, 4, ATTN_WIDTH:])

    small = [unpack(p) for p in (sg, sd, snm, snv)]
    loss = sg[SMALL_ROWS - 1, 0]
    sg3 = sg.reshape(DEPTH, SMALL_ROWS, D_MODEL)
    taps_grad_full = jnp.stack([sg3[:, 5, :ATTN_WIDTH], sg3[:, 5, ATTN_WIDTH:], sg3[:, 6, :ATTN_WIDTH]], axis=1)
    taps_grad = lax.dynamic_slice_in_dim(taps_grad_full, me * conv_channels, conv_channels, axis=2)

    def flat(a):
        return a.reshape(DEPTH * 3, conv_channels)

    td, tnm, tnv = _adamw_plain(flat(taps_grad), flat(conv_w), flat(m_conv_w), flat(v_conv_w), "adamw_taps")
    taps = [taps_grad] + [a.reshape(conv_w.shape) for a in (td, tnm, tnv)]

    def leaves(o):
        s = small[o]
        return (s["pre_mix_norm"], big["in"][o], taps[o], s["attn_out_norm"], s["conv_out_norm"], big["out"][o], s["post_mix_norm"],
                s["pre_ffn_norm"], hidden_major(big["gu"][o]), big["dn"][o], s["post_ffn_norm"])

    return (loss, dx[None], *leaves(0), *leaves(1), *leaves(2), *leaves(3))
```

```python
import math

import jax
import jax.numpy as jnp
from jax import lax
from jax.experimental import pallas as pl
from jax.experimental.pallas import tpu as pltpu

F32 = jnp.float32
BF16 = jnp.bfloat16
MESH = pl.DeviceIdType.MESH

SEQ = 4096
D_MODEL = 1024
DEPTH = 4
N_DEV = 8
ATTN_WIDTH = 512
IN_PROJ_WIDTH = 3072
FFN_HIDDEN = 2816
FFN_BLOCK = 2 * FFN_HIDDEN // N_DEV
W_IN_BLOCK = IN_PROJ_WIDTH // N_DEV
W_OUT_BLOCK = D_MODEL // N_DEV
W_DOWN_BLOCK = FFN_HIDDEN // N_DEV
HEAD_DIM = 64
ROPE_DIM = 16
ROPE_THETA = 500000.0
DILATIONS = (1, 4, 16)
HALF_WINDOW = 64
RMS_EPS = 1e-6
NEG_INF = -1e30
LANES = 128
Q_BLOCK = 128
K_WINDOW = Q_BLOCK + 2 * HALF_WINDOW
PERM_CHUNK = 256
ROW_TILE = 512
FFN_TILE = 256
WGRAD_COLS = 512
ADAM_LR, ADAM_B1, ADAM_B2, ADAM_EPS, ADAM_WD, ADAM_STEP = 0.001, 0.9, 0.999, 1e-08, 0.01, 10
MIB = 1024 * 1024
PINNED_BYTES = 256 * 1024

WEIGHT_KINDS = ("in", "out", "gu", "dn")
FULL_SHAPES = {"in": (D_MODEL, IN_PROJ_WIDTH), "out": (D_MODEL, D_MODEL), "gu": (N_DEV, FFN_BLOCK, D_MODEL), "dn": (FFN_HIDDEN, D_MODEL)}
SHARD_SHAPES = {"in": (D_MODEL, W_IN_BLOCK), "out": (W_OUT_BLOCK, D_MODEL), "gu": (FFN_BLOCK, D_MODEL), "dn": (W_DOWN_BLOCK, D_MODEL)}
ANY = pl.BlockSpec(memory_space=pl.ANY)


def _sds(shape, dtype):
    return jax.ShapeDtypeStruct(shape, dtype)


def _rows(width, tile=ROW_TILE):
    return pl.BlockSpec((tile, width), lambda i: (i, 0))


def _frows(width):
    return _rows(width, FFN_TILE)


def _cols(height, tile):
    return pl.BlockSpec((height, tile), lambda i: (0, i))


def _const(shape):
    return pl.BlockSpec(shape, lambda i: (0,) * len(shape))


def _const1(shape):
    return pl.BlockSpec(shape, lambda i: (0,) * len(shape), pipeline_mode=pl.Buffered(1))


HALO_ROWS = 16


def _halo_prev(width, tile=ROW_TILE):
    return pl.BlockSpec((HALO_ROWS, width), lambda i: (jnp.maximum(i * (tile // HALO_ROWS) - 1, 0), 0))


def _halo_next(width, tile=ROW_TILE):
    return pl.BlockSpec((HALO_ROWS, width), lambda i: (jnp.minimum((i + 1) * (tile // HALO_ROWS), SEQ // HALO_ROWS - 1), 0))


def _rms(x):
    r = lax.rsqrt(jnp.mean(x * x, axis=-1, keepdims=True) + RMS_EPS)
    return x * r, r


def _rms_bwd(dn, y, r):
    return r * (dn - y * jnp.mean(dn * y, axis=-1, keepdims=True))


def _dot(a, b):
    return jnp.dot(a, b, preferred_element_type=F32)


def _dot_nt(a, b):
    return lax.dot_general(a, b, (((1,), (1,)), ((), ())), preferred_element_type=F32)


def _dot_tn(a, b):
    return lax.dot_general(a, b, (((0,), (0,)), ((), ())), preferred_element_type=F32)


def _place():
    return lax.axis_index("x"), lax.axis_index("y"), lax.axis_index("c")


def _block_of(px, py, pc):
    return 4 * px + 2 * py + pc


def _weight_block(ref, kind, blk):
    if kind == "in":
        return ref.at[:, pl.ds(blk * W_IN_BLOCK, W_IN_BLOCK)]
    if kind == "out":
        return ref.at[pl.ds(blk * W_OUT_BLOCK, W_OUT_BLOCK), :]
    if kind == "gu":
        return ref.at[blk]
    return ref.at[pl.ds(blk * W_DOWN_BLOCK, W_DOWN_BLOCK), :]


def _dma_semaphores(n):
    return [pltpu.SemaphoreType.DMA((n, 7)), pltpu.SemaphoreType.DMA((n, 7)), pltpu.SemaphoreType.DMA((n,))]


class _Gather:
    def __init__(self, kinds, shards):
        self.kinds, self.operands = tuple(kinds), list(shards)
        self.tag = "gather_" + "_".join(kinds)
        self.out_shape = [_sds(FULL_SHAPES[k], BF16) for k in kinds]
        self.scratch = _dma_semaphores(len(kinds))

    def _parties(self):
        x, y, c = _place()
        return (x, y, c), (x, y, 1 - c), [(1 - x, y), (x, 1 - y), (1 - x, 1 - y)], c

    def _copy(self, outs, sems, w, k, block, to, src=None):
        dst = _weight_block(outs[w], self.kinds[w], _block_of(*block))
        return pltpu.make_async_remote_copy(src_ref=dst if src is None else src, dst_ref=dst, send_sem=sems[0].at[w, k],
                                            recv_sem=sems[1].at[w, k], device_id=to, device_id_type=MESH)

    def _own(self, srcs, outs, sems, w, me):
        return pltpu.make_async_copy(srcs[w], _weight_block(outs[w], self.kinds[w], _block_of(*me)), sems[2].at[w])

    def _first(self, srcs, outs, sems, w):
        me, sibling, chips, c = self._parties()
        return [self._copy(outs, sems, w, 0, me, sibling, src=srcs[w])] + [
            self._copy(outs, sems, w, 1 + j, me, (*chip, c), src=srcs[w]) for j, chip in enumerate(chips)]

    def start(self, srcs, outs, sems):
        me = self._parties()[0]
        for w in range(len(self.kinds)):
            self._own(srcs, outs, sems, w, me).start()
            for cp in self._first(srcs, outs, sems, w):
                cp.start()

    def forward(self, srcs, outs, sems):
        me, sibling, chips, c = self._parties()
        for j, chip in enumerate(chips):
            for w in range(len(self.kinds)):
                self._copy(outs, sems, w, 1 + j, (*chip, c), me).wait_recv()
                self._copy(outs, sems, w, 4 + j, (*chip, c), sibling).start()

    def finish(self, srcs, outs, sems):
        me, sibling, chips, c = self._parties()
        for w in range(len(self.kinds)):
            self._copy(outs, sems, w, 0, sibling, me).wait_recv()
            for j, chip in enumerate(chips):
                self._copy(outs, sems, w, 4 + j, (*chip, 1 - c), me).wait_recv()
        for w in range(len(self.kinds)):
            for cp in self._first(srcs, outs, sems, w):
                cp.wait_send()
            for j, chip in enumerate(chips):
                self._copy(outs, sems, w, 4 + j, (*chip, c), sibling).wait_send()
            self._own(srcs, outs, sems, w, me).wait()


def _peers(x, y, c):
    return [(x ^ a, y ^ b, c ^ e) for a in (0, 1) for b in (0, 1) for e in (0, 1) if (a, b, e) != (0, 0, 0)]


class _Scatter:
    def __init__(self, kinds, grads):
        self.kinds, self.operands = tuple(kinds), list(grads)
        self.tag = "scatter_" + "_".join(kinds)
        self.out_shape = [_sds((N_DEV,) + SHARD_SHAPES[k], BF16) for k in kinds]
        self.scratch = _dma_semaphores(len(kinds))

    def _copies(self, srcs, outs, sems):
        x, y, c = _place()
        me = _block_of(x, y, c)
        copies = []
        for w, kind in enumerate(self.kinds):
            copies.append(pltpu.make_async_copy(_weight_block(srcs[w], kind, me), outs[w].at[me], sems[2].at[w]))
            for k, peer in enumerate(_peers(x, y, c)):
                copies.append(pltpu.make_async_remote_copy(
                    src_ref=_weight_block(srcs[w], kind, _block_of(*peer)), dst_ref=outs[w].at[me],
                    send_sem=sems[0].at[w, k], recv_sem=sems[1].at[w, k], device_id=peer, device_id_type=MESH))
        return copies

    def start(self, srcs, outs, sems):
        for cp in self._copies(srcs, outs, sems):
            cp.start()

    def forward(self, srcs, outs, sems):
        pass

    def finish(self, srcs, outs, sems):
        for cp in self._copies(srcs, outs, sems):
            cp.wait()


N_CHIPS = N_DEV // 2


class _ScatterChips:
    def __init__(self, kinds, blocks):
        self.kinds, self.operands = tuple(kinds), list(blocks)
        self.tag = "scatter_chips_" + "_".join(kinds)
        self.out_shape = [_sds((N_CHIPS,) + SHARD_SHAPES[k], BF16) for k in kinds]
        self.scratch = _dma_semaphores(len(kinds))

    def _copies(self, srcs, outs, sems):
        x, y, c = _place()
        mine = 2 * x + y
        copies = []
        for w in range(len(self.kinds)):
            copies.append(pltpu.make_async_copy(srcs[w].at[mine], outs[w].at[mine], sems[2].at[w]))
            for j, (px, py) in enumerate([(1 - x, y), (x, 1 - y), (1 - x, 1 - y)]):
                copies.append(pltpu.make_async_remote_copy(
                    src_ref=srcs[w].at[2 * px + py], dst_ref=outs[w].at[mine], send_sem=sems[0].at[w, j], recv_sem=sems[1].at[w, j],
                    device_id=(px, py, c), device_id_type=MESH))
        return copies

    def start(self, srcs, outs, sems):
        for cp in self._copies(srcs, outs, sems):
            cp.start()

    def forward(self, srcs, outs, sems):
        pass

    def finish(self, srcs, outs, sems):
        for cp in self._copies(srcs, outs, sems):
            cp.wait()


def _in_hbm(a):
    return pltpu.with_memory_space_constraint(a, pltpu.HBM) if a.size * a.dtype.itemsize >= PINNED_BYTES else a


def _out_hbm(s):
    return pltpu.HBM(s.shape, s.dtype) if math.prod(s.shape) * jnp.dtype(s.dtype).itemsize >= PINNED_BYTES else s


def _call(body, args, *, grid, in_specs, out_specs, out_shape, scratch_shapes=(), vmem_mib=None, name, comm=None):
    kwargs = {} if vmem_mib is None else dict(compiler_params=pltpu.CompilerParams(vmem_limit_bytes=vmem_mib * MIB))
    in_specs, out_specs, out_shape, scratch_shapes = list(in_specs), list(out_specs), list(out_shape), list(scratch_shapes)
    args = [_in_hbm(a) for a in args]
    out_shape = [_out_hbm(s) for s in out_shape]
    if comm is None:
        res = pl.pallas_call(body, grid=grid, in_specs=in_specs, out_specs=out_specs, out_shape=out_shape,
                             scratch_shapes=scratch_shapes, name=name, **kwargs)(*args)
        return list(res), None
    n_in, n_out, n_scr = len(in_specs), len(out_specs), len(scratch_shapes)
    c_in, c_out = len(comm.operands), len(comm.out_shape)
    last = math.prod(grid) - 1

    def carried(*refs):
        cuts = [n_in, c_in, n_out, c_out, n_scr]
        parts, at = [], 0
        for n in cuts:
            parts.append(refs[at:at + n])
            at += n
        ins, c_ins, outs, c_outs, scr = parts
        sems = refs[at:]
        step = pl.program_id(0)
        for axis in range(1, len(grid)):
            step = step * grid[axis] + pl.program_id(axis)

        @pl.when(step == 0)
        def _():
            comm.start(c_ins, c_outs, sems)

        @pl.when(step == last)
        def _():
            comm.forward(c_ins, c_outs, sems)

        body(*ins, *outs, *scr)

        @pl.when(step == last)
        def _():
            comm.finish(c_ins, c_outs, sems)

    res = pl.pallas_call(carried, grid=grid, in_specs=in_specs + [ANY] * c_in, out_specs=out_specs + [ANY] * c_out,
                         out_shape=out_shape + [_out_hbm(s) for s in comm.out_shape], scratch_shapes=scratch_shapes + comm.scratch,
                         name=name + "_" + comm.tag, **kwargs)(*args, *[_in_hbm(a) for a in comm.operands])
    return list(res[:n_out]), list(res[n_out:])


def _comm_only(comm, name):
    def body(*refs):
        n_in, n_out = len(comm.operands), len(comm.out_shape)
        srcs, outs, sems = refs[:n_in], refs[n_in:n_in + n_out], refs[n_in + n_out:]
        comm.start(srcs, outs, sems)
        comm.forward(srcs, outs, sems)
        comm.finish(srcs, outs, sems)

    return pl.pallas_call(body, in_specs=[ANY] * len(comm.operands), out_specs=[ANY] * len(comm.out_shape),
                          out_shape=[_out_hbm(s) for s in comm.out_shape], scratch_shapes=comm.scratch,
                          name=name)(*[_in_hbm(a) for a in comm.operands])


class _GatherSmall:
    def __init__(self, v, tag):
        self.operands, self.tag = [v], tag
        self.out_shape = [_sds((N_DEV,) + v.shape, F32)]
        self.scratch = _dma_semaphores(1)

    def _copies(self, srcs, outs, sems):
        x, y, c = _place()
        me = _block_of(x, y, c)
        copies = [pltpu.make_async_copy(srcs[0], outs[0].at[me], sems[2].at[0])]
        for k, peer in enumerate(_peers(x, y, c)):
            copies.append(pltpu.make_async_remote_copy(src_ref=srcs[0], dst_ref=outs[0].at[me], send_sem=sems[0].at[0, k],
                                                       recv_sem=sems[1].at[0, k], device_id=peer, device_id_type=MESH))
        return copies

    def start(self, srcs, outs, sems):
        for cp in self._copies(srcs, outs, sems):
            cp.start()

    def forward(self, srcs, outs, sems):
        pass

    def finish(self, srcs, outs, sems):
        for cp in self._copies(srcs, outs, sems):
            cp.wait()


class _Together:
    def __init__(self, *comms):
        self.comms = comms
        self.operands = [a for cm in comms for a in cm.operands]
        self.out_shape = [s for cm in comms for s in cm.out_shape]
        self.scratch = [s for cm in comms for s in cm.scratch]
        self.tag = "_".join(cm.tag for cm in comms)
        self.kinds = tuple(kind for cm in comms for kind in getattr(cm, "kinds", ()))

    def _each(self, srcs, outs, sems):
        a = o = s = 0
        for cm in self.comms:
            na, no, ns = len(cm.operands), len(cm.out_shape), len(cm.scratch)
            yield cm, srcs[a:a + na], outs[o:o + no], sems[s:s + ns]
            a, o, s = a + na, o + no, s + ns

    def start(self, srcs, outs, sems):
        for cm, *refs in self._each(srcs, outs, sems):
            cm.start(*refs)

    def forward(self, srcs, outs, sems):
        for cm, *refs in self._each(srcs, outs, sems):
            cm.forward(*refs)

    def finish(self, srcs, outs, sems):
        for cm, *refs in self._each(srcs, outs, sems):
            cm.finish(*refs)


def _rope_tables(pos_col, freq_row, comm=None):
    def body(p_ref, f_ref, c_ref, sa_ref, sb_ref):
        ang = p_ref[...].astype(F32) * f_ref[...]
        lane = lax.broadcasted_iota(jnp.int32, ang.shape, 1) % HEAD_DIM
        cos, sin = jnp.cos(ang), jnp.sin(ang)
        c_ref[...] = jnp.where(lane < ROPE_DIM, cos, 1.0)
        sa_ref[...] = jnp.where(lane < ROPE_DIM // 2, -sin, 0.0)
        sb_ref[...] = jnp.where((lane >= ROPE_DIM // 2) & (lane < ROPE_DIM), sin, 0.0)

    return _call(body, (pos_col, freq_row), grid=(SEQ // ROW_TILE,), in_specs=[_rows(1), _const((1, LANES))],
                 out_specs=[_rows(LANES)] * 3, out_shape=[_sds((SEQ, LANES), F32)] * 3, name="rope_tables", comm=comm)


def _rotate(t, c, sa, sb):
    parts = []
    for g in range(ATTN_WIDTH // LANES):
        tg = t[:, g * LANES:(g + 1) * LANES]
        parts.append(tg * c + pltpu.roll(tg, LANES - 8, axis=1) * sa + pltpu.roll(tg, 8, axis=1) * sb)
    return jnp.concatenate(parts, axis=1)


def _rotate_transposed(dt, c, sa, sb):
    parts = []
    for g in range(ATTN_WIDTH // LANES):
        dg = dt[:, g * LANES:(g + 1) * LANES]
        parts.append(dg * c + pltpu.roll(dg * sa, 8, axis=1) + pltpu.roll(dg * sb, LANES - 8, axis=1))
    return jnp.concatenate(parts, axis=1)


def _inproj_fwd(x, g_pre, w_in, tc, tsa, tsb, comm=None):
    def body(x_ref, g_ref, w_ref, c_ref, sa_ref, sb_ref, q_ref, k_ref, v_ref, u_ref, b_ref, cc_ref):
        y, _ = _rms(x_ref[...])
        h = (y * g_ref[...]).astype(BF16)

        def proj(n):
            return _dot(h, w_ref[:, n * ATTN_WIDTH:(n + 1) * ATTN_WIDTH])

        c, sa, sb = c_ref[...], sa_ref[...], sb_ref[...]
        q_ref[...] = (_rotate(proj(0), c, sa, sb) * (HEAD_DIM ** -0.5)).astype(BF16)
        k_ref[...] = _rotate(proj(1), c, sa, sb).astype(BF16)
        v_ref[...] = proj(2).astype(BF16)
        u_ref[...] = proj(3).astype(BF16)
        b_ref[...] = proj(4).astype(BF16)
        cc_ref[...] = proj(5).astype(BF16)

    a = ATTN_WIDTH
    return _call(
        body, (x, g_pre, w_in, tc, tsa, tsb), grid=(SEQ // ROW_TILE,),
        in_specs=[_rows(D_MODEL), _const((1, D_MODEL)), _const1((D_MODEL, IN_PROJ_WIDTH)), _rows(LANES), _rows(LANES), _rows(LANES)],
        out_specs=[_rows(a)] * 6, out_shape=[_sds((SEQ, a), BF16)] * 6,
        vmem_mib=40, name="inproj_fwd", comm=comm)


def _head_masks():
    lane = lax.broadcasted_iota(jnp.int32, (1, LANES), 1)
    first = lane < HEAD_DIM
    return first, first.astype(F32), 1.0 - first.astype(F32)


def _perm_chunks(dil):
    length = SEQ // dil
    out = []
    for r in range(dil):
        for c0 in range(0, length, PERM_CHUNK):
            chunk = (r * length + c0) // PERM_CHUNK
            rows = pl.ds(c0, PERM_CHUNK) if dil == 1 else pl.ds(r + dil * c0, PERM_CHUNK, stride=dil)
            out.append((chunk, rows))
    return out


def _chunk(c, offset=0):
    return pl.ds(offset + c * PERM_CHUNK, PERM_CHUNK)


def _write_band_bias(bias_ref):
    rr = lax.broadcasted_iota(jnp.int32, (Q_BLOCK, K_WINDOW), 0)
    cc = lax.broadcasted_iota(jnp.int32, (Q_BLOCK, K_WINDOW), 1)
    band = (cc >= rr) & (cc - rr <= 2 * HALF_WINDOW)
    bias_ref[0] = jnp.where(band, 0.0, NEG_INF)
    bias_ref[1] = jnp.where(band & (cc >= HALF_WINDOW), 0.0, NEG_INF)
    bias_ref[2] = jnp.where(band & (cc < Q_BLOCK + HALF_WINDOW), 0.0, NEG_INF)


def _band_bias_index(m0, length):
    return jnp.where(m0 % length == 0, 1, 0) + jnp.where((m0 + Q_BLOCK) % length == 0, 2, 0)


def _zero_key_padding(bufs):
    pad = jnp.zeros((HALF_WINDOW, LANES), BF16)
    for buf in bufs:
        buf[pl.ds(0, HALF_WINDOW), :] = pad
        buf[pl.ds(SEQ + HALF_WINDOW, HALF_WINDOW), :] = pad


def _attn_fwd(q, k, v, comm=None):
    group = 8

    def body(q_ref, k_ref, v_ref, o_ref, lse_ref, q32, k32, v32, qa, qb, kp, vp, accp, mlp,
             acc0, acc1, acc2, ml0, ml1, ml2, bias_ref):
        first, mask_a, mask_b = _head_masks()
        low = lax.broadcasted_iota(jnp.int32, (1, LANES), 1) % HEAD_DIM < HEAD_DIM // 2
        _zero_key_padding((kp, vp))
        _write_band_bias(bias_ref)
        q32[...] = q_ref[...].astype(F32)
        k32[...] = k_ref[...].astype(F32)
        v32[...] = v_ref[...].astype(F32)
        natural = ((acc0, ml0), (acc1, ml1), (acc2, ml2))

        for branch, dil in enumerate(DILATIONS):
            length = SEQ // dil
            assert length >= 2 * Q_BLOCK
            chunks = _perm_chunks(dil)
            for c, rows in chunks:
                val = q32[rows, :]
                qa[_chunk(c), :] = (val * mask_a).astype(BF16)
                qb[_chunk(c), :] = (val * mask_b).astype(BF16)
                kp[_chunk(c, HALF_WINDOW), :] = k32[rows, :].astype(BF16)
                vp[_chunk(c, HALF_WINDOW), :] = v32[rows, :].astype(BF16)
            acc_dst, ml_dst = natural[branch] if dil == 1 else (accp, mlp)

            def blocks(i, carry, length=length, acc_dst=acc_dst, ml_dst=ml_dst):
                base = pl.multiple_of(i * (group * Q_BLOCK), group * Q_BLOCK)
                starts = [base + g * Q_BLOCK for g in range(group)]
                scores = [[_dot_nt(qx[pl.ds(m0, Q_BLOCK), :], kp[pl.ds(m0, K_WINDOW), :]) for qx in (qa, qb)] for m0 in starts]
                probs = []
                for m0, pair in zip(starts, scores):
                    bias = bias_ref[_band_bias_index(m0, length)]
                    stats, ps = [], []
                    for s in pair:
                        s = s + bias
                        m = jnp.max(s, axis=1, keepdims=True)
                        p = jnp.exp(s - m)
                        stats.append(jnp.where(low, m, jnp.sum(p, axis=1, keepdims=True)))
                        ps.append(p.astype(BF16))
                    ml_dst[pl.ds(m0, Q_BLOCK), :] = jnp.where(first, stats[0], stats[1])
                    probs.append(ps)
                for m0, ps in zip(starts, probs):
                    vw = vp[pl.ds(m0, K_WINDOW), :]
                    acc_dst[pl.ds(m0, Q_BLOCK), :] = jnp.where(first, _dot(ps[0], vw), _dot(ps[1], vw))
                return carry

            lax.fori_loop(0, SEQ // (group * Q_BLOCK), blocks, 0)

            if dil > 1:
                for c, rows in chunks:
                    natural[branch][0][rows, :] = accp[_chunk(c), :]
                    natural[branch][1][rows, :] = mlp[_chunk(c), :]

        for c in range(SEQ // PERM_CHUNK):
            packed = [ml[_chunk(c), :] for _, ml in natural]
            ms = [jnp.where(low, ml, pltpu.roll(ml, HEAD_DIM // 2, axis=1)) for ml in packed]
            ls = [jnp.where(low, pltpu.roll(ml, LANES - HEAD_DIM // 2, axis=1), ml) for ml in packed]
            m_all = jnp.maximum(jnp.maximum(ms[0], ms[1]), ms[2])
            es = [jnp.exp(m - m_all) for m in ms]
            z = ls[0] * es[0] + ls[1] * es[1] + ls[2] * es[2]
            num = natural[0][0][_chunk(c), :] * es[0] + natural[1][0][_chunk(c), :] * es[1] + natural[2][0][_chunk(c), :] * es[2]
            o_ref[_chunk(c), :] = num / z
            lse_ref[_chunk(c), :] = m_all + jnp.log(z)

    col = pl.BlockSpec((SEQ, LANES), lambda h: (0, h))
    padded = SEQ + 2 * HALF_WINDOW
    return _call(
        body, (q, k, v), grid=(ATTN_WIDTH // LANES,), in_specs=[col] * 3, out_specs=[col] * 2,
        out_shape=[_sds((SEQ, ATTN_WIDTH), F32)] * 2,
        scratch_shapes=[pltpu.VMEM((SEQ, LANES), F32)] * 3 + [pltpu.VMEM((SEQ, LANES), BF16)] * 2
        + [pltpu.VMEM((padded, LANES), BF16)] * 2 + [pltpu.VMEM((SEQ, LANES), F32)] * 8
        + [pltpu.VMEM((3, Q_BLOCK, K_WINDOW), F32)],
        vmem_mib=52, name="attn_fwd", comm=comm)


def _shifted(t, before, after, i):
    tile = t.shape[0]
    row = lax.broadcasted_iota(jnp.int32, (tile, 1), 0)
    before = jnp.where(i > 0, before, 0.0)
    after = jnp.where(i < SEQ // tile - 1, after, 0.0)
    return (jnp.where(row == 0, before, pltpu.roll(t, 1, axis=0)),
            jnp.where(row == tile - 1, after, pltpu.roll(t, tile - 1, axis=0)))


def _last_row(ref):
    return ref[HALO_ROWS - 1:HALO_ROWS, :].astype(F32)


def _first_row(ref):
    return ref[0:1, :].astype(F32)


def _conv_parts(u, c, u_prev, c_prev, u_next, c_next, cw, i):
    t = c * u
    t_prev, t_next = _shifted(t, _last_row(c_prev) * _last_row(u_prev), _first_row(c_next) * _first_row(u_next), i)
    s = cw[0:1, :] * t_prev + cw[1:2, :] * t + cw[2:3, :] * t_next
    return t, t_prev, t_next, s


def _mix_fwd(attn, u, b, c, conv_w, g_attn, g_conv, w_out, x, g_post):
    def body(a_ref, u_ref, b_ref, c_ref, up_ref, cp_ref, un_ref, cn_ref, cw_ref, ga_ref, gc_ref, w_ref, x_ref, gp_ref,
             x1_ref, mg_ref, mix_ref):
        i = pl.program_id(0)
        _, _, _, s = _conv_parts(u_ref[...].astype(F32), c_ref[...].astype(F32), up_ref, cp_ref, un_ref, cn_ref, cw_ref[...], i)
        ya, _ = _rms(a_ref[...])
        yc, _ = _rms(b_ref[...].astype(F32) * s)
        merged = jnp.concatenate([ya * ga_ref[...], yc * gc_ref[...]], axis=1).astype(BF16)
        mix = _dot(merged, w_ref[...])
        ym, _ = _rms(mix)
        mg_ref[...] = merged.T
        mix_ref[...] = mix.astype(BF16)
        x1_ref[...] = x_ref[...] + ym * gp_ref[...]

    a = ATTN_WIDTH
    return _call(
        body, (attn, u, b, c, u, c, u, c, conv_w, g_attn, g_conv, w_out, x, g_post), grid=(SEQ // ROW_TILE,),
        in_specs=[_rows(a)] * 4 + [_halo_prev(a)] * 2 + [_halo_next(a)] * 2
        + [_const((3, a)), _const((1, a)), _const((1, a)), _const1((D_MODEL, D_MODEL)), _rows(D_MODEL), _const((1, D_MODEL))],
        out_specs=[_rows(D_MODEL), _cols(D_MODEL, ROW_TILE), _rows(D_MODEL)],
        out_shape=[_sds((SEQ, D_MODEL), F32), _sds((D_MODEL, SEQ), BF16), _sds((SEQ, D_MODEL), BF16)],
        vmem_mib=40, name="mix_fwd")[0]


def _gu_spec():
    return pl.BlockSpec((N_DEV, FFN_TILE, FFN_BLOCK), lambda i: (0, i, 0))


def _ffn_fwd(x1, g_pre, w_gu, w_dn, g_post, comm=None, target=None):
    n_tiles = SEQ // ROW_TILE

    def body(*refs):
        if target is None:
            x_ref, g_ref, wgu_ref, wdn_ref, gp_ref, x2_ref, gu_ref, f_ref = refs
        else:
            x_ref, g_ref, wgu_ref, wdn_ref, gp_ref, t_ref, x2_ref, gu_ref, f_ref, loss_ref, acc = refs
        x1 = x_ref[...]
        y, _ = _rms(x1)
        h = (y * g_ref[...]).astype(BF16)
        f = jnp.zeros((ROW_TILE, D_MODEL), F32)

        def gate_up(j):
            return _dot_nt(h, wgu_ref[j]), _dot_nt(h, wgu_ref[j + N_DEV // 2])

        ahead = gate_up(0)
        for j in range(N_DEV // 2):
            gate, up = ahead
            if j + 1 < N_DEV // 2:
                ahead = gate_up(j + 1)
            gu_ref[j] = gate.astype(BF16)
            gu_ref[j + N_DEV // 2] = up.astype(BF16)
            act = (gate * jax.nn.sigmoid(gate) * up).astype(BF16)
            f = f + _dot(act, wdn_ref[pl.ds(j * FFN_BLOCK, FFN_BLOCK), :])
        yf, _ = _rms(f)
        f_ref[...] = f
        x2 = x1 + yf * gp_ref[...]
        if target is None:
            x2_ref[...] = x2
        else:
            i = pl.program_id(0)
            err = x2 - t_ref[...]
            x2_ref[...] = err * (1.0 / D_MODEL)

            @pl.when(i == 0)
            def _():
                acc[...] = jnp.zeros_like(acc)

            acc[...] += jnp.sum(err * err, axis=0, keepdims=True)

            @pl.when(i == n_tiles - 1)
            def _():
                loss_ref[...] = jnp.sum(acc[...], axis=1, keepdims=True) * (0.5 / D_MODEL)

    with_loss = target is not None
    return _call(
        body, (x1, g_pre, w_gu, w_dn, g_post) + ((target,) if with_loss else ()), grid=(n_tiles,),
        in_specs=[_rows(D_MODEL), _const((1, D_MODEL)), _const1((N_DEV, FFN_BLOCK, D_MODEL)), _const1((FFN_HIDDEN, D_MODEL)),
                  _const((1, D_MODEL))] + ([_rows(D_MODEL)] if with_loss else []),
        out_specs=[_rows(D_MODEL), pl.BlockSpec((N_DEV, ROW_TILE, FFN_BLOCK), lambda i: (0, i, 0)), _rows(D_MODEL)]
        + ([_const((1, 1))] if with_loss else []),
        out_shape=[_sds((SEQ, D_MODEL), F32), _sds((N_DEV, SEQ, FFN_BLOCK), BF16), _sds((SEQ, D_MODEL), F32)]
        + ([_sds((1, 1), F32)] if with_loss else []),
        scratch_shapes=[pltpu.VMEM((1, D_MODEL), F32)] if with_loss else [],
        vmem_mib=58, name="ffn_fwd_loss" if with_loss else "ffn_fwd", comm=comm)


def _accumulate(pairs, i):
    @pl.when(i == 0)
    def _():
        for ref, value in pairs:
            ref[...] = value

    @pl.when(i > 0)
    def _():
        for ref, value in pairs:
            ref[...] += value


def _colsum(v):
    return jnp.sum(v, axis=0, keepdims=True)


def _ffn_bwd(dx2, f, x1, gu, w_gu, w_dn, g_post, g_pre, comm=None):
    half = N_DEV // 2

    def body(dx2_ref, f_ref, x1_ref, gu_ref, wgu_ref, wdn_ref, gpost_ref, gpre_ref,
             dx1_ref, df_ref, act_ref, dgu_ref, h_ref, dgpost_ref, dgpre_ref):
        i = pl.program_id(0)
        dx2 = dx2_ref[...]
        yf, rf = _rms(f_ref[...])
        dg_post = _colsum(dx2 * yf)
        df = _rms_bwd(dx2 * gpost_ref[...], yf, rf).astype(BF16)
        df_ref[...] = df
        dh = jnp.zeros((FFN_TILE, D_MODEL), F32)

        def d_act(j):
            return _dot_nt(df, wdn_ref[pl.ds(j * FFN_BLOCK, FFN_BLOCK), :])

        ahead = d_act(0)
        for j in range(half):
            dact = ahead
            if j + 1 < half:
                ahead = d_act(j + 1)
            gate = gu_ref[j].astype(F32)
            up = gu_ref[j + half].astype(F32)
            sig = jax.nn.sigmoid(gate)
            silu = gate * sig
            act_ref[j] = (silu * up).astype(BF16)
            dgate = (dact * up * (sig * (1.0 + gate * (1.0 - sig)))).astype(BF16)
            dup = (dact * silu).astype(BF16)
            dgu_ref[j] = dgate
            dgu_ref[j + half] = dup
            dh = dh + _dot(dgate, wgu_ref[j]) + _dot(dup, wgu_ref[j + half])
        y1, r1 = _rms(x1_ref[...])
        h_ref[...] = (y1 * gpre_ref[...]).astype(BF16)
        dx1_ref[...] = dx2 + _rms_bwd(dh * gpre_ref[...], y1, r1)
        _accumulate([(dgpost_ref, dg_post), (dgpre_ref, _colsum(dh * y1))], i)

    act_spec = pl.BlockSpec((half, FFN_TILE, FFN_BLOCK), lambda i: (0, i, 0))
    return _call(
        body, (dx2, f, x1, gu, w_gu, w_dn, g_post, g_pre), grid=(SEQ // FFN_TILE,),
        in_specs=[_frows(D_MODEL)] * 3 + [_gu_spec(), _const1((N_DEV, FFN_BLOCK, D_MODEL)), _const1((FFN_HIDDEN, D_MODEL)),
                                          _const((1, D_MODEL)), _const((1, D_MODEL))],
        out_specs=[_frows(D_MODEL), _frows(D_MODEL), act_spec, _gu_spec(), _frows(D_MODEL), _const((1, D_MODEL)), _const((1, D_MODEL))],
        out_shape=[_sds((SEQ, D_MODEL), F32), _sds((SEQ, D_MODEL), BF16), _sds((half, SEQ, FFN_BLOCK), BF16),
                   _sds((N_DEV, SEQ, FFN_BLOCK), BF16), _sds((SEQ, D_MODEL), BF16), _sds((1, D_MODEL), F32), _sds((1, D_MODEL), F32)],
        vmem_mib=52, name="ffn_bwd", comm=comm)


def _wgrad(a_t, g, name):
    width = g.shape[1]

    def body(a_ref, g_ref, o_ref):
        o_ref[...] = _dot(a_ref[...], g_ref[...]).astype(BF16)

    return _call(body, (a_t, g), grid=(width // WGRAD_COLS,),
                 in_specs=[_const1((D_MODEL, SEQ)), pl.BlockSpec((SEQ, WGRAD_COLS), lambda j: (0, j))],
                 out_specs=[pl.BlockSpec((D_MODEL, WGRAD_COLS), lambda j: (0, j))], out_shape=[_sds((D_MODEL, width), BF16)],
                 vmem_mib=48, name=name)[0][0]


def _wgrad_paired(a, g, kind, to_owners=False):
    shard = SHARD_SHAPES[kind]
    per_chip = 2 if kind == "gu" else 1
    n_compute = per_chip * N_CHIPS
    n_steps = n_compute + 1

    def block(t):
        return jnp.minimum(t, n_compute - 1)

    if kind == "in":
        specs = [_const1((D_MODEL, SEQ)), pl.BlockSpec((SEQ, 2 * W_IN_BLOCK), lambda t: (0, block(t)))]
    else:
        specs = [pl.BlockSpec((1, SEQ, FFN_BLOCK), lambda t: (block(t), 0, 0)), _const1((SEQ, D_MODEL))]

    def body(a_ref, g_ref, o_ref, *rest):
        if to_owners:
            landing, sendbuf, recvbuf, keepbuf, send_sem, recv_sem, donebuf, owner_send, owner_recv, own_sem = rest
        else:
            sendbuf, recvbuf, keepbuf, send_sem, recv_sem = rest
        t = pl.program_id(0)
        x, y, c = _place()

        def exchange(q):
            return pltpu.make_async_remote_copy(src_ref=sendbuf.at[q], dst_ref=recvbuf.at[q], send_sem=send_sem.at[q],
                                                recv_sem=recv_sem.at[q], device_id=(x, y, 1 - c), device_id_type=MESH)

        my_chip = 2 * x + y

        def to_owner(q):
            return pltpu.make_async_remote_copy(src_ref=donebuf.at[q], dst_ref=landing.at[my_chip], send_sem=owner_send.at[q],
                                                recv_sem=owner_recv.at[my_chip], device_id=(q // 2, q % 2, c), device_id_type=MESH)

        def keep_own():
            return pltpu.make_async_copy(donebuf.at[my_chip], landing.at[my_chip], own_sem)

        @pl.when((t >= per_chip) & (t % per_chip == 0))
        def _():
            q = t // per_chip - 1
            exchange(q).wait_recv()
            done = (keepbuf[...] + recvbuf[q].astype(F32)).astype(BF16)
            o_ref[0] = done
            if to_owners:
                donebuf[q] = done

                @pl.when(q == my_chip)
                def _():
                    keep_own().start()

                @pl.when(q != my_chip)
                def _():
                    to_owner(q).start()

        @pl.when(t < n_compute)
        def _():
            q = t // per_chip
            if kind == "gu":
                r = _dot_tn(a_ref[0], g_ref[...])

                @pl.when(t % 2 == c)
                def _():
                    keepbuf[...] = r

                @pl.when(t % 2 != c)
                def _():
                    sendbuf[q] = r.astype(BF16)
                    exchange(q).start()
            else:
                if kind == "dn":
                    r = _dot_tn(a_ref[0], g_ref[...])
                    lower, upper = r[:W_DOWN_BLOCK], r[W_DOWN_BLOCK:]
                else:
                    r = _dot(a_ref[...], g_ref[...])
                    lower, upper = r[:, :W_IN_BLOCK], r[:, W_IN_BLOCK:]
                keepbuf[...] = jnp.where(c == 0, lower, upper)
                sendbuf[q] = jnp.where(c == 0, upper, lower).astype(BF16)
                exchange(q).start()

        @pl.when(t == n_steps - 1)
        def _():
            for q in range(N_CHIPS):
                exchange(q).wait_send()
            if to_owners:
                keep_own().wait()
                for s in range(N_CHIPS):
                    @pl.when(s != my_chip)
                    def _(s=s):
                        pltpu.make_async_remote_copy(src_ref=donebuf.at[s], dst_ref=landing.at[s], send_sem=owner_send.at[s],
                                                     recv_sem=owner_recv.at[s], device_id=(x, y, c), device_id_type=MESH).wait()

    slots = (N_CHIPS,) + shard
    owners = int(to_owners)
    res, _ = _call(
        body, (a, g), grid=(n_steps,), in_specs=specs,
        out_specs=[pl.BlockSpec((1,) + shard, lambda t: (jnp.clip(t // per_chip - 1, 0, N_CHIPS - 1), 0, 0))] + [ANY] * owners,
        out_shape=[_sds(slots, BF16)] * (1 + owners),
        scratch_shapes=[pltpu.VMEM(slots, BF16), pltpu.VMEM(slots, BF16), pltpu.VMEM(shard, F32),
                        pltpu.SemaphoreType.DMA((N_CHIPS,)), pltpu.SemaphoreType.DMA((N_CHIPS,))]
        + [pltpu.VMEM(slots, BF16), pltpu.SemaphoreType.DMA((N_CHIPS,)), pltpu.SemaphoreType.DMA((N_CHIPS,)), pltpu.SemaphoreType.DMA] * owners,
        vmem_mib=52, name="wgrad_" + kind + "_paired" + "_to_owners" * owners)
    return res[owners]


def _mix_bwd(dx1, mix, attn, u, b, c, conv_w, g_attn, g_conv, g_post, w_out):
    def body(dx1_ref, mix_ref, a_ref, u_ref, b_ref, c_ref, up_ref, cp_ref, un_ref, cn_ref, cw_ref, ga_ref, gc_ref, gp_ref, w_ref,
             dmix_ref, da_ref, ds_ref, db_ref, dgp_ref, dga_ref, dgc_ref):
        i = pl.program_id(0)
        dx1 = dx1_ref[...]
        ym, rm = _rms(mix_ref[...].astype(F32))
        dg_post = _colsum(dx1 * ym)
        dmix = _rms_bwd(dx1 * gp_ref[...], ym, rm).astype(BF16)
        dmix_ref[...] = dmix
        dmerged = _dot_nt(dmix, w_ref[...])
        dna, dnc = dmerged[:, :ATTN_WIDTH], dmerged[:, ATTN_WIDTH:]
        ya, ra = _rms(a_ref[...])
        da_ref[...] = _rms_bwd(dna * ga_ref[...], ya, ra)
        _, _, _, s = _conv_parts(u_ref[...].astype(F32), c_ref[...].astype(F32), up_ref, cp_ref, un_ref, cn_ref, cw_ref[...], i)
        gate_b = b_ref[...].astype(F32)
        yc, rc = _rms(gate_b * s)
        dy = _rms_bwd(dnc * gc_ref[...], yc, rc)
        db_ref[...] = (dy * s).astype(BF16)
        ds_ref[...] = (dy * gate_b).astype(BF16)
        _accumulate([(dgp_ref, dg_post), (dga_ref, _colsum(dna * ya)), (dgc_ref, _colsum(dnc * yc))], i)

    a = ATTN_WIDTH
    return _call(
        body, (dx1, mix, attn, u, b, c, u, c, u, c, conv_w, g_attn, g_conv, g_post, w_out), grid=(SEQ // ROW_TILE,),
        in_specs=[_rows(D_MODEL)] * 2 + [_rows(a)] * 4 + [_halo_prev(a)] * 2 + [_halo_next(a)] * 2
        + [_const((3, a)), _const((1, a)), _const((1, a)), _const((1, D_MODEL)), _const1((D_MODEL, D_MODEL))],
        out_specs=[_rows(D_MODEL)] + [_rows(a)] * 3 + [_const((1, D_MODEL)), _const((1, a)), _const((1, a))],
        out_shape=[_sds((SEQ, D_MODEL), BF16), _sds((SEQ, a), F32), _sds((SEQ, a), BF16), _sds((SEQ, a), BF16),
                   _sds((1, D_MODEL), F32), _sds((1, a), F32), _sds((1, a), F32)],
        vmem_mib=40, name="mix_bwd")[0]


def _attn_bwd(q, k, v, do, o, lse, comm=None):
    group = 8

    n_pairs = ATTN_WIDTH // LANES

    def body(q_hbm, k_hbm, v_hbm, do_hbm, o_hbm, lse_hbm, dq_ref, dk_ref, dv_ref,
             q_ref, k_ref, v_ref, do_ref, o_ref, lse_ref, in_sem,
             q32, k32, v32, qa, qb, doa, dob, kp, vp, lsep, dlp, dnat, dqp, dkp, dvp, bias_ref):
        pair = pl.program_id(0)
        sources = (q_hbm, k_hbm, v_hbm, do_hbm, o_hbm, lse_hbm)
        buffers = (q_ref, k_ref, v_ref, do_ref, o_ref, lse_ref)

        def fetch(i, p):
            return pltpu.make_async_copy(sources[i].at[:, pl.ds(pl.multiple_of(p * LANES, LANES), LANES)], buffers[i], in_sem.at[i])

        def prefetch(*which):
            @pl.when(pair + 1 < n_pairs)
            def _():
                for i in which:
                    fetch(i, pair + 1).start()

        @pl.when(pair == 0)
        def _():
            for i in range(len(sources)):
                fetch(i, 0).start()

        for i in range(len(sources)):
            fetch(i, pair).wait()

        first, mask_a, mask_b = _head_masks()
        _zero_key_padding((kp, vp))
        _write_band_bias(bias_ref)
        q32[...] = q_ref[...].astype(F32)
        k32[...] = k_ref[...].astype(F32)
        v32[...] = v_ref[...].astype(F32)
        prefetch(0, 1, 2)
        for c in range(SEQ // PERM_CHUNK):
            prod = do_ref[_chunk(c), :] * o_ref[_chunk(c), :]
            d_a = jnp.sum(prod * mask_a, axis=1, keepdims=True)
            d_b = jnp.sum(prod * mask_b, axis=1, keepdims=True)
            dnat[_chunk(c), :] = jnp.where(first, d_a, d_b)
        prefetch(4)

        for step, dil in enumerate(DILATIONS[1:] + DILATIONS[:1]):
            length = SEQ // dil
            assert length >= 2 * Q_BLOCK
            chunks = _perm_chunks(dil)
            for c, rows in chunks:
                val = q32[rows, :]
                qa[_chunk(c), :] = (val * mask_a).astype(BF16)
                qb[_chunk(c), :] = (val * mask_b).astype(BF16)
                val = do_ref[rows, :]
                doa[_chunk(c), :] = (val * mask_a).astype(BF16)
                dob[_chunk(c), :] = (val * mask_b).astype(BF16)
                kp[_chunk(c, HALF_WINDOW), :] = k32[rows, :].astype(BF16)
                vp[_chunk(c, HALF_WINDOW), :] = v32[rows, :].astype(BF16)
                lsep[_chunk(c), :] = lse_ref[rows, :]
                dlp[_chunk(c), :] = dnat[rows, :]
            if step == len(DILATIONS) - 1:
                prefetch(3, 5)
            zero = jnp.zeros((PERM_CHUNK, LANES), F32)
            for c in range(SEQ // PERM_CHUNK):
                dkp[_chunk(c), :] = zero
                dvp[_chunk(c), :] = zero
            dkp[pl.ds(SEQ, 2 * HALF_WINDOW), :] = zero[:2 * HALF_WINDOW]
            dvp[pl.ds(SEQ, 2 * HALF_WINDOW), :] = zero[:2 * HALF_WINDOW]

            heads = ((qa, doa, 0), (qb, dob, HEAD_DIM))

            def blocks(i, carry, length=length):
                base = pl.multiple_of(i * (group * Q_BLOCK), group * Q_BLOCK)
                starts = [base + g * Q_BLOCK for g in range(group)]
                raw = [[(_dot_nt(qx[pl.ds(m0, Q_BLOCK), :], kp[pl.ds(m0, K_WINDOW), :]),
                         _dot_nt(dox[pl.ds(m0, Q_BLOCK), :], vp[pl.ds(m0, K_WINDOW), :])) for qx, dox, _ in heads]
                       for m0 in starts]
                grads = []
                for m0, pair in zip(starts, raw):
                    bias = bias_ref[_band_bias_index(m0, length)]
                    lse_b, d_b = lsep[pl.ds(m0, Q_BLOCK), :], dlp[pl.ds(m0, Q_BLOCK), :]
                    out = []
                    for (s, dp), (_, _, col) in zip(pair, heads):
                        p = jnp.exp(s + bias - lse_b[:, col:col + 1])
                        out.append(((p * (dp - d_b[:, col:col + 1])).astype(BF16), p.astype(BF16)))
                    grads.append(out)
                for m0, out in zip(starts, grads):
                    qrows, krows = pl.ds(m0, Q_BLOCK), pl.ds(m0, K_WINDOW)
                    kw = kp[krows, :]
                    dk = jnp.zeros((K_WINDOW, LANES), F32)
                    dv = jnp.zeros((K_WINDOW, LANES), F32)
                    for (ds, p), (qx, dox, _) in zip(out, heads):
                        dk = dk + _dot_tn(ds, qx[qrows, :])
                        dv = dv + _dot_tn(p, dox[qrows, :])
                    dqp[qrows, :] = jnp.where(first, _dot(out[0][0], kw), _dot(out[1][0], kw)) * (HEAD_DIM ** -0.5)
                    dkp[krows, :] += dk
                    dvp[krows, :] += dv
                return carry

            lax.fori_loop(0, SEQ // (group * Q_BLOCK), blocks, 0)

            for c, rows in chunks:
                g_q, g_k, g_v = dqp[_chunk(c), :], dkp[_chunk(c, HALF_WINDOW), :], dvp[_chunk(c, HALF_WINDOW), :]
                if step == 0:
                    dq_ref[rows, :] = g_q
                    dk_ref[rows, :] = g_k
                    dv_ref[rows, :] = g_v
                else:
                    dq_ref[rows, :] = dq_ref[rows, :] + g_q
                    dk_ref[rows, :] = dk_ref[rows, :] + g_k
                    dv_ref[rows, :] = dv_ref[rows, :] + g_v

    col = pl.BlockSpec((SEQ, LANES), lambda h: (0, h))
    padded = SEQ + 2 * HALF_WINDOW
    return _call(
        body, (q, k, v, do, o, lse), grid=(n_pairs,), in_specs=[ANY] * 6, out_specs=[col] * 3,
        out_shape=[_sds((SEQ, ATTN_WIDTH), F32)] * 3,
        scratch_shapes=[pltpu.VMEM((SEQ, LANES), BF16)] * 3 + [pltpu.VMEM((SEQ, LANES), F32)] * 3 + [pltpu.SemaphoreType.DMA((6,))]
        + [pltpu.VMEM((SEQ, LANES), F32)] * 3 + [pltpu.VMEM((SEQ, LANES), BF16)] * 4
        + [pltpu.VMEM((padded, LANES), BF16)] * 2 + [pltpu.VMEM((SEQ, LANES), F32)] * 4 + [pltpu.VMEM((padded, LANES), F32)] * 2
        + [pltpu.VMEM((3, Q_BLOCK, K_WINDOW), F32)],
        vmem_mib=56, name="attn_bwd", comm=comm)


def _inproj_bwd(dq, dk, dv, ds, db, u, c, conv_w, tc, tsa, tsb, w_in, x, g_pre, dx1):
    def body(dq_ref, dk_ref, dv_ref, ds_ref, db_ref, u_ref, c_ref, dsp_ref, up_ref, cp_ref, dsn_ref, un_ref, cn_ref, cw_ref,
             tc_ref, tsa_ref, tsb_ref, w_ref, x_ref, g_ref, dx1_ref, dx_ref, dproj_ref, h_ref, dg_ref, dcw_ref):
        i = pl.program_id(0)
        cw = cw_ref[...]
        u, c = u_ref[...].astype(F32), c_ref[...].astype(F32)
        t, t_prev, t_next, _ = _conv_parts(u, c, up_ref, cp_ref, un_ref, cn_ref, cw, i)
        ds = ds_ref[...].astype(F32)
        ds_prev, ds_next = _shifted(ds, _last_row(dsp_ref), _first_row(dsn_ref), i)
        dt = cw[0:1, :] * ds_next + cw[1:2, :] * ds + cw[2:3, :] * ds_prev
        d_taps = jnp.concatenate([_colsum(ds * t_prev), _colsum(ds * t), _colsum(ds * t_next)], axis=0)
        tc_, tsa_, tsb_ = tc_ref[...], tsa_ref[...], tsb_ref[...]
        groups = ((2, lambda: dv_ref[...].astype(BF16)), (4, lambda: db_ref[...]),
                  (0, lambda: _rotate_transposed(dq_ref[...], tc_, tsa_, tsb_).astype(BF16)),
                  (1, lambda: _rotate_transposed(dk_ref[...], tc_, tsa_, tsb_).astype(BF16)),
                  (3, lambda: (dt * c).astype(BF16)), (5, lambda: (dt * u).astype(BF16)))
        dh = jnp.zeros((x_ref.shape[0], D_MODEL), F32)
        for n, make in groups:
            cols = pl.ds(n * a, a)
            part = make()
            dproj_ref[:, cols] = part
            dh = dh + _dot_nt(part, w_ref[:, cols])
        y, r = _rms(x_ref[...])
        h_ref[...] = (y * g_ref[...]).astype(BF16).T
        dx_ref[...] = dx1_ref[...] + _rms_bwd(dh * g_ref[...], y, r)
        _accumulate([(dg_ref, _colsum(dh * y)), (dcw_ref, d_taps)], i)

    a = ATTN_WIDTH
    tile = FFN_TILE

    def rows(width):
        return _rows(width, tile)

    return _call(
        body, (dq, dk, dv, ds, db, u, c, ds, u, c, ds, u, c, conv_w, tc, tsa, tsb, w_in, x, g_pre, dx1), grid=(SEQ // tile,),
        in_specs=[rows(a)] * 7 + [_halo_prev(a, tile)] * 3 + [_halo_next(a, tile)] * 3
        + [_const((3, a)), rows(LANES), rows(LANES), rows(LANES), _const1((D_MODEL, IN_PROJ_WIDTH)), rows(D_MODEL),
           _const((1, D_MODEL)), rows(D_MODEL)],
        out_specs=[rows(D_MODEL), rows(IN_PROJ_WIDTH), _cols(D_MODEL, tile), _const((1, D_MODEL)), _const((3, a))],
        out_shape=[_sds((SEQ, D_MODEL), F32), _sds((SEQ, IN_PROJ_WIDTH), BF16), _sds((D_MODEL, SEQ), BF16),
                   _sds((1, D_MODEL), F32), _sds((3, a), F32)],
        vmem_mib=48, name="inproj_bwd")


def _adamw_math(w, g, m, v):
    m = ADAM_B1 * m + (1.0 - ADAM_B1) * g
    v = ADAM_B2 * v + (1.0 - ADAM_B2) * (g * g)
    m_hat = m / (1.0 - ADAM_B1 ** ADAM_STEP)
    v_hat = v / (1.0 - ADAM_B2 ** ADAM_STEP)
    delta = -ADAM_LR * (m_hat / (jnp.sqrt(v_hat) + ADAM_EPS) + ADAM_WD * w)
    return delta, m, v


def _sum_parts(p_ref):
    g = p_ref[0].astype(F32)
    for k in range(1, p_ref.shape[0]):
        g = g + p_ref[k].astype(F32)
    return g


def _adamw_layers(parts, w, m, v, row_tile, name, comm=None):
    _, rows, cols = w.shape
    n_tiles = rows // row_tile

    def body(*refs):
        p_refs = refs[:DEPTH]
        w_ref, m_ref, v_ref, g_ref, d_ref, nm_ref, nv_ref = refs[DEPTH:]
        layer = pl.program_id(0)
        for l, p_ref in enumerate(p_refs):
            @pl.when(layer == l)
            def _(p_ref=p_ref):
                g = _sum_parts(p_ref)
                g_ref[0] = g
                d_ref[0], nm_ref[0], nv_ref[0] = _adamw_math(w_ref[0], g, m_ref[0], v_ref[0])

    def part_spec(l):
        return pl.BlockSpec((parts[l].shape[0], row_tile, cols),
                            lambda layer, i: (0, jnp.where(layer == l, i, jnp.where(layer < l, 0, n_tiles - 1)), 0))

    tile = pl.BlockSpec((1, row_tile, cols), lambda layer, i: (layer, i, 0))
    return _call(body, (*parts, w, m, v), grid=(DEPTH, n_tiles), in_specs=[part_spec(l) for l in range(DEPTH)] + [tile] * 3,
                 out_specs=[tile] * 4, out_shape=[_sds(w.shape, F32)] * 4, name=name, comm=comm)


def _adamw_sum8(parts, w, m, v, name):
    def body(p_ref, w_ref, m_ref, v_ref, g_ref, d_ref, nm_ref, nv_ref):
        g = _sum_parts(p_ref)
        g_ref[...] = g
        d_ref[...], nm_ref[...], nv_ref[...] = _adamw_math(w_ref[...], g, m_ref[...], v_ref[...])

    return pl.pallas_call(body, out_shape=[_sds(w.shape, F32)] * 4, name=name)(parts, w, m, v)


def _adamw_plain(g, w, m, v, name):
    def body(g_ref, w_ref, m_ref, v_ref, d_ref, nm_ref, nv_ref):
        d_ref[...], nm_ref[...], nv_ref[...] = _adamw_math(w_ref[...], g_ref[...], m_ref[...], v_ref[...])

    return pl.pallas_call(body, out_shape=[_sds(w.shape, F32)] * 3, name=name)(g, w, m, v)


SMALL_ROWS = 8
GAIN_NAMES = ("pre_mix_norm", "post_mix_norm", "pre_ffn_norm", "post_ffn_norm", "attn_out_norm", "conv_out_norm")


def _pack_small(pre_mix, post_mix, pre_ffn, post_ffn, attn_out, conv_out, taps):
    zeros = jnp.zeros((1, ATTN_WIDTH), F32)
    return jnp.concatenate([
        pre_mix, post_mix, pre_ffn, post_ffn, jnp.concatenate([attn_out, conv_out], axis=1),
        jnp.concatenate([taps[0:1], taps[1:2]], axis=1), jnp.concatenate([taps[2:3], zeros], axis=1),
        jnp.zeros((1, D_MODEL), F32)], axis=0)


def _rope(positions, comm=None):
    inv_freq = ROPE_THETA ** (-jnp.arange(0, ROPE_DIM, 2, dtype=F32) / ROPE_DIM)
    per_head = jnp.concatenate([inv_freq, inv_freq, jnp.zeros((HEAD_DIM - ROPE_DIM,), F32)])
    freq_row = jnp.tile(per_head, LANES // HEAD_DIM).reshape(1, LANES)
    return _rope_tables(positions.reshape(SEQ, 1), freq_row, comm)


def _layer_forward(h, gains, taps, tables, w, inproj_comm=None, attn_comm=None, ffn_comm=None, target=None):
    (q, k, v, u, b, c), landed = _inproj_fwd(h, gains["pre_mix_norm"], w["in"], *tables, comm=inproj_comm)
    if inproj_comm is not None:
        w = {**w, **dict(zip(inproj_comm.kinds, landed))}
    (attn, lse), landed = _attn_fwd(q, k, v, comm=attn_comm)
    if attn_comm is not None:
        w = {**w, **dict(zip(attn_comm.kinds, landed))}
    x1, merged, mix = _mix_fwd(attn, u, b, c, taps, gains["attn_out_norm"], gains["conv_out_norm"], w["out"], h, gains["post_mix_norm"])
    (x2, gu, f, *loss), landed_next = _ffn_fwd(x1, gains["pre_ffn_norm"], w["gu"], w["dn"], gains["post_ffn_norm"], comm=ffn_comm,
                                               target=target)
    out = x2 if target is None else (x2, loss[0])
    return out, (h, q, k, v, u, b, c, attn, lse, merged, mix, x1, gu, f), w, landed_next


class _GradientExchange:
    def ffn_grads(self, act, df, h2, dgu):
        return dict(dn=_wgrad_paired(act, df, "dn"), gu=_wgrad_paired(dgu, h2, "gu"))

    def w_out_grad(self, merged_t, dmix):
        return _wgrad(merged_t, dmix, "wgrad_out")

    def attention_passenger(self, ffn, g_out):
        return _Together(_ScatterChips(("gu", "dn"), (ffn["gu"], ffn["dn"])), _Scatter(("out",), (g_out,)))

    def w_in_grad(self, h_t, dproj, last):
        if last:
            return _wgrad_paired(h_t, dproj, "in", to_owners=True)
        return _wgrad(h_t, dproj, "wgrad_in")


def _layer_backward(dx, saved, gains, taps, tables, w, ffn_comm, exchange, last):
    x0, q, k, v, u, b, c, attn, lse, merged, mix, x1, gu, f = saved
    (dx1, df, act, dgu, h2, dg_post_ffn, dg_pre_ffn), landed_prev = _ffn_bwd(
        dx, f, x1, gu, w["gu"], w["dn"], gains["post_ffn_norm"], gains["pre_ffn_norm"], comm=ffn_comm)
    ffn = exchange.ffn_grads(act, df, h2, dgu)
    dmix, dattn, ds, db, dg_post_mix, dg_attn, dg_conv = _mix_bwd(
        dx1, mix, attn, u, b, c, taps, gains["attn_out_norm"], gains["conv_out_norm"], gains["post_mix_norm"], w["out"])
    g_out = exchange.w_out_grad(merged, dmix)
    attn_comm = exchange.attention_passenger(ffn, g_out)
    (dq, dk, dv), landed = _attn_bwd(q, k, v, dattn, attn, lse, comm=attn_comm)
    dx0, dproj, h1, dg_pre_mix, dtaps = _inproj_bwd(
        dq, dk, dv, ds, db, u, c, taps, *tables, w["in"], x0, gains["pre_mix_norm"], dx1)[0]
    g_in = exchange.w_in_grad(h1, dproj, last)
    small = _pack_small(dg_pre_mix, dg_post_mix, dg_pre_ffn, dg_post_ffn, dg_attn, dg_conv, dtaps)
    rest = dict(zip(attn_comm.kinds, landed)) if attn_comm is not None else dict(ffn, out=g_out)
    return dx0, g_in, small, landed_prev, rest


def kernel(x, positions, pre_mix_norm, w_in, conv_w, attn_out_norm, conv_out_norm, w_out, post_mix_norm, pre_ffn_norm, w_gate_up, w_down, post_ffn_norm, loss_target, m_pre_mix_norm, m_w_in, m_conv_w, m_attn_out_norm, m_conv_out_norm, m_w_out, m_post_mix_norm, m_pre_ffn_norm, m_w_gate_up, m_w_down, m_post_ffn_norm, v_pre_mix_norm, v_w_in, v_conv_w, v_attn_out_norm, v_conv_out_norm, v_w_out, v_post_mix_norm, v_pre_ffn_norm, v_w_gate_up, v_w_down, v_post_ffn_norm):
    mx, my, mc = _place()
    me = _block_of(mx, my, mc)
    conv_channels = conv_w.shape[-1]

    def hidden_major(a):
        return jnp.swapaxes(a, 1, 2)

    big_w = dict(zip(WEIGHT_KINDS, (w_in, w_out, hidden_major(w_gate_up), w_down)))
    big_m = dict(zip(WEIGHT_KINDS, (m_w_in, m_w_out, hidden_major(m_w_gate_up), m_w_down)))
    big_v = dict(zip(WEIGHT_KINDS, (v_w_in, v_w_out, hidden_major(v_w_gate_up), v_w_down)))
    shards = {kind: big_w[kind].astype(BF16) for kind in WEIGHT_KINDS}
    all_gains = dict(pre_mix_norm=pre_mix_norm, attn_out_norm=attn_out_norm, conv_out_norm=conv_out_norm, post_mix_norm=post_mix_norm,
                     pre_ffn_norm=pre_ffn_norm, post_ffn_norm=post_ffn_norm)

    def gather(kinds, l):
        return _Gather(kinds, [shards[kind][l] for kind in kinds])

    def gains(l):
        return {name: g[l:l + 1] for name, g in all_gains.items()}

    early = ("in", "out", "dn")
    taps_flat = jnp.pad(conv_w.reshape(-1), (0, 8 * LANES - conv_w.size)).reshape(8, LANES)
    tables, (first, taps_all) = _rope(positions, _Together(gather(early[:1], 0), _GatherSmall(taps_flat, "taps")))
    taps_all = taps_all.reshape(N_DEV, 8 * LANES)[:, :conv_w.size]
    conv_w_full = taps_all.reshape(N_DEV, DEPTH, 3, conv_channels).transpose(1, 2, 0, 3).reshape(DEPTH, 3, ATTN_WIDTH)
    weights = [dict(zip(early[:1], [first]))] + [None] * (DEPTH - 1)
    saved = [None] * DEPTH
    h = x[0]
    for l in range(DEPTH):
        inproj_comm = gather(early[1:], 0) if l == 0 else None
        ffn_comm = gather(early, l + 1) if l + 1 < DEPTH else None
        target = loss_target[0] if l + 1 == DEPTH else None
        h, saved[l], weights[l], landed = _layer_forward(h, gains(l), conv_w_full[l], tables, weights[l], inproj_comm,
                                                         gather(("gu",), l), ffn_comm, target)
        if ffn_comm is not None:
            weights[l + 1] = dict(zip(ffn_comm.kinds, landed))
    dx, local_loss = h

    parts = {kind: [None] * DEPTH for kind in WEIGHT_KINDS}
    small_grads = [None] * DEPTH
    g_in_above = None
    exchange = _GradientExchange()
    for l in reversed(range(DEPTH)):
        ffn_comm = _Scatter(("in",), (g_in_above,)) if g_in_above is not None else None
        dx, g_in_above, small_grads[l], landed, rest = _layer_backward(dx, saved[l], gains(l), conv_w_full[l], tables, weights[l], ffn_comm,
                                                                      exchange, l == 0)
        if ffn_comm is not None:
            parts["in"][l + 1] = landed[0]
        for kind, part in rest.items():
            parts[kind][l] = part

    parts["in"][0] = g_in_above
    packed = jnp.concatenate(small_grads, axis=0)
    packed = lax.dynamic_update_slice(packed, local_loss, (SMALL_ROWS - 1, 0))
    gathered = _comm_only(_GatherSmall(packed, "small"), "exchange_last")[0]
    tiles = {"in": 256, "out": 128, "gu": 176, "dn": 176}
    big = {kind: _adamw_layers(parts[kind], big_w[kind], big_m[kind], big_v[kind], tiles[kind], "adamw_" + kind)[0]
           for kind in WEIGHT_KINDS}


    def pack_state(state):
        rows = [_pack_small(*[state[name][l:l + 1] for name in GAIN_NAMES], jnp.zeros((3, ATTN_WIDTH), F32)) for l in range(DEPTH)]
        return jnp.concatenate(rows, axis=0)

    sw = pack_state(all_gains)
    sm = pack_state(dict(zip(GAIN_NAMES, (m_pre_mix_norm, m_post_mix_norm, m_pre_ffn_norm, m_post_ffn_norm, m_attn_out_norm, m_conv_out_norm))))
    sv = pack_state(dict(zip(GAIN_NAMES, (v_pre_mix_norm, v_post_mix_norm, v_pre_ffn_norm, v_post_ffn_norm, v_attn_out_norm, v_conv_out_norm))))
    sg, sd, snm, snv = _adamw_sum8(gathered, sw, sm, sv, "adamw_small")

    def unpack(p):
        p = p.reshape(DEPTH, SMALL_ROWS, D_MODEL)
        return dict(pre_mix_norm=p[:, 0], post_mix_norm=p[:, 1], pre_ffn_norm=p[:, 2], post_ffn_norm=p[:, 3],
                    attn_out_norm=p[:, 4, :ATTN_WIDTH], conv_out_norm=p[:, 4, ATTN_WIDTH:])

    small = [unpack(p) for p in (sg, sd, snm, snv)]
    loss = sg[SMALL_ROWS - 1, 0]
    sg3 = sg.reshape(DEPTH, SMALL_ROWS, D_MODEL)
    taps_grad_full = jnp.stack([sg3[:, 5, :ATTN_WIDTH], sg3[:, 5, ATTN_WIDTH:], sg3[:, 6, :ATTN_WIDTH]], axis=1)
    taps_grad = lax.dynamic_slice_in_dim(taps_grad_full, me * conv_channels, conv_channels, axis=2)

    def flat(a):
        return a.reshape(DEPTH * 3, conv_channels)

    td, tnm, tnv = _adamw_plain(flat(taps_grad), flat(conv_w), flat(m_conv_w), flat(v_conv_w), "adamw_taps")
    taps = [taps_grad] + [a.reshape(conv_w.shape) for a in (td, tnm, tnv)]

    def leaves(o):
        s = small[o]
        return (s["pre_mix_norm"], big["in"][o], taps[o], s["attn_out_norm"], s["conv_out_norm"], big["out"][o], s["post_mix_norm"],
                s["pre_ffn_norm"], hidden_major(big["gu"][o]), big["dn"][o], s["post_ffn_norm"])

    return (loss, dx[None], *leaves(0), *leaves(1), *leaves(2), *leaves(3))
```

```python
import math

import jax
import jax.numpy as jnp
from jax import lax
from jax.experimental import pallas as pl
from jax.experimental.pallas import tpu as pltpu

F32 = jnp.float32
BF16 = jnp.bfloat16
MESH = pl.DeviceIdType.MESH

SEQ = 4096
D_MODEL = 1024
DEPTH = 4
N_DEV = 8
ATTN_WIDTH = 512
IN_PROJ_WIDTH = 3072
FFN_HIDDEN = 2816
FFN_BLOCK = 2 * FFN_HIDDEN // N_DEV
W_IN_BLOCK = IN_PROJ_WIDTH // N_DEV
W_OUT_BLOCK = D_MODEL // N_DEV
W_DOWN_BLOCK = FFN_HIDDEN // N_DEV
HEAD_DIM = 64
ROPE_DIM = 16
ROPE_THETA = 500000.0
DILATIONS = (1, 4, 16)
HALF_WINDOW = 64
RMS_EPS = 1e-6
NEG_INF = -1e30
LANES = 128
Q_BLOCK = 128
K_WINDOW = Q_BLOCK + 2 * HALF_WINDOW
PERM_CHUNK = 256
ROW_TILE = 512
FFN_TILE = 256
WGRAD_COLS = 512
ADAM_LR, ADAM_B1, ADAM_B2, ADAM_EPS, ADAM_WD, ADAM_STEP = 0.001, 0.9, 0.999, 1e-08, 0.01, 10
MIB = 1024 * 1024
PINNED_BYTES = 256 * 1024

WEIGHT_KINDS = ("in", "out", "gu", "dn")
FULL_SHAPES = {"in": (D_MODEL, IN_PROJ_WIDTH), "out": (D_MODEL, D_MODEL), "gu": (N_DEV, FFN_BLOCK, D_MODEL), "dn": (FFN_HIDDEN, D_MODEL)}
SHARD_SHAPES = {"in": (D_MODEL, W_IN_BLOCK), "out": (W_OUT_BLOCK, D_MODEL), "gu": (FFN_BLOCK, D_MODEL), "dn": (W_DOWN_BLOCK, D_MODEL)}
ANY = pl.BlockSpec(memory_space=pl.ANY)


def _sds(shape, dtype):
    return jax.ShapeDtypeStruct(shape, dtype)


def _rows(width, tile=ROW_TILE):
    return pl.BlockSpec((tile, width), lambda i: (i, 0))


def _frows(width):
    return _rows(width, FFN_TILE)


def _cols(height, tile):
    return pl.BlockSpec((height, tile), lambda i: (0, i))


def _const(shape):
    return pl.BlockSpec(shape, lambda i: (0,) * len(shape))


def _const1(shape):
    return pl.BlockSpec(shape, lambda i: (0,) * len(shape), pipeline_mode=pl.Buffered(1))


HALO_ROWS = 16


def _halo_prev(width, tile=ROW_TILE):
    return pl.BlockSpec((HALO_ROWS, width), lambda i: (jnp.maximum(i * (tile // HALO_ROWS) - 1, 0), 0))


def _halo_next(width, tile=ROW_TILE):
    return pl.BlockSpec((HALO_ROWS, width), lambda i: (jnp.minimum((i + 1) * (tile // HALO_ROWS), SEQ // HALO_ROWS - 1), 0))


def _rms(x):
    r = lax.rsqrt(jnp.mean(x * x, axis=-1, keepdims=True) + RMS_EPS)
    return x * r, r


def _rms_bwd(dn, y, r):
    return r * (dn - y * jnp.mean(dn * y, axis=-1, keepdims=True))


def _dot(a, b):
    return jnp.dot(a, b, preferred_element_type=F32)


def _dot_nt(a, b):
    return lax.dot_general(a, b, (((1,), (1,)), ((), ())), preferred_element_type=F32)


def _dot_tn(a, b):
    return lax.dot_general(a, b, (((0,), (0,)), ((), ())), preferred_element_type=F32)


def _place():
    return lax.axis_index("x"), lax.axis_index("y"), lax.axis_index("c")


def _block_of(px, py, pc):
    return 4 * px + 2 * py + pc


def _weight_block(ref, kind, blk):
    if kind == "in":
        return ref.at[:, pl.ds(blk * W_IN_BLOCK, W_IN_BLOCK)]
    if kind == "out":
        return ref.at[pl.ds(blk * W_OUT_BLOCK, W_OUT_BLOCK), :]
    if kind == "gu":
        return ref.at[blk]
    return ref.at[pl.ds(blk * W_DOWN_BLOCK, W_DOWN_BLOCK), :]


def _dma_semaphores(n):
    return [pltpu.SemaphoreType.DMA((n, 7)), pltpu.SemaphoreType.DMA((n, 7)), pltpu.SemaphoreType.DMA((n,))]


class _Gather:
    def __init__(self, kinds, shards):
        self.kinds, self.operands = tuple(kinds), list(shards)
        self.tag = "gather_" + "_".join(kinds)
        self.out_shape = [_sds(FULL_SHAPES[k], BF16) for k in kinds]
        self.scratch = _dma_semaphores(len(kinds))

    def _parties(self):
        x, y, c = _place()
        return (x, y, c), (x, y, 1 - c), [(1 - x, y), (x, 1 - y), (1 - x, 1 - y)], c

    def _copy(self, outs, sems, w, k, block, to, src=None):
        dst = _weight_block(outs[w], self.kinds[w], _block_of(*block))
        return pltpu.make_async_remote_copy(src_ref=dst if src is None else src, dst_ref=dst, send_sem=sems[0].at[w, k],
                                            recv_sem=sems[1].at[w, k], device_id=to, device_id_type=MESH)

    def _own(self, srcs, outs, sems, w, me):
        return pltpu.make_async_copy(srcs[w], _weight_block(outs[w], self.kinds[w], _block_of(*me)), sems[2].at[w])

    def _first(self, srcs, outs, sems, w):
        me, sibling, chips, c = self._parties()
        return [self._copy(outs, sems, w, 0, me, sibling, src=srcs[w])] + [
            self._copy(outs, sems, w, 1 + j, me, (*chip, c), src=srcs[w]) for j, chip in enumerate(chips)]

    def start(self, srcs, outs, sems):
        me = self._parties()[0]
        for w in range(len(self.kinds)):
            self._own(srcs, outs, sems, w, me).start()
            for cp in self._first(srcs, outs, sems, w):
                cp.start()

    def forward(self, srcs, outs, sems):
        me, sibling, chips, c = self._parties()
        for j, chip in enumerate(chips):
            for w in range(len(self.kinds)):
                self._copy(outs, sems, w, 1 + j, (*chip, c), me).wait_recv()
                self._copy(outs, sems, w, 4 + j, (*chip, c), sibling).start()

    def finish(self, srcs, outs, sems):
        me, sibling, chips, c = self._parties()
        for w in range(len(self.kinds)):
            self._copy(outs, sems, w, 0, sibling, me).wait_recv()
            for j, chip in enumerate(chips):
                self._copy(outs, sems, w, 4 + j, (*chip, 1 - c), me).wait_recv()
        for w in range(len(self.kinds)):
            for cp in self._first(srcs, outs, sems, w):
                cp.wait_send()
            for j, chip in enumerate(chips):
                self._copy(outs, sems, w, 4 + j, (*chip, c), sibling).wait_send()
            self._own(srcs, outs, sems, w, me).wait()


def _peers(x, y, c):
    return [(x ^ a, y ^ b, c ^ e) for a in (0, 1) for b in (0, 1) for e in (0, 1) if (a, b, e) != (0, 0, 0)]


class _Scatter:
    def __init__(self, kinds, grads):
        self.kinds, self.operands = tuple(kinds), list(grads)
        self.tag = "scatter_" + "_".join(kinds)
        self.out_shape = [_sds((N_DEV,) + SHARD_SHAPES[k], BF16) for k in kinds]
        self.scratch = _dma_semaphores(len(kinds))

    def _copies(self, srcs, outs, sems):
        x, y, c = _place()
        me = _block_of(x, y, c)
        copies = []
        for w, kind in enumerate(self.kinds):
            copies.append(pltpu.make_async_copy(_weight_block(srcs[w], kind, me), outs[w].at[me], sems[2].at[w]))
            for k, peer in enumerate(_peers(x, y, c)):
                copies.append(pltpu.make_async_remote_copy(
                    src_ref=_weight_block(srcs[w], kind, _block_of(*peer)), dst_ref=outs[w].at[me],
                    send_sem=sems[0].at[w, k], recv_sem=sems[1].at[w, k], device_id=peer, device_id_type=MESH))
        return copies

    def start(self, srcs, outs, sems):
        for cp in self._copies(srcs, outs, sems):
            cp.start()

    def forward(self, srcs, outs, sems):
        pass

    def finish(self, srcs, outs, sems):
        for cp in self._copies(srcs, outs, sems):
            cp.wait()


N_CHIPS = N_DEV // 2


class _ScatterChips:
    def __init__(self, kinds, blocks):
        self.kinds, self.operands = tuple(kinds), list(blocks)
        self.tag = "scatter_chips_" + "_".join(kinds)
        self.out_shape = [_sds((N_CHIPS,) + SHARD_SHAPES[k], BF16) for k in kinds]
        self.scratch = _dma_semaphores(len(kinds))

    def _copies(self, srcs, outs, sems):
        x, y, c = _place()
        mine = 2 * x + y
        copies = []
        for w in range(len(self.kinds)):
            copies.append(pltpu.make_async_copy(srcs[w].at[mine], outs[w].at[mine], sems[2].at[w]))
            for j, (px, py) in enumerate([(1 - x, y), (x, 1 - y), (1 - x, 1 - y)]):
                copies.append(pltpu.make_async_remote_copy(
                    src_ref=srcs[w].at[2 * px + py], dst_ref=outs[w].at[mine], send_sem=sems[0].at[w, j], recv_sem=sems[1].at[w, j],
                    device_id=(px, py, c), device_id_type=MESH))
        return copies

    def start(self, srcs, outs, sems):
        for cp in self._copies(srcs, outs, sems):
            cp.start()

    def forward(self, srcs, outs, sems):
        pass

    def finish(self, srcs, outs, sems):
        for cp in self._copies(srcs, outs, sems):
            cp.wait()


def _in_hbm(a):
    return pltpu.with_memory_space_constraint(a, pltpu.HBM) if a.size * a.dtype.itemsize >= PINNED_BYTES else a


def _out_hbm(s):
    return pltpu.HBM(s.shape, s.dtype) if math.prod(s.shape) * jnp.dtype(s.dtype).itemsize >= PINNED_BYTES else s


def _call(body, args, *, grid, in_specs, out_specs, out_shape, scratch_shapes=(), vmem_mib=None, name, comm=None):
    kwargs = {} if vmem_mib is None else dict(compiler_params=pltpu.CompilerParams(vmem_limit_bytes=vmem_mib * MIB))
    in_specs, out_specs, out_shape, scratch_shapes = list(in_specs), list(out_specs), list(out_shape), list(scratch_shapes)
    args = [_in_hbm(a) for a in args]
    out_shape = [_out_hbm(s) for s in out_shape]
    if comm is None:
        res = pl.pallas_call(body, grid=grid, in_specs=in_specs, out_specs=out_specs, out_shape=out_shape,
                             scratch_shapes=scratch_shapes, name=name, **kwargs)(*args)
        return list(res), None
    n_in, n_out, n_scr = len(in_specs), len(out_specs), len(scratch_shapes)
    c_in, c_out = len(comm.operands), len(comm.out_shape)
    last = math.prod(grid) - 1

    def carried(*refs):
        cuts = [n_in, c_in, n_out, c_out, n_scr]
        parts, at = [], 0
        for n in cuts:
            parts.append(refs[at:at + n])
            at += n
        ins, c_ins, outs, c_outs, scr = parts
        sems = refs[at:]
        step = pl.program_id(0)
        for axis in range(1, len(grid)):
            step = step * grid[axis] + pl.program_id(axis)

        @pl.when(step == 0)
        def _():
            comm.start(c_ins, c_outs, sems)

        @pl.when(step == last)
        def _():
            comm.forward(c_ins, c_outs, sems)

        body(*ins, *outs, *scr)

        @pl.when(step == last)
        def _():
            comm.finish(c_ins, c_outs, sems)

    res = pl.pallas_call(carried, grid=grid, in_specs=in_specs + [ANY] * c_in, out_specs=out_specs + [ANY] * c_out,
                         out_shape=out_shape + [_out_hbm(s) for s in comm.out_shape], scratch_shapes=scratch_shapes + comm.scratch,
                         name=name + "_" + comm.tag, **kwargs)(*args, *[_in_hbm(a) for a in comm.operands])
    return list(res[:n_out]), list(res[n_out:])


def _comm_only(comm, name):
    def body(*refs):
        n_in, n_out = len(comm.operands), len(comm.out_shape)
        srcs, outs, sems = refs[:n_in], refs[n_in:n_in + n_out], refs[n_in + n_out:]
        comm.start(srcs, outs, sems)
        comm.forward(srcs, outs, sems)
        comm.finish(srcs, outs, sems)

    return pl.pallas_call(body, in_specs=[ANY] * len(comm.operands), out_specs=[ANY] * len(comm.out_shape),
                          out_shape=[_out_hbm(s) for s in comm.out_shape], scratch_shapes=comm.scratch,
                          name=name)(*[_in_hbm(a) for a in comm.operands])


class _GatherSmall:
    def __init__(self, v, tag):
        self.operands, self.tag = [v], tag
        self.out_shape = [_sds((N_DEV,) + v.shape, F32)]
        self.scratch = _dma_semaphores(1)

    def _copies(self, srcs, outs, sems):
        x, y, c = _place()
        me = _block_of(x, y, c)
        copies = [pltpu.make_async_copy(srcs[0], outs[0].at[me], sems[2].at[0])]
        for k, peer in enumerate(_peers(x, y, c)):
            copies.append(pltpu.make_async_remote_copy(src_ref=srcs[0], dst_ref=outs[0].at[me], send_sem=sems[0].at[0, k],
                                                       recv_sem=sems[1].at[0, k], device_id=peer, device_id_type=MESH))
        return copies

    def start(self, srcs, outs, sems):
        for cp in self._copies(srcs, outs, sems):
            cp.start()

    def forward(self, srcs, outs, sems):
        pass

    def finish(self, srcs, outs, sems):
        for cp in self._copies(srcs, outs, sems):
            cp.wait()


class _Together:
    def __init__(self, *comms):
        self.comms = comms
        self.operands = [a for cm in comms for a in cm.operands]
        self.out_shape = [s for cm in comms for s in cm.out_shape]
        self.scratch = [s for cm in comms for s in cm.scratch]
        self.tag = "_".join(cm.tag for cm in comms)
        self.kinds = tuple(kind for cm in comms for kind in getattr(cm, "kinds", ()))

    def _each(self, srcs, outs, sems):
        a = o = s = 0
        for cm in self.comms:
            na, no, ns = len(cm.operands), len(cm.out_shape), len(cm.scratch)
            yield cm, srcs[a:a + na], outs[o:o + no], sems[s:s + ns]
            a, o, s = a + na, o + no, s + ns

    def start(self, srcs, outs, sems):
        for cm, *refs in self._each(srcs, outs, sems):
            cm.start(*refs)

    def forward(self, srcs, outs, sems):
        for cm, *refs in self._each(srcs, outs, sems):
            cm.forward(*refs)

    def finish(self, srcs, outs, sems):
        for cm, *refs in self._each(srcs, outs, sems):
            cm.finish(*refs)


def _rope_tables(pos_col, freq_row, comm=None):
    def body(p_ref, f_ref, c_ref, sa_ref, sb_ref):
        ang = p_ref[...].astype(F32) * f_ref[...]
        lane = lax.broadcasted_iota(jnp.int32, ang.shape, 1) % HEAD_DIM
        cos, sin = jnp.cos(ang), jnp.sin(ang)
        c_ref[...] = jnp.where(lane < ROPE_DIM, cos, 1.0)
        sa_ref[...] = jnp.where(lane < ROPE_DIM // 2, -sin, 0.0)
        sb_ref[...] = jnp.where((lane >= ROPE_DIM // 2) & (lane < ROPE_DIM), sin, 0.0)

    return _call(body, (pos_col, freq_row), grid=(SEQ // ROW_TILE,), in_specs=[_rows(1), _const((1, LANES))],
                 out_specs=[_rows(LANES)] * 3, out_shape=[_sds((SEQ, LANES), F32)] * 3, name="rope_tables", comm=comm)


def _rotate(t, c, sa, sb):
    parts = []
    for g in range(ATTN_WIDTH // LANES):
        tg = t[:, g * LANES:(g + 1) * LANES]
        parts.append(tg * c + pltpu.roll(tg, LANES - 8, axis=1) * sa + pltpu.roll(tg, 8, axis=1) * sb)
    return jnp.concatenate(parts, axis=1)


def _rotate_transposed(dt, c, sa, sb):
    parts = []
    for g in range(ATTN_WIDTH // LANES):
        dg = dt[:, g * LANES:(g + 1) * LANES]
        parts.append(dg * c + pltpu.roll(dg * sa, 8, axis=1) + pltpu.roll(dg * sb, LANES - 8, axis=1))
    return jnp.concatenate(parts, axis=1)


def _inproj_fwd(x, g_pre, w_in, tc, tsa, tsb, comm=None):
    def body(x_ref, g_ref, w_ref, c_ref, sa_ref, sb_ref, q_ref, k_ref, v_ref, u_ref, b_ref, cc_ref):
        y, _ = _rms(x_ref[...])
        h = (y * g_ref[...]).astype(BF16)

        def proj(n):
            return _dot(h, w_ref[:, n * ATTN_WIDTH:(n + 1) * ATTN_WIDTH])

        c, sa, sb = c_ref[...], sa_ref[...], sb_ref[...]
        q_ref[...] = (_rotate(proj(0), c, sa, sb) * (HEAD_DIM ** -0.5)).astype(BF16)
        k_ref[...] = _rotate(proj(1), c, sa, sb).astype(BF16)
        v_ref[...] = proj(2).astype(BF16)
        u_ref[...] = proj(3).astype(BF16)
        b_ref[...] = proj(4).astype(BF16)
        cc_ref[...] = proj(5).astype(BF16)

    a = ATTN_WIDTH
    return _call(
        body, (x, g_pre, w_in, tc, tsa, tsb), grid=(SEQ // ROW_TILE,),
        in_specs=[_rows(D_MODEL), _const((1, D_MODEL)), _const1((D_MODEL, IN_PROJ_WIDTH)), _rows(LANES), _rows(LANES), _rows(LANES)],
        out_specs=[_rows(a)] * 6, out_shape=[_sds((SEQ, a), BF16)] * 6,
        vmem_mib=40, name="inproj_fwd", comm=comm)


def _head_masks():
    lane = lax.broadcasted_iota(jnp.int32, (1, LANES), 1)
    first = lane < HEAD_DIM
    return first, first.astype(F32), 1.0 - first.astype(F32)


def _perm_chunks(dil):
    length = SEQ // dil
    out = []
    for r in range(dil):
        for c0 in range(0, length, PERM_CHUNK):
            chunk = (r * length + c0) // PERM_CHUNK
            rows = pl.ds(c0, PERM_CHUNK) if dil == 1 else pl.ds(r + dil * c0, PERM_CHUNK, stride=dil)
            out.append((chunk, rows))
    return out


def _chunk(c, offset=0):
    return pl.ds(offset + c * PERM_CHUNK, PERM_CHUNK)


def _write_band_bias(bias_ref):
    rr = lax.broadcasted_iota(jnp.int32, (Q_BLOCK, K_WINDOW), 0)
    cc = lax.broadcasted_iota(jnp.int32, (Q_BLOCK, K_WINDOW), 1)
    band = (cc >= rr) & (cc - rr <= 2 * HALF_WINDOW)
    bias_ref[0] = jnp.where(band, 0.0, NEG_INF)
    bias_ref[1] = jnp.where(band & (cc >= HALF_WINDOW), 0.0, NEG_INF)
    bias_ref[2] = jnp.where(band & (cc < Q_BLOCK + HALF_WINDOW), 0.0, NEG_INF)


def _band_bias_index(m0, length):
    return jnp.where(m0 % length == 0, 1, 0) + jnp.where((m0 + Q_BLOCK) % length == 0, 2, 0)


def _zero_key_padding(bufs):
    pad = jnp.zeros((HALF_WINDOW, LANES), BF16)
    for buf in bufs:
        buf[pl.ds(0, HALF_WINDOW), :] = pad
        buf[pl.ds(SEQ + HALF_WINDOW, HALF_WINDOW), :] = pad


def _attn_fwd(q, k, v, comm=None):
    group = 8

    def body(q_ref, k_ref, v_ref, o_ref, lse_ref, q32, k32, v32, qa, qb, kp, vp, accp, mlp,
             acc0, acc1, acc2, ml0, ml1, ml2, bias_ref):
        first, mask_a, mask_b = _head_masks()
        low = lax.broadcasted_iota(jnp.int32, (1, LANES), 1) % HEAD_DIM < HEAD_DIM // 2
        _zero_key_padding((kp, vp))
        _write_band_bias(bias_ref)
        q32[...] = q_ref[...].astype(F32)
        k32[...] = k_ref[...].astype(F32)
        v32[...] = v_ref[...].astype(F32)
        natural = ((acc0, ml0), (acc1, ml1), (acc2, ml2))

        for branch, dil in enumerate(DILATIONS):
            length = SEQ // dil
            assert length >= 2 * Q_BLOCK
            chunks = _perm_chunks(dil)
            for c, rows in chunks:
                val = q32[rows, :]
                qa[_chunk(c), :] = (val * mask_a).astype(BF16)
                qb[_chunk(c), :] = (val * mask_b).astype(BF16)
                kp[_chunk(c, HALF_WINDOW), :] = k32[rows, :].astype(BF16)
                vp[_chunk(c, HALF_WINDOW), :] = v32[rows, :].astype(BF16)
            acc_dst, ml_dst = natural[branch] if dil == 1 else (accp, mlp)

            def blocks(i, carry, length=length, acc_dst=acc_dst, ml_dst=ml_dst):
                base = pl.multiple_of(i * (group * Q_BLOCK), group * Q_BLOCK)
                starts = [base + g * Q_BLOCK for g in range(group)]
                scores = [[_dot_nt(qx[pl.ds(m0, Q_BLOCK), :], kp[pl.ds(m0, K_WINDOW), :]) for qx in (qa, qb)] for m0 in starts]
                probs = []
                for m0, pair in zip(starts, scores):
                    bias = bias_ref[_band_bias_index(m0, length)]
                    stats, ps = [], []
                    for s in pair:
                        s = s + bias
                        m = jnp.max(s, axis=1, keepdims=True)
                        p = jnp.exp(s - m)
                        stats.append(jnp.where(low, m, jnp.sum(p, axis=1, keepdims=True)))
                        ps.append(p.astype(BF16))
                    ml_dst[pl.ds(m0, Q_BLOCK), :] = jnp.where(first, stats[0], stats[1])
                    probs.append(ps)
                for m0, ps in zip(starts, probs):
                    vw = vp[pl.ds(m0, K_WINDOW), :]
                    acc_dst[pl.ds(m0, Q_BLOCK), :] = jnp.where(first, _dot(ps[0], vw), _dot(ps[1], vw))
                return carry

            lax.fori_loop(0, SEQ // (group * Q_BLOCK), blocks, 0)

            if dil > 1:
                for c, rows in chunks:
                    natural[branch][0][rows, :] = accp[_chunk(c), :]
                    natural[branch][1][rows, :] = mlp[_chunk(c), :]

        for c in range(SEQ // PERM_CHUNK):
            packed = [ml[_chunk(c), :] for _, ml in natural]
            ms = [jnp.where(low, ml, pltpu.roll(ml, HEAD_DIM // 2, axis=1)) for ml in packed]
            ls = [jnp.where(low, pltpu.roll(ml, LANES - HEAD_DIM // 2, axis=1), ml) for ml in packed]
            m_all = jnp.maximum(jnp.maximum(ms[0], ms[1]), ms[2])
            es = [jnp.exp(m - m_all) for m in ms]
            z = ls[0] * es[0] + ls[1] * es[1] + ls[2] * es[2]
            num = natural[0][0][_chunk(c), :] * es[0] + natural[1][0][_chunk(c), :] * es[1] + natural[2][0][_chunk(c), :] * es[2]
            o_ref[_chunk(c), :] = num / z
            lse_ref[_chunk(c), :] = m_all + jnp.log(z)

    col = pl.BlockSpec((SEQ, LANES), lambda h: (0, h))
    padded = SEQ + 2 * HALF_WINDOW
    return _call(
        body, (q, k, v), grid=(ATTN_WIDTH // LANES,), in_specs=[col] * 3, out_specs=[col] * 2,
        out_shape=[_sds((SEQ, ATTN_WIDTH), F32)] * 2,
        scratch_shapes=[pltpu.VMEM((SEQ, LANES), F32)] * 3 + [pltpu.VMEM((SEQ, LANES), BF16)] * 2
        + [pltpu.VMEM((padded, LANES), BF16)] * 2 + [pltpu.VMEM((SEQ, LANES), F32)] * 8
        + [pltpu.VMEM((3, Q_BLOCK, K_WINDOW), F32)],
        vmem_mib=52, name="attn_fwd", comm=comm)


def _shifted(t, before, after, i):
    tile = t.shape[0]
    row = lax.broadcasted_iota(jnp.int32, (tile, 1), 0)
    before = jnp.where(i > 0, before, 0.0)
    after = jnp.where(i < SEQ // tile - 1, after, 0.0)
    return (jnp.where(row == 0, before, pltpu.roll(t, 1, axis=0)),
            jnp.where(row == tile - 1, after, pltpu.roll(t, tile - 1, axis=0)))


def _last_row(ref):
    return ref[HALO_ROWS - 1:HALO_ROWS, :].astype(F32)


def _first_row(ref):
    return ref[0:1, :].astype(F32)


def _conv_parts(u, c, u_prev, c_prev, u_next, c_next, cw, i):
    t = c * u
    t_prev, t_next = _shifted(t, _last_row(c_prev) * _last_row(u_prev), _first_row(c_next) * _first_row(u_next), i)
    s = cw[0:1, :] * t_prev + cw[1:2, :] * t + cw[2:3, :] * t_next
    return t, t_prev, t_next, s


def _mix_fwd(attn, u, b, c, conv_w, g_attn, g_conv, w_out, x, g_post):
    def body(a_ref, u_ref, b_ref, c_ref, up_ref, cp_ref, un_ref, cn_ref, cw_ref, ga_ref, gc_ref, w_ref, x_ref, gp_ref,
             x1_ref, mg_ref, mix_ref):
        i = pl.program_id(0)
        _, _, _, s = _conv_parts(u_ref[...].astype(F32), c_ref[...].astype(F32), up_ref, cp_ref, un_ref, cn_ref, cw_ref[...], i)
        ya, _ = _rms(a_ref[...])
        yc, _ = _rms(b_ref[...].astype(F32) * s)
        merged = jnp.concatenate([ya * ga_ref[...], yc * gc_ref[...]], axis=1).astype(BF16)
        mix = _dot(merged, w_ref[...])
        ym, _ = _rms(mix)
        mg_ref[...] = merged.T
        mix_ref[...] = mix.astype(BF16)
        x1_ref[...] = x_ref[...] + ym * gp_ref[...]

    a = ATTN_WIDTH
    return _call(
        body, (attn, u, b, c, u, c, u, c, conv_w, g_attn, g_conv, w_out, x, g_post), grid=(SEQ // ROW_TILE,),
        in_specs=[_rows(a)] * 4 + [_halo_prev(a)] * 2 + [_halo_next(a)] * 2
        + [_const((3, a)), _const((1, a)), _const((1, a)), _const1((D_MODEL, D_MODEL)), _rows(D_MODEL), _const((1, D_MODEL))],
        out_specs=[_rows(D_MODEL), _cols(D_MODEL, ROW_TILE), _rows(D_MODEL)],
        out_shape=[_sds((SEQ, D_MODEL), F32), _sds((D_MODEL, SEQ), BF16), _sds((SEQ, D_MODEL), BF16)],
        vmem_mib=40, name="mix_fwd")[0]


def _gu_spec():
    return pl.BlockSpec((N_DEV, FFN_TILE, FFN_BLOCK), lambda i: (0, i, 0))


def _ffn_fwd(x1, g_pre, w_gu, w_dn, g_post, comm=None, target=None):
    n_tiles = SEQ // ROW_TILE

    def body(*refs):
        if target is None:
            x_ref, g_ref, wgu_ref, wdn_ref, gp_ref, x2_ref, gu_ref, f_ref = refs
        else:
            x_ref, g_ref, wgu_ref, wdn_ref, gp_ref, t_ref, x2_ref, gu_ref, f_ref, loss_ref, acc = refs
        x1 = x_ref[...]
        y, _ = _rms(x1)
        h = (y * g_ref[...]).astype(BF16)
        f = jnp.zeros((ROW_TILE, D_MODEL), F32)

        def gate_up(j):
            return _dot_nt(h, wgu_ref[j]), _dot_nt(h, wgu_ref[j + N_DEV // 2])

        ahead = gate_up(0)
        for j in range(N_DEV // 2):
            gate, up = ahead
            if j + 1 < N_DEV // 2:
                ahead = gate_up(j + 1)
            gu_ref[j] = gate.astype(BF16)
            gu_ref[j + N_DEV // 2] = up.astype(BF16)
            act = (gate * jax.nn.sigmoid(gate) * up).astype(BF16)
            f = f + _dot(act, wdn_ref[pl.ds(j * FFN_BLOCK, FFN_BLOCK), :])
        yf, _ = _rms(f)
        f_ref[...] = f
        x2 = x1 + yf * gp_ref[...]
        if target is None:
            x2_ref[...] = x2
        else:
            i = pl.program_id(0)
            err = x2 - t_ref[...]
            x2_ref[...] = err * (1.0 / D_MODEL)

            @pl.when(i == 0)
            def _():
                acc[...] = jnp.zeros_like(acc)

            acc[...] += jnp.sum(err * err, axis=0, keepdims=True)

            @pl.when(i == n_tiles - 1)
            def _():
                loss_ref[...] = jnp.sum(acc[...], axis=1, keepdims=True) * (0.5 / D_MODEL)

    with_loss = target is not None
    return _call(
        body, (x1, g_pre, w_gu, w_dn, g_post) + ((target,) if with_loss else ()), grid=(n_tiles,),
        in_specs=[_rows(D_MODEL), _const((1, D_MODEL)), _const1((N_DEV, FFN_BLOCK, D_MODEL)), _const1((FFN_HIDDEN, D_MODEL)),
                  _const((1, D_MODEL))] + ([_rows(D_MODEL)] if with_loss else []),
        out_specs=[_rows(D_MODEL), pl.BlockSpec((N_DEV, ROW_TILE, FFN_BLOCK), lambda i: (0, i, 0)), _rows(D_MODEL)]
        + ([_const((1, 1))] if with_loss else []),
        out_shape=[_sds((SEQ, D_MODEL), F32), _sds((N_DEV, SEQ, FFN_BLOCK), BF16), _sds((SEQ, D_MODEL), F32)]
        + ([_sds((1, 1), F32)] if with_loss else []),
        scratch_shapes=[pltpu.VMEM((1, D_MODEL), F32)] if with_loss else [],
        vmem_mib=58, name="ffn_fwd_loss" if with_loss else "ffn_fwd", comm=comm)


def _accumulate(pairs, i):
    @pl.when(i == 0)
    def _():
        for ref, value in pairs:
            ref[...] = value

    @pl.when(i > 0)
    def _():
        for ref, value in pairs:
            ref[...] += value


def _colsum(v):
    return jnp.sum(v, axis=0, keepdims=True)


def _ffn_bwd(dx2, f, x1, gu, w_gu, w_dn, g_post, g_pre, comm=None):
    half = N_DEV // 2

    def body(dx2_ref, f_ref, x1_ref, gu_ref, wgu_ref, wdn_ref, gpost_ref, gpre_ref,
             dx1_ref, df_ref, act_ref, dgu_ref, h_ref, dgpost_ref, dgpre_ref):
        i = pl.program_id(0)
        dx2 = dx2_ref[...]
        yf, rf = _rms(f_ref[...])
        dg_post = _colsum(dx2 * yf)
        df = _rms_bwd(dx2 * gpost_ref[...], yf, rf).astype(BF16)
        df_ref[...] = df
        dh = jnp.zeros((FFN_TILE, D_MODEL), F32)

        def d_act(j):
            return _dot_nt(df, wdn_ref[pl.ds(j * FFN_BLOCK, FFN_BLOCK), :])

        ahead = d_act(0)
        for j in range(half):
            dact = ahead
            if j + 1 < half:
                ahead = d_act(j + 1)
            gate = gu_ref[j].astype(F32)
            up = gu_ref[j + half].astype(F32)
            sig = jax.nn.sigmoid(gate)
            silu = gate * sig
            act_ref[j] = (silu * up).astype(BF16)
            dgate = (dact * up * (sig * (1.0 + gate * (1.0 - sig)))).astype(BF16)
            dup = (dact * silu).astype(BF16)
            dgu_ref[j] = dgate
            dgu_ref[j + half] = dup
            dh = dh + _dot(dgate, wgu_ref[j]) + _dot(dup, wgu_ref[j + half])
        y1, r1 = _rms(x1_ref[...])
        h_ref[...] = (y1 * gpre_ref[...]).astype(BF16)
        dx1_ref[...] = dx2 + _rms_bwd(dh * gpre_ref[...], y1, r1)
        _accumulate([(dgpost_ref, dg_post), (dgpre_ref, _colsum(dh * y1))], i)

    act_spec = pl.BlockSpec((half, FFN_TILE, FFN_BLOCK), lambda i: (0, i, 0))
    return _call(
        body, (dx2, f, x1, gu, w_gu, w_dn, g_post, g_pre), grid=(SEQ // FFN_TILE,),
        in_specs=[_frows(D_MODEL)] * 3 + [_gu_spec(), _const1((N_DEV, FFN_BLOCK, D_MODEL)), _const1((FFN_HIDDEN, D_MODEL)),
                                          _const((1, D_MODEL)), _const((1, D_MODEL))],
        out_specs=[_frows(D_MODEL), _frows(D_MODEL), act_spec, _gu_spec(), _frows(D_MODEL), _const((1, D_MODEL)), _const((1, D_MODEL))],
        out_shape=[_sds((SEQ, D_MODEL), F32), _sds((SEQ, D_MODEL), BF16), _sds((half, SEQ, FFN_BLOCK), BF16),
                   _sds((N_DEV, SEQ, FFN_BLOCK), BF16), _sds((SEQ, D_MODEL), BF16), _sds((1, D_MODEL), F32), _sds((1, D_MODEL), F32)],
        vmem_mib=52, name="ffn_bwd", comm=comm)


def _wgrad(a_t, g, name):
    width = g.shape[1]

    def body(a_ref, g_ref, o_ref):
        o_ref[...] = _dot(a_ref[...], g_ref[...]).astype(BF16)

    return _call(body, (a_t, g), grid=(width // WGRAD_COLS,),
                 in_specs=[_const1((D_MODEL, SEQ)), pl.BlockSpec((SEQ, WGRAD_COLS), lambda j: (0, j))],
                 out_specs=[pl.BlockSpec((D_MODEL, WGRAD_COLS), lambda j: (0, j))], out_shape=[_sds((D_MODEL, width), BF16)],
                 vmem_mib=48, name=name)[0][0]


def _wgrad_paired(a, g, kind, to_owners=False):
    shard = SHARD_SHAPES[kind]
    per_chip = 2 if kind == "gu" else 1
    n_compute = per_chip * N_CHIPS
    n_steps = n_compute + 1

    def chip_of(turn):
        if not to_owners:
            return turn
        return (2 * lax.axis_index("x") + lax.axis_index("y") + 1 + turn) % N_CHIPS

    def block(t):
        return jnp.minimum(t, n_compute - 1)

    if kind == "in":
        specs = [_const1((D_MODEL, SEQ)), pl.BlockSpec((SEQ, 2 * W_IN_BLOCK), lambda t: (0, chip_of(block(t))))]
    else:
        specs = [pl.BlockSpec((1, SEQ, FFN_BLOCK), lambda t: (block(t), 0, 0)), _const1((SEQ, D_MODEL))]

    def body(a_ref, g_ref, o_ref, *rest):
        if to_owners:
            landing, sendbuf, recvbuf, keepbuf, send_sem, recv_sem, donebuf, owner_send, owner_recv, own_sem = rest
        else:
            sendbuf, recvbuf, keepbuf, send_sem, recv_sem = rest
        t = pl.program_id(0)
        x, y, c = _place()

        def exchange(q):
            return pltpu.make_async_remote_copy(src_ref=sendbuf.at[q], dst_ref=recvbuf.at[q], send_sem=send_sem.at[q],
                                                recv_sem=recv_sem.at[q], device_id=(x, y, 1 - c), device_id_type=MESH)

        my_chip = 2 * x + y

        def to_owner(q):
            return pltpu.make_async_remote_copy(src_ref=donebuf.at[q], dst_ref=landing.at[my_chip], send_sem=owner_send.at[q],
                                                recv_sem=owner_recv.at[my_chip], device_id=(q // 2, q % 2, c), device_id_type=MESH)

        def keep_own():
            return pltpu.make_async_copy(donebuf.at[my_chip], landing.at[my_chip], own_sem)

        @pl.when((t >= per_chip) & (t % per_chip == 0))
        def _():
            q = chip_of(t // per_chip - 1)
            exchange(q).wait_recv()
            done = (keepbuf[...] + recvbuf[q].astype(F32)).astype(BF16)
            o_ref[0] = done
            if to_owners:
                donebuf[q] = done

                @pl.when(q == my_chip)
                def _():
                    keep_own().start()

                @pl.when(q != my_chip)
                def _():
                    to_owner(q).start()

        @pl.when(t < n_compute)
        def _():
            q = chip_of(t // per_chip)
            if kind == "gu":
                r = _dot_tn(a_ref[0], g_ref[...])

                @pl.when(t % 2 == c)
                def _():
                    keepbuf[...] = r

                @pl.when(t % 2 != c)
                def _():
                    sendbuf[q] = r.astype(BF16)
                    exchange(q).start()
            else:
                if kind == "dn":
                    r = _dot_tn(a_ref[0], g_ref[...])
                    lower, upper = r[:W_DOWN_BLOCK], r[W_DOWN_BLOCK:]
                else:
                    r = _dot(a_ref[...], g_ref[...])
                    lower, upper = r[:, :W_IN_BLOCK], r[:, W_IN_BLOCK:]
                keepbuf[...] = jnp.where(c == 0, lower, upper)
                sendbuf[q] = jnp.where(c == 0, upper, lower).astype(BF16)
                exchange(q).start()

        @pl.when(t == n_steps - 1)
        def _():
            for q in range(N_CHIPS):
                exchange(q).wait_send()
            if to_owners:
                keep_own().wait()
                for s in range(N_CHIPS):
                    @pl.when(s != my_chip)
                    def _(s=s):
                        pltpu.make_async_remote_copy(src_ref=donebuf.at[s], dst_ref=landing.at[s], send_sem=owner_send.at[s],
                                                     recv_sem=owner_recv.at[s], device_id=(x, y, c), device_id_type=MESH).wait()

    slots = (N_CHIPS,) + shard
    owners = int(to_owners)
    res, _ = _call(
        body, (a, g), grid=(n_steps,), in_specs=specs,
        out_specs=[pl.BlockSpec((1,) + shard, lambda t: (chip_of(jnp.clip(t // per_chip - 1, 0, N_CHIPS - 1)), 0, 0))] + [ANY] * owners,
        out_shape=[_sds(slots, BF16)] * (1 + owners),
        scratch_shapes=[pltpu.VMEM(slots, BF16), pltpu.VMEM(slots, BF16), pltpu.VMEM(shard, F32),
                        pltpu.SemaphoreType.DMA((N_CHIPS,)), pltpu.SemaphoreType.DMA((N_CHIPS,))]
        + [pltpu.VMEM(slots, BF16), pltpu.SemaphoreType.DMA((N_CHIPS,)), pltpu.SemaphoreType.DMA((N_CHIPS,)), pltpu.SemaphoreType.DMA] * owners,
        vmem_mib=52, name="wgrad_" + kind + "_paired" + "_to_owners" * owners)
    return res[owners]


def _mix_bwd(dx1, mix, attn, u, b, c, conv_w, g_attn, g_conv, g_post, w_out):
    def body(dx1_ref, mix_ref, a_ref, u_ref, b_ref, c_ref, up_ref, cp_ref, un_ref, cn_ref, cw_ref, ga_ref, gc_ref, gp_ref, w_ref,
             dmix_ref, da_ref, ds_ref, db_ref, dgp_ref, dga_ref, dgc_ref):
        i = pl.program_id(0)
        dx1 = dx1_ref[...]
        ym, rm = _rms(mix_ref[...].astype(F32))
        dg_post = _colsum(dx1 * ym)
        dmix = _rms_bwd(dx1 * gp_ref[...], ym, rm).astype(BF16)
        dmix_ref[...] = dmix
        dmerged = _dot_nt(dmix, w_ref[...])
        dna, dnc = dmerged[:, :ATTN_WIDTH], dmerged[:, ATTN_WIDTH:]
        ya, ra = _rms(a_ref[...])
        da_ref[...] = _rms_bwd(dna * ga_ref[...], ya, ra)
        _, _, _, s = _conv_parts(u_ref[...].astype(F32), c_ref[...].astype(F32), up_ref, cp_ref, un_ref, cn_ref, cw_ref[...], i)
        gate_b = b_ref[...].astype(F32)
        yc, rc = _rms(gate_b * s)
        dy = _rms_bwd(dnc * gc_ref[...], yc, rc)
        db_ref[...] = (dy * s).astype(BF16)
        ds_ref[...] = (dy * gate_b).astype(BF16)
        _accumulate([(dgp_ref, dg_post), (dga_ref, _colsum(dna * ya)), (dgc_ref, _colsum(dnc * yc))], i)

    a = ATTN_WIDTH
    return _call(
        body, (dx1, mix, attn, u, b, c, u, c, u, c, conv_w, g_attn, g_conv, g_post, w_out), grid=(SEQ // ROW_TILE,),
        in_specs=[_rows(D_MODEL)] * 2 + [_rows(a)] * 4 + [_halo_prev(a)] * 2 + [_halo_next(a)] * 2
        + [_const((3, a)), _const((1, a)), _const((1, a)), _const((1, D_MODEL)), _const1((D_MODEL, D_MODEL))],
        out_specs=[_rows(D_MODEL)] + [_rows(a)] * 3 + [_const((1, D_MODEL)), _const((1, a)), _const((1, a))],
        out_shape=[_sds((SEQ, D_MODEL), BF16), _sds((SEQ, a), F32), _sds((SEQ, a), BF16), _sds((SEQ, a), BF16),
                   _sds((1, D_MODEL), F32), _sds((1, a), F32), _sds((1, a), F32)],
        vmem_mib=40, name="mix_bwd")[0]


def _attn_bwd(q, k, v, do, o, lse, comm=None):
    group = 8

    n_pairs = ATTN_WIDTH // LANES

    def body(q_hbm, k_hbm, v_hbm, do_hbm, o_hbm, lse_hbm, dq_ref, dk_ref, dv_ref,
             q_ref, k_ref, v_ref, do_ref, o_ref, lse_ref, in_sem,
             q32, k32, v32, qa, qb, doa, dob, kp, vp, lsep, dlp, dnat, dqp, dkp, dvp, bias_ref):
        pair = pl.program_id(0)
        sources = (q_hbm, k_hbm, v_hbm, do_hbm, o_hbm, lse_hbm)
        buffers = (q_ref, k_ref, v_ref, do_ref, o_ref, lse_ref)

        def fetch(i, p):
            return pltpu.make_async_copy(sources[i].at[:, pl.ds(pl.multiple_of(p * LANES, LANES), LANES)], buffers[i], in_sem.at[i])

        def prefetch(*which):
            @pl.when(pair + 1 < n_pairs)
            def _():
                for i in which:
                    fetch(i, pair + 1).start()

        @pl.when(pair == 0)
        def _():
            for i in range(len(sources)):
                fetch(i, 0).start()

        for i in range(len(sources)):
            fetch(i, pair).wait()

        first, mask_a, mask_b = _head_masks()
        _zero_key_padding((kp, vp))
        _write_band_bias(bias_ref)
        q32[...] = q_ref[...].astype(F32)
        k32[...] = k_ref[...].astype(F32)
        v32[...] = v_ref[...].astype(F32)
        prefetch(0, 1, 2)
        for c in range(SEQ // PERM_CHUNK):
            prod = do_ref[_chunk(c), :] * o_ref[_chunk(c), :]
            d_a = jnp.sum(prod * mask_a, axis=1, keepdims=True)
            d_b = jnp.sum(prod * mask_b, axis=1, keepdims=True)
            dnat[_chunk(c), :] = jnp.where(first, d_a, d_b)
        prefetch(4)

        for step, dil in enumerate(DILATIONS[1:] + DILATIONS[:1]):
            length = SEQ // dil
            assert length >= 2 * Q_BLOCK
            chunks = _perm_chunks(dil)
            for c, rows in chunks:
                val = q32[rows, :]
                qa[_chunk(c), :] = (val * mask_a).astype(BF16)
                qb[_chunk(c), :] = (val * mask_b).astype(BF16)
                val = do_ref[rows, :]
                doa[_chunk(c), :] = (val * mask_a).astype(BF16)
                dob[_chunk(c), :] = (val * mask_b).astype(BF16)
                kp[_chunk(c, HALF_WINDOW), :] = k32[rows, :].astype(BF16)
                vp[_chunk(c, HALF_WINDOW), :] = v32[rows, :].astype(BF16)
                lsep[_chunk(c), :] = lse_ref[rows, :]
                dlp[_chunk(c), :] = dnat[rows, :]
            if step == len(DILATIONS) - 1:
                prefetch(3, 5)
            zero = jnp.zeros((PERM_CHUNK, LANES), F32)
            for c in range(SEQ // PERM_CHUNK):
                dkp[_chunk(c), :] = zero
                dvp[_chunk(c), :] = zero
            dkp[pl.ds(SEQ, 2 * HALF_WINDOW), :] = zero[:2 * HALF_WINDOW]
            dvp[pl.ds(SEQ, 2 * HALF_WINDOW), :] = zero[:2 * HALF_WINDOW]

            heads = ((qa, doa, 0), (qb, dob, HEAD_DIM))

            def blocks(i, carry, length=length):
                base = pl.multiple_of(i * (group * Q_BLOCK), group * Q_BLOCK)
                starts = [base + g * Q_BLOCK for g in range(group)]
                raw = [[(_dot_nt(qx[pl.ds(m0, Q_BLOCK), :], kp[pl.ds(m0, K_WINDOW), :]),
                         _dot_nt(dox[pl.ds(m0, Q_BLOCK), :], vp[pl.ds(m0, K_WINDOW), :])) for qx, dox, _ in heads]
                       for m0 in starts]
                grads = []
                for m0, pair in zip(starts, raw):
                    bias = bias_ref[_band_bias_index(m0, length)]
                    lse_b, d_b = lsep[pl.ds(m0, Q_BLOCK), :], dlp[pl.ds(m0, Q_BLOCK), :]
                    out = []
                    for (s, dp), (_, _, col) in zip(pair, heads):
                        p = jnp.exp(s + bias - lse_b[:, col:col + 1])
                        out.append(((p * (dp - d_b[:, col:col + 1])).astype(BF16), p.astype(BF16)))
                    grads.append(out)
                for m0, out in zip(starts, grads):
                    qrows, krows = pl.ds(m0, Q_BLOCK), pl.ds(m0, K_WINDOW)
                    kw = kp[krows, :]
                    dk = jnp.zeros((K_WINDOW, LANES), F32)
                    dv = jnp.zeros((K_WINDOW, LANES), F32)
                    for (ds, p), (qx, dox, _) in zip(out, heads):
                        dk = dk + _dot_tn(ds, qx[qrows, :])
                        dv = dv + _dot_tn(p, dox[qrows, :])
                    dqp[qrows, :] = jnp.where(first, _dot(out[0][0], kw), _dot(out[1][0], kw)) * (HEAD_DIM ** -0.5)
                    dkp[krows, :] += dk
                    dvp[krows, :] += dv
                return carry

            lax.fori_loop(0, SEQ // (group * Q_BLOCK), blocks, 0)

            for c, rows in chunks:
                g_q, g_k, g_v = dqp[_chunk(c), :], dkp[_chunk(c, HALF_WINDOW), :], dvp[_chunk(c, HALF_WINDOW), :]
                if step == 0:
                    dq_ref[rows, :] = g_q
                    dk_ref[rows, :] = g_k
                    dv_ref[rows, :] = g_v
                else:
                    dq_ref[rows, :] = dq_ref[rows, :] + g_q
                    dk_ref[rows, :] = dk_ref[rows, :] + g_k
                    dv_ref[rows, :] = dv_ref[rows, :] + g_v

    col = pl.BlockSpec((SEQ, LANES), lambda h: (0, h))
    padded = SEQ + 2 * HALF_WINDOW
    return _call(
        body, (q, k, v, do, o, lse), grid=(n_pairs,), in_specs=[ANY] * 6, out_specs=[col] * 3,
        out_shape=[_sds((SEQ, ATTN_WIDTH), F32)] * 3,
        scratch_shapes=[pltpu.VMEM((SEQ, LANES), BF16)] * 3 + [pltpu.VMEM((SEQ, LANES), F32)] * 3 + [pltpu.SemaphoreType.DMA((6,))]
        + [pltpu.VMEM((SEQ, LANES), F32)] * 3 + [pltpu.VMEM((SEQ, LANES), BF16)] * 4
        + [pltpu.VMEM((padded, LANES), BF16)] * 2 + [pltpu.VMEM((SEQ, LANES), F32)] * 4 + [pltpu.VMEM((padded, LANES), F32)] * 2
        + [pltpu.VMEM((3, Q_BLOCK, K_WINDOW), F32)],
        vmem_mib=56, name="attn_bwd", comm=comm)


def _inproj_bwd(dq, dk, dv, ds, db, u, c, conv_w, tc, tsa, tsb, w_in, x, g_pre, dx1):
    def body(dq_ref, dk_ref, dv_ref, ds_ref, db_ref, u_ref, c_ref, dsp_ref, up_ref, cp_ref, dsn_ref, un_ref, cn_ref, cw_ref,
             tc_ref, tsa_ref, tsb_ref, w_ref, x_ref, g_ref, dx1_ref, dx_ref, dproj_ref, h_ref, dg_ref, dcw_ref):
        i = pl.program_id(0)
        cw = cw_ref[...]
        u, c = u_ref[...].astype(F32), c_ref[...].astype(F32)
        t, t_prev, t_next, _ = _conv_parts(u, c, up_ref, cp_ref, un_ref, cn_ref, cw, i)
        ds = ds_ref[...].astype(F32)
        ds_prev, ds_next = _shifted(ds, _last_row(dsp_ref), _first_row(dsn_ref), i)
        dt = cw[0:1, :] * ds_next + cw[1:2, :] * ds + cw[2:3, :] * ds_prev
        d_taps = jnp.concatenate([_colsum(ds * t_prev), _colsum(ds * t), _colsum(ds * t_next)], axis=0)
        tc_, tsa_, tsb_ = tc_ref[...], tsa_ref[...], tsb_ref[...]
        groups = ((2, lambda: dv_ref[...].astype(BF16)), (4, lambda: db_ref[...]),
                  (0, lambda: _rotate_transposed(dq_ref[...], tc_, tsa_, tsb_).astype(BF16)),
                  (1, lambda: _rotate_transposed(dk_ref[...], tc_, tsa_, tsb_).astype(BF16)),
                  (3, lambda: (dt * c).astype(BF16)), (5, lambda: (dt * u).astype(BF16)))
        dh = jnp.zeros((x_ref.shape[0], D_MODEL), F32)
        for n, make in groups:
            cols = pl.ds(n * a, a)
            part = make()
            dproj_ref[:, cols] = part
            dh = dh + _dot_nt(part, w_ref[:, cols])
        y, r = _rms(x_ref[...])
        h_ref[...] = (y * g_ref[...]).astype(BF16).T
        dx_ref[...] = dx1_ref[...] + _rms_bwd(dh * g_ref[...], y, r)
        _accumulate([(dg_ref, _colsum(dh * y)), (dcw_ref, d_taps)], i)

    a = ATTN_WIDTH
    tile = FFN_TILE

    def rows(width):
        return _rows(width, tile)

    return _call(
        body, (dq, dk, dv, ds, db, u, c, ds, u, c, ds, u, c, conv_w, tc, tsa, tsb, w_in, x, g_pre, dx1), grid=(SEQ // tile,),
        in_specs=[rows(a)] * 7 + [_halo_prev(a, tile)] * 3 + [_halo_next(a, tile)] * 3
        + [_const((3, a)), rows(LANES), rows(LANES), rows(LANES), _const1((D_MODEL, IN_PROJ_WIDTH)), rows(D_MODEL),
           _const((1, D_MODEL)), rows(D_MODEL)],
        out_specs=[rows(D_MODEL), rows(IN_PROJ_WIDTH), _cols(D_MODEL, tile), _const((1, D_MODEL)), _const((3, a))],
        out_shape=[_sds((SEQ, D_MODEL), F32), _sds((SEQ, IN_PROJ_WIDTH), BF16), _sds((D_MODEL, SEQ), BF16),
                   _sds((1, D_MODEL), F32), _sds((3, a), F32)],
        vmem_mib=48, name="inproj_bwd")


def _adamw_math(w, g, m, v):
    m = ADAM_B1 * m + (1.0 - ADAM_B1) * g
    v = ADAM_B2 * v + (1.0 - ADAM_B2) * (g * g)
    m_hat = m / (1.0 - ADAM_B1 ** ADAM_STEP)
    v_hat = v / (1.0 - ADAM_B2 ** ADAM_STEP)
    delta = -ADAM_LR * (m_hat / (jnp.sqrt(v_hat) + ADAM_EPS) + ADAM_WD * w)
    return delta, m, v


def _sum_parts(p_ref):
    g = p_ref[0].astype(F32)
    for k in range(1, p_ref.shape[0]):
        g = g + p_ref[k].astype(F32)
    return g


def _adamw_layers(parts, w, m, v, row_tile, name, comm=None):
    _, rows, cols = w.shape
    n_tiles = rows // row_tile

    def body(*refs):
        p_refs = refs[:DEPTH]
        w_ref, m_ref, v_ref, g_ref, d_ref, nm_ref, nv_ref = refs[DEPTH:]
        layer = pl.program_id(0)
        for l, p_ref in enumerate(p_refs):
            @pl.when(layer == l)
            def _(p_ref=p_ref):
                g = _sum_parts(p_ref)
                g_ref[0] = g
                d_ref[0], nm_ref[0], nv_ref[0] = _adamw_math(w_ref[0], g, m_ref[0], v_ref[0])

    def part_spec(l):
        return pl.BlockSpec((parts[l].shape[0], row_tile, cols),
                            lambda layer, i: (0, jnp.where(layer == l, i, jnp.where(layer < l, 0, n_tiles - 1)), 0))

    tile = pl.BlockSpec((1, row_tile, cols), lambda layer, i: (layer, i, 0))
    return _call(body, (*parts, w, m, v), grid=(DEPTH, n_tiles), in_specs=[part_spec(l) for l in range(DEPTH)] + [tile] * 3,
                 out_specs=[tile] * 4, out_shape=[_sds(w.shape, F32)] * 4, name=name, comm=comm)


def _adamw_sum8(parts, w, m, v, name):
    def body(p_ref, w_ref, m_ref, v_ref, g_ref, d_ref, nm_ref, nv_ref):
        g = _sum_parts(p_ref)
        g_ref[...] = g
        d_ref[...], nm_ref[...], nv_ref[...] = _adamw_math(w_ref[...], g, m_ref[...], v_ref[...])

    return pl.pallas_call(body, out_shape=[_sds(w.shape, F32)] * 4, name=name)(parts, w, m, v)


def _adamw_plain(g, w, m, v, name):
    def body(g_ref, w_ref, m_ref, v_ref, d_ref, nm_ref, nv_ref):
        d_ref[...], nm_ref[...], nv_ref[...] = _adamw_math(w_ref[...], g_ref[...], m_ref[...], v_ref[...])

    return pl.pallas_call(body, out_shape=[_sds(w.shape, F32)] * 3, name=name)(g, w, m, v)


SMALL_ROWS = 8
GAIN_NAMES = ("pre_mix_norm", "post_mix_norm", "pre_ffn_norm", "post_ffn_norm", "attn_out_norm", "conv_out_norm")


def _pack_small(pre_mix, post_mix, pre_ffn, post_ffn, attn_out, conv_out, taps):
    zeros = jnp.zeros((1, ATTN_WIDTH), F32)
    return jnp.concatenate([
        pre_mix, post_mix, pre_ffn, post_ffn, jnp.concatenate([attn_out, conv_out], axis=1),
        jnp.concatenate([taps[0:1], taps[1:2]], axis=1), jnp.concatenate([taps[2:3], zeros], axis=1),
        jnp.zeros((1, D_MODEL), F32)], axis=0)


def _rope(positions, comm=None):
    inv_freq = ROPE_THETA ** (-jnp.arange(0, ROPE_DIM, 2, dtype=F32) / ROPE_DIM)
    per_head = jnp.concatenate([inv_freq, inv_freq, jnp.zeros((HEAD_DIM - ROPE_DIM,), F32)])
    freq_row = jnp.tile(per_head, LANES // HEAD_DIM).reshape(1, LANES)
    return _rope_tables(positions.reshape(SEQ, 1), freq_row, comm)


def _layer_forward(h, gains, taps, tables, w, inproj_comm=None, attn_comm=None, ffn_comm=None, target=None):
    (q, k, v, u, b, c), landed = _inproj_fwd(h, gains["pre_mix_norm"], w["in"], *tables, comm=inproj_comm)
    if inproj_comm is not None:
        w = {**w, **dict(zip(inproj_comm.kinds, landed))}
    (attn, lse), landed = _attn_fwd(q, k, v, comm=attn_comm)
    if attn_comm is not None:
        w = {**w, **dict(zip(attn_comm.kinds, landed))}
    x1, merged, mix = _mix_fwd(attn, u, b, c, taps, gains["attn_out_norm"], gains["conv_out_norm"], w["out"], h, gains["post_mix_norm"])
    (x2, gu, f, *loss), landed_next = _ffn_fwd(x1, gains["pre_ffn_norm"], w["gu"], w["dn"], gains["post_ffn_norm"], comm=ffn_comm,
                                               target=target)
    out = x2 if target is None else (x2, loss[0])
    return out, (h, q, k, v, u, b, c, attn, lse, merged, mix, x1, gu, f), w, landed_next


class _GradientExchange:
    def ffn_grads(self, act, df, h2, dgu):
        return dict(dn=_wgrad_paired(act, df, "dn"), gu=_wgrad_paired(dgu, h2, "gu"))

    def w_out_grad(self, merged_t, dmix):
        return _wgrad(merged_t, dmix, "wgrad_out")

    def attention_passenger(self, ffn, g_out):
        return _Together(_ScatterChips(("gu", "dn"), (ffn["gu"], ffn["dn"])), _Scatter(("out",), (g_out,)))

    def w_in_grad(self, h_t, dproj, last):
        if last:
            return _wgrad_paired(h_t, dproj, "in", to_owners=True)
        return _wgrad(h_t, dproj, "wgrad_in")


def _layer_backward(dx, saved, gains, taps, tables, w, ffn_comm, exchange, last):
    x0, q, k, v, u, b, c, attn, lse, merged, mix, x1, gu, f = saved
    (dx1, df, act, dgu, h2, dg_post_ffn, dg_pre_ffn), landed_prev = _ffn_bwd(
        dx, f, x1, gu, w["gu"], w["dn"], gains["post_ffn_norm"], gains["pre_ffn_norm"], comm=ffn_comm)
    ffn = exchange.ffn_grads(act, df, h2, dgu)
    dmix, dattn, ds, db, dg_post_mix, dg_attn, dg_conv = _mix_bwd(
        dx1, mix, attn, u, b, c, taps, gains["attn_out_norm"], gains["conv_out_norm"], gains["post_mix_norm"], w["out"])
    g_out = exchange.w_out_grad(merged, dmix)
    attn_comm = exchange.attention_passenger(ffn, g_out)
    (dq, dk, dv), landed = _attn_bwd(q, k, v, dattn, attn, lse, comm=attn_comm)
    dx0, dproj, h1, dg_pre_mix, dtaps = _inproj_bwd(
        dq, dk, dv, ds, db, u, c, taps, *tables, w["in"], x0, gains["pre_mix_norm"], dx1)[0]
    g_in = exchange.w_in_grad(h1, dproj, last)
    small = _pack_small(dg_pre_mix, dg_post_mix, dg_pre_ffn, dg_post_ffn, dg_attn, dg_conv, dtaps)
    rest = dict(zip(attn_comm.kinds, landed)) if attn_comm is not None else dict(ffn, out=g_out)
    return dx0, g_in, small, landed_prev, rest


def kernel(x, positions, pre_mix_norm, w_in, conv_w, attn_out_norm, conv_out_norm, w_out, post_mix_norm, pre_ffn_norm, w_gate_up, w_down, post_ffn_norm, loss_target, m_pre_mix_norm, m_w_in, m_conv_w, m_attn_out_norm, m_conv_out_norm, m_w_out, m_post_mix_norm, m_pre_ffn_norm, m_w_gate_up, m_w_down, m_post_ffn_norm, v_pre_mix_norm, v_w_in, v_conv_w, v_attn_out_norm, v_conv_out_norm, v_w_out, v_post_mix_norm, v_pre_ffn_norm, v_w_gate_up, v_w_down, v_post_ffn_norm):
    mx, my, mc = _place()
    me = _block_of(mx, my, mc)
    conv_channels = conv_w.shape[-1]

    def hidden_major(a):
        return jnp.swapaxes(a, 1, 2)

    big_w = dict(zip(WEIGHT_KINDS, (w_in, w_out, hidden_major(w_gate_up), w_down)))
    big_m = dict(zip(WEIGHT_KINDS, (m_w_in, m_w_out, hidden_major(m_w_gate_up), m_w_down)))
    big_v = dict(zip(WEIGHT_KINDS, (v_w_in, v_w_out, hidden_major(v_w_gate_up), v_w_down)))
    shards = {kind: big_w[kind].astype(BF16) for kind in WEIGHT_KINDS}
    all_gains = dict(pre_mix_norm=pre_mix_norm, attn_out_norm=attn_out_norm, conv_out_norm=conv_out_norm, post_mix_norm=post_mix_norm,
                     pre_ffn_norm=pre_ffn_norm, post_ffn_norm=post_ffn_norm)

    def gather(kinds, l):
        return _Gather(kinds, [shards[kind][l] for kind in kinds])

    def gains(l):
        return {name: g[l:l + 1] for name, g in all_gains.items()}

    early = ("in", "out", "dn")
    taps_flat = jnp.pad(conv_w.reshape(-1), (0, 8 * LANES - conv_w.size)).reshape(8, LANES)
    tables, (first, taps_all) = _rope(positions, _Together(gather(early[:1], 0), _GatherSmall(taps_flat, "taps")))
    taps_all = taps_all.reshape(N_DEV, 8 * LANES)[:, :conv_w.size]
    conv_w_full = taps_all.reshape(N_DEV, DEPTH, 3, conv_channels).transpose(1, 2, 0, 3).reshape(DEPTH, 3, ATTN_WIDTH)
    weights = [dict(zip(early[:1], [first]))] + [None] * (DEPTH - 1)
    saved = [None] * DEPTH
    h = x[0]
    for l in range(DEPTH):
        inproj_comm = gather(early[1:], 0) if l == 0 else None
        ffn_comm = gather(early, l + 1) if l + 1 < DEPTH else None
        target = loss_target[0] if l + 1 == DEPTH else None
        h, saved[l], weights[l], landed = _layer_forward(h, gains(l), conv_w_full[l], tables, weights[l], inproj_comm,
                                                         gather(("gu",), l), ffn_comm, target)
        if ffn_comm is not None:
            weights[l + 1] = dict(zip(ffn_comm.kinds, landed))
    dx, local_loss = h

    parts = {kind: [None] * DEPTH for kind in WEIGHT_KINDS}
    small_grads = [None] * DEPTH
    g_in_above = None
    exchange = _GradientExchange()
    for l in reversed(range(DEPTH)):
        ffn_comm = _Scatter(("in",), (g_in_above,)) if g_in_above is not None else None
        dx, g_in_above, small_grads[l], landed, rest = _layer_backward(dx, saved[l], gains(l), conv_w_full[l], tables, weights[l], ffn_comm,
                                                                      exchange, l == 0)
        if ffn_comm is not None:
            parts["in"][l + 1] = landed[0]
        for kind, part in rest.items():
            parts[kind][l] = part

    parts["in"][0] = g_in_above
    packed = jnp.concatenate(small_grads, axis=0)
    packed = lax.dynamic_update_slice(packed, local_loss, (SMALL_ROWS - 1, 0))
    gathered = _comm_only(_GatherSmall(packed, "small"), "exchange_last")[0]
    tiles = {"in": 256, "out": 128, "gu": 176, "dn": 176}
    big = {kind: _adamw_layers(parts[kind], big_w[kind], big_m[kind], big_v[kind], tiles[kind], "adamw_" + kind)[0]
           for kind in WEIGHT_KINDS}


    def pack_state(state):
        rows = [_pack_small(*[state[name][l:l + 1] for name in GAIN_NAMES], jnp.zeros((3, ATTN_WIDTH), F32)) for l in range(DEPTH)]
        return jnp.concatenate(rows, axis=0)

    sw = pack_state(all_gains)
    sm = pack_state(dict(zip(GAIN_NAMES, (m_pre_mix_norm, m_post_mix_norm, m_pre_ffn_norm, m_post_ffn_norm, m_attn_out_norm, m_conv_out_norm))))
    sv = pack_state(dict(zip(GAIN_NAMES, (v_pre_mix_norm, v_post_mix_norm, v_pre_ffn_norm, v_post_ffn_norm, v_attn_out_norm, v_conv_out_norm))))
    sg, sd, snm, snv = _adamw_sum8(gathered, sw, sm, sv, "adamw_small")

    def unpack(p):
        p = p.reshape(DEPTH, SMALL_ROWS, D_MODEL)
        return dict(pre_mix_norm=p[:, 0], post_mix_norm=p[:, 1], pre_ffn_norm=p[:, 2], post_ffn_norm=p[:, 3],
                    attn_out_norm=p[:, 4, :ATTN_WIDTH], conv_out_norm=p[:, 4, ATTN_WIDTH:])

    small = [unpack(p) for p in (sg, sd, snm, snv)]
    loss = sg[SMALL_ROWS - 1, 0]
    sg3 = sg.reshape(DEPTH, SMALL_ROWS, D_MODEL)
    taps_grad_full = jnp.stack([sg3[:, 5, :ATTN_WIDTH], sg3[:, 5, ATTN_WIDTH:], sg3[:, 6, :ATTN_WIDTH]], axis=1)
    taps_grad = lax.dynamic_slice_in_dim(taps_grad_full, me * conv_channels, conv_channels, axis=2)

    def flat(a):
        return a.reshape(DEPTH * 3, conv_channels)

    td, tnm, tnv = _adamw_plain(flat(taps_grad), flat(conv_w), flat(m_conv_w), flat(v_conv_w), "adamw_taps")
    taps = [taps_grad] + [a.reshape(conv_w.shape) for a in (td, tnm, tnv)]

    def leaves(o):
        s = small[o]
        return (s["pre_mix_norm"], big["in"][o], taps[o], s["attn_out_norm"], s["conv_out_norm"], big["out"][o], s["post_mix_norm"],
                s["pre_ffn_norm"], hidden_major(big["gu"][o]), big["dn"][o], s["post_ffn_norm"])

    return (loss, dx[None], *leaves(0), *leaves(1), *leaves(2), *leaves(3))
```

```python
import math

import jax
import jax.numpy as jnp
from jax import lax
from jax.experimental import pallas as pl
from jax.experimental.pallas import tpu as pltpu

F32 = jnp.float32
BF16 = jnp.bfloat16
MESH = pl.DeviceIdType.MESH

SEQ = 4096
D_MODEL = 1024
DEPTH = 4
N_DEV = 8
ATTN_WIDTH = 512
IN_PROJ_WIDTH = 3072
FFN_HIDDEN = 2816
FFN_BLOCK = 2 * FFN_HIDDEN // N_DEV
W_IN_BLOCK = IN_PROJ_WIDTH // N_DEV
W_OUT_BLOCK = D_MODEL // N_DEV
W_DOWN_BLOCK = FFN_HIDDEN // N_DEV
HEAD_DIM = 64
ROPE_DIM = 16
ROPE_THETA = 500000.0
DILATIONS = (1, 4, 16)
HALF_WINDOW = 64
RMS_EPS = 1e-6
NEG_INF = -1e30
LANES = 128
Q_BLOCK = 128
K_WINDOW = Q_BLOCK + 2 * HALF_WINDOW
PERM_CHUNK = 256
ROW_TILE = 512
FFN_TILE = 256
WGRAD_COLS = 512
ADAM_LR, ADAM_B1, ADAM_B2, ADAM_EPS, ADAM_WD, ADAM_STEP = 0.001, 0.9, 0.999, 1e-08, 0.01, 10
MIB = 1024 * 1024
PINNED_BYTES = 256 * 1024

WEIGHT_KINDS = ("in", "out", "gu", "dn")
FULL_SHAPES = {"in": (D_MODEL, IN_PROJ_WIDTH), "out": (D_MODEL, D_MODEL), "gu": (N_DEV, FFN_BLOCK, D_MODEL), "dn": (FFN_HIDDEN, D_MODEL)}
SHARD_SHAPES = {"in": (D_MODEL, W_IN_BLOCK), "out": (W_OUT_BLOCK, D_MODEL), "gu": (FFN_BLOCK, D_MODEL), "dn": (W_DOWN_BLOCK, D_MODEL)}
ANY = pl.BlockSpec(memory_space=pl.ANY)


def _sds(shape, dtype):
    return jax.ShapeDtypeStruct(shape, dtype)


def _rows(width, tile=ROW_TILE):
    return pl.BlockSpec((tile, width), lambda i: (i, 0))


def _frows(width):
    return _rows(width, FFN_TILE)


def _cols(height, tile):
    return pl.BlockSpec((height, tile), lambda i: (0, i))


def _const(shape):
    return pl.BlockSpec(shape, lambda i: (0,) * len(shape))


def _const1(shape):
    return pl.BlockSpec(shape, lambda i: (0,) * len(shape), pipeline_mode=pl.Buffered(1))


HALO_ROWS = 16


def _halo_prev(width, tile=ROW_TILE):
    return pl.BlockSpec((HALO_ROWS, width), lambda i: (jnp.maximum(i * (tile // HALO_ROWS) - 1, 0), 0))


def _halo_next(width, tile=ROW_TILE):
    return pl.BlockSpec((HALO_ROWS, width), lambda i: (jnp.minimum((i + 1) * (tile // HALO_ROWS), SEQ // HALO_ROWS - 1), 0))


def _rms(x):
    r = lax.rsqrt(jnp.mean(x * x, axis=-1, keepdims=True) + RMS_EPS)
    return x * r, r


def _rms_bwd(dn, y, r):
    return r * (dn - y * jnp.mean(dn * y, axis=-1, keepdims=True))


def _dot(a, b):
    return jnp.dot(a, b, preferred_element_type=F32)


def _dot_nt(a, b):
    return lax.dot_general(a, b, (((1,), (1,)), ((), ())), preferred_element_type=F32)


def _dot_tn(a, b):
    return lax.dot_general(a, b, (((0,), (0,)), ((), ())), preferred_element_type=F32)


def _place():
    return lax.axis_index("x"), lax.axis_index("y"), lax.axis_index("c")


def _block_of(px, py, pc):
    return 4 * px + 2 * py + pc


def _weight_block(ref, kind, blk):
    if kind == "in":
        return ref.at[:, pl.ds(blk * W_IN_BLOCK, W_IN_BLOCK)]
    if kind == "out":
        return ref.at[pl.ds(blk * W_OUT_BLOCK, W_OUT_BLOCK), :]
    if kind == "gu":
        return ref.at[blk]
    return ref.at[pl.ds(blk * W_DOWN_BLOCK, W_DOWN_BLOCK), :]


def _dma_semaphores(n):
    return [pltpu.SemaphoreType.DMA((n, 7)), pltpu.SemaphoreType.DMA((n, 7)), pltpu.SemaphoreType.DMA((n,))]


class _Gather:
    def __init__(self, kinds, shards):
        self.kinds, self.operands = tuple(kinds), list(shards)
        self.tag = "gather_" + "_".join(kinds)
        self.out_shape = [_sds(FULL_SHAPES[k], BF16) for k in kinds]
        self.scratch = _dma_semaphores(len(kinds))

    def _parties(self):
        x, y, c = _place()
        return (x, y, c), (x, y, 1 - c), [(1 - x, y), (x, 1 - y), (1 - x, 1 - y)], c

    def _copy(self, outs, sems, w, k, block, to, src=None):
        dst = _weight_block(outs[w], self.kinds[w], _block_of(*block))
        return pltpu.make_async_remote_copy(src_ref=dst if src is None else src, dst_ref=dst, send_sem=sems[0].at[w, k],
                                            recv_sem=sems[1].at[w, k], device_id=to, device_id_type=MESH)

    def _own(self, srcs, outs, sems, w, me):
        return pltpu.make_async_copy(srcs[w], _weight_block(outs[w], self.kinds[w], _block_of(*me)), sems[2].at[w])

    def _first(self, srcs, outs, sems, w):
        me, sibling, chips, c = self._parties()
        return [self._copy(outs, sems, w, 0, me, sibling, src=srcs[w])] + [
            self._copy(outs, sems, w, 1 + j, me, (*chip, c), src=srcs[w]) for j, chip in enumerate(chips)]

    def start(self, srcs, outs, sems):
        me = self._parties()[0]
        for w in range(len(self.kinds)):
            self._own(srcs, outs, sems, w, me).start()
            for cp in self._first(srcs, outs, sems, w):
                cp.start()

    def forward(self, srcs, outs, sems):
        me, sibling, chips, c = self._parties()
        for j, chip in enumerate(chips):
            for w in range(len(self.kinds)):
                self._copy(outs, sems, w, 1 + j, (*chip, c), me).wait_recv()
                self._copy(outs, sems, w, 4 + j, (*chip, c), sibling).start()

    def finish(self, srcs, outs, sems):
        me, sibling, chips, c = self._parties()
        for w in range(len(self.kinds)):
            self._copy(outs, sems, w, 0, sibling, me).wait_recv()
            for j, chip in enumerate(chips):
                self._copy(outs, sems, w, 4 + j, (*chip, 1 - c), me).wait_recv()
        for w in range(len(self.kinds)):
            for cp in self._first(srcs, outs, sems, w):
                cp.wait_send()
            for j, chip in enumerate(chips):
                self._copy(outs, sems, w, 4 + j, (*chip, c), sibling).wait_send()
            self._own(srcs, outs, sems, w, me).wait()


def _peers(x, y, c):
    return [(x ^ a, y ^ b, c ^ e) for a in (0, 1) for b in (0, 1) for e in (0, 1) if (a, b, e) != (0, 0, 0)]


class _Scatter:
    def __init__(self, kinds, grads):
        self.kinds, self.operands = tuple(kinds), list(grads)
        self.tag = "scatter_" + "_".join(kinds)
        self.out_shape = [_sds((N_DEV,) + SHARD_SHAPES[k], BF16) for k in kinds]
        self.scratch = _dma_semaphores(len(kinds))

    def _copies(self, srcs, outs, sems):
        x, y, c = _place()
        me = _block_of(x, y, c)
        copies = []
        for w, kind in enumerate(self.kinds):
            copies.append(pltpu.make_async_copy(_weight_block(srcs[w], kind, me), outs[w].at[me], sems[2].at[w]))
            for k, peer in enumerate(_peers(x, y, c)):
                copies.append(pltpu.make_async_remote_copy(
                    src_ref=_weight_block(srcs[w], kind, _block_of(*peer)), dst_ref=outs[w].at[me],
                    send_sem=sems[0].at[w, k], recv_sem=sems[1].at[w, k], device_id=peer, device_id_type=MESH))
        return copies

    def start(self, srcs, outs, sems):
        for cp in self._copies(srcs, outs, sems):
            cp.start()

    def forward(self, srcs, outs, sems):
        pass

    def finish(self, srcs, outs, sems):
        for cp in self._copies(srcs, outs, sems):
            cp.wait()


N_CHIPS = N_DEV // 2


class _ScatterChips:
    def __init__(self, kinds, blocks):
        self.kinds, self.operands = tuple(kinds), list(blocks)
        self.tag = "scatter_chips_" + "_".join(kinds)
        self.out_shape = [_sds((N_CHIPS,) + SHARD_SHAPES[k], BF16) for k in kinds]
        self.scratch = _dma_semaphores(len(kinds))

    def _copies(self, srcs, outs, sems):
        x, y, c = _place()
        mine = 2 * x + y
        copies = []
        for w in range(len(self.kinds)):
            copies.append(pltpu.make_async_copy(srcs[w].at[mine], outs[w].at[mine], sems[2].at[w]))
            for j, (px, py) in enumerate([(1 - x, y), (x, 1 - y), (1 - x, 1 - y)]):
                copies.append(pltpu.make_async_remote_copy(
                    src_ref=srcs[w].at[2 * px + py], dst_ref=outs[w].at[mine], send_sem=sems[0].at[w, j], recv_sem=sems[1].at[w, j],
                    device_id=(px, py, c), device_id_type=MESH))
        return copies

    def start(self, srcs, outs, sems):
        for cp in self._copies(srcs, outs, sems):
            cp.start()

    def forward(self, srcs, outs, sems):
        pass

    def finish(self, srcs, outs, sems):
        for cp in self._copies(srcs, outs, sems):
            cp.wait()


def _in_hbm(a):
    return pltpu.with_memory_space_constraint(a, pltpu.HBM) if a.size * a.dtype.itemsize >= PINNED_BYTES else a


def _out_hbm(s):
    return pltpu.HBM(s.shape, s.dtype) if math.prod(s.shape) * jnp.dtype(s.dtype).itemsize >= PINNED_BYTES else s


def _call(body, args, *, grid, in_specs, out_specs, out_shape, scratch_shapes=(), vmem_mib=None, name, comm=None):
    kwargs = {} if vmem_mib is None else dict(compiler_params=pltpu.CompilerParams(vmem_limit_bytes=vmem_mib * MIB))
    in_specs, out_specs, out_shape, scratch_shapes = list(in_specs), list(out_specs), list(out_shape), list(scratch_shapes)
    args = [_in_hbm(a) for a in args]
    out_shape = [_out_hbm(s) for s in out_shape]
    if comm is None:
        res = pl.pallas_call(body, grid=grid, in_specs=in_specs, out_specs=out_specs, out_shape=out_shape,
                             scratch_shapes=scratch_shapes, name=name, **kwargs)(*args)
        return list(res), None
    n_in, n_out, n_scr = len(in_specs), len(out_specs), len(scratch_shapes)
    c_in, c_out = len(comm.operands), len(comm.out_shape)
    last = math.prod(grid) - 1

    def carried(*refs):
        cuts = [n_in, c_in, n_out, c_out, n_scr]
        parts, at = [], 0
        for n in cuts:
            parts.append(refs[at:at + n])
            at += n
        ins, c_ins, outs, c_outs, scr = parts
        sems = refs[at:]
        step = pl.program_id(0)
        for axis in range(1, len(grid)):
            step = step * grid[axis] + pl.program_id(axis)

        @pl.when(step == 0)
        def _():
            comm.start(c_ins, c_outs, sems)

        @pl.when(step == last)
        def _():
            comm.forward(c_ins, c_outs, sems)

        body(*ins, *outs, *scr)

        @pl.when(step == last)
        def _():
            comm.finish(c_ins, c_outs, sems)

    res = pl.pallas_call(carried, grid=grid, in_specs=in_specs + [ANY] * c_in, out_specs=out_specs + [ANY] * c_out,
                         out_shape=out_shape + [_out_hbm(s) for s in comm.out_shape], scratch_shapes=scratch_shapes + comm.scratch,
                         name=name + "_" + comm.tag, **kwargs)(*args, *[_in_hbm(a) for a in comm.operands])
    return list(res[:n_out]), list(res[n_out:])


class _GatherSmall:
    def __init__(self, v, tag):
        self.operands, self.tag = [v], tag
        self.out_shape = [_sds((N_DEV,) + v.shape, F32)]
        self.scratch = _dma_semaphores(1)

    def _copies(self, srcs, outs, sems):
        x, y, c = _place()
        me = _block_of(x, y, c)
        copies = [pltpu.make_async_copy(srcs[0], outs[0].at[me], sems[2].at[0])]
        for k, peer in enumerate(_peers(x, y, c)):
            copies.append(pltpu.make_async_remote_copy(src_ref=srcs[0], dst_ref=outs[0].at[me], send_sem=sems[0].at[0, k],
                                                       recv_sem=sems[1].at[0, k], device_id=peer, device_id_type=MESH))
        return copies

    def start(self, srcs, outs, sems):
        for cp in self._copies(srcs, outs, sems):
            cp.start()

    def forward(self, srcs, outs, sems):
        pass

    def finish(self, srcs, outs, sems):
        for cp in self._copies(srcs, outs, sems):
            cp.wait()


class _Together:
    def __init__(self, *comms):
        self.comms = comms
        self.operands = [a for cm in comms for a in cm.operands]
        self.out_shape = [s for cm in comms for s in cm.out_shape]
        self.scratch = [s for cm in comms for s in cm.scratch]
        self.tag = "_".join(cm.tag for cm in comms)
        self.kinds = tuple(kind for cm in comms for kind in getattr(cm, "kinds", ()))

    def _each(self, srcs, outs, sems):
        a = o = s = 0
        for cm in self.comms:
            na, no, ns = len(cm.operands), len(cm.out_shape), len(cm.scratch)
            yield cm, srcs[a:a + na], outs[o:o + no], sems[s:s + ns]
            a, o, s = a + na, o + no, s + ns

    def start(self, srcs, outs, sems):
        for cm, *refs in self._each(srcs, outs, sems):
            cm.start(*refs)

    def forward(self, srcs, outs, sems):
        for cm, *refs in self._each(srcs, outs, sems):
            cm.forward(*refs)

    def finish(self, srcs, outs, sems):
        for cm, *refs in self._each(srcs, outs, sems):
            cm.finish(*refs)


def _rope_tables(pos_col, freq_row, comm=None):
    def body(p_ref, f_ref, c_ref, sa_ref, sb_ref):
        ang = p_ref[...].astype(F32) * f_ref[...]
        lane = lax.broadcasted_iota(jnp.int32, ang.shape, 1) % HEAD_DIM
        cos, sin = jnp.cos(ang), jnp.sin(ang)
        c_ref[...] = jnp.where(lane < ROPE_DIM, cos, 1.0)
        sa_ref[...] = jnp.where(lane < ROPE_DIM // 2, -sin, 0.0)
        sb_ref[...] = jnp.where((lane >= ROPE_DIM // 2) & (lane < ROPE_DIM), sin, 0.0)

    return _call(body, (pos_col, freq_row), grid=(SEQ // ROW_TILE,), in_specs=[_rows(1), _const((1, LANES))],
                 out_specs=[_rows(LANES)] * 3, out_shape=[_sds((SEQ, LANES), F32)] * 3, name="rope_tables", comm=comm)


def _rotate(t, c, sa, sb):
    parts = []
    for g in range(ATTN_WIDTH // LANES):
        tg = t[:, g * LANES:(g + 1) * LANES]
        parts.append(tg * c + pltpu.roll(tg, LANES - 8, axis=1) * sa + pltpu.roll(tg, 8, axis=1) * sb)
    return jnp.concatenate(parts, axis=1)


def _rotate_transposed(dt, c, sa, sb):
    parts = []
    for g in range(ATTN_WIDTH // LANES):
        dg = dt[:, g * LANES:(g + 1) * LANES]
        parts.append(dg * c + pltpu.roll(dg * sa, 8, axis=1) + pltpu.roll(dg * sb, LANES - 8, axis=1))
    return jnp.concatenate(parts, axis=1)


def _inproj_fwd(x, g_pre, w_in, tc, tsa, tsb, comm=None):
    def body(x_ref, g_ref, w_ref, c_ref, sa_ref, sb_ref, q_ref, k_ref, v_ref, u_ref, b_ref, cc_ref):
        y, _ = _rms(x_ref[...])
        h = (y * g_ref[...]).astype(BF16)

        def proj(n):
            return _dot(h, w_ref[:, n * ATTN_WIDTH:(n + 1) * ATTN_WIDTH])

        c, sa, sb = c_ref[...], sa_ref[...], sb_ref[...]
        q_ref[...] = (_rotate(proj(0), c, sa, sb) * (HEAD_DIM ** -0.5)).astype(BF16)
        k_ref[...] = _rotate(proj(1), c, sa, sb).astype(BF16)
        v_ref[...] = proj(2).astype(BF16)
        u_ref[...] = proj(3).astype(BF16)
        b_ref[...] = proj(4).astype(BF16)
        cc_ref[...] = proj(5).astype(BF16)

    a = ATTN_WIDTH
    return _call(
        body, (x, g_pre, w_in, tc, tsa, tsb), grid=(SEQ // ROW_TILE,),
        in_specs=[_rows(D_MODEL), _const((1, D_MODEL)), _const1((D_MODEL, IN_PROJ_WIDTH)), _rows(LANES), _rows(LANES), _rows(LANES)],
        out_specs=[_rows(a)] * 6, out_shape=[_sds((SEQ, a), BF16)] * 6,
        vmem_mib=40, name="inproj_fwd", comm=comm)


def _head_masks():
    lane = lax.broadcasted_iota(jnp.int32, (1, LANES), 1)
    first = lane < HEAD_DIM
    return first, first.astype(F32), 1.0 - first.astype(F32)


def _perm_chunks(dil):
    length = SEQ // dil
    out = []
    for r in range(dil):
        for c0 in range(0, length, PERM_CHUNK):
            chunk = (r * length + c0) // PERM_CHUNK
            rows = pl.ds(c0, PERM_CHUNK) if dil == 1 else pl.ds(r + dil * c0, PERM_CHUNK, stride=dil)
            out.append((chunk, rows))
    return out


def _chunk(c, offset=0):
    return pl.ds(offset + c * PERM_CHUNK, PERM_CHUNK)


def _write_band_bias(bias_ref):
    rr = lax.broadcasted_iota(jnp.int32, (Q_BLOCK, K_WINDOW), 0)
    cc = lax.broadcasted_iota(jnp.int32, (Q_BLOCK, K_WINDOW), 1)
    band = (cc >= rr) & (cc - rr <= 2 * HALF_WINDOW)
    bias_ref[0] = jnp.where(band, 0.0, NEG_INF)
    bias_ref[1] = jnp.where(band & (cc >= HALF_WINDOW), 0.0, NEG_INF)
    bias_ref[2] = jnp.where(band & (cc < Q_BLOCK + HALF_WINDOW), 0.0, NEG_INF)


def _band_bias_index(m0, length):
    return jnp.where(m0 % length == 0, 1, 0) + jnp.where((m0 + Q_BLOCK) % length == 0, 2, 0)


def _zero_key_padding(bufs):
    pad = jnp.zeros((HALF_WINDOW, LANES), BF16)
    for buf in bufs:
        buf[pl.ds(0, HALF_WINDOW), :] = pad
        buf[pl.ds(SEQ + HALF_WINDOW, HALF_WINDOW), :] = pad


def _attn_fwd(q, k, v, comm=None):
    group = 8

    def body(q_ref, k_ref, v_ref, o_ref, lse_ref, q32, k32, v32, qa, qb, kp, vp, accp, mlp,
             acc0, acc1, acc2, ml0, ml1, ml2, bias_ref):
        first, mask_a, mask_b = _head_masks()
        low = lax.broadcasted_iota(jnp.int32, (1, LANES), 1) % HEAD_DIM < HEAD_DIM // 2
        _zero_key_padding((kp, vp))
        _write_band_bias(bias_ref)
        q32[...] = q_ref[...].astype(F32)
        k32[...] = k_ref[...].astype(F32)
        v32[...] = v_ref[...].astype(F32)
        natural = ((acc0, ml0), (acc1, ml1), (acc2, ml2))

        for branch, dil in enumerate(DILATIONS):
            length = SEQ // dil
            assert length >= 2 * Q_BLOCK
            chunks = _perm_chunks(dil)
            for c, rows in chunks:
                val = q32[rows, :]
                qa[_chunk(c), :] = (val * mask_a).astype(BF16)
                qb[_chunk(c), :] = (val * mask_b).astype(BF16)
                kp[_chunk(c, HALF_WINDOW), :] = k32[rows, :].astype(BF16)
                vp[_chunk(c, HALF_WINDOW), :] = v32[rows, :].astype(BF16)
            acc_dst, ml_dst = natural[branch] if dil == 1 else (accp, mlp)

            def blocks(i, carry, length=length, acc_dst=acc_dst, ml_dst=ml_dst):
                base = pl.multiple_of(i * (group * Q_BLOCK), group * Q_BLOCK)
                starts = [base + g * Q_BLOCK for g in range(group)]
                scores = [[_dot_nt(qx[pl.ds(m0, Q_BLOCK), :], kp[pl.ds(m0, K_WINDOW), :]) for qx in (qa, qb)] for m0 in starts]
                probs = []
                for m0, pair in zip(starts, scores):
                    bias = bias_ref[_band_bias_index(m0, length)]
                    stats, ps = [], []
                    for s in pair:
                        s = s + bias
                        m = jnp.max(s, axis=1, keepdims=True)
                        p = jnp.exp(s - m)
                        stats.append(jnp.where(low, m, jnp.sum(p, axis=1, keepdims=True)))
                        ps.append(p.astype(BF16))
                    ml_dst[pl.ds(m0, Q_BLOCK), :] = jnp.where(first, stats[0], stats[1])
                    probs.append(ps)
                for m0, ps in zip(starts, probs):
                    vw = vp[pl.ds(m0, K_WINDOW), :]
                    acc_dst[pl.ds(m0, Q_BLOCK), :] = jnp.where(first, _dot(ps[0], vw), _dot(ps[1], vw))
                return carry

            lax.fori_loop(0, SEQ // (group * Q_BLOCK), blocks, 0)

            if dil > 1:
                for c, rows in chunks:
                    natural[branch][0][rows, :] = accp[_chunk(c), :]
                    natural[branch][1][rows, :] = mlp[_chunk(c), :]

        for c in range(SEQ // PERM_CHUNK):
            packed = [ml[_chunk(c), :] for _, ml in natural]
            ms = [jnp.where(low, ml, pltpu.roll(ml, HEAD_DIM // 2, axis=1)) for ml in packed]
            ls = [jnp.where(low, pltpu.roll(ml, LANES - HEAD_DIM // 2, axis=1), ml) for ml in packed]
            m_all = jnp.maximum(jnp.maximum(ms[0], ms[1]), ms[2])
            es = [jnp.exp(m - m_all) for m in ms]
            z = ls[0] * es[0] + ls[1] * es[1] + ls[2] * es[2]
            num = natural[0][0][_chunk(c), :] * es[0] + natural[1][0][_chunk(c), :] * es[1] + natural[2][0][_chunk(c), :] * es[2]
            o_ref[_chunk(c), :] = num / z
            lse_ref[_chunk(c), :] = m_all + jnp.log(z)

    col = pl.BlockSpec((SEQ, LANES), lambda h: (0, h))
    padded = SEQ + 2 * HALF_WINDOW
    return _call(
        body, (q, k, v), grid=(ATTN_WIDTH // LANES,), in_specs=[col] * 3, out_specs=[col] * 2,
        out_shape=[_sds((SEQ, ATTN_WIDTH), F32)] * 2,
        scratch_shapes=[pltpu.VMEM((SEQ, LANES), F32)] * 3 + [pltpu.VMEM((SEQ, LANES), BF16)] * 2
        + [pltpu.VMEM((padded, LANES), BF16)] * 2 + [pltpu.VMEM((SEQ, LANES), F32)] * 8
        + [pltpu.VMEM((3, Q_BLOCK, K_WINDOW), F32)],
        vmem_mib=52, name="attn_fwd", comm=comm)


def _shifted(t, before, after, i):
    tile = t.shape[0]
    row = lax.broadcasted_iota(jnp.int32, (tile, 1), 0)
    before = jnp.where(i > 0, before, 0.0)
    after = jnp.where(i < SEQ // tile - 1, after, 0.0)
    return (jnp.where(row == 0, before, pltpu.roll(t, 1, axis=0)),
            jnp.where(row == tile - 1, after, pltpu.roll(t, tile - 1, axis=0)))


def _last_row(ref):
    return ref[HALO_ROWS - 1:HALO_ROWS, :].astype(F32)


def _first_row(ref):
    return ref[0:1, :].astype(F32)


def _conv_parts(u, c, u_prev, c_prev, u_next, c_next, cw, i):
    t = c * u
    t_prev, t_next = _shifted(t, _last_row(c_prev) * _last_row(u_prev), _first_row(c_next) * _first_row(u_next), i)
    s = cw[0:1, :] * t_prev + cw[1:2, :] * t + cw[2:3, :] * t_next
    return t, t_prev, t_next, s


def _mix_fwd(attn, u, b, c, conv_w, g_attn, g_conv, w_out, x, g_post, comm=None):
    def body(a_ref, u_ref, b_ref, c_ref, up_ref, cp_ref, un_ref, cn_ref, cw_ref, ga_ref, gc_ref, w_ref, x_ref, gp_ref,
             x1_ref, mg_ref, mix_ref):
        i = pl.program_id(0)
        _, _, _, s = _conv_parts(u_ref[...].astype(F32), c_ref[...].astype(F32), up_ref, cp_ref, un_ref, cn_ref, cw_ref[...], i)
        ya, _ = _rms(a_ref[...])
        yc, _ = _rms(b_ref[...].astype(F32) * s)
        merged = jnp.concatenate([ya * ga_ref[...], yc * gc_ref[...]], axis=1).astype(BF16)
        mix = _dot(merged, w_ref[...])
        ym, _ = _rms(mix)
        mg_ref[...] = merged.T
        mix_ref[...] = mix.astype(BF16)
        x1_ref[...] = x_ref[...] + ym * gp_ref[...]

    a = ATTN_WIDTH
    return _call(
        body, (attn, u, b, c, u, c, u, c, conv_w, g_attn, g_conv, w_out, x, g_post), grid=(SEQ // ROW_TILE,),
        in_specs=[_rows(a)] * 4 + [_halo_prev(a)] * 2 + [_halo_next(a)] * 2
        + [_const((3, a)), _const((1, a)), _const((1, a)), _const1((D_MODEL, D_MODEL)), _rows(D_MODEL), _const((1, D_MODEL))],
        out_specs=[_rows(D_MODEL), _cols(D_MODEL, ROW_TILE), _rows(D_MODEL)],
        out_shape=[_sds((SEQ, D_MODEL), F32), _sds((D_MODEL, SEQ), BF16), _sds((SEQ, D_MODEL), BF16)],
        vmem_mib=40, name="mix_fwd", comm=comm)


def _gu_spec():
    return pl.BlockSpec((N_DEV, FFN_TILE, FFN_BLOCK), lambda i: (0, i, 0))


def _ffn_fwd(x1, g_pre, w_gu, w_dn, g_post, comm=None, target=None):
    n_tiles = SEQ // ROW_TILE

    def body(*refs):
        if target is None:
            x_ref, g_ref, wgu_ref, wdn_ref, gp_ref, x2_ref, gu_ref, f_ref = refs
        else:
            x_ref, g_ref, wgu_ref, wdn_ref, gp_ref, t_ref, x2_ref, gu_ref, f_ref, loss_ref, acc = refs
        x1 = x_ref[...]
        y, _ = _rms(x1)
        h = (y * g_ref[...]).astype(BF16)
        f = jnp.zeros((ROW_TILE, D_MODEL), F32)

        def gate_up(j):
            return _dot_nt(h, wgu_ref[j]), _dot_nt(h, wgu_ref[j + N_DEV // 2])

        ahead = gate_up(0)
        for j in range(N_DEV // 2):
            gate, up = ahead
            if j + 1 < N_DEV // 2:
                ahead = gate_up(j + 1)
            gu_ref[j] = gate.astype(BF16)
            gu_ref[j + N_DEV // 2] = up.astype(BF16)
            act = (gate * jax.nn.sigmoid(gate) * up).astype(BF16)
            f = f + _dot(act, wdn_ref[pl.ds(j * FFN_BLOCK, FFN_BLOCK), :])
        yf, _ = _rms(f)
        f_ref[...] = f
        x2 = x1 + yf * gp_ref[...]
        if target is None:
            x2_ref[...] = x2
        else:
            i = pl.program_id(0)
            err = x2 - t_ref[...]
            x2_ref[...] = err * (1.0 / D_MODEL)

            @pl.when(i == 0)
            def _():
                acc[...] = jnp.zeros_like(acc)

            acc[...] += jnp.sum(err * err, axis=0, keepdims=True)

            @pl.when(i == n_tiles - 1)
            def _():
                loss_ref[...] = jnp.sum(acc[...], axis=1, keepdims=True) * (0.5 / D_MODEL)

    with_loss = target is not None
    return _call(
        body, (x1, g_pre, w_gu, w_dn, g_post) + ((target,) if with_loss else ()), grid=(n_tiles,),
        in_specs=[_rows(D_MODEL), _const((1, D_MODEL)), _const1((N_DEV, FFN_BLOCK, D_MODEL)), _const1((FFN_HIDDEN, D_MODEL)),
                  _const((1, D_MODEL))] + ([_rows(D_MODEL)] if with_loss else []),
        out_specs=[_rows(D_MODEL), pl.BlockSpec((N_DEV, ROW_TILE, FFN_BLOCK), lambda i: (0, i, 0)), _rows(D_MODEL)]
        + ([_const((1, 1))] if with_loss else []),
        out_shape=[_sds((SEQ, D_MODEL), F32), _sds((N_DEV, SEQ, FFN_BLOCK), BF16), _sds((SEQ, D_MODEL), F32)]
        + ([_sds((1, 1), F32)] if with_loss else []),
        scratch_shapes=[pltpu.VMEM((1, D_MODEL), F32)] if with_loss else [],
        vmem_mib=58, name="ffn_fwd_loss" if with_loss else "ffn_fwd", comm=comm)


def _accumulate(pairs, i):
    @pl.when(i == 0)
    def _():
        for ref, value in pairs:
            ref[...] = value

    @pl.when(i > 0)
    def _():
        for ref, value in pairs:
            ref[...] += value


def _colsum(v):
    return jnp.sum(v, axis=0, keepdims=True)


def _ffn_bwd(dx2, f, x1, gu, w_gu, w_dn, g_post, g_pre, comm=None):
    half = N_DEV // 2

    def body(dx2_ref, f_ref, x1_ref, gu_ref, wgu_ref, wdn_ref, gpost_ref, gpre_ref,
             dx1_ref, df_ref, act_ref, dgu_ref, h_ref, dgpost_ref, dgpre_ref):
        i = pl.program_id(0)
        dx2 = dx2_ref[...]
        yf, rf = _rms(f_ref[...])
        dg_post = _colsum(dx2 * yf)
        df = _rms_bwd(dx2 * gpost_ref[...], yf, rf).astype(BF16)
        df_ref[...] = df
        dh = jnp.zeros((FFN_TILE, D_MODEL), F32)

        def d_act(j):
            return _dot_nt(df, wdn_ref[pl.ds(j * FFN_BLOCK, FFN_BLOCK), :])

        ahead = d_act(0)
        for j in range(half):
            dact = ahead
            if j + 1 < half:
                ahead = d_act(j + 1)
            gate = gu_ref[j].astype(F32)
            up = gu_ref[j + half].astype(F32)
            sig = jax.nn.sigmoid(gate)
            silu = gate * sig
            act_ref[j] = (silu * up).astype(BF16)
            dgate = (dact * up * (sig * (1.0 + gate * (1.0 - sig)))).astype(BF16)
            dup = (dact * silu).astype(BF16)
            dgu_ref[j] = dgate
            dgu_ref[j + half] = dup
            dh = dh + _dot(dgate, wgu_ref[j]) + _dot(dup, wgu_ref[j + half])
        y1, r1 = _rms(x1_ref[...])
        h_ref[...] = (y1 * gpre_ref[...]).astype(BF16)
        dx1_ref[...] = dx2 + _rms_bwd(dh * gpre_ref[...], y1, r1)
        _accumulate([(dgpost_ref, dg_post), (dgpre_ref, _colsum(dh * y1))], i)

    act_spec = pl.BlockSpec((half, FFN_TILE, FFN_BLOCK), lambda i: (0, i, 0))
    return _call(
        body, (dx2, f, x1, gu, w_gu, w_dn, g_post, g_pre), grid=(SEQ // FFN_TILE,),
        in_specs=[_frows(D_MODEL)] * 3 + [_gu_spec(), _const1((N_DEV, FFN_BLOCK, D_MODEL)), _const1((FFN_HIDDEN, D_MODEL)),
                                          _const((1, D_MODEL)), _const((1, D_MODEL))],
        out_specs=[_frows(D_MODEL), _frows(D_MODEL), act_spec, _gu_spec(), _frows(D_MODEL), _const((1, D_MODEL)), _const((1, D_MODEL))],
        out_shape=[_sds((SEQ, D_MODEL), F32), _sds((SEQ, D_MODEL), BF16), _sds((half, SEQ, FFN_BLOCK), BF16),
                   _sds((N_DEV, SEQ, FFN_BLOCK), BF16), _sds((SEQ, D_MODEL), BF16), _sds((1, D_MODEL), F32), _sds((1, D_MODEL), F32)],
        vmem_mib=52, name="ffn_bwd", comm=comm)


def _wgrad(a_t, g, name):
    width = g.shape[1]

    def body(a_ref, g_ref, o_ref):
        o_ref[...] = _dot(a_ref[...], g_ref[...]).astype(BF16)

    return _call(body, (a_t, g), grid=(width // WGRAD_COLS,),
                 in_specs=[_const1((D_MODEL, SEQ)), pl.BlockSpec((SEQ, WGRAD_COLS), lambda j: (0, j))],
                 out_specs=[pl.BlockSpec((D_MODEL, WGRAD_COLS), lambda j: (0, j))], out_shape=[_sds((D_MODEL, width), BF16)],
                 vmem_mib=48, name=name)[0][0]


def _wgrad_paired(a, g, kind, to_owners=False, comm=None):
    shard = SHARD_SHAPES[kind]
    per_chip = 2 if kind == "gu" else 1
    n_compute = per_chip * N_CHIPS
    n_steps = n_compute + 1

    def chip_of(turn):
        if not to_owners:
            return turn
        return (2 * lax.axis_index("x") + lax.axis_index("y") + 1 + turn) % N_CHIPS

    def block(t):
        return jnp.minimum(t, n_compute - 1)

    if kind == "in":
        specs = [_const1((D_MODEL, SEQ)), pl.BlockSpec((SEQ, 2 * W_IN_BLOCK), lambda t: (0, chip_of(block(t))))]
    else:
        specs = [pl.BlockSpec((1, SEQ, FFN_BLOCK), lambda t: (block(t), 0, 0)), _const1((SEQ, D_MODEL))]

    def body(a_ref, g_ref, o_ref, *rest):
        if to_owners:
            landing, sendbuf, recvbuf, keepbuf, send_sem, recv_sem, donebuf, owner_send, owner_recv, own_sem = rest
        else:
            sendbuf, recvbuf, keepbuf, send_sem, recv_sem = rest
        t = pl.program_id(0)
        x, y, c = _place()

        def exchange(q):
            return pltpu.make_async_remote_copy(src_ref=sendbuf.at[q], dst_ref=recvbuf.at[q], send_sem=send_sem.at[q],
                                                recv_sem=recv_sem.at[q], device_id=(x, y, 1 - c), device_id_type=MESH)

        my_chip = 2 * x + y

        def to_owner(q):
            return pltpu.make_async_remote_copy(src_ref=donebuf.at[q], dst_ref=landing.at[my_chip], send_sem=owner_send.at[q],
                                                recv_sem=owner_recv.at[my_chip], device_id=(q // 2, q % 2, c), device_id_type=MESH)

        def keep_own():
            return pltpu.make_async_copy(donebuf.at[my_chip], landing.at[my_chip], own_sem)

        @pl.when((t >= per_chip) & (t % per_chip == 0))
        def _():
            q = chip_of(t // per_chip - 1)
            exchange(q).wait_recv()
            done = (keepbuf[...] + recvbuf[q].astype(F32)).astype(BF16)
            o_ref[0] = done
            if to_owners:
                donebuf[q] = done

                @pl.when(q == my_chip)
                def _():
                    keep_own().start()

                @pl.when(q != my_chip)
                def _():
                    to_owner(q).start()

        @pl.when(t < n_compute)
        def _():
            q = chip_of(t // per_chip)
            if kind == "gu":
                r = _dot_tn(a_ref[0], g_ref[...])

                @pl.when(t % 2 == c)
                def _():
                    keepbuf[...] = r

                @pl.when(t % 2 != c)
                def _():
                    sendbuf[q] = r.astype(BF16)
                    exchange(q).start()
            else:
                if kind == "dn":
                    r = _dot_tn(a_ref[0], g_ref[...])
                    lower, upper = r[:W_DOWN_BLOCK], r[W_DOWN_BLOCK:]
                else:
                    r = _dot(a_ref[...], g_ref[...])
                    lower, upper = r[:, :W_IN_BLOCK], r[:, W_IN_BLOCK:]
                keepbuf[...] = jnp.where(c == 0, lower, upper)
                sendbuf[q] = jnp.where(c == 0, upper, lower).astype(BF16)
                exchange(q).start()

        @pl.when(t == n_steps - 1)
        def _():
            for q in range(N_CHIPS):
                exchange(q).wait_send()
            if to_owners:
                keep_own().wait()
                for s in range(N_CHIPS):
                    @pl.when(s != my_chip)
                    def _(s=s):
                        pltpu.make_async_remote_copy(src_ref=donebuf.at[s], dst_ref=landing.at[s], send_sem=owner_send.at[s],
                                                     recv_sem=owner_recv.at[s], device_id=(x, y, c), device_id_type=MESH).wait()

    slots = (N_CHIPS,) + shard
    owners = int(to_owners)
    res, landed = _call(
        body, (a, g), grid=(n_steps,), in_specs=specs,
        out_specs=[pl.BlockSpec((1,) + shard, lambda t: (chip_of(jnp.clip(t // per_chip - 1, 0, N_CHIPS - 1)), 0, 0))] + [ANY] * owners,
        out_shape=[_sds(slots, BF16)] * (1 + owners),
        scratch_shapes=[pltpu.VMEM(slots, BF16), pltpu.VMEM(slots, BF16), pltpu.VMEM(shard, F32),
                        pltpu.SemaphoreType.DMA((N_CHIPS,)), pltpu.SemaphoreType.DMA((N_CHIPS,))]
        + [pltpu.VMEM(slots, BF16), pltpu.SemaphoreType.DMA((N_CHIPS,)), pltpu.SemaphoreType.DMA((N_CHIPS,)), pltpu.SemaphoreType.DMA] * owners,
        vmem_mib=52, name="wgrad_" + kind + "_paired" + "_to_owners" * owners, comm=comm)
    return res[owners] if comm is None else (res[owners], landed)


def _mix_bwd(dx1, mix, attn, u, b, c, conv_w, g_attn, g_conv, g_post, w_out):
    def body(dx1_ref, mix_ref, a_ref, u_ref, b_ref, c_ref, up_ref, cp_ref, un_ref, cn_ref, cw_ref, ga_ref, gc_ref, gp_ref, w_ref,
             dmix_ref, da_ref, ds_ref, db_ref, dgp_ref, dga_ref, dgc_ref):
        i = pl.program_id(0)
        dx1 = dx1_ref[...]
        ym, rm = _rms(mix_ref[...].astype(F32))
        dg_post = _colsum(dx1 * ym)
        dmix = _rms_bwd(dx1 * gp_ref[...], ym, rm).astype(BF16)
        dmix_ref[...] = dmix
        dmerged = _dot_nt(dmix, w_ref[...])
        dna, dnc = dmerged[:, :ATTN_WIDTH], dmerged[:, ATTN_WIDTH:]
        ya, ra = _rms(a_ref[...])
        da_ref[...] = _rms_bwd(dna * ga_ref[...], ya, ra)
        _, _, _, s = _conv_parts(u_ref[...].astype(F32), c_ref[...].astype(F32), up_ref, cp_ref, un_ref, cn_ref, cw_ref[...], i)
        gate_b = b_ref[...].astype(F32)
        yc, rc = _rms(gate_b * s)
        dy = _rms_bwd(dnc * gc_ref[...], yc, rc)
        db_ref[...] = (dy * s).astype(BF16)
        ds_ref[...] = (dy * gate_b).astype(BF16)
        _accumulate([(dgp_ref, dg_post), (dga_ref, _colsum(dna * ya)), (dgc_ref, _colsum(dnc * yc))], i)

    a = ATTN_WIDTH
    return _call(
        body, (dx1, mix, attn, u, b, c, u, c, u, c, conv_w, g_attn, g_conv, g_post, w_out), grid=(SEQ // ROW_TILE,),
        in_specs=[_rows(D_MODEL)] * 2 + [_rows(a)] * 4 + [_halo_prev(a)] * 2 + [_halo_next(a)] * 2
        + [_const((3, a)), _const((1, a)), _const((1, a)), _const((1, D_MODEL)), _const1((D_MODEL, D_MODEL))],
        out_specs=[_rows(D_MODEL)] + [_rows(a)] * 3 + [_const((1, D_MODEL)), _const((1, a)), _const((1, a))],
        out_shape=[_sds((SEQ, D_MODEL), BF16), _sds((SEQ, a), F32), _sds((SEQ, a), BF16), _sds((SEQ, a), BF16),
                   _sds((1, D_MODEL), F32), _sds((1, a), F32), _sds((1, a), F32)],
        vmem_mib=40, name="mix_bwd")[0]


def _attn_bwd(q, k, v, do, o, lse, comm=None):
    group = 8

    n_pairs = ATTN_WIDTH // LANES

    def body(q_hbm, k_hbm, v_hbm, do_hbm, o_hbm, lse_hbm, dq_ref, dk_ref, dv_ref,
             q_ref, k_ref, v_ref, do_ref, o_ref, lse_ref, in_sem,
             q32, k32, v32, qa, qb, doa, dob, kp, vp, lsep, dlp, dnat, dqp, dkp, dvp, bias_ref):
        pair = pl.program_id(0)
        sources = (q_hbm, k_hbm, v_hbm, do_hbm, o_hbm, lse_hbm)
        buffers = (q_ref, k_ref, v_ref, do_ref, o_ref, lse_ref)

        def fetch(i, p):
            return pltpu.make_async_copy(sources[i].at[:, pl.ds(pl.multiple_of(p * LANES, LANES), LANES)], buffers[i], in_sem.at[i])

        def prefetch(*which):
            @pl.when(pair + 1 < n_pairs)
            def _():
                for i in which:
                    fetch(i, pair + 1).start()

        @pl.when(pair == 0)
        def _():
            for i in range(len(sources)):
                fetch(i, 0).start()

        for i in range(len(sources)):
            fetch(i, pair).wait()

        first, mask_a, mask_b = _head_masks()
        _zero_key_padding((kp, vp))
        _write_band_bias(bias_ref)
        q32[...] = q_ref[...].astype(F32)
        k32[...] = k_ref[...].astype(F32)
        v32[...] = v_ref[...].astype(F32)
        prefetch(0, 1, 2)
        for c in range(SEQ // PERM_CHUNK):
            prod = do_ref[_chunk(c), :] * o_ref[_chunk(c), :]
            d_a = jnp.sum(prod * mask_a, axis=1, keepdims=True)
            d_b = jnp.sum(prod * mask_b, axis=1, keepdims=True)
            dnat[_chunk(c), :] = jnp.where(first, d_a, d_b)
        prefetch(4)

        for step, dil in enumerate(DILATIONS[1:] + DILATIONS[:1]):
            length = SEQ // dil
            assert length >= 2 * Q_BLOCK
            chunks = _perm_chunks(dil)
            for c, rows in chunks:
                val = q32[rows, :]
                qa[_chunk(c), :] = (val * mask_a).astype(BF16)
                qb[_chunk(c), :] = (val * mask_b).astype(BF16)
                val = do_ref[rows, :]
                doa[_chunk(c), :] = (val * mask_a).astype(BF16)
                dob[_chunk(c), :] = (val * mask_b).astype(BF16)
                kp[_chunk(c, HALF_WINDOW), :] = k32[rows, :].astype(BF16)
                vp[_chunk(c, HALF_WINDOW), :] = v32[rows, :].astype(BF16)
                lsep[_chunk(c), :] = lse_ref[rows, :]
                dlp[_chunk(c), :] = dnat[rows, :]
            if step == len(DILATIONS) - 1:
                prefetch(3, 5)
            zero = jnp.zeros((PERM_CHUNK, LANES), F32)
            for c in range(SEQ // PERM_CHUNK):
                dkp[_chunk(c), :] = zero
                dvp[_chunk(c), :] = zero
            dkp[pl.ds(SEQ, 2 * HALF_WINDOW), :] = zero[:2 * HALF_WINDOW]
            dvp[pl.ds(SEQ, 2 * HALF_WINDOW), :] = zero[:2 * HALF_WINDOW]

            heads = ((qa, doa, 0), (qb, dob, HEAD_DIM))

            def blocks(i, carry, length=length):
                base = pl.multiple_of(i * (group * Q_BLOCK), group * Q_BLOCK)
                starts = [base + g * Q_BLOCK for g in range(group)]
                raw = [[(_dot_nt(qx[pl.ds(m0, Q_BLOCK), :], kp[pl.ds(m0, K_WINDOW), :]),
                         _dot_nt(dox[pl.ds(m0, Q_BLOCK), :], vp[pl.ds(m0, K_WINDOW), :])) for qx, dox, _ in heads]
                       for m0 in starts]
                grads = []
                for m0, pair in zip(starts, raw):
                    bias = bias_ref[_band_bias_index(m0, length)]
                    lse_b, d_b = lsep[pl.ds(m0, Q_BLOCK), :], dlp[pl.ds(m0, Q_BLOCK), :]
                    out = []
                    for (s, dp), (_, _, col) in zip(pair, heads):
                        p = jnp.exp(s + bias - lse_b[:, col:col + 1])
                        out.append(((p * (dp - d_b[:, col:col + 1])).astype(BF16), p.astype(BF16)))
                    grads.append(out)
                for m0, out in zip(starts, grads):
                    qrows, krows = pl.ds(m0, Q_BLOCK), pl.ds(m0, K_WINDOW)
                    kw = kp[krows, :]
                    dk = jnp.zeros((K_WINDOW, LANES), F32)
                    dv = jnp.zeros((K_WINDOW, LANES), F32)
                    for (ds, p), (qx, dox, _) in zip(out, heads):
                        dk = dk + _dot_tn(ds, qx[qrows, :])
                        dv = dv + _dot_tn(p, dox[qrows, :])
                    dqp[qrows, :] = jnp.where(first, _dot(out[0][0], kw), _dot(out[1][0], kw)) * (HEAD_DIM ** -0.5)
                    dkp[krows, :] += dk
                    dvp[krows, :] += dv
                return carry

            lax.fori_loop(0, SEQ // (group * Q_BLOCK), blocks, 0)

            for c, rows in chunks:
                g_q, g_k, g_v = dqp[_chunk(c), :], dkp[_chunk(c, HALF_WINDOW), :], dvp[_chunk(c, HALF_WINDOW), :]
                if step == 0:
                    dq_ref[rows, :] = g_q
                    dk_ref[rows, :] = g_k
                    dv_ref[rows, :] = g_v
                else:
                    dq_ref[rows, :] = dq_ref[rows, :] + g_q
                    dk_ref[rows, :] = dk_ref[rows, :] + g_k
                    dv_ref[rows, :] = dv_ref[rows, :] + g_v

    col = pl.BlockSpec((SEQ, LANES), lambda h: (0, h))
    padded = SEQ + 2 * HALF_WINDOW
    return _call(
        body, (q, k, v, do, o, lse), grid=(n_pairs,), in_specs=[ANY] * 6, out_specs=[col] * 3,
        out_shape=[_sds((SEQ, ATTN_WIDTH), F32)] * 3,
        scratch_shapes=[pltpu.VMEM((SEQ, LANES), BF16)] * 3 + [pltpu.VMEM((SEQ, LANES), F32)] * 3 + [pltpu.SemaphoreType.DMA((6,))]
        + [pltpu.VMEM((SEQ, LANES), F32)] * 3 + [pltpu.VMEM((SEQ, LANES), BF16)] * 4
        + [pltpu.VMEM((padded, LANES), BF16)] * 2 + [pltpu.VMEM((SEQ, LANES), F32)] * 4 + [pltpu.VMEM((padded, LANES), F32)] * 2
        + [pltpu.VMEM((3, Q_BLOCK, K_WINDOW), F32)],
        vmem_mib=56, name="attn_bwd", comm=comm)


def _inproj_bwd(dq, dk, dv, ds, db, u, c, conv_w, tc, tsa, tsb, w_in, x, g_pre, dx1):
    def body(dq_ref, dk_ref, dv_ref, ds_ref, db_ref, u_ref, c_ref, dsp_ref, up_ref, cp_ref, dsn_ref, un_ref, cn_ref, cw_ref,
             tc_ref, tsa_ref, tsb_ref, w_ref, x_ref, g_ref, dx1_ref, dx_ref, dproj_ref, h_ref, dg_ref, dcw_ref):
        i = pl.program_id(0)
        cw = cw_ref[...]
        u, c = u_ref[...].astype(F32), c_ref[...].astype(F32)
        t, t_prev, t_next, _ = _conv_parts(u, c, up_ref, cp_ref, un_ref, cn_ref, cw, i)
        ds = ds_ref[...].astype(F32)
        ds_prev, ds_next = _shifted(ds, _last_row(dsp_ref), _first_row(dsn_ref), i)
        dt = cw[0:1, :] * ds_next + cw[1:2, :] * ds + cw[2:3, :] * ds_prev
        d_taps = jnp.concatenate([_colsum(ds * t_prev), _colsum(ds * t), _colsum(ds * t_next)], axis=0)
        tc_, tsa_, tsb_ = tc_ref[...], tsa_ref[...], tsb_ref[...]
        groups = ((2, lambda: dv_ref[...].astype(BF16)), (4, lambda: db_ref[...]),
                  (0, lambda: _rotate_transposed(dq_ref[...], tc_, tsa_, tsb_).astype(BF16)),
                  (1, lambda: _rotate_transposed(dk_ref[...], tc_, tsa_, tsb_).astype(BF16)),
                  (3, lambda: (dt * c).astype(BF16)), (5, lambda: (dt * u).astype(BF16)))
        dh = jnp.zeros((x_ref.shape[0], D_MODEL), F32)
        for n, make in groups:
            cols = pl.ds(n * a, a)
            part = make()
            dproj_ref[:, cols] = part
            dh = dh + _dot_nt(part, w_ref[:, cols])
        y, r = _rms(x_ref[...])
        h_ref[...] = (y * g_ref[...]).astype(BF16).T
        dx_ref[...] = dx1_ref[...] + _rms_bwd(dh * g_ref[...], y, r)
        _accumulate([(dg_ref, _colsum(dh * y)), (dcw_ref, d_taps)], i)

    a = ATTN_WIDTH
    tile = FFN_TILE

    def rows(width):
        return _rows(width, tile)

    return _call(
        body, (dq, dk, dv, ds, db, u, c, ds, u, c, ds, u, c, conv_w, tc, tsa, tsb, w_in, x, g_pre, dx1), grid=(SEQ // tile,),
        in_specs=[rows(a)] * 7 + [_halo_prev(a, tile)] * 3 + [_halo_next(a, tile)] * 3
        + [_const((3, a)), rows(LANES), rows(LANES), rows(LANES), _const1((D_MODEL, IN_PROJ_WIDTH)), rows(D_MODEL),
           _const((1, D_MODEL)), rows(D_MODEL)],
        out_specs=[rows(D_MODEL), rows(IN_PROJ_WIDTH), _cols(D_MODEL, tile), _const((1, D_MODEL)), _const((3, a))],
        out_shape=[_sds((SEQ, D_MODEL), F32), _sds((SEQ, IN_PROJ_WIDTH), BF16), _sds((D_MODEL, SEQ), BF16),
                   _sds((1, D_MODEL), F32), _sds((3, a), F32)],
        vmem_mib=48, name="inproj_bwd")


def _adamw_math(w, g, m, v):
    m = ADAM_B1 * m + (1.0 - ADAM_B1) * g
    v = ADAM_B2 * v + (1.0 - ADAM_B2) * (g * g)
    m_hat = m / (1.0 - ADAM_B1 ** ADAM_STEP)
    v_hat = v / (1.0 - ADAM_B2 ** ADAM_STEP)
    delta = -ADAM_LR * (m_hat / (jnp.sqrt(v_hat) + ADAM_EPS) + ADAM_WD * w)
    return delta, m, v


def _sum_parts(p_ref):
    g = p_ref[0].astype(F32)
    for k in range(1, p_ref.shape[0]):
        g = g + p_ref[k].astype(F32)
    return g


def _adamw_layers(parts, w, m, v, row_tile, name, comm=None):
    _, rows, cols = w.shape
    n_tiles = rows // row_tile

    def body(*refs):
        p_refs = refs[:DEPTH]
        w_ref, m_ref, v_ref, g_ref, d_ref, nm_ref, nv_ref = refs[DEPTH:]
        layer = pl.program_id(0)
        for l, p_ref in enumerate(p_refs):
            @pl.when(layer == l)
            def _(p_ref=p_ref):
                g = _sum_parts(p_ref)
                g_ref[0] = g
                d_ref[0], nm_ref[0], nv_ref[0] = _adamw_math(w_ref[0], g, m_ref[0], v_ref[0])

    def part_spec(l):
        return pl.BlockSpec((parts[l].shape[0], row_tile, cols),
                            lambda layer, i: (0, jnp.where(layer == l, i, jnp.where(layer < l, 0, n_tiles - 1)), 0))

    tile = pl.BlockSpec((1, row_tile, cols), lambda layer, i: (layer, i, 0))
    return _call(body, (*parts, w, m, v), grid=(DEPTH, n_tiles), in_specs=[part_spec(l) for l in range(DEPTH)] + [tile] * 3,
                 out_specs=[tile] * 4, out_shape=[_sds(w.shape, F32)] * 4, name=name, comm=comm)


def _adamw_sum8(parts, w, m, v, name):
    def body(p_ref, w_ref, m_ref, v_ref, g_ref, d_ref, nm_ref, nv_ref):
        g = _sum_parts(p_ref)
        g_ref[...] = g
        d_ref[...], nm_ref[...], nv_ref[...] = _adamw_math(w_ref[...], g, m_ref[...], v_ref[...])

    return pl.pallas_call(body, out_shape=[_sds(w.shape, F32)] * 4, name=name)(parts, w, m, v)


def _adamw_plain(g, w, m, v, name):
    def body(g_ref, w_ref, m_ref, v_ref, d_ref, nm_ref, nv_ref):
        d_ref[...], nm_ref[...], nv_ref[...] = _adamw_math(w_ref[...], g_ref[...], m_ref[...], v_ref[...])

    return pl.pallas_call(body, out_shape=[_sds(w.shape, F32)] * 3, name=name)(g, w, m, v)


SMALL_ROWS = 8
GAIN_NAMES = ("pre_mix_norm", "post_mix_norm", "pre_ffn_norm", "post_ffn_norm", "attn_out_norm", "conv_out_norm")


def _pack_small(pre_mix, post_mix, pre_ffn, post_ffn, attn_out, conv_out, taps):
    zeros = jnp.zeros((1, ATTN_WIDTH), F32)
    return jnp.concatenate([
        pre_mix, post_mix, pre_ffn, post_ffn, jnp.concatenate([attn_out, conv_out], axis=1),
        jnp.concatenate([taps[0:1], taps[1:2]], axis=1), jnp.concatenate([taps[2:3], zeros], axis=1),
        jnp.zeros((1, D_MODEL), F32)], axis=0)


def _rope(positions, comm=None):
    inv_freq = ROPE_THETA ** (-jnp.arange(0, ROPE_DIM, 2, dtype=F32) / ROPE_DIM)
    per_head = jnp.concatenate([inv_freq, inv_freq, jnp.zeros((HEAD_DIM - ROPE_DIM,), F32)])
    freq_row = jnp.tile(per_head, LANES // HEAD_DIM).reshape(1, LANES)
    return _rope_tables(positions.reshape(SEQ, 1), freq_row, comm)


def _layer_forward(h, gains, taps, tables, w, inproj_comm=None, attn_comm=None, mix_comm=None, ffn_comm=None, target=None):
    def arrived(comm, landed):
        return w if comm is None else {**w, **dict(zip(comm.kinds, landed))}

    (q, k, v, u, b, c), landed = _inproj_fwd(h, gains["pre_mix_norm"], w["in"], *tables, comm=inproj_comm)
    w = arrived(inproj_comm, landed)
    (attn, lse), landed = _attn_fwd(q, k, v, comm=attn_comm)
    w = arrived(attn_comm, landed)
    (x1, merged, mix), landed = _mix_fwd(attn, u, b, c, taps, gains["attn_out_norm"], gains["conv_out_norm"], w["out"], h,
                                         gains["post_mix_norm"], comm=mix_comm)
    w = arrived(mix_comm, landed)
    (x2, gu, f, *loss), landed_next = _ffn_fwd(x1, gains["pre_ffn_norm"], w["gu"], w["dn"], gains["post_ffn_norm"], comm=ffn_comm,
                                               target=target)
    out = x2 if target is None else (x2, loss[0])
    return out, (h, q, k, v, u, b, c, attn, lse, merged, mix, x1, gu, f), w, landed_next


class _GradientExchange:
    def ffn_grads(self, act, df, h2, dgu):
        return dict(dn=_wgrad_paired(act, df, "dn"), gu=_wgrad_paired(dgu, h2, "gu"))

    def w_out_grad(self, merged_t, dmix):
        return _wgrad(merged_t, dmix, "wgrad_out")

    def attention_passenger(self, ffn, g_out):
        return _Together(_ScatterChips(("gu", "dn"), (ffn["gu"], ffn["dn"])), _Scatter(("out",), (g_out,)))

    def w_in_grad(self, h_t, dproj, small_rows):
        if small_rows is None:
            return _wgrad(h_t, dproj, "wgrad_in"), None
        parts, landed = _wgrad_paired(h_t, dproj, "in", to_owners=True, comm=_GatherSmall(small_rows, "small"))
        return parts, landed[0]


def _layer_backward(dx, saved, gains, taps, tables, w, ffn_comm, exchange, last):
    x0, q, k, v, u, b, c, attn, lse, merged, mix, x1, gu, f = saved
    (dx1, df, act, dgu, h2, dg_post_ffn, dg_pre_ffn), landed_prev = _ffn_bwd(
        dx, f, x1, gu, w["gu"], w["dn"], gains["post_ffn_norm"], gains["pre_ffn_norm"], comm=ffn_comm)
    ffn = exchange.ffn_grads(act, df, h2, dgu)
    dmix, dattn, ds, db, dg_post_mix, dg_attn, dg_conv = _mix_bwd(
        dx1, mix, attn, u, b, c, taps, gains["attn_out_norm"], gains["conv_out_norm"], gains["post_mix_norm"], w["out"])
    g_out = exchange.w_out_grad(merged, dmix)
    attn_comm = exchange.attention_passenger(ffn, g_out)
    (dq, dk, dv), landed = _attn_bwd(q, k, v, dattn, attn, lse, comm=attn_comm)
    dx0, dproj, h1, dg_pre_mix, dtaps = _inproj_bwd(
        dq, dk, dv, ds, db, u, c, taps, *tables, w["in"], x0, gains["pre_mix_norm"], dx1)[0]
    small = _pack_small(dg_pre_mix, dg_post_mix, dg_pre_ffn, dg_post_ffn, dg_attn, dg_conv, dtaps)
    small_rows = None
    if last is not None:
        small_above, local_loss = last
        small_rows = lax.dynamic_update_slice(jnp.concatenate([small, small_above], axis=0), local_loss, (SMALL_ROWS - 1, 0))
    g_in, gathered = exchange.w_in_grad(h1, dproj, small_rows)
    rest = dict(zip(attn_comm.kinds, landed)) if attn_comm is not None else dict(ffn, out=g_out)
    return dx0, g_in, small, landed_prev, rest, gathered


def kernel(x, positions, pre_mix_norm, w_in, conv_w, attn_out_norm, conv_out_norm, w_out, post_mix_norm, pre_ffn_norm, w_gate_up, w_down, post_ffn_norm, loss_target, m_pre_mix_norm, m_w_in, m_conv_w, m_attn_out_norm, m_conv_out_norm, m_w_out, m_post_mix_norm, m_pre_ffn_norm, m_w_gate_up, m_w_down, m_post_ffn_norm, v_pre_mix_norm, v_w_in, v_conv_w, v_attn_out_norm, v_conv_out_norm, v_w_out, v_post_mix_norm, v_pre_ffn_norm, v_w_gate_up, v_w_down, v_post_ffn_norm):
    mx, my, mc = _place()
    me = _block_of(mx, my, mc)
    conv_channels = conv_w.shape[-1]

    def hidden_major(a):
        return jnp.swapaxes(a, 1, 2)

    big_w = dict(zip(WEIGHT_KINDS, (w_in, w_out, hidden_major(w_gate_up), w_down)))
    big_m = dict(zip(WEIGHT_KINDS, (m_w_in, m_w_out, hidden_major(m_w_gate_up), m_w_down)))
    big_v = dict(zip(WEIGHT_KINDS, (v_w_in, v_w_out, hidden_major(v_w_gate_up), v_w_down)))
    shards = {kind: big_w[kind].astype(BF16) for kind in WEIGHT_KINDS}
    all_gains = dict(pre_mix_norm=pre_mix_norm, attn_out_norm=attn_out_norm, conv_out_norm=conv_out_norm, post_mix_norm=post_mix_norm,
                     pre_ffn_norm=pre_ffn_norm, post_ffn_norm=post_ffn_norm)

    def gather(kinds, l):
        return _Gather(kinds, [shards[kind][l] for kind in kinds])

    def gains(l):
        return {name: g[l:l + 1] for name, g in all_gains.items()}

    early = ("in", "out", "dn")
    taps_flat = jnp.pad(conv_w.reshape(-1), (0, 8 * LANES - conv_w.size)).reshape(8, LANES)
    tables, (first, taps_all) = _rope(positions, _Together(gather(early[:1], 0), _GatherSmall(taps_flat, "taps")))
    taps_all = taps_all.reshape(N_DEV, 8 * LANES)[:, :conv_w.size]
    conv_w_full = taps_all.reshape(N_DEV, DEPTH, 3, conv_channels).transpose(1, 2, 0, 3).reshape(DEPTH, 3, ATTN_WIDTH)
    weights = [dict(zip(early[:1], [first]))] + [None] * (DEPTH - 1)
    saved = [None] * DEPTH
    h = x[0]
    for l in range(DEPTH):
        inproj_comm = gather(("out",), 0) if l == 0 else None
        mix_comm = gather(("dn",), 0) if l == 0 else None
        ffn_comm = gather(early, l + 1) if l + 1 < DEPTH else None
        target = loss_target[0] if l + 1 == DEPTH else None
        h, saved[l], weights[l], landed = _layer_forward(h, gains(l), conv_w_full[l], tables, weights[l], inproj_comm,
                                                         gather(("gu",), l), mix_comm, ffn_comm, target)
        if ffn_comm is not None:
            weights[l + 1] = dict(zip(ffn_comm.kinds, landed))
    dx, local_loss = h

    parts = {kind: [None] * DEPTH for kind in WEIGHT_KINDS}
    small_grads = [None] * DEPTH
    g_in_above = None
    exchange = _GradientExchange()
    for l in reversed(range(DEPTH)):
        ffn_comm = _Scatter(("in",), (g_in_above,)) if g_in_above is not None else None
        last = (jnp.concatenate(small_grads[1:], axis=0), local_loss) if l == 0 else None
        dx, g_in_above, small_grads[l], landed, rest, gathered = _layer_backward(
            dx, saved[l], gains(l), conv_w_full[l], tables, weights[l], ffn_comm, exchange, last)
        if ffn_comm is not None:
            parts["in"][l + 1] = landed[0]
        for kind, part in rest.items():
            parts[kind][l] = part

    parts["in"][0] = g_in_above
    tiles = {"in": 256, "out": 128, "gu": 176, "dn": 176}
    big = {kind: _adamw_layers(parts[kind], big_w[kind], big_m[kind], big_v[kind], tiles[kind], "adamw_" + kind)[0]
           for kind in WEIGHT_KINDS}


    def pack_state(state):
        rows = [_pack_small(*[state[name][l:l + 1] for name in GAIN_NAMES], jnp.zeros((3, ATTN_WIDTH), F32)) for l in range(DEPTH)]
        return jnp.concatenate(rows, axis=0)

    sw = pack_state(all_gains)
    sm = pack_state(dict(zip(GAIN_NAMES, (m_pre_mix_norm, m_post_mix_norm, m_pre_ffn_norm, m_post_ffn_norm, m_attn_out_norm, m_conv_out_norm))))
    sv = pack_state(dict(zip(GAIN_NAMES, (v_pre_mix_norm, v_post_mix_norm, v_pre_ffn_norm, v_post_ffn_norm, v_attn_out_norm, v_conv_out_norm))))
    sg, sd, snm, snv = _adamw_sum8(gathered, sw, sm, sv, "adamw_small")

    def unpack(p):
        p = p.reshape(DEPTH, SMALL_ROWS, D_MODEL)
        return dict(pre_mix_norm=p[:, 0], post_mix_norm=p[:, 1], pre_ffn_norm=p[:, 2], post_ffn_norm=p[:, 3],
                    attn_out_norm=p[:, 4, :ATTN_WIDTH], conv_out_norm=p[:, 4, ATTN_WIDTH:])

    small = [unpack(p) for p in (sg, sd, snm, snv)]
    loss = sg[SMALL_ROWS - 1, 0]
    sg3 = sg.reshape(DEPTH, SMALL_ROWS, D_MODEL)
    taps_grad_full = jnp.stack([sg3[:, 5, :ATTN_WIDTH], sg3[:, 5, ATTN_WIDTH:], sg3[:, 6, :ATTN_WIDTH]], axis=1)
    taps_grad = lax.dynamic_slice_in_dim(taps_grad_full, me * conv_channels, conv_channels, axis=2)

    def flat(a):
        return a.reshape(DEPTH * 3, conv_channels)

    td, tnm, tnv = _adamw_plain(flat(taps_grad), flat(conv_w), flat(m_conv_w), flat(v_conv_w), "adamw_taps")
    taps = [taps_grad] + [a.reshape(conv_w.shape) for a in (td, tnm, tnv)]

    def leaves(o):
        s = small[o]
        return (s["pre_mix_norm"], big["in"][o], taps[o], s["attn_out_norm"], s["conv_out_norm"], big["out"][o], s["post_mix_norm"],
                s["pre_ffn_norm"], hidden_major(big["gu"][o]), big["dn"][o], s["post_ffn_norm"])

    return (loss, dx[None], *leaves(0), *leaves(1), *leaves(2), *leaves(3))
```

```python
import math

import jax
import jax.numpy as jnp
from jax import lax
from jax.experimental import pallas as pl
from jax.experimental.pallas import tpu as pltpu

F32 = jnp.float32
BF16 = jnp.bfloat16
MESH = pl.DeviceIdType.MESH

SEQ = 4096
D_MODEL = 1024
DEPTH = 4
N_DEV = 8
ATTN_WIDTH = 512
IN_PROJ_WIDTH = 3072
FFN_HIDDEN = 2816
FFN_BLOCK = 2 * FFN_HIDDEN // N_DEV
W_IN_BLOCK = IN_PROJ_WIDTH // N_DEV
W_OUT_BLOCK = D_MODEL // N_DEV
W_DOWN_BLOCK = FFN_HIDDEN // N_DEV
HEAD_DIM = 64
ROPE_DIM = 16
ROPE_THETA = 500000.0
DILATIONS = (1, 4, 16)
HALF_WINDOW = 64
RMS_EPS = 1e-6
NEG_INF = -1e30
LANES = 128
Q_BLOCK = 128
K_WINDOW = Q_BLOCK + 2 * HALF_WINDOW
PERM_CHUNK = 256
ROW_TILE = 512
FFN_TILE = 256
WGRAD_COLS = 512
ADAM_LR, ADAM_B1, ADAM_B2, ADAM_EPS, ADAM_WD, ADAM_STEP = 0.001, 0.9, 0.999, 1e-08, 0.01, 10
MIB = 1024 * 1024
PINNED_BYTES = 256 * 1024

WEIGHT_KINDS = ("in", "out", "gu", "dn")
FULL_SHAPES = {"in": (D_MODEL, IN_PROJ_WIDTH), "out": (D_MODEL, D_MODEL), "gu": (N_DEV, FFN_BLOCK, D_MODEL), "dn": (FFN_HIDDEN, D_MODEL)}
SHARD_SHAPES = {"in": (D_MODEL, W_IN_BLOCK), "out": (W_OUT_BLOCK, D_MODEL), "gu": (FFN_BLOCK, D_MODEL), "dn": (W_DOWN_BLOCK, D_MODEL)}
ANY = pl.BlockSpec(memory_space=pl.ANY)


def _sds(shape, dtype):
    return jax.ShapeDtypeStruct(shape, dtype)


def _rows(width, tile=ROW_TILE):
    return pl.BlockSpec((tile, width), lambda i: (i, 0))


def _frows(width):
    return _rows(width, FFN_TILE)


def _cols(height, tile):
    return pl.BlockSpec((height, tile), lambda i: (0, i))


def _const(shape):
    return pl.BlockSpec(shape, lambda i: (0,) * len(shape))


def _const1(shape):
    return pl.BlockSpec(shape, lambda i: (0,) * len(shape), pipeline_mode=pl.Buffered(1))


HALO_ROWS = 16


def _halo_prev(width, tile=ROW_TILE):
    return pl.BlockSpec((HALO_ROWS, width), lambda i: (jnp.maximum(i * (tile // HALO_ROWS) - 1, 0), 0))


def _halo_next(width, tile=ROW_TILE):
    return pl.BlockSpec((HALO_ROWS, width), lambda i: (jnp.minimum((i + 1) * (tile // HALO_ROWS), SEQ // HALO_ROWS - 1), 0))


def _rms(x):
    r = lax.rsqrt(jnp.mean(x * x, axis=-1, keepdims=True) + RMS_EPS)
    return x * r, r


def _rms_bwd(dn, y, r):
    return r * (dn - y * jnp.mean(dn * y, axis=-1, keepdims=True))


def _dot(a, b):
    return jnp.dot(a, b, preferred_element_type=F32)


def _dot_nt(a, b):
    return lax.dot_general(a, b, (((1,), (1,)), ((), ())), preferred_element_type=F32)


def _dot_tn(a, b):
    return lax.dot_general(a, b, (((0,), (0,)), ((), ())), preferred_element_type=F32)


def _place():
    return lax.axis_index("x"), lax.axis_index("y"), lax.axis_index("c")


def _block_of(px, py, pc):
    return 4 * px + 2 * py + pc


def _weight_block(ref, kind, blk):
    if kind == "in":
        return ref.at[:, pl.ds(blk * W_IN_BLOCK, W_IN_BLOCK)]
    if kind == "out":
        return ref.at[pl.ds(blk * W_OUT_BLOCK, W_OUT_BLOCK), :]
    if kind == "gu":
        return ref.at[blk]
    return ref.at[pl.ds(blk * W_DOWN_BLOCK, W_DOWN_BLOCK), :]


def _dma_semaphores(n):
    return [pltpu.SemaphoreType.DMA((n, 7)), pltpu.SemaphoreType.DMA((n, 7)), pltpu.SemaphoreType.DMA((n,))]


class _Gather:
    def __init__(self, kinds, shards):
        self.kinds, self.operands = tuple(kinds), list(shards)
        self.tag = "gather_" + "_".join(kinds)
        self.out_shape = [_sds(FULL_SHAPES[k], BF16) for k in kinds]
        self.scratch = _dma_semaphores(len(kinds))

    def _parties(self):
        x, y, c = _place()
        return (x, y, c), (x, y, 1 - c), [(1 - x, y), (x, 1 - y), (1 - x, 1 - y)], c

    def _copy(self, outs, sems, w, k, block, to, src=None):
        dst = _weight_block(outs[w], self.kinds[w], _block_of(*block))
        return pltpu.make_async_remote_copy(src_ref=dst if src is None else src, dst_ref=dst, send_sem=sems[0].at[w, k],
                                            recv_sem=sems[1].at[w, k], device_id=to, device_id_type=MESH)

    def _own(self, srcs, outs, sems, w, me):
        return pltpu.make_async_copy(srcs[w], _weight_block(outs[w], self.kinds[w], _block_of(*me)), sems[2].at[w])

    def _first(self, srcs, outs, sems, w):
        me, sibling, chips, c = self._parties()
        return [self._copy(outs, sems, w, 0, me, sibling, src=srcs[w])] + [
            self._copy(outs, sems, w, 1 + j, me, (*chip, c), src=srcs[w]) for j, chip in enumerate(chips)]

    def start(self, srcs, outs, sems):
        me = self._parties()[0]
        for w in range(len(self.kinds)):
            self._own(srcs, outs, sems, w, me).start()
            for cp in self._first(srcs, outs, sems, w):
                cp.start()

    def forward(self, srcs, outs, sems):
        me, sibling, chips, c = self._parties()
        for j, chip in enumerate(chips):
            for w in range(len(self.kinds)):
                self._copy(outs, sems, w, 1 + j, (*chip, c), me).wait_recv()
                self._copy(outs, sems, w, 4 + j, (*chip, c), sibling).start()

    def finish(self, srcs, outs, sems):
        me, sibling, chips, c = self._parties()
        for w in range(len(self.kinds)):
            self._copy(outs, sems, w, 0, sibling, me).wait_recv()
            for j, chip in enumerate(chips):
                self._copy(outs, sems, w, 4 + j, (*chip, 1 - c), me).wait_recv()
        for w in range(len(self.kinds)):
            for cp in self._first(srcs, outs, sems, w):
                cp.wait_send()
            for j, chip in enumerate(chips):
                self._copy(outs, sems, w, 4 + j, (*chip, c), sibling).wait_send()
            self._own(srcs, outs, sems, w, me).wait()


def _peers(x, y, c):
    return [(x ^ a, y ^ b, c ^ e) for a in (0, 1) for b in (0, 1) for e in (0, 1) if (a, b, e) != (0, 0, 0)]


class _Scatter:
    def __init__(self, kinds, grads):
        self.kinds, self.operands = tuple(kinds), list(grads)
        self.tag = "scatter_" + "_".join(kinds)
        self.out_shape = [_sds((N_DEV,) + SHARD_SHAPES[k], BF16) for k in kinds]
        self.scratch = _dma_semaphores(len(kinds))

    def _copies(self, srcs, outs, sems):
        x, y, c = _place()
        me = _block_of(x, y, c)
        copies = []
        for w, kind in enumerate(self.kinds):
            copies.append(pltpu.make_async_copy(_weight_block(srcs[w], kind, me), outs[w].at[me], sems[2].at[w]))
            for k, peer in enumerate(_peers(x, y, c)):
                copies.append(pltpu.make_async_remote_copy(
                    src_ref=_weight_block(srcs[w], kind, _block_of(*peer)), dst_ref=outs[w].at[me],
                    send_sem=sems[0].at[w, k], recv_sem=sems[1].at[w, k], device_id=peer, device_id_type=MESH))
        return copies

    def start(self, srcs, outs, sems):
        for cp in self._copies(srcs, outs, sems):
            cp.start()

    def forward(self, srcs, outs, sems):
        pass

    def finish(self, srcs, outs, sems):
        for cp in self._copies(srcs, outs, sems):
            cp.wait()


N_CHIPS = N_DEV // 2


class _ScatterChips:
    def __init__(self, kinds, blocks):
        self.kinds, self.operands = tuple(kinds), list(blocks)
        self.tag = "scatter_chips_" + "_".join(kinds)
        self.out_shape = [_sds((N_CHIPS,) + SHARD_SHAPES[k], BF16) for k in kinds]
        self.scratch = _dma_semaphores(len(kinds))

    def _copies(self, srcs, outs, sems):
        x, y, c = _place()
        mine = 2 * x + y
        copies = []
        for w in range(len(self.kinds)):
            copies.append(pltpu.make_async_copy(srcs[w].at[mine], outs[w].at[mine], sems[2].at[w]))
            for j, (px, py) in enumerate([(1 - x, y), (x, 1 - y), (1 - x, 1 - y)]):
                copies.append(pltpu.make_async_remote_copy(
                    src_ref=srcs[w].at[2 * px + py], dst_ref=outs[w].at[mine], send_sem=sems[0].at[w, j], recv_sem=sems[1].at[w, j],
                    device_id=(px, py, c), device_id_type=MESH))
        return copies

    def start(self, srcs, outs, sems):
        for cp in self._copies(srcs, outs, sems):
            cp.start()

    def forward(self, srcs, outs, sems):
        pass

    def finish(self, srcs, outs, sems):
        for cp in self._copies(srcs, outs, sems):
            cp.wait()


def _in_hbm(a):
    return pltpu.with_memory_space_constraint(a, pltpu.HBM) if a.size * a.dtype.itemsize >= PINNED_BYTES else a


def _out_hbm(s):
    return pltpu.HBM(s.shape, s.dtype) if math.prod(s.shape) * jnp.dtype(s.dtype).itemsize >= PINNED_BYTES else s


def _call(body, args, *, grid, in_specs, out_specs, out_shape, scratch_shapes=(), vmem_mib=None, name, comm=None):
    kwargs = {} if vmem_mib is None else dict(compiler_params=pltpu.CompilerParams(vmem_limit_bytes=vmem_mib * MIB))
    in_specs, out_specs, out_shape, scratch_shapes = list(in_specs), list(out_specs), list(out_shape), list(scratch_shapes)
    args = [_in_hbm(a) for a in args]
    out_shape = [_out_hbm(s) for s in out_shape]
    if comm is None:
        res = pl.pallas_call(body, grid=grid, in_specs=in_specs, out_specs=out_specs, out_shape=out_shape,
                             scratch_shapes=scratch_shapes, name=name, **kwargs)(*args)
        return list(res), None
    n_in, n_out, n_scr = len(in_specs), len(out_specs), len(scratch_shapes)
    c_in, c_out = len(comm.operands), len(comm.out_shape)
    last = math.prod(grid) - 1

    def carried(*refs):
        cuts = [n_in, c_in, n_out, c_out, n_scr]
        parts, at = [], 0
        for n in cuts:
            parts.append(refs[at:at + n])
            at += n
        ins, c_ins, outs, c_outs, scr = parts
        sems = refs[at:]
        step = pl.program_id(0)
        for axis in range(1, len(grid)):
            step = step * grid[axis] + pl.program_id(axis)

        @pl.when(step == 0)
        def _():
            comm.start(c_ins, c_outs, sems)

        @pl.when(step == last)
        def _():
            comm.forward(c_ins, c_outs, sems)

        body(*ins, *outs, *scr)

        @pl.when(step == last)
        def _():
            comm.finish(c_ins, c_outs, sems)

    res = pl.pallas_call(carried, grid=grid, in_specs=in_specs + [ANY] * c_in, out_specs=out_specs + [ANY] * c_out,
                         out_shape=out_shape + [_out_hbm(s) for s in comm.out_shape], scratch_shapes=scratch_shapes + comm.scratch,
                         name=name + "_" + comm.tag, **kwargs)(*args, *[_in_hbm(a) for a in comm.operands])
    return list(res[:n_out]), list(res[n_out:])


class _GatherSmall:
    def __init__(self, v, tag):
        self.operands, self.tag = [v], tag
        self.out_shape = [_sds((N_DEV,) + v.shape, F32)]
        self.scratch = _dma_semaphores(1)

    def _copies(self, srcs, outs, sems):
        x, y, c = _place()
        me = _block_of(x, y, c)
        copies = [pltpu.make_async_copy(srcs[0], outs[0].at[me], sems[2].at[0])]
        for k, peer in enumerate(_peers(x, y, c)):
            copies.append(pltpu.make_async_remote_copy(src_ref=srcs[0], dst_ref=outs[0].at[me], send_sem=sems[0].at[0, k],
                                                       recv_sem=sems[1].at[0, k], device_id=peer, device_id_type=MESH))
        return copies

    def start(self, srcs, outs, sems):
        for cp in self._copies(srcs, outs, sems):
            cp.start()

    def forward(self, srcs, outs, sems):
        pass

    def finish(self, srcs, outs, sems):
        for cp in self._copies(srcs, outs, sems):
            cp.wait()


class _Together:
    def __init__(self, *comms):
        self.comms = comms
        self.operands = [a for cm in comms for a in cm.operands]
        self.out_shape = [s for cm in comms for s in cm.out_shape]
        self.scratch = [s for cm in comms for s in cm.scratch]
        self.tag = "_".join(cm.tag for cm in comms)
        self.kinds = tuple(kind for cm in comms for kind in getattr(cm, "kinds", ()))

    def _each(self, srcs, outs, sems):
        a = o = s = 0
        for cm in self.comms:
            na, no, ns = len(cm.operands), len(cm.out_shape), len(cm.scratch)
            yield cm, srcs[a:a + na], outs[o:o + no], sems[s:s + ns]
            a, o, s = a + na, o + no, s + ns

    def start(self, srcs, outs, sems):
        for cm, *refs in self._each(srcs, outs, sems):
            cm.start(*refs)

    def forward(self, srcs, outs, sems):
        for cm, *refs in self._each(srcs, outs, sems):
            cm.forward(*refs)

    def finish(self, srcs, outs, sems):
        for cm, *refs in self._each(srcs, outs, sems):
            cm.finish(*refs)


def _rope_tables(pos_col, freq_row, comm=None):
    def body(p_ref, f_ref, c_ref, sa_ref, sb_ref):
        ang = p_ref[...].astype(F32) * f_ref[...]
        lane = lax.broadcasted_iota(jnp.int32, ang.shape, 1) % HEAD_DIM
        cos, sin = jnp.cos(ang), jnp.sin(ang)
        c_ref[...] = jnp.where(lane < ROPE_DIM, cos, 1.0)
        sa_ref[...] = jnp.where(lane < ROPE_DIM // 2, -sin, 0.0)
        sb_ref[...] = jnp.where((lane >= ROPE_DIM // 2) & (lane < ROPE_DIM), sin, 0.0)

    return _call(body, (pos_col, freq_row), grid=(SEQ // ROW_TILE,), in_specs=[_rows(1), _const((1, LANES))],
                 out_specs=[_rows(LANES)] * 3, out_shape=[_sds((SEQ, LANES), F32)] * 3, name="rope_tables", comm=comm)


def _rotate(t, c, sa, sb):
    parts = []
    for g in range(ATTN_WIDTH // LANES):
        tg = t[:, g * LANES:(g + 1) * LANES]
        parts.append(tg * c + pltpu.roll(tg, LANES - 8, axis=1) * sa + pltpu.roll(tg, 8, axis=1) * sb)
    return jnp.concatenate(parts, axis=1)


def _rotate_transposed(dt, c, sa, sb):
    parts = []
    for g in range(ATTN_WIDTH // LANES):
        dg = dt[:, g * LANES:(g + 1) * LANES]
        parts.append(dg * c + pltpu.roll(dg * sa, 8, axis=1) + pltpu.roll(dg * sb, LANES - 8, axis=1))
    return jnp.concatenate(parts, axis=1)


def _inproj_fwd(x, g_pre, w_in, tc, tsa, tsb, comm=None):
    def body(x_ref, g_ref, w_ref, c_ref, sa_ref, sb_ref, q_ref, k_ref, v_ref, u_ref, b_ref, cc_ref):
        y, _ = _rms(x_ref[...])
        h = (y * g_ref[...]).astype(BF16)

        def proj(n):
            return _dot(h, w_ref[:, n * ATTN_WIDTH:(n + 1) * ATTN_WIDTH])

        c, sa, sb = c_ref[...], sa_ref[...], sb_ref[...]
        q_ref[...] = (_rotate(proj(0), c, sa, sb) * (HEAD_DIM ** -0.5)).astype(BF16)
        k_ref[...] = _rotate(proj(1), c, sa, sb).astype(BF16)
        v_ref[...] = proj(2).astype(BF16)
        u_ref[...] = proj(3).astype(BF16)
        b_ref[...] = proj(4).astype(BF16)
        cc_ref[...] = proj(5).astype(BF16)

    a = ATTN_WIDTH
    return _call(
        body, (x, g_pre, w_in, tc, tsa, tsb), grid=(SEQ // ROW_TILE,),
        in_specs=[_rows(D_MODEL), _const((1, D_MODEL)), _const1((D_MODEL, IN_PROJ_WIDTH)), _rows(LANES), _rows(LANES), _rows(LANES)],
        out_specs=[_rows(a)] * 6, out_shape=[_sds((SEQ, a), BF16)] * 6,
        vmem_mib=40, name="inproj_fwd", comm=comm)


def _head_masks():
    lane = lax.broadcasted_iota(jnp.int32, (1, LANES), 1)
    first = lane < HEAD_DIM
    return first, first.astype(F32), 1.0 - first.astype(F32)


def _perm_chunks(dil):
    length = SEQ // dil
    out = []
    for r in range(dil):
        for c0 in range(0, length, PERM_CHUNK):
            chunk = (r * length + c0) // PERM_CHUNK
            rows = pl.ds(c0, PERM_CHUNK) if dil == 1 else pl.ds(r + dil * c0, PERM_CHUNK, stride=dil)
            out.append((chunk, rows))
    return out


def _chunk(c, offset=0):
    return pl.ds(offset + c * PERM_CHUNK, PERM_CHUNK)


def _write_band_bias(bias_ref):
    rr = lax.broadcasted_iota(jnp.int32, (Q_BLOCK, K_WINDOW), 0)
    cc = lax.broadcasted_iota(jnp.int32, (Q_BLOCK, K_WINDOW), 1)
    band = (cc >= rr) & (cc - rr <= 2 * HALF_WINDOW)
    bias_ref[0] = jnp.where(band, 0.0, NEG_INF)
    bias_ref[1] = jnp.where(band & (cc >= HALF_WINDOW), 0.0, NEG_INF)
    bias_ref[2] = jnp.where(band & (cc < Q_BLOCK + HALF_WINDOW), 0.0, NEG_INF)


def _band_bias_index(m0, length):
    return jnp.where(m0 % length == 0, 1, 0) + jnp.where((m0 + Q_BLOCK) % length == 0, 2, 0)


def _zero_key_padding(bufs):
    pad = jnp.zeros((HALF_WINDOW, LANES), BF16)
    for buf in bufs:
        buf[pl.ds(0, HALF_WINDOW), :] = pad
        buf[pl.ds(SEQ + HALF_WINDOW, HALF_WINDOW), :] = pad


def _attn_fwd(q, k, v, comm=None):
    group = 8

    def body(q_ref, k_ref, v_ref, o_ref, lse_ref, q32, k32, v32, qa, qb, kp, vp, accp, mlp,
             acc0, acc1, acc2, ml0, ml1, ml2, bias_ref):
        first, mask_a, mask_b = _head_masks()
        low = lax.broadcasted_iota(jnp.int32, (1, LANES), 1) % HEAD_DIM < HEAD_DIM // 2
        _zero_key_padding((kp, vp))
        _write_band_bias(bias_ref)
        q32[...] = q_ref[...].astype(F32)
        k32[...] = k_ref[...].astype(F32)
        v32[...] = v_ref[...].astype(F32)
        natural = ((acc0, ml0), (acc1, ml1), (acc2, ml2))

        for branch, dil in enumerate(DILATIONS):
            length = SEQ // dil
            assert length >= 2 * Q_BLOCK
            chunks = _perm_chunks(dil)
            for c, rows in chunks:
                val = q32[rows, :]
                qa[_chunk(c), :] = (val * mask_a).astype(BF16)
                qb[_chunk(c), :] = (val * mask_b).astype(BF16)
                kp[_chunk(c, HALF_WINDOW), :] = k32[rows, :].astype(BF16)
                vp[_chunk(c, HALF_WINDOW), :] = v32[rows, :].astype(BF16)
            acc_dst, ml_dst = natural[branch] if dil == 1 else (accp, mlp)

            def blocks(i, carry, length=length, acc_dst=acc_dst, ml_dst=ml_dst):
                base = pl.multiple_of(i * (group * Q_BLOCK), group * Q_BLOCK)
                starts = [base + g * Q_BLOCK for g in range(group)]
                scores = [[_dot_nt(qx[pl.ds(m0, Q_BLOCK), :], kp[pl.ds(m0, K_WINDOW), :]) for qx in (qa, qb)] for m0 in starts]
                probs = []
                for m0, pair in zip(starts, scores):
                    bias = bias_ref[_band_bias_index(m0, length)]
                    stats, ps = [], []
                    for s in pair:
                        s = s + bias
                        m = jnp.max(s, axis=1, keepdims=True)
                        p = jnp.exp(s - m)
                        stats.append(jnp.where(low, m, jnp.sum(p, axis=1, keepdims=True)))
                        ps.append(p.astype(BF16))
                    ml_dst[pl.ds(m0, Q_BLOCK), :] = jnp.where(first, stats[0], stats[1])
                    probs.append(ps)
                for m0, ps in zip(starts, probs):
                    vw = vp[pl.ds(m0, K_WINDOW), :]
                    acc_dst[pl.ds(m0, Q_BLOCK), :] = jnp.where(first, _dot(ps[0], vw), _dot(ps[1], vw))
                return carry

            lax.fori_loop(0, SEQ // (group * Q_BLOCK), blocks, 0)

            if dil > 1:
                for c, rows in chunks:
                    natural[branch][0][rows, :] = accp[_chunk(c), :]
                    natural[branch][1][rows, :] = mlp[_chunk(c), :]

        for c in range(SEQ // PERM_CHUNK):
            packed = [ml[_chunk(c), :] for _, ml in natural]
            ms = [jnp.where(low, ml, pltpu.roll(ml, HEAD_DIM // 2, axis=1)) for ml in packed]
            ls = [jnp.where(low, pltpu.roll(ml, LANES - HEAD_DIM // 2, axis=1), ml) for ml in packed]
            m_all = jnp.maximum(jnp.maximum(ms[0], ms[1]), ms[2])
            es = [jnp.exp(m - m_all) for m in ms]
            z = ls[0] * es[0] + ls[1] * es[1] + ls[2] * es[2]
            num = natural[0][0][_chunk(c), :] * es[0] + natural[1][0][_chunk(c), :] * es[1] + natural[2][0][_chunk(c), :] * es[2]
            o_ref[_chunk(c), :] = num / z
            lse_ref[_chunk(c), :] = m_all + jnp.log(z)

    col = pl.BlockSpec((SEQ, LANES), lambda h: (0, h))
    padded = SEQ + 2 * HALF_WINDOW
    return _call(
        body, (q, k, v), grid=(ATTN_WIDTH // LANES,), in_specs=[col] * 3, out_specs=[col] * 2,
        out_shape=[_sds((SEQ, ATTN_WIDTH), F32)] * 2,
        scratch_shapes=[pltpu.VMEM((SEQ, LANES), F32)] * 3 + [pltpu.VMEM((SEQ, LANES), BF16)] * 2
        + [pltpu.VMEM((padded, LANES), BF16)] * 2 + [pltpu.VMEM((SEQ, LANES), F32)] * 8
        + [pltpu.VMEM((3, Q_BLOCK, K_WINDOW), F32)],
        vmem_mib=52, name="attn_fwd", comm=comm)


def _shifted(t, before, after, i):
    tile = t.shape[0]
    row = lax.broadcasted_iota(jnp.int32, (tile, 1), 0)
    before = jnp.where(i > 0, before, 0.0)
    after = jnp.where(i < SEQ // tile - 1, after, 0.0)
    return (jnp.where(row == 0, before, pltpu.roll(t, 1, axis=0)),
            jnp.where(row == tile - 1, after, pltpu.roll(t, tile - 1, axis=0)))


def _last_row(ref):
    return ref[HALO_ROWS - 1:HALO_ROWS, :].astype(F32)


def _first_row(ref):
    return ref[0:1, :].astype(F32)


def _conv_parts(u, c, u_prev, c_prev, u_next, c_next, cw, i):
    t = c * u
    t_prev, t_next = _shifted(t, _last_row(c_prev) * _last_row(u_prev), _first_row(c_next) * _first_row(u_next), i)
    s = cw[0:1, :] * t_prev + cw[1:2, :] * t + cw[2:3, :] * t_next
    return t, t_prev, t_next, s


def _mix_fwd(attn, u, b, c, conv_w, g_attn, g_conv, w_out, x, g_post, comm=None):
    def body(a_ref, u_ref, b_ref, c_ref, up_ref, cp_ref, un_ref, cn_ref, cw_ref, ga_ref, gc_ref, w_ref, x_ref, gp_ref,
             x1_ref, mg_ref, mix_ref):
        i = pl.program_id(0)
        _, _, _, s = _conv_parts(u_ref[...].astype(F32), c_ref[...].astype(F32), up_ref, cp_ref, un_ref, cn_ref, cw_ref[...], i)
        ya, _ = _rms(a_ref[...])
        yc, _ = _rms(b_ref[...].astype(F32) * s)
        merged = jnp.concatenate([ya * ga_ref[...], yc * gc_ref[...]], axis=1).astype(BF16)
        mix = _dot(merged, w_ref[...])
        ym, _ = _rms(mix)
        mg_ref[...] = merged.T
        mix_ref[...] = mix.astype(BF16)
        x1_ref[...] = x_ref[...] + ym * gp_ref[...]

    a = ATTN_WIDTH
    return _call(
        body, (attn, u, b, c, u, c, u, c, conv_w, g_attn, g_conv, w_out, x, g_post), grid=(SEQ // ROW_TILE,),
        in_specs=[_rows(a)] * 4 + [_halo_prev(a)] * 2 + [_halo_next(a)] * 2
        + [_const((3, a)), _const((1, a)), _const((1, a)), _const1((D_MODEL, D_MODEL)), _rows(D_MODEL), _const((1, D_MODEL))],
        out_specs=[_rows(D_MODEL), _cols(D_MODEL, ROW_TILE), _rows(D_MODEL)],
        out_shape=[_sds((SEQ, D_MODEL), F32), _sds((D_MODEL, SEQ), BF16), _sds((SEQ, D_MODEL), BF16)],
        vmem_mib=40, name="mix_fwd", comm=comm)


def _gu_spec():
    return pl.BlockSpec((N_DEV, FFN_TILE, FFN_BLOCK), lambda i: (0, i, 0))


def _ffn_fwd(x1, g_pre, w_gu, w_dn, g_post, comm=None, target=None):
    n_tiles = SEQ // ROW_TILE

    def body(*refs):
        if target is None:
            x_ref, g_ref, wgu_ref, wdn_ref, gp_ref, x2_ref, gu_ref, f_ref = refs
        else:
            x_ref, g_ref, wgu_ref, wdn_ref, gp_ref, t_ref, x2_ref, gu_ref, f_ref, loss_ref, acc = refs
        x1 = x_ref[...]
        y, _ = _rms(x1)
        h = (y * g_ref[...]).astype(BF16)
        f = jnp.zeros((ROW_TILE, D_MODEL), F32)

        def gate_up(j):
            return _dot_nt(h, wgu_ref[j]), _dot_nt(h, wgu_ref[j + N_DEV // 2])

        ahead = gate_up(0)
        for j in range(N_DEV // 2):
            gate, up = ahead
            if j + 1 < N_DEV // 2:
                ahead = gate_up(j + 1)
            gu_ref[j] = gate.astype(BF16)
            gu_ref[j + N_DEV // 2] = up.astype(BF16)
            act = (gate * jax.nn.sigmoid(gate) * up).astype(BF16)
            f = f + _dot(act, wdn_ref[pl.ds(j * FFN_BLOCK, FFN_BLOCK), :])
        yf, _ = _rms(f)
        f_ref[...] = f
        x2 = x1 + yf * gp_ref[...]
        if target is None:
            x2_ref[...] = x2
        else:
            i = pl.program_id(0)
            err = x2 - t_ref[...]
            x2_ref[...] = err * (1.0 / D_MODEL)

            @pl.when(i == 0)
            def _():
                acc[...] = jnp.zeros_like(acc)

            acc[...] += jnp.sum(err * err, axis=0, keepdims=True)

            @pl.when(i == n_tiles - 1)
            def _():
                loss_ref[...] = jnp.sum(acc[...], axis=1, keepdims=True) * (0.5 / D_MODEL)

    with_loss = target is not None
    return _call(
        body, (x1, g_pre, w_gu, w_dn, g_post) + ((target,) if with_loss else ()), grid=(n_tiles,),
        in_specs=[_rows(D_MODEL), _const((1, D_MODEL)), _const1((N_DEV, FFN_BLOCK, D_MODEL)), _const1((FFN_HIDDEN, D_MODEL)),
                  _const((1, D_MODEL))] + ([_rows(D_MODEL)] if with_loss else []),
        out_specs=[_rows(D_MODEL), pl.BlockSpec((N_DEV, ROW_TILE, FFN_BLOCK), lambda i: (0, i, 0)), _rows(D_MODEL)]
        + ([_const((1, 1))] if with_loss else []),
        out_shape=[_sds((SEQ, D_MODEL), F32), _sds((N_DEV, SEQ, FFN_BLOCK), BF16), _sds((SEQ, D_MODEL), F32)]
        + ([_sds((1, 1), F32)] if with_loss else []),
        scratch_shapes=[pltpu.VMEM((1, D_MODEL), F32)] if with_loss else [],
        vmem_mib=58, name="ffn_fwd_loss" if with_loss else "ffn_fwd", comm=comm)


def _accumulate(pairs, i):
    @pl.when(i == 0)
    def _():
        for ref, value in pairs:
            ref[...] = value

    @pl.when(i > 0)
    def _():
        for ref, value in pairs:
            ref[...] += value


def _colsum(v):
    return jnp.sum(v, axis=0, keepdims=True)


def _ffn_bwd(dx2, f, x1, gu, w_gu, w_dn, g_post, g_pre, comm=None):
    half = N_DEV // 2

    def body(dx2_ref, f_ref, x1_ref, gu_ref, wgu_ref, wdn_ref, gpost_ref, gpre_ref,
             dx1_ref, df_ref, act_ref, dgu_ref, h_ref, dgpost_ref, dgpre_ref):
        i = pl.program_id(0)
        dx2 = dx2_ref[...]
        yf, rf = _rms(f_ref[...])
        dg_post = _colsum(dx2 * yf)
        df = _rms_bwd(dx2 * gpost_ref[...], yf, rf).astype(BF16)
        df_ref[...] = df
        dh = jnp.zeros((FFN_TILE, D_MODEL), F32)

        def d_act(j):
            return _dot_nt(df, wdn_ref[pl.ds(j * FFN_BLOCK, FFN_BLOCK), :])

        ahead = d_act(0)
        for j in range(half):
            dact = ahead
            if j + 1 < half:
                ahead = d_act(j + 1)
            gate = gu_ref[j].astype(F32)
            up = gu_ref[j + half].astype(F32)
            sig = jax.nn.sigmoid(gate)
            silu = gate * sig
            act_ref[j] = (silu * up).astype(BF16)
            dgate = (dact * up * (sig * (1.0 + gate * (1.0 - sig)))).astype(BF16)
            dup = (dact * silu).astype(BF16)
            dgu_ref[j] = dgate
            dgu_ref[j + half] = dup
            dh = dh + _dot(dgate, wgu_ref[j]) + _dot(dup, wgu_ref[j + half])
        y1, r1 = _rms(x1_ref[...])
        h_ref[...] = (y1 * gpre_ref[...]).astype(BF16)
        dx1_ref[...] = dx2 + _rms_bwd(dh * gpre_ref[...], y1, r1)
        _accumulate([(dgpost_ref, dg_post), (dgpre_ref, _colsum(dh * y1))], i)

    act_spec = pl.BlockSpec((half, FFN_TILE, FFN_BLOCK), lambda i: (0, i, 0))
    return _call(
        body, (dx2, f, x1, gu, w_gu, w_dn, g_post, g_pre), grid=(SEQ // FFN_TILE,),
        in_specs=[_frows(D_MODEL)] * 3 + [_gu_spec(), _const1((N_DEV, FFN_BLOCK, D_MODEL)), _const1((FFN_HIDDEN, D_MODEL)),
                                          _const((1, D_MODEL)), _const((1, D_MODEL))],
        out_specs=[_frows(D_MODEL), _frows(D_MODEL), act_spec, _gu_spec(), _frows(D_MODEL), _const((1, D_MODEL)), _const((1, D_MODEL))],
        out_shape=[_sds((SEQ, D_MODEL), F32), _sds((SEQ, D_MODEL), BF16), _sds((half, SEQ, FFN_BLOCK), BF16),
                   _sds((N_DEV, SEQ, FFN_BLOCK), BF16), _sds((SEQ, D_MODEL), BF16), _sds((1, D_MODEL), F32), _sds((1, D_MODEL), F32)],
        vmem_mib=52, name="ffn_bwd", comm=comm)


def _wgrad(a_t, g, name):
    width = g.shape[1]

    def body(a_ref, g_ref, o_ref):
        o_ref[...] = _dot(a_ref[...], g_ref[...]).astype(BF16)

    return _call(body, (a_t, g), grid=(width // WGRAD_COLS,),
                 in_specs=[_const1((D_MODEL, SEQ)), pl.BlockSpec((SEQ, WGRAD_COLS), lambda j: (0, j))],
                 out_specs=[pl.BlockSpec((D_MODEL, WGRAD_COLS), lambda j: (0, j))], out_shape=[_sds((D_MODEL, width), BF16)],
                 vmem_mib=48, name=name)[0][0]


def _wgrad_paired(a, g, kind, to_owners=False, comm=None):
    shard = SHARD_SHAPES[kind]
    per_chip = 2 if kind == "gu" else 1
    n_compute = per_chip * N_CHIPS
    n_steps = n_compute + 1

    def chip_of(turn):
        if not to_owners:
            return turn
        return (2 * lax.axis_index("x") + lax.axis_index("y") + 1 + turn) % N_CHIPS

    def block(t):
        return jnp.minimum(t, n_compute - 1)

    if kind == "in":
        specs = [_const1((D_MODEL, SEQ)), pl.BlockSpec((SEQ, 2 * W_IN_BLOCK), lambda t: (0, chip_of(block(t))))]
    else:
        specs = [pl.BlockSpec((1, SEQ, FFN_BLOCK), lambda t: (block(t), 0, 0)), _const1((SEQ, D_MODEL))]

    def body(a_ref, g_ref, o_ref, *rest):
        if to_owners:
            landing, sendbuf, recvbuf, keepbuf, send_sem, recv_sem, donebuf, owner_send, owner_recv, own_sem = rest
        else:
            sendbuf, recvbuf, keepbuf, send_sem, recv_sem = rest
        t = pl.program_id(0)
        x, y, c = _place()

        def exchange(q):
            return pltpu.make_async_remote_copy(src_ref=sendbuf.at[q], dst_ref=recvbuf.at[q], send_sem=send_sem.at[q],
                                                recv_sem=recv_sem.at[q], device_id=(x, y, 1 - c), device_id_type=MESH)

        my_chip = 2 * x + y

        def to_owner(q):
            return pltpu.make_async_remote_copy(src_ref=donebuf.at[q], dst_ref=landing.at[my_chip], send_sem=owner_send.at[q],
                                                recv_sem=owner_recv.at[my_chip], device_id=(q // 2, q % 2, c), device_id_type=MESH)

        def keep_own():
            return pltpu.make_async_copy(donebuf.at[my_chip], landing.at[my_chip], own_sem)

        @pl.when((t >= per_chip) & (t % per_chip == 0))
        def _():
            q = chip_of(t // per_chip - 1)
            exchange(q).wait_recv()
            done = (keepbuf[...] + recvbuf[q].astype(F32)).astype(BF16)
            o_ref[0] = done
            if to_owners:
                donebuf[q] = done

                @pl.when(q == my_chip)
                def _():
                    keep_own().start()

                @pl.when(q != my_chip)
                def _():
                    to_owner(q).start()

        @pl.when(t < n_compute)
        def _():
            q = chip_of(t // per_chip)
            if kind == "gu":
                r = _dot_tn(a_ref[0], g_ref[...])

                @pl.when(t % 2 == c)
                def _():
                    keepbuf[...] = r

                @pl.when(t % 2 != c)
                def _():
                    sendbuf[q] = r.astype(BF16)
                    exchange(q).start()
            else:
                if kind == "dn":
                    r = _dot_tn(a_ref[0], g_ref[...])
                    lower, upper = r[:W_DOWN_BLOCK], r[W_DOWN_BLOCK:]
                else:
                    r = _dot(a_ref[...], g_ref[...])
                    lower, upper = r[:, :W_IN_BLOCK], r[:, W_IN_BLOCK:]
                keepbuf[...] = jnp.where(c == 0, lower, upper)
                sendbuf[q] = jnp.where(c == 0, upper, lower).astype(BF16)
                exchange(q).start()

        @pl.when(t == n_steps - 1)
        def _():
            for q in range(N_CHIPS):
                exchange(q).wait_send()
            if to_owners:
                keep_own().wait()
                for s in range(N_CHIPS):
                    @pl.when(s != my_chip)
                    def _(s=s):
                        pltpu.make_async_remote_copy(src_ref=donebuf.at[s], dst_ref=landing.at[s], send_sem=owner_send.at[s],
                                                     recv_sem=owner_recv.at[s], device_id=(x, y, c), device_id_type=MESH).wait()

    slots = (N_CHIPS,) + shard
    owners = int(to_owners)
    res, landed = _call(
        body, (a, g), grid=(n_steps,), in_specs=specs,
        out_specs=[pl.BlockSpec((1,) + shard, lambda t: (chip_of(jnp.clip(t // per_chip - 1, 0, N_CHIPS - 1)), 0, 0))] + [ANY] * owners,
        out_shape=[_sds(slots, BF16)] * (1 + owners),
        scratch_shapes=[pltpu.VMEM(slots, BF16), pltpu.VMEM(slots, BF16), pltpu.VMEM(shard, F32),
                        pltpu.SemaphoreType.DMA((N_CHIPS,)), pltpu.SemaphoreType.DMA((N_CHIPS,))]
        + [pltpu.VMEM(slots, BF16), pltpu.SemaphoreType.DMA((N_CHIPS,)), pltpu.SemaphoreType.DMA((N_CHIPS,)), pltpu.SemaphoreType.DMA] * owners,
        vmem_mib=52, name="wgrad_" + kind + "_paired" + "_to_owners" * owners, comm=comm)
    return res[owners] if comm is None else (res[owners], landed)


def _mix_bwd(dx1, mix, attn, u, b, c, conv_w, g_attn, g_conv, g_post, w_out):
    def body(dx1_ref, mix_ref, a_ref, u_ref, b_ref, c_ref, up_ref, cp_ref, un_ref, cn_ref, cw_ref, ga_ref, gc_ref, gp_ref, w_ref,
             dmix_ref, da_ref, ds_ref, db_ref, dgp_ref, dga_ref, dgc_ref):
        i = pl.program_id(0)
        dx1 = dx1_ref[...]
        ym, rm = _rms(mix_ref[...].astype(F32))
        dg_post = _colsum(dx1 * ym)
        dmix = _rms_bwd(dx1 * gp_ref[...], ym, rm).astype(BF16)
        dmix_ref[...] = dmix
        dmerged = _dot_nt(dmix, w_ref[...])
        dna, dnc = dmerged[:, :ATTN_WIDTH], dmerged[:, ATTN_WIDTH:]
        ya, ra = _rms(a_ref[...])
        da_ref[...] = _rms_bwd(dna * ga_ref[...], ya, ra)
        _, _, _, s = _conv_parts(u_ref[...].astype(F32), c_ref[...].astype(F32), up_ref, cp_ref, un_ref, cn_ref, cw_ref[...], i)
        gate_b = b_ref[...].astype(F32)
        yc, rc = _rms(gate_b * s)
        dy = _rms_bwd(dnc * gc_ref[...], yc, rc)
        db_ref[...] = (dy * s).astype(BF16)
        ds_ref[...] = (dy * gate_b).astype(BF16)
        _accumulate([(dgp_ref, dg_post), (dga_ref, _colsum(dna * ya)), (dgc_ref, _colsum(dnc * yc))], i)

    a = ATTN_WIDTH
    return _call(
        body, (dx1, mix, attn, u, b, c, u, c, u, c, conv_w, g_attn, g_conv, g_post, w_out), grid=(SEQ // ROW_TILE,),
        in_specs=[_rows(D_MODEL)] * 2 + [_rows(a)] * 4 + [_halo_prev(a)] * 2 + [_halo_next(a)] * 2
        + [_const((3, a)), _const((1, a)), _const((1, a)), _const((1, D_MODEL)), _const1((D_MODEL, D_MODEL))],
        out_specs=[_rows(D_MODEL)] + [_rows(a)] * 3 + [_const((1, D_MODEL)), _const((1, a)), _const((1, a))],
        out_shape=[_sds((SEQ, D_MODEL), BF16), _sds((SEQ, a), F32), _sds((SEQ, a), BF16), _sds((SEQ, a), BF16),
                   _sds((1, D_MODEL), F32), _sds((1, a), F32), _sds((1, a), F32)],
        vmem_mib=40, name="mix_bwd")[0]


def _attn_bwd(q, k, v, do, o, lse, comm=None):
    group = 8

    n_pairs = ATTN_WIDTH // LANES

    def body(q_hbm, k_hbm, v_hbm, do_hbm, o_hbm, lse_hbm, dq_ref, dk_ref, dv_ref,
             q_ref, k_ref, v_ref, do_ref, o_ref, lse_ref, in_sem,
             q32, k32, v32, qa, qb, doa, dob, kp, vp, lsep, dlp, dnat, dqp, dkp, dvp, bias_ref):
        pair = pl.program_id(0)
        sources = (q_hbm, k_hbm, v_hbm, do_hbm, o_hbm, lse_hbm)
        buffers = (q_ref, k_ref, v_ref, do_ref, o_ref, lse_ref)

        def fetch(i, p):
            return pltpu.make_async_copy(sources[i].at[:, pl.ds(pl.multiple_of(p * LANES, LANES), LANES)], buffers[i], in_sem.at[i])

        def prefetch(*which):
            @pl.when(pair + 1 < n_pairs)
            def _():
                for i in which:
                    fetch(i, pair + 1).start()

        @pl.when(pair == 0)
        def _():
            for i in range(len(sources)):
                fetch(i, 0).start()

        for i in range(len(sources)):
            fetch(i, pair).wait()

        first, mask_a, mask_b = _head_masks()
        _zero_key_padding((kp, vp))
        _write_band_bias(bias_ref)
        q32[...] = q_ref[...].astype(F32)
        k32[...] = k_ref[...].astype(F32)
        v32[...] = v_ref[...].astype(F32)
        prefetch(0, 1, 2)
        for c in range(SEQ // PERM_CHUNK):
            prod = do_ref[_chunk(c), :] * o_ref[_chunk(c), :]
            d_a = jnp.sum(prod * mask_a, axis=1, keepdims=True)
            d_b = jnp.sum(prod * mask_b, axis=1, keepdims=True)
            dnat[_chunk(c), :] = jnp.where(first, d_a, d_b)
        prefetch(4)

        for step, dil in enumerate(DILATIONS[1:] + DILATIONS[:1]):
            length = SEQ // dil
            assert length >= 2 * Q_BLOCK
            chunks = _perm_chunks(dil)
            for c, rows in chunks:
                val = q32[rows, :]
                qa[_chunk(c), :] = (val * mask_a).astype(BF16)
                qb[_chunk(c), :] = (val * mask_b).astype(BF16)
                val = do_ref[rows, :]
                doa[_chunk(c), :] = (val * mask_a).astype(BF16)
                dob[_chunk(c), :] = (val * mask_b).astype(BF16)
                kp[_chunk(c, HALF_WINDOW), :] = k32[rows, :].astype(BF16)
                vp[_chunk(c, HALF_WINDOW), :] = v32[rows, :].astype(BF16)
                lsep[_chunk(c), :] = lse_ref[rows, :]
                dlp[_chunk(c), :] = dnat[rows, :]
            if step == len(DILATIONS) - 1:
                prefetch(3, 5)
            zero = jnp.zeros((PERM_CHUNK, LANES), F32)
            for c in range(SEQ // PERM_CHUNK):
                dkp[_chunk(c), :] = zero
                dvp[_chunk(c), :] = zero
            dkp[pl.ds(SEQ, 2 * HALF_WINDOW), :] = zero[:2 * HALF_WINDOW]
            dvp[pl.ds(SEQ, 2 * HALF_WINDOW), :] = zero[:2 * HALF_WINDOW]

            heads = ((qa, doa, 0), (qb, dob, HEAD_DIM))

            def blocks(i, carry, length=length):
                base = pl.multiple_of(i * (group * Q_BLOCK), group * Q_BLOCK)
                starts = [base + g * Q_BLOCK for g in range(group)]
                raw = [[(_dot_nt(qx[pl.ds(m0, Q_BLOCK), :], kp[pl.ds(m0, K_WINDOW), :]),
                         _dot_nt(dox[pl.ds(m0, Q_BLOCK), :], vp[pl.ds(m0, K_WINDOW), :])) for qx, dox, _ in heads]
                       for m0 in starts]
                grads = []
                for m0, pair in zip(starts, raw):
                    bias = bias_ref[_band_bias_index(m0, length)]
                    lse_b, d_b = lsep[pl.ds(m0, Q_BLOCK), :], dlp[pl.ds(m0, Q_BLOCK), :]
                    out = []
                    for (s, dp), (_, _, col) in zip(pair, heads):
                        p = jnp.exp(s + bias - lse_b[:, col:col + 1])
                        out.append(((p * (dp - d_b[:, col:col + 1])).astype(BF16), p.astype(BF16)))
                    grads.append(out)
                for m0, out in zip(starts, grads):
                    qrows, krows = pl.ds(m0, Q_BLOCK), pl.ds(m0, K_WINDOW)
                    kw = kp[krows, :]
                    dk = jnp.zeros((K_WINDOW, LANES), F32)
                    dv = jnp.zeros((K_WINDOW, LANES), F32)
                    for (ds, p), (qx, dox, _) in zip(out, heads):
                        dk = dk + _dot_tn(ds, qx[qrows, :])
                        dv = dv + _dot_tn(p, dox[qrows, :])
                    dqp[qrows, :] = jnp.where(first, _dot(out[0][0], kw), _dot(out[1][0], kw)) * (HEAD_DIM ** -0.5)
                    dkp[krows, :] += dk
                    dvp[krows, :] += dv
                return carry

            lax.fori_loop(0, SEQ // (group * Q_BLOCK), blocks, 0)

            for c, rows in chunks:
                g_q, g_k, g_v = dqp[_chunk(c), :], dkp[_chunk(c, HALF_WINDOW), :], dvp[_chunk(c, HALF_WINDOW), :]
                if step == 0:
                    dq_ref[rows, :] = g_q
                    dk_ref[rows, :] = g_k
                    dv_ref[rows, :] = g_v
                else:
                    dq_ref[rows, :] = dq_ref[rows, :] + g_q
                    dk_ref[rows, :] = dk_ref[rows, :] + g_k
                    dv_ref[rows, :] = dv_ref[rows, :] + g_v

    col = pl.BlockSpec((SEQ, LANES), lambda h: (0, h))
    padded = SEQ + 2 * HALF_WINDOW
    return _call(
        body, (q, k, v, do, o, lse), grid=(n_pairs,), in_specs=[ANY] * 6, out_specs=[col] * 3,
        out_shape=[_sds((SEQ, ATTN_WIDTH), F32)] * 3,
        scratch_shapes=[pltpu.VMEM((SEQ, LANES), BF16)] * 3 + [pltpu.VMEM((SEQ, LANES), F32)] * 3 + [pltpu.SemaphoreType.DMA((6,))]
        + [pltpu.VMEM((SEQ, LANES), F32)] * 3 + [pltpu.VMEM((SEQ, LANES), BF16)] * 4
        + [pltpu.VMEM((padded, LANES), BF16)] * 2 + [pltpu.VMEM((SEQ, LANES), F32)] * 4 + [pltpu.VMEM((padded, LANES), F32)] * 2
        + [pltpu.VMEM((3, Q_BLOCK, K_WINDOW), F32)],
        vmem_mib=56, name="attn_bwd", comm=comm)


def _inproj_bwd(dq, dk, dv, ds, db, u, c, conv_w, tc, tsa, tsb, w_in, x, g_pre, dx1):
    def body(dq_ref, dk_ref, dv_ref, ds_ref, db_ref, u_ref, c_ref, dsp_ref, up_ref, cp_ref, dsn_ref, un_ref, cn_ref, cw_ref,
             tc_ref, tsa_ref, tsb_ref, w_ref, x_ref, g_ref, dx1_ref, dx_ref, dproj_ref, h_ref, dg_ref, dcw_ref):
        i = pl.program_id(0)
        cw = cw_ref[...]
        u, c = u_ref[...].astype(F32), c_ref[...].astype(F32)
        t, t_prev, t_next, _ = _conv_parts(u, c, up_ref, cp_ref, un_ref, cn_ref, cw, i)
        ds = ds_ref[...].astype(F32)
        ds_prev, ds_next = _shifted(ds, _last_row(dsp_ref), _first_row(dsn_ref), i)
        dt = cw[0:1, :] * ds_next + cw[1:2, :] * ds + cw[2:3, :] * ds_prev
        d_taps = jnp.concatenate([_colsum(ds * t_prev), _colsum(ds * t), _colsum(ds * t_next)], axis=0)
        tc_, tsa_, tsb_ = tc_ref[...], tsa_ref[...], tsb_ref[...]
        groups = ((2, lambda: dv_ref[...].astype(BF16)), (4, lambda: db_ref[...]),
                  (0, lambda: _rotate_transposed(dq_ref[...], tc_, tsa_, tsb_).astype(BF16)),
                  (1, lambda: _rotate_transposed(dk_ref[...], tc_, tsa_, tsb_).astype(BF16)),
                  (3, lambda: (dt * c).astype(BF16)), (5, lambda: (dt * u).astype(BF16)))
        dh = jnp.zeros((x_ref.shape[0], D_MODEL), F32)
        for n, make in groups:
            cols = pl.ds(n * a, a)
            part = make()
            dproj_ref[:, cols] = part
            dh = dh + _dot_nt(part, w_ref[:, cols])
        y, r = _rms(x_ref[...])
        h_ref[...] = (y * g_ref[...]).astype(BF16).T
        dx_ref[...] = dx1_ref[...] + _rms_bwd(dh * g_ref[...], y, r)
        _accumulate([(dg_ref, _colsum(dh * y)), (dcw_ref, d_taps)], i)

    a = ATTN_WIDTH
    tile = FFN_TILE

    def rows(width):
        return _rows(width, tile)

    return _call(
        body, (dq, dk, dv, ds, db, u, c, ds, u, c, ds, u, c, conv_w, tc, tsa, tsb, w_in, x, g_pre, dx1), grid=(SEQ // tile,),
        in_specs=[rows(a)] * 7 + [_halo_prev(a, tile)] * 3 + [_halo_next(a, tile)] * 3
        + [_const((3, a)), rows(LANES), rows(LANES), rows(LANES), _const1((D_MODEL, IN_PROJ_WIDTH)), rows(D_MODEL),
           _const((1, D_MODEL)), rows(D_MODEL)],
        out_specs=[rows(D_MODEL), rows(IN_PROJ_WIDTH), _cols(D_MODEL, tile), _const((1, D_MODEL)), _const((3, a))],
        out_shape=[_sds((SEQ, D_MODEL), F32), _sds((SEQ, IN_PROJ_WIDTH), BF16), _sds((D_MODEL, SEQ), BF16),
                   _sds((1, D_MODEL), F32), _sds((3, a), F32)],
        vmem_mib=48, name="inproj_bwd")


def _adamw_math(w, g, m, v):
    m = ADAM_B1 * m + (1.0 - ADAM_B1) * g
    v = ADAM_B2 * v + (1.0 - ADAM_B2) * (g * g)
    m_hat = m / (1.0 - ADAM_B1 ** ADAM_STEP)
    v_hat = v / (1.0 - ADAM_B2 ** ADAM_STEP)
    delta = -ADAM_LR * (m_hat / (jnp.sqrt(v_hat) + ADAM_EPS) + ADAM_WD * w)
    return delta, m, v


def _sum_parts(p_ref):
    g = p_ref[0].astype(F32)
    for k in range(1, p_ref.shape[0]):
        g = g + p_ref[k].astype(F32)
    return g


def _adamw_layers(parts, w, m, v, row_tile, name, comm=None):
    _, rows, cols = w.shape
    n_tiles = rows // row_tile

    def body(*refs):
        p_refs = refs[:DEPTH]
        w_ref, m_ref, v_ref, g_ref, d_ref, nm_ref, nv_ref = refs[DEPTH:]
        layer = pl.program_id(0)
        for l, p_ref in enumerate(p_refs):
            @pl.when(layer == l)
            def _(p_ref=p_ref):
                g = _sum_parts(p_ref)
                g_ref[0] = g
                d_ref[0], nm_ref[0], nv_ref[0] = _adamw_math(w_ref[0], g, m_ref[0], v_ref[0])

    def part_spec(l):
        return pl.BlockSpec((parts[l].shape[0], row_tile, cols),
                            lambda layer, i: (0, jnp.where(layer == l, i, jnp.where(layer < l, 0, n_tiles - 1)), 0))

    tile = pl.BlockSpec((1, row_tile, cols), lambda layer, i: (layer, i, 0))
    return _call(body, (*parts, w, m, v), grid=(DEPTH, n_tiles), in_specs=[part_spec(l) for l in range(DEPTH)] + [tile] * 3,
                 out_specs=[tile] * 4, out_shape=[_sds(w.shape, F32)] * 4, name=name, comm=comm)


def _adamw_small(parts, w, m, v):
    n = len(GAIN_NAMES)
    a = ATTN_WIDTH
    where = ((0, 0), (1, 0), (2, 0), (3, 0), (4, 0), (4, a))

    def body(*refs):
        p_ref = refs[0]
        w_refs, m_refs, v_refs = refs[1:1 + n], refs[1 + n:1 + 2 * n], refs[1 + 2 * n:1 + 3 * n]
        outs = refs[1 + 3 * n:1 + 7 * n]
        taps_ref, loss_ref, sum_ref = refs[1 + 7 * n:]
        sum_ref[...] = _sum_parts(p_ref)
        for k, (row, col) in enumerate(where):
            width = w_refs[k].shape[1]
            g = jnp.concatenate([sum_ref[pl.ds(l * SMALL_ROWS + row, 1), pl.ds(col, width)] for l in range(DEPTH)], axis=0)
            g_ref, d_ref, nm_ref, nv_ref = outs[4 * k:4 * k + 4]
            g_ref[...] = g
            d_ref[...], nm_ref[...], nv_ref[...] = _adamw_math(w_refs[k][...], g, m_refs[k][...], v_refs[k][...])
        for l in range(DEPTH):
            for tap, (row, col) in enumerate(((5, 0), (5, a), (6, 0))):
                taps_ref[l, tap:tap + 1, :] = sum_ref[l * SMALL_ROWS + row:l * SMALL_ROWS + row + 1, col:col + a]
        loss_ref[...] = sum_ref[SMALL_ROWS - 1:SMALL_ROWS, 0:1]

    out_shape = [_sds(x.shape, F32) for x in w for _ in range(4)] + [_sds((DEPTH, 3, a), F32), _sds((1, 1), F32)]
    res = pl.pallas_call(body, out_shape=out_shape, scratch_shapes=[pltpu.VMEM(parts.shape[1:], F32)],
                         name="adamw_small")(parts, *w, *m, *v)
    return [res[4 * k:4 * k + 4] for k in range(n)], res[4 * n], res[4 * n + 1]


def _adamw_plain(g, w, m, v, name):
    def body(g_ref, w_ref, m_ref, v_ref, d_ref, nm_ref, nv_ref):
        d_ref[...], nm_ref[...], nv_ref[...] = _adamw_math(w_ref[...], g_ref[...], m_ref[...], v_ref[...])

    return pl.pallas_call(body, out_shape=[_sds(w.shape, F32)] * 3, name=name)(g, w, m, v)


SMALL_ROWS = 8
GAIN_NAMES = ("pre_mix_norm", "post_mix_norm", "pre_ffn_norm", "post_ffn_norm", "attn_out_norm", "conv_out_norm")


def _pack_small(pre_mix, post_mix, pre_ffn, post_ffn, attn_out, conv_out, taps):
    zeros = jnp.zeros((1, ATTN_WIDTH), F32)
    return jnp.concatenate([
        pre_mix, post_mix, pre_ffn, post_ffn, jnp.concatenate([attn_out, conv_out], axis=1),
        jnp.concatenate([taps[0:1], taps[1:2]], axis=1), jnp.concatenate([taps[2:3], zeros], axis=1),
        jnp.zeros((1, D_MODEL), F32)], axis=0)


def _rope(positions, comm=None):
    inv_freq = ROPE_THETA ** (-jnp.arange(0, ROPE_DIM, 2, dtype=F32) / ROPE_DIM)
    per_head = jnp.concatenate([inv_freq, inv_freq, jnp.zeros((HEAD_DIM - ROPE_DIM,), F32)])
    freq_row = jnp.tile(per_head, LANES // HEAD_DIM).reshape(1, LANES)
    return _rope_tables(positions.reshape(SEQ, 1), freq_row, comm)


def _layer_forward(h, gains, taps, tables, w, inproj_comm=None, attn_comm=None, mix_comm=None, ffn_comm=None, target=None):
    def arrived(comm, landed):
        return w if comm is None else {**w, **dict(zip(comm.kinds, landed))}

    (q, k, v, u, b, c), landed = _inproj_fwd(h, gains["pre_mix_norm"], w["in"], *tables, comm=inproj_comm)
    w = arrived(inproj_comm, landed)
    (attn, lse), landed = _attn_fwd(q, k, v, comm=attn_comm)
    w = arrived(attn_comm, landed)
    (x1, merged, mix), landed = _mix_fwd(attn, u, b, c, taps, gains["attn_out_norm"], gains["conv_out_norm"], w["out"], h,
                                         gains["post_mix_norm"], comm=mix_comm)
    w = arrived(mix_comm, landed)
    (x2, gu, f, *loss), landed_next = _ffn_fwd(x1, gains["pre_ffn_norm"], w["gu"], w["dn"], gains["post_ffn_norm"], comm=ffn_comm,
                                               target=target)
    out = x2 if target is None else (x2, loss[0])
    return out, (h, q, k, v, u, b, c, attn, lse, merged, mix, x1, gu, f), w, landed_next


class _GradientExchange:
    def ffn_grads(self, act, df, h2, dgu):
        return dict(dn=_wgrad_paired(act, df, "dn"), gu=_wgrad_paired(dgu, h2, "gu"))

    def w_out_grad(self, merged_t, dmix):
        return _wgrad(merged_t, dmix, "wgrad_out")

    def attention_passenger(self, ffn, g_out):
        return _Together(_ScatterChips(("gu", "dn"), (ffn["gu"], ffn["dn"])), _Scatter(("out",), (g_out,)))

    def w_in_grad(self, h_t, dproj, small_rows):
        if small_rows is None:
            return _wgrad(h_t, dproj, "wgrad_in"), None
        parts, landed = _wgrad_paired(h_t, dproj, "in", to_owners=True, comm=_GatherSmall(small_rows, "small"))
        return parts, landed[0]


def _layer_backward(dx, saved, gains, taps, tables, w, ffn_comm, exchange, last):
    x0, q, k, v, u, b, c, attn, lse, merged, mix, x1, gu, f = saved
    (dx1, df, act, dgu, h2, dg_post_ffn, dg_pre_ffn), landed_prev = _ffn_bwd(
        dx, f, x1, gu, w["gu"], w["dn"], gains["post_ffn_norm"], gains["pre_ffn_norm"], comm=ffn_comm)
    ffn = exchange.ffn_grads(act, df, h2, dgu)
    dmix, dattn, ds, db, dg_post_mix, dg_attn, dg_conv = _mix_bwd(
        dx1, mix, attn, u, b, c, taps, gains["attn_out_norm"], gains["conv_out_norm"], gains["post_mix_norm"], w["out"])
    g_out = exchange.w_out_grad(merged, dmix)
    attn_comm = exchange.attention_passenger(ffn, g_out)
    (dq, dk, dv), landed = _attn_bwd(q, k, v, dattn, attn, lse, comm=attn_comm)
    dx0, dproj, h1, dg_pre_mix, dtaps = _inproj_bwd(
        dq, dk, dv, ds, db, u, c, taps, *tables, w["in"], x0, gains["pre_mix_norm"], dx1)[0]
    small = _pack_small(dg_pre_mix, dg_post_mix, dg_pre_ffn, dg_post_ffn, dg_attn, dg_conv, dtaps)
    small_rows = None
    if last is not None:
        small_above, local_loss = last
        small_rows = lax.dynamic_update_slice(jnp.concatenate([small, small_above], axis=0), local_loss, (SMALL_ROWS - 1, 0))
    g_in, gathered = exchange.w_in_grad(h1, dproj, small_rows)
    rest = dict(zip(attn_comm.kinds, landed)) if attn_comm is not None else dict(ffn, out=g_out)
    return dx0, g_in, small, landed_prev, rest, gathered


def kernel(x, positions, pre_mix_norm, w_in, conv_w, attn_out_norm, conv_out_norm, w_out, post_mix_norm, pre_ffn_norm, w_gate_up, w_down, post_ffn_norm, loss_target, m_pre_mix_norm, m_w_in, m_conv_w, m_attn_out_norm, m_conv_out_norm, m_w_out, m_post_mix_norm, m_pre_ffn_norm, m_w_gate_up, m_w_down, m_post_ffn_norm, v_pre_mix_norm, v_w_in, v_conv_w, v_attn_out_norm, v_conv_out_norm, v_w_out, v_post_mix_norm, v_pre_ffn_norm, v_w_gate_up, v_w_down, v_post_ffn_norm):
    mx, my, mc = _place()
    me = _block_of(mx, my, mc)
    conv_channels = conv_w.shape[-1]

    def hidden_major(a):
        return jnp.swapaxes(a, 1, 2)

    big_w = dict(zip(WEIGHT_KINDS, (w_in, w_out, hidden_major(w_gate_up), w_down)))
    big_m = dict(zip(WEIGHT_KINDS, (m_w_in, m_w_out, hidden_major(m_w_gate_up), m_w_down)))
    big_v = dict(zip(WEIGHT_KINDS, (v_w_in, v_w_out, hidden_major(v_w_gate_up), v_w_down)))
    shards = {kind: big_w[kind].astype(BF16) for kind in WEIGHT_KINDS}
    all_gains = dict(pre_mix_norm=pre_mix_norm, attn_out_norm=attn_out_norm, conv_out_norm=conv_out_norm, post_mix_norm=post_mix_norm,
                     pre_ffn_norm=pre_ffn_norm, post_ffn_norm=post_ffn_norm)

    def gather(kinds, l):
        return _Gather(kinds, [shards[kind][l] for kind in kinds])

    def gains(l):
        return {name: g[l:l + 1] for name, g in all_gains.items()}

    early = ("in", "out", "dn")
    taps_flat = jnp.pad(conv_w.reshape(-1), (0, 8 * LANES - conv_w.size)).reshape(8, LANES)
    tables, (first, taps_all) = _rope(positions, _Together(gather(early[:1], 0), _GatherSmall(taps_flat, "taps")))
    taps_all = taps_all.reshape(N_DEV, 8 * LANES)[:, :conv_w.size]
    conv_w_full = taps_all.reshape(N_DEV, DEPTH, 3, conv_channels).transpose(1, 2, 0, 3).reshape(DEPTH, 3, ATTN_WIDTH)
    weights = [dict(zip(early[:1], [first]))] + [None] * (DEPTH - 1)
    saved = [None] * DEPTH
    h = x[0]
    for l in range(DEPTH):
        inproj_comm = gather(("out",), 0) if l == 0 else None
        mix_comm = gather(("dn",), 0) if l == 0 else None
        ffn_comm = gather(early, l + 1) if l + 1 < DEPTH else None
        target = loss_target[0] if l + 1 == DEPTH else None
        h, saved[l], weights[l], landed = _layer_forward(h, gains(l), conv_w_full[l], tables, weights[l], inproj_comm,
                                                         gather(("gu",), l), mix_comm, ffn_comm, target)
        if ffn_comm is not None:
            weights[l + 1] = dict(zip(ffn_comm.kinds, landed))
    dx, local_loss = h

    parts = {kind: [None] * DEPTH for kind in WEIGHT_KINDS}
    small_grads = [None] * DEPTH
    g_in_above = None
    exchange = _GradientExchange()
    for l in reversed(range(DEPTH)):
        ffn_comm = _Scatter(("in",), (g_in_above,)) if g_in_above is not None else None
        last = (jnp.concatenate(small_grads[1:], axis=0), local_loss) if l == 0 else None
        dx, g_in_above, small_grads[l], landed, rest, gathered = _layer_backward(
            dx, saved[l], gains(l), conv_w_full[l], tables, weights[l], ffn_comm, exchange, last)
        if ffn_comm is not None:
            parts["in"][l + 1] = landed[0]
        for kind, part in rest.items():
            parts[kind][l] = part

    parts["in"][0] = g_in_above
    tiles = {"in": 256, "out": 128, "gu": 176, "dn": 176}
    big = {kind: _adamw_layers(parts[kind], big_w[kind], big_m[kind], big_v[kind], tiles[kind], "adamw_" + kind)[0]
           for kind in WEIGHT_KINDS}

    small_m = (m_pre_mix_norm, m_post_mix_norm, m_pre_ffn_norm, m_post_ffn_norm, m_attn_out_norm, m_conv_out_norm)
    small_v = (v_pre_mix_norm, v_post_mix_norm, v_pre_ffn_norm, v_post_ffn_norm, v_attn_out_norm, v_conv_out_norm)
    per_gain, taps_grad_full, loss = _adamw_small(gathered, [all_gains[name] for name in GAIN_NAMES], small_m, small_v)
    small = [{name: per_gain[k][o] for k, name in enumerate(GAIN_NAMES)} for o in range(4)]
    loss = loss[0, 0]
    taps_grad = lax.dynamic_slice_in_dim(taps_grad_full, me * conv_channels, conv_channels, axis=2)

    def flat(a):
        return a.reshape(DEPTH * 3, conv_channels)

    td, tnm, tnv = _adamw_plain(flat(taps_grad), flat(conv_w), flat(m_conv_w), flat(v_conv_w), "adamw_taps")
    taps = [taps_grad] + [a.reshape(conv_w.shape) for a in (td, tnm, tnv)]

    def leaves(o):
        s = small[o]
        return (s["pre_mix_norm"], big["in"][o], taps[o], s["attn_out_norm"], s["conv_out_norm"], big["out"][o], s["post_mix_norm"],
                s["pre_ffn_norm"], hidden_major(big["gu"][o]), big["dn"][o], s["post_ffn_norm"])

    return (loss, dx[None], *leaves(0), *leaves(1), *leaves(2), *leaves(3))
```

```python
import math

import jax
import jax.numpy as jnp
from jax import lax
from jax.experimental import pallas as pl
from jax.experimental.pallas import tpu as pltpu

F32 = jnp.float32
BF16 = jnp.bfloat16
MESH = pl.DeviceIdType.MESH

SEQ = 4096
D_MODEL = 1024
DEPTH = 4
N_DEV = 8
ATTN_WIDTH = 512
IN_PROJ_WIDTH = 3072
FFN_HIDDEN = 2816
FFN_BLOCK = 2 * FFN_HIDDEN // N_DEV
W_IN_BLOCK = IN_PROJ_WIDTH // N_DEV
W_OUT_BLOCK = D_MODEL // N_DEV
W_DOWN_BLOCK = FFN_HIDDEN // N_DEV
HEAD_DIM = 64
ROPE_DIM = 16
ROPE_THETA = 500000.0
DILATIONS = (1, 4, 16)
HALF_WINDOW = 64
RMS_EPS = 1e-6
NEG_INF = -1e30
LANES = 128
Q_BLOCK = 128
K_WINDOW = Q_BLOCK + 2 * HALF_WINDOW
PERM_CHUNK = 256
ROW_TILE = 512
FFN_TILE = 256
WGRAD_COLS = 512
ADAM_LR, ADAM_B1, ADAM_B2, ADAM_EPS, ADAM_WD, ADAM_STEP = 0.001, 0.9, 0.999, 1e-08, 0.01, 10
MIB = 1024 * 1024
PINNED_BYTES = 256 * 1024

WEIGHT_KINDS = ("in", "out", "gu", "dn")
FULL_SHAPES = {"in": (D_MODEL, IN_PROJ_WIDTH), "out": (D_MODEL, D_MODEL), "gu": (N_DEV, FFN_BLOCK, D_MODEL), "dn": (FFN_HIDDEN, D_MODEL)}
SHARD_SHAPES = {"in": (D_MODEL, W_IN_BLOCK), "out": (W_OUT_BLOCK, D_MODEL), "gu": (FFN_BLOCK, D_MODEL), "dn": (W_DOWN_BLOCK, D_MODEL)}
ANY = pl.BlockSpec(memory_space=pl.ANY)


def _sds(shape, dtype):
    return jax.ShapeDtypeStruct(shape, dtype)


def _rows(width, tile=ROW_TILE):
    return pl.BlockSpec((tile, width), lambda i: (i, 0))


def _frows(width):
    return _rows(width, FFN_TILE)


def _cols(height, tile):
    return pl.BlockSpec((height, tile), lambda i: (0, i))


def _const(shape):
    return pl.BlockSpec(shape, lambda i: (0,) * len(shape))


def _const1(shape):
    return pl.BlockSpec(shape, lambda i: (0,) * len(shape), pipeline_mode=pl.Buffered(1))


HALO_ROWS = 16


def _halo_prev(width, tile=ROW_TILE):
    return pl.BlockSpec((HALO_ROWS, width), lambda i: (jnp.maximum(i * (tile // HALO_ROWS) - 1, 0), 0))


def _halo_next(width, tile=ROW_TILE):
    return pl.BlockSpec((HALO_ROWS, width), lambda i: (jnp.minimum((i + 1) * (tile // HALO_ROWS), SEQ // HALO_ROWS - 1), 0))


def _rms(x):
    r = lax.rsqrt(jnp.mean(x * x, axis=-1, keepdims=True) + RMS_EPS)
    return x * r, r


def _rms_bwd(dn, y, r):
    return r * (dn - y * jnp.mean(dn * y, axis=-1, keepdims=True))


def _dot(a, b):
    return jnp.dot(a, b, preferred_element_type=F32)


def _dot_nt(a, b):
    return lax.dot_general(a, b, (((1,), (1,)), ((), ())), preferred_element_type=F32)


def _dot_tn(a, b):
    return lax.dot_general(a, b, (((0,), (0,)), ((), ())), preferred_element_type=F32)


def _place():
    return lax.axis_index("x"), lax.axis_index("y"), lax.axis_index("c")


def _block_of(px, py, pc):
    return 4 * px + 2 * py + pc


def _weight_block(ref, kind, blk):
    if kind == "in":
        return ref.at[:, pl.ds(blk * W_IN_BLOCK, W_IN_BLOCK)]
    if kind == "out":
        return ref.at[pl.ds(blk * W_OUT_BLOCK, W_OUT_BLOCK), :]
    if kind == "gu":
        return ref.at[blk]
    return ref.at[pl.ds(blk * W_DOWN_BLOCK, W_DOWN_BLOCK), :]


def _dma_semaphores(n):
    return [pltpu.SemaphoreType.DMA((n, 7)), pltpu.SemaphoreType.DMA((n, 7)), pltpu.SemaphoreType.DMA((n,))]


class _Gather:
    def __init__(self, kinds, shards):
        self.kinds, self.operands = tuple(kinds), list(shards)
        self.tag = "gather_" + "_".join(kinds)
        self.out_shape = [_sds(FULL_SHAPES[k], BF16) for k in kinds]
        self.scratch = _dma_semaphores(len(kinds))

    def _parties(self):
        x, y, c = _place()
        return (x, y, c), (x, y, 1 - c), [(1 - x, y), (x, 1 - y), (1 - x, 1 - y)], c

    def _copy(self, outs, sems, w, k, block, to, src=None):
        dst = _weight_block(outs[w], self.kinds[w], _block_of(*block))
        return pltpu.make_async_remote_copy(src_ref=dst if src is None else src, dst_ref=dst, send_sem=sems[0].at[w, k],
                                            recv_sem=sems[1].at[w, k], device_id=to, device_id_type=MESH)

    def _own(self, srcs, outs, sems, w, me):
        return pltpu.make_async_copy(srcs[w], _weight_block(outs[w], self.kinds[w], _block_of(*me)), sems[2].at[w])

    def _first(self, srcs, outs, sems, w):
        me, sibling, chips, c = self._parties()
        return [self._copy(outs, sems, w, 0, me, sibling, src=srcs[w])] + [
            self._copy(outs, sems, w, 1 + j, me, (*chip, c), src=srcs[w]) for j, chip in enumerate(chips)]

    def start(self, srcs, outs, sems):
        me = self._parties()[0]
        for w in range(len(self.kinds)):
            self._own(srcs, outs, sems, w, me).start()
            for cp in self._first(srcs, outs, sems, w):
                cp.start()

    def forward(self, srcs, outs, sems):
        me, sibling, chips, c = self._parties()
        for j, chip in enumerate(chips):
            for w in range(len(self.kinds)):
                self._copy(outs, sems, w, 1 + j, (*chip, c), me).wait_recv()
                self._copy(outs, sems, w, 4 + j, (*chip, c), sibling).start()

    def finish(self, srcs, outs, sems):
        me, sibling, chips, c = self._parties()
        for w in range(len(self.kinds)):
            self._copy(outs, sems, w, 0, sibling, me).wait_recv()
            for j, chip in enumerate(chips):
                self._copy(outs, sems, w, 4 + j, (*chip, 1 - c), me).wait_recv()
        for w in range(len(self.kinds)):
            for cp in self._first(srcs, outs, sems, w):
                cp.wait_send()
            for j, chip in enumerate(chips):
                self._copy(outs, sems, w, 4 + j, (*chip, c), sibling).wait_send()
            self._own(srcs, outs, sems, w, me).wait()


def _peers(x, y, c):
    return [(x ^ a, y ^ b, c ^ e) for a in (0, 1) for b in (0, 1) for e in (0, 1) if (a, b, e) != (0, 0, 0)]


class _Scatter:
    def __init__(self, kinds, grads):
        self.kinds, self.operands = tuple(kinds), list(grads)
        self.tag = "scatter_" + "_".join(kinds)
        self.out_shape = [_sds((N_DEV,) + SHARD_SHAPES[k], BF16) for k in kinds]
        self.scratch = _dma_semaphores(len(kinds))

    def _copies(self, srcs, outs, sems):
        x, y, c = _place()
        me = _block_of(x, y, c)
        copies = []
        for w, kind in enumerate(self.kinds):
            copies.append(pltpu.make_async_copy(_weight_block(srcs[w], kind, me), outs[w].at[me], sems[2].at[w]))
            for k, peer in enumerate(_peers(x, y, c)):
                copies.append(pltpu.make_async_remote_copy(
                    src_ref=_weight_block(srcs[w], kind, _block_of(*peer)), dst_ref=outs[w].at[me],
                    send_sem=sems[0].at[w, k], recv_sem=sems[1].at[w, k], device_id=peer, device_id_type=MESH))
        return copies

    def start(self, srcs, outs, sems):
        for cp in self._copies(srcs, outs, sems):
            cp.start()

    def forward(self, srcs, outs, sems):
        pass

    def finish(self, srcs, outs, sems):
        for cp in self._copies(srcs, outs, sems):
            cp.wait()


N_CHIPS = N_DEV // 2


class _ScatterChips:
    def __init__(self, kinds, blocks):
        self.kinds, self.operands = tuple(kinds), list(blocks)
        self.tag = "scatter_chips_" + "_".join(kinds)
        self.out_shape = [_sds((N_CHIPS,) + SHARD_SHAPES[k], BF16) for k in kinds]
        self.scratch = _dma_semaphores(len(kinds))

    def _copies(self, srcs, outs, sems):
        x, y, c = _place()
        mine = 2 * x + y
        copies = []
        for w in range(len(self.kinds)):
            copies.append(pltpu.make_async_copy(srcs[w].at[mine], outs[w].at[mine], sems[2].at[w]))
            for j, (px, py) in enumerate([(1 - x, y), (x, 1 - y), (1 - x, 1 - y)]):
                copies.append(pltpu.make_async_remote_copy(
                    src_ref=srcs[w].at[2 * px + py], dst_ref=outs[w].at[mine], send_sem=sems[0].at[w, j], recv_sem=sems[1].at[w, j],
                    device_id=(px, py, c), device_id_type=MESH))
        return copies

    def start(self, srcs, outs, sems):
        for cp in self._copies(srcs, outs, sems):
            cp.start()

    def forward(self, srcs, outs, sems):
        pass

    def finish(self, srcs, outs, sems):
        for cp in self._copies(srcs, outs, sems):
            cp.wait()


def _in_hbm(a):
    return pltpu.with_memory_space_constraint(a, pltpu.HBM) if a.size * a.dtype.itemsize >= PINNED_BYTES else a


def _out_hbm(s):
    return pltpu.HBM(s.shape, s.dtype) if math.prod(s.shape) * jnp.dtype(s.dtype).itemsize >= PINNED_BYTES else s


def _call(body, args, *, grid, in_specs, out_specs, out_shape, scratch_shapes=(), vmem_mib=None, name, comm=None):
    kwargs = {} if vmem_mib is None else dict(compiler_params=pltpu.CompilerParams(vmem_limit_bytes=vmem_mib * MIB))
    in_specs, out_specs, out_shape, scratch_shapes = list(in_specs), list(out_specs), list(out_shape), list(scratch_shapes)
    args = [_in_hbm(a) for a in args]
    out_shape = [_out_hbm(s) for s in out_shape]
    if comm is None:
        res = pl.pallas_call(body, grid=grid, in_specs=in_specs, out_specs=out_specs, out_shape=out_shape,
                             scratch_shapes=scratch_shapes, name=name, **kwargs)(*args)
        return list(res), None
    n_in, n_out, n_scr = len(in_specs), len(out_specs), len(scratch_shapes)
    c_in, c_out = len(comm.operands), len(comm.out_shape)
    last = math.prod(grid) - 1

    def carried(*refs):
        cuts = [n_in, c_in, n_out, c_out, n_scr]
        parts, at = [], 0
        for n in cuts:
            parts.append(refs[at:at + n])
            at += n
        ins, c_ins, outs, c_outs, scr = parts
        sems = refs[at:]
        step = pl.program_id(0)
        for axis in range(1, len(grid)):
            step = step * grid[axis] + pl.program_id(axis)

        @pl.when(step == 0)
        def _():
            comm.start(c_ins, c_outs, sems)

        @pl.when(step == last)
        def _():
            comm.forward(c_ins, c_outs, sems)

        body(*ins, *outs, *scr)

        @pl.when(step == last)
        def _():
            comm.finish(c_ins, c_outs, sems)

    res = pl.pallas_call(carried, grid=grid, in_specs=in_specs + [ANY] * c_in, out_specs=out_specs + [ANY] * c_out,
                         out_shape=out_shape + [_out_hbm(s) for s in comm.out_shape], scratch_shapes=scratch_shapes + comm.scratch,
                         name=name + "_" + comm.tag, **kwargs)(*args, *[_in_hbm(a) for a in comm.operands])
    return list(res[:n_out]), list(res[n_out:])


class _GatherSmall:
    def __init__(self, v, tag):
        self.operands, self.tag = [v], tag
        self.out_shape = [_sds((N_DEV,) + v.shape, F32)]
        self.scratch = _dma_semaphores(1)

    def _copies(self, srcs, outs, sems):
        x, y, c = _place()
        me = _block_of(x, y, c)
        copies = [pltpu.make_async_copy(srcs[0], outs[0].at[me], sems[2].at[0])]
        for k, peer in enumerate(_peers(x, y, c)):
            copies.append(pltpu.make_async_remote_copy(src_ref=srcs[0], dst_ref=outs[0].at[me], send_sem=sems[0].at[0, k],
                                                       recv_sem=sems[1].at[0, k], device_id=peer, device_id_type=MESH))
        return copies

    def start(self, srcs, outs, sems):
        for cp in self._copies(srcs, outs, sems):
            cp.start()

    def forward(self, srcs, outs, sems):
        pass

    def finish(self, srcs, outs, sems):
        for cp in self._copies(srcs, outs, sems):
            cp.wait()


class _Together:
    def __init__(self, *comms):
        self.comms = comms
        self.operands = [a for cm in comms for a in cm.operands]
        self.out_shape = [s for cm in comms for s in cm.out_shape]
        self.scratch = [s for cm in comms for s in cm.scratch]
        self.tag = "_".join(cm.tag for cm in comms)
        self.kinds = tuple(kind for cm in comms for kind in getattr(cm, "kinds", ()))

    def _each(self, srcs, outs, sems):
        a = o = s = 0
        for cm in self.comms:
            na, no, ns = len(cm.operands), len(cm.out_shape), len(cm.scratch)
            yield cm, srcs[a:a + na], outs[o:o + no], sems[s:s + ns]
            a, o, s = a + na, o + no, s + ns

    def start(self, srcs, outs, sems):
        for cm, *refs in self._each(srcs, outs, sems):
            cm.start(*refs)

    def forward(self, srcs, outs, sems):
        for cm, *refs in self._each(srcs, outs, sems):
            cm.forward(*refs)

    def finish(self, srcs, outs, sems):
        for cm, *refs in self._each(srcs, outs, sems):
            cm.finish(*refs)


def _rope_tables(pos_col, freq_row, comm=None):
    def body(p_ref, f_ref, c_ref, sa_ref, sb_ref):
        ang = p_ref[...].astype(F32) * f_ref[...]
        lane = lax.broadcasted_iota(jnp.int32, ang.shape, 1) % HEAD_DIM
        cos, sin = jnp.cos(ang), jnp.sin(ang)
        c_ref[...] = jnp.where(lane < ROPE_DIM, cos, 1.0)
        sa_ref[...] = jnp.where(lane < ROPE_DIM // 2, -sin, 0.0)
        sb_ref[...] = jnp.where((lane >= ROPE_DIM // 2) & (lane < ROPE_DIM), sin, 0.0)

    return _call(body, (pos_col, freq_row), grid=(SEQ // ROW_TILE,), in_specs=[_rows(1), _const((1, LANES))],
                 out_specs=[_rows(LANES)] * 3, out_shape=[_sds((SEQ, LANES), F32)] * 3, name="rope_tables", comm=comm)


def _rotate(t, c, sa, sb):
    parts = []
    for g in range(ATTN_WIDTH // LANES):
        tg = t[:, g * LANES:(g + 1) * LANES]
        parts.append(tg * c + pltpu.roll(tg, LANES - 8, axis=1) * sa + pltpu.roll(tg, 8, axis=1) * sb)
    return jnp.concatenate(parts, axis=1)


def _rotate_transposed(dt, c, sa, sb):
    parts = []
    for g in range(ATTN_WIDTH // LANES):
        dg = dt[:, g * LANES:(g + 1) * LANES]
        parts.append(dg * c + pltpu.roll(dg * sa, 8, axis=1) + pltpu.roll(dg * sb, LANES - 8, axis=1))
    return jnp.concatenate(parts, axis=1)


def _inproj_fwd(x, g_pre, w_in, tc, tsa, tsb, comm=None):
    def body(x_ref, g_ref, w_ref, c_ref, sa_ref, sb_ref, q_ref, k_ref, v_ref, u_ref, b_ref, cc_ref):
        y, _ = _rms(x_ref[...])
        h = (y * g_ref[...]).astype(BF16)

        def proj(n):
            return _dot(h, w_ref[:, n * ATTN_WIDTH:(n + 1) * ATTN_WIDTH])

        c, sa, sb = c_ref[...], sa_ref[...], sb_ref[...]
        q_ref[...] = (_rotate(proj(0), c, sa, sb) * (HEAD_DIM ** -0.5)).astype(BF16)
        k_ref[...] = _rotate(proj(1), c, sa, sb).astype(BF16)
        v_ref[...] = proj(2).astype(BF16)
        u_ref[...] = proj(3).astype(BF16)
        b_ref[...] = proj(4).astype(BF16)
        cc_ref[...] = proj(5).astype(BF16)

    a = ATTN_WIDTH
    return _call(
        body, (x, g_pre, w_in, tc, tsa, tsb), grid=(SEQ // ROW_TILE,),
        in_specs=[_rows(D_MODEL), _const((1, D_MODEL)), _const1((D_MODEL, IN_PROJ_WIDTH)), _rows(LANES), _rows(LANES), _rows(LANES)],
        out_specs=[_rows(a)] * 6, out_shape=[_sds((SEQ, a), BF16)] * 6,
        vmem_mib=40, name="inproj_fwd", comm=comm)


def _head_masks():
    lane = lax.broadcasted_iota(jnp.int32, (1, LANES), 1)
    first = lane < HEAD_DIM
    return first, first.astype(F32), 1.0 - first.astype(F32)


def _perm_chunks(dil):
    length = SEQ // dil
    out = []
    for r in range(dil):
        for c0 in range(0, length, PERM_CHUNK):
            chunk = (r * length + c0) // PERM_CHUNK
            rows = pl.ds(c0, PERM_CHUNK) if dil == 1 else pl.ds(r + dil * c0, PERM_CHUNK, stride=dil)
            out.append((chunk, rows))
    return out


def _chunk(c, offset=0):
    return pl.ds(offset + c * PERM_CHUNK, PERM_CHUNK)


def _write_band_bias(bias_ref):
    rr = lax.broadcasted_iota(jnp.int32, (Q_BLOCK, K_WINDOW), 0)
    cc = lax.broadcasted_iota(jnp.int32, (Q_BLOCK, K_WINDOW), 1)
    band = (cc >= rr) & (cc - rr <= 2 * HALF_WINDOW)
    bias_ref[0] = jnp.where(band, 0.0, NEG_INF)
    bias_ref[1] = jnp.where(band & (cc >= HALF_WINDOW), 0.0, NEG_INF)
    bias_ref[2] = jnp.where(band & (cc < Q_BLOCK + HALF_WINDOW), 0.0, NEG_INF)


def _band_bias_index(m0, length):
    return jnp.where(m0 % length == 0, 1, 0) + jnp.where((m0 + Q_BLOCK) % length == 0, 2, 0)


def _zero_key_padding(bufs):
    pad = jnp.zeros((HALF_WINDOW, LANES), BF16)
    for buf in bufs:
        buf[pl.ds(0, HALF_WINDOW), :] = pad
        buf[pl.ds(SEQ + HALF_WINDOW, HALF_WINDOW), :] = pad


def _attn_fwd(q, k, v, comm=None):
    group = 8

    def body(q_ref, k_ref, v_ref, o_ref, lse_ref, q32, k32, v32, qa, qb, kp, vp, accp, mlp,
             acc0, acc1, acc2, ml0, ml1, ml2, bias_ref):
        first, mask_a, mask_b = _head_masks()
        low = lax.broadcasted_iota(jnp.int32, (1, LANES), 1) % HEAD_DIM < HEAD_DIM // 2
        _zero_key_padding((kp, vp))
        _write_band_bias(bias_ref)
        q32[...] = q_ref[...].astype(F32)
        k32[...] = k_ref[...].astype(F32)
        v32[...] = v_ref[...].astype(F32)
        natural = ((acc0, ml0), (acc1, ml1), (acc2, ml2))

        for branch, dil in enumerate(DILATIONS):
            length = SEQ // dil
            assert length >= 2 * Q_BLOCK
            chunks = _perm_chunks(dil)
            for c, rows in chunks:
                val = q32[rows, :]
                qa[_chunk(c), :] = (val * mask_a).astype(BF16)
                qb[_chunk(c), :] = (val * mask_b).astype(BF16)
                kp[_chunk(c, HALF_WINDOW), :] = k32[rows, :].astype(BF16)
                vp[_chunk(c, HALF_WINDOW), :] = v32[rows, :].astype(BF16)
            acc_dst, ml_dst = natural[branch] if dil == 1 else (accp, mlp)

            def blocks(i, carry, length=length, acc_dst=acc_dst, ml_dst=ml_dst):
                base = pl.multiple_of(i * (group * Q_BLOCK), group * Q_BLOCK)
                starts = [base + g * Q_BLOCK for g in range(group)]
                scores = [[_dot_nt(qx[pl.ds(m0, Q_BLOCK), :], kp[pl.ds(m0, K_WINDOW), :]) for qx in (qa, qb)] for m0 in starts]
                probs = []
                for m0, pair in zip(starts, scores):
                    bias = bias_ref[_band_bias_index(m0, length)]
                    stats, ps = [], []
                    for s in pair:
                        s = s + bias
                        m = jnp.max(s, axis=1, keepdims=True)
                        p = jnp.exp(s - m)
                        stats.append(jnp.where(low, m, jnp.sum(p, axis=1, keepdims=True)))
                        ps.append(p.astype(BF16))
                    ml_dst[pl.ds(m0, Q_BLOCK), :] = jnp.where(first, stats[0], stats[1])
                    probs.append(ps)
                for m0, ps in zip(starts, probs):
                    vw = vp[pl.ds(m0, K_WINDOW), :]
                    acc_dst[pl.ds(m0, Q_BLOCK), :] = jnp.where(first, _dot(ps[0], vw), _dot(ps[1], vw))
                return carry

            lax.fori_loop(0, SEQ // (group * Q_BLOCK), blocks, 0)

            if dil > 1:
                for c, rows in chunks:
                    natural[branch][0][rows, :] = accp[_chunk(c), :]
                    natural[branch][1][rows, :] = mlp[_chunk(c), :]

        for c in range(SEQ // PERM_CHUNK):
            packed = [ml[_chunk(c), :] for _, ml in natural]
            ms = [jnp.where(low, ml, pltpu.roll(ml, HEAD_DIM // 2, axis=1)) for ml in packed]
            ls = [jnp.where(low, pltpu.roll(ml, LANES - HEAD_DIM // 2, axis=1), ml) for ml in packed]
            m_all = jnp.maximum(jnp.maximum(ms[0], ms[1]), ms[2])
            es = [jnp.exp(m - m_all) for m in ms]
            z = ls[0] * es[0] + ls[1] * es[1] + ls[2] * es[2]
            num = natural[0][0][_chunk(c), :] * es[0] + natural[1][0][_chunk(c), :] * es[1] + natural[2][0][_chunk(c), :] * es[2]
            o_ref[_chunk(c), :] = num / z
            lse_ref[_chunk(c), :] = m_all + jnp.log(z)

    col = pl.BlockSpec((SEQ, LANES), lambda h: (0, h))
    padded = SEQ + 2 * HALF_WINDOW
    return _call(
        body, (q, k, v), grid=(ATTN_WIDTH // LANES,), in_specs=[col] * 3, out_specs=[col] * 2,
        out_shape=[_sds((SEQ, ATTN_WIDTH), F32)] * 2,
        scratch_shapes=[pltpu.VMEM((SEQ, LANES), F32)] * 3 + [pltpu.VMEM((SEQ, LANES), BF16)] * 2
        + [pltpu.VMEM((padded, LANES), BF16)] * 2 + [pltpu.VMEM((SEQ, LANES), F32)] * 8
        + [pltpu.VMEM((3, Q_BLOCK, K_WINDOW), F32)],
        vmem_mib=52, name="attn_fwd", comm=comm)


def _shifted(t, before, after, i):
    tile = t.shape[0]
    row = lax.broadcasted_iota(jnp.int32, (tile, 1), 0)
    before = jnp.where(i > 0, before, 0.0)
    after = jnp.where(i < SEQ // tile - 1, after, 0.0)
    return (jnp.where(row == 0, before, pltpu.roll(t, 1, axis=0)),
            jnp.where(row == tile - 1, after, pltpu.roll(t, tile - 1, axis=0)))


def _last_row(ref):
    return ref[HALO_ROWS - 1:HALO_ROWS, :].astype(F32)


def _first_row(ref):
    return ref[0:1, :].astype(F32)


def _conv_parts(u, c, u_prev, c_prev, u_next, c_next, cw, i):
    t = c * u
    t_prev, t_next = _shifted(t, _last_row(c_prev) * _last_row(u_prev), _first_row(c_next) * _first_row(u_next), i)
    s = cw[0:1, :] * t_prev + cw[1:2, :] * t + cw[2:3, :] * t_next
    return t, t_prev, t_next, s


def _mix_fwd(attn, u, b, c, conv_w, g_attn, g_conv, w_out, x, g_post, comm=None):
    def body(a_ref, u_ref, b_ref, c_ref, up_ref, cp_ref, un_ref, cn_ref, cw_ref, ga_ref, gc_ref, w_ref, x_ref, gp_ref,
             x1_ref, mg_ref, mix_ref):
        i = pl.program_id(0)
        _, _, _, s = _conv_parts(u_ref[...].astype(F32), c_ref[...].astype(F32), up_ref, cp_ref, un_ref, cn_ref, cw_ref[...], i)
        ya, _ = _rms(a_ref[...])
        yc, _ = _rms(b_ref[...].astype(F32) * s)
        merged = jnp.concatenate([ya * ga_ref[...], yc * gc_ref[...]], axis=1).astype(BF16)
        mix = _dot(merged, w_ref[...])
        ym, _ = _rms(mix)
        mg_ref[...] = merged.T
        mix_ref[...] = mix.astype(BF16)
        x1_ref[...] = x_ref[...] + ym * gp_ref[...]

    a = ATTN_WIDTH
    return _call(
        body, (attn, u, b, c, u, c, u, c, conv_w, g_attn, g_conv, w_out, x, g_post), grid=(SEQ // ROW_TILE,),
        in_specs=[_rows(a)] * 4 + [_halo_prev(a)] * 2 + [_halo_next(a)] * 2
        + [_const((3, a)), _const((1, a)), _const((1, a)), _const1((D_MODEL, D_MODEL)), _rows(D_MODEL), _const((1, D_MODEL))],
        out_specs=[_rows(D_MODEL), _cols(D_MODEL, ROW_TILE), _rows(D_MODEL)],
        out_shape=[_sds((SEQ, D_MODEL), F32), _sds((D_MODEL, SEQ), BF16), _sds((SEQ, D_MODEL), BF16)],
        vmem_mib=40, name="mix_fwd", comm=comm)


def _gu_spec():
    return pl.BlockSpec((N_DEV, FFN_TILE, FFN_BLOCK), lambda i: (0, i, 0))


def _ffn_fwd(x1, g_pre, w_gu, w_dn, g_post, comm=None, target=None):
    n_tiles = SEQ // ROW_TILE

    def body(*refs):
        if target is None:
            x_ref, g_ref, wgu_ref, wdn_ref, gp_ref, x2_ref, gu_ref, f_ref = refs
        else:
            x_ref, g_ref, wgu_ref, wdn_ref, gp_ref, t_ref, x2_ref, gu_ref, f_ref, loss_ref, acc = refs
        x1 = x_ref[...]
        y, _ = _rms(x1)
        h = (y * g_ref[...]).astype(BF16)
        f = jnp.zeros((ROW_TILE, D_MODEL), F32)

        def gate_up(j):
            return _dot_nt(h, wgu_ref[j]), _dot_nt(h, wgu_ref[j + N_DEV // 2])

        ahead = gate_up(0)
        for j in range(N_DEV // 2):
            gate, up = ahead
            if j + 1 < N_DEV // 2:
                ahead = gate_up(j + 1)
            gu_ref[j] = gate.astype(BF16)
            gu_ref[j + N_DEV // 2] = up.astype(BF16)
            act = (gate * jax.nn.sigmoid(gate) * up).astype(BF16)
            f = f + _dot(act, wdn_ref[pl.ds(j * FFN_BLOCK, FFN_BLOCK), :])
        yf, _ = _rms(f)
        f_ref[...] = f
        x2 = x1 + yf * gp_ref[...]
        if target is None:
            x2_ref[...] = x2
        else:
            i = pl.program_id(0)
            err = x2 - t_ref[...]
            x2_ref[...] = err * (1.0 / D_MODEL)

            @pl.when(i == 0)
            def _():
                acc[...] = jnp.zeros_like(acc)

            acc[...] += jnp.sum(err * err, axis=0, keepdims=True)

            @pl.when(i == n_tiles - 1)
            def _():
                loss_ref[...] = jnp.sum(acc[...], axis=1, keepdims=True) * (0.5 / D_MODEL)

    with_loss = target is not None
    return _call(
        body, (x1, g_pre, w_gu, w_dn, g_post) + ((target,) if with_loss else ()), grid=(n_tiles,),
        in_specs=[_rows(D_MODEL), _const((1, D_MODEL)), _const1((N_DEV, FFN_BLOCK, D_MODEL)), _const1((FFN_HIDDEN, D_MODEL)),
                  _const((1, D_MODEL))] + ([_rows(D_MODEL)] if with_loss else []),
        out_specs=[_rows(D_MODEL), pl.BlockSpec((N_DEV, ROW_TILE, FFN_BLOCK), lambda i: (0, i, 0)), _rows(D_MODEL)]
        + ([_const((1, 1))] if with_loss else []),
        out_shape=[_sds((SEQ, D_MODEL), F32), _sds((N_DEV, SEQ, FFN_BLOCK), BF16), _sds((SEQ, D_MODEL), F32)]
        + ([_sds((1, 1), F32)] if with_loss else []),
        scratch_shapes=[pltpu.VMEM((1, D_MODEL), F32)] if with_loss else [],
        vmem_mib=58, name="ffn_fwd_loss" if with_loss else "ffn_fwd", comm=comm)


def _accumulate(pairs, i):
    @pl.when(i == 0)
    def _():
        for ref, value in pairs:
            ref[...] = value

    @pl.when(i > 0)
    def _():
        for ref, value in pairs:
            ref[...] += value


def _colsum(v):
    return jnp.sum(v, axis=0, keepdims=True)


def _ffn_bwd(dx2, f, x1, gu, w_gu, w_dn, g_post, g_pre, comm=None):
    half = N_DEV // 2

    def body(dx2_ref, f_ref, x1_ref, gu_ref, wgu_ref, wdn_ref, gpost_ref, gpre_ref,
             dx1_ref, df_ref, act_ref, dgu_ref, h_ref, dgpost_ref, dgpre_ref):
        i = pl.program_id(0)
        dx2 = dx2_ref[...]
        yf, rf = _rms(f_ref[...])
        dg_post = _colsum(dx2 * yf)
        df = _rms_bwd(dx2 * gpost_ref[...], yf, rf).astype(BF16)
        df_ref[...] = df
        dh = jnp.zeros((FFN_TILE, D_MODEL), F32)

        def d_act(j):
            return _dot_nt(df, wdn_ref[pl.ds(j * FFN_BLOCK, FFN_BLOCK), :])

        ahead = d_act(0)
        for j in range(half):
            dact = ahead
            if j + 1 < half:
                ahead = d_act(j + 1)
            gate = gu_ref[j].astype(F32)
            up = gu_ref[j + half].astype(F32)
            sig = jax.nn.sigmoid(gate)
            silu = gate * sig
            act_ref[j] = (silu * up).astype(BF16)
            dgate = (dact * up * (sig * (1.0 + gate * (1.0 - sig)))).astype(BF16)
            dup = (dact * silu).astype(BF16)
            dgu_ref[j] = dgate
            dgu_ref[j + half] = dup
            dh = dh + _dot(dgate, wgu_ref[j]) + _dot(dup, wgu_ref[j + half])
        y1, r1 = _rms(x1_ref[...])
        h_ref[...] = (y1 * gpre_ref[...]).astype(BF16)
        dx1_ref[...] = dx2 + _rms_bwd(dh * gpre_ref[...], y1, r1)
        _accumulate([(dgpost_ref, dg_post), (dgpre_ref, _colsum(dh * y1))], i)

    act_spec = pl.BlockSpec((half, FFN_TILE, FFN_BLOCK), lambda i: (0, i, 0))
    return _call(
        body, (dx2, f, x1, gu, w_gu, w_dn, g_post, g_pre), grid=(SEQ // FFN_TILE,),
        in_specs=[_frows(D_MODEL)] * 3 + [_gu_spec(), _const1((N_DEV, FFN_BLOCK, D_MODEL)), _const1((FFN_HIDDEN, D_MODEL)),
                                          _const((1, D_MODEL)), _const((1, D_MODEL))],
        out_specs=[_frows(D_MODEL), _frows(D_MODEL), act_spec, _gu_spec(), _frows(D_MODEL), _const((1, D_MODEL)), _const((1, D_MODEL))],
        out_shape=[_sds((SEQ, D_MODEL), F32), _sds((SEQ, D_MODEL), BF16), _sds((half, SEQ, FFN_BLOCK), BF16),
                   _sds((N_DEV, SEQ, FFN_BLOCK), BF16), _sds((SEQ, D_MODEL), BF16), _sds((1, D_MODEL), F32), _sds((1, D_MODEL), F32)],
        vmem_mib=52, name="ffn_bwd", comm=comm)


def _wgrad(a_t, g, name):
    width = g.shape[1]

    def body(a_ref, g_ref, o_ref):
        o_ref[...] = _dot(a_ref[...], g_ref[...]).astype(BF16)

    return _call(body, (a_t, g), grid=(width // WGRAD_COLS,),
                 in_specs=[_const1((D_MODEL, SEQ)), pl.BlockSpec((SEQ, WGRAD_COLS), lambda j: (0, j))],
                 out_specs=[pl.BlockSpec((D_MODEL, WGRAD_COLS), lambda j: (0, j))], out_shape=[_sds((D_MODEL, width), BF16)],
                 vmem_mib=48, name=name)[0][0]


def _wgrad_paired(a, g, kind, to_owners=False, comm=None):
    shard = SHARD_SHAPES[kind]
    per_chip = 2 if kind == "gu" else 1
    n_compute = per_chip * N_CHIPS
    n_steps = n_compute + 1

    def chip_of(turn):
        if not to_owners:
            return turn
        return (2 * lax.axis_index("x") + lax.axis_index("y") + 1 + turn) % N_CHIPS

    def block(t):
        return jnp.minimum(t, n_compute - 1)

    if kind == "in":
        specs = [_const1((D_MODEL, SEQ)), pl.BlockSpec((SEQ, 2 * W_IN_BLOCK), lambda t: (0, chip_of(block(t))))]
    else:
        specs = [pl.BlockSpec((1, SEQ, FFN_BLOCK), lambda t: (block(t), 0, 0)), _const1((SEQ, D_MODEL))]

    def body(a_ref, g_ref, o_ref, *rest):
        if to_owners:
            landing, sendbuf, recvbuf, keepbuf, send_sem, recv_sem, donebuf, owner_send, owner_recv, own_sem = rest
        else:
            sendbuf, recvbuf, keepbuf, send_sem, recv_sem = rest
        t = pl.program_id(0)
        x, y, c = _place()

        def exchange(q):
            return pltpu.make_async_remote_copy(src_ref=sendbuf.at[q], dst_ref=recvbuf.at[q], send_sem=send_sem.at[q],
                                                recv_sem=recv_sem.at[q], device_id=(x, y, 1 - c), device_id_type=MESH)

        my_chip = 2 * x + y

        def to_owner(q):
            return pltpu.make_async_remote_copy(src_ref=donebuf.at[q], dst_ref=landing.at[my_chip], send_sem=owner_send.at[q],
                                                recv_sem=owner_recv.at[my_chip], device_id=(q // 2, q % 2, c), device_id_type=MESH)

        def keep_own():
            return pltpu.make_async_copy(donebuf.at[my_chip], landing.at[my_chip], own_sem)

        @pl.when((t >= per_chip) & (t % per_chip == 0))
        def _():
            q = chip_of(t // per_chip - 1)
            exchange(q).wait_recv()
            done = (keepbuf[...] + recvbuf[q].astype(F32)).astype(BF16)
            o_ref[0] = done
            if to_owners:
                donebuf[q] = done

                @pl.when(q == my_chip)
                def _():
                    keep_own().start()

                @pl.when(q != my_chip)
                def _():
                    to_owner(q).start()

        @pl.when(t < n_compute)
        def _():
            q = chip_of(t // per_chip)
            if kind == "gu":
                r = _dot_tn(a_ref[0], g_ref[...])

                @pl.when(t % 2 == c)
                def _():
                    keepbuf[...] = r

                @pl.when(t % 2 != c)
                def _():
                    sendbuf[q] = r.astype(BF16)
                    exchange(q).start()
            else:
                if kind == "dn":
                    r = _dot_tn(a_ref[0], g_ref[...])
                    lower, upper = r[:W_DOWN_BLOCK], r[W_DOWN_BLOCK:]
                else:
                    r = _dot(a_ref[...], g_ref[...])
                    lower, upper = r[:, :W_IN_BLOCK], r[:, W_IN_BLOCK:]
                keepbuf[...] = jnp.where(c == 0, lower, upper)
                sendbuf[q] = jnp.where(c == 0, upper, lower).astype(BF16)
                exchange(q).start()

        @pl.when(t == n_steps - 1)
        def _():
            for q in range(N_CHIPS):
                exchange(q).wait_send()
            if to_owners:
                keep_own().wait()
                for s in range(N_CHIPS):
                    @pl.when(s != my_chip)
                    def _(s=s):
                        pltpu.make_async_remote_copy(src_ref=donebuf.at[s], dst_ref=landing.at[s], send_sem=owner_send.at[s],
                                                     recv_sem=owner_recv.at[s], device_id=(x, y, c), device_id_type=MESH).wait()

    slots = (N_CHIPS,) + shard
    owners = int(to_owners)
    res, landed = _call(
        body, (a, g), grid=(n_steps,), in_specs=specs,
        out_specs=[pl.BlockSpec((1,) + shard, lambda t: (chip_of(jnp.clip(t // per_chip - 1, 0, N_CHIPS - 1)), 0, 0))] + [ANY] * owners,
        out_shape=[_sds(slots, BF16)] * (1 + owners),
        scratch_shapes=[pltpu.VMEM(slots, BF16), pltpu.VMEM(slots, BF16), pltpu.VMEM(shard, F32),
                        pltpu.SemaphoreType.DMA((N_CHIPS,)), pltpu.SemaphoreType.DMA((N_CHIPS,))]
        + [pltpu.VMEM(slots, BF16), pltpu.SemaphoreType.DMA((N_CHIPS,)), pltpu.SemaphoreType.DMA((N_CHIPS,)), pltpu.SemaphoreType.DMA] * owners,
        vmem_mib=52, name="wgrad_" + kind + "_paired" + "_to_owners" * owners, comm=comm)
    return res[owners] if comm is None else (res[owners], landed)


def _mix_bwd(dx1, mix, attn, u, b, c, conv_w, g_attn, g_conv, g_post, w_out):
    def body(dx1_ref, mix_ref, a_ref, u_ref, b_ref, c_ref, up_ref, cp_ref, un_ref, cn_ref, cw_ref, ga_ref, gc_ref, gp_ref, w_ref,
             dmix_ref, da_ref, ds_ref, db_ref, dgp_ref, dga_ref, dgc_ref):
        i = pl.program_id(0)
        dx1 = dx1_ref[...]
        ym, rm = _rms(mix_ref[...].astype(F32))
        dg_post = _colsum(dx1 * ym)
        dmix = _rms_bwd(dx1 * gp_ref[...], ym, rm).astype(BF16)
        dmix_ref[...] = dmix
        dmerged = _dot_nt(dmix, w_ref[...])
        dna, dnc = dmerged[:, :ATTN_WIDTH], dmerged[:, ATTN_WIDTH:]
        ya, ra = _rms(a_ref[...])
        da_ref[...] = _rms_bwd(dna * ga_ref[...], ya, ra)
        _, _, _, s = _conv_parts(u_ref[...].astype(F32), c_ref[...].astype(F32), up_ref, cp_ref, un_ref, cn_ref, cw_ref[...], i)
        gate_b = b_ref[...].astype(F32)
        yc, rc = _rms(gate_b * s)
        dy = _rms_bwd(dnc * gc_ref[...], yc, rc)
        db_ref[...] = (dy * s).astype(BF16)
        ds_ref[...] = (dy * gate_b).astype(BF16)
        _accumulate([(dgp_ref, dg_post), (dga_ref, _colsum(dna * ya)), (dgc_ref, _colsum(dnc * yc))], i)

    a = ATTN_WIDTH
    return _call(
        body, (dx1, mix, attn, u, b, c, u, c, u, c, conv_w, g_attn, g_conv, g_post, w_out), grid=(SEQ // ROW_TILE,),
        in_specs=[_rows(D_MODEL)] * 2 + [_rows(a)] * 4 + [_halo_prev(a)] * 2 + [_halo_next(a)] * 2
        + [_const((3, a)), _const((1, a)), _const((1, a)), _const((1, D_MODEL)), _const1((D_MODEL, D_MODEL))],
        out_specs=[_rows(D_MODEL)] + [_rows(a)] * 3 + [_const((1, D_MODEL)), _const((1, a)), _const((1, a))],
        out_shape=[_sds((SEQ, D_MODEL), BF16), _sds((SEQ, a), F32), _sds((SEQ, a), BF16), _sds((SEQ, a), BF16),
                   _sds((1, D_MODEL), F32), _sds((1, a), F32), _sds((1, a), F32)],
        vmem_mib=40, name="mix_bwd")[0]


def _attn_bwd(q, k, v, do, o, lse, comm=None):
    group = 8

    n_pairs = ATTN_WIDTH // LANES

    def body(q_hbm, k_hbm, v_hbm, do_hbm, o_hbm, lse_hbm, dq_ref, dk_ref, dv_ref,
             q_ref, k_ref, v_ref, do_ref, o_ref, lse_ref, in_sem,
             q32, k32, v32, qa, qb, doa, dob, kp, vp, lsep, dlp, dnat, dqp, dkp, dvp, bias_ref):
        pair = pl.program_id(0)
        sources = (q_hbm, k_hbm, v_hbm, do_hbm, o_hbm, lse_hbm)
        buffers = (q_ref, k_ref, v_ref, do_ref, o_ref, lse_ref)

        def fetch(i, p):
            return pltpu.make_async_copy(sources[i].at[:, pl.ds(pl.multiple_of(p * LANES, LANES), LANES)], buffers[i], in_sem.at[i])

        def prefetch(*which):
            @pl.when(pair + 1 < n_pairs)
            def _():
                for i in which:
                    fetch(i, pair + 1).start()

        @pl.when(pair == 0)
        def _():
            for i in range(len(sources)):
                fetch(i, 0).start()

        for i in range(len(sources)):
            fetch(i, pair).wait()

        first, mask_a, mask_b = _head_masks()
        _zero_key_padding((kp, vp))
        _write_band_bias(bias_ref)
        q32[...] = q_ref[...].astype(F32)
        k32[...] = k_ref[...].astype(F32)
        v32[...] = v_ref[...].astype(F32)
        prefetch(0, 1, 2)
        for c in range(SEQ // PERM_CHUNK):
            prod = do_ref[_chunk(c), :] * o_ref[_chunk(c), :]
            d_a = jnp.sum(prod * mask_a, axis=1, keepdims=True)
            d_b = jnp.sum(prod * mask_b, axis=1, keepdims=True)
            dnat[_chunk(c), :] = jnp.where(first, d_a, d_b)
        prefetch(4)

        for step, dil in enumerate(DILATIONS[1:] + DILATIONS[:1]):
            length = SEQ // dil
            assert length >= 2 * Q_BLOCK
            chunks = _perm_chunks(dil)
            for c, rows in chunks:
                val = q32[rows, :]
                qa[_chunk(c), :] = (val * mask_a).astype(BF16)
                qb[_chunk(c), :] = (val * mask_b).astype(BF16)
                val = do_ref[rows, :]
                doa[_chunk(c), :] = (val * mask_a).astype(BF16)
                dob[_chunk(c), :] = (val * mask_b).astype(BF16)
                kp[_chunk(c, HALF_WINDOW), :] = k32[rows, :].astype(BF16)
                vp[_chunk(c, HALF_WINDOW), :] = v32[rows, :].astype(BF16)
                lsep[_chunk(c), :] = lse_ref[rows, :]
                dlp[_chunk(c), :] = dnat[rows, :]
            if step == len(DILATIONS) - 1:
                prefetch(3, 5)
            zero = jnp.zeros((PERM_CHUNK, LANES), F32)
            for c in range(SEQ // PERM_CHUNK):
                dkp[_chunk(c), :] = zero
                dvp[_chunk(c), :] = zero
            dkp[pl.ds(SEQ, 2 * HALF_WINDOW), :] = zero[:2 * HALF_WINDOW]
            dvp[pl.ds(SEQ, 2 * HALF_WINDOW), :] = zero[:2 * HALF_WINDOW]

            heads = ((qa, doa, 0), (qb, dob, HEAD_DIM))

            def blocks(i, carry, length=length):
                base = pl.multiple_of(i * (group * Q_BLOCK), group * Q_BLOCK)
                starts = [base + g * Q_BLOCK for g in range(group)]
                raw = [[(_dot_nt(qx[pl.ds(m0, Q_BLOCK), :], kp[pl.ds(m0, K_WINDOW), :]),
                         _dot_nt(dox[pl.ds(m0, Q_BLOCK), :], vp[pl.ds(m0, K_WINDOW), :])) for qx, dox, _ in heads]
                       for m0 in starts]
                grads = []
                for m0, pair in zip(starts, raw):
                    bias = bias_ref[_band_bias_index(m0, length)]
                    lse_b, d_b = lsep[pl.ds(m0, Q_BLOCK), :], dlp[pl.ds(m0, Q_BLOCK), :]
                    out = []
                    for (s, dp), (_, _, col) in zip(pair, heads):
                        p = jnp.exp(s + bias - lse_b[:, col:col + 1])
                        out.append(((p * (dp - d_b[:, col:col + 1])).astype(BF16), p.astype(BF16)))
                    grads.append(out)
                for m0, out in zip(starts, grads):
                    qrows, krows = pl.ds(m0, Q_BLOCK), pl.ds(m0, K_WINDOW)
                    kw = kp[krows, :]
                    dk = jnp.zeros((K_WINDOW, LANES), F32)
                    dv = jnp.zeros((K_WINDOW, LANES), F32)
                    for (ds, p), (qx, dox, _) in zip(out, heads):
                        dk = dk + _dot_tn(ds, qx[qrows, :])
                        dv = dv + _dot_tn(p, dox[qrows, :])
                    dqp[qrows, :] = jnp.where(first, _dot(out[0][0], kw), _dot(out[1][0], kw)) * (HEAD_DIM ** -0.5)
                    dkp[krows, :] += dk
                    dvp[krows, :] += dv
                return carry

            lax.fori_loop(0, SEQ // (group * Q_BLOCK), blocks, 0)

            for c, rows in chunks:
                g_q, g_k, g_v = dqp[_chunk(c), :], dkp[_chunk(c, HALF_WINDOW), :], dvp[_chunk(c, HALF_WINDOW), :]
                if step == 0:
                    dq_ref[rows, :] = g_q
                    dk_ref[rows, :] = g_k
                    dv_ref[rows, :] = g_v
                else:
                    dq_ref[rows, :] = dq_ref[rows, :] + g_q
                    dk_ref[rows, :] = dk_ref[rows, :] + g_k
                    dv_ref[rows, :] = dv_ref[rows, :] + g_v

    col = pl.BlockSpec((SEQ, LANES), lambda h: (0, h))
    padded = SEQ + 2 * HALF_WINDOW
    return _call(
        body, (q, k, v, do, o, lse), grid=(n_pairs,), in_specs=[ANY] * 6, out_specs=[col] * 3,
        out_shape=[_sds((SEQ, ATTN_WIDTH), F32)] * 3,
        scratch_shapes=[pltpu.VMEM((SEQ, LANES), BF16)] * 3 + [pltpu.VMEM((SEQ, LANES), F32)] * 3 + [pltpu.SemaphoreType.DMA((6,))]
        + [pltpu.VMEM((SEQ, LANES), F32)] * 3 + [pltpu.VMEM((SEQ, LANES), BF16)] * 4
        + [pltpu.VMEM((padded, LANES), BF16)] * 2 + [pltpu.VMEM((SEQ, LANES), F32)] * 4 + [pltpu.VMEM((padded, LANES), F32)] * 2
        + [pltpu.VMEM((3, Q_BLOCK, K_WINDOW), F32)],
        vmem_mib=56, name="attn_bwd", comm=comm)


def _inproj_bwd(dq, dk, dv, ds, db, u, c, conv_w, tc, tsa, tsb, w_in, x, g_pre, dx1):
    def body(dq_ref, dk_ref, dv_ref, ds_ref, db_ref, u_ref, c_ref, dsp_ref, up_ref, cp_ref, dsn_ref, un_ref, cn_ref, cw_ref,
             tc_ref, tsa_ref, tsb_ref, w_ref, x_ref, g_ref, dx1_ref, dx_ref, dproj_ref, h_ref, dg_ref, dcw_ref):
        i = pl.program_id(0)
        cw = cw_ref[...]
        u, c = u_ref[...].astype(F32), c_ref[...].astype(F32)
        t, t_prev, t_next, _ = _conv_parts(u, c, up_ref, cp_ref, un_ref, cn_ref, cw, i)
        ds = ds_ref[...].astype(F32)
        ds_prev, ds_next = _shifted(ds, _last_row(dsp_ref), _first_row(dsn_ref), i)
        dt = cw[0:1, :] * ds_next + cw[1:2, :] * ds + cw[2:3, :] * ds_prev
        d_taps = jnp.concatenate([_colsum(ds * t_prev), _colsum(ds * t), _colsum(ds * t_next)], axis=0)
        tc_, tsa_, tsb_ = tc_ref[...], tsa_ref[...], tsb_ref[...]
        groups = ((2, lambda: dv_ref[...].astype(BF16)), (4, lambda: db_ref[...]),
                  (0, lambda: _rotate_transposed(dq_ref[...], tc_, tsa_, tsb_).astype(BF16)),
                  (1, lambda: _rotate_transposed(dk_ref[...], tc_, tsa_, tsb_).astype(BF16)),
                  (3, lambda: (dt * c).astype(BF16)), (5, lambda: (dt * u).astype(BF16)))
        dh = jnp.zeros((x_ref.shape[0], D_MODEL), F32)
        for n, make in groups:
            cols = pl.ds(n * a, a)
            part = make()
            dproj_ref[:, cols] = part
            dh = dh + _dot_nt(part, w_ref[:, cols])
        y, r = _rms(x_ref[...])
        h_ref[...] = (y * g_ref[...]).astype(BF16).T
        dx_ref[...] = dx1_ref[...] + _rms_bwd(dh * g_ref[...], y, r)
        _accumulate([(dg_ref, _colsum(dh * y)), (dcw_ref, d_taps)], i)

    a = ATTN_WIDTH
    tile = FFN_TILE

    def rows(width):
        return _rows(width, tile)

    return _call(
        body, (dq, dk, dv, ds, db, u, c, ds, u, c, ds, u, c, conv_w, tc, tsa, tsb, w_in, x, g_pre, dx1), grid=(SEQ // tile,),
        in_specs=[rows(a)] * 7 + [_halo_prev(a, tile)] * 3 + [_halo_next(a, tile)] * 3
        + [_const((3, a)), rows(LANES), rows(LANES), rows(LANES), _const1((D_MODEL, IN_PROJ_WIDTH)), rows(D_MODEL),
           _const((1, D_MODEL)), rows(D_MODEL)],
        out_specs=[rows(D_MODEL), rows(IN_PROJ_WIDTH), _cols(D_MODEL, tile), _const((1, D_MODEL)), _const((3, a))],
        out_shape=[_sds((SEQ, D_MODEL), F32), _sds((SEQ, IN_PROJ_WIDTH), BF16), _sds((D_MODEL, SEQ), BF16),
                   _sds((1, D_MODEL), F32), _sds((3, a), F32)],
        vmem_mib=48, name="inproj_bwd")


def _adamw_math(w, g, m, v):
    m = ADAM_B1 * m + (1.0 - ADAM_B1) * g
    v = ADAM_B2 * v + (1.0 - ADAM_B2) * (g * g)
    m_hat = m / (1.0 - ADAM_B1 ** ADAM_STEP)
    v_hat = v / (1.0 - ADAM_B2 ** ADAM_STEP)
    delta = -ADAM_LR * (m_hat / (jnp.sqrt(v_hat) + ADAM_EPS) + ADAM_WD * w)
    return delta, m, v


def _sum_parts(p_ref):
    g = p_ref[0].astype(F32)
    for k in range(1, p_ref.shape[0]):
        g = g + p_ref[k].astype(F32)
    return g


def _adamw_layers(parts, w, m, v, row_tile, name, comm=None):
    _, rows, cols = w.shape
    n_tiles = rows // row_tile

    def body(*refs):
        p_refs = refs[:DEPTH]
        w_ref, m_ref, v_ref, g_ref, d_ref, nm_ref, nv_ref = refs[DEPTH:]
        layer = pl.program_id(0)
        for l, p_ref in enumerate(p_refs):
            @pl.when(layer == l)
            def _(p_ref=p_ref):
                g = _sum_parts(p_ref)
                g_ref[0] = g
                d_ref[0], nm_ref[0], nv_ref[0] = _adamw_math(w_ref[0], g, m_ref[0], v_ref[0])

    def part_spec(l):
        return pl.BlockSpec((parts[l].shape[0], row_tile, cols),
                            lambda layer, i: (0, jnp.where(layer == l, i, jnp.where(layer < l, 0, n_tiles - 1)), 0))

    tile = pl.BlockSpec((1, row_tile, cols), lambda layer, i: (layer, i, 0))
    return _call(body, (*parts, w, m, v), grid=(DEPTH, n_tiles), in_specs=[part_spec(l) for l in range(DEPTH)] + [tile] * 3,
                 out_specs=[tile] * 4, out_shape=[_sds(w.shape, F32)] * 4, name=name, comm=comm)


def _adamw_small(parts, w, m, v):
    n = len(GAIN_NAMES)
    a = ATTN_WIDTH
    where = ((0, 0), (1, 0), (2, 0), (3, 0), (4, 0), (4, a))

    def body(*refs):
        p_ref = refs[0]
        w_refs, m_refs, v_refs = refs[1:1 + n], refs[1 + n:1 + 2 * n], refs[1 + 2 * n:1 + 3 * n]
        outs = refs[1 + 3 * n:1 + 7 * n]
        taps_ref, loss_ref, sum_ref = refs[1 + 7 * n:]
        sum_ref[...] = _sum_parts(p_ref)
        for k, (row, col) in enumerate(where):
            width = w_refs[k].shape[1]
            g = jnp.concatenate([sum_ref[pl.ds(l * SMALL_ROWS + row, 1), pl.ds(col, width)] for l in range(DEPTH)], axis=0)
            g_ref, d_ref, nm_ref, nv_ref = outs[4 * k:4 * k + 4]
            g_ref[...] = g
            d_ref[...], nm_ref[...], nv_ref[...] = _adamw_math(w_refs[k][...], g, m_refs[k][...], v_refs[k][...])
        for l in range(DEPTH):
            for tap, (row, col) in enumerate(((5, 0), (5, a), (6, 0))):
                taps_ref[l, tap:tap + 1, :] = sum_ref[l * SMALL_ROWS + row:l * SMALL_ROWS + row + 1, col:col + a]
        loss_ref[...] = sum_ref[SMALL_ROWS - 1:SMALL_ROWS, 0:1]

    out_shape = [_sds(x.shape, F32) for x in w for _ in range(4)] + [_sds((DEPTH, 3, a), F32), _sds((1, 1), F32)]
    res = pl.pallas_call(body, out_shape=out_shape, scratch_shapes=[pltpu.VMEM(parts.shape[1:], F32)],
                         name="adamw_small")(parts, *w, *m, *v)
    return [res[4 * k:4 * k + 4] for k in range(n)], res[4 * n], res[4 * n + 1]


def _adamw_plain(g, w, m, v, name):
    def body(g_ref, w_ref, m_ref, v_ref, d_ref, nm_ref, nv_ref):
        d_ref[...], nm_ref[...], nv_ref[...] = _adamw_math(w_ref[...], g_ref[...], m_ref[...], v_ref[...])

    return pl.pallas_call(body, out_shape=[_sds(w.shape, F32)] * 3, name=name)(g, w, m, v)


SMALL_ROWS = 8
GAIN_NAMES = ("pre_mix_norm", "post_mix_norm", "pre_ffn_norm", "post_ffn_norm", "attn_out_norm", "conv_out_norm")


def _pack_small(pre_mix, post_mix, pre_ffn, post_ffn, attn_out, conv_out, taps):
    zeros = jnp.zeros((1, ATTN_WIDTH), F32)
    return jnp.concatenate([
        pre_mix, post_mix, pre_ffn, post_ffn, jnp.concatenate([attn_out, conv_out], axis=1),
        jnp.concatenate([taps[0:1], taps[1:2]], axis=1), jnp.concatenate([taps[2:3], zeros], axis=1),
        jnp.zeros((1, D_MODEL), F32)], axis=0)


def _rope(positions, comm=None):
    inv_freq = ROPE_THETA ** (-jnp.arange(0, ROPE_DIM, 2, dtype=F32) / ROPE_DIM)
    per_head = jnp.concatenate([inv_freq, inv_freq, jnp.zeros((HEAD_DIM - ROPE_DIM,), F32)])
    freq_row = jnp.tile(per_head, LANES // HEAD_DIM).reshape(1, LANES)
    return _rope_tables(positions.reshape(SEQ, 1), freq_row, comm)


def _layer_forward(h, gains, taps, tables, w, inproj_comm=None, attn_comm=None, mix_comm=None, ffn_comm=None, target=None):
    def arrived(comm, landed):
        return w if comm is None else {**w, **dict(zip(comm.kinds, landed))}

    (q, k, v, u, b, c), landed = _inproj_fwd(h, gains["pre_mix_norm"], w["in"], *tables, comm=inproj_comm)
    w = arrived(inproj_comm, landed)
    (attn, lse), landed = _attn_fwd(q, k, v, comm=attn_comm)
    w = arrived(attn_comm, landed)
    (x1, merged, mix), landed = _mix_fwd(attn, u, b, c, taps, gains["attn_out_norm"], gains["conv_out_norm"], w["out"], h,
                                         gains["post_mix_norm"], comm=mix_comm)
    w = arrived(mix_comm, landed)
    (x2, gu, f, *loss), landed_next = _ffn_fwd(x1, gains["pre_ffn_norm"], w["gu"], w["dn"], gains["post_ffn_norm"], comm=ffn_comm,
                                               target=target)
    out = x2 if target is None else (x2, loss[0])
    return out, (h, q, k, v, u, b, c, attn, lse, merged, mix, x1, gu, f), w, landed_next


class _GradientExchange:
    def ffn_grads(self, act, df, h2, dgu):
        return dict(dn=_wgrad_paired(act, df, "dn"), gu=_wgrad_paired(dgu, h2, "gu"))

    def w_out_grad(self, merged_t, dmix):
        return _wgrad(merged_t, dmix, "wgrad_out")

    def attention_passenger(self, ffn, g_out):
        return _Together(_ScatterChips(("gu", "dn"), (ffn["gu"], ffn["dn"])), _Scatter(("out",), (g_out,)))

    def w_in_grad(self, h_t, dproj, small_rows):
        if small_rows is None:
            return _wgrad(h_t, dproj, "wgrad_in"), None
        parts, landed = _wgrad_paired(h_t, dproj, "in", to_owners=True, comm=_GatherSmall(small_rows, "small"))
        return parts, landed[0]


def _layer_backward(dx, saved, gains, taps, tables, w, ffn_comm, exchange, last):
    x0, q, k, v, u, b, c, attn, lse, merged, mix, x1, gu, f = saved
    (dx1, df, act, dgu, h2, dg_post_ffn, dg_pre_ffn), landed_prev = _ffn_bwd(
        dx, f, x1, gu, w["gu"], w["dn"], gains["post_ffn_norm"], gains["pre_ffn_norm"], comm=ffn_comm)
    ffn = exchange.ffn_grads(act, df, h2, dgu)
    dmix, dattn, ds, db, dg_post_mix, dg_attn, dg_conv = _mix_bwd(
        dx1, mix, attn, u, b, c, taps, gains["attn_out_norm"], gains["conv_out_norm"], gains["post_mix_norm"], w["out"])
    g_out = exchange.w_out_grad(merged, dmix)
    attn_comm = exchange.attention_passenger(ffn, g_out)
    (dq, dk, dv), landed = _attn_bwd(q, k, v, dattn, attn, lse, comm=attn_comm)
    dx0, dproj, h1, dg_pre_mix, dtaps = _inproj_bwd(
        dq, dk, dv, ds, db, u, c, taps, *tables, w["in"], x0, gains["pre_mix_norm"], dx1)[0]
    small = _pack_small(dg_pre_mix, dg_post_mix, dg_pre_ffn, dg_post_ffn, dg_attn, dg_conv, dtaps)
    small_rows = None
    if last is not None:
        small_above, local_loss = last
        small_rows = lax.dynamic_update_slice(jnp.concatenate([small, small_above], axis=0), local_loss, (SMALL_ROWS - 1, 0))
    g_in, gathered = exchange.w_in_grad(h1, dproj, small_rows)
    rest = dict(zip(attn_comm.kinds, landed)) if attn_comm is not None else dict(ffn, out=g_out)
    return dx0, g_in, small, landed_prev, rest, gathered


def kernel(x, positions, pre_mix_norm, w_in, conv_w, attn_out_norm, conv_out_norm, w_out, post_mix_norm, pre_ffn_norm, w_gate_up, w_down, post_ffn_norm, loss_target, m_pre_mix_norm, m_w_in, m_conv_w, m_attn_out_norm, m_conv_out_norm, m_w_out, m_post_mix_norm, m_pre_ffn_norm, m_w_gate_up, m_w_down, m_post_ffn_norm, v_pre_mix_norm, v_w_in, v_conv_w, v_attn_out_norm, v_conv_out_norm, v_w_out, v_post_mix_norm, v_pre_ffn_norm, v_w_gate_up, v_w_down, v_post_ffn_norm):
    mx, my, mc = _place()
    me = _block_of(mx, my, mc)
    conv_channels = conv_w.shape[-1]

    def hidden_major(a):
        return jnp.swapaxes(a, 1, 2)

    big_w = dict(zip(WEIGHT_KINDS, (w_in, w_out, hidden_major(w_gate_up), w_down)))
    big_m = dict(zip(WEIGHT_KINDS, (m_w_in, m_w_out, hidden_major(m_w_gate_up), m_w_down)))
    big_v = dict(zip(WEIGHT_KINDS, (v_w_in, v_w_out, hidden_major(v_w_gate_up), v_w_down)))
    all_gains = dict(pre_mix_norm=pre_mix_norm, attn_out_norm=attn_out_norm, conv_out_norm=conv_out_norm, post_mix_norm=post_mix_norm,
                     pre_ffn_norm=pre_ffn_norm, post_ffn_norm=post_ffn_norm)

    def gather(kinds, l):
        return _Gather(kinds, [big_w[kind][l].astype(BF16) for kind in kinds])

    def gains(l):
        return {name: g[l:l + 1] for name, g in all_gains.items()}

    early = ("in", "out", "dn")
    taps_flat = jnp.pad(conv_w.reshape(-1), (0, 8 * LANES - conv_w.size)).reshape(8, LANES)
    tables, (first, taps_all) = _rope(positions, _Together(gather(early[:1], 0), _GatherSmall(taps_flat, "taps")))
    taps_all = taps_all.reshape(N_DEV, 8 * LANES)[:, :conv_w.size]
    conv_w_full = taps_all.reshape(N_DEV, DEPTH, 3, conv_channels).transpose(1, 2, 0, 3).reshape(DEPTH, 3, ATTN_WIDTH)
    weights = [dict(zip(early[:1], [first]))] + [None] * (DEPTH - 1)
    saved = [None] * DEPTH
    h = x[0]
    for l in range(DEPTH):
        inproj_comm = gather(("out",), 0) if l == 0 else None
        mix_comm = gather(("dn",), 0) if l == 0 else None
        ffn_comm = gather(early, l + 1) if l + 1 < DEPTH else None
        target = loss_target[0] if l + 1 == DEPTH else None
        h, saved[l], weights[l], landed = _layer_forward(h, gains(l), conv_w_full[l], tables, weights[l], inproj_comm,
                                                         gather(("gu",), l), mix_comm, ffn_comm, target)
        if ffn_comm is not None:
            weights[l + 1] = dict(zip(ffn_comm.kinds, landed))
    dx, local_loss = h

    parts = {kind: [None] * DEPTH for kind in WEIGHT_KINDS}
    small_grads = [None] * DEPTH
    g_in_above = None
    exchange = _GradientExchange()
    for l in reversed(range(DEPTH)):
        ffn_comm = _Scatter(("in",), (g_in_above,)) if g_in_above is not None else None
        last = (jnp.concatenate(small_grads[1:], axis=0), local_loss) if l == 0 else None
        dx, g_in_above, small_grads[l], landed, rest, gathered = _layer_backward(
            dx, saved[l], gains(l), conv_w_full[l], tables, weights[l], ffn_comm, exchange, last)
        if ffn_comm is not None:
            parts["in"][l + 1] = landed[0]
        for kind, part in rest.items():
            parts[kind][l] = part

    parts["in"][0] = g_in_above
    tiles = {"in": 256, "out": 128, "gu": 176, "dn": 176}
    big = {kind: _adamw_layers(parts[kind], big_w[kind], big_m[kind], big_v[kind], tiles[kind], "adamw_" + kind)[0]
           for kind in WEIGHT_KINDS}

    small_m = (m_pre_mix_norm, m_post_mix_norm, m_pre_ffn_norm, m_post_ffn_norm, m_attn_out_norm, m_conv_out_norm)
    small_v = (v_pre_mix_norm, v_post_mix_norm, v_pre_ffn_norm, v_post_ffn_norm, v_attn_out_norm, v_conv_out_norm)
    per_gain, taps_grad_full, loss = _adamw_small(gathered, [all_gains[name] for name in GAIN_NAMES], small_m, small_v)
    small = [{name: per_gain[k][o] for k, name in enumerate(GAIN_NAMES)} for o in range(4)]
    loss = loss[0, 0]
    taps_grad = lax.dynamic_slice_in_dim(taps_grad_full, me * conv_channels, conv_channels, axis=2)

    taps = [taps_grad, *_adamw_plain(taps_grad, conv_w, m_conv_w, v_conv_w, "adamw_taps")]

    def leaves(o):
        s = small[o]
        return (s["pre_mix_norm"], big["in"][o], taps[o], s["attn_out_norm"], s["conv_out_norm"], big["out"][o], s["post_mix_norm"],
                s["pre_ffn_norm"], hidden_major(big["gu"][o]), big["dn"][o], s["post_ffn_norm"])

    return (loss, dx[None], *leaves(0), *leaves(1), *leaves(2), *leaves(3))
```

```python
import math

import jax
import jax.numpy as jnp
from jax import lax
from jax.experimental import pallas as pl
from jax.experimental.pallas import tpu as pltpu

F32 = jnp.float32
BF16 = jnp.bfloat16
MESH = pl.DeviceIdType.MESH

SEQ = 4096
D_MODEL = 1024
DEPTH = 4
N_DEV = 8
ATTN_WIDTH = 512
IN_PROJ_WIDTH = 3072
FFN_HIDDEN = 2816
FFN_BLOCK = 2 * FFN_HIDDEN // N_DEV
W_IN_BLOCK = IN_PROJ_WIDTH // N_DEV
W_OUT_BLOCK = D_MODEL // N_DEV
W_DOWN_BLOCK = FFN_HIDDEN // N_DEV
HEAD_DIM = 64
ROPE_DIM = 16
ROPE_THETA = 500000.0
DILATIONS = (1, 4, 16)
HALF_WINDOW = 64
RMS_EPS = 1e-6
NEG_INF = -1e30
LANES = 128
Q_BLOCK = 128
K_WINDOW = Q_BLOCK + 2 * HALF_WINDOW
PERM_CHUNK = 256
ROW_TILE = 512
FFN_TILE = 256
WGRAD_COLS = 512
ADAM_LR, ADAM_B1, ADAM_B2, ADAM_EPS, ADAM_WD, ADAM_STEP = 0.001, 0.9, 0.999, 1e-08, 0.01, 10
MIB = 1024 * 1024
PINNED_BYTES = 256 * 1024

WEIGHT_KINDS = ("in", "out", "gu", "dn")
FULL_SHAPES = {"in": (D_MODEL, IN_PROJ_WIDTH), "out": (D_MODEL, D_MODEL), "gu": (N_DEV, FFN_BLOCK, D_MODEL), "dn": (FFN_HIDDEN, D_MODEL)}
SHARD_SHAPES = {"in": (D_MODEL, W_IN_BLOCK), "out": (W_OUT_BLOCK, D_MODEL), "gu": (FFN_BLOCK, D_MODEL), "dn": (W_DOWN_BLOCK, D_MODEL)}
ANY = pl.BlockSpec(memory_space=pl.ANY)


def _sds(shape, dtype):
    return jax.ShapeDtypeStruct(shape, dtype)


def _rows(width, tile=ROW_TILE):
    return pl.BlockSpec((tile, width), lambda i: (i, 0))


def _frows(width):
    return _rows(width, FFN_TILE)


def _cols(height, tile):
    return pl.BlockSpec((height, tile), lambda i: (0, i))


def _const(shape):
    return pl.BlockSpec(shape, lambda i: (0,) * len(shape))


def _const1(shape):
    return pl.BlockSpec(shape, lambda i: (0,) * len(shape), pipeline_mode=pl.Buffered(1))


HALO_ROWS = 16


def _halo_prev(width, tile=ROW_TILE):
    return pl.BlockSpec((HALO_ROWS, width), lambda i: (jnp.maximum(i * (tile // HALO_ROWS) - 1, 0), 0))


def _halo_next(width, tile=ROW_TILE):
    return pl.BlockSpec((HALO_ROWS, width), lambda i: (jnp.minimum((i + 1) * (tile // HALO_ROWS), SEQ // HALO_ROWS - 1), 0))


def _rms(x):
    r = lax.rsqrt(jnp.mean(x * x, axis=-1, keepdims=True) + RMS_EPS)
    return x * r, r


def _rms_bwd(dn, y, r):
    return r * (dn - y * jnp.mean(dn * y, axis=-1, keepdims=True))


def _dot(a, b):
    return jnp.dot(a, b, preferred_element_type=F32)


def _dot_nt(a, b):
    return lax.dot_general(a, b, (((1,), (1,)), ((), ())), preferred_element_type=F32)


def _dot_tn(a, b):
    return lax.dot_general(a, b, (((0,), (0,)), ((), ())), preferred_element_type=F32)


def _place():
    return lax.axis_index("x"), lax.axis_index("y"), lax.axis_index("c")


def _block_of(px, py, pc):
    return 4 * px + 2 * py + pc


def _weight_block(ref, kind, blk):
    if kind == "in":
        return ref.at[:, pl.ds(blk * W_IN_BLOCK, W_IN_BLOCK)]
    if kind == "out":
        return ref.at[pl.ds(blk * W_OUT_BLOCK, W_OUT_BLOCK), :]
    if kind == "gu":
        return ref.at[blk]
    return ref.at[pl.ds(blk * W_DOWN_BLOCK, W_DOWN_BLOCK), :]


def _dma_semaphores(n):
    return [pltpu.SemaphoreType.DMA((n, 7)), pltpu.SemaphoreType.DMA((n, 7)), pltpu.SemaphoreType.DMA((n,))]


class _Gather:
    def __init__(self, kinds, shards):
        self.kinds, self.operands = tuple(kinds), list(shards)
        self.tag = "gather_" + "_".join(kinds)
        self.out_shape = [_sds(FULL_SHAPES[k], BF16) for k in kinds]
        self.scratch = _dma_semaphores(len(kinds))

    def _parties(self):
        x, y, c = _place()
        return (x, y, c), (x, y, 1 - c), [(1 - x, y), (x, 1 - y), (1 - x, 1 - y)], c

    def _copy(self, outs, sems, w, k, block, to, src=None):
        dst = _weight_block(outs[w], self.kinds[w], _block_of(*block))
        return pltpu.make_async_remote_copy(src_ref=dst if src is None else src, dst_ref=dst, send_sem=sems[0].at[w, k],
                                            recv_sem=sems[1].at[w, k], device_id=to, device_id_type=MESH)

    def _own(self, srcs, outs, sems, w, me):
        return pltpu.make_async_copy(srcs[w], _weight_block(outs[w], self.kinds[w], _block_of(*me)), sems[2].at[w])

    def _first(self, srcs, outs, sems, w):
        me, sibling, chips, c = self._parties()
        return [self._copy(outs, sems, w, 0, me, sibling, src=srcs[w])] + [
            self._copy(outs, sems, w, 1 + j, me, (*chip, c), src=srcs[w]) for j, chip in enumerate(chips)]

    def start(self, srcs, outs, sems):
        me = self._parties()[0]
        for w in range(len(self.kinds)):
            self._own(srcs, outs, sems, w, me).start()
            for cp in self._first(srcs, outs, sems, w):
                cp.start()

    def forward(self, srcs, outs, sems):
        me, sibling, chips, c = self._parties()
        for j, chip in enumerate(chips):
            for w in range(len(self.kinds)):
                self._copy(outs, sems, w, 1 + j, (*chip, c), me).wait_recv()
                self._copy(outs, sems, w, 4 + j, (*chip, c), sibling).start()

    def finish(self, srcs, outs, sems):
        me, sibling, chips, c = self._parties()
        for w in range(len(self.kinds)):
            self._copy(outs, sems, w, 0, sibling, me).wait_recv()
            for j, chip in enumerate(chips):
                self._copy(outs, sems, w, 4 + j, (*chip, 1 - c), me).wait_recv()
        for w in range(len(self.kinds)):
            for cp in self._first(srcs, outs, sems, w):
                cp.wait_send()
            for j, chip in enumerate(chips):
                self._copy(outs, sems, w, 4 + j, (*chip, c), sibling).wait_send()
            self._own(srcs, outs, sems, w, me).wait()


def _peers(x, y, c):
    return [(x ^ a, y ^ b, c ^ e) for a in (0, 1) for b in (0, 1) for e in (0, 1) if (a, b, e) != (0, 0, 0)]


class _Scatter:
    def __init__(self, kinds, grads):
        self.kinds, self.operands = tuple(kinds), list(grads)
        self.tag = "scatter_" + "_".join(kinds)
        self.out_shape = [_sds((N_DEV,) + SHARD_SHAPES[k], BF16) for k in kinds]
        self.scratch = _dma_semaphores(len(kinds))

    def _copies(self, srcs, outs, sems):
        x, y, c = _place()
        me = _block_of(x, y, c)
        copies = []
        for w, kind in enumerate(self.kinds):
            copies.append(pltpu.make_async_copy(_weight_block(srcs[w], kind, me), outs[w].at[me], sems[2].at[w]))
            for k, peer in enumerate(_peers(x, y, c)):
                copies.append(pltpu.make_async_remote_copy(
                    src_ref=_weight_block(srcs[w], kind, _block_of(*peer)), dst_ref=outs[w].at[me],
                    send_sem=sems[0].at[w, k], recv_sem=sems[1].at[w, k], device_id=peer, device_id_type=MESH))
        return copies

    def start(self, srcs, outs, sems):
        for cp in self._copies(srcs, outs, sems):
            cp.start()

    def forward(self, srcs, outs, sems):
        pass

    def finish(self, srcs, outs, sems):
        for cp in self._copies(srcs, outs, sems):
            cp.wait()


N_CHIPS = N_DEV // 2


class _ScatterChips:
    def __init__(self, kinds, blocks):
        self.kinds, self.operands = tuple(kinds), list(blocks)
        self.tag = "scatter_chips_" + "_".join(kinds)
        self.out_shape = [_sds((N_CHIPS,) + SHARD_SHAPES[k], BF16) for k in kinds]
        self.scratch = _dma_semaphores(len(kinds))

    def _copies(self, srcs, outs, sems):
        x, y, c = _place()
        mine = 2 * x + y
        copies = []
        for w in range(len(self.kinds)):
            copies.append(pltpu.make_async_copy(srcs[w].at[mine], outs[w].at[mine], sems[2].at[w]))
            for j, (px, py) in enumerate([(1 - x, y), (x, 1 - y), (1 - x, 1 - y)]):
                copies.append(pltpu.make_async_remote_copy(
                    src_ref=srcs[w].at[2 * px + py], dst_ref=outs[w].at[mine], send_sem=sems[0].at[w, j], recv_sem=sems[1].at[w, j],
                    device_id=(px, py, c), device_id_type=MESH))
        return copies

    def start(self, srcs, outs, sems):
        for cp in self._copies(srcs, outs, sems):
            cp.start()

    def forward(self, srcs, outs, sems):
        pass

    def finish(self, srcs, outs, sems):
        for cp in self._copies(srcs, outs, sems):
            cp.wait()


def _in_hbm(a):
    return pltpu.with_memory_space_constraint(a, pltpu.HBM) if a.size * a.dtype.itemsize >= PINNED_BYTES else a


def _out_hbm(s):
    return pltpu.HBM(s.shape, s.dtype) if math.prod(s.shape) * jnp.dtype(s.dtype).itemsize >= PINNED_BYTES else s


def _call(body, args, *, grid, in_specs, out_specs, out_shape, scratch_shapes=(), vmem_mib=None, name, comm=None):
    kwargs = {} if vmem_mib is None else dict(compiler_params=pltpu.CompilerParams(vmem_limit_bytes=vmem_mib * MIB))
    in_specs, out_specs, out_shape, scratch_shapes = list(in_specs), list(out_specs), list(out_shape), list(scratch_shapes)
    args = [_in_hbm(a) for a in args]
    out_shape = [_out_hbm(s) for s in out_shape]
    if comm is None:
        res = pl.pallas_call(body, grid=grid, in_specs=in_specs, out_specs=out_specs, out_shape=out_shape,
                             scratch_shapes=scratch_shapes, name=name, **kwargs)(*args)
        return list(res), None
    n_in, n_out, n_scr = len(in_specs), len(out_specs), len(scratch_shapes)
    c_in, c_out = len(comm.operands), len(comm.out_shape)
    last = math.prod(grid) - 1

    def carried(*refs):
        cuts = [n_in, c_in, n_out, c_out, n_scr]
        parts, at = [], 0
        for n in cuts:
            parts.append(refs[at:at + n])
            at += n
        ins, c_ins, outs, c_outs, scr = parts
        sems = refs[at:]
        step = pl.program_id(0)
        for axis in range(1, len(grid)):
            step = step * grid[axis] + pl.program_id(axis)

        @pl.when(step == 0)
        def _():
            comm.start(c_ins, c_outs, sems)

        @pl.when(step == last)
        def _():
            comm.forward(c_ins, c_outs, sems)

        body(*ins, *outs, *scr)

        @pl.when(step == last)
        def _():
            comm.finish(c_ins, c_outs, sems)

    res = pl.pallas_call(carried, grid=grid, in_specs=in_specs + [ANY] * c_in, out_specs=out_specs + [ANY] * c_out,
                         out_shape=out_shape + [_out_hbm(s) for s in comm.out_shape], scratch_shapes=scratch_shapes + comm.scratch,
                         name=name + "_" + comm.tag, **kwargs)(*args, *[_in_hbm(a) for a in comm.operands])
    return list(res[:n_out]), list(res[n_out:])


class _GatherSmall:
    def __init__(self, v, tag):
        self.operands, self.tag = [v], tag
        self.out_shape = [_sds((N_DEV,) + v.shape, F32)]
        self.scratch = _dma_semaphores(1)

    def _copies(self, srcs, outs, sems):
        x, y, c = _place()
        me = _block_of(x, y, c)
        copies = [pltpu.make_async_copy(srcs[0], outs[0].at[me], sems[2].at[0])]
        for k, peer in enumerate(_peers(x, y, c)):
            copies.append(pltpu.make_async_remote_copy(src_ref=srcs[0], dst_ref=outs[0].at[me], send_sem=sems[0].at[0, k],
                                                       recv_sem=sems[1].at[0, k], device_id=peer, device_id_type=MESH))
        return copies

    def start(self, srcs, outs, sems):
        for cp in self._copies(srcs, outs, sems):
            cp.start()

    def forward(self, srcs, outs, sems):
        pass

    def finish(self, srcs, outs, sems):
        for cp in self._copies(srcs, outs, sems):
            cp.wait()


class _Together:
    def __init__(self, *comms):
        self.comms = comms
        self.operands = [a for cm in comms for a in cm.operands]
        self.out_shape = [s for cm in comms for s in cm.out_shape]
        self.scratch = [s for cm in comms for s in cm.scratch]
        self.tag = "_".join(cm.tag for cm in comms)
        self.kinds = tuple(kind for cm in comms for kind in getattr(cm, "kinds", ()))

    def _each(self, srcs, outs, sems):
        a = o = s = 0
        for cm in self.comms:
            na, no, ns = len(cm.operands), len(cm.out_shape), len(cm.scratch)
            yield cm, srcs[a:a + na], outs[o:o + no], sems[s:s + ns]
            a, o, s = a + na, o + no, s + ns

    def start(self, srcs, outs, sems):
        for cm, *refs in self._each(srcs, outs, sems):
            cm.start(*refs)

    def forward(self, srcs, outs, sems):
        for cm, *refs in self._each(srcs, outs, sems):
            cm.forward(*refs)

    def finish(self, srcs, outs, sems):
        for cm, *refs in self._each(srcs, outs, sems):
            cm.finish(*refs)


def _rope_tables(pos_col, freq_row, comm=None):
    def body(p_ref, f_ref, c_ref, sa_ref, sb_ref):
        ang = p_ref[...].astype(F32) * f_ref[...]
        lane = lax.broadcasted_iota(jnp.int32, ang.shape, 1) % HEAD_DIM
        cos, sin = jnp.cos(ang), jnp.sin(ang)
        c_ref[...] = jnp.where(lane < ROPE_DIM, cos, 1.0)
        sa_ref[...] = jnp.where(lane < ROPE_DIM // 2, -sin, 0.0)
        sb_ref[...] = jnp.where((lane >= ROPE_DIM // 2) & (lane < ROPE_DIM), sin, 0.0)

    return _call(body, (pos_col, freq_row), grid=(SEQ // ROW_TILE,), in_specs=[_rows(1), _const((1, LANES))],
                 out_specs=[_rows(LANES)] * 3, out_shape=[_sds((SEQ, LANES), F32)] * 3, name="rope_tables", comm=comm)


def _rotate(t, c, sa, sb):
    parts = []
    for g in range(ATTN_WIDTH // LANES):
        tg = t[:, g * LANES:(g + 1) * LANES]
        parts.append(tg * c + pltpu.roll(tg, LANES - 8, axis=1) * sa + pltpu.roll(tg, 8, axis=1) * sb)
    return jnp.concatenate(parts, axis=1)


def _rotate_transposed(dt, c, sa, sb):
    parts = []
    for g in range(ATTN_WIDTH // LANES):
        dg = dt[:, g * LANES:(g + 1) * LANES]
        parts.append(dg * c + pltpu.roll(dg * sa, 8, axis=1) + pltpu.roll(dg * sb, LANES - 8, axis=1))
    return jnp.concatenate(parts, axis=1)


def _inproj_fwd(x, g_pre, w_in, tc, tsa, tsb, comm=None):
    def body(x_ref, g_ref, w_ref, c_ref, sa_ref, sb_ref, q_ref, k_ref, v_ref, u_ref, b_ref, cc_ref):
        y, _ = _rms(x_ref[...])
        h = (y * g_ref[...]).astype(BF16)

        def proj(n):
            return _dot(h, w_ref[:, n * ATTN_WIDTH:(n + 1) * ATTN_WIDTH])

        c, sa, sb = c_ref[...], sa_ref[...], sb_ref[...]
        q_ref[...] = (_rotate(proj(0), c, sa, sb) * (HEAD_DIM ** -0.5)).astype(BF16)
        k_ref[...] = _rotate(proj(1), c, sa, sb).astype(BF16)
        v_ref[...] = proj(2).astype(BF16)
        u_ref[...] = proj(3).astype(BF16)
        b_ref[...] = proj(4).astype(BF16)
        cc_ref[...] = proj(5).astype(BF16)

    a = ATTN_WIDTH
    return _call(
        body, (x, g_pre, w_in, tc, tsa, tsb), grid=(SEQ // ROW_TILE,),
        in_specs=[_rows(D_MODEL), _const((1, D_MODEL)), _const1((D_MODEL, IN_PROJ_WIDTH)), _rows(LANES), _rows(LANES), _rows(LANES)],
        out_specs=[_rows(a)] * 6, out_shape=[_sds((SEQ, a), BF16)] * 6,
        vmem_mib=40, name="inproj_fwd", comm=comm)


def _head_masks():
    lane = lax.broadcasted_iota(jnp.int32, (1, LANES), 1)
    first = lane < HEAD_DIM
    return first, first.astype(F32), 1.0 - first.astype(F32)


def _perm_chunks(dil):
    length = SEQ // dil
    out = []
    for r in range(dil):
        for c0 in range(0, length, PERM_CHUNK):
            chunk = (r * length + c0) // PERM_CHUNK
            rows = pl.ds(c0, PERM_CHUNK) if dil == 1 else pl.ds(r + dil * c0, PERM_CHUNK, stride=dil)
            out.append((chunk, rows))
    return out


def _chunk(c, offset=0):
    return pl.ds(offset + c * PERM_CHUNK, PERM_CHUNK)


def _write_band_bias(bias_ref):
    rr = lax.broadcasted_iota(jnp.int32, (Q_BLOCK, K_WINDOW), 0)
    cc = lax.broadcasted_iota(jnp.int32, (Q_BLOCK, K_WINDOW), 1)
    band = (cc >= rr) & (cc - rr <= 2 * HALF_WINDOW)
    bias_ref[0] = jnp.where(band, 0.0, NEG_INF)
    bias_ref[1] = jnp.where(band & (cc >= HALF_WINDOW), 0.0, NEG_INF)
    bias_ref[2] = jnp.where(band & (cc < Q_BLOCK + HALF_WINDOW), 0.0, NEG_INF)


def _band_bias_index(m0, length):
    return jnp.where(m0 % length == 0, 1, 0) + jnp.where((m0 + Q_BLOCK) % length == 0, 2, 0)


def _zero_key_padding(bufs):
    pad = jnp.zeros((HALF_WINDOW, LANES), BF16)
    for buf in bufs:
        buf[pl.ds(0, HALF_WINDOW), :] = pad
        buf[pl.ds(SEQ + HALF_WINDOW, HALF_WINDOW), :] = pad


def _attn_fwd(q, k, v, comm=None):
    group = 8

    def body(q_ref, k_ref, v_ref, o_ref, lse_ref, q32, k32, v32, qa, qb, kp, vp, accp, mlp,
             acc0, acc1, acc2, ml0, ml1, ml2, bias_ref):
        first, mask_a, mask_b = _head_masks()
        low = lax.broadcasted_iota(jnp.int32, (1, LANES), 1) % HEAD_DIM < HEAD_DIM // 2
        _zero_key_padding((kp, vp))
        _write_band_bias(bias_ref)
        q32[...] = q_ref[...].astype(F32)
        k32[...] = k_ref[...].astype(F32)
        v32[...] = v_ref[...].astype(F32)
        natural = ((acc0, ml0), (acc1, ml1), (acc2, ml2))

        for branch, dil in enumerate(DILATIONS):
            length = SEQ // dil
            assert length >= 2 * Q_BLOCK
            chunks = _perm_chunks(dil)
            for c, rows in chunks:
                val = q32[rows, :]
                qa[_chunk(c), :] = (val * mask_a).astype(BF16)
                qb[_chunk(c), :] = (val * mask_b).astype(BF16)
                kp[_chunk(c, HALF_WINDOW), :] = k32[rows, :].astype(BF16)
                vp[_chunk(c, HALF_WINDOW), :] = v32[rows, :].astype(BF16)
            acc_dst, ml_dst = natural[branch] if dil == 1 else (accp, mlp)

            def blocks(i, carry, length=length, acc_dst=acc_dst, ml_dst=ml_dst):
                base = pl.multiple_of(i * (group * Q_BLOCK), group * Q_BLOCK)
                starts = [base + g * Q_BLOCK for g in range(group)]
                scores = [[_dot_nt(qx[pl.ds(m0, Q_BLOCK), :], kp[pl.ds(m0, K_WINDOW), :]) for qx in (qa, qb)] for m0 in starts]
                probs = []
                for m0, pair in zip(starts, scores):
                    bias = bias_ref[_band_bias_index(m0, length)]
                    stats, ps = [], []
                    for s in pair:
                        s = s + bias
                        m = jnp.max(s, axis=1, keepdims=True)
                        p = jnp.exp(s - m)
                        stats.append(jnp.where(low, m, jnp.sum(p, axis=1, keepdims=True)))
                        ps.append(p.astype(BF16))
                    ml_dst[pl.ds(m0, Q_BLOCK), :] = jnp.where(first, stats[0], stats[1])
                    probs.append(ps)
                for m0, ps in zip(starts, probs):
                    vw = vp[pl.ds(m0, K_WINDOW), :]
                    acc_dst[pl.ds(m0, Q_BLOCK), :] = jnp.where(first, _dot(ps[0], vw), _dot(ps[1], vw))
                return carry

            lax.fori_loop(0, SEQ // (group * Q_BLOCK), blocks, 0)

            if dil > 1:
                for c, rows in chunks:
                    natural[branch][0][rows, :] = accp[_chunk(c), :]
                    natural[branch][1][rows, :] = mlp[_chunk(c), :]

        for c in range(SEQ // PERM_CHUNK):
            packed = [ml[_chunk(c), :] for _, ml in natural]
            ms = [jnp.where(low, ml, pltpu.roll(ml, HEAD_DIM // 2, axis=1)) for ml in packed]
            ls = [jnp.where(low, pltpu.roll(ml, LANES - HEAD_DIM // 2, axis=1), ml) for ml in packed]
            m_all = jnp.maximum(jnp.maximum(ms[0], ms[1]), ms[2])
            es = [jnp.exp(m - m_all) for m in ms]
            z = ls[0] * es[0] + ls[1] * es[1] + ls[2] * es[2]
            num = natural[0][0][_chunk(c), :] * es[0] + natural[1][0][_chunk(c), :] * es[1] + natural[2][0][_chunk(c), :] * es[2]
            o_ref[_chunk(c), :] = num / z
            lse_ref[_chunk(c), :] = m_all + jnp.log(z)

    col = pl.BlockSpec((SEQ, LANES), lambda h: (0, h))
    padded = SEQ + 2 * HALF_WINDOW
    return _call(
        body, (q, k, v), grid=(ATTN_WIDTH // LANES,), in_specs=[col] * 3, out_specs=[col] * 2,
        out_shape=[_sds((SEQ, ATTN_WIDTH), F32)] * 2,
        scratch_shapes=[pltpu.VMEM((SEQ, LANES), F32)] * 3 + [pltpu.VMEM((SEQ, LANES), BF16)] * 2
        + [pltpu.VMEM((padded, LANES), BF16)] * 2 + [pltpu.VMEM((SEQ, LANES), F32)] * 8
        + [pltpu.VMEM((3, Q_BLOCK, K_WINDOW), F32)],
        vmem_mib=52, name="attn_fwd", comm=comm)


def _shifted(t, before, after, i):
    tile = t.shape[0]
    row = lax.broadcasted_iota(jnp.int32, (tile, 1), 0)
    before = jnp.where(i > 0, before, 0.0)
    after = jnp.where(i < SEQ // tile - 1, after, 0.0)
    return (jnp.where(row == 0, before, pltpu.roll(t, 1, axis=0)),
            jnp.where(row == tile - 1, after, pltpu.roll(t, tile - 1, axis=0)))


def _last_row(ref):
    return ref[HALO_ROWS - 1:HALO_ROWS, :].astype(F32)


def _first_row(ref):
    return ref[0:1, :].astype(F32)


def _conv_parts(u, c, u_prev, c_prev, u_next, c_next, cw, i):
    t = c * u
    t_prev, t_next = _shifted(t, _last_row(c_prev) * _last_row(u_prev), _first_row(c_next) * _first_row(u_next), i)
    s = cw[0:1, :] * t_prev + cw[1:2, :] * t + cw[2:3, :] * t_next
    return t, t_prev, t_next, s


def _mix_fwd(attn, u, b, c, conv_w, g_attn, g_conv, w_out, x, g_post, comm=None):
    def body(a_ref, u_ref, b_ref, c_ref, up_ref, cp_ref, un_ref, cn_ref, cw_ref, ga_ref, gc_ref, w_ref, x_ref, gp_ref,
             x1_ref, mg_ref, mix_ref):
        i = pl.program_id(0)
        _, _, _, s = _conv_parts(u_ref[...].astype(F32), c_ref[...].astype(F32), up_ref, cp_ref, un_ref, cn_ref, cw_ref[...], i)
        ya, _ = _rms(a_ref[...])
        yc, _ = _rms(b_ref[...].astype(F32) * s)
        merged = jnp.concatenate([ya * ga_ref[...], yc * gc_ref[...]], axis=1).astype(BF16)
        mix = _dot(merged, w_ref[...])
        ym, _ = _rms(mix)
        mg_ref[...] = merged.T
        mix_ref[...] = mix.astype(BF16)
        x1_ref[...] = x_ref[...] + ym * gp_ref[...]

    a = ATTN_WIDTH
    return _call(
        body, (attn, u, b, c, u, c, u, c, conv_w, g_attn, g_conv, w_out, x, g_post), grid=(SEQ // ROW_TILE,),
        in_specs=[_rows(a)] * 4 + [_halo_prev(a)] * 2 + [_halo_next(a)] * 2
        + [_const((3, a)), _const((1, a)), _const((1, a)), _const1((D_MODEL, D_MODEL)), _rows(D_MODEL), _const((1, D_MODEL))],
        out_specs=[_rows(D_MODEL), _cols(D_MODEL, ROW_TILE), _rows(D_MODEL)],
        out_shape=[_sds((SEQ, D_MODEL), F32), _sds((D_MODEL, SEQ), BF16), _sds((SEQ, D_MODEL), BF16)],
        vmem_mib=40, name="mix_fwd", comm=comm)


def _gu_spec():
    return pl.BlockSpec((N_DEV, FFN_TILE, FFN_BLOCK), lambda i: (0, i, 0))


def _ffn_fwd(x1, g_pre, w_gu, w_dn, g_post, comm=None, target=None):
    n_tiles = SEQ // ROW_TILE

    def body(*refs):
        if target is None:
            x_ref, g_ref, wgu_ref, wdn_ref, gp_ref, x2_ref, gu_ref, f_ref = refs
        else:
            x_ref, g_ref, wgu_ref, wdn_ref, gp_ref, t_ref, x2_ref, gu_ref, f_ref, loss_ref, acc = refs
        x1 = x_ref[...]
        y, _ = _rms(x1)
        h = (y * g_ref[...]).astype(BF16)
        f = jnp.zeros((ROW_TILE, D_MODEL), F32)

        def gate_up(j):
            return _dot_nt(h, wgu_ref[j]), _dot_nt(h, wgu_ref[j + N_DEV // 2])

        ahead = gate_up(0)
        for j in range(N_DEV // 2):
            gate, up = ahead
            if j + 1 < N_DEV // 2:
                ahead = gate_up(j + 1)
            gu_ref[j] = gate.astype(BF16)
            gu_ref[j + N_DEV // 2] = up.astype(BF16)
            act = (gate * jax.nn.sigmoid(gate) * up).astype(BF16)
            f = f + _dot(act, wdn_ref[pl.ds(j * FFN_BLOCK, FFN_BLOCK), :])
        yf, _ = _rms(f)
        f_ref[...] = f
        x2 = x1 + yf * gp_ref[...]
        if target is None:
            x2_ref[...] = x2
        else:
            i = pl.program_id(0)
            err = x2 - t_ref[...]
            x2_ref[...] = err * (1.0 / D_MODEL)

            @pl.when(i == 0)
            def _():
                acc[...] = jnp.zeros_like(acc)

            acc[...] += jnp.sum(err * err, axis=0, keepdims=True)

            @pl.when(i == n_tiles - 1)
            def _():
                loss_ref[...] = jnp.sum(acc[...], axis=1, keepdims=True) * (0.5 / D_MODEL)

    with_loss = target is not None
    return _call(
        body, (x1, g_pre, w_gu, w_dn, g_post) + ((target,) if with_loss else ()), grid=(n_tiles,),
        in_specs=[_rows(D_MODEL), _const((1, D_MODEL)), _const1((N_DEV, FFN_BLOCK, D_MODEL)), _const1((FFN_HIDDEN, D_MODEL)),
                  _const((1, D_MODEL))] + ([_rows(D_MODEL)] if with_loss else []),
        out_specs=[_rows(D_MODEL), pl.BlockSpec((N_DEV, ROW_TILE, FFN_BLOCK), lambda i: (0, i, 0)), _rows(D_MODEL)]
        + ([_const((1, 1))] if with_loss else []),
        out_shape=[_sds((SEQ, D_MODEL), F32), _sds((N_DEV, SEQ, FFN_BLOCK), BF16), _sds((SEQ, D_MODEL), F32)]
        + ([_sds((1, 1), F32)] if with_loss else []),
        scratch_shapes=[pltpu.VMEM((1, D_MODEL), F32)] if with_loss else [],
        vmem_mib=58, name="ffn_fwd_loss" if with_loss else "ffn_fwd", comm=comm)


def _accumulate(pairs, i):
    @pl.when(i == 0)
    def _():
        for ref, value in pairs:
            ref[...] = value

    @pl.when(i > 0)
    def _():
        for ref, value in pairs:
            ref[...] += value


def _colsum(v):
    return jnp.sum(v, axis=0, keepdims=True)


def _ffn_bwd(dx2, f, x1, gu, w_gu, w_dn, g_post, g_pre, comm=None):
    half = N_DEV // 2

    def body(dx2_ref, f_ref, x1_ref, gu_ref, wgu_ref, wdn_ref, gpost_ref, gpre_ref,
             dx1_ref, df_ref, act_ref, dgu_ref, h_ref, dgpost_ref, dgpre_ref):
        i = pl.program_id(0)
        dx2 = dx2_ref[...]
        yf, rf = _rms(f_ref[...])
        dg_post = _colsum(dx2 * yf)
        df = _rms_bwd(dx2 * gpost_ref[...], yf, rf).astype(BF16)
        df_ref[...] = df
        dh = jnp.zeros((FFN_TILE, D_MODEL), F32)

        def d_act(j):
            return _dot_nt(df, wdn_ref[pl.ds(j * FFN_BLOCK, FFN_BLOCK), :])

        ahead = d_act(0)
        for j in range(half):
            dact = ahead
            if j + 1 < half:
                ahead = d_act(j + 1)
            gate = gu_ref[j].astype(F32)
            up = gu_ref[j + half].astype(F32)
            sig = jax.nn.sigmoid(gate)
            silu = gate * sig
            act_ref[j] = (silu * up).astype(BF16)
            dgate = (dact * up * (sig * (1.0 + gate * (1.0 - sig)))).astype(BF16)
            dup = (dact * silu).astype(BF16)
            dgu_ref[j] = dgate
            dgu_ref[j + half] = dup
            dh = dh + _dot(dgate, wgu_ref[j]) + _dot(dup, wgu_ref[j + half])
        y1, r1 = _rms(x1_ref[...])
        h_ref[...] = (y1 * gpre_ref[...]).astype(BF16)
        dx1_ref[...] = dx2 + _rms_bwd(dh * gpre_ref[...], y1, r1)
        _accumulate([(dgpost_ref, dg_post), (dgpre_ref, _colsum(dh * y1))], i)

    act_spec = pl.BlockSpec((half, FFN_TILE, FFN_BLOCK), lambda i: (0, i, 0))
    return _call(
        body, (dx2, f, x1, gu, w_gu, w_dn, g_post, g_pre), grid=(SEQ // FFN_TILE,),
        in_specs=[_frows(D_MODEL)] * 3 + [_gu_spec(), _const1((N_DEV, FFN_BLOCK, D_MODEL)), _const1((FFN_HIDDEN, D_MODEL)),
                                          _const((1, D_MODEL)), _const((1, D_MODEL))],
        out_specs=[_frows(D_MODEL), _frows(D_MODEL), act_spec, _gu_spec(), _frows(D_MODEL), _const((1, D_MODEL)), _const((1, D_MODEL))],
        out_shape=[_sds((SEQ, D_MODEL), F32), _sds((SEQ, D_MODEL), BF16), _sds((half, SEQ, FFN_BLOCK), BF16),
                   _sds((N_DEV, SEQ, FFN_BLOCK), BF16), _sds((SEQ, D_MODEL), BF16), _sds((1, D_MODEL), F32), _sds((1, D_MODEL), F32)],
        vmem_mib=52, name="ffn_bwd", comm=comm)


def _wgrad(a_t, g, name):
    width = g.shape[1]

    def body(a_ref, g_ref, o_ref):
        o_ref[...] = _dot(a_ref[...], g_ref[...]).astype(BF16)

    return _call(body, (a_t, g), grid=(width // WGRAD_COLS,),
                 in_specs=[_const1((D_MODEL, SEQ)), pl.BlockSpec((SEQ, WGRAD_COLS), lambda j: (0, j))],
                 out_specs=[pl.BlockSpec((D_MODEL, WGRAD_COLS), lambda j: (0, j))], out_shape=[_sds((D_MODEL, width), BF16)],
                 vmem_mib=48, name=name)[0][0]


def _wgrad_paired(a, g, kind, to_owners=False, comm=None):
    shard = SHARD_SHAPES[kind]
    per_chip = 2 if kind == "gu" else 1
    n_compute = per_chip * N_CHIPS
    n_steps = n_compute + 1

    def chip_of(turn):
        if not to_owners:
            return turn
        return (2 * lax.axis_index("x") + lax.axis_index("y") + 1 + turn) % N_CHIPS

    def block(t):
        return jnp.minimum(t, n_compute - 1)

    if kind == "in":
        specs = [_const1((D_MODEL, SEQ)), pl.BlockSpec((SEQ, 2 * W_IN_BLOCK), lambda t: (0, chip_of(block(t))))]
    else:
        specs = [pl.BlockSpec((1, SEQ, FFN_BLOCK), lambda t: (block(t), 0, 0)), _const1((SEQ, D_MODEL))]

    def body(a_ref, g_ref, o_ref, *rest):
        if to_owners:
            landing, sendbuf, recvbuf, keepbuf, send_sem, recv_sem, donebuf, owner_send, owner_recv, own_sem = rest
        else:
            sendbuf, recvbuf, keepbuf, send_sem, recv_sem = rest
        t = pl.program_id(0)
        x, y, c = _place()

        def exchange(q):
            return pltpu.make_async_remote_copy(src_ref=sendbuf.at[q], dst_ref=recvbuf.at[q], send_sem=send_sem.at[q],
                                                recv_sem=recv_sem.at[q], device_id=(x, y, 1 - c), device_id_type=MESH)

        my_chip = 2 * x + y

        def to_owner(q):
            return pltpu.make_async_remote_copy(src_ref=donebuf.at[q], dst_ref=landing.at[my_chip], send_sem=owner_send.at[q],
                                                recv_sem=owner_recv.at[my_chip], device_id=(q // 2, q % 2, c), device_id_type=MESH)

        def keep_own():
            return pltpu.make_async_copy(donebuf.at[my_chip], landing.at[my_chip], own_sem)

        @pl.when((t >= per_chip) & (t % per_chip == 0))
        def _():
            q = chip_of(t // per_chip - 1)
            exchange(q).wait_recv()
            done = (keepbuf[...] + recvbuf[q].astype(F32)).astype(BF16)
            o_ref[0] = done
            if to_owners:
                donebuf[q] = done

                @pl.when(q == my_chip)
                def _():
                    keep_own().start()

                @pl.when(q != my_chip)
                def _():
                    to_owner(q).start()

        @pl.when(t < n_compute)
        def _():
            q = chip_of(t // per_chip)
            if kind == "gu":
                r = _dot_tn(a_ref[0], g_ref[...])

                @pl.when(t % 2 == c)
                def _():
                    keepbuf[...] = r

                @pl.when(t % 2 != c)
                def _():
                    sendbuf[q] = r.astype(BF16)
                    exchange(q).start()
            else:
                if kind == "dn":
                    r = _dot_tn(a_ref[0], g_ref[...])
                    lower, upper = r[:W_DOWN_BLOCK], r[W_DOWN_BLOCK:]
                else:
                    r = _dot(a_ref[...], g_ref[...])
                    lower, upper = r[:, :W_IN_BLOCK], r[:, W_IN_BLOCK:]
                keepbuf[...] = jnp.where(c == 0, lower, upper)
                sendbuf[q] = jnp.where(c == 0, upper, lower).astype(BF16)
                exchange(q).start()

        @pl.when(t == n_steps - 1)
        def _():
            for q in range(N_CHIPS):
                exchange(q).wait_send()
            if to_owners:
                keep_own().wait()
                for s in range(N_CHIPS):
                    @pl.when(s != my_chip)
                    def _(s=s):
                        pltpu.make_async_remote_copy(src_ref=donebuf.at[s], dst_ref=landing.at[s], send_sem=owner_send.at[s],
                                                     recv_sem=owner_recv.at[s], device_id=(x, y, c), device_id_type=MESH).wait()

    slots = (N_CHIPS,) + shard
    owners = int(to_owners)
    res, landed = _call(
        body, (a, g), grid=(n_steps,), in_specs=specs,
        out_specs=[pl.BlockSpec((1,) + shard, lambda t: (chip_of(jnp.clip(t // per_chip - 1, 0, N_CHIPS - 1)), 0, 0))] + [ANY] * owners,
        out_shape=[_sds(slots, BF16)] * (1 + owners),
        scratch_shapes=[pltpu.VMEM(slots, BF16), pltpu.VMEM(slots, BF16), pltpu.VMEM(shard, F32),
                        pltpu.SemaphoreType.DMA((N_CHIPS,)), pltpu.SemaphoreType.DMA((N_CHIPS,))]
        + [pltpu.VMEM(slots, BF16), pltpu.SemaphoreType.DMA((N_CHIPS,)), pltpu.SemaphoreType.DMA((N_CHIPS,)), pltpu.SemaphoreType.DMA] * owners,
        vmem_mib=52, name="wgrad_" + kind + "_paired" + "_to_owners" * owners, comm=comm)
    return res[owners] if comm is None else (res[owners], landed)


def _mix_bwd(dx1, mix, attn, u, b, c, conv_w, g_attn, g_conv, g_post, w_out):
    def body(dx1_ref, mix_ref, a_ref, u_ref, b_ref, c_ref, up_ref, cp_ref, un_ref, cn_ref, cw_ref, ga_ref, gc_ref, gp_ref, w_ref,
             dmix_ref, da_ref, ds_ref, db_ref, dgp_ref, dga_ref, dgc_ref):
        i = pl.program_id(0)
        dx1 = dx1_ref[...]
        ym, rm = _rms(mix_ref[...].astype(F32))
        dg_post = _colsum(dx1 * ym)
        dmix = _rms_bwd(dx1 * gp_ref[...], ym, rm).astype(BF16)
        dmix_ref[...] = dmix
        dmerged = _dot_nt(dmix, w_ref[...])
        dna, dnc = dmerged[:, :ATTN_WIDTH], dmerged[:, ATTN_WIDTH:]
        ya, ra = _rms(a_ref[...])
        da_ref[...] = _rms_bwd(dna * ga_ref[...], ya, ra)
        _, _, _, s = _conv_parts(u_ref[...].astype(F32), c_ref[...].astype(F32), up_ref, cp_ref, un_ref, cn_ref, cw_ref[...], i)
        gate_b = b_ref[...].astype(F32)
        yc, rc = _rms(gate_b * s)
        dy = _rms_bwd(dnc * gc_ref[...], yc, rc)
        db_ref[...] = (dy * s).astype(BF16)
        ds_ref[...] = (dy * gate_b).astype(BF16)
        _accumulate([(dgp_ref, dg_post), (dga_ref, _colsum(dna * ya)), (dgc_ref, _colsum(dnc * yc))], i)

    a = ATTN_WIDTH
    return _call(
        body, (dx1, mix, attn, u, b, c, u, c, u, c, conv_w, g_attn, g_conv, g_post, w_out), grid=(SEQ // ROW_TILE,),
        in_specs=[_rows(D_MODEL)] * 2 + [_rows(a)] * 4 + [_halo_prev(a)] * 2 + [_halo_next(a)] * 2
        + [_const((3, a)), _const((1, a)), _const((1, a)), _const((1, D_MODEL)), _const1((D_MODEL, D_MODEL))],
        out_specs=[_rows(D_MODEL)] + [_rows(a)] * 3 + [_const((1, D_MODEL)), _const((1, a)), _const((1, a))],
        out_shape=[_sds((SEQ, D_MODEL), BF16), _sds((SEQ, a), F32), _sds((SEQ, a), BF16), _sds((SEQ, a), BF16),
                   _sds((1, D_MODEL), F32), _sds((1, a), F32), _sds((1, a), F32)],
        vmem_mib=40, name="mix_bwd")[0]


def _attn_bwd(q, k, v, do, o, lse, comm=None):
    group = 8

    n_pairs = ATTN_WIDTH // LANES

    def body(q_hbm, k_hbm, v_hbm, do_hbm, o_hbm, lse_hbm, dq_ref, dk_ref, dv_ref,
             q_ref, k_ref, v_ref, do_ref, o_ref, lse_ref, in_sem,
             q32, k32, v32, qa, qb, doa, dob, kp, vp, lsep, dlp, dnat, dqp, dkp, dvp, bias_ref):
        pair = pl.program_id(0)
        sources = (q_hbm, k_hbm, v_hbm, do_hbm, o_hbm, lse_hbm)
        buffers = (q_ref, k_ref, v_ref, do_ref, o_ref, lse_ref)

        def fetch(i, p):
            return pltpu.make_async_copy(sources[i].at[:, pl.ds(pl.multiple_of(p * LANES, LANES), LANES)], buffers[i], in_sem.at[i])

        def prefetch(*which):
            @pl.when(pair + 1 < n_pairs)
            def _():
                for i in which:
                    fetch(i, pair + 1).start()

        @pl.when(pair == 0)
        def _():
            for i in range(len(sources)):
                fetch(i, 0).start()

        for i in range(len(sources)):
            fetch(i, pair).wait()

        first, mask_a, mask_b = _head_masks()
        _zero_key_padding((kp, vp))
        _write_band_bias(bias_ref)
        q32[...] = q_ref[...].astype(F32)
        k32[...] = k_ref[...].astype(F32)
        v32[...] = v_ref[...].astype(F32)
        prefetch(0, 1, 2)
        for c in range(SEQ // PERM_CHUNK):
            prod = do_ref[_chunk(c), :] * o_ref[_chunk(c), :]
            d_a = jnp.sum(prod * mask_a, axis=1, keepdims=True)
            d_b = jnp.sum(prod * mask_b, axis=1, keepdims=True)
            dnat[_chunk(c), :] = jnp.where(first, d_a, d_b)
        prefetch(4)

        for step, dil in enumerate(DILATIONS[1:] + DILATIONS[:1]):
            length = SEQ // dil
            assert length >= 2 * Q_BLOCK
            chunks = _perm_chunks(dil)
            for c, rows in chunks:
                val = q32[rows, :]
                qa[_chunk(c), :] = (val * mask_a).astype(BF16)
                qb[_chunk(c), :] = (val * mask_b).astype(BF16)
                val = do_ref[rows, :]
                doa[_chunk(c), :] = (val * mask_a).astype(BF16)
                dob[_chunk(c), :] = (val * mask_b).astype(BF16)
                kp[_chunk(c, HALF_WINDOW), :] = k32[rows, :].astype(BF16)
                vp[_chunk(c, HALF_WINDOW), :] = v32[rows, :].astype(BF16)
                lsep[_chunk(c), :] = lse_ref[rows, :]
                dlp[_chunk(c), :] = dnat[rows, :]
            if step == len(DILATIONS) - 1:
                prefetch(3, 5)
            zero = jnp.zeros((PERM_CHUNK, LANES), F32)
            for c in range(SEQ // PERM_CHUNK):
                dkp[_chunk(c), :] = zero
                dvp[_chunk(c), :] = zero
            dkp[pl.ds(SEQ, 2 * HALF_WINDOW), :] = zero[:2 * HALF_WINDOW]
            dvp[pl.ds(SEQ, 2 * HALF_WINDOW), :] = zero[:2 * HALF_WINDOW]

            heads = ((qa, doa, 0), (qb, dob, HEAD_DIM))

            def blocks(i, carry, length=length):
                base = pl.multiple_of(i * (group * Q_BLOCK), group * Q_BLOCK)
                starts = [base + g * Q_BLOCK for g in range(group)]
                raw = [[(_dot_nt(qx[pl.ds(m0, Q_BLOCK), :], kp[pl.ds(m0, K_WINDOW), :]),
                         _dot_nt(dox[pl.ds(m0, Q_BLOCK), :], vp[pl.ds(m0, K_WINDOW), :])) for qx, dox, _ in heads]
                       for m0 in starts]
                grads = []
                for m0, pair in zip(starts, raw):
                    bias = bias_ref[_band_bias_index(m0, length)]
                    lse_b, d_b = lsep[pl.ds(m0, Q_BLOCK), :], dlp[pl.ds(m0, Q_BLOCK), :]
                    out = []
                    for (s, dp), (_, _, col) in zip(pair, heads):
                        p = jnp.exp(s + bias - lse_b[:, col:col + 1])
                        out.append(((p * (dp - d_b[:, col:col + 1])).astype(BF16), p.astype(BF16)))
                    grads.append(out)
                for m0, out in zip(starts, grads):
                    qrows, krows = pl.ds(m0, Q_BLOCK), pl.ds(m0, K_WINDOW)
                    kw = kp[krows, :]
                    dk = jnp.zeros((K_WINDOW, LANES), F32)
                    dv = jnp.zeros((K_WINDOW, LANES), F32)
                    for (ds, p), (qx, dox, _) in zip(out, heads):
                        dk = dk + _dot_tn(ds, qx[qrows, :])
                        dv = dv + _dot_tn(p, dox[qrows, :])
                    dqp[qrows, :] = jnp.where(first, _dot(out[0][0], kw), _dot(out[1][0], kw)) * (HEAD_DIM ** -0.5)
                    dkp[krows, :] += dk
                    dvp[krows, :] += dv
                return carry

            lax.fori_loop(0, SEQ // (group * Q_BLOCK), blocks, 0)

            for c, rows in chunks:
                g_q, g_k, g_v = dqp[_chunk(c), :], dkp[_chunk(c, HALF_WINDOW), :], dvp[_chunk(c, HALF_WINDOW), :]
                if step == 0:
                    dq_ref[rows, :] = g_q
                    dk_ref[rows, :] = g_k
                    dv_ref[rows, :] = g_v
                else:
                    dq_ref[rows, :] = dq_ref[rows, :] + g_q
                    dk_ref[rows, :] = dk_ref[rows, :] + g_k
                    dv_ref[rows, :] = dv_ref[rows, :] + g_v

    col = pl.BlockSpec((SEQ, LANES), lambda h: (0, h))
    padded = SEQ + 2 * HALF_WINDOW
    return _call(
        body, (q, k, v, do, o, lse), grid=(n_pairs,), in_specs=[ANY] * 6, out_specs=[col] * 3,
        out_shape=[_sds((SEQ, ATTN_WIDTH), F32)] * 3,
        scratch_shapes=[pltpu.VMEM((SEQ, LANES), BF16)] * 3 + [pltpu.VMEM((SEQ, LANES), F32)] * 3 + [pltpu.SemaphoreType.DMA((6,))]
        + [pltpu.VMEM((SEQ, LANES), F32)] * 3 + [pltpu.VMEM((SEQ, LANES), BF16)] * 4
        + [pltpu.VMEM((padded, LANES), BF16)] * 2 + [pltpu.VMEM((SEQ, LANES), F32)] * 4 + [pltpu.VMEM((padded, LANES), F32)] * 2
        + [pltpu.VMEM((3, Q_BLOCK, K_WINDOW), F32)],
        vmem_mib=56, name="attn_bwd", comm=comm)


def _inproj_bwd(dq, dk, dv, ds, db, u, c, conv_w, tc, tsa, tsb, w_in, x, g_pre, dx1):
    def body(dq_ref, dk_ref, dv_ref, ds_ref, db_ref, u_ref, c_ref, dsp_ref, up_ref, cp_ref, dsn_ref, un_ref, cn_ref, cw_ref,
             tc_ref, tsa_ref, tsb_ref, w_ref, x_ref, g_ref, dx1_ref, dx_ref, dproj_ref, h_ref, dg_ref, dcw_ref):
        i = pl.program_id(0)
        cw = cw_ref[...]
        u, c = u_ref[...].astype(F32), c_ref[...].astype(F32)
        t, t_prev, t_next, _ = _conv_parts(u, c, up_ref, cp_ref, un_ref, cn_ref, cw, i)
        ds = ds_ref[...].astype(F32)
        ds_prev, ds_next = _shifted(ds, _last_row(dsp_ref), _first_row(dsn_ref), i)
        dt = cw[0:1, :] * ds_next + cw[1:2, :] * ds + cw[2:3, :] * ds_prev
        d_taps = jnp.concatenate([_colsum(ds * t_prev), _colsum(ds * t), _colsum(ds * t_next)], axis=0)
        tc_, tsa_, tsb_ = tc_ref[...], tsa_ref[...], tsb_ref[...]
        groups = ((2, lambda: dv_ref[...].astype(BF16)), (4, lambda: db_ref[...]),
                  (0, lambda: _rotate_transposed(dq_ref[...], tc_, tsa_, tsb_).astype(BF16)),
                  (1, lambda: _rotate_transposed(dk_ref[...], tc_, tsa_, tsb_).astype(BF16)),
                  (3, lambda: (dt * c).astype(BF16)), (5, lambda: (dt * u).astype(BF16)))
        dh = jnp.zeros((x_ref.shape[0], D_MODEL), F32)
        for n, make in groups:
            cols = pl.ds(n * a, a)
            part = make()
            dproj_ref[:, cols] = part
            dh = dh + _dot_nt(part, w_ref[:, cols])
        y, r = _rms(x_ref[...])
        h_ref[...] = (y * g_ref[...]).astype(BF16).T
        dx_ref[...] = dx1_ref[...] + _rms_bwd(dh * g_ref[...], y, r)
        _accumulate([(dg_ref, _colsum(dh * y)), (dcw_ref, d_taps)], i)

    a = ATTN_WIDTH
    tile = FFN_TILE

    def rows(width):
        return _rows(width, tile)

    return _call(
        body, (dq, dk, dv, ds, db, u, c, ds, u, c, ds, u, c, conv_w, tc, tsa, tsb, w_in, x, g_pre, dx1), grid=(SEQ // tile,),
        in_specs=[rows(a)] * 7 + [_halo_prev(a, tile)] * 3 + [_halo_next(a, tile)] * 3
        + [_const((3, a)), rows(LANES), rows(LANES), rows(LANES), _const1((D_MODEL, IN_PROJ_WIDTH)), rows(D_MODEL),
           _const((1, D_MODEL)), rows(D_MODEL)],
        out_specs=[rows(D_MODEL), rows(IN_PROJ_WIDTH), _cols(D_MODEL, tile), _const((1, D_MODEL)), _const((3, a))],
        out_shape=[_sds((SEQ, D_MODEL), F32), _sds((SEQ, IN_PROJ_WIDTH), BF16), _sds((D_MODEL, SEQ), BF16),
                   _sds((1, D_MODEL), F32), _sds((3, a), F32)],
        vmem_mib=48, name="inproj_bwd")


def _adamw_math(w, g, m, v):
    m = ADAM_B1 * m + (1.0 - ADAM_B1) * g
    v = ADAM_B2 * v + (1.0 - ADAM_B2) * (g * g)
    m_hat = m / (1.0 - ADAM_B1 ** ADAM_STEP)
    v_hat = v / (1.0 - ADAM_B2 ** ADAM_STEP)
    delta = -ADAM_LR * (m_hat / (jnp.sqrt(v_hat) + ADAM_EPS) + ADAM_WD * w)
    return delta, m, v


def _sum_parts(p_ref):
    g = p_ref[0].astype(F32)
    for k in range(1, p_ref.shape[0]):
        g = g + p_ref[k].astype(F32)
    return g


def _adamw_layers(parts, w, m, v, row_tile, name, comm=None):
    _, rows, cols = w.shape
    n_tiles = rows // row_tile

    def body(*refs):
        p_refs = refs[:DEPTH]
        w_ref, m_ref, v_ref, g_ref, d_ref, nm_ref, nv_ref = refs[DEPTH:]
        layer = pl.program_id(0)
        for l, p_ref in enumerate(p_refs):
            @pl.when(layer == l)
            def _(p_ref=p_ref):
                g = _sum_parts(p_ref)
                g_ref[0] = g
                d_ref[0], nm_ref[0], nv_ref[0] = _adamw_math(w_ref[0], g, m_ref[0], v_ref[0])

    def part_spec(l):
        return pl.BlockSpec((parts[l].shape[0], row_tile, cols),
                            lambda layer, i: (0, jnp.where(layer == l, i, jnp.where(layer < l, 0, n_tiles - 1)), 0))

    tile = pl.BlockSpec((1, row_tile, cols), lambda layer, i: (layer, i, 0))
    return _call(body, (*parts, w, m, v), grid=(DEPTH, n_tiles), in_specs=[part_spec(l) for l in range(DEPTH)] + [tile] * 3,
                 out_specs=[tile] * 4, out_shape=[_sds(w.shape, F32)] * 4, name=name, comm=comm)


def _adamw_small(parts, w, m, v):
    n = len(GAIN_NAMES)
    a = ATTN_WIDTH
    where = ((0, 0), (1, 0), (2, 0), (3, 0), (4, 0), (4, a))

    def body(*refs):
        p_ref = refs[0]
        w_refs, m_refs, v_refs = refs[1:1 + n], refs[1 + n:1 + 2 * n], refs[1 + 2 * n:1 + 3 * n]
        outs = refs[1 + 3 * n:1 + 7 * n]
        taps_ref, loss_ref, sum_ref = refs[1 + 7 * n:]
        sum_ref[...] = _sum_parts(p_ref)
        for k, (row, col) in enumerate(where):
            width = w_refs[k].shape[1]
            g = jnp.concatenate([sum_ref[pl.ds(l * SMALL_ROWS + row, 1), pl.ds(col, width)] for l in range(DEPTH)], axis=0)
            g_ref, d_ref, nm_ref, nv_ref = outs[4 * k:4 * k + 4]
            g_ref[...] = g
            d_ref[...], nm_ref[...], nv_ref[...] = _adamw_math(w_refs[k][...], g, m_refs[k][...], v_refs[k][...])
        for l in range(DEPTH):
            for tap, (row, col) in enumerate(((5, 0), (5, a), (6, 0))):
                taps_ref[l, tap:tap + 1, :] = sum_ref[l * SMALL_ROWS + row:l * SMALL_ROWS + row + 1, col:col + a]
        loss_ref[...] = sum_ref[SMALL_ROWS - 1:SMALL_ROWS, 0:1]

    out_shape = [_sds(x.shape, F32) for x in w for _ in range(4)] + [_sds((DEPTH, 3, a), F32), _sds((1, 1), F32)]
    res = pl.pallas_call(body, out_shape=out_shape, scratch_shapes=[pltpu.VMEM(parts.shape[1:], F32)],
                         name="adamw_small")(parts, *w, *m, *v)
    return [res[4 * k:4 * k + 4] for k in range(n)], res[4 * n], res[4 * n + 1]


def _adamw_plain(g, w, m, v, name):
    def body(g_ref, w_ref, m_ref, v_ref, d_ref, nm_ref, nv_ref):
        d_ref[...], nm_ref[...], nv_ref[...] = _adamw_math(w_ref[...], g_ref[...], m_ref[...], v_ref[...])

    return pl.pallas_call(body, out_shape=[_sds(w.shape, F32)] * 3, name=name)(g, w, m, v)


SMALL_ROWS = 8
GAIN_NAMES = ("pre_mix_norm", "post_mix_norm", "pre_ffn_norm", "post_ffn_norm", "attn_out_norm", "conv_out_norm")


def _pack_small(pre_mix, post_mix, pre_ffn, post_ffn, attn_out, conv_out, taps):
    zeros = jnp.zeros((1, ATTN_WIDTH), F32)
    return jnp.concatenate([
        pre_mix, post_mix, pre_ffn, post_ffn, jnp.concatenate([attn_out, conv_out], axis=1),
        jnp.concatenate([taps[0:1], taps[1:2]], axis=1), jnp.concatenate([taps[2:3], zeros], axis=1),
        jnp.zeros((1, D_MODEL), F32)], axis=0)


def _rope(positions, comm=None):
    inv_freq = ROPE_THETA ** (-jnp.arange(0, ROPE_DIM, 2, dtype=F32) / ROPE_DIM)
    per_head = jnp.concatenate([inv_freq, inv_freq, jnp.zeros((HEAD_DIM - ROPE_DIM,), F32)])
    freq_row = jnp.tile(per_head, LANES // HEAD_DIM).reshape(1, LANES)
    return _rope_tables(positions.reshape(SEQ, 1), freq_row, comm)


def _layer_forward(h, gains, taps, tables, w, inproj_comm=None, attn_comm=None, mix_comm=None, ffn_comm=None, target=None):
    def arrived(comm, landed):
        return w if comm is None else {**w, **dict(zip(comm.kinds, landed))}

    (q, k, v, u, b, c), landed = _inproj_fwd(h, gains["pre_mix_norm"], w["in"], *tables, comm=inproj_comm)
    w = arrived(inproj_comm, landed)
    (attn, lse), landed = _attn_fwd(q, k, v, comm=attn_comm)
    w = arrived(attn_comm, landed)
    (x1, merged, mix), landed = _mix_fwd(attn, u, b, c, taps, gains["attn_out_norm"], gains["conv_out_norm"], w["out"], h,
                                         gains["post_mix_norm"], comm=mix_comm)
    w = arrived(mix_comm, landed)
    (x2, gu, f, *loss), landed_next = _ffn_fwd(x1, gains["pre_ffn_norm"], w["gu"], w["dn"], gains["post_ffn_norm"], comm=ffn_comm,
                                               target=target)
    out = x2 if target is None else (x2, loss[0])
    return out, (h, q, k, v, u, b, c, attn, lse, merged, mix, x1, gu, f), w, landed_next


class _GradientExchange:
    def ffn_grads(self, act, df, h2, dgu):
        return dict(dn=_wgrad_paired(act, df, "dn"), gu=_wgrad_paired(dgu, h2, "gu"))

    def w_out_grad(self, merged_t, dmix):
        return _wgrad(merged_t, dmix, "wgrad_out")

    def attention_passenger(self, ffn, g_out):
        return _Together(_ScatterChips(("gu", "dn"), (ffn["gu"], ffn["dn"])), _Scatter(("out",), (g_out,)))

    def w_in_grad(self, h_t, dproj, small_rows):
        if small_rows is None:
            return _wgrad(h_t, dproj, "wgrad_in"), None
        parts, landed = _wgrad_paired(h_t, dproj, "in", to_owners=True, comm=_GatherSmall(small_rows, "small"))
        return parts, landed[0]


def _layer_backward(dx, saved, gains, taps, tables, w, ffn_comm, exchange, last):
    x0, q, k, v, u, b, c, attn, lse, merged, mix, x1, gu, f = saved
    (dx1, df, act, dgu, h2, dg_post_ffn, dg_pre_ffn), landed_prev = _ffn_bwd(
        dx, f, x1, gu, w["gu"], w["dn"], gains["post_ffn_norm"], gains["pre_ffn_norm"], comm=ffn_comm)
    ffn = exchange.ffn_grads(act, df, h2, dgu)
    dmix, dattn, ds, db, dg_post_mix, dg_attn, dg_conv = _mix_bwd(
        dx1, mix, attn, u, b, c, taps, gains["attn_out_norm"], gains["conv_out_norm"], gains["post_mix_norm"], w["out"])
    g_out = exchange.w_out_grad(merged, dmix)
    attn_comm = exchange.attention_passenger(ffn, g_out)
    (dq, dk, dv), landed = _attn_bwd(q, k, v, dattn, attn, lse, comm=attn_comm)
    dx0, dproj, h1, dg_pre_mix, dtaps = _inproj_bwd(
        dq, dk, dv, ds, db, u, c, taps, *tables, w["in"], x0, gains["pre_mix_norm"], dx1)[0]
    small = _pack_small(dg_pre_mix, dg_post_mix, dg_pre_ffn, dg_post_ffn, dg_attn, dg_conv, dtaps)
    small_rows = None
    if last is not None:
        small_above, local_loss = last
        small_rows = lax.dynamic_update_slice(jnp.concatenate([small, small_above], axis=0), local_loss, (SMALL_ROWS - 1, 0))
    g_in, gathered = exchange.w_in_grad(h1, dproj, small_rows)
    rest = dict(zip(attn_comm.kinds, landed)) if attn_comm is not None else dict(ffn, out=g_out)
    return dx0, g_in, small, landed_prev, rest, gathered


def kernel(x, positions, pre_mix_norm, w_in, conv_w, attn_out_norm, conv_out_norm, w_out, post_mix_norm, pre_ffn_norm, w_gate_up, w_down, post_ffn_norm, loss_target, m_pre_mix_norm, m_w_in, m_conv_w, m_attn_out_norm, m_conv_out_norm, m_w_out, m_post_mix_norm, m_pre_ffn_norm, m_w_gate_up, m_w_down, m_post_ffn_norm, v_pre_mix_norm, v_w_in, v_conv_w, v_attn_out_norm, v_conv_out_norm, v_w_out, v_post_mix_norm, v_pre_ffn_norm, v_w_gate_up, v_w_down, v_post_ffn_norm):
    mx, my, mc = _place()
    me = _block_of(mx, my, mc)
    conv_channels = conv_w.shape[-1]

    def hidden_major(a):
        return jnp.swapaxes(a, 1, 2)

    big_w = dict(zip(WEIGHT_KINDS, (w_in, w_out, hidden_major(w_gate_up), w_down)))
    big_m = dict(zip(WEIGHT_KINDS, (m_w_in, m_w_out, hidden_major(m_w_gate_up), m_w_down)))
    big_v = dict(zip(WEIGHT_KINDS, (v_w_in, v_w_out, hidden_major(v_w_gate_up), v_w_down)))
    all_gains = dict(pre_mix_norm=pre_mix_norm, attn_out_norm=attn_out_norm, conv_out_norm=conv_out_norm, post_mix_norm=post_mix_norm,
                     pre_ffn_norm=pre_ffn_norm, post_ffn_norm=post_ffn_norm)

    def gather(kinds, l):
        return _Gather(kinds, [big_w[kind][l].astype(BF16) for kind in kinds])

    def gains(l):
        return {name: g[l:l + 1] for name, g in all_gains.items()}

    early = ("in", "out", "dn")
    taps_flat = jnp.pad(conv_w.reshape(-1), (0, 8 * LANES - conv_w.size)).reshape(8, LANES)
    tables, (first, taps_all) = _rope(positions, _Together(gather(early[:1], 0), _GatherSmall(taps_flat, "taps")))
    taps_all = taps_all.reshape(N_DEV, 8 * LANES)[:, :conv_w.size]
    conv_w_full = taps_all.reshape(N_DEV, DEPTH, 3, conv_channels).transpose(1, 2, 0, 3).reshape(DEPTH, 3, ATTN_WIDTH)
    weights = [dict(zip(early[:1], [first]))] + [None] * (DEPTH - 1)
    saved = [None] * DEPTH
    h = x[0]
    for l in range(DEPTH):
        inproj_comm = gather(("dn",), 0) if l == 0 else None
        attn_comm = gather(("gu", "out") if l == 0 else ("gu",), l)
        ffn_comm = gather(early, l + 1) if l + 1 < DEPTH else None
        target = loss_target[0] if l + 1 == DEPTH else None
        h, saved[l], weights[l], landed = _layer_forward(h, gains(l), conv_w_full[l], tables, weights[l], inproj_comm,
                                                         attn_comm, None, ffn_comm, target)
        if ffn_comm is not None:
            weights[l + 1] = dict(zip(ffn_comm.kinds, landed))
    dx, local_loss = h

    parts = {kind: [None] * DEPTH for kind in WEIGHT_KINDS}
    small_grads = [None] * DEPTH
    g_in_above = None
    exchange = _GradientExchange()
    for l in reversed(range(DEPTH)):
        ffn_comm = _Scatter(("in",), (g_in_above,)) if g_in_above is not None else None
        last = (jnp.concatenate(small_grads[1:], axis=0), local_loss) if l == 0 else None
        dx, g_in_above, small_grads[l], landed, rest, gathered = _layer_backward(
            dx, saved[l], gains(l), conv_w_full[l], tables, weights[l], ffn_comm, exchange, last)
        if ffn_comm is not None:
            parts["in"][l + 1] = landed[0]
        for kind, part in rest.items():
            parts[kind][l] = part

    parts["in"][0] = g_in_above
    tiles = {"in": 256, "out": 128, "gu": 176, "dn": 176}
    big = {kind: _adamw_layers(parts[kind], big_w[kind], big_m[kind], big_v[kind], tiles[kind], "adamw_" + kind)[0]
           for kind in WEIGHT_KINDS}

    small_m = (m_pre_mix_norm, m_post_mix_norm, m_pre_ffn_norm, m_post_ffn_norm, m_attn_out_norm, m_conv_out_norm)
    small_v = (v_pre_mix_norm, v_post_mix_norm, v_pre_ffn_norm, v_post_ffn_norm, v_attn_out_norm, v_conv_out_norm)
    per_gain, taps_grad_full, loss = _adamw_small(gathered, [all_gains[name] for name in GAIN_NAMES], small_m, small_v)
    small = [{name: per_gain[k][o] for k, name in enumerate(GAIN_NAMES)} for o in range(4)]
    loss = loss[0, 0]
    taps_grad = lax.dynamic_slice_in_dim(taps_grad_full, me * conv_channels, conv_channels, axis=2)

    taps = [taps_grad, *_adamw_plain(taps_grad, conv_w, m_conv_w, v_conv_w, "adamw_taps")]

    def leaves(o):
        s = small[o]
        return (s["pre_mix_norm"], big["in"][o], taps[o], s["attn_out_norm"], s["conv_out_norm"], big["out"][o], s["post_mix_norm"],
                s["pre_ffn_norm"], hidden_major(big["gu"][o]), big["dn"][o], s["post_ffn_norm"])

    return (loss, dx[None], *leaves(0), *leaves(1), *leaves(2), *leaves(3))
```

```python
import math

import jax
import jax.numpy as jnp
from jax import lax
from jax.experimental import pallas as pl
from jax.experimental.pallas import tpu as pltpu

F32 = jnp.float32
BF16 = jnp.bfloat16
MESH = pl.DeviceIdType.MESH

SEQ = 4096
D_MODEL = 1024
DEPTH = 4
N_DEV = 8
ATTN_WIDTH = 512
IN_PROJ_WIDTH = 3072
FFN_HIDDEN = 2816
FFN_BLOCK = 2 * FFN_HIDDEN // N_DEV
W_IN_BLOCK = IN_PROJ_WIDTH // N_DEV
W_OUT_BLOCK = D_MODEL // N_DEV
W_DOWN_BLOCK = FFN_HIDDEN // N_DEV
HEAD_DIM = 64
ROPE_DIM = 16
ROPE_THETA = 500000.0
DILATIONS = (1, 4, 16)
HALF_WINDOW = 64
RMS_EPS = 1e-6
NEG_INF = -1e30
LANES = 128
Q_BLOCK = 128
K_WINDOW = Q_BLOCK + 2 * HALF_WINDOW
PERM_CHUNK = 256
ROW_TILE = 512
FFN_TILE = 256
WGRAD_COLS = 512
ADAM_LR, ADAM_B1, ADAM_B2, ADAM_EPS, ADAM_WD, ADAM_STEP = 0.001, 0.9, 0.999, 1e-08, 0.01, 10
MIB = 1024 * 1024
PINNED_BYTES = 256 * 1024

WEIGHT_KINDS = ("in", "out", "gu", "dn")
FULL_SHAPES = {"in": (D_MODEL, IN_PROJ_WIDTH), "out": (D_MODEL, D_MODEL), "gu": (N_DEV, FFN_BLOCK, D_MODEL), "dn": (FFN_HIDDEN, D_MODEL)}
SHARD_SHAPES = {"in": (D_MODEL, W_IN_BLOCK), "out": (W_OUT_BLOCK, D_MODEL), "gu": (FFN_BLOCK, D_MODEL), "dn": (W_DOWN_BLOCK, D_MODEL)}
ANY = pl.BlockSpec(memory_space=pl.ANY)


def _sds(shape, dtype):
    return jax.ShapeDtypeStruct(shape, dtype)


def _rows(width, tile=ROW_TILE):
    return pl.BlockSpec((tile, width), lambda i: (i, 0))


def _frows(width):
    return _rows(width, FFN_TILE)


def _cols(height, tile):
    return pl.BlockSpec((height, tile), lambda i: (0, i))


def _const(shape):
    return pl.BlockSpec(shape, lambda i: (0,) * len(shape))


def _const1(shape):
    return pl.BlockSpec(shape, lambda i: (0,) * len(shape), pipeline_mode=pl.Buffered(1))


HALO_ROWS = 16


def _halo_prev(width, tile=ROW_TILE):
    return pl.BlockSpec((HALO_ROWS, width), lambda i: (jnp.maximum(i * (tile // HALO_ROWS) - 1, 0), 0))


def _halo_next(width, tile=ROW_TILE):
    return pl.BlockSpec((HALO_ROWS, width), lambda i: (jnp.minimum((i + 1) * (tile // HALO_ROWS), SEQ // HALO_ROWS - 1), 0))


def _rms(x):
    r = lax.rsqrt(jnp.mean(x * x, axis=-1, keepdims=True) + RMS_EPS)
    return x * r, r


def _rms_bwd(dn, y, r):
    return r * (dn - y * jnp.mean(dn * y, axis=-1, keepdims=True))


def _dot(a, b):
    return jnp.dot(a, b, preferred_element_type=F32)


def _dot_nt(a, b):
    return lax.dot_general(a, b, (((1,), (1,)), ((), ())), preferred_element_type=F32)


def _dot_tn(a, b):
    return lax.dot_general(a, b, (((0,), (0,)), ((), ())), preferred_element_type=F32)


def _place():
    return lax.axis_index("x"), lax.axis_index("y"), lax.axis_index("c")


def _block_of(px, py, pc):
    return 4 * px + 2 * py + pc


def _weight_block(ref, kind, blk):
    if kind == "in":
        return ref.at[:, pl.ds(blk * W_IN_BLOCK, W_IN_BLOCK)]
    if kind == "out":
        return ref.at[pl.ds(blk * W_OUT_BLOCK, W_OUT_BLOCK), :]
    if kind == "gu":
        return ref.at[blk]
    return ref.at[pl.ds(blk * W_DOWN_BLOCK, W_DOWN_BLOCK), :]


def _dma_semaphores(n):
    return [pltpu.SemaphoreType.DMA((n, 7)), pltpu.SemaphoreType.DMA((n, 7)), pltpu.SemaphoreType.DMA((n,))]


class _Gather:
    def __init__(self, kinds, shards, layer):
        self.kinds, self.operands, self.layer = tuple(kinds), list(shards), layer
        self.tag = "gather_" + "_".join(kinds)
        self.out_shape = [_sds(FULL_SHAPES[k], BF16) for k in kinds]
        self.scratch = _dma_semaphores(len(kinds))

    def _parties(self):
        x, y, c = _place()
        return (x, y, c), (x, y, 1 - c), [(1 - x, y), (x, 1 - y), (1 - x, 1 - y)], c

    def _copy(self, outs, sems, w, k, block, to, src=None):
        dst = _weight_block(outs[w], self.kinds[w], _block_of(*block))
        return pltpu.make_async_remote_copy(src_ref=dst if src is None else src, dst_ref=dst, send_sem=sems[0].at[w, k],
                                            recv_sem=sems[1].at[w, k], device_id=to, device_id_type=MESH)

    def _own(self, srcs, outs, sems, w, me):
        return pltpu.make_async_copy(srcs[w].at[self.layer], _weight_block(outs[w], self.kinds[w], _block_of(*me)), sems[2].at[w])

    def _first(self, srcs, outs, sems, w):
        me, sibling, chips, c = self._parties()
        shard = srcs[w].at[self.layer]
        return [self._copy(outs, sems, w, 0, me, sibling, src=shard)] + [
            self._copy(outs, sems, w, 1 + j, me, (*chip, c), src=shard) for j, chip in enumerate(chips)]

    def start(self, srcs, outs, sems):
        me = self._parties()[0]
        for w in range(len(self.kinds)):
            self._own(srcs, outs, sems, w, me).start()
            for cp in self._first(srcs, outs, sems, w):
                cp.start()

    def forward(self, srcs, outs, sems):
        me, sibling, chips, c = self._parties()
        for j, chip in enumerate(chips):
            for w in range(len(self.kinds)):
                self._copy(outs, sems, w, 1 + j, (*chip, c), me).wait_recv()
                self._copy(outs, sems, w, 4 + j, (*chip, c), sibling).start()

    def finish(self, srcs, outs, sems):
        me, sibling, chips, c = self._parties()
        for w in range(len(self.kinds)):
            self._copy(outs, sems, w, 0, sibling, me).wait_recv()
            for j, chip in enumerate(chips):
                self._copy(outs, sems, w, 4 + j, (*chip, 1 - c), me).wait_recv()
        for w in range(len(self.kinds)):
            for cp in self._first(srcs, outs, sems, w):
                cp.wait_send()
            for j, chip in enumerate(chips):
                self._copy(outs, sems, w, 4 + j, (*chip, c), sibling).wait_send()
            self._own(srcs, outs, sems, w, me).wait()


def _peers(x, y, c):
    return [(x ^ a, y ^ b, c ^ e) for a in (0, 1) for b in (0, 1) for e in (0, 1) if (a, b, e) != (0, 0, 0)]


class _Scatter:
    def __init__(self, kinds, grads):
        self.kinds, self.operands = tuple(kinds), list(grads)
        self.tag = "scatter_" + "_".join(kinds)
        self.out_shape = [_sds((N_DEV,) + SHARD_SHAPES[k], BF16) for k in kinds]
        self.scratch = _dma_semaphores(len(kinds))

    def _copies(self, srcs, outs, sems):
        x, y, c = _place()
        me = _block_of(x, y, c)
        copies = []
        for w, kind in enumerate(self.kinds):
            copies.append(pltpu.make_async_copy(_weight_block(srcs[w], kind, me), outs[w].at[me], sems[2].at[w]))
            for k, peer in enumerate(_peers(x, y, c)):
                copies.append(pltpu.make_async_remote_copy(
                    src_ref=_weight_block(srcs[w], kind, _block_of(*peer)), dst_ref=outs[w].at[me],
                    send_sem=sems[0].at[w, k], recv_sem=sems[1].at[w, k], device_id=peer, device_id_type=MESH))
        return copies

    def start(self, srcs, outs, sems):
        for cp in self._copies(srcs, outs, sems):
            cp.start()

    def forward(self, srcs, outs, sems):
        pass

    def finish(self, srcs, outs, sems):
        for cp in self._copies(srcs, outs, sems):
            cp.wait()


N_CHIPS = N_DEV // 2


class _ScatterChips:
    def __init__(self, kinds, blocks):
        self.kinds, self.operands = tuple(kinds), list(blocks)
        self.tag = "scatter_chips_" + "_".join(kinds)
        self.out_shape = [_sds((N_CHIPS,) + SHARD_SHAPES[k], BF16) for k in kinds]
        self.scratch = _dma_semaphores(len(kinds))

    def _copies(self, srcs, outs, sems):
        x, y, c = _place()
        mine = 2 * x + y
        copies = []
        for w in range(len(self.kinds)):
            copies.append(pltpu.make_async_copy(srcs[w].at[mine], outs[w].at[mine], sems[2].at[w]))
            for j, (px, py) in enumerate([(1 - x, y), (x, 1 - y), (1 - x, 1 - y)]):
                copies.append(pltpu.make_async_remote_copy(
                    src_ref=srcs[w].at[2 * px + py], dst_ref=outs[w].at[mine], send_sem=sems[0].at[w, j], recv_sem=sems[1].at[w, j],
                    device_id=(px, py, c), device_id_type=MESH))
        return copies

    def start(self, srcs, outs, sems):
        for cp in self._copies(srcs, outs, sems):
            cp.start()

    def forward(self, srcs, outs, sems):
        pass

    def finish(self, srcs, outs, sems):
        for cp in self._copies(srcs, outs, sems):
            cp.wait()


def _in_hbm(a):
    return pltpu.with_memory_space_constraint(a, pltpu.HBM) if a.size * a.dtype.itemsize >= PINNED_BYTES else a


def _out_hbm(s):
    return pltpu.HBM(s.shape, s.dtype) if math.prod(s.shape) * jnp.dtype(s.dtype).itemsize >= PINNED_BYTES else s


def _call(body, args, *, grid, in_specs, out_specs, out_shape, scratch_shapes=(), vmem_mib=None, name, comm=None):
    kwargs = {} if vmem_mib is None else dict(compiler_params=pltpu.CompilerParams(vmem_limit_bytes=vmem_mib * MIB))
    in_specs, out_specs, out_shape, scratch_shapes = list(in_specs), list(out_specs), list(out_shape), list(scratch_shapes)
    args = [_in_hbm(a) for a in args]
    out_shape = [_out_hbm(s) for s in out_shape]
    if comm is None:
        res = pl.pallas_call(body, grid=grid, in_specs=in_specs, out_specs=out_specs, out_shape=out_shape,
                             scratch_shapes=scratch_shapes, name=name, **kwargs)(*args)
        return list(res), None
    n_in, n_out, n_scr = len(in_specs), len(out_specs), len(scratch_shapes)
    c_in, c_out = len(comm.operands), len(comm.out_shape)
    last = math.prod(grid) - 1

    def carried(*refs):
        cuts = [n_in, c_in, n_out, c_out, n_scr]
        parts, at = [], 0
        for n in cuts:
            parts.append(refs[at:at + n])
            at += n
        ins, c_ins, outs, c_outs, scr = parts
        sems = refs[at:]
        step = pl.program_id(0)
        for axis in range(1, len(grid)):
            step = step * grid[axis] + pl.program_id(axis)

        @pl.when(step == 0)
        def _():
            comm.start(c_ins, c_outs, sems)

        @pl.when(step == last)
        def _():
            comm.forward(c_ins, c_outs, sems)

        body(*ins, *outs, *scr)

        @pl.when(step == last)
        def _():
            comm.finish(c_ins, c_outs, sems)

    res = pl.pallas_call(carried, grid=grid, in_specs=in_specs + [ANY] * c_in, out_specs=out_specs + [ANY] * c_out,
                         out_shape=out_shape + [_out_hbm(s) for s in comm.out_shape], scratch_shapes=scratch_shapes + comm.scratch,
                         name=name + "_" + comm.tag, **kwargs)(*args, *[_in_hbm(a) for a in comm.operands])
    return list(res[:n_out]), list(res[n_out:])


class _GatherSmall:
    def __init__(self, v, tag):
        self.operands, self.tag = [v], tag
        self.out_shape = [_sds((N_DEV,) + v.shape, F32)]
        self.scratch = _dma_semaphores(1)

    def _copies(self, srcs, outs, sems):
        x, y, c = _place()
        me = _block_of(x, y, c)
        copies = [pltpu.make_async_copy(srcs[0], outs[0].at[me], sems[2].at[0])]
        for k, peer in enumerate(_peers(x, y, c)):
            copies.append(pltpu.make_async_remote_copy(src_ref=srcs[0], dst_ref=outs[0].at[me], send_sem=sems[0].at[0, k],
                                                       recv_sem=sems[1].at[0, k], device_id=peer, device_id_type=MESH))
        return copies

    def start(self, srcs, outs, sems):
        for cp in self._copies(srcs, outs, sems):
            cp.start()

    def forward(self, srcs, outs, sems):
        pass

    def finish(self, srcs, outs, sems):
        for cp in self._copies(srcs, outs, sems):
            cp.wait()


class _Together:
    def __init__(self, *comms):
        self.comms = comms
        self.operands = [a for cm in comms for a in cm.operands]
        self.out_shape = [s for cm in comms for s in cm.out_shape]
        self.scratch = [s for cm in comms for s in cm.scratch]
        self.tag = "_".join(cm.tag for cm in comms)
        self.kinds = tuple(kind for cm in comms for kind in getattr(cm, "kinds", ()))

    def _each(self, srcs, outs, sems):
        a = o = s = 0
        for cm in self.comms:
            na, no, ns = len(cm.operands), len(cm.out_shape), len(cm.scratch)
            yield cm, srcs[a:a + na], outs[o:o + no], sems[s:s + ns]
            a, o, s = a + na, o + no, s + ns

    def start(self, srcs, outs, sems):
        for cm, *refs in self._each(srcs, outs, sems):
            cm.start(*refs)

    def forward(self, srcs, outs, sems):
        for cm, *refs in self._each(srcs, outs, sems):
            cm.forward(*refs)

    def finish(self, srcs, outs, sems):
        for cm, *refs in self._each(srcs, outs, sems):
            cm.finish(*refs)


def _rope_tables(pos_col, freq_row, comm=None):
    def body(p_ref, f_ref, c_ref, sa_ref, sb_ref):
        ang = p_ref[...].astype(F32) * f_ref[...]
        lane = lax.broadcasted_iota(jnp.int32, ang.shape, 1) % HEAD_DIM
        cos, sin = jnp.cos(ang), jnp.sin(ang)
        c_ref[...] = jnp.where(lane < ROPE_DIM, cos, 1.0)
        sa_ref[...] = jnp.where(lane < ROPE_DIM // 2, -sin, 0.0)
        sb_ref[...] = jnp.where((lane >= ROPE_DIM // 2) & (lane < ROPE_DIM), sin, 0.0)

    return _call(body, (pos_col, freq_row), grid=(SEQ // ROW_TILE,), in_specs=[_rows(1), _const((1, LANES))],
                 out_specs=[_rows(LANES)] * 3, out_shape=[_sds((SEQ, LANES), F32)] * 3, name="rope_tables", comm=comm)


def _rotate(t, c, sa, sb):
    parts = []
    for g in range(ATTN_WIDTH // LANES):
        tg = t[:, g * LANES:(g + 1) * LANES]
        parts.append(tg * c + pltpu.roll(tg, LANES - 8, axis=1) * sa + pltpu.roll(tg, 8, axis=1) * sb)
    return jnp.concatenate(parts, axis=1)


def _rotate_transposed(dt, c, sa, sb):
    parts = []
    for g in range(ATTN_WIDTH // LANES):
        dg = dt[:, g * LANES:(g + 1) * LANES]
        parts.append(dg * c + pltpu.roll(dg * sa, 8, axis=1) + pltpu.roll(dg * sb, LANES - 8, axis=1))
    return jnp.concatenate(parts, axis=1)


def _inproj_fwd(x, g_pre, w_in, tc, tsa, tsb, comm=None):
    def body(x_ref, g_ref, w_ref, c_ref, sa_ref, sb_ref, q_ref, k_ref, v_ref, u_ref, b_ref, cc_ref):
        y, _ = _rms(x_ref[...])
        h = (y * g_ref[...]).astype(BF16)

        def proj(n):
            return _dot(h, w_ref[:, n * ATTN_WIDTH:(n + 1) * ATTN_WIDTH])

        c, sa, sb = c_ref[...], sa_ref[...], sb_ref[...]
        q_ref[...] = (_rotate(proj(0), c, sa, sb) * (HEAD_DIM ** -0.5)).astype(BF16)
        k_ref[...] = _rotate(proj(1), c, sa, sb).astype(BF16)
        v_ref[...] = proj(2).astype(BF16)
        u_ref[...] = proj(3).astype(BF16)
        b_ref[...] = proj(4).astype(BF16)
        cc_ref[...] = proj(5).astype(BF16)

    a = ATTN_WIDTH
    return _call(
        body, (x, g_pre, w_in, tc, tsa, tsb), grid=(SEQ // ROW_TILE,),
        in_specs=[_rows(D_MODEL), _const((1, D_MODEL)), _const1((D_MODEL, IN_PROJ_WIDTH)), _rows(LANES), _rows(LANES), _rows(LANES)],
        out_specs=[_rows(a)] * 6, out_shape=[_sds((SEQ, a), BF16)] * 6,
        vmem_mib=40, name="inproj_fwd", comm=comm)


def _head_masks():
    lane = lax.broadcasted_iota(jnp.int32, (1, LANES), 1)
    first = lane < HEAD_DIM
    return first, first.astype(F32), 1.0 - first.astype(F32)


def _perm_chunks(dil):
    length = SEQ // dil
    out = []
    for r in range(dil):
        for c0 in range(0, length, PERM_CHUNK):
            chunk = (r * length + c0) // PERM_CHUNK
            rows = pl.ds(c0, PERM_CHUNK) if dil == 1 else pl.ds(r + dil * c0, PERM_CHUNK, stride=dil)
            out.append((chunk, rows))
    return out


def _chunk(c, offset=0):
    return pl.ds(offset + c * PERM_CHUNK, PERM_CHUNK)


def _write_band_bias(bias_ref):
    rr = lax.broadcasted_iota(jnp.int32, (Q_BLOCK, K_WINDOW), 0)
    cc = lax.broadcasted_iota(jnp.int32, (Q_BLOCK, K_WINDOW), 1)
    band = (cc >= rr) & (cc - rr <= 2 * HALF_WINDOW)
    bias_ref[0] = jnp.where(band, 0.0, NEG_INF)
    bias_ref[1] = jnp.where(band & (cc >= HALF_WINDOW), 0.0, NEG_INF)
    bias_ref[2] = jnp.where(band & (cc < Q_BLOCK + HALF_WINDOW), 0.0, NEG_INF)


def _band_bias_index(m0, length):
    return jnp.where(m0 % length == 0, 1, 0) + jnp.where((m0 + Q_BLOCK) % length == 0, 2, 0)


def _zero_key_padding(bufs):
    pad = jnp.zeros((HALF_WINDOW, LANES), BF16)
    for buf in bufs:
        buf[pl.ds(0, HALF_WINDOW), :] = pad
        buf[pl.ds(SEQ + HALF_WINDOW, HALF_WINDOW), :] = pad


def _attn_fwd(q, k, v, comm=None):
    group = 8

    def body(q_ref, k_ref, v_ref, o_ref, lse_ref, q32, k32, v32, qa, qb, kp, vp, accp, mlp,
             acc0, acc1, acc2, ml0, ml1, ml2, bias_ref):
        first, mask_a, mask_b = _head_masks()
        low = lax.broadcasted_iota(jnp.int32, (1, LANES), 1) % HEAD_DIM < HEAD_DIM // 2
        _zero_key_padding((kp, vp))
        _write_band_bias(bias_ref)
        q32[...] = q_ref[...].astype(F32)
        k32[...] = k_ref[...].astype(F32)
        v32[...] = v_ref[...].astype(F32)
        natural = ((acc0, ml0), (acc1, ml1), (acc2, ml2))

        for branch, dil in enumerate(DILATIONS):
            length = SEQ // dil
            assert length >= 2 * Q_BLOCK
            chunks = _perm_chunks(dil)
            for c, rows in chunks:
                val = q32[rows, :]
                qa[_chunk(c), :] = (val * mask_a).astype(BF16)
                qb[_chunk(c), :] = (val * mask_b).astype(BF16)
                kp[_chunk(c, HALF_WINDOW), :] = k32[rows, :].astype(BF16)
                vp[_chunk(c, HALF_WINDOW), :] = v32[rows, :].astype(BF16)
            acc_dst, ml_dst = natural[branch] if dil == 1 else (accp, mlp)

            def blocks(i, carry, length=length, acc_dst=acc_dst, ml_dst=ml_dst):
                base = pl.multiple_of(i * (group * Q_BLOCK), group * Q_BLOCK)
                starts = [base + g * Q_BLOCK for g in range(group)]
                scores = [[_dot_nt(qx[pl.ds(m0, Q_BLOCK), :], kp[pl.ds(m0, K_WINDOW), :]) for qx in (qa, qb)] for m0 in starts]
                probs = []
                for m0, pair in zip(starts, scores):
                    bias = bias_ref[_band_bias_index(m0, length)]
                    stats, ps = [], []
                    for s in pair:
                        s = s + bias
                        m = jnp.max(s, axis=1, keepdims=True)
                        p = jnp.exp(s - m)
                        stats.append(jnp.where(low, m, jnp.sum(p, axis=1, keepdims=True)))
                        ps.append(p.astype(BF16))
                    ml_dst[pl.ds(m0, Q_BLOCK), :] = jnp.where(first, stats[0], stats[1])
                    probs.append(ps)
                for m0, ps in zip(starts, probs):
                    vw = vp[pl.ds(m0, K_WINDOW), :]
                    acc_dst[pl.ds(m0, Q_BLOCK), :] = jnp.where(first, _dot(ps[0], vw), _dot(ps[1], vw))
                return carry

            lax.fori_loop(0, SEQ // (group * Q_BLOCK), blocks, 0)

            if dil > 1:
                for c, rows in chunks:
                    natural[branch][0][rows, :] = accp[_chunk(c), :]
                    natural[branch][1][rows, :] = mlp[_chunk(c), :]

        for c in range(SEQ // PERM_CHUNK):
            def up(t):
                return jnp.where(low, t, pltpu.roll(t, HEAD_DIM // 2, axis=1))

            packed = [ml[_chunk(c), :] for _, ml in natural]
            top = jnp.maximum(jnp.maximum(packed[0], packed[1]), packed[2])
            m_all = up(top)
            es = [up(jnp.exp(ml - top)) for ml in packed]
            z = packed[0] * es[0] + packed[1] * es[1] + packed[2] * es[2]
            z = jnp.where(low, pltpu.roll(z, LANES - HEAD_DIM // 2, axis=1), z)
            num = natural[0][0][_chunk(c), :] * es[0] + natural[1][0][_chunk(c), :] * es[1] + natural[2][0][_chunk(c), :] * es[2]
            o_ref[_chunk(c), :] = num / z
            lse_ref[_chunk(c), :] = m_all + jnp.log(z)

    col = pl.BlockSpec((SEQ, LANES), lambda h: (0, h))
    padded = SEQ + 2 * HALF_WINDOW
    return _call(
        body, (q, k, v), grid=(ATTN_WIDTH // LANES,), in_specs=[col] * 3, out_specs=[col] * 2,
        out_shape=[_sds((SEQ, ATTN_WIDTH), F32)] * 2,
        scratch_shapes=[pltpu.VMEM((SEQ, LANES), F32)] * 3 + [pltpu.VMEM((SEQ, LANES), BF16)] * 2
        + [pltpu.VMEM((padded, LANES), BF16)] * 2 + [pltpu.VMEM((SEQ, LANES), F32)] * 8
        + [pltpu.VMEM((3, Q_BLOCK, K_WINDOW), F32)],
        vmem_mib=52, name="attn_fwd", comm=comm)


def _shifted(t, before, after, i):
    tile = t.shape[0]
    row = lax.broadcasted_iota(jnp.int32, (tile, 1), 0)
    before = jnp.where(i > 0, before, 0.0)
    after = jnp.where(i < SEQ // tile - 1, after, 0.0)
    return (jnp.where(row == 0, before, pltpu.roll(t, 1, axis=0)),
            jnp.where(row == tile - 1, after, pltpu.roll(t, tile - 1, axis=0)))


def _last_row(ref):
    return ref[HALO_ROWS - 1:HALO_ROWS, :].astype(F32)


def _first_row(ref):
    return ref[0:1, :].astype(F32)


def _conv_parts(u, c, u_prev, c_prev, u_next, c_next, cw, i):
    t = c * u
    t_prev, t_next = _shifted(t, _last_row(c_prev) * _last_row(u_prev), _first_row(c_next) * _first_row(u_next), i)
    s = cw[0:1, :] * t_prev + cw[1:2, :] * t + cw[2:3, :] * t_next
    return t, t_prev, t_next, s


def _mix_fwd(attn, u, b, c, conv_w, g_attn, g_conv, w_out, x, g_post, comm=None):
    def body(a_ref, u_ref, b_ref, c_ref, up_ref, cp_ref, un_ref, cn_ref, cw_ref, ga_ref, gc_ref, w_ref, x_ref, gp_ref,
             x1_ref, mg_ref, mix_ref):
        i = pl.program_id(0)
        _, _, _, s = _conv_parts(u_ref[...].astype(F32), c_ref[...].astype(F32), up_ref, cp_ref, un_ref, cn_ref, cw_ref[...], i)
        ya, _ = _rms(a_ref[...])
        yc, _ = _rms(b_ref[...].astype(F32) * s)
        merged = jnp.concatenate([ya * ga_ref[...], yc * gc_ref[...]], axis=1).astype(BF16)
        mix = _dot(merged, w_ref[...])
        ym, _ = _rms(mix)
        mg_ref[...] = merged.T
        mix_ref[...] = mix.astype(BF16)
        x1_ref[...] = x_ref[...] + ym * gp_ref[...]

    a = ATTN_WIDTH
    return _call(
        body, (attn, u, b, c, u, c, u, c, conv_w, g_attn, g_conv, w_out, x, g_post), grid=(SEQ // ROW_TILE,),
        in_specs=[_rows(a)] * 4 + [_halo_prev(a)] * 2 + [_halo_next(a)] * 2
        + [_const((3, a)), _const((1, a)), _const((1, a)), _const1((D_MODEL, D_MODEL)), _rows(D_MODEL), _const((1, D_MODEL))],
        out_specs=[_rows(D_MODEL), _cols(D_MODEL, ROW_TILE), _rows(D_MODEL)],
        out_shape=[_sds((SEQ, D_MODEL), F32), _sds((D_MODEL, SEQ), BF16), _sds((SEQ, D_MODEL), BF16)],
        vmem_mib=40, name="mix_fwd", comm=comm)


def _gu_spec():
    return pl.BlockSpec((N_DEV, FFN_TILE, FFN_BLOCK), lambda i: (0, i, 0))


def _ffn_fwd(x1, g_pre, w_gu, w_dn, g_post, comm=None, target=None):
    n_tiles = SEQ // ROW_TILE

    def body(*refs):
        if target is None:
            x_ref, g_ref, wgu_ref, wdn_ref, gp_ref, x2_ref, gu_ref, f_ref = refs
        else:
            x_ref, g_ref, wgu_ref, wdn_ref, gp_ref, t_ref, x2_ref, gu_ref, f_ref, loss_ref, acc = refs
        x1 = x_ref[...]
        y, _ = _rms(x1)
        h = (y * g_ref[...]).astype(BF16)
        f = jnp.zeros((ROW_TILE, D_MODEL), F32)

        def gate_up(j):
            return _dot_nt(h, wgu_ref[j]), _dot_nt(h, wgu_ref[j + N_DEV // 2])

        ahead = gate_up(0)
        for j in range(N_DEV // 2):
            gate, up = ahead
            if j + 1 < N_DEV // 2:
                ahead = gate_up(j + 1)
            gu_ref[j] = gate.astype(BF16)
            gu_ref[j + N_DEV // 2] = up.astype(BF16)
            act = (gate * jax.nn.sigmoid(gate) * up).astype(BF16)
            f = f + _dot(act, wdn_ref[pl.ds(j * FFN_BLOCK, FFN_BLOCK), :])
        yf, _ = _rms(f)
        f_ref[...] = f
        x2 = x1 + yf * gp_ref[...]
        if target is None:
            x2_ref[...] = x2
        else:
            i = pl.program_id(0)
            err = x2 - t_ref[...]
            x2_ref[...] = err * (1.0 / D_MODEL)

            @pl.when(i == 0)
            def _():
                acc[...] = jnp.zeros_like(acc)

            acc[...] += jnp.sum(err * err, axis=0, keepdims=True)

            @pl.when(i == n_tiles - 1)
            def _():
                loss_ref[...] = jnp.sum(acc[...], axis=1, keepdims=True) * (0.5 / D_MODEL)

    with_loss = target is not None
    return _call(
        body, (x1, g_pre, w_gu, w_dn, g_post) + ((target,) if with_loss else ()), grid=(n_tiles,),
        in_specs=[_rows(D_MODEL), _const((1, D_MODEL)), _const1((N_DEV, FFN_BLOCK, D_MODEL)), _const1((FFN_HIDDEN, D_MODEL)),
                  _const((1, D_MODEL))] + ([_rows(D_MODEL)] if with_loss else []),
        out_specs=[_rows(D_MODEL), pl.BlockSpec((N_DEV, ROW_TILE, FFN_BLOCK), lambda i: (0, i, 0)), _rows(D_MODEL)]
        + ([_const((1, 1))] if with_loss else []),
        out_shape=[_sds((SEQ, D_MODEL), F32), _sds((N_DEV, SEQ, FFN_BLOCK), BF16), _sds((SEQ, D_MODEL), F32)]
        + ([_sds((1, 1), F32)] if with_loss else []),
        scratch_shapes=[pltpu.VMEM((1, D_MODEL), F32)] if with_loss else [],
        vmem_mib=58, name="ffn_fwd_loss" if with_loss else "ffn_fwd", comm=comm)


def _accumulate(pairs, i):
    @pl.when(i == 0)
    def _():
        for ref, value in pairs:
            ref[...] = value

    @pl.when(i > 0)
    def _():
        for ref, value in pairs:
            ref[...] += value


def _colsum(v):
    return jnp.sum(v, axis=0, keepdims=True)


def _ffn_bwd(dx2, f, x1, gu, w_gu, w_dn, g_post, g_pre, comm=None):
    half = N_DEV // 2

    def body(dx2_ref, f_ref, x1_ref, gu_ref, wgu_ref, wdn_ref, gpost_ref, gpre_ref,
             dx1_ref, df_ref, act_ref, dgu_ref, h_ref, dgpost_ref, dgpre_ref):
        i = pl.program_id(0)
        dx2 = dx2_ref[...]
        yf, rf = _rms(f_ref[...])
        dg_post = _colsum(dx2 * yf)
        df = _rms_bwd(dx2 * gpost_ref[...], yf, rf).astype(BF16)
        df_ref[...] = df
        dh = jnp.zeros((FFN_TILE, D_MODEL), F32)

        def d_act(j):
            return _dot_nt(df, wdn_ref[pl.ds(j * FFN_BLOCK, FFN_BLOCK), :])

        ahead = d_act(0)
        for j in range(half):
            dact = ahead
            if j + 1 < half:
                ahead = d_act(j + 1)
            gate = gu_ref[j].astype(F32)
            up = gu_ref[j + half].astype(F32)
            sig = jax.nn.sigmoid(gate)
            silu = gate * sig
            act_ref[j] = (silu * up).astype(BF16)
            dgate = (dact * up * (sig * (1.0 + gate * (1.0 - sig)))).astype(BF16)
            dup = (dact * silu).astype(BF16)
            dgu_ref[j] = dgate
            dgu_ref[j + half] = dup
            dh = dh + _dot(dgate, wgu_ref[j]) + _dot(dup, wgu_ref[j + half])
        y1, r1 = _rms(x1_ref[...])
        h_ref[...] = (y1 * gpre_ref[...]).astype(BF16)
        dx1_ref[...] = dx2 + _rms_bwd(dh * gpre_ref[...], y1, r1)
        _accumulate([(dgpost_ref, dg_post), (dgpre_ref, _colsum(dh * y1))], i)

    act_spec = pl.BlockSpec((half, FFN_TILE, FFN_BLOCK), lambda i: (0, i, 0))
    return _call(
        body, (dx2, f, x1, gu, w_gu, w_dn, g_post, g_pre), grid=(SEQ // FFN_TILE,),
        in_specs=[_frows(D_MODEL)] * 3 + [_gu_spec(), _const1((N_DEV, FFN_BLOCK, D_MODEL)), _const1((FFN_HIDDEN, D_MODEL)),
                                          _const((1, D_MODEL)), _const((1, D_MODEL))],
        out_specs=[_frows(D_MODEL), _frows(D_MODEL), act_spec, _gu_spec(), _frows(D_MODEL), _const((1, D_MODEL)), _const((1, D_MODEL))],
        out_shape=[_sds((SEQ, D_MODEL), F32), _sds((SEQ, D_MODEL), BF16), _sds((half, SEQ, FFN_BLOCK), BF16),
                   _sds((N_DEV, SEQ, FFN_BLOCK), BF16), _sds((SEQ, D_MODEL), BF16), _sds((1, D_MODEL), F32), _sds((1, D_MODEL), F32)],
        vmem_mib=52, name="ffn_bwd", comm=comm)


def _wgrad(a_t, g, name):
    width = g.shape[1]

    def body(a_ref, g_ref, o_ref):
        o_ref[...] = _dot(a_ref[...], g_ref[...]).astype(BF16)

    return _call(body, (a_t, g), grid=(width // WGRAD_COLS,),
                 in_specs=[_const1((D_MODEL, SEQ)), pl.BlockSpec((SEQ, WGRAD_COLS), lambda j: (0, j))],
                 out_specs=[pl.BlockSpec((D_MODEL, WGRAD_COLS), lambda j: (0, j))], out_shape=[_sds((D_MODEL, width), BF16)],
                 vmem_mib=48, name=name)[0][0]


def _wgrad_paired(a, g, kind, to_owners=False, comm=None):
    shard = SHARD_SHAPES[kind]
    per_chip = 2 if kind == "gu" else 1
    n_compute = per_chip * N_CHIPS
    n_steps = n_compute + 1

    def chip_of(turn):
        if not to_owners:
            return turn
        return (2 * lax.axis_index("x") + lax.axis_index("y") + 1 + turn) % N_CHIPS

    def block(t):
        return jnp.minimum(t, n_compute - 1)

    if kind == "in":
        specs = [_const1((D_MODEL, SEQ)), pl.BlockSpec((SEQ, 2 * W_IN_BLOCK), lambda t: (0, chip_of(block(t))))]
    else:
        specs = [pl.BlockSpec((1, SEQ, FFN_BLOCK), lambda t: (block(t), 0, 0)), _const1((SEQ, D_MODEL))]

    def body(a_ref, g_ref, o_ref, *rest):
        if to_owners:
            landing, sendbuf, recvbuf, keepbuf, send_sem, recv_sem, donebuf, owner_send, owner_recv, own_sem = rest
        else:
            sendbuf, recvbuf, keepbuf, send_sem, recv_sem = rest
        t = pl.program_id(0)
        x, y, c = _place()

        def exchange(q):
            return pltpu.make_async_remote_copy(src_ref=sendbuf.at[q], dst_ref=recvbuf.at[q], send_sem=send_sem.at[q],
                                                recv_sem=recv_sem.at[q], device_id=(x, y, 1 - c), device_id_type=MESH)

        my_chip = 2 * x + y

        def to_owner(q):
            return pltpu.make_async_remote_copy(src_ref=donebuf.at[q], dst_ref=landing.at[my_chip], send_sem=owner_send.at[q],
                                                recv_sem=owner_recv.at[my_chip], device_id=(q // 2, q % 2, c), device_id_type=MESH)

        def keep_own():
            return pltpu.make_async_copy(donebuf.at[my_chip], landing.at[my_chip], own_sem)

        @pl.when((t >= per_chip) & (t % per_chip == 0))
        def _():
            q = chip_of(t // per_chip - 1)
            exchange(q).wait_recv()
            done = (keepbuf[...] + recvbuf[q].astype(F32)).astype(BF16)
            o_ref[0] = done
            if to_owners:
                donebuf[q] = done

                @pl.when(q == my_chip)
                def _():
                    keep_own().start()

                @pl.when(q != my_chip)
                def _():
                    to_owner(q).start()

        @pl.when(t < n_compute)
        def _():
            q = chip_of(t // per_chip)
            if kind == "gu":
                r = _dot_tn(a_ref[0], g_ref[...])

                @pl.when(t % 2 == c)
                def _():
                    keepbuf[...] = r

                @pl.when(t % 2 != c)
                def _():
                    sendbuf[q] = r.astype(BF16)
                    exchange(q).start()
            else:
                if kind == "dn":
                    r = _dot_tn(a_ref[0], g_ref[...])
                    lower, upper = r[:W_DOWN_BLOCK], r[W_DOWN_BLOCK:]
                else:
                    r = _dot(a_ref[...], g_ref[...])
                    lower, upper = r[:, :W_IN_BLOCK], r[:, W_IN_BLOCK:]
                keepbuf[...] = jnp.where(c == 0, lower, upper)
                sendbuf[q] = jnp.where(c == 0, upper, lower).astype(BF16)
                exchange(q).start()

        @pl.when(t == n_steps - 1)
        def _():
            for q in range(N_CHIPS):
                exchange(q).wait_send()
            if to_owners:
                keep_own().wait()
                for s in range(N_CHIPS):
                    @pl.when(s != my_chip)
                    def _(s=s):
                        pltpu.make_async_remote_copy(src_ref=donebuf.at[s], dst_ref=landing.at[s], send_sem=owner_send.at[s],
                                                     recv_sem=owner_recv.at[s], device_id=(x, y, c), device_id_type=MESH).wait()

    slots = (N_CHIPS,) + shard
    owners = int(to_owners)
    res, landed = _call(
        body, (a, g), grid=(n_steps,), in_specs=specs,
        out_specs=[pl.BlockSpec((1,) + shard, lambda t: (chip_of(jnp.clip(t // per_chip - 1, 0, N_CHIPS - 1)), 0, 0))] + [ANY] * owners,
        out_shape=[_sds(slots, BF16)] * (1 + owners),
        scratch_shapes=[pltpu.VMEM(slots, BF16), pltpu.VMEM(slots, BF16), pltpu.VMEM(shard, F32),
                        pltpu.SemaphoreType.DMA((N_CHIPS,)), pltpu.SemaphoreType.DMA((N_CHIPS,))]
        + [pltpu.VMEM(slots, BF16), pltpu.SemaphoreType.DMA((N_CHIPS,)), pltpu.SemaphoreType.DMA((N_CHIPS,)), pltpu.SemaphoreType.DMA] * owners,
        vmem_mib=52, name="wgrad_" + kind + "_paired" + "_to_owners" * owners, comm=comm)
    return res[owners] if comm is None else (res[owners], landed)


def _mix_bwd(dx1, mix, attn, u, b, c, conv_w, g_attn, g_conv, g_post, w_out):
    def body(dx1_ref, mix_ref, a_ref, u_ref, b_ref, c_ref, up_ref, cp_ref, un_ref, cn_ref, cw_ref, ga_ref, gc_ref, gp_ref, w_ref,
             dmix_ref, da_ref, ds_ref, db_ref, dgp_ref, dga_ref, dgc_ref):
        i = pl.program_id(0)
        dx1 = dx1_ref[...]
        ym, rm = _rms(mix_ref[...].astype(F32))
        dg_post = _colsum(dx1 * ym)
        dmix = _rms_bwd(dx1 * gp_ref[...], ym, rm).astype(BF16)
        dmix_ref[...] = dmix
        dmerged = _dot_nt(dmix, w_ref[...])
        dna, dnc = dmerged[:, :ATTN_WIDTH], dmerged[:, ATTN_WIDTH:]
        ya, ra = _rms(a_ref[...])
        da_ref[...] = _rms_bwd(dna * ga_ref[...], ya, ra)
        _, _, _, s = _conv_parts(u_ref[...].astype(F32), c_ref[...].astype(F32), up_ref, cp_ref, un_ref, cn_ref, cw_ref[...], i)
        gate_b = b_ref[...].astype(F32)
        yc, rc = _rms(gate_b * s)
        dy = _rms_bwd(dnc * gc_ref[...], yc, rc)
        db_ref[...] = (dy * s).astype(BF16)
        ds_ref[...] = (dy * gate_b).astype(BF16)
        _accumulate([(dgp_ref, dg_post), (dga_ref, _colsum(dna * ya)), (dgc_ref, _colsum(dnc * yc))], i)

    a = ATTN_WIDTH
    return _call(
        body, (dx1, mix, attn, u, b, c, u, c, u, c, conv_w, g_attn, g_conv, g_post, w_out), grid=(SEQ // ROW_TILE,),
        in_specs=[_rows(D_MODEL)] * 2 + [_rows(a)] * 4 + [_halo_prev(a)] * 2 + [_halo_next(a)] * 2
        + [_const((3, a)), _const((1, a)), _const((1, a)), _const((1, D_MODEL)), _const1((D_MODEL, D_MODEL))],
        out_specs=[_rows(D_MODEL)] + [_rows(a)] * 3 + [_const((1, D_MODEL)), _const((1, a)), _const((1, a))],
        out_shape=[_sds((SEQ, D_MODEL), BF16), _sds((SEQ, a), F32), _sds((SEQ, a), BF16), _sds((SEQ, a), BF16),
                   _sds((1, D_MODEL), F32), _sds((1, a), F32), _sds((1, a), F32)],
        vmem_mib=40, name="mix_bwd")[0]


def _attn_bwd(q, k, v, do, o, lse, comm=None):
    group = 8

    n_pairs = ATTN_WIDTH // LANES

    def body(q_hbm, k_hbm, v_hbm, do_hbm, o_hbm, lse_hbm, dq_ref, dk_ref, dv_ref,
             q_ref, k_ref, v_ref, do_ref, o_ref, lse_ref, in_sem,
             q32, k32, v32, qa, qb, doa, dob, kp, vp, lsep, dlp, dnat, dqp, dkp, dvp, bias_ref):
        pair = pl.program_id(0)
        sources = (q_hbm, k_hbm, v_hbm, do_hbm, o_hbm, lse_hbm)
        buffers = (q_ref, k_ref, v_ref, do_ref, o_ref, lse_ref)

        def fetch(i, p):
            return pltpu.make_async_copy(sources[i].at[:, pl.ds(pl.multiple_of(p * LANES, LANES), LANES)], buffers[i], in_sem.at[i])

        def prefetch(*which):
            @pl.when(pair + 1 < n_pairs)
            def _():
                for i in which:
                    fetch(i, pair + 1).start()

        @pl.when(pair == 0)
        def _():
            for i in range(len(sources)):
                fetch(i, 0).start()

        for i in range(len(sources)):
            fetch(i, pair).wait()

        first, mask_a, mask_b = _head_masks()
        _zero_key_padding((kp, vp))
        _write_band_bias(bias_ref)
        q32[...] = q_ref[...].astype(F32)
        k32[...] = k_ref[...].astype(F32)
        v32[...] = v_ref[...].astype(F32)
        prefetch(0, 1, 2)
        for c in range(SEQ // PERM_CHUNK):
            prod = do_ref[_chunk(c), :] * o_ref[_chunk(c), :]
            d_a = jnp.sum(prod * mask_a, axis=1, keepdims=True)
            d_b = jnp.sum(prod * mask_b, axis=1, keepdims=True)
            dnat[_chunk(c), :] = jnp.where(first, d_a, d_b)
        prefetch(4)

        for step, dil in enumerate(DILATIONS[1:] + DILATIONS[:1]):
            length = SEQ // dil
            assert length >= 2 * Q_BLOCK
            chunks = _perm_chunks(dil)
            for c, rows in chunks:
                val = q32[rows, :]
                qa[_chunk(c), :] = (val * mask_a).astype(BF16)
                qb[_chunk(c), :] = (val * mask_b).astype(BF16)
                val = do_ref[rows, :]
                doa[_chunk(c), :] = (val * mask_a).astype(BF16)
                dob[_chunk(c), :] = (val * mask_b).astype(BF16)
                kp[_chunk(c, HALF_WINDOW), :] = k32[rows, :].astype(BF16)
                vp[_chunk(c, HALF_WINDOW), :] = v32[rows, :].astype(BF16)
                lsep[_chunk(c), :] = lse_ref[rows, :]
                dlp[_chunk(c), :] = dnat[rows, :]
            if step == len(DILATIONS) - 1:
                prefetch(3, 5)
            zero = jnp.zeros((PERM_CHUNK, LANES), F32)
            for c in range(SEQ // PERM_CHUNK):
                dkp[_chunk(c), :] = zero
                dvp[_chunk(c), :] = zero
            dkp[pl.ds(SEQ, 2 * HALF_WINDOW), :] = zero[:2 * HALF_WINDOW]
            dvp[pl.ds(SEQ, 2 * HALF_WINDOW), :] = zero[:2 * HALF_WINDOW]

            heads = ((qa, doa, 0), (qb, dob, HEAD_DIM))

            def blocks(i, carry, length=length):
                base = pl.multiple_of(i * (group * Q_BLOCK), group * Q_BLOCK)
                starts = [base + g * Q_BLOCK for g in range(group)]
                raw = [[(_dot_nt(qx[pl.ds(m0, Q_BLOCK), :], kp[pl.ds(m0, K_WINDOW), :]),
                         _dot_nt(dox[pl.ds(m0, Q_BLOCK), :], vp[pl.ds(m0, K_WINDOW), :])) for qx, dox, _ in heads]
                       for m0 in starts]
                grads = []
                for m0, pair in zip(starts, raw):
                    bias = bias_ref[_band_bias_index(m0, length)]
                    lse_b, d_b = lsep[pl.ds(m0, Q_BLOCK), :], dlp[pl.ds(m0, Q_BLOCK), :]
                    out = []
                    for (s, dp), (_, _, col) in zip(pair, heads):
                        p = jnp.exp(s + bias - lse_b[:, col:col + 1])
                        out.append(((p * (dp - d_b[:, col:col + 1])).astype(BF16), p.astype(BF16)))
                    grads.append(out)
                for m0, out in zip(starts, grads):
                    qrows, krows = pl.ds(m0, Q_BLOCK), pl.ds(m0, K_WINDOW)
                    kw = kp[krows, :]
                    dk = jnp.zeros((K_WINDOW, LANES), F32)
                    dv = jnp.zeros((K_WINDOW, LANES), F32)
                    for (ds, p), (qx, dox, _) in zip(out, heads):
                        dk = dk + _dot_tn(ds, qx[qrows, :])
                        dv = dv + _dot_tn(p, dox[qrows, :])
                    dqp[qrows, :] = jnp.where(first, _dot(out[0][0], kw), _dot(out[1][0], kw)) * (HEAD_DIM ** -0.5)
                    dkp[krows, :] += dk
                    dvp[krows, :] += dv
                return carry

            lax.fori_loop(0, SEQ // (group * Q_BLOCK), blocks, 0)

            for c, rows in chunks:
                g_q, g_k, g_v = dqp[_chunk(c), :], dkp[_chunk(c, HALF_WINDOW), :], dvp[_chunk(c, HALF_WINDOW), :]
                if step == 0:
                    dq_ref[rows, :] = g_q
                    dk_ref[rows, :] = g_k
                    dv_ref[rows, :] = g_v
                else:
                    dq_ref[rows, :] = dq_ref[rows, :] + g_q
                    dk_ref[rows, :] = dk_ref[rows, :] + g_k
                    dv_ref[rows, :] = dv_ref[rows, :] + g_v

    col = pl.BlockSpec((SEQ, LANES), lambda h: (0, h))
    padded = SEQ + 2 * HALF_WINDOW
    return _call(
        body, (q, k, v, do, o, lse), grid=(n_pairs,), in_specs=[ANY] * 6, out_specs=[col] * 3,
        out_shape=[_sds((SEQ, ATTN_WIDTH), F32)] * 3,
        scratch_shapes=[pltpu.VMEM((SEQ, LANES), BF16)] * 3 + [pltpu.VMEM((SEQ, LANES), F32)] * 3 + [pltpu.SemaphoreType.DMA((6,))]
        + [pltpu.VMEM((SEQ, LANES), F32)] * 3 + [pltpu.VMEM((SEQ, LANES), BF16)] * 4
        + [pltpu.VMEM((padded, LANES), BF16)] * 2 + [pltpu.VMEM((SEQ, LANES), F32)] * 4 + [pltpu.VMEM((padded, LANES), F32)] * 2
        + [pltpu.VMEM((3, Q_BLOCK, K_WINDOW), F32)],
        vmem_mib=56, name="attn_bwd", comm=comm)


def _inproj_bwd(dq, dk, dv, ds, db, u, c, conv_w, tc, tsa, tsb, w_in, x, g_pre, dx1):
    def body(dq_ref, dk_ref, dv_ref, ds_ref, db_ref, u_ref, c_ref, dsp_ref, up_ref, cp_ref, dsn_ref, un_ref, cn_ref, cw_ref,
             tc_ref, tsa_ref, tsb_ref, w_ref, x_ref, g_ref, dx1_ref, dx_ref, dproj_ref, h_ref, dg_ref, dcw_ref):
        i = pl.program_id(0)
        cw = cw_ref[...]
        u, c = u_ref[...].astype(F32), c_ref[...].astype(F32)
        t, t_prev, t_next, _ = _conv_parts(u, c, up_ref, cp_ref, un_ref, cn_ref, cw, i)
        ds = ds_ref[...].astype(F32)
        ds_prev, ds_next = _shifted(ds, _last_row(dsp_ref), _first_row(dsn_ref), i)
        dt = cw[0:1, :] * ds_next + cw[1:2, :] * ds + cw[2:3, :] * ds_prev
        d_taps = jnp.concatenate([_colsum(ds * t_prev), _colsum(ds * t), _colsum(ds * t_next)], axis=0)
        tc_, tsa_, tsb_ = tc_ref[...], tsa_ref[...], tsb_ref[...]
        groups = ((2, lambda: dv_ref[...].astype(BF16)), (4, lambda: db_ref[...]),
                  (0, lambda: _rotate_transposed(dq_ref[...], tc_, tsa_, tsb_).astype(BF16)),
                  (1, lambda: _rotate_transposed(dk_ref[...], tc_, tsa_, tsb_).astype(BF16)),
                  (3, lambda: (dt * c).astype(BF16)), (5, lambda: (dt * u).astype(BF16)))
        dh = jnp.zeros((x_ref.shape[0], D_MODEL), F32)
        for n, make in groups:
            cols = pl.ds(n * a, a)
            part = make()
            dproj_ref[:, cols] = part
            dh = dh + _dot_nt(part, w_ref[:, cols])
        y, r = _rms(x_ref[...])
        h_ref[...] = (y * g_ref[...]).astype(BF16).T
        dx_ref[...] = dx1_ref[...] + _rms_bwd(dh * g_ref[...], y, r)
        _accumulate([(dg_ref, _colsum(dh * y)), (dcw_ref, d_taps)], i)

    a = ATTN_WIDTH
    tile = FFN_TILE

    def rows(width):
        return _rows(width, tile)

    return _call(
        body, (dq, dk, dv, ds, db, u, c, ds, u, c, ds, u, c, conv_w, tc, tsa, tsb, w_in, x, g_pre, dx1), grid=(SEQ // tile,),
        in_specs=[rows(a)] * 7 + [_halo_prev(a, tile)] * 3 + [_halo_next(a, tile)] * 3
        + [_const((3, a)), rows(LANES), rows(LANES), rows(LANES), _const1((D_MODEL, IN_PROJ_WIDTH)), rows(D_MODEL),
           _const((1, D_MODEL)), rows(D_MODEL)],
        out_specs=[rows(D_MODEL), rows(IN_PROJ_WIDTH), _cols(D_MODEL, tile), _const((1, D_MODEL)), _const((3, a))],
        out_shape=[_sds((SEQ, D_MODEL), F32), _sds((SEQ, IN_PROJ_WIDTH), BF16), _sds((D_MODEL, SEQ), BF16),
                   _sds((1, D_MODEL), F32), _sds((3, a), F32)],
        vmem_mib=48, name="inproj_bwd")


def _adamw_math(w, g, m, v):
    m = ADAM_B1 * m + (1.0 - ADAM_B1) * g
    v = ADAM_B2 * v + (1.0 - ADAM_B2) * (g * g)
    m_hat = m / (1.0 - ADAM_B1 ** ADAM_STEP)
    v_hat = v / (1.0 - ADAM_B2 ** ADAM_STEP)
    delta = -ADAM_LR * (m_hat / (jnp.sqrt(v_hat) + ADAM_EPS) + ADAM_WD * w)
    return delta, m, v


def _sum_parts(p_ref):
    g = p_ref[0].astype(F32)
    for k in range(1, p_ref.shape[0]):
        g = g + p_ref[k].astype(F32)
    return g


def _adamw_layers(parts, w, m, v, row_tile, name, comm=None):
    _, rows, cols = w.shape
    n_tiles = rows // row_tile

    def body(*refs):
        p_refs = refs[:DEPTH]
        w_ref, m_ref, v_ref, g_ref, d_ref, nm_ref, nv_ref = refs[DEPTH:]
        layer = pl.program_id(0)
        for l, p_ref in enumerate(p_refs):
            @pl.when(layer == l)
            def _(p_ref=p_ref):
                g = _sum_parts(p_ref)
                g_ref[0] = g
                d_ref[0], nm_ref[0], nv_ref[0] = _adamw_math(w_ref[0], g, m_ref[0], v_ref[0])

    def part_spec(l):
        return pl.BlockSpec((parts[l].shape[0], row_tile, cols),
                            lambda layer, i: (0, jnp.where(layer == l, i, jnp.where(layer < l, 0, n_tiles - 1)), 0))

    tile = pl.BlockSpec((1, row_tile, cols), lambda layer, i: (layer, i, 0))
    return _call(body, (*parts, w, m, v), grid=(DEPTH, n_tiles), in_specs=[part_spec(l) for l in range(DEPTH)] + [tile] * 3,
                 out_specs=[tile] * 4, out_shape=[_sds(w.shape, F32)] * 4, name=name, comm=comm)


def _adamw_small(parts, w, m, v):
    n = len(GAIN_NAMES)
    a = ATTN_WIDTH
    where = ((0, 0), (1, 0), (2, 0), (3, 0), (4, 0), (4, a))

    def body(*refs):
        p_ref = refs[0]
        w_refs, m_refs, v_refs = refs[1:1 + n], refs[1 + n:1 + 2 * n], refs[1 + 2 * n:1 + 3 * n]
        outs = refs[1 + 3 * n:1 + 7 * n]
        taps_ref, loss_ref, sum_ref = refs[1 + 7 * n:]
        sum_ref[...] = _sum_parts(p_ref)
        for k, (row, col) in enumerate(where):
            width = w_refs[k].shape[1]
            g = jnp.concatenate([sum_ref[pl.ds(l * SMALL_ROWS + row, 1), pl.ds(col, width)] for l in range(DEPTH)], axis=0)
            g_ref, d_ref, nm_ref, nv_ref = outs[4 * k:4 * k + 4]
            g_ref[...] = g
            d_ref[...], nm_ref[...], nv_ref[...] = _adamw_math(w_refs[k][...], g, m_refs[k][...], v_refs[k][...])
        for l in range(DEPTH):
            for tap, (row, col) in enumerate(((5, 0), (5, a), (6, 0))):
                taps_ref[l, tap:tap + 1, :] = sum_ref[l * SMALL_ROWS + row:l * SMALL_ROWS + row + 1, col:col + a]
        loss_ref[...] = sum_ref[SMALL_ROWS - 1:SMALL_ROWS, 0:1]

    out_shape = [_sds(x.shape, F32) for x in w for _ in range(4)] + [_sds((DEPTH, 3, a), F32), _sds((1, 1), F32)]
    res = pl.pallas_call(body, out_shape=out_shape, scratch_shapes=[pltpu.VMEM(parts.shape[1:], F32)],
                         name="adamw_small")(parts, *w, *m, *v)
    return [res[4 * k:4 * k + 4] for k in range(n)], res[4 * n], res[4 * n + 1]


def _adamw_plain(g, w, m, v, name):
    def body(g_ref, w_ref, m_ref, v_ref, d_ref, nm_ref, nv_ref):
        d_ref[...], nm_ref[...], nv_ref[...] = _adamw_math(w_ref[...], g_ref[...], m_ref[...], v_ref[...])

    return pl.pallas_call(body, out_shape=[_sds(w.shape, F32)] * 3, name=name)(g, w, m, v)


SMALL_ROWS = 8
GAIN_NAMES = ("pre_mix_norm", "post_mix_norm", "pre_ffn_norm", "post_ffn_norm", "attn_out_norm", "conv_out_norm")


def _pack_small(pre_mix, post_mix, pre_ffn, post_ffn, attn_out, conv_out, taps):
    zeros = jnp.zeros((1, ATTN_WIDTH), F32)
    return jnp.concatenate([
        pre_mix, post_mix, pre_ffn, post_ffn, jnp.concatenate([attn_out, conv_out], axis=1),
        jnp.concatenate([taps[0:1], taps[1:2]], axis=1), jnp.concatenate([taps[2:3], zeros], axis=1),
        jnp.zeros((1, D_MODEL), F32)], axis=0)


def _rope(positions, comm=None):
    inv_freq = ROPE_THETA ** (-jnp.arange(0, ROPE_DIM, 2, dtype=F32) / ROPE_DIM)
    per_head = jnp.concatenate([inv_freq, inv_freq, jnp.zeros((HEAD_DIM - ROPE_DIM,), F32)])
    freq_row = jnp.tile(per_head, LANES // HEAD_DIM).reshape(1, LANES)
    return _rope_tables(positions.reshape(SEQ, 1), freq_row, comm)


def _layer_forward(h, gains, taps, tables, w, inproj_comm=None, attn_comm=None, mix_comm=None, ffn_comm=None, target=None):
    def arrived(comm, landed):
        return w if comm is None else {**w, **dict(zip(comm.kinds, landed))}

    (q, k, v, u, b, c), landed = _inproj_fwd(h, gains["pre_mix_norm"], w["in"], *tables, comm=inproj_comm)
    w = arrived(inproj_comm, landed)
    (attn, lse), landed = _attn_fwd(q, k, v, comm=attn_comm)
    w = arrived(attn_comm, landed)
    (x1, merged, mix), landed = _mix_fwd(attn, u, b, c, taps, gains["attn_out_norm"], gains["conv_out_norm"], w["out"], h,
                                         gains["post_mix_norm"], comm=mix_comm)
    w = arrived(mix_comm, landed)
    (x2, gu, f, *loss), landed_next = _ffn_fwd(x1, gains["pre_ffn_norm"], w["gu"], w["dn"], gains["post_ffn_norm"], comm=ffn_comm,
                                               target=target)
    out = x2 if target is None else (x2, loss[0])
    return out, (h, q, k, v, u, b, c, attn, lse, merged, mix, x1, gu, f), w, landed_next


class _GradientExchange:
    def ffn_grads(self, act, df, h2, dgu):
        return dict(dn=_wgrad_paired(act, df, "dn"), gu=_wgrad_paired(dgu, h2, "gu"))

    def w_out_grad(self, merged_t, dmix):
        return _wgrad(merged_t, dmix, "wgrad_out")

    def attention_passenger(self, ffn, g_out):
        return _Together(_ScatterChips(("gu", "dn"), (ffn["gu"], ffn["dn"])), _Scatter(("out",), (g_out,)))

    def w_in_grad(self, h_t, dproj, small_rows):
        if small_rows is None:
            return _wgrad(h_t, dproj, "wgrad_in"), None
        parts, landed = _wgrad_paired(h_t, dproj, "in", to_owners=True, comm=_GatherSmall(small_rows, "small"))
        return parts, landed[0]


def _layer_backward(dx, saved, gains, taps, tables, w, ffn_comm, exchange, last):
    x0, q, k, v, u, b, c, attn, lse, merged, mix, x1, gu, f = saved
    (dx1, df, act, dgu, h2, dg_post_ffn, dg_pre_ffn), landed_prev = _ffn_bwd(
        dx, f, x1, gu, w["gu"], w["dn"], gains["post_ffn_norm"], gains["pre_ffn_norm"], comm=ffn_comm)
    ffn = exchange.ffn_grads(act, df, h2, dgu)
    dmix, dattn, ds, db, dg_post_mix, dg_attn, dg_conv = _mix_bwd(
        dx1, mix, attn, u, b, c, taps, gains["attn_out_norm"], gains["conv_out_norm"], gains["post_mix_norm"], w["out"])
    g_out = exchange.w_out_grad(merged, dmix)
    attn_comm = exchange.attention_passenger(ffn, g_out)
    (dq, dk, dv), landed = _attn_bwd(q, k, v, dattn, attn, lse, comm=attn_comm)
    dx0, dproj, h1, dg_pre_mix, dtaps = _inproj_bwd(
        dq, dk, dv, ds, db, u, c, taps, *tables, w["in"], x0, gains["pre_mix_norm"], dx1)[0]
    small = _pack_small(dg_pre_mix, dg_post_mix, dg_pre_ffn, dg_post_ffn, dg_attn, dg_conv, dtaps)
    small_rows = None
    if last is not None:
        small_above, local_loss = last
        small_rows = lax.dynamic_update_slice(jnp.concatenate([small, small_above], axis=0), local_loss, (SMALL_ROWS - 1, 0))
    g_in, gathered = exchange.w_in_grad(h1, dproj, small_rows)
    rest = dict(zip(attn_comm.kinds, landed)) if attn_comm is not None else dict(ffn, out=g_out)
    return dx0, g_in, small, landed_prev, rest, gathered


def kernel(x, positions, pre_mix_norm, w_in, conv_w, attn_out_norm, conv_out_norm, w_out, post_mix_norm, pre_ffn_norm, w_gate_up, w_down, post_ffn_norm, loss_target, m_pre_mix_norm, m_w_in, m_conv_w, m_attn_out_norm, m_conv_out_norm, m_w_out, m_post_mix_norm, m_pre_ffn_norm, m_w_gate_up, m_w_down, m_post_ffn_norm, v_pre_mix_norm, v_w_in, v_conv_w, v_attn_out_norm, v_conv_out_norm, v_w_out, v_post_mix_norm, v_pre_ffn_norm, v_w_gate_up, v_w_down, v_post_ffn_norm):
    mx, my, mc = _place()
    me = _block_of(mx, my, mc)
    conv_channels = conv_w.shape[-1]

    def hidden_major(a):
        return jnp.swapaxes(a, 1, 2)

    big_w = dict(zip(WEIGHT_KINDS, (w_in, w_out, hidden_major(w_gate_up), w_down)))
    big_m = dict(zip(WEIGHT_KINDS, (m_w_in, m_w_out, hidden_major(m_w_gate_up), m_w_down)))
    big_v = dict(zip(WEIGHT_KINDS, (v_w_in, v_w_out, hidden_major(v_w_gate_up), v_w_down)))
    shards = {kind: big_w[kind].astype(BF16) for kind in WEIGHT_KINDS}
    all_gains = dict(pre_mix_norm=pre_mix_norm, attn_out_norm=attn_out_norm, conv_out_norm=conv_out_norm, post_mix_norm=post_mix_norm,
                     pre_ffn_norm=pre_ffn_norm, post_ffn_norm=post_ffn_norm)

    def gather(kinds, l):
        return _Gather(kinds, [shards[kind] for kind in kinds], l)

    def gains(l):
        return {name: g[l:l + 1] for name, g in all_gains.items()}

    early = ("in", "out", "dn")
    taps_flat = jnp.pad(conv_w.reshape(-1), (0, 8 * LANES - conv_w.size)).reshape(8, LANES)
    tables, (first, taps_all) = _rope(positions, _Together(gather(early[:1], 0), _GatherSmall(taps_flat, "taps")))
    taps_all = taps_all.reshape(N_DEV, 8 * LANES)[:, :conv_w.size]
    conv_w_full = taps_all.reshape(N_DEV, DEPTH, 3, conv_channels).transpose(1, 2, 0, 3).reshape(DEPTH, 3, ATTN_WIDTH)
    weights = [dict(zip(early[:1], [first]))] + [None] * (DEPTH - 1)
    saved = [None] * DEPTH
    h = x[0]
    for l in range(DEPTH):
        inproj_comm = gather(("out",), 0) if l == 0 else None
        mix_comm = gather(("dn",), 0) if l == 0 else None
        ffn_comm = gather(early, l + 1) if l + 1 < DEPTH else None
        target = loss_target[0] if l + 1 == DEPTH else None
        h, saved[l], weights[l], landed = _layer_forward(h, gains(l), conv_w_full[l], tables, weights[l], inproj_comm,
                                                         gather(("gu",), l), mix_comm, ffn_comm, target)
        if ffn_comm is not None:
            weights[l + 1] = dict(zip(ffn_comm.kinds, landed))
    dx, local_loss = h

    parts = {kind: [None] * DEPTH for kind in WEIGHT_KINDS}
    small_grads = [None] * DEPTH
    g_in_above = None
    exchange = _GradientExchange()
    for l in reversed(range(DEPTH)):
        ffn_comm = _Scatter(("in",), (g_in_above,)) if g_in_above is not None else None
        last = (jnp.concatenate(small_grads[1:], axis=0), local_loss) if l == 0 else None
        dx, g_in_above, small_grads[l], landed, rest, gathered = _layer_backward(
            dx, saved[l], gains(l), conv_w_full[l], tables, weights[l], ffn_comm, exchange, last)
        if ffn_comm is not None:
            parts["in"][l + 1] = landed[0]
        for kind, part in rest.items():
            parts[kind][l] = part

    parts["in"][0] = g_in_above
    tiles = {"in": 256, "out": 128, "gu": 176, "dn": 176}
    big = {kind: _adamw_layers(parts[kind], big_w[kind], big_m[kind], big_v[kind], tiles[kind], "adamw_" + kind)[0]
           for kind in WEIGHT_KINDS}

    small_m = (m_pre_mix_norm, m_post_mix_norm, m_pre_ffn_norm, m_post_ffn_norm, m_attn_out_norm, m_conv_out_norm)
    small_v = (v_pre_mix_norm, v_post_mix_norm, v_pre_ffn_norm, v_post_ffn_norm, v_attn_out_norm, v_conv_out_norm)
    per_gain, taps_grad_full, loss = _adamw_small(gathered, [all_gains[name] for name in GAIN_NAMES], small_m, small_v)
    small = [{name: per_gain[k][o] for k, name in enumerate(GAIN_NAMES)} for o in range(4)]
    loss = loss[0, 0]
    taps_grad = lax.dynamic_slice_in_dim(taps_grad_full, me * conv_channels, conv_channels, axis=2)

    taps = [taps_grad, *_adamw_plain(taps_grad, conv_w, m_conv_w, v_conv_w, "adamw_taps")]

    def leaves(o):
        s = small[o]
        return (s["pre_mix_norm"], big["in"][o], taps[o], s["attn_out_norm"], s["conv_out_norm"], big["out"][o], s["post_mix_norm"],
                s["pre_ffn_norm"], hidden_major(big["gu"][o]), big["dn"][o], s["post_ffn_norm"])

    return (loss, dx[None], *leaves(0), *leaves(1), *leaves(2), *leaves(3))
```

```python
import math

import jax
import jax.numpy as jnp
from jax import lax
from jax.experimental import pallas as pl
from jax.experimental.pallas import tpu as pltpu

F32 = jnp.float32
BF16 = jnp.bfloat16
MESH = pl.DeviceIdType.MESH

SEQ = 4096
D_MODEL = 1024
DEPTH = 4
N_DEV = 8
ATTN_WIDTH = 512
IN_PROJ_WIDTH = 3072
FFN_HIDDEN = 2816
FFN_BLOCK = 2 * FFN_HIDDEN // N_DEV
W_IN_BLOCK = IN_PROJ_WIDTH // N_DEV
W_OUT_BLOCK = D_MODEL // N_DEV
W_DOWN_BLOCK = FFN_HIDDEN // N_DEV
HEAD_DIM = 64
ROPE_DIM = 16
ROPE_THETA = 500000.0
DILATIONS = (1, 4, 16)
HALF_WINDOW = 64
RMS_EPS = 1e-6
NEG_INF = -1e30
LANES = 128
Q_BLOCK = 128
K_WINDOW = Q_BLOCK + 2 * HALF_WINDOW
PERM_CHUNK = 256
ROW_TILE = 512
FFN_TILE = 256
WGRAD_COLS = 512
ADAM_LR, ADAM_B1, ADAM_B2, ADAM_EPS, ADAM_WD, ADAM_STEP = 0.001, 0.9, 0.999, 1e-08, 0.01, 10
MIB = 1024 * 1024
PINNED_BYTES = 256 * 1024

WEIGHT_KINDS = ("in", "out", "gu", "dn")
FULL_SHAPES = {"in": (D_MODEL, IN_PROJ_WIDTH), "out": (D_MODEL, D_MODEL), "gu": (N_DEV, FFN_BLOCK, D_MODEL), "dn": (FFN_HIDDEN, D_MODEL)}
SHARD_SHAPES = {"in": (D_MODEL, W_IN_BLOCK), "out": (W_OUT_BLOCK, D_MODEL), "gu": (FFN_BLOCK, D_MODEL), "dn": (W_DOWN_BLOCK, D_MODEL)}
ANY = pl.BlockSpec(memory_space=pl.ANY)


def _sds(shape, dtype):
    return jax.ShapeDtypeStruct(shape, dtype)


def _rows(width, tile=ROW_TILE):
    return pl.BlockSpec((tile, width), lambda i: (i, 0))


def _frows(width):
    return _rows(width, FFN_TILE)


def _cols(height, tile):
    return pl.BlockSpec((height, tile), lambda i: (0, i))


def _const(shape):
    return pl.BlockSpec(shape, lambda i: (0,) * len(shape))


def _const1(shape):
    return pl.BlockSpec(shape, lambda i: (0,) * len(shape), pipeline_mode=pl.Buffered(1))


HALO_ROWS = 16


def _halo_prev(width, tile=ROW_TILE):
    return pl.BlockSpec((HALO_ROWS, width), lambda i: (jnp.maximum(i * (tile // HALO_ROWS) - 1, 0), 0))


def _halo_next(width, tile=ROW_TILE):
    return pl.BlockSpec((HALO_ROWS, width), lambda i: (jnp.minimum((i + 1) * (tile // HALO_ROWS), SEQ // HALO_ROWS - 1), 0))


def _rms(x):
    r = lax.rsqrt(jnp.mean(x * x, axis=-1, keepdims=True) + RMS_EPS)
    return x * r, r


def _rms_bwd(dn, y, r):
    return r * (dn - y * jnp.mean(dn * y, axis=-1, keepdims=True))


def _dot(a, b):
    return jnp.dot(a, b, preferred_element_type=F32)


def _dot_nt(a, b):
    return lax.dot_general(a, b, (((1,), (1,)), ((), ())), preferred_element_type=F32)


def _dot_tn(a, b):
    return lax.dot_general(a, b, (((0,), (0,)), ((), ())), preferred_element_type=F32)


def _place():
    return lax.axis_index("x"), lax.axis_index("y"), lax.axis_index("c")


def _block_of(px, py, pc):
    return 4 * px + 2 * py + pc


def _weight_block(ref, kind, blk):
    if kind == "in":
        return ref.at[:, pl.ds(blk * W_IN_BLOCK, W_IN_BLOCK)]
    if kind == "out":
        return ref.at[pl.ds(blk * W_OUT_BLOCK, W_OUT_BLOCK), :]
    if kind == "gu":
        return ref.at[blk]
    return ref.at[pl.ds(blk * W_DOWN_BLOCK, W_DOWN_BLOCK), :]


def _dma_semaphores(n):
    return [pltpu.SemaphoreType.DMA((n, 7)), pltpu.SemaphoreType.DMA((n, 7)), pltpu.SemaphoreType.DMA((n,))]


class _Gather:
    def __init__(self, kinds, shards, layer):
        self.kinds, self.operands, self.layer = tuple(kinds), list(shards), layer
        self.tag = "gather_" + "_".join(kinds)
        self.out_shape = [_sds(FULL_SHAPES[k], BF16) for k in kinds]
        self.scratch = _dma_semaphores(len(kinds))

    def _parties(self):
        x, y, c = _place()
        return (x, y, c), (x, y, 1 - c), [(1 - x, y), (x, 1 - y), (1 - x, 1 - y)], c

    def _copy(self, outs, sems, w, k, block, to, src=None):
        dst = _weight_block(outs[w], self.kinds[w], _block_of(*block))
        return pltpu.make_async_remote_copy(src_ref=dst if src is None else src, dst_ref=dst, send_sem=sems[0].at[w, k],
                                            recv_sem=sems[1].at[w, k], device_id=to, device_id_type=MESH)

    def _own(self, srcs, outs, sems, w, me):
        return pltpu.make_async_copy(srcs[w].at[self.layer], _weight_block(outs[w], self.kinds[w], _block_of(*me)), sems[2].at[w])

    def _first(self, srcs, outs, sems, w):
        me, sibling, chips, c = self._parties()
        shard = srcs[w].at[self.layer]
        return [self._copy(outs, sems, w, 0, me, sibling, src=shard)] + [
            self._copy(outs, sems, w, 1 + j, me, (*chip, c), src=shard) for j, chip in enumerate(chips)]

    def start(self, srcs, outs, sems):
        me = self._parties()[0]
        for w in range(len(self.kinds)):
            self._own(srcs, outs, sems, w, me).start()
            for cp in self._first(srcs, outs, sems, w):
                cp.start()

    def forward(self, srcs, outs, sems):
        me, sibling, chips, c = self._parties()
        for j, chip in enumerate(chips):
            for w in range(len(self.kinds)):
                self._copy(outs, sems, w, 1 + j, (*chip, c), me).wait_recv()
                self._copy(outs, sems, w, 4 + j, (*chip, c), sibling).start()

    def finish(self, srcs, outs, sems):
        me, sibling, chips, c = self._parties()
        for w in range(len(self.kinds)):
            self._copy(outs, sems, w, 0, sibling, me).wait_recv()
            for j, chip in enumerate(chips):
                self._copy(outs, sems, w, 4 + j, (*chip, 1 - c), me).wait_recv()
        for w in range(len(self.kinds)):
            for cp in self._first(srcs, outs, sems, w):
                cp.wait_send()
            for j, chip in enumerate(chips):
                self._copy(outs, sems, w, 4 + j, (*chip, c), sibling).wait_send()
            self._own(srcs, outs, sems, w, me).wait()


def _peers(x, y, c):
    return [(x ^ a, y ^ b, c ^ e) for a in (0, 1) for b in (0, 1) for e in (0, 1) if (a, b, e) != (0, 0, 0)]


class _Scatter:
    def __init__(self, kinds, grads):
        self.kinds, self.operands = tuple(kinds), list(grads)
        self.tag = "scatter_" + "_".join(kinds)
        self.out_shape = [_sds((N_DEV,) + SHARD_SHAPES[k], BF16) for k in kinds]
        self.scratch = _dma_semaphores(len(kinds))

    def _copies(self, srcs, outs, sems):
        x, y, c = _place()
        me = _block_of(x, y, c)
        copies = []
        for w, kind in enumerate(self.kinds):
            copies.append(pltpu.make_async_copy(_weight_block(srcs[w], kind, me), outs[w].at[me], sems[2].at[w]))
            for k, peer in enumerate(_peers(x, y, c)):
                copies.append(pltpu.make_async_remote_copy(
                    src_ref=_weight_block(srcs[w], kind, _block_of(*peer)), dst_ref=outs[w].at[me],
                    send_sem=sems[0].at[w, k], recv_sem=sems[1].at[w, k], device_id=peer, device_id_type=MESH))
        return copies

    def start(self, srcs, outs, sems):
        for cp in self._copies(srcs, outs, sems):
            cp.start()

    def forward(self, srcs, outs, sems):
        pass

    def finish(self, srcs, outs, sems):
        for cp in self._copies(srcs, outs, sems):
            cp.wait()


N_CHIPS = N_DEV // 2


class _ScatterChips:
    def __init__(self, kinds, blocks):
        self.kinds, self.operands = tuple(kinds), list(blocks)
        self.tag = "scatter_chips_" + "_".join(kinds)
        self.out_shape = [_sds((N_CHIPS,) + SHARD_SHAPES[k], BF16) for k in kinds]
        self.scratch = _dma_semaphores(len(kinds))

    def _copies(self, srcs, outs, sems):
        x, y, c = _place()
        mine = 2 * x + y
        copies = []
        for w in range(len(self.kinds)):
            copies.append(pltpu.make_async_copy(srcs[w].at[mine], outs[w].at[mine], sems[2].at[w]))
            for j, (px, py) in enumerate([(1 - x, y), (x, 1 - y), (1 - x, 1 - y)]):
                copies.append(pltpu.make_async_remote_copy(
                    src_ref=srcs[w].at[2 * px + py], dst_ref=outs[w].at[mine], send_sem=sems[0].at[w, j], recv_sem=sems[1].at[w, j],
                    device_id=(px, py, c), device_id_type=MESH))
        return copies

    def start(self, srcs, outs, sems):
        for cp in self._copies(srcs, outs, sems):
            cp.start()

    def forward(self, srcs, outs, sems):
        pass

    def finish(self, srcs, outs, sems):
        for cp in self._copies(srcs, outs, sems):
            cp.wait()


def _in_hbm(a):
    return pltpu.with_memory_space_constraint(a, pltpu.HBM) if a.size * a.dtype.itemsize >= PINNED_BYTES else a


def _out_hbm(s):
    return pltpu.HBM(s.shape, s.dtype) if math.prod(s.shape) * jnp.dtype(s.dtype).itemsize >= PINNED_BYTES else s


def _call(body, args, *, grid, in_specs, out_specs, out_shape, scratch_shapes=(), vmem_mib=None, name, comm=None):
    kwargs = {} if vmem_mib is None else dict(compiler_params=pltpu.CompilerParams(vmem_limit_bytes=vmem_mib * MIB))
    in_specs, out_specs, out_shape, scratch_shapes = list(in_specs), list(out_specs), list(out_shape), list(scratch_shapes)
    args = [_in_hbm(a) for a in args]
    out_shape = [_out_hbm(s) for s in out_shape]
    if comm is None:
        res = pl.pallas_call(body, grid=grid, in_specs=in_specs, out_specs=out_specs, out_shape=out_shape,
                             scratch_shapes=scratch_shapes, name=name, **kwargs)(*args)
        return list(res), None
    n_in, n_out, n_scr = len(in_specs), len(out_specs), len(scratch_shapes)
    c_in, c_out = len(comm.operands), len(comm.out_shape)
    last = math.prod(grid) - 1

    def carried(*refs):
        cuts = [n_in, c_in, n_out, c_out, n_scr]
        parts, at = [], 0
        for n in cuts:
            parts.append(refs[at:at + n])
            at += n
        ins, c_ins, outs, c_outs, scr = parts
        sems = refs[at:]
        step = pl.program_id(0)
        for axis in range(1, len(grid)):
            step = step * grid[axis] + pl.program_id(axis)

        @pl.when(step == 0)
        def _():
            comm.start(c_ins, c_outs, sems)

        @pl.when(step == last)
        def _():
            comm.forward(c_ins, c_outs, sems)

        body(*ins, *outs, *scr)

        @pl.when(step == last)
        def _():
            comm.finish(c_ins, c_outs, sems)

    res = pl.pallas_call(carried, grid=grid, in_specs=in_specs + [ANY] * c_in, out_specs=out_specs + [ANY] * c_out,
                         out_shape=out_shape + [_out_hbm(s) for s in comm.out_shape], scratch_shapes=scratch_shapes + comm.scratch,
                         name=name + "_" + comm.tag, **kwargs)(*args, *[_in_hbm(a) for a in comm.operands])
    return list(res[:n_out]), list(res[n_out:])


class _GatherSmall:
    def __init__(self, v, tag):
        self.operands, self.tag = [v], tag
        self.out_shape = [_sds((N_DEV,) + v.shape, F32)]
        self.scratch = _dma_semaphores(1)

    def _copies(self, srcs, outs, sems):
        x, y, c = _place()
        me = _block_of(x, y, c)
        copies = [pltpu.make_async_copy(srcs[0], outs[0].at[me], sems[2].at[0])]
        for k, peer in enumerate(_peers(x, y, c)):
            copies.append(pltpu.make_async_remote_copy(src_ref=srcs[0], dst_ref=outs[0].at[me], send_sem=sems[0].at[0, k],
                                                       recv_sem=sems[1].at[0, k], device_id=peer, device_id_type=MESH))
        return copies

    def start(self, srcs, outs, sems):
        for cp in self._copies(srcs, outs, sems):
            cp.start()

    def forward(self, srcs, outs, sems):
        pass

    def finish(self, srcs, outs, sems):
        for cp in self._copies(srcs, outs, sems):
            cp.wait()


class _Together:
    def __init__(self, *comms):
        self.comms = comms
        self.operands = [a for cm in comms for a in cm.operands]
        self.out_shape = [s for cm in comms for s in cm.out_shape]
        self.scratch = [s for cm in comms for s in cm.scratch]
        self.tag = "_".join(cm.tag for cm in comms)
        self.kinds = tuple(kind for cm in comms for kind in getattr(cm, "kinds", ()))

    def _each(self, srcs, outs, sems):
        a = o = s = 0
        for cm in self.comms:
            na, no, ns = len(cm.operands), len(cm.out_shape), len(cm.scratch)
            yield cm, srcs[a:a + na], outs[o:o + no], sems[s:s + ns]
            a, o, s = a + na, o + no, s + ns

    def start(self, srcs, outs, sems):
        for cm, *refs in self._each(srcs, outs, sems):
            cm.start(*refs)

    def forward(self, srcs, outs, sems):
        for cm, *refs in self._each(srcs, outs, sems):
            cm.forward(*refs)

    def finish(self, srcs, outs, sems):
        for cm, *refs in self._each(srcs, outs, sems):
            cm.finish(*refs)


def _rope_tables(pos_col, freq_row, casts, comm=None):
    steps = SEQ // ROW_TILE
    assert steps == 2 * DEPTH

    def body(p_ref, f_ref, *refs):
        wide, (c_ref, sa_ref, sb_ref), narrow = refs[:len(casts)], refs[len(casts):len(casts) + 3], refs[len(casts) + 3:]
        ang = p_ref[...].astype(F32) * f_ref[...]
        lane = lax.broadcasted_iota(jnp.int32, ang.shape, 1) % HEAD_DIM
        cos, sin = jnp.cos(ang), jnp.sin(ang)
        c_ref[...] = jnp.where(lane < ROPE_DIM, cos, 1.0)
        sa_ref[...] = jnp.where(lane < ROPE_DIM // 2, -sin, 0.0)
        sb_ref[...] = jnp.where((lane >= ROPE_DIM // 2) & (lane < ROPE_DIM), sin, 0.0)
        for src, dst in zip(wide, narrow):
            dst[...] = src[...].astype(BF16)

    halves = [pl.BlockSpec((1, a.shape[1] // 2, a.shape[2]), lambda i: (i // 2, i % 2, 0)) for a in casts]
    return _call(body, (pos_col, freq_row, *casts), grid=(steps,), in_specs=[_rows(1), _const((1, LANES))] + halves,
                 out_specs=[_rows(LANES)] * 3 + halves,
                 out_shape=[_sds((SEQ, LANES), F32)] * 3 + [_sds(a.shape, BF16) for a in casts], name="rope_tables", comm=comm)


def _rotate(t, c, sa, sb):
    parts = []
    for g in range(ATTN_WIDTH // LANES):
        tg = t[:, g * LANES:(g + 1) * LANES]
        parts.append(tg * c + pltpu.roll(tg, LANES - 8, axis=1) * sa + pltpu.roll(tg, 8, axis=1) * sb)
    return jnp.concatenate(parts, axis=1)


def _rotate_transposed(dt, c, sa, sb):
    parts = []
    for g in range(ATTN_WIDTH // LANES):
        dg = dt[:, g * LANES:(g + 1) * LANES]
        parts.append(dg * c + pltpu.roll(dg * sa, 8, axis=1) + pltpu.roll(dg * sb, LANES - 8, axis=1))
    return jnp.concatenate(parts, axis=1)


def _inproj_fwd(x, g_pre, w_in, tc, tsa, tsb, comm=None):
    def body(x_ref, g_ref, w_ref, c_ref, sa_ref, sb_ref, q_ref, k_ref, v_ref, u_ref, b_ref, cc_ref):
        y, _ = _rms(x_ref[...])
        h = (y * g_ref[...]).astype(BF16)

        def proj(n):
            return _dot(h, w_ref[:, n * ATTN_WIDTH:(n + 1) * ATTN_WIDTH])

        c, sa, sb = c_ref[...], sa_ref[...], sb_ref[...]
        q_ref[...] = (_rotate(proj(0), c, sa, sb) * (HEAD_DIM ** -0.5)).astype(BF16)
        k_ref[...] = _rotate(proj(1), c, sa, sb).astype(BF16)
        v_ref[...] = proj(2).astype(BF16)
        u_ref[...] = proj(3).astype(BF16)
        b_ref[...] = proj(4).astype(BF16)
        cc_ref[...] = proj(5).astype(BF16)

    a = ATTN_WIDTH
    return _call(
        body, (x, g_pre, w_in, tc, tsa, tsb), grid=(SEQ // ROW_TILE,),
        in_specs=[_rows(D_MODEL), _const((1, D_MODEL)), _const1((D_MODEL, IN_PROJ_WIDTH)), _rows(LANES), _rows(LANES), _rows(LANES)],
        out_specs=[_rows(a)] * 6, out_shape=[_sds((SEQ, a), BF16)] * 6,
        vmem_mib=40, name="inproj_fwd", comm=comm)


def _head_masks():
    lane = lax.broadcasted_iota(jnp.int32, (1, LANES), 1)
    first = lane < HEAD_DIM
    return first, first.astype(F32), 1.0 - first.astype(F32)


def _perm_chunks(dil):
    length = SEQ // dil
    out = []
    for r in range(dil):
        for c0 in range(0, length, PERM_CHUNK):
            chunk = (r * length + c0) // PERM_CHUNK
            rows = pl.ds(c0, PERM_CHUNK) if dil == 1 else pl.ds(r + dil * c0, PERM_CHUNK, stride=dil)
            out.append((chunk, rows))
    return out


def _chunk(c, offset=0):
    return pl.ds(offset + c * PERM_CHUNK, PERM_CHUNK)


def _write_band_bias(bias_ref):
    rr = lax.broadcasted_iota(jnp.int32, (Q_BLOCK, K_WINDOW), 0)
    cc = lax.broadcasted_iota(jnp.int32, (Q_BLOCK, K_WINDOW), 1)
    band = (cc >= rr) & (cc - rr <= 2 * HALF_WINDOW)
    bias_ref[0] = jnp.where(band, 0.0, NEG_INF)
    bias_ref[1] = jnp.where(band & (cc >= HALF_WINDOW), 0.0, NEG_INF)
    bias_ref[2] = jnp.where(band & (cc < Q_BLOCK + HALF_WINDOW), 0.0, NEG_INF)


def _band_bias_index(m0, length):
    return jnp.where(m0 % length == 0, 1, 0) + jnp.where((m0 + Q_BLOCK) % length == 0, 2, 0)


def _zero_key_padding(bufs):
    pad = jnp.zeros((HALF_WINDOW, LANES), BF16)
    for buf in bufs:
        buf[pl.ds(0, HALF_WINDOW), :] = pad
        buf[pl.ds(SEQ + HALF_WINDOW, HALF_WINDOW), :] = pad


def _attn_fwd(q, k, v, comm=None):
    group = 8

    def body(q_ref, k_ref, v_ref, o_ref, lse_ref, q32, k32, v32, qa, qb, kp, vp, accp, mlp,
             acc0, acc1, acc2, ml0, ml1, ml2, bias_ref):
        first, mask_a, mask_b = _head_masks()
        low = lax.broadcasted_iota(jnp.int32, (1, LANES), 1) % HEAD_DIM < HEAD_DIM // 2
        _zero_key_padding((kp, vp))
        _write_band_bias(bias_ref)
        q32[...] = q_ref[...].astype(F32)
        k32[...] = k_ref[...].astype(F32)
        v32[...] = v_ref[...].astype(F32)
        natural = ((acc0, ml0), (acc1, ml1), (acc2, ml2))

        for branch, dil in enumerate(DILATIONS):
            length = SEQ // dil
            assert length >= 2 * Q_BLOCK
            chunks = _perm_chunks(dil)
            for c, rows in chunks:
                val = q32[rows, :]
                qa[_chunk(c), :] = (val * mask_a).astype(BF16)
                qb[_chunk(c), :] = (val * mask_b).astype(BF16)
                kp[_chunk(c, HALF_WINDOW), :] = k32[rows, :].astype(BF16)
                vp[_chunk(c, HALF_WINDOW), :] = v32[rows, :].astype(BF16)
            acc_dst, ml_dst = natural[branch] if dil == 1 else (accp, mlp)

            def blocks(i, carry, length=length, acc_dst=acc_dst, ml_dst=ml_dst):
                base = pl.multiple_of(i * (group * Q_BLOCK), group * Q_BLOCK)
                starts = [base + g * Q_BLOCK for g in range(group)]
                scores = [[_dot_nt(qx[pl.ds(m0, Q_BLOCK), :], kp[pl.ds(m0, K_WINDOW), :]) for qx in (qa, qb)] for m0 in starts]
                probs = []
                for m0, pair in zip(starts, scores):
                    bias = bias_ref[_band_bias_index(m0, length)]
                    stats, ps = [], []
                    for s in pair:
                        s = s + bias
                        m = jnp.max(s, axis=1, keepdims=True)
                        p = jnp.exp(s - m)
                        stats.append(jnp.where(low, m, jnp.sum(p, axis=1, keepdims=True)))
                        ps.append(p.astype(BF16))
                    ml_dst[pl.ds(m0, Q_BLOCK), :] = jnp.where(first, stats[0], stats[1])
                    probs.append(ps)
                for m0, ps in zip(starts, probs):
                    vw = vp[pl.ds(m0, K_WINDOW), :]
                    acc_dst[pl.ds(m0, Q_BLOCK), :] = jnp.where(first, _dot(ps[0], vw), _dot(ps[1], vw))
                return carry

            lax.fori_loop(0, SEQ // (group * Q_BLOCK), blocks, 0)

            if dil > 1:
                for c, rows in chunks:
                    natural[branch][0][rows, :] = accp[_chunk(c), :]
                    natural[branch][1][rows, :] = mlp[_chunk(c), :]

        for c in range(SEQ // PERM_CHUNK):
            def up(t):
                return jnp.where(low, t, pltpu.roll(t, HEAD_DIM // 2, axis=1))

            packed = [ml[_chunk(c), :] for _, ml in natural]
            top = jnp.maximum(jnp.maximum(packed[0], packed[1]), packed[2])
            m_all = up(top)
            es = [up(jnp.exp(ml - top)) for ml in packed]
            z = packed[0] * es[0] + packed[1] * es[1] + packed[2] * es[2]
            z = jnp.where(low, pltpu.roll(z, LANES - HEAD_DIM // 2, axis=1), z)
            num = natural[0][0][_chunk(c), :] * es[0] + natural[1][0][_chunk(c), :] * es[1] + natural[2][0][_chunk(c), :] * es[2]
            o_ref[_chunk(c), :] = num / z
            lse_ref[_chunk(c), :] = m_all + jnp.log(z)

    col = pl.BlockSpec((SEQ, LANES), lambda h: (0, h))
    padded = SEQ + 2 * HALF_WINDOW
    return _call(
        body, (q, k, v), grid=(ATTN_WIDTH // LANES,), in_specs=[col] * 3, out_specs=[col] * 2,
        out_shape=[_sds((SEQ, ATTN_WIDTH), F32)] * 2,
        scratch_shapes=[pltpu.VMEM((SEQ, LANES), F32)] * 3 + [pltpu.VMEM((SEQ, LANES), BF16)] * 2
        + [pltpu.VMEM((padded, LANES), BF16)] * 2 + [pltpu.VMEM((SEQ, LANES), F32)] * 8
        + [pltpu.VMEM((3, Q_BLOCK, K_WINDOW), F32)],
        vmem_mib=52, name="attn_fwd", comm=comm)


def _shifted(t, before, after, i):
    tile = t.shape[0]
    row = lax.broadcasted_iota(jnp.int32, (tile, 1), 0)
    before = jnp.where(i > 0, before, 0.0)
    after = jnp.where(i < SEQ // tile - 1, after, 0.0)
    return (jnp.where(row == 0, before, pltpu.roll(t, 1, axis=0)),
            jnp.where(row == tile - 1, after, pltpu.roll(t, tile - 1, axis=0)))


def _last_row(ref):
    return ref[HALO_ROWS - 1:HALO_ROWS, :].astype(F32)


def _first_row(ref):
    return ref[0:1, :].astype(F32)


def _conv_parts(u, c, u_prev, c_prev, u_next, c_next, cw, i):
    t = c * u
    t_prev, t_next = _shifted(t, _last_row(c_prev) * _last_row(u_prev), _first_row(c_next) * _first_row(u_next), i)
    s = cw[0:1, :] * t_prev + cw[1:2, :] * t + cw[2:3, :] * t_next
    return t, t_prev, t_next, s


def _mix_fwd(attn, u, b, c, conv_w, g_attn, g_conv, w_out, x, g_post, comm=None):
    def body(a_ref, u_ref, b_ref, c_ref, up_ref, cp_ref, un_ref, cn_ref, cw_ref, ga_ref, gc_ref, w_ref, x_ref, gp_ref,
             x1_ref, mg_ref, mix_ref):
        i = pl.program_id(0)
        _, _, _, s = _conv_parts(u_ref[...].astype(F32), c_ref[...].astype(F32), up_ref, cp_ref, un_ref, cn_ref, cw_ref[...], i)
        ya, _ = _rms(a_ref[...])
        yc, _ = _rms(b_ref[...].astype(F32) * s)
        merged = jnp.concatenate([ya * ga_ref[...], yc * gc_ref[...]], axis=1).astype(BF16)
        mix = _dot(merged, w_ref[...])
        ym, _ = _rms(mix)
        mg_ref[...] = merged.T
        mix_ref[...] = mix.astype(BF16)
        x1_ref[...] = x_ref[...] + ym * gp_ref[...]

    a = ATTN_WIDTH
    return _call(
        body, (attn, u, b, c, u, c, u, c, conv_w, g_attn, g_conv, w_out, x, g_post), grid=(SEQ // ROW_TILE,),
        in_specs=[_rows(a)] * 4 + [_halo_prev(a)] * 2 + [_halo_next(a)] * 2
        + [_const((3, a)), _const((1, a)), _const((1, a)), _const1((D_MODEL, D_MODEL)), _rows(D_MODEL), _const((1, D_MODEL))],
        out_specs=[_rows(D_MODEL), _cols(D_MODEL, ROW_TILE), _rows(D_MODEL)],
        out_shape=[_sds((SEQ, D_MODEL), F32), _sds((D_MODEL, SEQ), BF16), _sds((SEQ, D_MODEL), BF16)],
        vmem_mib=40, name="mix_fwd", comm=comm)


def _gu_spec():
    return pl.BlockSpec((N_DEV, FFN_TILE, FFN_BLOCK), lambda i: (0, i, 0))


def _ffn_fwd(x1, g_pre, w_gu, w_dn, g_post, comm=None, target=None):
    n_tiles = SEQ // ROW_TILE

    def body(*refs):
        if target is None:
            x_ref, g_ref, wgu_ref, wdn_ref, gp_ref, x2_ref, gu_ref, f_ref = refs
        else:
            x_ref, g_ref, wgu_ref, wdn_ref, gp_ref, t_ref, x2_ref, gu_ref, f_ref, loss_ref, acc = refs
        x1 = x_ref[...]
        y, _ = _rms(x1)
        h = (y * g_ref[...]).astype(BF16)
        f = jnp.zeros((ROW_TILE, D_MODEL), F32)

        def gate_up(j):
            return _dot_nt(h, wgu_ref[j]), _dot_nt(h, wgu_ref[j + N_DEV // 2])

        ahead = gate_up(0)
        for j in range(N_DEV // 2):
            gate, up = ahead
            if j + 1 < N_DEV // 2:
                ahead = gate_up(j + 1)
            gu_ref[j] = gate.astype(BF16)
            gu_ref[j + N_DEV // 2] = up.astype(BF16)
            act = (gate * jax.nn.sigmoid(gate) * up).astype(BF16)
            f = f + _dot(act, wdn_ref[pl.ds(j * FFN_BLOCK, FFN_BLOCK), :])
        yf, _ = _rms(f)
        f_ref[...] = f
        x2 = x1 + yf * gp_ref[...]
        if target is None:
            x2_ref[...] = x2
        else:
            i = pl.program_id(0)
            err = x2 - t_ref[...]
            x2_ref[...] = err * (1.0 / D_MODEL)

            @pl.when(i == 0)
            def _():
                acc[...] = jnp.zeros_like(acc)

            acc[...] += jnp.sum(err * err, axis=0, keepdims=True)

            @pl.when(i == n_tiles - 1)
            def _():
                loss_ref[...] = jnp.sum(acc[...], axis=1, keepdims=True) * (0.5 / D_MODEL)

    with_loss = target is not None
    return _call(
        body, (x1, g_pre, w_gu, w_dn, g_post) + ((target,) if with_loss else ()), grid=(n_tiles,),
        in_specs=[_rows(D_MODEL), _const((1, D_MODEL)), _const1((N_DEV, FFN_BLOCK, D_MODEL)), _const1((FFN_HIDDEN, D_MODEL)),
                  _const((1, D_MODEL))] + ([_rows(D_MODEL)] if with_loss else []),
        out_specs=[_rows(D_MODEL), pl.BlockSpec((N_DEV, ROW_TILE, FFN_BLOCK), lambda i: (0, i, 0)), _rows(D_MODEL)]
        + ([_const((1, 1))] if with_loss else []),
        out_shape=[_sds((SEQ, D_MODEL), F32), _sds((N_DEV, SEQ, FFN_BLOCK), BF16), _sds((SEQ, D_MODEL), F32)]
        + ([_sds((1, 1), F32)] if with_loss else []),
        scratch_shapes=[pltpu.VMEM((1, D_MODEL), F32)] if with_loss else [],
        vmem_mib=58, name="ffn_fwd_loss" if with_loss else "ffn_fwd", comm=comm)


def _accumulate(pairs, i):
    @pl.when(i == 0)
    def _():
        for ref, value in pairs:
            ref[...] = value

    @pl.when(i > 0)
    def _():
        for ref, value in pairs:
            ref[...] += value


def _colsum(v):
    return jnp.sum(v, axis=0, keepdims=True)


def _ffn_bwd(dx2, f, x1, gu, w_gu, w_dn, g_post, g_pre, comm=None):
    half = N_DEV // 2

    def body(dx2_ref, f_ref, x1_ref, gu_ref, wgu_ref, wdn_ref, gpost_ref, gpre_ref,
             dx1_ref, df_ref, act_ref, dgu_ref, h_ref, dgpost_ref, dgpre_ref):
        i = pl.program_id(0)
        dx2 = dx2_ref[...]
        yf, rf = _rms(f_ref[...])
        dg_post = _colsum(dx2 * yf)
        df = _rms_bwd(dx2 * gpost_ref[...], yf, rf).astype(BF16)
        df_ref[...] = df
        dh = jnp.zeros((FFN_TILE, D_MODEL), F32)

        def d_act(j):
            return _dot_nt(df, wdn_ref[pl.ds(j * FFN_BLOCK, FFN_BLOCK), :])

        ahead = d_act(0)
        for j in range(half):
            dact = ahead
            if j + 1 < half:
                ahead = d_act(j + 1)
            gate = gu_ref[j].astype(F32)
            up = gu_ref[j + half].astype(F32)
            sig = jax.nn.sigmoid(gate)
            silu = gate * sig
            act_ref[j] = (silu * up).astype(BF16)
            dgate = (dact * up * (sig * (1.0 + gate * (1.0 - sig)))).astype(BF16)
            dup = (dact * silu).astype(BF16)
            dgu_ref[j] = dgate
            dgu_ref[j + half] = dup
            dh = dh + _dot(dgate, wgu_ref[j]) + _dot(dup, wgu_ref[j + half])
        y1, r1 = _rms(x1_ref[...])
        h_ref[...] = (y1 * gpre_ref[...]).astype(BF16)
        dx1_ref[...] = dx2 + _rms_bwd(dh * gpre_ref[...], y1, r1)
        _accumulate([(dgpost_ref, dg_post), (dgpre_ref, _colsum(dh * y1))], i)

    act_spec = pl.BlockSpec((half, FFN_TILE, FFN_BLOCK), lambda i: (0, i, 0))
    return _call(
        body, (dx2, f, x1, gu, w_gu, w_dn, g_post, g_pre), grid=(SEQ // FFN_TILE,),
        in_specs=[_frows(D_MODEL)] * 3 + [_gu_spec(), _const1((N_DEV, FFN_BLOCK, D_MODEL)), _const1((FFN_HIDDEN, D_MODEL)),
                                          _const((1, D_MODEL)), _const((1, D_MODEL))],
        out_specs=[_frows(D_MODEL), _frows(D_MODEL), act_spec, _gu_spec(), _frows(D_MODEL), _const((1, D_MODEL)), _const((1, D_MODEL))],
        out_shape=[_sds((SEQ, D_MODEL), F32), _sds((SEQ, D_MODEL), BF16), _sds((half, SEQ, FFN_BLOCK), BF16),
                   _sds((N_DEV, SEQ, FFN_BLOCK), BF16), _sds((SEQ, D_MODEL), BF16), _sds((1, D_MODEL), F32), _sds((1, D_MODEL), F32)],
        vmem_mib=52, name="ffn_bwd", comm=comm)


def _wgrad(a_t, g, name):
    width = g.shape[1]

    def body(a_ref, g_ref, o_ref):
        o_ref[...] = _dot(a_ref[...], g_ref[...]).astype(BF16)

    return _call(body, (a_t, g), grid=(width // WGRAD_COLS,),
                 in_specs=[_const1((D_MODEL, SEQ)), pl.BlockSpec((SEQ, WGRAD_COLS), lambda j: (0, j))],
                 out_specs=[pl.BlockSpec((D_MODEL, WGRAD_COLS), lambda j: (0, j))], out_shape=[_sds((D_MODEL, width), BF16)],
                 vmem_mib=48, name=name)[0][0]


def _wgrad_paired(a, g, kind, to_owners=False, comm=None):
    shard = SHARD_SHAPES[kind]
    per_chip = 2 if kind == "gu" else 1
    n_compute = per_chip * N_CHIPS
    n_steps = n_compute + 1

    def chip_of(turn):
        if not to_owners:
            return turn
        return (2 * lax.axis_index("x") + lax.axis_index("y") + 1 + turn) % N_CHIPS

    def block(t):
        return jnp.minimum(t, n_compute - 1)

    if kind == "in":
        specs = [_const1((D_MODEL, SEQ)), pl.BlockSpec((SEQ, 2 * W_IN_BLOCK), lambda t: (0, chip_of(block(t))))]
    else:
        specs = [pl.BlockSpec((1, SEQ, FFN_BLOCK), lambda t: (block(t), 0, 0)), _const1((SEQ, D_MODEL))]

    def body(a_ref, g_ref, o_ref, *rest):
        if to_owners:
            landing, sendbuf, recvbuf, keepbuf, send_sem, recv_sem, donebuf, owner_send, owner_recv, own_sem = rest
        else:
            sendbuf, recvbuf, keepbuf, send_sem, recv_sem = rest
        t = pl.program_id(0)
        x, y, c = _place()

        def exchange(q):
            return pltpu.make_async_remote_copy(src_ref=sendbuf.at[q], dst_ref=recvbuf.at[q], send_sem=send_sem.at[q],
                                                recv_sem=recv_sem.at[q], device_id=(x, y, 1 - c), device_id_type=MESH)

        my_chip = 2 * x + y

        def to_owner(q):
            return pltpu.make_async_remote_copy(src_ref=donebuf.at[q], dst_ref=landing.at[my_chip], send_sem=owner_send.at[q],
                                                recv_sem=owner_recv.at[my_chip], device_id=(q // 2, q % 2, c), device_id_type=MESH)

        def keep_own():
            return pltpu.make_async_copy(donebuf.at[my_chip], landing.at[my_chip], own_sem)

        @pl.when((t >= per_chip) & (t % per_chip == 0))
        def _():
            q = chip_of(t // per_chip - 1)
            exchange(q).wait_recv()
            done = (keepbuf[...] + recvbuf[q].astype(F32)).astype(BF16)
            o_ref[0] = done
            if to_owners:
                donebuf[q] = done

                @pl.when(q == my_chip)
                def _():
                    keep_own().start()

                @pl.when(q != my_chip)
                def _():
                    to_owner(q).start()

        @pl.when(t < n_compute)
        def _():
            q = chip_of(t // per_chip)
            if kind == "gu":
                r = _dot_tn(a_ref[0], g_ref[...])

                @pl.when(t % 2 == c)
                def _():
                    keepbuf[...] = r

                @pl.when(t % 2 != c)
                def _():
                    sendbuf[q] = r.astype(BF16)
                    exchange(q).start()
            else:
                if kind == "dn":
                    r = _dot_tn(a_ref[0], g_ref[...])
                    lower, upper = r[:W_DOWN_BLOCK], r[W_DOWN_BLOCK:]
                else:
                    r = _dot(a_ref[...], g_ref[...])
                    lower, upper = r[:, :W_IN_BLOCK], r[:, W_IN_BLOCK:]
                keepbuf[...] = jnp.where(c == 0, lower, upper)
                sendbuf[q] = jnp.where(c == 0, upper, lower).astype(BF16)
                exchange(q).start()

        @pl.when(t == n_steps - 1)
        def _():
            for q in range(N_CHIPS):
                exchange(q).wait_send()
            if to_owners:
                keep_own().wait()
                for s in range(N_CHIPS):
                    @pl.when(s != my_chip)
                    def _(s=s):
                        pltpu.make_async_remote_copy(src_ref=donebuf.at[s], dst_ref=landing.at[s], send_sem=owner_send.at[s],
                                                     recv_sem=owner_recv.at[s], device_id=(x, y, c), device_id_type=MESH).wait()

    slots = (N_CHIPS,) + shard
    owners = int(to_owners)
    res, landed = _call(
        body, (a, g), grid=(n_steps,), in_specs=specs,
        out_specs=[pl.BlockSpec((1,) + shard, lambda t: (chip_of(jnp.clip(t // per_chip - 1, 0, N_CHIPS - 1)), 0, 0))] + [ANY] * owners,
        out_shape=[_sds(slots, BF16)] * (1 + owners),
        scratch_shapes=[pltpu.VMEM(slots, BF16), pltpu.VMEM(slots, BF16), pltpu.VMEM(shard, F32),
                        pltpu.SemaphoreType.DMA((N_CHIPS,)), pltpu.SemaphoreType.DMA((N_CHIPS,))]
        + [pltpu.VMEM(slots, BF16), pltpu.SemaphoreType.DMA((N_CHIPS,)), pltpu.SemaphoreType.DMA((N_CHIPS,)), pltpu.SemaphoreType.DMA] * owners,
        vmem_mib=52, name="wgrad_" + kind + "_paired" + "_to_owners" * owners, comm=comm)
    return res[owners] if comm is None else (res[owners], landed)


def _mix_bwd(dx1, mix, attn, u, b, c, conv_w, g_attn, g_conv, g_post, w_out):
    def body(dx1_ref, mix_ref, a_ref, u_ref, b_ref, c_ref, up_ref, cp_ref, un_ref, cn_ref, cw_ref, ga_ref, gc_ref, gp_ref, w_ref,
             dmix_ref, da_ref, ds_ref, db_ref, dgp_ref, dga_ref, dgc_ref):
        i = pl.program_id(0)
        dx1 = dx1_ref[...]
        ym, rm = _rms(mix_ref[...].astype(F32))
        dg_post = _colsum(dx1 * ym)
        dmix = _rms_bwd(dx1 * gp_ref[...], ym, rm).astype(BF16)
        dmix_ref[...] = dmix
        dmerged = _dot_nt(dmix, w_ref[...])
        dna, dnc = dmerged[:, :ATTN_WIDTH], dmerged[:, ATTN_WIDTH:]
        ya, ra = _rms(a_ref[...])
        da_ref[...] = _rms_bwd(dna * ga_ref[...], ya, ra)
        _, _, _, s = _conv_parts(u_ref[...].astype(F32), c_ref[...].astype(F32), up_ref, cp_ref, un_ref, cn_ref, cw_ref[...], i)
        gate_b = b_ref[...].astype(F32)
        yc, rc = _rms(gate_b * s)
        dy = _rms_bwd(dnc * gc_ref[...], yc, rc)
        db_ref[...] = (dy * s).astype(BF16)
        ds_ref[...] = (dy * gate_b).astype(BF16)
        _accumulate([(dgp_ref, dg_post), (dga_ref, _colsum(dna * ya)), (dgc_ref, _colsum(dnc * yc))], i)

    a = ATTN_WIDTH
    return _call(
        body, (dx1, mix, attn, u, b, c, u, c, u, c, conv_w, g_attn, g_conv, g_post, w_out), grid=(SEQ // ROW_TILE,),
        in_specs=[_rows(D_MODEL)] * 2 + [_rows(a)] * 4 + [_halo_prev(a)] * 2 + [_halo_next(a)] * 2
        + [_const((3, a)), _const((1, a)), _const((1, a)), _const((1, D_MODEL)), _const1((D_MODEL, D_MODEL))],
        out_specs=[_rows(D_MODEL)] + [_rows(a)] * 3 + [_const((1, D_MODEL)), _const((1, a)), _const((1, a))],
        out_shape=[_sds((SEQ, D_MODEL), BF16), _sds((SEQ, a), F32), _sds((SEQ, a), BF16), _sds((SEQ, a), BF16),
                   _sds((1, D_MODEL), F32), _sds((1, a), F32), _sds((1, a), F32)],
        vmem_mib=40, name="mix_bwd")[0]


def _attn_bwd(q, k, v, do, o, lse, comm=None):
    group = 8

    n_pairs = ATTN_WIDTH // LANES

    def body(q_hbm, k_hbm, v_hbm, do_hbm, o_hbm, lse_hbm, dq_ref, dk_ref, dv_ref,
             q_ref, k_ref, v_ref, do_ref, o_ref, lse_ref, in_sem,
             q32, k32, v32, qa, qb, doa, dob, kp, vp, lsep, dlp, dnat, dqp, dkp, dvp, bias_ref):
        pair = pl.program_id(0)
        sources = (q_hbm, k_hbm, v_hbm, do_hbm, o_hbm, lse_hbm)
        buffers = (q_ref, k_ref, v_ref, do_ref, o_ref, lse_ref)

        def fetch(i, p):
            return pltpu.make_async_copy(sources[i].at[:, pl.ds(pl.multiple_of(p * LANES, LANES), LANES)], buffers[i], in_sem.at[i])

        def prefetch(*which):
            @pl.when(pair + 1 < n_pairs)
            def _():
                for i in which:
                    fetch(i, pair + 1).start()

        @pl.when(pair == 0)
        def _():
            for i in range(len(sources)):
                fetch(i, 0).start()

        for i in range(len(sources)):
            fetch(i, pair).wait()

        first, mask_a, mask_b = _head_masks()
        _zero_key_padding((kp, vp))
        _write_band_bias(bias_ref)
        q32[...] = q_ref[...].astype(F32)
        k32[...] = k_ref[...].astype(F32)
        v32[...] = v_ref[...].astype(F32)
        prefetch(0, 1, 2)
        for c in range(SEQ // PERM_CHUNK):
            prod = do_ref[_chunk(c), :] * o_ref[_chunk(c), :]
            d_a = jnp.sum(prod * mask_a, axis=1, keepdims=True)
            d_b = jnp.sum(prod * mask_b, axis=1, keepdims=True)
            dnat[_chunk(c), :] = jnp.where(first, d_a, d_b)
        prefetch(4)

        for step, dil in enumerate(DILATIONS[1:] + DILATIONS[:1]):
            length = SEQ // dil
            assert length >= 2 * Q_BLOCK
            chunks = _perm_chunks(dil)
            for c, rows in chunks:
                val = q32[rows, :]
                qa[_chunk(c), :] = (val * mask_a).astype(BF16)
                qb[_chunk(c), :] = (val * mask_b).astype(BF16)
                val = do_ref[rows, :]
                doa[_chunk(c), :] = (val * mask_a).astype(BF16)
                dob[_chunk(c), :] = (val * mask_b).astype(BF16)
                kp[_chunk(c, HALF_WINDOW), :] = k32[rows, :].astype(BF16)
                vp[_chunk(c, HALF_WINDOW), :] = v32[rows, :].astype(BF16)
                lsep[_chunk(c), :] = lse_ref[rows, :]
                dlp[_chunk(c), :] = dnat[rows, :]
            if step == len(DILATIONS) - 1:
                prefetch(3, 5)
            zero = jnp.zeros((PERM_CHUNK, LANES), F32)
            for c in range(SEQ // PERM_CHUNK):
                dkp[_chunk(c), :] = zero
                dvp[_chunk(c), :] = zero
            dkp[pl.ds(SEQ, 2 * HALF_WINDOW), :] = zero[:2 * HALF_WINDOW]
            dvp[pl.ds(SEQ, 2 * HALF_WINDOW), :] = zero[:2 * HALF_WINDOW]

            heads = ((qa, doa, 0), (qb, dob, HEAD_DIM))

            def blocks(i, carry, length=length):
                base = pl.multiple_of(i * (group * Q_BLOCK), group * Q_BLOCK)
                starts = [base + g * Q_BLOCK for g in range(group)]
                raw = [[(_dot_nt(qx[pl.ds(m0, Q_BLOCK), :], kp[pl.ds(m0, K_WINDOW), :]),
                         _dot_nt(dox[pl.ds(m0, Q_BLOCK), :], vp[pl.ds(m0, K_WINDOW), :])) for qx, dox, _ in heads]
                       for m0 in starts]
                grads = []
                for m0, pair in zip(starts, raw):
                    bias = bias_ref[_band_bias_index(m0, length)]
                    lse_b, d_b = lsep[pl.ds(m0, Q_BLOCK), :], dlp[pl.ds(m0, Q_BLOCK), :]
                    out = []
                    for (s, dp), (_, _, col) in zip(pair, heads):
                        p = jnp.exp(s + bias - lse_b[:, col:col + 1])
                        out.append(((p * (dp - d_b[:, col:col + 1])).astype(BF16), p.astype(BF16)))
                    grads.append(out)
                for m0, out in zip(starts, grads):
                    qrows, krows = pl.ds(m0, Q_BLOCK), pl.ds(m0, K_WINDOW)
                    kw = kp[krows, :]
                    dk = jnp.zeros((K_WINDOW, LANES), F32)
                    dv = jnp.zeros((K_WINDOW, LANES), F32)
                    for (ds, p), (qx, dox, _) in zip(out, heads):
                        dk = dk + _dot_tn(ds, qx[qrows, :])
                        dv = dv + _dot_tn(p, dox[qrows, :])
                    dqp[qrows, :] = jnp.where(first, _dot(out[0][0], kw), _dot(out[1][0], kw)) * (HEAD_DIM ** -0.5)
                    dkp[krows, :] += dk
                    dvp[krows, :] += dv
                return carry

            lax.fori_loop(0, SEQ // (group * Q_BLOCK), blocks, 0)

            for c, rows in chunks:
                g_q, g_k, g_v = dqp[_chunk(c), :], dkp[_chunk(c, HALF_WINDOW), :], dvp[_chunk(c, HALF_WINDOW), :]
                if step == 0:
                    dq_ref[rows, :] = g_q
                    dk_ref[rows, :] = g_k
                    dv_ref[rows, :] = g_v
                else:
                    dq_ref[rows, :] = dq_ref[rows, :] + g_q
                    dk_ref[rows, :] = dk_ref[rows, :] + g_k
                    dv_ref[rows, :] = dv_ref[rows, :] + g_v

    col = pl.BlockSpec((SEQ, LANES), lambda h: (0, h))
    padded = SEQ + 2 * HALF_WINDOW
    return _call(
        body, (q, k, v, do, o, lse), grid=(n_pairs,), in_specs=[ANY] * 6, out_specs=[col] * 3,
        out_shape=[_sds((SEQ, ATTN_WIDTH), F32)] * 3,
        scratch_shapes=[pltpu.VMEM((SEQ, LANES), BF16)] * 3 + [pltpu.VMEM((SEQ, LANES), F32)] * 3 + [pltpu.SemaphoreType.DMA((6,))]
        + [pltpu.VMEM((SEQ, LANES), F32)] * 3 + [pltpu.VMEM((SEQ, LANES), BF16)] * 4
        + [pltpu.VMEM((padded, LANES), BF16)] * 2 + [pltpu.VMEM((SEQ, LANES), F32)] * 4 + [pltpu.VMEM((padded, LANES), F32)] * 2
        + [pltpu.VMEM((3, Q_BLOCK, K_WINDOW), F32)],
        vmem_mib=56, name="attn_bwd", comm=comm)


def _inproj_bwd(dq, dk, dv, ds, db, u, c, conv_w, tc, tsa, tsb, w_in, x, g_pre, dx1):
    def body(dq_ref, dk_ref, dv_ref, ds_ref, db_ref, u_ref, c_ref, dsp_ref, up_ref, cp_ref, dsn_ref, un_ref, cn_ref, cw_ref,
             tc_ref, tsa_ref, tsb_ref, w_ref, x_ref, g_ref, dx1_ref, dx_ref, dproj_ref, h_ref, dg_ref, dcw_ref):
        i = pl.program_id(0)
        cw = cw_ref[...]
        u, c = u_ref[...].astype(F32), c_ref[...].astype(F32)
        t, t_prev, t_next, _ = _conv_parts(u, c, up_ref, cp_ref, un_ref, cn_ref, cw, i)
        ds = ds_ref[...].astype(F32)
        ds_prev, ds_next = _shifted(ds, _last_row(dsp_ref), _first_row(dsn_ref), i)
        dt = cw[0:1, :] * ds_next + cw[1:2, :] * ds + cw[2:3, :] * ds_prev
        d_taps = jnp.concatenate([_colsum(ds * t_prev), _colsum(ds * t), _colsum(ds * t_next)], axis=0)
        tc_, tsa_, tsb_ = tc_ref[...], tsa_ref[...], tsb_ref[...]
        groups = ((2, lambda: dv_ref[...].astype(BF16)), (4, lambda: db_ref[...]),
                  (0, lambda: _rotate_transposed(dq_ref[...], tc_, tsa_, tsb_).astype(BF16)),
                  (1, lambda: _rotate_transposed(dk_ref[...], tc_, tsa_, tsb_).astype(BF16)),
                  (3, lambda: (dt * c).astype(BF16)), (5, lambda: (dt * u).astype(BF16)))
        dh = jnp.zeros((x_ref.shape[0], D_MODEL), F32)
        for n, make in groups:
            cols = pl.ds(n * a, a)
            part = make()
            dproj_ref[:, cols] = part
            dh = dh + _dot_nt(part, w_ref[:, cols])
        y, r = _rms(x_ref[...])
        h_ref[...] = (y * g_ref[...]).astype(BF16).T
        dx_ref[...] = dx1_ref[...] + _rms_bwd(dh * g_ref[...], y, r)
        _accumulate([(dg_ref, _colsum(dh * y)), (dcw_ref, d_taps)], i)

    a = ATTN_WIDTH
    tile = FFN_TILE

    def rows(width):
        return _rows(width, tile)

    return _call(
        body, (dq, dk, dv, ds, db, u, c, ds, u, c, ds, u, c, conv_w, tc, tsa, tsb, w_in, x, g_pre, dx1), grid=(SEQ // tile,),
        in_specs=[rows(a)] * 7 + [_halo_prev(a, tile)] * 3 + [_halo_next(a, tile)] * 3
        + [_const((3, a)), rows(LANES), rows(LANES), rows(LANES), _const1((D_MODEL, IN_PROJ_WIDTH)), rows(D_MODEL),
           _const((1, D_MODEL)), rows(D_MODEL)],
        out_specs=[rows(D_MODEL), rows(IN_PROJ_WIDTH), _cols(D_MODEL, tile), _const((1, D_MODEL)), _const((3, a))],
        out_shape=[_sds((SEQ, D_MODEL), F32), _sds((SEQ, IN_PROJ_WIDTH), BF16), _sds((D_MODEL, SEQ), BF16),
                   _sds((1, D_MODEL), F32), _sds((3, a), F32)],
        vmem_mib=48, name="inproj_bwd")


def _adamw_math(w, g, m, v):
    m = ADAM_B1 * m + (1.0 - ADAM_B1) * g
    v = ADAM_B2 * v + (1.0 - ADAM_B2) * (g * g)
    m_hat = m / (1.0 - ADAM_B1 ** ADAM_STEP)
    v_hat = v / (1.0 - ADAM_B2 ** ADAM_STEP)
    delta = -ADAM_LR * (m_hat / (jnp.sqrt(v_hat) + ADAM_EPS) + ADAM_WD * w)
    return delta, m, v


def _sum_parts(p_ref):
    g = p_ref[0].astype(F32)
    for k in range(1, p_ref.shape[0]):
        g = g + p_ref[k].astype(F32)
    return g


def _adamw_layers(parts, w, m, v, row_tile, name, comm=None):
    _, rows, cols = w.shape
    n_tiles = rows // row_tile

    def body(*refs):
        p_refs = refs[:DEPTH]
        w_ref, m_ref, v_ref, g_ref, d_ref, nm_ref, nv_ref = refs[DEPTH:]
        layer = pl.program_id(0)
        for l, p_ref in enumerate(p_refs):
            @pl.when(layer == l)
            def _(p_ref=p_ref):
                g = _sum_parts(p_ref)
                g_ref[0] = g
                d_ref[0], nm_ref[0], nv_ref[0] = _adamw_math(w_ref[0], g, m_ref[0], v_ref[0])

    def part_spec(l):
        return pl.BlockSpec((parts[l].shape[0], row_tile, cols),
                            lambda layer, i: (0, jnp.where(layer == l, i, jnp.where(layer < l, 0, n_tiles - 1)), 0))

    tile = pl.BlockSpec((1, row_tile, cols), lambda layer, i: (layer, i, 0))
    return _call(body, (*parts, w, m, v), grid=(DEPTH, n_tiles), in_specs=[part_spec(l) for l in range(DEPTH)] + [tile] * 3,
                 out_specs=[tile] * 4, out_shape=[_sds(w.shape, F32)] * 4, name=name, comm=comm)


def _adamw_small(parts, w, m, v):
    n = len(GAIN_NAMES)
    a = ATTN_WIDTH
    where = ((0, 0), (1, 0), (2, 0), (3, 0), (4, 0), (4, a))

    def body(*refs):
        p_ref = refs[0]
        w_refs, m_refs, v_refs = refs[1:1 + n], refs[1 + n:1 + 2 * n], refs[1 + 2 * n:1 + 3 * n]
        outs = refs[1 + 3 * n:1 + 7 * n]
        taps_ref, loss_ref, sum_ref = refs[1 + 7 * n:]
        sum_ref[...] = _sum_parts(p_ref)
        for k, (row, col) in enumerate(where):
            width = w_refs[k].shape[1]
            g = jnp.concatenate([sum_ref[pl.ds(l * SMALL_ROWS + row, 1), pl.ds(col, width)] for l in range(DEPTH)], axis=0)
            g_ref, d_ref, nm_ref, nv_ref = outs[4 * k:4 * k + 4]
            g_ref[...] = g
            d_ref[...], nm_ref[...], nv_ref[...] = _adamw_math(w_refs[k][...], g, m_refs[k][...], v_refs[k][...])
        for l in range(DEPTH):
            for tap, (row, col) in enumerate(((5, 0), (5, a), (6, 0))):
                taps_ref[l, tap:tap + 1, :] = sum_ref[l * SMALL_ROWS + row:l * SMALL_ROWS + row + 1, col:col + a]
        loss_ref[...] = sum_ref[SMALL_ROWS - 1:SMALL_ROWS, 0:1]

    out_shape = [_sds(x.shape, F32) for x in w for _ in range(4)] + [_sds((DEPTH, 3, a), F32), _sds((1, 1), F32)]
    res = pl.pallas_call(body, out_shape=out_shape, scratch_shapes=[pltpu.VMEM(parts.shape[1:], F32)],
                         name="adamw_small")(parts, *w, *m, *v)
    return [res[4 * k:4 * k + 4] for k in range(n)], res[4 * n], res[4 * n + 1]


def _adamw_plain(g, w, m, v, name):
    def body(g_ref, w_ref, m_ref, v_ref, d_ref, nm_ref, nv_ref):
        d_ref[...], nm_ref[...], nv_ref[...] = _adamw_math(w_ref[...], g_ref[...], m_ref[...], v_ref[...])

    return pl.pallas_call(body, out_shape=[_sds(w.shape, F32)] * 3, name=name)(g, w, m, v)


SMALL_ROWS = 8
GAIN_NAMES = ("pre_mix_norm", "post_mix_norm", "pre_ffn_norm", "post_ffn_norm", "attn_out_norm", "conv_out_norm")


def _pack_small(pre_mix, post_mix, pre_ffn, post_ffn, attn_out, conv_out, taps):
    zeros = jnp.zeros((1, ATTN_WIDTH), F32)
    return jnp.concatenate([
        pre_mix, post_mix, pre_ffn, post_ffn, jnp.concatenate([attn_out, conv_out], axis=1),
        jnp.concatenate([taps[0:1], taps[1:2]], axis=1), jnp.concatenate([taps[2:3], zeros], axis=1),
        jnp.zeros((1, D_MODEL), F32)], axis=0)


def _rope(positions, casts, comm=None):
    inv_freq = ROPE_THETA ** (-jnp.arange(0, ROPE_DIM, 2, dtype=F32) / ROPE_DIM)
    per_head = jnp.concatenate([inv_freq, inv_freq, jnp.zeros((HEAD_DIM - ROPE_DIM,), F32)])
    freq_row = jnp.tile(per_head, LANES // HEAD_DIM).reshape(1, LANES)
    return _rope_tables(positions.reshape(SEQ, 1), freq_row, casts, comm)


def _layer_forward(h, gains, taps, tables, w, inproj_comm=None, attn_comm=None, mix_comm=None, ffn_comm=None, target=None):
    def arrived(comm, landed):
        return w if comm is None else {**w, **dict(zip(comm.kinds, landed))}

    (q, k, v, u, b, c), landed = _inproj_fwd(h, gains["pre_mix_norm"], w["in"], *tables, comm=inproj_comm)
    w = arrived(inproj_comm, landed)
    (attn, lse), landed = _attn_fwd(q, k, v, comm=attn_comm)
    w = arrived(attn_comm, landed)
    (x1, merged, mix), landed = _mix_fwd(attn, u, b, c, taps, gains["attn_out_norm"], gains["conv_out_norm"], w["out"], h,
                                         gains["post_mix_norm"], comm=mix_comm)
    w = arrived(mix_comm, landed)
    (x2, gu, f, *loss), landed_next = _ffn_fwd(x1, gains["pre_ffn_norm"], w["gu"], w["dn"], gains["post_ffn_norm"], comm=ffn_comm,
                                               target=target)
    out = x2 if target is None else (x2, loss[0])
    return out, (h, q, k, v, u, b, c, attn, lse, merged, mix, x1, gu, f), w, landed_next


class _GradientExchange:
    def ffn_grads(self, act, df, h2, dgu):
        return dict(dn=_wgrad_paired(act, df, "dn"), gu=_wgrad_paired(dgu, h2, "gu"))

    def w_out_grad(self, merged_t, dmix):
        return _wgrad(merged_t, dmix, "wgrad_out")

    def attention_passenger(self, ffn, g_out):
        return _Together(_ScatterChips(("gu", "dn"), (ffn["gu"], ffn["dn"])), _Scatter(("out",), (g_out,)))

    def w_in_grad(self, h_t, dproj, small_rows):
        if small_rows is None:
            return _wgrad(h_t, dproj, "wgrad_in"), None
        parts, landed = _wgrad_paired(h_t, dproj, "in", to_owners=True, comm=_GatherSmall(small_rows, "small"))
        return parts, landed[0]


def _layer_backward(dx, saved, gains, taps, tables, w, ffn_comm, exchange, last):
    x0, q, k, v, u, b, c, attn, lse, merged, mix, x1, gu, f = saved
    (dx1, df, act, dgu, h2, dg_post_ffn, dg_pre_ffn), landed_prev = _ffn_bwd(
        dx, f, x1, gu, w["gu"], w["dn"], gains["post_ffn_norm"], gains["pre_ffn_norm"], comm=ffn_comm)
    ffn = exchange.ffn_grads(act, df, h2, dgu)
    dmix, dattn, ds, db, dg_post_mix, dg_attn, dg_conv = _mix_bwd(
        dx1, mix, attn, u, b, c, taps, gains["attn_out_norm"], gains["conv_out_norm"], gains["post_mix_norm"], w["out"])
    g_out = exchange.w_out_grad(merged, dmix)
    attn_comm = exchange.attention_passenger(ffn, g_out)
    (dq, dk, dv), landed = _attn_bwd(q, k, v, dattn, attn, lse, comm=attn_comm)
    dx0, dproj, h1, dg_pre_mix, dtaps = _inproj_bwd(
        dq, dk, dv, ds, db, u, c, taps, *tables, w["in"], x0, gains["pre_mix_norm"], dx1)[0]
    small = _pack_small(dg_pre_mix, dg_post_mix, dg_pre_ffn, dg_post_ffn, dg_attn, dg_conv, dtaps)
    small_rows = None
    if last is not None:
        small_above, local_loss = last
        small_rows = lax.dynamic_update_slice(jnp.concatenate([small, small_above], axis=0), local_loss, (SMALL_ROWS - 1, 0))
    g_in, gathered = exchange.w_in_grad(h1, dproj, small_rows)
    rest = dict(zip(attn_comm.kinds, landed)) if attn_comm is not None else dict(ffn, out=g_out)
    return dx0, g_in, small, landed_prev, rest, gathered


def kernel(x, positions, pre_mix_norm, w_in, conv_w, attn_out_norm, conv_out_norm, w_out, post_mix_norm, pre_ffn_norm, w_gate_up, w_down, post_ffn_norm, loss_target, m_pre_mix_norm, m_w_in, m_conv_w, m_attn_out_norm, m_conv_out_norm, m_w_out, m_post_mix_norm, m_pre_ffn_norm, m_w_gate_up, m_w_down, m_post_ffn_norm, v_pre_mix_norm, v_w_in, v_conv_w, v_attn_out_norm, v_conv_out_norm, v_w_out, v_post_mix_norm, v_pre_ffn_norm, v_w_gate_up, v_w_down, v_post_ffn_norm):
    mx, my, mc = _place()
    me = _block_of(mx, my, mc)
    conv_channels = conv_w.shape[-1]

    def hidden_major(a):
        return jnp.swapaxes(a, 1, 2)

    big_w = dict(zip(WEIGHT_KINDS, (w_in, w_out, hidden_major(w_gate_up), w_down)))
    big_m = dict(zip(WEIGHT_KINDS, (m_w_in, m_w_out, hidden_major(m_w_gate_up), m_w_down)))
    big_v = dict(zip(WEIGHT_KINDS, (v_w_in, v_w_out, hidden_major(v_w_gate_up), v_w_down)))
    early = ("in", "out", "dn")
    shards = {early[0]: big_w[early[0]].astype(BF16)}
    all_gains = dict(pre_mix_norm=pre_mix_norm, attn_out_norm=attn_out_norm, conv_out_norm=conv_out_norm, post_mix_norm=post_mix_norm,
                     pre_ffn_norm=pre_ffn_norm, post_ffn_norm=post_ffn_norm)

    def gather(kinds, l):
        return _Gather(kinds, [shards[kind] for kind in kinds], l)

    def gains(l):
        return {name: g[l:l + 1] for name, g in all_gains.items()}

    taps_flat = jnp.pad(conv_w.reshape(-1), (0, 8 * LANES - conv_w.size)).reshape(8, LANES)
    later = [kind for kind in WEIGHT_KINDS if kind != early[0]]
    made, (first, taps_all) = _rope(positions, [big_w[kind] for kind in later],
                                    _Together(gather(early[:1], 0), _GatherSmall(taps_flat, "taps")))
    tables = made[:3]
    shards.update(zip(later, made[3:]))
    taps_all = taps_all.reshape(N_DEV, 8 * LANES)[:, :conv_w.size]
    conv_w_full = taps_all.reshape(N_DEV, DEPTH, 3, conv_channels).transpose(1, 2, 0, 3).reshape(DEPTH, 3, ATTN_WIDTH)
    weights = [dict(zip(early[:1], [first]))] + [None] * (DEPTH - 1)
    saved = [None] * DEPTH
    h = x[0]
    for l in range(DEPTH):
        inproj_comm = gather(("out",), 0) if l == 0 else None
        mix_comm = gather(("dn",), 0) if l == 0 else None
        ffn_comm = gather(early, l + 1) if l + 1 < DEPTH else None
        target = loss_target[0] if l + 1 == DEPTH else None
        h, saved[l], weights[l], landed = _layer_forward(h, gains(l), conv_w_full[l], tables, weights[l], inproj_comm,
                                                         gather(("gu",), l), mix_comm, ffn_comm, target)
        if ffn_comm is not None:
            weights[l + 1] = dict(zip(ffn_comm.kinds, landed))
    dx, local_loss = h

    parts = {kind: [None] * DEPTH for kind in WEIGHT_KINDS}
    small_grads = [None] * DEPTH
    g_in_above = None
    exchange = _GradientExchange()
    for l in reversed(range(DEPTH)):
        ffn_comm = _Scatter(("in",), (g_in_above,)) if g_in_above is not None else None
        last = (jnp.concatenate(small_grads[1:], axis=0), local_loss) if l == 0 else None
        dx, g_in_above, small_grads[l], landed, rest, gathered = _layer_backward(
            dx, saved[l], gains(l), conv_w_full[l], tables, weights[l], ffn_comm, exchange, last)
        if ffn_comm is not None:
            parts["in"][l + 1] = landed[0]
        for kind, part in rest.items():
            parts[kind][l] = part

    parts["in"][0] = g_in_above
    tiles = {"in": 256, "out": 128, "gu": 176, "dn": 176}
    big = {kind: _adamw_layers(parts[kind], big_w[kind], big_m[kind], big_v[kind], tiles[kind], "adamw_" + kind)[0]
           for kind in WEIGHT_KINDS}

    small_m = (m_pre_mix_norm, m_post_mix_norm, m_pre_ffn_norm, m_post_ffn_norm, m_attn_out_norm, m_conv_out_norm)
    small_v = (v_pre_mix_norm, v_post_mix_norm, v_pre_ffn_norm, v_post_ffn_norm, v_attn_out_norm, v_conv_out_norm)
    per_gain, taps_grad_full, loss = _adamw_small(gathered, [all_gains[name] for name in GAIN_NAMES], small_m, small_v)
    small = [{name: per_gain[k][o] for k, name in enumerate(GAIN_NAMES)} for o in range(4)]
    loss = loss[0, 0]
    taps_grad = lax.dynamic_slice_in_dim(taps_grad_full, me * conv_channels, conv_channels, axis=2)

    taps = [taps_grad, *_adamw_plain(taps_grad, conv_w, m_conv_w, v_conv_w, "adamw_taps")]

    def leaves(o):
        s = small[o]
        return (s["pre_mix_norm"], big["in"][o], taps[o], s["attn_out_norm"], s["conv_out_norm"], big["out"][o], s["post_mix_norm"],
                s["pre_ffn_norm"], hidden_major(big["gu"][o]), big["dn"][o], s["post_ffn_norm"])

    return (loss, dx[None], *leaves(0), *leaves(1), *leaves(2), *leaves(3))
```

```python
import math

import jax
import jax.numpy as jnp
from jax import lax
from jax.experimental import pallas as pl
from jax.experimental.pallas import tpu as pltpu

F32 = jnp.float32
BF16 = jnp.bfloat16
MESH = pl.DeviceIdType.MESH

SEQ = 4096
D_MODEL = 1024
DEPTH = 4
N_DEV = 8
ATTN_WIDTH = 512
IN_PROJ_WIDTH = 3072
FFN_HIDDEN = 2816
FFN_BLOCK = 2 * FFN_HIDDEN // N_DEV
W_IN_BLOCK = IN_PROJ_WIDTH // N_DEV
W_OUT_BLOCK = D_MODEL // N_DEV
W_DOWN_BLOCK = FFN_HIDDEN // N_DEV
HEAD_DIM = 64
ROPE_DIM = 16
ROPE_THETA = 500000.0
DILATIONS = (1, 4, 16)
HALF_WINDOW = 64
RMS_EPS = 1e-6
NEG_INF = -1e30
LANES = 128
Q_BLOCK = 128
K_WINDOW = Q_BLOCK + 2 * HALF_WINDOW
PERM_CHUNK = 256
ROW_TILE = 512
FFN_TILE = 256
WGRAD_COLS = 512
ADAM_LR, ADAM_B1, ADAM_B2, ADAM_EPS, ADAM_WD, ADAM_STEP = 0.001, 0.9, 0.999, 1e-08, 0.01, 10
MIB = 1024 * 1024
PINNED_BYTES = 256 * 1024

WEIGHT_KINDS = ("in", "out", "gu", "dn")
FULL_SHAPES = {"in": (D_MODEL, IN_PROJ_WIDTH), "out": (D_MODEL, D_MODEL), "gu": (N_DEV, FFN_BLOCK, D_MODEL), "dn": (FFN_HIDDEN, D_MODEL)}
SHARD_SHAPES = {"in": (D_MODEL, W_IN_BLOCK), "out": (W_OUT_BLOCK, D_MODEL), "gu": (FFN_BLOCK, D_MODEL), "dn": (W_DOWN_BLOCK, D_MODEL)}
ANY = pl.BlockSpec(memory_space=pl.ANY)


def _sds(shape, dtype):
    return jax.ShapeDtypeStruct(shape, dtype)


def _rows(width, tile=ROW_TILE):
    return pl.BlockSpec((tile, width), lambda i: (i, 0))


def _frows(width):
    return _rows(width, FFN_TILE)


def _cols(height, tile):
    return pl.BlockSpec((height, tile), lambda i: (0, i))


def _const(shape):
    return pl.BlockSpec(shape, lambda i: (0,) * len(shape))


def _const1(shape):
    return pl.BlockSpec(shape, lambda i: (0,) * len(shape), pipeline_mode=pl.Buffered(1))


HALO_ROWS = 16


def _halo_prev(width, tile=ROW_TILE):
    return pl.BlockSpec((HALO_ROWS, width), lambda i: (jnp.maximum(i * (tile // HALO_ROWS) - 1, 0), 0))


def _halo_next(width, tile=ROW_TILE):
    return pl.BlockSpec((HALO_ROWS, width), lambda i: (jnp.minimum((i + 1) * (tile // HALO_ROWS), SEQ // HALO_ROWS - 1), 0))


def _rms(x):
    r = lax.rsqrt(jnp.mean(x * x, axis=-1, keepdims=True) + RMS_EPS)
    return x * r, r


def _rms_bwd(dn, y, r):
    return r * (dn - y * jnp.mean(dn * y, axis=-1, keepdims=True))


def _dot(a, b):
    return jnp.dot(a, b, preferred_element_type=F32)


def _dot_nt(a, b):
    return lax.dot_general(a, b, (((1,), (1,)), ((), ())), preferred_element_type=F32)


def _dot_tn(a, b):
    return lax.dot_general(a, b, (((0,), (0,)), ((), ())), preferred_element_type=F32)


def _place():
    return lax.axis_index("x"), lax.axis_index("y"), lax.axis_index("c")


def _block_of(px, py, pc):
    return 4 * px + 2 * py + pc


def _weight_block(ref, kind, blk):
    if kind == "in":
        return ref.at[:, pl.ds(blk * W_IN_BLOCK, W_IN_BLOCK)]
    if kind == "out":
        return ref.at[pl.ds(blk * W_OUT_BLOCK, W_OUT_BLOCK), :]
    if kind == "gu":
        return ref.at[blk]
    return ref.at[pl.ds(blk * W_DOWN_BLOCK, W_DOWN_BLOCK), :]


def _dma_semaphores(n):
    return [pltpu.SemaphoreType.DMA((n, 7)), pltpu.SemaphoreType.DMA((n, 7)), pltpu.SemaphoreType.DMA((n,))]


class _Gather:
    def __init__(self, kinds, shards, layer):
        self.kinds, self.operands, self.layer = tuple(kinds), list(shards), layer
        self.tag = "gather_" + "_".join(kinds)
        self.out_shape = [_sds(FULL_SHAPES[k], BF16) for k in kinds]
        self.scratch = _dma_semaphores(len(kinds))

    def _parties(self):
        x, y, c = _place()
        return (x, y, c), (x, y, 1 - c), [(1 - x, y), (x, 1 - y), (1 - x, 1 - y)], c

    def _copy(self, outs, sems, w, k, block, to, src=None):
        dst = _weight_block(outs[w], self.kinds[w], _block_of(*block))
        return pltpu.make_async_remote_copy(src_ref=dst if src is None else src, dst_ref=dst, send_sem=sems[0].at[w, k],
                                            recv_sem=sems[1].at[w, k], device_id=to, device_id_type=MESH)

    def _own(self, srcs, outs, sems, w, me):
        return pltpu.make_async_copy(srcs[w].at[self.layer], _weight_block(outs[w], self.kinds[w], _block_of(*me)), sems[2].at[w])

    def _first(self, srcs, outs, sems, w):
        me, sibling, chips, c = self._parties()
        shard = srcs[w].at[self.layer]
        return [self._copy(outs, sems, w, 0, me, sibling, src=shard)] + [
            self._copy(outs, sems, w, 1 + j, me, (*chip, c), src=shard) for j, chip in enumerate(chips)]

    def start(self, srcs, outs, sems):
        me = self._parties()[0]
        for w in range(len(self.kinds)):
            self._own(srcs, outs, sems, w, me).start()
            for cp in self._first(srcs, outs, sems, w):
                cp.start()

    def forward(self, srcs, outs, sems):
        me, sibling, chips, c = self._parties()
        for j, chip in enumerate(chips):
            for w in range(len(self.kinds)):
                self._copy(outs, sems, w, 1 + j, (*chip, c), me).wait_recv()
                self._copy(outs, sems, w, 4 + j, (*chip, c), sibling).start()

    def finish(self, srcs, outs, sems):
        me, sibling, chips, c = self._parties()
        for w in range(len(self.kinds)):
            self._copy(outs, sems, w, 0, sibling, me).wait_recv()
            for j, chip in enumerate(chips):
                self._copy(outs, sems, w, 4 + j, (*chip, 1 - c), me).wait_recv()
        for w in range(len(self.kinds)):
            for cp in self._first(srcs, outs, sems, w):
                cp.wait_send()
            for j, chip in enumerate(chips):
                self._copy(outs, sems, w, 4 + j, (*chip, c), sibling).wait_send()
            self._own(srcs, outs, sems, w, me).wait()


def _peers(x, y, c):
    return [(x ^ a, y ^ b, c ^ e) for a in (0, 1) for b in (0, 1) for e in (0, 1) if (a, b, e) != (0, 0, 0)]


class _Scatter:
    def __init__(self, kinds, grads):
        self.kinds, self.operands = tuple(kinds), list(grads)
        self.tag = "scatter_" + "_".join(kinds)
        self.out_shape = [_sds((N_DEV,) + SHARD_SHAPES[k], BF16) for k in kinds]
        self.scratch = _dma_semaphores(len(kinds))

    def _copies(self, srcs, outs, sems):
        x, y, c = _place()
        me = _block_of(x, y, c)
        copies = []
        for w, kind in enumerate(self.kinds):
            copies.append(pltpu.make_async_copy(_weight_block(srcs[w], kind, me), outs[w].at[me], sems[2].at[w]))
            for k, peer in enumerate(_peers(x, y, c)):
                copies.append(pltpu.make_async_remote_copy(
                    src_ref=_weight_block(srcs[w], kind, _block_of(*peer)), dst_ref=outs[w].at[me],
                    send_sem=sems[0].at[w, k], recv_sem=sems[1].at[w, k], device_id=peer, device_id_type=MESH))
        return copies

    def start(self, srcs, outs, sems):
        for cp in self._copies(srcs, outs, sems):
            cp.start()

    def forward(self, srcs, outs, sems):
        pass

    def finish(self, srcs, outs, sems):
        for cp in self._copies(srcs, outs, sems):
            cp.wait()


N_CHIPS = N_DEV // 2


class _ScatterChips:
    def __init__(self, kinds, blocks):
        self.kinds, self.operands = tuple(kinds), list(blocks)
        self.tag = "scatter_chips_" + "_".join(kinds)
        self.out_shape = [_sds((N_CHIPS,) + SHARD_SHAPES[k], BF16) for k in kinds]
        self.scratch = _dma_semaphores(len(kinds))

    def _copies(self, srcs, outs, sems):
        x, y, c = _place()
        mine = 2 * x + y
        copies = []
        for w in range(len(self.kinds)):
            copies.append(pltpu.make_async_copy(srcs[w].at[mine], outs[w].at[mine], sems[2].at[w]))
            for j, (px, py) in enumerate([(1 - x, y), (x, 1 - y), (1 - x, 1 - y)]):
                copies.append(pltpu.make_async_remote_copy(
                    src_ref=srcs[w].at[2 * px + py], dst_ref=outs[w].at[mine], send_sem=sems[0].at[w, j], recv_sem=sems[1].at[w, j],
                    device_id=(px, py, c), device_id_type=MESH))
        return copies

    def start(self, srcs, outs, sems):
        for cp in self._copies(srcs, outs, sems):
            cp.start()

    def forward(self, srcs, outs, sems):
        pass

    def finish(self, srcs, outs, sems):
        for cp in self._copies(srcs, outs, sems):
            cp.wait()


def _in_hbm(a):
    return pltpu.with_memory_space_constraint(a, pltpu.HBM) if a.size * a.dtype.itemsize >= PINNED_BYTES else a


def _out_hbm(s):
    return pltpu.HBM(s.shape, s.dtype) if math.prod(s.shape) * jnp.dtype(s.dtype).itemsize >= PINNED_BYTES else s


def _call(body, args, *, grid, in_specs, out_specs, out_shape, scratch_shapes=(), vmem_mib=None, name, comm=None):
    kwargs = {} if vmem_mib is None else dict(compiler_params=pltpu.CompilerParams(vmem_limit_bytes=vmem_mib * MIB))
    in_specs, out_specs, out_shape, scratch_shapes = list(in_specs), list(out_specs), list(out_shape), list(scratch_shapes)
    args = [_in_hbm(a) for a in args]
    out_shape = [_out_hbm(s) for s in out_shape]
    if comm is None:
        res = pl.pallas_call(body, grid=grid, in_specs=in_specs, out_specs=out_specs, out_shape=out_shape,
                             scratch_shapes=scratch_shapes, name=name, **kwargs)(*args)
        return list(res), None
    n_in, n_out, n_scr = len(in_specs), len(out_specs), len(scratch_shapes)
    c_in, c_out = len(comm.operands), len(comm.out_shape)
    last = math.prod(grid) - 1

    def carried(*refs):
        cuts = [n_in, c_in, n_out, c_out, n_scr]
        parts, at = [], 0
        for n in cuts:
            parts.append(refs[at:at + n])
            at += n
        ins, c_ins, outs, c_outs, scr = parts
        sems = refs[at:]
        step = pl.program_id(0)
        for axis in range(1, len(grid)):
            step = step * grid[axis] + pl.program_id(axis)

        @pl.when(step == 0)
        def _():
            comm.start(c_ins, c_outs, sems)

        @pl.when(step == last)
        def _():
            comm.forward(c_ins, c_outs, sems)

        body(*ins, *outs, *scr)

        @pl.when(step == last)
        def _():
            comm.finish(c_ins, c_outs, sems)

    res = pl.pallas_call(carried, grid=grid, in_specs=in_specs + [ANY] * c_in, out_specs=out_specs + [ANY] * c_out,
                         out_shape=out_shape + [_out_hbm(s) for s in comm.out_shape], scratch_shapes=scratch_shapes + comm.scratch,
                         name=name + "_" + comm.tag, **kwargs)(*args, *[_in_hbm(a) for a in comm.operands])
    return list(res[:n_out]), list(res[n_out:])


class _GatherSmall:
    def __init__(self, v, tag):
        self.operands, self.tag = [v], tag
        self.out_shape = [_sds((N_DEV,) + v.shape, F32)]
        self.scratch = _dma_semaphores(1)

    def _copies(self, srcs, outs, sems):
        x, y, c = _place()
        me = _block_of(x, y, c)
        copies = [pltpu.make_async_copy(srcs[0], outs[0].at[me], sems[2].at[0])]
        for k, peer in enumerate(_peers(x, y, c)):
            copies.append(pltpu.make_async_remote_copy(src_ref=srcs[0], dst_ref=outs[0].at[me], send_sem=sems[0].at[0, k],
                                                       recv_sem=sems[1].at[0, k], device_id=peer, device_id_type=MESH))
        return copies

    def start(self, srcs, outs, sems):
        for cp in self._copies(srcs, outs, sems):
            cp.start()

    def forward(self, srcs, outs, sems):
        pass

    def finish(self, srcs, outs, sems):
        for cp in self._copies(srcs, outs, sems):
            cp.wait()


class _Together:
    def __init__(self, *comms):
        self.comms = comms
        self.operands = [a for cm in comms for a in cm.operands]
        self.out_shape = [s for cm in comms for s in cm.out_shape]
        self.scratch = [s for cm in comms for s in cm.scratch]
        self.tag = "_".join(cm.tag for cm in comms)
        self.kinds = tuple(kind for cm in comms for kind in getattr(cm, "kinds", ()))

    def _each(self, srcs, outs, sems):
        a = o = s = 0
        for cm in self.comms:
            na, no, ns = len(cm.operands), len(cm.out_shape), len(cm.scratch)
            yield cm, srcs[a:a + na], outs[o:o + no], sems[s:s + ns]
            a, o, s = a + na, o + no, s + ns

    def start(self, srcs, outs, sems):
        for cm, *refs in self._each(srcs, outs, sems):
            cm.start(*refs)

    def forward(self, srcs, outs, sems):
        for cm, *refs in self._each(srcs, outs, sems):
            cm.forward(*refs)

    def finish(self, srcs, outs, sems):
        for cm, *refs in self._each(srcs, outs, sems):
            cm.finish(*refs)


def _rope_tables(pos_col, freq_row, casts, comm=None):
    steps = SEQ // ROW_TILE
    assert steps == 2 * DEPTH

    def body(p_ref, f_ref, *refs):
        wide, (c_ref, sa_ref, sb_ref), narrow = refs[:len(casts)], refs[len(casts):len(casts) + 3], refs[len(casts) + 3:]
        ang = p_ref[...].astype(F32) * f_ref[...]
        lane = lax.broadcasted_iota(jnp.int32, ang.shape, 1) % HEAD_DIM
        cos, sin = jnp.cos(ang), jnp.sin(ang)
        c_ref[...] = jnp.where(lane < ROPE_DIM, cos, 1.0)
        sa_ref[...] = jnp.where(lane < ROPE_DIM // 2, -sin, 0.0)
        sb_ref[...] = jnp.where((lane >= ROPE_DIM // 2) & (lane < ROPE_DIM), sin, 0.0)
        for src, dst in zip(wide, narrow):
            dst[...] = src[...].astype(BF16)

    halves = [pl.BlockSpec((1, a.shape[1] // 2, a.shape[2]), lambda i: (i // 2, i % 2, 0)) for a in casts]
    return _call(body, (pos_col, freq_row, *casts), grid=(steps,), in_specs=[_rows(1), _const((1, LANES))] + halves,
                 out_specs=[_rows(LANES)] * 3 + halves,
                 out_shape=[_sds((SEQ, LANES), F32)] * 3 + [_sds(a.shape, BF16) for a in casts], name="rope_tables", comm=comm)


def _rotate(t, c, sa, sb):
    parts = []
    for g in range(ATTN_WIDTH // LANES):
        tg = t[:, g * LANES:(g + 1) * LANES]
        parts.append(tg * c + pltpu.roll(tg, LANES - 8, axis=1) * sa + pltpu.roll(tg, 8, axis=1) * sb)
    return jnp.concatenate(parts, axis=1)


def _rotate_transposed(dt, c, sa, sb):
    parts = []
    for g in range(ATTN_WIDTH // LANES):
        dg = dt[:, g * LANES:(g + 1) * LANES]
        parts.append(dg * c + pltpu.roll(dg * sa, 8, axis=1) + pltpu.roll(dg * sb, LANES - 8, axis=1))
    return jnp.concatenate(parts, axis=1)


def _inproj_fwd(x, g_pre, w_in, tc, tsa, tsb, comm=None):
    def body(x_ref, g_ref, w_ref, c_ref, sa_ref, sb_ref, q_ref, k_ref, v_ref, u_ref, b_ref, cc_ref):
        y, _ = _rms(x_ref[...])
        h = (y * g_ref[...]).astype(BF16)

        def proj(n):
            return _dot(h, w_ref[:, n * ATTN_WIDTH:(n + 1) * ATTN_WIDTH])

        c, sa, sb = c_ref[...], sa_ref[...], sb_ref[...]
        q_ref[...] = (_rotate(proj(0), c, sa, sb) * (HEAD_DIM ** -0.5)).astype(BF16)
        k_ref[...] = _rotate(proj(1), c, sa, sb).astype(BF16)
        v_ref[...] = proj(2).astype(BF16)
        u_ref[...] = proj(3).astype(BF16)
        b_ref[...] = proj(4).astype(BF16)
        cc_ref[...] = proj(5).astype(BF16)

    a = ATTN_WIDTH
    return _call(
        body, (x, g_pre, w_in, tc, tsa, tsb), grid=(SEQ // ROW_TILE,),
        in_specs=[_rows(D_MODEL), _const((1, D_MODEL)), _const1((D_MODEL, IN_PROJ_WIDTH)), _rows(LANES), _rows(LANES), _rows(LANES)],
        out_specs=[_rows(a)] * 6, out_shape=[_sds((SEQ, a), BF16)] * 6,
        vmem_mib=40, name="inproj_fwd", comm=comm)


def _head_masks():
    lane = lax.broadcasted_iota(jnp.int32, (1, LANES), 1)
    first = lane < HEAD_DIM
    return first, first.astype(F32), 1.0 - first.astype(F32)


def _perm_chunks(dil):
    length = SEQ // dil
    out = []
    for r in range(dil):
        for c0 in range(0, length, PERM_CHUNK):
            chunk = (r * length + c0) // PERM_CHUNK
            rows = pl.ds(c0, PERM_CHUNK) if dil == 1 else pl.ds(r + dil * c0, PERM_CHUNK, stride=dil)
            out.append((chunk, rows))
    return out


def _chunk(c, offset=0):
    return pl.ds(offset + c * PERM_CHUNK, PERM_CHUNK)


def _write_band_bias(bias_ref):
    rr = lax.broadcasted_iota(jnp.int32, (Q_BLOCK, K_WINDOW), 0)
    cc = lax.broadcasted_iota(jnp.int32, (Q_BLOCK, K_WINDOW), 1)
    band = (cc >= rr) & (cc - rr <= 2 * HALF_WINDOW)
    bias_ref[0] = jnp.where(band, 0.0, NEG_INF)
    bias_ref[1] = jnp.where(band & (cc >= HALF_WINDOW), 0.0, NEG_INF)
    bias_ref[2] = jnp.where(band & (cc < Q_BLOCK + HALF_WINDOW), 0.0, NEG_INF)


def _band_bias_index(m0, length):
    return jnp.where(m0 % length == 0, 1, 0) + jnp.where((m0 + Q_BLOCK) % length == 0, 2, 0)


def _zero_key_padding(bufs):
    pad = jnp.zeros((HALF_WINDOW, LANES), BF16)
    for buf in bufs:
        buf[pl.ds(0, HALF_WINDOW), :] = pad
        buf[pl.ds(SEQ + HALF_WINDOW, HALF_WINDOW), :] = pad


def _attn_fwd(q, k, v, comm=None):
    group = 8

    def body(q_ref, k_ref, v_ref, o_ref, lse_ref, q32, k32, v32, qa, qb, kp, vp, accp, mlp,
             acc0, acc1, acc2, ml0, ml1, ml2, bias_ref):
        first, mask_a, mask_b = _head_masks()
        low = lax.broadcasted_iota(jnp.int32, (1, LANES), 1) % HEAD_DIM < HEAD_DIM // 2
        _zero_key_padding((kp, vp))
        _write_band_bias(bias_ref)
        q32[...] = q_ref[...].astype(F32)
        k32[...] = k_ref[...].astype(F32)
        v32[...] = v_ref[...].astype(F32)
        natural = ((acc0, ml0), (acc1, ml1), (acc2, ml2))

        for branch, dil in enumerate(DILATIONS):
            length = SEQ // dil
            assert length >= 2 * Q_BLOCK
            chunks = _perm_chunks(dil)
            for c, rows in chunks:
                val = q32[rows, :]
                qa[_chunk(c), :] = (val * mask_a).astype(BF16)
                qb[_chunk(c), :] = (val * mask_b).astype(BF16)
                kp[_chunk(c, HALF_WINDOW), :] = k32[rows, :].astype(BF16)
                vp[_chunk(c, HALF_WINDOW), :] = v32[rows, :].astype(BF16)
            acc_dst, ml_dst = natural[branch] if dil == 1 else (accp, mlp)

            def blocks(i, carry, length=length, acc_dst=acc_dst, ml_dst=ml_dst):
                base = pl.multiple_of(i * (group * Q_BLOCK), group * Q_BLOCK)
                starts = [base + g * Q_BLOCK for g in range(group)]
                scores = [[_dot_nt(qx[pl.ds(m0, Q_BLOCK), :], kp[pl.ds(m0, K_WINDOW), :]) for qx in (qa, qb)] for m0 in starts]
                probs = []
                for m0, pair in zip(starts, scores):
                    bias = bias_ref[_band_bias_index(m0, length)]
                    stats, ps = [], []
                    for s in pair:
                        s = s + bias
                        m = jnp.max(s, axis=1, keepdims=True)
                        p = jnp.exp(s - m)
                        stats.append(jnp.where(low, m, jnp.sum(p, axis=1, keepdims=True)))
                        ps.append(p.astype(BF16))
                    ml_dst[pl.ds(m0, Q_BLOCK), :] = jnp.where(first, stats[0], stats[1])
                    probs.append(ps)
                for m0, ps in zip(starts, probs):
                    vw = vp[pl.ds(m0, K_WINDOW), :]
                    acc_dst[pl.ds(m0, Q_BLOCK), :] = jnp.where(first, _dot(ps[0], vw), _dot(ps[1], vw))
                return carry

            lax.fori_loop(0, SEQ // (group * Q_BLOCK), blocks, 0)

            if dil > 1:
                for c, rows in chunks:
                    natural[branch][0][rows, :] = accp[_chunk(c), :]
                    natural[branch][1][rows, :] = mlp[_chunk(c), :]

        for c in range(SEQ // PERM_CHUNK):
            def up(t):
                return jnp.where(low, t, pltpu.roll(t, HEAD_DIM // 2, axis=1))

            packed = [ml[_chunk(c), :] for _, ml in natural]
            top = jnp.maximum(jnp.maximum(packed[0], packed[1]), packed[2])
            m_all = up(top)
            es = [up(jnp.exp(ml - top)) for ml in packed]
            z = packed[0] * es[0] + packed[1] * es[1] + packed[2] * es[2]
            z = jnp.where(low, pltpu.roll(z, LANES - HEAD_DIM // 2, axis=1), z)
            num = natural[0][0][_chunk(c), :] * es[0] + natural[1][0][_chunk(c), :] * es[1] + natural[2][0][_chunk(c), :] * es[2]
            o_ref[_chunk(c), :] = num / z
            lse_ref[_chunk(c), :] = m_all + jnp.log(z)

    col = pl.BlockSpec((SEQ, LANES), lambda h: (0, h))
    padded = SEQ + 2 * HALF_WINDOW
    return _call(
        body, (q, k, v), grid=(ATTN_WIDTH // LANES,), in_specs=[col] * 3, out_specs=[col] * 2,
        out_shape=[_sds((SEQ, ATTN_WIDTH), F32)] * 2,
        scratch_shapes=[pltpu.VMEM((SEQ, LANES), F32)] * 3 + [pltpu.VMEM((SEQ, LANES), BF16)] * 2
        + [pltpu.VMEM((padded, LANES), BF16)] * 2 + [pltpu.VMEM((SEQ, LANES), F32)] * 8
        + [pltpu.VMEM((3, Q_BLOCK, K_WINDOW), F32)],
        vmem_mib=52, name="attn_fwd", comm=comm)


def _shifted(t, before, after, i):
    tile = t.shape[0]
    row = lax.broadcasted_iota(jnp.int32, (tile, 1), 0)
    before = jnp.where(i > 0, before, 0.0)
    after = jnp.where(i < SEQ // tile - 1, after, 0.0)
    return (jnp.where(row == 0, before, pltpu.roll(t, 1, axis=0)),
            jnp.where(row == tile - 1, after, pltpu.roll(t, tile - 1, axis=0)))


def _last_row(ref):
    return ref[HALO_ROWS - 1:HALO_ROWS, :].astype(F32)


def _first_row(ref):
    return ref[0:1, :].astype(F32)


def _conv_parts(u, c, u_prev, c_prev, u_next, c_next, cw, i):
    t = c * u
    t_prev, t_next = _shifted(t, _last_row(c_prev) * _last_row(u_prev), _first_row(c_next) * _first_row(u_next), i)
    s = cw[0:1, :] * t_prev + cw[1:2, :] * t + cw[2:3, :] * t_next
    return t, t_prev, t_next, s


def _mix_fwd(attn, u, b, c, conv_w, g_attn, g_conv, w_out, x, g_post, comm=None):
    def body(a_ref, u_ref, b_ref, c_ref, up_ref, cp_ref, un_ref, cn_ref, cw_ref, ga_ref, gc_ref, w_ref, x_ref, gp_ref,
             x1_ref, mg_ref, mix_ref):
        i = pl.program_id(0)
        _, _, _, s = _conv_parts(u_ref[...].astype(F32), c_ref[...].astype(F32), up_ref, cp_ref, un_ref, cn_ref, cw_ref[...], i)
        ya, _ = _rms(a_ref[...])
        yc, _ = _rms(b_ref[...].astype(F32) * s)
        merged = jnp.concatenate([ya * ga_ref[...], yc * gc_ref[...]], axis=1).astype(BF16)
        mix = _dot(merged, w_ref[...])
        ym, _ = _rms(mix)
        mg_ref[...] = merged.T
        mix_ref[...] = mix.astype(BF16)
        x1_ref[...] = x_ref[...] + ym * gp_ref[...]

    a = ATTN_WIDTH
    return _call(
        body, (attn, u, b, c, u, c, u, c, conv_w, g_attn, g_conv, w_out, x, g_post), grid=(SEQ // ROW_TILE,),
        in_specs=[_rows(a)] * 4 + [_halo_prev(a)] * 2 + [_halo_next(a)] * 2
        + [_const((3, a)), _const((1, a)), _const((1, a)), _const1((D_MODEL, D_MODEL)), _rows(D_MODEL), _const((1, D_MODEL))],
        out_specs=[_rows(D_MODEL), _cols(D_MODEL, ROW_TILE), _rows(D_MODEL)],
        out_shape=[_sds((SEQ, D_MODEL), F32), _sds((D_MODEL, SEQ), BF16), _sds((SEQ, D_MODEL), BF16)],
        vmem_mib=40, name="mix_fwd", comm=comm)


def _gu_spec():
    return pl.BlockSpec((N_DEV, FFN_TILE, FFN_BLOCK), lambda i: (0, i, 0))


def _ffn_fwd(x1, g_pre, w_gu, w_dn, g_post, comm=None, target=None):
    n_tiles = SEQ // ROW_TILE

    def body(*refs):
        if target is None:
            x_ref, g_ref, wgu_ref, wdn_ref, gp_ref, x2_ref, gu_ref, f_ref = refs
        else:
            x_ref, g_ref, wgu_ref, wdn_ref, gp_ref, t_ref, x2_ref, gu_ref, f_ref, loss_ref, acc = refs
        x1 = x_ref[...]
        y, _ = _rms(x1)
        h = (y * g_ref[...]).astype(BF16)
        f = jnp.zeros((ROW_TILE, D_MODEL), F32)

        def gate_up(j):
            return _dot_nt(h, wgu_ref[j]), _dot_nt(h, wgu_ref[j + N_DEV // 2])

        ahead = gate_up(0)
        for j in range(N_DEV // 2):
            gate, up = ahead
            if j + 1 < N_DEV // 2:
                ahead = gate_up(j + 1)
            gu_ref[j] = gate.astype(BF16)
            gu_ref[j + N_DEV // 2] = up.astype(BF16)
            act = (gate * jax.nn.sigmoid(gate) * up).astype(BF16)
            f = f + _dot(act, wdn_ref[pl.ds(j * FFN_BLOCK, FFN_BLOCK), :])
        yf, _ = _rms(f)
        f_ref[...] = f
        x2 = x1 + yf * gp_ref[...]
        if target is None:
            x2_ref[...] = x2
        else:
            i = pl.program_id(0)
            err = x2 - t_ref[...]
            x2_ref[...] = err * (1.0 / D_MODEL)

            @pl.when(i == 0)
            def _():
                acc[...] = jnp.zeros_like(acc)

            acc[...] += jnp.sum(err * err, axis=0, keepdims=True)

            @pl.when(i == n_tiles - 1)
            def _():
                loss_ref[...] = jnp.sum(acc[...], axis=1, keepdims=True) * (0.5 / D_MODEL)

    with_loss = target is not None
    return _call(
        body, (x1, g_pre, w_gu, w_dn, g_post) + ((target,) if with_loss else ()), grid=(n_tiles,),
        in_specs=[_rows(D_MODEL), _const((1, D_MODEL)), _const1((N_DEV, FFN_BLOCK, D_MODEL)), _const1((FFN_HIDDEN, D_MODEL)),
                  _const((1, D_MODEL))] + ([_rows(D_MODEL)] if with_loss else []),
        out_specs=[_rows(D_MODEL), pl.BlockSpec((N_DEV, ROW_TILE, FFN_BLOCK), lambda i: (0, i, 0)), _rows(D_MODEL)]
        + ([_const((1, 1))] if with_loss else []),
        out_shape=[_sds((SEQ, D_MODEL), F32), _sds((N_DEV, SEQ, FFN_BLOCK), BF16), _sds((SEQ, D_MODEL), F32)]
        + ([_sds((1, 1), F32)] if with_loss else []),
        scratch_shapes=[pltpu.VMEM((1, D_MODEL), F32)] if with_loss else [],
        vmem_mib=58, name="ffn_fwd_loss" if with_loss else "ffn_fwd", comm=comm)


def _accumulate(pairs, i):
    @pl.when(i == 0)
    def _():
        for ref, value in pairs:
            ref[...] = value

    @pl.when(i > 0)
    def _():
        for ref, value in pairs:
            ref[...] += value


def _colsum(v):
    return jnp.sum(v, axis=0, keepdims=True)


def _ffn_bwd(dx2, f, x1, gu, w_gu, w_dn, g_post, g_pre, comm=None):
    half = N_DEV // 2

    def body(dx2_ref, f_ref, x1_ref, gu_ref, wgu_ref, wdn_ref, gpost_ref, gpre_ref,
             dx1_ref, df_ref, act_ref, dgu_ref, h_ref, dgpost_ref, dgpre_ref):
        i = pl.program_id(0)
        dx2 = dx2_ref[...]
        yf, rf = _rms(f_ref[...])
        dg_post = _colsum(dx2 * yf)
        df = _rms_bwd(dx2 * gpost_ref[...], yf, rf).astype(BF16)
        df_ref[...] = df
        dh = jnp.zeros((FFN_TILE, D_MODEL), F32)

        def d_act(j):
            return _dot_nt(df, wdn_ref[pl.ds(j * FFN_BLOCK, FFN_BLOCK), :])

        ahead = d_act(0)
        for j in range(half):
            dact = ahead
            if j + 1 < half:
                ahead = d_act(j + 1)
            gate = gu_ref[j].astype(F32)
            up = gu_ref[j + half].astype(F32)
            sig = jax.nn.sigmoid(gate)
            silu = gate * sig
            act_ref[j] = (silu * up).astype(BF16)
            dgate = (dact * up * (sig * (1.0 + gate * (1.0 - sig)))).astype(BF16)
            dup = (dact * silu).astype(BF16)
            dgu_ref[j] = dgate
            dgu_ref[j + half] = dup
            dh = dh + _dot(dgate, wgu_ref[j]) + _dot(dup, wgu_ref[j + half])
        y1, r1 = _rms(x1_ref[...])
        h_ref[...] = (y1 * gpre_ref[...]).astype(BF16)
        dx1_ref[...] = dx2 + _rms_bwd(dh * gpre_ref[...], y1, r1)
        _accumulate([(dgpost_ref, dg_post), (dgpre_ref, _colsum(dh * y1))], i)

    act_spec = pl.BlockSpec((half, FFN_TILE, FFN_BLOCK), lambda i: (0, i, 0))
    return _call(
        body, (dx2, f, x1, gu, w_gu, w_dn, g_post, g_pre), grid=(SEQ // FFN_TILE,),
        in_specs=[_frows(D_MODEL)] * 3 + [_gu_spec(), _const1((N_DEV, FFN_BLOCK, D_MODEL)), _const1((FFN_HIDDEN, D_MODEL)),
                                          _const((1, D_MODEL)), _const((1, D_MODEL))],
        out_specs=[_frows(D_MODEL), _frows(D_MODEL), act_spec, _gu_spec(), _frows(D_MODEL), _const((1, D_MODEL)), _const((1, D_MODEL))],
        out_shape=[_sds((SEQ, D_MODEL), F32), _sds((SEQ, D_MODEL), BF16), _sds((half, SEQ, FFN_BLOCK), BF16),
                   _sds((N_DEV, SEQ, FFN_BLOCK), BF16), _sds((SEQ, D_MODEL), BF16), _sds((1, D_MODEL), F32), _sds((1, D_MODEL), F32)],
        vmem_mib=52, name="ffn_bwd", comm=comm)


def _wgrad(a_t, g, name):
    width = g.shape[1]

    def body(a_ref, g_ref, o_ref):
        o_ref[...] = _dot(a_ref[...], g_ref[...]).astype(BF16)

    return _call(body, (a_t, g), grid=(width // WGRAD_COLS,),
                 in_specs=[_const1((D_MODEL, SEQ)), pl.BlockSpec((SEQ, WGRAD_COLS), lambda j: (0, j))],
                 out_specs=[pl.BlockSpec((D_MODEL, WGRAD_COLS), lambda j: (0, j))], out_shape=[_sds((D_MODEL, width), BF16)],
                 vmem_mib=48, name=name)[0][0]


def _wgrad_paired(a, g, kind, to_owners=False, comm=None):
    shard = SHARD_SHAPES[kind]
    per_chip = 2 if kind == "gu" else 1
    n_compute = per_chip * N_CHIPS
    n_steps = n_compute + 1

    def chip_of(turn):
        if not to_owners:
            return turn
        return (2 * lax.axis_index("x") + lax.axis_index("y") + 1 + turn) % N_CHIPS

    def block(t):
        return jnp.minimum(t, n_compute - 1)

    if kind == "in":
        specs = [_const1((D_MODEL, SEQ)), pl.BlockSpec((SEQ, 2 * W_IN_BLOCK), lambda t: (0, chip_of(block(t))))]
    else:
        specs = [pl.BlockSpec((1, SEQ, FFN_BLOCK), lambda t: (block(t), 0, 0)), _const1((SEQ, D_MODEL))]

    def body(a_ref, g_ref, o_ref, *rest):
        if to_owners:
            landing, sendbuf, recvbuf, keepbuf, send_sem, recv_sem, donebuf, owner_send, owner_recv, own_sem = rest
        else:
            sendbuf, recvbuf, keepbuf, send_sem, recv_sem = rest
        t = pl.program_id(0)
        x, y, c = _place()

        def exchange(q):
            return pltpu.make_async_remote_copy(src_ref=sendbuf.at[q], dst_ref=recvbuf.at[q], send_sem=send_sem.at[q],
                                                recv_sem=recv_sem.at[q], device_id=(x, y, 1 - c), device_id_type=MESH)

        my_chip = 2 * x + y

        def to_owner(q):
            return pltpu.make_async_remote_copy(src_ref=donebuf.at[q], dst_ref=landing.at[my_chip], send_sem=owner_send.at[q],
                                                recv_sem=owner_recv.at[my_chip], device_id=(q // 2, q % 2, c), device_id_type=MESH)

        def keep_own():
            return pltpu.make_async_copy(donebuf.at[my_chip], landing.at[my_chip], own_sem)

        @pl.when((t >= per_chip) & (t % per_chip == 0))
        def _():
            q = chip_of(t // per_chip - 1)
            exchange(q).wait_recv()
            done = (keepbuf[...] + recvbuf[q].astype(F32)).astype(BF16)
            o_ref[0] = done
            if to_owners:
                donebuf[q] = done

                @pl.when(q == my_chip)
                def _():
                    keep_own().start()

                @pl.when(q != my_chip)
                def _():
                    to_owner(q).start()

        @pl.when(t < n_compute)
        def _():
            q = chip_of(t // per_chip)
            if kind == "gu":
                r = _dot_tn(a_ref[0], g_ref[...])

                @pl.when(t % 2 == c)
                def _():
                    keepbuf[...] = r

                @pl.when(t % 2 != c)
                def _():
                    sendbuf[q] = r.astype(BF16)
                    exchange(q).start()
            else:
                if kind == "dn":
                    r = _dot_tn(a_ref[0], g_ref[...])
                    lower, upper = r[:W_DOWN_BLOCK], r[W_DOWN_BLOCK:]
                else:
                    r = _dot(a_ref[...], g_ref[...])
                    lower, upper = r[:, :W_IN_BLOCK], r[:, W_IN_BLOCK:]
                keepbuf[...] = jnp.where(c == 0, lower, upper)
                sendbuf[q] = jnp.where(c == 0, upper, lower).astype(BF16)
                exchange(q).start()

        @pl.when(t == n_steps - 1)
        def _():
            for q in range(N_CHIPS):
                exchange(q).wait_send()
            if to_owners:
                keep_own().wait()
                for s in range(N_CHIPS):
                    @pl.when(s != my_chip)
                    def _(s=s):
                        pltpu.make_async_remote_copy(src_ref=donebuf.at[s], dst_ref=landing.at[s], send_sem=owner_send.at[s],
                                                     recv_sem=owner_recv.at[s], device_id=(x, y, c), device_id_type=MESH).wait()

    slots = (N_CHIPS,) + shard
    owners = int(to_owners)
    res, landed = _call(
        body, (a, g), grid=(n_steps,), in_specs=specs,
        out_specs=[pl.BlockSpec((1,) + shard, lambda t: (chip_of(jnp.clip(t // per_chip - 1, 0, N_CHIPS - 1)), 0, 0))] + [ANY] * owners,
        out_shape=[_sds(slots, BF16)] * (1 + owners),
        scratch_shapes=[pltpu.VMEM(slots, BF16), pltpu.VMEM(slots, BF16), pltpu.VMEM(shard, F32),
                        pltpu.SemaphoreType.DMA((N_CHIPS,)), pltpu.SemaphoreType.DMA((N_CHIPS,))]
        + [pltpu.VMEM(slots, BF16), pltpu.SemaphoreType.DMA((N_CHIPS,)), pltpu.SemaphoreType.DMA((N_CHIPS,)), pltpu.SemaphoreType.DMA] * owners,
        vmem_mib=52, name="wgrad_" + kind + "_paired" + "_to_owners" * owners, comm=comm)
    return res[owners] if comm is None else (res[owners], landed)


def _mix_bwd(dx1, mix, attn, u, b, c, conv_w, g_attn, g_conv, g_post, w_out):
    def body(dx1_ref, mix_ref, a_ref, u_ref, b_ref, c_ref, up_ref, cp_ref, un_ref, cn_ref, cw_ref, ga_ref, gc_ref, gp_ref, w_ref,
             dmix_ref, da_ref, ds_ref, db_ref, dgp_ref, dga_ref, dgc_ref):
        i = pl.program_id(0)
        dx1 = dx1_ref[...]
        ym, rm = _rms(mix_ref[...].astype(F32))
        dg_post = _colsum(dx1 * ym)
        dmix = _rms_bwd(dx1 * gp_ref[...], ym, rm).astype(BF16)
        dmix_ref[...] = dmix
        dmerged = _dot_nt(dmix, w_ref[...])
        dna, dnc = dmerged[:, :ATTN_WIDTH], dmerged[:, ATTN_WIDTH:]
        ya, ra = _rms(a_ref[...])
        da_ref[...] = _rms_bwd(dna * ga_ref[...], ya, ra)
        _, _, _, s = _conv_parts(u_ref[...].astype(F32), c_ref[...].astype(F32), up_ref, cp_ref, un_ref, cn_ref, cw_ref[...], i)
        gate_b = b_ref[...].astype(F32)
        yc, rc = _rms(gate_b * s)
        dy = _rms_bwd(dnc * gc_ref[...], yc, rc)
        db_ref[...] = (dy * s).astype(BF16)
        ds_ref[...] = (dy * gate_b).astype(BF16)
        _accumulate([(dgp_ref, dg_post), (dga_ref, _colsum(dna * ya)), (dgc_ref, _colsum(dnc * yc))], i)

    a = ATTN_WIDTH
    return _call(
        body, (dx1, mix, attn, u, b, c, u, c, u, c, conv_w, g_attn, g_conv, g_post, w_out), grid=(SEQ // ROW_TILE,),
        in_specs=[_rows(D_MODEL)] * 2 + [_rows(a)] * 4 + [_halo_prev(a)] * 2 + [_halo_next(a)] * 2
        + [_const((3, a)), _const((1, a)), _const((1, a)), _const((1, D_MODEL)), _const1((D_MODEL, D_MODEL))],
        out_specs=[_rows(D_MODEL)] + [_rows(a)] * 3 + [_const((1, D_MODEL)), _const((1, a)), _const((1, a))],
        out_shape=[_sds((SEQ, D_MODEL), BF16), _sds((SEQ, a), F32), _sds((SEQ, a), BF16), _sds((SEQ, a), BF16),
                   _sds((1, D_MODEL), F32), _sds((1, a), F32), _sds((1, a), F32)],
        vmem_mib=40, name="mix_bwd")[0]


def _attn_bwd(q, k, v, do, o, lse, comm=None):
    group = 8

    n_pairs = ATTN_WIDTH // LANES

    def body(q_hbm, k_hbm, v_hbm, do_hbm, o_hbm, lse_hbm, dq_ref, dk_ref, dv_ref,
             q_ref, k_ref, v_ref, do_ref, o_ref, lse_ref, in_sem,
             q32, k32, v32, qa, qb, doa, dob, kp, vp, lsep, dlp, dnat, dqp, dkp, dvp, bias_ref):
        pair = pl.program_id(0)
        sources = (q_hbm, k_hbm, v_hbm, do_hbm, o_hbm, lse_hbm)
        buffers = (q_ref, k_ref, v_ref, do_ref, o_ref, lse_ref)

        def fetch(i, p):
            return pltpu.make_async_copy(sources[i].at[:, pl.ds(pl.multiple_of(p * LANES, LANES), LANES)], buffers[i], in_sem.at[i])

        def prefetch(*which):
            @pl.when(pair + 1 < n_pairs)
            def _():
                for i in which:
                    fetch(i, pair + 1).start()

        @pl.when(pair == 0)
        def _():
            for i in range(len(sources)):
                fetch(i, 0).start()

        for i in range(len(sources)):
            fetch(i, pair).wait()

        first, mask_a, mask_b = _head_masks()
        _zero_key_padding((kp, vp))
        _write_band_bias(bias_ref)
        q32[...] = q_ref[...].astype(F32)
        k32[...] = k_ref[...].astype(F32)
        v32[...] = v_ref[...].astype(F32)
        prefetch(0, 1, 2)
        for c in range(SEQ // PERM_CHUNK):
            prod = do_ref[_chunk(c), :] * o_ref[_chunk(c), :]
            d_a = jnp.sum(prod * mask_a, axis=1, keepdims=True)
            d_b = jnp.sum(prod * mask_b, axis=1, keepdims=True)
            dnat[_chunk(c), :] = jnp.where(first, d_a, d_b)
        prefetch(4)

        for step, dil in enumerate(DILATIONS[1:] + DILATIONS[:1]):
            length = SEQ // dil
            assert length >= 2 * Q_BLOCK
            chunks = _perm_chunks(dil)
            for c, rows in chunks:
                val = q32[rows, :]
                qa[_chunk(c), :] = (val * mask_a).astype(BF16)
                qb[_chunk(c), :] = (val * mask_b).astype(BF16)
                val = do_ref[rows, :]
                doa[_chunk(c), :] = (val * mask_a).astype(BF16)
                dob[_chunk(c), :] = (val * mask_b).astype(BF16)
                kp[_chunk(c, HALF_WINDOW), :] = k32[rows, :].astype(BF16)
                vp[_chunk(c, HALF_WINDOW), :] = v32[rows, :].astype(BF16)
                lsep[_chunk(c), :] = lse_ref[rows, :]
                dlp[_chunk(c), :] = dnat[rows, :]
            if step == len(DILATIONS) - 1:
                prefetch(3, 5)
            zero = jnp.zeros((PERM_CHUNK, LANES), F32)
            for c in range(SEQ // PERM_CHUNK):
                dkp[_chunk(c), :] = zero
                dvp[_chunk(c), :] = zero
            dkp[pl.ds(SEQ, 2 * HALF_WINDOW), :] = zero[:2 * HALF_WINDOW]
            dvp[pl.ds(SEQ, 2 * HALF_WINDOW), :] = zero[:2 * HALF_WINDOW]

            heads = ((qa, doa, 0), (qb, dob, HEAD_DIM))

            def blocks(i, carry, length=length):
                base = pl.multiple_of(i * (group * Q_BLOCK), group * Q_BLOCK)
                starts = [base + g * Q_BLOCK for g in range(group)]
                raw = [[(_dot_nt(qx[pl.ds(m0, Q_BLOCK), :], kp[pl.ds(m0, K_WINDOW), :]),
                         _dot_nt(dox[pl.ds(m0, Q_BLOCK), :], vp[pl.ds(m0, K_WINDOW), :])) for qx, dox, _ in heads]
                       for m0 in starts]
                grads = []
                for m0, pair in zip(starts, raw):
                    bias = bias_ref[_band_bias_index(m0, length)]
                    lse_b, d_b = lsep[pl.ds(m0, Q_BLOCK), :], dlp[pl.ds(m0, Q_BLOCK), :]
                    out = []
                    for (s, dp), (_, _, col) in zip(pair, heads):
                        p = jnp.exp(s + bias - lse_b[:, col:col + 1])
                        out.append(((p * (dp - d_b[:, col:col + 1])).astype(BF16), p.astype(BF16)))
                    grads.append(out)
                for m0, out in zip(starts, grads):
                    qrows, krows = pl.ds(m0, Q_BLOCK), pl.ds(m0, K_WINDOW)
                    kw = kp[krows, :]
                    dk = jnp.zeros((K_WINDOW, LANES), F32)
                    dv = jnp.zeros((K_WINDOW, LANES), F32)
                    for (ds, p), (qx, dox, _) in zip(out, heads):
                        dk = dk + _dot_tn(ds, qx[qrows, :])
                        dv = dv + _dot_tn(p, dox[qrows, :])
                    dqp[qrows, :] = jnp.where(first, _dot(out[0][0], kw), _dot(out[1][0], kw)) * (HEAD_DIM ** -0.5)
                    dkp[krows, :] += dk
                    dvp[krows, :] += dv
                return carry

            lax.fori_loop(0, SEQ // (group * Q_BLOCK), blocks, 0)

            for c, rows in chunks:
                g_q, g_k, g_v = dqp[_chunk(c), :], dkp[_chunk(c, HALF_WINDOW), :], dvp[_chunk(c, HALF_WINDOW), :]
                if step == 0:
                    dq_ref[rows, :] = g_q
                    dk_ref[rows, :] = g_k
                    dv_ref[rows, :] = g_v
                else:
                    dq_ref[rows, :] = dq_ref[rows, :] + g_q
                    dk_ref[rows, :] = dk_ref[rows, :] + g_k
                    dv_ref[rows, :] = dv_ref[rows, :] + g_v

    col = pl.BlockSpec((SEQ, LANES), lambda h: (0, h))
    padded = SEQ + 2 * HALF_WINDOW
    return _call(
        body, (q, k, v, do, o, lse), grid=(n_pairs,), in_specs=[ANY] * 6, out_specs=[col] * 3,
        out_shape=[_sds((SEQ, ATTN_WIDTH), F32)] * 3,
        scratch_shapes=[pltpu.VMEM((SEQ, LANES), BF16)] * 3 + [pltpu.VMEM((SEQ, LANES), F32)] * 3 + [pltpu.SemaphoreType.DMA((6,))]
        + [pltpu.VMEM((SEQ, LANES), F32)] * 3 + [pltpu.VMEM((SEQ, LANES), BF16)] * 4
        + [pltpu.VMEM((padded, LANES), BF16)] * 2 + [pltpu.VMEM((SEQ, LANES), F32)] * 4 + [pltpu.VMEM((padded, LANES), F32)] * 2
        + [pltpu.VMEM((3, Q_BLOCK, K_WINDOW), F32)],
        vmem_mib=56, name="attn_bwd", comm=comm)


def _inproj_bwd(dq, dk, dv, ds, db, u, c, conv_w, tc, tsa, tsb, w_in, x, g_pre, dx1):
    def body(dq_ref, dk_ref, dv_ref, ds_ref, db_ref, u_ref, c_ref, dsp_ref, up_ref, cp_ref, dsn_ref, un_ref, cn_ref, cw_ref,
             tc_ref, tsa_ref, tsb_ref, w_ref, x_ref, g_ref, dx1_ref, dx_ref, dproj_ref, h_ref, dg_ref, dcw_ref):
        i = pl.program_id(0)
        cw = cw_ref[...]
        u, c = u_ref[...].astype(F32), c_ref[...].astype(F32)
        t, t_prev, t_next, _ = _conv_parts(u, c, up_ref, cp_ref, un_ref, cn_ref, cw, i)
        ds = ds_ref[...].astype(F32)
        ds_prev, ds_next = _shifted(ds, _last_row(dsp_ref), _first_row(dsn_ref), i)
        dt = cw[0:1, :] * ds_next + cw[1:2, :] * ds + cw[2:3, :] * ds_prev
        d_taps = jnp.concatenate([_colsum(ds * t_prev), _colsum(ds * t), _colsum(ds * t_next)], axis=0)
        tc_, tsa_, tsb_ = tc_ref[...], tsa_ref[...], tsb_ref[...]
        groups = ((2, lambda: dv_ref[...].astype(BF16)), (4, lambda: db_ref[...]),
                  (0, lambda: _rotate_transposed(dq_ref[...], tc_, tsa_, tsb_).astype(BF16)),
                  (1, lambda: _rotate_transposed(dk_ref[...], tc_, tsa_, tsb_).astype(BF16)),
                  (3, lambda: (dt * c).astype(BF16)), (5, lambda: (dt * u).astype(BF16)))
        dh = jnp.zeros((x_ref.shape[0], D_MODEL), F32)
        for n, make in groups:
            cols = pl.ds(n * a, a)
            part = make()
            dproj_ref[:, cols] = part
            dh = dh + _dot_nt(part, w_ref[:, cols])
        y, r = _rms(x_ref[...])
        h_ref[...] = (y * g_ref[...]).astype(BF16).T
        dx_ref[...] = dx1_ref[...] + _rms_bwd(dh * g_ref[...], y, r)
        _accumulate([(dg_ref, _colsum(dh * y)), (dcw_ref, d_taps)], i)

    a = ATTN_WIDTH
    tile = FFN_TILE

    def rows(width):
        return _rows(width, tile)

    return _call(
        body, (dq, dk, dv, ds, db, u, c, ds, u, c, ds, u, c, conv_w, tc, tsa, tsb, w_in, x, g_pre, dx1), grid=(SEQ // tile,),
        in_specs=[rows(a)] * 7 + [_halo_prev(a, tile)] * 3 + [_halo_next(a, tile)] * 3
        + [_const((3, a)), rows(LANES), rows(LANES), rows(LANES), _const1((D_MODEL, IN_PROJ_WIDTH)), rows(D_MODEL),
           _const((1, D_MODEL)), rows(D_MODEL)],
        out_specs=[rows(D_MODEL), rows(IN_PROJ_WIDTH), _cols(D_MODEL, tile), _const((1, D_MODEL)), _const((3, a))],
        out_shape=[_sds((SEQ, D_MODEL), F32), _sds((SEQ, IN_PROJ_WIDTH), BF16), _sds((D_MODEL, SEQ), BF16),
                   _sds((1, D_MODEL), F32), _sds((3, a), F32)],
        vmem_mib=48, name="inproj_bwd")


def _adamw_math(w, g, m, v):
    m = ADAM_B1 * m + (1.0 - ADAM_B1) * g
    v = ADAM_B2 * v + (1.0 - ADAM_B2) * (g * g)
    m_hat = m / (1.0 - ADAM_B1 ** ADAM_STEP)
    v_hat = v / (1.0 - ADAM_B2 ** ADAM_STEP)
    delta = -ADAM_LR * (m_hat / (jnp.sqrt(v_hat) + ADAM_EPS) + ADAM_WD * w)
    return delta, m, v


def _sum_parts(p_ref):
    g = p_ref[0].astype(F32)
    for k in range(1, p_ref.shape[0]):
        g = g + p_ref[k].astype(F32)
    return g


def _adamw_layers(parts, w, m, v, row_tile, name, comm=None):
    _, rows, cols = w.shape
    assert rows % row_tile == 0
    n_tiles = rows // row_tile

    def body(*refs):
        p_refs = refs[:DEPTH]
        w_ref, m_ref, v_ref, g_ref, d_ref, nm_ref, nv_ref = refs[DEPTH:]
        layer = pl.program_id(0)
        for l, p_ref in enumerate(p_refs):
            @pl.when(layer == l)
            def _(p_ref=p_ref):
                g = _sum_parts(p_ref)
                g_ref[0] = g
                d_ref[0], nm_ref[0], nv_ref[0] = _adamw_math(w_ref[0], g, m_ref[0], v_ref[0])

    def part_spec(l):
        return pl.BlockSpec((parts[l].shape[0], row_tile, cols),
                            lambda layer, i: (0, jnp.where(layer == l, i, jnp.where(layer < l, 0, n_tiles - 1)), 0))

    tile = pl.BlockSpec((1, row_tile, cols), lambda layer, i: (layer, i, 0))
    buffered = 2 * row_tile * cols * (sum(p.shape[0] for p in parts) * 2 + 7 * 4)
    return _call(body, (*parts, w, m, v), grid=(DEPTH, n_tiles), in_specs=[part_spec(l) for l in range(DEPTH)] + [tile] * 3,
                 out_specs=[tile] * 4, out_shape=[_sds(w.shape, F32)] * 4, vmem_mib=-(-buffered // MIB) + 8, name=name, comm=comm)


def _adamw_small(parts, w, m, v):
    n = len(GAIN_NAMES)
    a = ATTN_WIDTH
    where = ((0, 0), (1, 0), (2, 0), (3, 0), (4, 0), (4, a))

    def body(*refs):
        p_ref = refs[0]
        w_refs, m_refs, v_refs = refs[1:1 + n], refs[1 + n:1 + 2 * n], refs[1 + 2 * n:1 + 3 * n]
        outs = refs[1 + 3 * n:1 + 7 * n]
        taps_ref, loss_ref, sum_ref = refs[1 + 7 * n:]
        sum_ref[...] = _sum_parts(p_ref)
        for k, (row, col) in enumerate(where):
            width = w_refs[k].shape[1]
            g = jnp.concatenate([sum_ref[pl.ds(l * SMALL_ROWS + row, 1), pl.ds(col, width)] for l in range(DEPTH)], axis=0)
            g_ref, d_ref, nm_ref, nv_ref = outs[4 * k:4 * k + 4]
            g_ref[...] = g
            d_ref[...], nm_ref[...], nv_ref[...] = _adamw_math(w_refs[k][...], g, m_refs[k][...], v_refs[k][...])
        for l in range(DEPTH):
            for tap, (row, col) in enumerate(((5, 0), (5, a), (6, 0))):
                taps_ref[l, tap:tap + 1, :] = sum_ref[l * SMALL_ROWS + row:l * SMALL_ROWS + row + 1, col:col + a]
        loss_ref[...] = sum_ref[SMALL_ROWS - 1:SMALL_ROWS, 0:1]

    out_shape = [_sds(x.shape, F32) for x in w for _ in range(4)] + [_sds((DEPTH, 3, a), F32), _sds((1, 1), F32)]
    res = pl.pallas_call(body, out_shape=out_shape, scratch_shapes=[pltpu.VMEM(parts.shape[1:], F32)],
                         name="adamw_small")(parts, *w, *m, *v)
    return [res[4 * k:4 * k + 4] for k in range(n)], res[4 * n], res[4 * n + 1]


def _adamw_plain(g, w, m, v, name):
    def body(g_ref, w_ref, m_ref, v_ref, d_ref, nm_ref, nv_ref):
        d_ref[...], nm_ref[...], nv_ref[...] = _adamw_math(w_ref[...], g_ref[...], m_ref[...], v_ref[...])

    return pl.pallas_call(body, out_shape=[_sds(w.shape, F32)] * 3, name=name)(g, w, m, v)


SMALL_ROWS = 8
GAIN_NAMES = ("pre_mix_norm", "post_mix_norm", "pre_ffn_norm", "post_ffn_norm", "attn_out_norm", "conv_out_norm")


def _pack_small(pre_mix, post_mix, pre_ffn, post_ffn, attn_out, conv_out, taps):
    zeros = jnp.zeros((1, ATTN_WIDTH), F32)
    return jnp.concatenate([
        pre_mix, post_mix, pre_ffn, post_ffn, jnp.concatenate([attn_out, conv_out], axis=1),
        jnp.concatenate([taps[0:1], taps[1:2]], axis=1), jnp.concatenate([taps[2:3], zeros], axis=1),
        jnp.zeros((1, D_MODEL), F32)], axis=0)


def _rope(positions, casts, comm=None):
    inv_freq = ROPE_THETA ** (-jnp.arange(0, ROPE_DIM, 2, dtype=F32) / ROPE_DIM)
    per_head = jnp.concatenate([inv_freq, inv_freq, jnp.zeros((HEAD_DIM - ROPE_DIM,), F32)])
    freq_row = jnp.tile(per_head, LANES // HEAD_DIM).reshape(1, LANES)
    return _rope_tables(positions.reshape(SEQ, 1), freq_row, casts, comm)


def _layer_forward(h, gains, taps, tables, w, inproj_comm=None, attn_comm=None, mix_comm=None, ffn_comm=None, target=None):
    def arrived(comm, landed):
        return w if comm is None else {**w, **dict(zip(comm.kinds, landed))}

    (q, k, v, u, b, c), landed = _inproj_fwd(h, gains["pre_mix_norm"], w["in"], *tables, comm=inproj_comm)
    w = arrived(inproj_comm, landed)
    (attn, lse), landed = _attn_fwd(q, k, v, comm=attn_comm)
    w = arrived(attn_comm, landed)
    (x1, merged, mix), landed = _mix_fwd(attn, u, b, c, taps, gains["attn_out_norm"], gains["conv_out_norm"], w["out"], h,
                                         gains["post_mix_norm"], comm=mix_comm)
    w = arrived(mix_comm, landed)
    (x2, gu, f, *loss), landed_next = _ffn_fwd(x1, gains["pre_ffn_norm"], w["gu"], w["dn"], gains["post_ffn_norm"], comm=ffn_comm,
                                               target=target)
    out = x2 if target is None else (x2, loss[0])
    return out, (h, q, k, v, u, b, c, attn, lse, merged, mix, x1, gu, f), w, landed_next


class _GradientExchange:
    def ffn_grads(self, act, df, h2, dgu):
        return dict(dn=_wgrad_paired(act, df, "dn"), gu=_wgrad_paired(dgu, h2, "gu"))

    def w_out_grad(self, merged_t, dmix):
        return _wgrad(merged_t, dmix, "wgrad_out")

    def attention_passenger(self, ffn, g_out):
        return _Together(_ScatterChips(("gu", "dn"), (ffn["gu"], ffn["dn"])), _Scatter(("out",), (g_out,)))

    def w_in_grad(self, h_t, dproj, small_rows):
        if small_rows is None:
            return _wgrad(h_t, dproj, "wgrad_in"), None
        parts, landed = _wgrad_paired(h_t, dproj, "in", to_owners=True, comm=_GatherSmall(small_rows, "small"))
        return parts, landed[0]


def _layer_backward(dx, saved, gains, taps, tables, w, ffn_comm, exchange, last):
    x0, q, k, v, u, b, c, attn, lse, merged, mix, x1, gu, f = saved
    (dx1, df, act, dgu, h2, dg_post_ffn, dg_pre_ffn), landed_prev = _ffn_bwd(
        dx, f, x1, gu, w["gu"], w["dn"], gains["post_ffn_norm"], gains["pre_ffn_norm"], comm=ffn_comm)
    ffn = exchange.ffn_grads(act, df, h2, dgu)
    dmix, dattn, ds, db, dg_post_mix, dg_attn, dg_conv = _mix_bwd(
        dx1, mix, attn, u, b, c, taps, gains["attn_out_norm"], gains["conv_out_norm"], gains["post_mix_norm"], w["out"])
    g_out = exchange.w_out_grad(merged, dmix)
    attn_comm = exchange.attention_passenger(ffn, g_out)
    (dq, dk, dv), landed = _attn_bwd(q, k, v, dattn, attn, lse, comm=attn_comm)
    dx0, dproj, h1, dg_pre_mix, dtaps = _inproj_bwd(
        dq, dk, dv, ds, db, u, c, taps, *tables, w["in"], x0, gains["pre_mix_norm"], dx1)[0]
    small = _pack_small(dg_pre_mix, dg_post_mix, dg_pre_ffn, dg_post_ffn, dg_attn, dg_conv, dtaps)
    small_rows = None
    if last is not None:
        small_above, local_loss = last
        small_rows = lax.dynamic_update_slice(jnp.concatenate([small, small_above], axis=0), local_loss, (SMALL_ROWS - 1, 0))
    g_in, gathered = exchange.w_in_grad(h1, dproj, small_rows)
    rest = dict(zip(attn_comm.kinds, landed)) if attn_comm is not None else dict(ffn, out=g_out)
    return dx0, g_in, small, landed_prev, rest, gathered


def kernel(x, positions, pre_mix_norm, w_in, conv_w, attn_out_norm, conv_out_norm, w_out, post_mix_norm, pre_ffn_norm, w_gate_up, w_down, post_ffn_norm, loss_target, m_pre_mix_norm, m_w_in, m_conv_w, m_attn_out_norm, m_conv_out_norm, m_w_out, m_post_mix_norm, m_pre_ffn_norm, m_w_gate_up, m_w_down, m_post_ffn_norm, v_pre_mix_norm, v_w_in, v_conv_w, v_attn_out_norm, v_conv_out_norm, v_w_out, v_post_mix_norm, v_pre_ffn_norm, v_w_gate_up, v_w_down, v_post_ffn_norm):
    mx, my, mc = _place()
    me = _block_of(mx, my, mc)
    conv_channels = conv_w.shape[-1]

    def hidden_major(a):
        return jnp.swapaxes(a, 1, 2)

    big_w = dict(zip(WEIGHT_KINDS, (w_in, w_out, hidden_major(w_gate_up), w_down)))
    big_m = dict(zip(WEIGHT_KINDS, (m_w_in, m_w_out, hidden_major(m_w_gate_up), m_w_down)))
    big_v = dict(zip(WEIGHT_KINDS, (v_w_in, v_w_out, hidden_major(v_w_gate_up), v_w_down)))
    early = ("in", "out", "dn")
    shards = {early[0]: big_w[early[0]].astype(BF16)}
    all_gains = dict(pre_mix_norm=pre_mix_norm, attn_out_norm=attn_out_norm, conv_out_norm=conv_out_norm, post_mix_norm=post_mix_norm,
                     pre_ffn_norm=pre_ffn_norm, post_ffn_norm=post_ffn_norm)

    def gather(kinds, l):
        return _Gather(kinds, [shards[kind] for kind in kinds], l)

    def gains(l):
        return {name: g[l:l + 1] for name, g in all_gains.items()}

    taps_flat = jnp.pad(conv_w.reshape(-1), (0, 8 * LANES - conv_w.size)).reshape(8, LANES)
    later = [kind for kind in WEIGHT_KINDS if kind != early[0]]
    made, (first, taps_all) = _rope(positions, [big_w[kind] for kind in later],
                                    _Together(gather(early[:1], 0), _GatherSmall(taps_flat, "taps")))
    tables = made[:3]
    shards.update(zip(later, made[3:]))
    taps_all = taps_all.reshape(N_DEV, 8 * LANES)[:, :conv_w.size]
    conv_w_full = taps_all.reshape(N_DEV, DEPTH, 3, conv_channels).transpose(1, 2, 0, 3).reshape(DEPTH, 3, ATTN_WIDTH)
    weights = [dict(zip(early[:1], [first]))] + [None] * (DEPTH - 1)
    saved = [None] * DEPTH
    h = x[0]
    for l in range(DEPTH):
        inproj_comm = gather(("out",), 0) if l == 0 else None
        mix_comm = gather(("dn",), 0) if l == 0 else None
        ffn_comm = gather(early, l + 1) if l + 1 < DEPTH else None
        target = loss_target[0] if l + 1 == DEPTH else None
        h, saved[l], weights[l], landed = _layer_forward(h, gains(l), conv_w_full[l], tables, weights[l], inproj_comm,
                                                         gather(("gu",), l), mix_comm, ffn_comm, target)
        if ffn_comm is not None:
            weights[l + 1] = dict(zip(ffn_comm.kinds, landed))
    dx, local_loss = h

    parts = {kind: [None] * DEPTH for kind in WEIGHT_KINDS}
    small_grads = [None] * DEPTH
    g_in_above = None
    exchange = _GradientExchange()
    for l in reversed(range(DEPTH)):
        ffn_comm = _Scatter(("in",), (g_in_above,)) if g_in_above is not None else None
        last = (jnp.concatenate(small_grads[1:], axis=0), local_loss) if l == 0 else None
        dx, g_in_above, small_grads[l], landed, rest, gathered = _layer_backward(
            dx, saved[l], gains(l), conv_w_full[l], tables, weights[l], ffn_comm, exchange, last)
        if ffn_comm is not None:
            parts["in"][l + 1] = landed[0]
        for kind, part in rest.items():
            parts[kind][l] = part

    parts["in"][0] = g_in_above
    tiles = {"in": 512, "out": 128, "gu": 352, "dn": 176}
    big = {kind: _adamw_layers(parts[kind], big_w[kind], big_m[kind], big_v[kind], tiles[kind], "adamw_" + kind)[0]
           for kind in WEIGHT_KINDS}

    small_m = (m_pre_mix_norm, m_post_mix_norm, m_pre_ffn_norm, m_post_ffn_norm, m_attn_out_norm, m_conv_out_norm)
    small_v = (v_pre_mix_norm, v_post_mix_norm, v_pre_ffn_norm, v_post_ffn_norm, v_attn_out_norm, v_conv_out_norm)
    per_gain, taps_grad_full, loss = _adamw_small(gathered, [all_gains[name] for name in GAIN_NAMES], small_m, small_v)
    small = [{name: per_gain[k][o] for k, name in enumerate(GAIN_NAMES)} for o in range(4)]
    loss = loss[0, 0]
    taps_grad = lax.dynamic_slice_in_dim(taps_grad_full, me * conv_channels, conv_channels, axis=2)

    taps = [taps_grad, *_adamw_plain(taps_grad, conv_w, m_conv_w, v_conv_w, "adamw_taps")]

    def leaves(o):
        s = small[o]
        return (s["pre_mix_norm"], big["in"][o], taps[o], s["attn_out_norm"], s["conv_out_norm"], big["out"][o], s["post_mix_norm"],
                s["pre_ffn_norm"], hidden_major(big["gu"][o]), big["dn"][o], s["post_ffn_norm"])

    return (loss, dx[None], *leaves(0), *leaves(1), *leaves(2), *leaves(3))
```

```python
import math

import jax
import jax.numpy as jnp
from jax import lax
from jax.experimental import pallas as pl
from jax.experimental.pallas import tpu as pltpu

F32 = jnp.float32
BF16 = jnp.bfloat16
MESH = pl.DeviceIdType.MESH

SEQ = 4096
D_MODEL = 1024
DEPTH = 4
N_DEV = 8
ATTN_WIDTH = 512
IN_PROJ_WIDTH = 3072
FFN_HIDDEN = 2816
FFN_BLOCK = 2 * FFN_HIDDEN // N_DEV
W_IN_BLOCK = IN_PROJ_WIDTH // N_DEV
W_OUT_BLOCK = D_MODEL // N_DEV
W_DOWN_BLOCK = FFN_HIDDEN // N_DEV
HEAD_DIM = 64
ROPE_DIM = 16
ROPE_THETA = 500000.0
DILATIONS = (1, 4, 16)
HALF_WINDOW = 64
RMS_EPS = 1e-6
NEG_INF = -1e30
LANES = 128
Q_BLOCK = 128
K_WINDOW = Q_BLOCK + 2 * HALF_WINDOW
PERM_CHUNK = 256
ROW_TILE = 512
FFN_TILE = 256
WGRAD_COLS = 512
ADAM_LR, ADAM_B1, ADAM_B2, ADAM_EPS, ADAM_WD, ADAM_STEP = 0.001, 0.9, 0.999, 1e-08, 0.01, 10
MIB = 1024 * 1024
PINNED_BYTES = 256 * 1024

WEIGHT_KINDS = ("in", "out", "gu", "dn")
FULL_SHAPES = {"in": (D_MODEL, IN_PROJ_WIDTH), "out": (D_MODEL, D_MODEL), "gu": (N_DEV, FFN_BLOCK, D_MODEL), "dn": (FFN_HIDDEN, D_MODEL)}
SHARD_SHAPES = {"in": (D_MODEL, W_IN_BLOCK), "out": (W_OUT_BLOCK, D_MODEL), "gu": (FFN_BLOCK, D_MODEL), "dn": (W_DOWN_BLOCK, D_MODEL)}
ANY = pl.BlockSpec(memory_space=pl.ANY)


def _sds(shape, dtype):
    return jax.ShapeDtypeStruct(shape, dtype)


def _rows(width, tile=ROW_TILE):
    return pl.BlockSpec((tile, width), lambda i: (i, 0))


def _frows(width):
    return _rows(width, FFN_TILE)


def _cols(height, tile):
    return pl.BlockSpec((height, tile), lambda i: (0, i))


def _const(shape):
    return pl.BlockSpec(shape, lambda i: (0,) * len(shape))


def _const1(shape):
    return pl.BlockSpec(shape, lambda i: (0,) * len(shape), pipeline_mode=pl.Buffered(1))


HALO_ROWS = 16


def _halo_prev(width, tile=ROW_TILE):
    return pl.BlockSpec((HALO_ROWS, width), lambda i: (jnp.maximum(i * (tile // HALO_ROWS) - 1, 0), 0))


def _halo_next(width, tile=ROW_TILE):
    return pl.BlockSpec((HALO_ROWS, width), lambda i: (jnp.minimum((i + 1) * (tile // HALO_ROWS), SEQ // HALO_ROWS - 1), 0))


def _rms(x):
    r = lax.rsqrt(jnp.mean(x * x, axis=-1, keepdims=True) + RMS_EPS)
    return x * r, r


def _rms_bwd(dn, y, r):
    return r * (dn - y * jnp.mean(dn * y, axis=-1, keepdims=True))


def _dot(a, b):
    return jnp.dot(a, b, preferred_element_type=F32)


def _dot_nt(a, b):
    return lax.dot_general(a, b, (((1,), (1,)), ((), ())), preferred_element_type=F32)


def _dot_tn(a, b):
    return lax.dot_general(a, b, (((0,), (0,)), ((), ())), preferred_element_type=F32)


def _place():
    return lax.axis_index("x"), lax.axis_index("y"), lax.axis_index("c")


def _block_of(px, py, pc):
    return 4 * px + 2 * py + pc


def _weight_block(ref, kind, blk):
    if kind == "in":
        return ref.at[:, pl.ds(blk * W_IN_BLOCK, W_IN_BLOCK)]
    if kind == "out":
        return ref.at[pl.ds(blk * W_OUT_BLOCK, W_OUT_BLOCK), :]
    if kind == "gu":
        return ref.at[blk]
    return ref.at[pl.ds(blk * W_DOWN_BLOCK, W_DOWN_BLOCK), :]


def _dma_semaphores(n):
    return [pltpu.SemaphoreType.DMA((n, 7)), pltpu.SemaphoreType.DMA((n, 7)), pltpu.SemaphoreType.DMA((n,))]


class _AsOne:
    def __init__(self, copies):
        self.copies = copies

    def start(self):
        for cp in self.copies:
            cp.start()

    def wait_recv(self):
        for cp in self.copies:
            cp.wait_recv()

    def wait_send(self):
        for cp in self.copies:
            cp.wait_send()


class _Gather:
    def __init__(self, kinds, shards, layer):
        self.kinds, self.operands, self.layer = tuple(kinds), list(shards), layer
        self.tag = "gather_" + "_".join(kinds)
        self.out_shape = [_sds(FULL_SHAPES[k], BF16) for k in kinds]
        n = len(kinds)
        self.scratch = [pltpu.SemaphoreType.DMA((n, 14)), pltpu.SemaphoreType.DMA((n, 14)), pltpu.SemaphoreType.DMA((n,))]

    def _parties(self):
        x, y, c = _place()
        return (x, y, c), (x, y, 1 - c), [(1 - x, y), (x, 1 - y), (1 - x, 1 - y)], c

    def _copy(self, outs, sems, w, k, block, to, src=None):
        dst = _weight_block(outs[w], self.kinds[w], _block_of(*block))
        src = dst if src is None else src
        half = SHARD_SHAPES[self.kinds[w]][0] // 2
        return _AsOne([pltpu.make_async_remote_copy(
            src_ref=src.at[pl.ds(h * half, half)], dst_ref=dst.at[pl.ds(h * half, half)], send_sem=sems[0].at[w, 2 * k + h],
            recv_sem=sems[1].at[w, 2 * k + h], device_id=to, device_id_type=MESH) for h in range(2)])

    def _own(self, srcs, outs, sems, w, me):
        return pltpu.make_async_copy(srcs[w].at[self.layer], _weight_block(outs[w], self.kinds[w], _block_of(*me)), sems[2].at[w])

    def _first(self, srcs, outs, sems, w):
        me, sibling, chips, c = self._parties()
        shard = srcs[w].at[self.layer]
        return [self._copy(outs, sems, w, 0, me, sibling, src=shard)] + [
            self._copy(outs, sems, w, 1 + j, me, (*chip, c), src=shard) for j, chip in enumerate(chips)]

    def start(self, srcs, outs, sems):
        me = self._parties()[0]
        for w in range(len(self.kinds)):
            self._own(srcs, outs, sems, w, me).start()
            for cp in self._first(srcs, outs, sems, w):
                cp.start()

    def forward(self, srcs, outs, sems):
        me, sibling, chips, c = self._parties()
        for j, chip in enumerate(chips):
            for w in range(len(self.kinds)):
                self._copy(outs, sems, w, 1 + j, (*chip, c), me).wait_recv()
                self._copy(outs, sems, w, 4 + j, (*chip, c), sibling).start()

    def finish(self, srcs, outs, sems):
        me, sibling, chips, c = self._parties()
        for w in range(len(self.kinds)):
            self._copy(outs, sems, w, 0, sibling, me).wait_recv()
            for j, chip in enumerate(chips):
                self._copy(outs, sems, w, 4 + j, (*chip, 1 - c), me).wait_recv()
        for w in range(len(self.kinds)):
            for cp in self._first(srcs, outs, sems, w):
                cp.wait_send()
            for j, chip in enumerate(chips):
                self._copy(outs, sems, w, 4 + j, (*chip, c), sibling).wait_send()
            self._own(srcs, outs, sems, w, me).wait()


def _peers(x, y, c):
    return [(x ^ a, y ^ b, c ^ e) for a in (0, 1) for b in (0, 1) for e in (0, 1) if (a, b, e) != (0, 0, 0)]


class _Scatter:
    def __init__(self, kinds, grads):
        self.kinds, self.operands = tuple(kinds), list(grads)
        self.tag = "scatter_" + "_".join(kinds)
        self.out_shape = [_sds((N_DEV,) + SHARD_SHAPES[k], BF16) for k in kinds]
        self.scratch = _dma_semaphores(len(kinds))

    def _copies(self, srcs, outs, sems):
        x, y, c = _place()
        me = _block_of(x, y, c)
        copies = []
        for w, kind in enumerate(self.kinds):
            copies.append(pltpu.make_async_copy(_weight_block(srcs[w], kind, me), outs[w].at[me], sems[2].at[w]))
            for k, peer in enumerate(_peers(x, y, c)):
                copies.append(pltpu.make_async_remote_copy(
                    src_ref=_weight_block(srcs[w], kind, _block_of(*peer)), dst_ref=outs[w].at[me],
                    send_sem=sems[0].at[w, k], recv_sem=sems[1].at[w, k], device_id=peer, device_id_type=MESH))
        return copies

    def start(self, srcs, outs, sems):
        for cp in self._copies(srcs, outs, sems):
            cp.start()

    def forward(self, srcs, outs, sems):
        pass

    def finish(self, srcs, outs, sems):
        for cp in self._copies(srcs, outs, sems):
            cp.wait()


N_CHIPS = N_DEV // 2


class _ScatterChips:
    def __init__(self, kinds, blocks):
        self.kinds, self.operands = tuple(kinds), list(blocks)
        self.tag = "scatter_chips_" + "_".join(kinds)
        self.out_shape = [_sds((N_CHIPS,) + SHARD_SHAPES[k], BF16) for k in kinds]
        self.scratch = _dma_semaphores(len(kinds))

    def _copies(self, srcs, outs, sems):
        x, y, c = _place()
        mine = 2 * x + y
        copies = []
        for w in range(len(self.kinds)):
            copies.append(pltpu.make_async_copy(srcs[w].at[mine], outs[w].at[mine], sems[2].at[w]))
            for j, (px, py) in enumerate([(1 - x, y), (x, 1 - y), (1 - x, 1 - y)]):
                copies.append(pltpu.make_async_remote_copy(
                    src_ref=srcs[w].at[2 * px + py], dst_ref=outs[w].at[mine], send_sem=sems[0].at[w, j], recv_sem=sems[1].at[w, j],
                    device_id=(px, py, c), device_id_type=MESH))
        return copies

    def start(self, srcs, outs, sems):
        for cp in self._copies(srcs, outs, sems):
            cp.start()

    def forward(self, srcs, outs, sems):
        pass

    def finish(self, srcs, outs, sems):
        for cp in self._copies(srcs, outs, sems):
            cp.wait()


def _in_hbm(a):
    return pltpu.with_memory_space_constraint(a, pltpu.HBM) if a.size * a.dtype.itemsize >= PINNED_BYTES else a


def _out_hbm(s):
    return pltpu.HBM(s.shape, s.dtype) if math.prod(s.shape) * jnp.dtype(s.dtype).itemsize >= PINNED_BYTES else s


def _call(body, args, *, grid, in_specs, out_specs, out_shape, scratch_shapes=(), vmem_mib=None, name, comm=None):
    kwargs = {} if vmem_mib is None else dict(compiler_params=pltpu.CompilerParams(vmem_limit_bytes=vmem_mib * MIB))
    in_specs, out_specs, out_shape, scratch_shapes = list(in_specs), list(out_specs), list(out_shape), list(scratch_shapes)
    args = [_in_hbm(a) for a in args]
    out_shape = [_out_hbm(s) for s in out_shape]
    if comm is None:
        res = pl.pallas_call(body, grid=grid, in_specs=in_specs, out_specs=out_specs, out_shape=out_shape,
                             scratch_shapes=scratch_shapes, name=name, **kwargs)(*args)
        return list(res), None
    n_in, n_out, n_scr = len(in_specs), len(out_specs), len(scratch_shapes)
    c_in, c_out = len(comm.operands), len(comm.out_shape)
    last = math.prod(grid) - 1

    def carried(*refs):
        cuts = [n_in, c_in, n_out, c_out, n_scr]
        parts, at = [], 0
        for n in cuts:
            parts.append(refs[at:at + n])
            at += n
        ins, c_ins, outs, c_outs, scr = parts
        sems = refs[at:]
        step = pl.program_id(0)
        for axis in range(1, len(grid)):
            step = step * grid[axis] + pl.program_id(axis)

        @pl.when(step == 0)
        def _():
            comm.start(c_ins, c_outs, sems)

        @pl.when(step == last)
        def _():
            comm.forward(c_ins, c_outs, sems)

        body(*ins, *outs, *scr)

        @pl.when(step == last)
        def _():
            comm.finish(c_ins, c_outs, sems)

    res = pl.pallas_call(carried, grid=grid, in_specs=in_specs + [ANY] * c_in, out_specs=out_specs + [ANY] * c_out,
                         out_shape=out_shape + [_out_hbm(s) for s in comm.out_shape], scratch_shapes=scratch_shapes + comm.scratch,
                         name=name + "_" + comm.tag, **kwargs)(*args, *[_in_hbm(a) for a in comm.operands])
    return list(res[:n_out]), list(res[n_out:])


class _GatherSmall:
    def __init__(self, v, tag):
        self.operands, self.tag = [v], tag
        self.out_shape = [_sds((N_DEV,) + v.shape, F32)]
        self.scratch = _dma_semaphores(1)

    def _copies(self, srcs, outs, sems):
        x, y, c = _place()
        me = _block_of(x, y, c)
        copies = [pltpu.make_async_copy(srcs[0], outs[0].at[me], sems[2].at[0])]
        for k, peer in enumerate(_peers(x, y, c)):
            copies.append(pltpu.make_async_remote_copy(src_ref=srcs[0], dst_ref=outs[0].at[me], send_sem=sems[0].at[0, k],
                                                       recv_sem=sems[1].at[0, k], device_id=peer, device_id_type=MESH))
        return copies

    def start(self, srcs, outs, sems):
        for cp in self._copies(srcs, outs, sems):
            cp.start()

    def forward(self, srcs, outs, sems):
        pass

    def finish(self, srcs, outs, sems):
        for cp in self._copies(srcs, outs, sems):
            cp.wait()


class _Together:
    def __init__(self, *comms):
        self.comms = comms
        self.operands = [a for cm in comms for a in cm.operands]
        self.out_shape = [s for cm in comms for s in cm.out_shape]
        self.scratch = [s for cm in comms for s in cm.scratch]
        self.tag = "_".join(cm.tag for cm in comms)
        self.kinds = tuple(kind for cm in comms for kind in getattr(cm, "kinds", ()))

    def _each(self, srcs, outs, sems):
        a = o = s = 0
        for cm in self.comms:
            na, no, ns = len(cm.operands), len(cm.out_shape), len(cm.scratch)
            yield cm, srcs[a:a + na], outs[o:o + no], sems[s:s + ns]
            a, o, s = a + na, o + no, s + ns

    def start(self, srcs, outs, sems):
        for cm, *refs in self._each(srcs, outs, sems):
            cm.start(*refs)

    def forward(self, srcs, outs, sems):
        for cm, *refs in self._each(srcs, outs, sems):
            cm.forward(*refs)

    def finish(self, srcs, outs, sems):
        for cm, *refs in self._each(srcs, outs, sems):
            cm.finish(*refs)


def _rope_tables(pos_col, freq_row, casts, comm=None):
    steps = SEQ // ROW_TILE
    assert steps == 2 * DEPTH

    def body(p_ref, f_ref, *refs):
        wide, (c_ref, sa_ref, sb_ref), narrow = refs[:len(casts)], refs[len(casts):len(casts) + 3], refs[len(casts) + 3:]
        ang = p_ref[...].astype(F32) * f_ref[...]
        lane = lax.broadcasted_iota(jnp.int32, ang.shape, 1) % HEAD_DIM
        cos, sin = jnp.cos(ang), jnp.sin(ang)
        c_ref[...] = jnp.where(lane < ROPE_DIM, cos, 1.0)
        sa_ref[...] = jnp.where(lane < ROPE_DIM // 2, -sin, 0.0)
        sb_ref[...] = jnp.where((lane >= ROPE_DIM // 2) & (lane < ROPE_DIM), sin, 0.0)
        for src, dst in zip(wide, narrow):
            dst[...] = src[...].astype(BF16)

    halves = [pl.BlockSpec((1, a.shape[1] // 2, a.shape[2]), lambda i: (i // 2, i % 2, 0)) for a in casts]
    return _call(body, (pos_col, freq_row, *casts), grid=(steps,), in_specs=[_rows(1), _const((1, LANES))] + halves,
                 out_specs=[_rows(LANES)] * 3 + halves,
                 out_shape=[_sds((SEQ, LANES), F32)] * 3 + [_sds(a.shape, BF16) for a in casts], name="rope_tables", comm=comm)


def _rotate(t, c, sa, sb):
    parts = []
    for g in range(ATTN_WIDTH // LANES):
        tg = t[:, g * LANES:(g + 1) * LANES]
        parts.append(tg * c + pltpu.roll(tg, LANES - 8, axis=1) * sa + pltpu.roll(tg, 8, axis=1) * sb)
    return jnp.concatenate(parts, axis=1)


def _rotate_transposed(dt, c, sa, sb):
    parts = []
    for g in range(ATTN_WIDTH // LANES):
        dg = dt[:, g * LANES:(g + 1) * LANES]
        parts.append(dg * c + pltpu.roll(dg * sa, 8, axis=1) + pltpu.roll(dg * sb, LANES - 8, axis=1))
    return jnp.concatenate(parts, axis=1)


def _inproj_fwd(x, g_pre, w_in, tc, tsa, tsb, comm=None):
    def body(x_ref, g_ref, w_ref, c_ref, sa_ref, sb_ref, q_ref, k_ref, v_ref, u_ref, b_ref, cc_ref):
        y, _ = _rms(x_ref[...])
        h = (y * g_ref[...]).astype(BF16)

        def proj(n):
            return _dot(h, w_ref[:, n * ATTN_WIDTH:(n + 1) * ATTN_WIDTH])

        c, sa, sb = c_ref[...], sa_ref[...], sb_ref[...]
        q_ref[...] = (_rotate(proj(0), c, sa, sb) * (HEAD_DIM ** -0.5)).astype(BF16)
        k_ref[...] = _rotate(proj(1), c, sa, sb).astype(BF16)
        v_ref[...] = proj(2).astype(BF16)
        u_ref[...] = proj(3).astype(BF16)
        b_ref[...] = proj(4).astype(BF16)
        cc_ref[...] = proj(5).astype(BF16)

    a = ATTN_WIDTH
    return _call(
        body, (x, g_pre, w_in, tc, tsa, tsb), grid=(SEQ // ROW_TILE,),
        in_specs=[_rows(D_MODEL), _const((1, D_MODEL)), _const1((D_MODEL, IN_PROJ_WIDTH)), _rows(LANES), _rows(LANES), _rows(LANES)],
        out_specs=[_rows(a)] * 6, out_shape=[_sds((SEQ, a), BF16)] * 6,
        vmem_mib=40, name="inproj_fwd", comm=comm)


def _head_masks():
    lane = lax.broadcasted_iota(jnp.int32, (1, LANES), 1)
    first = lane < HEAD_DIM
    return first, first.astype(F32), 1.0 - first.astype(F32)


def _perm_chunks(dil):
    length = SEQ // dil
    out = []
    for r in range(dil):
        for c0 in range(0, length, PERM_CHUNK):
            chunk = (r * length + c0) // PERM_CHUNK
            rows = pl.ds(c0, PERM_CHUNK) if dil == 1 else pl.ds(r + dil * c0, PERM_CHUNK, stride=dil)
            out.append((chunk, rows))
    return out


def _chunk(c, offset=0):
    return pl.ds(offset + c * PERM_CHUNK, PERM_CHUNK)


def _write_band_bias(bias_ref):
    rr = lax.broadcasted_iota(jnp.int32, (Q_BLOCK, K_WINDOW), 0)
    cc = lax.broadcasted_iota(jnp.int32, (Q_BLOCK, K_WINDOW), 1)
    band = (cc >= rr) & (cc - rr <= 2 * HALF_WINDOW)
    bias_ref[0] = jnp.where(band, 0.0, NEG_INF)
    bias_ref[1] = jnp.where(band & (cc >= HALF_WINDOW), 0.0, NEG_INF)
    bias_ref[2] = jnp.where(band & (cc < Q_BLOCK + HALF_WINDOW), 0.0, NEG_INF)


def _band_bias_index(m0, length):
    return jnp.where(m0 % length == 0, 1, 0) + jnp.where((m0 + Q_BLOCK) % length == 0, 2, 0)


def _zero_key_padding(bufs):
    pad = jnp.zeros((HALF_WINDOW, LANES), BF16)
    for buf in bufs:
        buf[pl.ds(0, HALF_WINDOW), :] = pad
        buf[pl.ds(SEQ + HALF_WINDOW, HALF_WINDOW), :] = pad


def _attn_fwd(q, k, v, comm=None):
    group = 8

    def body(q_ref, k_ref, v_ref, o_ref, lse_ref, q32, k32, v32, qa, qb, kp, vp, accp, mlp,
             acc0, acc1, acc2, ml0, ml1, ml2, bias_ref):
        first, mask_a, mask_b = _head_masks()
        low = lax.broadcasted_iota(jnp.int32, (1, LANES), 1) % HEAD_DIM < HEAD_DIM // 2
        _zero_key_padding((kp, vp))
        _write_band_bias(bias_ref)
        q32[...] = q_ref[...].astype(F32)
        k32[...] = k_ref[...].astype(F32)
        v32[...] = v_ref[...].astype(F32)
        natural = ((acc0, ml0), (acc1, ml1), (acc2, ml2))

        for branch, dil in enumerate(DILATIONS):
            length = SEQ // dil
            assert length >= 2 * Q_BLOCK
            chunks = _perm_chunks(dil)
            for c, rows in chunks:
                val = q32[rows, :]
                qa[_chunk(c), :] = (val * mask_a).astype(BF16)
                qb[_chunk(c), :] = (val * mask_b).astype(BF16)
                kp[_chunk(c, HALF_WINDOW), :] = k32[rows, :].astype(BF16)
                vp[_chunk(c, HALF_WINDOW), :] = v32[rows, :].astype(BF16)
            acc_dst, ml_dst = natural[branch] if dil == 1 else (accp, mlp)

            def blocks(i, carry, length=length, acc_dst=acc_dst, ml_dst=ml_dst):
                base = pl.multiple_of(i * (group * Q_BLOCK), group * Q_BLOCK)
                starts = [base + g * Q_BLOCK for g in range(group)]
                scores = [[_dot_nt(qx[pl.ds(m0, Q_BLOCK), :], kp[pl.ds(m0, K_WINDOW), :]) for qx in (qa, qb)] for m0 in starts]
                probs = []
                for m0, pair in zip(starts, scores):
                    bias = bias_ref[_band_bias_index(m0, length)]
                    stats, ps = [], []
                    for s in pair:
                        s = s + bias
                        m = jnp.max(s, axis=1, keepdims=True)
                        p = jnp.exp(s - m)
                        stats.append(jnp.where(low, m, jnp.sum(p, axis=1, keepdims=True)))
                        ps.append(p.astype(BF16))
                    ml_dst[pl.ds(m0, Q_BLOCK), :] = jnp.where(first, stats[0], stats[1])
                    probs.append(ps)
                for m0, ps in zip(starts, probs):
                    vw = vp[pl.ds(m0, K_WINDOW), :]
                    acc_dst[pl.ds(m0, Q_BLOCK), :] = jnp.where(first, _dot(ps[0], vw), _dot(ps[1], vw))
                return carry

            lax.fori_loop(0, SEQ // (group * Q_BLOCK), blocks, 0)

            if dil > 1:
                for c, rows in chunks:
                    natural[branch][0][rows, :] = accp[_chunk(c), :]
                    natural[branch][1][rows, :] = mlp[_chunk(c), :]

        for c in range(SEQ // PERM_CHUNK):
            def up(t):
                return jnp.where(low, t, pltpu.roll(t, HEAD_DIM // 2, axis=1))

            packed = [ml[_chunk(c), :] for _, ml in natural]
            top = jnp.maximum(jnp.maximum(packed[0], packed[1]), packed[2])
            m_all = up(top)
            es = [up(jnp.exp(ml - top)) for ml in packed]
            z = packed[0] * es[0] + packed[1] * es[1] + packed[2] * es[2]
            z = jnp.where(low, pltpu.roll(z, LANES - HEAD_DIM // 2, axis=1), z)
            num = natural[0][0][_chunk(c), :] * es[0] + natural[1][0][_chunk(c), :] * es[1] + natural[2][0][_chunk(c), :] * es[2]
            o_ref[_chunk(c), :] = num / z
            lse_ref[_chunk(c), :] = m_all + jnp.log(z)

    col = pl.BlockSpec((SEQ, LANES), lambda h: (0, h))
    padded = SEQ + 2 * HALF_WINDOW
    return _call(
        body, (q, k, v), grid=(ATTN_WIDTH // LANES,), in_specs=[col] * 3, out_specs=[col] * 2,
        out_shape=[_sds((SEQ, ATTN_WIDTH), F32)] * 2,
        scratch_shapes=[pltpu.VMEM((SEQ, LANES), F32)] * 3 + [pltpu.VMEM((SEQ, LANES), BF16)] * 2
        + [pltpu.VMEM((padded, LANES), BF16)] * 2 + [pltpu.VMEM((SEQ, LANES), F32)] * 8
        + [pltpu.VMEM((3, Q_BLOCK, K_WINDOW), F32)],
        vmem_mib=52, name="attn_fwd", comm=comm)


def _shifted(t, before, after, i):
    tile = t.shape[0]
    row = lax.broadcasted_iota(jnp.int32, (tile, 1), 0)
    before = jnp.where(i > 0, before, 0.0)
    after = jnp.where(i < SEQ // tile - 1, after, 0.0)
    return (jnp.where(row == 0, before, pltpu.roll(t, 1, axis=0)),
            jnp.where(row == tile - 1, after, pltpu.roll(t, tile - 1, axis=0)))


def _last_row(ref):
    return ref[HALO_ROWS - 1:HALO_ROWS, :].astype(F32)


def _first_row(ref):
    return ref[0:1, :].astype(F32)


def _conv_parts(u, c, u_prev, c_prev, u_next, c_next, cw, i):
    t = c * u
    t_prev, t_next = _shifted(t, _last_row(c_prev) * _last_row(u_prev), _first_row(c_next) * _first_row(u_next), i)
    s = cw[0:1, :] * t_prev + cw[1:2, :] * t + cw[2:3, :] * t_next
    return t, t_prev, t_next, s


def _mix_fwd(attn, u, b, c, conv_w, g_attn, g_conv, w_out, x, g_post, comm=None):
    def body(a_ref, u_ref, b_ref, c_ref, up_ref, cp_ref, un_ref, cn_ref, cw_ref, ga_ref, gc_ref, w_ref, x_ref, gp_ref,
             x1_ref, mg_ref, mix_ref):
        i = pl.program_id(0)
        _, _, _, s = _conv_parts(u_ref[...].astype(F32), c_ref[...].astype(F32), up_ref, cp_ref, un_ref, cn_ref, cw_ref[...], i)
        ya, _ = _rms(a_ref[...])
        yc, _ = _rms(b_ref[...].astype(F32) * s)
        merged = jnp.concatenate([ya * ga_ref[...], yc * gc_ref[...]], axis=1).astype(BF16)
        mix = _dot(merged, w_ref[...])
        ym, _ = _rms(mix)
        mg_ref[...] = merged.T
        mix_ref[...] = mix.astype(BF16)
        x1_ref[...] = x_ref[...] + ym * gp_ref[...]

    a = ATTN_WIDTH
    return _call(
        body, (attn, u, b, c, u, c, u, c, conv_w, g_attn, g_conv, w_out, x, g_post), grid=(SEQ // ROW_TILE,),
        in_specs=[_rows(a)] * 4 + [_halo_prev(a)] * 2 + [_halo_next(a)] * 2
        + [_const((3, a)), _const((1, a)), _const((1, a)), _const1((D_MODEL, D_MODEL)), _rows(D_MODEL), _const((1, D_MODEL))],
        out_specs=[_rows(D_MODEL), _cols(D_MODEL, ROW_TILE), _rows(D_MODEL)],
        out_shape=[_sds((SEQ, D_MODEL), F32), _sds((D_MODEL, SEQ), BF16), _sds((SEQ, D_MODEL), BF16)],
        vmem_mib=40, name="mix_fwd", comm=comm)


def _gu_spec():
    return pl.BlockSpec((N_DEV, FFN_TILE, FFN_BLOCK), lambda i: (0, i, 0))


def _ffn_fwd(x1, g_pre, w_gu, w_dn, g_post, comm=None, target=None):
    n_tiles = SEQ // ROW_TILE

    def body(*refs):
        if target is None:
            x_ref, g_ref, wgu_ref, wdn_ref, gp_ref, x2_ref, gu_ref, f_ref = refs
        else:
            x_ref, g_ref, wgu_ref, wdn_ref, gp_ref, t_ref, x2_ref, gu_ref, f_ref, loss_ref, acc = refs
        x1 = x_ref[...]
        y, _ = _rms(x1)
        h = (y * g_ref[...]).astype(BF16)
        f = jnp.zeros((ROW_TILE, D_MODEL), F32)

        def gate_up(j):
            return _dot_nt(h, wgu_ref[j]), _dot_nt(h, wgu_ref[j + N_DEV // 2])

        ahead = gate_up(0)
        for j in range(N_DEV // 2):
            gate, up = ahead
            if j + 1 < N_DEV // 2:
                ahead = gate_up(j + 1)
            gu_ref[j] = gate.astype(BF16)
            gu_ref[j + N_DEV // 2] = up.astype(BF16)
            act = (gate * jax.nn.sigmoid(gate) * up).astype(BF16)
            f = f + _dot(act, wdn_ref[pl.ds(j * FFN_BLOCK, FFN_BLOCK), :])
        yf, _ = _rms(f)
        f_ref[...] = f
        x2 = x1 + yf * gp_ref[...]
        if target is None:
            x2_ref[...] = x2
        else:
            i = pl.program_id(0)
            err = x2 - t_ref[...]
            x2_ref[...] = err * (1.0 / D_MODEL)

            @pl.when(i == 0)
            def _():
                acc[...] = jnp.zeros_like(acc)

            acc[...] += jnp.sum(err * err, axis=0, keepdims=True)

            @pl.when(i == n_tiles - 1)
            def _():
                loss_ref[...] = jnp.sum(acc[...], axis=1, keepdims=True) * (0.5 / D_MODEL)

    with_loss = target is not None
    return _call(
        body, (x1, g_pre, w_gu, w_dn, g_post) + ((target,) if with_loss else ()), grid=(n_tiles,),
        in_specs=[_rows(D_MODEL), _const((1, D_MODEL)), _const1((N_DEV, FFN_BLOCK, D_MODEL)), _const1((FFN_HIDDEN, D_MODEL)),
                  _const((1, D_MODEL))] + ([_rows(D_MODEL)] if with_loss else []),
        out_specs=[_rows(D_MODEL), pl.BlockSpec((N_DEV, ROW_TILE, FFN_BLOCK), lambda i: (0, i, 0)), _rows(D_MODEL)]
        + ([_const((1, 1))] if with_loss else []),
        out_shape=[_sds((SEQ, D_MODEL), F32), _sds((N_DEV, SEQ, FFN_BLOCK), BF16), _sds((SEQ, D_MODEL), F32)]
        + ([_sds((1, 1), F32)] if with_loss else []),
        scratch_shapes=[pltpu.VMEM((1, D_MODEL), F32)] if with_loss else [],
        vmem_mib=58, name="ffn_fwd_loss" if with_loss else "ffn_fwd", comm=comm)


def _accumulate(pairs, i):
    @pl.when(i == 0)
    def _():
        for ref, value in pairs:
            ref[...] = value

    @pl.when(i > 0)
    def _():
        for ref, value in pairs:
            ref[...] += value


def _colsum(v):
    return jnp.sum(v, axis=0, keepdims=True)


def _ffn_bwd(dx2, f, x1, gu, w_gu, w_dn, g_post, g_pre, comm=None):
    half = N_DEV // 2

    def body(dx2_ref, f_ref, x1_ref, gu_ref, wgu_ref, wdn_ref, gpost_ref, gpre_ref,
             dx1_ref, df_ref, act_ref, dgu_ref, h_ref, dgpost_ref, dgpre_ref):
        i = pl.program_id(0)
        dx2 = dx2_ref[...]
        yf, rf = _rms(f_ref[...])
        dg_post = _colsum(dx2 * yf)
        df = _rms_bwd(dx2 * gpost_ref[...], yf, rf).astype(BF16)
        df_ref[...] = df
        dh = jnp.zeros((FFN_TILE, D_MODEL), F32)

        def d_act(j):
            return _dot_nt(df, wdn_ref[pl.ds(j * FFN_BLOCK, FFN_BLOCK), :])

        ahead = d_act(0)
        for j in range(half):
            dact = ahead
            if j + 1 < half:
                ahead = d_act(j + 1)
            gate = gu_ref[j].astype(F32)
            up = gu_ref[j + half].astype(F32)
            sig = jax.nn.sigmoid(gate)
            silu = gate * sig
            act_ref[j] = (silu * up).astype(BF16)
            dgate = (dact * up * (sig * (1.0 + gate * (1.0 - sig)))).astype(BF16)
            dup = (dact * silu).astype(BF16)
            dgu_ref[j] = dgate
            dgu_ref[j + half] = dup
            dh = dh + _dot(dgate, wgu_ref[j]) + _dot(dup, wgu_ref[j + half])
        y1, r1 = _rms(x1_ref[...])
        h_ref[...] = (y1 * gpre_ref[...]).astype(BF16)
        dx1_ref[...] = dx2 + _rms_bwd(dh * gpre_ref[...], y1, r1)
        _accumulate([(dgpost_ref, dg_post), (dgpre_ref, _colsum(dh * y1))], i)

    act_spec = pl.BlockSpec((half, FFN_TILE, FFN_BLOCK), lambda i: (0, i, 0))
    return _call(
        body, (dx2, f, x1, gu, w_gu, w_dn, g_post, g_pre), grid=(SEQ // FFN_TILE,),
        in_specs=[_frows(D_MODEL)] * 3 + [_gu_spec(), _const1((N_DEV, FFN_BLOCK, D_MODEL)), _const1((FFN_HIDDEN, D_MODEL)),
                                          _const((1, D_MODEL)), _const((1, D_MODEL))],
        out_specs=[_frows(D_MODEL), _frows(D_MODEL), act_spec, _gu_spec(), _frows(D_MODEL), _const((1, D_MODEL)), _const((1, D_MODEL))],
        out_shape=[_sds((SEQ, D_MODEL), F32), _sds((SEQ, D_MODEL), BF16), _sds((half, SEQ, FFN_BLOCK), BF16),
                   _sds((N_DEV, SEQ, FFN_BLOCK), BF16), _sds((SEQ, D_MODEL), BF16), _sds((1, D_MODEL), F32), _sds((1, D_MODEL), F32)],
        vmem_mib=52, name="ffn_bwd", comm=comm)


def _wgrad(a_t, g, name):
    width = g.shape[1]

    def body(a_ref, g_ref, o_ref):
        o_ref[...] = _dot(a_ref[...], g_ref[...]).astype(BF16)

    return _call(body, (a_t, g), grid=(width // WGRAD_COLS,),
                 in_specs=[_const1((D_MODEL, SEQ)), pl.BlockSpec((SEQ, WGRAD_COLS), lambda j: (0, j))],
                 out_specs=[pl.BlockSpec((D_MODEL, WGRAD_COLS), lambda j: (0, j))], out_shape=[_sds((D_MODEL, width), BF16)],
                 vmem_mib=48, name=name)[0][0]


def _wgrad_paired(a, g, kind, to_owners=False, comm=None):
    shard = SHARD_SHAPES[kind]
    per_chip = 2 if kind == "gu" else 1
    n_compute = per_chip * N_CHIPS
    n_steps = n_compute + 1

    def chip_of(turn):
        if not to_owners:
            return turn
        return (2 * lax.axis_index("x") + lax.axis_index("y") + 1 + turn) % N_CHIPS

    def block(t):
        return jnp.minimum(t, n_compute - 1)

    if kind == "in":
        specs = [_const1((D_MODEL, SEQ)), pl.BlockSpec((SEQ, 2 * W_IN_BLOCK), lambda t: (0, chip_of(block(t))))]
    else:
        specs = [pl.BlockSpec((1, SEQ, FFN_BLOCK), lambda t: (block(t), 0, 0)), _const1((SEQ, D_MODEL))]

    def body(a_ref, g_ref, o_ref, *rest):
        if to_owners:
            landing, sendbuf, recvbuf, keepbuf, send_sem, recv_sem, donebuf, owner_send, owner_recv, own_sem = rest
        else:
            sendbuf, recvbuf, keepbuf, send_sem, recv_sem = rest
        t = pl.program_id(0)
        x, y, c = _place()

        def exchange(q):
            return pltpu.make_async_remote_copy(src_ref=sendbuf.at[q], dst_ref=recvbuf.at[q], send_sem=send_sem.at[q],
                                                recv_sem=recv_sem.at[q], device_id=(x, y, 1 - c), device_id_type=MESH)

        my_chip = 2 * x + y

        def to_owner(q):
            return pltpu.make_async_remote_copy(src_ref=donebuf.at[q], dst_ref=landing.at[my_chip], send_sem=owner_send.at[q],
                                                recv_sem=owner_recv.at[my_chip], device_id=(q // 2, q % 2, c), device_id_type=MESH)

        def keep_own():
            return pltpu.make_async_copy(donebuf.at[my_chip], landing.at[my_chip], own_sem)

        @pl.when((t >= per_chip) & (t % per_chip == 0))
        def _():
            q = chip_of(t // per_chip - 1)
            exchange(q).wait_recv()
            done = (keepbuf[...] + recvbuf[q].astype(F32)).astype(BF16)
            o_ref[0] = done
            if to_owners:
                donebuf[q] = done

                @pl.when(q == my_chip)
                def _():
                    keep_own().start()

                @pl.when(q != my_chip)
                def _():
                    to_owner(q).start()

        @pl.when(t < n_compute)
        def _():
            q = chip_of(t // per_chip)
            if kind == "gu":
                r = _dot_tn(a_ref[0], g_ref[...])

                @pl.when(t % 2 == c)
                def _():
                    keepbuf[...] = r

                @pl.when(t % 2 != c)
                def _():
                    sendbuf[q] = r.astype(BF16)
                    exchange(q).start()
            else:
                if kind == "dn":
                    r = _dot_tn(a_ref[0], g_ref[...])
                    lower, upper = r[:W_DOWN_BLOCK], r[W_DOWN_BLOCK:]
                else:
                    r = _dot(a_ref[...], g_ref[...])
                    lower, upper = r[:, :W_IN_BLOCK], r[:, W_IN_BLOCK:]
                keepbuf[...] = jnp.where(c == 0, lower, upper)
                sendbuf[q] = jnp.where(c == 0, upper, lower).astype(BF16)
                exchange(q).start()

        @pl.when(t == n_steps - 1)
        def _():
            for q in range(N_CHIPS):
                exchange(q).wait_send()
            if to_owners:
                keep_own().wait()
                for s in range(N_CHIPS):
                    @pl.when(s != my_chip)
                    def _(s=s):
                        pltpu.make_async_remote_copy(src_ref=donebuf.at[s], dst_ref=landing.at[s], send_sem=owner_send.at[s],
                                                     recv_sem=owner_recv.at[s], device_id=(x, y, c), device_id_type=MESH).wait()

    slots = (N_CHIPS,) + shard
    owners = int(to_owners)
    res, landed = _call(
        body, (a, g), grid=(n_steps,), in_specs=specs,
        out_specs=[pl.BlockSpec((1,) + shard, lambda t: (chip_of(jnp.clip(t // per_chip - 1, 0, N_CHIPS - 1)), 0, 0))] + [ANY] * owners,
        out_shape=[_sds(slots, BF16)] * (1 + owners),
        scratch_shapes=[pltpu.VMEM(slots, BF16), pltpu.VMEM(slots, BF16), pltpu.VMEM(shard, F32),
                        pltpu.SemaphoreType.DMA((N_CHIPS,)), pltpu.SemaphoreType.DMA((N_CHIPS,))]
        + [pltpu.VMEM(slots, BF16), pltpu.SemaphoreType.DMA((N_CHIPS,)), pltpu.SemaphoreType.DMA((N_CHIPS,)), pltpu.SemaphoreType.DMA] * owners,
        vmem_mib=52, name="wgrad_" + kind + "_paired" + "_to_owners" * owners, comm=comm)
    return res[owners] if comm is None else (res[owners], landed)


def _mix_bwd(dx1, mix, attn, u, b, c, conv_w, g_attn, g_conv, g_post, w_out):
    def body(dx1_ref, mix_ref, a_ref, u_ref, b_ref, c_ref, up_ref, cp_ref, un_ref, cn_ref, cw_ref, ga_ref, gc_ref, gp_ref, w_ref,
             dmix_ref, da_ref, ds_ref, db_ref, dgp_ref, dga_ref, dgc_ref):
        i = pl.program_id(0)
        dx1 = dx1_ref[...]
        ym, rm = _rms(mix_ref[...].astype(F32))
        dg_post = _colsum(dx1 * ym)
        dmix = _rms_bwd(dx1 * gp_ref[...], ym, rm).astype(BF16)
        dmix_ref[...] = dmix
        dmerged = _dot_nt(dmix, w_ref[...])
        dna, dnc = dmerged[:, :ATTN_WIDTH], dmerged[:, ATTN_WIDTH:]
        ya, ra = _rms(a_ref[...])
        da_ref[...] = _rms_bwd(dna * ga_ref[...], ya, ra)
        _, _, _, s = _conv_parts(u_ref[...].astype(F32), c_ref[...].astype(F32), up_ref, cp_ref, un_ref, cn_ref, cw_ref[...], i)
        gate_b = b_ref[...].astype(F32)
        yc, rc = _rms(gate_b * s)
        dy = _rms_bwd(dnc * gc_ref[...], yc, rc)
        db_ref[...] = (dy * s).astype(BF16)
        ds_ref[...] = (dy * gate_b).astype(BF16)
        _accumulate([(dgp_ref, dg_post), (dga_ref, _colsum(dna * ya)), (dgc_ref, _colsum(dnc * yc))], i)

    a = ATTN_WIDTH
    return _call(
        body, (dx1, mix, attn, u, b, c, u, c, u, c, conv_w, g_attn, g_conv, g_post, w_out), grid=(SEQ // ROW_TILE,),
        in_specs=[_rows(D_MODEL)] * 2 + [_rows(a)] * 4 + [_halo_prev(a)] * 2 + [_halo_next(a)] * 2
        + [_const((3, a)), _const((1, a)), _const((1, a)), _const((1, D_MODEL)), _const1((D_MODEL, D_MODEL))],
        out_specs=[_rows(D_MODEL)] + [_rows(a)] * 3 + [_const((1, D_MODEL)), _const((1, a)), _const((1, a))],
        out_shape=[_sds((SEQ, D_MODEL), BF16), _sds((SEQ, a), F32), _sds((SEQ, a), BF16), _sds((SEQ, a), BF16),
                   _sds((1, D_MODEL), F32), _sds((1, a), F32), _sds((1, a), F32)],
        vmem_mib=40, name="mix_bwd")[0]


def _attn_bwd(q, k, v, do, o, lse, comm=None):
    group = 8

    n_pairs = ATTN_WIDTH // LANES

    def body(q_hbm, k_hbm, v_hbm, do_hbm, o_hbm, lse_hbm, dq_ref, dk_ref, dv_ref,
             q_ref, k_ref, v_ref, do_ref, o_ref, lse_ref, in_sem,
             q32, k32, v32, qa, qb, doa, dob, kp, vp, lsep, dlp, dnat, dqp, dkp, dvp, bias_ref):
        pair = pl.program_id(0)
        sources = (q_hbm, k_hbm, v_hbm, do_hbm, o_hbm, lse_hbm)
        buffers = (q_ref, k_ref, v_ref, do_ref, o_ref, lse_ref)

        def fetch(i, p):
            return pltpu.make_async_copy(sources[i].at[:, pl.ds(pl.multiple_of(p * LANES, LANES), LANES)], buffers[i], in_sem.at[i])

        def prefetch(*which):
            @pl.when(pair + 1 < n_pairs)
            def _():
                for i in which:
                    fetch(i, pair + 1).start()

        @pl.when(pair == 0)
        def _():
            for i in range(len(sources)):
                fetch(i, 0).start()

        for i in range(len(sources)):
            fetch(i, pair).wait()

        first, mask_a, mask_b = _head_masks()
        _zero_key_padding((kp, vp))
        _write_band_bias(bias_ref)
        q32[...] = q_ref[...].astype(F32)
        k32[...] = k_ref[...].astype(F32)
        v32[...] = v_ref[...].astype(F32)
        prefetch(0, 1, 2)
        for c in range(SEQ // PERM_CHUNK):
            prod = do_ref[_chunk(c), :] * o_ref[_chunk(c), :]
            d_a = jnp.sum(prod * mask_a, axis=1, keepdims=True)
            d_b = jnp.sum(prod * mask_b, axis=1, keepdims=True)
            dnat[_chunk(c), :] = jnp.where(first, d_a, d_b)
        prefetch(4)

        for step, dil in enumerate(DILATIONS[1:] + DILATIONS[:1]):
            length = SEQ // dil
            assert length >= 2 * Q_BLOCK
            chunks = _perm_chunks(dil)
            for c, rows in chunks:
                val = q32[rows, :]
                qa[_chunk(c), :] = (val * mask_a).astype(BF16)
                qb[_chunk(c), :] = (val * mask_b).astype(BF16)
                val = do_ref[rows, :]
                doa[_chunk(c), :] = (val * mask_a).astype(BF16)
                dob[_chunk(c), :] = (val * mask_b).astype(BF16)
                kp[_chunk(c, HALF_WINDOW), :] = k32[rows, :].astype(BF16)
                vp[_chunk(c, HALF_WINDOW), :] = v32[rows, :].astype(BF16)
                lsep[_chunk(c), :] = lse_ref[rows, :]
                dlp[_chunk(c), :] = dnat[rows, :]
            if step == len(DILATIONS) - 1:
                prefetch(3, 5)
            zero = jnp.zeros((PERM_CHUNK, LANES), F32)
            for c in range(SEQ // PERM_CHUNK):
                dkp[_chunk(c), :] = zero
                dvp[_chunk(c), :] = zero
            dkp[pl.ds(SEQ, 2 * HALF_WINDOW), :] = zero[:2 * HALF_WINDOW]
            dvp[pl.ds(SEQ, 2 * HALF_WINDOW), :] = zero[:2 * HALF_WINDOW]

            heads = ((qa, doa, 0), (qb, dob, HEAD_DIM))

            def blocks(i, carry, length=length):
                base = pl.multiple_of(i * (group * Q_BLOCK), group * Q_BLOCK)
                starts = [base + g * Q_BLOCK for g in range(group)]
                raw = [[(_dot_nt(qx[pl.ds(m0, Q_BLOCK), :], kp[pl.ds(m0, K_WINDOW), :]),
                         _dot_nt(dox[pl.ds(m0, Q_BLOCK), :], vp[pl.ds(m0, K_WINDOW), :])) for qx, dox, _ in heads]
                       for m0 in starts]
                grads = []
                for m0, pair in zip(starts, raw):
                    bias = bias_ref[_band_bias_index(m0, length)]
                    lse_b, d_b = lsep[pl.ds(m0, Q_BLOCK), :], dlp[pl.ds(m0, Q_BLOCK), :]
                    out = []
                    for (s, dp), (_, _, col) in zip(pair, heads):
                        p = jnp.exp(s + bias - lse_b[:, col:col + 1])
                        out.append(((p * (dp - d_b[:, col:col + 1])).astype(BF16), p.astype(BF16)))
                    grads.append(out)
                for m0, out in zip(starts, grads):
                    qrows, krows = pl.ds(m0, Q_BLOCK), pl.ds(m0, K_WINDOW)
                    kw = kp[krows, :]
                    dk = jnp.zeros((K_WINDOW, LANES), F32)
                    dv = jnp.zeros((K_WINDOW, LANES), F32)
                    for (ds, p), (qx, dox, _) in zip(out, heads):
                        dk = dk + _dot_tn(ds, qx[qrows, :])
                        dv = dv + _dot_tn(p, dox[qrows, :])
                    dqp[qrows, :] = jnp.where(first, _dot(out[0][0], kw), _dot(out[1][0], kw)) * (HEAD_DIM ** -0.5)
                    dkp[krows, :] += dk
                    dvp[krows, :] += dv
                return carry

            lax.fori_loop(0, SEQ // (group * Q_BLOCK), blocks, 0)

            for c, rows in chunks:
                g_q, g_k, g_v = dqp[_chunk(c), :], dkp[_chunk(c, HALF_WINDOW), :], dvp[_chunk(c, HALF_WINDOW), :]
                if step == 0:
                    dq_ref[rows, :] = g_q
                    dk_ref[rows, :] = g_k
                    dv_ref[rows, :] = g_v
                else:
                    dq_ref[rows, :] = dq_ref[rows, :] + g_q
                    dk_ref[rows, :] = dk_ref[rows, :] + g_k
                    dv_ref[rows, :] = dv_ref[rows, :] + g_v

    col = pl.BlockSpec((SEQ, LANES), lambda h: (0, h))
    padded = SEQ + 2 * HALF_WINDOW
    return _call(
        body, (q, k, v, do, o, lse), grid=(n_pairs,), in_specs=[ANY] * 6, out_specs=[col] * 3,
        out_shape=[_sds((SEQ, ATTN_WIDTH), F32)] * 3,
        scratch_shapes=[pltpu.VMEM((SEQ, LANES), BF16)] * 3 + [pltpu.VMEM((SEQ, LANES), F32)] * 3 + [pltpu.SemaphoreType.DMA((6,))]
        + [pltpu.VMEM((SEQ, LANES), F32)] * 3 + [pltpu.VMEM((SEQ, LANES), BF16)] * 4
        + [pltpu.VMEM((padded, LANES), BF16)] * 2 + [pltpu.VMEM((SEQ, LANES), F32)] * 4 + [pltpu.VMEM((padded, LANES), F32)] * 2
        + [pltpu.VMEM((3, Q_BLOCK, K_WINDOW), F32)],
        vmem_mib=56, name="attn_bwd", comm=comm)


def _inproj_bwd(dq, dk, dv, ds, db, u, c, conv_w, tc, tsa, tsb, w_in, x, g_pre, dx1):
    def body(dq_ref, dk_ref, dv_ref, ds_ref, db_ref, u_ref, c_ref, dsp_ref, up_ref, cp_ref, dsn_ref, un_ref, cn_ref, cw_ref,
             tc_ref, tsa_ref, tsb_ref, w_ref, x_ref, g_ref, dx1_ref, dx_ref, dproj_ref, h_ref, dg_ref, dcw_ref):
        i = pl.program_id(0)
        cw = cw_ref[...]
        u, c = u_ref[...].astype(F32), c_ref[...].astype(F32)
        t, t_prev, t_next, _ = _conv_parts(u, c, up_ref, cp_ref, un_ref, cn_ref, cw, i)
        ds = ds_ref[...].astype(F32)
        ds_prev, ds_next = _shifted(ds, _last_row(dsp_ref), _first_row(dsn_ref), i)
        dt = cw[0:1, :] * ds_next + cw[1:2, :] * ds + cw[2:3, :] * ds_prev
        d_taps = jnp.concatenate([_colsum(ds * t_prev), _colsum(ds * t), _colsum(ds * t_next)], axis=0)
        tc_, tsa_, tsb_ = tc_ref[...], tsa_ref[...], tsb_ref[...]
        groups = ((2, lambda: dv_ref[...].astype(BF16)), (4, lambda: db_ref[...]),
                  (0, lambda: _rotate_transposed(dq_ref[...], tc_, tsa_, tsb_).astype(BF16)),
                  (1, lambda: _rotate_transposed(dk_ref[...], tc_, tsa_, tsb_).astype(BF16)),
                  (3, lambda: (dt * c).astype(BF16)), (5, lambda: (dt * u).astype(BF16)))
        dh = jnp.zeros((x_ref.shape[0], D_MODEL), F32)
        for n, make in groups:
            cols = pl.ds(n * a, a)
            part = make()
            dproj_ref[:, cols] = part
            dh = dh + _dot_nt(part, w_ref[:, cols])
        y, r = _rms(x_ref[...])
        h_ref[...] = (y * g_ref[...]).astype(BF16).T
        dx_ref[...] = dx1_ref[...] + _rms_bwd(dh * g_ref[...], y, r)
        _accumulate([(dg_ref, _colsum(dh * y)), (dcw_ref, d_taps)], i)

    a = ATTN_WIDTH
    tile = FFN_TILE

    def rows(width):
        return _rows(width, tile)

    return _call(
        body, (dq, dk, dv, ds, db, u, c, ds, u, c, ds, u, c, conv_w, tc, tsa, tsb, w_in, x, g_pre, dx1), grid=(SEQ // tile,),
        in_specs=[rows(a)] * 7 + [_halo_prev(a, tile)] * 3 + [_halo_next(a, tile)] * 3
        + [_const((3, a)), rows(LANES), rows(LANES), rows(LANES), _const1((D_MODEL, IN_PROJ_WIDTH)), rows(D_MODEL),
           _const((1, D_MODEL)), rows(D_MODEL)],
        out_specs=[rows(D_MODEL), rows(IN_PROJ_WIDTH), _cols(D_MODEL, tile), _const((1, D_MODEL)), _const((3, a))],
        out_shape=[_sds((SEQ, D_MODEL), F32), _sds((SEQ, IN_PROJ_WIDTH), BF16), _sds((D_MODEL, SEQ), BF16),
                   _sds((1, D_MODEL), F32), _sds((3, a), F32)],
        vmem_mib=48, name="inproj_bwd")


def _adamw_math(w, g, m, v):
    m = ADAM_B1 * m + (1.0 - ADAM_B1) * g
    v = ADAM_B2 * v + (1.0 - ADAM_B2) * (g * g)
    m_hat = m / (1.0 - ADAM_B1 ** ADAM_STEP)
    v_hat = v / (1.0 - ADAM_B2 ** ADAM_STEP)
    delta = -ADAM_LR * (m_hat / (jnp.sqrt(v_hat) + ADAM_EPS) + ADAM_WD * w)
    return delta, m, v


def _sum_parts(p_ref):
    g = p_ref[0].astype(F32)
    for k in range(1, p_ref.shape[0]):
        g = g + p_ref[k].astype(F32)
    return g


def _adamw_layers(parts, w, m, v, row_tile, name, comm=None):
    _, rows, cols = w.shape
    assert rows % row_tile == 0
    n_tiles = rows // row_tile

    def body(*refs):
        p_refs = refs[:DEPTH]
        w_ref, m_ref, v_ref, g_ref, d_ref, nm_ref, nv_ref = refs[DEPTH:]
        layer = pl.program_id(0)
        for l, p_ref in enumerate(p_refs):
            @pl.when(layer == l)
            def _(p_ref=p_ref):
                g = _sum_parts(p_ref)
                g_ref[0] = g
                d_ref[0], nm_ref[0], nv_ref[0] = _adamw_math(w_ref[0], g, m_ref[0], v_ref[0])

    def part_spec(l):
        return pl.BlockSpec((parts[l].shape[0], row_tile, cols),
                            lambda layer, i: (0, jnp.where(layer == l, i, jnp.where(layer < l, 0, n_tiles - 1)), 0))

    tile = pl.BlockSpec((1, row_tile, cols), lambda layer, i: (layer, i, 0))
    buffered = 2 * row_tile * cols * (sum(p.shape[0] for p in parts) * 2 + 7 * 4)
    return _call(body, (*parts, w, m, v), grid=(DEPTH, n_tiles), in_specs=[part_spec(l) for l in range(DEPTH)] + [tile] * 3,
                 out_specs=[tile] * 4, out_shape=[_sds(w.shape, F32)] * 4, vmem_mib=-(-buffered // MIB) + 8, name=name, comm=comm)


def _adamw_small(parts, w, m, v):
    n = len(GAIN_NAMES)
    a = ATTN_WIDTH
    where = ((0, 0), (1, 0), (2, 0), (3, 0), (4, 0), (4, a))

    def body(*refs):
        p_ref = refs[0]
        w_refs, m_refs, v_refs = refs[1:1 + n], refs[1 + n:1 + 2 * n], refs[1 + 2 * n:1 + 3 * n]
        outs = refs[1 + 3 * n:1 + 7 * n]
        taps_ref, loss_ref, sum_ref = refs[1 + 7 * n:]
        sum_ref[...] = _sum_parts(p_ref)
        for k, (row, col) in enumerate(where):
            width = w_refs[k].shape[1]
            g = jnp.concatenate([sum_ref[pl.ds(l * SMALL_ROWS + row, 1), pl.ds(col, width)] for l in range(DEPTH)], axis=0)
            g_ref, d_ref, nm_ref, nv_ref = outs[4 * k:4 * k + 4]
            g_ref[...] = g
            d_ref[...], nm_ref[...], nv_ref[...] = _adamw_math(w_refs[k][...], g, m_refs[k][...], v_refs[k][...])
        for l in range(DEPTH):
            for tap, (row, col) in enumerate(((5, 0), (5, a), (6, 0))):
                taps_ref[l, tap:tap + 1, :] = sum_ref[l * SMALL_ROWS + row:l * SMALL_ROWS + row + 1, col:col + a]
        loss_ref[...] = sum_ref[SMALL_ROWS - 1:SMALL_ROWS, 0:1]

    out_shape = [_sds(x.shape, F32) for x in w for _ in range(4)] + [_sds((DEPTH, 3, a), F32), _sds((1, 1), F32)]
    res = pl.pallas_call(body, out_shape=out_shape, scratch_shapes=[pltpu.VMEM(parts.shape[1:], F32)],
                         name="adamw_small")(parts, *w, *m, *v)
    return [res[4 * k:4 * k + 4] for k in range(n)], res[4 * n], res[4 * n + 1]


def _adamw_plain(g, w, m, v, name):
    def body(g_ref, w_ref, m_ref, v_ref, d_ref, nm_ref, nv_ref):
        d_ref[...], nm_ref[...], nv_ref[...] = _adamw_math(w_ref[...], g_ref[...], m_ref[...], v_ref[...])

    return pl.pallas_call(body, out_shape=[_sds(w.shape, F32)] * 3, name=name)(g, w, m, v)


SMALL_ROWS = 8
GAIN_NAMES = ("pre_mix_norm", "post_mix_norm", "pre_ffn_norm", "post_ffn_norm", "attn_out_norm", "conv_out_norm")


def _pack_small(pre_mix, post_mix, pre_ffn, post_ffn, attn_out, conv_out, taps):
    zeros = jnp.zeros((1, ATTN_WIDTH), F32)
    return jnp.concatenate([
        pre_mix, post_mix, pre_ffn, post_ffn, jnp.concatenate([attn_out, conv_out], axis=1),
        jnp.concatenate([taps[0:1], taps[1:2]], axis=1), jnp.concatenate([taps[2:3], zeros], axis=1),
        jnp.zeros((1, D_MODEL), F32)], axis=0)


def _rope(positions, casts, comm=None):
    inv_freq = ROPE_THETA ** (-jnp.arange(0, ROPE_DIM, 2, dtype=F32) / ROPE_DIM)
    per_head = jnp.concatenate([inv_freq, inv_freq, jnp.zeros((HEAD_DIM - ROPE_DIM,), F32)])
    freq_row = jnp.tile(per_head, LANES // HEAD_DIM).reshape(1, LANES)
    return _rope_tables(positions.reshape(SEQ, 1), freq_row, casts, comm)


def _layer_forward(h, gains, taps, tables, w, inproj_comm=None, attn_comm=None, mix_comm=None, ffn_comm=None, target=None):
    def arrived(comm, landed):
        return w if comm is None else {**w, **dict(zip(comm.kinds, landed))}

    (q, k, v, u, b, c), landed = _inproj_fwd(h, gains["pre_mix_norm"], w["in"], *tables, comm=inproj_comm)
    w = arrived(inproj_comm, landed)
    (attn, lse), landed = _attn_fwd(q, k, v, comm=attn_comm)
    w = arrived(attn_comm, landed)
    (x1, merged, mix), landed = _mix_fwd(attn, u, b, c, taps, gains["attn_out_norm"], gains["conv_out_norm"], w["out"], h,
                                         gains["post_mix_norm"], comm=mix_comm)
    w = arrived(mix_comm, landed)
    (x2, gu, f, *loss), landed_next = _ffn_fwd(x1, gains["pre_ffn_norm"], w["gu"], w["dn"], gains["post_ffn_norm"], comm=ffn_comm,
                                               target=target)
    out = x2 if target is None else (x2, loss[0])
    return out, (h, q, k, v, u, b, c, attn, lse, merged, mix, x1, gu, f), w, landed_next


class _GradientExchange:
    def ffn_grads(self, act, df, h2, dgu):
        return dict(dn=_wgrad_paired(act, df, "dn"), gu=_wgrad_paired(dgu, h2, "gu"))

    def w_out_grad(self, merged_t, dmix):
        return _wgrad(merged_t, dmix, "wgrad_out")

    def attention_passenger(self, ffn, g_out):
        return _Together(_ScatterChips(("gu", "dn"), (ffn["gu"], ffn["dn"])), _Scatter(("out",), (g_out,)))

    def w_in_grad(self, h_t, dproj, small_rows):
        if small_rows is None:
            return _wgrad(h_t, dproj, "wgrad_in"), None
        parts, landed = _wgrad_paired(h_t, dproj, "in", to_owners=True, comm=_GatherSmall(small_rows, "small"))
        return parts, landed[0]


def _layer_backward(dx, saved, gains, taps, tables, w, ffn_comm, exchange, last):
    x0, q, k, v, u, b, c, attn, lse, merged, mix, x1, gu, f = saved
    (dx1, df, act, dgu, h2, dg_post_ffn, dg_pre_ffn), landed_prev = _ffn_bwd(
        dx, f, x1, gu, w["gu"], w["dn"], gains["post_ffn_norm"], gains["pre_ffn_norm"], comm=ffn_comm)
    ffn = exchange.ffn_grads(act, df, h2, dgu)
    dmix, dattn, ds, db, dg_post_mix, dg_attn, dg_conv = _mix_bwd(
        dx1, mix, attn, u, b, c, taps, gains["attn_out_norm"], gains["conv_out_norm"], gains["post_mix_norm"], w["out"])
    g_out = exchange.w_out_grad(merged, dmix)
    attn_comm = exchange.attention_passenger(ffn, g_out)
    (dq, dk, dv), landed = _attn_bwd(q, k, v, dattn, attn, lse, comm=attn_comm)
    dx0, dproj, h1, dg_pre_mix, dtaps = _inproj_bwd(
        dq, dk, dv, ds, db, u, c, taps, *tables, w["in"], x0, gains["pre_mix_norm"], dx1)[0]
    small = _pack_small(dg_pre_mix, dg_post_mix, dg_pre_ffn, dg_post_ffn, dg_attn, dg_conv, dtaps)
    small_rows = None
    if last is not None:
        small_above, local_loss = last
        small_rows = lax.dynamic_update_slice(jnp.concatenate([small, small_above], axis=0), local_loss, (SMALL_ROWS - 1, 0))
    g_in, gathered = exchange.w_in_grad(h1, dproj, small_rows)
    rest = dict(zip(attn_comm.kinds, landed)) if attn_comm is not None else dict(ffn, out=g_out)
    return dx0, g_in, small, landed_prev, rest, gathered


def kernel(x, positions, pre_mix_norm, w_in, conv_w, attn_out_norm, conv_out_norm, w_out, post_mix_norm, pre_ffn_norm, w_gate_up, w_down, post_ffn_norm, loss_target, m_pre_mix_norm, m_w_in, m_conv_w, m_attn_out_norm, m_conv_out_norm, m_w_out, m_post_mix_norm, m_pre_ffn_norm, m_w_gate_up, m_w_down, m_post_ffn_norm, v_pre_mix_norm, v_w_in, v_conv_w, v_attn_out_norm, v_conv_out_norm, v_w_out, v_post_mix_norm, v_pre_ffn_norm, v_w_gate_up, v_w_down, v_post_ffn_norm):
    mx, my, mc = _place()
    me = _block_of(mx, my, mc)
    conv_channels = conv_w.shape[-1]

    def hidden_major(a):
        return jnp.swapaxes(a, 1, 2)

    big_w = dict(zip(WEIGHT_KINDS, (w_in, w_out, hidden_major(w_gate_up), w_down)))
    big_m = dict(zip(WEIGHT_KINDS, (m_w_in, m_w_out, hidden_major(m_w_gate_up), m_w_down)))
    big_v = dict(zip(WEIGHT_KINDS, (v_w_in, v_w_out, hidden_major(v_w_gate_up), v_w_down)))
    early = ("in", "out", "dn")
    shards = {early[0]: big_w[early[0]].astype(BF16)}
    all_gains = dict(pre_mix_norm=pre_mix_norm, attn_out_norm=attn_out_norm, conv_out_norm=conv_out_norm, post_mix_norm=post_mix_norm,
                     pre_ffn_norm=pre_ffn_norm, post_ffn_norm=post_ffn_norm)

    def gather(kinds, l):
        return _Gather(kinds, [shards[kind] for kind in kinds], l)

    def gains(l):
        return {name: g[l:l + 1] for name, g in all_gains.items()}

    taps_flat = jnp.pad(conv_w.reshape(-1), (0, 8 * LANES - conv_w.size)).reshape(8, LANES)
    later = [kind for kind in WEIGHT_KINDS if kind != early[0]]
    made, (first, taps_all) = _rope(positions, [big_w[kind] for kind in later],
                                    _Together(gather(early[:1], 0), _GatherSmall(taps_flat, "taps")))
    tables = made[:3]
    shards.update(zip(later, made[3:]))
    taps_all = taps_all.reshape(N_DEV, 8 * LANES)[:, :conv_w.size]
    conv_w_full = taps_all.reshape(N_DEV, DEPTH, 3, conv_channels).transpose(1, 2, 0, 3).reshape(DEPTH, 3, ATTN_WIDTH)
    weights = [dict(zip(early[:1], [first]))] + [None] * (DEPTH - 1)
    saved = [None] * DEPTH
    h = x[0]
    for l in range(DEPTH):
        inproj_comm = gather(("out",), 0) if l == 0 else None
        mix_comm = gather(("dn",), 0) if l == 0 else None
        ffn_comm = gather(early, l + 1) if l + 1 < DEPTH else None
        target = loss_target[0] if l + 1 == DEPTH else None
        h, saved[l], weights[l], landed = _layer_forward(h, gains(l), conv_w_full[l], tables, weights[l], inproj_comm,
                                                         gather(("gu",), l), mix_comm, ffn_comm, target)
        if ffn_comm is not None:
            weights[l + 1] = dict(zip(ffn_comm.kinds, landed))
    dx, local_loss = h

    parts = {kind: [None] * DEPTH for kind in WEIGHT_KINDS}
    small_grads = [None] * DEPTH
    g_in_above = None
    exchange = _GradientExchange()
    for l in reversed(range(DEPTH)):
        ffn_comm = _Scatter(("in",), (g_in_above,)) if g_in_above is not None else None
        last = (jnp.concatenate(small_grads[1:], axis=0), local_loss) if l == 0 else None
        dx, g_in_above, small_grads[l], landed, rest, gathered = _layer_backward(
            dx, saved[l], gains(l), conv_w_full[l], tables, weights[l], ffn_comm, exchange, last)
        if ffn_comm is not None:
            parts["in"][l + 1] = landed[0]
        for kind, part in rest.items():
            parts[kind][l] = part

    parts["in"][0] = g_in_above
    tiles = {"in": 512, "out": 128, "gu": 352, "dn": 176}
    big = {kind: _adamw_layers(parts[kind], big_w[kind], big_m[kind], big_v[kind], tiles[kind], "adamw_" + kind)[0]
           for kind in WEIGHT_KINDS}

    small_m = (m_pre_mix_norm, m_post_mix_norm, m_pre_ffn_norm, m_post_ffn_norm, m_attn_out_norm, m_conv_out_norm)
    small_v = (v_pre_mix_norm, v_post_mix_norm, v_pre_ffn_norm, v_post_ffn_norm, v_attn_out_norm, v_conv_out_norm)
    per_gain, taps_grad_full, loss = _adamw_small(gathered, [all_gains[name] for name in GAIN_NAMES], small_m, small_v)
    small = [{name: per_gain[k][o] for k, name in enumerate(GAIN_NAMES)} for o in range(4)]
    loss = loss[0, 0]
    taps_grad = lax.dynamic_slice_in_dim(taps_grad_full, me * conv_channels, conv_channels, axis=2)

    taps = [taps_grad, *_adamw_plain(taps_grad, conv_w, m_conv_w, v_conv_w, "adamw_taps")]

    def leaves(o):
        s = small[o]
        return (s["pre_mix_norm"], big["in"][o], taps[o], s["attn_out_norm"], s["conv_out_norm"], big["out"][o], s["post_mix_norm"],
                s["pre_ffn_norm"], hidden_major(big["gu"][o]), big["dn"][o], s["post_ffn_norm"])

    return (loss, dx[None], *leaves(0), *leaves(1), *leaves(2), *leaves(3))
```

```python
import math

import jax
import jax.numpy as jnp
from jax import lax
from jax.experimental import pallas as pl
from jax.experimental.pallas import tpu as pltpu

F32 = jnp.float32
BF16 = jnp.bfloat16
MESH = pl.DeviceIdType.MESH

SEQ = 4096
D_MODEL = 1024
DEPTH = 4
N_DEV = 8
ATTN_WIDTH = 512
IN_PROJ_WIDTH = 3072
FFN_HIDDEN = 2816
FFN_BLOCK = 2 * FFN_HIDDEN // N_DEV
W_IN_BLOCK = IN_PROJ_WIDTH // N_DEV
W_OUT_BLOCK = D_MODEL // N_DEV
W_DOWN_BLOCK = FFN_HIDDEN // N_DEV
HEAD_DIM = 64
ROPE_DIM = 16
ROPE_THETA = 500000.0
DILATIONS = (1, 4, 16)
HALF_WINDOW = 64
RMS_EPS = 1e-6
NEG_INF = -1e30
LANES = 128
Q_BLOCK = 128
K_WINDOW = Q_BLOCK + 2 * HALF_WINDOW
PERM_CHUNK = 256
ROW_TILE = 512
FFN_TILE = 256
WGRAD_COLS = 512
ADAM_LR, ADAM_B1, ADAM_B2, ADAM_EPS, ADAM_WD, ADAM_STEP = 0.001, 0.9, 0.999, 1e-08, 0.01, 10
MIB = 1024 * 1024
PINNED_BYTES = 256 * 1024
PREFETCH_DMA_THREAD = 1

WEIGHT_KINDS = ("in", "out", "gu", "dn")
FULL_SHAPES = {"in": (D_MODEL, IN_PROJ_WIDTH), "out": (D_MODEL, D_MODEL), "gu": (N_DEV, FFN_BLOCK, D_MODEL), "dn": (FFN_HIDDEN, D_MODEL)}
SHARD_SHAPES = {"in": (D_MODEL, W_IN_BLOCK), "out": (W_OUT_BLOCK, D_MODEL), "gu": (FFN_BLOCK, D_MODEL), "dn": (W_DOWN_BLOCK, D_MODEL)}
ANY = pl.BlockSpec(memory_space=pl.ANY)


def _sds(shape, dtype):
    return jax.ShapeDtypeStruct(shape, dtype)


def _rows(width, tile=ROW_TILE):
    return pl.BlockSpec((tile, width), lambda i: (i, 0))


def _frows(width):
    return _rows(width, FFN_TILE)


def _cols(height, tile):
    return pl.BlockSpec((height, tile), lambda i: (0, i))


def _const(shape):
    return pl.BlockSpec(shape, lambda i: (0,) * len(shape))


def _const1(shape):
    return pl.BlockSpec(shape, lambda i: (0,) * len(shape), pipeline_mode=pl.Buffered(1))


HALO_ROWS = 16


def _halo_prev(width, tile=ROW_TILE):
    return pl.BlockSpec((HALO_ROWS, width), lambda i: (jnp.maximum(i * (tile // HALO_ROWS) - 1, 0), 0))


def _halo_next(width, tile=ROW_TILE):
    return pl.BlockSpec((HALO_ROWS, width), lambda i: (jnp.minimum((i + 1) * (tile // HALO_ROWS), SEQ // HALO_ROWS - 1), 0))


def _rms(x):
    r = lax.rsqrt(jnp.mean(x * x, axis=-1, keepdims=True) + RMS_EPS)
    return x * r, r


def _rms_bwd(dn, y, r):
    return r * (dn - y * jnp.mean(dn * y, axis=-1, keepdims=True))


def _dot(a, b):
    return jnp.dot(a, b, preferred_element_type=F32)


def _dot_nt(a, b):
    return lax.dot_general(a, b, (((1,), (1,)), ((), ())), preferred_element_type=F32)


def _dot_tn(a, b):
    return lax.dot_general(a, b, (((0,), (0,)), ((), ())), preferred_element_type=F32)


def _place():
    return lax.axis_index("x"), lax.axis_index("y"), lax.axis_index("c")


def _block_of(px, py, pc):
    return 4 * px + 2 * py + pc


def _weight_block(ref, kind, blk):
    if kind == "in":
        return ref.at[:, pl.ds(blk * W_IN_BLOCK, W_IN_BLOCK)]
    if kind == "out":
        return ref.at[pl.ds(blk * W_OUT_BLOCK, W_OUT_BLOCK), :]
    if kind == "gu":
        return ref.at[blk]
    return ref.at[pl.ds(blk * W_DOWN_BLOCK, W_DOWN_BLOCK), :]


def _dma_semaphores(n):
    return [pltpu.SemaphoreType.DMA((n, 7)), pltpu.SemaphoreType.DMA((n, 7)), pltpu.SemaphoreType.DMA((n,))]


class _Gather:
    def __init__(self, kinds, shards, layer):
        self.kinds, self.operands, self.layer = tuple(kinds), list(shards), layer
        self.tag = "gather_" + "_".join(kinds)
        self.out_shape = [_sds(FULL_SHAPES[k], BF16) for k in kinds]
        self.scratch = _dma_semaphores(len(kinds))

    def _parties(self):
        x, y, c = _place()
        return (x, y, c), (x, y, 1 - c), [(1 - x, y), (x, 1 - y), (1 - x, 1 - y)], c

    def _copy(self, outs, sems, w, k, block, to, src=None):
        dst = _weight_block(outs[w], self.kinds[w], _block_of(*block))
        return pltpu.make_async_remote_copy(src_ref=dst if src is None else src, dst_ref=dst, send_sem=sems[0].at[w, k],
                                            recv_sem=sems[1].at[w, k], device_id=to, device_id_type=MESH)

    def _own(self, srcs, outs, sems, w, me):
        return pltpu.make_async_copy(srcs[w].at[self.layer], _weight_block(outs[w], self.kinds[w], _block_of(*me)), sems[2].at[w])

    def _first(self, srcs, outs, sems, w):
        me, sibling, chips, c = self._parties()
        shard = srcs[w].at[self.layer]
        return [self._copy(outs, sems, w, 0, me, sibling, src=shard)] + [
            self._copy(outs, sems, w, 1 + j, me, (*chip, c), src=shard) for j, chip in enumerate(chips)]

    def start(self, srcs, outs, sems):
        me = self._parties()[0]
        for w in range(len(self.kinds)):
            self._own(srcs, outs, sems, w, me).start()
            for cp in self._first(srcs, outs, sems, w):
                cp.start()

    def forward(self, srcs, outs, sems):
        me, sibling, chips, c = self._parties()
        for j, chip in enumerate(chips):
            for w in range(len(self.kinds)):
                self._copy(outs, sems, w, 1 + j, (*chip, c), me).wait_recv()
                self._copy(outs, sems, w, 4 + j, (*chip, c), sibling).start()

    def finish(self, srcs, outs, sems):
        me, sibling, chips, c = self._parties()
        for w in range(len(self.kinds)):
            self._copy(outs, sems, w, 0, sibling, me).wait_recv()
            for j, chip in enumerate(chips):
                self._copy(outs, sems, w, 4 + j, (*chip, 1 - c), me).wait_recv()
        for w in range(len(self.kinds)):
            for cp in self._first(srcs, outs, sems, w):
                cp.wait_send()
            for j, chip in enumerate(chips):
                self._copy(outs, sems, w, 4 + j, (*chip, c), sibling).wait_send()
            self._own(srcs, outs, sems, w, me).wait()


def _peers(x, y, c):
    return [(x ^ a, y ^ b, c ^ e) for a in (0, 1) for b in (0, 1) for e in (0, 1) if (a, b, e) != (0, 0, 0)]


class _Scatter:
    def __init__(self, kinds, grads):
        self.kinds, self.operands = tuple(kinds), list(grads)
        self.tag = "scatter_" + "_".join(kinds)
        self.out_shape = [_sds((N_DEV,) + SHARD_SHAPES[k], BF16) for k in kinds]
        self.scratch = _dma_semaphores(len(kinds))

    def _copies(self, srcs, outs, sems):
        x, y, c = _place()
        me = _block_of(x, y, c)
        copies = []
        for w, kind in enumerate(self.kinds):
            copies.append(pltpu.make_async_copy(_weight_block(srcs[w], kind, me), outs[w].at[me], sems[2].at[w]))
            for k, peer in enumerate(_peers(x, y, c)):
                copies.append(pltpu.make_async_remote_copy(
                    src_ref=_weight_block(srcs[w], kind, _block_of(*peer)), dst_ref=outs[w].at[me],
                    send_sem=sems[0].at[w, k], recv_sem=sems[1].at[w, k], device_id=peer, device_id_type=MESH))
        return copies

    def start(self, srcs, outs, sems):
        for cp in self._copies(srcs, outs, sems):
            cp.start()

    def forward(self, srcs, outs, sems):
        pass

    def finish(self, srcs, outs, sems):
        for cp in self._copies(srcs, outs, sems):
            cp.wait()


N_CHIPS = N_DEV // 2


class _ScatterChips:
    def __init__(self, kinds, blocks):
        self.kinds, self.operands = tuple(kinds), list(blocks)
        self.tag = "scatter_chips_" + "_".join(kinds)
        self.out_shape = [_sds((N_CHIPS,) + SHARD_SHAPES[k], BF16) for k in kinds]
        self.scratch = _dma_semaphores(len(kinds))

    def _copies(self, srcs, outs, sems):
        x, y, c = _place()
        mine = 2 * x + y
        copies = []
        for w in range(len(self.kinds)):
            copies.append(pltpu.make_async_copy(srcs[w].at[mine], outs[w].at[mine], sems[2].at[w]))
            for j, (px, py) in enumerate([(1 - x, y), (x, 1 - y), (1 - x, 1 - y)]):
                copies.append(pltpu.make_async_remote_copy(
                    src_ref=srcs[w].at[2 * px + py], dst_ref=outs[w].at[mine], send_sem=sems[0].at[w, j], recv_sem=sems[1].at[w, j],
                    device_id=(px, py, c), device_id_type=MESH))
        return copies

    def start(self, srcs, outs, sems):
        for cp in self._copies(srcs, outs, sems):
            cp.start()

    def forward(self, srcs, outs, sems):
        pass

    def finish(self, srcs, outs, sems):
        for cp in self._copies(srcs, outs, sems):
            cp.wait()


def _in_hbm(a):
    return pltpu.with_memory_space_constraint(a, pltpu.HBM) if a.size * a.dtype.itemsize >= PINNED_BYTES else a


def _out_hbm(s):
    return pltpu.HBM(s.shape, s.dtype) if math.prod(s.shape) * jnp.dtype(s.dtype).itemsize >= PINNED_BYTES else s


def _call(body, args, *, grid, in_specs, out_specs, out_shape, scratch_shapes=(), vmem_mib=None, name, comm=None):
    kwargs = {} if vmem_mib is None else dict(compiler_params=pltpu.CompilerParams(vmem_limit_bytes=vmem_mib * MIB))
    in_specs, out_specs, out_shape, scratch_shapes = list(in_specs), list(out_specs), list(out_shape), list(scratch_shapes)
    args = [_in_hbm(a) for a in args]
    out_shape = [_out_hbm(s) for s in out_shape]
    if comm is None:
        res = pl.pallas_call(body, grid=grid, in_specs=in_specs, out_specs=out_specs, out_shape=out_shape,
                             scratch_shapes=scratch_shapes, name=name, **kwargs)(*args)
        return list(res), None
    n_in, n_out, n_scr = len(in_specs), len(out_specs), len(scratch_shapes)
    c_in, c_out = len(comm.operands), len(comm.out_shape)
    last = math.prod(grid) - 1

    def carried(*refs):
        cuts = [n_in, c_in, n_out, c_out, n_scr]
        parts, at = [], 0
        for n in cuts:
            parts.append(refs[at:at + n])
            at += n
        ins, c_ins, outs, c_outs, scr = parts
        sems = refs[at:]
        step = pl.program_id(0)
        for axis in range(1, len(grid)):
            step = step * grid[axis] + pl.program_id(axis)

        @pl.when(step == 0)
        def _():
            comm.start(c_ins, c_outs, sems)

        @pl.when(step == last)
        def _():
            comm.forward(c_ins, c_outs, sems)

        body(*ins, *outs, *scr)

        @pl.when(step == last)
        def _():
            comm.finish(c_ins, c_outs, sems)

    res = pl.pallas_call(carried, grid=grid, in_specs=in_specs + [ANY] * c_in, out_specs=out_specs + [ANY] * c_out,
                         out_shape=out_shape + [_out_hbm(s) for s in comm.out_shape], scratch_shapes=scratch_shapes + comm.scratch,
                         name=name + "_" + comm.tag, **kwargs)(*args, *[_in_hbm(a) for a in comm.operands])
    return list(res[:n_out]), list(res[n_out:])


class _GatherSmall:
    def __init__(self, v, tag):
        self.operands, self.tag = [v], tag
        self.out_shape = [_sds((N_DEV,) + v.shape, F32)]
        self.scratch = _dma_semaphores(1)

    def _copies(self, srcs, outs, sems):
        x, y, c = _place()
        me = _block_of(x, y, c)
        copies = [pltpu.make_async_copy(srcs[0], outs[0].at[me], sems[2].at[0])]
        for k, peer in enumerate(_peers(x, y, c)):
            copies.append(pltpu.make_async_remote_copy(src_ref=srcs[0], dst_ref=outs[0].at[me], send_sem=sems[0].at[0, k],
                                                       recv_sem=sems[1].at[0, k], device_id=peer, device_id_type=MESH))
        return copies

    def start(self, srcs, outs, sems):
        for cp in self._copies(srcs, outs, sems):
            cp.start()

    def forward(self, srcs, outs, sems):
        pass

    def finish(self, srcs, outs, sems):
        for cp in self._copies(srcs, outs, sems):
            cp.wait()


class _Together:
    def __init__(self, *comms):
        self.comms = comms
        self.operands = [a for cm in comms for a in cm.operands]
        self.out_shape = [s for cm in comms for s in cm.out_shape]
        self.scratch = [s for cm in comms for s in cm.scratch]
        self.tag = "_".join(cm.tag for cm in comms)
        self.kinds = tuple(kind for cm in comms for kind in getattr(cm, "kinds", ()))

    def _each(self, srcs, outs, sems):
        a = o = s = 0
        for cm in self.comms:
            na, no, ns = len(cm.operands), len(cm.out_shape), len(cm.scratch)
            yield cm, srcs[a:a + na], outs[o:o + no], sems[s:s + ns]
            a, o, s = a + na, o + no, s + ns

    def start(self, srcs, outs, sems):
        for cm, *refs in self._each(srcs, outs, sems):
            cm.start(*refs)

    def forward(self, srcs, outs, sems):
        for cm, *refs in self._each(srcs, outs, sems):
            cm.forward(*refs)

    def finish(self, srcs, outs, sems):
        for cm, *refs in self._each(srcs, outs, sems):
            cm.finish(*refs)


def _rope_tables(pos_col, freq_row, casts, comm=None):
    steps = SEQ // ROW_TILE
    assert steps == 2 * DEPTH

    def body(p_ref, f_ref, *refs):
        wide, (c_ref, sa_ref, sb_ref), narrow = refs[:len(casts)], refs[len(casts):len(casts) + 3], refs[len(casts) + 3:]
        ang = p_ref[...].astype(F32) * f_ref[...]
        lane = lax.broadcasted_iota(jnp.int32, ang.shape, 1) % HEAD_DIM
        cos, sin = jnp.cos(ang), jnp.sin(ang)
        c_ref[...] = jnp.where(lane < ROPE_DIM, cos, 1.0)
        sa_ref[...] = jnp.where(lane < ROPE_DIM // 2, -sin, 0.0)
        sb_ref[...] = jnp.where((lane >= ROPE_DIM // 2) & (lane < ROPE_DIM), sin, 0.0)
        for src, dst in zip(wide, narrow):
            dst[...] = src[...].astype(BF16)

    halves = [pl.BlockSpec((1, a.shape[1] // 2, a.shape[2]), lambda i: (i // 2, i % 2, 0)) for a in casts]
    return _call(body, (pos_col, freq_row, *casts), grid=(steps,), in_specs=[_rows(1), _const((1, LANES))] + halves,
                 out_specs=[_rows(LANES)] * 3 + halves,
                 out_shape=[_sds((SEQ, LANES), F32)] * 3 + [_sds(a.shape, BF16) for a in casts], name="rope_tables", comm=comm)


def _rotate(t, c, sa, sb):
    parts = []
    for g in range(ATTN_WIDTH // LANES):
        tg = t[:, g * LANES:(g + 1) * LANES]
        parts.append(tg * c + pltpu.roll(tg, LANES - 8, axis=1) * sa + pltpu.roll(tg, 8, axis=1) * sb)
    return jnp.concatenate(parts, axis=1)


def _rotate_transposed(dt, c, sa, sb):
    parts = []
    for g in range(ATTN_WIDTH // LANES):
        dg = dt[:, g * LANES:(g + 1) * LANES]
        parts.append(dg * c + pltpu.roll(dg * sa, 8, axis=1) + pltpu.roll(dg * sb, LANES - 8, axis=1))
    return jnp.concatenate(parts, axis=1)


def _inproj_fwd(x, g_pre, w_in, tc, tsa, tsb, comm=None):
    def body(x_ref, g_ref, w_ref, c_ref, sa_ref, sb_ref, q_ref, k_ref, v_ref, u_ref, b_ref, cc_ref):
        y, _ = _rms(x_ref[...])
        h = (y * g_ref[...]).astype(BF16)

        def proj(n):
            return _dot(h, w_ref[:, n * ATTN_WIDTH:(n + 1) * ATTN_WIDTH])

        c, sa, sb = c_ref[...], sa_ref[...], sb_ref[...]
        q_ref[...] = (_rotate(proj(0), c, sa, sb) * (HEAD_DIM ** -0.5)).astype(BF16)
        k_ref[...] = _rotate(proj(1), c, sa, sb).astype(BF16)
        v_ref[...] = proj(2).astype(BF16)
        u_ref[...] = proj(3).astype(BF16)
        b_ref[...] = proj(4).astype(BF16)
        cc_ref[...] = proj(5).astype(BF16)

    a = ATTN_WIDTH
    return _call(
        body, (x, g_pre, w_in, tc, tsa, tsb), grid=(SEQ // ROW_TILE,),
        in_specs=[_rows(D_MODEL), _const((1, D_MODEL)), _const1((D_MODEL, IN_PROJ_WIDTH)), _rows(LANES), _rows(LANES), _rows(LANES)],
        out_specs=[_rows(a)] * 6, out_shape=[_sds((SEQ, a), BF16)] * 6,
        vmem_mib=40, name="inproj_fwd", comm=comm)


def _head_masks():
    lane = lax.broadcasted_iota(jnp.int32, (1, LANES), 1)
    first = lane < HEAD_DIM
    return first, first.astype(F32), 1.0 - first.astype(F32)


def _perm_chunks(dil):
    length = SEQ // dil
    out = []
    for r in range(dil):
        for c0 in range(0, length, PERM_CHUNK):
            chunk = (r * length + c0) // PERM_CHUNK
            rows = pl.ds(c0, PERM_CHUNK) if dil == 1 else pl.ds(r + dil * c0, PERM_CHUNK, stride=dil)
            out.append((chunk, rows))
    return out


def _chunk(c, offset=0):
    return pl.ds(offset + c * PERM_CHUNK, PERM_CHUNK)


def _write_band_bias(bias_ref):
    rr = lax.broadcasted_iota(jnp.int32, (Q_BLOCK, K_WINDOW), 0)
    cc = lax.broadcasted_iota(jnp.int32, (Q_BLOCK, K_WINDOW), 1)
    band = (cc >= rr) & (cc - rr <= 2 * HALF_WINDOW)
    bias_ref[0] = jnp.where(band, 0.0, NEG_INF)
    bias_ref[1] = jnp.where(band & (cc >= HALF_WINDOW), 0.0, NEG_INF)
    bias_ref[2] = jnp.where(band & (cc < Q_BLOCK + HALF_WINDOW), 0.0, NEG_INF)


def _band_bias_index(m0, length):
    return jnp.where(m0 % length == 0, 1, 0) + jnp.where((m0 + Q_BLOCK) % length == 0, 2, 0)


def _zero_key_padding(bufs):
    pad = jnp.zeros((HALF_WINDOW, LANES), BF16)
    for buf in bufs:
        buf[pl.ds(0, HALF_WINDOW), :] = pad
        buf[pl.ds(SEQ + HALF_WINDOW, HALF_WINDOW), :] = pad


def _attn_fwd(q, k, v, comm=None):
    group = 8

    def body(q_ref, k_ref, v_ref, o_ref, lse_ref, q32, k32, v32, qa, qb, kp, vp, accp, mlp,
             acc0, acc1, acc2, ml0, ml1, ml2, bias_ref):
        first, mask_a, mask_b = _head_masks()
        low = lax.broadcasted_iota(jnp.int32, (1, LANES), 1) % HEAD_DIM < HEAD_DIM // 2
        _zero_key_padding((kp, vp))
        _write_band_bias(bias_ref)
        q32[...] = q_ref[...].astype(F32)
        k32[...] = k_ref[...].astype(F32)
        v32[...] = v_ref[...].astype(F32)
        natural = ((acc0, ml0), (acc1, ml1), (acc2, ml2))

        for branch, dil in enumerate(DILATIONS):
            length = SEQ // dil
            assert length >= 2 * Q_BLOCK
            chunks = _perm_chunks(dil)
            for c, rows in chunks:
                val = q32[rows, :]
                qa[_chunk(c), :] = (val * mask_a).astype(BF16)
                qb[_chunk(c), :] = (val * mask_b).astype(BF16)
                kp[_chunk(c, HALF_WINDOW), :] = k32[rows, :].astype(BF16)
                vp[_chunk(c, HALF_WINDOW), :] = v32[rows, :].astype(BF16)
            acc_dst, ml_dst = natural[branch] if dil == 1 else (accp, mlp)

            def blocks(i, carry, length=length, acc_dst=acc_dst, ml_dst=ml_dst):
                base = pl.multiple_of(i * (group * Q_BLOCK), group * Q_BLOCK)
                starts = [base + g * Q_BLOCK for g in range(group)]
                scores = [[_dot_nt(qx[pl.ds(m0, Q_BLOCK), :], kp[pl.ds(m0, K_WINDOW), :]) for qx in (qa, qb)] for m0 in starts]
                probs = []
                for m0, pair in zip(starts, scores):
                    bias = bias_ref[_band_bias_index(m0, length)]
                    stats, ps = [], []
                    for s in pair:
                        s = s + bias
                        m = jnp.max(s, axis=1, keepdims=True)
                        p = jnp.exp(s - m)
                        stats.append(jnp.where(low, m, jnp.sum(p, axis=1, keepdims=True)))
                        ps.append(p.astype(BF16))
                    ml_dst[pl.ds(m0, Q_BLOCK), :] = jnp.where(first, stats[0], stats[1])
                    probs.append(ps)
                for m0, ps in zip(starts, probs):
                    vw = vp[pl.ds(m0, K_WINDOW), :]
                    acc_dst[pl.ds(m0, Q_BLOCK), :] = jnp.where(first, _dot(ps[0], vw), _dot(ps[1], vw))
                return carry

            lax.fori_loop(0, SEQ // (group * Q_BLOCK), blocks, 0)

            if dil > 1:
                for c, rows in chunks:
                    natural[branch][0][rows, :] = accp[_chunk(c), :]
                    natural[branch][1][rows, :] = mlp[_chunk(c), :]

        for c in range(SEQ // PERM_CHUNK):
            def up(t):
                return jnp.where(low, t, pltpu.roll(t, HEAD_DIM // 2, axis=1))

            packed = [ml[_chunk(c), :] for _, ml in natural]
            top = jnp.maximum(jnp.maximum(packed[0], packed[1]), packed[2])
            m_all = up(top)
            es = [up(jnp.exp(ml - top)) for ml in packed]
            z = packed[0] * es[0] + packed[1] * es[1] + packed[2] * es[2]
            z = jnp.where(low, pltpu.roll(z, LANES - HEAD_DIM // 2, axis=1), z)
            num = natural[0][0][_chunk(c), :] * es[0] + natural[1][0][_chunk(c), :] * es[1] + natural[2][0][_chunk(c), :] * es[2]
            o_ref[_chunk(c), :] = num / z
            lse_ref[_chunk(c), :] = m_all + jnp.log(z)

    col = pl.BlockSpec((SEQ, LANES), lambda h: (0, h))
    padded = SEQ + 2 * HALF_WINDOW
    return _call(
        body, (q, k, v), grid=(ATTN_WIDTH // LANES,), in_specs=[col] * 3, out_specs=[col] * 2,
        out_shape=[_sds((SEQ, ATTN_WIDTH), F32)] * 2,
        scratch_shapes=[pltpu.VMEM((SEQ, LANES), F32)] * 3 + [pltpu.VMEM((SEQ, LANES), BF16)] * 2
        + [pltpu.VMEM((padded, LANES), BF16)] * 2 + [pltpu.VMEM((SEQ, LANES), F32)] * 8
        + [pltpu.VMEM((3, Q_BLOCK, K_WINDOW), F32)],
        vmem_mib=52, name="attn_fwd", comm=comm)


def _shifted(t, before, after, i):
    tile = t.shape[0]
    row = lax.broadcasted_iota(jnp.int32, (tile, 1), 0)
    before = jnp.where(i > 0, before, 0.0)
    after = jnp.where(i < SEQ // tile - 1, after, 0.0)
    return (jnp.where(row == 0, before, pltpu.roll(t, 1, axis=0)),
            jnp.where(row == tile - 1, after, pltpu.roll(t, tile - 1, axis=0)))


def _last_row(ref):
    return ref[HALO_ROWS - 1:HALO_ROWS, :].astype(F32)


def _first_row(ref):
    return ref[0:1, :].astype(F32)


def _conv_parts(u, c, u_prev, c_prev, u_next, c_next, cw, i):
    t = c * u
    t_prev, t_next = _shifted(t, _last_row(c_prev) * _last_row(u_prev), _first_row(c_next) * _first_row(u_next), i)
    s = cw[0:1, :] * t_prev + cw[1:2, :] * t + cw[2:3, :] * t_next
    return t, t_prev, t_next, s


def _mix_fwd(attn, u, b, c, conv_w, g_attn, g_conv, w_out, x, g_post, comm=None):
    def body(a_ref, u_ref, b_ref, c_ref, up_ref, cp_ref, un_ref, cn_ref, cw_ref, ga_ref, gc_ref, w_ref, x_ref, gp_ref,
             x1_ref, mg_ref, mix_ref):
        i = pl.program_id(0)
        _, _, _, s = _conv_parts(u_ref[...].astype(F32), c_ref[...].astype(F32), up_ref, cp_ref, un_ref, cn_ref, cw_ref[...], i)
        ya, _ = _rms(a_ref[...])
        yc, _ = _rms(b_ref[...].astype(F32) * s)
        merged = jnp.concatenate([ya * ga_ref[...], yc * gc_ref[...]], axis=1).astype(BF16)
        mix = _dot(merged, w_ref[...])
        ym, _ = _rms(mix)
        mg_ref[...] = merged.T
        mix_ref[...] = mix.astype(BF16)
        x1_ref[...] = x_ref[...] + ym * gp_ref[...]

    a = ATTN_WIDTH
    return _call(
        body, (attn, u, b, c, u, c, u, c, conv_w, g_attn, g_conv, w_out, x, g_post), grid=(SEQ // ROW_TILE,),
        in_specs=[_rows(a)] * 4 + [_halo_prev(a)] * 2 + [_halo_next(a)] * 2
        + [_const((3, a)), _const((1, a)), _const((1, a)), _const1((D_MODEL, D_MODEL)), _rows(D_MODEL), _const((1, D_MODEL))],
        out_specs=[_rows(D_MODEL), _cols(D_MODEL, ROW_TILE), _rows(D_MODEL)],
        out_shape=[_sds((SEQ, D_MODEL), F32), _sds((D_MODEL, SEQ), BF16), _sds((SEQ, D_MODEL), BF16)],
        vmem_mib=40, name="mix_fwd", comm=comm)


def _gu_spec():
    return pl.BlockSpec((N_DEV, FFN_TILE, FFN_BLOCK), lambda i: (0, i, 0))


def _ffn_fwd(x1, g_pre, w_gu, w_dn, g_post, comm=None, target=None):
    n_tiles = SEQ // ROW_TILE

    def body(*refs):
        if target is None:
            x_ref, g_ref, wgu_ref, wdn_ref, gp_ref, x2_ref, gu_ref, f_ref = refs
        else:
            x_ref, g_ref, wgu_ref, wdn_ref, gp_ref, t_ref, x2_ref, gu_ref, f_ref, loss_ref, acc = refs
        x1 = x_ref[...]
        y, _ = _rms(x1)
        h = (y * g_ref[...]).astype(BF16)
        f = jnp.zeros((ROW_TILE, D_MODEL), F32)

        def gate_up(j):
            return _dot_nt(h, wgu_ref[j]), _dot_nt(h, wgu_ref[j + N_DEV // 2])

        ahead = gate_up(0)
        for j in range(N_DEV // 2):
            gate, up = ahead
            if j + 1 < N_DEV // 2:
                ahead = gate_up(j + 1)
            gu_ref[j] = gate.astype(BF16)
            gu_ref[j + N_DEV // 2] = up.astype(BF16)
            act = (gate * jax.nn.sigmoid(gate) * up).astype(BF16)
            f = f + _dot(act, wdn_ref[pl.ds(j * FFN_BLOCK, FFN_BLOCK), :])
        yf, _ = _rms(f)
        f_ref[...] = f
        x2 = x1 + yf * gp_ref[...]
        if target is None:
            x2_ref[...] = x2
        else:
            i = pl.program_id(0)
            err = x2 - t_ref[...]
            x2_ref[...] = err * (1.0 / D_MODEL)

            @pl.when(i == 0)
            def _():
                acc[...] = jnp.zeros_like(acc)

            acc[...] += jnp.sum(err * err, axis=0, keepdims=True)

            @pl.when(i == n_tiles - 1)
            def _():
                loss_ref[...] = jnp.sum(acc[...], axis=1, keepdims=True) * (0.5 / D_MODEL)

    with_loss = target is not None
    return _call(
        body, (x1, g_pre, w_gu, w_dn, g_post) + ((target,) if with_loss else ()), grid=(n_tiles,),
        in_specs=[_rows(D_MODEL), _const((1, D_MODEL)), _const1((N_DEV, FFN_BLOCK, D_MODEL)), _const1((FFN_HIDDEN, D_MODEL)),
                  _const((1, D_MODEL))] + ([_rows(D_MODEL)] if with_loss else []),
        out_specs=[_rows(D_MODEL), pl.BlockSpec((N_DEV, ROW_TILE, FFN_BLOCK), lambda i: (0, i, 0)), _rows(D_MODEL)]
        + ([_const((1, 1))] if with_loss else []),
        out_shape=[_sds((SEQ, D_MODEL), F32), _sds((N_DEV, SEQ, FFN_BLOCK), BF16), _sds((SEQ, D_MODEL), F32)]
        + ([_sds((1, 1), F32)] if with_loss else []),
        scratch_shapes=[pltpu.VMEM((1, D_MODEL), F32)] if with_loss else [],
        vmem_mib=58, name="ffn_fwd_loss" if with_loss else "ffn_fwd", comm=comm)


def _accumulate(pairs, i):
    @pl.when(i == 0)
    def _():
        for ref, value in pairs:
            ref[...] = value

    @pl.when(i > 0)
    def _():
        for ref, value in pairs:
            ref[...] += value


def _colsum(v):
    return jnp.sum(v, axis=0, keepdims=True)


def _ffn_bwd(dx2, f, x1, gu, w_gu, w_dn, g_post, g_pre, comm=None):
    half = N_DEV // 2

    def body(dx2_ref, f_ref, x1_ref, gu_ref, wgu_ref, wdn_ref, gpost_ref, gpre_ref,
             dx1_ref, df_ref, act_ref, dgu_ref, h_ref, dgpost_ref, dgpre_ref):
        i = pl.program_id(0)
        dx2 = dx2_ref[...]
        yf, rf = _rms(f_ref[...])
        dg_post = _colsum(dx2 * yf)
        df = _rms_bwd(dx2 * gpost_ref[...], yf, rf).astype(BF16)
        df_ref[...] = df
        dh = jnp.zeros((FFN_TILE, D_MODEL), F32)

        def d_act(j):
            return _dot_nt(df, wdn_ref[pl.ds(j * FFN_BLOCK, FFN_BLOCK), :])

        ahead = d_act(0)
        for j in range(half):
            dact = ahead
            if j + 1 < half:
                ahead = d_act(j + 1)
            gate = gu_ref[j].astype(F32)
            up = gu_ref[j + half].astype(F32)
            sig = jax.nn.sigmoid(gate)
            silu = gate * sig
            act_ref[j] = (silu * up).astype(BF16)
            dgate = (dact * up * (sig * (1.0 + gate * (1.0 - sig)))).astype(BF16)
            dup = (dact * silu).astype(BF16)
            dgu_ref[j] = dgate
            dgu_ref[j + half] = dup
            dh = dh + _dot(dgate, wgu_ref[j]) + _dot(dup, wgu_ref[j + half])
        y1, r1 = _rms(x1_ref[...])
        h_ref[...] = (y1 * gpre_ref[...]).astype(BF16)
        dx1_ref[...] = dx2 + _rms_bwd(dh * gpre_ref[...], y1, r1)
        _accumulate([(dgpost_ref, dg_post), (dgpre_ref, _colsum(dh * y1))], i)

    act_spec = pl.BlockSpec((half, FFN_TILE, FFN_BLOCK), lambda i: (0, i, 0))
    return _call(
        body, (dx2, f, x1, gu, w_gu, w_dn, g_post, g_pre), grid=(SEQ // FFN_TILE,),
        in_specs=[_frows(D_MODEL)] * 3 + [_gu_spec(), _const1((N_DEV, FFN_BLOCK, D_MODEL)), _const1((FFN_HIDDEN, D_MODEL)),
                                          _const((1, D_MODEL)), _const((1, D_MODEL))],
        out_specs=[_frows(D_MODEL), _frows(D_MODEL), act_spec, _gu_spec(), _frows(D_MODEL), _const((1, D_MODEL)), _const((1, D_MODEL))],
        out_shape=[_sds((SEQ, D_MODEL), F32), _sds((SEQ, D_MODEL), BF16), _sds((half, SEQ, FFN_BLOCK), BF16),
                   _sds((N_DEV, SEQ, FFN_BLOCK), BF16), _sds((SEQ, D_MODEL), BF16), _sds((1, D_MODEL), F32), _sds((1, D_MODEL), F32)],
        vmem_mib=52, name="ffn_bwd", comm=comm)


def _wgrad(a_t, g, name):
    width = g.shape[1]

    def body(a_ref, g_ref, o_ref):
        o_ref[...] = _dot(a_ref[...], g_ref[...]).astype(BF16)

    return _call(body, (a_t, g), grid=(width // WGRAD_COLS,),
                 in_specs=[_const1((D_MODEL, SEQ)), pl.BlockSpec((SEQ, WGRAD_COLS), lambda j: (0, j))],
                 out_specs=[pl.BlockSpec((D_MODEL, WGRAD_COLS), lambda j: (0, j))], out_shape=[_sds((D_MODEL, width), BF16)],
                 vmem_mib=48, name=name)[0][0]


def _wgrad_paired(a, g, kind, to_owners=False, comm=None):
    shard = SHARD_SHAPES[kind]
    per_chip = 2 if kind == "gu" else 1
    n_compute = per_chip * N_CHIPS
    n_steps = n_compute + 1

    def chip_of(turn):
        if not to_owners:
            return turn
        return (2 * lax.axis_index("x") + lax.axis_index("y") + 1 + turn) % N_CHIPS

    def block(t):
        return jnp.minimum(t, n_compute - 1)

    if kind == "in":
        specs = [_const1((D_MODEL, SEQ)), pl.BlockSpec((SEQ, 2 * W_IN_BLOCK), lambda t: (0, chip_of(block(t))))]
    else:
        specs = [pl.BlockSpec((1, SEQ, FFN_BLOCK), lambda t: (block(t), 0, 0)), _const1((SEQ, D_MODEL))]

    def body(a_ref, g_ref, o_ref, *rest):
        if to_owners:
            landing, sendbuf, recvbuf, keepbuf, send_sem, recv_sem, donebuf, owner_send, owner_recv, own_sem = rest
        else:
            sendbuf, recvbuf, keepbuf, send_sem, recv_sem = rest
        t = pl.program_id(0)
        x, y, c = _place()

        def exchange(q):
            return pltpu.make_async_remote_copy(src_ref=sendbuf.at[q], dst_ref=recvbuf.at[q], send_sem=send_sem.at[q],
                                                recv_sem=recv_sem.at[q], device_id=(x, y, 1 - c), device_id_type=MESH)

        my_chip = 2 * x + y

        def to_owner(q):
            return pltpu.make_async_remote_copy(src_ref=donebuf.at[q], dst_ref=landing.at[my_chip], send_sem=owner_send.at[q],
                                                recv_sem=owner_recv.at[my_chip], device_id=(q // 2, q % 2, c), device_id_type=MESH)

        def keep_own():
            return pltpu.make_async_copy(donebuf.at[my_chip], landing.at[my_chip], own_sem)

        @pl.when((t >= per_chip) & (t % per_chip == 0))
        def _():
            q = chip_of(t // per_chip - 1)
            exchange(q).wait_recv()
            done = (keepbuf[...] + recvbuf[q].astype(F32)).astype(BF16)
            o_ref[0] = done
            if to_owners:
                donebuf[q] = done

                @pl.when(q == my_chip)
                def _():
                    keep_own().start()

                @pl.when(q != my_chip)
                def _():
                    to_owner(q).start()

        @pl.when(t < n_compute)
        def _():
            q = chip_of(t // per_chip)
            if kind == "gu":
                r = _dot_tn(a_ref[0], g_ref[...])

                @pl.when(t % 2 == c)
                def _():
                    keepbuf[...] = r

                @pl.when(t % 2 != c)
                def _():
                    sendbuf[q] = r.astype(BF16)
                    exchange(q).start()
            else:
                if kind == "dn":
                    r = _dot_tn(a_ref[0], g_ref[...])
                    lower, upper = r[:W_DOWN_BLOCK], r[W_DOWN_BLOCK:]
                else:
                    r = _dot(a_ref[...], g_ref[...])
                    lower, upper = r[:, :W_IN_BLOCK], r[:, W_IN_BLOCK:]
                keepbuf[...] = jnp.where(c == 0, lower, upper)
                sendbuf[q] = jnp.where(c == 0, upper, lower).astype(BF16)
                exchange(q).start()

        @pl.when(t == n_steps - 1)
        def _():
            for q in range(N_CHIPS):
                exchange(q).wait_send()
            if to_owners:
                keep_own().wait()
                for s in range(N_CHIPS):
                    @pl.when(s != my_chip)
                    def _(s=s):
                        pltpu.make_async_remote_copy(src_ref=donebuf.at[s], dst_ref=landing.at[s], send_sem=owner_send.at[s],
                                                     recv_sem=owner_recv.at[s], device_id=(x, y, c), device_id_type=MESH).wait()

    slots = (N_CHIPS,) + shard
    owners = int(to_owners)
    res, landed = _call(
        body, (a, g), grid=(n_steps,), in_specs=specs,
        out_specs=[pl.BlockSpec((1,) + shard, lambda t: (chip_of(jnp.clip(t // per_chip - 1, 0, N_CHIPS - 1)), 0, 0))] + [ANY] * owners,
        out_shape=[_sds(slots, BF16)] * (1 + owners),
        scratch_shapes=[pltpu.VMEM(slots, BF16), pltpu.VMEM(slots, BF16), pltpu.VMEM(shard, F32),
                        pltpu.SemaphoreType.DMA((N_CHIPS,)), pltpu.SemaphoreType.DMA((N_CHIPS,))]
        + [pltpu.VMEM(slots, BF16), pltpu.SemaphoreType.DMA((N_CHIPS,)), pltpu.SemaphoreType.DMA((N_CHIPS,)), pltpu.SemaphoreType.DMA] * owners,
        vmem_mib=52, name="wgrad_" + kind + "_paired" + "_to_owners" * owners, comm=comm)
    return res[owners] if comm is None else (res[owners], landed)


def _mix_bwd(dx1, mix, attn, u, b, c, conv_w, g_attn, g_conv, g_post, w_out):
    def body(dx1_ref, mix_ref, a_ref, u_ref, b_ref, c_ref, up_ref, cp_ref, un_ref, cn_ref, cw_ref, ga_ref, gc_ref, gp_ref, w_ref,
             dmix_ref, da_ref, ds_ref, db_ref, dgp_ref, dga_ref, dgc_ref):
        i = pl.program_id(0)
        dx1 = dx1_ref[...]
        ym, rm = _rms(mix_ref[...].astype(F32))
        dg_post = _colsum(dx1 * ym)
        dmix = _rms_bwd(dx1 * gp_ref[...], ym, rm).astype(BF16)
        dmix_ref[...] = dmix
        dmerged = _dot_nt(dmix, w_ref[...])
        dna, dnc = dmerged[:, :ATTN_WIDTH], dmerged[:, ATTN_WIDTH:]
        ya, ra = _rms(a_ref[...])
        da_ref[...] = _rms_bwd(dna * ga_ref[...], ya, ra)
        _, _, _, s = _conv_parts(u_ref[...].astype(F32), c_ref[...].astype(F32), up_ref, cp_ref, un_ref, cn_ref, cw_ref[...], i)
        gate_b = b_ref[...].astype(F32)
        yc, rc = _rms(gate_b * s)
        dy = _rms_bwd(dnc * gc_ref[...], yc, rc)
        db_ref[...] = (dy * s).astype(BF16)
        ds_ref[...] = (dy * gate_b).astype(BF16)
        _accumulate([(dgp_ref, dg_post), (dga_ref, _colsum(dna * ya)), (dgc_ref, _colsum(dnc * yc))], i)

    a = ATTN_WIDTH
    return _call(
        body, (dx1, mix, attn, u, b, c, u, c, u, c, conv_w, g_attn, g_conv, g_post, w_out), grid=(SEQ // ROW_TILE,),
        in_specs=[_rows(D_MODEL)] * 2 + [_rows(a)] * 4 + [_halo_prev(a)] * 2 + [_halo_next(a)] * 2
        + [_const((3, a)), _const((1, a)), _const((1, a)), _const((1, D_MODEL)), _const1((D_MODEL, D_MODEL))],
        out_specs=[_rows(D_MODEL)] + [_rows(a)] * 3 + [_const((1, D_MODEL)), _const((1, a)), _const((1, a))],
        out_shape=[_sds((SEQ, D_MODEL), BF16), _sds((SEQ, a), F32), _sds((SEQ, a), BF16), _sds((SEQ, a), BF16),
                   _sds((1, D_MODEL), F32), _sds((1, a), F32), _sds((1, a), F32)],
        vmem_mib=40, name="mix_bwd")[0]


def _attn_bwd(q, k, v, do, o, lse, comm=None):
    group = 8

    n_pairs = ATTN_WIDTH // LANES

    def body(q_hbm, k_hbm, v_hbm, do_hbm, o_hbm, lse_hbm, dq_ref, dk_ref, dv_ref,
             q_ref, k_ref, v_ref, do_ref, o_ref, lse_ref, in_sem,
             q32, k32, v32, qa, qb, doa, dob, kp, vp, lsep, dlp, dnat, dqp, dkp, dvp, bias_ref):
        pair = pl.program_id(0)
        sources = (q_hbm, k_hbm, v_hbm, do_hbm, o_hbm, lse_hbm)
        buffers = (q_ref, k_ref, v_ref, do_ref, o_ref, lse_ref)

        def fetch(i, p):
            return pltpu.make_async_copy(sources[i].at[:, pl.ds(pl.multiple_of(p * LANES, LANES), LANES)], buffers[i], in_sem.at[i])

        def prefetch(*which):
            @pl.when(pair + 1 < n_pairs)
            def _():
                for i in which:
                    fetch(i, pair + 1).start(priority=PREFETCH_DMA_THREAD)

        @pl.when(pair == 0)
        def _():
            for i in range(len(sources)):
                fetch(i, 0).start(priority=PREFETCH_DMA_THREAD)

        for i in range(len(sources)):
            fetch(i, pair).wait()

        first, mask_a, mask_b = _head_masks()
        _zero_key_padding((kp, vp))
        _write_band_bias(bias_ref)
        q32[...] = q_ref[...].astype(F32)
        k32[...] = k_ref[...].astype(F32)
        v32[...] = v_ref[...].astype(F32)
        prefetch(0, 1, 2)
        for c in range(SEQ // PERM_CHUNK):
            prod = do_ref[_chunk(c), :] * o_ref[_chunk(c), :]
            d_a = jnp.sum(prod * mask_a, axis=1, keepdims=True)
            d_b = jnp.sum(prod * mask_b, axis=1, keepdims=True)
            dnat[_chunk(c), :] = jnp.where(first, d_a, d_b)
        prefetch(4)

        for step, dil in enumerate(DILATIONS[1:] + DILATIONS[:1]):
            length = SEQ // dil
            assert length >= 2 * Q_BLOCK
            chunks = _perm_chunks(dil)
            for c, rows in chunks:
                val = q32[rows, :]
                qa[_chunk(c), :] = (val * mask_a).astype(BF16)
                qb[_chunk(c), :] = (val * mask_b).astype(BF16)
                val = do_ref[rows, :]
                doa[_chunk(c), :] = (val * mask_a).astype(BF16)
                dob[_chunk(c), :] = (val * mask_b).astype(BF16)
                kp[_chunk(c, HALF_WINDOW), :] = k32[rows, :].astype(BF16)
                vp[_chunk(c, HALF_WINDOW), :] = v32[rows, :].astype(BF16)
                lsep[_chunk(c), :] = lse_ref[rows, :]
                dlp[_chunk(c), :] = dnat[rows, :]
            if step == len(DILATIONS) - 1:
                prefetch(3, 5)
            zero = jnp.zeros((PERM_CHUNK, LANES), F32)
            for c in range(SEQ // PERM_CHUNK):
                dkp[_chunk(c), :] = zero
                dvp[_chunk(c), :] = zero
            dkp[pl.ds(SEQ, 2 * HALF_WINDOW), :] = zero[:2 * HALF_WINDOW]
            dvp[pl.ds(SEQ, 2 * HALF_WINDOW), :] = zero[:2 * HALF_WINDOW]

            heads = ((qa, doa, 0), (qb, dob, HEAD_DIM))

            def blocks(i, carry, length=length):
                base = pl.multiple_of(i * (group * Q_BLOCK), group * Q_BLOCK)
                starts = [base + g * Q_BLOCK for g in range(group)]
                raw = [[(_dot_nt(qx[pl.ds(m0, Q_BLOCK), :], kp[pl.ds(m0, K_WINDOW), :]),
                         _dot_nt(dox[pl.ds(m0, Q_BLOCK), :], vp[pl.ds(m0, K_WINDOW), :])) for qx, dox, _ in heads]
                       for m0 in starts]
                grads = []
                for m0, pair in zip(starts, raw):
                    bias = bias_ref[_band_bias_index(m0, length)]
                    lse_b, d_b = lsep[pl.ds(m0, Q_BLOCK), :], dlp[pl.ds(m0, Q_BLOCK), :]
                    out = []
                    for (s, dp), (_, _, col) in zip(pair, heads):
                        p = jnp.exp(s + bias - lse_b[:, col:col + 1])
                        out.append(((p * (dp - d_b[:, col:col + 1])).astype(BF16), p.astype(BF16)))
                    grads.append(out)
                for m0, out in zip(starts, grads):
                    qrows, krows = pl.ds(m0, Q_BLOCK), pl.ds(m0, K_WINDOW)
                    kw = kp[krows, :]
                    dk = jnp.zeros((K_WINDOW, LANES), F32)
                    dv = jnp.zeros((K_WINDOW, LANES), F32)
                    for (ds, p), (qx, dox, _) in zip(out, heads):
                        dk = dk + _dot_tn(ds, qx[qrows, :])
                        dv = dv + _dot_tn(p, dox[qrows, :])
                    dqp[qrows, :] = jnp.where(first, _dot(out[0][0], kw), _dot(out[1][0], kw)) * (HEAD_DIM ** -0.5)
                    dkp[krows, :] += dk
                    dvp[krows, :] += dv
                return carry

            lax.fori_loop(0, SEQ // (group * Q_BLOCK), blocks, 0)

            for c, rows in chunks:
                g_q, g_k, g_v = dqp[_chunk(c), :], dkp[_chunk(c, HALF_WINDOW), :], dvp[_chunk(c, HALF_WINDOW), :]
                if step == 0:
                    dq_ref[rows, :] = g_q
                    dk_ref[rows, :] = g_k
                    dv_ref[rows, :] = g_v
                else:
                    dq_ref[rows, :] = dq_ref[rows, :] + g_q
                    dk_ref[rows, :] = dk_ref[rows, :] + g_k
                    dv_ref[rows, :] = dv_ref[rows, :] + g_v

    col = pl.BlockSpec((SEQ, LANES), lambda h: (0, h))
    padded = SEQ + 2 * HALF_WINDOW
    return _call(
        body, (q, k, v, do, o, lse), grid=(n_pairs,), in_specs=[ANY] * 6, out_specs=[col] * 3,
        out_shape=[_sds((SEQ, ATTN_WIDTH), F32)] * 3,
        scratch_shapes=[pltpu.VMEM((SEQ, LANES), BF16)] * 3 + [pltpu.VMEM((SEQ, LANES), F32)] * 3 + [pltpu.SemaphoreType.DMA((6,))]
        + [pltpu.VMEM((SEQ, LANES), F32)] * 3 + [pltpu.VMEM((SEQ, LANES), BF16)] * 4
        + [pltpu.VMEM((padded, LANES), BF16)] * 2 + [pltpu.VMEM((SEQ, LANES), F32)] * 4 + [pltpu.VMEM((padded, LANES), F32)] * 2
        + [pltpu.VMEM((3, Q_BLOCK, K_WINDOW), F32)],
        vmem_mib=56, name="attn_bwd", comm=comm)


def _inproj_bwd(dq, dk, dv, ds, db, u, c, conv_w, tc, tsa, tsb, w_in, x, g_pre, dx1):
    def body(dq_ref, dk_ref, dv_ref, ds_ref, db_ref, u_ref, c_ref, dsp_ref, up_ref, cp_ref, dsn_ref, un_ref, cn_ref, cw_ref,
             tc_ref, tsa_ref, tsb_ref, w_ref, x_ref, g_ref, dx1_ref, dx_ref, dproj_ref, h_ref, dg_ref, dcw_ref):
        i = pl.program_id(0)
        cw = cw_ref[...]
        u, c = u_ref[...].astype(F32), c_ref[...].astype(F32)
        t, t_prev, t_next, _ = _conv_parts(u, c, up_ref, cp_ref, un_ref, cn_ref, cw, i)
        ds = ds_ref[...].astype(F32)
        ds_prev, ds_next = _shifted(ds, _last_row(dsp_ref), _first_row(dsn_ref), i)
        dt = cw[0:1, :] * ds_next + cw[1:2, :] * ds + cw[2:3, :] * ds_prev
        d_taps = jnp.concatenate([_colsum(ds * t_prev), _colsum(ds * t), _colsum(ds * t_next)], axis=0)
        tc_, tsa_, tsb_ = tc_ref[...], tsa_ref[...], tsb_ref[...]
        groups = ((2, lambda: dv_ref[...].astype(BF16)), (4, lambda: db_ref[...]),
                  (0, lambda: _rotate_transposed(dq_ref[...], tc_, tsa_, tsb_).astype(BF16)),
                  (1, lambda: _rotate_transposed(dk_ref[...], tc_, tsa_, tsb_).astype(BF16)),
                  (3, lambda: (dt * c).astype(BF16)), (5, lambda: (dt * u).astype(BF16)))
        dh = jnp.zeros((x_ref.shape[0], D_MODEL), F32)
        for n, make in groups:
            cols = pl.ds(n * a, a)
            part = make()
            dproj_ref[:, cols] = part
            dh = dh + _dot_nt(part, w_ref[:, cols])
        y, r = _rms(x_ref[...])
        h_ref[...] = (y * g_ref[...]).astype(BF16).T
        dx_ref[...] = dx1_ref[...] + _rms_bwd(dh * g_ref[...], y, r)
        _accumulate([(dg_ref, _colsum(dh * y)), (dcw_ref, d_taps)], i)

    a = ATTN_WIDTH
    tile = FFN_TILE

    def rows(width):
        return _rows(width, tile)

    return _call(
        body, (dq, dk, dv, ds, db, u, c, ds, u, c, ds, u, c, conv_w, tc, tsa, tsb, w_in, x, g_pre, dx1), grid=(SEQ // tile,),
        in_specs=[rows(a)] * 7 + [_halo_prev(a, tile)] * 3 + [_halo_next(a, tile)] * 3
        + [_const((3, a)), rows(LANES), rows(LANES), rows(LANES), _const1((D_MODEL, IN_PROJ_WIDTH)), rows(D_MODEL),
           _const((1, D_MODEL)), rows(D_MODEL)],
        out_specs=[rows(D_MODEL), rows(IN_PROJ_WIDTH), _cols(D_MODEL, tile), _const((1, D_MODEL)), _const((3, a))],
        out_shape=[_sds((SEQ, D_MODEL), F32), _sds((SEQ, IN_PROJ_WIDTH), BF16), _sds((D_MODEL, SEQ), BF16),
                   _sds((1, D_MODEL), F32), _sds((3, a), F32)],
        vmem_mib=48, name="inproj_bwd")


def _adamw_math(w, g, m, v):
    m = ADAM_B1 * m + (1.0 - ADAM_B1) * g
    v = ADAM_B2 * v + (1.0 - ADAM_B2) * (g * g)
    m_hat = m / (1.0 - ADAM_B1 ** ADAM_STEP)
    v_hat = v / (1.0 - ADAM_B2 ** ADAM_STEP)
    delta = -ADAM_LR * (m_hat / (jnp.sqrt(v_hat) + ADAM_EPS) + ADAM_WD * w)
    return delta, m, v


def _sum_parts(p_ref):
    g = p_ref[0].astype(F32)
    for k in range(1, p_ref.shape[0]):
        g = g + p_ref[k].astype(F32)
    return g


def _adamw_layers(parts, w, m, v, row_tile, name, comm=None):
    _, rows, cols = w.shape
    assert rows % row_tile == 0
    n_tiles = rows // row_tile

    def body(*refs):
        p_refs = refs[:DEPTH]
        w_ref, m_ref, v_ref, g_ref, d_ref, nm_ref, nv_ref = refs[DEPTH:]
        layer = pl.program_id(0)
        for l, p_ref in enumerate(p_refs):
            @pl.when(layer == l)
            def _(p_ref=p_ref):
                g = _sum_parts(p_ref)
                g_ref[0] = g
                d_ref[0], nm_ref[0], nv_ref[0] = _adamw_math(w_ref[0], g, m_ref[0], v_ref[0])

    def part_spec(l):
        return pl.BlockSpec((parts[l].shape[0], row_tile, cols),
                            lambda layer, i: (0, jnp.where(layer == l, i, jnp.where(layer < l, 0, n_tiles - 1)), 0))

    tile = pl.BlockSpec((1, row_tile, cols), lambda layer, i: (layer, i, 0))
    buffered = 2 * row_tile * cols * (sum(p.shape[0] for p in parts) * 2 + 7 * 4)
    return _call(body, (*parts, w, m, v), grid=(DEPTH, n_tiles), in_specs=[part_spec(l) for l in range(DEPTH)] + [tile] * 3,
                 out_specs=[tile] * 4, out_shape=[_sds(w.shape, F32)] * 4, vmem_mib=-(-buffered // MIB) + 8, name=name, comm=comm)


def _adamw_small(parts, w, m, v):
    n = len(GAIN_NAMES)
    a = ATTN_WIDTH
    where = ((0, 0), (1, 0), (2, 0), (3, 0), (4, 0), (4, a))

    def body(*refs):
        p_ref = refs[0]
        w_refs, m_refs, v_refs = refs[1:1 + n], refs[1 + n:1 + 2 * n], refs[1 + 2 * n:1 + 3 * n]
        outs = refs[1 + 3 * n:1 + 7 * n]
        taps_ref, loss_ref, sum_ref = refs[1 + 7 * n:]
        sum_ref[...] = _sum_parts(p_ref)
        for k, (row, col) in enumerate(where):
            width = w_refs[k].shape[1]
            g = jnp.concatenate([sum_ref[pl.ds(l * SMALL_ROWS + row, 1), pl.ds(col, width)] for l in range(DEPTH)], axis=0)
            g_ref, d_ref, nm_ref, nv_ref = outs[4 * k:4 * k + 4]
            g_ref[...] = g
            d_ref[...], nm_ref[...], nv_ref[...] = _adamw_math(w_refs[k][...], g, m_refs[k][...], v_refs[k][...])
        for l in range(DEPTH):
            for tap, (row, col) in enumerate(((5, 0), (5, a), (6, 0))):
                taps_ref[l, tap:tap + 1, :] = sum_ref[l * SMALL_ROWS + row:l * SMALL_ROWS + row + 1, col:col + a]
        loss_ref[...] = sum_ref[SMALL_ROWS - 1:SMALL_ROWS, 0:1]

    out_shape = [_sds(x.shape, F32) for x in w for _ in range(4)] + [_sds((DEPTH, 3, a), F32), _sds((1, 1), F32)]
    res = pl.pallas_call(body, out_shape=out_shape, scratch_shapes=[pltpu.VMEM(parts.shape[1:], F32)],
                         name="adamw_small")(parts, *w, *m, *v)
    return [res[4 * k:4 * k + 4] for k in range(n)], res[4 * n], res[4 * n + 1]


def _adamw_plain(g, w, m, v, name):
    def body(g_ref, w_ref, m_ref, v_ref, d_ref, nm_ref, nv_ref):
        d_ref[...], nm_ref[...], nv_ref[...] = _adamw_math(w_ref[...], g_ref[...], m_ref[...], v_ref[...])

    return pl.pallas_call(body, out_shape=[_sds(w.shape, F32)] * 3, name=name)(g, w, m, v)


SMALL_ROWS = 8
GAIN_NAMES = ("pre_mix_norm", "post_mix_norm", "pre_ffn_norm", "post_ffn_norm", "attn_out_norm", "conv_out_norm")


def _pack_small(pre_mix, post_mix, pre_ffn, post_ffn, attn_out, conv_out, taps):
    zeros = jnp.zeros((1, ATTN_WIDTH), F32)
    return jnp.concatenate([
        pre_mix, post_mix, pre_ffn, post_ffn, jnp.concatenate([attn_out, conv_out], axis=1),
        jnp.concatenate([taps[0:1], taps[1:2]], axis=1), jnp.concatenate([taps[2:3], zeros], axis=1),
        jnp.zeros((1, D_MODEL), F32)], axis=0)


def _rope(positions, casts, comm=None):
    inv_freq = ROPE_THETA ** (-jnp.arange(0, ROPE_DIM, 2, dtype=F32) / ROPE_DIM)
    per_head = jnp.concatenate([inv_freq, inv_freq, jnp.zeros((HEAD_DIM - ROPE_DIM,), F32)])
    freq_row = jnp.tile(per_head, LANES // HEAD_DIM).reshape(1, LANES)
    return _rope_tables(positions.reshape(SEQ, 1), freq_row, casts, comm)


def _layer_forward(h, gains, taps, tables, w, inproj_comm=None, attn_comm=None, mix_comm=None, ffn_comm=None, target=None):
    def arrived(comm, landed):
        return w if comm is None else {**w, **dict(zip(comm.kinds, landed))}

    (q, k, v, u, b, c), landed = _inproj_fwd(h, gains["pre_mix_norm"], w["in"], *tables, comm=inproj_comm)
    w = arrived(inproj_comm, landed)
    (attn, lse), landed = _attn_fwd(q, k, v, comm=attn_comm)
    w = arrived(attn_comm, landed)
    (x1, merged, mix), landed = _mix_fwd(attn, u, b, c, taps, gains["attn_out_norm"], gains["conv_out_norm"], w["out"], h,
                                         gains["post_mix_norm"], comm=mix_comm)
    w = arrived(mix_comm, landed)
    (x2, gu, f, *loss), landed_next = _ffn_fwd(x1, gains["pre_ffn_norm"], w["gu"], w["dn"], gains["post_ffn_norm"], comm=ffn_comm,
                                               target=target)
    out = x2 if target is None else (x2, loss[0])
    return out, (h, q, k, v, u, b, c, attn, lse, merged, mix, x1, gu, f), w, landed_next


class _GradientExchange:
    def ffn_grads(self, act, df, h2, dgu):
        return dict(dn=_wgrad_paired(act, df, "dn"), gu=_wgrad_paired(dgu, h2, "gu"))

    def w_out_grad(self, merged_t, dmix):
        return _wgrad(merged_t, dmix, "wgrad_out")

    def attention_passenger(self, ffn, g_out):
        return _Together(_ScatterChips(("gu", "dn"), (ffn["gu"], ffn["dn"])), _Scatter(("out",), (g_out,)))

    def w_in_grad(self, h_t, dproj, small_rows):
        if small_rows is None:
            return _wgrad(h_t, dproj, "wgrad_in"), None
        parts, landed = _wgrad_paired(h_t, dproj, "in", to_owners=True, comm=_GatherSmall(small_rows, "small"))
        return parts, landed[0]


def _layer_backward(dx, saved, gains, taps, tables, w, ffn_comm, exchange, last):
    x0, q, k, v, u, b, c, attn, lse, merged, mix, x1, gu, f = saved
    (dx1, df, act, dgu, h2, dg_post_ffn, dg_pre_ffn), landed_prev = _ffn_bwd(
        dx, f, x1, gu, w["gu"], w["dn"], gains["post_ffn_norm"], gains["pre_ffn_norm"], comm=ffn_comm)
    ffn = exchange.ffn_grads(act, df, h2, dgu)
    dmix, dattn, ds, db, dg_post_mix, dg_attn, dg_conv = _mix_bwd(
        dx1, mix, attn, u, b, c, taps, gains["attn_out_norm"], gains["conv_out_norm"], gains["post_mix_norm"], w["out"])
    g_out = exchange.w_out_grad(merged, dmix)
    attn_comm = exchange.attention_passenger(ffn, g_out)
    (dq, dk, dv), landed = _attn_bwd(q, k, v, dattn, attn, lse, comm=attn_comm)
    dx0, dproj, h1, dg_pre_mix, dtaps = _inproj_bwd(
        dq, dk, dv, ds, db, u, c, taps, *tables, w["in"], x0, gains["pre_mix_norm"], dx1)[0]
    small = _pack_small(dg_pre_mix, dg_post_mix, dg_pre_ffn, dg_post_ffn, dg_attn, dg_conv, dtaps)
    small_rows = None
    if last is not None:
        small_above, local_loss = last
        small_rows = lax.dynamic_update_slice(jnp.concatenate([small, small_above], axis=0), local_loss, (SMALL_ROWS - 1, 0))
    g_in, gathered = exchange.w_in_grad(h1, dproj, small_rows)
    rest = dict(zip(attn_comm.kinds, landed)) if attn_comm is not None else dict(ffn, out=g_out)
    return dx0, g_in, small, landed_prev, rest, gathered


def kernel(x, positions, pre_mix_norm, w_in, conv_w, attn_out_norm, conv_out_norm, w_out, post_mix_norm, pre_ffn_norm, w_gate_up, w_down, post_ffn_norm, loss_target, m_pre_mix_norm, m_w_in, m_conv_w, m_attn_out_norm, m_conv_out_norm, m_w_out, m_post_mix_norm, m_pre_ffn_norm, m_w_gate_up, m_w_down, m_post_ffn_norm, v_pre_mix_norm, v_w_in, v_conv_w, v_attn_out_norm, v_conv_out_norm, v_w_out, v_post_mix_norm, v_pre_ffn_norm, v_w_gate_up, v_w_down, v_post_ffn_norm):
    mx, my, mc = _place()
    me = _block_of(mx, my, mc)
    conv_channels = conv_w.shape[-1]

    def hidden_major(a):
        return jnp.swapaxes(a, 1, 2)

    big_w = dict(zip(WEIGHT_KINDS, (w_in, w_out, hidden_major(w_gate_up), w_down)))
    big_m = dict(zip(WEIGHT_KINDS, (m_w_in, m_w_out, hidden_major(m_w_gate_up), m_w_down)))
    big_v = dict(zip(WEIGHT_KINDS, (v_w_in, v_w_out, hidden_major(v_w_gate_up), v_w_down)))
    early = ("in", "out", "dn")
    shards = {early[0]: big_w[early[0]].astype(BF16)}
    all_gains = dict(pre_mix_norm=pre_mix_norm, attn_out_norm=attn_out_norm, conv_out_norm=conv_out_norm, post_mix_norm=post_mix_norm,
                     pre_ffn_norm=pre_ffn_norm, post_ffn_norm=post_ffn_norm)

    def gather(kinds, l):
        return _Gather(kinds, [shards[kind] for kind in kinds], l)

    def gains(l):
        return {name: g[l:l + 1] for name, g in all_gains.items()}

    taps_flat = jnp.pad(conv_w.reshape(-1), (0, 8 * LANES - conv_w.size)).reshape(8, LANES)
    later = [kind for kind in WEIGHT_KINDS if kind != early[0]]
    made, (first, taps_all) = _rope(positions, [big_w[kind] for kind in later],
                                    _Together(gather(early[:1], 0), _GatherSmall(taps_flat, "taps")))
    tables = made[:3]
    shards.update(zip(later, made[3:]))
    taps_all = taps_all.reshape(N_DEV, 8 * LANES)[:, :conv_w.size]
    conv_w_full = taps_all.reshape(N_DEV, DEPTH, 3, conv_channels).transpose(1, 2, 0, 3).reshape(DEPTH, 3, ATTN_WIDTH)
    weights = [dict(zip(early[:1], [first]))] + [None] * (DEPTH - 1)
    saved = [None] * DEPTH
    h = x[0]
    for l in range(DEPTH):
        inproj_comm = gather(("out",), 0) if l == 0 else None
        mix_comm = gather(("dn",), 0) if l == 0 else None
        ffn_comm = gather(early, l + 1) if l + 1 < DEPTH else None
        target = loss_target[0] if l + 1 == DEPTH else None
        h, saved[l], weights[l], landed = _layer_forward(h, gains(l), conv_w_full[l], tables, weights[l], inproj_comm,
                                                         gather(("gu",), l), mix_comm, ffn_comm, target)
        if ffn_comm is not None:
            weights[l + 1] = dict(zip(ffn_comm.kinds, landed))
    dx, local_loss = h

    parts = {kind: [None] * DEPTH for kind in WEIGHT_KINDS}
    small_grads = [None] * DEPTH
    g_in_above = None
    exchange = _GradientExchange()
    for l in reversed(range(DEPTH)):
        ffn_comm = _Scatter(("in",), (g_in_above,)) if g_in_above is not None else None
        last = (jnp.concatenate(small_grads[1:], axis=0), local_loss) if l == 0 else None
        dx, g_in_above, small_grads[l], landed, rest, gathered = _layer_backward(
            dx, saved[l], gains(l), conv_w_full[l], tables, weights[l], ffn_comm, exchange, last)
        if ffn_comm is not None:
            parts["in"][l + 1] = landed[0]
        for kind, part in rest.items():
            parts[kind][l] = part

    parts["in"][0] = g_in_above
    tiles = {"in": 512, "out": 128, "gu": 352, "dn": 176}
    big = {kind: _adamw_layers(parts[kind], big_w[kind], big_m[kind], big_v[kind], tiles[kind], "adamw_" + kind)[0]
           for kind in WEIGHT_KINDS}

    small_m = (m_pre_mix_norm, m_post_mix_norm, m_pre_ffn_norm, m_post_ffn_norm, m_attn_out_norm, m_conv_out_norm)
    small_v = (v_pre_mix_norm, v_post_mix_norm, v_pre_ffn_norm, v_post_ffn_norm, v_attn_out_norm, v_conv_out_norm)
    per_gain, taps_grad_full, loss = _adamw_small(gathered, [all_gains[name] for name in GAIN_NAMES], small_m, small_v)
    small = [{name: per_gain[k][o] for k, name in enumerate(GAIN_NAMES)} for o in range(4)]
    loss = loss[0, 0]
    taps_grad = lax.dynamic_slice_in_dim(taps_grad_full, me * conv_channels, conv_channels, axis=2)

    taps = [taps_grad, *_adamw_plain(taps_grad, conv_w, m_conv_w, v_conv_w, "adamw_taps")]

    def leaves(o):
        s = small[o]
        return (s["pre_mix_norm"], big["in"][o], taps[o], s["attn_out_norm"], s["conv_out_norm"], big["out"][o], s["post_mix_norm"],
                s["pre_ffn_norm"], hidden_major(big["gu"][o]), big["dn"][o], s["post_ffn_norm"])

    return (loss, dx[None], *leaves(0), *leaves(1), *leaves(2), *leaves(3))
```
